```python
import jax, jax.numpy as jnp
from jax import lax
import numpy as np

D_MODEL = 2048
BATCH = 8
SEQ = 2048
DEPTH = 2

CTX_LEN = 256
GRID_W = 64
EPS = 1e-6

POOL_WINDOWS = (2, 4, 8, 16)
POOL_W = D_MODEL // 2
POOL_GROUP = POOL_W // len(POOL_WINDOWS)
CONV_W = D_MODEL // 2
CONV_K = 3
EVEN_IN = 2 * POOL_W + 4 * CONV_W

HG_DK = 128
HG_HEADS = D_MODEL // HG_DK
HG_DV = D_MODEL // HG_HEADS
HG_K = HG_HEADS * HG_DK
HG_V = HG_HEADS * HG_DV
ODD_IN = 3 * HG_K + 2 * HG_V
CHUNK = 64

N_EVEN = (DEPTH + 1) // 2
N_ODD = DEPTH // 2

kernel_name = "hybrid_pool_conv_hgrn2_diffusion_block"


def rms_norm(x, g):
    xf = x.astype(jnp.float32)
    y = xf * lax.rsqrt(jnp.mean(xf * xf, axis=-1, keepdims=True) + EPS)
    return (y * g.astype(jnp.float32)).astype(x.dtype)


def box_mean(x, w, axis):
    n = x.shape[axis]
    cs = jnp.cumsum(x.astype(jnp.float32), axis=axis)
    pad = [(0, 0)] * x.ndim
    pad[axis] = (1, 0)
    cs = jnp.pad(cs, pad)
    t = jnp.arange(n)
    lo = jnp.maximum(t - w // 2, 0)
    hi = jnp.minimum(t + w // 2 - 1, n - 1)
    s = jnp.take(cs, hi + 1, axis=axis) - jnp.take(cs, lo, axis=axis)
    cnt = (hi - lo + 1).astype(jnp.float32).reshape((n,) + (1,) * (x.ndim - axis - 1))
    return (s / cnt).astype(x.dtype)


def multiscale_pool(v, pool_w, pool_scale, on_grid):
    b, n, _ = v.shape
    outs = []
    for gi, w in enumerate(POOL_WINDOWS):
        vg = v[..., gi * POOL_GROUP:(gi + 1) * POOL_GROUP]
        if on_grid:
            rows = n // GRID_W
            m = box_mean(box_mean(vg.reshape(b, rows, GRID_W, POOL_GROUP), w, 1), w, 2)
            m = m.reshape(b, n, POOL_GROUP)
        else:
            m = box_mean(vg, w, 1)
        outs.append(m - vg)
    pooled = jnp.stack(outs, axis=2)
    mixed = jnp.einsum('bngi,gio->bngo', pooled, pool_w).reshape(b, n, POOL_W)
    return mixed * pool_scale


def depthwise_conv3(u, w, bias):
    n = u.shape[1]
    up = jnp.pad(u, ((0, 0), (1, 1), (0, 0)))
    return up[:, :n] * w[0] + up[:, 1:n + 1] * w[1] + up[:, 2:] * w[2] + bias


def pool_conv_mixer(h, w_in, pool_w, pool_scale, conv_w, conv_b, w_out, on_grid):
    idx = [int(i) for i in np.cumsum([POOL_W, POOL_W, CONV_W, CONV_W, CONV_W])]
    a_v, a_g, b_x, b_b, b_c, b_g = jnp.split(h @ w_in, idx, axis=-1)
    a_out = multiscale_pool(a_v, pool_w, pool_scale, on_grid) * jax.nn.silu(a_g)
    b_out = b_b * depthwise_conv3(b_c * b_x, conv_w, conv_b) * jax.nn.silu(b_g)
    return jnp.concatenate([a_out, b_out], axis=-1) @ w_out


def _heads(t, d):
    return t.reshape(t.shape[0], t.shape[1], HG_HEADS, d)


def _chunks(t):
    b, n, hh, d = t.shape
    return t.reshape(b, n // CHUNK, CHUNK, hh, d)


def _rev(t, reverse):
    return jnp.flip(t, axis=1) if reverse else t


def hgrn2_gates(z, lb):
    zf = z.astype(jnp.float32)
    f = lb + (1.0 - lb) * jax.nn.sigmoid(zf)
    k = (1.0 - lb) * jax.nn.sigmoid(-zf)
    return _heads(k, HG_DK), _heads(jnp.log(f), HG_DK)


def chunk_state_scan(kc, vc, bc, s0, keep_starts):
    b_last = bc[:, :, -1]
    k_dec = kc * jnp.exp(b_last[:, :, None] - bc)
    ds = jnp.einsum('bnchk,bnchv->bnhkv', k_dec, vc)
    decay = jnp.exp(b_last)

    def step(s, inp):
        d, dsn = inp
        return d[..., None] * s + dsn, (s if keep_starts else None)

    s_fin, starts = lax.scan(step, s0, (jnp.moveaxis(decay, 1, 0), jnp.moveaxis(ds, 1, 0)))
    return s_fin, starts


def hgrn2_chunk_scan(q, k, v, logf, s0):
    qc, kc, vc = _chunks(q), _chunks(k), _chunks(v)
    bc = jnp.cumsum(_chunks(logf), axis=2)
    s_fin, starts = chunk_state_scan(kc, vc, bc, s0, True)
    starts = jnp.moveaxis(starts, 0, 1)
    q_dec = qc * jnp.exp(bc)
    k_inv = kc * jnp.exp(-bc)
    inter = jnp.einsum('bnchk,bnhkv->bnchv', q_dec, starts)
    scores = jnp.einsum('bnchk,bnshk->bnhcs', q_dec, k_inv)
    mask = jnp.tril(jnp.ones((CHUNK, CHUNK), dtype=bool))
    scores = jnp.where(mask, scores, 0.0)
    intra = jnp.einsum('bnhcs,bnshv->bnchv', scores, vc)
    o = (inter + intra).reshape(q.shape[0], q.shape[1], HG_HEADS, HG_DV)
    return o, s_fin


def hgrn2_final_state(k, v, logf, s0):
    bc = jnp.cumsum(_chunks(logf), axis=2)
    s_fin, _ = chunk_state_scan(_chunks(k), _chunks(v), bc, s0, False)
    return s_fin


def hgrn2_readout(o, g, onorm_g, w_out):
    b, n = o.shape[:2]
    o = o * lax.rsqrt(jnp.mean(o * o, axis=-1, keepdims=True) + EPS)
    o = o.reshape(b, n, HG_V) * onorm_g.astype(jnp.float32) * jax.nn.silu(g.astype(jnp.float32))
    return o.astype(g.dtype) @ w_out


def hgrn2_mixer(h_lat, h_ctx, w_in, onorm_g, w_out, lb_fwd, lb_bwd, ctx_out):
    idx = [int(i) for i in np.cumsum([HG_K, HG_K, HG_V, HG_K])]
    zf_l, zb_l, i_l, q_l, g_l = jnp.split(h_lat @ w_in, idx, axis=-1)
    if ctx_out:
        zf_c, zb_c, i_c, q_c, g_c = jnp.split(h_ctx @ w_in, idx, axis=-1)
        q_c = _heads(q_c.astype(jnp.float32), HG_DK)
    else:
        zf_c, zb_c, i_c = jnp.split(h_ctx @ w_in[:, :idx[2]], idx[:2], axis=-1)
    q_l = _heads(q_l.astype(jnp.float32), HG_DK)
    v_l = _heads(i_l.astype(jnp.float32), HG_DV)
    v_c = _heads(i_c.astype(jnp.float32), HG_DV)
    s0 = jnp.zeros((h_lat.shape[0], HG_HEADS, HG_DK, HG_DV), jnp.float32)
    outs_l, outs_c = [], []
    for lb, z_l, z_c, reverse in ((lb_fwd, zf_l, zf_c, False), (lb_bwd, zb_l, zb_c, True)):
        k_l, lf_l = hgrn2_gates(z_l, lb)
        k_c, lf_c = hgrn2_gates(z_c, lb)
        if ctx_out:
            o_c, s_ctx = hgrn2_chunk_scan(_rev(q_c, reverse), _rev(k_c, reverse),
                                          _rev(v_c, reverse), _rev(lf_c, reverse), s0)
            outs_c.append(_rev(o_c, reverse))
        else:
            s_ctx = hgrn2_final_state(_rev(k_c, reverse), _rev(v_c, reverse),
                                      _rev(lf_c, reverse), s0)
        o_l, _ = hgrn2_chunk_scan(_rev(q_l, reverse), _rev(k_l, reverse),
                                  _rev(v_l, reverse), _rev(lf_l, reverse), s_ctx)
        outs_l.append(_rev(o_l, reverse))
    y_l = hgrn2_readout(outs_l[0] + outs_l[1], g_l, onorm_g, w_out)
    y_c = hgrn2_readout(outs_c[0] + outs_c[1], g_c, onorm_g, w_out) if ctx_out else None
    return y_l, y_c


def _fwd_setup_inputs(seed: int = 0) -> dict:
    key = jax.random.key(seed)
    ks = jax.random.split(key, 18)
    D = D_MODEL

    def nrm(k, shape, s):
        return jax.random.normal(k, shape, jnp.float32) * s

    return {
        "x": nrm(ks[0], (BATCH, SEQ, D), 1.0),
        "c": nrm(ks[1], (BATCH, D), 1.0),
        "ctx": nrm(ks[2], (BATCH, CTX_LEN, D), 1.0),
        "c_ctx": nrm(ks[3], (D,), 1.0),
        "ada_w": nrm(ks[4], (DEPTH, D, 3 * D), 0.5 * D ** -0.5),
        "ada_b": nrm(ks[5], (DEPTH, 3 * D), 0.01),
        "pre_g": 1.0 + nrm(ks[6], (DEPTH, D), 0.05),
        "post_g": 1.0 + nrm(ks[7], (DEPTH, D), 0.05),
        "ev_w_in": nrm(ks[8], (N_EVEN, D, EVEN_IN), D ** -0.5),
        "ev_pool_w": nrm(ks[9], (N_EVEN, len(POOL_WINDOWS), POOL_GROUP, POOL_GROUP), POOL_GROUP ** -0.5),
        "ev_pool_scale": 1.0 + nrm(ks[10], (N_EVEN, POOL_W), 0.1),
        "ev_conv_w": nrm(ks[11], (N_EVEN, CONV_K, CONV_W), CONV_K ** -0.5),
        "ev_conv_b": nrm(ks[12], (N_EVEN, CONV_W), 0.01),
        "ev_w_out": nrm(ks[13], (N_EVEN, POOL_W + CONV_W, D), (POOL_W + CONV_W) ** -0.5),
        "od_w_in": nrm(ks[14], (N_ODD, D, ODD_IN), D ** -0.5),
        "od_onorm_g": 1.0 + nrm(ks[15], (N_ODD, HG_V), 0.05),
        "od_w_out": nrm(ks[16], (N_ODD, HG_V, D), HG_V ** -0.5),
        "lb_logits": nrm(ks[17], (2, DEPTH + 1, HG_K), 0.1),
    }


def _fwd_reference(x, c, ctx, c_ctx, ada_w, ada_b, pre_g, post_g, ev_w_in, ev_pool_w,
              ev_pool_scale, ev_conv_w, ev_conv_b, ev_w_out, od_w_in, od_onorm_g,
              od_w_out, lb_logits):
    lb_table = jnp.cumsum(jax.nn.softmax(lb_logits.astype(jnp.float32), axis=1), axis=1)
    s_lat = jax.nn.silu(c)
    s_ctx = jax.nn.silu(c_ctx)
    for layer in range(DEPTH):
        last = layer == DEPTH - 1
        shift, scale, gate = jnp.split((s_lat @ ada_w[layer] + ada_b[layer])[:, None, :], 3, axis=-1)
        h = rms_norm(x, pre_g[layer]) * (1.0 + scale) + shift
        need_ctx = (layer % 2 == 1) or not last
        if need_ctx:
            shift_c, scale_c, gate_c = jnp.split(s_ctx @ ada_w[layer] + ada_b[layer], 3)
            hc = rms_norm(ctx, pre_g[layer]) * (1.0 + scale_c) + shift_c
        if layer % 2 == 0:
            e = layer // 2
            y = pool_conv_mixer(h, ev_w_in[e], ev_pool_w[e], ev_pool_scale[e], ev_conv_w[e],
                                ev_conv_b[e], ev_w_out[e], True)
            if not last:
                yc = pool_conv_mixer(hc, ev_w_in[e], ev_pool_w[e], ev_pool_scale[e], ev_conv_w[e],
                                     ev_conv_b[e], ev_w_out[e], False)
        else:
            o = layer // 2
            y, yc = hgrn2_mixer(h, hc, od_w_in[o], od_onorm_g[o], od_w_out[o],
                                lb_table[0, layer], lb_table[1, layer], not last)
        x = x + gate * rms_norm(y, post_g[layer])
        if not last:
            ctx = ctx + gate_c * rms_norm(yc, post_g[layer])
    return x


import jax as _jax
import jax.numpy as _jnp

TWIN_FORMAT = 'train_step'
FWD_PARAMS = ['x', 'c', 'ctx', 'c_ctx', 'ada_w', 'ada_b', 'pre_g', 'post_g', 'ev_w_in', 'ev_pool_w', 'ev_pool_scale', 'ev_conv_w', 'ev_conv_b', 'ev_w_out', 'od_w_in', 'od_onorm_g', 'od_w_out', 'lb_logits']
TWIN_WEIGHTS = ['c_ctx', 'ada_w', 'ada_b', 'pre_g', 'post_g', 'ev_w_in', 'ev_pool_w', 'ev_pool_scale', 'ev_conv_w', 'ev_conv_b', 'ev_w_out', 'od_w_in', 'od_onorm_g', 'od_w_out', 'lb_logits']
TWIN_DIFF_INPUT = 'x'
TWIN_INPUTS = ['x', 'c', 'ctx', 'c_ctx', 'ada_w', 'ada_b', 'pre_g', 'post_g', 'ev_w_in', 'ev_pool_w', 'ev_pool_scale', 'ev_conv_w', 'ev_conv_b', 'ev_w_out', 'od_w_in', 'od_onorm_g', 'od_w_out', 'lb_logits', 'loss_target', 'm_c_ctx', 'm_ada_w', 'm_ada_b', 'm_pre_g', 'm_post_g', 'm_ev_w_in', 'm_ev_pool_w', 'm_ev_pool_scale', 'm_ev_conv_w', 'm_ev_conv_b', 'm_ev_w_out', 'm_od_w_in', 'm_od_onorm_g', 'm_od_w_out', 'm_lb_logits', 'v_c_ctx', 'v_ada_w', 'v_ada_b', 'v_pre_g', 'v_post_g', 'v_ev_w_in', 'v_ev_pool_w', 'v_ev_pool_scale', 'v_ev_conv_w', 'v_ev_conv_b', 'v_ev_w_out', 'v_od_w_in', 'v_od_onorm_g', 'v_od_w_out', 'v_lb_logits']
TWIN_OUTPUTS = ['loss', 'grad_x', 'grad_c_ctx', 'grad_ada_w', 'grad_ada_b', 'grad_pre_g', 'grad_post_g', 'grad_ev_w_in', 'grad_ev_pool_w', 'grad_ev_pool_scale', 'grad_ev_conv_w', 'grad_ev_conv_b', 'grad_ev_w_out', 'grad_od_w_in', 'grad_od_onorm_g', 'grad_od_w_out', 'grad_lb_logits', 'delta_c_ctx', 'delta_ada_w', 'delta_ada_b', 'delta_pre_g', 'delta_post_g', 'delta_ev_w_in', 'delta_ev_pool_w', 'delta_ev_pool_scale', 'delta_ev_conv_w', 'delta_ev_conv_b', 'delta_ev_w_out', 'delta_od_w_in', 'delta_od_onorm_g', 'delta_od_w_out', 'delta_lb_logits', 'new_m_c_ctx', 'new_m_ada_w', 'new_m_ada_b', 'new_m_pre_g', 'new_m_post_g', 'new_m_ev_w_in', 'new_m_ev_pool_w', 'new_m_ev_pool_scale', 'new_m_ev_conv_w', 'new_m_ev_conv_b', 'new_m_ev_w_out', 'new_m_od_w_in', 'new_m_od_onorm_g', 'new_m_od_w_out', 'new_m_lb_logits', 'new_v_c_ctx', 'new_v_ada_w', 'new_v_ada_b', 'new_v_pre_g', 'new_v_post_g', 'new_v_ev_w_in', 'new_v_ev_pool_w', 'new_v_ev_pool_scale', 'new_v_ev_conv_w', 'new_v_ev_conv_b', 'new_v_ev_w_out', 'new_v_od_w_in', 'new_v_od_onorm_g', 'new_v_od_w_out', 'new_v_lb_logits']
TWIN_LEAF_KINDS = {'loss': 'loss', 'grad_x': 'grad_x', 'grad_c_ctx': 'grad_w', 'grad_ada_w': 'grad_w', 'grad_ada_b': 'grad_w', 'grad_pre_g': 'grad_w', 'grad_post_g': 'grad_w', 'grad_ev_w_in': 'grad_w', 'grad_ev_pool_w': 'grad_w', 'grad_ev_pool_scale': 'grad_w', 'grad_ev_conv_w': 'grad_w', 'grad_ev_conv_b': 'grad_w', 'grad_ev_w_out': 'grad_w', 'grad_od_w_in': 'grad_w', 'grad_od_onorm_g': 'grad_w', 'grad_od_w_out': 'grad_w', 'grad_lb_logits': 'grad_w', 'delta_c_ctx': 'delta_w', 'delta_ada_w': 'delta_w', 'delta_ada_b': 'delta_w', 'delta_pre_g': 'delta_w', 'delta_post_g': 'delta_w', 'delta_ev_w_in': 'delta_w', 'delta_ev_pool_w': 'delta_w', 'delta_ev_pool_scale': 'delta_w', 'delta_ev_conv_w': 'delta_w', 'delta_ev_conv_b': 'delta_w', 'delta_ev_w_out': 'delta_w', 'delta_od_w_in': 'delta_w', 'delta_od_onorm_g': 'delta_w', 'delta_od_w_out': 'delta_w', 'delta_lb_logits': 'delta_w', 'new_m_c_ctx': 'new_m', 'new_m_ada_w': 'new_m', 'new_m_ada_b': 'new_m', 'new_m_pre_g': 'new_m', 'new_m_post_g': 'new_m', 'new_m_ev_w_in': 'new_m', 'new_m_ev_pool_w': 'new_m', 'new_m_ev_pool_scale': 'new_m', 'new_m_ev_conv_w': 'new_m', 'new_m_ev_conv_b': 'new_m', 'new_m_ev_w_out': 'new_m', 'new_m_od_w_in': 'new_m', 'new_m_od_onorm_g': 'new_m', 'new_m_od_w_out': 'new_m', 'new_m_lb_logits': 'new_m', 'new_v_c_ctx': 'new_v', 'new_v_ada_w': 'new_v', 'new_v_ada_b': 'new_v', 'new_v_pre_g': 'new_v', 'new_v_post_g': 'new_v', 'new_v_ev_w_in': 'new_v', 'new_v_ev_pool_w': 'new_v', 'new_v_ev_pool_scale': 'new_v', 'new_v_ev_conv_w': 'new_v', 'new_v_ev_conv_b': 'new_v', 'new_v_ev_w_out': 'new_v', 'new_v_od_w_in': 'new_v', 'new_v_od_onorm_g': 'new_v', 'new_v_od_w_out': 'new_v', 'new_v_lb_logits': 'new_v'}


def _forward(args):
    return _fwd_reference(*[args[k] for k in FWD_PARAMS])


def _output_shape():
    out = _jax.eval_shape(lambda: _forward(_fwd_setup_inputs(0)))
    return out.shape, out.dtype

N_MICROBATCH = 1
ADAM_LR = 0.001
ADAM_B1 = 0.9
ADAM_B2 = 0.999
ADAM_EPS = 1e-08
ADAM_WD = 0.01
ADAM_STEP = 10
PER_EXAMPLE_BATCH_AXIS = {'x': 0, 'c': 0, 'ctx': 0, 'loss_target': 0}
SHARED_INPUTS = []
_WEIGHT_DTYPES = {'c_ctx': _jnp.float32, 'ada_w': _jnp.float32, 'ada_b': _jnp.float32, 'pre_g': _jnp.float32, 'post_g': _jnp.float32, 'ev_w_in': _jnp.float32, 'ev_pool_w': _jnp.float32, 'ev_pool_scale': _jnp.float32, 'ev_conv_w': _jnp.float32, 'ev_conv_b': _jnp.float32, 'ev_w_out': _jnp.float32, 'od_w_in': _jnp.float32, 'od_onorm_g': _jnp.float32, 'od_w_out': _jnp.float32, 'lb_logits': _jnp.float32}
MOMENT_SCALE = {'c_ctx': 2.011790e-03, 'ada_w': 4.169446e-01, 'ada_b': 7.781646e-01, 'pre_g': 6.054107e-02, 'post_g': 9.052208e-01, 'ev_w_in': 3.580091e-02, 'ev_pool_w': 3.173098e-02, 'ev_pool_scale': 3.539257e-02, 'ev_conv_w': 3.909748e-02, 'ev_conv_b': 3.467341e-02, 'ev_w_out': 3.594109e-02, 'od_w_in': 3.000174e-02, 'od_onorm_g': 3.086337e-02, 'od_w_out': 3.160381e-02, 'lb_logits': 9.394996e-03}


def _to_microbatches(a, axis):
    t = _jnp.moveaxis(a, axis, 0)
    t = t.reshape((N_MICROBATCH, t.shape[0] // N_MICROBATCH) + t.shape[1:])
    return _jnp.moveaxis(t, 1, axis + 1)


def setup_inputs(seed: int = 0) -> dict:
    inp = _fwd_setup_inputs(seed)
    key = _jax.random.fold_in(_jax.random.key(seed), 7919)
    shape, _ = _output_shape()
    out = dict(inp)
    out["loss_target"] = _jax.random.normal(_jax.random.fold_in(key, 0), shape, _jnp.float32)
    for i, name in enumerate(TWIN_WEIGHTS):
        w = inp[name].astype(_jnp.float32)
        if MOMENT_SCALE is None:
            s = _jnp.sqrt(_jnp.mean(_jnp.square(w)) + 1e-30)
        else:
            s = MOMENT_SCALE[name]
        km, kv = _jax.random.split(_jax.random.fold_in(key, i + 1))
        out[name] = w
        out["m_" + name] = s * _jax.random.normal(km, w.shape, _jnp.float32)
        out["v_" + name] = (s * s) * _jax.random.uniform(kv, w.shape, _jnp.float32, 0.5, 1.5)
    if N_MICROBATCH > 1:
        for name, axis in PER_EXAMPLE_BATCH_AXIS.items():
            out[name] = _to_microbatches(out[name], axis)
    return {'x': out['x'], 'c': out['c'], 'ctx': out['ctx'], 'c_ctx': out['c_ctx'], 'ada_w': out['ada_w'], 'ada_b': out['ada_b'], 'pre_g': out['pre_g'], 'post_g': out['post_g'], 'ev_w_in': out['ev_w_in'], 'ev_pool_w': out['ev_pool_w'], 'ev_pool_scale': out['ev_pool_scale'], 'ev_conv_w': out['ev_conv_w'], 'ev_conv_b': out['ev_conv_b'], 'ev_w_out': out['ev_w_out'], 'od_w_in': out['od_w_in'], 'od_onorm_g': out['od_onorm_g'], 'od_w_out': out['od_w_out'], 'lb_logits': out['lb_logits'], 'loss_target': out['loss_target'], 'm_c_ctx': out['m_c_ctx'], 'm_ada_w': out['m_ada_w'], 'm_ada_b': out['m_ada_b'], 'm_pre_g': out['m_pre_g'], 'm_post_g': out['m_post_g'], 'm_ev_w_in': out['m_ev_w_in'], 'm_ev_pool_w': out['m_ev_pool_w'], 'm_ev_pool_scale': out['m_ev_pool_scale'], 'm_ev_conv_w': out['m_ev_conv_w'], 'm_ev_conv_b': out['m_ev_conv_b'], 'm_ev_w_out': out['m_ev_w_out'], 'm_od_w_in': out['m_od_w_in'], 'm_od_onorm_g': out['m_od_onorm_g'], 'm_od_w_out': out['m_od_w_out'], 'm_lb_logits': out['m_lb_logits'], 'v_c_ctx': out['v_c_ctx'], 'v_ada_w': out['v_ada_w'], 'v_ada_b': out['v_ada_b'], 'v_pre_g': out['v_pre_g'], 'v_post_g': out['v_post_g'], 'v_ev_w_in': out['v_ev_w_in'], 'v_ev_pool_w': out['v_ev_pool_w'], 'v_ev_pool_scale': out['v_ev_pool_scale'], 'v_ev_conv_w': out['v_ev_conv_w'], 'v_ev_conv_b': out['v_ev_conv_b'], 'v_ev_w_out': out['v_ev_w_out'], 'v_od_w_in': out['v_od_w_in'], 'v_od_onorm_g': out['v_od_onorm_g'], 'v_od_w_out': out['v_od_w_out'], 'v_lb_logits': out['v_lb_logits']}


def _loss(weights, diff, rest, loss_target):
    with _jax.named_scope("forward"):
        args = {**rest, TWIN_DIFF_INPUT: diff, **{k: w.astype(_WEIGHT_DTYPES[k]) for k, w in weights.items()}}
        y = _forward(args)
    with _jax.named_scope("loss_head"):
        err = _jnp.square(y.astype(_jnp.float32) - loss_target)
        return 0.5 * _jnp.sum(_jnp.mean(err, axis=-1)) if err.ndim else 0.5 * err


def _adamw(w, g, m, v):
    m = ADAM_B1 * m + (1.0 - ADAM_B1) * g
    v = ADAM_B2 * v + (1.0 - ADAM_B2) * _jnp.square(g)
    m_hat = m / (1.0 - ADAM_B1 ** ADAM_STEP)
    v_hat = v / (1.0 - ADAM_B2 ** ADAM_STEP)
    delta = -ADAM_LR * (m_hat / (_jnp.sqrt(v_hat) + ADAM_EPS) + ADAM_WD * w)
    return delta, m, v


def reference(x, c, ctx, c_ctx, ada_w, ada_b, pre_g, post_g, ev_w_in, ev_pool_w, ev_pool_scale, ev_conv_w, ev_conv_b, ev_w_out, od_w_in, od_onorm_g, od_w_out, lb_logits, loss_target, m_c_ctx, m_ada_w, m_ada_b, m_pre_g, m_post_g, m_ev_w_in, m_ev_pool_w, m_ev_pool_scale, m_ev_conv_w, m_ev_conv_b, m_ev_w_out, m_od_w_in, m_od_onorm_g, m_od_w_out, m_lb_logits, v_c_ctx, v_ada_w, v_ada_b, v_pre_g, v_post_g, v_ev_w_in, v_ev_pool_w, v_ev_pool_scale, v_ev_conv_w, v_ev_conv_b, v_ev_w_out, v_od_w_in, v_od_onorm_g, v_od_w_out, v_lb_logits):
    given = dict(x=x, c=c, ctx=ctx, c_ctx=c_ctx, ada_w=ada_w, ada_b=ada_b, pre_g=pre_g, post_g=post_g, ev_w_in=ev_w_in, ev_pool_w=ev_pool_w, ev_pool_scale=ev_pool_scale, ev_conv_w=ev_conv_w, ev_conv_b=ev_conv_b, ev_w_out=ev_w_out, od_w_in=od_w_in, od_onorm_g=od_onorm_g, od_w_out=od_w_out, lb_logits=lb_logits, loss_target=loss_target, m_c_ctx=m_c_ctx, m_ada_w=m_ada_w, m_ada_b=m_ada_b, m_pre_g=m_pre_g, m_post_g=m_post_g, m_ev_w_in=m_ev_w_in, m_ev_pool_w=m_ev_pool_w, m_ev_pool_scale=m_ev_pool_scale, m_ev_conv_w=m_ev_conv_w, m_ev_conv_b=m_ev_conv_b, m_ev_w_out=m_ev_w_out, m_od_w_in=m_od_w_in, m_od_onorm_g=m_od_onorm_g, m_od_w_out=m_od_w_out, m_lb_logits=m_lb_logits, v_c_ctx=v_c_ctx, v_ada_w=v_ada_w, v_ada_b=v_ada_b, v_pre_g=v_pre_g, v_post_g=v_post_g, v_ev_w_in=v_ev_w_in, v_ev_pool_w=v_ev_pool_w, v_ev_pool_scale=v_ev_pool_scale, v_ev_conv_w=v_ev_conv_w, v_ev_conv_b=v_ev_conv_b, v_ev_w_out=v_ev_w_out, v_od_w_in=v_od_w_in, v_od_onorm_g=v_od_onorm_g, v_od_w_out=v_od_w_out, v_lb_logits=v_lb_logits)
    weights = {n: given[n] for n in TWIN_WEIGHTS}
    shared = {n: given[n] for n in SHARED_INPUTS}
    per_example = {n: given[n] for n in ['x', 'c', 'ctx']}
    grad_fn = _jax.value_and_grad(_loss, argnums=(0, 1))

    def one_microbatch(ex, loss_target):
        ex = dict(ex)
        diff = ex.pop(TWIN_DIFF_INPUT)
        return grad_fn(weights, diff, {**shared, **ex}, loss_target)

    if N_MICROBATCH == 1:
        loss, (grad_w, grad_x) = one_microbatch(per_example, given["loss_target"])
    else:
        def body(carry, xs):
            loss_sum, grad_sum = carry
            l_k, (gw_k, gx_k) = one_microbatch(xs[0], xs[1])
            with _jax.named_scope("update"):
                return (loss_sum + l_k, _jax.tree.map(_jnp.add, grad_sum, gw_k)), gx_k

        init = (_jnp.zeros((), _jnp.float32), _jax.tree.map(_jnp.zeros_like, weights))
        (loss, grad_w), grad_x = _jax.lax.scan(body, init, (per_example, given["loss_target"]))
    with _jax.named_scope("update"):
        delta_w, new_m, new_v = {}, {}, {}
        for n in TWIN_WEIGHTS:
            delta_w[n], new_m[n], new_v[n] = _adamw(weights[n], grad_w[n], given["m_" + n], given["v_" + n])
    return (loss, grad_x, *[grad_w[n] for n in TWIN_WEIGHTS], *[delta_w[n] for n in TWIN_WEIGHTS],
            *[new_m[n] for n in TWIN_WEIGHTS], *[new_v[n] for n in TWIN_WEIGHTS])
```

```python
import functools

import jax
import jax.numpy as jnp
from jax import lax
from jax.experimental import pallas as pl
from jax.experimental.pallas import tpu as pltpu

EPS = 1e-6
GRID_W_LOG2 = 6
CHUNK = 64
HEAD = 128
N_POOL = 4
ADAM_LR, ADAM_B1, ADAM_B2, ADAM_EPS, ADAM_WD, ADAM_STEP = 0.001, 0.9, 0.999, 1e-08, 0.01, 10
VMEM_LIMIT = 56 * 1024 * 1024
MESH = pl.DeviceIdType.MESH
F32, BF16 = jnp.float32, jnp.bfloat16
ANY = pl.BlockSpec(memory_space=pl.ANY)
VMEM = pl.BlockSpec(memory_space=pltpu.VMEM)


def _cp(**kw):
    return pltpu.CompilerParams(vmem_limit_bytes=VMEM_LIMIT, **kw)


def _silu(x):
    return x * jax.nn.sigmoid(x)


def _dsilu(x):
    s = jax.nn.sigmoid(x)
    return s * (1.0 + x * (1.0 - s))


def _dot(a, b, dims=((1,), (0,)), precision=None):
    return lax.dot_general(a, b, (dims, ((), ())), preferred_element_type=F32, precision=precision)


NN = ((1,), (0,))
NT = ((1,), (1,))
TN = ((0,), (0,))


def _row_block(cx):
    return 256 if cx % 256 == 0 else 128


def normmod_fwd(xs, g, shift, scale, cx):
    t, d = xs.shape
    tm = _row_block(cx)
    nctx = cx // tm

    def body(x_ref, g_ref, sh_ref, sc_ref, h_ref):
        is_ctx = pl.program_id(0) < nctx
        x = x_ref[...]
        rstd = lax.rsqrt(jnp.mean(x * x, axis=-1, keepdims=True) + EPS)
        sc = jnp.where(is_ctx, sc_ref[0:1, :], sc_ref[1:2, :])
        sh = jnp.where(is_ctx, sh_ref[0:1, :], sh_ref[1:2, :])
        h_ref[...] = ((x * rstd) * g_ref[...] * (1.0 + sc) + sh).astype(BF16)

    row = pl.BlockSpec((tm, d), lambda i: (i, 0))
    vec = lambda r: pl.BlockSpec((r, d), lambda i: (0, 0))
    return pl.pallas_call(
        body, name="normmod_fwd", grid=(t // tm,),
        in_specs=[row, vec(1), vec(2), vec(2)], out_specs=row,
        out_shape=jax.ShapeDtypeStruct((t, d), BF16), compiler_params=_cp(),
    )(xs, g, shift, scale)


def normmod_bwd(xs, dh, g, scale, dres, cx, res_is_latent_only):
    t, d = xs.shape
    tm = _row_block(cx)
    nctx = cx // tm

    def body(x_ref, dh_ref, g_ref, sc_ref, dres_ref, dx_ref, dg_ref, dsh_ref, dsc_ref):
        i = pl.program_id(0)
        is_ctx = i < nctx

        @pl.when(i == 0)
        def _():
            dg_ref[...] = jnp.zeros_like(dg_ref)
            dsh_ref[...] = jnp.zeros_like(dsh_ref)
            dsc_ref[...] = jnp.zeros_like(dsc_ref)

        x = x_ref[...]
        dh = dh_ref[...]
        gv = g_ref[...]
        rstd = lax.rsqrt(jnp.mean(x * x, axis=-1, keepdims=True) + EPS)
        xhat = x * rstd
        sc = jnp.where(is_ctx, sc_ref[0:1, :], sc_ref[1:2, :])
        dsh = jnp.sum(dh, axis=0, keepdims=True)
        dhx = dh * xhat
        dsc = jnp.sum(dhx * gv, axis=0, keepdims=True)
        dg_ref[...] += jnp.sum(dhx * (1.0 + sc), axis=0, keepdims=True)
        zero = jnp.zeros_like(dsh)
        dsh_ref[0:1, :] += jnp.where(is_ctx, dsh, zero)
        dsh_ref[1:2, :] += jnp.where(is_ctx, zero, dsh)
        dsc_ref[0:1, :] += jnp.where(is_ctx, dsc, zero)
        dsc_ref[1:2, :] += jnp.where(is_ctx, zero, dsc)
        dxhat = dh * (gv * (1.0 + sc))
        dx = rstd * (dxhat - xhat * jnp.mean(dxhat * xhat, axis=-1, keepdims=True))
        res = dres_ref[...]
        if res_is_latent_only:
            res = jnp.where(is_ctx, jnp.zeros_like(res), res)
        dx_ref[...] = dx + res

    row = pl.BlockSpec((tm, d), lambda i: (i, 0))
    if res_is_latent_only:
        res_spec = pl.BlockSpec((tm, d), lambda i: (jnp.maximum(i - nctx, 0), 0))
    else:
        res_spec = row
    vec = lambda r: pl.BlockSpec((r, d), lambda i: (0, 0))
    return pl.pallas_call(
        body, name="normmod_bwd", grid=(t // tm,),
        in_specs=[row, row, vec(1), vec(2), res_spec],
        out_specs=[row, vec(1), vec(2), vec(2)],
        out_shape=[jax.ShapeDtypeStruct((t, d), F32), jax.ShapeDtypeStruct((1, d), F32),
                   jax.ShapeDtypeStruct((2, d), F32), jax.ShapeDtypeStruct((2, d), F32)],
        compiler_params=_cp(),
    )(xs, dh, g, scale, dres)


def post_fwd(xs, y, pg, gate, cx):
    t, d = xs.shape
    tm = _row_block(cx)
    nctx = cx // tm

    def body(x_ref, y_ref, pg_ref, gate_ref, o_ref):
        is_ctx = pl.program_id(0) < nctx
        y = y_ref[...]
        rstd = lax.rsqrt(jnp.mean(y * y, axis=-1, keepdims=True) + EPS)
        gt = jnp.where(is_ctx, gate_ref[0:1, :], gate_ref[1:2, :])
        o_ref[...] = x_ref[...] + gt * ((y * rstd) * pg_ref[...])

    row = pl.BlockSpec((tm, d), lambda i: (i, 0))
    vec = lambda r: pl.BlockSpec((r, d), lambda i: (0, 0))
    return pl.pallas_call(
        body, name="post_fwd", grid=(t // tm,),
        in_specs=[row, row, vec(1), vec(2)], out_specs=row,
        out_shape=jax.ShapeDtypeStruct((t, d), F32), compiler_params=_cp(),
    )(xs, y, pg, gate)


def post_loss(xs, y, pg, gate, target, cx):
    t, d = xs.shape
    n = y.shape[0]
    tm = _row_block(cx)
    nctx = cx // tm

    def body(x_ref, y_ref, pg_ref, gate_ref, tgt_ref, sq_ref, dx_ref):
        @pl.when(pl.program_id(0) == 0)
        def _():
            sq_ref[...] = jnp.zeros_like(sq_ref)

        y = y_ref[...]
        rstd = lax.rsqrt(jnp.mean(y * y, axis=-1, keepdims=True) + EPS)
        x2 = x_ref[...] + gate_ref[1:2, :] * ((y * rstd) * pg_ref[...])
        err = x2 - tgt_ref[...]
        sq_ref[...] += jnp.sum(err * err)
        dx_ref[...] = err * (1.0 / d)

    row = pl.BlockSpec((tm, d), lambda i: (i, 0))
    xrow = pl.BlockSpec((tm, d), lambda i: (i + nctx, 0))
    vec = lambda r: pl.BlockSpec((r, d), lambda i: (0, 0))
    return pl.pallas_call(
        body, name="post_loss", grid=(n // tm,),
        in_specs=[xrow, row, vec(1), vec(2), row],
        out_specs=[pl.BlockSpec((8, 128), lambda i: (0, 0)), row],
        out_shape=[jax.ShapeDtypeStruct((8, 128), F32), jax.ShapeDtypeStruct((n, d), F32)],
        compiler_params=_cp(),
    )(xs, y, pg, gate, target)


def post_bwd(dxo, y, pg, gate, cx, latent_only):
    m, d = y.shape
    tm = _row_block(cx)
    nctx = 0 if latent_only else cx // tm

    def body(dx_ref, y_ref, pg_ref, gate_ref, dy_ref, dgate_ref, dpg_ref):
        i = pl.program_id(0)
        is_ctx = i < nctx

        @pl.when(i == 0)
        def _():
            dgate_ref[...] = jnp.zeros_like(dgate_ref)
            dpg_ref[...] = jnp.zeros_like(dpg_ref)

        y = y_ref[...]
        dx = dx_ref[...]
        pgv = pg_ref[...]
        rstd = lax.rsqrt(jnp.mean(y * y, axis=-1, keepdims=True) + EPS)
        yhat = y * rstd
        gt = jnp.where(is_ctx, gate_ref[0:1, :], gate_ref[1:2, :])
        dxy = dx * yhat
        dgt = jnp.sum(dxy * pgv, axis=0, keepdims=True)
        zero = jnp.zeros_like(dgt)
        dgate_ref[0:1, :] += jnp.where(is_ctx, dgt, zero)
        dgate_ref[1:2, :] += jnp.where(is_ctx, zero, dgt)
        dpg_ref[...] += jnp.sum(dxy * gt, axis=0, keepdims=True)
        dyhat = dx * (gt * pgv)
        dy = rstd * (dyhat - yhat * jnp.mean(dyhat * yhat, axis=-1, keepdims=True))
        dy_ref[...] = dy.astype(BF16)

    row = pl.BlockSpec((tm, d), lambda i: (i, 0))
    vec = lambda r: pl.BlockSpec((r, d), lambda i: (0, 0))
    return pl.pallas_call(
        body, name="post_bwd", grid=(m // tm,),
        in_specs=[row, row, vec(1), vec(2)], out_specs=[row, vec(2), vec(1)],
        out_shape=[jax.ShapeDtypeStruct((m, d), BF16), jax.ShapeDtypeStruct((2, d), F32),
                   jax.ShapeDtypeStruct((1, d), F32)],
        compiler_params=_cp(),
    )(dxo, y, pg, gate)


def _split_rows(m):
    for cand in (1024, 768, 512, 384, 256, 128):
        if m % cand == 0 and m // cand >= 2:
            return cand
    return m


def mm_nn(a, w3, sec, tn, name):
    m, k = a.shape
    q, _, ws = w3.shape
    n = q * ws
    tpq, tps = ws // tn, sec // tn
    tm = 256 if m % 256 == 0 else 128

    def body(a_ref, w_ref, o_ref):
        w = w_ref[...]

        def step(i, carry):
            rows = pl.ds(pl.multiple_of(i * tm, tm), tm)
            o_ref[rows, :] = _dot(a_ref[rows, :], w)
            return carry

        lax.fori_loop(0, m // tm, step, 0)

    return pl.pallas_call(
        body, name=name, grid=(n // tn,),
        in_specs=[pl.BlockSpec((m, k), lambda j: (0, 0)),
                  pl.BlockSpec((None, k, tn), lambda j: (j // tpq, 0, j % tpq))],
        out_specs=pl.BlockSpec((None, m, tn), lambda j: (j // tps, 0, j % tps)),
        out_shape=jax.ShapeDtypeStruct((n // sec, m, sec), F32), compiler_params=_cp(),
    )(a, w3)


def mm_nt(a3, w3, tn, name):
    s, m, sec = a3.shape
    q, k, ws = w3.shape
    n = q * ws
    tpq, tps = ws // tn, sec // tn
    mb = _split_rows(m)

    def body(a_ref, w_ref, o_ref):
        @pl.when(pl.program_id(1) == 0)
        def _():
            o_ref[...] = jnp.zeros_like(o_ref)

        o_ref[...] += _dot(a_ref[...], w_ref[...], NT)

    return pl.pallas_call(
        body, name=name, grid=(m // mb, n // tn),
        in_specs=[pl.BlockSpec((None, mb, tn), lambda i, j: (j // tps, i, j % tps)),
                  pl.BlockSpec((None, k, tn), lambda i, j: (j // tpq, 0, j % tpq))],
        out_specs=pl.BlockSpec((mb, k), lambda i, j: (i, 0)),
        out_shape=jax.ShapeDtypeStruct((m, k), F32), compiler_params=_cp(),
    )(a3, w3)


def mm_tn(a, b3, ws, tn, name):
    m, k = a.shape
    s, _, sec = b3.shape
    n = s * sec
    tpq, tps = ws // tn, sec // tn
    kb = 256 if k % 256 == 0 else 128

    def body(a_ref, b_ref, o_ref):
        b = b_ref[...]
        for i in range(k // kb):
            o_ref[i * kb:(i + 1) * kb, :] = _dot(a_ref[:, i * kb:(i + 1) * kb], b, TN)

    return pl.pallas_call(
        body, name=name, grid=(n // tn,),
        in_specs=[pl.BlockSpec((m, k), lambda j: (0, 0)),
                  pl.BlockSpec((None, m, tn), lambda j: (j // tps, 0, j % tps))],
        out_specs=pl.BlockSpec((None, k, tn), lambda j: (j // tpq, 0, j % tpq)),
        out_shape=jax.ShapeDtypeStruct((n // ws, k, ws), F32), compiler_params=_cp(),
    )(a, b3)


def _pool_mask(gi, row0, tm, t, cx, seq, transposed):
    half = jnp.left_shift(1, gi)
    r = lax.broadcasted_iota(jnp.int32, (tm, 1), 0) + row0
    c = lax.broadcasted_iota(jnp.int32, (1, t), 1)
    out_tok, src_tok = (c, r) if transposed else (r, c)

    def parts(tok):
        lat = tok - cx
        return tok < cx, lat >> GRID_W_LOG2, lat & ((1 << GRID_W_LOG2) - 1)

    o_ctx, o_row, o_col = parts(out_tok)
    s_ctx, s_row, s_col = parts(src_tok)

    def inside(o, s):
        return (s >= o - half) & (s <= o + half - 1)

    ctx_hit = o_ctx & s_ctx & inside(out_tok, src_tok)
    lat_hit = (~o_ctx) & (~s_ctx) & inside(o_row, s_row) & inside(o_col, s_col)
    mask = jnp.where(ctx_hit | lat_hit, 1.0, 0.0).astype(BF16)

    own_ctx, own_row, own_col = parts(r)

    def count(pos, size):
        return jnp.minimum(pos + half - 1, size - 1) - jnp.maximum(pos - half, 0) + 1

    cnt = jnp.where(own_ctx, count(r, cx),
                    count(own_row, seq >> GRID_W_LOG2) * count(own_col, 1 << GRID_W_LOG2))
    return mask, 1.0 / cnt.astype(F32)


def mix_a_fwd(z0, pool_w, pool_scale, cx):
    _, t, half_d = z0.shape
    g = half_d // N_POOL
    seq = t - cx
    tm = _row_block(cx)

    def body(v_ref, ag_ref, w_ref, sc_ref, u_ref, vb_ref):
        gi = pl.program_id(0)
        vb_ref[...] = v_ref[...].astype(BF16)
        w = w_ref[...].astype(BF16)
        sc = sc_ref[...]

        def step(i, carry):
            row0 = pl.multiple_of(i * tm, tm)
            rows = pl.ds(row0, tm)
            mask, inv = _pool_mask(gi, row0, tm, t, cx, seq, False)
            pooled = _dot(mask, vb_ref[...]) * inv - v_ref[rows, :]
            mixed = _dot(pooled.astype(BF16), w) * sc
            u_ref[rows, :] = (mixed * _silu(ag_ref[rows, :])).astype(BF16)
            return carry

        lax.fori_loop(0, t // tm, step, 0)

    sec = lambda s: pl.BlockSpec((None, t, g), lambda j: (s, 0, j))
    return pl.pallas_call(
        body, name="mix_a_fwd", grid=(N_POOL,),
        in_specs=[sec(0), sec(1), pl.BlockSpec((None, g, g), lambda j: (j, 0, 0)),
                  pl.BlockSpec((1, g), lambda j: (0, j))],
        out_specs=pl.BlockSpec((t, g), lambda j: (0, j)),
        out_shape=jax.ShapeDtypeStruct((t, half_d), BF16),
        scratch_shapes=[pltpu.VMEM((t, g), BF16)], compiler_params=_cp(),
    )(z0, z0, pool_w, pool_scale)


def mix_a_bwd(z0, du, pool_w, pool_scale, cx):
    _, t, half_d = z0.shape
    g = half_d // N_POOL
    seq = t - cx
    tm = _row_block(cx)
    gq = g // 4

    def body(v_ref, ag_ref, du_ref, w_ref, sc_ref, dz_ref, dw_ref, dsc_ref,
             vb_ref, pooled_ref, dmx_ref, dpl_ref, wdp_ref):
        gi = pl.program_id(0)
        vb_ref[...] = v_ref[...].astype(BF16)
        w = w_ref[...].astype(BF16)
        sc = sc_ref[...]

        def first(i, dsc):
            row0 = pl.multiple_of(i * tm, tm)
            rows = pl.ds(row0, tm)
            mask, inv = _pool_mask(gi, row0, tm, t, cx, seq, False)
            pooled = (_dot(mask, vb_ref[...]) * inv - v_ref[rows, :]).astype(BF16)
            pooled_ref[rows, :] = pooled
            mixed = _dot(pooled, w)
            ag = ag_ref[rows, :]
            duv = du_ref[rows, :]
            dz_ref[1, rows, :] = (duv * (mixed * sc) * _dsilu(ag)).astype(BF16)
            dms = duv * _silu(ag)
            dmixed = (dms * sc).astype(BF16)
            dmx_ref[rows, :] = dmixed
            dpooled = _dot(dmixed, w, NT)
            dpl_ref[rows, :] = dpooled
            wdp_ref[rows, :] = (dpooled * inv).astype(BF16)
            return dsc + jnp.sum(dms * mixed, axis=0, keepdims=True)

        dsc_ref[...] = lax.fori_loop(0, t // tm, first, jnp.zeros((1, g), F32))
        dw = _dot(pooled_ref[...], dmx_ref[...], TN)
        for qi in range(4):
            dw_ref[qi] = dw[qi * gq:(qi + 1) * gq, :]

        def second(i, carry):
            row0 = pl.multiple_of(i * tm, tm)
            rows = pl.ds(row0, tm)
            mask_t, _ = _pool_mask(gi, row0, tm, t, cx, seq, True)
            dz_ref[0, rows, :] = (_dot(mask_t, wdp_ref[...]) - dpl_ref[rows, :]).astype(BF16)
            return carry

        lax.fori_loop(0, t // tm, second, 0)

    sec = lambda s: pl.BlockSpec((None, t, g), lambda j: (s, 0, j))
    return pl.pallas_call(
        body, name="mix_a_bwd", grid=(N_POOL,),
        in_specs=[sec(0), sec(1), pl.BlockSpec((t, g), lambda j: (0, j)),
                  pl.BlockSpec((None, g, g), lambda j: (j, 0, 0)),
                  pl.BlockSpec((1, g), lambda j: (0, j))],
        out_specs=[pl.BlockSpec((2, t, g), lambda j: (0, 0, j)),
                   pl.BlockSpec((4, None, gq, g), lambda j: (0, j, 0, 0)),
                   pl.BlockSpec((1, g), lambda j: (0, j))],
        out_shape=[jax.ShapeDtypeStruct((2, t, half_d), BF16),
                   jax.ShapeDtypeStruct((4, N_POOL, gq, g), F32),
                   jax.ShapeDtypeStruct((1, half_d), F32)],
        scratch_shapes=[pltpu.VMEM((t, g), BF16), pltpu.VMEM((t, g), BF16), pltpu.VMEM((t, g), BF16),
                        pltpu.VMEM((t, g), F32), pltpu.VMEM((t, g), BF16)],
        compiler_params=_cp(),
    )(z0, z0, du, pool_w, pool_scale)


def _conv_masks(t, cx):
    r = lax.broadcasted_iota(jnp.int32, (t, 1), 0)
    has_prev = jnp.where((r == 0) | (r == cx), 0.0, 1.0)
    has_next = jnp.where((r == cx - 1) | (r == t - 1), 0.0, 1.0)
    return has_prev, has_next


def mix_b_fwd(z0, conv_w, conv_b, cx):
    _, t, half_d = z0.shape
    gb = 128

    def body(bx_ref, bb_ref, bc_ref, bg_ref, w_ref, b_ref, u_ref):
        has_prev, has_next = _conv_masks(t, cx)
        tt = bc_ref[...] * bx_ref[...]
        prev = pltpu.roll(tt, 1, 0) * has_prev
        nxt = pltpu.roll(tt, t - 1, 0) * has_next
        cv = prev * w_ref[0:1, :] + tt * w_ref[1:2, :] + nxt * w_ref[2:3, :] + b_ref[...]
        u_ref[...] = (bb_ref[...] * cv * _silu(bg_ref[...])).astype(BF16)

    sec = lambda s: pl.BlockSpec((None, t, gb), lambda j: (s, 0, j))
    return pl.pallas_call(
        body, name="mix_b_fwd", grid=(half_d // gb,),
        in_specs=[sec(2), sec(3), sec(4), sec(5), pl.BlockSpec((3, gb), lambda j: (0, j)),
                  pl.BlockSpec((1, gb), lambda j: (0, j))],
        out_specs=pl.BlockSpec((t, gb), lambda j: (0, j)),
        out_shape=jax.ShapeDtypeStruct((t, half_d), BF16), compiler_params=_cp(),
    )(z0, z0, z0, z0, conv_w, conv_b)


def mix_b_bwd(z0, du, conv_w, conv_b, cx):
    _, t, half_d = z0.shape
    gb = 128
    off = half_d // gb

    def body(bx_ref, bb_ref, bc_ref, bg_ref, du_ref, w_ref, b_ref, dz_ref, dw_ref, db_ref):
        has_prev, has_next = _conv_masks(t, cx)
        bx, bb, bc, bg = bx_ref[...], bb_ref[...], bc_ref[...], bg_ref[...]
        duv = du_ref[...]
        tt = bc * bx
        prev = pltpu.roll(tt, 1, 0) * has_prev
        nxt = pltpu.roll(tt, t - 1, 0) * has_next
        w0, w1, w2 = w_ref[0:1, :], w_ref[1:2, :], w_ref[2:3, :]
        cv = prev * w0 + tt * w1 + nxt * w2 + b_ref[...]
        sg = _silu(bg)
        dz_ref[1] = (duv * cv * sg).astype(BF16)
        dz_ref[3] = (duv * bb * cv * _dsilu(bg)).astype(BF16)
        dcv = duv * bb * sg
        dw_ref[0:1, :] = jnp.sum(dcv * prev, axis=0, keepdims=True)
        dw_ref[1:2, :] = jnp.sum(dcv * tt, axis=0, keepdims=True)
        dw_ref[2:3, :] = jnp.sum(dcv * nxt, axis=0, keepdims=True)
        db_ref[...] = jnp.sum(dcv, axis=0, keepdims=True)
        dt = (pltpu.roll(dcv * has_prev, t - 1, 0) * w0 + dcv * w1
              + pltpu.roll(dcv * has_next, 1, 0) * w2)
        dz_ref[0] = (dt * bc).astype(BF16)
        dz_ref[2] = (dt * bx).astype(BF16)

    sec = lambda s: pl.BlockSpec((None, t, gb), lambda j: (s, 0, j))
    return pl.pallas_call(
        body, name="mix_b_bwd", grid=(half_d // gb,),
        in_specs=[sec(2), sec(3), sec(4), sec(5), pl.BlockSpec((t, gb), lambda j: (0, j + off)),
                  pl.BlockSpec((3, gb), lambda j: (0, j)), pl.BlockSpec((1, gb), lambda j: (0, j))],
        out_specs=[pl.BlockSpec((4, t, gb), lambda j: (0, 0, j)),
                   pl.BlockSpec((3, gb), lambda j: (0, j)), pl.BlockSpec((1, gb), lambda j: (0, j))],
        out_shape=[jax.ShapeDtypeStruct((4, t, half_d), BF16),
                   jax.ShapeDtypeStruct((3, half_d), F32), jax.ShapeDtypeStruct((1, half_d), F32)],
        compiler_params=_cp(),
    )(z0, z0, z0, z0, du, conv_w, conv_b)


def _lower_bound(lbl_ref, d):
    l0, l1, l2 = lbl_ref[d, 0:1, :], lbl_ref[d, 1:2, :], lbl_ref[d, 2:3, :]
    mx = jnp.maximum(jnp.maximum(l0, l1), l2)
    e0, e1, e2 = jnp.exp(l0 - mx), jnp.exp(l1 - mx), jnp.exp(l2 - mx)
    inv = 1.0 / (e0 + e1 + e2)
    return (e0 + e1) * inv, (e0 * inv, e1 * inv, e2 * inv)


def _chunk_consts(d):
    r = lax.broadcasted_iota(jnp.int32, (CHUNK, CHUNK), 0)
    c = lax.broadcasted_iota(jnp.int32, (CHUNK, CHUNK), 1)
    keep = (c <= r) if d == 0 else (c >= r)
    return jnp.where(keep, 1.0, 0.0).astype(F32), keep


def _chunk_of_step(s, d, nc, ncc):
    if d == 0:
        return s
    return jnp.where(s < ncc, ncc - 1 - s, nc - 1 + ncc - s)


def _chunk_terms(lfc, kc, qc, cum):
    bc = _dot(cum, lfc, precision=lax.Precision.HIGHEST)
    bl = jnp.sum(lfc, axis=0, keepdims=True)
    e = jnp.exp(bc)
    einv = jnp.exp(-bc)
    erem = jnp.exp(bl - bc)
    return e, einv, erem, jnp.exp(bl), qc * e, kc * einv, kc * erem


def hgrn_fwd(z1, lbl, onorm, cx):
    _, t, d = z1.shape
    seq = t - cx
    nc, ncc = t // CHUNK, cx // CHUNK

    def body(zf_ref, zb_ref, v_ref, q_ref, g_ref, lbl_ref, on_ref, o_ref, r_ref,
             lf_ref, k_ref, oacc_ref, st_ref):
        for dr, z_ref in ((0, zf_ref), (1, zb_ref)):
            lbv, _ = _lower_bound(lbl_ref, dr)
            z = z_ref[...]
            lf_ref[...] = jnp.log(lbv + (1.0 - lbv) * jax.nn.sigmoid(z))
            k_ref[...] = (1.0 - lbv) * jax.nn.sigmoid(-z)
            st_ref[...] = jnp.zeros_like(st_ref)
            cum, keep = _chunk_consts(dr)

            def step(s, carry, dr=dr, cum=cum, keep=keep):
                n = _chunk_of_step(s, dr, nc, ncc)
                rows = pl.ds(pl.multiple_of(n * CHUNK, CHUNK), CHUNK)
                vc = v_ref[rows, :].astype(BF16)
                _, _, _, dec, qd, ki, kd = _chunk_terms(lf_ref[rows, :], k_ref[rows, :], q_ref[rows, :], cum)
                qdb = qd.astype(BF16)
                a = jnp.where(keep, _dot(qdb, ki.astype(BF16), NT), 0.0)
                st = st_ref[...]
                oc = _dot(qdb, st.astype(BF16), NT) + _dot(a.astype(BF16), vc)
                st_ref[...] = st * dec + _dot(vc, kd.astype(BF16), TN)
                if dr == 0:
                    oacc_ref[rows, :] = oc
                else:
                    oacc_ref[rows, :] += oc
                return carry

            lax.fori_loop(0, nc, step, 0)

        o = oacc_ref[cx:, :]
        o_ref[...] = o
        rstd = lax.rsqrt(jnp.mean(o * o, axis=-1, keepdims=True) + EPS)
        r_ref[...] = (o * rstd * on_ref[...] * _silu(g_ref[cx:, :])).astype(BF16)

    sec = lambda s: pl.BlockSpec((None, t, HEAD), lambda h: (s, 0, h))
    col = pl.BlockSpec((seq, HEAD), lambda h: (0, h))
    return pl.pallas_call(
        body, name="hgrn_fwd", grid=(d // HEAD,),
        in_specs=[sec(0), sec(1), sec(2), sec(3), sec(4),
                  pl.BlockSpec((2, 3, HEAD), lambda h: (0, 0, h)), pl.BlockSpec((1, HEAD), lambda h: (0, h))],
        out_specs=[col, col],
        out_shape=[jax.ShapeDtypeStruct((seq, d), F32), jax.ShapeDtypeStruct((seq, d), BF16)],
        scratch_shapes=[pltpu.VMEM((t, HEAD), F32), pltpu.VMEM((t, HEAD), F32), pltpu.VMEM((t, HEAD), F32),
                        pltpu.VMEM((HEAD, HEAD), F32)],
        compiler_params=_cp(),
    )(z1, z1, z1, z1, z1, lbl, onorm)


def hgrn_bwd(z1, lbl, onorm, o, dr_out, cx):
    _, t, d = z1.shape
    seq = t - cx
    nc, ncc = t // CHUNK, cx // CHUNK

    def body(zf_ref, zb_ref, v_ref, q_ref, g_ref, lbl_ref, on_ref, o_ref, dr_ref,
             dz_ref, don_ref, dlb_ref,
             lf_ref, k_ref, do_ref, dq_ref, dv_ref, dk_ref, dlf_ref, ssc_ref, dst_ref):
        o = o_ref[...]
        g = g_ref[cx:, :]
        drv = dr_ref[...]
        onv = on_ref[...]
        rstd = lax.rsqrt(jnp.mean(o * o, axis=-1, keepdims=True) + EPS)
        ohat = o * rstd
        sg = _silu(g)
        don_ref[...] = jnp.sum(drv * ohat * sg, axis=0, keepdims=True)
        dz_ref[4, :cx, :] = jnp.zeros((cx, HEAD), BF16)
        dz_ref[4, cx:, :] = (drv * ohat * onv * _dsilu(g)).astype(BF16)
        dohat = drv * onv * sg
        do_ref[:cx, :] = jnp.zeros((cx, HEAD), F32)
        do_ref[cx:, :] = rstd * (dohat - ohat * jnp.mean(dohat * ohat, axis=-1, keepdims=True))

        for dr, z_ref in ((0, zf_ref), (1, zb_ref)):
            lbv, _ = _lower_bound(lbl_ref, dr)
            z = z_ref[...]
            lf_ref[...] = jnp.log(lbv + (1.0 - lbv) * jax.nn.sigmoid(z))
            k_ref[...] = (1.0 - lbv) * jax.nn.sigmoid(-z)
            cum, keep = _chunk_consts(dr)
            cum_t, _ = _chunk_consts(1 - dr)

            st_init = jnp.zeros((HEAD, HEAD), F32)

            def state_step(s, st, dr=dr, cum=cum):
                n = _chunk_of_step(s, dr, nc, ncc)
                rows = pl.ds(pl.multiple_of(n * CHUNK, CHUNK), CHUNK)
                ssc_ref[n] = st
                _, _, _, dec, _, _, kd = _chunk_terms(lf_ref[rows, :], k_ref[rows, :], q_ref[rows, :], cum)
                return st * dec + _dot(v_ref[rows, :].astype(BF16), kd.astype(BF16), TN)

            lax.fori_loop(0, nc, state_step, st_init)
            dst_ref[...] = jnp.zeros_like(dst_ref)

            def grad_step(s2, carry, dr=dr, cum=cum, cum_t=cum_t, keep=keep):
                n = _chunk_of_step(nc - 1 - s2, dr, nc, ncc)
                rows = pl.ds(pl.multiple_of(n * CHUNK, CHUNK), CHUNK)
                vc = v_ref[rows, :].astype(BF16)
                e, einv, erem, dec, qd, ki, kd = _chunk_terms(
                    lf_ref[rows, :], k_ref[rows, :], q_ref[rows, :], cum)
                qdb, kib, kdb = qd.astype(BF16), ki.astype(BF16), kd.astype(BF16)
                doc = do_ref[rows, :].astype(BF16)
                st0 = ssc_ref[n]
                dst = dst_ref[...]
                dstb = dst.astype(BF16)
                a = jnp.where(keep, _dot(qdb, kib, NT), 0.0).astype(BF16)
                da = jnp.where(keep, _dot(doc, vc, NT), 0.0).astype(BF16)
                dqd = _dot(doc, st0.astype(BF16)) + _dot(da, kib)
                dki = _dot(da, qdb, TN)
                dv = _dot(a, doc, TN) + _dot(kdb, dstb, NT)
                dkd = _dot(vc, dstb)
                ddec = jnp.sum(dst * st0, axis=0, keepdims=True)
                dst_ref[...] = _dot(doc, qdb, TN) + dst * dec
                dbc = dqd * qd - dki * ki - dkd * kd
                dbl = jnp.sum(dkd * kd, axis=0, keepdims=True) + ddec * dec
                dlf_ref[rows, :] = _dot(cum_t, dbc, precision=lax.Precision.HIGHEST) + dbl
                dk_ref[rows, :] = dki * einv + dkd * erem
                if dr == 0:
                    dq_ref[rows, :] = dqd * e
                    dv_ref[rows, :] = dv
                else:
                    dq_ref[rows, :] += dqd * e
                    dv_ref[rows, :] += dv
                return carry

            lax.fori_loop(0, nc, grad_step, 0)

            sig = jax.nn.sigmoid(z)
            one_lb = 1.0 - lbv
            f = lbv + one_lb * sig
            dlf = dlf_ref[...]
            dk = dk_ref[...]
            dsig = (dlf / f - dk) * one_lb
            dz_ref[dr] = (dsig * sig * (1.0 - sig)).astype(BF16)
            dlb_ref[dr:dr + 1, :] = jnp.sum((dlf / f - dk) * (1.0 - sig), axis=0, keepdims=True)

        dz_ref[2] = dv_ref[...].astype(BF16)
        dz_ref[3] = dq_ref[...].astype(BF16)

    sec = lambda s: pl.BlockSpec((None, t, HEAD), lambda h: (s, 0, h))
    col = pl.BlockSpec((seq, HEAD), lambda h: (0, h))
    tvec = pltpu.VMEM((t, HEAD), F32)
    return pl.pallas_call(
        body, name="hgrn_bwd", grid=(d // HEAD,),
        in_specs=[sec(0), sec(1), sec(2), sec(3), sec(4),
                  pl.BlockSpec((2, 3, HEAD), lambda h: (0, 0, h)), pl.BlockSpec((1, HEAD), lambda h: (0, h)),
                  col, col],
        out_specs=[pl.BlockSpec((5, t, HEAD), lambda h: (0, 0, h)),
                   pl.BlockSpec((1, HEAD), lambda h: (0, h)), pl.BlockSpec((2, HEAD), lambda h: (0, h))],
        out_shape=[jax.ShapeDtypeStruct((5, t, d), BF16), jax.ShapeDtypeStruct((1, d), F32),
                   jax.ShapeDtypeStruct((2, d), F32)],
        scratch_shapes=[tvec, tvec, tvec, tvec, tvec, tvec, tvec,
                        pltpu.VMEM((nc, HEAD, HEAD), F32), pltpu.VMEM((HEAD, HEAD), F32)],
        compiler_params=_cp(),
    )(z1, z1, z1, z1, z1, lbl, onorm, o, dr_out)


def _place():
    x, y, c = lax.axis_index("x"), lax.axis_index("y"), lax.axis_index("c")
    chips = [(1 - x, y), (x, 1 - y), (1 - x, 1 - y)]
    return x, y, c, chips


def allgather_shards(shards):
    n = len(shards)

    def body(*refs):
        ins, outs = refs[:n], refs[n:2 * n]
        send_sems, recv_sems, local_sems = refs[2 * n:]
        x, y, c, chips = _place()
        p = 2 * x + y
        started = []
        for i in range(n):
            local = pltpu.make_async_copy(ins[i], outs[i].at[p], local_sems.at[i])
            local.start()
            started.append(local)
        half = [pl.ds(c * (s.shape[0] // 2), s.shape[0] // 2) for s in shards]
        other = [pl.ds((1 - c) * (s.shape[0] // 2), s.shape[0] // 2) for s in shards]

        def remote(i, k, src, dst, to):
            return pltpu.make_async_remote_copy(src_ref=src, dst_ref=dst, send_sem=send_sems.at[6 * i + k],
                                                recv_sem=recv_sems.at[6 * i + k], device_id=to, device_id_type=MESH)

        sends = []
        for i in range(n):
            for j, chip in enumerate(chips):
                cp = remote(i, j, ins[i].at[half[i]], outs[i].at[p, half[i]], (*chip, c))
                cp.start()
                sends.append(cp)
        for i in range(n):
            for j, chip in enumerate(chips):
                landed = outs[i].at[2 * chip[0] + chip[1], half[i]]
                remote(i, j, landed, landed, (x, y, c)).wait_recv()
                cp = remote(i, 3 + j, landed, landed, (x, y, 1 - c))
                cp.start()
                sends.append(cp)
        for i in range(n):
            for j, chip in enumerate(chips):
                landed = outs[i].at[2 * chip[0] + chip[1], other[i]]
                remote(i, 3 + j, landed, landed, (x, y, c)).wait_recv()
        for cp in sends:
            cp.wait_send()
        for cp in started:
            cp.wait()

    return pl.pallas_call(
        body, name="allgather_shards",
        in_specs=[ANY] * n, out_specs=[ANY] * n,
        out_shape=[jax.ShapeDtypeStruct((4,) + s.shape, s.dtype) for s in shards],
        scratch_shapes=[pltpu.SemaphoreType.DMA((6 * n,)), pltpu.SemaphoreType.DMA((6 * n,)),
                        pltpu.SemaphoreType.DMA((n,))],
        compiler_params=pltpu.CompilerParams(has_side_effects=True),
    )(*shards)


def exchange_halves(grads):
    n = len(grads)

    def body(*refs):
        ins, outs = refs[:n], refs[n:2 * n]
        send_sems, recv_sems = refs[2 * n:]
        x, y, c, _ = _place()
        copies = []
        for i in range(n):
            hr = grads[i].shape[1] // 2
            cp = pltpu.make_async_remote_copy(
                src_ref=ins[i].at[:, pl.ds((1 - c) * hr, hr)], dst_ref=outs[i],
                send_sem=send_sems.at[i], recv_sem=recv_sems.at[i],
                device_id=(x, y, 1 - c), device_id_type=MESH)
            cp.start()
            copies.append(cp)
        for cp in copies:
            cp.wait()

    return pl.pallas_call(
        body, name="exchange_halves",
        in_specs=[ANY] * n, out_specs=[ANY] * n,
        out_shape=[jax.ShapeDtypeStruct((4, g.shape[1] // 2, g.shape[2]), g.dtype) for g in grads],
        scratch_shapes=[pltpu.SemaphoreType.DMA((n,)), pltpu.SemaphoreType.DMA((n,))],
        compiler_params=pltpu.CompilerParams(has_side_effects=True),
    )(*grads)


def pair_sum(grad, got, core):
    _, r, cc = grad.shape
    hr = r // 2
    tr = 256 if hr % 256 == 0 else hr

    def body(core_ref, a_ref, b_ref, s_ref, sb_ref):
        s = a_ref[...] + b_ref[...]
        s_ref[...] = s
        sb_ref[...] = s.astype(BF16)

    nb = hr // tr
    grid_spec = pltpu.PrefetchScalarGridSpec(
        num_scalar_prefetch=1, grid=(4, nb),
        in_specs=[pl.BlockSpec((None, tr, cc), lambda qi, i, core_ref: (qi, core_ref[0] * nb + i, 0)),
                  pl.BlockSpec((None, tr, cc), lambda qi, i, core_ref: (qi, i, 0))],
        out_specs=[pl.BlockSpec((None, tr, cc), lambda qi, i, core_ref: (qi, i, 0)),
                   pl.BlockSpec((None, tr, cc), lambda qi, i, core_ref: (qi, i, 0))])
    return pl.pallas_call(
        body, name="pair_sum", grid_spec=grid_spec,
        out_shape=[jax.ShapeDtypeStruct((4, hr, cc), F32), jax.ShapeDtypeStruct((4, hr, cc), BF16)],
        compiler_params=_cp(),
    )(core, grad, got)


def scatter_to_owners(parts):
    n = len(parts)

    def body(*refs):
        ins, outs = refs[:n], refs[n:2 * n]
        send_sems, recv_sems = refs[2 * n:]
        x, y, c, chips = _place()
        copies = []
        for i in range(n):
            for j, chip in enumerate(chips):
                cp = pltpu.make_async_remote_copy(
                    src_ref=ins[i].at[2 * chip[0] + chip[1]], dst_ref=outs[i].at[j],
                    send_sem=send_sems.at[3 * i + j], recv_sem=recv_sems.at[3 * i + j],
                    device_id=(*chip, c), device_id_type=MESH)
                cp.start()
                copies.append(cp)
        for cp in copies:
            cp.wait()

    return pl.pallas_call(
        body, name="scatter_to_owners",
        in_specs=[ANY] * n, out_specs=[ANY] * n,
        out_shape=[jax.ShapeDtypeStruct((3,) + p.shape[1:], p.dtype) for p in parts],
        scratch_shapes=[pltpu.SemaphoreType.DMA((3 * n,)), pltpu.SemaphoreType.DMA((3 * n,))],
        compiler_params=pltpu.CompilerParams(has_side_effects=True),
    )(*parts)


def owner_sum(own, got, chip):
    _, hr, cc = own.shape
    tr = 256 if hr % 256 == 0 else hr

    def body(chip_ref, a_ref, b_ref, o_ref):
        s = a_ref[...] + b_ref[0].astype(F32)
        s = s + b_ref[1].astype(F32)
        o_ref[...] = s + b_ref[2].astype(F32)

    grid_spec = pltpu.PrefetchScalarGridSpec(
        num_scalar_prefetch=1, grid=(hr // tr,),
        in_specs=[pl.BlockSpec((None, tr, cc), lambda i, chip_ref: (chip_ref[0], i, 0)),
                  pl.BlockSpec((3, tr, cc), lambda i, chip_ref: (0, i, 0))],
        out_specs=pl.BlockSpec((tr, cc), lambda i, chip_ref: (i, 0)))
    return pl.pallas_call(
        body, name="owner_sum", grid_spec=grid_spec,
        out_shape=jax.ShapeDtypeStruct((hr, cc), F32), compiler_params=_cp(),
    )(chip, own, got)


def share_halves(reduced):
    n = len(reduced)

    def body(*refs):
        ins, outs = refs[:n], refs[n:2 * n]
        send_sems, recv_sems, local_sems = refs[2 * n:]
        x, y, c, _ = _place()
        copies = []
        for i in range(n):
            hr = reduced[i].shape[0]
            mine = outs[i].at[pl.ds(c * hr, hr)]
            local = pltpu.make_async_copy(ins[i], mine, local_sems.at[i])
            local.start()
            cp = pltpu.make_async_remote_copy(
                src_ref=ins[i], dst_ref=mine, send_sem=send_sems.at[i], recv_sem=recv_sems.at[i],
                device_id=(x, y, 1 - c), device_id_type=MESH)
            cp.start()
            copies.append((local, cp, outs[i].at[pl.ds((1 - c) * hr, hr)]))
        for i, (local, cp, theirs) in enumerate(copies):
            local.wait()
            cp.wait_send()
            pltpu.make_async_remote_copy(
                src_ref=ins[i], dst_ref=theirs, send_sem=send_sems.at[i], recv_sem=recv_sems.at[i],
                device_id=(x, y, c), device_id_type=MESH).wait_recv()

    return pl.pallas_call(
        body, name="share_halves",
        in_specs=[ANY] * n, out_specs=[ANY] * n,
        out_shape=[jax.ShapeDtypeStruct((2 * r.shape[0], r.shape[1]), r.dtype) for r in reduced],
        scratch_shapes=[pltpu.SemaphoreType.DMA((n,)), pltpu.SemaphoreType.DMA((n,)),
                        pltpu.SemaphoreType.DMA((n,))],
        compiler_params=pltpu.CompilerParams(has_side_effects=True),
    )(*reduced)


def allgather8(v, name):
    r, n = v.shape

    def body(v_ref, out_ref, send_sems, recv_sems):
        x, y, c, _ = _place()
        me = 4 * x + 2 * y + c
        out_ref[me] = v_ref[...]

        def copy(k, slot, to):
            return pltpu.make_async_remote_copy(
                src_ref=v_ref, dst_ref=out_ref.at[slot], send_sem=send_sems.at[k - 1],
                recv_sem=recv_sems.at[k - 1], device_id=to, device_id_type=MESH)

        peers = []
        for k in range(1, 8):
            px = 1 - x if (k >> 2) & 1 else x
            py = 1 - y if (k >> 1) & 1 else y
            pc = 1 - c if k & 1 else c
            peers.append((px, py, pc))
            copy(k, me, (px, py, pc)).start()
        for k, (px, py, pc) in enumerate(peers, start=1):
            copy(k, 4 * px + 2 * py + pc, (x, y, c)).wait_recv()
        for k, peer in enumerate(peers, start=1):
            copy(k, me, peer).wait_send()

    return pl.pallas_call(
        body, name=name, in_specs=[VMEM], out_specs=VMEM,
        out_shape=jax.ShapeDtypeStruct((8, r, n), v.dtype),
        scratch_shapes=[pltpu.SemaphoreType.DMA((7,)), pltpu.SemaphoreType.DMA((7,))],
        compiler_params=_cp(has_side_effects=True),
    )(v)


def cast_bf16(w, name):
    r, c = w.shape
    tr = 256 if r % 256 == 0 else r

    def body(w_ref, o_ref):
        o_ref[...] = w_ref[...].astype(BF16)

    blk = pl.BlockSpec((tr, c), lambda i: (i, 0))
    return pl.pallas_call(body, name=name, grid=(r // tr,), in_specs=[blk], out_specs=blk,
                          out_shape=jax.ShapeDtypeStruct((r, c), BF16), compiler_params=_cp())(w)


def ada_fwd(s_in, ada_w, ada_b, tn):
    nl, d, ws = ada_w.shape

    def body(s_ref, w_ref, b_ref, so_ref, mod_ref):
        s = _silu(s_ref[...])
        so_ref[...] = s
        mod_ref[...] = _dot(s.astype(BF16), w_ref[...].astype(BF16)) + b_ref[...]

    return pl.pallas_call(
        body, name="ada_fwd", grid=(nl, ws // tn),
        in_specs=[pl.BlockSpec((16, d), lambda l, j: (0, 0)),
                  pl.BlockSpec((None, d, tn), lambda l, j: (l, 0, j)),
                  pl.BlockSpec((None, 1, tn), lambda l, j: (l, 0, j))],
        out_specs=[pl.BlockSpec((16, d), lambda l, j: (0, 0)),
                   pl.BlockSpec((None, 16, tn), lambda l, j: (l, 0, j))],
        out_shape=[jax.ShapeDtypeStruct((16, d), F32), jax.ShapeDtypeStruct((nl, 16, ws), F32)],
        compiler_params=_cp(),
    )(s_in, ada_w, ada_b)


def _adamw_math(w, g, m, v):
    m = ADAM_B1 * m + (1.0 - ADAM_B1) * g
    v = ADAM_B2 * v + (1.0 - ADAM_B2) * (g * g)
    m_hat = m / (1.0 - ADAM_B1 ** ADAM_STEP)
    v_hat = v / (1.0 - ADAM_B2 ** ADAM_STEP)
    delta = -ADAM_LR * (m_hat / (jnp.sqrt(v_hat) + ADAM_EPS) + ADAM_WD * w)
    return delta, m, v


def ada_bwd_adamw(s, dm, w, m, v):
    nl, d, ws = w.shape
    tr = 256 if d % 256 == 0 else 128

    def body(s_ref, dm_ref, w_ref, m_ref, v_ref, g_ref, dl_ref, mo_ref, vo_ref, dc_ref):
        dmv = dm_ref[...].astype(BF16)
        wv = w_ref[...]
        g = _dot(s_ref[...].astype(BF16), dmv, TN)
        g_ref[...] = g
        dl_ref[...], mo_ref[...], vo_ref[...] = _adamw_math(wv, g, m_ref[...], v_ref[...])
        dc_ref[...] = _dot(dmv[8:16, :], wv.astype(BF16), NT)

    wblk = pl.BlockSpec((None, tr, ws), lambda l, i: (l, i, 0))
    wshape = jax.ShapeDtypeStruct((nl, d, ws), F32)
    return pl.pallas_call(
        body, name="ada_bwd_adamw", grid=(nl, d // tr),
        in_specs=[pl.BlockSpec((16, tr), lambda l, i: (0, i)),
                  pl.BlockSpec((None, 16, ws), lambda l, i: (l, 0, 0)), wblk, wblk, wblk],
        out_specs=[wblk, wblk, wblk, wblk, pl.BlockSpec((None, 8, tr), lambda l, i: (l, 0, i))],
        out_shape=[wshape, wshape, wshape, wshape, jax.ShapeDtypeStruct((nl, 8, d), F32)],
        compiler_params=_cp(),
    )(s, dm, w, m, v)


def adamw(w, g, m, v, name):
    r, c = w.shape
    tr = 256 if r % 256 == 0 else r

    def body(w_ref, g_ref, m_ref, v_ref, dl_ref, mo_ref, vo_ref):
        dl_ref[...], mo_ref[...], vo_ref[...] = _adamw_math(w_ref[...], g_ref[...], m_ref[...], v_ref[...])

    blk = pl.BlockSpec((tr, c), lambda i: (i, 0))
    shape = jax.ShapeDtypeStruct((r, c), F32)
    return pl.pallas_call(body, name=name, grid=(r // tr,), in_specs=[blk] * 4, out_specs=[blk] * 3,
                          out_shape=[shape] * 3, compiler_params=_cp())(w, g, m, v)


SMALL_ROWS = 24
ROW_MOD = 10


def small_reduce(gathered):
    _, rows, d = gathered.shape

    def body(g_ref, o_ref):
        tot = g_ref[0]
        for b in range(1, 8):
            tot = tot + g_ref[b]
        o_ref[0:rows, :] = tot
        for layer in range(2):
            lat = ROW_MOD + 6 * layer
            o_ref[24 + 3 * layer:27 + 3 * layer, :] = tot[lat:lat + 3, :] + tot[lat + 3:lat + 6, :]
        o_ref[30:32, :] = jnp.zeros((2, d), F32)

    return pl.pallas_call(body, name="small_reduce", in_specs=[VMEM], out_specs=VMEM,
                          out_shape=jax.ShapeDtypeStruct((32, d), F32), compiler_params=_cp())(gathered)


def lb_logits_grad(lbl, dlb):
    _, _, n = lbl.shape

    def body(l_ref, d_ref, o_ref):
        for dr in range(2):
            _, (p0, p1, p2) = _lower_bound(l_ref, dr)
            dv = d_ref[dr:dr + 1, :]
            o_ref[dr, 0:1, :] = p0 * p2 * dv
            o_ref[dr, 1:2, :] = p1 * p2 * dv
            o_ref[dr, 2:3, :] = -p2 * (p0 + p1) * dv

    return pl.pallas_call(body, name="lb_logits_grad", in_specs=[VMEM, VMEM], out_specs=VMEM,
                          out_shape=jax.ShapeDtypeStruct((2, 3, n), F32), compiler_params=_cp())(lbl, dlb)


def c_ctx_grad(parts, c_ctx):
    d = c_ctx.shape[1]

    def body(p_ref, c_ref, o_ref):
        tot = p_ref[0, 0:1, :]
        for chip in range(1, 4):
            tot = tot + p_ref[2 * chip, 0:1, :]
        o_ref[...] = tot * _dsilu(c_ref[...])

    return pl.pallas_call(body, name="c_ctx_grad", in_specs=[VMEM, VMEM], out_specs=VMEM,
                          out_shape=jax.ShapeDtypeStruct((1, d), F32), compiler_params=_cp())(parts, c_ctx)


def _reduce_scatter(grads, core, chip):
    got = exchange_halves(grads)
    sums = [pair_sum(g, r, core) for g, r in zip(grads, got)]
    recv = scatter_to_owners([sb for _, sb in sums])
    reduced = [owner_sum(s, r, chip) for (s, _), r in zip(sums, recv)]
    return share_halves(reduced)


def kernel(x, c, ctx, c_ctx, ada_w, ada_b, pre_g, post_g, ev_w_in, ev_pool_w, ev_pool_scale, ev_conv_w, ev_conv_b, ev_w_out, od_w_in, od_onorm_g, od_w_out, lb_logits, loss_target, m_c_ctx, m_ada_w, m_ada_b, m_pre_g, m_post_g, m_ev_w_in, m_ev_pool_w, m_ev_pool_scale, m_ev_conv_w, m_ev_conv_b, m_ev_w_out, m_od_w_in, m_od_onorm_g, m_od_w_out, m_lb_logits, v_c_ctx, v_ada_w, v_ada_b, v_pre_g, v_post_g, v_ev_w_in, v_ev_pool_w, v_ev_pool_scale, v_ev_conv_w, v_ev_conv_b, v_ev_w_out, v_od_w_in, v_od_onorm_g, v_od_w_out, v_lb_logits):
    _, seq, d = x.shape
    cx = ctx.shape[1]
    t = cx + seq
    half_d = d // 2
    g = half_d // N_POOL
    tn = d // 4
    xi, yi, ci = lax.axis_index("x"), lax.axis_index("y"), lax.axis_index("c")
    chip = 2 * xi + yi
    me = 2 * chip + ci
    core_arr = jnp.reshape(ci, (1,)).astype(jnp.int32)
    chip_arr = jnp.reshape(chip, (1,)).astype(jnp.int32)

    pad_rows = 32 - 17
    small = jnp.concatenate([
        ev_pool_w.reshape(g, g), ev_conv_w.reshape(3, g), od_onorm_g.reshape(2, g),
        lb_logits.reshape(12, g), jnp.zeros((pad_rows, g), F32)], axis=0)
    ev_in_g, od_in_g, ev_out_g, od_out_g, small_g = allgather_shards([
        cast_bf16(ev_w_in[0], "cast_ev_w_in"), cast_bf16(od_w_in[0], "cast_od_w_in"),
        cast_bf16(ev_w_out[0], "cast_ev_w_out"), cast_bf16(od_w_out[0], "cast_od_w_out"), small])
    ev_out3 = ev_out_g.reshape(1, d, d)
    od_out3 = od_out_g.reshape(1, d, d)
    pool_w_full = small_g[:, :g].reshape(4, N_POOL, g // 4, g).transpose(1, 0, 2, 3).reshape(N_POOL, g, g)
    conv_w_full = small_g[:, g:g + 3].transpose(1, 0, 2).reshape(3, half_d)
    onorm_full = small_g[:, g + 3:g + 5].reshape(1, d)
    lbl_full = small_g[:, g + 5:g + 17].reshape(4, 2, 3, 2 * g).transpose(1, 2, 0, 3).reshape(2, 3, d)

    c_rows = jnp.concatenate([c, jnp.zeros((7, d), F32)], axis=0)
    c_all = allgather8(c_rows, "allgather_c")[:, 0, :]
    s_in = jnp.concatenate([c_all, c_ctx.reshape(1, d), jnp.zeros((7, d), F32)], axis=0)
    ws_ada = ada_w.shape[2]
    ada_b_mine = lax.dynamic_slice(ada_b, (0, chip * ws_ada), (2, ws_ada)).reshape(2, 1, ws_ada)
    s_act, mod_mine = ada_fwd(s_in, ada_w, ada_b_mine, tn)
    mod_all = allgather8(mod_mine.reshape(32, ws_ada), "allgather_mod")
    mod_full = mod_all[0::2].reshape(4, 2, 16, ws_ada).transpose(1, 2, 0, 3).reshape(2, 16, 3 * d)
    mod_lat = lax.dynamic_slice(mod_full, (0, me, 0), (2, 1, 3 * d))
    mods = jnp.concatenate([mod_full[:, 8:9], mod_lat], axis=1)
    shift, scale, gate = mods[:, :, :d], mods[:, :, d:2 * d], mods[:, :, 2 * d:]

    xs = jnp.concatenate([ctx[0], x[0]], axis=0)

    h0 = normmod_fwd(xs, pre_g[0:1], shift[0], scale[0], cx)
    z0 = mm_nn(h0, ev_in_g, half_d, tn, "mm_ev_in")
    u_a = mix_a_fwd(z0, pool_w_full, ev_pool_scale, cx)
    u_b = mix_b_fwd(z0, conv_w_full, ev_conv_b, cx)
    u = jnp.concatenate([u_a, u_b], axis=1)
    y0 = mm_nn(u, ev_out3, d, tn, "mm_ev_out")[0]
    xs1 = post_fwd(xs, y0, post_g[0:1], gate[0], cx)

    h1 = normmod_fwd(xs1, pre_g[1:2], shift[1], scale[1], cx)
    z1 = mm_nn(h1, od_in_g, d, tn, "mm_od_in")
    o1, r1 = hgrn_fwd(z1, lbl_full, onorm_full, cx)
    y1 = mm_nn(r1, od_out3, d, tn, "mm_od_out")[0]
    sq, dx2 = post_loss(xs1, y1, post_g[1:2], gate[1], loss_target[0], cx)
    loss = lax.psum(sq[0, 0] * (0.5 / d), ("x", "y", "c"))

    dy1, dgate1, dpost1 = post_bwd(dx2, y1, post_g[1:2], gate[1], cx, True)
    dr1 = mm_nt(dy1[None], od_out3, tn, "mm_od_out_dx")
    g_od_out = mm_tn(r1, dy1[None], d, tn, "mm_od_out_dw")
    dz1, donorm, dlb = hgrn_bwd(z1, lbl_full, onorm_full, o1, dr1, cx)
    dh1 = mm_nt(dz1, od_in_g, tn, "mm_od_in_dx")
    g_od_in = mm_tn(h1, dz1, od_in_g.shape[2], tn, "mm_od_in_dw")
    dxs1, dpre1, dshift1, dscale1 = normmod_bwd(xs1, dh1, pre_g[1:2], scale[1], dx2, cx, True)

    dy0, dgate0, dpost0 = post_bwd(dxs1, y0, post_g[0:1], gate[0], cx, False)
    du = mm_nt(dy0[None], ev_out3, tn, "mm_ev_out_dx")
    g_ev_out = mm_tn(u, dy0[None], d, tn, "mm_ev_out_dw")
    dz0a, g_pool_w, dpool_scale = mix_a_bwd(z0, du, pool_w_full, ev_pool_scale, cx)
    dz0b, dconv_w, dconv_b = mix_b_bwd(z0, du, conv_w_full, ev_conv_b, cx)
    dz0 = jnp.concatenate([dz0a, dz0b], axis=0)
    dh0 = mm_nt(dz0, ev_in_g, tn, "mm_ev_in_dx")
    g_ev_in = mm_tn(h0, dz0, ev_in_g.shape[2], tn, "mm_ev_in_dw")
    dxs0, dpre0, dshift0, dscale0 = normmod_bwd(xs, dh0, pre_g[0:1], scale[0], dxs1, cx, False)
    grad_x = dxs0[cx:][None]

    grad_ev_w_in, grad_od_w_in, grad_ev_w_out, grad_od_w_out, grad_pool_w = _reduce_scatter(
        [g_ev_in, g_od_in, g_ev_out.reshape(4, d // 4, d), g_od_out.reshape(4, d // 4, d),
         g_pool_w.reshape(4, g, g)], core_arr, chip_arr)

    zrow = jnp.zeros((1, d), F32)
    small_rows = jnp.concatenate([
        dpre0, dpre1, dpost0, dpost1,
        jnp.concatenate([dpool_scale, dconv_b], axis=1),
        jnp.concatenate([dconv_w.reshape(1, 3 * half_d), jnp.zeros((1, half_d), F32)], axis=1).reshape(2, d),
        donorm, dlb,
        dshift0[1:2], dscale0[1:2], dgate0[1:2], dshift0[0:1], dscale0[0:1], dgate0[0:1],
        dshift1[1:2], dscale1[1:2], dgate1[1:2], dshift1[0:1], dscale1[0:1], zrow,
        zrow, zrow], axis=0)
    small_all = allgather8(small_rows, "allgather_small")
    tot = small_reduce(small_all)

    dm_rows = []
    for layer in range(2):
        lat = ROW_MOD + 6 * layer
        dm_lat = small_all[:, lat:lat + 3].reshape(8, 3 * d)
        dm_ctx = tot[lat + 3:lat + 6].reshape(1, 3 * d)
        dm_rows.append(jnp.concatenate([dm_lat, dm_ctx, jnp.zeros((7, 3 * d), F32)], axis=0))
    dm_full = jnp.stack(dm_rows)
    dm_mine = lax.dynamic_slice(dm_full, (0, 0, chip * ws_ada), (2, 16, ws_ada))
    grad_ada_w, delta_ada_w, new_m_ada_w, new_v_ada_w, dctx_part = ada_bwd_adamw(
        s_act, dm_mine, ada_w, m_ada_w, v_ada_w)
    dctx_all = allgather8(dctx_part[0] + dctx_part[1], "allgather_dctx")
    grad_c_ctx = c_ctx_grad(dctx_all, c_ctx.reshape(1, d)).reshape(d)

    grad_ada_b = tot[24:30].reshape(2, 3 * d)
    grad_pre_g = tot[0:2]
    grad_post_g = tot[2:4]
    grad_ev_pool_scale = tot[4:5, :half_d]
    grad_ev_conv_b = tot[4:5, half_d:]
    conv_w_tot = tot[5:7].reshape(1, 2 * d)[:, :3 * half_d].reshape(3, N_POOL, g)
    grad_ev_conv_w = lax.dynamic_slice(conv_w_tot, (0, chip, 0), (3, 1, g)).reshape(1, 3, g)
    grad_od_onorm_g = lax.dynamic_slice(tot[7:8], (0, chip * 2 * g), (1, 2 * g))
    dlb_mine = lax.dynamic_slice(tot[8:10], (0, chip * 2 * g), (2, 2 * g))
    grad_lb_logits = lb_logits_grad(lb_logits, dlb_mine)
    grad_ev_w_in = grad_ev_w_in[None]
    grad_od_w_in = grad_od_w_in[None]
    grad_ev_w_out = grad_ev_w_out[None]
    grad_od_w_out = grad_od_w_out[None]
    grad_ev_pool_w = grad_pool_w.reshape(1, N_POOL, g // 4, g)

    def step(w, gr, m, v, name):
        shape = w.shape
        cols = shape[-1]
        two_d = lambda a: a.reshape(-1, cols)
        dl, mo, vo = adamw(two_d(w), two_d(gr), two_d(m), two_d(v), "adamw_" + name)
        return dl.reshape(shape), mo.reshape(shape), vo.reshape(shape)

    upd = {
        "c_ctx": step(c_ctx, grad_c_ctx, m_c_ctx, v_c_ctx, "c_ctx"),
        "ada_w": (delta_ada_w, new_m_ada_w, new_v_ada_w),
        "ada_b": step(ada_b, grad_ada_b, m_ada_b, v_ada_b, "ada_b"),
        "pre_g": step(pre_g, grad_pre_g, m_pre_g, v_pre_g, "pre_g"),
        "post_g": step(post_g, grad_post_g, m_post_g, v_post_g, "post_g"),
        "ev_w_in": step(ev_w_in, grad_ev_w_in, m_ev_w_in, v_ev_w_in, "ev_w_in"),
        "ev_pool_w": step(ev_pool_w, grad_ev_pool_w, m_ev_pool_w, v_ev_pool_w, "ev_pool_w"),
        "ev_pool_scale": step(ev_pool_scale, grad_ev_pool_scale, m_ev_pool_scale, v_ev_pool_scale, "ev_pool_scale"),
        "ev_conv_w": step(ev_conv_w, grad_ev_conv_w, m_ev_conv_w, v_ev_conv_w, "ev_conv_w"),
        "ev_conv_b": step(ev_conv_b, grad_ev_conv_b, m_ev_conv_b, v_ev_conv_b, "ev_conv_b"),
        "ev_w_out": step(ev_w_out, grad_ev_w_out, m_ev_w_out, v_ev_w_out, "ev_w_out"),
        "od_w_in": step(od_w_in, grad_od_w_in, m_od_w_in, v_od_w_in, "od_w_in"),
        "od_onorm_g": step(od_onorm_g, grad_od_onorm_g, m_od_onorm_g, v_od_onorm_g, "od_onorm_g"),
        "od_w_out": step(od_w_out, grad_od_w_out, m_od_w_out, v_od_w_out, "od_w_out"),
        "lb_logits": step(lb_logits, grad_lb_logits, m_lb_logits, v_lb_logits, "lb_logits"),
    }
    names = ["c_ctx", "ada_w", "ada_b", "pre_g", "post_g", "ev_w_in", "ev_pool_w", "ev_pool_scale",
             "ev_conv_w", "ev_conv_b", "ev_w_out", "od_w_in", "od_onorm_g", "od_w_out", "lb_logits"]
    grads = [grad_c_ctx, grad_ada_w, grad_ada_b, grad_pre_g, grad_post_g, grad_ev_w_in, grad_ev_pool_w,
             grad_ev_pool_scale, grad_ev_conv_w, grad_ev_conv_b, grad_ev_w_out, grad_od_w_in,
             grad_od_onorm_g, grad_od_w_out, grad_lb_logits]
    return (loss, grad_x, *grads, *[upd[k][0] for k in names], *[upd[k][1] for k in names],
            *[upd[k][2] for k in names])
```

```python
import functools

import jax
import jax.numpy as jnp
from jax import lax
from jax.experimental import pallas as pl
from jax.experimental.pallas import tpu as pltpu

EPS = 1e-6
GRID_W_LOG2 = 6
CHUNK = 64
HEAD = 128
N_POOL = 4
ADAM_LR, ADAM_B1, ADAM_B2, ADAM_EPS, ADAM_WD, ADAM_STEP = 0.001, 0.9, 0.999, 1e-08, 0.01, 10
VMEM_LIMIT = 56 * 1024 * 1024
MESH = pl.DeviceIdType.MESH
F32, BF16 = jnp.float32, jnp.bfloat16
ANY = pl.BlockSpec(memory_space=pl.ANY)
VMEM = pl.BlockSpec(memory_space=pltpu.VMEM)


def _cp(**kw):
    return pltpu.CompilerParams(vmem_limit_bytes=VMEM_LIMIT, **kw)


def _silu(x):
    return x * jax.nn.sigmoid(x)


def _dsilu(x):
    s = jax.nn.sigmoid(x)
    return s * (1.0 + x * (1.0 - s))


def _dot(a, b, dims=((1,), (0,)), precision=None):
    return lax.dot_general(a, b, (dims, ((), ())), preferred_element_type=F32, precision=precision)


NN = ((1,), (0,))
NT = ((1,), (1,))
TN = ((0,), (0,))


def _row_block(cx):
    return 256 if cx % 256 == 0 else 128


def normmod_fwd(xs, g, shift, scale, cx):
    t, d = xs.shape
    tm = _row_block(cx)
    nctx = cx // tm

    def body(x_ref, g_ref, sh_ref, sc_ref, h_ref):
        is_ctx = pl.program_id(0) < nctx
        x = x_ref[...]
        rstd = lax.rsqrt(jnp.mean(x * x, axis=-1, keepdims=True) + EPS)
        sc = jnp.where(is_ctx, sc_ref[0:1, :], sc_ref[1:2, :])
        sh = jnp.where(is_ctx, sh_ref[0:1, :], sh_ref[1:2, :])
        h_ref[...] = ((x * rstd) * g_ref[...] * (1.0 + sc) + sh).astype(BF16)

    row = pl.BlockSpec((tm, d), lambda i: (i, 0))
    vec = lambda r: pl.BlockSpec((r, d), lambda i: (0, 0))
    return pl.pallas_call(
        body, name="normmod_fwd", grid=(t // tm,),
        in_specs=[row, vec(1), vec(2), vec(2)], out_specs=row,
        out_shape=jax.ShapeDtypeStruct((t, d), BF16), compiler_params=_cp(),
    )(xs, g, shift, scale)


def normmod_bwd(xs, dh, g, scale, dres, cx, res_is_latent_only):
    t, d = xs.shape
    tm = _row_block(cx)
    nctx = cx // tm

    def body(x_ref, dh_ref, g_ref, sc_ref, dres_ref, dx_ref, dg_ref, dsh_ref, dsc_ref):
        i = pl.program_id(0)
        is_ctx = i < nctx

        @pl.when(i == 0)
        def _():
            dg_ref[...] = jnp.zeros_like(dg_ref)
            dsh_ref[...] = jnp.zeros_like(dsh_ref)
            dsc_ref[...] = jnp.zeros_like(dsc_ref)

        x = x_ref[...]
        dh = dh_ref[...]
        gv = g_ref[...]
        rstd = lax.rsqrt(jnp.mean(x * x, axis=-1, keepdims=True) + EPS)
        xhat = x * rstd
        sc = jnp.where(is_ctx, sc_ref[0:1, :], sc_ref[1:2, :])
        dsh = jnp.sum(dh, axis=0, keepdims=True)
        dhx = dh * xhat
        dsc = jnp.sum(dhx * gv, axis=0, keepdims=True)
        dg_ref[...] += jnp.sum(dhx * (1.0 + sc), axis=0, keepdims=True)
        zero = jnp.zeros_like(dsh)
        dsh_ref[0:1, :] += jnp.where(is_ctx, dsh, zero)
        dsh_ref[1:2, :] += jnp.where(is_ctx, zero, dsh)
        dsc_ref[0:1, :] += jnp.where(is_ctx, dsc, zero)
        dsc_ref[1:2, :] += jnp.where(is_ctx, zero, dsc)
        dxhat = dh * (gv * (1.0 + sc))
        dx = rstd * (dxhat - xhat * jnp.mean(dxhat * xhat, axis=-1, keepdims=True))
        res = dres_ref[...]
        if res_is_latent_only:
            res = jnp.where(is_ctx, jnp.zeros_like(res), res)
        dx_ref[...] = dx + res

    row = pl.BlockSpec((tm, d), lambda i: (i, 0))
    if res_is_latent_only:
        res_spec = pl.BlockSpec((tm, d), lambda i: (jnp.maximum(i - nctx, 0), 0))
    else:
        res_spec = row
    vec = lambda r: pl.BlockSpec((r, d), lambda i: (0, 0))
    return pl.pallas_call(
        body, name="normmod_bwd", grid=(t // tm,),
        in_specs=[row, row, vec(1), vec(2), res_spec],
        out_specs=[row, vec(1), vec(2), vec(2)],
        out_shape=[jax.ShapeDtypeStruct((t, d), F32), jax.ShapeDtypeStruct((1, d), F32),
                   jax.ShapeDtypeStruct((2, d), F32), jax.ShapeDtypeStruct((2, d), F32)],
        compiler_params=_cp(),
    )(xs, dh, g, scale, dres)


def post_fwd(xs, y, pg, gate, cx):
    t, d = xs.shape
    tm = _row_block(cx)
    nctx = cx // tm

    def body(x_ref, y_ref, pg_ref, gate_ref, o_ref):
        is_ctx = pl.program_id(0) < nctx
        y = y_ref[...]
        rstd = lax.rsqrt(jnp.mean(y * y, axis=-1, keepdims=True) + EPS)
        gt = jnp.where(is_ctx, gate_ref[0:1, :], gate_ref[1:2, :])
        o_ref[...] = x_ref[...] + gt * ((y * rstd) * pg_ref[...])

    row = pl.BlockSpec((tm, d), lambda i: (i, 0))
    vec = lambda r: pl.BlockSpec((r, d), lambda i: (0, 0))
    return pl.pallas_call(
        body, name="post_fwd", grid=(t // tm,),
        in_specs=[row, row, vec(1), vec(2)], out_specs=row,
        out_shape=jax.ShapeDtypeStruct((t, d), F32), compiler_params=_cp(),
    )(xs, y, pg, gate)


def post_loss(xs, y, pg, gate, target, cx):
    t, d = xs.shape
    n = y.shape[0]
    tm = _row_block(cx)
    nctx = cx // tm

    def body(x_ref, y_ref, pg_ref, gate_ref, tgt_ref, sq_ref, dx_ref):
        @pl.when(pl.program_id(0) == 0)
        def _():
            sq_ref[...] = jnp.zeros_like(sq_ref)

        y = y_ref[...]
        rstd = lax.rsqrt(jnp.mean(y * y, axis=-1, keepdims=True) + EPS)
        x2 = x_ref[...] + gate_ref[1:2, :] * ((y * rstd) * pg_ref[...])
        err = x2 - tgt_ref[...]
        sq_ref[...] += jnp.sum(err * err)
        dx_ref[...] = err * (1.0 / d)

    row = pl.BlockSpec((tm, d), lambda i: (i, 0))
    xrow = pl.BlockSpec((tm, d), lambda i: (i + nctx, 0))
    vec = lambda r: pl.BlockSpec((r, d), lambda i: (0, 0))
    return pl.pallas_call(
        body, name="post_loss", grid=(n // tm,),
        in_specs=[xrow, row, vec(1), vec(2), row],
        out_specs=[pl.BlockSpec((8, 128), lambda i: (0, 0)), row],
        out_shape=[jax.ShapeDtypeStruct((8, 128), F32), jax.ShapeDtypeStruct((n, d), F32)],
        compiler_params=_cp(),
    )(xs, y, pg, gate, target)


def post_bwd(dxo, y, pg, gate, cx, latent_only):
    m, d = y.shape
    tm = _row_block(cx)
    nctx = 0 if latent_only else cx // tm

    def body(dx_ref, y_ref, pg_ref, gate_ref, dy_ref, dgate_ref, dpg_ref):
        i = pl.program_id(0)
        is_ctx = i < nctx

        @pl.when(i == 0)
        def _():
            dgate_ref[...] = jnp.zeros_like(dgate_ref)
            dpg_ref[...] = jnp.zeros_like(dpg_ref)

        y = y_ref[...]
        dx = dx_ref[...]
        pgv = pg_ref[...]
        rstd = lax.rsqrt(jnp.mean(y * y, axis=-1, keepdims=True) + EPS)
        yhat = y * rstd
        gt = jnp.where(is_ctx, gate_ref[0:1, :], gate_ref[1:2, :])
        dxy = dx * yhat
        dgt = jnp.sum(dxy * pgv, axis=0, keepdims=True)
        zero = jnp.zeros_like(dgt)
        dgate_ref[0:1, :] += jnp.where(is_ctx, dgt, zero)
        dgate_ref[1:2, :] += jnp.where(is_ctx, zero, dgt)
        dpg_ref[...] += jnp.sum(dxy * gt, axis=0, keepdims=True)
        dyhat = dx * (gt * pgv)
        dy = rstd * (dyhat - yhat * jnp.mean(dyhat * yhat, axis=-1, keepdims=True))
        dy_ref[...] = dy.astype(BF16)

    row = pl.BlockSpec((tm, d), lambda i: (i, 0))
    vec = lambda r: pl.BlockSpec((r, d), lambda i: (0, 0))
    return pl.pallas_call(
        body, name="post_bwd", grid=(m // tm,),
        in_specs=[row, row, vec(1), vec(2)], out_specs=[row, vec(2), vec(1)],
        out_shape=[jax.ShapeDtypeStruct((m, d), BF16), jax.ShapeDtypeStruct((2, d), F32),
                   jax.ShapeDtypeStruct((1, d), F32)],
        compiler_params=_cp(),
    )(dxo, y, pg, gate)


def _split_rows(m):
    for cand in (1024, 768, 512, 384, 256, 128):
        if m % cand == 0 and m // cand >= 2:
            return cand
    return m


def mm_nn(a, w3, sec, tn, name):
    m, k = a.shape
    q, _, ws = w3.shape
    n = q * ws
    tpq, tps = ws // tn, sec // tn
    tm = 256 if m % 256 == 0 else 128

    def body(a_ref, w_ref, o_ref):
        w = w_ref[...]

        def step(i, carry):
            rows = pl.ds(pl.multiple_of(i * tm, tm), tm)
            o_ref[rows, :] = _dot(a_ref[rows, :], w)
            return carry

        lax.fori_loop(0, m // tm, step, 0)

    return pl.pallas_call(
        body, name=name, grid=(n // tn,),
        in_specs=[pl.BlockSpec((m, k), lambda j: (0, 0)),
                  pl.BlockSpec((None, k, tn), lambda j: (j // tpq, 0, j % tpq))],
        out_specs=pl.BlockSpec((None, m, tn), lambda j: (j // tps, 0, j % tps)),
        out_shape=jax.ShapeDtypeStruct((n // sec, m, sec), F32), compiler_params=_cp(),
    )(a, w3)


def mm_nt(a3, w3, tn, name):
    s, m, sec = a3.shape
    q, k, ws = w3.shape
    n = q * ws
    tpq, tps = ws // tn, sec // tn
    mb = _split_rows(m)

    def body(a_ref, w_ref, o_ref):
        @pl.when(pl.program_id(1) == 0)
        def _():
            o_ref[...] = jnp.zeros_like(o_ref)

        o_ref[...] += _dot(a_ref[...], w_ref[...], NT)

    return pl.pallas_call(
        body, name=name, grid=(m // mb, n // tn),
        in_specs=[pl.BlockSpec((None, mb, tn), lambda i, j: (j // tps, i, j % tps)),
                  pl.BlockSpec((None, k, tn), lambda i, j: (j // tpq, 0, j % tpq))],
        out_specs=pl.BlockSpec((mb, k), lambda i, j: (i, 0)),
        out_shape=jax.ShapeDtypeStruct((m, k), F32), compiler_params=_cp(),
    )(a3, w3)


def mm_tn(a, b3, ws, tn, name):
    m, k = a.shape
    s, _, sec = b3.shape
    n = s * sec
    tpq, tps = ws // tn, sec // tn
    kb = 256 if k % 256 == 0 else 128

    def body(a_ref, b_ref, o_ref):
        b = b_ref[...]
        for i in range(k // kb):
            o_ref[i * kb:(i + 1) * kb, :] = _dot(a_ref[:, i * kb:(i + 1) * kb], b, TN)

    return pl.pallas_call(
        body, name=name, grid=(n // tn,),
        in_specs=[pl.BlockSpec((m, k), lambda j: (0, 0)),
                  pl.BlockSpec((None, m, tn), lambda j: (j // tps, 0, j % tps))],
        out_specs=pl.BlockSpec((None, k, tn), lambda j: (j // tpq, 0, j % tpq)),
        out_shape=jax.ShapeDtypeStruct((n // ws, k, ws), F32), compiler_params=_cp(),
    )(a, b3)


def _pool_mask(gi, row0, tm, t, cx, seq, transposed):
    half = jnp.left_shift(1, gi)
    r = lax.broadcasted_iota(jnp.int32, (tm, 1), 0) + row0
    c = lax.broadcasted_iota(jnp.int32, (1, t), 1)
    out_tok, src_tok = (c, r) if transposed else (r, c)

    def parts(tok):
        lat = tok - cx
        return tok < cx, lat >> GRID_W_LOG2, lat & ((1 << GRID_W_LOG2) - 1)

    o_ctx, o_row, o_col = parts(out_tok)
    s_ctx, s_row, s_col = parts(src_tok)

    def inside(o, s):
        return (s >= o - half) & (s <= o + half - 1)

    ctx_hit = o_ctx & s_ctx & inside(out_tok, src_tok)
    lat_hit = (~o_ctx) & (~s_ctx) & inside(o_row, s_row) & inside(o_col, s_col)
    mask = jnp.where(ctx_hit | lat_hit, 1.0, 0.0).astype(BF16)

    own_ctx, own_row, own_col = parts(r)

    def count(pos, size):
        return jnp.minimum(pos + half - 1, size - 1) - jnp.maximum(pos - half, 0) + 1

    cnt = jnp.where(own_ctx, count(r, cx),
                    count(own_row, seq >> GRID_W_LOG2) * count(own_col, 1 << GRID_W_LOG2))
    return mask, 1.0 / cnt.astype(F32)


def mix_a_fwd(z0, pool_w, pool_scale, cx):
    _, t, half_d = z0.shape
    g = half_d // N_POOL
    seq = t - cx
    tm = _row_block(cx)

    def body(v_ref, ag_ref, w_ref, sc_ref, u_ref, vb_ref):
        gi = pl.program_id(0)
        vb_ref[...] = v_ref[...].astype(BF16)
        w = w_ref[...].astype(BF16)
        sc = sc_ref[...]

        def step(i, carry):
            row0 = pl.multiple_of(i * tm, tm)
            rows = pl.ds(row0, tm)
            mask, inv = _pool_mask(gi, row0, tm, t, cx, seq, False)
            pooled = _dot(mask, vb_ref[...]) * inv - v_ref[rows, :]
            mixed = _dot(pooled.astype(BF16), w) * sc
            u_ref[rows, :] = (mixed * _silu(ag_ref[rows, :])).astype(BF16)
            return carry

        lax.fori_loop(0, t // tm, step, 0)

    sec = lambda s: pl.BlockSpec((None, t, g), lambda j: (s, 0, j))
    return pl.pallas_call(
        body, name="mix_a_fwd", grid=(N_POOL,),
        in_specs=[sec(0), sec(1), pl.BlockSpec((None, g, g), lambda j: (j, 0, 0)),
                  pl.BlockSpec((1, g), lambda j: (0, j))],
        out_specs=pl.BlockSpec((t, g), lambda j: (0, j)),
        out_shape=jax.ShapeDtypeStruct((t, half_d), BF16),
        scratch_shapes=[pltpu.VMEM((t, g), BF16)], compiler_params=_cp(),
    )(z0, z0, pool_w, pool_scale)


def mix_a_bwd(z0, du, pool_w, pool_scale, cx):
    _, t, half_d = z0.shape
    g = half_d // N_POOL
    seq = t - cx
    tm = _row_block(cx)
    gq = g // 4

    def body(v_ref, ag_ref, du_ref, w_ref, sc_ref, dz_ref, dw_ref, dsc_ref,
             vb_ref, pooled_ref, dmx_ref, dpl_ref, wdp_ref):
        gi = pl.program_id(0)
        vb_ref[...] = v_ref[...].astype(BF16)
        w = w_ref[...].astype(BF16)
        sc = sc_ref[...]

        def first(i, dsc):
            row0 = pl.multiple_of(i * tm, tm)
            rows = pl.ds(row0, tm)
            mask, inv = _pool_mask(gi, row0, tm, t, cx, seq, False)
            pooled = (_dot(mask, vb_ref[...]) * inv - v_ref[rows, :]).astype(BF16)
            pooled_ref[rows, :] = pooled
            mixed = _dot(pooled, w)
            ag = ag_ref[rows, :]
            duv = du_ref[rows, :]
            dz_ref[1, rows, :] = (duv * (mixed * sc) * _dsilu(ag)).astype(BF16)
            dms = duv * _silu(ag)
            dmixed = (dms * sc).astype(BF16)
            dmx_ref[rows, :] = dmixed
            dpooled = _dot(dmixed, w, NT)
            dpl_ref[rows, :] = dpooled
            wdp_ref[rows, :] = (dpooled * inv).astype(BF16)
            return dsc + jnp.sum(dms * mixed, axis=0, keepdims=True)

        dsc_ref[...] = lax.fori_loop(0, t // tm, first, jnp.zeros((1, g), F32))
        dw = _dot(pooled_ref[...], dmx_ref[...], TN)
        for qi in range(4):
            dw_ref[qi] = dw[qi * gq:(qi + 1) * gq, :]

        def second(i, carry):
            row0 = pl.multiple_of(i * tm, tm)
            rows = pl.ds(row0, tm)
            mask_t, _ = _pool_mask(gi, row0, tm, t, cx, seq, True)
            dz_ref[0, rows, :] = (_dot(mask_t, wdp_ref[...]) - dpl_ref[rows, :]).astype(BF16)
            return carry

        lax.fori_loop(0, t // tm, second, 0)

    sec = lambda s: pl.BlockSpec((None, t, g), lambda j: (s, 0, j))
    return pl.pallas_call(
        body, name="mix_a_bwd", grid=(N_POOL,),
        in_specs=[sec(0), sec(1), pl.BlockSpec((t, g), lambda j: (0, j)),
                  pl.BlockSpec((None, g, g), lambda j: (j, 0, 0)),
                  pl.BlockSpec((1, g), lambda j: (0, j))],
        out_specs=[pl.BlockSpec((2, t, g), lambda j: (0, 0, j)),
                   pl.BlockSpec((4, None, gq, g), lambda j: (0, j, 0, 0)),
                   pl.BlockSpec((1, g), lambda j: (0, j))],
        out_shape=[jax.ShapeDtypeStruct((2, t, half_d), BF16),
                   jax.ShapeDtypeStruct((4, N_POOL, gq, g), F32),
                   jax.ShapeDtypeStruct((1, half_d), F32)],
        scratch_shapes=[pltpu.VMEM((t, g), BF16), pltpu.VMEM((t, g), BF16), pltpu.VMEM((t, g), BF16),
                        pltpu.VMEM((t, g), F32), pltpu.VMEM((t, g), BF16)],
        compiler_params=_cp(),
    )(z0, z0, du, pool_w, pool_scale)


def _conv_masks(t, cx):
    r = lax.broadcasted_iota(jnp.int32, (t, 1), 0)
    has_prev = jnp.where((r == 0) | (r == cx), 0.0, 1.0)
    has_next = jnp.where((r == cx - 1) | (r == t - 1), 0.0, 1.0)
    return has_prev, has_next


def mix_b_fwd(z0, conv_w, conv_b, cx):
    _, t, half_d = z0.shape
    gb = 128

    def body(bx_ref, bb_ref, bc_ref, bg_ref, w_ref, b_ref, u_ref):
        has_prev, has_next = _conv_masks(t, cx)
        tt = bc_ref[...] * bx_ref[...]
        prev = pltpu.roll(tt, 1, 0) * has_prev
        nxt = pltpu.roll(tt, t - 1, 0) * has_next
        cv = prev * w_ref[0:1, :] + tt * w_ref[1:2, :] + nxt * w_ref[2:3, :] + b_ref[...]
        u_ref[...] = (bb_ref[...] * cv * _silu(bg_ref[...])).astype(BF16)

    sec = lambda s: pl.BlockSpec((None, t, gb), lambda j: (s, 0, j))
    return pl.pallas_call(
        body, name="mix_b_fwd", grid=(half_d // gb,),
        in_specs=[sec(2), sec(3), sec(4), sec(5), pl.BlockSpec((3, gb), lambda j: (0, j)),
                  pl.BlockSpec((1, gb), lambda j: (0, j))],
        out_specs=pl.BlockSpec((t, gb), lambda j: (0, j)),
        out_shape=jax.ShapeDtypeStruct((t, half_d), BF16), compiler_params=_cp(),
    )(z0, z0, z0, z0, conv_w, conv_b)


def mix_b_bwd(z0, du, conv_w, conv_b, cx):
    _, t, half_d = z0.shape
    gb = 128
    off = half_d // gb

    def body(bx_ref, bb_ref, bc_ref, bg_ref, du_ref, w_ref, b_ref, dz_ref, dw_ref, db_ref):
        has_prev, has_next = _conv_masks(t, cx)
        bx, bb, bc, bg = bx_ref[...], bb_ref[...], bc_ref[...], bg_ref[...]
        duv = du_ref[...]
        tt = bc * bx
        prev = pltpu.roll(tt, 1, 0) * has_prev
        nxt = pltpu.roll(tt, t - 1, 0) * has_next
        w0, w1, w2 = w_ref[0:1, :], w_ref[1:2, :], w_ref[2:3, :]
        cv = prev * w0 + tt * w1 + nxt * w2 + b_ref[...]
        sg = _silu(bg)
        dz_ref[1] = (duv * cv * sg).astype(BF16)
        dz_ref[3] = (duv * bb * cv * _dsilu(bg)).astype(BF16)
        dcv = duv * bb * sg
        dw_ref[0:1, :] = jnp.sum(dcv * prev, axis=0, keepdims=True)
        dw_ref[1:2, :] = jnp.sum(dcv * tt, axis=0, keepdims=True)
        dw_ref[2:3, :] = jnp.sum(dcv * nxt, axis=0, keepdims=True)
        db_ref[...] = jnp.sum(dcv, axis=0, keepdims=True)
        dt = (pltpu.roll(dcv * has_prev, t - 1, 0) * w0 + dcv * w1
              + pltpu.roll(dcv * has_next, 1, 0) * w2)
        dz_ref[0] = (dt * bc).astype(BF16)
        dz_ref[2] = (dt * bx).astype(BF16)

    sec = lambda s: pl.BlockSpec((None, t, gb), lambda j: (s, 0, j))
    return pl.pallas_call(
        body, name="mix_b_bwd", grid=(half_d // gb,),
        in_specs=[sec(2), sec(3), sec(4), sec(5), pl.BlockSpec((t, gb), lambda j: (0, j + off)),
                  pl.BlockSpec((3, gb), lambda j: (0, j)), pl.BlockSpec((1, gb), lambda j: (0, j))],
        out_specs=[pl.BlockSpec((4, t, gb), lambda j: (0, 0, j)),
                   pl.BlockSpec((3, gb), lambda j: (0, j)), pl.BlockSpec((1, gb), lambda j: (0, j))],
        out_shape=[jax.ShapeDtypeStruct((4, t, half_d), BF16),
                   jax.ShapeDtypeStruct((3, half_d), F32), jax.ShapeDtypeStruct((1, half_d), F32)],
        compiler_params=_cp(),
    )(z0, z0, z0, z0, du, conv_w, conv_b)


def _lower_bound(lbl_ref, d):
    l0, l1, l2 = lbl_ref[d, 0:1, :], lbl_ref[d, 1:2, :], lbl_ref[d, 2:3, :]
    mx = jnp.maximum(jnp.maximum(l0, l1), l2)
    e0, e1, e2 = jnp.exp(l0 - mx), jnp.exp(l1 - mx), jnp.exp(l2 - mx)
    inv = 1.0 / (e0 + e1 + e2)
    return (e0 + e1) * inv, (e0 * inv, e1 * inv, e2 * inv)


def _chunk_consts(d):
    r = lax.broadcasted_iota(jnp.int32, (CHUNK, CHUNK), 0)
    c = lax.broadcasted_iota(jnp.int32, (CHUNK, CHUNK), 1)
    keep = (c <= r) if d == 0 else (c >= r)
    return jnp.where(keep, 1.0, 0.0).astype(F32), keep


def _chunk_of_step(s, d, nc, ncc):
    if d == 0:
        return s
    return jnp.where(s < ncc, ncc - 1 - s, nc - 1 + ncc - s)


def _chunk_terms(lfc, kc, qc, cum):
    bc = _dot(cum, lfc, precision=lax.Precision.HIGHEST)
    bl = jnp.sum(lfc, axis=0, keepdims=True)
    e = jnp.exp(bc)
    einv = jnp.exp(-bc)
    erem = jnp.exp(bl - bc)
    return e, einv, erem, jnp.exp(bl), qc * e, kc * einv, kc * erem


def hgrn_fwd(z1, lbl, onorm, cx):
    _, t, d = z1.shape
    seq = t - cx
    nc, ncc = t // CHUNK, cx // CHUNK

    def body(zf_ref, zb_ref, v_ref, q_ref, g_ref, lbl_ref, on_ref, o_ref, r_ref,
             lf_ref, k_ref, oacc_ref, st_ref):
        for dr, z_ref in ((0, zf_ref), (1, zb_ref)):
            lbv, _ = _lower_bound(lbl_ref, dr)
            z = z_ref[...]
            lf_ref[...] = jnp.log(lbv + (1.0 - lbv) * jax.nn.sigmoid(z))
            k_ref[...] = (1.0 - lbv) * jax.nn.sigmoid(-z)
            st_ref[...] = jnp.zeros_like(st_ref)
            cum, keep = _chunk_consts(dr)

            def step(s, carry, dr=dr, cum=cum, keep=keep):
                n = _chunk_of_step(s, dr, nc, ncc)
                rows = pl.ds(pl.multiple_of(n * CHUNK, CHUNK), CHUNK)
                vc = v_ref[rows, :].astype(BF16)
                _, _, _, dec, qd, ki, kd = _chunk_terms(lf_ref[rows, :], k_ref[rows, :], q_ref[rows, :], cum)
                qdb = qd.astype(BF16)
                a = jnp.where(keep, _dot(qdb, ki.astype(BF16), NT), 0.0)
                st = st_ref[...]
                oc = _dot(qdb, st.astype(BF16), NT) + _dot(a.astype(BF16), vc)
                st_ref[...] = st * dec + _dot(vc, kd.astype(BF16), TN)
                if dr == 0:
                    oacc_ref[rows, :] = oc
                else:
                    oacc_ref[rows, :] += oc
                return carry

            lax.fori_loop(0, nc, step, 0)

        o = oacc_ref[cx:, :]
        o_ref[...] = o
        rstd = lax.rsqrt(jnp.mean(o * o, axis=-1, keepdims=True) + EPS)
        r_ref[...] = (o * rstd * on_ref[...] * _silu(g_ref[cx:, :])).astype(BF16)

    sec = lambda s: pl.BlockSpec((None, t, HEAD), lambda h: (s, 0, h))
    col = pl.BlockSpec((seq, HEAD), lambda h: (0, h))
    return pl.pallas_call(
        body, name="hgrn_fwd", grid=(d // HEAD,),
        in_specs=[sec(0), sec(1), sec(2), sec(3), sec(4),
                  pl.BlockSpec((2, 3, HEAD), lambda h: (0, 0, h)), pl.BlockSpec((1, HEAD), lambda h: (0, h))],
        out_specs=[col, col],
        out_shape=[jax.ShapeDtypeStruct((seq, d), F32), jax.ShapeDtypeStruct((seq, d), BF16)],
        scratch_shapes=[pltpu.VMEM((t, HEAD), F32), pltpu.VMEM((t, HEAD), F32), pltpu.VMEM((t, HEAD), F32),
                        pltpu.VMEM((HEAD, HEAD), F32)],
        compiler_params=_cp(),
    )(z1, z1, z1, z1, z1, lbl, onorm)


def hgrn_bwd(z1, lbl, onorm, o, dr_out, cx):
    _, t, d = z1.shape
    seq = t - cx
    nc, ncc = t // CHUNK, cx // CHUNK

    def body(zf_ref, zb_ref, v_ref, q_ref, g_ref, lbl_ref, on_ref, o_ref, dr_ref,
             dz_ref, don_ref, dlb_ref,
             lf_ref, k_ref, do_ref, dq_ref, dv_ref, dk_ref, dlf_ref, ssc_ref, dst_ref):
        o = o_ref[...]
        g = g_ref[cx:, :]
        drv = dr_ref[...]
        onv = on_ref[...]
        rstd = lax.rsqrt(jnp.mean(o * o, axis=-1, keepdims=True) + EPS)
        ohat = o * rstd
        sg = _silu(g)
        don_ref[...] = jnp.sum(drv * ohat * sg, axis=0, keepdims=True)
        dz_ref[4, :cx, :] = jnp.zeros((cx, HEAD), BF16)
        dz_ref[4, cx:, :] = (drv * ohat * onv * _dsilu(g)).astype(BF16)
        dohat = drv * onv * sg
        do_ref[:cx, :] = jnp.zeros((cx, HEAD), F32)
        do_ref[cx:, :] = rstd * (dohat - ohat * jnp.mean(dohat * ohat, axis=-1, keepdims=True))

        for dr, z_ref in ((0, zf_ref), (1, zb_ref)):
            lbv, _ = _lower_bound(lbl_ref, dr)
            z = z_ref[...]
            lf_ref[...] = jnp.log(lbv + (1.0 - lbv) * jax.nn.sigmoid(z))
            k_ref[...] = (1.0 - lbv) * jax.nn.sigmoid(-z)
            cum, keep = _chunk_consts(dr)
            cum_t, _ = _chunk_consts(1 - dr)

            st_init = jnp.zeros((HEAD, HEAD), F32)

            def state_step(s, st, dr=dr, cum=cum):
                n = _chunk_of_step(s, dr, nc, ncc)
                rows = pl.ds(pl.multiple_of(n * CHUNK, CHUNK), CHUNK)
                ssc_ref[n] = st
                _, _, _, dec, _, _, kd = _chunk_terms(lf_ref[rows, :], k_ref[rows, :], q_ref[rows, :], cum)
                return st * dec + _dot(v_ref[rows, :].astype(BF16), kd.astype(BF16), TN)

            lax.fori_loop(0, nc, state_step, st_init)
            dst_ref[...] = jnp.zeros_like(dst_ref)

            def grad_step(s2, carry, dr=dr, cum=cum, cum_t=cum_t, keep=keep):
                n = _chunk_of_step(nc - 1 - s2, dr, nc, ncc)
                rows = pl.ds(pl.multiple_of(n * CHUNK, CHUNK), CHUNK)
                vc = v_ref[rows, :].astype(BF16)
                e, einv, erem, dec, qd, ki, kd = _chunk_terms(
                    lf_ref[rows, :], k_ref[rows, :], q_ref[rows, :], cum)
                qdb, kib, kdb = qd.astype(BF16), ki.astype(BF16), kd.astype(BF16)
                doc = do_ref[rows, :].astype(BF16)
                st0 = ssc_ref[n]
                dst = dst_ref[...]
                dstb = dst.astype(BF16)
                a = jnp.where(keep, _dot(qdb, kib, NT), 0.0).astype(BF16)
                da = jnp.where(keep, _dot(doc, vc, NT), 0.0).astype(BF16)
                dqd = _dot(doc, st0.astype(BF16)) + _dot(da, kib)
                dki = _dot(da, qdb, TN)
                dv = _dot(a, doc, TN) + _dot(kdb, dstb, NT)
                dkd = _dot(vc, dstb)
                ddec = jnp.sum(dst * st0, axis=0, keepdims=True)
                dst_ref[...] = _dot(doc, qdb, TN) + dst * dec
                dbc = dqd * qd - dki * ki - dkd * kd
                dbl = jnp.sum(dkd * kd, axis=0, keepdims=True) + ddec * dec
                dlf_ref[rows, :] = _dot(cum_t, dbc, precision=lax.Precision.HIGHEST) + dbl
                dk_ref[rows, :] = dki * einv + dkd * erem
                if dr == 0:
                    dq_ref[rows, :] = dqd * e
                    dv_ref[rows, :] = dv
                else:
                    dq_ref[rows, :] += dqd * e
                    dv_ref[rows, :] += dv
                return carry

            lax.fori_loop(0, nc, grad_step, 0)

            sig = jax.nn.sigmoid(z)
            one_lb = 1.0 - lbv
            f = lbv + one_lb * sig
            dlf = dlf_ref[...]
            dk = dk_ref[...]
            dsig = (dlf / f - dk) * one_lb
            dz_ref[dr] = (dsig * sig * (1.0 - sig)).astype(BF16)
            dlb_ref[dr:dr + 1, :] = jnp.sum((dlf / f - dk) * (1.0 - sig), axis=0, keepdims=True)

        dz_ref[2] = dv_ref[...].astype(BF16)
        dz_ref[3] = dq_ref[...].astype(BF16)

    sec = lambda s: pl.BlockSpec((None, t, HEAD), lambda h: (s, 0, h))
    col = pl.BlockSpec((seq, HEAD), lambda h: (0, h))
    tvec = pltpu.VMEM((t, HEAD), F32)
    return pl.pallas_call(
        body, name="hgrn_bwd", grid=(d // HEAD,),
        in_specs=[sec(0), sec(1), sec(2), sec(3), sec(4),
                  pl.BlockSpec((2, 3, HEAD), lambda h: (0, 0, h)), pl.BlockSpec((1, HEAD), lambda h: (0, h)),
                  col, col],
        out_specs=[pl.BlockSpec((5, t, HEAD), lambda h: (0, 0, h)),
                   pl.BlockSpec((1, HEAD), lambda h: (0, h)), pl.BlockSpec((2, HEAD), lambda h: (0, h))],
        out_shape=[jax.ShapeDtypeStruct((5, t, d), BF16), jax.ShapeDtypeStruct((1, d), F32),
                   jax.ShapeDtypeStruct((2, d), F32)],
        scratch_shapes=[tvec, tvec, tvec, tvec, tvec, tvec, tvec,
                        pltpu.VMEM((nc, HEAD, HEAD), F32), pltpu.VMEM((HEAD, HEAD), F32)],
        compiler_params=_cp(),
    )(z1, z1, z1, z1, z1, lbl, onorm, o, dr_out)


def _place():
    x, y, c = lax.axis_index("x"), lax.axis_index("y"), lax.axis_index("c")
    chips = [(1 - x, y), (x, 1 - y), (1 - x, 1 - y)]
    return x, y, c, chips


def allgather_shards(bufs):
    n = len(bufs)

    def body(*refs):
        outs = refs[n:2 * n]
        send_sems, recv_sems = refs[2 * n:]
        x, y, c, chips = _place()
        p = 2 * x + y
        half = [pl.ds(c * (s.shape[1] // 2), s.shape[1] // 2) for s in bufs]
        other = [pl.ds((1 - c) * (s.shape[1] // 2), s.shape[1] // 2) for s in bufs]

        def remote(i, k, src, dst, to):
            return pltpu.make_async_remote_copy(src_ref=src, dst_ref=dst, send_sem=send_sems.at[6 * i + k],
                                                recv_sem=recv_sems.at[6 * i + k], device_id=to, device_id_type=MESH)

        sends = []
        for i in range(n):
            for j, chip in enumerate(chips):
                mine = outs[i].at[p, half[i]]
                cp = remote(i, j, mine, mine, (*chip, c))
                cp.start()
                sends.append(cp)
        for i in range(n):
            for j, chip in enumerate(chips):
                landed = outs[i].at[2 * chip[0] + chip[1], half[i]]
                remote(i, j, landed, landed, (x, y, c)).wait_recv()
                cp = remote(i, 3 + j, landed, landed, (x, y, 1 - c))
                cp.start()
                sends.append(cp)
        for i in range(n):
            for j, chip in enumerate(chips):
                landed = outs[i].at[2 * chip[0] + chip[1], other[i]]
                remote(i, 3 + j, landed, landed, (x, y, c)).wait_recv()
        for cp in sends:
            cp.wait_send()

    return pl.pallas_call(
        body, name="allgather_shards",
        in_specs=[ANY] * n, out_specs=[ANY] * n,
        out_shape=[jax.ShapeDtypeStruct(s.shape, s.dtype) for s in bufs],
        input_output_aliases={i: i for i in range(n)},
        scratch_shapes=[pltpu.SemaphoreType.DMA((6 * n,)), pltpu.SemaphoreType.DMA((6 * n,))],
        compiler_params=pltpu.CompilerParams(has_side_effects=True),
    )(*bufs)


def exchange_halves(grads):
    n = len(grads)

    def body(*refs):
        ins, outs = refs[:n], refs[n:2 * n]
        send_sems, recv_sems = refs[2 * n:]
        x, y, c, _ = _place()
        copies = []
        for i in range(n):
            hr = grads[i].shape[1] // 2
            cp = pltpu.make_async_remote_copy(
                src_ref=ins[i].at[:, pl.ds((1 - c) * hr, hr)], dst_ref=outs[i],
                send_sem=send_sems.at[i], recv_sem=recv_sems.at[i],
                device_id=(x, y, 1 - c), device_id_type=MESH)
            cp.start()
            copies.append(cp)
        for cp in copies:
            cp.wait()

    return pl.pallas_call(
        body, name="exchange_halves",
        in_specs=[ANY] * n, out_specs=[ANY] * n,
        out_shape=[jax.ShapeDtypeStruct((4, g.shape[1] // 2, g.shape[2]), g.dtype) for g in grads],
        scratch_shapes=[pltpu.SemaphoreType.DMA((n,)), pltpu.SemaphoreType.DMA((n,))],
        compiler_params=pltpu.CompilerParams(has_side_effects=True),
    )(*grads)


def pair_sum(grad, got, core):
    _, r, cc = grad.shape
    hr = r // 2
    tr = 256 if hr % 256 == 0 else hr

    def body(core_ref, a_ref, b_ref, s_ref, sb_ref):
        s = a_ref[...] + b_ref[...]
        s_ref[...] = s
        sb_ref[...] = s.astype(BF16)

    nb = hr // tr
    grid_spec = pltpu.PrefetchScalarGridSpec(
        num_scalar_prefetch=1, grid=(4, nb),
        in_specs=[pl.BlockSpec((None, tr, cc), lambda qi, i, core_ref: (qi, core_ref[0] * nb + i, 0)),
                  pl.BlockSpec((None, tr, cc), lambda qi, i, core_ref: (qi, i, 0))],
        out_specs=[pl.BlockSpec((None, tr, cc), lambda qi, i, core_ref: (qi, i, 0)),
                   pl.BlockSpec((None, tr, cc), lambda qi, i, core_ref: (qi, i, 0))])
    return pl.pallas_call(
        body, name="pair_sum", grid_spec=grid_spec,
        out_shape=[jax.ShapeDtypeStruct((4, hr, cc), F32), jax.ShapeDtypeStruct((4, hr, cc), BF16)],
        compiler_params=_cp(),
    )(core, grad, got)


def scatter_to_owners(parts):
    n = len(parts)

    def body(*refs):
        ins, outs = refs[:n], refs[n:2 * n]
        send_sems, recv_sems = refs[2 * n:]
        x, y, c, chips = _place()
        copies = []
        for i in range(n):
            for j, chip in enumerate(chips):
                cp = pltpu.make_async_remote_copy(
                    src_ref=ins[i].at[2 * chip[0] + chip[1]], dst_ref=outs[i].at[j],
                    send_sem=send_sems.at[3 * i + j], recv_sem=recv_sems.at[3 * i + j],
                    device_id=(*chip, c), device_id_type=MESH)
                cp.start()
                copies.append(cp)
        for cp in copies:
            cp.wait()

    return pl.pallas_call(
        body, name="scatter_to_owners",
        in_specs=[ANY] * n, out_specs=[ANY] * n,
        out_shape=[jax.ShapeDtypeStruct((3,) + p.shape[1:], p.dtype) for p in parts],
        scratch_shapes=[pltpu.SemaphoreType.DMA((3 * n,)), pltpu.SemaphoreType.DMA((3 * n,))],
        compiler_params=pltpu.CompilerParams(has_side_effects=True),
    )(*parts)


def owner_sum(own, got, chip_core):
    _, hr, cc = own.shape
    tr = 256 if hr % 256 == 0 else hr
    nb = hr // tr

    def body(cc_ref, a_ref, b_ref, o_ref):
        s = a_ref[...] + b_ref[0].astype(F32)
        s = s + b_ref[1].astype(F32)
        o_ref[...] = s + b_ref[2].astype(F32)

    grid_spec = pltpu.PrefetchScalarGridSpec(
        num_scalar_prefetch=1, grid=(nb,),
        in_specs=[pl.BlockSpec((None, tr, cc), lambda i, cc_ref: (cc_ref[0], i, 0)),
                  pl.BlockSpec((3, tr, cc), lambda i, cc_ref: (0, i, 0))],
        out_specs=pl.BlockSpec((tr, cc), lambda i, cc_ref: (cc_ref[1] * nb + i, 0)))
    return pl.pallas_call(
        body, name="owner_sum", grid_spec=grid_spec,
        out_shape=jax.ShapeDtypeStruct((2 * hr, cc), F32), compiler_params=_cp(),
    )(chip_core, own, got)


def share_halves(bufs):
    n = len(bufs)

    def body(*refs):
        outs = refs[n:2 * n]
        send_sems, recv_sems = refs[2 * n:]
        x, y, c, _ = _place()
        copies = []
        for i in range(n):
            hr = bufs[i].shape[0] // 2
            mine = outs[i].at[pl.ds(c * hr, hr)]
            cp = pltpu.make_async_remote_copy(
                src_ref=mine, dst_ref=mine, send_sem=send_sems.at[i], recv_sem=recv_sems.at[i],
                device_id=(x, y, 1 - c), device_id_type=MESH)
            cp.start()
            copies.append((cp, outs[i].at[pl.ds((1 - c) * hr, hr)]))
        for i, (cp, theirs) in enumerate(copies):
            cp.wait_send()
            pltpu.make_async_remote_copy(
                src_ref=theirs, dst_ref=theirs, send_sem=send_sems.at[i], recv_sem=recv_sems.at[i],
                device_id=(x, y, c), device_id_type=MESH).wait_recv()

    return pl.pallas_call(
        body, name="share_halves",
        in_specs=[ANY] * n, out_specs=[ANY] * n,
        out_shape=[jax.ShapeDtypeStruct(b.shape, b.dtype) for b in bufs],
        input_output_aliases={i: i for i in range(n)},
        scratch_shapes=[pltpu.SemaphoreType.DMA((n,)), pltpu.SemaphoreType.DMA((n,))],
        compiler_params=pltpu.CompilerParams(has_side_effects=True),
    )(*bufs)


def allgather8(v, name):
    r, n = v.shape

    def body(v_ref, out_ref, send_sems, recv_sems):
        x, y, c, _ = _place()
        me = 4 * x + 2 * y + c
        out_ref[me] = v_ref[...]

        def copy(k, slot, to):
            return pltpu.make_async_remote_copy(
                src_ref=v_ref, dst_ref=out_ref.at[slot], send_sem=send_sems.at[k - 1],
                recv_sem=recv_sems.at[k - 1], device_id=to, device_id_type=MESH)

        peers = []
        for k in range(1, 8):
            px = 1 - x if (k >> 2) & 1 else x
            py = 1 - y if (k >> 1) & 1 else y
            pc = 1 - c if k & 1 else c
            peers.append((px, py, pc))
            copy(k, me, (px, py, pc)).start()
        for k, (px, py, pc) in enumerate(peers, start=1):
            copy(k, 4 * px + 2 * py + pc, (x, y, c)).wait_recv()
        for k, peer in enumerate(peers, start=1):
            copy(k, me, peer).wait_send()

    return pl.pallas_call(
        body, name=name, in_specs=[VMEM], out_specs=VMEM,
        out_shape=jax.ShapeDtypeStruct((8, r, n), v.dtype),
        scratch_shapes=[pltpu.SemaphoreType.DMA((7,)), pltpu.SemaphoreType.DMA((7,))],
        compiler_params=_cp(has_side_effects=True),
    )(v)


def put_in_slot(w, chip, dtype, name):
    r, c = w.shape
    tr = 256 if r % 256 == 0 else r

    def body(chip_ref, w_ref, o_ref):
        o_ref[...] = w_ref[...].astype(dtype)

    grid_spec = pltpu.PrefetchScalarGridSpec(
        num_scalar_prefetch=1, grid=(r // tr,),
        in_specs=[pl.BlockSpec((tr, c), lambda i, chip_ref: (i, 0))],
        out_specs=pl.BlockSpec((None, tr, c), lambda i, chip_ref: (chip_ref[0], i, 0)))
    return pl.pallas_call(body, name=name, grid_spec=grid_spec,
                          out_shape=jax.ShapeDtypeStruct((4, r, c), dtype), compiler_params=_cp())(chip, w)


def ada_fwd(s_in, ada_w, ada_b, tn):
    nl, d, ws = ada_w.shape

    def body(s_ref, w_ref, b_ref, so_ref, mod_ref):
        s = _silu(s_ref[...])
        so_ref[...] = s
        mod_ref[...] = _dot(s.astype(BF16), w_ref[...].astype(BF16)) + b_ref[...]

    return pl.pallas_call(
        body, name="ada_fwd", grid=(nl, ws // tn),
        in_specs=[pl.BlockSpec((16, d), lambda l, j: (0, 0)),
                  pl.BlockSpec((None, d, tn), lambda l, j: (l, 0, j)),
                  pl.BlockSpec((None, 1, tn), lambda l, j: (l, 0, j))],
        out_specs=[pl.BlockSpec((16, d), lambda l, j: (0, 0)),
                   pl.BlockSpec((None, 16, tn), lambda l, j: (l, 0, j))],
        out_shape=[jax.ShapeDtypeStruct((16, d), F32), jax.ShapeDtypeStruct((nl, 16, ws), F32)],
        compiler_params=_cp(),
    )(s_in, ada_w, ada_b)


def _adamw_math(w, g, m, v):
    m = ADAM_B1 * m + (1.0 - ADAM_B1) * g
    v = ADAM_B2 * v + (1.0 - ADAM_B2) * (g * g)
    m_hat = m / (1.0 - ADAM_B1 ** ADAM_STEP)
    v_hat = v / (1.0 - ADAM_B2 ** ADAM_STEP)
    delta = -ADAM_LR * (m_hat / (jnp.sqrt(v_hat) + ADAM_EPS) + ADAM_WD * w)
    return delta, m, v


def ada_bwd_adamw(s, dm, w, m, v):
    nl, d, ws = w.shape
    tr = 256 if d % 256 == 0 else 128

    def body(s_ref, dm_ref, w_ref, m_ref, v_ref, g_ref, dl_ref, mo_ref, vo_ref, dc_ref):
        dmv = dm_ref[...].astype(BF16)
        wv = w_ref[...]
        g = _dot(s_ref[...].astype(BF16), dmv, TN)
        g_ref[...] = g
        dl_ref[...], mo_ref[...], vo_ref[...] = _adamw_math(wv, g, m_ref[...], v_ref[...])
        dc_ref[...] = _dot(dmv[8:16, :], wv.astype(BF16), NT)

    wblk = pl.BlockSpec((None, tr, ws), lambda l, i: (l, i, 0))
    wshape = jax.ShapeDtypeStruct((nl, d, ws), F32)
    return pl.pallas_call(
        body, name="ada_bwd_adamw", grid=(nl, d // tr),
        in_specs=[pl.BlockSpec((16, tr), lambda l, i: (0, i)),
                  pl.BlockSpec((None, 16, ws), lambda l, i: (l, 0, 0)), wblk, wblk, wblk],
        out_specs=[wblk, wblk, wblk, wblk, pl.BlockSpec((None, 8, tr), lambda l, i: (l, 0, i))],
        out_shape=[wshape, wshape, wshape, wshape, jax.ShapeDtypeStruct((nl, 8, d), F32)],
        compiler_params=_cp(),
    )(s, dm, w, m, v)


def adamw(w, g, m, v, name):
    r, c = w.shape
    tr = 256 if r % 256 == 0 else r

    def body(w_ref, g_ref, m_ref, v_ref, dl_ref, mo_ref, vo_ref):
        dl_ref[...], mo_ref[...], vo_ref[...] = _adamw_math(w_ref[...], g_ref[...], m_ref[...], v_ref[...])

    blk = pl.BlockSpec((tr, c), lambda i: (i, 0))
    shape = jax.ShapeDtypeStruct((r, c), F32)
    return pl.pallas_call(body, name=name, grid=(r // tr,), in_specs=[blk] * 4, out_specs=[blk] * 3,
                          out_shape=[shape] * 3, compiler_params=_cp())(w, g, m, v)


SMALL_ROWS = 24
ROW_MOD = 10


def small_reduce(gathered):
    _, rows, d = gathered.shape

    def body(g_ref, o_ref):
        tot = g_ref[0]
        for b in range(1, 8):
            tot = tot + g_ref[b]
        o_ref[0:rows, :] = tot
        for layer in range(2):
            lat = ROW_MOD + 6 * layer
            o_ref[24 + 3 * layer:27 + 3 * layer, :] = tot[lat:lat + 3, :] + tot[lat + 3:lat + 6, :]
        o_ref[30:32, :] = jnp.zeros((2, d), F32)

    return pl.pallas_call(body, name="small_reduce", in_specs=[VMEM], out_specs=VMEM,
                          out_shape=jax.ShapeDtypeStruct((32, d), F32), compiler_params=_cp())(gathered)


def lb_logits_grad(lbl, dlb):
    _, _, n = lbl.shape

    def body(l_ref, d_ref, o_ref):
        for dr in range(2):
            _, (p0, p1, p2) = _lower_bound(l_ref, dr)
            dv = d_ref[dr:dr + 1, :]
            o_ref[dr, 0:1, :] = p0 * p2 * dv
            o_ref[dr, 1:2, :] = p1 * p2 * dv
            o_ref[dr, 2:3, :] = -p2 * (p0 + p1) * dv

    return pl.pallas_call(body, name="lb_logits_grad", in_specs=[VMEM, VMEM], out_specs=VMEM,
                          out_shape=jax.ShapeDtypeStruct((2, 3, n), F32), compiler_params=_cp())(lbl, dlb)


def c_ctx_grad(parts, c_ctx):
    d = c_ctx.shape[1]

    def body(p_ref, c_ref, o_ref):
        tot = p_ref[0, 0:1, :]
        for chip in range(1, 4):
            tot = tot + p_ref[2 * chip, 0:1, :]
        o_ref[...] = tot * _dsilu(c_ref[...])

    return pl.pallas_call(body, name="c_ctx_grad", in_specs=[VMEM, VMEM], out_specs=VMEM,
                          out_shape=jax.ShapeDtypeStruct((1, d), F32), compiler_params=_cp())(parts, c_ctx)


def _reduce_scatter(grads, core, chip_core):
    got = exchange_halves(grads)
    sums = [pair_sum(g, r, core) for g, r in zip(grads, got)]
    recv = scatter_to_owners([sb for _, sb in sums])
    reduced = [owner_sum(s, r, chip_core) for (s, _), r in zip(sums, recv)]
    return share_halves(reduced)


def kernel(x, c, ctx, c_ctx, ada_w, ada_b, pre_g, post_g, ev_w_in, ev_pool_w, ev_pool_scale, ev_conv_w, ev_conv_b, ev_w_out, od_w_in, od_onorm_g, od_w_out, lb_logits, loss_target, m_c_ctx, m_ada_w, m_ada_b, m_pre_g, m_post_g, m_ev_w_in, m_ev_pool_w, m_ev_pool_scale, m_ev_conv_w, m_ev_conv_b, m_ev_w_out, m_od_w_in, m_od_onorm_g, m_od_w_out, m_lb_logits, v_c_ctx, v_ada_w, v_ada_b, v_pre_g, v_post_g, v_ev_w_in, v_ev_pool_w, v_ev_pool_scale, v_ev_conv_w, v_ev_conv_b, v_ev_w_out, v_od_w_in, v_od_onorm_g, v_od_w_out, v_lb_logits):
    _, seq, d = x.shape
    cx = ctx.shape[1]
    t = cx + seq
    half_d = d // 2
    g = half_d // N_POOL
    tn = d // 4
    xi, yi, ci = lax.axis_index("x"), lax.axis_index("y"), lax.axis_index("c")
    chip = 2 * xi + yi
    me = 2 * chip + ci
    core_arr = jnp.reshape(ci, (1,)).astype(jnp.int32)
    chip_arr = jnp.reshape(chip, (1,)).astype(jnp.int32)
    chip_core_arr = jnp.stack([chip, ci]).astype(jnp.int32)

    pad = lambda a, rows: jnp.concatenate([a, jnp.zeros((rows - a.shape[0], g), F32)], axis=0)
    small = jnp.concatenate([
        ev_pool_w.reshape(g, g), pad(ev_conv_w.reshape(3, g), 8), pad(od_onorm_g.reshape(2, g), 8),
        pad(lb_logits.reshape(12, g), 16)], axis=0)
    ev_in_g, od_in_g, ev_out_g, od_out_g, small_g = allgather_shards([
        put_in_slot(ev_w_in[0], chip_arr, BF16, "cast_ev_w_in"),
        put_in_slot(od_w_in[0], chip_arr, BF16, "cast_od_w_in"),
        put_in_slot(ev_w_out[0], chip_arr, BF16, "cast_ev_w_out"),
        put_in_slot(od_w_out[0], chip_arr, BF16, "cast_od_w_out"),
        put_in_slot(small, chip_arr, F32, "place_small")])
    ev_out3 = ev_out_g.reshape(1, d, d)
    od_out3 = od_out_g.reshape(1, d, d)
    pool_w_full = small_g[:, :g].reshape(4, N_POOL, g // 4, g).transpose(1, 0, 2, 3).reshape(N_POOL, g, g)
    conv_w_full = small_g[:, g:g + 3].transpose(1, 0, 2).reshape(3, half_d)
    onorm_full = small_g[:, g + 8:g + 10].reshape(1, d)
    lbl_full = small_g[:, g + 16:g + 28].reshape(4, 2, 3, 2 * g).transpose(1, 2, 0, 3).reshape(2, 3, d)

    c_rows = jnp.concatenate([c, jnp.zeros((7, d), F32)], axis=0)
    c_all = allgather8(c_rows, "allgather_c")[:, 0, :]
    s_in = jnp.concatenate([c_all, c_ctx.reshape(1, d), jnp.zeros((7, d), F32)], axis=0)
    ws_ada = ada_w.shape[2]
    ada_b_mine = lax.dynamic_slice(ada_b, (0, chip * ws_ada), (2, ws_ada)).reshape(2, 1, ws_ada)
    s_act, mod_mine = ada_fwd(s_in, ada_w, ada_b_mine, tn)
    mod_all = allgather8(mod_mine.reshape(32, ws_ada), "allgather_mod")
    mod_full = mod_all[0::2].reshape(4, 2, 16, ws_ada).transpose(1, 2, 0, 3).reshape(2, 16, 3 * d)
    mod_lat = lax.dynamic_slice(mod_full, (0, me, 0), (2, 1, 3 * d))
    mods = jnp.concatenate([mod_full[:, 8:9], mod_lat], axis=1)
    shift, scale, gate = mods[:, :, :d], mods[:, :, d:2 * d], mods[:, :, 2 * d:]

    xs = jnp.concatenate([ctx[0], x[0]], axis=0)

    h0 = normmod_fwd(xs, pre_g[0:1], shift[0], scale[0], cx)
    z0 = mm_nn(h0, ev_in_g, half_d, tn, "mm_ev_in")
    u_a = mix_a_fwd(z0, pool_w_full, ev_pool_scale, cx)
    u_b = mix_b_fwd(z0, conv_w_full, ev_conv_b, cx)
    u = jnp.concatenate([u_a, u_b], axis=1)
    y0 = mm_nn(u, ev_out3, d, tn, "mm_ev_out")[0]
    xs1 = post_fwd(xs, y0, post_g[0:1], gate[0], cx)

    h1 = normmod_fwd(xs1, pre_g[1:2], shift[1], scale[1], cx)
    z1 = mm_nn(h1, od_in_g, d, tn, "mm_od_in")
    o1, r1 = hgrn_fwd(z1, lbl_full, onorm_full, cx)
    y1 = mm_nn(r1, od_out3, d, tn, "mm_od_out")[0]
    sq, dx2 = post_loss(xs1, y1, post_g[1:2], gate[1], loss_target[0], cx)
    loss = lax.psum(sq[0, 0] * (0.5 / d), ("x", "y", "c"))

    dy1, dgate1, dpost1 = post_bwd(dx2, y1, post_g[1:2], gate[1], cx, True)
    dr1 = mm_nt(dy1[None], od_out3, tn, "mm_od_out_dx")
    g_od_out = mm_tn(r1, dy1[None], d, tn, "mm_od_out_dw")
    dz1, donorm, dlb = hgrn_bwd(z1, lbl_full, onorm_full, o1, dr1, cx)
    dh1 = mm_nt(dz1, od_in_g, tn, "mm_od_in_dx")
    g_od_in = mm_tn(h1, dz1, od_in_g.shape[2], tn, "mm_od_in_dw")
    dxs1, dpre1, dshift1, dscale1 = normmod_bwd(xs1, dh1, pre_g[1:2], scale[1], dx2, cx, True)

    dy0, dgate0, dpost0 = post_bwd(dxs1, y0, post_g[0:1], gate[0], cx, False)
    du = mm_nt(dy0[None], ev_out3, tn, "mm_ev_out_dx")
    g_ev_out = mm_tn(u, dy0[None], d, tn, "mm_ev_out_dw")
    dz0a, g_pool_w, dpool_scale = mix_a_bwd(z0, du, pool_w_full, ev_pool_scale, cx)
    dz0b, dconv_w, dconv_b = mix_b_bwd(z0, du, conv_w_full, ev_conv_b, cx)
    dz0 = jnp.concatenate([dz0a, dz0b], axis=0)
    dh0 = mm_nt(dz0, ev_in_g, tn, "mm_ev_in_dx")
    g_ev_in = mm_tn(h0, dz0, ev_in_g.shape[2], tn, "mm_ev_in_dw")
    dxs0, dpre0, dshift0, dscale0 = normmod_bwd(xs, dh0, pre_g[0:1], scale[0], dxs1, cx, False)
    grad_x = dxs0[cx:][None]

    grad_ev_w_in, grad_od_w_in, grad_ev_w_out, grad_od_w_out, grad_pool_w = _reduce_scatter(
        [g_ev_in, g_od_in, g_ev_out.reshape(4, d // 4, d), g_od_out.reshape(4, d // 4, d),
         g_pool_w.reshape(4, g, g)], core_arr, chip_core_arr)

    zrow = jnp.zeros((1, d), F32)
    small_rows = jnp.concatenate([
        dpre0, dpre1, dpost0, dpost1,
        jnp.concatenate([dpool_scale, dconv_b], axis=1),
        jnp.concatenate([dconv_w.reshape(1, 3 * half_d), jnp.zeros((1, half_d), F32)], axis=1).reshape(2, d),
        donorm, dlb,
        dshift0[1:2], dscale0[1:2], dgate0[1:2], dshift0[0:1], dscale0[0:1], dgate0[0:1],
        dshift1[1:2], dscale1[1:2], dgate1[1:2], dshift1[0:1], dscale1[0:1], zrow,
        zrow, zrow], axis=0)
    small_all = allgather8(small_rows, "allgather_small")
    tot = small_reduce(small_all)

    dm_rows = []
    for layer in range(2):
        lat = ROW_MOD + 6 * layer
        dm_lat = small_all[:, lat:lat + 3].reshape(8, 3 * d)
        dm_ctx = tot[lat + 3:lat + 6].reshape(1, 3 * d)
        dm_rows.append(jnp.concatenate([dm_lat, dm_ctx, jnp.zeros((7, 3 * d), F32)], axis=0))
    dm_full = jnp.stack(dm_rows)
    dm_mine = lax.dynamic_slice(dm_full, (0, 0, chip * ws_ada), (2, 16, ws_ada))
    grad_ada_w, delta_ada_w, new_m_ada_w, new_v_ada_w, dctx_part = ada_bwd_adamw(
        s_act, dm_mine, ada_w, m_ada_w, v_ada_w)
    dctx_all = allgather8(dctx_part[0] + dctx_part[1], "allgather_dctx")
    grad_c_ctx = c_ctx_grad(dctx_all, c_ctx.reshape(1, d)).reshape(d)

    grad_ada_b = tot[24:30].reshape(2, 3 * d)
    grad_pre_g = tot[0:2]
    grad_post_g = tot[2:4]
    grad_ev_pool_scale = tot[4:5, :half_d]
    grad_ev_conv_b = tot[4:5, half_d:]
    conv_w_tot = tot[5:7].reshape(1, 2 * d)[:, :3 * half_d].reshape(3, N_POOL, g)
    grad_ev_conv_w = lax.dynamic_slice(conv_w_tot, (0, chip, 0), (3, 1, g)).reshape(1, 3, g)
    grad_od_onorm_g = lax.dynamic_slice(tot[7:8], (0, chip * 2 * g), (1, 2 * g))
    dlb_mine = lax.dynamic_slice(tot[8:10], (0, chip * 2 * g), (2, 2 * g))
    grad_lb_logits = lb_logits_grad(lb_logits, dlb_mine)
    grad_ev_w_in = grad_ev_w_in[None]
    grad_od_w_in = grad_od_w_in[None]
    grad_ev_w_out = grad_ev_w_out[None]
    grad_od_w_out = grad_od_w_out[None]
    grad_ev_pool_w = grad_pool_w.reshape(1, N_POOL, g // 4, g)

    def step(w, gr, m, v, name):
        shape = w.shape
        cols = shape[-1]
        two_d = lambda a: a.reshape(-1, cols)
        dl, mo, vo = adamw(two_d(w), two_d(gr), two_d(m), two_d(v), "adamw_" + name)
        return dl.reshape(shape), mo.reshape(shape), vo.reshape(shape)

    upd = {
        "c_ctx": step(c_ctx, grad_c_ctx, m_c_ctx, v_c_ctx, "c_ctx"),
        "ada_w": (delta_ada_w, new_m_ada_w, new_v_ada_w),
        "ada_b": step(ada_b, grad_ada_b, m_ada_b, v_ada_b, "ada_b"),
        "pre_g": step(pre_g, grad_pre_g, m_pre_g, v_pre_g, "pre_g"),
        "post_g": step(post_g, grad_post_g, m_post_g, v_post_g, "post_g"),
        "ev_w_in": step(ev_w_in, grad_ev_w_in, m_ev_w_in, v_ev_w_in, "ev_w_in"),
        "ev_pool_w": step(ev_pool_w, grad_ev_pool_w, m_ev_pool_w, v_ev_pool_w, "ev_pool_w"),
        "ev_pool_scale": step(ev_pool_scale, grad_ev_pool_scale, m_ev_pool_scale, v_ev_pool_scale, "ev_pool_scale"),
        "ev_conv_w": step(ev_conv_w, grad_ev_conv_w, m_ev_conv_w, v_ev_conv_w, "ev_conv_w"),
        "ev_conv_b": step(ev_conv_b, grad_ev_conv_b, m_ev_conv_b, v_ev_conv_b, "ev_conv_b"),
        "ev_w_out": step(ev_w_out, grad_ev_w_out, m_ev_w_out, v_ev_w_out, "ev_w_out"),
        "od_w_in": step(od_w_in, grad_od_w_in, m_od_w_in, v_od_w_in, "od_w_in"),
        "od_onorm_g": step(od_onorm_g, grad_od_onorm_g, m_od_onorm_g, v_od_onorm_g, "od_onorm_g"),
        "od_w_out": step(od_w_out, grad_od_w_out, m_od_w_out, v_od_w_out, "od_w_out"),
        "lb_logits": step(lb_logits, grad_lb_logits, m_lb_logits, v_lb_logits, "lb_logits"),
    }
    names = ["c_ctx", "ada_w", "ada_b", "pre_g", "post_g", "ev_w_in", "ev_pool_w", "ev_pool_scale",
             "ev_conv_w", "ev_conv_b", "ev_w_out", "od_w_in", "od_onorm_g", "od_w_out", "lb_logits"]
    grads = [grad_c_ctx, grad_ada_w, grad_ada_b, grad_pre_g, grad_post_g, grad_ev_w_in, grad_ev_pool_w,
             grad_ev_pool_scale, grad_ev_conv_w, grad_ev_conv_b, grad_ev_w_out, grad_od_w_in,
             grad_od_onorm_g, grad_od_w_out, grad_lb_logits]
    return (loss, grad_x, *grads, *[upd[k][0] for k in names], *[upd[k][1] for k in names],
            *[upd[k][2] for k in names])
```

```python
import functools

import jax
import jax.numpy as jnp
from jax import lax
from jax.experimental import pallas as pl
from jax.experimental.pallas import tpu as pltpu

EPS = 1e-6
GRID_W_LOG2 = 6
CHUNK = 64
HEAD = 128
N_POOL = 4
ADAM_LR, ADAM_B1, ADAM_B2, ADAM_EPS, ADAM_WD, ADAM_STEP = 0.001, 0.9, 0.999, 1e-08, 0.01, 10
VMEM_LIMIT = 56 * 1024 * 1024
MESH = pl.DeviceIdType.MESH
F32, BF16 = jnp.float32, jnp.bfloat16
ANY = pl.BlockSpec(memory_space=pl.ANY)
VMEM = pl.BlockSpec(memory_space=pltpu.VMEM)


def _cp(**kw):
    return pltpu.CompilerParams(vmem_limit_bytes=VMEM_LIMIT, **kw)


def _silu(x):
    return x * jax.nn.sigmoid(x)


def _dsilu(x):
    s = jax.nn.sigmoid(x)
    return s * (1.0 + x * (1.0 - s))


def _dot(a, b, dims=((1,), (0,)), precision=None):
    return lax.dot_general(a, b, (dims, ((), ())), preferred_element_type=F32, precision=precision)


NN = ((1,), (0,))
NT = ((1,), (1,))
TN = ((0,), (0,))


def _row_block(cx):
    return 256 if cx % 256 == 0 else 128


def normmod_fwd(xs, g, shift, scale, cx):
    t, d = xs.shape
    tm = _row_block(cx)
    nctx = cx // tm

    def body(x_ref, g_ref, sh_ref, sc_ref, h_ref):
        is_ctx = pl.program_id(0) < nctx
        x = x_ref[...]
        rstd = lax.rsqrt(jnp.mean(x * x, axis=-1, keepdims=True) + EPS)
        sc = jnp.where(is_ctx, sc_ref[0:1, :], sc_ref[1:2, :])
        sh = jnp.where(is_ctx, sh_ref[0:1, :], sh_ref[1:2, :])
        h_ref[...] = ((x * rstd) * g_ref[...] * (1.0 + sc) + sh).astype(BF16)

    row = pl.BlockSpec((tm, d), lambda i: (i, 0))
    vec = lambda r: pl.BlockSpec((r, d), lambda i: (0, 0))
    return pl.pallas_call(
        body, name="normmod_fwd", grid=(t // tm,),
        in_specs=[row, vec(1), vec(2), vec(2)], out_specs=row,
        out_shape=jax.ShapeDtypeStruct((t, d), BF16), compiler_params=_cp(),
    )(xs, g, shift, scale)


def normmod_bwd(xs, dh, g, scale, dres, cx, res_is_latent_only):
    t, d = xs.shape
    tm = _row_block(cx)
    nctx = cx // tm

    def body(x_ref, dh_ref, g_ref, sc_ref, dres_ref, dx_ref, dg_ref, dsh_ref, dsc_ref):
        i = pl.program_id(0)
        is_ctx = i < nctx

        @pl.when(i == 0)
        def _():
            dg_ref[...] = jnp.zeros_like(dg_ref)
            dsh_ref[...] = jnp.zeros_like(dsh_ref)
            dsc_ref[...] = jnp.zeros_like(dsc_ref)

        x = x_ref[...]
        dh = dh_ref[...]
        gv = g_ref[...]
        rstd = lax.rsqrt(jnp.mean(x * x, axis=-1, keepdims=True) + EPS)
        xhat = x * rstd
        sc = jnp.where(is_ctx, sc_ref[0:1, :], sc_ref[1:2, :])
        dsh = jnp.sum(dh, axis=0, keepdims=True)
        dhx = dh * xhat
        dsc = jnp.sum(dhx * gv, axis=0, keepdims=True)
        dg_ref[...] += jnp.sum(dhx * (1.0 + sc), axis=0, keepdims=True)
        zero = jnp.zeros_like(dsh)
        dsh_ref[0:1, :] += jnp.where(is_ctx, dsh, zero)
        dsh_ref[1:2, :] += jnp.where(is_ctx, zero, dsh)
        dsc_ref[0:1, :] += jnp.where(is_ctx, dsc, zero)
        dsc_ref[1:2, :] += jnp.where(is_ctx, zero, dsc)
        dxhat = dh * (gv * (1.0 + sc))
        dx = rstd * (dxhat - xhat * jnp.mean(dxhat * xhat, axis=-1, keepdims=True))
        res = dres_ref[...]
        if res_is_latent_only:
            res = jnp.where(is_ctx, jnp.zeros_like(res), res)
        dx_ref[...] = dx + res

    row = pl.BlockSpec((tm, d), lambda i: (i, 0))
    if res_is_latent_only:
        res_spec = pl.BlockSpec((tm, d), lambda i: (jnp.maximum(i - nctx, 0), 0))
    else:
        res_spec = row
    vec = lambda r: pl.BlockSpec((r, d), lambda i: (0, 0))
    return pl.pallas_call(
        body, name="normmod_bwd", grid=(t // tm,),
        in_specs=[row, row, vec(1), vec(2), res_spec],
        out_specs=[row, vec(1), vec(2), vec(2)],
        out_shape=[jax.ShapeDtypeStruct((t, d), F32), jax.ShapeDtypeStruct((1, d), F32),
                   jax.ShapeDtypeStruct((2, d), F32), jax.ShapeDtypeStruct((2, d), F32)],
        compiler_params=_cp(),
    )(xs, dh, g, scale, dres)


def post_fwd(xs, y, pg, gate, cx):
    t, d = xs.shape
    tm = _row_block(cx)
    nctx = cx // tm

    def body(x_ref, y_ref, pg_ref, gate_ref, o_ref):
        is_ctx = pl.program_id(0) < nctx
        y = y_ref[...]
        rstd = lax.rsqrt(jnp.mean(y * y, axis=-1, keepdims=True) + EPS)
        gt = jnp.where(is_ctx, gate_ref[0:1, :], gate_ref[1:2, :])
        o_ref[...] = x_ref[...] + gt * ((y * rstd) * pg_ref[...])

    row = pl.BlockSpec((tm, d), lambda i: (i, 0))
    vec = lambda r: pl.BlockSpec((r, d), lambda i: (0, 0))
    return pl.pallas_call(
        body, name="post_fwd", grid=(t // tm,),
        in_specs=[row, row, vec(1), vec(2)], out_specs=row,
        out_shape=jax.ShapeDtypeStruct((t, d), F32), compiler_params=_cp(),
    )(xs, y, pg, gate)


def post_loss(xs, y, pg, gate, target, cx):
    t, d = xs.shape
    n = y.shape[0]
    tm = _row_block(cx)
    nctx = cx // tm

    def body(x_ref, y_ref, pg_ref, gate_ref, tgt_ref, sq_ref, dx_ref):
        @pl.when(pl.program_id(0) == 0)
        def _():
            sq_ref[...] = jnp.zeros_like(sq_ref)

        y = y_ref[...]
        rstd = lax.rsqrt(jnp.mean(y * y, axis=-1, keepdims=True) + EPS)
        x2 = x_ref[...] + gate_ref[1:2, :] * ((y * rstd) * pg_ref[...])
        err = x2 - tgt_ref[...]
        sq_ref[...] += jnp.sum(err * err)
        dx_ref[...] = err * (1.0 / d)

    row = pl.BlockSpec((tm, d), lambda i: (i, 0))
    xrow = pl.BlockSpec((tm, d), lambda i: (i + nctx, 0))
    vec = lambda r: pl.BlockSpec((r, d), lambda i: (0, 0))
    return pl.pallas_call(
        body, name="post_loss", grid=(n // tm,),
        in_specs=[xrow, row, vec(1), vec(2), row],
        out_specs=[pl.BlockSpec((8, 128), lambda i: (0, 0)), row],
        out_shape=[jax.ShapeDtypeStruct((8, 128), F32), jax.ShapeDtypeStruct((n, d), F32)],
        compiler_params=_cp(),
    )(xs, y, pg, gate, target)


def post_bwd(dxo, y, pg, gate, cx, latent_only):
    m, d = y.shape
    tm = _row_block(cx)
    nctx = 0 if latent_only else cx // tm

    def body(dx_ref, y_ref, pg_ref, gate_ref, dy_ref, dgate_ref, dpg_ref):
        i = pl.program_id(0)
        is_ctx = i < nctx

        @pl.when(i == 0)
        def _():
            dgate_ref[...] = jnp.zeros_like(dgate_ref)
            dpg_ref[...] = jnp.zeros_like(dpg_ref)

        y = y_ref[...]
        dx = dx_ref[...]
        pgv = pg_ref[...]
        rstd = lax.rsqrt(jnp.mean(y * y, axis=-1, keepdims=True) + EPS)
        yhat = y * rstd
        gt = jnp.where(is_ctx, gate_ref[0:1, :], gate_ref[1:2, :])
        dxy = dx * yhat
        dgt = jnp.sum(dxy * pgv, axis=0, keepdims=True)
        zero = jnp.zeros_like(dgt)
        dgate_ref[0:1, :] += jnp.where(is_ctx, dgt, zero)
        dgate_ref[1:2, :] += jnp.where(is_ctx, zero, dgt)
        dpg_ref[...] += jnp.sum(dxy * gt, axis=0, keepdims=True)
        dyhat = dx * (gt * pgv)
        dy = rstd * (dyhat - yhat * jnp.mean(dyhat * yhat, axis=-1, keepdims=True))
        dy_ref[...] = dy.astype(BF16)

    row = pl.BlockSpec((tm, d), lambda i: (i, 0))
    vec = lambda r: pl.BlockSpec((r, d), lambda i: (0, 0))
    return pl.pallas_call(
        body, name="post_bwd", grid=(m // tm,),
        in_specs=[row, row, vec(1), vec(2)], out_specs=[row, vec(2), vec(1)],
        out_shape=[jax.ShapeDtypeStruct((m, d), BF16), jax.ShapeDtypeStruct((2, d), F32),
                   jax.ShapeDtypeStruct((1, d), F32)],
        compiler_params=_cp(),
    )(dxo, y, pg, gate)


def _split_rows(m):
    for cand in (1024, 768, 512, 384, 256, 128):
        if m % cand == 0 and m // cand >= 2:
            return cand
    return m


def mm_nn(a, w3, sec, tn, name):
    m, k = a.shape
    q, _, ws = w3.shape
    n = q * ws
    tpq, tps = ws // tn, sec // tn
    tm = 256 if m % 256 == 0 else 128

    def body(a_ref, w_ref, o_ref):
        w = w_ref[...]

        def step(i, carry):
            rows = pl.ds(pl.multiple_of(i * tm, tm), tm)
            o_ref[rows, :] = _dot(a_ref[rows, :], w)
            return carry

        lax.fori_loop(0, m // tm, step, 0)

    return pl.pallas_call(
        body, name=name, grid=(n // tn,),
        in_specs=[pl.BlockSpec((m, k), lambda j: (0, 0)),
                  pl.BlockSpec((None, k, tn), lambda j: (j // tpq, 0, j % tpq))],
        out_specs=pl.BlockSpec((None, m, tn), lambda j: (j // tps, 0, j % tps)),
        out_shape=jax.ShapeDtypeStruct((n // sec, m, sec), F32), compiler_params=_cp(),
    )(a, w3)


def mm_nt(a3, w3, tn, name):
    s, m, sec = a3.shape
    q, k, ws = w3.shape
    n = q * ws
    tpq, tps = ws // tn, sec // tn
    mb = _split_rows(m)

    def body(a_ref, w_ref, o_ref):
        @pl.when(pl.program_id(1) == 0)
        def _():
            o_ref[...] = jnp.zeros_like(o_ref)

        o_ref[...] += _dot(a_ref[...], w_ref[...], NT)

    return pl.pallas_call(
        body, name=name, grid=(m // mb, n // tn),
        in_specs=[pl.BlockSpec((None, mb, tn), lambda i, j: (j // tps, i, j % tps)),
                  pl.BlockSpec((None, k, tn), lambda i, j: (j // tpq, 0, j % tpq))],
        out_specs=pl.BlockSpec((mb, k), lambda i, j: (i, 0)),
        out_shape=jax.ShapeDtypeStruct((m, k), F32), compiler_params=_cp(),
    )(a3, w3)


def mm_tn(a, b3, ws, tn, name):
    m, k = a.shape
    s, _, sec = b3.shape
    n = s * sec
    tpq, tps = ws // tn, sec // tn
    kb = 256 if k % 256 == 0 else 128

    def body(a_ref, b_ref, o_ref):
        b = b_ref[...]
        for i in range(k // kb):
            o_ref[i * kb:(i + 1) * kb, :] = _dot(a_ref[:, i * kb:(i + 1) * kb], b, TN)

    return pl.pallas_call(
        body, name=name, grid=(n // tn,),
        in_specs=[pl.BlockSpec((m, k), lambda j: (0, 0)),
                  pl.BlockSpec((None, m, tn), lambda j: (j // tps, 0, j % tps))],
        out_specs=pl.BlockSpec((None, k, tn), lambda j: (j // tpq, 0, j % tpq)),
        out_shape=jax.ShapeDtypeStruct((n // ws, k, ws), F32), compiler_params=_cp(),
    )(a, b3)


def _pool_mask(gi, row0, tm, t, cx, seq, transposed):
    half = jnp.left_shift(1, gi)
    r = lax.broadcasted_iota(jnp.int32, (tm, 1), 0) + row0
    c = lax.broadcasted_iota(jnp.int32, (1, t), 1)
    out_tok, src_tok = (c, r) if transposed else (r, c)

    def parts(tok):
        lat = tok - cx
        return tok < cx, lat >> GRID_W_LOG2, lat & ((1 << GRID_W_LOG2) - 1)

    o_ctx, o_row, o_col = parts(out_tok)
    s_ctx, s_row, s_col = parts(src_tok)

    def inside(o, s):
        return (s >= o - half) & (s <= o + half - 1)

    ctx_hit = o_ctx & s_ctx & inside(out_tok, src_tok)
    lat_hit = (~o_ctx) & (~s_ctx) & inside(o_row, s_row) & inside(o_col, s_col)
    mask = jnp.where(ctx_hit | lat_hit, 1.0, 0.0).astype(BF16)

    own_ctx, own_row, own_col = parts(r)

    def count(pos, size):
        return jnp.minimum(pos + half - 1, size - 1) - jnp.maximum(pos - half, 0) + 1

    cnt = jnp.where(own_ctx, count(r, cx),
                    count(own_row, seq >> GRID_W_LOG2) * count(own_col, 1 << GRID_W_LOG2))
    return mask, 1.0 / cnt.astype(F32)


def mix_a_fwd(z0, pool_w, pool_scale, cx):
    _, t, half_d = z0.shape
    g = half_d // N_POOL
    seq = t - cx
    tm = _row_block(cx)

    def body(v_ref, ag_ref, w_ref, sc_ref, u_ref, vb_ref):
        gi = pl.program_id(0)
        vb_ref[...] = v_ref[...].astype(BF16)
        w = w_ref[...].astype(BF16)
        sc = sc_ref[...]

        def step(i, carry):
            row0 = pl.multiple_of(i * tm, tm)
            rows = pl.ds(row0, tm)
            mask, inv = _pool_mask(gi, row0, tm, t, cx, seq, False)
            pooled = _dot(mask, vb_ref[...]) * inv - v_ref[rows, :]
            mixed = _dot(pooled.astype(BF16), w) * sc
            u_ref[rows, :] = (mixed * _silu(ag_ref[rows, :])).astype(BF16)
            return carry

        lax.fori_loop(0, t // tm, step, 0)

    sec = lambda s: pl.BlockSpec((None, t, g), lambda j: (s, 0, j))
    return pl.pallas_call(
        body, name="mix_a_fwd", grid=(N_POOL,),
        in_specs=[sec(0), sec(1), pl.BlockSpec((None, g, g), lambda j: (j, 0, 0)),
                  pl.BlockSpec((1, g), lambda j: (0, j))],
        out_specs=pl.BlockSpec((t, g), lambda j: (0, j)),
        out_shape=jax.ShapeDtypeStruct((t, half_d), BF16),
        scratch_shapes=[pltpu.VMEM((t, g), BF16)], compiler_params=_cp(),
    )(z0, z0, pool_w, pool_scale)


def mix_a_bwd(z0, du, pool_w, pool_scale, cx):
    _, t, half_d = z0.shape
    g = half_d // N_POOL
    seq = t - cx
    tm = _row_block(cx)
    gq = g // 4

    def body(v_ref, ag_ref, du_ref, w_ref, sc_ref, dz_ref, dw_ref, dsc_ref,
             vb_ref, pooled_ref, dmx_ref, dpl_ref, wdp_ref):
        gi = pl.program_id(0)
        vb_ref[...] = v_ref[...].astype(BF16)
        w = w_ref[...].astype(BF16)
        sc = sc_ref[...]

        def first(i, dsc):
            row0 = pl.multiple_of(i * tm, tm)
            rows = pl.ds(row0, tm)
            mask, inv = _pool_mask(gi, row0, tm, t, cx, seq, False)
            pooled = (_dot(mask, vb_ref[...]) * inv - v_ref[rows, :]).astype(BF16)
            pooled_ref[rows, :] = pooled
            mixed = _dot(pooled, w)
            ag = ag_ref[rows, :]
            duv = du_ref[rows, :]
            dz_ref[1, rows, :] = (duv * (mixed * sc) * _dsilu(ag)).astype(BF16)
            dms = duv * _silu(ag)
            dmixed = (dms * sc).astype(BF16)
            dmx_ref[rows, :] = dmixed
            dpooled = _dot(dmixed, w, NT)
            dpl_ref[rows, :] = dpooled
            wdp_ref[rows, :] = (dpooled * inv).astype(BF16)
            return dsc + jnp.sum(dms * mixed, axis=0, keepdims=True)

        dsc_ref[...] = lax.fori_loop(0, t // tm, first, jnp.zeros((1, g), F32))
        dw = _dot(pooled_ref[...], dmx_ref[...], TN)
        for qi in range(4):
            dw_ref[qi] = dw[qi * gq:(qi + 1) * gq, :]

        def second(i, carry):
            row0 = pl.multiple_of(i * tm, tm)
            rows = pl.ds(row0, tm)
            mask_t, _ = _pool_mask(gi, row0, tm, t, cx, seq, True)
            dz_ref[0, rows, :] = (_dot(mask_t, wdp_ref[...]) - dpl_ref[rows, :]).astype(BF16)
            return carry

        lax.fori_loop(0, t // tm, second, 0)

    sec = lambda s: pl.BlockSpec((None, t, g), lambda j: (s, 0, j))
    return pl.pallas_call(
        body, name="mix_a_bwd", grid=(N_POOL,),
        in_specs=[sec(0), sec(1), pl.BlockSpec((t, g), lambda j: (0, j)),
                  pl.BlockSpec((None, g, g), lambda j: (j, 0, 0)),
                  pl.BlockSpec((1, g), lambda j: (0, j))],
        out_specs=[pl.BlockSpec((2, t, g), lambda j: (0, 0, j)),
                   pl.BlockSpec((4, None, gq, g), lambda j: (0, j, 0, 0)),
                   pl.BlockSpec((1, g), lambda j: (0, j))],
        out_shape=[jax.ShapeDtypeStruct((2, t, half_d), BF16),
                   jax.ShapeDtypeStruct((4, N_POOL, gq, g), F32),
                   jax.ShapeDtypeStruct((1, half_d), F32)],
        scratch_shapes=[pltpu.VMEM((t, g), BF16), pltpu.VMEM((t, g), BF16), pltpu.VMEM((t, g), BF16),
                        pltpu.VMEM((t, g), F32), pltpu.VMEM((t, g), BF16)],
        compiler_params=_cp(),
    )(z0, z0, du, pool_w, pool_scale)


def _conv_masks(t, cx):
    r = lax.broadcasted_iota(jnp.int32, (t, 1), 0)
    has_prev = jnp.where((r == 0) | (r == cx), 0.0, 1.0)
    has_next = jnp.where((r == cx - 1) | (r == t - 1), 0.0, 1.0)
    return has_prev, has_next


def mix_b_fwd(z0, conv_w, conv_b, cx):
    _, t, half_d = z0.shape
    gb = 128

    def body(bx_ref, bb_ref, bc_ref, bg_ref, w_ref, b_ref, u_ref):
        has_prev, has_next = _conv_masks(t, cx)
        tt = bc_ref[...] * bx_ref[...]
        prev = pltpu.roll(tt, 1, 0) * has_prev
        nxt = pltpu.roll(tt, t - 1, 0) * has_next
        cv = prev * w_ref[0:1, :] + tt * w_ref[1:2, :] + nxt * w_ref[2:3, :] + b_ref[...]
        u_ref[...] = (bb_ref[...] * cv * _silu(bg_ref[...])).astype(BF16)

    sec = lambda s: pl.BlockSpec((None, t, gb), lambda j: (s, 0, j))
    return pl.pallas_call(
        body, name="mix_b_fwd", grid=(half_d // gb,),
        in_specs=[sec(2), sec(3), sec(4), sec(5), pl.BlockSpec((3, gb), lambda j: (0, j)),
                  pl.BlockSpec((1, gb), lambda j: (0, j))],
        out_specs=pl.BlockSpec((t, gb), lambda j: (0, j)),
        out_shape=jax.ShapeDtypeStruct((t, half_d), BF16), compiler_params=_cp(),
    )(z0, z0, z0, z0, conv_w, conv_b)


def mix_b_bwd(z0, du, conv_w, conv_b, cx):
    _, t, half_d = z0.shape
    gb = 128
    off = half_d // gb

    def body(bx_ref, bb_ref, bc_ref, bg_ref, du_ref, w_ref, b_ref, dz_ref, dw_ref, db_ref):
        has_prev, has_next = _conv_masks(t, cx)
        bx, bb, bc, bg = bx_ref[...], bb_ref[...], bc_ref[...], bg_ref[...]
        duv = du_ref[...]
        tt = bc * bx
        prev = pltpu.roll(tt, 1, 0) * has_prev
        nxt = pltpu.roll(tt, t - 1, 0) * has_next
        w0, w1, w2 = w_ref[0:1, :], w_ref[1:2, :], w_ref[2:3, :]
        cv = prev * w0 + tt * w1 + nxt * w2 + b_ref[...]
        sg = _silu(bg)
        dz_ref[1] = (duv * cv * sg).astype(BF16)
        dz_ref[3] = (duv * bb * cv * _dsilu(bg)).astype(BF16)
        dcv = duv * bb * sg
        dw_ref[0:1, :] = jnp.sum(dcv * prev, axis=0, keepdims=True)
        dw_ref[1:2, :] = jnp.sum(dcv * tt, axis=0, keepdims=True)
        dw_ref[2:3, :] = jnp.sum(dcv * nxt, axis=0, keepdims=True)
        db_ref[...] = jnp.sum(dcv, axis=0, keepdims=True)
        dt = (pltpu.roll(dcv * has_prev, t - 1, 0) * w0 + dcv * w1
              + pltpu.roll(dcv * has_next, 1, 0) * w2)
        dz_ref[0] = (dt * bc).astype(BF16)
        dz_ref[2] = (dt * bx).astype(BF16)

    sec = lambda s: pl.BlockSpec((None, t, gb), lambda j: (s, 0, j))
    return pl.pallas_call(
        body, name="mix_b_bwd", grid=(half_d // gb,),
        in_specs=[sec(2), sec(3), sec(4), sec(5), pl.BlockSpec((t, gb), lambda j: (0, j + off)),
                  pl.BlockSpec((3, gb), lambda j: (0, j)), pl.BlockSpec((1, gb), lambda j: (0, j))],
        out_specs=[pl.BlockSpec((4, t, gb), lambda j: (0, 0, j)),
                   pl.BlockSpec((3, gb), lambda j: (0, j)), pl.BlockSpec((1, gb), lambda j: (0, j))],
        out_shape=[jax.ShapeDtypeStruct((4, t, half_d), BF16),
                   jax.ShapeDtypeStruct((3, half_d), F32), jax.ShapeDtypeStruct((1, half_d), F32)],
        compiler_params=_cp(),
    )(z0, z0, z0, z0, du, conv_w, conv_b)


def _lower_bound(lbl_ref, d):
    l0, l1, l2 = lbl_ref[d, 0:1, :], lbl_ref[d, 1:2, :], lbl_ref[d, 2:3, :]
    mx = jnp.maximum(jnp.maximum(l0, l1), l2)
    e0, e1, e2 = jnp.exp(l0 - mx), jnp.exp(l1 - mx), jnp.exp(l2 - mx)
    inv = 1.0 / (e0 + e1 + e2)
    return (e0 + e1) * inv, (e0 * inv, e1 * inv, e2 * inv)


def _chunk_consts(d):
    r = lax.broadcasted_iota(jnp.int32, (CHUNK, CHUNK), 0)
    c = lax.broadcasted_iota(jnp.int32, (CHUNK, CHUNK), 1)
    keep = (c <= r) if d == 0 else (c >= r)
    return jnp.where(keep, 1.0, 0.0).astype(F32), keep


def _chunk_of_step(s, d, nc, ncc):
    if d == 0:
        return s
    return jnp.where(s < ncc, ncc - 1 - s, nc - 1 + ncc - s)


def _chunk_terms(lfc, kc, qc, cum):
    bc = _dot(cum, lfc, precision=lax.Precision.HIGHEST)
    bl = jnp.sum(lfc, axis=0, keepdims=True)
    e = jnp.exp(bc)
    einv = jnp.exp(-bc)
    erem = jnp.exp(bl - bc)
    return e, einv, erem, jnp.exp(bl), qc * e, kc * einv, kc * erem


def hgrn_fwd(z1, lbl, onorm, cx):
    _, t, d = z1.shape
    seq = t - cx
    nc, ncc = t // CHUNK, cx // CHUNK

    def body(zf_ref, zb_ref, v_ref, q_ref, g_ref, lbl_ref, on_ref, o_ref, r_ref,
             lf_ref, k_ref, oacc_ref, st_ref):
        for dr, z_ref in ((0, zf_ref), (1, zb_ref)):
            lbv, _ = _lower_bound(lbl_ref, dr)
            z = z_ref[...]
            lf_ref[...] = jnp.log(lbv + (1.0 - lbv) * jax.nn.sigmoid(z))
            k_ref[...] = (1.0 - lbv) * jax.nn.sigmoid(-z)
            st_ref[...] = jnp.zeros_like(st_ref)
            cum, keep = _chunk_consts(dr)

            def step(s, carry, dr=dr, cum=cum, keep=keep):
                n = _chunk_of_step(s, dr, nc, ncc)
                rows = pl.ds(pl.multiple_of(n * CHUNK, CHUNK), CHUNK)
                vc = v_ref[rows, :].astype(BF16)
                _, _, _, dec, qd, ki, kd = _chunk_terms(lf_ref[rows, :], k_ref[rows, :], q_ref[rows, :], cum)
                qdb = qd.astype(BF16)
                a = jnp.where(keep, _dot(qdb, ki.astype(BF16), NT), 0.0)
                st = st_ref[...]
                oc = _dot(qdb, st.astype(BF16), NT) + _dot(a.astype(BF16), vc)
                st_ref[...] = st * dec + _dot(vc, kd.astype(BF16), TN)
                if dr == 0:
                    oacc_ref[rows, :] = oc
                else:
                    oacc_ref[rows, :] += oc
                return carry

            lax.fori_loop(0, nc, step, 0, unroll=4)

        o = oacc_ref[cx:, :]
        o_ref[...] = o
        rstd = lax.rsqrt(jnp.mean(o * o, axis=-1, keepdims=True) + EPS)
        r_ref[...] = (o * rstd * on_ref[...] * _silu(g_ref[cx:, :])).astype(BF16)

    sec = lambda s: pl.BlockSpec((None, t, HEAD), lambda h: (s, 0, h))
    col = pl.BlockSpec((seq, HEAD), lambda h: (0, h))
    return pl.pallas_call(
        body, name="hgrn_fwd", grid=(d // HEAD,),
        in_specs=[sec(0), sec(1), sec(2), sec(3), sec(4),
                  pl.BlockSpec((2, 3, HEAD), lambda h: (0, 0, h)), pl.BlockSpec((1, HEAD), lambda h: (0, h))],
        out_specs=[col, col],
        out_shape=[jax.ShapeDtypeStruct((seq, d), F32), jax.ShapeDtypeStruct((seq, d), BF16)],
        scratch_shapes=[pltpu.VMEM((t, HEAD), F32), pltpu.VMEM((t, HEAD), F32), pltpu.VMEM((t, HEAD), F32),
                        pltpu.VMEM((HEAD, HEAD), F32)],
        compiler_params=_cp(),
    )(z1, z1, z1, z1, z1, lbl, onorm)


def hgrn_bwd(z1, lbl, onorm, o, dr_out, cx):
    _, t, d = z1.shape
    seq = t - cx
    nc, ncc = t // CHUNK, cx // CHUNK

    def body(zf_ref, zb_ref, v_ref, q_ref, g_ref, lbl_ref, on_ref, o_ref, dr_ref,
             dz_ref, don_ref, dlb_ref,
             lf_ref, k_ref, do_ref, dq_ref, dv_ref, dk_ref, dlf_ref, ssc_ref, dst_ref):
        o = o_ref[...]
        g = g_ref[cx:, :]
        drv = dr_ref[...]
        onv = on_ref[...]
        rstd = lax.rsqrt(jnp.mean(o * o, axis=-1, keepdims=True) + EPS)
        ohat = o * rstd
        sg = _silu(g)
        don_ref[...] = jnp.sum(drv * ohat * sg, axis=0, keepdims=True)
        dz_ref[4, :cx, :] = jnp.zeros((cx, HEAD), BF16)
        dz_ref[4, cx:, :] = (drv * ohat * onv * _dsilu(g)).astype(BF16)
        dohat = drv * onv * sg
        do_ref[:cx, :] = jnp.zeros((cx, HEAD), F32)
        do_ref[cx:, :] = rstd * (dohat - ohat * jnp.mean(dohat * ohat, axis=-1, keepdims=True))

        for dr, z_ref in ((0, zf_ref), (1, zb_ref)):
            lbv, _ = _lower_bound(lbl_ref, dr)
            z = z_ref[...]
            lf_ref[...] = jnp.log(lbv + (1.0 - lbv) * jax.nn.sigmoid(z))
            k_ref[...] = (1.0 - lbv) * jax.nn.sigmoid(-z)
            cum, keep = _chunk_consts(dr)
            cum_t, _ = _chunk_consts(1 - dr)

            st_init = jnp.zeros((HEAD, HEAD), F32)

            def state_step(s, st, dr=dr, cum=cum):
                n = _chunk_of_step(s, dr, nc, ncc)
                rows = pl.ds(pl.multiple_of(n * CHUNK, CHUNK), CHUNK)
                ssc_ref[n] = st
                _, _, _, dec, _, _, kd = _chunk_terms(lf_ref[rows, :], k_ref[rows, :], q_ref[rows, :], cum)
                return st * dec + _dot(v_ref[rows, :].astype(BF16), kd.astype(BF16), TN)

            lax.fori_loop(0, nc, state_step, st_init, unroll=4)
            dst_ref[...] = jnp.zeros_like(dst_ref)

            def grad_step(s2, carry, dr=dr, cum=cum, cum_t=cum_t, keep=keep):
                n = _chunk_of_step(nc - 1 - s2, dr, nc, ncc)
                rows = pl.ds(pl.multiple_of(n * CHUNK, CHUNK), CHUNK)
                vc = v_ref[rows, :].astype(BF16)
                e, einv, erem, dec, qd, ki, kd = _chunk_terms(
                    lf_ref[rows, :], k_ref[rows, :], q_ref[rows, :], cum)
                qdb, kib, kdb = qd.astype(BF16), ki.astype(BF16), kd.astype(BF16)
                doc = do_ref[rows, :].astype(BF16)
                st0 = ssc_ref[n]
                dst = dst_ref[...]
                dstb = dst.astype(BF16)
                a = jnp.where(keep, _dot(qdb, kib, NT), 0.0).astype(BF16)
                da = jnp.where(keep, _dot(doc, vc, NT), 0.0).astype(BF16)
                dqd = _dot(doc, st0.astype(BF16)) + _dot(da, kib)
                dki = _dot(da, qdb, TN)
                dv = _dot(a, doc, TN) + _dot(kdb, dstb, NT)
                dkd = _dot(vc, dstb)
                ddec = jnp.sum(dst * st0, axis=0, keepdims=True)
                dst_ref[...] = _dot(doc, qdb, TN) + dst * dec
                dbc = dqd * qd - dki * ki - dkd * kd
                dbl = jnp.sum(dkd * kd, axis=0, keepdims=True) + ddec * dec
                dlf_ref[rows, :] = _dot(cum_t, dbc, precision=lax.Precision.HIGHEST) + dbl
                dk_ref[rows, :] = dki * einv + dkd * erem
                if dr == 0:
                    dq_ref[rows, :] = dqd * e
                    dv_ref[rows, :] = dv
                else:
                    dq_ref[rows, :] += dqd * e
                    dv_ref[rows, :] += dv
                return carry

            lax.fori_loop(0, nc, grad_step, 0, unroll=2)

            sig = jax.nn.sigmoid(z)
            one_lb = 1.0 - lbv
            f = lbv + one_lb * sig
            dlf = dlf_ref[...]
            dk = dk_ref[...]
            dsig = (dlf / f - dk) * one_lb
            dz_ref[dr] = (dsig * sig * (1.0 - sig)).astype(BF16)
            dlb_ref[dr:dr + 1, :] = jnp.sum((dlf / f - dk) * (1.0 - sig), axis=0, keepdims=True)

        dz_ref[2] = dv_ref[...].astype(BF16)
        dz_ref[3] = dq_ref[...].astype(BF16)

    sec = lambda s: pl.BlockSpec((None, t, HEAD), lambda h: (s, 0, h))
    col = pl.BlockSpec((seq, HEAD), lambda h: (0, h))
    tvec = pltpu.VMEM((t, HEAD), F32)
    return pl.pallas_call(
        body, name="hgrn_bwd", grid=(d // HEAD,),
        in_specs=[sec(0), sec(1), sec(2), sec(3), sec(4),
                  pl.BlockSpec((2, 3, HEAD), lambda h: (0, 0, h)), pl.BlockSpec((1, HEAD), lambda h: (0, h)),
                  col, col],
        out_specs=[pl.BlockSpec((5, t, HEAD), lambda h: (0, 0, h)),
                   pl.BlockSpec((1, HEAD), lambda h: (0, h)), pl.BlockSpec((2, HEAD), lambda h: (0, h))],
        out_shape=[jax.ShapeDtypeStruct((5, t, d), BF16), jax.ShapeDtypeStruct((1, d), F32),
                   jax.ShapeDtypeStruct((2, d), F32)],
        scratch_shapes=[tvec, tvec, tvec, tvec, tvec, tvec, tvec,
                        pltpu.VMEM((nc, HEAD, HEAD), F32), pltpu.VMEM((HEAD, HEAD), F32)],
        compiler_params=_cp(),
    )(z1, z1, z1, z1, z1, lbl, onorm, o, dr_out)


def _gates(z, lbv):
    e = jnp.exp(-jnp.abs(z))
    r = 1.0 / (1.0 + e)
    er = e * r
    pos = z >= 0.0
    sig = jnp.where(pos, r, er)
    nsig = jnp.where(pos, er, r)
    return sig, nsig, lbv + (1.0 - lbv) * sig


def _split3(x):
    hi = x.astype(BF16)
    r1 = x - hi.astype(F32)
    mid = r1.astype(BF16)
    lo = (r1 - mid.astype(F32)).astype(BF16)
    return jnp.concatenate([hi, mid, lo], axis=1)


def _cumsum_chunk(cum, x):
    y = _dot(cum, _split3(x))
    return y[:, :HEAD] + y[:, HEAD:2 * HEAD] + y[:, 2 * HEAD:]


def _chunk_rows(n):
    return pl.ds(pl.multiple_of(n * CHUNK, CHUNK), CHUNK)


def _group(nc):
    return next(u for u in (4, 3, 2, 1) if nc % u == 0)


def _decay_pass(lf_ref, bc_ref, dec_ref, cum, nc):
    grp = _group(nc)

    def step(m, carry):
        ns = [m * grp + u for u in range(grp)]
        lfc = [lf_ref[_chunk_rows(n), :] for n in ns]
        bc = [_cumsum_chunk(cum, x) for x in lfc]
        for u, n in enumerate(ns):
            bc_ref[_chunk_rows(n), :] = bc[u]
            dec_ref[n] = jnp.broadcast_to(jnp.exp(jnp.sum(lfc[u], axis=0, keepdims=True)), (8, HEAD))
        return carry

    lax.fori_loop(0, nc // grp, step, 0)


def hgrn_fwd(z1, lbl, onorm, cx):
    _, t, d = z1.shape
    seq = t - cx
    nc, ncc = t // CHUNK, cx // CHUNK

    grp = _group(nc)

    def body(zf_ref, zb_ref, v_ref, q_ref, g_ref, lbl_ref, on_ref, o_ref, r_ref,
             lf_ref, k_ref, bc_ref, dec_ref, qd_ref, ki_ref, oacc_ref, ds_ref):
        for dr, z_ref in ((0, zf_ref), (1, zb_ref)):
            lbv, _ = _lower_bound(lbl_ref, dr)
            _, nsig, f = _gates(z_ref[...], lbv)
            lf_ref[...] = jnp.log(f)
            k_ref[...] = (1.0 - lbv) * nsig
            cum, keep = _chunk_consts(dr)
            _decay_pass(lf_ref, bc_ref, dec_ref, cum.astype(BF16), nc)
            bc = bc_ref[...]
            qd_ref[...] = (q_ref[...] * jnp.exp(bc)).astype(BF16)
            ki_ref[...] = (k_ref[...] * jnp.exp(-bc)).astype(BF16)

            def local_step(m, carry, dr=dr, keep=keep):
                ns = [m * grp + u for u in range(grp)]
                rows = [_chunk_rows(n) for n in ns]
                qd = [qd_ref[r, :] for r in rows]
                ki = [ki_ref[r, :] for r in rows]
                vc = [v_ref[r, :].astype(BF16) for r in rows]
                sc = [_dot(qd[u], ki[u], NT) for u in range(grp)]
                inc = [_dot(vc[u], ki[u], TN) for u in range(grp)]
                a = [jnp.where(keep, s, 0.0).astype(BF16) for s in sc]
                intra = [_dot(a[u], vc[u]) for u in range(grp)]
                for u in range(grp):
                    ds_ref[ns[u]] = inc[u] * dec_ref[ns[u]][0:1, :]
                    if dr == 0:
                        oacc_ref[rows[u], :] = intra[u]
                    else:
                        oacc_ref[rows[u], :] += intra[u]
                return carry

            lax.fori_loop(0, nc // grp, local_step, 0)

            def state_step(m, st, dr=dr):
                ns = [_chunk_of_step(m * grp + u, dr, nc, ncc) for u in range(grp)]
                rows = [_chunk_rows(n) for n in ns]
                sts = []
                for n in ns:
                    sts.append(st.astype(BF16))
                    st = st * dec_ref[n][0:1, :] + ds_ref[n]
                inter = [_dot(qd_ref[rows[u], :], sts[u], NT) for u in range(grp)]
                for u in range(grp):
                    oacc_ref[rows[u], :] += inter[u]
                return st

            lax.fori_loop(0, nc // grp, state_step, jnp.zeros((HEAD, HEAD), F32))

        o = oacc_ref[cx:, :]
        o_ref[...] = o
        rstd = lax.rsqrt(jnp.mean(o * o, axis=-1, keepdims=True) + EPS)
        r_ref[...] = (o * rstd * on_ref[...] * _silu(g_ref[cx:, :])).astype(BF16)

    sec = lambda s: pl.BlockSpec((None, t, HEAD), lambda h: (s, 0, h))
    col = pl.BlockSpec((seq, HEAD), lambda h: (0, h))
    tf32, tb16 = pltpu.VMEM((t, HEAD), F32), pltpu.VMEM((t, HEAD), BF16)
    return pl.pallas_call(
        body, name="hgrn_fwd", grid=(d // HEAD,),
        in_specs=[sec(0), sec(1), sec(2), sec(3), sec(4),
                  pl.BlockSpec((2, 3, HEAD), lambda h: (0, 0, h)), pl.BlockSpec((1, HEAD), lambda h: (0, h))],
        out_specs=[col, col],
        out_shape=[jax.ShapeDtypeStruct((seq, d), F32), jax.ShapeDtypeStruct((seq, d), BF16)],
        scratch_shapes=[tf32, tf32, tf32, pltpu.VMEM((nc, 8, HEAD), F32), tb16, tb16, tf32,
                        pltpu.VMEM((nc, HEAD, HEAD), F32)],
        compiler_params=_cp(),
    )(z1, z1, z1, z1, z1, lbl, onorm)


def hgrn_bwd(z1, lbl, onorm, o, dr_out, cx):
    _, t, d = z1.shape
    seq = t - cx
    nc, ncc = t // CHUNK, cx // CHUNK

    grp2 = 2 if nc % 2 == 0 else 1
    grp = grp2

    def body(zf_ref, zb_ref, v_ref, q_ref, g_ref, lbl_ref, on_ref, o_ref, dr_ref,
             dz_ref, don_ref, dlb_ref,
             lf_ref, k_ref, bc_ref, dec_ref, qd_ref, ki_ref, do_ref,
             dqd_ref, dki_ref, dq_ref, dv_ref, ds_ref, dsl_ref):
        o = o_ref[...]
        g = g_ref[cx:, :]
        drv = dr_ref[...]
        onv = on_ref[...]
        rstd = lax.rsqrt(jnp.mean(o * o, axis=-1, keepdims=True) + EPS)
        ohat = o * rstd
        sg = _silu(g)
        don_ref[...] = jnp.sum(drv * ohat * sg, axis=0, keepdims=True)
        dz_ref[4, :cx, :] = jnp.zeros((cx, HEAD), BF16)
        dz_ref[4, cx:, :] = (drv * ohat * onv * _dsilu(g)).astype(BF16)
        dohat = drv * onv * sg
        do_ref[:cx, :] = jnp.zeros((cx, HEAD), BF16)
        do_ref[cx:, :] = (rstd * (dohat - ohat * jnp.mean(dohat * ohat, axis=-1, keepdims=True))).astype(BF16)

        for dr, z_ref in ((0, zf_ref), (1, zb_ref)):
            lbv, _ = _lower_bound(lbl_ref, dr)
            _, nsig, f = _gates(z_ref[...], lbv)
            lf_ref[...] = jnp.log(f)
            k_ref[...] = (1.0 - lbv) * nsig
            cum, keep = _chunk_consts(dr)
            cum_t = _chunk_consts(1 - dr)[0].astype(BF16)
            _decay_pass(lf_ref, bc_ref, dec_ref, cum.astype(BF16), nc)
            bc = bc_ref[...]
            qd_ref[...] = (q_ref[...] * jnp.exp(bc)).astype(BF16)
            ki_ref[...] = (k_ref[...] * jnp.exp(-bc)).astype(BF16)

            def local_step(m, carry, dr=dr, keep=keep):
                ns = [m * grp + u for u in range(grp)]
                rows = [_chunk_rows(n) for n in ns]
                rng = range(grp)
                qd = [qd_ref[r, :] for r in rows]
                ki = [ki_ref[r, :] for r in rows]
                doc = [do_ref[r, :] for r in rows]
                vc = [v_ref[r, :].astype(BF16) for r in rows]
                sc = [_dot(qd[u], ki[u], NT) for u in rng]
                dsc = [_dot(doc[u], vc[u], NT) for u in rng]
                inc = [_dot(vc[u], ki[u], TN) for u in rng]
                dinc = [_dot(doc[u], qd[u], TN) for u in rng]
                a = [jnp.where(keep, s, 0.0).astype(BF16) for s in sc]
                da = [jnp.where(keep, s, 0.0).astype(BF16) for s in dsc]
                dqd = [_dot(da[u], ki[u]) for u in rng]
                dki = [_dot(da[u], qd[u], TN) for u in rng]
                dv = [_dot(a[u], doc[u], TN) for u in rng]
                for u in rng:
                    ds_ref[ns[u]] = inc[u] * dec_ref[ns[u]][0:1, :]
                    dsl_ref[ns[u]] = dinc[u]
                    dqd_ref[rows[u], :] = dqd[u]
                    dki_ref[rows[u], :] = dki[u]
                    if dr == 0:
                        dv_ref[rows[u], :] = dv[u]
                    else:
                        dv_ref[rows[u], :] += dv[u]
                return carry

            lax.fori_loop(0, nc // grp, local_step, 0)

            def state_step(s, st, dr=dr):
                n = _chunk_of_step(s, dr, nc, ncc)
                inc = ds_ref[n]
                ds_ref[n] = st
                return st * dec_ref[n][0:1, :] + inc

            lax.fori_loop(0, nc, state_step, jnp.zeros((HEAD, HEAD), F32), unroll=4)

            def dstate_step(s, dst, dr=dr):
                n = _chunk_of_step(nc - 1 - s, dr, nc, ncc)
                inc = dsl_ref[n]
                dsl_ref[n] = dst
                return inc + dst * dec_ref[n][0:1, :]

            lax.fori_loop(0, nc, dstate_step, jnp.zeros((HEAD, HEAD), F32), unroll=4)

            def grad_step(m, carry, dr=dr, cum_t=cum_t):
                ns = [m * grp2 + u for u in range(grp2)]
                rows = [_chunk_rows(n) for n in ns]
                rng = range(grp2)
                st0 = [ds_ref[n] for n in ns]
                dst = [dsl_ref[n] for n in ns]
                dstb = [x.astype(BF16) for x in dst]
                dec = [dec_ref[n][0:1, :] for n in ns]
                doc = [do_ref[r, :] for r in rows]
                vc = [v_ref[r, :].astype(BF16) for r in rows]
                e = [jnp.exp(bc_ref[r, :]) for r in rows]
                einv = [jnp.exp(-bc_ref[r, :]) for r in rows]
                qd = [q_ref[rows[u], :] * e[u] for u in rng]
                ki = [k_ref[rows[u], :] * einv[u] for u in rng]
                kd = [ki[u] * dec[u] for u in rng]
                dqd_st = [_dot(doc[u], st0[u].astype(BF16)) for u in rng]
                dkd = [_dot(vc[u], dstb[u]) for u in rng]
                dv_st = [_dot(kd[u].astype(BF16), dstb[u], NT) for u in rng]
                dqd = [dqd_ref[rows[u], :] + dqd_st[u] for u in rng]
                dki = [dki_ref[r, :] for r in rows]
                dbc = [dqd[u] * qd[u] - dki[u] * ki[u] - dkd[u] * kd[u] for u in rng]
                cs = [_cumsum_chunk(cum_t, x) for x in dbc]
                for u in rng:
                    ddec = jnp.sum(dst[u] * st0[u], axis=0, keepdims=True)
                    dbl = jnp.sum(dkd[u] * kd[u], axis=0, keepdims=True) + ddec * dec[u]
                    dv_ref[rows[u], :] += dv_st[u]
                    dqd_ref[rows[u], :] = cs[u] + dbl
                    dki_ref[rows[u], :] = dki[u] * einv[u] + dkd[u] * (einv[u] * dec[u])
                    if dr == 0:
                        dq_ref[rows[u], :] = dqd[u] * e[u]
                    else:
                        dq_ref[rows[u], :] += dqd[u] * e[u]
                return carry

            lax.fori_loop(0, nc // grp2, grad_step, 0)

            sig, nsig, f = _gates(z_ref[...], lbv)
            common = (dqd_ref[...] / f - dki_ref[...]) * nsig
            dz_ref[dr] = (common * ((1.0 - lbv) * sig)).astype(BF16)
            dlb_ref[dr:dr + 1, :] = jnp.sum(common, axis=0, keepdims=True)

        dz_ref[2] = dv_ref[...].astype(BF16)
        dz_ref[3] = dq_ref[...].astype(BF16)

    sec = lambda s: pl.BlockSpec((None, t, HEAD), lambda h: (s, 0, h))
    col = pl.BlockSpec((seq, HEAD), lambda h: (0, h))
    tf32, tb16 = pltpu.VMEM((t, HEAD), F32), pltpu.VMEM((t, HEAD), BF16)
    states = pltpu.VMEM((nc, HEAD, HEAD), F32)
    return pl.pallas_call(
        body, name="hgrn_bwd", grid=(d // HEAD,),
        in_specs=[sec(0), sec(1), sec(2), sec(3), sec(4),
                  pl.BlockSpec((2, 3, HEAD), lambda h: (0, 0, h)), pl.BlockSpec((1, HEAD), lambda h: (0, h)),
                  col, col],
        out_specs=[pl.BlockSpec((5, t, HEAD), lambda h: (0, 0, h)),
                   pl.BlockSpec((1, HEAD), lambda h: (0, h)), pl.BlockSpec((2, HEAD), lambda h: (0, h))],
        out_shape=[jax.ShapeDtypeStruct((5, t, d), BF16), jax.ShapeDtypeStruct((1, d), F32),
                   jax.ShapeDtypeStruct((2, d), F32)],
        scratch_shapes=[tf32, tf32, tf32, pltpu.VMEM((nc, 8, HEAD), F32), tb16, tb16, tb16,
                        tf32, tf32, tf32, tf32, states, states],
        compiler_params=_cp(),
    )(z1, z1, z1, z1, z1, lbl, onorm, o, dr_out)


def _place():
    x, y, c = lax.axis_index("x"), lax.axis_index("y"), lax.axis_index("c")
    chips = [(1 - x, y), (x, 1 - y), (1 - x, 1 - y)]
    return x, y, c, chips


def allgather_shards(bufs):
    n = len(bufs)

    def body(*refs):
        outs = refs[n:2 * n]
        send_sems, recv_sems = refs[2 * n:]
        x, y, c, chips = _place()
        p = 2 * x + y
        half = [pl.ds(c * (s.shape[1] // 2), s.shape[1] // 2) for s in bufs]
        other = [pl.ds((1 - c) * (s.shape[1] // 2), s.shape[1] // 2) for s in bufs]

        def remote(i, k, src, dst, to):
            return pltpu.make_async_remote_copy(src_ref=src, dst_ref=dst, send_sem=send_sems.at[6 * i + k],
                                                recv_sem=recv_sems.at[6 * i + k], device_id=to, device_id_type=MESH)

        sends = []
        for i in range(n):
            for j, chip in enumerate(chips):
                mine = outs[i].at[p, half[i]]
                cp = remote(i, j, mine, mine, (*chip, c))
                cp.start()
                sends.append(cp)
        for i in range(n):
            for j, chip in enumerate(chips):
                landed = outs[i].at[2 * chip[0] + chip[1], half[i]]
                remote(i, j, landed, landed, (x, y, c)).wait_recv()
                cp = remote(i, 3 + j, landed, landed, (x, y, 1 - c))
                cp.start()
                sends.append(cp)
        for i in range(n):
            for j, chip in enumerate(chips):
                landed = outs[i].at[2 * chip[0] + chip[1], other[i]]
                remote(i, 3 + j, landed, landed, (x, y, c)).wait_recv()
        for cp in sends:
            cp.wait_send()

    return pl.pallas_call(
        body, name="allgather_shards",
        in_specs=[ANY] * n, out_specs=[ANY] * n,
        out_shape=[jax.ShapeDtypeStruct(s.shape, s.dtype) for s in bufs],
        input_output_aliases={i: i for i in range(n)},
        scratch_shapes=[pltpu.SemaphoreType.DMA((6 * n,)), pltpu.SemaphoreType.DMA((6 * n,))],
        compiler_params=pltpu.CompilerParams(has_side_effects=True),
    )(*bufs)


def exchange_halves(grads):
    n = len(grads)

    def body(*refs):
        ins, outs = refs[:n], refs[n:2 * n]
        send_sems, recv_sems = refs[2 * n:]
        x, y, c, _ = _place()
        copies = []
        for i in range(n):
            hr = grads[i].shape[1] // 2
            cp = pltpu.make_async_remote_copy(
                src_ref=ins[i].at[:, pl.ds((1 - c) * hr, hr)], dst_ref=outs[i],
                send_sem=send_sems.at[i], recv_sem=recv_sems.at[i],
                device_id=(x, y, 1 - c), device_id_type=MESH)
            cp.start()
            copies.append(cp)
        for cp in copies:
            cp.wait()

    return pl.pallas_call(
        body, name="exchange_halves",
        in_specs=[ANY] * n, out_specs=[ANY] * n,
        out_shape=[jax.ShapeDtypeStruct((4, g.shape[1] // 2, g.shape[2]), g.dtype) for g in grads],
        scratch_shapes=[pltpu.SemaphoreType.DMA((n,)), pltpu.SemaphoreType.DMA((n,))],
        compiler_params=pltpu.CompilerParams(has_side_effects=True),
    )(*grads)


def pair_sum(grad, got, core):
    _, r, cc = grad.shape
    hr = r // 2
    tr = 256 if hr % 256 == 0 else hr

    def body(core_ref, a_ref, b_ref, s_ref, sb_ref):
        s = a_ref[...] + b_ref[...]
        s_ref[...] = s
        sb_ref[...] = s.astype(BF16)

    nb = hr // tr
    grid_spec = pltpu.PrefetchScalarGridSpec(
        num_scalar_prefetch=1, grid=(4, nb),
        in_specs=[pl.BlockSpec((None, tr, cc), lambda qi, i, core_ref: (qi, core_ref[0] * nb + i, 0)),
                  pl.BlockSpec((None, tr, cc), lambda qi, i, core_ref: (qi, i, 0))],
        out_specs=[pl.BlockSpec((None, tr, cc), lambda qi, i, core_ref: (qi, i, 0)),
                   pl.BlockSpec((None, tr, cc), lambda qi, i, core_ref: (qi, i, 0))])
    return pl.pallas_call(
        body, name="pair_sum", grid_spec=grid_spec,
        out_shape=[jax.ShapeDtypeStruct((4, hr, cc), F32), jax.ShapeDtypeStruct((4, hr, cc), BF16)],
        compiler_params=_cp(),
    )(core, grad, got)


def scatter_to_owners(parts):
    n = len(parts)

    def body(*refs):
        ins, outs = refs[:n], refs[n:2 * n]
        send_sems, recv_sems = refs[2 * n:]
        x, y, c, chips = _place()
        copies = []
        for i in range(n):
            for j, chip in enumerate(chips):
                cp = pltpu.make_async_remote_copy(
                    src_ref=ins[i].at[2 * chip[0] + chip[1]], dst_ref=outs[i].at[j],
                    send_sem=send_sems.at[3 * i + j], recv_sem=recv_sems.at[3 * i + j],
                    device_id=(*chip, c), device_id_type=MESH)
                cp.start()
                copies.append(cp)
        for cp in copies:
            cp.wait()

    return pl.pallas_call(
        body, name="scatter_to_owners",
        in_specs=[ANY] * n, out_specs=[ANY] * n,
        out_shape=[jax.ShapeDtypeStruct((3,) + p.shape[1:], p.dtype) for p in parts],
        scratch_shapes=[pltpu.SemaphoreType.DMA((3 * n,)), pltpu.SemaphoreType.DMA((3 * n,))],
        compiler_params=pltpu.CompilerParams(has_side_effects=True),
    )(*parts)


def owner_sum(own, got, chip_core):
    _, hr, cc = own.shape
    tr = 256 if hr % 256 == 0 else hr
    nb = hr // tr

    def body(cc_ref, a_ref, b_ref, o_ref):
        s = a_ref[...] + b_ref[0].astype(F32)
        s = s + b_ref[1].astype(F32)
        o_ref[...] = s + b_ref[2].astype(F32)

    grid_spec = pltpu.PrefetchScalarGridSpec(
        num_scalar_prefetch=1, grid=(nb,),
        in_specs=[pl.BlockSpec((None, tr, cc), lambda i, cc_ref: (cc_ref[0], i, 0)),
                  pl.BlockSpec((3, tr, cc), lambda i, cc_ref: (0, i, 0))],
        out_specs=pl.BlockSpec((tr, cc), lambda i, cc_ref: (cc_ref[1] * nb + i, 0)))
    return pl.pallas_call(
        body, name="owner_sum", grid_spec=grid_spec,
        out_shape=jax.ShapeDtypeStruct((2 * hr, cc), F32), compiler_params=_cp(),
    )(chip_core, own, got)


def share_halves(bufs):
    n = len(bufs)

    def body(*refs):
        outs = refs[n:2 * n]
        send_sems, recv_sems = refs[2 * n:]
        x, y, c, _ = _place()
        copies = []
        for i in range(n):
            hr = bufs[i].shape[0] // 2
            mine = outs[i].at[pl.ds(c * hr, hr)]
            cp = pltpu.make_async_remote_copy(
                src_ref=mine, dst_ref=mine, send_sem=send_sems.at[i], recv_sem=recv_sems.at[i],
                device_id=(x, y, 1 - c), device_id_type=MESH)
            cp.start()
            copies.append((cp, outs[i].at[pl.ds((1 - c) * hr, hr)]))
        for i, (cp, theirs) in enumerate(copies):
            cp.wait_send()
            pltpu.make_async_remote_copy(
                src_ref=theirs, dst_ref=theirs, send_sem=send_sems.at[i], recv_sem=recv_sems.at[i],
                device_id=(x, y, c), device_id_type=MESH).wait_recv()

    return pl.pallas_call(
        body, name="share_halves",
        in_specs=[ANY] * n, out_specs=[ANY] * n,
        out_shape=[jax.ShapeDtypeStruct(b.shape, b.dtype) for b in bufs],
        input_output_aliases={i: i for i in range(n)},
        scratch_shapes=[pltpu.SemaphoreType.DMA((n,)), pltpu.SemaphoreType.DMA((n,))],
        compiler_params=pltpu.CompilerParams(has_side_effects=True),
    )(*bufs)


def allgather8(v, name):
    r, n = v.shape

    def body(v_ref, out_ref, send_sems, recv_sems):
        x, y, c, _ = _place()
        me = 4 * x + 2 * y + c
        out_ref[me] = v_ref[...]

        def copy(k, slot, to):
            return pltpu.make_async_remote_copy(
                src_ref=v_ref, dst_ref=out_ref.at[slot], send_sem=send_sems.at[k - 1],
                recv_sem=recv_sems.at[k - 1], device_id=to, device_id_type=MESH)

        peers = []
        for k in range(1, 8):
            px = 1 - x if (k >> 2) & 1 else x
            py = 1 - y if (k >> 1) & 1 else y
            pc = 1 - c if k & 1 else c
            peers.append((px, py, pc))
            copy(k, me, (px, py, pc)).start()
        for k, (px, py, pc) in enumerate(peers, start=1):
            copy(k, 4 * px + 2 * py + pc, (x, y, c)).wait_recv()
        for k, peer in enumerate(peers, start=1):
            copy(k, me, peer).wait_send()

    return pl.pallas_call(
        body, name=name, in_specs=[VMEM], out_specs=VMEM,
        out_shape=jax.ShapeDtypeStruct((8, r, n), v.dtype),
        scratch_shapes=[pltpu.SemaphoreType.DMA((7,)), pltpu.SemaphoreType.DMA((7,))],
        compiler_params=_cp(has_side_effects=True),
    )(v)


def put_in_slot(w, chip, dtype, name):
    r, c = w.shape
    tr = 256 if r % 256 == 0 else r

    def body(chip_ref, w_ref, o_ref):
        o_ref[...] = w_ref[...].astype(dtype)

    grid_spec = pltpu.PrefetchScalarGridSpec(
        num_scalar_prefetch=1, grid=(r // tr,),
        in_specs=[pl.BlockSpec((tr, c), lambda i, chip_ref: (i, 0))],
        out_specs=pl.BlockSpec((None, tr, c), lambda i, chip_ref: (chip_ref[0], i, 0)))
    return pl.pallas_call(body, name=name, grid_spec=grid_spec,
                          out_shape=jax.ShapeDtypeStruct((4, r, c), dtype), compiler_params=_cp())(chip, w)


def ada_fwd(s_in, ada_w, ada_b, tn):
    nl, d, ws = ada_w.shape

    def body(s_ref, w_ref, b_ref, so_ref, mod_ref):
        s = _silu(s_ref[...])
        so_ref[...] = s
        mod_ref[...] = _dot(s.astype(BF16), w_ref[...].astype(BF16)) + b_ref[...]

    return pl.pallas_call(
        body, name="ada_fwd", grid=(nl, ws // tn),
        in_specs=[pl.BlockSpec((16, d), lambda l, j: (0, 0)),
                  pl.BlockSpec((None, d, tn), lambda l, j: (l, 0, j)),
                  pl.BlockSpec((None, 1, tn), lambda l, j: (l, 0, j))],
        out_specs=[pl.BlockSpec((16, d), lambda l, j: (0, 0)),
                   pl.BlockSpec((None, 16, tn), lambda l, j: (l, 0, j))],
        out_shape=[jax.ShapeDtypeStruct((16, d), F32), jax.ShapeDtypeStruct((nl, 16, ws), F32)],
        compiler_params=_cp(),
    )(s_in, ada_w, ada_b)


def _adamw_math(w, g, m, v):
    m = ADAM_B1 * m + (1.0 - ADAM_B1) * g
    v = ADAM_B2 * v + (1.0 - ADAM_B2) * (g * g)
    m_hat = m / (1.0 - ADAM_B1 ** ADAM_STEP)
    v_hat = v / (1.0 - ADAM_B2 ** ADAM_STEP)
    delta = -ADAM_LR * (m_hat / (jnp.sqrt(v_hat) + ADAM_EPS) + ADAM_WD * w)
    return delta, m, v


def ada_bwd_adamw(s, dm, w, m, v):
    nl, d, ws = w.shape
    tr = 256 if d % 256 == 0 else 128

    def body(s_ref, dm_ref, w_ref, m_ref, v_ref, g_ref, dl_ref, mo_ref, vo_ref, dc_ref):
        dmv = dm_ref[...].astype(BF16)
        wv = w_ref[...]
        g = _dot(s_ref[...].astype(BF16), dmv, TN)
        g_ref[...] = g
        dl_ref[...], mo_ref[...], vo_ref[...] = _adamw_math(wv, g, m_ref[...], v_ref[...])
        dc_ref[...] = _dot(dmv[8:16, :], wv.astype(BF16), NT)

    wblk = pl.BlockSpec((None, tr, ws), lambda l, i: (l, i, 0))
    wshape = jax.ShapeDtypeStruct((nl, d, ws), F32)
    return pl.pallas_call(
        body, name="ada_bwd_adamw", grid=(nl, d // tr),
        in_specs=[pl.BlockSpec((16, tr), lambda l, i: (0, i)),
                  pl.BlockSpec((None, 16, ws), lambda l, i: (l, 0, 0)), wblk, wblk, wblk],
        out_specs=[wblk, wblk, wblk, wblk, pl.BlockSpec((None, 8, tr), lambda l, i: (l, 0, i))],
        out_shape=[wshape, wshape, wshape, wshape, jax.ShapeDtypeStruct((nl, 8, d), F32)],
        compiler_params=_cp(),
    )(s, dm, w, m, v)


def adamw(w, g, m, v, name):
    r, c = w.shape
    tr = 256 if r % 256 == 0 else r

    def body(w_ref, g_ref, m_ref, v_ref, dl_ref, mo_ref, vo_ref):
        dl_ref[...], mo_ref[...], vo_ref[...] = _adamw_math(w_ref[...], g_ref[...], m_ref[...], v_ref[...])

    blk = pl.BlockSpec((tr, c), lambda i: (i, 0))
    shape = jax.ShapeDtypeStruct((r, c), F32)
    return pl.pallas_call(body, name=name, grid=(r // tr,), in_specs=[blk] * 4, out_specs=[blk] * 3,
                          out_shape=[shape] * 3, compiler_params=_cp())(w, g, m, v)


SMALL_ROWS = 24
ROW_MOD = 10


def small_reduce(gathered):
    _, rows, d = gathered.shape

    def body(g_ref, o_ref):
        tot = g_ref[0]
        for b in range(1, 8):
            tot = tot + g_ref[b]
        o_ref[0:rows, :] = tot
        for layer in range(2):
            lat = ROW_MOD + 6 * layer
            o_ref[24 + 3 * layer:27 + 3 * layer, :] = tot[lat:lat + 3, :] + tot[lat + 3:lat + 6, :]
        o_ref[30:32, :] = jnp.zeros((2, d), F32)

    return pl.pallas_call(body, name="small_reduce", in_specs=[VMEM], out_specs=VMEM,
                          out_shape=jax.ShapeDtypeStruct((32, d), F32), compiler_params=_cp())(gathered)


def lb_logits_grad(lbl, dlb):
    _, _, n = lbl.shape

    def body(l_ref, d_ref, o_ref):
        for dr in range(2):
            _, (p0, p1, p2) = _lower_bound(l_ref, dr)
            dv = d_ref[dr:dr + 1, :]
            o_ref[dr, 0:1, :] = p0 * p2 * dv
            o_ref[dr, 1:2, :] = p1 * p2 * dv
            o_ref[dr, 2:3, :] = -p2 * (p0 + p1) * dv

    return pl.pallas_call(body, name="lb_logits_grad", in_specs=[VMEM, VMEM], out_specs=VMEM,
                          out_shape=jax.ShapeDtypeStruct((2, 3, n), F32), compiler_params=_cp())(lbl, dlb)


def c_ctx_grad(parts, c_ctx):
    d = c_ctx.shape[1]

    def body(p_ref, c_ref, o_ref):
        tot = p_ref[0, 0:1, :]
        for chip in range(1, 4):
            tot = tot + p_ref[2 * chip, 0:1, :]
        o_ref[...] = tot * _dsilu(c_ref[...])

    return pl.pallas_call(body, name="c_ctx_grad", in_specs=[VMEM, VMEM], out_specs=VMEM,
                          out_shape=jax.ShapeDtypeStruct((1, d), F32), compiler_params=_cp())(parts, c_ctx)


def _reduce_scatter(grads, core, chip_core):
    got = exchange_halves(grads)
    sums = [pair_sum(g, r, core) for g, r in zip(grads, got)]
    recv = scatter_to_owners([sb for _, sb in sums])
    reduced = [owner_sum(s, r, chip_core) for (s, _), r in zip(sums, recv)]
    return share_halves(reduced)


def kernel(x, c, ctx, c_ctx, ada_w, ada_b, pre_g, post_g, ev_w_in, ev_pool_w, ev_pool_scale, ev_conv_w, ev_conv_b, ev_w_out, od_w_in, od_onorm_g, od_w_out, lb_logits, loss_target, m_c_ctx, m_ada_w, m_ada_b, m_pre_g, m_post_g, m_ev_w_in, m_ev_pool_w, m_ev_pool_scale, m_ev_conv_w, m_ev_conv_b, m_ev_w_out, m_od_w_in, m_od_onorm_g, m_od_w_out, m_lb_logits, v_c_ctx, v_ada_w, v_ada_b, v_pre_g, v_post_g, v_ev_w_in, v_ev_pool_w, v_ev_pool_scale, v_ev_conv_w, v_ev_conv_b, v_ev_w_out, v_od_w_in, v_od_onorm_g, v_od_w_out, v_lb_logits):
    _, seq, d = x.shape
    cx = ctx.shape[1]
    t = cx + seq
    half_d = d // 2
    g = half_d // N_POOL
    tn = d // 4
    xi, yi, ci = lax.axis_index("x"), lax.axis_index("y"), lax.axis_index("c")
    chip = 2 * xi + yi
    me = 2 * chip + ci
    core_arr = jnp.reshape(ci, (1,)).astype(jnp.int32)
    chip_arr = jnp.reshape(chip, (1,)).astype(jnp.int32)
    chip_core_arr = jnp.stack([chip, ci]).astype(jnp.int32)

    pad = lambda a, rows: jnp.concatenate([a, jnp.zeros((rows - a.shape[0], g), F32)], axis=0)
    small = jnp.concatenate([
        ev_pool_w.reshape(g, g), pad(ev_conv_w.reshape(3, g), 8), pad(od_onorm_g.reshape(2, g), 8),
        pad(lb_logits.reshape(12, g), 16)], axis=0)
    ev_in_g, od_in_g, ev_out_g, od_out_g, small_g = allgather_shards([
        put_in_slot(ev_w_in[0], chip_arr, BF16, "cast_ev_w_in"),
        put_in_slot(od_w_in[0], chip_arr, BF16, "cast_od_w_in"),
        put_in_slot(ev_w_out[0], chip_arr, BF16, "cast_ev_w_out"),
        put_in_slot(od_w_out[0], chip_arr, BF16, "cast_od_w_out"),
        put_in_slot(small, chip_arr, F32, "place_small")])
    ev_out3 = ev_out_g.reshape(1, d, d)
    od_out3 = od_out_g.reshape(1, d, d)
    pool_w_full = small_g[:, :g].reshape(4, N_POOL, g // 4, g).transpose(1, 0, 2, 3).reshape(N_POOL, g, g)
    conv_w_full = small_g[:, g:g + 3].transpose(1, 0, 2).reshape(3, half_d)
    onorm_full = small_g[:, g + 8:g + 10].reshape(1, d)
    lbl_full = small_g[:, g + 16:g + 28].reshape(4, 2, 3, 2 * g).transpose(1, 2, 0, 3).reshape(2, 3, d)

    c_rows = jnp.concatenate([c, jnp.zeros((7, d), F32)], axis=0)
    c_all = allgather8(c_rows, "allgather_c")[:, 0, :]
    s_in = jnp.concatenate([c_all, c_ctx.reshape(1, d), jnp.zeros((7, d), F32)], axis=0)
    ws_ada = ada_w.shape[2]
    ada_b_mine = lax.dynamic_slice(ada_b, (0, chip * ws_ada), (2, ws_ada)).reshape(2, 1, ws_ada)
    s_act, mod_mine = ada_fwd(s_in, ada_w, ada_b_mine, tn)
    mod_all = allgather8(mod_mine.reshape(32, ws_ada), "allgather_mod")
    mod_full = mod_all[0::2].reshape(4, 2, 16, ws_ada).transpose(1, 2, 0, 3).reshape(2, 16, 3 * d)
    mod_lat = lax.dynamic_slice(mod_full, (0, me, 0), (2, 1, 3 * d))
    mods = jnp.concatenate([mod_full[:, 8:9], mod_lat], axis=1)
    shift, scale, gate = mods[:, :, :d], mods[:, :, d:2 * d], mods[:, :, 2 * d:]

    xs = jnp.concatenate([ctx[0], x[0]], axis=0)

    h0 = normmod_fwd(xs, pre_g[0:1], shift[0], scale[0], cx)
    z0 = mm_nn(h0, ev_in_g, half_d, tn, "mm_ev_in")
    u_a = mix_a_fwd(z0, pool_w_full, ev_pool_scale, cx)
    u_b = mix_b_fwd(z0, conv_w_full, ev_conv_b, cx)
    u = jnp.concatenate([u_a, u_b], axis=1)
    y0 = mm_nn(u, ev_out3, d, tn, "mm_ev_out")[0]
    xs1 = post_fwd(xs, y0, post_g[0:1], gate[0], cx)

    h1 = normmod_fwd(xs1, pre_g[1:2], shift[1], scale[1], cx)
    z1 = mm_nn(h1, od_in_g, d, tn, "mm_od_in")
    o1, r1 = hgrn_fwd(z1, lbl_full, onorm_full, cx)
    y1 = mm_nn(r1, od_out3, d, tn, "mm_od_out")[0]
    sq, dx2 = post_loss(xs1, y1, post_g[1:2], gate[1], loss_target[0], cx)
    loss = lax.psum(sq[0, 0] * (0.5 / d), ("x", "y", "c"))

    dy1, dgate1, dpost1 = post_bwd(dx2, y1, post_g[1:2], gate[1], cx, True)
    dr1 = mm_nt(dy1[None], od_out3, tn, "mm_od_out_dx")
    g_od_out = mm_tn(r1, dy1[None], d, tn, "mm_od_out_dw")
    dz1, donorm, dlb = hgrn_bwd(z1, lbl_full, onorm_full, o1, dr1, cx)
    dh1 = mm_nt(dz1, od_in_g, tn, "mm_od_in_dx")
    g_od_in = mm_tn(h1, dz1, od_in_g.shape[2], tn, "mm_od_in_dw")
    dxs1, dpre1, dshift1, dscale1 = normmod_bwd(xs1, dh1, pre_g[1:2], scale[1], dx2, cx, True)

    dy0, dgate0, dpost0 = post_bwd(dxs1, y0, post_g[0:1], gate[0], cx, False)
    du = mm_nt(dy0[None], ev_out3, tn, "mm_ev_out_dx")
    g_ev_out = mm_tn(u, dy0[None], d, tn, "mm_ev_out_dw")
    dz0a, g_pool_w, dpool_scale = mix_a_bwd(z0, du, pool_w_full, ev_pool_scale, cx)
    dz0b, dconv_w, dconv_b = mix_b_bwd(z0, du, conv_w_full, ev_conv_b, cx)
    dz0 = jnp.concatenate([dz0a, dz0b], axis=0)
    dh0 = mm_nt(dz0, ev_in_g, tn, "mm_ev_in_dx")
    g_ev_in = mm_tn(h0, dz0, ev_in_g.shape[2], tn, "mm_ev_in_dw")
    dxs0, dpre0, dshift0, dscale0 = normmod_bwd(xs, dh0, pre_g[0:1], scale[0], dxs1, cx, False)
    grad_x = dxs0[cx:][None]

    grad_ev_w_in, grad_od_w_in, grad_ev_w_out, grad_od_w_out, grad_pool_w = _reduce_scatter(
        [g_ev_in, g_od_in, g_ev_out.reshape(4, d // 4, d), g_od_out.reshape(4, d // 4, d),
         g_pool_w.reshape(4, g, g)], core_arr, chip_core_arr)

    zrow = jnp.zeros((1, d), F32)
    small_rows = jnp.concatenate([
        dpre0, dpre1, dpost0, dpost1,
        jnp.concatenate([dpool_scale, dconv_b], axis=1),
        jnp.concatenate([dconv_w.reshape(1, 3 * half_d), jnp.zeros((1, half_d), F32)], axis=1).reshape(2, d),
        donorm, dlb,
        dshift0[1:2], dscale0[1:2], dgate0[1:2], dshift0[0:1], dscale0[0:1], dgate0[0:1],
        dshift1[1:2], dscale1[1:2], dgate1[1:2], dshift1[0:1], dscale1[0:1], zrow,
        zrow, zrow], axis=0)
    small_all = allgather8(small_rows, "allgather_small")
    tot = small_reduce(small_all)

    dm_rows = []
    for layer in range(2):
        lat = ROW_MOD + 6 * layer
        dm_lat = small_all[:, lat:lat + 3].reshape(8, 3 * d)
        dm_ctx = tot[lat + 3:lat + 6].reshape(1, 3 * d)
        dm_rows.append(jnp.concatenate([dm_lat, dm_ctx, jnp.zeros((7, 3 * d), F32)], axis=0))
    dm_full = jnp.stack(dm_rows)
    dm_mine = lax.dynamic_slice(dm_full, (0, 0, chip * ws_ada), (2, 16, ws_ada))
    grad_ada_w, delta_ada_w, new_m_ada_w, new_v_ada_w, dctx_part = ada_bwd_adamw(
        s_act, dm_mine, ada_w, m_ada_w, v_ada_w)
    dctx_all = allgather8(dctx_part[0] + dctx_part[1], "allgather_dctx")
    grad_c_ctx = c_ctx_grad(dctx_all, c_ctx.reshape(1, d)).reshape(d)

    grad_ada_b = tot[24:30].reshape(2, 3 * d)
    grad_pre_g = tot[0:2]
    grad_post_g = tot[2:4]
    grad_ev_pool_scale = tot[4:5, :half_d]
    grad_ev_conv_b = tot[4:5, half_d:]
    conv_w_tot = tot[5:7].reshape(1, 2 * d)[:, :3 * half_d].reshape(3, N_POOL, g)
    grad_ev_conv_w = lax.dynamic_slice(conv_w_tot, (0, chip, 0), (3, 1, g)).reshape(1, 3, g)
    grad_od_onorm_g = lax.dynamic_slice(tot[7:8], (0, chip * 2 * g), (1, 2 * g))
    dlb_mine = lax.dynamic_slice(tot[8:10], (0, chip * 2 * g), (2, 2 * g))
    grad_lb_logits = lb_logits_grad(lb_logits, dlb_mine)
    grad_ev_w_in = grad_ev_w_in[None]
    grad_od_w_in = grad_od_w_in[None]
    grad_ev_w_out = grad_ev_w_out[None]
    grad_od_w_out = grad_od_w_out[None]
    grad_ev_pool_w = grad_pool_w.reshape(1, N_POOL, g // 4, g)

    def step(w, gr, m, v, name):
        shape = w.shape
        cols = shape[-1]
        two_d = lambda a: a.reshape(-1, cols)
        dl, mo, vo = adamw(two_d(w), two_d(gr), two_d(m), two_d(v), "adamw_" + name)
        return dl.reshape(shape), mo.reshape(shape), vo.reshape(shape)

    upd = {
        "c_ctx": step(c_ctx, grad_c_ctx, m_c_ctx, v_c_ctx, "c_ctx"),
        "ada_w": (delta_ada_w, new_m_ada_w, new_v_ada_w),
        "ada_b": step(ada_b, grad_ada_b, m_ada_b, v_ada_b, "ada_b"),
        "pre_g": step(pre_g, grad_pre_g, m_pre_g, v_pre_g, "pre_g"),
        "post_g": step(post_g, grad_post_g, m_post_g, v_post_g, "post_g"),
        "ev_w_in": step(ev_w_in, grad_ev_w_in, m_ev_w_in, v_ev_w_in, "ev_w_in"),
        "ev_pool_w": step(ev_pool_w, grad_ev_pool_w, m_ev_pool_w, v_ev_pool_w, "ev_pool_w"),
        "ev_pool_scale": step(ev_pool_scale, grad_ev_pool_scale, m_ev_pool_scale, v_ev_pool_scale, "ev_pool_scale"),
        "ev_conv_w": step(ev_conv_w, grad_ev_conv_w, m_ev_conv_w, v_ev_conv_w, "ev_conv_w"),
        "ev_conv_b": step(ev_conv_b, grad_ev_conv_b, m_ev_conv_b, v_ev_conv_b, "ev_conv_b"),
        "ev_w_out": step(ev_w_out, grad_ev_w_out, m_ev_w_out, v_ev_w_out, "ev_w_out"),
        "od_w_in": step(od_w_in, grad_od_w_in, m_od_w_in, v_od_w_in, "od_w_in"),
        "od_onorm_g": step(od_onorm_g, grad_od_onorm_g, m_od_onorm_g, v_od_onorm_g, "od_onorm_g"),
        "od_w_out": step(od_w_out, grad_od_w_out, m_od_w_out, v_od_w_out, "od_w_out"),
        "lb_logits": step(lb_logits, grad_lb_logits, m_lb_logits, v_lb_logits, "lb_logits"),
    }
    names = ["c_ctx", "ada_w", "ada_b", "pre_g", "post_g", "ev_w_in", "ev_pool_w", "ev_pool_scale",
             "ev_conv_w", "ev_conv_b", "ev_w_out", "od_w_in", "od_onorm_g", "od_w_out", "lb_logits"]
    grads = [grad_c_ctx, grad_ada_w, grad_ada_b, grad_pre_g, grad_post_g, grad_ev_w_in, grad_ev_pool_w,
             grad_ev_pool_scale, grad_ev_conv_w, grad_ev_conv_b, grad_ev_w_out, grad_od_w_in,
             grad_od_onorm_g, grad_od_w_out, grad_lb_logits]
    return (loss, grad_x, *grads, *[upd[k][0] for k in names], *[upd[k][1] for k in names],
            *[upd[k][2] for k in names])
```

```python
import functools

import jax
import jax.numpy as jnp
from jax import lax
from jax.experimental import pallas as pl
from jax.experimental.pallas import tpu as pltpu

EPS = 1e-6
GRID_W_LOG2 = 6
CHUNK = 64
HEAD = 128
N_POOL = 4
ADAM_LR, ADAM_B1, ADAM_B2, ADAM_EPS, ADAM_WD, ADAM_STEP = 0.001, 0.9, 0.999, 1e-08, 0.01, 10
VMEM_LIMIT = 56 * 1024 * 1024
MESH = pl.DeviceIdType.MESH
F32, BF16 = jnp.float32, jnp.bfloat16
ANY = pl.BlockSpec(memory_space=pl.ANY)
VMEM = pl.BlockSpec(memory_space=pltpu.VMEM)


def _cp(**kw):
    return pltpu.CompilerParams(vmem_limit_bytes=VMEM_LIMIT, **kw)


def _silu(x):
    return x * jax.nn.sigmoid(x)


def _dsilu(x):
    s = jax.nn.sigmoid(x)
    return s * (1.0 + x * (1.0 - s))


def _dot(a, b, dims=((1,), (0,)), precision=None):
    return lax.dot_general(a, b, (dims, ((), ())), preferred_element_type=F32, precision=precision)


NN = ((1,), (0,))
NT = ((1,), (1,))
TN = ((0,), (0,))


def _row_block(cx):
    return 256 if cx % 256 == 0 else 128


def normmod_fwd(xs, g, shift, scale, cx):
    t, d = xs.shape
    tm = _row_block(cx)
    nctx = cx // tm

    def body(x_ref, g_ref, sh_ref, sc_ref, h_ref):
        is_ctx = pl.program_id(0) < nctx
        x = x_ref[...]
        rstd = lax.rsqrt(jnp.mean(x * x, axis=-1, keepdims=True) + EPS)
        sc = jnp.where(is_ctx, sc_ref[0:1, :], sc_ref[1:2, :])
        sh = jnp.where(is_ctx, sh_ref[0:1, :], sh_ref[1:2, :])
        h_ref[...] = ((x * rstd) * g_ref[...] * (1.0 + sc) + sh).astype(BF16)

    row = pl.BlockSpec((tm, d), lambda i: (i, 0))
    vec = lambda r: pl.BlockSpec((r, d), lambda i: (0, 0))
    return pl.pallas_call(
        body, name="normmod_fwd", grid=(t // tm,),
        in_specs=[row, vec(1), vec(2), vec(2)], out_specs=row,
        out_shape=jax.ShapeDtypeStruct((t, d), BF16), compiler_params=_cp(),
    )(xs, g, shift, scale)


def normmod_bwd(xs, dh, g, scale, dres, cx, res_is_latent_only):
    t, d = xs.shape
    tm = _row_block(cx)
    nctx = cx // tm

    def body(x_ref, dh_ref, g_ref, sc_ref, dres_ref, dx_ref, dg_ref, dsh_ref, dsc_ref):
        i = pl.program_id(0)
        is_ctx = i < nctx

        @pl.when(i == 0)
        def _():
            dg_ref[...] = jnp.zeros_like(dg_ref)
            dsh_ref[...] = jnp.zeros_like(dsh_ref)
            dsc_ref[...] = jnp.zeros_like(dsc_ref)

        x = x_ref[...]
        dh = dh_ref[...]
        gv = g_ref[...]
        rstd = lax.rsqrt(jnp.mean(x * x, axis=-1, keepdims=True) + EPS)
        xhat = x * rstd
        sc = jnp.where(is_ctx, sc_ref[0:1, :], sc_ref[1:2, :])
        dsh = jnp.sum(dh, axis=0, keepdims=True)
        dhx = dh * xhat
        dsc = jnp.sum(dhx * gv, axis=0, keepdims=True)
        dg_ref[...] += jnp.sum(dhx * (1.0 + sc), axis=0, keepdims=True)
        zero = jnp.zeros_like(dsh)
        dsh_ref[0:1, :] += jnp.where(is_ctx, dsh, zero)
        dsh_ref[1:2, :] += jnp.where(is_ctx, zero, dsh)
        dsc_ref[0:1, :] += jnp.where(is_ctx, dsc, zero)
        dsc_ref[1:2, :] += jnp.where(is_ctx, zero, dsc)
        dxhat = dh * (gv * (1.0 + sc))
        dx = rstd * (dxhat - xhat * jnp.mean(dxhat * xhat, axis=-1, keepdims=True))
        res = dres_ref[...]
        if res_is_latent_only:
            res = jnp.where(is_ctx, jnp.zeros_like(res), res)
        dx_ref[...] = dx + res

    row = pl.BlockSpec((tm, d), lambda i: (i, 0))
    if res_is_latent_only:
        res_spec = pl.BlockSpec((tm, d), lambda i: (jnp.maximum(i - nctx, 0), 0))
    else:
        res_spec = row
    vec = lambda r: pl.BlockSpec((r, d), lambda i: (0, 0))
    return pl.pallas_call(
        body, name="normmod_bwd", grid=(t // tm,),
        in_specs=[row, row, vec(1), vec(2), res_spec],
        out_specs=[row, vec(1), vec(2), vec(2)],
        out_shape=[jax.ShapeDtypeStruct((t, d), F32), jax.ShapeDtypeStruct((1, d), F32),
                   jax.ShapeDtypeStruct((2, d), F32), jax.ShapeDtypeStruct((2, d), F32)],
        compiler_params=_cp(),
    )(xs, dh, g, scale, dres)


def post_fwd(xs, y, pg, gate, cx):
    t, d = xs.shape
    tm = _row_block(cx)
    nctx = cx // tm

    def body(x_ref, y_ref, pg_ref, gate_ref, o_ref):
        is_ctx = pl.program_id(0) < nctx
        y = y_ref[...]
        rstd = lax.rsqrt(jnp.mean(y * y, axis=-1, keepdims=True) + EPS)
        gt = jnp.where(is_ctx, gate_ref[0:1, :], gate_ref[1:2, :])
        o_ref[...] = x_ref[...] + gt * ((y * rstd) * pg_ref[...])

    row = pl.BlockSpec((tm, d), lambda i: (i, 0))
    vec = lambda r: pl.BlockSpec((r, d), lambda i: (0, 0))
    return pl.pallas_call(
        body, name="post_fwd", grid=(t // tm,),
        in_specs=[row, row, vec(1), vec(2)], out_specs=row,
        out_shape=jax.ShapeDtypeStruct((t, d), F32), compiler_params=_cp(),
    )(xs, y, pg, gate)


def post_loss(xs, y, pg, gate, target, cx):
    t, d = xs.shape
    n = y.shape[0]
    tm = _row_block(cx)
    nctx = cx // tm

    def body(x_ref, y_ref, pg_ref, gate_ref, tgt_ref, sq_ref, dx_ref):
        @pl.when(pl.program_id(0) == 0)
        def _():
            sq_ref[...] = jnp.zeros_like(sq_ref)

        y = y_ref[...]
        rstd = lax.rsqrt(jnp.mean(y * y, axis=-1, keepdims=True) + EPS)
        x2 = x_ref[...] + gate_ref[1:2, :] * ((y * rstd) * pg_ref[...])
        err = x2 - tgt_ref[...]
        sq_ref[...] += jnp.sum(err * err)
        dx_ref[...] = err * (1.0 / d)

    row = pl.BlockSpec((tm, d), lambda i: (i, 0))
    xrow = pl.BlockSpec((tm, d), lambda i: (i + nctx, 0))
    vec = lambda r: pl.BlockSpec((r, d), lambda i: (0, 0))
    return pl.pallas_call(
        body, name="post_loss", grid=(n // tm,),
        in_specs=[xrow, row, vec(1), vec(2), row],
        out_specs=[pl.BlockSpec((8, 128), lambda i: (0, 0)), row],
        out_shape=[jax.ShapeDtypeStruct((8, 128), F32), jax.ShapeDtypeStruct((n, d), F32)],
        compiler_params=_cp(),
    )(xs, y, pg, gate, target)


def post_bwd(dxo, y, pg, gate, cx, latent_only):
    m, d = y.shape
    tm = _row_block(cx)
    nctx = 0 if latent_only else cx // tm

    def body(dx_ref, y_ref, pg_ref, gate_ref, dy_ref, dgate_ref, dpg_ref):
        i = pl.program_id(0)
        is_ctx = i < nctx

        @pl.when(i == 0)
        def _():
            dgate_ref[...] = jnp.zeros_like(dgate_ref)
            dpg_ref[...] = jnp.zeros_like(dpg_ref)

        y = y_ref[...]
        dx = dx_ref[...]
        pgv = pg_ref[...]
        rstd = lax.rsqrt(jnp.mean(y * y, axis=-1, keepdims=True) + EPS)
        yhat = y * rstd
        gt = jnp.where(is_ctx, gate_ref[0:1, :], gate_ref[1:2, :])
        dxy = dx * yhat
        dgt = jnp.sum(dxy * pgv, axis=0, keepdims=True)
        zero = jnp.zeros_like(dgt)
        dgate_ref[0:1, :] += jnp.where(is_ctx, dgt, zero)
        dgate_ref[1:2, :] += jnp.where(is_ctx, zero, dgt)
        dpg_ref[...] += jnp.sum(dxy * gt, axis=0, keepdims=True)
        dyhat = dx * (gt * pgv)
        dy = rstd * (dyhat - yhat * jnp.mean(dyhat * yhat, axis=-1, keepdims=True))
        dy_ref[...] = dy.astype(BF16)

    row = pl.BlockSpec((tm, d), lambda i: (i, 0))
    vec = lambda r: pl.BlockSpec((r, d), lambda i: (0, 0))
    return pl.pallas_call(
        body, name="post_bwd", grid=(m // tm,),
        in_specs=[row, row, vec(1), vec(2)], out_specs=[row, vec(2), vec(1)],
        out_shape=[jax.ShapeDtypeStruct((m, d), BF16), jax.ShapeDtypeStruct((2, d), F32),
                   jax.ShapeDtypeStruct((1, d), F32)],
        compiler_params=_cp(),
    )(dxo, y, pg, gate)


def _split_rows(m):
    for cand in (1024, 768, 512, 384, 256, 128):
        if m % cand == 0 and m // cand >= 2:
            return cand
    return m


def mm_nn(a, w3, sec, tn, name):
    m, k = a.shape
    q, _, ws = w3.shape
    n = q * ws
    tpq, tps = ws // tn, sec // tn
    tm = 256 if m % 256 == 0 else 128

    def body(a_ref, w_ref, o_ref):
        w = w_ref[...]

        def step(i, carry):
            rows = pl.ds(pl.multiple_of(i * tm, tm), tm)
            o_ref[rows, :] = _dot(a_ref[rows, :], w)
            return carry

        lax.fori_loop(0, m // tm, step, 0)

    return pl.pallas_call(
        body, name=name, grid=(n // tn,),
        in_specs=[pl.BlockSpec((m, k), lambda j: (0, 0)),
                  pl.BlockSpec((None, k, tn), lambda j: (j // tpq, 0, j % tpq))],
        out_specs=pl.BlockSpec((None, m, tn), lambda j: (j // tps, 0, j % tps)),
        out_shape=jax.ShapeDtypeStruct((n // sec, m, sec), F32), compiler_params=_cp(),
    )(a, w3)


def mm_nt(a3, w3, tn, name):
    s, m, sec = a3.shape
    q, k, ws = w3.shape
    n = q * ws
    tpq, tps = ws // tn, sec // tn
    mb = _split_rows(m)

    def body(a_ref, w_ref, o_ref):
        @pl.when(pl.program_id(1) == 0)
        def _():
            o_ref[...] = jnp.zeros_like(o_ref)

        o_ref[...] += _dot(a_ref[...], w_ref[...], NT)

    return pl.pallas_call(
        body, name=name, grid=(m // mb, n // tn),
        in_specs=[pl.BlockSpec((None, mb, tn), lambda i, j: (j // tps, i, j % tps)),
                  pl.BlockSpec((None, k, tn), lambda i, j: (j // tpq, 0, j % tpq))],
        out_specs=pl.BlockSpec((mb, k), lambda i, j: (i, 0)),
        out_shape=jax.ShapeDtypeStruct((m, k), F32), compiler_params=_cp(),
    )(a3, w3)


def mm_tn(a, b3, ws, tn, name):
    m, k = a.shape
    s, _, sec = b3.shape
    n = s * sec
    tpq, tps = ws // tn, sec // tn
    kb = 256 if k % 256 == 0 else 128

    def body(a_ref, b_ref, o_ref):
        b = b_ref[...]
        for i in range(k // kb):
            o_ref[i * kb:(i + 1) * kb, :] = _dot(a_ref[:, i * kb:(i + 1) * kb], b, TN)

    return pl.pallas_call(
        body, name=name, grid=(n // tn,),
        in_specs=[pl.BlockSpec((m, k), lambda j: (0, 0)),
                  pl.BlockSpec((None, m, tn), lambda j: (j // tps, 0, j % tps))],
        out_specs=pl.BlockSpec((None, k, tn), lambda j: (j // tpq, 0, j % tpq)),
        out_shape=jax.ShapeDtypeStruct((n // ws, k, ws), F32), compiler_params=_cp(),
    )(a, b3)


def _pool_mask(gi, row0, tm, t, cx, seq, transposed):
    half = jnp.left_shift(1, gi)
    r = lax.broadcasted_iota(jnp.int32, (tm, 1), 0) + row0
    c = lax.broadcasted_iota(jnp.int32, (1, t), 1)
    out_tok, src_tok = (c, r) if transposed else (r, c)

    def parts(tok):
        lat = tok - cx
        return tok < cx, lat >> GRID_W_LOG2, lat & ((1 << GRID_W_LOG2) - 1)

    o_ctx, o_row, o_col = parts(out_tok)
    s_ctx, s_row, s_col = parts(src_tok)

    def inside(o, s):
        return (s >= o - half) & (s <= o + half - 1)

    ctx_hit = o_ctx & s_ctx & inside(out_tok, src_tok)
    lat_hit = (~o_ctx) & (~s_ctx) & inside(o_row, s_row) & inside(o_col, s_col)
    mask = jnp.where(ctx_hit | lat_hit, 1.0, 0.0).astype(BF16)

    own_ctx, own_row, own_col = parts(r)

    def count(pos, size):
        return jnp.minimum(pos + half - 1, size - 1) - jnp.maximum(pos - half, 0) + 1

    cnt = jnp.where(own_ctx, count(r, cx),
                    count(own_row, seq >> GRID_W_LOG2) * count(own_col, 1 << GRID_W_LOG2))
    return mask, 1.0 / cnt.astype(F32)


def mix_a_fwd(z0, pool_w, pool_scale, cx):
    _, t, half_d = z0.shape
    g = half_d // N_POOL
    seq = t - cx
    tm = _row_block(cx)

    def body(v_ref, ag_ref, w_ref, sc_ref, u_ref, vb_ref):
        gi = pl.program_id(0)
        vb_ref[...] = v_ref[...].astype(BF16)
        w = w_ref[...].astype(BF16)
        sc = sc_ref[...]

        def step(i, carry):
            row0 = pl.multiple_of(i * tm, tm)
            rows = pl.ds(row0, tm)
            mask, inv = _pool_mask(gi, row0, tm, t, cx, seq, False)
            pooled = _dot(mask, vb_ref[...]) * inv - v_ref[rows, :]
            mixed = _dot(pooled.astype(BF16), w) * sc
            u_ref[rows, :] = (mixed * _silu(ag_ref[rows, :])).astype(BF16)
            return carry

        lax.fori_loop(0, t // tm, step, 0)

    sec = lambda s: pl.BlockSpec((None, t, g), lambda j: (s, 0, j))
    return pl.pallas_call(
        body, name="mix_a_fwd", grid=(N_POOL,),
        in_specs=[sec(0), sec(1), pl.BlockSpec((None, g, g), lambda j: (j, 0, 0)),
                  pl.BlockSpec((1, g), lambda j: (0, j))],
        out_specs=pl.BlockSpec((t, g), lambda j: (0, j)),
        out_shape=jax.ShapeDtypeStruct((t, half_d), BF16),
        scratch_shapes=[pltpu.VMEM((t, g), BF16)], compiler_params=_cp(),
    )(z0, z0, pool_w, pool_scale)


def mix_a_bwd(z0, du, pool_w, pool_scale, cx):
    _, t, half_d = z0.shape
    g = half_d // N_POOL
    seq = t - cx
    tm = _row_block(cx)
    gq = g // 4

    def body(v_ref, ag_ref, du_ref, w_ref, sc_ref, dz_ref, dw_ref, dsc_ref,
             vb_ref, pooled_ref, dmx_ref, dpl_ref, wdp_ref):
        gi = pl.program_id(0)
        vb_ref[...] = v_ref[...].astype(BF16)
        w = w_ref[...].astype(BF16)
        sc = sc_ref[...]

        def first(i, dsc):
            row0 = pl.multiple_of(i * tm, tm)
            rows = pl.ds(row0, tm)
            mask, inv = _pool_mask(gi, row0, tm, t, cx, seq, False)
            pooled = (_dot(mask, vb_ref[...]) * inv - v_ref[rows, :]).astype(BF16)
            pooled_ref[rows, :] = pooled
            mixed = _dot(pooled, w)
            ag = ag_ref[rows, :]
            duv = du_ref[rows, :]
            dz_ref[1, rows, :] = (duv * (mixed * sc) * _dsilu(ag)).astype(BF16)
            dms = duv * _silu(ag)
            dmixed = (dms * sc).astype(BF16)
            dmx_ref[rows, :] = dmixed
            dpooled = _dot(dmixed, w, NT)
            dpl_ref[rows, :] = dpooled
            wdp_ref[rows, :] = (dpooled * inv).astype(BF16)
            return dsc + jnp.sum(dms * mixed, axis=0, keepdims=True)

        dsc_ref[...] = lax.fori_loop(0, t // tm, first, jnp.zeros((1, g), F32))
        dw = _dot(pooled_ref[...], dmx_ref[...], TN)
        for qi in range(4):
            dw_ref[qi] = dw[qi * gq:(qi + 1) * gq, :]

        def second(i, carry):
            row0 = pl.multiple_of(i * tm, tm)
            rows = pl.ds(row0, tm)
            mask_t, _ = _pool_mask(gi, row0, tm, t, cx, seq, True)
            dz_ref[0, rows, :] = (_dot(mask_t, wdp_ref[...]) - dpl_ref[rows, :]).astype(BF16)
            return carry

        lax.fori_loop(0, t // tm, second, 0)

    sec = lambda s: pl.BlockSpec((None, t, g), lambda j: (s, 0, j))
    return pl.pallas_call(
        body, name="mix_a_bwd", grid=(N_POOL,),
        in_specs=[sec(0), sec(1), pl.BlockSpec((t, g), lambda j: (0, j)),
                  pl.BlockSpec((None, g, g), lambda j: (j, 0, 0)),
                  pl.BlockSpec((1, g), lambda j: (0, j))],
        out_specs=[pl.BlockSpec((2, t, g), lambda j: (0, 0, j)),
                   pl.BlockSpec((4, None, gq, g), lambda j: (0, j, 0, 0)),
                   pl.BlockSpec((1, g), lambda j: (0, j))],
        out_shape=[jax.ShapeDtypeStruct((2, t, half_d), BF16),
                   jax.ShapeDtypeStruct((4, N_POOL, gq, g), F32),
                   jax.ShapeDtypeStruct((1, half_d), F32)],
        scratch_shapes=[pltpu.VMEM((t, g), BF16), pltpu.VMEM((t, g), BF16), pltpu.VMEM((t, g), BF16),
                        pltpu.VMEM((t, g), F32), pltpu.VMEM((t, g), BF16)],
        compiler_params=_cp(),
    )(z0, z0, du, pool_w, pool_scale)


def _conv_masks(t, cx):
    r = lax.broadcasted_iota(jnp.int32, (t, 1), 0)
    has_prev = jnp.where((r == 0) | (r == cx), 0.0, 1.0)
    has_next = jnp.where((r == cx - 1) | (r == t - 1), 0.0, 1.0)
    return has_prev, has_next


def mix_b_fwd(z0, conv_w, conv_b, cx):
    _, t, half_d = z0.shape
    gb = 128

    def body(bx_ref, bb_ref, bc_ref, bg_ref, w_ref, b_ref, u_ref):
        has_prev, has_next = _conv_masks(t, cx)
        tt = bc_ref[...] * bx_ref[...]
        prev = pltpu.roll(tt, 1, 0) * has_prev
        nxt = pltpu.roll(tt, t - 1, 0) * has_next
        cv = prev * w_ref[0:1, :] + tt * w_ref[1:2, :] + nxt * w_ref[2:3, :] + b_ref[...]
        u_ref[...] = (bb_ref[...] * cv * _silu(bg_ref[...])).astype(BF16)

    sec = lambda s: pl.BlockSpec((None, t, gb), lambda j: (s, 0, j))
    return pl.pallas_call(
        body, name="mix_b_fwd", grid=(half_d // gb,),
        in_specs=[sec(2), sec(3), sec(4), sec(5), pl.BlockSpec((3, gb), lambda j: (0, j)),
                  pl.BlockSpec((1, gb), lambda j: (0, j))],
        out_specs=pl.BlockSpec((t, gb), lambda j: (0, j)),
        out_shape=jax.ShapeDtypeStruct((t, half_d), BF16), compiler_params=_cp(),
    )(z0, z0, z0, z0, conv_w, conv_b)


def mix_b_bwd(z0, du, conv_w, conv_b, cx):
    _, t, half_d = z0.shape
    gb = 128
    off = half_d // gb

    def body(bx_ref, bb_ref, bc_ref, bg_ref, du_ref, w_ref, b_ref, dz_ref, dw_ref, db_ref):
        has_prev, has_next = _conv_masks(t, cx)
        bx, bb, bc, bg = bx_ref[...], bb_ref[...], bc_ref[...], bg_ref[...]
        duv = du_ref[...]
        tt = bc * bx
        prev = pltpu.roll(tt, 1, 0) * has_prev
        nxt = pltpu.roll(tt, t - 1, 0) * has_next
        w0, w1, w2 = w_ref[0:1, :], w_ref[1:2, :], w_ref[2:3, :]
        cv = prev * w0 + tt * w1 + nxt * w2 + b_ref[...]
        sg = _silu(bg)
        dz_ref[1] = (duv * cv * sg).astype(BF16)
        dz_ref[3] = (duv * bb * cv * _dsilu(bg)).astype(BF16)
        dcv = duv * bb * sg
        dw_ref[0:1, :] = jnp.sum(dcv * prev, axis=0, keepdims=True)
        dw_ref[1:2, :] = jnp.sum(dcv * tt, axis=0, keepdims=True)
        dw_ref[2:3, :] = jnp.sum(dcv * nxt, axis=0, keepdims=True)
        db_ref[...] = jnp.sum(dcv, axis=0, keepdims=True)
        dt = (pltpu.roll(dcv * has_prev, t - 1, 0) * w0 + dcv * w1
              + pltpu.roll(dcv * has_next, 1, 0) * w2)
        dz_ref[0] = (dt * bc).astype(BF16)
        dz_ref[2] = (dt * bx).astype(BF16)

    sec = lambda s: pl.BlockSpec((None, t, gb), lambda j: (s, 0, j))
    return pl.pallas_call(
        body, name="mix_b_bwd", grid=(half_d // gb,),
        in_specs=[sec(2), sec(3), sec(4), sec(5), pl.BlockSpec((t, gb), lambda j: (0, j + off)),
                  pl.BlockSpec((3, gb), lambda j: (0, j)), pl.BlockSpec((1, gb), lambda j: (0, j))],
        out_specs=[pl.BlockSpec((4, t, gb), lambda j: (0, 0, j)),
                   pl.BlockSpec((3, gb), lambda j: (0, j)), pl.BlockSpec((1, gb), lambda j: (0, j))],
        out_shape=[jax.ShapeDtypeStruct((4, t, half_d), BF16),
                   jax.ShapeDtypeStruct((3, half_d), F32), jax.ShapeDtypeStruct((1, half_d), F32)],
        compiler_params=_cp(),
    )(z0, z0, z0, z0, du, conv_w, conv_b)


def _lower_bound(lbl_ref, d):
    l0, l1, l2 = lbl_ref[d, 0:1, :], lbl_ref[d, 1:2, :], lbl_ref[d, 2:3, :]
    mx = jnp.maximum(jnp.maximum(l0, l1), l2)
    e0, e1, e2 = jnp.exp(l0 - mx), jnp.exp(l1 - mx), jnp.exp(l2 - mx)
    inv = 1.0 / (e0 + e1 + e2)
    return (e0 + e1) * inv, (e0 * inv, e1 * inv, e2 * inv)


def _chunk_consts(d):
    r = lax.broadcasted_iota(jnp.int32, (CHUNK, CHUNK), 0)
    c = lax.broadcasted_iota(jnp.int32, (CHUNK, CHUNK), 1)
    keep = (c <= r) if d == 0 else (c >= r)
    return jnp.where(keep, 1.0, 0.0).astype(F32), keep


def _chunk_of_step(s, d, nc, ncc):
    if d == 0:
        return s
    return jnp.where(s < ncc, ncc - 1 - s, nc - 1 + ncc - s)


def _chunk_terms(lfc, kc, qc, cum):
    bc = _dot(cum, lfc, precision=lax.Precision.HIGHEST)
    bl = jnp.sum(lfc, axis=0, keepdims=True)
    e = jnp.exp(bc)
    einv = jnp.exp(-bc)
    erem = jnp.exp(bl - bc)
    return e, einv, erem, jnp.exp(bl), qc * e, kc * einv, kc * erem


def hgrn_fwd(z1, lbl, onorm, cx):
    _, t, d = z1.shape
    seq = t - cx
    nc, ncc = t // CHUNK, cx // CHUNK

    def body(zf_ref, zb_ref, v_ref, q_ref, g_ref, lbl_ref, on_ref, o_ref, r_ref,
             lf_ref, k_ref, oacc_ref, st_ref):
        for dr, z_ref in ((0, zf_ref), (1, zb_ref)):
            lbv, _ = _lower_bound(lbl_ref, dr)
            z = z_ref[...]
            lf_ref[...] = jnp.log(lbv + (1.0 - lbv) * jax.nn.sigmoid(z))
            k_ref[...] = (1.0 - lbv) * jax.nn.sigmoid(-z)
            st_ref[...] = jnp.zeros_like(st_ref)
            cum, keep = _chunk_consts(dr)

            def step(s, carry, dr=dr, cum=cum, keep=keep):
                n = _chunk_of_step(s, dr, nc, ncc)
                rows = pl.ds(pl.multiple_of(n * CHUNK, CHUNK), CHUNK)
                vc = v_ref[rows, :].astype(BF16)
                _, _, _, dec, qd, ki, kd = _chunk_terms(lf_ref[rows, :], k_ref[rows, :], q_ref[rows, :], cum)
                qdb = qd.astype(BF16)
                a = jnp.where(keep, _dot(qdb, ki.astype(BF16), NT), 0.0)
                st = st_ref[...]
                oc = _dot(qdb, st.astype(BF16), NT) + _dot(a.astype(BF16), vc)
                st_ref[...] = st * dec + _dot(vc, kd.astype(BF16), TN)
                if dr == 0:
                    oacc_ref[rows, :] = oc
                else:
                    oacc_ref[rows, :] += oc
                return carry

            lax.fori_loop(0, nc, step, 0, unroll=4)

        o = oacc_ref[cx:, :]
        o_ref[...] = o
        rstd = lax.rsqrt(jnp.mean(o * o, axis=-1, keepdims=True) + EPS)
        r_ref[...] = (o * rstd * on_ref[...] * _silu(g_ref[cx:, :])).astype(BF16)

    sec = lambda s: pl.BlockSpec((None, t, HEAD), lambda h: (s, 0, h))
    col = pl.BlockSpec((seq, HEAD), lambda h: (0, h))
    return pl.pallas_call(
        body, name="hgrn_fwd", grid=(d // HEAD,),
        in_specs=[sec(0), sec(1), sec(2), sec(3), sec(4),
                  pl.BlockSpec((2, 3, HEAD), lambda h: (0, 0, h)), pl.BlockSpec((1, HEAD), lambda h: (0, h))],
        out_specs=[col, col],
        out_shape=[jax.ShapeDtypeStruct((seq, d), F32), jax.ShapeDtypeStruct((seq, d), BF16)],
        scratch_shapes=[pltpu.VMEM((t, HEAD), F32), pltpu.VMEM((t, HEAD), F32), pltpu.VMEM((t, HEAD), F32),
                        pltpu.VMEM((HEAD, HEAD), F32)],
        compiler_params=_cp(),
    )(z1, z1, z1, z1, z1, lbl, onorm)


def hgrn_bwd(z1, lbl, onorm, o, dr_out, cx):
    _, t, d = z1.shape
    seq = t - cx
    nc, ncc = t // CHUNK, cx // CHUNK

    def body(zf_ref, zb_ref, v_ref, q_ref, g_ref, lbl_ref, on_ref, o_ref, dr_ref,
             dz_ref, don_ref, dlb_ref,
             lf_ref, k_ref, do_ref, dq_ref, dv_ref, dk_ref, dlf_ref, ssc_ref, dst_ref):
        o = o_ref[...]
        g = g_ref[cx:, :]
        drv = dr_ref[...]
        onv = on_ref[...]
        rstd = lax.rsqrt(jnp.mean(o * o, axis=-1, keepdims=True) + EPS)
        ohat = o * rstd
        sg = _silu(g)
        don_ref[...] = jnp.sum(drv * ohat * sg, axis=0, keepdims=True)
        dz_ref[4, :cx, :] = jnp.zeros((cx, HEAD), BF16)
        dz_ref[4, cx:, :] = (drv * ohat * onv * _dsilu(g)).astype(BF16)
        dohat = drv * onv * sg
        do_ref[:cx, :] = jnp.zeros((cx, HEAD), F32)
        do_ref[cx:, :] = rstd * (dohat - ohat * jnp.mean(dohat * ohat, axis=-1, keepdims=True))

        for dr, z_ref in ((0, zf_ref), (1, zb_ref)):
            lbv, _ = _lower_bound(lbl_ref, dr)
            z = z_ref[...]
            lf_ref[...] = jnp.log(lbv + (1.0 - lbv) * jax.nn.sigmoid(z))
            k_ref[...] = (1.0 - lbv) * jax.nn.sigmoid(-z)
            cum, keep = _chunk_consts(dr)
            cum_t, _ = _chunk_consts(1 - dr)

            st_init = jnp.zeros((HEAD, HEAD), F32)

            def state_step(s, st, dr=dr, cum=cum):
                n = _chunk_of_step(s, dr, nc, ncc)
                rows = pl.ds(pl.multiple_of(n * CHUNK, CHUNK), CHUNK)
                ssc_ref[n] = st
                _, _, _, dec, _, _, kd = _chunk_terms(lf_ref[rows, :], k_ref[rows, :], q_ref[rows, :], cum)
                return st * dec + _dot(v_ref[rows, :].astype(BF16), kd.astype(BF16), TN)

            lax.fori_loop(0, nc, state_step, st_init, unroll=4)
            dst_ref[...] = jnp.zeros_like(dst_ref)

            def grad_step(s2, carry, dr=dr, cum=cum, cum_t=cum_t, keep=keep):
                n = _chunk_of_step(nc - 1 - s2, dr, nc, ncc)
                rows = pl.ds(pl.multiple_of(n * CHUNK, CHUNK), CHUNK)
                vc = v_ref[rows, :].astype(BF16)
                e, einv, erem, dec, qd, ki, kd = _chunk_terms(
                    lf_ref[rows, :], k_ref[rows, :], q_ref[rows, :], cum)
                qdb, kib, kdb = qd.astype(BF16), ki.astype(BF16), kd.astype(BF16)
                doc = do_ref[rows, :].astype(BF16)
                st0 = ssc_ref[n]
                dst = dst_ref[...]
                dstb = dst.astype(BF16)
                a = jnp.where(keep, _dot(qdb, kib, NT), 0.0).astype(BF16)
                da = jnp.where(keep, _dot(doc, vc, NT), 0.0).astype(BF16)
                dqd = _dot(doc, st0.astype(BF16)) + _dot(da, kib)
                dki = _dot(da, qdb, TN)
                dv = _dot(a, doc, TN) + _dot(kdb, dstb, NT)
                dkd = _dot(vc, dstb)
                ddec = jnp.sum(dst * st0, axis=0, keepdims=True)
                dst_ref[...] = _dot(doc, qdb, TN) + dst * dec
                dbc = dqd * qd - dki * ki - dkd * kd
                dbl = jnp.sum(dkd * kd, axis=0, keepdims=True) + ddec * dec
                dlf_ref[rows, :] = _dot(cum_t, dbc, precision=lax.Precision.HIGHEST) + dbl
                dk_ref[rows, :] = dki * einv + dkd * erem
                if dr == 0:
                    dq_ref[rows, :] = dqd * e
                    dv_ref[rows, :] = dv
                else:
                    dq_ref[rows, :] += dqd * e
                    dv_ref[rows, :] += dv
                return carry

            lax.fori_loop(0, nc, grad_step, 0, unroll=2)

            sig = jax.nn.sigmoid(z)
            one_lb = 1.0 - lbv
            f = lbv + one_lb * sig
            dlf = dlf_ref[...]
            dk = dk_ref[...]
            dsig = (dlf / f - dk) * one_lb
            dz_ref[dr] = (dsig * sig * (1.0 - sig)).astype(BF16)
            dlb_ref[dr:dr + 1, :] = jnp.sum((dlf / f - dk) * (1.0 - sig), axis=0, keepdims=True)

        dz_ref[2] = dv_ref[...].astype(BF16)
        dz_ref[3] = dq_ref[...].astype(BF16)

    sec = lambda s: pl.BlockSpec((None, t, HEAD), lambda h: (s, 0, h))
    col = pl.BlockSpec((seq, HEAD), lambda h: (0, h))
    tvec = pltpu.VMEM((t, HEAD), F32)
    return pl.pallas_call(
        body, name="hgrn_bwd", grid=(d // HEAD,),
        in_specs=[sec(0), sec(1), sec(2), sec(3), sec(4),
                  pl.BlockSpec((2, 3, HEAD), lambda h: (0, 0, h)), pl.BlockSpec((1, HEAD), lambda h: (0, h)),
                  col, col],
        out_specs=[pl.BlockSpec((5, t, HEAD), lambda h: (0, 0, h)),
                   pl.BlockSpec((1, HEAD), lambda h: (0, h)), pl.BlockSpec((2, HEAD), lambda h: (0, h))],
        out_shape=[jax.ShapeDtypeStruct((5, t, d), BF16), jax.ShapeDtypeStruct((1, d), F32),
                   jax.ShapeDtypeStruct((2, d), F32)],
        scratch_shapes=[tvec, tvec, tvec, tvec, tvec, tvec, tvec,
                        pltpu.VMEM((nc, HEAD, HEAD), F32), pltpu.VMEM((HEAD, HEAD), F32)],
        compiler_params=_cp(),
    )(z1, z1, z1, z1, z1, lbl, onorm, o, dr_out)


def _gates(z, lbv):
    e = jnp.exp(-jnp.abs(z))
    r = 1.0 / (1.0 + e)
    er = e * r
    pos = z >= 0.0
    sig = jnp.where(pos, r, er)
    nsig = jnp.where(pos, er, r)
    return sig, nsig, lbv + (1.0 - lbv) * sig


def _split3(x):
    hi = x.astype(BF16)
    r1 = x - hi.astype(F32)
    mid = r1.astype(BF16)
    lo = (r1 - mid.astype(F32)).astype(BF16)
    return jnp.concatenate([hi, mid, lo], axis=1)


def _cumsum_chunk(cum, x):
    y = _dot(cum, _split3(x))
    return y[:, :HEAD] + y[:, HEAD:2 * HEAD] + y[:, 2 * HEAD:]


def _chunk_rows(n):
    return pl.ds(pl.multiple_of(n * CHUNK, CHUNK), CHUNK)


def _group(nc):
    return next(u for u in (4, 3, 2, 1) if nc % u == 0)


def _decay_pass(lf_ref, bc_ref, dec_ref, cum, nc):
    grp = _group(nc)

    def step(m, carry):
        ns = [m * grp + u for u in range(grp)]
        lfc = [lf_ref[_chunk_rows(n), :] for n in ns]
        bc = [_cumsum_chunk(cum, x) for x in lfc]
        for u, n in enumerate(ns):
            bc_ref[_chunk_rows(n), :] = bc[u]
            dec_ref[n] = jnp.broadcast_to(jnp.exp(jnp.sum(lfc[u], axis=0, keepdims=True)), (8, HEAD))
        return carry

    lax.fori_loop(0, nc // grp, step, 0)


def hgrn_fwd(z1, lbl, onorm, cx):
    _, t, d = z1.shape
    seq = t - cx
    nc, ncc = t // CHUNK, cx // CHUNK

    grp = _group(nc)

    def body(zf_ref, zb_ref, v_ref, q_ref, g_ref, lbl_ref, on_ref, o_ref, r_ref,
             lf_ref, k_ref, bc_ref, dec_ref, qd_ref, ki_ref, oacc_ref, ds_ref):
        for dr, z_ref in ((0, zf_ref), (1, zb_ref)):
            lbv, _ = _lower_bound(lbl_ref, dr)
            _, nsig, f = _gates(z_ref[...], lbv)
            lf_ref[...] = jnp.log(f)
            k_ref[...] = (1.0 - lbv) * nsig
            cum, keep = _chunk_consts(dr)
            _decay_pass(lf_ref, bc_ref, dec_ref, cum.astype(BF16), nc)
            bc = bc_ref[...]
            qd_ref[...] = (q_ref[...] * jnp.exp(bc)).astype(BF16)
            ki_ref[...] = (k_ref[...] * jnp.exp(-bc)).astype(BF16)

            def local_step(m, carry, dr=dr, keep=keep):
                ns = [m * grp + u for u in range(grp)]
                rows = [_chunk_rows(n) for n in ns]
                qd = [qd_ref[r, :] for r in rows]
                ki = [ki_ref[r, :] for r in rows]
                vc = [v_ref[r, :].astype(BF16) for r in rows]
                sc = [_dot(qd[u], ki[u], NT) for u in range(grp)]
                inc = [_dot(vc[u], ki[u], TN) for u in range(grp)]
                a = [jnp.where(keep, s, 0.0).astype(BF16) for s in sc]
                intra = [_dot(a[u], vc[u]) for u in range(grp)]
                for u in range(grp):
                    ds_ref[ns[u]] = inc[u] * dec_ref[ns[u]][0:1, :]
                    if dr == 0:
                        oacc_ref[rows[u], :] = intra[u]
                    else:
                        oacc_ref[rows[u], :] += intra[u]
                return carry

            lax.fori_loop(0, nc // grp, local_step, 0)

            def state_step(m, st, dr=dr):
                ns = [_chunk_of_step(m * grp + u, dr, nc, ncc) for u in range(grp)]
                rows = [_chunk_rows(n) for n in ns]
                sts = []
                for n in ns:
                    sts.append(st.astype(BF16))
                    st = st * dec_ref[n][0:1, :] + ds_ref[n]
                inter = [_dot(qd_ref[rows[u], :], sts[u], NT) for u in range(grp)]
                for u in range(grp):
                    oacc_ref[rows[u], :] += inter[u]
                return st

            lax.fori_loop(0, nc // grp, state_step, jnp.zeros((HEAD, HEAD), F32))

        o = oacc_ref[cx:, :]
        o_ref[...] = o
        rstd = lax.rsqrt(jnp.mean(o * o, axis=-1, keepdims=True) + EPS)
        r_ref[...] = (o * rstd * on_ref[...] * _silu(g_ref[cx:, :])).astype(BF16)

    sec = lambda s: pl.BlockSpec((None, t, HEAD), lambda h: (s, 0, h))
    col = pl.BlockSpec((seq, HEAD), lambda h: (0, h))
    tf32, tb16 = pltpu.VMEM((t, HEAD), F32), pltpu.VMEM((t, HEAD), BF16)
    return pl.pallas_call(
        body, name="hgrn_fwd", grid=(d // HEAD,),
        in_specs=[sec(0), sec(1), sec(2), sec(3), sec(4),
                  pl.BlockSpec((2, 3, HEAD), lambda h: (0, 0, h)), pl.BlockSpec((1, HEAD), lambda h: (0, h))],
        out_specs=[col, col],
        out_shape=[jax.ShapeDtypeStruct((seq, d), F32), jax.ShapeDtypeStruct((seq, d), BF16)],
        scratch_shapes=[tf32, tf32, tf32, pltpu.VMEM((nc, 8, HEAD), F32), tb16, tb16, tf32,
                        pltpu.VMEM((nc, HEAD, HEAD), F32)],
        compiler_params=_cp(),
    )(z1, z1, z1, z1, z1, lbl, onorm)


def hgrn_bwd(z1, lbl, onorm, o, dr_out, cx):
    _, t, d = z1.shape
    seq = t - cx
    nc, ncc = t // CHUNK, cx // CHUNK

    grp2 = 2 if nc % 2 == 0 else 1
    grp = grp2

    def body(zf_ref, zb_ref, v_ref, q_ref, g_ref, lbl_ref, on_ref, o_ref, dr_ref,
             dz_ref, don_ref, dlb_ref,
             lf_ref, k_ref, bc_ref, dec_ref, qd_ref, ki_ref, do_ref,
             dqd_ref, dki_ref, dq_ref, dv_ref, ds_ref, dsl_ref):
        o = o_ref[...]
        g = g_ref[cx:, :]
        drv = dr_ref[...]
        onv = on_ref[...]
        rstd = lax.rsqrt(jnp.mean(o * o, axis=-1, keepdims=True) + EPS)
        ohat = o * rstd
        sg = _silu(g)
        don_ref[...] = jnp.sum(drv * ohat * sg, axis=0, keepdims=True)
        dz_ref[4, :cx, :] = jnp.zeros((cx, HEAD), BF16)
        dz_ref[4, cx:, :] = (drv * ohat * onv * _dsilu(g)).astype(BF16)
        dohat = drv * onv * sg
        do_ref[:cx, :] = jnp.zeros((cx, HEAD), BF16)
        do_ref[cx:, :] = (rstd * (dohat - ohat * jnp.mean(dohat * ohat, axis=-1, keepdims=True))).astype(BF16)

        for dr, z_ref in ((0, zf_ref), (1, zb_ref)):
            lbv, _ = _lower_bound(lbl_ref, dr)
            _, nsig, f = _gates(z_ref[...], lbv)
            lf_ref[...] = jnp.log(f)
            k_ref[...] = (1.0 - lbv) * nsig
            cum, keep = _chunk_consts(dr)
            cum_t = _chunk_consts(1 - dr)[0].astype(BF16)
            _decay_pass(lf_ref, bc_ref, dec_ref, cum.astype(BF16), nc)
            bc = bc_ref[...]
            qd_ref[...] = (q_ref[...] * jnp.exp(bc)).astype(BF16)
            ki_ref[...] = (k_ref[...] * jnp.exp(-bc)).astype(BF16)

            def local_step(m, carry, dr=dr, keep=keep):
                ns = [m * grp + u for u in range(grp)]
                rows = [_chunk_rows(n) for n in ns]
                rng = range(grp)
                qd = [qd_ref[r, :] for r in rows]
                ki = [ki_ref[r, :] for r in rows]
                doc = [do_ref[r, :] for r in rows]
                vc = [v_ref[r, :].astype(BF16) for r in rows]
                sc = [_dot(qd[u], ki[u], NT) for u in rng]
                dsc = [_dot(doc[u], vc[u], NT) for u in rng]
                inc = [_dot(vc[u], ki[u], TN) for u in rng]
                dinc = [_dot(doc[u], qd[u], TN) for u in rng]
                a = [jnp.where(keep, s, 0.0).astype(BF16) for s in sc]
                da = [jnp.where(keep, s, 0.0).astype(BF16) for s in dsc]
                dqd = [_dot(da[u], ki[u]) for u in rng]
                dki = [_dot(da[u], qd[u], TN) for u in rng]
                dv = [_dot(a[u], doc[u], TN) for u in rng]
                for u in rng:
                    ds_ref[ns[u]] = inc[u] * dec_ref[ns[u]][0:1, :]
                    dsl_ref[ns[u]] = dinc[u]
                    dqd_ref[rows[u], :] = dqd[u]
                    dki_ref[rows[u], :] = dki[u]
                    if dr == 0:
                        dv_ref[rows[u], :] = dv[u]
                    else:
                        dv_ref[rows[u], :] += dv[u]
                return carry

            lax.fori_loop(0, nc // grp, local_step, 0)

            def state_step(s, st, dr=dr):
                n = _chunk_of_step(s, dr, nc, ncc)
                inc = ds_ref[n]
                ds_ref[n] = st
                return st * dec_ref[n][0:1, :] + inc

            lax.fori_loop(0, nc, state_step, jnp.zeros((HEAD, HEAD), F32), unroll=4)

            def dstate_step(s, dst, dr=dr):
                n = _chunk_of_step(nc - 1 - s, dr, nc, ncc)
                inc = dsl_ref[n]
                dsl_ref[n] = dst
                return inc + dst * dec_ref[n][0:1, :]

            lax.fori_loop(0, nc, dstate_step, jnp.zeros((HEAD, HEAD), F32), unroll=4)

            def grad_step(m, carry, dr=dr, cum_t=cum_t):
                ns = [m * grp2 + u for u in range(grp2)]
                rows = [_chunk_rows(n) for n in ns]
                rng = range(grp2)
                st0 = [ds_ref[n] for n in ns]
                dst = [dsl_ref[n] for n in ns]
                dstb = [x.astype(BF16) for x in dst]
                dec = [dec_ref[n][0:1, :] for n in ns]
                doc = [do_ref[r, :] for r in rows]
                vc = [v_ref[r, :].astype(BF16) for r in rows]
                e = [jnp.exp(bc_ref[r, :]) for r in rows]
                einv = [jnp.exp(-bc_ref[r, :]) for r in rows]
                qd = [q_ref[rows[u], :] * e[u] for u in rng]
                ki = [k_ref[rows[u], :] * einv[u] for u in rng]
                kd = [ki[u] * dec[u] for u in rng]
                dqd_st = [_dot(doc[u], st0[u].astype(BF16)) for u in rng]
                dkd = [_dot(vc[u], dstb[u]) for u in rng]
                dv_st = [_dot(kd[u].astype(BF16), dstb[u], NT) for u in rng]
                dqd = [dqd_ref[rows[u], :] + dqd_st[u] for u in rng]
                dki = [dki_ref[r, :] for r in rows]
                dbc = [dqd[u] * qd[u] - dki[u] * ki[u] - dkd[u] * kd[u] for u in rng]
                cs = [_cumsum_chunk(cum_t, x) for x in dbc]
                for u in rng:
                    ddec = jnp.sum(dst[u] * st0[u], axis=0, keepdims=True)
                    dbl = jnp.sum(dkd[u] * kd[u], axis=0, keepdims=True) + ddec * dec[u]
                    dv_ref[rows[u], :] += dv_st[u]
                    dqd_ref[rows[u], :] = cs[u] + dbl
                    dki_ref[rows[u], :] = dki[u] * einv[u] + dkd[u] * (einv[u] * dec[u])
                    if dr == 0:
                        dq_ref[rows[u], :] = dqd[u] * e[u]
                    else:
                        dq_ref[rows[u], :] += dqd[u] * e[u]
                return carry

            lax.fori_loop(0, nc // grp2, grad_step, 0)

            sig, nsig, f = _gates(z_ref[...], lbv)
            common = (dqd_ref[...] / f - dki_ref[...]) * nsig
            dz_ref[dr] = (common * ((1.0 - lbv) * sig)).astype(BF16)
            dlb_ref[dr:dr + 1, :] = jnp.sum(common, axis=0, keepdims=True)

        dz_ref[2] = dv_ref[...].astype(BF16)
        dz_ref[3] = dq_ref[...].astype(BF16)

    sec = lambda s: pl.BlockSpec((None, t, HEAD), lambda h: (s, 0, h))
    col = pl.BlockSpec((seq, HEAD), lambda h: (0, h))
    tf32, tb16 = pltpu.VMEM((t, HEAD), F32), pltpu.VMEM((t, HEAD), BF16)
    states = pltpu.VMEM((nc, HEAD, HEAD), F32)
    return pl.pallas_call(
        body, name="hgrn_bwd", grid=(d // HEAD,),
        in_specs=[sec(0), sec(1), sec(2), sec(3), sec(4),
                  pl.BlockSpec((2, 3, HEAD), lambda h: (0, 0, h)), pl.BlockSpec((1, HEAD), lambda h: (0, h)),
                  col, col],
        out_specs=[pl.BlockSpec((5, t, HEAD), lambda h: (0, 0, h)),
                   pl.BlockSpec((1, HEAD), lambda h: (0, h)), pl.BlockSpec((2, HEAD), lambda h: (0, h))],
        out_shape=[jax.ShapeDtypeStruct((5, t, d), BF16), jax.ShapeDtypeStruct((1, d), F32),
                   jax.ShapeDtypeStruct((2, d), F32)],
        scratch_shapes=[tf32, tf32, tf32, pltpu.VMEM((nc, 8, HEAD), F32), tb16, tb16, tb16,
                        tf32, tf32, tf32, tf32, states, states],
        compiler_params=_cp(),
    )(z1, z1, z1, z1, z1, lbl, onorm, o, dr_out)


def _place():
    x, y, c = lax.axis_index("x"), lax.axis_index("y"), lax.axis_index("c")
    chips = [(1 - x, y), (x, 1 - y), (1 - x, 1 - y)]
    return x, y, c, chips


def allgather_shards(bufs):
    n = len(bufs)

    def body(*refs):
        outs = refs[n:2 * n]
        send_sems, recv_sems = refs[2 * n:]
        x, y, c, chips = _place()
        p = 2 * x + y
        half = [pl.ds(c * (s.shape[1] // 2), s.shape[1] // 2) for s in bufs]
        other = [pl.ds((1 - c) * (s.shape[1] // 2), s.shape[1] // 2) for s in bufs]

        def remote(i, k, src, dst, to):
            return pltpu.make_async_remote_copy(src_ref=src, dst_ref=dst, send_sem=send_sems.at[6 * i + k],
                                                recv_sem=recv_sems.at[6 * i + k], device_id=to, device_id_type=MESH)

        sends = []
        for i in range(n):
            for j, chip in enumerate(chips):
                mine = outs[i].at[p, half[i]]
                cp = remote(i, j, mine, mine, (*chip, c))
                cp.start()
                sends.append(cp)
        for i in range(n):
            for j, chip in enumerate(chips):
                landed = outs[i].at[2 * chip[0] + chip[1], half[i]]
                remote(i, j, landed, landed, (x, y, c)).wait_recv()
                cp = remote(i, 3 + j, landed, landed, (x, y, 1 - c))
                cp.start()
                sends.append(cp)
        for i in range(n):
            for j, chip in enumerate(chips):
                landed = outs[i].at[2 * chip[0] + chip[1], other[i]]
                remote(i, 3 + j, landed, landed, (x, y, c)).wait_recv()
        for cp in sends:
            cp.wait_send()

    return pl.pallas_call(
        body, name="allgather_shards",
        in_specs=[ANY] * n, out_specs=[ANY] * n,
        out_shape=[jax.ShapeDtypeStruct(s.shape, s.dtype) for s in bufs],
        input_output_aliases={i: i for i in range(n)},
        scratch_shapes=[pltpu.SemaphoreType.DMA((6 * n,)), pltpu.SemaphoreType.DMA((6 * n,))],
        compiler_params=pltpu.CompilerParams(has_side_effects=True),
    )(*bufs)


def exchange_halves(grads):
    n = len(grads)

    def body(*refs):
        ins, outs = refs[:n], refs[n:2 * n]
        send_sems, recv_sems = refs[2 * n:]
        x, y, c, _ = _place()
        copies = []
        for i in range(n):
            hr = grads[i].shape[1] // 2
            cp = pltpu.make_async_remote_copy(
                src_ref=ins[i].at[:, pl.ds((1 - c) * hr, hr)], dst_ref=outs[i],
                send_sem=send_sems.at[i], recv_sem=recv_sems.at[i],
                device_id=(x, y, 1 - c), device_id_type=MESH)
            cp.start()
            copies.append(cp)
        for cp in copies:
            cp.wait()

    return pl.pallas_call(
        body, name="exchange_halves",
        in_specs=[ANY] * n, out_specs=[ANY] * n,
        out_shape=[jax.ShapeDtypeStruct((4, g.shape[1] // 2, g.shape[2]), g.dtype) for g in grads],
        scratch_shapes=[pltpu.SemaphoreType.DMA((n,)), pltpu.SemaphoreType.DMA((n,))],
        compiler_params=pltpu.CompilerParams(has_side_effects=True),
    )(*grads)


def pair_sum(grad, got, core):
    _, r, cc = grad.shape
    hr = r // 2
    tr = 256 if hr % 256 == 0 else hr

    def body(core_ref, a_ref, b_ref, s_ref, sb_ref):
        s = a_ref[...] + b_ref[...]
        s_ref[...] = s
        sb_ref[...] = s.astype(BF16)

    nb = hr // tr
    grid_spec = pltpu.PrefetchScalarGridSpec(
        num_scalar_prefetch=1, grid=(4, nb),
        in_specs=[pl.BlockSpec((None, tr, cc), lambda qi, i, core_ref: (qi, core_ref[0] * nb + i, 0)),
                  pl.BlockSpec((None, tr, cc), lambda qi, i, core_ref: (qi, i, 0))],
        out_specs=[pl.BlockSpec((None, tr, cc), lambda qi, i, core_ref: (qi, i, 0)),
                   pl.BlockSpec((None, tr, cc), lambda qi, i, core_ref: (qi, i, 0))])
    return pl.pallas_call(
        body, name="pair_sum", grid_spec=grid_spec,
        out_shape=[jax.ShapeDtypeStruct((4, hr, cc), F32), jax.ShapeDtypeStruct((4, hr, cc), BF16)],
        compiler_params=_cp(),
    )(core, grad, got)


def scatter_to_owners(parts):
    n = len(parts)

    def body(*refs):
        ins, outs = refs[:n], refs[n:2 * n]
        send_sems, recv_sems = refs[2 * n:]
        x, y, c, chips = _place()
        copies = []
        for i in range(n):
            for j, chip in enumerate(chips):
                cp = pltpu.make_async_remote_copy(
                    src_ref=ins[i].at[2 * chip[0] + chip[1]], dst_ref=outs[i].at[j],
                    send_sem=send_sems.at[3 * i + j], recv_sem=recv_sems.at[3 * i + j],
                    device_id=(*chip, c), device_id_type=MESH)
                cp.start()
                copies.append(cp)
        for cp in copies:
            cp.wait()

    return pl.pallas_call(
        body, name="scatter_to_owners",
        in_specs=[ANY] * n, out_specs=[ANY] * n,
        out_shape=[jax.ShapeDtypeStruct((3,) + p.shape[1:], p.dtype) for p in parts],
        scratch_shapes=[pltpu.SemaphoreType.DMA((3 * n,)), pltpu.SemaphoreType.DMA((3 * n,))],
        compiler_params=pltpu.CompilerParams(has_side_effects=True),
    )(*parts)


def owner_sum(own, got, chip_core):
    _, hr, cc = own.shape
    tr = 256 if hr % 256 == 0 else hr
    nb = hr // tr

    def body(cc_ref, a_ref, b_ref, o_ref):
        s = a_ref[...] + b_ref[0].astype(F32)
        s = s + b_ref[1].astype(F32)
        o_ref[...] = s + b_ref[2].astype(F32)

    grid_spec = pltpu.PrefetchScalarGridSpec(
        num_scalar_prefetch=1, grid=(nb,),
        in_specs=[pl.BlockSpec((None, tr, cc), lambda i, cc_ref: (cc_ref[0], i, 0)),
                  pl.BlockSpec((3, tr, cc), lambda i, cc_ref: (0, i, 0))],
        out_specs=pl.BlockSpec((tr, cc), lambda i, cc_ref: (cc_ref[1] * nb + i, 0)))
    return pl.pallas_call(
        body, name="owner_sum", grid_spec=grid_spec,
        out_shape=jax.ShapeDtypeStruct((2 * hr, cc), F32), compiler_params=_cp(),
    )(chip_core, own, got)


def share_halves(bufs):
    n = len(bufs)

    def body(*refs):
        outs = refs[n:2 * n]
        send_sems, recv_sems = refs[2 * n:]
        x, y, c, _ = _place()
        copies = []
        for i in range(n):
            hr = bufs[i].shape[0] // 2
            mine = outs[i].at[pl.ds(c * hr, hr)]
            cp = pltpu.make_async_remote_copy(
                src_ref=mine, dst_ref=mine, send_sem=send_sems.at[i], recv_sem=recv_sems.at[i],
                device_id=(x, y, 1 - c), device_id_type=MESH)
            cp.start()
            copies.append((cp, outs[i].at[pl.ds((1 - c) * hr, hr)]))
        for i, (cp, theirs) in enumerate(copies):
            cp.wait_send()
            pltpu.make_async_remote_copy(
                src_ref=theirs, dst_ref=theirs, send_sem=send_sems.at[i], recv_sem=recv_sems.at[i],
                device_id=(x, y, c), device_id_type=MESH).wait_recv()

    return pl.pallas_call(
        body, name="share_halves",
        in_specs=[ANY] * n, out_specs=[ANY] * n,
        out_shape=[jax.ShapeDtypeStruct(b.shape, b.dtype) for b in bufs],
        input_output_aliases={i: i for i in range(n)},
        scratch_shapes=[pltpu.SemaphoreType.DMA((n,)), pltpu.SemaphoreType.DMA((n,))],
        compiler_params=pltpu.CompilerParams(has_side_effects=True),
    )(*bufs)


def allgather8(v, name):
    r, n = v.shape

    def body(v_ref, out_ref, send_sems, recv_sems):
        x, y, c, _ = _place()
        me = 4 * x + 2 * y + c
        out_ref[me] = v_ref[...]

        def copy(k, slot, to):
            return pltpu.make_async_remote_copy(
                src_ref=v_ref, dst_ref=out_ref.at[slot], send_sem=send_sems.at[k - 1],
                recv_sem=recv_sems.at[k - 1], device_id=to, device_id_type=MESH)

        peers = []
        for k in range(1, 8):
            px = 1 - x if (k >> 2) & 1 else x
            py = 1 - y if (k >> 1) & 1 else y
            pc = 1 - c if k & 1 else c
            peers.append((px, py, pc))
            copy(k, me, (px, py, pc)).start()
        for k, (px, py, pc) in enumerate(peers, start=1):
            copy(k, 4 * px + 2 * py + pc, (x, y, c)).wait_recv()
        for k, peer in enumerate(peers, start=1):
            copy(k, me, peer).wait_send()

    return pl.pallas_call(
        body, name=name, in_specs=[VMEM], out_specs=VMEM,
        out_shape=jax.ShapeDtypeStruct((8, r, n), v.dtype),
        scratch_shapes=[pltpu.SemaphoreType.DMA((7,)), pltpu.SemaphoreType.DMA((7,))],
        compiler_params=_cp(has_side_effects=True),
    )(v)


HBM = pl.BlockSpec(memory_space=pltpu.HBM)
SEM = pl.BlockSpec(memory_space=pltpu.SEMAPHORE)
DATAFLOW = pltpu.SideEffectType.DATAFLOW_SIDE_EFFECTING


def _descriptors(plan, refs, send_sems, recv_sems):
    x, y, c, _ = _place()
    sends, recvs = plan(refs)
    out = [pltpu.make_async_remote_copy(src_ref=src, dst_ref=dst, send_sem=send_sems.at[k],
                                        recv_sem=recv_sems.at[k], device_id=to, device_id_type=MESH)
           for k, (src, dst, to) in enumerate(sends)]
    inn = [pltpu.make_async_remote_copy(src_ref=land, dst_ref=land, send_sem=send_sems.at[k],
                                        recv_sem=recv_sems.at[k], device_id=(x, y, c), device_id_type=MESH)
           for k, land in enumerate(recvs)]
    return out, inn


def copies_start(name, arrays, n_copies, plan, after):
    na = len(arrays)

    def body(*refs):
        out, _ = _descriptors(plan, refs[:na], refs[na + 1], refs[na + 2])
        for cp in out:
            cp.start()
        refs[-1][...] = jnp.zeros((8, 128), F32)

    res = pl.pallas_call(
        body, name=name,
        out_shape=(pltpu.SemaphoreType.DMA((n_copies,)), pltpu.SemaphoreType.DMA((n_copies,)),
                   *[pltpu.HBM(a.shape, a.dtype) for a in arrays], jax.ShapeDtypeStruct((8, 128), F32)),
        in_specs=[HBM] * na + [ANY], out_specs=(SEM, SEM, *[HBM] * na, VMEM),
        input_output_aliases={i: i + 2 for i in range(na)},
        compiler_params=pltpu.CompilerParams(has_side_effects=DATAFLOW),
    )(*[pltpu.with_memory_space_constraint(a, pltpu.HBM) for a in arrays], after)
    return res[0], res[1], list(res[2:2 + na]), res[-1]


def copies_wait(name, started, plan, after):
    send_sems, recv_sems, arrays, _ = started
    na = len(arrays)

    def body(*refs):
        out, inn = _descriptors(plan, refs[:na], refs[na], refs[na + 1])
        for cp in out:
            cp.wait_send()
        for cp in inn:
            cp.wait_recv()

    return pl.pallas_call(
        body, name=name, out_shape=tuple(pltpu.HBM(a.shape, a.dtype) for a in arrays),
        in_specs=[HBM] * na + [SEM, SEM, ANY], out_specs=tuple([HBM] * na),
        input_output_aliases={i: i for i in range(na)},
        compiler_params=pltpu.CompilerParams(has_side_effects=DATAFLOW),
    )(*arrays, send_sems, recv_sems, after)


def _rows_half(r, c):
    return pl.ds(c * (r // 2), r // 2), pl.ds((1 - c) * (r // 2), r // 2)


def plan_gather_ici(refs):
    x, y, c, chips = _place()
    p = 2 * x + y
    sends, recvs = [], []
    for buf in refs:
        mine, _ = _rows_half(buf.shape[1], c)
        for chip in chips:
            sends.append((buf.at[p, mine], buf.at[p, mine], (*chip, c)))
            recvs.append(buf.at[2 * chip[0] + chip[1], mine])
    return sends, recvs


def plan_gather_d2d(refs):
    x, y, c, chips = _place()
    sends, recvs = [], []
    for buf in refs:
        mine, theirs = _rows_half(buf.shape[1], c)
        for chip in chips:
            slot = 2 * chip[0] + chip[1]
            sends.append((buf.at[slot, mine], buf.at[slot, mine], (x, y, 1 - c)))
            recvs.append(buf.at[slot, theirs])
    return sends, recvs


def plan_exchange(refs):
    x, y, c, _ = _place()
    n = len(refs) // 2
    sends, recvs = [], []
    for grad, land in zip(refs[:n], refs[n:]):
        _, theirs = _rows_half(grad.shape[1], c)
        sends.append((grad.at[:, theirs], land, (x, y, 1 - c)))
        recvs.append(land)
    return sends, recvs


def plan_scatter(refs):
    x, y, c, chips = _place()
    n = len(refs) // 2
    sends, recvs = [], []
    for part, land in zip(refs[:n], refs[n:]):
        for j, chip in enumerate(chips):
            sends.append((part.at[2 * chip[0] + chip[1]], land.at[j], (*chip, c)))
            recvs.append(land.at[j])
    return sends, recvs


def plan_share(refs):
    x, y, c, _ = _place()
    sends, recvs = [], []
    for buf in refs:
        mine, theirs = _rows_half(buf.shape[0], c)
        sends.append((buf.at[mine], buf.at[mine], (x, y, 1 - c)))
        recvs.append(buf.at[theirs])
    return sends, recvs


def put_in_slot(w, chip, dtype, name):
    r, c = w.shape
    tr = 256 if r % 256 == 0 else r

    def body(chip_ref, w_ref, o_ref):
        o_ref[...] = w_ref[...].astype(dtype)

    grid_spec = pltpu.PrefetchScalarGridSpec(
        num_scalar_prefetch=1, grid=(r // tr,),
        in_specs=[pl.BlockSpec((tr, c), lambda i, chip_ref: (i, 0))],
        out_specs=pl.BlockSpec((None, tr, c), lambda i, chip_ref: (chip_ref[0], i, 0)))
    return pl.pallas_call(body, name=name, grid_spec=grid_spec,
                          out_shape=jax.ShapeDtypeStruct((4, r, c), dtype), compiler_params=_cp())(chip, w)


def ada_fwd(s_in, ada_w, ada_b, tn):
    nl, d, ws = ada_w.shape

    def body(s_ref, w_ref, b_ref, so_ref, mod_ref):
        s = _silu(s_ref[...])
        so_ref[...] = s
        mod_ref[...] = _dot(s.astype(BF16), w_ref[...].astype(BF16)) + b_ref[...]

    return pl.pallas_call(
        body, name="ada_fwd", grid=(nl, ws // tn),
        in_specs=[pl.BlockSpec((16, d), lambda l, j: (0, 0)),
                  pl.BlockSpec((None, d, tn), lambda l, j: (l, 0, j)),
                  pl.BlockSpec((None, 1, tn), lambda l, j: (l, 0, j))],
        out_specs=[pl.BlockSpec((16, d), lambda l, j: (0, 0)),
                   pl.BlockSpec((None, 16, tn), lambda l, j: (l, 0, j))],
        out_shape=[jax.ShapeDtypeStruct((16, d), F32), jax.ShapeDtypeStruct((nl, 16, ws), F32)],
        compiler_params=_cp(),
    )(s_in, ada_w, ada_b)


def _adamw_math(w, g, m, v):
    m = ADAM_B1 * m + (1.0 - ADAM_B1) * g
    v = ADAM_B2 * v + (1.0 - ADAM_B2) * (g * g)
    m_hat = m / (1.0 - ADAM_B1 ** ADAM_STEP)
    v_hat = v / (1.0 - ADAM_B2 ** ADAM_STEP)
    delta = -ADAM_LR * (m_hat / (jnp.sqrt(v_hat) + ADAM_EPS) + ADAM_WD * w)
    return delta, m, v


def ada_bwd_adamw(s, dm, w, m, v):
    nl, d, ws = w.shape
    tr = 256 if d % 256 == 0 else 128

    def body(s_ref, dm_ref, w_ref, m_ref, v_ref, g_ref, dl_ref, mo_ref, vo_ref, dc_ref):
        dmv = dm_ref[...].astype(BF16)
        wv = w_ref[...]
        g = _dot(s_ref[...].astype(BF16), dmv, TN)
        g_ref[...] = g
        dl_ref[...], mo_ref[...], vo_ref[...] = _adamw_math(wv, g, m_ref[...], v_ref[...])
        dc_ref[...] = _dot(dmv[8:16, :], wv.astype(BF16), NT)

    wblk = pl.BlockSpec((None, tr, ws), lambda l, i: (l, i, 0))
    wshape = jax.ShapeDtypeStruct((nl, d, ws), F32)
    return pl.pallas_call(
        body, name="ada_bwd_adamw", grid=(nl, d // tr),
        in_specs=[pl.BlockSpec((16, tr), lambda l, i: (0, i)),
                  pl.BlockSpec((None, 16, ws), lambda l, i: (l, 0, 0)), wblk, wblk, wblk],
        out_specs=[wblk, wblk, wblk, wblk, pl.BlockSpec((None, 8, tr), lambda l, i: (l, 0, i))],
        out_shape=[wshape, wshape, wshape, wshape, jax.ShapeDtypeStruct((nl, 8, d), F32)],
        compiler_params=_cp(),
    )(s, dm, w, m, v)


def adamw(w, g, m, v, name):
    r, c = w.shape
    tr = 256 if r % 256 == 0 else r

    def body(w_ref, g_ref, m_ref, v_ref, dl_ref, mo_ref, vo_ref):
        dl_ref[...], mo_ref[...], vo_ref[...] = _adamw_math(w_ref[...], g_ref[...], m_ref[...], v_ref[...])

    blk = pl.BlockSpec((tr, c), lambda i: (i, 0))
    shape = jax.ShapeDtypeStruct((r, c), F32)
    return pl.pallas_call(body, name=name, grid=(r // tr,), in_specs=[blk] * 4, out_specs=[blk] * 3,
                          out_shape=[shape] * 3, compiler_params=_cp())(w, g, m, v)


SMALL_ROWS = 24
ROW_MOD = 10


def small_reduce(gathered):
    _, rows, d = gathered.shape

    def body(g_ref, o_ref):
        tot = g_ref[0]
        for b in range(1, 8):
            tot = tot + g_ref[b]
        o_ref[0:rows, :] = tot
        for layer in range(2):
            lat = ROW_MOD + 6 * layer
            o_ref[24 + 3 * layer:27 + 3 * layer, :] = tot[lat:lat + 3, :] + tot[lat + 3:lat + 6, :]
        o_ref[30:32, :] = jnp.zeros((2, d), F32)

    return pl.pallas_call(body, name="small_reduce", in_specs=[VMEM], out_specs=VMEM,
                          out_shape=jax.ShapeDtypeStruct((32, d), F32), compiler_params=_cp())(gathered)


def lb_logits_grad(lbl, dlb):
    _, _, n = lbl.shape

    def body(l_ref, d_ref, o_ref):
        for dr in range(2):
            _, (p0, p1, p2) = _lower_bound(l_ref, dr)
            dv = d_ref[dr:dr + 1, :]
            o_ref[dr, 0:1, :] = p0 * p2 * dv
            o_ref[dr, 1:2, :] = p1 * p2 * dv
            o_ref[dr, 2:3, :] = -p2 * (p0 + p1) * dv

    return pl.pallas_call(body, name="lb_logits_grad", in_specs=[VMEM, VMEM], out_specs=VMEM,
                          out_shape=jax.ShapeDtypeStruct((2, 3, n), F32), compiler_params=_cp())(lbl, dlb)


def c_ctx_grad(parts, c_ctx):
    d = c_ctx.shape[1]

    def body(p_ref, c_ref, o_ref):
        tot = p_ref[0, 0:1, :]
        for chip in range(1, 4):
            tot = tot + p_ref[2 * chip, 0:1, :]
        o_ref[...] = tot * _dsilu(c_ref[...])

    return pl.pallas_call(body, name="c_ctx_grad", in_specs=[VMEM, VMEM], out_specs=VMEM,
                          out_shape=jax.ShapeDtypeStruct((1, d), F32), compiler_params=_cp())(parts, c_ctx)


def _reduce_scatter(grads, core, chip_core):
    got = exchange_halves(grads)
    sums = [pair_sum(g, r, core) for g, r in zip(grads, got)]
    recv = scatter_to_owners([sb for _, sb in sums])
    reduced = [owner_sum(s, r, chip_core) for (s, _), r in zip(sums, recv)]
    return share_halves(reduced)


def kernel(x, c, ctx, c_ctx, ada_w, ada_b, pre_g, post_g, ev_w_in, ev_pool_w, ev_pool_scale, ev_conv_w, ev_conv_b, ev_w_out, od_w_in, od_onorm_g, od_w_out, lb_logits, loss_target, m_c_ctx, m_ada_w, m_ada_b, m_pre_g, m_post_g, m_ev_w_in, m_ev_pool_w, m_ev_pool_scale, m_ev_conv_w, m_ev_conv_b, m_ev_w_out, m_od_w_in, m_od_onorm_g, m_od_w_out, m_lb_logits, v_c_ctx, v_ada_w, v_ada_b, v_pre_g, v_post_g, v_ev_w_in, v_ev_pool_w, v_ev_pool_scale, v_ev_conv_w, v_ev_conv_b, v_ev_w_out, v_od_w_in, v_od_onorm_g, v_od_w_out, v_lb_logits):
    _, seq, d = x.shape
    cx = ctx.shape[1]
    t = cx + seq
    half_d = d // 2
    g = half_d // N_POOL
    tn = d // 4
    xi, yi, ci = lax.axis_index("x"), lax.axis_index("y"), lax.axis_index("c")
    chip = 2 * xi + yi
    me = 2 * chip + ci
    core_arr = jnp.reshape(ci, (1,)).astype(jnp.int32)
    chip_arr = jnp.reshape(chip, (1,)).astype(jnp.int32)
    chip_core_arr = jnp.stack([chip, ci]).astype(jnp.int32)

    pad = lambda a, rows: jnp.concatenate([a, jnp.zeros((rows - a.shape[0], g), F32)], axis=0)
    small = jnp.concatenate([
        ev_pool_w.reshape(g, g), pad(ev_conv_w.reshape(3, g), 8), pad(od_onorm_g.reshape(2, g), 8),
        pad(lb_logits.reshape(12, g), 16)], axis=0)
    ev_in_g, ev_out_g, small_g = allgather_shards([
        put_in_slot(ev_w_in[0], chip_arr, BF16, "cast_ev_w_in"),
        put_in_slot(ev_w_out[0], chip_arr, BF16, "cast_ev_w_out"),
        put_in_slot(small, chip_arr, F32, "place_small")])
    od_ici = copies_start("gather_od_ici_start", [
        put_in_slot(od_w_in[0], chip_arr, BF16, "cast_od_w_in"),
        put_in_slot(od_w_out[0], chip_arr, BF16, "cast_od_w_out")], 6, plan_gather_ici, small_g)
    ev_out3 = ev_out_g.reshape(1, d, d)
    pool_w_full = small_g[:, :g].reshape(4, N_POOL, g // 4, g).transpose(1, 0, 2, 3).reshape(N_POOL, g, g)
    conv_w_full = small_g[:, g:g + 3].transpose(1, 0, 2).reshape(3, half_d)
    onorm_full = small_g[:, g + 8:g + 10].reshape(1, d)
    lbl_full = small_g[:, g + 16:g + 28].reshape(4, 2, 3, 2 * g).transpose(1, 2, 0, 3).reshape(2, 3, d)

    c_rows = jnp.concatenate([c, jnp.zeros((7, d), F32)], axis=0)
    c_all = allgather8(c_rows, "allgather_c")[:, 0, :]
    s_in = jnp.concatenate([c_all, c_ctx.reshape(1, d), jnp.zeros((7, d), F32)], axis=0)
    ws_ada = ada_w.shape[2]
    ada_b_mine = lax.dynamic_slice(ada_b, (0, chip * ws_ada), (2, ws_ada)).reshape(2, 1, ws_ada)
    s_act, mod_mine = ada_fwd(s_in, ada_w, ada_b_mine, tn)
    mod_all = allgather8(mod_mine.reshape(32, ws_ada), "allgather_mod")
    mod_full = mod_all[0::2].reshape(4, 2, 16, ws_ada).transpose(1, 2, 0, 3).reshape(2, 16, 3 * d)
    mod_lat = lax.dynamic_slice(mod_full, (0, me, 0), (2, 1, 3 * d))
    mods = jnp.concatenate([mod_full[:, 8:9], mod_lat], axis=1)
    shift, scale, gate = mods[:, :, :d], mods[:, :, d:2 * d], mods[:, :, 2 * d:]

    xs = jnp.concatenate([ctx[0], x[0]], axis=0)

    h0 = normmod_fwd(xs, pre_g[0:1] + od_ici[3][0:1, 0:1], shift[0], scale[0], cx)
    z0 = mm_nn(h0, ev_in_g, half_d, tn, "mm_ev_in")
    u_a = mix_a_fwd(z0, pool_w_full, ev_pool_scale, cx)
    u_b = mix_b_fwd(z0, conv_w_full, ev_conv_b, cx)
    u = jnp.concatenate([u_a, u_b], axis=1)
    y0 = mm_nn(u, ev_out3, d, tn, "mm_ev_out")[0]
    xs1 = post_fwd(xs, y0, post_g[0:1], gate[0], cx)
    od_d2d = copies_start("gather_od_d2d_start",
                          copies_wait("gather_od_ici_wait", od_ici, plan_gather_ici, xs1),
                          6, plan_gather_d2d, xs1)
    od_in_g, od_out_g = copies_wait("gather_od_d2d_wait", od_d2d, plan_gather_d2d, od_d2d[3])
    od_out3 = od_out_g.reshape(1, d, d)

    h1 = normmod_fwd(xs1, pre_g[1:2], shift[1], scale[1], cx)
    z1 = mm_nn(h1, od_in_g, d, tn, "mm_od_in")
    o1, r1 = hgrn_fwd(z1, lbl_full, onorm_full, cx)
    y1 = mm_nn(r1, od_out3, d, tn, "mm_od_out")[0]
    sq, dx2 = post_loss(xs1, y1, post_g[1:2], gate[1], loss_target[0], cx)
    loss = lax.psum(sq[0, 0] * (0.5 / d), ("x", "y", "c"))

    dy1, dgate1, dpost1 = post_bwd(dx2, y1, post_g[1:2], gate[1], cx, True)
    dr1 = mm_nt(dy1[None], od_out3, tn, "mm_od_out_dx")
    g_od_out = mm_tn(r1, dy1[None], d, tn, "mm_od_out_dw")
    dz1, donorm, dlb = hgrn_bwd(z1, lbl_full, onorm_full, o1, dr1, cx)
    dh1 = mm_nt(dz1, od_in_g, tn, "mm_od_in_dx")
    g_od_in = mm_tn(h1, dz1, od_in_g.shape[2], tn, "mm_od_in_dw")
    dxs1, dpre1, dshift1, dscale1 = normmod_bwd(xs1, dh1, pre_g[1:2], scale[1], dx2, cx, True)

    od_grads = [g_od_in, g_od_out.reshape(4, d // 4, d)]
    half_zone = lambda a, lead, dt: lax.empty((lead, a.shape[1] // 2, a.shape[2]), dt)
    od_ex = copies_start("reduce_od_exchange_start", od_grads + [half_zone(a, 4, F32) for a in od_grads],
                         2, plan_exchange, dxs1)

    dy0, dgate0, dpost0 = post_bwd(dxs1, y0, post_g[0:1] + od_ex[3][0:1, 0:1], gate[0], cx, False)
    du = mm_nt(dy0[None], ev_out3, tn, "mm_ev_out_dx")
    g_ev_out = mm_tn(u, dy0[None], d, tn, "mm_ev_out_dw")
    od_got = copies_wait("reduce_od_exchange_wait", od_ex, plan_exchange, g_ev_out)
    od_sums = [pair_sum(od_got[i], od_got[2 + i], core_arr) for i in range(2)]
    od_sc = copies_start("reduce_od_scatter_start",
                         [sb for _, sb in od_sums] + [half_zone(a, 3, BF16) for a in od_grads],
                         6, plan_scatter, du)
    dz0a, g_pool_w, dpool_scale = mix_a_bwd(z0, du, pool_w_full, ev_pool_scale + od_sc[3][0:1, 0:1], cx)
    dz0b, dconv_w, dconv_b = mix_b_bwd(z0, du, conv_w_full, ev_conv_b + od_sc[3][0:1, 0:1], cx)
    dz0 = jnp.concatenate([dz0a, dz0b], axis=0)
    dh0 = mm_nt(dz0, ev_in_g, tn, "mm_ev_in_dx")
    g_ev_in = mm_tn(h0, dz0, ev_in_g.shape[2], tn, "mm_ev_in_dw")
    dxs0, dpre0, dshift0, dscale0 = normmod_bwd(xs, dh0, pre_g[0:1], scale[0], dxs1, cx, False)
    grad_x = dxs0[cx:][None]
    od_recv = copies_wait("reduce_od_scatter_wait", od_sc, plan_scatter, dxs0)
    od_sh = copies_start("reduce_od_share_start",
                         [owner_sum(od_sums[i][0], od_recv[2 + i], chip_core_arr) for i in range(2)],
                         2, plan_share, dxs0)

    grad_ev_w_in, grad_ev_w_out, grad_pool_w = _reduce_scatter(
        [g_ev_in, g_ev_out.reshape(4, d // 4, d), g_pool_w.reshape(4, g, g)], core_arr, chip_core_arr)
    grad_od_w_in, grad_od_w_out = copies_wait("reduce_od_share_wait", od_sh, plan_share, grad_ev_w_out)

    zrow = jnp.zeros((1, d), F32)
    small_rows = jnp.concatenate([
        dpre0, dpre1, dpost0, dpost1,
        jnp.concatenate([dpool_scale, dconv_b], axis=1),
        jnp.concatenate([dconv_w.reshape(1, 3 * half_d), jnp.zeros((1, half_d), F32)], axis=1).reshape(2, d),
        donorm, dlb,
        dshift0[1:2], dscale0[1:2], dgate0[1:2], dshift0[0:1], dscale0[0:1], dgate0[0:1],
        dshift1[1:2], dscale1[1:2], dgate1[1:2], dshift1[0:1], dscale1[0:1], zrow,
        zrow, zrow], axis=0)
    small_all = allgather8(small_rows, "allgather_small")
    tot = small_reduce(small_all)

    dm_rows = []
    for layer in range(2):
        lat = ROW_MOD + 6 * layer
        dm_lat = small_all[:, lat:lat + 3].reshape(8, 3 * d)
        dm_ctx = tot[lat + 3:lat + 6].reshape(1, 3 * d)
        dm_rows.append(jnp.concatenate([dm_lat, dm_ctx, jnp.zeros((7, 3 * d), F32)], axis=0))
    dm_full = jnp.stack(dm_rows)
    dm_mine = lax.dynamic_slice(dm_full, (0, 0, chip * ws_ada), (2, 16, ws_ada))
    grad_ada_w, delta_ada_w, new_m_ada_w, new_v_ada_w, dctx_part = ada_bwd_adamw(
        s_act, dm_mine, ada_w, m_ada_w, v_ada_w)
    dctx_all = allgather8(dctx_part[0] + dctx_part[1], "allgather_dctx")
    grad_c_ctx = c_ctx_grad(dctx_all, c_ctx.reshape(1, d)).reshape(d)

    grad_ada_b = tot[24:30].reshape(2, 3 * d)
    grad_pre_g = tot[0:2]
    grad_post_g = tot[2:4]
    grad_ev_pool_scale = tot[4:5, :half_d]
    grad_ev_conv_b = tot[4:5, half_d:]
    conv_w_tot = tot[5:7].reshape(1, 2 * d)[:, :3 * half_d].reshape(3, N_POOL, g)
    grad_ev_conv_w = lax.dynamic_slice(conv_w_tot, (0, chip, 0), (3, 1, g)).reshape(1, 3, g)
    grad_od_onorm_g = lax.dynamic_slice(tot[7:8], (0, chip * 2 * g), (1, 2 * g))
    dlb_mine = lax.dynamic_slice(tot[8:10], (0, chip * 2 * g), (2, 2 * g))
    grad_lb_logits = lb_logits_grad(lb_logits, dlb_mine)
    grad_ev_w_in = grad_ev_w_in[None]
    grad_od_w_in = grad_od_w_in[None]
    grad_ev_w_out = grad_ev_w_out[None]
    grad_od_w_out = grad_od_w_out[None]
    grad_ev_pool_w = grad_pool_w.reshape(1, N_POOL, g // 4, g)

    def step(w, gr, m, v, name):
        shape = w.shape
        cols = shape[-1]
        two_d = lambda a: a.reshape(-1, cols)
        dl, mo, vo = adamw(two_d(w), two_d(gr), two_d(m), two_d(v), "adamw_" + name)
        return dl.reshape(shape), mo.reshape(shape), vo.reshape(shape)

    upd = {
        "c_ctx": step(c_ctx, grad_c_ctx, m_c_ctx, v_c_ctx, "c_ctx"),
        "ada_w": (delta_ada_w, new_m_ada_w, new_v_ada_w),
        "ada_b": step(ada_b, grad_ada_b, m_ada_b, v_ada_b, "ada_b"),
        "pre_g": step(pre_g, grad_pre_g, m_pre_g, v_pre_g, "pre_g"),
        "post_g": step(post_g, grad_post_g, m_post_g, v_post_g, "post_g"),
        "ev_w_in": step(ev_w_in, grad_ev_w_in, m_ev_w_in, v_ev_w_in, "ev_w_in"),
        "ev_pool_w": step(ev_pool_w, grad_ev_pool_w, m_ev_pool_w, v_ev_pool_w, "ev_pool_w"),
        "ev_pool_scale": step(ev_pool_scale, grad_ev_pool_scale, m_ev_pool_scale, v_ev_pool_scale, "ev_pool_scale"),
        "ev_conv_w": step(ev_conv_w, grad_ev_conv_w, m_ev_conv_w, v_ev_conv_w, "ev_conv_w"),
        "ev_conv_b": step(ev_conv_b, grad_ev_conv_b, m_ev_conv_b, v_ev_conv_b, "ev_conv_b"),
        "ev_w_out": step(ev_w_out, grad_ev_w_out, m_ev_w_out, v_ev_w_out, "ev_w_out"),
        "od_w_in": step(od_w_in, grad_od_w_in, m_od_w_in, v_od_w_in, "od_w_in"),
        "od_onorm_g": step(od_onorm_g, grad_od_onorm_g, m_od_onorm_g, v_od_onorm_g, "od_onorm_g"),
        "od_w_out": step(od_w_out, grad_od_w_out, m_od_w_out, v_od_w_out, "od_w_out"),
        "lb_logits": step(lb_logits, grad_lb_logits, m_lb_logits, v_lb_logits, "lb_logits"),
    }
    names = ["c_ctx", "ada_w", "ada_b", "pre_g", "post_g", "ev_w_in", "ev_pool_w", "ev_pool_scale",
             "ev_conv_w", "ev_conv_b", "ev_w_out", "od_w_in", "od_onorm_g", "od_w_out", "lb_logits"]
    grads = [grad_c_ctx, grad_ada_w, grad_ada_b, grad_pre_g, grad_post_g, grad_ev_w_in, grad_ev_pool_w,
             grad_ev_pool_scale, grad_ev_conv_w, grad_ev_conv_b, grad_ev_w_out, grad_od_w_in,
             grad_od_onorm_g, grad_od_w_out, grad_lb_logits]
    return (loss, grad_x, *grads, *[upd[k][0] for k in names], *[upd[k][1] for k in names],
            *[upd[k][2] for k in names])
```

```python
import functools

import jax
import jax.numpy as jnp
from jax import lax
from jax.experimental import pallas as pl
from jax.experimental.pallas import tpu as pltpu

EPS = 1e-6
GRID_W_LOG2 = 6
CHUNK = 64
HEAD = 128
N_POOL = 4
ADAM_LR, ADAM_B1, ADAM_B2, ADAM_EPS, ADAM_WD, ADAM_STEP = 0.001, 0.9, 0.999, 1e-08, 0.01, 10
VMEM_LIMIT = 56 * 1024 * 1024
MESH = pl.DeviceIdType.MESH
F32, BF16 = jnp.float32, jnp.bfloat16
ANY = pl.BlockSpec(memory_space=pl.ANY)
VMEM = pl.BlockSpec(memory_space=pltpu.VMEM)


def _cp(**kw):
    return pltpu.CompilerParams(vmem_limit_bytes=VMEM_LIMIT, **kw)


def _silu(x):
    return x * jax.nn.sigmoid(x)


def _dsilu(x):
    s = jax.nn.sigmoid(x)
    return s * (1.0 + x * (1.0 - s))


def _dot(a, b, dims=((1,), (0,)), precision=None):
    return lax.dot_general(a, b, (dims, ((), ())), preferred_element_type=F32, precision=precision)


NN = ((1,), (0,))
NT = ((1,), (1,))
TN = ((0,), (0,))


def _row_block(cx):
    return 256 if cx % 256 == 0 else 128


def normmod_fwd(xs, g, shift, scale, cx):
    t, d = xs.shape
    tm = _row_block(cx)
    nctx = cx // tm

    def body(x_ref, g_ref, sh_ref, sc_ref, h_ref):
        is_ctx = pl.program_id(0) < nctx
        x = x_ref[...]
        rstd = lax.rsqrt(jnp.mean(x * x, axis=-1, keepdims=True) + EPS)
        sc = jnp.where(is_ctx, sc_ref[0:1, :], sc_ref[1:2, :])
        sh = jnp.where(is_ctx, sh_ref[0:1, :], sh_ref[1:2, :])
        h_ref[...] = ((x * rstd) * g_ref[...] * (1.0 + sc) + sh).astype(BF16)

    row = pl.BlockSpec((tm, d), lambda i: (i, 0))
    vec = lambda r: pl.BlockSpec((r, d), lambda i: (0, 0))
    return pl.pallas_call(
        body, name="normmod_fwd", grid=(t // tm,),
        in_specs=[row, vec(1), vec(2), vec(2)], out_specs=row,
        out_shape=jax.ShapeDtypeStruct((t, d), BF16), compiler_params=_cp(),
    )(xs, g, shift, scale)


def normmod_bwd(xs, dh, g, scale, dres, cx, res_is_latent_only):
    t, d = xs.shape
    tm = _row_block(cx)
    nctx = cx // tm

    def body(x_ref, dh_ref, g_ref, sc_ref, dres_ref, dx_ref, dg_ref, dsh_ref, dsc_ref):
        i = pl.program_id(0)
        is_ctx = i < nctx

        @pl.when(i == 0)
        def _():
            dg_ref[...] = jnp.zeros_like(dg_ref)
            dsh_ref[...] = jnp.zeros_like(dsh_ref)
            dsc_ref[...] = jnp.zeros_like(dsc_ref)

        x = x_ref[...]
        dh = dh_ref[...]
        gv = g_ref[...]
        rstd = lax.rsqrt(jnp.mean(x * x, axis=-1, keepdims=True) + EPS)
        xhat = x * rstd
        sc = jnp.where(is_ctx, sc_ref[0:1, :], sc_ref[1:2, :])
        dsh = jnp.sum(dh, axis=0, keepdims=True)
        dhx = dh * xhat
        dsc = jnp.sum(dhx * gv, axis=0, keepdims=True)
        dg_ref[...] += jnp.sum(dhx * (1.0 + sc), axis=0, keepdims=True)
        zero = jnp.zeros_like(dsh)
        dsh_ref[0:1, :] += jnp.where(is_ctx, dsh, zero)
        dsh_ref[1:2, :] += jnp.where(is_ctx, zero, dsh)
        dsc_ref[0:1, :] += jnp.where(is_ctx, dsc, zero)
        dsc_ref[1:2, :] += jnp.where(is_ctx, zero, dsc)
        dxhat = dh * (gv * (1.0 + sc))
        dx = rstd * (dxhat - xhat * jnp.mean(dxhat * xhat, axis=-1, keepdims=True))
        res = dres_ref[...]
        if res_is_latent_only:
            res = jnp.where(is_ctx, jnp.zeros_like(res), res)
        dx_ref[...] = dx + res

    row = pl.BlockSpec((tm, d), lambda i: (i, 0))
    if res_is_latent_only:
        res_spec = pl.BlockSpec((tm, d), lambda i: (jnp.maximum(i - nctx, 0), 0))
    else:
        res_spec = row
    vec = lambda r: pl.BlockSpec((r, d), lambda i: (0, 0))
    return pl.pallas_call(
        body, name="normmod_bwd", grid=(t // tm,),
        in_specs=[row, row, vec(1), vec(2), res_spec],
        out_specs=[row, vec(1), vec(2), vec(2)],
        out_shape=[jax.ShapeDtypeStruct((t, d), F32), jax.ShapeDtypeStruct((1, d), F32),
                   jax.ShapeDtypeStruct((2, d), F32), jax.ShapeDtypeStruct((2, d), F32)],
        compiler_params=_cp(),
    )(xs, dh, g, scale, dres)


def post_fwd(xs, y, pg, gate, cx):
    t, d = xs.shape
    tm = _row_block(cx)
    nctx = cx // tm

    def body(x_ref, y_ref, pg_ref, gate_ref, o_ref):
        is_ctx = pl.program_id(0) < nctx
        y = y_ref[...]
        rstd = lax.rsqrt(jnp.mean(y * y, axis=-1, keepdims=True) + EPS)
        gt = jnp.where(is_ctx, gate_ref[0:1, :], gate_ref[1:2, :])
        o_ref[...] = x_ref[...] + gt * ((y * rstd) * pg_ref[...])

    row = pl.BlockSpec((tm, d), lambda i: (i, 0))
    vec = lambda r: pl.BlockSpec((r, d), lambda i: (0, 0))
    return pl.pallas_call(
        body, name="post_fwd", grid=(t // tm,),
        in_specs=[row, row, vec(1), vec(2)], out_specs=row,
        out_shape=jax.ShapeDtypeStruct((t, d), F32), compiler_params=_cp(),
    )(xs, y, pg, gate)


def post_loss(xs, y, pg, gate, target, cx):
    t, d = xs.shape
    n = y.shape[0]
    tm = _row_block(cx)
    nctx = cx // tm

    def body(x_ref, y_ref, pg_ref, gate_ref, tgt_ref, sq_ref, dx_ref):
        @pl.when(pl.program_id(0) == 0)
        def _():
            sq_ref[...] = jnp.zeros_like(sq_ref)

        y = y_ref[...]
        rstd = lax.rsqrt(jnp.mean(y * y, axis=-1, keepdims=True) + EPS)
        x2 = x_ref[...] + gate_ref[1:2, :] * ((y * rstd) * pg_ref[...])
        err = x2 - tgt_ref[...]
        sq_ref[...] += jnp.sum(err * err)
        dx_ref[...] = err * (1.0 / d)

    row = pl.BlockSpec((tm, d), lambda i: (i, 0))
    xrow = pl.BlockSpec((tm, d), lambda i: (i + nctx, 0))
    vec = lambda r: pl.BlockSpec((r, d), lambda i: (0, 0))
    return pl.pallas_call(
        body, name="post_loss", grid=(n // tm,),
        in_specs=[xrow, row, vec(1), vec(2), row],
        out_specs=[pl.BlockSpec((8, 128), lambda i: (0, 0)), row],
        out_shape=[jax.ShapeDtypeStruct((8, 128), F32), jax.ShapeDtypeStruct((n, d), F32)],
        compiler_params=_cp(),
    )(xs, y, pg, gate, target)


def post_bwd(dxo, y, pg, gate, cx, latent_only):
    m, d = y.shape
    tm = _row_block(cx)
    nctx = 0 if latent_only else cx // tm

    def body(dx_ref, y_ref, pg_ref, gate_ref, dy_ref, dgate_ref, dpg_ref):
        i = pl.program_id(0)
        is_ctx = i < nctx

        @pl.when(i == 0)
        def _():
            dgate_ref[...] = jnp.zeros_like(dgate_ref)
            dpg_ref[...] = jnp.zeros_like(dpg_ref)

        y = y_ref[...]
        dx = dx_ref[...]
        pgv = pg_ref[...]
        rstd = lax.rsqrt(jnp.mean(y * y, axis=-1, keepdims=True) + EPS)
        yhat = y * rstd
        gt = jnp.where(is_ctx, gate_ref[0:1, :], gate_ref[1:2, :])
        dxy = dx * yhat
        dgt = jnp.sum(dxy * pgv, axis=0, keepdims=True)
        zero = jnp.zeros_like(dgt)
        dgate_ref[0:1, :] += jnp.where(is_ctx, dgt, zero)
        dgate_ref[1:2, :] += jnp.where(is_ctx, zero, dgt)
        dpg_ref[...] += jnp.sum(dxy * gt, axis=0, keepdims=True)
        dyhat = dx * (gt * pgv)
        dy = rstd * (dyhat - yhat * jnp.mean(dyhat * yhat, axis=-1, keepdims=True))
        dy_ref[...] = dy.astype(BF16)

    row = pl.BlockSpec((tm, d), lambda i: (i, 0))
    vec = lambda r: pl.BlockSpec((r, d), lambda i: (0, 0))
    return pl.pallas_call(
        body, name="post_bwd", grid=(m // tm,),
        in_specs=[row, row, vec(1), vec(2)], out_specs=[row, vec(2), vec(1)],
        out_shape=[jax.ShapeDtypeStruct((m, d), BF16), jax.ShapeDtypeStruct((2, d), F32),
                   jax.ShapeDtypeStruct((1, d), F32)],
        compiler_params=_cp(),
    )(dxo, y, pg, gate)


def _split_rows(m):
    for cand in (1024, 768, 512, 384, 256, 128):
        if m % cand == 0 and m // cand >= 2:
            return cand
    return m


def mm_nn(a, w3, sec, tn, name):
    m, k = a.shape
    q, _, ws = w3.shape
    n = q * ws
    tpq, tps = ws // tn, sec // tn
    tm = 256 if m % 256 == 0 else 128

    def body(a_ref, w_ref, o_ref):
        w = w_ref[...]

        def step(i, carry):
            rows = pl.ds(pl.multiple_of(i * tm, tm), tm)
            o_ref[rows, :] = _dot(a_ref[rows, :], w)
            return carry

        lax.fori_loop(0, m // tm, step, 0)

    return pl.pallas_call(
        body, name=name, grid=(n // tn,),
        in_specs=[pl.BlockSpec((m, k), lambda j: (0, 0)),
                  pl.BlockSpec((None, k, tn), lambda j: (j // tpq, 0, j % tpq))],
        out_specs=pl.BlockSpec((None, m, tn), lambda j: (j // tps, 0, j % tps)),
        out_shape=jax.ShapeDtypeStruct((n // sec, m, sec), F32), compiler_params=_cp(),
    )(a, w3)


def mm_nt(a3, w3, tn, name):
    s, m, sec = a3.shape
    q, k, ws = w3.shape
    n = q * ws
    tpq, tps = ws // tn, sec // tn
    mb = _split_rows(m)

    def body(a_ref, w_ref, o_ref):
        @pl.when(pl.program_id(1) == 0)
        def _():
            o_ref[...] = jnp.zeros_like(o_ref)

        o_ref[...] += _dot(a_ref[...], w_ref[...], NT)

    return pl.pallas_call(
        body, name=name, grid=(m // mb, n // tn),
        in_specs=[pl.BlockSpec((None, mb, tn), lambda i, j: (j // tps, i, j % tps)),
                  pl.BlockSpec((None, k, tn), lambda i, j: (j // tpq, 0, j % tpq))],
        out_specs=pl.BlockSpec((mb, k), lambda i, j: (i, 0)),
        out_shape=jax.ShapeDtypeStruct((m, k), F32), compiler_params=_cp(),
    )(a3, w3)


def mm_tn(a, b3, ws, tn, name):
    m, k = a.shape
    s, _, sec = b3.shape
    n = s * sec
    tpq, tps = ws // tn, sec // tn
    kb = 256 if k % 256 == 0 else 128

    def body(a_ref, b_ref, o_ref):
        b = b_ref[...]
        for i in range(k // kb):
            o_ref[i * kb:(i + 1) * kb, :] = _dot(a_ref[:, i * kb:(i + 1) * kb], b, TN)

    return pl.pallas_call(
        body, name=name, grid=(n // tn,),
        in_specs=[pl.BlockSpec((m, k), lambda j: (0, 0)),
                  pl.BlockSpec((None, m, tn), lambda j: (j // tps, 0, j % tps))],
        out_specs=pl.BlockSpec((None, k, tn), lambda j: (j // tpq, 0, j % tpq)),
        out_shape=jax.ShapeDtypeStruct((n // ws, k, ws), F32), compiler_params=_cp(),
    )(a, b3)


def _pool_mask(gi, row0, tm, t, cx, seq, transposed):
    half = jnp.left_shift(1, gi)
    r = lax.broadcasted_iota(jnp.int32, (tm, 1), 0) + row0
    c = lax.broadcasted_iota(jnp.int32, (1, t), 1)
    out_tok, src_tok = (c, r) if transposed else (r, c)

    def parts(tok):
        lat = tok - cx
        return tok < cx, lat >> GRID_W_LOG2, lat & ((1 << GRID_W_LOG2) - 1)

    o_ctx, o_row, o_col = parts(out_tok)
    s_ctx, s_row, s_col = parts(src_tok)

    def inside(o, s):
        return (s >= o - half) & (s <= o + half - 1)

    ctx_hit = o_ctx & s_ctx & inside(out_tok, src_tok)
    lat_hit = (~o_ctx) & (~s_ctx) & inside(o_row, s_row) & inside(o_col, s_col)
    mask = jnp.where(ctx_hit | lat_hit, 1.0, 0.0).astype(BF16)

    own_ctx, own_row, own_col = parts(r)

    def count(pos, size):
        return jnp.minimum(pos + half - 1, size - 1) - jnp.maximum(pos - half, 0) + 1

    cnt = jnp.where(own_ctx, count(r, cx),
                    count(own_row, seq >> GRID_W_LOG2) * count(own_col, 1 << GRID_W_LOG2))
    return mask, 1.0 / cnt.astype(F32)


def mix_a_fwd(z0, pool_w, pool_scale, cx):
    _, t, half_d = z0.shape
    g = half_d // N_POOL
    seq = t - cx
    tm = _row_block(cx)

    def body(v_ref, ag_ref, w_ref, sc_ref, u_ref, vb_ref):
        gi = pl.program_id(0)
        vb_ref[...] = v_ref[...].astype(BF16)
        w = w_ref[...].astype(BF16)
        sc = sc_ref[...]

        def step(i, carry):
            row0 = pl.multiple_of(i * tm, tm)
            rows = pl.ds(row0, tm)
            mask, inv = _pool_mask(gi, row0, tm, t, cx, seq, False)
            pooled = _dot(mask, vb_ref[...]) * inv - v_ref[rows, :]
            mixed = _dot(pooled.astype(BF16), w) * sc
            u_ref[rows, :] = (mixed * _silu(ag_ref[rows, :])).astype(BF16)
            return carry

        lax.fori_loop(0, t // tm, step, 0)

    sec = lambda s: pl.BlockSpec((None, t, g), lambda j: (s, 0, j))
    return pl.pallas_call(
        body, name="mix_a_fwd", grid=(N_POOL,),
        in_specs=[sec(0), sec(1), pl.BlockSpec((None, g, g), lambda j: (j, 0, 0)),
                  pl.BlockSpec((1, g), lambda j: (0, j))],
        out_specs=pl.BlockSpec((t, g), lambda j: (0, j)),
        out_shape=jax.ShapeDtypeStruct((t, half_d), BF16),
        scratch_shapes=[pltpu.VMEM((t, g), BF16)], compiler_params=_cp(),
    )(z0, z0, pool_w, pool_scale)


def mix_a_bwd(z0, du, pool_w, pool_scale, cx):
    _, t, half_d = z0.shape
    g = half_d // N_POOL
    seq = t - cx
    tm = _row_block(cx)
    gq = g // 4

    def body(v_ref, ag_ref, du_ref, w_ref, sc_ref, dz_ref, dw_ref, dsc_ref,
             vb_ref, pooled_ref, dmx_ref, dpl_ref, wdp_ref):
        gi = pl.program_id(0)
        vb_ref[...] = v_ref[...].astype(BF16)
        w = w_ref[...].astype(BF16)
        sc = sc_ref[...]

        def first(i, dsc):
            row0 = pl.multiple_of(i * tm, tm)
            rows = pl.ds(row0, tm)
            mask, inv = _pool_mask(gi, row0, tm, t, cx, seq, False)
            pooled = (_dot(mask, vb_ref[...]) * inv - v_ref[rows, :]).astype(BF16)
            pooled_ref[rows, :] = pooled
            mixed = _dot(pooled, w)
            ag = ag_ref[rows, :]
            duv = du_ref[rows, :]
            dz_ref[1, rows, :] = (duv * (mixed * sc) * _dsilu(ag)).astype(BF16)
            dms = duv * _silu(ag)
            dmixed = (dms * sc).astype(BF16)
            dmx_ref[rows, :] = dmixed
            dpooled = _dot(dmixed, w, NT)
            dpl_ref[rows, :] = dpooled
            wdp_ref[rows, :] = (dpooled * inv).astype(BF16)
            return dsc + jnp.sum(dms * mixed, axis=0, keepdims=True)

        dsc_ref[...] = lax.fori_loop(0, t // tm, first, jnp.zeros((1, g), F32))
        dw = _dot(pooled_ref[...], dmx_ref[...], TN)
        for qi in range(4):
            dw_ref[qi] = dw[qi * gq:(qi + 1) * gq, :]

        def second(i, carry):
            row0 = pl.multiple_of(i * tm, tm)
            rows = pl.ds(row0, tm)
            mask_t, _ = _pool_mask(gi, row0, tm, t, cx, seq, True)
            dz_ref[0, rows, :] = (_dot(mask_t, wdp_ref[...]) - dpl_ref[rows, :]).astype(BF16)
            return carry

        lax.fori_loop(0, t // tm, second, 0)

    sec = lambda s: pl.BlockSpec((None, t, g), lambda j: (s, 0, j))
    return pl.pallas_call(
        body, name="mix_a_bwd", grid=(N_POOL,),
        in_specs=[sec(0), sec(1), pl.BlockSpec((t, g), lambda j: (0, j)),
                  pl.BlockSpec((None, g, g), lambda j: (j, 0, 0)),
                  pl.BlockSpec((1, g), lambda j: (0, j))],
        out_specs=[pl.BlockSpec((2, t, g), lambda j: (0, 0, j)),
                   pl.BlockSpec((4, None, gq, g), lambda j: (0, j, 0, 0)),
                   pl.BlockSpec((1, g), lambda j: (0, j))],
        out_shape=[jax.ShapeDtypeStruct((2, t, half_d), BF16),
                   jax.ShapeDtypeStruct((4, N_POOL, gq, g), F32),
                   jax.ShapeDtypeStruct((1, half_d), F32)],
        scratch_shapes=[pltpu.VMEM((t, g), BF16), pltpu.VMEM((t, g), BF16), pltpu.VMEM((t, g), BF16),
                        pltpu.VMEM((t, g), F32), pltpu.VMEM((t, g), BF16)],
        compiler_params=_cp(),
    )(z0, z0, du, pool_w, pool_scale)


def _conv_masks(t, cx):
    r = lax.broadcasted_iota(jnp.int32, (t, 1), 0)
    has_prev = jnp.where((r == 0) | (r == cx), 0.0, 1.0)
    has_next = jnp.where((r == cx - 1) | (r == t - 1), 0.0, 1.0)
    return has_prev, has_next


def mix_b_fwd(z0, conv_w, conv_b, cx):
    _, t, half_d = z0.shape
    gb = 128

    def body(bx_ref, bb_ref, bc_ref, bg_ref, w_ref, b_ref, u_ref):
        has_prev, has_next = _conv_masks(t, cx)
        tt = bc_ref[...] * bx_ref[...]
        prev = pltpu.roll(tt, 1, 0) * has_prev
        nxt = pltpu.roll(tt, t - 1, 0) * has_next
        cv = prev * w_ref[0:1, :] + tt * w_ref[1:2, :] + nxt * w_ref[2:3, :] + b_ref[...]
        u_ref[...] = (bb_ref[...] * cv * _silu(bg_ref[...])).astype(BF16)

    sec = lambda s: pl.BlockSpec((None, t, gb), lambda j: (s, 0, j))
    return pl.pallas_call(
        body, name="mix_b_fwd", grid=(half_d // gb,),
        in_specs=[sec(2), sec(3), sec(4), sec(5), pl.BlockSpec((3, gb), lambda j: (0, j)),
                  pl.BlockSpec((1, gb), lambda j: (0, j))],
        out_specs=pl.BlockSpec((t, gb), lambda j: (0, j)),
        out_shape=jax.ShapeDtypeStruct((t, half_d), BF16), compiler_params=_cp(),
    )(z0, z0, z0, z0, conv_w, conv_b)


def mix_b_bwd(z0, du, conv_w, conv_b, cx):
    _, t, half_d = z0.shape
    gb = 128
    off = half_d // gb

    def body(bx_ref, bb_ref, bc_ref, bg_ref, du_ref, w_ref, b_ref, dz_ref, dw_ref, db_ref):
        has_prev, has_next = _conv_masks(t, cx)
        bx, bb, bc, bg = bx_ref[...], bb_ref[...], bc_ref[...], bg_ref[...]
        duv = du_ref[...]
        tt = bc * bx
        prev = pltpu.roll(tt, 1, 0) * has_prev
        nxt = pltpu.roll(tt, t - 1, 0) * has_next
        w0, w1, w2 = w_ref[0:1, :], w_ref[1:2, :], w_ref[2:3, :]
        cv = prev * w0 + tt * w1 + nxt * w2 + b_ref[...]
        sg = _silu(bg)
        dz_ref[1] = (duv * cv * sg).astype(BF16)
        dz_ref[3] = (duv * bb * cv * _dsilu(bg)).astype(BF16)
        dcv = duv * bb * sg
        dw_ref[0:1, :] = jnp.sum(dcv * prev, axis=0, keepdims=True)
        dw_ref[1:2, :] = jnp.sum(dcv * tt, axis=0, keepdims=True)
        dw_ref[2:3, :] = jnp.sum(dcv * nxt, axis=0, keepdims=True)
        db_ref[...] = jnp.sum(dcv, axis=0, keepdims=True)
        dt = (pltpu.roll(dcv * has_prev, t - 1, 0) * w0 + dcv * w1
              + pltpu.roll(dcv * has_next, 1, 0) * w2)
        dz_ref[0] = (dt * bc).astype(BF16)
        dz_ref[2] = (dt * bx).astype(BF16)

    sec = lambda s: pl.BlockSpec((None, t, gb), lambda j: (s, 0, j))
    return pl.pallas_call(
        body, name="mix_b_bwd", grid=(half_d // gb,),
        in_specs=[sec(2), sec(3), sec(4), sec(5), pl.BlockSpec((t, gb), lambda j: (0, j + off)),
                  pl.BlockSpec((3, gb), lambda j: (0, j)), pl.BlockSpec((1, gb), lambda j: (0, j))],
        out_specs=[pl.BlockSpec((4, t, gb), lambda j: (0, 0, j)),
                   pl.BlockSpec((3, gb), lambda j: (0, j)), pl.BlockSpec((1, gb), lambda j: (0, j))],
        out_shape=[jax.ShapeDtypeStruct((4, t, half_d), BF16),
                   jax.ShapeDtypeStruct((3, half_d), F32), jax.ShapeDtypeStruct((1, half_d), F32)],
        compiler_params=_cp(),
    )(z0, z0, z0, z0, du, conv_w, conv_b)


def _lower_bound(lbl_ref, d):
    l0, l1, l2 = lbl_ref[d, 0:1, :], lbl_ref[d, 1:2, :], lbl_ref[d, 2:3, :]
    mx = jnp.maximum(jnp.maximum(l0, l1), l2)
    e0, e1, e2 = jnp.exp(l0 - mx), jnp.exp(l1 - mx), jnp.exp(l2 - mx)
    inv = 1.0 / (e0 + e1 + e2)
    return (e0 + e1) * inv, (e0 * inv, e1 * inv, e2 * inv)


def _chunk_consts(d):
    r = lax.broadcasted_iota(jnp.int32, (CHUNK, CHUNK), 0)
    c = lax.broadcasted_iota(jnp.int32, (CHUNK, CHUNK), 1)
    keep = (c <= r) if d == 0 else (c >= r)
    return jnp.where(keep, 1.0, 0.0).astype(F32), keep


def _chunk_of_step(s, d, nc, ncc):
    if d == 0:
        return s
    return jnp.where(s < ncc, ncc - 1 - s, nc - 1 + ncc - s)


def _chunk_terms(lfc, kc, qc, cum):
    bc = _dot(cum, lfc, precision=lax.Precision.HIGHEST)
    bl = jnp.sum(lfc, axis=0, keepdims=True)
    e = jnp.exp(bc)
    einv = jnp.exp(-bc)
    erem = jnp.exp(bl - bc)
    return e, einv, erem, jnp.exp(bl), qc * e, kc * einv, kc * erem


def hgrn_fwd(z1, lbl, onorm, cx):
    _, t, d = z1.shape
    seq = t - cx
    nc, ncc = t // CHUNK, cx // CHUNK

    def body(zf_ref, zb_ref, v_ref, q_ref, g_ref, lbl_ref, on_ref, o_ref, r_ref,
             lf_ref, k_ref, oacc_ref, st_ref):
        for dr, z_ref in ((0, zf_ref), (1, zb_ref)):
            lbv, _ = _lower_bound(lbl_ref, dr)
            z = z_ref[...]
            lf_ref[...] = jnp.log(lbv + (1.0 - lbv) * jax.nn.sigmoid(z))
            k_ref[...] = (1.0 - lbv) * jax.nn.sigmoid(-z)
            st_ref[...] = jnp.zeros_like(st_ref)
            cum, keep = _chunk_consts(dr)

            def step(s, carry, dr=dr, cum=cum, keep=keep):
                n = _chunk_of_step(s, dr, nc, ncc)
                rows = pl.ds(pl.multiple_of(n * CHUNK, CHUNK), CHUNK)
                vc = v_ref[rows, :].astype(BF16)
                _, _, _, dec, qd, ki, kd = _chunk_terms(lf_ref[rows, :], k_ref[rows, :], q_ref[rows, :], cum)
                qdb = qd.astype(BF16)
                a = jnp.where(keep, _dot(qdb, ki.astype(BF16), NT), 0.0)
                st = st_ref[...]
                oc = _dot(qdb, st.astype(BF16), NT) + _dot(a.astype(BF16), vc)
                st_ref[...] = st * dec + _dot(vc, kd.astype(BF16), TN)
                if dr == 0:
                    oacc_ref[rows, :] = oc
                else:
                    oacc_ref[rows, :] += oc
                return carry

            lax.fori_loop(0, nc, step, 0, unroll=4)

        o = oacc_ref[cx:, :]
        o_ref[...] = o
        rstd = lax.rsqrt(jnp.mean(o * o, axis=-1, keepdims=True) + EPS)
        r_ref[...] = (o * rstd * on_ref[...] * _silu(g_ref[cx:, :])).astype(BF16)

    sec = lambda s: pl.BlockSpec((None, t, HEAD), lambda h: (s, 0, h))
    col = pl.BlockSpec((seq, HEAD), lambda h: (0, h))
    return pl.pallas_call(
        body, name="hgrn_fwd", grid=(d // HEAD,),
        in_specs=[sec(0), sec(1), sec(2), sec(3), sec(4),
                  pl.BlockSpec((2, 3, HEAD), lambda h: (0, 0, h)), pl.BlockSpec((1, HEAD), lambda h: (0, h))],
        out_specs=[col, col],
        out_shape=[jax.ShapeDtypeStruct((seq, d), F32), jax.ShapeDtypeStruct((seq, d), BF16)],
        scratch_shapes=[pltpu.VMEM((t, HEAD), F32), pltpu.VMEM((t, HEAD), F32), pltpu.VMEM((t, HEAD), F32),
                        pltpu.VMEM((HEAD, HEAD), F32)],
        compiler_params=_cp(),
    )(z1, z1, z1, z1, z1, lbl, onorm)


def hgrn_bwd(z1, lbl, onorm, o, dr_out, cx):
    _, t, d = z1.shape
    seq = t - cx
    nc, ncc = t // CHUNK, cx // CHUNK

    def body(zf_ref, zb_ref, v_ref, q_ref, g_ref, lbl_ref, on_ref, o_ref, dr_ref,
             dz_ref, don_ref, dlb_ref,
             lf_ref, k_ref, do_ref, dq_ref, dv_ref, dk_ref, dlf_ref, ssc_ref, dst_ref):
        o = o_ref[...]
        g = g_ref[cx:, :]
        drv = dr_ref[...]
        onv = on_ref[...]
        rstd = lax.rsqrt(jnp.mean(o * o, axis=-1, keepdims=True) + EPS)
        ohat = o * rstd
        sg = _silu(g)
        don_ref[...] = jnp.sum(drv * ohat * sg, axis=0, keepdims=True)
        dz_ref[4, :cx, :] = jnp.zeros((cx, HEAD), BF16)
        dz_ref[4, cx:, :] = (drv * ohat * onv * _dsilu(g)).astype(BF16)
        dohat = drv * onv * sg
        do_ref[:cx, :] = jnp.zeros((cx, HEAD), F32)
        do_ref[cx:, :] = rstd * (dohat - ohat * jnp.mean(dohat * ohat, axis=-1, keepdims=True))

        for dr, z_ref in ((0, zf_ref), (1, zb_ref)):
            lbv, _ = _lower_bound(lbl_ref, dr)
            z = z_ref[...]
            lf_ref[...] = jnp.log(lbv + (1.0 - lbv) * jax.nn.sigmoid(z))
            k_ref[...] = (1.0 - lbv) * jax.nn.sigmoid(-z)
            cum, keep = _chunk_consts(dr)
            cum_t, _ = _chunk_consts(1 - dr)

            st_init = jnp.zeros((HEAD, HEAD), F32)

            def state_step(s, st, dr=dr, cum=cum):
                n = _chunk_of_step(s, dr, nc, ncc)
                rows = pl.ds(pl.multiple_of(n * CHUNK, CHUNK), CHUNK)
                ssc_ref[n] = st
                _, _, _, dec, _, _, kd = _chunk_terms(lf_ref[rows, :], k_ref[rows, :], q_ref[rows, :], cum)
                return st * dec + _dot(v_ref[rows, :].astype(BF16), kd.astype(BF16), TN)

            lax.fori_loop(0, nc, state_step, st_init, unroll=4)
            dst_ref[...] = jnp.zeros_like(dst_ref)

            def grad_step(s2, carry, dr=dr, cum=cum, cum_t=cum_t, keep=keep):
                n = _chunk_of_step(nc - 1 - s2, dr, nc, ncc)
                rows = pl.ds(pl.multiple_of(n * CHUNK, CHUNK), CHUNK)
                vc = v_ref[rows, :].astype(BF16)
                e, einv, erem, dec, qd, ki, kd = _chunk_terms(
                    lf_ref[rows, :], k_ref[rows, :], q_ref[rows, :], cum)
                qdb, kib, kdb = qd.astype(BF16), ki.astype(BF16), kd.astype(BF16)
                doc = do_ref[rows, :].astype(BF16)
                st0 = ssc_ref[n]
                dst = dst_ref[...]
                dstb = dst.astype(BF16)
                a = jnp.where(keep, _dot(qdb, kib, NT), 0.0).astype(BF16)
                da = jnp.where(keep, _dot(doc, vc, NT), 0.0).astype(BF16)
                dqd = _dot(doc, st0.astype(BF16)) + _dot(da, kib)
                dki = _dot(da, qdb, TN)
                dv = _dot(a, doc, TN) + _dot(kdb, dstb, NT)
                dkd = _dot(vc, dstb)
                ddec = jnp.sum(dst * st0, axis=0, keepdims=True)
                dst_ref[...] = _dot(doc, qdb, TN) + dst * dec
                dbc = dqd * qd - dki * ki - dkd * kd
                dbl = jnp.sum(dkd * kd, axis=0, keepdims=True) + ddec * dec
                dlf_ref[rows, :] = _dot(cum_t, dbc, precision=lax.Precision.HIGHEST) + dbl
                dk_ref[rows, :] = dki * einv + dkd * erem
                if dr == 0:
                    dq_ref[rows, :] = dqd * e
                    dv_ref[rows, :] = dv
                else:
                    dq_ref[rows, :] += dqd * e
                    dv_ref[rows, :] += dv
                return carry

            lax.fori_loop(0, nc, grad_step, 0, unroll=2)

            sig = jax.nn.sigmoid(z)
            one_lb = 1.0 - lbv
            f = lbv + one_lb * sig
            dlf = dlf_ref[...]
            dk = dk_ref[...]
            dsig = (dlf / f - dk) * one_lb
            dz_ref[dr] = (dsig * sig * (1.0 - sig)).astype(BF16)
            dlb_ref[dr:dr + 1, :] = jnp.sum((dlf / f - dk) * (1.0 - sig), axis=0, keepdims=True)

        dz_ref[2] = dv_ref[...].astype(BF16)
        dz_ref[3] = dq_ref[...].astype(BF16)

    sec = lambda s: pl.BlockSpec((None, t, HEAD), lambda h: (s, 0, h))
    col = pl.BlockSpec((seq, HEAD), lambda h: (0, h))
    tvec = pltpu.VMEM((t, HEAD), F32)
    return pl.pallas_call(
        body, name="hgrn_bwd", grid=(d // HEAD,),
        in_specs=[sec(0), sec(1), sec(2), sec(3), sec(4),
                  pl.BlockSpec((2, 3, HEAD), lambda h: (0, 0, h)), pl.BlockSpec((1, HEAD), lambda h: (0, h)),
                  col, col],
        out_specs=[pl.BlockSpec((5, t, HEAD), lambda h: (0, 0, h)),
                   pl.BlockSpec((1, HEAD), lambda h: (0, h)), pl.BlockSpec((2, HEAD), lambda h: (0, h))],
        out_shape=[jax.ShapeDtypeStruct((5, t, d), BF16), jax.ShapeDtypeStruct((1, d), F32),
                   jax.ShapeDtypeStruct((2, d), F32)],
        scratch_shapes=[tvec, tvec, tvec, tvec, tvec, tvec, tvec,
                        pltpu.VMEM((nc, HEAD, HEAD), F32), pltpu.VMEM((HEAD, HEAD), F32)],
        compiler_params=_cp(),
    )(z1, z1, z1, z1, z1, lbl, onorm, o, dr_out)


def _gates(z, lbv):
    e = jnp.exp(-jnp.abs(z))
    r = 1.0 / (1.0 + e)
    er = e * r
    pos = z >= 0.0
    sig = jnp.where(pos, r, er)
    nsig = jnp.where(pos, er, r)
    return sig, nsig, lbv + (1.0 - lbv) * sig


def _split3(x):
    hi = x.astype(BF16)
    r1 = x - hi.astype(F32)
    mid = r1.astype(BF16)
    lo = (r1 - mid.astype(F32)).astype(BF16)
    return jnp.concatenate([hi, mid, lo], axis=1)


def _cumsum_chunk(cum, x):
    y = _dot(cum, _split3(x))
    return y[:, :HEAD] + y[:, HEAD:2 * HEAD] + y[:, 2 * HEAD:]


def _chunk_rows(n):
    return pl.ds(pl.multiple_of(n * CHUNK, CHUNK), CHUNK)


def _group(nc):
    return next(u for u in (4, 3, 2, 1) if nc % u == 0)


def _decay_pass(lf_ref, bc_ref, dec_ref, cum, nc):
    grp = _group(nc)

    def step(m, carry):
        ns = [m * grp + u for u in range(grp)]
        lfc = [lf_ref[_chunk_rows(n), :] for n in ns]
        bc = [_cumsum_chunk(cum, x) for x in lfc]
        for u, n in enumerate(ns):
            bc_ref[_chunk_rows(n), :] = bc[u]
            dec_ref[n] = jnp.broadcast_to(jnp.exp(jnp.sum(lfc[u], axis=0, keepdims=True)), (8, HEAD))
        return carry

    lax.fori_loop(0, nc // grp, step, 0)


def hgrn_fwd(z1, lbl, onorm, cx):
    _, t, d = z1.shape
    seq = t - cx
    nc, ncc = t // CHUNK, cx // CHUNK

    grp = _group(nc)

    def body(zf_ref, zb_ref, v_ref, q_ref, g_ref, lbl_ref, on_ref, o_ref, r_ref,
             lf_ref, k_ref, bc_ref, dec_ref, qd_ref, ki_ref, oacc_ref, ds_ref):
        for dr, z_ref in ((0, zf_ref), (1, zb_ref)):
            lbv, _ = _lower_bound(lbl_ref, dr)
            _, nsig, f = _gates(z_ref[...], lbv)
            lf_ref[...] = jnp.log(f)
            k_ref[...] = (1.0 - lbv) * nsig
            cum, keep = _chunk_consts(dr)
            _decay_pass(lf_ref, bc_ref, dec_ref, cum.astype(BF16), nc)
            bc = bc_ref[...]
            qd_ref[...] = (q_ref[...] * jnp.exp(bc)).astype(BF16)
            ki_ref[...] = (k_ref[...] * jnp.exp(-bc)).astype(BF16)

            def local_step(m, carry, dr=dr, keep=keep):
                ns = [m * grp + u for u in range(grp)]
                rows = [_chunk_rows(n) for n in ns]
                qd = [qd_ref[r, :] for r in rows]
                ki = [ki_ref[r, :] for r in rows]
                vc = [v_ref[r, :].astype(BF16) for r in rows]
                sc = [_dot(qd[u], ki[u], NT) for u in range(grp)]
                inc = [_dot(vc[u], ki[u], TN) for u in range(grp)]
                a = [jnp.where(keep, s, 0.0).astype(BF16) for s in sc]
                intra = [_dot(a[u], vc[u]) for u in range(grp)]
                for u in range(grp):
                    ds_ref[ns[u]] = inc[u] * dec_ref[ns[u]][0:1, :]
                    if dr == 0:
                        oacc_ref[rows[u], :] = intra[u]
                    else:
                        oacc_ref[rows[u], :] += intra[u]
                return carry

            lax.fori_loop(0, nc // grp, local_step, 0)

            def state_step(m, st, dr=dr):
                ns = [_chunk_of_step(m * grp + u, dr, nc, ncc) for u in range(grp)]
                rows = [_chunk_rows(n) for n in ns]
                sts = []
                for n in ns:
                    sts.append(st.astype(BF16))
                    st = st * dec_ref[n][0:1, :] + ds_ref[n]
                inter = [_dot(qd_ref[rows[u], :], sts[u], NT) for u in range(grp)]
                for u in range(grp):
                    oacc_ref[rows[u], :] += inter[u]
                return st

            lax.fori_loop(0, nc // grp, state_step, jnp.zeros((HEAD, HEAD), F32))

        o = oacc_ref[cx:, :]
        o_ref[...] = o
        rstd = lax.rsqrt(jnp.mean(o * o, axis=-1, keepdims=True) + EPS)
        r_ref[...] = (o * rstd * on_ref[...] * _silu(g_ref[cx:, :])).astype(BF16)

    sec = lambda s: pl.BlockSpec((None, t, HEAD), lambda h: (s, 0, h))
    col = pl.BlockSpec((seq, HEAD), lambda h: (0, h))
    tf32, tb16 = pltpu.VMEM((t, HEAD), F32), pltpu.VMEM((t, HEAD), BF16)
    return pl.pallas_call(
        body, name="hgrn_fwd", grid=(d // HEAD,),
        in_specs=[sec(0), sec(1), sec(2), sec(3), sec(4),
                  pl.BlockSpec((2, 3, HEAD), lambda h: (0, 0, h)), pl.BlockSpec((1, HEAD), lambda h: (0, h))],
        out_specs=[col, col],
        out_shape=[jax.ShapeDtypeStruct((seq, d), F32), jax.ShapeDtypeStruct((seq, d), BF16)],
        scratch_shapes=[tf32, tf32, tf32, pltpu.VMEM((nc, 8, HEAD), F32), tb16, tb16, tf32,
                        pltpu.VMEM((nc, HEAD, HEAD), F32)],
        compiler_params=_cp(),
    )(z1, z1, z1, z1, z1, lbl, onorm)


def hgrn_bwd(z1, lbl, onorm, o, dr_out, cx):
    _, t, d = z1.shape
    seq = t - cx
    nc, ncc = t // CHUNK, cx // CHUNK

    grp2 = 2 if nc % 2 == 0 else 1
    grp = grp2

    def body(zf_ref, zb_ref, v_ref, q_ref, g_ref, lbl_ref, on_ref, o_ref, dr_ref,
             dz_ref, don_ref, dlb_ref,
             lf_ref, k_ref, bc_ref, dec_ref, qd_ref, ki_ref, do_ref,
             dqd_ref, dki_ref, dq_ref, dv_ref, ds_ref, dsl_ref):
        o = o_ref[...]
        g = g_ref[cx:, :]
        drv = dr_ref[...]
        onv = on_ref[...]
        rstd = lax.rsqrt(jnp.mean(o * o, axis=-1, keepdims=True) + EPS)
        ohat = o * rstd
        sg = _silu(g)
        don_ref[...] = jnp.sum(drv * ohat * sg, axis=0, keepdims=True)
        dz_ref[4, :cx, :] = jnp.zeros((cx, HEAD), BF16)
        dz_ref[4, cx:, :] = (drv * ohat * onv * _dsilu(g)).astype(BF16)
        dohat = drv * onv * sg
        do_ref[:cx, :] = jnp.zeros((cx, HEAD), BF16)
        do_ref[cx:, :] = (rstd * (dohat - ohat * jnp.mean(dohat * ohat, axis=-1, keepdims=True))).astype(BF16)

        for dr, z_ref in ((0, zf_ref), (1, zb_ref)):
            lbv, _ = _lower_bound(lbl_ref, dr)
            _, nsig, f = _gates(z_ref[...], lbv)
            lf_ref[...] = jnp.log(f)
            k_ref[...] = (1.0 - lbv) * nsig
            cum, keep = _chunk_consts(dr)
            cum_t = _chunk_consts(1 - dr)[0].astype(BF16)
            _decay_pass(lf_ref, bc_ref, dec_ref, cum.astype(BF16), nc)
            bc = bc_ref[...]
            qd_ref[...] = (q_ref[...] * jnp.exp(bc)).astype(BF16)
            ki_ref[...] = (k_ref[...] * jnp.exp(-bc)).astype(BF16)

            def local_step(m, carry, dr=dr, keep=keep):
                ns = [m * grp + u for u in range(grp)]
                rows = [_chunk_rows(n) for n in ns]
                rng = range(grp)
                qd = [qd_ref[r, :] for r in rows]
                ki = [ki_ref[r, :] for r in rows]
                doc = [do_ref[r, :] for r in rows]
                vc = [v_ref[r, :].astype(BF16) for r in rows]
                sc = [_dot(qd[u], ki[u], NT) for u in rng]
                dsc = [_dot(doc[u], vc[u], NT) for u in rng]
                inc = [_dot(vc[u], ki[u], TN) for u in rng]
                dinc = [_dot(doc[u], qd[u], TN) for u in rng]
                a = [jnp.where(keep, s, 0.0).astype(BF16) for s in sc]
                da = [jnp.where(keep, s, 0.0).astype(BF16) for s in dsc]
                dqd = [_dot(da[u], ki[u]) for u in rng]
                dki = [_dot(da[u], qd[u], TN) for u in rng]
                dv = [_dot(a[u], doc[u], TN) for u in rng]
                for u in rng:
                    ds_ref[ns[u]] = inc[u] * dec_ref[ns[u]][0:1, :]
                    dsl_ref[ns[u]] = dinc[u]
                    dqd_ref[rows[u], :] = dqd[u]
                    dki_ref[rows[u], :] = dki[u]
                    if dr == 0:
                        dv_ref[rows[u], :] = dv[u]
                    else:
                        dv_ref[rows[u], :] += dv[u]
                return carry

            lax.fori_loop(0, nc // grp, local_step, 0)

            def state_step(s, st, dr=dr):
                n = _chunk_of_step(s, dr, nc, ncc)
                inc = ds_ref[n]
                ds_ref[n] = st
                return st * dec_ref[n][0:1, :] + inc

            lax.fori_loop(0, nc, state_step, jnp.zeros((HEAD, HEAD), F32), unroll=4)

            def dstate_step(s, dst, dr=dr):
                n = _chunk_of_step(nc - 1 - s, dr, nc, ncc)
                inc = dsl_ref[n]
                dsl_ref[n] = dst
                return inc + dst * dec_ref[n][0:1, :]

            lax.fori_loop(0, nc, dstate_step, jnp.zeros((HEAD, HEAD), F32), unroll=4)

            def grad_step(m, carry, dr=dr, cum_t=cum_t):
                ns = [m * grp2 + u for u in range(grp2)]
                rows = [_chunk_rows(n) for n in ns]
                rng = range(grp2)
                st0 = [ds_ref[n] for n in ns]
                dst = [dsl_ref[n] for n in ns]
                dstb = [x.astype(BF16) for x in dst]
                dec = [dec_ref[n][0:1, :] for n in ns]
                doc = [do_ref[r, :] for r in rows]
                vc = [v_ref[r, :].astype(BF16) for r in rows]
                e = [jnp.exp(bc_ref[r, :]) for r in rows]
                einv = [jnp.exp(-bc_ref[r, :]) for r in rows]
                qd = [q_ref[rows[u], :] * e[u] for u in rng]
                ki = [k_ref[rows[u], :] * einv[u] for u in rng]
                kd = [ki[u] * dec[u] for u in rng]
                dqd_st = [_dot(doc[u], st0[u].astype(BF16)) for u in rng]
                dkd = [_dot(vc[u], dstb[u]) for u in rng]
                dv_st = [_dot(kd[u].astype(BF16), dstb[u], NT) for u in rng]
                dqd = [dqd_ref[rows[u], :] + dqd_st[u] for u in rng]
                dki = [dki_ref[r, :] for r in rows]
                dbc = [dqd[u] * qd[u] - dki[u] * ki[u] - dkd[u] * kd[u] for u in rng]
                cs = [_cumsum_chunk(cum_t, x) for x in dbc]
                for u in rng:
                    ddec = jnp.sum(dst[u] * st0[u], axis=0, keepdims=True)
                    dbl = jnp.sum(dkd[u] * kd[u], axis=0, keepdims=True) + ddec * dec[u]
                    dv_ref[rows[u], :] += dv_st[u]
                    dqd_ref[rows[u], :] = cs[u] + dbl
                    dki_ref[rows[u], :] = dki[u] * einv[u] + dkd[u] * (einv[u] * dec[u])
                    if dr == 0:
                        dq_ref[rows[u], :] = dqd[u] * e[u]
                    else:
                        dq_ref[rows[u], :] += dqd[u] * e[u]
                return carry

            lax.fori_loop(0, nc // grp2, grad_step, 0)

            sig, nsig, f = _gates(z_ref[...], lbv)
            common = (dqd_ref[...] / f - dki_ref[...]) * nsig
            dz_ref[dr] = (common * ((1.0 - lbv) * sig)).astype(BF16)
            dlb_ref[dr:dr + 1, :] = jnp.sum(common, axis=0, keepdims=True)

        dz_ref[2] = dv_ref[...].astype(BF16)
        dz_ref[3] = dq_ref[...].astype(BF16)

    sec = lambda s: pl.BlockSpec((None, t, HEAD), lambda h: (s, 0, h))
    col = pl.BlockSpec((seq, HEAD), lambda h: (0, h))
    tf32, tb16 = pltpu.VMEM((t, HEAD), F32), pltpu.VMEM((t, HEAD), BF16)
    states = pltpu.VMEM((nc, HEAD, HEAD), F32)
    return pl.pallas_call(
        body, name="hgrn_bwd", grid=(d // HEAD,),
        in_specs=[sec(0), sec(1), sec(2), sec(3), sec(4),
                  pl.BlockSpec((2, 3, HEAD), lambda h: (0, 0, h)), pl.BlockSpec((1, HEAD), lambda h: (0, h)),
                  col, col],
        out_specs=[pl.BlockSpec((5, t, HEAD), lambda h: (0, 0, h)),
                   pl.BlockSpec((1, HEAD), lambda h: (0, h)), pl.BlockSpec((2, HEAD), lambda h: (0, h))],
        out_shape=[jax.ShapeDtypeStruct((5, t, d), BF16), jax.ShapeDtypeStruct((1, d), F32),
                   jax.ShapeDtypeStruct((2, d), F32)],
        scratch_shapes=[tf32, tf32, tf32, pltpu.VMEM((nc, 8, HEAD), F32), tb16, tb16, tb16,
                        tf32, tf32, tf32, tf32, states, states],
        compiler_params=_cp(),
    )(z1, z1, z1, z1, z1, lbl, onorm, o, dr_out)


def _place():
    x, y, c = lax.axis_index("x"), lax.axis_index("y"), lax.axis_index("c")
    chips = [(1 - x, y), (x, 1 - y), (1 - x, 1 - y)]
    return x, y, c, chips


def allgather_shards(bufs):
    n = len(bufs)

    def body(*refs):
        outs = refs[n:2 * n]
        send_sems, recv_sems = refs[2 * n:]
        x, y, c, chips = _place()
        p = 2 * x + y
        half = [pl.ds(c * (s.shape[1] // 2), s.shape[1] // 2) for s in bufs]
        other = [pl.ds((1 - c) * (s.shape[1] // 2), s.shape[1] // 2) for s in bufs]

        def remote(i, k, src, dst, to):
            return pltpu.make_async_remote_copy(src_ref=src, dst_ref=dst, send_sem=send_sems.at[6 * i + k],
                                                recv_sem=recv_sems.at[6 * i + k], device_id=to, device_id_type=MESH)

        sends = []
        for i in range(n):
            for j, chip in enumerate(chips):
                mine = outs[i].at[p, half[i]]
                cp = remote(i, j, mine, mine, (*chip, c))
                cp.start()
                sends.append(cp)
        for i in range(n):
            for j, chip in enumerate(chips):
                landed = outs[i].at[2 * chip[0] + chip[1], half[i]]
                remote(i, j, landed, landed, (x, y, c)).wait_recv()
                cp = remote(i, 3 + j, landed, landed, (x, y, 1 - c))
                cp.start()
                sends.append(cp)
        for i in range(n):
            for j, chip in enumerate(chips):
                landed = outs[i].at[2 * chip[0] + chip[1], other[i]]
                remote(i, 3 + j, landed, landed, (x, y, c)).wait_recv()
        for cp in sends:
            cp.wait_send()

    return pl.pallas_call(
        body, name="allgather_shards",
        in_specs=[ANY] * n, out_specs=[ANY] * n,
        out_shape=[jax.ShapeDtypeStruct(s.shape, s.dtype) for s in bufs],
        input_output_aliases={i: i for i in range(n)},
        scratch_shapes=[pltpu.SemaphoreType.DMA((6 * n,)), pltpu.SemaphoreType.DMA((6 * n,))],
        compiler_params=pltpu.CompilerParams(has_side_effects=True),
    )(*bufs)


def exchange_halves(grads):
    n = len(grads)

    def body(*refs):
        ins, outs = refs[:n], refs[n:2 * n]
        send_sems, recv_sems = refs[2 * n:]
        x, y, c, _ = _place()
        copies = []
        for i in range(n):
            hr = grads[i].shape[1] // 2
            cp = pltpu.make_async_remote_copy(
                src_ref=ins[i].at[:, pl.ds((1 - c) * hr, hr)], dst_ref=outs[i],
                send_sem=send_sems.at[i], recv_sem=recv_sems.at[i],
                device_id=(x, y, 1 - c), device_id_type=MESH)
            cp.start()
            copies.append(cp)
        for cp in copies:
            cp.wait()

    return pl.pallas_call(
        body, name="exchange_halves",
        in_specs=[ANY] * n, out_specs=[ANY] * n,
        out_shape=[jax.ShapeDtypeStruct((4, g.shape[1] // 2, g.shape[2]), g.dtype) for g in grads],
        scratch_shapes=[pltpu.SemaphoreType.DMA((n,)), pltpu.SemaphoreType.DMA((n,))],
        compiler_params=pltpu.CompilerParams(has_side_effects=True),
    )(*grads)


def pair_sum(grad, got, core):
    _, r, cc = grad.shape
    hr = r // 2
    tr = 256 if hr % 256 == 0 else hr

    def body(core_ref, a_ref, b_ref, s_ref, sb_ref):
        s = a_ref[...] + b_ref[...]
        s_ref[...] = s
        sb_ref[...] = s.astype(BF16)

    nb = hr // tr
    grid_spec = pltpu.PrefetchScalarGridSpec(
        num_scalar_prefetch=1, grid=(4, nb),
        in_specs=[pl.BlockSpec((None, tr, cc), lambda qi, i, core_ref: (qi, core_ref[0] * nb + i, 0)),
                  pl.BlockSpec((None, tr, cc), lambda qi, i, core_ref: (qi, i, 0))],
        out_specs=[pl.BlockSpec((None, tr, cc), lambda qi, i, core_ref: (qi, i, 0)),
                   pl.BlockSpec((None, tr, cc), lambda qi, i, core_ref: (qi, i, 0))])
    return pl.pallas_call(
        body, name="pair_sum", grid_spec=grid_spec,
        out_shape=[jax.ShapeDtypeStruct((4, hr, cc), F32), jax.ShapeDtypeStruct((4, hr, cc), BF16)],
        compiler_params=_cp(),
    )(core, grad, got)


def scatter_to_owners(parts):
    n = len(parts)

    def body(*refs):
        ins, outs = refs[:n], refs[n:2 * n]
        send_sems, recv_sems = refs[2 * n:]
        x, y, c, chips = _place()
        copies = []
        for i in range(n):
            for j, chip in enumerate(chips):
                cp = pltpu.make_async_remote_copy(
                    src_ref=ins[i].at[2 * chip[0] + chip[1]], dst_ref=outs[i].at[j],
                    send_sem=send_sems.at[3 * i + j], recv_sem=recv_sems.at[3 * i + j],
                    device_id=(*chip, c), device_id_type=MESH)
                cp.start()
                copies.append(cp)
        for cp in copies:
            cp.wait()

    return pl.pallas_call(
        body, name="scatter_to_owners",
        in_specs=[ANY] * n, out_specs=[ANY] * n,
        out_shape=[jax.ShapeDtypeStruct((3,) + p.shape[1:], p.dtype) for p in parts],
        scratch_shapes=[pltpu.SemaphoreType.DMA((3 * n,)), pltpu.SemaphoreType.DMA((3 * n,))],
        compiler_params=pltpu.CompilerParams(has_side_effects=True),
    )(*parts)


def owner_sum(own, got, chip_core):
    _, hr, cc = own.shape
    tr = 256 if hr % 256 == 0 else hr
    nb = hr // tr

    def body(cc_ref, a_ref, b_ref, o_ref):
        s = a_ref[...] + b_ref[0].astype(F32)
        s = s + b_ref[1].astype(F32)
        o_ref[...] = s + b_ref[2].astype(F32)

    grid_spec = pltpu.PrefetchScalarGridSpec(
        num_scalar_prefetch=1, grid=(nb,),
        in_specs=[pl.BlockSpec((None, tr, cc), lambda i, cc_ref: (cc_ref[0], i, 0)),
                  pl.BlockSpec((3, tr, cc), lambda i, cc_ref: (0, i, 0))],
        out_specs=pl.BlockSpec((tr, cc), lambda i, cc_ref: (cc_ref[1] * nb + i, 0)))
    return pl.pallas_call(
        body, name="owner_sum", grid_spec=grid_spec,
        out_shape=jax.ShapeDtypeStruct((2 * hr, cc), F32), compiler_params=_cp(),
    )(chip_core, own, got)


def share_halves(bufs):
    n = len(bufs)

    def body(*refs):
        outs = refs[n:2 * n]
        send_sems, recv_sems = refs[2 * n:]
        x, y, c, _ = _place()
        copies = []
        for i in range(n):
            hr = bufs[i].shape[0] // 2
            mine = outs[i].at[pl.ds(c * hr, hr)]
            cp = pltpu.make_async_remote_copy(
                src_ref=mine, dst_ref=mine, send_sem=send_sems.at[i], recv_sem=recv_sems.at[i],
                device_id=(x, y, 1 - c), device_id_type=MESH)
            cp.start()
            copies.append((cp, outs[i].at[pl.ds((1 - c) * hr, hr)]))
        for i, (cp, theirs) in enumerate(copies):
            cp.wait_send()
            pltpu.make_async_remote_copy(
                src_ref=theirs, dst_ref=theirs, send_sem=send_sems.at[i], recv_sem=recv_sems.at[i],
                device_id=(x, y, c), device_id_type=MESH).wait_recv()

    return pl.pallas_call(
        body, name="share_halves",
        in_specs=[ANY] * n, out_specs=[ANY] * n,
        out_shape=[jax.ShapeDtypeStruct(b.shape, b.dtype) for b in bufs],
        input_output_aliases={i: i for i in range(n)},
        scratch_shapes=[pltpu.SemaphoreType.DMA((n,)), pltpu.SemaphoreType.DMA((n,))],
        compiler_params=pltpu.CompilerParams(has_side_effects=True),
    )(*bufs)


def allgather8(v, name):
    r, n = v.shape

    def body(v_ref, out_ref, send_sems, recv_sems):
        x, y, c, _ = _place()
        me = 4 * x + 2 * y + c
        out_ref[me] = v_ref[...]

        def copy(k, slot, to):
            return pltpu.make_async_remote_copy(
                src_ref=v_ref, dst_ref=out_ref.at[slot], send_sem=send_sems.at[k - 1],
                recv_sem=recv_sems.at[k - 1], device_id=to, device_id_type=MESH)

        peers = []
        for k in range(1, 8):
            px = 1 - x if (k >> 2) & 1 else x
            py = 1 - y if (k >> 1) & 1 else y
            pc = 1 - c if k & 1 else c
            peers.append((px, py, pc))
            copy(k, me, (px, py, pc)).start()
        for k, (px, py, pc) in enumerate(peers, start=1):
            copy(k, 4 * px + 2 * py + pc, (x, y, c)).wait_recv()
        for k, peer in enumerate(peers, start=1):
            copy(k, me, peer).wait_send()

    return pl.pallas_call(
        body, name=name, in_specs=[VMEM], out_specs=VMEM,
        out_shape=jax.ShapeDtypeStruct((8, r, n), v.dtype),
        scratch_shapes=[pltpu.SemaphoreType.DMA((7,)), pltpu.SemaphoreType.DMA((7,))],
        compiler_params=_cp(has_side_effects=True),
    )(v)


HBM = pl.BlockSpec(memory_space=pltpu.HBM)
SEM = pl.BlockSpec(memory_space=pltpu.SEMAPHORE)
DATAFLOW = pltpu.SideEffectType.DATAFLOW_SIDE_EFFECTING


def _descriptors(plan, refs, send_sems, recv_sems):
    x, y, c, _ = _place()
    sends, recvs = plan(refs)
    out = [pltpu.make_async_remote_copy(src_ref=src, dst_ref=dst, send_sem=send_sems.at[k],
                                        recv_sem=recv_sems.at[k], device_id=to, device_id_type=MESH)
           for k, (src, dst, to) in enumerate(sends)]
    inn = [pltpu.make_async_remote_copy(src_ref=land, dst_ref=land, send_sem=send_sems.at[k],
                                        recv_sem=recv_sems.at[k], device_id=(x, y, c), device_id_type=MESH)
           for k, land in enumerate(recvs)]
    return out, inn


def copies_start(name, arrays, n_copies, plan, after):
    na = len(arrays)

    def body(*refs):
        out, _ = _descriptors(plan, refs[:na], refs[na + 1], refs[na + 2])
        for cp in out:
            cp.start()
        refs[-1][...] = jnp.zeros((8, 128), F32)

    res = pl.pallas_call(
        body, name=name,
        out_shape=(pltpu.SemaphoreType.DMA((n_copies,)), pltpu.SemaphoreType.DMA((n_copies,)),
                   *[pltpu.HBM(a.shape, a.dtype) for a in arrays], jax.ShapeDtypeStruct((8, 128), F32)),
        in_specs=[HBM] * na + [ANY], out_specs=(SEM, SEM, *[HBM] * na, VMEM),
        input_output_aliases={i: i + 2 for i in range(na)},
        compiler_params=pltpu.CompilerParams(has_side_effects=DATAFLOW),
    )(*[pltpu.with_memory_space_constraint(a, pltpu.HBM) for a in arrays], after)
    return res[0], res[1], list(res[2:2 + na]), res[-1]


def copies_wait(name, started, plan, after):
    send_sems, recv_sems, arrays, _ = started
    na = len(arrays)

    def body(*refs):
        out, inn = _descriptors(plan, refs[:na], refs[na], refs[na + 1])
        for cp in out:
            cp.wait_send()
        for cp in inn:
            cp.wait_recv()
        refs[-1][...] = jnp.zeros((8, 128), F32)

    res = pl.pallas_call(
        body, name=name,
        out_shape=(*[pltpu.HBM(a.shape, a.dtype) for a in arrays], jax.ShapeDtypeStruct((8, 128), F32)),
        in_specs=[HBM] * na + [SEM, SEM, ANY], out_specs=(*[HBM] * na, VMEM),
        input_output_aliases={i: i for i in range(na)},
        compiler_params=pltpu.CompilerParams(has_side_effects=DATAFLOW),
    )(*arrays, send_sems, recv_sems, after)
    return list(res[:na]), res[-1]


def _rows_half(r, c):
    return pl.ds(c * (r // 2), r // 2), pl.ds((1 - c) * (r // 2), r // 2)


def plan_gather_ici(refs):
    x, y, c, chips = _place()
    p = 2 * x + y
    sends, recvs = [], []
    for buf in refs:
        mine, _ = _rows_half(buf.shape[1], c)
        for chip in chips:
            sends.append((buf.at[p, mine], buf.at[p, mine], (*chip, c)))
            recvs.append(buf.at[2 * chip[0] + chip[1], mine])
    return sends, recvs


def plan_gather_d2d(refs):
    x, y, c, chips = _place()
    sends, recvs = [], []
    for buf in refs:
        mine, theirs = _rows_half(buf.shape[1], c)
        for chip in chips:
            slot = 2 * chip[0] + chip[1]
            sends.append((buf.at[slot, mine], buf.at[slot, mine], (x, y, 1 - c)))
            recvs.append(buf.at[slot, theirs])
    return sends, recvs


def plan_exchange(refs):
    x, y, c, _ = _place()
    n = len(refs) // 2
    sends, recvs = [], []
    for grad, land in zip(refs[:n], refs[n:]):
        _, theirs = _rows_half(grad.shape[1], c)
        sends.append((grad.at[:, theirs], land, (x, y, 1 - c)))
        recvs.append(land)
    return sends, recvs


def plan_scatter(refs):
    x, y, c, chips = _place()
    n = len(refs) // 2
    sends, recvs = [], []
    for part, land in zip(refs[:n], refs[n:]):
        for j, chip in enumerate(chips):
            sends.append((part.at[2 * chip[0] + chip[1]], land.at[j], (*chip, c)))
            recvs.append(land.at[j])
    return sends, recvs


def plan_share(refs):
    x, y, c, _ = _place()
    sends, recvs = [], []
    for buf in refs:
        mine, theirs = _rows_half(buf.shape[0], c)
        sends.append((buf.at[mine], buf.at[mine], (x, y, 1 - c)))
        recvs.append(buf.at[theirs])
    return sends, recvs


def put_in_slot(w, chip, dtype, name):
    r, c = w.shape
    tr = 256 if r % 256 == 0 else r

    def body(chip_ref, w_ref, o_ref):
        o_ref[...] = w_ref[...].astype(dtype)

    grid_spec = pltpu.PrefetchScalarGridSpec(
        num_scalar_prefetch=1, grid=(r // tr,),
        in_specs=[pl.BlockSpec((tr, c), lambda i, chip_ref: (i, 0))],
        out_specs=pl.BlockSpec((None, tr, c), lambda i, chip_ref: (chip_ref[0], i, 0)))
    return pl.pallas_call(body, name=name, grid_spec=grid_spec,
                          out_shape=jax.ShapeDtypeStruct((4, r, c), dtype), compiler_params=_cp())(chip, w)


def ada_fwd(s_in, ada_w, ada_b, tn):
    nl, d, ws = ada_w.shape

    def body(s_ref, w_ref, b_ref, so_ref, mod_ref):
        s = _silu(s_ref[...])
        so_ref[...] = s
        mod_ref[...] = _dot(s.astype(BF16), w_ref[...].astype(BF16)) + b_ref[...]

    return pl.pallas_call(
        body, name="ada_fwd", grid=(nl, ws // tn),
        in_specs=[pl.BlockSpec((16, d), lambda l, j: (0, 0)),
                  pl.BlockSpec((None, d, tn), lambda l, j: (l, 0, j)),
                  pl.BlockSpec((None, 1, tn), lambda l, j: (l, 0, j))],
        out_specs=[pl.BlockSpec((16, d), lambda l, j: (0, 0)),
                   pl.BlockSpec((None, 16, tn), lambda l, j: (l, 0, j))],
        out_shape=[jax.ShapeDtypeStruct((16, d), F32), jax.ShapeDtypeStruct((nl, 16, ws), F32)],
        compiler_params=_cp(),
    )(s_in, ada_w, ada_b)


def _adamw_math(w, g, m, v):
    m = ADAM_B1 * m + (1.0 - ADAM_B1) * g
    v = ADAM_B2 * v + (1.0 - ADAM_B2) * (g * g)
    m_hat = m / (1.0 - ADAM_B1 ** ADAM_STEP)
    v_hat = v / (1.0 - ADAM_B2 ** ADAM_STEP)
    delta = -ADAM_LR * (m_hat / (jnp.sqrt(v_hat) + ADAM_EPS) + ADAM_WD * w)
    return delta, m, v


def ada_bwd_adamw(s, dm, w, m, v):
    nl, d, ws = w.shape
    tr = 256 if d % 256 == 0 else 128

    def body(s_ref, dm_ref, w_ref, m_ref, v_ref, g_ref, dl_ref, mo_ref, vo_ref, dc_ref):
        dmv = dm_ref[...].astype(BF16)
        wv = w_ref[...]
        g = _dot(s_ref[...].astype(BF16), dmv, TN)
        g_ref[...] = g
        dl_ref[...], mo_ref[...], vo_ref[...] = _adamw_math(wv, g, m_ref[...], v_ref[...])
        dc_ref[...] = _dot(dmv[8:16, :], wv.astype(BF16), NT)

    wblk = pl.BlockSpec((None, tr, ws), lambda l, i: (l, i, 0))
    wshape = jax.ShapeDtypeStruct((nl, d, ws), F32)
    return pl.pallas_call(
        body, name="ada_bwd_adamw", grid=(nl, d // tr),
        in_specs=[pl.BlockSpec((16, tr), lambda l, i: (0, i)),
                  pl.BlockSpec((None, 16, ws), lambda l, i: (l, 0, 0)), wblk, wblk, wblk],
        out_specs=[wblk, wblk, wblk, wblk, pl.BlockSpec((None, 8, tr), lambda l, i: (l, 0, i))],
        out_shape=[wshape, wshape, wshape, wshape, jax.ShapeDtypeStruct((nl, 8, d), F32)],
        compiler_params=_cp(),
    )(s, dm, w, m, v)


def adamw(w, g, m, v, name):
    r, c = w.shape
    tr = 256 if r % 256 == 0 else r

    def body(w_ref, g_ref, m_ref, v_ref, dl_ref, mo_ref, vo_ref):
        dl_ref[...], mo_ref[...], vo_ref[...] = _adamw_math(w_ref[...], g_ref[...], m_ref[...], v_ref[...])

    blk = pl.BlockSpec((tr, c), lambda i: (i, 0))
    shape = jax.ShapeDtypeStruct((r, c), F32)
    return pl.pallas_call(body, name=name, grid=(r // tr,), in_specs=[blk] * 4, out_specs=[blk] * 3,
                          out_shape=[shape] * 3, compiler_params=_cp())(w, g, m, v)


SMALL_ROWS = 24
ROW_MOD = 10


def small_reduce(gathered):
    _, rows, d = gathered.shape

    def body(g_ref, o_ref):
        tot = g_ref[0]
        for b in range(1, 8):
            tot = tot + g_ref[b]
        o_ref[0:rows, :] = tot
        for layer in range(2):
            lat = ROW_MOD + 6 * layer
            o_ref[24 + 3 * layer:27 + 3 * layer, :] = tot[lat:lat + 3, :] + tot[lat + 3:lat + 6, :]
        o_ref[30:32, :] = jnp.zeros((2, d), F32)

    return pl.pallas_call(body, name="small_reduce", in_specs=[VMEM], out_specs=VMEM,
                          out_shape=jax.ShapeDtypeStruct((32, d), F32), compiler_params=_cp())(gathered)


def lb_logits_grad(lbl, dlb):
    _, _, n = lbl.shape

    def body(l_ref, d_ref, o_ref):
        for dr in range(2):
            _, (p0, p1, p2) = _lower_bound(l_ref, dr)
            dv = d_ref[dr:dr + 1, :]
            o_ref[dr, 0:1, :] = p0 * p2 * dv
            o_ref[dr, 1:2, :] = p1 * p2 * dv
            o_ref[dr, 2:3, :] = -p2 * (p0 + p1) * dv

    return pl.pallas_call(body, name="lb_logits_grad", in_specs=[VMEM, VMEM], out_specs=VMEM,
                          out_shape=jax.ShapeDtypeStruct((2, 3, n), F32), compiler_params=_cp())(lbl, dlb)


def c_ctx_grad(parts, c_ctx):
    d = c_ctx.shape[1]

    def body(p_ref, c_ref, o_ref):
        tot = p_ref[0, 0:1, :]
        for chip in range(1, 4):
            tot = tot + p_ref[2 * chip, 0:1, :]
        o_ref[...] = tot * _dsilu(c_ref[...])

    return pl.pallas_call(body, name="c_ctx_grad", in_specs=[VMEM, VMEM], out_specs=VMEM,
                          out_shape=jax.ShapeDtypeStruct((1, d), F32), compiler_params=_cp())(parts, c_ctx)


def _reduce_scatter(grads, core, chip_core):
    got = exchange_halves(grads)
    sums = [pair_sum(g, r, core) for g, r in zip(grads, got)]
    recv = scatter_to_owners([sb for _, sb in sums])
    reduced = [owner_sum(s, r, chip_core) for (s, _), r in zip(sums, recv)]
    return share_halves(reduced)


def kernel(x, c, ctx, c_ctx, ada_w, ada_b, pre_g, post_g, ev_w_in, ev_pool_w, ev_pool_scale, ev_conv_w, ev_conv_b, ev_w_out, od_w_in, od_onorm_g, od_w_out, lb_logits, loss_target, m_c_ctx, m_ada_w, m_ada_b, m_pre_g, m_post_g, m_ev_w_in, m_ev_pool_w, m_ev_pool_scale, m_ev_conv_w, m_ev_conv_b, m_ev_w_out, m_od_w_in, m_od_onorm_g, m_od_w_out, m_lb_logits, v_c_ctx, v_ada_w, v_ada_b, v_pre_g, v_post_g, v_ev_w_in, v_ev_pool_w, v_ev_pool_scale, v_ev_conv_w, v_ev_conv_b, v_ev_w_out, v_od_w_in, v_od_onorm_g, v_od_w_out, v_lb_logits):
    _, seq, d = x.shape
    cx = ctx.shape[1]
    t = cx + seq
    half_d = d // 2
    g = half_d // N_POOL
    tn = d // 4
    xi, yi, ci = lax.axis_index("x"), lax.axis_index("y"), lax.axis_index("c")
    chip = 2 * xi + yi
    me = 2 * chip + ci
    core_arr = jnp.reshape(ci, (1,)).astype(jnp.int32)
    chip_arr = jnp.reshape(chip, (1,)).astype(jnp.int32)
    chip_core_arr = jnp.stack([chip, ci]).astype(jnp.int32)

    pad = lambda a, rows: jnp.concatenate([a, jnp.zeros((rows - a.shape[0], g), F32)], axis=0)
    small = jnp.concatenate([
        ev_pool_w.reshape(g, g), pad(ev_conv_w.reshape(3, g), 8), pad(od_onorm_g.reshape(2, g), 8),
        pad(lb_logits.reshape(12, g), 16)], axis=0)
    ev_in_g, ev_out_g, small_g = allgather_shards([
        put_in_slot(ev_w_in[0], chip_arr, BF16, "cast_ev_w_in"),
        put_in_slot(ev_w_out[0], chip_arr, BF16, "cast_ev_w_out"),
        put_in_slot(small, chip_arr, F32, "place_small")])
    ev_out3 = ev_out_g.reshape(1, d, d)
    pool_w_full = small_g[:, :g].reshape(4, N_POOL, g // 4, g).transpose(1, 0, 2, 3).reshape(N_POOL, g, g)
    conv_w_full = small_g[:, g:g + 3].transpose(1, 0, 2).reshape(3, half_d)
    onorm_full = small_g[:, g + 8:g + 10].reshape(1, d)
    lbl_full = small_g[:, g + 16:g + 28].reshape(4, 2, 3, 2 * g).transpose(1, 2, 0, 3).reshape(2, 3, d)

    c_rows = jnp.concatenate([c, jnp.zeros((7, d), F32)], axis=0)
    c_all = allgather8(c_rows, "allgather_c")[:, 0, :]
    s_in = jnp.concatenate([c_all, c_ctx.reshape(1, d), jnp.zeros((7, d), F32)], axis=0)
    ws_ada = ada_w.shape[2]
    ada_b_mine = lax.dynamic_slice(ada_b, (0, chip * ws_ada), (2, ws_ada)).reshape(2, 1, ws_ada)
    s_act, mod_mine = ada_fwd(s_in, ada_w, ada_b_mine, tn)
    mod_all = allgather8(mod_mine.reshape(32, ws_ada), "allgather_mod")
    od_ici = copies_start("gather_od_ici_start", [
        put_in_slot(od_w_in[0], chip_arr, BF16, "cast_od_w_in"),
        put_in_slot(od_w_out[0], chip_arr, BF16, "cast_od_w_out")], 6, plan_gather_ici, mod_all)
    mod_full = mod_all[0::2].reshape(4, 2, 16, ws_ada).transpose(1, 2, 0, 3).reshape(2, 16, 3 * d)
    mod_lat = lax.dynamic_slice(mod_full, (0, me, 0), (2, 1, 3 * d))
    mods = jnp.concatenate([mod_full[:, 8:9], mod_lat], axis=1)
    shift, scale, gate = mods[:, :, :d], mods[:, :, d:2 * d], mods[:, :, 2 * d:]

    xs = jnp.concatenate([ctx[0], x[0]], axis=0)

    h0 = normmod_fwd(xs, pre_g[0:1] + od_ici[3][0:1, 0:1], shift[0], scale[0], cx)
    z0 = mm_nn(h0, ev_in_g, half_d, tn, "mm_ev_in")
    u_a = mix_a_fwd(z0, pool_w_full, ev_pool_scale, cx)
    u_b = mix_b_fwd(z0, conv_w_full, ev_conv_b, cx)
    u = jnp.concatenate([u_a, u_b], axis=1)
    y0 = mm_nn(u, ev_out3, d, tn, "mm_ev_out")[0]
    xs1 = post_fwd(xs, y0, post_g[0:1], gate[0], cx)
    od_d2d = copies_start("gather_od_d2d_start",
                          copies_wait("gather_od_ici_wait", od_ici, plan_gather_ici, xs1)[0],
                          6, plan_gather_d2d, xs1)
    (od_in_g, od_out_g), _ = copies_wait("gather_od_d2d_wait", od_d2d, plan_gather_d2d, od_d2d[3])
    od_out3 = od_out_g.reshape(1, d, d)

    h1 = normmod_fwd(xs1, pre_g[1:2], shift[1], scale[1], cx)
    z1 = mm_nn(h1, od_in_g, d, tn, "mm_od_in")
    o1, r1 = hgrn_fwd(z1, lbl_full, onorm_full, cx)
    y1 = mm_nn(r1, od_out3, d, tn, "mm_od_out")[0]
    sq, dx2 = post_loss(xs1, y1, post_g[1:2], gate[1], loss_target[0], cx)
    loss = lax.psum(sq[0, 0] * (0.5 / d), ("x", "y", "c"))

    dy1, dgate1, dpost1 = post_bwd(dx2, y1, post_g[1:2], gate[1], cx, True)
    dr1 = mm_nt(dy1[None], od_out3, tn, "mm_od_out_dx")
    g_od_out = mm_tn(r1, dy1[None], d, tn, "mm_od_out_dw")
    dz1, donorm, dlb = hgrn_bwd(z1, lbl_full, onorm_full, o1, dr1, cx)
    dh1 = mm_nt(dz1, od_in_g, tn, "mm_od_in_dx")
    g_od_in = mm_tn(h1, dz1, od_in_g.shape[2], tn, "mm_od_in_dw")
    dxs1, dpre1, dshift1, dscale1 = normmod_bwd(xs1, dh1, pre_g[1:2], scale[1], dx2, cx, True)

    od_grads = [g_od_in, g_od_out.reshape(4, d // 4, d)]
    half_zone = lambda a, lead, dt: lax.empty((lead, a.shape[1] // 2, a.shape[2]), dt)
    od_ex = copies_start("reduce_od_exchange_start", od_grads + [half_zone(a, 4, F32) for a in od_grads],
                         2, plan_exchange, dxs1)

    dy0, dgate0, dpost0 = post_bwd(dxs1, y0, post_g[0:1] + od_ex[3][0:1, 0:1], gate[0], cx, False)
    du = mm_nt(dy0[None], ev_out3, tn, "mm_ev_out_dx")
    g_ev_out = mm_tn(u, dy0[None], d, tn, "mm_ev_out_dw")
    od_got, _ = copies_wait("reduce_od_exchange_wait", od_ex, plan_exchange, g_ev_out)
    od_sums = [pair_sum(od_got[i], od_got[2 + i], core_arr) for i in range(2)]
    od_sc = copies_start("reduce_od_scatter_start",
                         [sb for _, sb in od_sums] + [half_zone(a, 3, BF16) for a in od_grads],
                         6, plan_scatter, du)
    dz0a, g_pool_w, dpool_scale = mix_a_bwd(z0, du, pool_w_full, ev_pool_scale + od_sc[3][0:1, 0:1], cx)
    dz0b, dconv_w, dconv_b = mix_b_bwd(z0, du, conv_w_full, ev_conv_b + od_sc[3][0:1, 0:1], cx)
    dz0 = jnp.concatenate([dz0a, dz0b], axis=0)
    dh0 = mm_nt(dz0, ev_in_g, tn, "mm_ev_in_dx")
    g_ev_in = mm_tn(h0, dz0, ev_in_g.shape[2], tn, "mm_ev_in_dw")
    dxs0, dpre0, dshift0, dscale0 = normmod_bwd(xs, dh0, pre_g[0:1], scale[0], dxs1, cx, False)
    grad_x = dxs0[cx:][None]
    od_recv, _ = copies_wait("reduce_od_scatter_wait", od_sc, plan_scatter, dxs0)
    ev_grads = [g_ev_in, g_ev_out.reshape(4, d // 4, d), g_pool_w.reshape(4, g, g)]
    ev_ex = copies_start("reduce_ev_exchange_start", ev_grads + [half_zone(a, 4, F32) for a in ev_grads],
                         3, plan_exchange, dxs0)
    od_sh = copies_start("reduce_od_share_start",
                         [owner_sum(od_sums[i][0], od_recv[2 + i], chip_core_arr) for i in range(2)],
                         2, plan_share, ev_ex[3])

    zrow = jnp.zeros((1, d), F32)
    small_rows = jnp.concatenate([
        dpre0, dpre1, dpost0, dpost1,
        jnp.concatenate([dpool_scale, dconv_b], axis=1),
        jnp.concatenate([dconv_w.reshape(1, 3 * half_d), jnp.zeros((1, half_d), F32)], axis=1).reshape(2, d),
        donorm, dlb,
        dshift0[1:2], dscale0[1:2], dgate0[1:2], dshift0[0:1], dscale0[0:1], dgate0[0:1],
        dshift1[1:2], dscale1[1:2], dgate1[1:2], dshift1[0:1], dscale1[0:1], zrow,
        zrow, zrow], axis=0)
    small_all = allgather8(small_rows, "allgather_small")
    tot = small_reduce(small_all)

    dm_rows = []
    for layer in range(2):
        lat = ROW_MOD + 6 * layer
        dm_lat = small_all[:, lat:lat + 3].reshape(8, 3 * d)
        dm_ctx = tot[lat + 3:lat + 6].reshape(1, 3 * d)
        dm_rows.append(jnp.concatenate([dm_lat, dm_ctx, jnp.zeros((7, 3 * d), F32)], axis=0))
    dm_full = jnp.stack(dm_rows)
    dm_mine = lax.dynamic_slice(dm_full, (0, 0, chip * ws_ada), (2, 16, ws_ada))

    ev_got, _ = copies_wait("reduce_ev_exchange_wait", ev_ex, plan_exchange, tot)
    ev_sums = [pair_sum(ev_got[i], ev_got[3 + i], core_arr) for i in range(3)]
    ev_sc = copies_start("reduce_ev_scatter_start",
                         [sb for _, sb in ev_sums] + [half_zone(a, 3, BF16) for a in ev_grads],
                         9, plan_scatter, tot)
    behind = ev_sc[3][0:1, 0:1]
    grad_ada_w, delta_ada_w, new_m_ada_w, new_v_ada_w, dctx_part = ada_bwd_adamw(
        s_act, dm_mine + behind, ada_w, m_ada_w, v_ada_w)
    (grad_od_w_in, grad_od_w_out), _ = copies_wait("reduce_od_share_wait", od_sh, plan_share, ev_sc[3])
    od_w_in_upd = adamw(od_w_in[0], grad_od_w_in, m_od_w_in[0], v_od_w_in[0], "adamw_od_w_in")
    od_w_out_upd = adamw(od_w_out[0], grad_od_w_out, m_od_w_out[0], v_od_w_out[0], "adamw_od_w_out")
    ev_recv, ev_landed = copies_wait("reduce_ev_scatter_wait", ev_sc, plan_scatter, od_w_in_upd[0])
    grad_ev_w_in, grad_ev_w_out, grad_pool_w = share_halves(
        [owner_sum(ev_sums[i][0], ev_recv[3 + i], chip_core_arr) for i in range(3)])
    dctx_all = allgather8(dctx_part[0] + dctx_part[1] + ev_landed[0:1, 0:1], "allgather_dctx")
    grad_c_ctx = c_ctx_grad(dctx_all, c_ctx.reshape(1, d)).reshape(d)

    grad_ada_b = tot[24:30].reshape(2, 3 * d)
    grad_pre_g = tot[0:2]
    grad_post_g = tot[2:4]
    grad_ev_pool_scale = tot[4:5, :half_d]
    grad_ev_conv_b = tot[4:5, half_d:]
    conv_w_tot = tot[5:7].reshape(1, 2 * d)[:, :3 * half_d].reshape(3, N_POOL, g)
    grad_ev_conv_w = lax.dynamic_slice(conv_w_tot, (0, chip, 0), (3, 1, g)).reshape(1, 3, g)
    grad_od_onorm_g = lax.dynamic_slice(tot[7:8], (0, chip * 2 * g), (1, 2 * g))
    dlb_mine = lax.dynamic_slice(tot[8:10], (0, chip * 2 * g), (2, 2 * g))
    grad_lb_logits = lb_logits_grad(lb_logits, dlb_mine)
    grad_ev_w_in = grad_ev_w_in[None]
    grad_od_w_in = grad_od_w_in[None]
    grad_ev_w_out = grad_ev_w_out[None]
    grad_od_w_out = grad_od_w_out[None]
    grad_ev_pool_w = grad_pool_w.reshape(1, N_POOL, g // 4, g)

    def step(w, gr, m, v, name):
        shape = w.shape
        cols = shape[-1]
        two_d = lambda a: a.reshape(-1, cols)
        dl, mo, vo = adamw(two_d(w), two_d(gr), two_d(m), two_d(v), "adamw_" + name)
        return dl.reshape(shape), mo.reshape(shape), vo.reshape(shape)

    upd = {
        "c_ctx": step(c_ctx, grad_c_ctx, m_c_ctx, v_c_ctx, "c_ctx"),
        "ada_w": (delta_ada_w, new_m_ada_w, new_v_ada_w),
        "ada_b": step(ada_b, grad_ada_b, m_ada_b, v_ada_b, "ada_b"),
        "pre_g": step(pre_g, grad_pre_g, m_pre_g, v_pre_g, "pre_g"),
        "post_g": step(post_g, grad_post_g, m_post_g, v_post_g, "post_g"),
        "ev_w_in": step(ev_w_in, grad_ev_w_in, m_ev_w_in, v_ev_w_in, "ev_w_in"),
        "ev_pool_w": step(ev_pool_w, grad_ev_pool_w, m_ev_pool_w, v_ev_pool_w, "ev_pool_w"),
        "ev_pool_scale": step(ev_pool_scale, grad_ev_pool_scale, m_ev_pool_scale, v_ev_pool_scale, "ev_pool_scale"),
        "ev_conv_w": step(ev_conv_w, grad_ev_conv_w, m_ev_conv_w, v_ev_conv_w, "ev_conv_w"),
        "ev_conv_b": step(ev_conv_b, grad_ev_conv_b, m_ev_conv_b, v_ev_conv_b, "ev_conv_b"),
        "ev_w_out": step(ev_w_out, grad_ev_w_out, m_ev_w_out, v_ev_w_out, "ev_w_out"),
        "od_w_in": tuple(a[None] for a in od_w_in_upd),
        "od_onorm_g": step(od_onorm_g, grad_od_onorm_g, m_od_onorm_g, v_od_onorm_g, "od_onorm_g"),
        "od_w_out": tuple(a[None] for a in od_w_out_upd),
        "lb_logits": step(lb_logits, grad_lb_logits, m_lb_logits, v_lb_logits, "lb_logits"),
    }
    names = ["c_ctx", "ada_w", "ada_b", "pre_g", "post_g", "ev_w_in", "ev_pool_w", "ev_pool_scale",
             "ev_conv_w", "ev_conv_b", "ev_w_out", "od_w_in", "od_onorm_g", "od_w_out", "lb_logits"]
    grads = [grad_c_ctx, grad_ada_w, grad_ada_b, grad_pre_g, grad_post_g, grad_ev_w_in, grad_ev_pool_w,
             grad_ev_pool_scale, grad_ev_conv_w, grad_ev_conv_b, grad_ev_w_out, grad_od_w_in,
             grad_od_onorm_g, grad_od_w_out, grad_lb_logits]
    return (loss, grad_x, *grads, *[upd[k][0] for k in names], *[upd[k][1] for k in names],
            *[upd[k][2] for k in names])
```

```python
import functools

import jax
import jax.numpy as jnp
from jax import lax
from jax.experimental import pallas as pl
from jax.experimental.pallas import tpu as pltpu

EPS = 1e-6
GRID_W_LOG2 = 6
CHUNK = 64
HEAD = 128
N_POOL = 4
ADAM_LR, ADAM_B1, ADAM_B2, ADAM_EPS, ADAM_WD, ADAM_STEP = 0.001, 0.9, 0.999, 1e-08, 0.01, 10
VMEM_LIMIT = 56 * 1024 * 1024
MESH = pl.DeviceIdType.MESH
F32, BF16 = jnp.float32, jnp.bfloat16
ANY = pl.BlockSpec(memory_space=pl.ANY)
VMEM = pl.BlockSpec(memory_space=pltpu.VMEM)


def _cp(**kw):
    return pltpu.CompilerParams(vmem_limit_bytes=VMEM_LIMIT, **kw)


def _silu(x):
    return x * jax.nn.sigmoid(x)


def _dsilu(x):
    s = jax.nn.sigmoid(x)
    return s * (1.0 + x * (1.0 - s))


def _dot(a, b, dims=((1,), (0,)), precision=None):
    return lax.dot_general(a, b, (dims, ((), ())), preferred_element_type=F32, precision=precision)


NN = ((1,), (0,))
NT = ((1,), (1,))
TN = ((0,), (0,))


def _row_block(cx):
    return 256 if cx % 256 == 0 else 128


def normmod_fwd(xs, g, shift, scale, cx):
    t, d = xs.shape
    tm = _row_block(cx)
    nctx = cx // tm

    def body(x_ref, g_ref, sh_ref, sc_ref, h_ref):
        is_ctx = pl.program_id(0) < nctx
        x = x_ref[...]
        rstd = lax.rsqrt(jnp.mean(x * x, axis=-1, keepdims=True) + EPS)
        sc = jnp.where(is_ctx, sc_ref[0:1, :], sc_ref[1:2, :])
        sh = jnp.where(is_ctx, sh_ref[0:1, :], sh_ref[1:2, :])
        h_ref[...] = ((x * rstd) * g_ref[...] * (1.0 + sc) + sh).astype(BF16)

    row = pl.BlockSpec((tm, d), lambda i: (i, 0))
    vec = lambda r: pl.BlockSpec((r, d), lambda i: (0, 0))
    return pl.pallas_call(
        body, name="normmod_fwd", grid=(t // tm,),
        in_specs=[row, vec(1), vec(2), vec(2)], out_specs=row,
        out_shape=jax.ShapeDtypeStruct((t, d), BF16), compiler_params=_cp(),
    )(xs, g, shift, scale)


def normmod_bwd(xs, dh, g, scale, dres, cx, res_is_latent_only):
    t, d = xs.shape
    tm = _row_block(cx)
    nctx = cx // tm

    def body(x_ref, dh_ref, g_ref, sc_ref, dres_ref, dx_ref, dg_ref, dsh_ref, dsc_ref):
        i = pl.program_id(0)
        is_ctx = i < nctx

        @pl.when(i == 0)
        def _():
            dg_ref[...] = jnp.zeros_like(dg_ref)
            dsh_ref[...] = jnp.zeros_like(dsh_ref)
            dsc_ref[...] = jnp.zeros_like(dsc_ref)

        x = x_ref[...]
        dh = dh_ref[...]
        gv = g_ref[...]
        rstd = lax.rsqrt(jnp.mean(x * x, axis=-1, keepdims=True) + EPS)
        xhat = x * rstd
        sc = jnp.where(is_ctx, sc_ref[0:1, :], sc_ref[1:2, :])
        dsh = jnp.sum(dh, axis=0, keepdims=True)
        dhx = dh * xhat
        dsc = jnp.sum(dhx * gv, axis=0, keepdims=True)
        dg_ref[...] += jnp.sum(dhx * (1.0 + sc), axis=0, keepdims=True)
        zero = jnp.zeros_like(dsh)
        dsh_ref[0:1, :] += jnp.where(is_ctx, dsh, zero)
        dsh_ref[1:2, :] += jnp.where(is_ctx, zero, dsh)
        dsc_ref[0:1, :] += jnp.where(is_ctx, dsc, zero)
        dsc_ref[1:2, :] += jnp.where(is_ctx, zero, dsc)
        dxhat = dh * (gv * (1.0 + sc))
        dx = rstd * (dxhat - xhat * jnp.mean(dxhat * xhat, axis=-1, keepdims=True))
        res = dres_ref[...]
        if res_is_latent_only:
            res = jnp.where(is_ctx, jnp.zeros_like(res), res)
        dx_ref[...] = dx + res

    row = pl.BlockSpec((tm, d), lambda i: (i, 0))
    if res_is_latent_only:
        res_spec = pl.BlockSpec((tm, d), lambda i: (jnp.maximum(i - nctx, 0), 0))
    else:
        res_spec = row
    vec = lambda r: pl.BlockSpec((r, d), lambda i: (0, 0))
    return pl.pallas_call(
        body, name="normmod_bwd", grid=(t // tm,),
        in_specs=[row, row, vec(1), vec(2), res_spec],
        out_specs=[row, vec(1), vec(2), vec(2)],
        out_shape=[jax.ShapeDtypeStruct((t, d), F32), jax.ShapeDtypeStruct((1, d), F32),
                   jax.ShapeDtypeStruct((2, d), F32), jax.ShapeDtypeStruct((2, d), F32)],
        compiler_params=_cp(),
    )(xs, dh, g, scale, dres)


def post_fwd(xs, y, pg, gate, cx):
    t, d = xs.shape
    tm = _row_block(cx)
    nctx = cx // tm

    def body(x_ref, y_ref, pg_ref, gate_ref, o_ref):
        is_ctx = pl.program_id(0) < nctx
        y = y_ref[...]
        rstd = lax.rsqrt(jnp.mean(y * y, axis=-1, keepdims=True) + EPS)
        gt = jnp.where(is_ctx, gate_ref[0:1, :], gate_ref[1:2, :])
        o_ref[...] = x_ref[...] + gt * ((y * rstd) * pg_ref[...])

    row = pl.BlockSpec((tm, d), lambda i: (i, 0))
    vec = lambda r: pl.BlockSpec((r, d), lambda i: (0, 0))
    return pl.pallas_call(
        body, name="post_fwd", grid=(t // tm,),
        in_specs=[row, row, vec(1), vec(2)], out_specs=row,
        out_shape=jax.ShapeDtypeStruct((t, d), F32), compiler_params=_cp(),
    )(xs, y, pg, gate)


def post_loss(xs, y, pg, gate, target, cx):
    t, d = xs.shape
    n = y.shape[0]
    tm = _row_block(cx)
    nctx = cx // tm

    def body(x_ref, y_ref, pg_ref, gate_ref, tgt_ref, sq_ref, dx_ref):
        @pl.when(pl.program_id(0) == 0)
        def _():
            sq_ref[...] = jnp.zeros_like(sq_ref)

        y = y_ref[...]
        rstd = lax.rsqrt(jnp.mean(y * y, axis=-1, keepdims=True) + EPS)
        x2 = x_ref[...] + gate_ref[1:2, :] * ((y * rstd) * pg_ref[...])
        err = x2 - tgt_ref[...]
        sq_ref[...] += jnp.sum(err * err)
        dx_ref[...] = err * (1.0 / d)

    row = pl.BlockSpec((tm, d), lambda i: (i, 0))
    xrow = pl.BlockSpec((tm, d), lambda i: (i + nctx, 0))
    vec = lambda r: pl.BlockSpec((r, d), lambda i: (0, 0))
    return pl.pallas_call(
        body, name="post_loss", grid=(n // tm,),
        in_specs=[xrow, row, vec(1), vec(2), row],
        out_specs=[pl.BlockSpec((8, 128), lambda i: (0, 0)), row],
        out_shape=[jax.ShapeDtypeStruct((8, 128), F32), jax.ShapeDtypeStruct((n, d), F32)],
        compiler_params=_cp(),
    )(xs, y, pg, gate, target)


def post_bwd(dxo, y, pg, gate, cx, latent_only):
    m, d = y.shape
    tm = _row_block(cx)
    nctx = 0 if latent_only else cx // tm

    def body(dx_ref, y_ref, pg_ref, gate_ref, dy_ref, dgate_ref, dpg_ref):
        i = pl.program_id(0)
        is_ctx = i < nctx

        @pl.when(i == 0)
        def _():
            dgate_ref[...] = jnp.zeros_like(dgate_ref)
            dpg_ref[...] = jnp.zeros_like(dpg_ref)

        y = y_ref[...]
        dx = dx_ref[...]
        pgv = pg_ref[...]
        rstd = lax.rsqrt(jnp.mean(y * y, axis=-1, keepdims=True) + EPS)
        yhat = y * rstd
        gt = jnp.where(is_ctx, gate_ref[0:1, :], gate_ref[1:2, :])
        dxy = dx * yhat
        dgt = jnp.sum(dxy * pgv, axis=0, keepdims=True)
        zero = jnp.zeros_like(dgt)
        dgate_ref[0:1, :] += jnp.where(is_ctx, dgt, zero)
        dgate_ref[1:2, :] += jnp.where(is_ctx, zero, dgt)
        dpg_ref[...] += jnp.sum(dxy * gt, axis=0, keepdims=True)
        dyhat = dx * (gt * pgv)
        dy = rstd * (dyhat - yhat * jnp.mean(dyhat * yhat, axis=-1, keepdims=True))
        dy_ref[...] = dy.astype(BF16)

    row = pl.BlockSpec((tm, d), lambda i: (i, 0))
    vec = lambda r: pl.BlockSpec((r, d), lambda i: (0, 0))
    return pl.pallas_call(
        body, name="post_bwd", grid=(m // tm,),
        in_specs=[row, row, vec(1), vec(2)], out_specs=[row, vec(2), vec(1)],
        out_shape=[jax.ShapeDtypeStruct((m, d), BF16), jax.ShapeDtypeStruct((2, d), F32),
                   jax.ShapeDtypeStruct((1, d), F32)],
        compiler_params=_cp(),
    )(dxo, y, pg, gate)


def _split_rows(m):
    for cand in (1024, 768, 512, 384, 256, 128):
        if m % cand == 0 and m // cand >= 2:
            return cand
    return m


def mm_nn(a, w3, sec, tn, name):
    m, k = a.shape
    q, _, ws = w3.shape
    n = q * ws
    tpq, tps = ws // tn, sec // tn
    tm = 256 if m % 256 == 0 else 128

    def body(a_ref, w_ref, o_ref):
        w = w_ref[...]

        def step(i, carry):
            rows = pl.ds(pl.multiple_of(i * tm, tm), tm)
            o_ref[rows, :] = _dot(a_ref[rows, :], w)
            return carry

        lax.fori_loop(0, m // tm, step, 0)

    return pl.pallas_call(
        body, name=name, grid=(n // tn,),
        in_specs=[pl.BlockSpec((m, k), lambda j: (0, 0)),
                  pl.BlockSpec((None, k, tn), lambda j: (j // tpq, 0, j % tpq))],
        out_specs=pl.BlockSpec((None, m, tn), lambda j: (j // tps, 0, j % tps)),
        out_shape=jax.ShapeDtypeStruct((n // sec, m, sec), F32), compiler_params=_cp(),
    )(a, w3)


def mm_nt(a3, w3, tn, name):
    s, m, sec = a3.shape
    q, k, ws = w3.shape
    n = q * ws
    tpq, tps = ws // tn, sec // tn
    mb = _split_rows(m)

    def body(a_ref, w_ref, o_ref):
        @pl.when(pl.program_id(1) == 0)
        def _():
            o_ref[...] = jnp.zeros_like(o_ref)

        o_ref[...] += _dot(a_ref[...], w_ref[...], NT)

    return pl.pallas_call(
        body, name=name, grid=(m // mb, n // tn),
        in_specs=[pl.BlockSpec((None, mb, tn), lambda i, j: (j // tps, i, j % tps)),
                  pl.BlockSpec((None, k, tn), lambda i, j: (j // tpq, 0, j % tpq))],
        out_specs=pl.BlockSpec((mb, k), lambda i, j: (i, 0)),
        out_shape=jax.ShapeDtypeStruct((m, k), F32), compiler_params=_cp(),
    )(a3, w3)


def mm_tn(a, b3, ws, tn, name):
    m, k = a.shape
    s, _, sec = b3.shape
    n = s * sec
    tpq, tps = ws // tn, sec // tn
    kb = 256 if k % 256 == 0 else 128

    def body(a_ref, b_ref, o_ref):
        b = b_ref[...]
        for i in range(k // kb):
            o_ref[i * kb:(i + 1) * kb, :] = _dot(a_ref[:, i * kb:(i + 1) * kb], b, TN)

    return pl.pallas_call(
        body, name=name, grid=(n // tn,),
        in_specs=[pl.BlockSpec((m, k), lambda j: (0, 0)),
                  pl.BlockSpec((None, m, tn), lambda j: (j // tps, 0, j % tps))],
        out_specs=pl.BlockSpec((None, k, tn), lambda j: (j // tpq, 0, j % tpq)),
        out_shape=jax.ShapeDtypeStruct((n // ws, k, ws), F32), compiler_params=_cp(),
    )(a, b3)


def _pool_mask(gi, row0, tm, t, cx, seq, transposed):
    half = jnp.left_shift(1, gi)
    r = lax.broadcasted_iota(jnp.int32, (tm, 1), 0) + row0
    c = lax.broadcasted_iota(jnp.int32, (1, t), 1)
    out_tok, src_tok = (c, r) if transposed else (r, c)

    def parts(tok):
        lat = tok - cx
        return tok < cx, lat >> GRID_W_LOG2, lat & ((1 << GRID_W_LOG2) - 1)

    o_ctx, o_row, o_col = parts(out_tok)
    s_ctx, s_row, s_col = parts(src_tok)

    def inside(o, s):
        return (s >= o - half) & (s <= o + half - 1)

    ctx_hit = o_ctx & s_ctx & inside(out_tok, src_tok)
    lat_hit = (~o_ctx) & (~s_ctx) & inside(o_row, s_row) & inside(o_col, s_col)
    mask = jnp.where(ctx_hit | lat_hit, 1.0, 0.0).astype(BF16)

    own_ctx, own_row, own_col = parts(r)

    def count(pos, size):
        return jnp.minimum(pos + half - 1, size - 1) - jnp.maximum(pos - half, 0) + 1

    cnt = jnp.where(own_ctx, count(r, cx),
                    count(own_row, seq >> GRID_W_LOG2) * count(own_col, 1 << GRID_W_LOG2))
    return mask, 1.0 / cnt.astype(F32)


def mix_a_fwd(z0, pool_w, pool_scale, cx):
    _, t, half_d = z0.shape
    g = half_d // N_POOL
    seq = t - cx
    tm = _row_block(cx)

    def body(v_ref, ag_ref, w_ref, sc_ref, u_ref, vb_ref):
        gi = pl.program_id(0)
        vb_ref[...] = v_ref[...].astype(BF16)
        w = w_ref[...].astype(BF16)
        sc = sc_ref[...]

        def step(i, carry):
            row0 = pl.multiple_of(i * tm, tm)
            rows = pl.ds(row0, tm)
            mask, inv = _pool_mask(gi, row0, tm, t, cx, seq, False)
            pooled = _dot(mask, vb_ref[...]) * inv - v_ref[rows, :]
            mixed = _dot(pooled.astype(BF16), w) * sc
            u_ref[rows, :] = (mixed * _silu(ag_ref[rows, :])).astype(BF16)
            return carry

        lax.fori_loop(0, t // tm, step, 0)

    sec = lambda s: pl.BlockSpec((None, t, g), lambda j: (s, 0, j))
    return pl.pallas_call(
        body, name="mix_a_fwd", grid=(N_POOL,),
        in_specs=[sec(0), sec(1), pl.BlockSpec((None, g, g), lambda j: (j, 0, 0)),
                  pl.BlockSpec((1, g), lambda j: (0, j))],
        out_specs=pl.BlockSpec((t, g), lambda j: (0, j)),
        out_shape=jax.ShapeDtypeStruct((t, half_d), BF16),
        scratch_shapes=[pltpu.VMEM((t, g), BF16)], compiler_params=_cp(),
    )(z0, z0, pool_w, pool_scale)


def mix_a_bwd(z0, du, pool_w, pool_scale, cx):
    _, t, half_d = z0.shape
    g = half_d // N_POOL
    seq = t - cx
    tm = _row_block(cx)
    gq = g // 4

    def body(v_ref, ag_ref, du_ref, w_ref, sc_ref, dz_ref, dw_ref, dsc_ref,
             vb_ref, pooled_ref, dmx_ref, dpl_ref, wdp_ref):
        gi = pl.program_id(0)
        vb_ref[...] = v_ref[...].astype(BF16)
        w = w_ref[...].astype(BF16)
        sc = sc_ref[...]

        def first(i, dsc):
            row0 = pl.multiple_of(i * tm, tm)
            rows = pl.ds(row0, tm)
            mask, inv = _pool_mask(gi, row0, tm, t, cx, seq, False)
            pooled = (_dot(mask, vb_ref[...]) * inv - v_ref[rows, :]).astype(BF16)
            pooled_ref[rows, :] = pooled
            mixed = _dot(pooled, w)
            ag = ag_ref[rows, :]
            duv = du_ref[rows, :]
            dz_ref[1, rows, :] = (duv * (mixed * sc) * _dsilu(ag)).astype(BF16)
            dms = duv * _silu(ag)
            dmixed = (dms * sc).astype(BF16)
            dmx_ref[rows, :] = dmixed
            dpooled = _dot(dmixed, w, NT)
            dpl_ref[rows, :] = dpooled
            wdp_ref[rows, :] = (dpooled * inv).astype(BF16)
            return dsc + jnp.sum(dms * mixed, axis=0, keepdims=True)

        dsc_ref[...] = lax.fori_loop(0, t // tm, first, jnp.zeros((1, g), F32))
        dw = _dot(pooled_ref[...], dmx_ref[...], TN)
        for qi in range(4):
            dw_ref[qi] = dw[qi * gq:(qi + 1) * gq, :]

        def second(i, carry):
            row0 = pl.multiple_of(i * tm, tm)
            rows = pl.ds(row0, tm)
            mask_t, _ = _pool_mask(gi, row0, tm, t, cx, seq, True)
            dz_ref[0, rows, :] = (_dot(mask_t, wdp_ref[...]) - dpl_ref[rows, :]).astype(BF16)
            return carry

        lax.fori_loop(0, t // tm, second, 0)

    sec = lambda s: pl.BlockSpec((None, t, g), lambda j: (s, 0, j))
    return pl.pallas_call(
        body, name="mix_a_bwd", grid=(N_POOL,),
        in_specs=[sec(0), sec(1), pl.BlockSpec((t, g), lambda j: (0, j)),
                  pl.BlockSpec((None, g, g), lambda j: (j, 0, 0)),
                  pl.BlockSpec((1, g), lambda j: (0, j))],
        out_specs=[pl.BlockSpec((2, t, g), lambda j: (0, 0, j)),
                   pl.BlockSpec((4, None, gq, g), lambda j: (0, j, 0, 0)),
                   pl.BlockSpec((1, g), lambda j: (0, j))],
        out_shape=[jax.ShapeDtypeStruct((2, t, half_d), BF16),
                   jax.ShapeDtypeStruct((4, N_POOL, gq, g), F32),
                   jax.ShapeDtypeStruct((1, half_d), F32)],
        scratch_shapes=[pltpu.VMEM((t, g), BF16), pltpu.VMEM((t, g), BF16), pltpu.VMEM((t, g), BF16),
                        pltpu.VMEM((t, g), F32), pltpu.VMEM((t, g), BF16)],
        compiler_params=_cp(),
    )(z0, z0, du, pool_w, pool_scale)


def _conv_masks(t, cx):
    r = lax.broadcasted_iota(jnp.int32, (t, 1), 0)
    has_prev = jnp.where((r == 0) | (r == cx), 0.0, 1.0)
    has_next = jnp.where((r == cx - 1) | (r == t - 1), 0.0, 1.0)
    return has_prev, has_next


def mix_b_fwd(z0, conv_w, conv_b, cx):
    _, t, half_d = z0.shape
    gb = 128

    def body(bx_ref, bb_ref, bc_ref, bg_ref, w_ref, b_ref, u_ref):
        has_prev, has_next = _conv_masks(t, cx)
        tt = bc_ref[...] * bx_ref[...]
        prev = pltpu.roll(tt, 1, 0) * has_prev
        nxt = pltpu.roll(tt, t - 1, 0) * has_next
        cv = prev * w_ref[0:1, :] + tt * w_ref[1:2, :] + nxt * w_ref[2:3, :] + b_ref[...]
        u_ref[...] = (bb_ref[...] * cv * _silu(bg_ref[...])).astype(BF16)

    sec = lambda s: pl.BlockSpec((None, t, gb), lambda j: (s, 0, j))
    return pl.pallas_call(
        body, name="mix_b_fwd", grid=(half_d // gb,),
        in_specs=[sec(2), sec(3), sec(4), sec(5), pl.BlockSpec((3, gb), lambda j: (0, j)),
                  pl.BlockSpec((1, gb), lambda j: (0, j))],
        out_specs=pl.BlockSpec((t, gb), lambda j: (0, j)),
        out_shape=jax.ShapeDtypeStruct((t, half_d), BF16), compiler_params=_cp(),
    )(z0, z0, z0, z0, conv_w, conv_b)


def mix_b_bwd(z0, du, conv_w, conv_b, cx):
    _, t, half_d = z0.shape
    gb = 128
    off = half_d // gb

    def body(bx_ref, bb_ref, bc_ref, bg_ref, du_ref, w_ref, b_ref, dz_ref, dw_ref, db_ref):
        has_prev, has_next = _conv_masks(t, cx)
        bx, bb, bc, bg = bx_ref[...], bb_ref[...], bc_ref[...], bg_ref[...]
        duv = du_ref[...]
        tt = bc * bx
        prev = pltpu.roll(tt, 1, 0) * has_prev
        nxt = pltpu.roll(tt, t - 1, 0) * has_next
        w0, w1, w2 = w_ref[0:1, :], w_ref[1:2, :], w_ref[2:3, :]
        cv = prev * w0 + tt * w1 + nxt * w2 + b_ref[...]
        sg = _silu(bg)
        dz_ref[1] = (duv * cv * sg).astype(BF16)
        dz_ref[3] = (duv * bb * cv * _dsilu(bg)).astype(BF16)
        dcv = duv * bb * sg
        dw_ref[0:1, :] = jnp.sum(dcv * prev, axis=0, keepdims=True)
        dw_ref[1:2, :] = jnp.sum(dcv * tt, axis=0, keepdims=True)
        dw_ref[2:3, :] = jnp.sum(dcv * nxt, axis=0, keepdims=True)
        db_ref[...] = jnp.sum(dcv, axis=0, keepdims=True)
        dt = (pltpu.roll(dcv * has_prev, t - 1, 0) * w0 + dcv * w1
              + pltpu.roll(dcv * has_next, 1, 0) * w2)
        dz_ref[0] = (dt * bc).astype(BF16)
        dz_ref[2] = (dt * bx).astype(BF16)

    sec = lambda s: pl.BlockSpec((None, t, gb), lambda j: (s, 0, j))
    return pl.pallas_call(
        body, name="mix_b_bwd", grid=(half_d // gb,),
        in_specs=[sec(2), sec(3), sec(4), sec(5), pl.BlockSpec((t, gb), lambda j: (0, j + off)),
                  pl.BlockSpec((3, gb), lambda j: (0, j)), pl.BlockSpec((1, gb), lambda j: (0, j))],
        out_specs=[pl.BlockSpec((4, t, gb), lambda j: (0, 0, j)),
                   pl.BlockSpec((3, gb), lambda j: (0, j)), pl.BlockSpec((1, gb), lambda j: (0, j))],
        out_shape=[jax.ShapeDtypeStruct((4, t, half_d), BF16),
                   jax.ShapeDtypeStruct((3, half_d), F32), jax.ShapeDtypeStruct((1, half_d), F32)],
        compiler_params=_cp(),
    )(z0, z0, z0, z0, du, conv_w, conv_b)


def _lower_bound(lbl_ref, d):
    l0, l1, l2 = lbl_ref[d, 0:1, :], lbl_ref[d, 1:2, :], lbl_ref[d, 2:3, :]
    mx = jnp.maximum(jnp.maximum(l0, l1), l2)
    e0, e1, e2 = jnp.exp(l0 - mx), jnp.exp(l1 - mx), jnp.exp(l2 - mx)
    inv = 1.0 / (e0 + e1 + e2)
    return (e0 + e1) * inv, (e0 * inv, e1 * inv, e2 * inv)


def _chunk_consts(d):
    r = lax.broadcasted_iota(jnp.int32, (CHUNK, CHUNK), 0)
    c = lax.broadcasted_iota(jnp.int32, (CHUNK, CHUNK), 1)
    keep = (c <= r) if d == 0 else (c >= r)
    return jnp.where(keep, 1.0, 0.0).astype(F32), keep


def _chunk_of_step(s, d, nc, ncc):
    if d == 0:
        return s
    return jnp.where(s < ncc, ncc - 1 - s, nc - 1 + ncc - s)


def _chunk_terms(lfc, kc, qc, cum):
    bc = _dot(cum, lfc, precision=lax.Precision.HIGHEST)
    bl = jnp.sum(lfc, axis=0, keepdims=True)
    e = jnp.exp(bc)
    einv = jnp.exp(-bc)
    erem = jnp.exp(bl - bc)
    return e, einv, erem, jnp.exp(bl), qc * e, kc * einv, kc * erem


def hgrn_fwd(z1, lbl, onorm, cx):
    _, t, d = z1.shape
    seq = t - cx
    nc, ncc = t // CHUNK, cx // CHUNK

    def body(zf_ref, zb_ref, v_ref, q_ref, g_ref, lbl_ref, on_ref, o_ref, r_ref,
             lf_ref, k_ref, oacc_ref, st_ref):
        for dr, z_ref in ((0, zf_ref), (1, zb_ref)):
            lbv, _ = _lower_bound(lbl_ref, dr)
            z = z_ref[...]
            lf_ref[...] = jnp.log(lbv + (1.0 - lbv) * jax.nn.sigmoid(z))
            k_ref[...] = (1.0 - lbv) * jax.nn.sigmoid(-z)
            st_ref[...] = jnp.zeros_like(st_ref)
            cum, keep = _chunk_consts(dr)

            def step(s, carry, dr=dr, cum=cum, keep=keep):
                n = _chunk_of_step(s, dr, nc, ncc)
                rows = pl.ds(pl.multiple_of(n * CHUNK, CHUNK), CHUNK)
                vc = v_ref[rows, :].astype(BF16)
                _, _, _, dec, qd, ki, kd = _chunk_terms(lf_ref[rows, :], k_ref[rows, :], q_ref[rows, :], cum)
                qdb = qd.astype(BF16)
                a = jnp.where(keep, _dot(qdb, ki.astype(BF16), NT), 0.0)
                st = st_ref[...]
                oc = _dot(qdb, st.astype(BF16), NT) + _dot(a.astype(BF16), vc)
                st_ref[...] = st * dec + _dot(vc, kd.astype(BF16), TN)
                if dr == 0:
                    oacc_ref[rows, :] = oc
                else:
                    oacc_ref[rows, :] += oc
                return carry

            lax.fori_loop(0, nc, step, 0, unroll=4)

        o = oacc_ref[cx:, :]
        o_ref[...] = o
        rstd = lax.rsqrt(jnp.mean(o * o, axis=-1, keepdims=True) + EPS)
        r_ref[...] = (o * rstd * on_ref[...] * _silu(g_ref[cx:, :])).astype(BF16)

    sec = lambda s: pl.BlockSpec((None, t, HEAD), lambda h: (s, 0, h))
    col = pl.BlockSpec((seq, HEAD), lambda h: (0, h))
    return pl.pallas_call(
        body, name="hgrn_fwd", grid=(d // HEAD,),
        in_specs=[sec(0), sec(1), sec(2), sec(3), sec(4),
                  pl.BlockSpec((2, 3, HEAD), lambda h: (0, 0, h)), pl.BlockSpec((1, HEAD), lambda h: (0, h))],
        out_specs=[col, col],
        out_shape=[jax.ShapeDtypeStruct((seq, d), F32), jax.ShapeDtypeStruct((seq, d), BF16)],
        scratch_shapes=[pltpu.VMEM((t, HEAD), F32), pltpu.VMEM((t, HEAD), F32), pltpu.VMEM((t, HEAD), F32),
                        pltpu.VMEM((HEAD, HEAD), F32)],
        compiler_params=_cp(),
    )(z1, z1, z1, z1, z1, lbl, onorm)


def hgrn_bwd(z1, lbl, onorm, o, dr_out, cx):
    _, t, d = z1.shape
    seq = t - cx
    nc, ncc = t // CHUNK, cx // CHUNK

    def body(zf_ref, zb_ref, v_ref, q_ref, g_ref, lbl_ref, on_ref, o_ref, dr_ref,
             dz_ref, don_ref, dlb_ref,
             lf_ref, k_ref, do_ref, dq_ref, dv_ref, dk_ref, dlf_ref, ssc_ref, dst_ref):
        o = o_ref[...]
        g = g_ref[cx:, :]
        drv = dr_ref[...]
        onv = on_ref[...]
        rstd = lax.rsqrt(jnp.mean(o * o, axis=-1, keepdims=True) + EPS)
        ohat = o * rstd
        sg = _silu(g)
        don_ref[...] = jnp.sum(drv * ohat * sg, axis=0, keepdims=True)
        dz_ref[4, :cx, :] = jnp.zeros((cx, HEAD), BF16)
        dz_ref[4, cx:, :] = (drv * ohat * onv * _dsilu(g)).astype(BF16)
        dohat = drv * onv * sg
        do_ref[:cx, :] = jnp.zeros((cx, HEAD), F32)
        do_ref[cx:, :] = rstd * (dohat - ohat * jnp.mean(dohat * ohat, axis=-1, keepdims=True))

        for dr, z_ref in ((0, zf_ref), (1, zb_ref)):
            lbv, _ = _lower_bound(lbl_ref, dr)
            z = z_ref[...]
            lf_ref[...] = jnp.log(lbv + (1.0 - lbv) * jax.nn.sigmoid(z))
            k_ref[...] = (1.0 - lbv) * jax.nn.sigmoid(-z)
            cum, keep = _chunk_consts(dr)
            cum_t, _ = _chunk_consts(1 - dr)

            st_init = jnp.zeros((HEAD, HEAD), F32)

            def state_step(s, st, dr=dr, cum=cum):
                n = _chunk_of_step(s, dr, nc, ncc)
                rows = pl.ds(pl.multiple_of(n * CHUNK, CHUNK), CHUNK)
                ssc_ref[n] = st
                _, _, _, dec, _, _, kd = _chunk_terms(lf_ref[rows, :], k_ref[rows, :], q_ref[rows, :], cum)
                return st * dec + _dot(v_ref[rows, :].astype(BF16), kd.astype(BF16), TN)

            lax.fori_loop(0, nc, state_step, st_init, unroll=4)
            dst_ref[...] = jnp.zeros_like(dst_ref)

            def grad_step(s2, carry, dr=dr, cum=cum, cum_t=cum_t, keep=keep):
                n = _chunk_of_step(nc - 1 - s2, dr, nc, ncc)
                rows = pl.ds(pl.multiple_of(n * CHUNK, CHUNK), CHUNK)
                vc = v_ref[rows, :].astype(BF16)
                e, einv, erem, dec, qd, ki, kd = _chunk_terms(
                    lf_ref[rows, :], k_ref[rows, :], q_ref[rows, :], cum)
                qdb, kib, kdb = qd.astype(BF16), ki.astype(BF16), kd.astype(BF16)
                doc = do_ref[rows, :].astype(BF16)
                st0 = ssc_ref[n]
                dst = dst_ref[...]
                dstb = dst.astype(BF16)
                a = jnp.where(keep, _dot(qdb, kib, NT), 0.0).astype(BF16)
                da = jnp.where(keep, _dot(doc, vc, NT), 0.0).astype(BF16)
                dqd = _dot(doc, st0.astype(BF16)) + _dot(da, kib)
                dki = _dot(da, qdb, TN)
                dv = _dot(a, doc, TN) + _dot(kdb, dstb, NT)
                dkd = _dot(vc, dstb)
                ddec = jnp.sum(dst * st0, axis=0, keepdims=True)
                dst_ref[...] = _dot(doc, qdb, TN) + dst * dec
                dbc = dqd * qd - dki * ki - dkd * kd
                dbl = jnp.sum(dkd * kd, axis=0, keepdims=True) + ddec * dec
                dlf_ref[rows, :] = _dot(cum_t, dbc, precision=lax.Precision.HIGHEST) + dbl
                dk_ref[rows, :] = dki * einv + dkd * erem
                if dr == 0:
                    dq_ref[rows, :] = dqd * e
                    dv_ref[rows, :] = dv
                else:
                    dq_ref[rows, :] += dqd * e
                    dv_ref[rows, :] += dv
                return carry

            lax.fori_loop(0, nc, grad_step, 0, unroll=2)

            sig = jax.nn.sigmoid(z)
            one_lb = 1.0 - lbv
            f = lbv + one_lb * sig
            dlf = dlf_ref[...]
            dk = dk_ref[...]
            dsig = (dlf / f - dk) * one_lb
            dz_ref[dr] = (dsig * sig * (1.0 - sig)).astype(BF16)
            dlb_ref[dr:dr + 1, :] = jnp.sum((dlf / f - dk) * (1.0 - sig), axis=0, keepdims=True)

        dz_ref[2] = dv_ref[...].astype(BF16)
        dz_ref[3] = dq_ref[...].astype(BF16)

    sec = lambda s: pl.BlockSpec((None, t, HEAD), lambda h: (s, 0, h))
    col = pl.BlockSpec((seq, HEAD), lambda h: (0, h))
    tvec = pltpu.VMEM((t, HEAD), F32)
    return pl.pallas_call(
        body, name="hgrn_bwd", grid=(d // HEAD,),
        in_specs=[sec(0), sec(1), sec(2), sec(3), sec(4),
                  pl.BlockSpec((2, 3, HEAD), lambda h: (0, 0, h)), pl.BlockSpec((1, HEAD), lambda h: (0, h)),
                  col, col],
        out_specs=[pl.BlockSpec((5, t, HEAD), lambda h: (0, 0, h)),
                   pl.BlockSpec((1, HEAD), lambda h: (0, h)), pl.BlockSpec((2, HEAD), lambda h: (0, h))],
        out_shape=[jax.ShapeDtypeStruct((5, t, d), BF16), jax.ShapeDtypeStruct((1, d), F32),
                   jax.ShapeDtypeStruct((2, d), F32)],
        scratch_shapes=[tvec, tvec, tvec, tvec, tvec, tvec, tvec,
                        pltpu.VMEM((nc, HEAD, HEAD), F32), pltpu.VMEM((HEAD, HEAD), F32)],
        compiler_params=_cp(),
    )(z1, z1, z1, z1, z1, lbl, onorm, o, dr_out)


def _gates(z, lbv):
    e = jnp.exp(-jnp.abs(z))
    r = 1.0 / (1.0 + e)
    er = e * r
    pos = z >= 0.0
    sig = jnp.where(pos, r, er)
    nsig = jnp.where(pos, er, r)
    return sig, nsig, lbv + (1.0 - lbv) * sig


def _split3(x):
    hi = x.astype(BF16)
    r1 = x - hi.astype(F32)
    mid = r1.astype(BF16)
    lo = (r1 - mid.astype(F32)).astype(BF16)
    return jnp.concatenate([hi, mid, lo], axis=1)


def _cumsum_chunk(cum, x):
    y = _dot(cum, _split3(x))
    return y[:, :HEAD] + y[:, HEAD:2 * HEAD] + y[:, 2 * HEAD:]


def _chunk_rows(n):
    return pl.ds(pl.multiple_of(n * CHUNK, CHUNK), CHUNK)


def _group(nc):
    return next(u for u in (4, 3, 2, 1) if nc % u == 0)


def _decay_pass(lf_ref, bc_ref, dec_ref, cum, nc):
    grp = _group(nc)

    def step(m, carry):
        ns = [m * grp + u for u in range(grp)]
        lfc = [lf_ref[_chunk_rows(n), :] for n in ns]
        bc = [_cumsum_chunk(cum, x) for x in lfc]
        for u, n in enumerate(ns):
            bc_ref[_chunk_rows(n), :] = bc[u]
            dec_ref[n] = jnp.broadcast_to(jnp.exp(jnp.sum(lfc[u], axis=0, keepdims=True)), (8, HEAD))
        return carry

    lax.fori_loop(0, nc // grp, step, 0)


def hgrn_fwd(z1, lbl, onorm, cx):
    _, t, d = z1.shape
    seq = t - cx
    nc, ncc = t // CHUNK, cx // CHUNK

    grp = _group(nc)

    def body(zf_ref, zb_ref, v_ref, q_ref, g_ref, lbl_ref, on_ref, o_ref, r_ref,
             lf_ref, k_ref, bc_ref, dec_ref, qd_ref, ki_ref, oacc_ref, ds_ref):
        for dr, z_ref in ((0, zf_ref), (1, zb_ref)):
            lbv, _ = _lower_bound(lbl_ref, dr)
            _, nsig, f = _gates(z_ref[...], lbv)
            lf_ref[...] = jnp.log(f)
            k_ref[...] = (1.0 - lbv) * nsig
            cum, keep = _chunk_consts(dr)
            _decay_pass(lf_ref, bc_ref, dec_ref, cum.astype(BF16), nc)
            bc = bc_ref[...]
            qd_ref[...] = (q_ref[...] * jnp.exp(bc)).astype(BF16)
            ki_ref[...] = (k_ref[...] * jnp.exp(-bc)).astype(BF16)

            def local_step(m, carry, dr=dr, keep=keep):
                ns = [m * grp + u for u in range(grp)]
                rows = [_chunk_rows(n) for n in ns]
                qd = [qd_ref[r, :] for r in rows]
                ki = [ki_ref[r, :] for r in rows]
                vc = [v_ref[r, :].astype(BF16) for r in rows]
                sc = [_dot(qd[u], ki[u], NT) for u in range(grp)]
                inc = [_dot(vc[u], ki[u], TN) for u in range(grp)]
                a = [jnp.where(keep, s, 0.0).astype(BF16) for s in sc]
                intra = [_dot(a[u], vc[u]) for u in range(grp)]
                for u in range(grp):
                    ds_ref[ns[u]] = inc[u] * dec_ref[ns[u]][0:1, :]
                    if dr == 0:
                        oacc_ref[rows[u], :] = intra[u]
                    else:
                        oacc_ref[rows[u], :] += intra[u]
                return carry

            lax.fori_loop(0, nc // grp, local_step, 0)

            def state_step(m, st, dr=dr):
                ns = [_chunk_of_step(m * grp + u, dr, nc, ncc) for u in range(grp)]
                rows = [_chunk_rows(n) for n in ns]
                sts = []
                for n in ns:
                    sts.append(st.astype(BF16))
                    st = st * dec_ref[n][0:1, :] + ds_ref[n]
                inter = [_dot(qd_ref[rows[u], :], sts[u], NT) for u in range(grp)]
                for u in range(grp):
                    oacc_ref[rows[u], :] += inter[u]
                return st

            lax.fori_loop(0, nc // grp, state_step, jnp.zeros((HEAD, HEAD), F32))

        o = oacc_ref[cx:, :]
        o_ref[...] = o
        rstd = lax.rsqrt(jnp.mean(o * o, axis=-1, keepdims=True) + EPS)
        r_ref[...] = (o * rstd * on_ref[...] * _silu(g_ref[cx:, :])).astype(BF16)

    sec = lambda s: pl.BlockSpec((None, t, HEAD), lambda h: (s, 0, h))
    col = pl.BlockSpec((seq, HEAD), lambda h: (0, h))
    tf32, tb16 = pltpu.VMEM((t, HEAD), F32), pltpu.VMEM((t, HEAD), BF16)
    return pl.pallas_call(
        body, name="hgrn_fwd", grid=(d // HEAD,),
        in_specs=[sec(0), sec(1), sec(2), sec(3), sec(4),
                  pl.BlockSpec((2, 3, HEAD), lambda h: (0, 0, h)), pl.BlockSpec((1, HEAD), lambda h: (0, h))],
        out_specs=[col, col],
        out_shape=[jax.ShapeDtypeStruct((seq, d), F32), jax.ShapeDtypeStruct((seq, d), BF16)],
        scratch_shapes=[tf32, tf32, tf32, pltpu.VMEM((nc, 8, HEAD), F32), tb16, tb16, tf32,
                        pltpu.VMEM((nc, HEAD, HEAD), F32)],
        compiler_params=_cp(),
    )(z1, z1, z1, z1, z1, lbl, onorm)


def hgrn_bwd(z1, lbl, onorm, o, dr_out, cx):
    _, t, d = z1.shape
    seq = t - cx
    nc, ncc = t // CHUNK, cx // CHUNK

    grp2 = 2 if nc % 2 == 0 else 1
    grp = _group(nc)

    def body(zf_ref, zb_ref, v_ref, q_ref, g_ref, lbl_ref, on_ref, o_ref, dr_ref,
             dz_ref, don_ref, dlb_ref,
             lf_ref, k_ref, bc_ref, dec_ref, qd_ref, ki_ref, do_ref,
             dqd_ref, dki_ref, dq_ref, dv_ref, ds_ref, dsl_ref):
        o = o_ref[...]
        g = g_ref[cx:, :]
        drv = dr_ref[...]
        onv = on_ref[...]
        rstd = lax.rsqrt(jnp.mean(o * o, axis=-1, keepdims=True) + EPS)
        ohat = o * rstd
        sg = _silu(g)
        don_ref[...] = jnp.sum(drv * ohat * sg, axis=0, keepdims=True)
        dz_ref[4, :cx, :] = jnp.zeros((cx, HEAD), BF16)
        dz_ref[4, cx:, :] = (drv * ohat * onv * _dsilu(g)).astype(BF16)
        dohat = drv * onv * sg
        do_ref[:cx, :] = jnp.zeros((cx, HEAD), BF16)
        do_ref[cx:, :] = (rstd * (dohat - ohat * jnp.mean(dohat * ohat, axis=-1, keepdims=True))).astype(BF16)

        for dr, z_ref in ((0, zf_ref), (1, zb_ref)):
            lbv, _ = _lower_bound(lbl_ref, dr)
            _, nsig, f = _gates(z_ref[...], lbv)
            lf_ref[...] = jnp.log(f)
            k_ref[...] = (1.0 - lbv) * nsig
            cum, keep = _chunk_consts(dr)
            cum_t = _chunk_consts(1 - dr)[0].astype(BF16)
            _decay_pass(lf_ref, bc_ref, dec_ref, cum.astype(BF16), nc)
            bc = bc_ref[...]
            qd_ref[...] = (q_ref[...] * jnp.exp(bc)).astype(BF16)
            ki_ref[...] = (k_ref[...] * jnp.exp(-bc)).astype(BF16)

            def local_step(m, carry, dr=dr, keep=keep):
                ns = [m * grp + u for u in range(grp)]
                rows = [_chunk_rows(n) for n in ns]
                rng = range(grp)
                qd = [qd_ref[r, :] for r in rows]
                ki = [ki_ref[r, :] for r in rows]
                doc = [do_ref[r, :] for r in rows]
                vc = [v_ref[r, :].astype(BF16) for r in rows]
                sc = [_dot(qd[u], ki[u], NT) for u in rng]
                dsc = [_dot(doc[u], vc[u], NT) for u in rng]
                inc = [_dot(vc[u], ki[u], TN) for u in rng]
                dinc = [_dot(doc[u], qd[u], TN) for u in rng]
                a = [jnp.where(keep, s, 0.0).astype(BF16) for s in sc]
                da = [jnp.where(keep, s, 0.0).astype(BF16) for s in dsc]
                dqd = [_dot(da[u], ki[u]) for u in rng]
                dki = [_dot(da[u], qd[u], TN) for u in rng]
                dv = [_dot(a[u], doc[u], TN) for u in rng]
                for u in rng:
                    ds_ref[ns[u]] = inc[u] * dec_ref[ns[u]][0:1, :]
                    dsl_ref[ns[u]] = dinc[u]
                    dqd_ref[rows[u], :] = dqd[u]
                    dki_ref[rows[u], :] = dki[u]
                    if dr == 0:
                        dv_ref[rows[u], :] = dv[u]
                    else:
                        dv_ref[rows[u], :] += dv[u]
                return carry

            lax.fori_loop(0, nc // grp, local_step, 0)

            def state_step(s, st, dr=dr):
                n = _chunk_of_step(s, dr, nc, ncc)
                inc = ds_ref[n]
                ds_ref[n] = st
                return st * dec_ref[n][0:1, :] + inc

            lax.fori_loop(0, nc, state_step, jnp.zeros((HEAD, HEAD), F32), unroll=4)

            def dstate_step(s, dst, dr=dr):
                n = _chunk_of_step(nc - 1 - s, dr, nc, ncc)
                inc = dsl_ref[n]
                dsl_ref[n] = dst
                return inc + dst * dec_ref[n][0:1, :]

            lax.fori_loop(0, nc, dstate_step, jnp.zeros((HEAD, HEAD), F32), unroll=4)

            def grad_step(m, carry, dr=dr, cum_t=cum_t):
                ns = [m * grp2 + u for u in range(grp2)]
                rows = [_chunk_rows(n) for n in ns]
                rng = range(grp2)
                st0 = [ds_ref[n] for n in ns]
                dst = [dsl_ref[n] for n in ns]
                dstb = [x.astype(BF16) for x in dst]
                dec = [dec_ref[n][0:1, :] for n in ns]
                doc = [do_ref[r, :] for r in rows]
                vc = [v_ref[r, :].astype(BF16) for r in rows]
                e = [jnp.exp(bc_ref[r, :]) for r in rows]
                einv = [jnp.exp(-bc_ref[r, :]) for r in rows]
                qd = [q_ref[rows[u], :] * e[u] for u in rng]
                ki = [k_ref[rows[u], :] * einv[u] for u in rng]
                kd = [ki[u] * dec[u] for u in rng]
                dqd_st = [_dot(doc[u], st0[u].astype(BF16)) for u in rng]
                dkd = [_dot(vc[u], dstb[u]) for u in rng]
                dv_st = [_dot(kd[u].astype(BF16), dstb[u], NT) for u in rng]
                dqd = [dqd_ref[rows[u], :] + dqd_st[u] for u in rng]
                dki = [dki_ref[r, :] for r in rows]
                dbc = [dqd[u] * qd[u] - dki[u] * ki[u] - dkd[u] * kd[u] for u in rng]
                cs = [_cumsum_chunk(cum_t, x) for x in dbc]
                for u in rng:
                    ddec = jnp.sum(dst[u] * st0[u], axis=0, keepdims=True)
                    dbl = jnp.sum(dkd[u] * kd[u], axis=0, keepdims=True) + ddec * dec[u]
                    dv_ref[rows[u], :] += dv_st[u]
                    dqd_ref[rows[u], :] = cs[u] + dbl
                    dki_ref[rows[u], :] = dki[u] * einv[u] + dkd[u] * (einv[u] * dec[u])
                    if dr == 0:
                        dq_ref[rows[u], :] = dqd[u] * e[u]
                    else:
                        dq_ref[rows[u], :] += dqd[u] * e[u]
                return carry

            lax.fori_loop(0, nc // grp2, grad_step, 0)

            sig, nsig, f = _gates(z_ref[...], lbv)
            common = (dqd_ref[...] / f - dki_ref[...]) * nsig
            dz_ref[dr] = (common * ((1.0 - lbv) * sig)).astype(BF16)
            dlb_ref[dr:dr + 1, :] = jnp.sum(common, axis=0, keepdims=True)

        dz_ref[2] = dv_ref[...].astype(BF16)
        dz_ref[3] = dq_ref[...].astype(BF16)

    sec = lambda s: pl.BlockSpec((None, t, HEAD), lambda h: (s, 0, h))
    col = pl.BlockSpec((seq, HEAD), lambda h: (0, h))
    tf32, tb16 = pltpu.VMEM((t, HEAD), F32), pltpu.VMEM((t, HEAD), BF16)
    states = pltpu.VMEM((nc, HEAD, HEAD), F32)
    return pl.pallas_call(
        body, name="hgrn_bwd", grid=(d // HEAD,),
        in_specs=[sec(0), sec(1), sec(2), sec(3), sec(4),
                  pl.BlockSpec((2, 3, HEAD), lambda h: (0, 0, h)), pl.BlockSpec((1, HEAD), lambda h: (0, h)),
                  col, col],
        out_specs=[pl.BlockSpec((5, t, HEAD), lambda h: (0, 0, h)),
                   pl.BlockSpec((1, HEAD), lambda h: (0, h)), pl.BlockSpec((2, HEAD), lambda h: (0, h))],
        out_shape=[jax.ShapeDtypeStruct((5, t, d), BF16), jax.ShapeDtypeStruct((1, d), F32),
                   jax.ShapeDtypeStruct((2, d), F32)],
        scratch_shapes=[tf32, tf32, tf32, pltpu.VMEM((nc, 8, HEAD), F32), tb16, tb16, tb16,
                        tf32, tf32, tf32, tf32, states, states],
        compiler_params=_cp(),
    )(z1, z1, z1, z1, z1, lbl, onorm, o, dr_out)


def _place():
    x, y, c = lax.axis_index("x"), lax.axis_index("y"), lax.axis_index("c")
    chips = [(1 - x, y), (x, 1 - y), (1 - x, 1 - y)]
    return x, y, c, chips


def allgather_shards(bufs):
    n = len(bufs)

    def body(*refs):
        outs = refs[n:2 * n]
        done_ref, send_sems, recv_sems = refs[2 * n:]
        done_ref[...] = jnp.zeros((8, 128), F32)
        x, y, c, chips = _place()
        p = 2 * x + y
        half = [pl.ds(c * (s.shape[1] // 2), s.shape[1] // 2) for s in bufs]
        other = [pl.ds((1 - c) * (s.shape[1] // 2), s.shape[1] // 2) for s in bufs]

        def remote(i, k, src, dst, to):
            return pltpu.make_async_remote_copy(src_ref=src, dst_ref=dst, send_sem=send_sems.at[6 * i + k],
                                                recv_sem=recv_sems.at[6 * i + k], device_id=to, device_id_type=MESH)

        sends = []
        for i in range(n):
            for j, chip in enumerate(chips):
                mine = outs[i].at[p, half[i]]
                cp = remote(i, j, mine, mine, (*chip, c))
                cp.start()
                sends.append(cp)
        for i in range(n):
            for j, chip in enumerate(chips):
                landed = outs[i].at[2 * chip[0] + chip[1], half[i]]
                remote(i, j, landed, landed, (x, y, c)).wait_recv()
                cp = remote(i, 3 + j, landed, landed, (x, y, 1 - c))
                cp.start()
                sends.append(cp)
        for i in range(n):
            for j, chip in enumerate(chips):
                landed = outs[i].at[2 * chip[0] + chip[1], other[i]]
                remote(i, 3 + j, landed, landed, (x, y, c)).wait_recv()
        for cp in sends:
            cp.wait_send()

    return pl.pallas_call(
        body, name="allgather_shards",
        in_specs=[ANY] * n, out_specs=[ANY] * n + [VMEM],
        out_shape=[jax.ShapeDtypeStruct(s.shape, s.dtype) for s in bufs] + [jax.ShapeDtypeStruct((8, 128), F32)],
        input_output_aliases={i: i for i in range(n)},
        scratch_shapes=[pltpu.SemaphoreType.DMA((6 * n,)), pltpu.SemaphoreType.DMA((6 * n,))],
        compiler_params=pltpu.CompilerParams(has_side_effects=True),
    )(*bufs)


def exchange_halves(grads):
    n = len(grads)

    def body(*refs):
        ins, outs = refs[:n], refs[n:2 * n]
        send_sems, recv_sems = refs[2 * n:]
        x, y, c, _ = _place()
        copies = []
        for i in range(n):
            hr = grads[i].shape[1] // 2
            cp = pltpu.make_async_remote_copy(
                src_ref=ins[i].at[:, pl.ds((1 - c) * hr, hr)], dst_ref=outs[i],
                send_sem=send_sems.at[i], recv_sem=recv_sems.at[i],
                device_id=(x, y, 1 - c), device_id_type=MESH)
            cp.start()
            copies.append(cp)
        for cp in copies:
            cp.wait()

    return pl.pallas_call(
        body, name="exchange_halves",
        in_specs=[ANY] * n, out_specs=[ANY] * n,
        out_shape=[jax.ShapeDtypeStruct((4, g.shape[1] // 2, g.shape[2]), g.dtype) for g in grads],
        scratch_shapes=[pltpu.SemaphoreType.DMA((n,)), pltpu.SemaphoreType.DMA((n,))],
        compiler_params=pltpu.CompilerParams(has_side_effects=True),
    )(*grads)


def pair_sum(grad, got, core):
    _, r, cc = grad.shape
    hr = r // 2
    tr = 256 if hr % 256 == 0 else hr

    def body(core_ref, a_ref, b_ref, s_ref, sb_ref):
        s = a_ref[...] + b_ref[...]
        s_ref[...] = s
        sb_ref[...] = s.astype(BF16)

    nb = hr // tr
    grid_spec = pltpu.PrefetchScalarGridSpec(
        num_scalar_prefetch=1, grid=(4, nb),
        in_specs=[pl.BlockSpec((None, tr, cc), lambda qi, i, core_ref: (qi, core_ref[0] * nb + i, 0)),
                  pl.BlockSpec((None, tr, cc), lambda qi, i, core_ref: (qi, i, 0))],
        out_specs=[pl.BlockSpec((None, tr, cc), lambda qi, i, core_ref: (qi, i, 0)),
                   pl.BlockSpec((None, tr, cc), lambda qi, i, core_ref: (qi, i, 0))])
    return pl.pallas_call(
        body, name="pair_sum", grid_spec=grid_spec,
        out_shape=[jax.ShapeDtypeStruct((4, hr, cc), F32), jax.ShapeDtypeStruct((4, hr, cc), BF16)],
        compiler_params=_cp(),
    )(core, grad, got)


def scatter_to_owners(parts):
    n = len(parts)

    def body(*refs):
        ins, outs = refs[:n], refs[n:2 * n]
        send_sems, recv_sems = refs[2 * n:]
        x, y, c, chips = _place()
        copies = []
        for i in range(n):
            for j, chip in enumerate(chips):
                cp = pltpu.make_async_remote_copy(
                    src_ref=ins[i].at[2 * chip[0] + chip[1]], dst_ref=outs[i].at[j],
                    send_sem=send_sems.at[3 * i + j], recv_sem=recv_sems.at[3 * i + j],
                    device_id=(*chip, c), device_id_type=MESH)
                cp.start()
                copies.append(cp)
        for cp in copies:
            cp.wait()

    return pl.pallas_call(
        body, name="scatter_to_owners",
        in_specs=[ANY] * n, out_specs=[ANY] * n,
        out_shape=[jax.ShapeDtypeStruct((3,) + p.shape[1:], p.dtype) for p in parts],
        scratch_shapes=[pltpu.SemaphoreType.DMA((3 * n,)), pltpu.SemaphoreType.DMA((3 * n,))],
        compiler_params=pltpu.CompilerParams(has_side_effects=True),
    )(*parts)


def owner_sum(own, got, chip_core):
    _, hr, cc = own.shape
    tr = 256 if hr % 256 == 0 else hr
    nb = hr // tr

    def body(cc_ref, a_ref, b_ref, o_ref):
        s = a_ref[...] + b_ref[0].astype(F32)
        s = s + b_ref[1].astype(F32)
        o_ref[...] = s + b_ref[2].astype(F32)

    grid_spec = pltpu.PrefetchScalarGridSpec(
        num_scalar_prefetch=1, grid=(nb,),
        in_specs=[pl.BlockSpec((None, tr, cc), lambda i, cc_ref: (cc_ref[0], i, 0)),
                  pl.BlockSpec((3, tr, cc), lambda i, cc_ref: (0, i, 0))],
        out_specs=pl.BlockSpec((tr, cc), lambda i, cc_ref: (cc_ref[1] * nb + i, 0)))
    return pl.pallas_call(
        body, name="owner_sum", grid_spec=grid_spec,
        out_shape=jax.ShapeDtypeStruct((2 * hr, cc), F32), compiler_params=_cp(),
    )(chip_core, own, got)


def share_halves(bufs):
    n = len(bufs)

    def body(*refs):
        outs = refs[n:2 * n]
        send_sems, recv_sems = refs[2 * n:]
        x, y, c, _ = _place()
        copies = []
        for i in range(n):
            hr = bufs[i].shape[0] // 2
            mine = outs[i].at[pl.ds(c * hr, hr)]
            cp = pltpu.make_async_remote_copy(
                src_ref=mine, dst_ref=mine, send_sem=send_sems.at[i], recv_sem=recv_sems.at[i],
                device_id=(x, y, 1 - c), device_id_type=MESH)
            cp.start()
            copies.append((cp, outs[i].at[pl.ds((1 - c) * hr, hr)]))
        for i, (cp, theirs) in enumerate(copies):
            cp.wait_send()
            pltpu.make_async_remote_copy(
                src_ref=theirs, dst_ref=theirs, send_sem=send_sems.at[i], recv_sem=recv_sems.at[i],
                device_id=(x, y, c), device_id_type=MESH).wait_recv()

    return pl.pallas_call(
        body, name="share_halves",
        in_specs=[ANY] * n, out_specs=[ANY] * n,
        out_shape=[jax.ShapeDtypeStruct(b.shape, b.dtype) for b in bufs],
        input_output_aliases={i: i for i in range(n)},
        scratch_shapes=[pltpu.SemaphoreType.DMA((n,)), pltpu.SemaphoreType.DMA((n,))],
        compiler_params=pltpu.CompilerParams(has_side_effects=True),
    )(*bufs)


def allgather8(v, name):
    r, n = v.shape

    def body(v_ref, out_ref, send_sems, recv_sems):
        x, y, c, _ = _place()
        me = 4 * x + 2 * y + c
        out_ref[me] = v_ref[...]

        def copy(k, slot, to):
            return pltpu.make_async_remote_copy(
                src_ref=v_ref, dst_ref=out_ref.at[slot], send_sem=send_sems.at[k - 1],
                recv_sem=recv_sems.at[k - 1], device_id=to, device_id_type=MESH)

        peers = []
        for k in range(1, 8):
            px = 1 - x if (k >> 2) & 1 else x
            py = 1 - y if (k >> 1) & 1 else y
            pc = 1 - c if k & 1 else c
            peers.append((px, py, pc))
            copy(k, me, (px, py, pc)).start()
        for k, (px, py, pc) in enumerate(peers, start=1):
            copy(k, 4 * px + 2 * py + pc, (x, y, c)).wait_recv()
        for k, peer in enumerate(peers, start=1):
            copy(k, me, peer).wait_send()

    return pl.pallas_call(
        body, name=name, in_specs=[VMEM], out_specs=VMEM,
        out_shape=jax.ShapeDtypeStruct((8, r, n), v.dtype),
        scratch_shapes=[pltpu.SemaphoreType.DMA((7,)), pltpu.SemaphoreType.DMA((7,))],
        compiler_params=_cp(has_side_effects=True),
    )(v)


HBM = pl.BlockSpec(memory_space=pltpu.HBM)
SEM = pl.BlockSpec(memory_space=pltpu.SEMAPHORE)
DATAFLOW = pltpu.SideEffectType.DATAFLOW_SIDE_EFFECTING


def _descriptors(plan, refs, send_sems, recv_sems, arrivals=True):
    x, y, c, _ = _place()
    sends, recvs = plan(refs)
    out = [pltpu.make_async_remote_copy(src_ref=src, dst_ref=dst, send_sem=send_sems.at[k],
                                        recv_sem=recv_sems.at[k], device_id=to, device_id_type=MESH)
           for k, (src, dst, to) in enumerate(sends)]
    if not arrivals:
        return out, []
    inn = [pltpu.make_async_remote_copy(src_ref=land, dst_ref=land, send_sem=send_sems.at[k],
                                        recv_sem=recv_sems.at[k], device_id=(x, y, c), device_id_type=MESH)
           for k, land in enumerate(recvs)]
    return out, inn


def copies_start(name, arrays, n_copies, plan, after):
    na = len(arrays)

    def body(*refs):
        out, _ = _descriptors(plan, refs[:na], refs[na + 1], refs[na + 2], arrivals=False)
        for cp in out:
            cp.start()
        refs[-1][...] = jnp.zeros((8, 128), F32)

    res = pl.pallas_call(
        body, name=name,
        out_shape=(pltpu.SemaphoreType.DMA((n_copies,)), pltpu.SemaphoreType.DMA((n_copies,)),
                   *[pltpu.HBM(a.shape, a.dtype) for a in arrays], jax.ShapeDtypeStruct((8, 128), F32)),
        in_specs=[HBM] * na + [ANY], out_specs=(SEM, SEM, *[HBM] * na, VMEM),
        input_output_aliases={i: i + 2 for i in range(na)},
        compiler_params=pltpu.CompilerParams(has_side_effects=DATAFLOW),
    )(*[pltpu.with_memory_space_constraint(a, pltpu.HBM) for a in arrays], after)
    return res[0], res[1], list(res[2:2 + na]), res[-1]


def copies_wait(name, started, plan, after):
    send_sems, recv_sems, arrays, _ = started
    na = len(arrays)

    def body(*refs):
        out, inn = _descriptors(plan, refs[:na], refs[na], refs[na + 1])
        for cp in out:
            cp.wait_send()
        for cp in inn:
            cp.wait_recv()
        refs[-1][...] = jnp.zeros((8, 128), F32)

    res = pl.pallas_call(
        body, name=name,
        out_shape=(*[pltpu.HBM(a.shape, a.dtype) for a in arrays], jax.ShapeDtypeStruct((8, 128), F32)),
        in_specs=[HBM] * na + [SEM, SEM, ANY], out_specs=(*[HBM] * na, VMEM),
        input_output_aliases={i: i for i in range(na)},
        compiler_params=pltpu.CompilerParams(has_side_effects=DATAFLOW),
    )(*arrays, send_sems, recv_sems, after)
    return list(res[:na]), res[-1]


def _rows_half(r, c):
    return pl.ds(c * (r // 2), r // 2), pl.ds((1 - c) * (r // 2), r // 2)


def plan_gather_ici(refs):
    x, y, c, chips = _place()
    p = 2 * x + y
    sends, recvs = [], []
    for buf in refs:
        mine, _ = _rows_half(buf.shape[1], c)
        for chip in chips:
            sends.append((buf.at[p, mine], buf.at[p, mine], (*chip, c)))
            recvs.append(buf.at[2 * chip[0] + chip[1], mine])
    return sends, recvs


def plan_gather_d2d(refs):
    x, y, c, chips = _place()
    sends, recvs = [], []
    for buf in refs:
        mine, theirs = _rows_half(buf.shape[1], c)
        for chip in chips:
            slot = 2 * chip[0] + chip[1]
            sends.append((buf.at[slot, mine], buf.at[slot, mine], (x, y, 1 - c)))
            recvs.append(buf.at[slot, theirs])
    return sends, recvs


def plan_exchange(refs):
    x, y, c, _ = _place()
    n = len(refs) // 2
    sends, recvs = [], []
    for grad, land in zip(refs[:n], refs[n:]):
        _, theirs = _rows_half(grad.shape[1], c)
        sends.append((grad.at[:, theirs], land, (x, y, 1 - c)))
        recvs.append(land)
    return sends, recvs


def plan_scatter(refs):
    x, y, c, chips = _place()
    n = len(refs) // 2
    sends, recvs = [], []
    for part, land in zip(refs[:n], refs[n:]):
        for j, chip in enumerate(chips):
            sends.append((part.at[2 * chip[0] + chip[1]], land.at[j], (*chip, c)))
            recvs.append(land.at[j])
    return sends, recvs


def plan_share(refs):
    x, y, c, _ = _place()
    sends, recvs = [], []
    for buf in refs:
        mine, theirs = _rows_half(buf.shape[0], c)
        sends.append((buf.at[mine], buf.at[mine], (x, y, 1 - c)))
        recvs.append(buf.at[theirs])
    return sends, recvs


def put_in_slot(w, chip, dtype, name):
    r, c = w.shape
    tr = 256 if r % 256 == 0 else r

    def body(chip_ref, w_ref, o_ref):
        o_ref[...] = w_ref[...].astype(dtype)

    grid_spec = pltpu.PrefetchScalarGridSpec(
        num_scalar_prefetch=1, grid=(r // tr,),
        in_specs=[pl.BlockSpec((tr, c), lambda i, chip_ref: (i, 0))],
        out_specs=pl.BlockSpec((None, tr, c), lambda i, chip_ref: (chip_ref[0], i, 0)))
    return pl.pallas_call(body, name=name, grid_spec=grid_spec,
                          out_shape=jax.ShapeDtypeStruct((4, r, c), dtype), compiler_params=_cp())(chip, w)


def ada_fwd(s_in, ada_w, ada_b, tn):
    nl, d, ws = ada_w.shape

    def body(s_ref, w_ref, b_ref, so_ref, mod_ref):
        s = _silu(s_ref[...])
        so_ref[...] = s
        mod_ref[...] = _dot(s.astype(BF16), w_ref[...].astype(BF16)) + b_ref[...]

    return pl.pallas_call(
        body, name="ada_fwd", grid=(nl, ws // tn),
        in_specs=[pl.BlockSpec((16, d), lambda l, j: (0, 0)),
                  pl.BlockSpec((None, d, tn), lambda l, j: (l, 0, j)),
                  pl.BlockSpec((None, 1, tn), lambda l, j: (l, 0, j))],
        out_specs=[pl.BlockSpec((16, d), lambda l, j: (0, 0)),
                   pl.BlockSpec((None, 16, tn), lambda l, j: (l, 0, j))],
        out_shape=[jax.ShapeDtypeStruct((16, d), F32), jax.ShapeDtypeStruct((nl, 16, ws), F32)],
        compiler_params=_cp(),
    )(s_in, ada_w, ada_b)


def _adamw_math(w, g, m, v):
    m = ADAM_B1 * m + (1.0 - ADAM_B1) * g
    v = ADAM_B2 * v + (1.0 - ADAM_B2) * (g * g)
    m_hat = m / (1.0 - ADAM_B1 ** ADAM_STEP)
    v_hat = v / (1.0 - ADAM_B2 ** ADAM_STEP)
    delta = -ADAM_LR * (m_hat / (jnp.sqrt(v_hat) + ADAM_EPS) + ADAM_WD * w)
    return delta, m, v


def ada_bwd_adamw(s, dm, w, m, v):
    nl, d, ws = w.shape
    tr = 256 if d % 256 == 0 else 128

    def body(s_ref, dm_ref, w_ref, m_ref, v_ref, g_ref, dl_ref, mo_ref, vo_ref, dc_ref):
        dmv = dm_ref[...].astype(BF16)
        wv = w_ref[...]
        g = _dot(s_ref[...].astype(BF16), dmv, TN)
        g_ref[...] = g
        dl_ref[...], mo_ref[...], vo_ref[...] = _adamw_math(wv, g, m_ref[...], v_ref[...])
        dc_ref[...] = _dot(dmv[8:16, :], wv.astype(BF16), NT)

    wblk = pl.BlockSpec((None, tr, ws), lambda l, i: (l, i, 0))
    wshape = jax.ShapeDtypeStruct((nl, d, ws), F32)
    return pl.pallas_call(
        body, name="ada_bwd_adamw", grid=(nl, d // tr),
        in_specs=[pl.BlockSpec((16, tr), lambda l, i: (0, i)),
                  pl.BlockSpec((None, 16, ws), lambda l, i: (l, 0, 0)), wblk, wblk, wblk],
        out_specs=[wblk, wblk, wblk, wblk, pl.BlockSpec((None, 8, tr), lambda l, i: (l, 0, i))],
        out_shape=[wshape, wshape, wshape, wshape, jax.ShapeDtypeStruct((nl, 8, d), F32)],
        compiler_params=_cp(),
    )(s, dm, w, m, v)


def adamw(w, g, m, v, name):
    r, c = w.shape
    tr = 256 if r % 256 == 0 else r

    def body(w_ref, g_ref, m_ref, v_ref, dl_ref, mo_ref, vo_ref):
        dl_ref[...], mo_ref[...], vo_ref[...] = _adamw_math(w_ref[...], g_ref[...], m_ref[...], v_ref[...])

    blk = pl.BlockSpec((tr, c), lambda i: (i, 0))
    shape = jax.ShapeDtypeStruct((r, c), F32)
    return pl.pallas_call(body, name=name, grid=(r // tr,), in_specs=[blk] * 4, out_specs=[blk] * 3,
                          out_shape=[shape] * 3, compiler_params=_cp())(w, g, m, v)


SMALL_ROWS = 24
ROW_MOD = 10


def small_reduce(gathered):
    _, rows, d = gathered.shape

    def body(g_ref, o_ref):
        tot = g_ref[0]
        for b in range(1, 8):
            tot = tot + g_ref[b]
        o_ref[0:rows, :] = tot
        for layer in range(2):
            lat = ROW_MOD + 6 * layer
            o_ref[24 + 3 * layer:27 + 3 * layer, :] = tot[lat:lat + 3, :] + tot[lat + 3:lat + 6, :]
        o_ref[30:32, :] = jnp.zeros((2, d), F32)

    return pl.pallas_call(body, name="small_reduce", in_specs=[VMEM], out_specs=VMEM,
                          out_shape=jax.ShapeDtypeStruct((32, d), F32), compiler_params=_cp())(gathered)


def lb_logits_grad(lbl, dlb):
    _, _, n = lbl.shape

    def body(l_ref, d_ref, o_ref):
        for dr in range(2):
            _, (p0, p1, p2) = _lower_bound(l_ref, dr)
            dv = d_ref[dr:dr + 1, :]
            o_ref[dr, 0:1, :] = p0 * p2 * dv
            o_ref[dr, 1:2, :] = p1 * p2 * dv
            o_ref[dr, 2:3, :] = -p2 * (p0 + p1) * dv

    return pl.pallas_call(body, name="lb_logits_grad", in_specs=[VMEM, VMEM], out_specs=VMEM,
                          out_shape=jax.ShapeDtypeStruct((2, 3, n), F32), compiler_params=_cp())(lbl, dlb)


def c_ctx_grad(parts, c_ctx):
    d = c_ctx.shape[1]

    def body(p_ref, c_ref, o_ref):
        tot = p_ref[0, 0:1, :]
        for chip in range(1, 4):
            tot = tot + p_ref[2 * chip, 0:1, :]
        o_ref[...] = tot * _dsilu(c_ref[...])

    return pl.pallas_call(body, name="c_ctx_grad", in_specs=[VMEM, VMEM], out_specs=VMEM,
                          out_shape=jax.ShapeDtypeStruct((1, d), F32), compiler_params=_cp())(parts, c_ctx)


def _reduce_scatter(grads, core, chip_core):
    got = exchange_halves(grads)
    sums = [pair_sum(g, r, core) for g, r in zip(grads, got)]
    recv = scatter_to_owners([sb for _, sb in sums])
    reduced = [owner_sum(s, r, chip_core) for (s, _), r in zip(sums, recv)]
    return share_halves(reduced)


def kernel(x, c, ctx, c_ctx, ada_w, ada_b, pre_g, post_g, ev_w_in, ev_pool_w, ev_pool_scale, ev_conv_w, ev_conv_b, ev_w_out, od_w_in, od_onorm_g, od_w_out, lb_logits, loss_target, m_c_ctx, m_ada_w, m_ada_b, m_pre_g, m_post_g, m_ev_w_in, m_ev_pool_w, m_ev_pool_scale, m_ev_conv_w, m_ev_conv_b, m_ev_w_out, m_od_w_in, m_od_onorm_g, m_od_w_out, m_lb_logits, v_c_ctx, v_ada_w, v_ada_b, v_pre_g, v_post_g, v_ev_w_in, v_ev_pool_w, v_ev_pool_scale, v_ev_conv_w, v_ev_conv_b, v_ev_w_out, v_od_w_in, v_od_onorm_g, v_od_w_out, v_lb_logits):
    _, seq, d = x.shape
    cx = ctx.shape[1]
    t = cx + seq
    half_d = d // 2
    g = half_d // N_POOL
    tn = d // 4
    xi, yi, ci = lax.axis_index("x"), lax.axis_index("y"), lax.axis_index("c")
    chip = 2 * xi + yi
    me = 2 * chip + ci
    core_arr = jnp.reshape(ci, (1,)).astype(jnp.int32)
    chip_arr = jnp.reshape(chip, (1,)).astype(jnp.int32)
    chip_core_arr = jnp.stack([chip, ci]).astype(jnp.int32)

    pad = lambda a, rows: jnp.concatenate([a, jnp.zeros((rows - a.shape[0], g), F32)], axis=0)
    small = jnp.concatenate([
        ev_pool_w.reshape(g, g), pad(ev_conv_w.reshape(3, g), 8), pad(od_onorm_g.reshape(2, g), 8),
        pad(lb_logits.reshape(12, g), 16)], axis=0)
    ev_in_g, ev_out_g, small_g, ev_done = allgather_shards([
        put_in_slot(ev_w_in[0], chip_arr, BF16, "cast_ev_w_in"),
        put_in_slot(ev_w_out[0], chip_arr, BF16, "cast_ev_w_out"),
        put_in_slot(small, chip_arr, F32, "place_small")])
    ev_out3 = ev_out_g.reshape(1, d, d)
    pool_w_full = small_g[:, :g].reshape(4, N_POOL, g // 4, g).transpose(1, 0, 2, 3).reshape(N_POOL, g, g)
    conv_w_full = small_g[:, g:g + 3].transpose(1, 0, 2).reshape(3, half_d)
    onorm_full = small_g[:, g + 8:g + 10].reshape(1, d)
    lbl_full = small_g[:, g + 16:g + 28].reshape(4, 2, 3, 2 * g).transpose(1, 2, 0, 3).reshape(2, 3, d)

    c_rows = jnp.concatenate([c + ev_done[0:1, 0:1], jnp.zeros((7, d), F32)], axis=0)
    c_all = allgather8(c_rows, "allgather_c")[:, 0, :]
    s_in = jnp.concatenate([c_all, c_ctx.reshape(1, d), jnp.zeros((7, d), F32)], axis=0)
    ws_ada = ada_w.shape[2]
    ada_b_mine = lax.dynamic_slice(ada_b, (0, chip * ws_ada), (2, ws_ada)).reshape(2, 1, ws_ada)
    s_act, mod_mine = ada_fwd(s_in, ada_w, ada_b_mine, tn)
    mod_all = allgather8(mod_mine.reshape(32, ws_ada), "allgather_mod")
    od_ici = copies_start("gather_od_ici_start", [
        put_in_slot(od_w_in[0], chip_arr, BF16, "cast_od_w_in"),
        put_in_slot(od_w_out[0], chip_arr, BF16, "cast_od_w_out")], 6, plan_gather_ici, mod_all)
    mod_full = mod_all[0::2].reshape(4, 2, 16, ws_ada).transpose(1, 2, 0, 3).reshape(2, 16, 3 * d)
    mod_lat = lax.dynamic_slice(mod_full, (0, me, 0), (2, 1, 3 * d))
    mods = jnp.concatenate([mod_full[:, 8:9], mod_lat], axis=1)
    shift, scale, gate = mods[:, :, :d], mods[:, :, d:2 * d], mods[:, :, 2 * d:]

    xs = jnp.concatenate([ctx[0], x[0]], axis=0)

    h0 = normmod_fwd(xs, pre_g[0:1] + od_ici[3][0:1, 0:1], shift[0], scale[0], cx)
    z0 = mm_nn(h0, ev_in_g, half_d, tn, "mm_ev_in")
    u_a = mix_a_fwd(z0, pool_w_full, ev_pool_scale, cx)
    u_b = mix_b_fwd(z0, conv_w_full, ev_conv_b, cx)
    u = jnp.concatenate([u_a, u_b], axis=1)
    y0 = mm_nn(u, ev_out3, d, tn, "mm_ev_out")[0]
    xs1 = post_fwd(xs, y0, post_g[0:1], gate[0], cx)
    od_d2d = copies_start("gather_od_d2d_start",
                          copies_wait("gather_od_ici_wait", od_ici, plan_gather_ici, xs1)[0],
                          6, plan_gather_d2d, xs1)
    (od_in_g, od_out_g), _ = copies_wait("gather_od_d2d_wait", od_d2d, plan_gather_d2d, od_d2d[3])
    od_out3 = od_out_g.reshape(1, d, d)

    h1 = normmod_fwd(xs1, pre_g[1:2], shift[1], scale[1], cx)
    z1 = mm_nn(h1, od_in_g, d, tn, "mm_od_in")
    o1, r1 = hgrn_fwd(z1, lbl_full, onorm_full, cx)
    y1 = mm_nn(r1, od_out3, d, tn, "mm_od_out")[0]
    sq, dx2 = post_loss(xs1, y1, post_g[1:2], gate[1], loss_target[0], cx)
    loss = lax.psum(sq[0, 0] * (0.5 / d), ("x", "y", "c"))

    dy1, dgate1, dpost1 = post_bwd(dx2, y1, post_g[1:2], gate[1], cx, True)
    dr1 = mm_nt(dy1[None], od_out3, tn, "mm_od_out_dx")
    g_od_out = mm_tn(r1, dy1[None], d, tn, "mm_od_out_dw")
    dz1, donorm, dlb = hgrn_bwd(z1, lbl_full, onorm_full, o1, dr1, cx)
    dh1 = mm_nt(dz1, od_in_g, tn, "mm_od_in_dx")
    g_od_in = mm_tn(h1, dz1, od_in_g.shape[2], tn, "mm_od_in_dw")
    dxs1, dpre1, dshift1, dscale1 = normmod_bwd(xs1, dh1, pre_g[1:2], scale[1], dx2, cx, True)

    od_grads = [g_od_in, g_od_out.reshape(4, d // 4, d)]
    half_zone = lambda a, lead, dt: lax.empty((lead, a.shape[1] // 2, a.shape[2]), dt)
    od_ex = copies_start("reduce_od_exchange_start", od_grads + [half_zone(a, 4, F32) for a in od_grads],
                         2, plan_exchange, dxs1)

    dy0, dgate0, dpost0 = post_bwd(dxs1, y0, post_g[0:1] + od_ex[3][0:1, 0:1], gate[0], cx, False)
    du = mm_nt(dy0[None], ev_out3, tn, "mm_ev_out_dx")
    g_ev_out = mm_tn(u, dy0[None], d, tn, "mm_ev_out_dw")
    od_got, _ = copies_wait("reduce_od_exchange_wait", od_ex, plan_exchange, g_ev_out)
    od_sums = [pair_sum(od_got[i], od_got[2 + i], core_arr) for i in range(2)]
    od_sc = copies_start("reduce_od_scatter_start",
                         [sb for _, sb in od_sums] + [half_zone(a, 3, BF16) for a in od_grads],
                         6, plan_scatter, du)
    dz0a, g_pool_w, dpool_scale = mix_a_bwd(z0, du, pool_w_full, ev_pool_scale + od_sc[3][0:1, 0:1], cx)
    dz0b, dconv_w, dconv_b = mix_b_bwd(z0, du, conv_w_full, ev_conv_b + od_sc[3][0:1, 0:1], cx)
    dz0 = jnp.concatenate([dz0a, dz0b], axis=0)
    dh0 = mm_nt(dz0, ev_in_g, tn, "mm_ev_in_dx")
    g_ev_in = mm_tn(h0, dz0, ev_in_g.shape[2], tn, "mm_ev_in_dw")
    dxs0, dpre0, dshift0, dscale0 = normmod_bwd(xs, dh0, pre_g[0:1], scale[0], dxs1, cx, False)
    grad_x = dxs0[cx:][None]
    od_recv, _ = copies_wait("reduce_od_scatter_wait", od_sc, plan_scatter, dxs0)
    ev_grads = [g_ev_in, g_ev_out.reshape(4, d // 4, d), g_pool_w.reshape(4, g, g)]
    ev_ex = copies_start("reduce_ev_exchange_start", ev_grads + [half_zone(a, 4, F32) for a in ev_grads],
                         3, plan_exchange, dxs0)
    od_sh = copies_start("reduce_od_share_start",
                         [owner_sum(od_sums[i][0], od_recv[2 + i], chip_core_arr) for i in range(2)],
                         2, plan_share, ev_ex[3])

    zrow = jnp.zeros((1, d), F32)
    small_rows = jnp.concatenate([
        dpre0, dpre1, dpost0, dpost1,
        jnp.concatenate([dpool_scale, dconv_b], axis=1),
        jnp.concatenate([dconv_w.reshape(1, 3 * half_d), jnp.zeros((1, half_d), F32)], axis=1).reshape(2, d),
        donorm, dlb,
        dshift0[1:2], dscale0[1:2], dgate0[1:2], dshift0[0:1], dscale0[0:1], dgate0[0:1],
        dshift1[1:2], dscale1[1:2], dgate1[1:2], dshift1[0:1], dscale1[0:1], zrow,
        zrow, zrow], axis=0)
    small_all = allgather8(small_rows, "allgather_small")
    tot = small_reduce(small_all)

    dm_rows = []
    for layer in range(2):
        lat = ROW_MOD + 6 * layer
        dm_lat = small_all[:, lat:lat + 3].reshape(8, 3 * d)
        dm_ctx = tot[lat + 3:lat + 6].reshape(1, 3 * d)
        dm_rows.append(jnp.concatenate([dm_lat, dm_ctx, jnp.zeros((7, 3 * d), F32)], axis=0))
    dm_full = jnp.stack(dm_rows)
    dm_mine = lax.dynamic_slice(dm_full, (0, 0, chip * ws_ada), (2, 16, ws_ada))

    ev_got, _ = copies_wait("reduce_ev_exchange_wait", ev_ex, plan_exchange, tot)
    ev_sums = [pair_sum(ev_got[i], ev_got[3 + i], core_arr) for i in range(3)]
    ev_sc = copies_start("reduce_ev_scatter_start",
                         [sb for _, sb in ev_sums] + [half_zone(a, 3, BF16) for a in ev_grads],
                         9, plan_scatter, tot)
    behind = ev_sc[3][0:1, 0:1]
    grad_ada_w, delta_ada_w, new_m_ada_w, new_v_ada_w, dctx_part = ada_bwd_adamw(
        s_act, dm_mine + behind, ada_w, m_ada_w, v_ada_w)
    (grad_od_w_in, grad_od_w_out), _ = copies_wait("reduce_od_share_wait", od_sh, plan_share, ev_sc[3])
    od_w_in_upd = adamw(od_w_in[0], grad_od_w_in, m_od_w_in[0], v_od_w_in[0], "adamw_od_w_in")
    od_w_out_upd = adamw(od_w_out[0], grad_od_w_out, m_od_w_out[0], v_od_w_out[0], "adamw_od_w_out")
    ev_recv, ev_landed = copies_wait("reduce_ev_scatter_wait", ev_sc, plan_scatter, od_w_in_upd[0])
    grad_ev_w_in, grad_ev_w_out, grad_pool_w = share_halves(
        [owner_sum(ev_sums[i][0], ev_recv[3 + i], chip_core_arr) for i in range(3)])
    dctx_all = allgather8(dctx_part[0] + dctx_part[1] + ev_landed[0:1, 0:1], "allgather_dctx")
    grad_c_ctx = c_ctx_grad(dctx_all, c_ctx.reshape(1, d)).reshape(d)

    grad_ada_b = tot[24:30].reshape(2, 3 * d)
    grad_pre_g = tot[0:2]
    grad_post_g = tot[2:4]
    grad_ev_pool_scale = tot[4:5, :half_d]
    grad_ev_conv_b = tot[4:5, half_d:]
    conv_w_tot = tot[5:7].reshape(1, 2 * d)[:, :3 * half_d].reshape(3, N_POOL, g)
    grad_ev_conv_w = lax.dynamic_slice(conv_w_tot, (0, chip, 0), (3, 1, g)).reshape(1, 3, g)
    grad_od_onorm_g = lax.dynamic_slice(tot[7:8], (0, chip * 2 * g), (1, 2 * g))
    dlb_mine = lax.dynamic_slice(tot[8:10], (0, chip * 2 * g), (2, 2 * g))
    grad_lb_logits = lb_logits_grad(lb_logits, dlb_mine)
    grad_ev_w_in = grad_ev_w_in[None]
    grad_od_w_in = grad_od_w_in[None]
    grad_ev_w_out = grad_ev_w_out[None]
    grad_od_w_out = grad_od_w_out[None]
    grad_ev_pool_w = grad_pool_w.reshape(1, N_POOL, g // 4, g)

    def step(w, gr, m, v, name):
        shape = w.shape
        cols = shape[-1]
        two_d = lambda a: a.reshape(-1, cols)
        dl, mo, vo = adamw(two_d(w), two_d(gr), two_d(m), two_d(v), "adamw_" + name)
        return dl.reshape(shape), mo.reshape(shape), vo.reshape(shape)

    upd = {
        "c_ctx": step(c_ctx, grad_c_ctx, m_c_ctx, v_c_ctx, "c_ctx"),
        "ada_w": (delta_ada_w, new_m_ada_w, new_v_ada_w),
        "ada_b": step(ada_b, grad_ada_b, m_ada_b, v_ada_b, "ada_b"),
        "pre_g": step(pre_g, grad_pre_g, m_pre_g, v_pre_g, "pre_g"),
        "post_g": step(post_g, grad_post_g, m_post_g, v_post_g, "post_g"),
        "ev_w_in": step(ev_w_in, grad_ev_w_in, m_ev_w_in, v_ev_w_in, "ev_w_in"),
        "ev_pool_w": step(ev_pool_w, grad_ev_pool_w, m_ev_pool_w, v_ev_pool_w, "ev_pool_w"),
        "ev_pool_scale": step(ev_pool_scale, grad_ev_pool_scale, m_ev_pool_scale, v_ev_pool_scale, "ev_pool_scale"),
        "ev_conv_w": step(ev_conv_w, grad_ev_conv_w, m_ev_conv_w, v_ev_conv_w, "ev_conv_w"),
        "ev_conv_b": step(ev_conv_b, grad_ev_conv_b, m_ev_conv_b, v_ev_conv_b, "ev_conv_b"),
        "ev_w_out": step(ev_w_out, grad_ev_w_out, m_ev_w_out, v_ev_w_out, "ev_w_out"),
        "od_w_in": tuple(a[None] for a in od_w_in_upd),
        "od_onorm_g": step(od_onorm_g, grad_od_onorm_g, m_od_onorm_g, v_od_onorm_g, "od_onorm_g"),
        "od_w_out": tuple(a[None] for a in od_w_out_upd),
        "lb_logits": step(lb_logits, grad_lb_logits, m_lb_logits, v_lb_logits, "lb_logits"),
    }
    names = ["c_ctx", "ada_w", "ada_b", "pre_g", "post_g", "ev_w_in", "ev_pool_w", "ev_pool_scale",
             "ev_conv_w", "ev_conv_b", "ev_w_out", "od_w_in", "od_onorm_g", "od_w_out", "lb_logits"]
    grads = [grad_c_ctx, grad_ada_w, grad_ada_b, grad_pre_g, grad_post_g, grad_ev_w_in, grad_ev_pool_w,
             grad_ev_pool_scale, grad_ev_conv_w, grad_ev_conv_b, grad_ev_w_out, grad_od_w_in,
             grad_od_onorm_g, grad_od_w_out, grad_lb_logits]
    return (loss, grad_x, *grads, *[upd[k][0] for k in names], *[upd[k][1] for k in names],
            *[upd[k][2] for k in names])
```

```python
import functools

import jax
import jax.numpy as jnp
from jax import lax
from jax.experimental import pallas as pl
from jax.experimental.pallas import tpu as pltpu

EPS = 1e-6
GRID_W_LOG2 = 6
CHUNK = 64
HEAD = 128
N_POOL = 4
ADAM_LR, ADAM_B1, ADAM_B2, ADAM_EPS, ADAM_WD, ADAM_STEP = 0.001, 0.9, 0.999, 1e-08, 0.01, 10
VMEM_LIMIT = 56 * 1024 * 1024
MESH = pl.DeviceIdType.MESH
F32, BF16 = jnp.float32, jnp.bfloat16
ANY = pl.BlockSpec(memory_space=pl.ANY)
VMEM = pl.BlockSpec(memory_space=pltpu.VMEM)


def _cp(**kw):
    return pltpu.CompilerParams(vmem_limit_bytes=VMEM_LIMIT, **kw)


def _silu(x):
    return x * jax.nn.sigmoid(x)


def _dsilu(x):
    s = jax.nn.sigmoid(x)
    return s * (1.0 + x * (1.0 - s))


def _dot(a, b, dims=((1,), (0,)), precision=None):
    return lax.dot_general(a, b, (dims, ((), ())), preferred_element_type=F32, precision=precision)


NN = ((1,), (0,))
NT = ((1,), (1,))
TN = ((0,), (0,))


def _row_block(cx):
    return 256 if cx % 256 == 0 else 128


def normmod_fwd(xs, g, shift, scale, cx):
    t, d = xs.shape
    tm = _row_block(cx)
    nctx = cx // tm

    def body(x_ref, g_ref, sh_ref, sc_ref, h_ref):
        is_ctx = pl.program_id(0) < nctx
        x = x_ref[...]
        rstd = lax.rsqrt(jnp.mean(x * x, axis=-1, keepdims=True) + EPS)
        sc = jnp.where(is_ctx, sc_ref[0:1, :], sc_ref[1:2, :])
        sh = jnp.where(is_ctx, sh_ref[0:1, :], sh_ref[1:2, :])
        h_ref[...] = ((x * rstd) * g_ref[...] * (1.0 + sc) + sh).astype(BF16)

    row = pl.BlockSpec((tm, d), lambda i: (i, 0))
    vec = lambda r: pl.BlockSpec((r, d), lambda i: (0, 0))
    return pl.pallas_call(
        body, name="normmod_fwd", grid=(t // tm,),
        in_specs=[row, vec(1), vec(2), vec(2)], out_specs=row,
        out_shape=jax.ShapeDtypeStruct((t, d), BF16), compiler_params=_cp(),
    )(xs, g, shift, scale)


def normmod_bwd(xs, dh, g, scale, dres, cx, res_is_latent_only, dx_latent_only=False):
    t, d = xs.shape
    tm = _row_block(cx)
    nctx = cx // tm

    def body(x_ref, dh_ref, g_ref, sc_ref, dres_ref, dx_ref, dg_ref, dsh_ref, dsc_ref):
        i = pl.program_id(0)
        is_ctx = i < nctx

        @pl.when(i == 0)
        def _():
            dg_ref[...] = jnp.zeros_like(dg_ref)
            dsh_ref[...] = jnp.zeros_like(dsh_ref)
            dsc_ref[...] = jnp.zeros_like(dsc_ref)

        x = x_ref[...]
        dh = dh_ref[...]
        gv = g_ref[...]
        rstd = lax.rsqrt(jnp.mean(x * x, axis=-1, keepdims=True) + EPS)
        xhat = x * rstd
        sc = jnp.where(is_ctx, sc_ref[0:1, :], sc_ref[1:2, :])
        dsh = jnp.sum(dh, axis=0, keepdims=True)
        dhx = dh * xhat
        dsc = jnp.sum(dhx * gv, axis=0, keepdims=True)
        dg_ref[...] += jnp.sum(dhx * (1.0 + sc), axis=0, keepdims=True)
        zero = jnp.zeros_like(dsh)
        dsh_ref[0:1, :] += jnp.where(is_ctx, dsh, zero)
        dsh_ref[1:2, :] += jnp.where(is_ctx, zero, dsh)
        dsc_ref[0:1, :] += jnp.where(is_ctx, dsc, zero)
        dsc_ref[1:2, :] += jnp.where(is_ctx, zero, dsc)
        dxhat = dh * (gv * (1.0 + sc))
        dx = rstd * (dxhat - xhat * jnp.mean(dxhat * xhat, axis=-1, keepdims=True))
        res = dres_ref[...]
        if res_is_latent_only:
            res = jnp.where(is_ctx, jnp.zeros_like(res), res)
        dx_ref[...] = dx + res

    row = pl.BlockSpec((tm, d), lambda i: (i, 0))
    if res_is_latent_only:
        res_spec = pl.BlockSpec((tm, d), lambda i: (jnp.maximum(i - nctx, 0), 0))
    else:
        res_spec = row
    vec = lambda r: pl.BlockSpec((r, d), lambda i: (0, 0))
    dx_spec = pl.BlockSpec((tm, d), lambda i: (jnp.maximum(i - nctx, 0), 0)) if dx_latent_only else row
    return pl.pallas_call(
        body, name="normmod_bwd", grid=(t // tm,),
        in_specs=[row, row, vec(1), vec(2), res_spec],
        out_specs=[dx_spec, vec(1), vec(2), vec(2)],
        out_shape=[jax.ShapeDtypeStruct((t - cx if dx_latent_only else t, d), F32), jax.ShapeDtypeStruct((1, d), F32),
                   jax.ShapeDtypeStruct((2, d), F32), jax.ShapeDtypeStruct((2, d), F32)],
        compiler_params=_cp(),
    )(xs, dh, g, scale, dres)


def post_fwd(xs, y, pg, gate, cx):
    t, d = xs.shape
    tm = _row_block(cx)
    nctx = cx // tm

    def body(x_ref, y_ref, pg_ref, gate_ref, o_ref):
        is_ctx = pl.program_id(0) < nctx
        y = y_ref[...]
        rstd = lax.rsqrt(jnp.mean(y * y, axis=-1, keepdims=True) + EPS)
        gt = jnp.where(is_ctx, gate_ref[0:1, :], gate_ref[1:2, :])
        o_ref[...] = x_ref[...] + gt * ((y * rstd) * pg_ref[...])

    row = pl.BlockSpec((tm, d), lambda i: (i, 0))
    vec = lambda r: pl.BlockSpec((r, d), lambda i: (0, 0))
    return pl.pallas_call(
        body, name="post_fwd", grid=(t // tm,),
        in_specs=[row, row, vec(1), vec(2)], out_specs=row,
        out_shape=jax.ShapeDtypeStruct((t, d), F32), compiler_params=_cp(),
    )(xs, y, pg, gate)


def post_loss(xs, y, pg, gate, target, cx):
    t, d = xs.shape
    n = y.shape[0]
    tm = _row_block(cx)
    nctx = cx // tm

    def body(x_ref, y_ref, pg_ref, gate_ref, tgt_ref, sq_ref, dx_ref):
        @pl.when(pl.program_id(0) == 0)
        def _():
            sq_ref[...] = jnp.zeros_like(sq_ref)

        y = y_ref[...]
        rstd = lax.rsqrt(jnp.mean(y * y, axis=-1, keepdims=True) + EPS)
        x2 = x_ref[...] + gate_ref[1:2, :] * ((y * rstd) * pg_ref[...])
        err = x2 - tgt_ref[...]
        sq_ref[...] += jnp.sum(err * err)
        dx_ref[...] = err * (1.0 / d)

    row = pl.BlockSpec((tm, d), lambda i: (i, 0))
    xrow = pl.BlockSpec((tm, d), lambda i: (i + nctx, 0))
    vec = lambda r: pl.BlockSpec((r, d), lambda i: (0, 0))
    return pl.pallas_call(
        body, name="post_loss", grid=(n // tm,),
        in_specs=[xrow, row, vec(1), vec(2), row],
        out_specs=[pl.BlockSpec((8, 128), lambda i: (0, 0)), row],
        out_shape=[jax.ShapeDtypeStruct((8, 128), F32), jax.ShapeDtypeStruct((n, d), F32)],
        compiler_params=_cp(),
    )(xs, y, pg, gate, target)


def post_bwd(dxo, y, pg, gate, cx, latent_only):
    m, d = y.shape
    tm = _row_block(cx)
    nctx = 0 if latent_only else cx // tm

    def body(dx_ref, y_ref, pg_ref, gate_ref, dy_ref, dgate_ref, dpg_ref):
        i = pl.program_id(0)
        is_ctx = i < nctx

        @pl.when(i == 0)
        def _():
            dgate_ref[...] = jnp.zeros_like(dgate_ref)
            dpg_ref[...] = jnp.zeros_like(dpg_ref)

        y = y_ref[...]
        dx = dx_ref[...]
        pgv = pg_ref[...]
        rstd = lax.rsqrt(jnp.mean(y * y, axis=-1, keepdims=True) + EPS)
        yhat = y * rstd
        gt = jnp.where(is_ctx, gate_ref[0:1, :], gate_ref[1:2, :])
        dxy = dx * yhat
        dgt = jnp.sum(dxy * pgv, axis=0, keepdims=True)
        zero = jnp.zeros_like(dgt)
        dgate_ref[0:1, :] += jnp.where(is_ctx, dgt, zero)
        dgate_ref[1:2, :] += jnp.where(is_ctx, zero, dgt)
        dpg_ref[...] += jnp.sum(dxy * gt, axis=0, keepdims=True)
        dyhat = dx * (gt * pgv)
        dy = rstd * (dyhat - yhat * jnp.mean(dyhat * yhat, axis=-1, keepdims=True))
        dy_ref[...] = dy.astype(BF16)

    row = pl.BlockSpec((tm, d), lambda i: (i, 0))
    vec = lambda r: pl.BlockSpec((r, d), lambda i: (0, 0))
    return pl.pallas_call(
        body, name="post_bwd", grid=(m // tm,),
        in_specs=[row, row, vec(1), vec(2)], out_specs=[row, vec(2), vec(1)],
        out_shape=[jax.ShapeDtypeStruct((m, d), BF16), jax.ShapeDtypeStruct((2, d), F32),
                   jax.ShapeDtypeStruct((1, d), F32)],
        compiler_params=_cp(),
    )(dxo, y, pg, gate)


def _split_rows(m):
    for cand in (1024, 768, 512, 384, 256, 128):
        if m % cand == 0 and m // cand >= 2:
            return cand
    return m


def mm_nn(a, w3, sec, tn, name):
    m, k = a.shape
    q, _, ws = w3.shape
    n = q * ws
    tpq, tps = ws // tn, sec // tn
    tm = next(c for c in (768, 512, 256, 128) if m % c == 0)

    def body(a_ref, w_ref, o_ref):
        w = w_ref[...]

        def step(i, carry):
            rows = pl.ds(pl.multiple_of(i * tm, tm), tm)
            o_ref[rows, :] = _dot(a_ref[rows, :], w)
            return carry

        lax.fori_loop(0, m // tm, step, 0)

    return pl.pallas_call(
        body, name=name, grid=(n // tn,),
        in_specs=[pl.BlockSpec((m, k), lambda j: (0, 0)),
                  pl.BlockSpec((None, k, tn), lambda j: (j // tpq, 0, j % tpq))],
        out_specs=pl.BlockSpec((None, m, tn), lambda j: (j // tps, 0, j % tps)),
        out_shape=jax.ShapeDtypeStruct((n // sec, m, sec), F32), compiler_params=_cp(),
    )(a, w3)


def mm_nt(a3, w3, tn, name):
    s, m, sec = a3.shape
    q, k, ws = w3.shape
    n = q * ws
    tpq, tps = ws // tn, sec // tn
    mb = _split_rows(m)

    def body(a_ref, w_ref, o_ref):
        @pl.when(pl.program_id(1) == 0)
        def _():
            o_ref[...] = jnp.zeros_like(o_ref)

        o_ref[...] += _dot(a_ref[...], w_ref[...], NT)

    return pl.pallas_call(
        body, name=name, grid=(m // mb, n // tn),
        in_specs=[pl.BlockSpec((None, mb, tn), lambda i, j: (j // tps, i, j % tps)),
                  pl.BlockSpec((None, k, tn), lambda i, j: (j // tpq, 0, j % tpq))],
        out_specs=pl.BlockSpec((mb, k), lambda i, j: (i, 0)),
        out_shape=jax.ShapeDtypeStruct((m, k), F32), compiler_params=_cp(),
    )(a3, w3)


def mm_tn(a, b3, ws, tn, name):
    m, k = a.shape
    s, _, sec = b3.shape
    n = s * sec
    tpq, tps = ws // tn, sec // tn
    kb = 256 if k % 256 == 0 else 128

    def body(a_ref, b_ref, o_ref):
        b = b_ref[...]
        for i in range(k // kb):
            o_ref[i * kb:(i + 1) * kb, :] = _dot(a_ref[:, i * kb:(i + 1) * kb], b, TN).astype(BF16)

    return pl.pallas_call(
        body, name=name, grid=(n // tn,),
        in_specs=[pl.BlockSpec((m, k), lambda j: (0, 0)),
                  pl.BlockSpec((None, m, tn), lambda j: (j // tps, 0, j % tps))],
        out_specs=pl.BlockSpec((None, k, tn), lambda j: (j // tpq, 0, j % tpq)),
        out_shape=jax.ShapeDtypeStruct((n // ws, k, ws), BF16), compiler_params=_cp(),
    )(a, b3)


def _pool_mask(gi, row0, tm, t, cx, seq, transposed):
    half = jnp.left_shift(1, gi)
    r = lax.broadcasted_iota(jnp.int32, (tm, 1), 0) + row0
    c = lax.broadcasted_iota(jnp.int32, (1, t), 1)
    out_tok, src_tok = (c, r) if transposed else (r, c)

    def parts(tok):
        lat = tok - cx
        return tok < cx, lat >> GRID_W_LOG2, lat & ((1 << GRID_W_LOG2) - 1)

    o_ctx, o_row, o_col = parts(out_tok)
    s_ctx, s_row, s_col = parts(src_tok)

    def inside(o, s):
        return (s >= o - half) & (s <= o + half - 1)

    ctx_hit = o_ctx & s_ctx & inside(out_tok, src_tok)
    lat_hit = (~o_ctx) & (~s_ctx) & inside(o_row, s_row) & inside(o_col, s_col)
    mask = jnp.where(ctx_hit | lat_hit, 1.0, 0.0).astype(BF16)

    own_ctx, own_row, own_col = parts(r)

    def count(pos, size):
        return jnp.minimum(pos + half - 1, size - 1) - jnp.maximum(pos - half, 0) + 1

    cnt = jnp.where(own_ctx, count(r, cx),
                    count(own_row, seq >> GRID_W_LOG2) * count(own_col, 1 << GRID_W_LOG2))
    return mask, 1.0 / cnt.astype(F32)


def mix_a_fwd(z0, pool_w, pool_scale, cx):
    _, t, half_d = z0.shape
    g = half_d // N_POOL
    seq = t - cx
    tm = _row_block(cx)

    def body(v_ref, ag_ref, w_ref, sc_ref, u_ref, vb_ref):
        gi = pl.program_id(0)
        vb_ref[...] = v_ref[...].astype(BF16)
        w = w_ref[...].astype(BF16)
        sc = sc_ref[...]

        def step(i, carry):
            row0 = pl.multiple_of(i * tm, tm)
            rows = pl.ds(row0, tm)
            mask, inv = _pool_mask(gi, row0, tm, t, cx, seq, False)
            pooled = _dot(mask, vb_ref[...]) * inv - v_ref[rows, :]
            mixed = _dot(pooled.astype(BF16), w) * sc
            u_ref[rows, :] = (mixed * _silu(ag_ref[rows, :])).astype(BF16)
            return carry

        lax.fori_loop(0, t // tm, step, 0)

    sec = lambda s: pl.BlockSpec((None, t, g), lambda j: (s, 0, j))
    return pl.pallas_call(
        body, name="mix_a_fwd", grid=(N_POOL,),
        in_specs=[sec(0), sec(1), pl.BlockSpec((None, g, g), lambda j: (j, 0, 0)),
                  pl.BlockSpec((1, g), lambda j: (0, j))],
        out_specs=pl.BlockSpec((t, g), lambda j: (0, j)),
        out_shape=jax.ShapeDtypeStruct((t, half_d), BF16),
        scratch_shapes=[pltpu.VMEM((t, g), BF16)], compiler_params=_cp(),
    )(z0, z0, pool_w, pool_scale)


def mix_a_bwd(z0, du, pool_w, pool_scale, cx):
    _, t, half_d = z0.shape
    g = half_d // N_POOL
    seq = t - cx
    tm = _row_block(cx)
    gq = g // 4

    def body(v_ref, ag_ref, du_ref, w_ref, sc_ref, dz_ref, dw_ref, dsc_ref,
             vb_ref, pooled_ref, dmx_ref, dpl_ref, wdp_ref):
        gi = pl.program_id(0)
        vb_ref[...] = v_ref[...].astype(BF16)
        w = w_ref[...].astype(BF16)
        sc = sc_ref[...]

        def first(i, dsc):
            row0 = pl.multiple_of(i * tm, tm)
            rows = pl.ds(row0, tm)
            mask, inv = _pool_mask(gi, row0, tm, t, cx, seq, False)
            pooled = (_dot(mask, vb_ref[...]) * inv - v_ref[rows, :]).astype(BF16)
            pooled_ref[rows, :] = pooled
            mixed = _dot(pooled, w)
            ag = ag_ref[rows, :]
            duv = du_ref[rows, :]
            dz_ref[1, rows, :] = (duv * (mixed * sc) * _dsilu(ag)).astype(BF16)
            dms = duv * _silu(ag)
            dmixed = (dms * sc).astype(BF16)
            dmx_ref[rows, :] = dmixed
            dpooled = _dot(dmixed, w, NT)
            dpl_ref[rows, :] = dpooled
            wdp_ref[rows, :] = (dpooled * inv).astype(BF16)
            return dsc + jnp.sum(dms * mixed, axis=0, keepdims=True)

        dsc_ref[...] = lax.fori_loop(0, t // tm, first, jnp.zeros((1, g), F32))
        dw = _dot(pooled_ref[...], dmx_ref[...], TN)
        for qi in range(4):
            dw_ref[qi] = dw[qi * gq:(qi + 1) * gq, :]

        def second(i, carry):
            row0 = pl.multiple_of(i * tm, tm)
            rows = pl.ds(row0, tm)
            mask_t, _ = _pool_mask(gi, row0, tm, t, cx, seq, True)
            dz_ref[0, rows, :] = (_dot(mask_t, wdp_ref[...]) - dpl_ref[rows, :]).astype(BF16)
            return carry

        lax.fori_loop(0, t // tm, second, 0)

    sec = lambda s: pl.BlockSpec((None, t, g), lambda j: (s, 0, j))
    return pl.pallas_call(
        body, name="mix_a_bwd", grid=(N_POOL,),
        in_specs=[sec(0), sec(1), pl.BlockSpec((t, g), lambda j: (0, j)),
                  pl.BlockSpec((None, g, g), lambda j: (j, 0, 0)),
                  pl.BlockSpec((1, g), lambda j: (0, j))],
        out_specs=[pl.BlockSpec((2, t, g), lambda j: (0, 0, j)),
                   pl.BlockSpec((4, None, gq, g), lambda j: (0, j, 0, 0)),
                   pl.BlockSpec((1, g), lambda j: (0, j))],
        out_shape=[jax.ShapeDtypeStruct((2, t, half_d), BF16),
                   jax.ShapeDtypeStruct((4, N_POOL, gq, g), F32),
                   jax.ShapeDtypeStruct((1, half_d), F32)],
        scratch_shapes=[pltpu.VMEM((t, g), BF16), pltpu.VMEM((t, g), BF16), pltpu.VMEM((t, g), BF16),
                        pltpu.VMEM((t, g), F32), pltpu.VMEM((t, g), BF16)],
        compiler_params=_cp(),
    )(z0, z0, du, pool_w, pool_scale)


def _conv_masks(t, cx):
    r = lax.broadcasted_iota(jnp.int32, (t, 1), 0)
    has_prev = jnp.where((r == 0) | (r == cx), 0.0, 1.0)
    has_next = jnp.where((r == cx - 1) | (r == t - 1), 0.0, 1.0)
    return has_prev, has_next


def mix_b_fwd(z0, conv_w, conv_b, cx):
    _, t, half_d = z0.shape
    gb = 128

    def body(bx_ref, bb_ref, bc_ref, bg_ref, w_ref, b_ref, u_ref):
        has_prev, has_next = _conv_masks(t, cx)
        tt = bc_ref[...] * bx_ref[...]
        prev = pltpu.roll(tt, 1, 0) * has_prev
        nxt = pltpu.roll(tt, t - 1, 0) * has_next
        cv = prev * w_ref[0:1, :] + tt * w_ref[1:2, :] + nxt * w_ref[2:3, :] + b_ref[...]
        u_ref[...] = (bb_ref[...] * cv * _silu(bg_ref[...])).astype(BF16)

    sec = lambda s: pl.BlockSpec((None, t, gb), lambda j: (s, 0, j))
    return pl.pallas_call(
        body, name="mix_b_fwd", grid=(half_d // gb,),
        in_specs=[sec(2), sec(3), sec(4), sec(5), pl.BlockSpec((3, gb), lambda j: (0, j)),
                  pl.BlockSpec((1, gb), lambda j: (0, j))],
        out_specs=pl.BlockSpec((t, gb), lambda j: (0, j)),
        out_shape=jax.ShapeDtypeStruct((t, half_d), BF16), compiler_params=_cp(),
    )(z0, z0, z0, z0, conv_w, conv_b)


def mix_b_bwd(z0, du, conv_w, conv_b, cx):
    _, t, half_d = z0.shape
    gb = 128
    off = half_d // gb

    def body(bx_ref, bb_ref, bc_ref, bg_ref, du_ref, w_ref, b_ref, dz_ref, dw_ref, db_ref):
        has_prev, has_next = _conv_masks(t, cx)
        bx, bb, bc, bg = bx_ref[...], bb_ref[...], bc_ref[...], bg_ref[...]
        duv = du_ref[...]
        tt = bc * bx
        prev = pltpu.roll(tt, 1, 0) * has_prev
        nxt = pltpu.roll(tt, t - 1, 0) * has_next
        w0, w1, w2 = w_ref[0:1, :], w_ref[1:2, :], w_ref[2:3, :]
        cv = prev * w0 + tt * w1 + nxt * w2 + b_ref[...]
        sg = _silu(bg)
        dz_ref[1] = (duv * cv * sg).astype(BF16)
        dz_ref[3] = (duv * bb * cv * _dsilu(bg)).astype(BF16)
        dcv = duv * bb * sg
        dw_ref[0:1, :] = jnp.sum(dcv * prev, axis=0, keepdims=True)
        dw_ref[1:2, :] = jnp.sum(dcv * tt, axis=0, keepdims=True)
        dw_ref[2:3, :] = jnp.sum(dcv * nxt, axis=0, keepdims=True)
        db_ref[...] = jnp.sum(dcv, axis=0, keepdims=True)
        dt = (pltpu.roll(dcv * has_prev, t - 1, 0) * w0 + dcv * w1
              + pltpu.roll(dcv * has_next, 1, 0) * w2)
        dz_ref[0] = (dt * bc).astype(BF16)
        dz_ref[2] = (dt * bx).astype(BF16)

    sec = lambda s: pl.BlockSpec((None, t, gb), lambda j: (s, 0, j))
    return pl.pallas_call(
        body, name="mix_b_bwd", grid=(half_d // gb,),
        in_specs=[sec(2), sec(3), sec(4), sec(5), pl.BlockSpec((t, gb), lambda j: (0, j + off)),
                  pl.BlockSpec((3, gb), lambda j: (0, j)), pl.BlockSpec((1, gb), lambda j: (0, j))],
        out_specs=[pl.BlockSpec((4, t, gb), lambda j: (0, 0, j)),
                   pl.BlockSpec((3, gb), lambda j: (0, j)), pl.BlockSpec((1, gb), lambda j: (0, j))],
        out_shape=[jax.ShapeDtypeStruct((4, t, half_d), BF16),
                   jax.ShapeDtypeStruct((3, half_d), F32), jax.ShapeDtypeStruct((1, half_d), F32)],
        compiler_params=_cp(),
    )(z0, z0, z0, z0, du, conv_w, conv_b)


def _lower_bound(lbl_ref, d):
    l0, l1, l2 = lbl_ref[d, 0:1, :], lbl_ref[d, 1:2, :], lbl_ref[d, 2:3, :]
    mx = jnp.maximum(jnp.maximum(l0, l1), l2)
    e0, e1, e2 = jnp.exp(l0 - mx), jnp.exp(l1 - mx), jnp.exp(l2 - mx)
    inv = 1.0 / (e0 + e1 + e2)
    return (e0 + e1) * inv, (e0 * inv, e1 * inv, e2 * inv)


def _chunk_consts(d):
    r = lax.broadcasted_iota(jnp.int32, (CHUNK, CHUNK), 0)
    c = lax.broadcasted_iota(jnp.int32, (CHUNK, CHUNK), 1)
    keep = (c <= r) if d == 0 else (c >= r)
    return jnp.where(keep, 1.0, 0.0).astype(F32), keep


def _chunk_of_step(s, d, nc, ncc):
    if d == 0:
        return s
    return jnp.where(s < ncc, ncc - 1 - s, nc - 1 + ncc - s)


def _chunk_terms(lfc, kc, qc, cum):
    bc = _dot(cum, lfc, precision=lax.Precision.HIGHEST)
    bl = jnp.sum(lfc, axis=0, keepdims=True)
    e = jnp.exp(bc)
    einv = jnp.exp(-bc)
    erem = jnp.exp(bl - bc)
    return e, einv, erem, jnp.exp(bl), qc * e, kc * einv, kc * erem


def hgrn_fwd(z1, lbl, onorm, cx):
    _, t, d = z1.shape
    seq = t - cx
    nc, ncc = t // CHUNK, cx // CHUNK

    def body(zf_ref, zb_ref, v_ref, q_ref, g_ref, lbl_ref, on_ref, o_ref, r_ref,
             lf_ref, k_ref, oacc_ref, st_ref):
        for dr, z_ref in ((0, zf_ref), (1, zb_ref)):
            lbv, _ = _lower_bound(lbl_ref, dr)
            z = z_ref[...]
            lf_ref[...] = jnp.log(lbv + (1.0 - lbv) * jax.nn.sigmoid(z))
            k_ref[...] = (1.0 - lbv) * jax.nn.sigmoid(-z)
            st_ref[...] = jnp.zeros_like(st_ref)
            cum, keep = _chunk_consts(dr)

            def step(s, carry, dr=dr, cum=cum, keep=keep):
                n = _chunk_of_step(s, dr, nc, ncc)
                rows = pl.ds(pl.multiple_of(n * CHUNK, CHUNK), CHUNK)
                vc = v_ref[rows, :].astype(BF16)
                _, _, _, dec, qd, ki, kd = _chunk_terms(lf_ref[rows, :], k_ref[rows, :], q_ref[rows, :], cum)
                qdb = qd.astype(BF16)
                a = jnp.where(keep, _dot(qdb, ki.astype(BF16), NT), 0.0)
                st = st_ref[...]
                oc = _dot(qdb, st.astype(BF16), NT) + _dot(a.astype(BF16), vc)
                st_ref[...] = st * dec + _dot(vc, kd.astype(BF16), TN)
                if dr == 0:
                    oacc_ref[rows, :] = oc
                else:
                    oacc_ref[rows, :] += oc
                return carry

            lax.fori_loop(0, nc, step, 0, unroll=4)

        o = oacc_ref[cx:, :]
        o_ref[...] = o
        rstd = lax.rsqrt(jnp.mean(o * o, axis=-1, keepdims=True) + EPS)
        r_ref[...] = (o * rstd * on_ref[...] * _silu(g_ref[cx:, :])).astype(BF16)

    sec = lambda s: pl.BlockSpec((None, t, HEAD), lambda h: (s, 0, h))
    col = pl.BlockSpec((seq, HEAD), lambda h: (0, h))
    return pl.pallas_call(
        body, name="hgrn_fwd", grid=(d // HEAD,),
        in_specs=[sec(0), sec(1), sec(2), sec(3), sec(4),
                  pl.BlockSpec((2, 3, HEAD), lambda h: (0, 0, h)), pl.BlockSpec((1, HEAD), lambda h: (0, h))],
        out_specs=[col, col],
        out_shape=[jax.ShapeDtypeStruct((seq, d), F32), jax.ShapeDtypeStruct((seq, d), BF16)],
        scratch_shapes=[pltpu.VMEM((t, HEAD), F32), pltpu.VMEM((t, HEAD), F32), pltpu.VMEM((t, HEAD), F32),
                        pltpu.VMEM((HEAD, HEAD), F32)],
        compiler_params=_cp(),
    )(z1, z1, z1, z1, z1, lbl, onorm)


def hgrn_bwd(z1, lbl, onorm, o, dr_out, cx):
    _, t, d = z1.shape
    seq = t - cx
    nc, ncc = t // CHUNK, cx // CHUNK

    def body(zf_ref, zb_ref, v_ref, q_ref, g_ref, lbl_ref, on_ref, o_ref, dr_ref,
             dz_ref, don_ref, dlb_ref,
             lf_ref, k_ref, do_ref, dq_ref, dv_ref, dk_ref, dlf_ref, ssc_ref, dst_ref):
        o = o_ref[...]
        g = g_ref[cx:, :]
        drv = dr_ref[...]
        onv = on_ref[...]
        rstd = lax.rsqrt(jnp.mean(o * o, axis=-1, keepdims=True) + EPS)
        ohat = o * rstd
        sg = _silu(g)
        don_ref[...] = jnp.sum(drv * ohat * sg, axis=0, keepdims=True)
        dz_ref[4, :cx, :] = jnp.zeros((cx, HEAD), BF16)
        dz_ref[4, cx:, :] = (drv * ohat * onv * _dsilu(g)).astype(BF16)
        dohat = drv * onv * sg
        do_ref[:cx, :] = jnp.zeros((cx, HEAD), F32)
        do_ref[cx:, :] = rstd * (dohat - ohat * jnp.mean(dohat * ohat, axis=-1, keepdims=True))

        for dr, z_ref in ((0, zf_ref), (1, zb_ref)):
            lbv, _ = _lower_bound(lbl_ref, dr)
            z = z_ref[...]
            lf_ref[...] = jnp.log(lbv + (1.0 - lbv) * jax.nn.sigmoid(z))
            k_ref[...] = (1.0 - lbv) * jax.nn.sigmoid(-z)
            cum, keep = _chunk_consts(dr)
            cum_t, _ = _chunk_consts(1 - dr)

            st_init = jnp.zeros((HEAD, HEAD), F32)

            def state_step(s, st, dr=dr, cum=cum):
                n = _chunk_of_step(s, dr, nc, ncc)
                rows = pl.ds(pl.multiple_of(n * CHUNK, CHUNK), CHUNK)
                ssc_ref[n] = st
                _, _, _, dec, _, _, kd = _chunk_terms(lf_ref[rows, :], k_ref[rows, :], q_ref[rows, :], cum)
                return st * dec + _dot(v_ref[rows, :].astype(BF16), kd.astype(BF16), TN)

            lax.fori_loop(0, nc, state_step, st_init, unroll=4)
            dst_ref[...] = jnp.zeros_like(dst_ref)

            def grad_step(s2, carry, dr=dr, cum=cum, cum_t=cum_t, keep=keep):
                n = _chunk_of_step(nc - 1 - s2, dr, nc, ncc)
                rows = pl.ds(pl.multiple_of(n * CHUNK, CHUNK), CHUNK)
                vc = v_ref[rows, :].astype(BF16)
                e, einv, erem, dec, qd, ki, kd = _chunk_terms(
                    lf_ref[rows, :], k_ref[rows, :], q_ref[rows, :], cum)
                qdb, kib, kdb = qd.astype(BF16), ki.astype(BF16), kd.astype(BF16)
                doc = do_ref[rows, :].astype(BF16)
                st0 = ssc_ref[n]
                dst = dst_ref[...]
                dstb = dst.astype(BF16)
                a = jnp.where(keep, _dot(qdb, kib, NT), 0.0).astype(BF16)
                da = jnp.where(keep, _dot(doc, vc, NT), 0.0).astype(BF16)
                dqd = _dot(doc, st0.astype(BF16)) + _dot(da, kib)
                dki = _dot(da, qdb, TN)
                dv = _dot(a, doc, TN) + _dot(kdb, dstb, NT)
                dkd = _dot(vc, dstb)
                ddec = jnp.sum(dst * st0, axis=0, keepdims=True)
                dst_ref[...] = _dot(doc, qdb, TN) + dst * dec
                dbc = dqd * qd - dki * ki - dkd * kd
                dbl = jnp.sum(dkd * kd, axis=0, keepdims=True) + ddec * dec
                dlf_ref[rows, :] = _dot(cum_t, dbc, precision=lax.Precision.HIGHEST) + dbl
                dk_ref[rows, :] = dki * einv + dkd * erem
                if dr == 0:
                    dq_ref[rows, :] = dqd * e
                    dv_ref[rows, :] = dv
                else:
                    dq_ref[rows, :] += dqd * e
                    dv_ref[rows, :] += dv
                return carry

            lax.fori_loop(0, nc, grad_step, 0, unroll=2)

            sig = jax.nn.sigmoid(z)
            one_lb = 1.0 - lbv
            f = lbv + one_lb * sig
            dlf = dlf_ref[...]
            dk = dk_ref[...]
            dsig = (dlf / f - dk) * one_lb
            dz_ref[dr] = (dsig * sig * (1.0 - sig)).astype(BF16)
            dlb_ref[dr:dr + 1, :] = jnp.sum((dlf / f - dk) * (1.0 - sig), axis=0, keepdims=True)

        dz_ref[2] = dv_ref[...].astype(BF16)
        dz_ref[3] = dq_ref[...].astype(BF16)

    sec = lambda s: pl.BlockSpec((None, t, HEAD), lambda h: (s, 0, h))
    col = pl.BlockSpec((seq, HEAD), lambda h: (0, h))
    tvec = pltpu.VMEM((t, HEAD), F32)
    return pl.pallas_call(
        body, name="hgrn_bwd", grid=(d // HEAD,),
        in_specs=[sec(0), sec(1), sec(2), sec(3), sec(4),
                  pl.BlockSpec((2, 3, HEAD), lambda h: (0, 0, h)), pl.BlockSpec((1, HEAD), lambda h: (0, h)),
                  col, col],
        out_specs=[pl.BlockSpec((5, t, HEAD), lambda h: (0, 0, h)),
                   pl.BlockSpec((1, HEAD), lambda h: (0, h)), pl.BlockSpec((2, HEAD), lambda h: (0, h))],
        out_shape=[jax.ShapeDtypeStruct((5, t, d), BF16), jax.ShapeDtypeStruct((1, d), F32),
                   jax.ShapeDtypeStruct((2, d), F32)],
        scratch_shapes=[tvec, tvec, tvec, tvec, tvec, tvec, tvec,
                        pltpu.VMEM((nc, HEAD, HEAD), F32), pltpu.VMEM((HEAD, HEAD), F32)],
        compiler_params=_cp(),
    )(z1, z1, z1, z1, z1, lbl, onorm, o, dr_out)


def _gates(z, lbv):
    e = jnp.exp(-jnp.abs(z))
    r = 1.0 / (1.0 + e)
    er = e * r
    pos = z >= 0.0
    sig = jnp.where(pos, r, er)
    nsig = jnp.where(pos, er, r)
    return sig, nsig, lbv + (1.0 - lbv) * sig


def _split3(x):
    hi = x.astype(BF16)
    r1 = x - hi.astype(F32)
    mid = r1.astype(BF16)
    lo = (r1 - mid.astype(F32)).astype(BF16)
    return jnp.concatenate([hi, mid, lo], axis=1)


def _cumsum_chunk(cum, x):
    y = _dot(cum, _split3(x))
    return y[:, :HEAD] + y[:, HEAD:2 * HEAD] + y[:, 2 * HEAD:]


def _chunk_rows(n):
    return pl.ds(pl.multiple_of(n * CHUNK, CHUNK), CHUNK)


def _group(nc):
    return next(u for u in (4, 3, 2, 1) if nc % u == 0)


def _decay_pass(lf_ref, bc_ref, dec_ref, cum, nc):
    grp = _group(nc)

    def step(m, carry):
        ns = [m * grp + u for u in range(grp)]
        lfc = [lf_ref[_chunk_rows(n), :] for n in ns]
        bc = [_cumsum_chunk(cum, x) for x in lfc]
        for u, n in enumerate(ns):
            bc_ref[_chunk_rows(n), :] = bc[u]
            dec_ref[n] = jnp.broadcast_to(jnp.exp(jnp.sum(lfc[u], axis=0, keepdims=True)), (8, HEAD))
        return carry

    lax.fori_loop(0, nc // grp, step, 0)


def hgrn_fwd(z1, lbl, onorm, cx):
    _, t, d = z1.shape
    seq = t - cx
    nc, ncc = t // CHUNK, cx // CHUNK

    grp = _group(nc)

    def body(zf_ref, zb_ref, v_ref, q_ref, g_ref, lbl_ref, on_ref, o_ref, r_ref,
             lf_ref, k_ref, bc_ref, dec_ref, qd_ref, ki_ref, oacc_ref, ds_ref):
        for dr, z_ref in ((0, zf_ref), (1, zb_ref)):
            lbv, _ = _lower_bound(lbl_ref, dr)
            _, nsig, f = _gates(z_ref[...], lbv)
            lf_ref[...] = jnp.log(f)
            k_ref[...] = (1.0 - lbv) * nsig
            cum, keep = _chunk_consts(dr)
            _decay_pass(lf_ref, bc_ref, dec_ref, cum.astype(BF16), nc)
            bc = bc_ref[...]
            qd_ref[...] = (q_ref[...] * jnp.exp(bc)).astype(BF16)
            ki_ref[...] = (k_ref[...] * jnp.exp(-bc)).astype(BF16)

            def local_step(m, carry, dr=dr, keep=keep):
                ns = [m * grp + u for u in range(grp)]
                rows = [_chunk_rows(n) for n in ns]
                qd = [qd_ref[r, :] for r in rows]
                ki = [ki_ref[r, :] for r in rows]
                vc = [v_ref[r, :].astype(BF16) for r in rows]
                sc = [_dot(qd[u], ki[u], NT) for u in range(grp)]
                inc = [_dot(vc[u], ki[u], TN) for u in range(grp)]
                a = [jnp.where(keep, s, 0.0).astype(BF16) for s in sc]
                intra = [_dot(a[u], vc[u]) for u in range(grp)]
                for u in range(grp):
                    ds_ref[ns[u]] = inc[u] * dec_ref[ns[u]][0:1, :]
                    if dr == 0:
                        oacc_ref[rows[u], :] = intra[u]
                    else:
                        oacc_ref[rows[u], :] += intra[u]
                return carry

            lax.fori_loop(0, nc // grp, local_step, 0)

            def state_step(m, st, dr=dr):
                ns = [_chunk_of_step(m * grp + u, dr, nc, ncc) for u in range(grp)]
                rows = [_chunk_rows(n) for n in ns]
                sts = []
                for n in ns:
                    sts.append(st.astype(BF16))
                    st = st * dec_ref[n][0:1, :] + ds_ref[n]
                inter = [_dot(qd_ref[rows[u], :], sts[u], NT) for u in range(grp)]
                for u in range(grp):
                    oacc_ref[rows[u], :] += inter[u]
                return st

            lax.fori_loop(0, nc // grp, state_step, jnp.zeros((HEAD, HEAD), F32))

        o = oacc_ref[cx:, :]
        o_ref[...] = o
        rstd = lax.rsqrt(jnp.mean(o * o, axis=-1, keepdims=True) + EPS)
        r_ref[...] = (o * rstd * on_ref[...] * _silu(g_ref[cx:, :])).astype(BF16)

    sec = lambda s: pl.BlockSpec((None, t, HEAD), lambda h: (s, 0, h))
    col = pl.BlockSpec((seq, HEAD), lambda h: (0, h))
    tf32, tb16 = pltpu.VMEM((t, HEAD), F32), pltpu.VMEM((t, HEAD), BF16)
    return pl.pallas_call(
        body, name="hgrn_fwd", grid=(d // HEAD,),
        in_specs=[sec(0), sec(1), sec(2), sec(3), sec(4),
                  pl.BlockSpec((2, 3, HEAD), lambda h: (0, 0, h)), pl.BlockSpec((1, HEAD), lambda h: (0, h))],
        out_specs=[col, col],
        out_shape=[jax.ShapeDtypeStruct((seq, d), F32), jax.ShapeDtypeStruct((seq, d), BF16)],
        scratch_shapes=[tf32, tf32, tf32, pltpu.VMEM((nc, 8, HEAD), F32), tb16, tb16, tf32,
                        pltpu.VMEM((nc, HEAD, HEAD), F32)],
        compiler_params=_cp(),
    )(z1, z1, z1, z1, z1, lbl, onorm)


def hgrn_bwd(z1, lbl, onorm, o, dr_out, cx):
    _, t, d = z1.shape
    seq = t - cx
    nc, ncc = t // CHUNK, cx // CHUNK

    grp2 = grp = _group(nc)

    def body(zf_ref, zb_ref, v_ref, q_ref, g_ref, lbl_ref, on_ref, o_ref, dr_ref,
             dz_ref, don_ref, dlb_ref,
             lf_ref, k_ref, bc_ref, dec_ref, qd_ref, ki_ref, do_ref,
             dqd_ref, dki_ref, dq_ref, dv_ref, ds_ref, dsl_ref):
        o = o_ref[...]
        g = g_ref[cx:, :]
        drv = dr_ref[...]
        onv = on_ref[...]
        rstd = lax.rsqrt(jnp.mean(o * o, axis=-1, keepdims=True) + EPS)
        ohat = o * rstd
        sg = _silu(g)
        don_ref[...] = jnp.sum(drv * ohat * sg, axis=0, keepdims=True)
        dz_ref[4, :cx, :] = jnp.zeros((cx, HEAD), BF16)
        dz_ref[4, cx:, :] = (drv * ohat * onv * _dsilu(g)).astype(BF16)
        dohat = drv * onv * sg
        do_ref[:cx, :] = jnp.zeros((cx, HEAD), BF16)
        do_ref[cx:, :] = (rstd * (dohat - ohat * jnp.mean(dohat * ohat, axis=-1, keepdims=True))).astype(BF16)

        for dr, z_ref in ((0, zf_ref), (1, zb_ref)):
            lbv, _ = _lower_bound(lbl_ref, dr)
            _, nsig, f = _gates(z_ref[...], lbv)
            lf_ref[...] = jnp.log(f)
            k_ref[...] = (1.0 - lbv) * nsig
            cum, keep = _chunk_consts(dr)
            cum_t = _chunk_consts(1 - dr)[0].astype(BF16)
            _decay_pass(lf_ref, bc_ref, dec_ref, cum.astype(BF16), nc)
            bc = bc_ref[...]
            qd_ref[...] = (q_ref[...] * jnp.exp(bc)).astype(BF16)
            ki_ref[...] = (k_ref[...] * jnp.exp(-bc)).astype(BF16)

            def local_step(m, carry, dr=dr, keep=keep):
                ns = [m * grp + u for u in range(grp)]
                rows = [_chunk_rows(n) for n in ns]
                rng = range(grp)
                qd = [qd_ref[r, :] for r in rows]
                ki = [ki_ref[r, :] for r in rows]
                doc = [do_ref[r, :] for r in rows]
                vc = [v_ref[r, :].astype(BF16) for r in rows]
                sc = [_dot(qd[u], ki[u], NT) for u in rng]
                dsc = [_dot(doc[u], vc[u], NT) for u in rng]
                inc = [_dot(vc[u], ki[u], TN) for u in rng]
                dinc = [_dot(doc[u], qd[u], TN) for u in rng]
                a = [jnp.where(keep, s, 0.0).astype(BF16) for s in sc]
                da = [jnp.where(keep, s, 0.0).astype(BF16) for s in dsc]
                dqd = [_dot(da[u], ki[u]) for u in rng]
                dki = [_dot(da[u], qd[u], TN) for u in rng]
                dv = [_dot(a[u], doc[u], TN) for u in rng]
                for u in rng:
                    ds_ref[ns[u]] = inc[u] * dec_ref[ns[u]][0:1, :]
                    dsl_ref[ns[u]] = dinc[u]
                    dqd_ref[rows[u], :] = dqd[u]
                    dki_ref[rows[u], :] = dki[u]
                    if dr == 0:
                        dv_ref[rows[u], :] = dv[u]
                    else:
                        dv_ref[rows[u], :] += dv[u]
                return carry

            lax.fori_loop(0, nc // grp, local_step, 0)

            def state_step(s, st, dr=dr):
                n = _chunk_of_step(s, dr, nc, ncc)
                inc = ds_ref[n]
                ds_ref[n] = st
                return st * dec_ref[n][0:1, :] + inc

            lax.fori_loop(0, nc, state_step, jnp.zeros((HEAD, HEAD), F32), unroll=4)

            def dstate_step(s, dst, dr=dr):
                n = _chunk_of_step(nc - 1 - s, dr, nc, ncc)
                inc = dsl_ref[n]
                dsl_ref[n] = dst
                return inc + dst * dec_ref[n][0:1, :]

            lax.fori_loop(0, nc, dstate_step, jnp.zeros((HEAD, HEAD), F32), unroll=4)

            def grad_step(m, carry, dr=dr, cum_t=cum_t):
                ns = [m * grp2 + u for u in range(grp2)]
                rows = [_chunk_rows(n) for n in ns]
                rng = range(grp2)
                st0 = [ds_ref[n] for n in ns]
                dst = [dsl_ref[n] for n in ns]
                dstb = [x.astype(BF16) for x in dst]
                dec = [dec_ref[n][0:1, :] for n in ns]
                doc = [do_ref[r, :] for r in rows]
                vc = [v_ref[r, :].astype(BF16) for r in rows]
                e = [jnp.exp(bc_ref[r, :]) for r in rows]
                einv = [jnp.exp(-bc_ref[r, :]) for r in rows]
                qd = [q_ref[rows[u], :] * e[u] for u in rng]
                ki = [k_ref[rows[u], :] * einv[u] for u in rng]
                kd = [ki[u] * dec[u] for u in rng]
                dqd_st = [_dot(doc[u], st0[u].astype(BF16)) for u in rng]
                dkd = [_dot(vc[u], dstb[u]) for u in rng]
                dv_st = [_dot(kd[u].astype(BF16), dstb[u], NT) for u in rng]
                dqd = [dqd_ref[rows[u], :] + dqd_st[u] for u in rng]
                dki = [dki_ref[r, :] for r in rows]
                dbc = [dqd[u] * qd[u] - dki[u] * ki[u] - dkd[u] * kd[u] for u in rng]
                cs = [_cumsum_chunk(cum_t, x) for x in dbc]
                for u in rng:
                    ddec = jnp.sum(dst[u] * st0[u], axis=0, keepdims=True)
                    dbl = jnp.sum(dkd[u] * kd[u], axis=0, keepdims=True) + ddec * dec[u]
                    dv_ref[rows[u], :] += dv_st[u]
                    dqd_ref[rows[u], :] = cs[u] + dbl
                    dki_ref[rows[u], :] = dki[u] * einv[u] + dkd[u] * (einv[u] * dec[u])
                    if dr == 0:
                        dq_ref[rows[u], :] = dqd[u] * e[u]
                    else:
                        dq_ref[rows[u], :] += dqd[u] * e[u]
                return carry

            lax.fori_loop(0, nc // grp2, grad_step, 0)

            sig, nsig, f = _gates(z_ref[...], lbv)
            common = (dqd_ref[...] / f - dki_ref[...]) * nsig
            dz_ref[dr] = (common * ((1.0 - lbv) * sig)).astype(BF16)
            dlb_ref[dr:dr + 1, :] = jnp.sum(common, axis=0, keepdims=True)

        dz_ref[2] = dv_ref[...].astype(BF16)
        dz_ref[3] = dq_ref[...].astype(BF16)

    sec = lambda s: pl.BlockSpec((None, t, HEAD), lambda h: (s, 0, h))
    col = pl.BlockSpec((seq, HEAD), lambda h: (0, h))
    tf32, tb16 = pltpu.VMEM((t, HEAD), F32), pltpu.VMEM((t, HEAD), BF16)
    states = pltpu.VMEM((nc, HEAD, HEAD), F32)
    return pl.pallas_call(
        body, name="hgrn_bwd", grid=(d // HEAD,),
        in_specs=[sec(0), sec(1), sec(2), sec(3), sec(4),
                  pl.BlockSpec((2, 3, HEAD), lambda h: (0, 0, h)), pl.BlockSpec((1, HEAD), lambda h: (0, h)),
                  col, col],
        out_specs=[pl.BlockSpec((5, t, HEAD), lambda h: (0, 0, h)),
                   pl.BlockSpec((1, HEAD), lambda h: (0, h)), pl.BlockSpec((2, HEAD), lambda h: (0, h))],
        out_shape=[jax.ShapeDtypeStruct((5, t, d), BF16), jax.ShapeDtypeStruct((1, d), F32),
                   jax.ShapeDtypeStruct((2, d), F32)],
        scratch_shapes=[tf32, tf32, tf32, pltpu.VMEM((nc, 8, HEAD), F32), tb16, tb16, tb16,
                        tf32, tf32, tf32, tf32, states, states],
        compiler_params=_cp(),
    )(z1, z1, z1, z1, z1, lbl, onorm, o, dr_out)


def _place():
    x, y, c = lax.axis_index("x"), lax.axis_index("y"), lax.axis_index("c")
    chips = [(1 - x, y), (x, 1 - y), (1 - x, 1 - y)]
    return x, y, c, chips


def allgather_shards(bufs):
    n = len(bufs)

    def body(*refs):
        outs = refs[n:2 * n]
        done_ref, send_sems, recv_sems = refs[2 * n:]
        done_ref[...] = jnp.zeros((8, 128), F32)
        x, y, c, chips = _place()
        p = 2 * x + y
        half = [pl.ds(c * (s.shape[1] // 2), s.shape[1] // 2) for s in bufs]
        other = [pl.ds((1 - c) * (s.shape[1] // 2), s.shape[1] // 2) for s in bufs]

        def remote(i, k, src, dst, to):
            return pltpu.make_async_remote_copy(src_ref=src, dst_ref=dst, send_sem=send_sems.at[6 * i + k],
                                                recv_sem=recv_sems.at[6 * i + k], device_id=to, device_id_type=MESH)

        sends = []
        for i in range(n):
            for j, chip in enumerate(chips):
                mine = outs[i].at[p, half[i]]
                cp = remote(i, j, mine, mine, (*chip, c))
                cp.start()
                sends.append(cp)
        for i in range(n):
            for j, chip in enumerate(chips):
                landed = outs[i].at[2 * chip[0] + chip[1], half[i]]
                remote(i, j, landed, landed, (x, y, c)).wait_recv()
                cp = remote(i, 3 + j, landed, landed, (x, y, 1 - c))
                cp.start()
                sends.append(cp)
        for i in range(n):
            for j, chip in enumerate(chips):
                landed = outs[i].at[2 * chip[0] + chip[1], other[i]]
                remote(i, 3 + j, landed, landed, (x, y, c)).wait_recv()
        for cp in sends:
            cp.wait_send()

    return pl.pallas_call(
        body, name="allgather_shards",
        in_specs=[ANY] * n, out_specs=[ANY] * n + [VMEM],
        out_shape=[jax.ShapeDtypeStruct(s.shape, s.dtype) for s in bufs] + [jax.ShapeDtypeStruct((8, 128), F32)],
        input_output_aliases={i: i for i in range(n)},
        scratch_shapes=[pltpu.SemaphoreType.DMA((6 * n,)), pltpu.SemaphoreType.DMA((6 * n,))],
        compiler_params=pltpu.CompilerParams(has_side_effects=True),
    )(*bufs)


def exchange_halves(grads):
    n = len(grads)

    def body(*refs):
        ins, outs = refs[:n], refs[n:2 * n]
        send_sems, recv_sems = refs[2 * n:]
        x, y, c, _ = _place()
        copies = []
        for i in range(n):
            hr = grads[i].shape[1] // 2
            cp = pltpu.make_async_remote_copy(
                src_ref=ins[i].at[:, pl.ds((1 - c) * hr, hr)], dst_ref=outs[i],
                send_sem=send_sems.at[i], recv_sem=recv_sems.at[i],
                device_id=(x, y, 1 - c), device_id_type=MESH)
            cp.start()
            copies.append(cp)
        for cp in copies:
            cp.wait()

    return pl.pallas_call(
        body, name="exchange_halves",
        in_specs=[ANY] * n, out_specs=[ANY] * n,
        out_shape=[jax.ShapeDtypeStruct((4, g.shape[1] // 2, g.shape[2]), g.dtype) for g in grads],
        scratch_shapes=[pltpu.SemaphoreType.DMA((n,)), pltpu.SemaphoreType.DMA((n,))],
        compiler_params=pltpu.CompilerParams(has_side_effects=True),
    )(*grads)


def pair_sum(grad, got, chip_core):
    _, r, cc = grad.shape
    hr = r // 2
    tr = 256 if hr % 256 == 0 else hr
    nb = hr // tr

    def body(cc_ref, a_ref, b_ref, own_ref, sb_ref):
        s = a_ref[...].astype(F32) + b_ref[...].astype(F32)
        sb_ref[...] = s.astype(BF16)

        @pl.when(pl.program_id(1) == cc_ref[0])
        def _():
            own_ref[...] = s

    grid_spec = pltpu.PrefetchScalarGridSpec(
        num_scalar_prefetch=1, grid=(nb, 4),
        in_specs=[pl.BlockSpec((None, tr, cc), lambda i, qi, cc_ref: (qi, cc_ref[1] * nb + i, 0)),
                  pl.BlockSpec((None, tr, cc), lambda i, qi, cc_ref: (qi, i, 0))],
        out_specs=[pl.BlockSpec((tr, cc), lambda i, qi, cc_ref: (i, 0)),
                   pl.BlockSpec((None, tr, cc), lambda i, qi, cc_ref: (qi, i, 0))])
    return pl.pallas_call(
        body, name="pair_sum", grid_spec=grid_spec,
        out_shape=[jax.ShapeDtypeStruct((hr, cc), F32), jax.ShapeDtypeStruct((4, hr, cc), BF16)],
        compiler_params=_cp(),
    )(chip_core, grad, got)


def scatter_to_owners(parts):
    n = len(parts)

    def body(*refs):
        ins, outs = refs[:n], refs[n:2 * n]
        send_sems, recv_sems = refs[2 * n:]
        x, y, c, chips = _place()
        copies = []
        for i in range(n):
            for j, chip in enumerate(chips):
                cp = pltpu.make_async_remote_copy(
                    src_ref=ins[i].at[2 * chip[0] + chip[1]], dst_ref=outs[i].at[j],
                    send_sem=send_sems.at[3 * i + j], recv_sem=recv_sems.at[3 * i + j],
                    device_id=(*chip, c), device_id_type=MESH)
                cp.start()
                copies.append(cp)
        for cp in copies:
            cp.wait()

    return pl.pallas_call(
        body, name="scatter_to_owners",
        in_specs=[ANY] * n, out_specs=[ANY] * n,
        out_shape=[jax.ShapeDtypeStruct((3,) + p.shape[1:], p.dtype) for p in parts],
        scratch_shapes=[pltpu.SemaphoreType.DMA((3 * n,)), pltpu.SemaphoreType.DMA((3 * n,))],
        compiler_params=pltpu.CompilerParams(has_side_effects=True),
    )(*parts)


def owner_sum(own, got, chip_core):
    hr, cc = own.shape
    tr = 256 if hr % 256 == 0 else hr
    nb = hr // tr

    def body(cc_ref, a_ref, b_ref, o_ref):
        s = a_ref[...] + b_ref[0].astype(F32)
        s = s + b_ref[1].astype(F32)
        o_ref[...] = s + b_ref[2].astype(F32)

    grid_spec = pltpu.PrefetchScalarGridSpec(
        num_scalar_prefetch=1, grid=(nb,),
        in_specs=[pl.BlockSpec((tr, cc), lambda i, cc_ref: (i, 0)),
                  pl.BlockSpec((3, tr, cc), lambda i, cc_ref: (0, i, 0))],
        out_specs=pl.BlockSpec((tr, cc), lambda i, cc_ref: (cc_ref[1] * nb + i, 0)))
    return pl.pallas_call(
        body, name="owner_sum", grid_spec=grid_spec,
        out_shape=jax.ShapeDtypeStruct((2 * hr, cc), F32), compiler_params=_cp(),
    )(chip_core, own, got)


def share_halves(bufs):
    n = len(bufs)

    def body(*refs):
        outs = refs[n:2 * n]
        send_sems, recv_sems = refs[2 * n:]
        x, y, c, _ = _place()
        copies = []
        for i in range(n):
            hr = bufs[i].shape[0] // 2
            mine = outs[i].at[pl.ds(c * hr, hr)]
            cp = pltpu.make_async_remote_copy(
                src_ref=mine, dst_ref=mine, send_sem=send_sems.at[i], recv_sem=recv_sems.at[i],
                device_id=(x, y, 1 - c), device_id_type=MESH)
            cp.start()
            copies.append((cp, outs[i].at[pl.ds((1 - c) * hr, hr)]))
        for i, (cp, theirs) in enumerate(copies):
            cp.wait_send()
            pltpu.make_async_remote_copy(
                src_ref=theirs, dst_ref=theirs, send_sem=send_sems.at[i], recv_sem=recv_sems.at[i],
                device_id=(x, y, c), device_id_type=MESH).wait_recv()

    return pl.pallas_call(
        body, name="share_halves",
        in_specs=[ANY] * n, out_specs=[ANY] * n,
        out_shape=[jax.ShapeDtypeStruct(b.shape, b.dtype) for b in bufs],
        input_output_aliases={i: i for i in range(n)},
        scratch_shapes=[pltpu.SemaphoreType.DMA((n,)), pltpu.SemaphoreType.DMA((n,))],
        compiler_params=pltpu.CompilerParams(has_side_effects=True),
    )(*bufs)


def allgather8(v, name):
    r, n = v.shape

    def body(v_ref, out_ref, send_sems, recv_sems):
        x, y, c, _ = _place()
        me = 4 * x + 2 * y + c
        out_ref[me] = v_ref[...]

        def copy(k, slot, to):
            return pltpu.make_async_remote_copy(
                src_ref=v_ref, dst_ref=out_ref.at[slot], send_sem=send_sems.at[k - 1],
                recv_sem=recv_sems.at[k - 1], device_id=to, device_id_type=MESH)

        peers = []
        for k in range(1, 8):
            px = 1 - x if (k >> 2) & 1 else x
            py = 1 - y if (k >> 1) & 1 else y
            pc = 1 - c if k & 1 else c
            peers.append((px, py, pc))
            copy(k, me, (px, py, pc)).start()
        for k, (px, py, pc) in enumerate(peers, start=1):
            copy(k, 4 * px + 2 * py + pc, (x, y, c)).wait_recv()
        for k, peer in enumerate(peers, start=1):
            copy(k, me, peer).wait_send()

    return pl.pallas_call(
        body, name=name, in_specs=[VMEM], out_specs=VMEM,
        out_shape=jax.ShapeDtypeStruct((8, r, n), v.dtype),
        scratch_shapes=[pltpu.SemaphoreType.DMA((7,)), pltpu.SemaphoreType.DMA((7,))],
        compiler_params=_cp(has_side_effects=True),
    )(v)


HBM = pl.BlockSpec(memory_space=pltpu.HBM)
SEM = pl.BlockSpec(memory_space=pltpu.SEMAPHORE)
DATAFLOW = pltpu.SideEffectType.DATAFLOW_SIDE_EFFECTING


def _descriptors(plan, refs, send_sems, recv_sems, arrivals=True):
    x, y, c, _ = _place()
    sends, recvs = plan(refs)
    out = [pltpu.make_async_remote_copy(src_ref=src, dst_ref=dst, send_sem=send_sems.at[k],
                                        recv_sem=recv_sems.at[k], device_id=to, device_id_type=MESH)
           for k, (src, dst, to) in enumerate(sends)]
    if not arrivals:
        return out, []
    inn = [pltpu.make_async_remote_copy(src_ref=land, dst_ref=land, send_sem=send_sems.at[k],
                                        recv_sem=recv_sems.at[k], device_id=(x, y, c), device_id_type=MESH)
           for k, land in enumerate(recvs)]
    return out, inn


def copies_start(name, arrays, n_copies, plan, after):
    na = len(arrays)

    def body(*refs):
        out, _ = _descriptors(plan, refs[:na], refs[na + 1], refs[na + 2], arrivals=False)
        for cp in out:
            cp.start()
        refs[-1][...] = jnp.zeros((8, 128), F32)

    res = pl.pallas_call(
        body, name=name,
        out_shape=(pltpu.SemaphoreType.DMA((n_copies,)), pltpu.SemaphoreType.DMA((n_copies,)),
                   *[pltpu.HBM(a.shape, a.dtype) for a in arrays], jax.ShapeDtypeStruct((8, 128), F32)),
        in_specs=[HBM] * na + [ANY], out_specs=(SEM, SEM, *[HBM] * na, VMEM),
        input_output_aliases={i: i + 2 for i in range(na)},
        compiler_params=pltpu.CompilerParams(has_side_effects=DATAFLOW),
    )(*[pltpu.with_memory_space_constraint(a, pltpu.HBM) for a in arrays], after)
    return res[0], res[1], list(res[2:2 + na]), res[-1]


def copies_wait(name, started, plan, after):
    send_sems, recv_sems, arrays, _ = started
    na = len(arrays)

    def body(*refs):
        out, inn = _descriptors(plan, refs[:na], refs[na], refs[na + 1])
        for cp in out:
            cp.wait_send()
        for cp in inn:
            cp.wait_recv()
        refs[-1][...] = jnp.zeros((8, 128), F32)

    res = pl.pallas_call(
        body, name=name,
        out_shape=(*[pltpu.HBM(a.shape, a.dtype) for a in arrays], jax.ShapeDtypeStruct((8, 128), F32)),
        in_specs=[HBM] * na + [SEM, SEM, ANY], out_specs=(*[HBM] * na, VMEM),
        input_output_aliases={i: i for i in range(na)},
        compiler_params=pltpu.CompilerParams(has_side_effects=DATAFLOW),
    )(*arrays, send_sems, recv_sems, after)
    return list(res[:na]), res[-1]


def _rows_half(r, c):
    return pl.ds(c * (r // 2), r // 2), pl.ds((1 - c) * (r // 2), r // 2)


def plan_gather_ici(refs):
    x, y, c, chips = _place()
    p = 2 * x + y
    sends, recvs = [], []
    for buf in refs:
        mine, _ = _rows_half(buf.shape[1], c)
        for chip in chips:
            sends.append((buf.at[p, mine], buf.at[p, mine], (*chip, c)))
            recvs.append(buf.at[2 * chip[0] + chip[1], mine])
    return sends, recvs


def plan_gather_d2d(refs):
    x, y, c, chips = _place()
    sends, recvs = [], []
    for buf in refs:
        mine, theirs = _rows_half(buf.shape[1], c)
        for chip in chips:
            slot = 2 * chip[0] + chip[1]
            sends.append((buf.at[slot, mine], buf.at[slot, mine], (x, y, 1 - c)))
            recvs.append(buf.at[slot, theirs])
    return sends, recvs


def plan_exchange(refs):
    x, y, c, _ = _place()
    n = len(refs) // 2
    sends, recvs = [], []
    for grad, land in zip(refs[:n], refs[n:]):
        _, theirs = _rows_half(grad.shape[1], c)
        sends.append((grad.at[:, theirs], land, (x, y, 1 - c)))
        recvs.append(land)
    return sends, recvs


def plan_scatter(refs):
    x, y, c, chips = _place()
    n = len(refs) // 2
    sends, recvs = [], []
    for part, land in zip(refs[:n], refs[n:]):
        for j, chip in enumerate(chips):
            sends.append((part.at[2 * chip[0] + chip[1]], land.at[j], (*chip, c)))
            recvs.append(land.at[j])
    return sends, recvs


def plan_share(refs):
    x, y, c, _ = _place()
    sends, recvs = [], []
    for buf in refs:
        mine, theirs = _rows_half(buf.shape[0], c)
        sends.append((buf.at[mine], buf.at[mine], (x, y, 1 - c)))
        recvs.append(buf.at[theirs])
    return sends, recvs


def put_in_slot(w, chip, dtype, name):
    r, c = w.shape
    tr = 256 if r % 256 == 0 else r

    def body(chip_ref, w_ref, o_ref):
        o_ref[...] = w_ref[...].astype(dtype)

    grid_spec = pltpu.PrefetchScalarGridSpec(
        num_scalar_prefetch=1, grid=(r // tr,),
        in_specs=[pl.BlockSpec((tr, c), lambda i, chip_ref: (i, 0))],
        out_specs=pl.BlockSpec((None, tr, c), lambda i, chip_ref: (chip_ref[0], i, 0)))
    return pl.pallas_call(body, name=name, grid_spec=grid_spec,
                          out_shape=jax.ShapeDtypeStruct((4, r, c), dtype), compiler_params=_cp())(chip, w)


def ada_fwd(s_in, ada_w, ada_b, tn):
    nl, d, ws = ada_w.shape

    def body(s_ref, w_ref, b_ref, so_ref, mod_ref):
        s = _silu(s_ref[...])
        so_ref[...] = s
        mod_ref[...] = _dot(s.astype(BF16), w_ref[...].astype(BF16)) + b_ref[...]

    return pl.pallas_call(
        body, name="ada_fwd", grid=(nl, ws // tn),
        in_specs=[pl.BlockSpec((16, d), lambda l, j: (0, 0)),
                  pl.BlockSpec((None, d, tn), lambda l, j: (l, 0, j)),
                  pl.BlockSpec((None, 1, tn), lambda l, j: (l, 0, j))],
        out_specs=[pl.BlockSpec((16, d), lambda l, j: (0, 0)),
                   pl.BlockSpec((None, 16, tn), lambda l, j: (l, 0, j))],
        out_shape=[jax.ShapeDtypeStruct((16, d), F32), jax.ShapeDtypeStruct((nl, 16, ws), F32)],
        compiler_params=_cp(),
    )(s_in, ada_w, ada_b)


def _adamw_math(w, g, m, v):
    m = ADAM_B1 * m + (1.0 - ADAM_B1) * g
    v = ADAM_B2 * v + (1.0 - ADAM_B2) * (g * g)
    m_hat = m / (1.0 - ADAM_B1 ** ADAM_STEP)
    v_hat = v / (1.0 - ADAM_B2 ** ADAM_STEP)
    delta = -ADAM_LR * (m_hat / (jnp.sqrt(v_hat) + ADAM_EPS) + ADAM_WD * w)
    return delta, m, v


def ada_bwd_adamw(s, dm, w, m, v):
    nl, d, ws = w.shape
    tr = 256 if d % 256 == 0 else 128

    def body(s_ref, dm_ref, w_ref, m_ref, v_ref, g_ref, dl_ref, mo_ref, vo_ref, dc_ref):
        dmv = dm_ref[...].astype(BF16)
        wv = w_ref[...]
        g = _dot(s_ref[...].astype(BF16), dmv, TN)
        g_ref[...] = g
        dl_ref[...], mo_ref[...], vo_ref[...] = _adamw_math(wv, g, m_ref[...], v_ref[...])
        dc_ref[...] = _dot(dmv[8:16, :], wv.astype(BF16), NT)

    wblk = pl.BlockSpec((None, tr, ws), lambda l, i: (l, i, 0))
    wshape = jax.ShapeDtypeStruct((nl, d, ws), F32)
    return pl.pallas_call(
        body, name="ada_bwd_adamw", grid=(nl, d // tr),
        in_specs=[pl.BlockSpec((16, tr), lambda l, i: (0, i)),
                  pl.BlockSpec((None, 16, ws), lambda l, i: (l, 0, 0)), wblk, wblk, wblk],
        out_specs=[wblk, wblk, wblk, wblk, pl.BlockSpec((None, 8, tr), lambda l, i: (l, 0, i))],
        out_shape=[wshape, wshape, wshape, wshape, jax.ShapeDtypeStruct((nl, 8, d), F32)],
        compiler_params=_cp(),
    )(s, dm, w, m, v)


def adamw(w, g, m, v, name):
    r, c = w.shape
    tr = 256 if r % 256 == 0 else r

    def body(w_ref, g_ref, m_ref, v_ref, dl_ref, mo_ref, vo_ref):
        dl_ref[...], mo_ref[...], vo_ref[...] = _adamw_math(w_ref[...], g_ref[...], m_ref[...], v_ref[...])

    blk = pl.BlockSpec((tr, c), lambda i: (i, 0))
    shape = jax.ShapeDtypeStruct((r, c), F32)
    return pl.pallas_call(body, name=name, grid=(r // tr,), in_specs=[blk] * 4, out_specs=[blk] * 3,
                          out_shape=[shape] * 3, compiler_params=_cp())(w, g, m, v)


SMALL_ROWS = 24
ROW_MOD = 10


def small_reduce(gathered):
    _, rows, d = gathered.shape

    def body(g_ref, o_ref):
        tot = g_ref[0]
        for b in range(1, 8):
            tot = tot + g_ref[b]
        o_ref[0:rows, :] = tot
        for layer in range(2):
            lat = ROW_MOD + 6 * layer
            o_ref[24 + 3 * layer:27 + 3 * layer, :] = tot[lat:lat + 3, :] + tot[lat + 3:lat + 6, :]
        o_ref[30:32, :] = jnp.zeros((2, d), F32)

    return pl.pallas_call(body, name="small_reduce", in_specs=[VMEM], out_specs=VMEM,
                          out_shape=jax.ShapeDtypeStruct((32, d), F32), compiler_params=_cp())(gathered)


def lb_logits_grad(lbl, dlb):
    _, _, n = lbl.shape

    def body(l_ref, d_ref, o_ref):
        for dr in range(2):
            _, (p0, p1, p2) = _lower_bound(l_ref, dr)
            dv = d_ref[dr:dr + 1, :]
            o_ref[dr, 0:1, :] = p0 * p2 * dv
            o_ref[dr, 1:2, :] = p1 * p2 * dv
            o_ref[dr, 2:3, :] = -p2 * (p0 + p1) * dv

    return pl.pallas_call(body, name="lb_logits_grad", in_specs=[VMEM, VMEM], out_specs=VMEM,
                          out_shape=jax.ShapeDtypeStruct((2, 3, n), F32), compiler_params=_cp())(lbl, dlb)


def c_ctx_grad(parts, c_ctx):
    d = c_ctx.shape[1]

    def body(p_ref, c_ref, o_ref):
        tot = p_ref[0, 0:1, :]
        for chip in range(1, 4):
            tot = tot + p_ref[2 * chip, 0:1, :]
        o_ref[...] = tot * _dsilu(c_ref[...])

    return pl.pallas_call(body, name="c_ctx_grad", in_specs=[VMEM, VMEM], out_specs=VMEM,
                          out_shape=jax.ShapeDtypeStruct((1, d), F32), compiler_params=_cp())(parts, c_ctx)


def _reduce_scatter(grads, core, chip_core):
    got = exchange_halves(grads)
    sums = [pair_sum(g, r, core) for g, r in zip(grads, got)]
    recv = scatter_to_owners([sb for _, sb in sums])
    reduced = [owner_sum(s, r, chip_core) for (s, _), r in zip(sums, recv)]
    return share_halves(reduced)


def kernel(x, c, ctx, c_ctx, ada_w, ada_b, pre_g, post_g, ev_w_in, ev_pool_w, ev_pool_scale, ev_conv_w, ev_conv_b, ev_w_out, od_w_in, od_onorm_g, od_w_out, lb_logits, loss_target, m_c_ctx, m_ada_w, m_ada_b, m_pre_g, m_post_g, m_ev_w_in, m_ev_pool_w, m_ev_pool_scale, m_ev_conv_w, m_ev_conv_b, m_ev_w_out, m_od_w_in, m_od_onorm_g, m_od_w_out, m_lb_logits, v_c_ctx, v_ada_w, v_ada_b, v_pre_g, v_post_g, v_ev_w_in, v_ev_pool_w, v_ev_pool_scale, v_ev_conv_w, v_ev_conv_b, v_ev_w_out, v_od_w_in, v_od_onorm_g, v_od_w_out, v_lb_logits):
    _, seq, d = x.shape
    cx = ctx.shape[1]
    t = cx + seq
    half_d = d // 2
    g = half_d // N_POOL
    tn = d // 4
    xi, yi, ci = lax.axis_index("x"), lax.axis_index("y"), lax.axis_index("c")
    chip = 2 * xi + yi
    me = 2 * chip + ci
    core_arr = jnp.reshape(ci, (1,)).astype(jnp.int32)
    chip_arr = jnp.reshape(chip, (1,)).astype(jnp.int32)
    chip_core_arr = jnp.stack([chip, ci]).astype(jnp.int32)

    pad = lambda a, rows: jnp.concatenate([a, jnp.zeros((rows - a.shape[0], g), F32)], axis=0)
    small = jnp.concatenate([
        ev_pool_w.reshape(g, g), pad(ev_conv_w.reshape(3, g), 8), pad(od_onorm_g.reshape(2, g), 8),
        pad(lb_logits.reshape(12, g), 16)], axis=0)
    ev_in_g, ev_out_g, small_g, ev_done = allgather_shards([
        put_in_slot(ev_w_in[0], chip_arr, BF16, "cast_ev_w_in"),
        put_in_slot(ev_w_out[0], chip_arr, BF16, "cast_ev_w_out"),
        put_in_slot(small, chip_arr, F32, "place_small")])
    ev_out3 = ev_out_g.reshape(1, d, d)
    pool_w_full = small_g[:, :g].reshape(4, N_POOL, g // 4, g).transpose(1, 0, 2, 3).reshape(N_POOL, g, g)
    conv_w_full = small_g[:, g:g + 3].transpose(1, 0, 2).reshape(3, half_d)
    onorm_full = small_g[:, g + 8:g + 10].reshape(1, d)
    lbl_full = small_g[:, g + 16:g + 28].reshape(4, 2, 3, 2 * g).transpose(1, 2, 0, 3).reshape(2, 3, d)

    c_rows = jnp.concatenate([c + ev_done[0:1, 0:1], jnp.zeros((7, d), F32)], axis=0)
    c_all = allgather8(c_rows, "allgather_c")[:, 0, :]
    s_in = jnp.concatenate([c_all, c_ctx.reshape(1, d), jnp.zeros((7, d), F32)], axis=0)
    ws_ada = ada_w.shape[2]
    ada_b_mine = lax.dynamic_slice(ada_b, (0, chip * ws_ada), (2, ws_ada)).reshape(2, 1, ws_ada)
    s_act, mod_mine = ada_fwd(s_in, ada_w, ada_b_mine, tn)
    mod_all = allgather8(mod_mine.reshape(32, ws_ada), "allgather_mod")
    od_ici = copies_start("gather_od_ici_start", [
        put_in_slot(od_w_in[0], chip_arr, BF16, "cast_od_w_in"),
        put_in_slot(od_w_out[0], chip_arr, BF16, "cast_od_w_out")], 6, plan_gather_ici, mod_all)
    mod_full = mod_all[0::2].reshape(4, 2, 16, ws_ada).transpose(1, 2, 0, 3).reshape(2, 16, 3 * d)
    mod_lat = lax.dynamic_slice(mod_full, (0, me, 0), (2, 1, 3 * d))
    mods = jnp.concatenate([mod_full[:, 8:9], mod_lat], axis=1)
    shift, scale, gate = mods[:, :, :d], mods[:, :, d:2 * d], mods[:, :, 2 * d:]

    xs = jnp.concatenate([ctx[0], x[0]], axis=0)

    h0 = normmod_fwd(xs, pre_g[0:1] + od_ici[3][0:1, 0:1], shift[0], scale[0], cx)
    z0 = mm_nn(h0, ev_in_g, half_d, tn, "mm_ev_in")
    u_a = mix_a_fwd(z0, pool_w_full, ev_pool_scale, cx)
    u_b = mix_b_fwd(z0, conv_w_full, ev_conv_b, cx)
    u = jnp.concatenate([u_a, u_b], axis=1)
    y0 = mm_nn(u, ev_out3, d, tn, "mm_ev_out")[0]
    xs1 = post_fwd(xs, y0, post_g[0:1], gate[0], cx)
    od_d2d = copies_start("gather_od_d2d_start",
                          copies_wait("gather_od_ici_wait", od_ici, plan_gather_ici, xs1)[0],
                          6, plan_gather_d2d, xs1)
    (od_in_g, od_out_g), _ = copies_wait("gather_od_d2d_wait", od_d2d, plan_gather_d2d, od_d2d[3])
    od_out3 = od_out_g.reshape(1, d, d)

    h1 = normmod_fwd(xs1, pre_g[1:2], shift[1], scale[1], cx)
    z1 = mm_nn(h1, od_in_g, d, tn, "mm_od_in")
    o1, r1 = hgrn_fwd(z1, lbl_full, onorm_full, cx)
    y1 = mm_nn(r1, od_out3, d, tn, "mm_od_out")[0]
    sq, dx2 = post_loss(xs1, y1, post_g[1:2], gate[1], loss_target[0], cx)
    loss = lax.psum(sq[0, 0] * (0.5 / d), ("x", "y", "c"))

    dy1, dgate1, dpost1 = post_bwd(dx2, y1, post_g[1:2], gate[1], cx, True)
    dr1 = mm_nt(dy1[None], od_out3, tn, "mm_od_out_dx")
    g_od_out = mm_tn(r1, dy1[None], d, tn, "mm_od_out_dw")
    dz1, donorm, dlb = hgrn_bwd(z1, lbl_full, onorm_full, o1, dr1, cx)
    dh1 = mm_nt(dz1, od_in_g, tn, "mm_od_in_dx")
    g_od_in = mm_tn(h1, dz1, od_in_g.shape[2], tn, "mm_od_in_dw")
    dxs1, dpre1, dshift1, dscale1 = normmod_bwd(xs1, dh1, pre_g[1:2], scale[1], dx2, cx, True)

    od_grads = [g_od_in, g_od_out.reshape(4, d // 4, d)]
    half_zone = lambda a, lead, dt: lax.empty((lead, a.shape[1] // 2, a.shape[2]), dt)
    od_ex = copies_start("reduce_od_exchange_start", od_grads + [half_zone(a, 4, a.dtype) for a in od_grads],
                         2, plan_exchange, dxs1)

    dy0, dgate0, dpost0 = post_bwd(dxs1, y0, post_g[0:1] + od_ex[3][0:1, 0:1], gate[0], cx, False)
    du = mm_nt(dy0[None], ev_out3, tn, "mm_ev_out_dx")
    g_ev_out = mm_tn(u, dy0[None], d, tn, "mm_ev_out_dw")
    od_got, _ = copies_wait("reduce_od_exchange_wait", od_ex, plan_exchange, g_ev_out)
    od_sums = [pair_sum(od_got[i], od_got[2 + i], chip_core_arr) for i in range(2)]
    od_sc = copies_start("reduce_od_scatter_start",
                         [sb for _, sb in od_sums] + [half_zone(a, 3, BF16) for a in od_grads],
                         6, plan_scatter, du)
    dz0a, g_pool_w, dpool_scale = mix_a_bwd(z0, du, pool_w_full, ev_pool_scale + od_sc[3][0:1, 0:1], cx)
    dz0b, dconv_w, dconv_b = mix_b_bwd(z0, du, conv_w_full, ev_conv_b + od_sc[3][0:1, 0:1], cx)
    dz0 = jnp.concatenate([dz0a, dz0b], axis=0)
    dh0 = mm_nt(dz0, ev_in_g, tn, "mm_ev_in_dx")
    g_ev_in = mm_tn(h0, dz0, ev_in_g.shape[2], tn, "mm_ev_in_dw")
    dxs0, dpre0, dshift0, dscale0 = normmod_bwd(xs, dh0, pre_g[0:1], scale[0], dxs1, cx, False, True)
    grad_x = dxs0[None]
    od_recv, _ = copies_wait("reduce_od_scatter_wait", od_sc, plan_scatter, dxs0)
    ev_grads = [g_ev_in, g_ev_out.reshape(4, d // 4, d), g_pool_w.reshape(4, g, g)]
    ev_ex = copies_start("reduce_ev_exchange_start", ev_grads + [half_zone(a, 4, a.dtype) for a in ev_grads],
                         3, plan_exchange, dxs0)
    od_sh = copies_start("reduce_od_share_start",
                         [owner_sum(od_sums[i][0], od_recv[2 + i], chip_core_arr) for i in range(2)],
                         2, plan_share, ev_ex[3])

    zrow = jnp.zeros((1, d), F32)
    small_rows = jnp.concatenate([
        dpre0, dpre1, dpost0, dpost1,
        jnp.concatenate([dpool_scale, dconv_b], axis=1),
        jnp.concatenate([dconv_w.reshape(1, 3 * half_d), jnp.zeros((1, half_d), F32)], axis=1).reshape(2, d),
        donorm, dlb,
        dshift0[1:2], dscale0[1:2], dgate0[1:2], dshift0[0:1], dscale0[0:1], dgate0[0:1],
        dshift1[1:2], dscale1[1:2], dgate1[1:2], dshift1[0:1], dscale1[0:1], zrow,
        zrow, zrow], axis=0)
    small_all = allgather8(small_rows, "allgather_small")
    tot = small_reduce(small_all)

    dm_rows = []
    for layer in range(2):
        lat = ROW_MOD + 6 * layer
        dm_lat = small_all[:, lat:lat + 3].reshape(8, 3 * d)
        dm_ctx = tot[lat + 3:lat + 6].reshape(1, 3 * d)
        dm_rows.append(jnp.concatenate([dm_lat, dm_ctx, jnp.zeros((7, 3 * d), F32)], axis=0))
    dm_full = jnp.stack(dm_rows)
    dm_mine = lax.dynamic_slice(dm_full, (0, 0, chip * ws_ada), (2, 16, ws_ada))

    ev_got, _ = copies_wait("reduce_ev_exchange_wait", ev_ex, plan_exchange, tot)
    ev_sums = [pair_sum(ev_got[i], ev_got[3 + i], chip_core_arr) for i in range(3)]
    ev_sc = copies_start("reduce_ev_scatter_start",
                         [sb for _, sb in ev_sums] + [half_zone(a, 3, BF16) for a in ev_grads],
                         9, plan_scatter, tot)
    behind = ev_sc[3][0:1, 0:1]
    grad_ada_w, delta_ada_w, new_m_ada_w, new_v_ada_w, dctx_part = ada_bwd_adamw(
        s_act, dm_mine + behind, ada_w, m_ada_w, v_ada_w)
    (grad_od_w_in, grad_od_w_out), _ = copies_wait("reduce_od_share_wait", od_sh, plan_share, ev_sc[3])
    od_w_in_upd = adamw(od_w_in[0], grad_od_w_in, m_od_w_in[0], v_od_w_in[0], "adamw_od_w_in")
    od_w_out_upd = adamw(od_w_out[0], grad_od_w_out, m_od_w_out[0], v_od_w_out[0], "adamw_od_w_out")
    ev_recv, ev_landed = copies_wait("reduce_ev_scatter_wait", ev_sc, plan_scatter, od_w_in_upd[0])
    grad_ev_w_in, grad_ev_w_out, grad_pool_w = share_halves(
        [owner_sum(ev_sums[i][0], ev_recv[3 + i], chip_core_arr) for i in range(3)])
    dctx_all = allgather8(dctx_part[0] + dctx_part[1] + ev_landed[0:1, 0:1], "allgather_dctx")
    grad_c_ctx = c_ctx_grad(dctx_all, c_ctx.reshape(1, d)).reshape(d)

    grad_ada_b = tot[24:30].reshape(2, 3 * d)
    grad_pre_g = tot[0:2]
    grad_post_g = tot[2:4]
    grad_ev_pool_scale = tot[4:5, :half_d]
    grad_ev_conv_b = tot[4:5, half_d:]
    conv_w_tot = tot[5:7].reshape(1, 2 * d)[:, :3 * half_d].reshape(3, N_POOL, g)
    grad_ev_conv_w = lax.dynamic_slice(conv_w_tot, (0, chip, 0), (3, 1, g)).reshape(1, 3, g)
    grad_od_onorm_g = lax.dynamic_slice(tot[7:8], (0, chip * 2 * g), (1, 2 * g))
    dlb_mine = lax.dynamic_slice(tot[8:10], (0, chip * 2 * g), (2, 2 * g))
    grad_lb_logits = lb_logits_grad(lb_logits, dlb_mine)
    grad_ev_w_in = grad_ev_w_in[None]
    grad_od_w_in = grad_od_w_in[None]
    grad_ev_w_out = grad_ev_w_out[None]
    grad_od_w_out = grad_od_w_out[None]
    grad_ev_pool_w = grad_pool_w.reshape(1, N_POOL, g // 4, g)

    def step(w, gr, m, v, name):
        shape = w.shape
        cols = shape[-1]
        two_d = lambda a: a.reshape(-1, cols)
        dl, mo, vo = adamw(two_d(w), two_d(gr), two_d(m), two_d(v), "adamw_" + name)
        return dl.reshape(shape), mo.reshape(shape), vo.reshape(shape)

    upd = {
        "c_ctx": step(c_ctx, grad_c_ctx, m_c_ctx, v_c_ctx, "c_ctx"),
        "ada_w": (delta_ada_w, new_m_ada_w, new_v_ada_w),
        "ada_b": step(ada_b, grad_ada_b, m_ada_b, v_ada_b, "ada_b"),
        "pre_g": step(pre_g, grad_pre_g, m_pre_g, v_pre_g, "pre_g"),
        "post_g": step(post_g, grad_post_g, m_post_g, v_post_g, "post_g"),
        "ev_w_in": step(ev_w_in, grad_ev_w_in, m_ev_w_in, v_ev_w_in, "ev_w_in"),
        "ev_pool_w": step(ev_pool_w, grad_ev_pool_w, m_ev_pool_w, v_ev_pool_w, "ev_pool_w"),
        "ev_pool_scale": step(ev_pool_scale, grad_ev_pool_scale, m_ev_pool_scale, v_ev_pool_scale, "ev_pool_scale"),
        "ev_conv_w": step(ev_conv_w, grad_ev_conv_w, m_ev_conv_w, v_ev_conv_w, "ev_conv_w"),
        "ev_conv_b": step(ev_conv_b, grad_ev_conv_b, m_ev_conv_b, v_ev_conv_b, "ev_conv_b"),
        "ev_w_out": step(ev_w_out, grad_ev_w_out, m_ev_w_out, v_ev_w_out, "ev_w_out"),
        "od_w_in": tuple(a[None] for a in od_w_in_upd),
        "od_onorm_g": step(od_onorm_g, grad_od_onorm_g, m_od_onorm_g, v_od_onorm_g, "od_onorm_g"),
        "od_w_out": tuple(a[None] for a in od_w_out_upd),
        "lb_logits": step(lb_logits, grad_lb_logits, m_lb_logits, v_lb_logits, "lb_logits"),
    }
    names = ["c_ctx", "ada_w", "ada_b", "pre_g", "post_g", "ev_w_in", "ev_pool_w", "ev_pool_scale",
             "ev_conv_w", "ev_conv_b", "ev_w_out", "od_w_in", "od_onorm_g", "od_w_out", "lb_logits"]
    grads = [grad_c_ctx, grad_ada_w, grad_ada_b, grad_pre_g, grad_post_g, grad_ev_w_in, grad_ev_pool_w,
             grad_ev_pool_scale, grad_ev_conv_w, grad_ev_conv_b, grad_ev_w_out, grad_od_w_in,
             grad_od_onorm_g, grad_od_w_out, grad_lb_logits]
    return (loss, grad_x, *grads, *[upd[k][0] for k in names], *[upd[k][1] for k in names],
            *[upd[k][2] for k in names])
```

```python
import functools

import jax
import jax.numpy as jnp
from jax import lax
from jax.experimental import pallas as pl
from jax.experimental.pallas import tpu as pltpu

EPS = 1e-6
GRID_W_LOG2 = 6
CHUNK = 64
HEAD = 128
N_POOL = 4
ADAM_LR, ADAM_B1, ADAM_B2, ADAM_EPS, ADAM_WD, ADAM_STEP = 0.001, 0.9, 0.999, 1e-08, 0.01, 10
VMEM_LIMIT = 56 * 1024 * 1024
MESH = pl.DeviceIdType.MESH
F32, BF16 = jnp.float32, jnp.bfloat16
ANY = pl.BlockSpec(memory_space=pl.ANY)
VMEM = pl.BlockSpec(memory_space=pltpu.VMEM)


def _cp(**kw):
    return pltpu.CompilerParams(vmem_limit_bytes=VMEM_LIMIT, **kw)


def _silu(x):
    return x * jax.nn.sigmoid(x)


def _dsilu(x):
    s = jax.nn.sigmoid(x)
    return s * (1.0 + x * (1.0 - s))


def _dot(a, b, dims=((1,), (0,)), precision=None):
    return lax.dot_general(a, b, (dims, ((), ())), preferred_element_type=F32, precision=precision)


NN = ((1,), (0,))
NT = ((1,), (1,))
TN = ((0,), (0,))


def _row_block(cx):
    return 256 if cx % 256 == 0 else 128


def normmod_fwd(xs, g, shift, scale, cx):
    t, d = xs.shape
    tm = _row_block(cx)
    nctx = cx // tm

    def body(x_ref, g_ref, sh_ref, sc_ref, h_ref):
        is_ctx = pl.program_id(0) < nctx
        x = x_ref[...]
        rstd = lax.rsqrt(jnp.mean(x * x, axis=-1, keepdims=True) + EPS)
        sc = jnp.where(is_ctx, sc_ref[0:1, :], sc_ref[1:2, :])
        sh = jnp.where(is_ctx, sh_ref[0:1, :], sh_ref[1:2, :])
        h_ref[...] = ((x * rstd) * g_ref[...] * (1.0 + sc) + sh).astype(BF16)

    row = pl.BlockSpec((tm, d), lambda i: (i, 0))
    vec = lambda r: pl.BlockSpec((r, d), lambda i: (0, 0))
    return pl.pallas_call(
        body, name="normmod_fwd", grid=(t // tm,),
        in_specs=[row, vec(1), vec(2), vec(2)], out_specs=row,
        out_shape=jax.ShapeDtypeStruct((t, d), BF16), compiler_params=_cp(),
    )(xs, g, shift, scale)


def normmod_bwd(xs, dh, g, scale, dres, cx, res_is_latent_only, dx_latent_only=False):
    t, d = xs.shape
    tm = _row_block(cx)
    nctx = cx // tm

    def body(x_ref, dh_ref, g_ref, sc_ref, dres_ref, dx_ref, dg_ref, dsh_ref, dsc_ref):
        i = pl.program_id(0)
        is_ctx = i < nctx

        @pl.when(i == 0)
        def _():
            dg_ref[...] = jnp.zeros_like(dg_ref)
            dsh_ref[...] = jnp.zeros_like(dsh_ref)
            dsc_ref[...] = jnp.zeros_like(dsc_ref)

        x = x_ref[...]
        dh = dh_ref[...]
        gv = g_ref[...]
        rstd = lax.rsqrt(jnp.mean(x * x, axis=-1, keepdims=True) + EPS)
        xhat = x * rstd
        sc = jnp.where(is_ctx, sc_ref[0:1, :], sc_ref[1:2, :])
        dsh = jnp.sum(dh, axis=0, keepdims=True)
        dhx = dh * xhat
        dsc = jnp.sum(dhx * gv, axis=0, keepdims=True)
        dg_ref[...] += jnp.sum(dhx * (1.0 + sc), axis=0, keepdims=True)
        zero = jnp.zeros_like(dsh)
        dsh_ref[0:1, :] += jnp.where(is_ctx, dsh, zero)
        dsh_ref[1:2, :] += jnp.where(is_ctx, zero, dsh)
        dsc_ref[0:1, :] += jnp.where(is_ctx, dsc, zero)
        dsc_ref[1:2, :] += jnp.where(is_ctx, zero, dsc)
        dxhat = dh * (gv * (1.0 + sc))
        dx = rstd * (dxhat - xhat * jnp.mean(dxhat * xhat, axis=-1, keepdims=True))
        res = dres_ref[...]
        if res_is_latent_only:
            res = jnp.where(is_ctx, jnp.zeros_like(res), res)
        dx_ref[...] = dx + res

    row = pl.BlockSpec((tm, d), lambda i: (i, 0))
    if res_is_latent_only:
        res_spec = pl.BlockSpec((tm, d), lambda i: (jnp.maximum(i - nctx, 0), 0))
    else:
        res_spec = row
    vec = lambda r: pl.BlockSpec((r, d), lambda i: (0, 0))
    dx_spec = pl.BlockSpec((tm, d), lambda i: (jnp.maximum(i - nctx, 0), 0)) if dx_latent_only else row
    return pl.pallas_call(
        body, name="normmod_bwd", grid=(t // tm,),
        in_specs=[row, row, vec(1), vec(2), res_spec],
        out_specs=[dx_spec, vec(1), vec(2), vec(2)],
        out_shape=[jax.ShapeDtypeStruct((t - cx if dx_latent_only else t, d), F32), jax.ShapeDtypeStruct((1, d), F32),
                   jax.ShapeDtypeStruct((2, d), F32), jax.ShapeDtypeStruct((2, d), F32)],
        compiler_params=_cp(),
    )(xs, dh, g, scale, dres)


def post_fwd(xs, y, pg, gate, cx):
    t, d = xs.shape
    tm = _row_block(cx)
    nctx = cx // tm

    def body(x_ref, y_ref, pg_ref, gate_ref, o_ref):
        is_ctx = pl.program_id(0) < nctx
        y = y_ref[...]
        rstd = lax.rsqrt(jnp.mean(y * y, axis=-1, keepdims=True) + EPS)
        gt = jnp.where(is_ctx, gate_ref[0:1, :], gate_ref[1:2, :])
        o_ref[...] = x_ref[...] + gt * ((y * rstd) * pg_ref[...])

    row = pl.BlockSpec((tm, d), lambda i: (i, 0))
    vec = lambda r: pl.BlockSpec((r, d), lambda i: (0, 0))
    return pl.pallas_call(
        body, name="post_fwd", grid=(t // tm,),
        in_specs=[row, row, vec(1), vec(2)], out_specs=row,
        out_shape=jax.ShapeDtypeStruct((t, d), F32), compiler_params=_cp(),
    )(xs, y, pg, gate)


def post_loss(xs, y, pg, gate, target, cx):
    t, d = xs.shape
    n = y.shape[0]
    tm = _row_block(cx)
    nctx = cx // tm

    def body(x_ref, y_ref, pg_ref, gate_ref, tgt_ref, sq_ref, dx_ref):
        @pl.when(pl.program_id(0) == 0)
        def _():
            sq_ref[...] = jnp.zeros_like(sq_ref)

        y = y_ref[...]
        rstd = lax.rsqrt(jnp.mean(y * y, axis=-1, keepdims=True) + EPS)
        x2 = x_ref[...] + gate_ref[1:2, :] * ((y * rstd) * pg_ref[...])
        err = x2 - tgt_ref[...]
        sq_ref[...] += jnp.sum(err * err)
        dx_ref[...] = err * (1.0 / d)

    row = pl.BlockSpec((tm, d), lambda i: (i, 0))
    xrow = pl.BlockSpec((tm, d), lambda i: (i + nctx, 0))
    vec = lambda r: pl.BlockSpec((r, d), lambda i: (0, 0))
    return pl.pallas_call(
        body, name="post_loss", grid=(n // tm,),
        in_specs=[xrow, row, vec(1), vec(2), row],
        out_specs=[pl.BlockSpec((8, 128), lambda i: (0, 0)), row],
        out_shape=[jax.ShapeDtypeStruct((8, 128), F32), jax.ShapeDtypeStruct((n, d), F32)],
        compiler_params=_cp(),
    )(xs, y, pg, gate, target)


def post_bwd(dxo, y, pg, gate, cx, latent_only):
    m, d = y.shape
    tm = _row_block(cx)
    nctx = 0 if latent_only else cx // tm

    def body(dx_ref, y_ref, pg_ref, gate_ref, dy_ref, dgate_ref, dpg_ref):
        i = pl.program_id(0)
        is_ctx = i < nctx

        @pl.when(i == 0)
        def _():
            dgate_ref[...] = jnp.zeros_like(dgate_ref)
            dpg_ref[...] = jnp.zeros_like(dpg_ref)

        y = y_ref[...]
        dx = dx_ref[...]
        pgv = pg_ref[...]
        rstd = lax.rsqrt(jnp.mean(y * y, axis=-1, keepdims=True) + EPS)
        yhat = y * rstd
        gt = jnp.where(is_ctx, gate_ref[0:1, :], gate_ref[1:2, :])
        dxy = dx * yhat
        dgt = jnp.sum(dxy * pgv, axis=0, keepdims=True)
        zero = jnp.zeros_like(dgt)
        dgate_ref[0:1, :] += jnp.where(is_ctx, dgt, zero)
        dgate_ref[1:2, :] += jnp.where(is_ctx, zero, dgt)
        dpg_ref[...] += jnp.sum(dxy * gt, axis=0, keepdims=True)
        dyhat = dx * (gt * pgv)
        dy = rstd * (dyhat - yhat * jnp.mean(dyhat * yhat, axis=-1, keepdims=True))
        dy_ref[...] = dy.astype(BF16)

    row = pl.BlockSpec((tm, d), lambda i: (i, 0))
    vec = lambda r: pl.BlockSpec((r, d), lambda i: (0, 0))
    return pl.pallas_call(
        body, name="post_bwd", grid=(m // tm,),
        in_specs=[row, row, vec(1), vec(2)], out_specs=[row, vec(2), vec(1)],
        out_shape=[jax.ShapeDtypeStruct((m, d), BF16), jax.ShapeDtypeStruct((2, d), F32),
                   jax.ShapeDtypeStruct((1, d), F32)],
        compiler_params=_cp(),
    )(dxo, y, pg, gate)


def _split_rows(m):
    for cand in (1024, 768, 512, 384, 256, 128):
        if m % cand == 0 and m // cand >= 2:
            return cand
    return m


def mm_nn(a, w3, sec, tn, name):
    m, k = a.shape
    q, _, ws = w3.shape
    n = q * ws
    tpq, tps = ws // tn, sec // tn
    tm = next(c for c in (768, 512, 256, 128) if m % c == 0)

    def body(a_ref, w_ref, o_ref):
        w = w_ref[...]

        def step(i, carry):
            rows = pl.ds(pl.multiple_of(i * tm, tm), tm)
            o_ref[rows, :] = _dot(a_ref[rows, :], w)
            return carry

        lax.fori_loop(0, m // tm, step, 0)

    return pl.pallas_call(
        body, name=name, grid=(n // tn,),
        in_specs=[pl.BlockSpec((m, k), lambda j: (0, 0)),
                  pl.BlockSpec((None, k, tn), lambda j: (j // tpq, 0, j % tpq))],
        out_specs=pl.BlockSpec((None, m, tn), lambda j: (j // tps, 0, j % tps)),
        out_shape=jax.ShapeDtypeStruct((n // sec, m, sec), F32), compiler_params=_cp(),
    )(a, w3)


def _two_stacks(a3, b3, tn):
    sec = a3.shape[2]
    tps = sec // tn
    n1 = a3.shape[0] * tps
    first = lambda j: (jnp.minimum(j, n1 - 1) // tps, jnp.minimum(j, n1 - 1) % tps)
    second = lambda j: (jnp.maximum(j - n1, 0) // tps, jnp.maximum(j - n1, 0) % tps)
    return n1, first, second


def mm_nt(a3, b3, w3, tn, name):
    if b3 is None:
        b3 = a3
    _, m, sec = a3.shape
    q, k, ws = w3.shape
    n = q * ws
    tpq = ws // tn
    mb = _split_rows(m)
    n1, first, second = _two_stacks(a3, b3, tn)

    def body(a_ref, b_ref, w_ref, o_ref):
        j = pl.program_id(1)

        @pl.when(j == 0)
        def _():
            o_ref[...] = jnp.zeros_like(o_ref)

        @pl.when(j < n1)
        def _():
            o_ref[...] += _dot(a_ref[...], w_ref[...], NT)

        @pl.when(j >= n1)
        def _():
            o_ref[...] += _dot(b_ref[...], w_ref[...], NT)

    return pl.pallas_call(
        body, name=name, grid=(m // mb, n // tn),
        in_specs=[pl.BlockSpec((None, mb, tn), lambda i, j: (first(j)[0], i, first(j)[1])),
                  pl.BlockSpec((None, mb, tn), lambda i, j: (second(j)[0], i, second(j)[1])),
                  pl.BlockSpec((None, k, tn), lambda i, j: (j // tpq, 0, j % tpq))],
        out_specs=pl.BlockSpec((mb, k), lambda i, j: (i, 0)),
        out_shape=jax.ShapeDtypeStruct((m, k), F32), compiler_params=_cp(),
    )(a3, b3, w3)


def mm_tn(a, b3, c3, ws, tn, name):
    m, k = a.shape
    sec = b3.shape[2]
    n = (b3.shape[0] + (0 if c3 is None else c3.shape[0])) * sec
    if c3 is None:
        c3 = b3
    tpq = ws // tn
    kb = 256 if k % 256 == 0 else 128
    n1, first, second = _two_stacks(b3, c3, tn)

    def body(a_ref, b_ref, c_ref, o_ref):
        def product(rhs_ref):
            rhs = rhs_ref[...]
            for i in range(k // kb):
                o_ref[i * kb:(i + 1) * kb, :] = _dot(a_ref[:, i * kb:(i + 1) * kb], rhs, TN).astype(BF16)

        @pl.when(pl.program_id(0) < n1)
        def _():
            product(b_ref)

        @pl.when(pl.program_id(0) >= n1)
        def _():
            product(c_ref)

    return pl.pallas_call(
        body, name=name, grid=(n // tn,),
        in_specs=[pl.BlockSpec((m, k), lambda j: (0, 0)),
                  pl.BlockSpec((None, m, tn), lambda j: (first(j)[0], 0, first(j)[1])),
                  pl.BlockSpec((None, m, tn), lambda j: (second(j)[0], 0, second(j)[1]))],
        out_specs=pl.BlockSpec((None, k, tn), lambda j: (j // tpq, 0, j % tpq)),
        out_shape=jax.ShapeDtypeStruct((n // ws, k, ws), BF16), compiler_params=_cp(),
    )(a, b3, c3)


POOL_REACH = 8 << GRID_W_LOG2


def _token_parts(tok, cx):
    lat = tok - cx
    return tok < cx, lat >> GRID_W_LOG2, lat & ((1 << GRID_W_LOG2) - 1)


def _pool_mask(gi, row0, col0, tm, t, cx, transposed):
    half = jnp.left_shift(1, gi)
    r = lax.broadcasted_iota(jnp.int32, (tm, 1), 0) + row0
    c = lax.broadcasted_iota(jnp.int32, (1, tm), 1) + col0
    out_tok, src_tok = (c, r) if transposed else (r, c)
    o_ctx, o_row, o_col = _token_parts(out_tok, cx)
    s_ctx, s_row, s_col = _token_parts(src_tok, cx)

    def inside(o, s):
        return (s >= o - half) & (s <= o + half - 1)

    ctx_hit = o_ctx & s_ctx & inside(out_tok, src_tok)
    lat_hit = (~o_ctx) & (~s_ctx) & inside(o_row, s_row) & inside(o_col, s_col)
    return jnp.where((ctx_hit | lat_hit) & (c >= 0) & (c < t), 1.0, 0.0).astype(BF16)


def _pool_inv_count(gi, row0, tm, cx, seq):
    half = jnp.left_shift(1, gi)
    r = lax.broadcasted_iota(jnp.int32, (tm, 1), 0) + row0
    is_ctx, row, col = _token_parts(r, cx)

    def count(pos, size):
        return jnp.minimum(pos + half - 1, size - 1) - jnp.maximum(pos - half, 0) + 1

    cnt = jnp.where(is_ctx, count(r, cx), count(row, seq >> GRID_W_LOG2) * count(col, 1 << GRID_W_LOG2))
    return 1.0 / cnt.astype(F32)


def _window_sum(gi, i, tm, t, cx, src_ref, transposed):
    side = POOL_REACH // tm
    acc = None
    for off in range(-side, side + 1):
        blk = jnp.clip(i + off, 0, t // tm - 1)
        mask = _pool_mask(gi, i * tm, (i + off) * tm, tm, t, cx, transposed)
        part = _dot(mask, src_ref[pl.ds(pl.multiple_of(blk * tm, tm), tm), :])
        acc = part if acc is None else acc + part
    return acc


def mix_a_fwd(z0, pool_w, pool_scale, cx):
    _, t, half_d = z0.shape
    g = half_d // N_POOL
    seq = t - cx
    tm = _row_block(cx)

    def body(v_ref, ag_ref, w_ref, sc_ref, u_ref, vb_ref):
        gi = pl.program_id(0)
        vb_ref[...] = v_ref[...].astype(BF16)
        w = w_ref[...].astype(BF16)
        sc = sc_ref[...]

        def step(i, carry):
            row0 = pl.multiple_of(i * tm, tm)
            rows = pl.ds(row0, tm)
            inv = _pool_inv_count(gi, row0, tm, cx, seq)
            pooled = _window_sum(gi, i, tm, t, cx, vb_ref, False) * inv - v_ref[rows, :]
            mixed = _dot(pooled.astype(BF16), w) * sc
            u_ref[rows, :] = (mixed * _silu(ag_ref[rows, :])).astype(BF16)
            return carry

        lax.fori_loop(0, t // tm, step, 0)

    sec = lambda s: pl.BlockSpec((None, t, g), lambda j: (s, 0, j))
    return pl.pallas_call(
        body, name="mix_a_fwd", grid=(N_POOL,),
        in_specs=[sec(0), sec(1), pl.BlockSpec((None, g, g), lambda j: (j, 0, 0)),
                  pl.BlockSpec((1, g), lambda j: (0, j))],
        out_specs=pl.BlockSpec((t, g), lambda j: (0, j)),
        out_shape=jax.ShapeDtypeStruct((t, half_d), BF16),
        scratch_shapes=[pltpu.VMEM((t, g), BF16)], compiler_params=_cp(),
    )(z0, z0, pool_w, pool_scale)


def mix_a_bwd(z0, du, pool_w, pool_scale, cx):
    _, t, half_d = z0.shape
    g = half_d // N_POOL
    seq = t - cx
    tm = _row_block(cx)
    gq = g // 4

    def body(v_ref, ag_ref, du_ref, w_ref, sc_ref, dz_ref, dw_ref, dsc_ref,
             vb_ref, pooled_ref, dmx_ref, dpl_ref, wdp_ref):
        gi = pl.program_id(0)
        vb_ref[...] = v_ref[...].astype(BF16)
        w = w_ref[...].astype(BF16)
        sc = sc_ref[...]

        def first(i, dsc):
            row0 = pl.multiple_of(i * tm, tm)
            rows = pl.ds(row0, tm)
            inv = _pool_inv_count(gi, row0, tm, cx, seq)
            pooled = (_window_sum(gi, i, tm, t, cx, vb_ref, False) * inv - v_ref[rows, :]).astype(BF16)
            pooled_ref[rows, :] = pooled
            mixed = _dot(pooled, w)
            ag = ag_ref[rows, :]
            duv = du_ref[rows, :]
            dz_ref[1, rows, :] = (duv * (mixed * sc) * _dsilu(ag)).astype(BF16)
            dms = duv * _silu(ag)
            dmixed = (dms * sc).astype(BF16)
            dmx_ref[rows, :] = dmixed
            dpooled = _dot(dmixed, w, NT)
            dpl_ref[rows, :] = dpooled
            wdp_ref[rows, :] = (dpooled * inv).astype(BF16)
            return dsc + jnp.sum(dms * mixed, axis=0, keepdims=True)

        dsc_ref[...] = lax.fori_loop(0, t // tm, first, jnp.zeros((1, g), F32))
        dw = _dot(pooled_ref[...], dmx_ref[...], TN)
        for qi in range(4):
            dw_ref[qi] = dw[qi * gq:(qi + 1) * gq, :]

        def second(i, carry):
            row0 = pl.multiple_of(i * tm, tm)
            rows = pl.ds(row0, tm)
            dz_ref[0, rows, :] = (_window_sum(gi, i, tm, t, cx, wdp_ref, True) - dpl_ref[rows, :]).astype(BF16)
            return carry

        lax.fori_loop(0, t // tm, second, 0)

    sec = lambda s: pl.BlockSpec((None, t, g), lambda j: (s, 0, j))
    return pl.pallas_call(
        body, name="mix_a_bwd", grid=(N_POOL,),
        in_specs=[sec(0), sec(1), pl.BlockSpec((t, g), lambda j: (0, j)),
                  pl.BlockSpec((None, g, g), lambda j: (j, 0, 0)),
                  pl.BlockSpec((1, g), lambda j: (0, j))],
        out_specs=[pl.BlockSpec((2, t, g), lambda j: (0, 0, j)),
                   pl.BlockSpec((4, None, gq, g), lambda j: (0, j, 0, 0)),
                   pl.BlockSpec((1, g), lambda j: (0, j))],
        out_shape=[jax.ShapeDtypeStruct((2, t, half_d), BF16),
                   jax.ShapeDtypeStruct((4, N_POOL, gq, g), F32),
                   jax.ShapeDtypeStruct((1, half_d), F32)],
        scratch_shapes=[pltpu.VMEM((t, g), BF16), pltpu.VMEM((t, g), BF16), pltpu.VMEM((t, g), BF16),
                        pltpu.VMEM((t, g), F32), pltpu.VMEM((t, g), BF16)],
        compiler_params=_cp(),
    )(z0, z0, du, pool_w, pool_scale)


def _conv_masks(t, cx):
    r = lax.broadcasted_iota(jnp.int32, (t, 1), 0)
    has_prev = jnp.where((r == 0) | (r == cx), 0.0, 1.0)
    has_next = jnp.where((r == cx - 1) | (r == t - 1), 0.0, 1.0)
    return has_prev, has_next


def mix_b_fwd(z0, conv_w, conv_b, cx):
    _, t, half_d = z0.shape
    gb = 128

    def body(bx_ref, bb_ref, bc_ref, bg_ref, w_ref, b_ref, u_ref):
        has_prev, has_next = _conv_masks(t, cx)
        tt = bc_ref[...] * bx_ref[...]
        prev = pltpu.roll(tt, 1, 0) * has_prev
        nxt = pltpu.roll(tt, t - 1, 0) * has_next
        cv = prev * w_ref[0:1, :] + tt * w_ref[1:2, :] + nxt * w_ref[2:3, :] + b_ref[...]
        u_ref[...] = (bb_ref[...] * cv * _silu(bg_ref[...])).astype(BF16)

    sec = lambda s: pl.BlockSpec((None, t, gb), lambda j: (s, 0, j))
    return pl.pallas_call(
        body, name="mix_b_fwd", grid=(half_d // gb,),
        in_specs=[sec(2), sec(3), sec(4), sec(5), pl.BlockSpec((3, gb), lambda j: (0, j)),
                  pl.BlockSpec((1, gb), lambda j: (0, j))],
        out_specs=pl.BlockSpec((t, gb), lambda j: (0, j)),
        out_shape=jax.ShapeDtypeStruct((t, half_d), BF16), compiler_params=_cp(),
    )(z0, z0, z0, z0, conv_w, conv_b)


def mix_b_bwd(z0, du, conv_w, conv_b, cx):
    _, t, half_d = z0.shape
    gb = 128
    off = half_d // gb

    def body(bx_ref, bb_ref, bc_ref, bg_ref, du_ref, w_ref, b_ref, dz_ref, dw_ref, db_ref):
        has_prev, has_next = _conv_masks(t, cx)
        bx, bb, bc, bg = bx_ref[...], bb_ref[...], bc_ref[...], bg_ref[...]
        duv = du_ref[...]
        tt = bc * bx
        prev = pltpu.roll(tt, 1, 0) * has_prev
        nxt = pltpu.roll(tt, t - 1, 0) * has_next
        w0, w1, w2 = w_ref[0:1, :], w_ref[1:2, :], w_ref[2:3, :]
        cv = prev * w0 + tt * w1 + nxt * w2 + b_ref[...]
        sg = _silu(bg)
        dz_ref[1] = (duv * cv * sg).astype(BF16)
        dz_ref[3] = (duv * bb * cv * _dsilu(bg)).astype(BF16)
        dcv = duv * bb * sg
        dw_ref[0:1, :] = jnp.sum(dcv * prev, axis=0, keepdims=True)
        dw_ref[1:2, :] = jnp.sum(dcv * tt, axis=0, keepdims=True)
        dw_ref[2:3, :] = jnp.sum(dcv * nxt, axis=0, keepdims=True)
        db_ref[...] = jnp.sum(dcv, axis=0, keepdims=True)
        dt = (pltpu.roll(dcv * has_prev, t - 1, 0) * w0 + dcv * w1
              + pltpu.roll(dcv * has_next, 1, 0) * w2)
        dz_ref[0] = (dt * bc).astype(BF16)
        dz_ref[2] = (dt * bx).astype(BF16)

    sec = lambda s: pl.BlockSpec((None, t, gb), lambda j: (s, 0, j))
    return pl.pallas_call(
        body, name="mix_b_bwd", grid=(half_d // gb,),
        in_specs=[sec(2), sec(3), sec(4), sec(5), pl.BlockSpec((t, gb), lambda j: (0, j + off)),
                  pl.BlockSpec((3, gb), lambda j: (0, j)), pl.BlockSpec((1, gb), lambda j: (0, j))],
        out_specs=[pl.BlockSpec((4, t, gb), lambda j: (0, 0, j)),
                   pl.BlockSpec((3, gb), lambda j: (0, j)), pl.BlockSpec((1, gb), lambda j: (0, j))],
        out_shape=[jax.ShapeDtypeStruct((4, t, half_d), BF16),
                   jax.ShapeDtypeStruct((3, half_d), F32), jax.ShapeDtypeStruct((1, half_d), F32)],
        compiler_params=_cp(),
    )(z0, z0, z0, z0, du, conv_w, conv_b)


def _lower_bound(lbl_ref, d):
    l0, l1, l2 = lbl_ref[d, 0:1, :], lbl_ref[d, 1:2, :], lbl_ref[d, 2:3, :]
    mx = jnp.maximum(jnp.maximum(l0, l1), l2)
    e0, e1, e2 = jnp.exp(l0 - mx), jnp.exp(l1 - mx), jnp.exp(l2 - mx)
    inv = 1.0 / (e0 + e1 + e2)
    return (e0 + e1) * inv, (e0 * inv, e1 * inv, e2 * inv)


def _chunk_consts(d):
    r = lax.broadcasted_iota(jnp.int32, (CHUNK, CHUNK), 0)
    c = lax.broadcasted_iota(jnp.int32, (CHUNK, CHUNK), 1)
    keep = (c <= r) if d == 0 else (c >= r)
    return jnp.where(keep, 1.0, 0.0).astype(F32), keep


def _chunk_of_step(s, d, nc, ncc):
    if d == 0:
        return s
    return jnp.where(s < ncc, ncc - 1 - s, nc - 1 + ncc - s)


def _chunk_terms(lfc, kc, qc, cum):
    bc = _dot(cum, lfc, precision=lax.Precision.HIGHEST)
    bl = jnp.sum(lfc, axis=0, keepdims=True)
    e = jnp.exp(bc)
    einv = jnp.exp(-bc)
    erem = jnp.exp(bl - bc)
    return e, einv, erem, jnp.exp(bl), qc * e, kc * einv, kc * erem


def hgrn_fwd(z1, lbl, onorm, cx):
    _, t, d = z1.shape
    seq = t - cx
    nc, ncc = t // CHUNK, cx // CHUNK

    def body(zf_ref, zb_ref, v_ref, q_ref, g_ref, lbl_ref, on_ref, o_ref, r_ref,
             lf_ref, k_ref, oacc_ref, st_ref):
        for dr, z_ref in ((0, zf_ref), (1, zb_ref)):
            lbv, _ = _lower_bound(lbl_ref, dr)
            z = z_ref[...]
            lf_ref[...] = jnp.log(lbv + (1.0 - lbv) * jax.nn.sigmoid(z))
            k_ref[...] = (1.0 - lbv) * jax.nn.sigmoid(-z)
            st_ref[...] = jnp.zeros_like(st_ref)
            cum, keep = _chunk_consts(dr)

            def step(s, carry, dr=dr, cum=cum, keep=keep):
                n = _chunk_of_step(s, dr, nc, ncc)
                rows = pl.ds(pl.multiple_of(n * CHUNK, CHUNK), CHUNK)
                vc = v_ref[rows, :].astype(BF16)
                _, _, _, dec, qd, ki, kd = _chunk_terms(lf_ref[rows, :], k_ref[rows, :], q_ref[rows, :], cum)
                qdb = qd.astype(BF16)
                a = jnp.where(keep, _dot(qdb, ki.astype(BF16), NT), 0.0)
                st = st_ref[...]
                oc = _dot(qdb, st.astype(BF16), NT) + _dot(a.astype(BF16), vc)
                st_ref[...] = st * dec + _dot(vc, kd.astype(BF16), TN)
                if dr == 0:
                    oacc_ref[rows, :] = oc
                else:
                    oacc_ref[rows, :] += oc
                return carry

            lax.fori_loop(0, nc, step, 0, unroll=4)

        o = oacc_ref[cx:, :]
        o_ref[...] = o
        rstd = lax.rsqrt(jnp.mean(o * o, axis=-1, keepdims=True) + EPS)
        r_ref[...] = (o * rstd * on_ref[...] * _silu(g_ref[cx:, :])).astype(BF16)

    sec = lambda s: pl.BlockSpec((None, t, HEAD), lambda h: (s, 0, h))
    col = pl.BlockSpec((seq, HEAD), lambda h: (0, h))
    return pl.pallas_call(
        body, name="hgrn_fwd", grid=(d // HEAD,),
        in_specs=[sec(0), sec(1), sec(2), sec(3), sec(4),
                  pl.BlockSpec((2, 3, HEAD), lambda h: (0, 0, h)), pl.BlockSpec((1, HEAD), lambda h: (0, h))],
        out_specs=[col, col],
        out_shape=[jax.ShapeDtypeStruct((seq, d), F32), jax.ShapeDtypeStruct((seq, d), BF16)],
        scratch_shapes=[pltpu.VMEM((t, HEAD), F32), pltpu.VMEM((t, HEAD), F32), pltpu.VMEM((t, HEAD), F32),
                        pltpu.VMEM((HEAD, HEAD), F32)],
        compiler_params=_cp(),
    )(z1, z1, z1, z1, z1, lbl, onorm)


def hgrn_bwd(z1, lbl, onorm, o, dr_out, cx):
    _, t, d = z1.shape
    seq = t - cx
    nc, ncc = t // CHUNK, cx // CHUNK

    def body(zf_ref, zb_ref, v_ref, q_ref, g_ref, lbl_ref, on_ref, o_ref, dr_ref,
             dz_ref, don_ref, dlb_ref,
             lf_ref, k_ref, do_ref, dq_ref, dv_ref, dk_ref, dlf_ref, ssc_ref, dst_ref):
        o = o_ref[...]
        g = g_ref[cx:, :]
        drv = dr_ref[...]
        onv = on_ref[...]
        rstd = lax.rsqrt(jnp.mean(o * o, axis=-1, keepdims=True) + EPS)
        ohat = o * rstd
        sg = _silu(g)
        don_ref[...] = jnp.sum(drv * ohat * sg, axis=0, keepdims=True)
        dz_ref[4, :cx, :] = jnp.zeros((cx, HEAD), BF16)
        dz_ref[4, cx:, :] = (drv * ohat * onv * _dsilu(g)).astype(BF16)
        dohat = drv * onv * sg
        do_ref[:cx, :] = jnp.zeros((cx, HEAD), F32)
        do_ref[cx:, :] = rstd * (dohat - ohat * jnp.mean(dohat * ohat, axis=-1, keepdims=True))

        for dr, z_ref in ((0, zf_ref), (1, zb_ref)):
            lbv, _ = _lower_bound(lbl_ref, dr)
            z = z_ref[...]
            lf_ref[...] = jnp.log(lbv + (1.0 - lbv) * jax.nn.sigmoid(z))
            k_ref[...] = (1.0 - lbv) * jax.nn.sigmoid(-z)
            cum, keep = _chunk_consts(dr)
            cum_t, _ = _chunk_consts(1 - dr)

            st_init = jnp.zeros((HEAD, HEAD), F32)

            def state_step(s, st, dr=dr, cum=cum):
                n = _chunk_of_step(s, dr, nc, ncc)
                rows = pl.ds(pl.multiple_of(n * CHUNK, CHUNK), CHUNK)
                ssc_ref[n] = st
                _, _, _, dec, _, _, kd = _chunk_terms(lf_ref[rows, :], k_ref[rows, :], q_ref[rows, :], cum)
                return st * dec + _dot(v_ref[rows, :].astype(BF16), kd.astype(BF16), TN)

            lax.fori_loop(0, nc, state_step, st_init, unroll=4)
            dst_ref[...] = jnp.zeros_like(dst_ref)

            def grad_step(s2, carry, dr=dr, cum=cum, cum_t=cum_t, keep=keep):
                n = _chunk_of_step(nc - 1 - s2, dr, nc, ncc)
                rows = pl.ds(pl.multiple_of(n * CHUNK, CHUNK), CHUNK)
                vc = v_ref[rows, :].astype(BF16)
                e, einv, erem, dec, qd, ki, kd = _chunk_terms(
                    lf_ref[rows, :], k_ref[rows, :], q_ref[rows, :], cum)
                qdb, kib, kdb = qd.astype(BF16), ki.astype(BF16), kd.astype(BF16)
                doc = do_ref[rows, :].astype(BF16)
                st0 = ssc_ref[n]
                dst = dst_ref[...]
                dstb = dst.astype(BF16)
                a = jnp.where(keep, _dot(qdb, kib, NT), 0.0).astype(BF16)
                da = jnp.where(keep, _dot(doc, vc, NT), 0.0).astype(BF16)
                dqd = _dot(doc, st0.astype(BF16)) + _dot(da, kib)
                dki = _dot(da, qdb, TN)
                dv = _dot(a, doc, TN) + _dot(kdb, dstb, NT)
                dkd = _dot(vc, dstb)
                ddec = jnp.sum(dst * st0, axis=0, keepdims=True)
                dst_ref[...] = _dot(doc, qdb, TN) + dst * dec
                dbc = dqd * qd - dki * ki - dkd * kd
                dbl = jnp.sum(dkd * kd, axis=0, keepdims=True) + ddec * dec
                dlf_ref[rows, :] = _dot(cum_t, dbc, precision=lax.Precision.HIGHEST) + dbl
                dk_ref[rows, :] = dki * einv + dkd * erem
                if dr == 0:
                    dq_ref[rows, :] = dqd * e
                    dv_ref[rows, :] = dv
                else:
                    dq_ref[rows, :] += dqd * e
                    dv_ref[rows, :] += dv
                return carry

            lax.fori_loop(0, nc, grad_step, 0, unroll=2)

            sig = jax.nn.sigmoid(z)
            one_lb = 1.0 - lbv
            f = lbv + one_lb * sig
            dlf = dlf_ref[...]
            dk = dk_ref[...]
            dsig = (dlf / f - dk) * one_lb
            dz_ref[dr] = (dsig * sig * (1.0 - sig)).astype(BF16)
            dlb_ref[dr:dr + 1, :] = jnp.sum((dlf / f - dk) * (1.0 - sig), axis=0, keepdims=True)

        dz_ref[2] = dv_ref[...].astype(BF16)
        dz_ref[3] = dq_ref[...].astype(BF16)

    sec = lambda s: pl.BlockSpec((None, t, HEAD), lambda h: (s, 0, h))
    col = pl.BlockSpec((seq, HEAD), lambda h: (0, h))
    tvec = pltpu.VMEM((t, HEAD), F32)
    return pl.pallas_call(
        body, name="hgrn_bwd", grid=(d // HEAD,),
        in_specs=[sec(0), sec(1), sec(2), sec(3), sec(4),
                  pl.BlockSpec((2, 3, HEAD), lambda h: (0, 0, h)), pl.BlockSpec((1, HEAD), lambda h: (0, h)),
                  col, col],
        out_specs=[pl.BlockSpec((5, t, HEAD), lambda h: (0, 0, h)),
                   pl.BlockSpec((1, HEAD), lambda h: (0, h)), pl.BlockSpec((2, HEAD), lambda h: (0, h))],
        out_shape=[jax.ShapeDtypeStruct((5, t, d), BF16), jax.ShapeDtypeStruct((1, d), F32),
                   jax.ShapeDtypeStruct((2, d), F32)],
        scratch_shapes=[tvec, tvec, tvec, tvec, tvec, tvec, tvec,
                        pltpu.VMEM((nc, HEAD, HEAD), F32), pltpu.VMEM((HEAD, HEAD), F32)],
        compiler_params=_cp(),
    )(z1, z1, z1, z1, z1, lbl, onorm, o, dr_out)


def _gates(z, lbv):
    e = jnp.exp(-jnp.abs(z))
    r = 1.0 / (1.0 + e)
    er = e * r
    pos = z >= 0.0
    sig = jnp.where(pos, r, er)
    nsig = jnp.where(pos, er, r)
    return sig, nsig, lbv + (1.0 - lbv) * sig


def _split3(x):
    hi = x.astype(BF16)
    r1 = x - hi.astype(F32)
    mid = r1.astype(BF16)
    lo = (r1 - mid.astype(F32)).astype(BF16)
    return jnp.concatenate([hi, mid, lo], axis=1)


def _cumsum_chunk(cum, x):
    y = _dot(cum, _split3(x))
    return y[:, :HEAD] + y[:, HEAD:2 * HEAD] + y[:, 2 * HEAD:]


def _chunk_rows(n):
    return pl.ds(pl.multiple_of(n * CHUNK, CHUNK), CHUNK)


def _group(nc):
    return next(u for u in (4, 3, 2, 1) if nc % u == 0)


def _decay_pass(lf_ref, bc_ref, dec_ref, cum, nc):
    grp = _group(nc)

    def step(m, carry):
        ns = [m * grp + u for u in range(grp)]
        lfc = [lf_ref[_chunk_rows(n), :] for n in ns]
        bc = [_cumsum_chunk(cum, x) for x in lfc]
        for u, n in enumerate(ns):
            bc_ref[_chunk_rows(n), :] = bc[u]
            dec_ref[n] = jnp.broadcast_to(jnp.exp(jnp.sum(lfc[u], axis=0, keepdims=True)), (8, HEAD))
        return carry

    lax.fori_loop(0, nc // grp, step, 0)


def hgrn_fwd(z1, lbl, onorm, cx):
    _, t, d = z1.shape
    seq = t - cx
    nc, ncc = t // CHUNK, cx // CHUNK

    grp = _group(nc)

    def body(zf_ref, zb_ref, v_ref, q_ref, g_ref, lbl_ref, on_ref, o_ref, r_ref,
             lf_ref, k_ref, bc_ref, dec_ref, qd_ref, ki_ref, oacc_ref, ds_ref):
        for dr, z_ref in ((0, zf_ref), (1, zb_ref)):
            lbv, _ = _lower_bound(lbl_ref, dr)
            _, nsig, f = _gates(z_ref[...], lbv)
            lf_ref[...] = jnp.log(f)
            k_ref[...] = (1.0 - lbv) * nsig
            cum, keep = _chunk_consts(dr)
            _decay_pass(lf_ref, bc_ref, dec_ref, cum.astype(BF16), nc)
            bc = bc_ref[...]
            qd_ref[...] = (q_ref[...] * jnp.exp(bc)).astype(BF16)
            ki_ref[...] = (k_ref[...] * jnp.exp(-bc)).astype(BF16)

            def local_step(m, carry, dr=dr, keep=keep):
                ns = [m * grp + u for u in range(grp)]
                rows = [_chunk_rows(n) for n in ns]
                qd = [qd_ref[r, :] for r in rows]
                ki = [ki_ref[r, :] for r in rows]
                vc = [v_ref[r, :].astype(BF16) for r in rows]
                sc = [_dot(qd[u], ki[u], NT) for u in range(grp)]
                inc = [_dot(vc[u], ki[u], TN) for u in range(grp)]
                a = [jnp.where(keep, s, 0.0).astype(BF16) for s in sc]
                intra = [_dot(a[u], vc[u]) for u in range(grp)]
                for u in range(grp):
                    ds_ref[ns[u]] = inc[u] * dec_ref[ns[u]][0:1, :]
                    if dr == 0:
                        oacc_ref[rows[u], :] = intra[u]
                    else:
                        oacc_ref[rows[u], :] += intra[u]
                return carry

            lax.fori_loop(0, nc // grp, local_step, 0)

            def state_step(m, st, dr=dr):
                ns = [_chunk_of_step(m * grp + u, dr, nc, ncc) for u in range(grp)]
                rows = [_chunk_rows(n) for n in ns]
                sts = []
                for n in ns:
                    sts.append(st.astype(BF16))
                    st = st * dec_ref[n][0:1, :] + ds_ref[n]
                inter = [_dot(qd_ref[rows[u], :], sts[u], NT) for u in range(grp)]
                for u in range(grp):
                    oacc_ref[rows[u], :] += inter[u]
                return st

            lax.fori_loop(0, nc // grp, state_step, jnp.zeros((HEAD, HEAD), F32))

        o = oacc_ref[cx:, :]
        o_ref[...] = o
        rstd = lax.rsqrt(jnp.mean(o * o, axis=-1, keepdims=True) + EPS)
        r_ref[...] = (o * rstd * on_ref[...] * _silu(g_ref[cx:, :])).astype(BF16)

    sec = lambda s: pl.BlockSpec((None, t, HEAD), lambda h: (s, 0, h))
    col = pl.BlockSpec((seq, HEAD), lambda h: (0, h))
    tf32, tb16 = pltpu.VMEM((t, HEAD), F32), pltpu.VMEM((t, HEAD), BF16)
    return pl.pallas_call(
        body, name="hgrn_fwd", grid=(d // HEAD,),
        in_specs=[sec(0), sec(1), sec(2), sec(3), sec(4),
                  pl.BlockSpec((2, 3, HEAD), lambda h: (0, 0, h)), pl.BlockSpec((1, HEAD), lambda h: (0, h))],
        out_specs=[col, col],
        out_shape=[jax.ShapeDtypeStruct((seq, d), F32), jax.ShapeDtypeStruct((seq, d), BF16)],
        scratch_shapes=[tf32, tf32, tf32, pltpu.VMEM((nc, 8, HEAD), F32), tb16, tb16, tf32,
                        pltpu.VMEM((nc, HEAD, HEAD), F32)],
        compiler_params=_cp(),
    )(z1, z1, z1, z1, z1, lbl, onorm)


def hgrn_bwd(z1, lbl, onorm, o, dr_out, cx):
    _, t, d = z1.shape
    seq = t - cx
    nc, ncc = t // CHUNK, cx // CHUNK

    grp2 = grp = _group(nc)

    def body(zf_ref, zb_ref, v_ref, q_ref, g_ref, lbl_ref, on_ref, o_ref, dr_ref,
             dz_ref, don_ref, dlb_ref,
             lf_ref, k_ref, bc_ref, dec_ref, qd_ref, ki_ref, do_ref,
             dqd_ref, dki_ref, dq_ref, dv_ref, ds_ref, dsl_ref):
        o = o_ref[...]
        g = g_ref[cx:, :]
        drv = dr_ref[...]
        onv = on_ref[...]
        rstd = lax.rsqrt(jnp.mean(o * o, axis=-1, keepdims=True) + EPS)
        ohat = o * rstd
        sg = _silu(g)
        don_ref[...] = jnp.sum(drv * ohat * sg, axis=0, keepdims=True)
        dz_ref[4, :cx, :] = jnp.zeros((cx, HEAD), BF16)
        dz_ref[4, cx:, :] = (drv * ohat * onv * _dsilu(g)).astype(BF16)
        dohat = drv * onv * sg
        do_ref[:cx, :] = jnp.zeros((cx, HEAD), BF16)
        do_ref[cx:, :] = (rstd * (dohat - ohat * jnp.mean(dohat * ohat, axis=-1, keepdims=True))).astype(BF16)

        for dr, z_ref in ((0, zf_ref), (1, zb_ref)):
            lbv, _ = _lower_bound(lbl_ref, dr)
            _, nsig, f = _gates(z_ref[...], lbv)
            lf_ref[...] = jnp.log(f)
            k_ref[...] = (1.0 - lbv) * nsig
            cum, keep = _chunk_consts(dr)
            cum_t = _chunk_consts(1 - dr)[0].astype(BF16)
            _decay_pass(lf_ref, bc_ref, dec_ref, cum.astype(BF16), nc)
            bc = bc_ref[...]
            qd_ref[...] = (q_ref[...] * jnp.exp(bc)).astype(BF16)
            ki_ref[...] = (k_ref[...] * jnp.exp(-bc)).astype(BF16)

            def local_step(m, carry, dr=dr, keep=keep):
                ns = [m * grp + u for u in range(grp)]
                rows = [_chunk_rows(n) for n in ns]
                rng = range(grp)
                qd = [qd_ref[r, :] for r in rows]
                ki = [ki_ref[r, :] for r in rows]
                doc = [do_ref[r, :] for r in rows]
                vc = [v_ref[r, :].astype(BF16) for r in rows]
                sc = [_dot(qd[u], ki[u], NT) for u in rng]
                dsc = [_dot(doc[u], vc[u], NT) for u in rng]
                inc = [_dot(vc[u], ki[u], TN) for u in rng]
                dinc = [_dot(doc[u], qd[u], TN) for u in rng]
                a = [jnp.where(keep, s, 0.0).astype(BF16) for s in sc]
                da = [jnp.where(keep, s, 0.0).astype(BF16) for s in dsc]
                dqd = [_dot(da[u], ki[u]) for u in rng]
                dki = [_dot(da[u], qd[u], TN) for u in rng]
                dv = [_dot(a[u], doc[u], TN) for u in rng]
                for u in rng:
                    ds_ref[ns[u]] = inc[u] * dec_ref[ns[u]][0:1, :]
                    dsl_ref[ns[u]] = dinc[u]
                    dqd_ref[rows[u], :] = dqd[u]
                    dki_ref[rows[u], :] = dki[u]
                    if dr == 0:
                        dv_ref[rows[u], :] = dv[u]
                    else:
                        dv_ref[rows[u], :] += dv[u]
                return carry

            lax.fori_loop(0, nc // grp, local_step, 0)

            def state_step(s, st, dr=dr):
                n = _chunk_of_step(s, dr, nc, ncc)
                inc = ds_ref[n]
                ds_ref[n] = st
                return st * dec_ref[n][0:1, :] + inc

            lax.fori_loop(0, nc, state_step, jnp.zeros((HEAD, HEAD), F32), unroll=4)

            def dstate_step(s, dst, dr=dr):
                n = _chunk_of_step(nc - 1 - s, dr, nc, ncc)
                inc = dsl_ref[n]
                dsl_ref[n] = dst
                return inc + dst * dec_ref[n][0:1, :]

            lax.fori_loop(0, nc, dstate_step, jnp.zeros((HEAD, HEAD), F32), unroll=4)

            def grad_step(m, carry, dr=dr, cum_t=cum_t):
                ns = [m * grp2 + u for u in range(grp2)]
                rows = [_chunk_rows(n) for n in ns]
                rng = range(grp2)
                st0 = [ds_ref[n] for n in ns]
                dst = [dsl_ref[n] for n in ns]
                dstb = [x.astype(BF16) for x in dst]
                dec = [dec_ref[n][0:1, :] for n in ns]
                doc = [do_ref[r, :] for r in rows]
                vc = [v_ref[r, :].astype(BF16) for r in rows]
                e = [jnp.exp(bc_ref[r, :]) for r in rows]
                einv = [jnp.exp(-bc_ref[r, :]) for r in rows]
                qd = [q_ref[rows[u], :] * e[u] for u in rng]
                ki = [k_ref[rows[u], :] * einv[u] for u in rng]
                kd = [ki[u] * dec[u] for u in rng]
                dqd_st = [_dot(doc[u], st0[u].astype(BF16)) for u in rng]
                dkd = [_dot(vc[u], dstb[u]) for u in rng]
                dv_st = [_dot(kd[u].astype(BF16), dstb[u], NT) for u in rng]
                dqd = [dqd_ref[rows[u], :] + dqd_st[u] for u in rng]
                dki = [dki_ref[r, :] for r in rows]
                dbc = [dqd[u] * qd[u] - dki[u] * ki[u] - dkd[u] * kd[u] for u in rng]
                cs = [_cumsum_chunk(cum_t, x) for x in dbc]
                for u in rng:
                    ddec = jnp.sum(dst[u] * st0[u], axis=0, keepdims=True)
                    dbl = jnp.sum(dkd[u] * kd[u], axis=0, keepdims=True) + ddec * dec[u]
                    dv_ref[rows[u], :] += dv_st[u]
                    dqd_ref[rows[u], :] = cs[u] + dbl
                    dki_ref[rows[u], :] = dki[u] * einv[u] + dkd[u] * (einv[u] * dec[u])
                    if dr == 0:
                        dq_ref[rows[u], :] = dqd[u] * e[u]
                    else:
                        dq_ref[rows[u], :] += dqd[u] * e[u]
                return carry

            lax.fori_loop(0, nc // grp2, grad_step, 0)

            sig, nsig, f = _gates(z_ref[...], lbv)
            common = (dqd_ref[...] / f - dki_ref[...]) * nsig
            dz_ref[dr] = (common * ((1.0 - lbv) * sig)).astype(BF16)
            dlb_ref[dr:dr + 1, :] = jnp.sum(common, axis=0, keepdims=True)

        dz_ref[2] = dv_ref[...].astype(BF16)
        dz_ref[3] = dq_ref[...].astype(BF16)

    sec = lambda s: pl.BlockSpec((None, t, HEAD), lambda h: (s, 0, h))
    col = pl.BlockSpec((seq, HEAD), lambda h: (0, h))
    tf32, tb16 = pltpu.VMEM((t, HEAD), F32), pltpu.VMEM((t, HEAD), BF16)
    states = pltpu.VMEM((nc, HEAD, HEAD), F32)
    return pl.pallas_call(
        body, name="hgrn_bwd", grid=(d // HEAD,),
        in_specs=[sec(0), sec(1), sec(2), sec(3), sec(4),
                  pl.BlockSpec((2, 3, HEAD), lambda h: (0, 0, h)), pl.BlockSpec((1, HEAD), lambda h: (0, h)),
                  col, col],
        out_specs=[pl.BlockSpec((5, t, HEAD), lambda h: (0, 0, h)),
                   pl.BlockSpec((1, HEAD), lambda h: (0, h)), pl.BlockSpec((2, HEAD), lambda h: (0, h))],
        out_shape=[jax.ShapeDtypeStruct((5, t, d), BF16), jax.ShapeDtypeStruct((1, d), F32),
                   jax.ShapeDtypeStruct((2, d), F32)],
        scratch_shapes=[tf32, tf32, tf32, pltpu.VMEM((nc, 8, HEAD), F32), tb16, tb16, tb16,
                        tf32, tf32, tf32, tf32, states, states],
        compiler_params=_cp(),
    )(z1, z1, z1, z1, z1, lbl, onorm, o, dr_out)


def _place():
    x, y, c = lax.axis_index("x"), lax.axis_index("y"), lax.axis_index("c")
    chips = [(1 - x, y), (x, 1 - y), (1 - x, 1 - y)]
    return x, y, c, chips


def allgather_shards(bufs):
    n = len(bufs)

    def body(*refs):
        outs = refs[n:2 * n]
        done_ref, send_sems, recv_sems = refs[2 * n:]
        done_ref[...] = jnp.zeros((8, 128), F32)
        x, y, c, chips = _place()
        p = 2 * x + y
        half = [pl.ds(c * (s.shape[1] // 2), s.shape[1] // 2) for s in bufs]
        other = [pl.ds((1 - c) * (s.shape[1] // 2), s.shape[1] // 2) for s in bufs]

        def remote(i, k, src, dst, to):
            return pltpu.make_async_remote_copy(src_ref=src, dst_ref=dst, send_sem=send_sems.at[6 * i + k],
                                                recv_sem=recv_sems.at[6 * i + k], device_id=to, device_id_type=MESH)

        sends = []
        for i in range(n):
            for j, chip in enumerate(chips):
                mine = outs[i].at[p, half[i]]
                cp = remote(i, j, mine, mine, (*chip, c))
                cp.start()
                sends.append(cp)
        for i in range(n):
            for j, chip in enumerate(chips):
                landed = outs[i].at[2 * chip[0] + chip[1], half[i]]
                remote(i, j, landed, landed, (x, y, c)).wait_recv()
                cp = remote(i, 3 + j, landed, landed, (x, y, 1 - c))
                cp.start()
                sends.append(cp)
        for i in range(n):
            for j, chip in enumerate(chips):
                landed = outs[i].at[2 * chip[0] + chip[1], other[i]]
                remote(i, 3 + j, landed, landed, (x, y, c)).wait_recv()
        for cp in sends:
            cp.wait_send()

    return pl.pallas_call(
        body, name="allgather_shards",
        in_specs=[ANY] * n, out_specs=[ANY] * n + [VMEM],
        out_shape=[jax.ShapeDtypeStruct(s.shape, s.dtype) for s in bufs] + [jax.ShapeDtypeStruct((8, 128), F32)],
        input_output_aliases={i: i for i in range(n)},
        scratch_shapes=[pltpu.SemaphoreType.DMA((6 * n,)), pltpu.SemaphoreType.DMA((6 * n,))],
        compiler_params=pltpu.CompilerParams(has_side_effects=True),
    )(*bufs)


def exchange_halves(grads):
    n = len(grads)

    def body(*refs):
        ins, outs = refs[:n], refs[n:2 * n]
        send_sems, recv_sems = refs[2 * n:]
        x, y, c, _ = _place()
        copies = []
        for i in range(n):
            hr = grads[i].shape[1] // 2
            cp = pltpu.make_async_remote_copy(
                src_ref=ins[i].at[:, pl.ds((1 - c) * hr, hr)], dst_ref=outs[i],
                send_sem=send_sems.at[i], recv_sem=recv_sems.at[i],
                device_id=(x, y, 1 - c), device_id_type=MESH)
            cp.start()
            copies.append(cp)
        for cp in copies:
            cp.wait()

    return pl.pallas_call(
        body, name="exchange_halves",
        in_specs=[ANY] * n, out_specs=[ANY] * n,
        out_shape=[jax.ShapeDtypeStruct((4, g.shape[1] // 2, g.shape[2]), g.dtype) for g in grads],
        scratch_shapes=[pltpu.SemaphoreType.DMA((n,)), pltpu.SemaphoreType.DMA((n,))],
        compiler_params=pltpu.CompilerParams(has_side_effects=True),
    )(*grads)


def pair_sum(grad, got, chip_core):
    _, r, cc = grad.shape
    hr = r // 2
    tr = 256 if hr % 256 == 0 else hr
    nb = hr // tr

    def body(cc_ref, a_ref, b_ref, own_ref, sb_ref):
        s = a_ref[...].astype(F32) + b_ref[...].astype(F32)
        sb_ref[...] = s.astype(BF16)

        @pl.when(pl.program_id(1) == cc_ref[0])
        def _():
            own_ref[...] = s

    grid_spec = pltpu.PrefetchScalarGridSpec(
        num_scalar_prefetch=1, grid=(nb, 4),
        in_specs=[pl.BlockSpec((None, tr, cc), lambda i, qi, cc_ref: (qi, cc_ref[1] * nb + i, 0)),
                  pl.BlockSpec((None, tr, cc), lambda i, qi, cc_ref: (qi, i, 0))],
        out_specs=[pl.BlockSpec((tr, cc), lambda i, qi, cc_ref: (i, 0)),
                   pl.BlockSpec((None, tr, cc), lambda i, qi, cc_ref: (qi, i, 0))])
    return pl.pallas_call(
        body, name="pair_sum", grid_spec=grid_spec,
        out_shape=[jax.ShapeDtypeStruct((hr, cc), F32), jax.ShapeDtypeStruct((4, hr, cc), BF16)],
        compiler_params=_cp(),
    )(chip_core, grad, got)


def scatter_to_owners(parts):
    n = len(parts)

    def body(*refs):
        ins, outs = refs[:n], refs[n:2 * n]
        send_sems, recv_sems = refs[2 * n:]
        x, y, c, chips = _place()
        copies = []
        for i in range(n):
            for j, chip in enumerate(chips):
                cp = pltpu.make_async_remote_copy(
                    src_ref=ins[i].at[2 * chip[0] + chip[1]], dst_ref=outs[i].at[j],
                    send_sem=send_sems.at[3 * i + j], recv_sem=recv_sems.at[3 * i + j],
                    device_id=(*chip, c), device_id_type=MESH)
                cp.start()
                copies.append(cp)
        for cp in copies:
            cp.wait()

    return pl.pallas_call(
        body, name="scatter_to_owners",
        in_specs=[ANY] * n, out_specs=[ANY] * n,
        out_shape=[jax.ShapeDtypeStruct((3,) + p.shape[1:], p.dtype) for p in parts],
        scratch_shapes=[pltpu.SemaphoreType.DMA((3 * n,)), pltpu.SemaphoreType.DMA((3 * n,))],
        compiler_params=pltpu.CompilerParams(has_side_effects=True),
    )(*parts)


def owner_sum(own, got, chip_core):
    hr, cc = own.shape
    tr = 256 if hr % 256 == 0 else hr
    nb = hr // tr

    def body(cc_ref, a_ref, b_ref, o_ref):
        s = a_ref[...] + b_ref[0].astype(F32)
        s = s + b_ref[1].astype(F32)
        o_ref[...] = s + b_ref[2].astype(F32)

    grid_spec = pltpu.PrefetchScalarGridSpec(
        num_scalar_prefetch=1, grid=(nb,),
        in_specs=[pl.BlockSpec((tr, cc), lambda i, cc_ref: (i, 0)),
                  pl.BlockSpec((3, tr, cc), lambda i, cc_ref: (0, i, 0))],
        out_specs=pl.BlockSpec((tr, cc), lambda i, cc_ref: (cc_ref[1] * nb + i, 0)))
    return pl.pallas_call(
        body, name="owner_sum", grid_spec=grid_spec,
        out_shape=jax.ShapeDtypeStruct((2 * hr, cc), F32), compiler_params=_cp(),
    )(chip_core, own, got)


def share_halves(bufs):
    n = len(bufs)

    def body(*refs):
        outs = refs[n:2 * n]
        send_sems, recv_sems = refs[2 * n:]
        x, y, c, _ = _place()
        copies = []
        for i in range(n):
            hr = bufs[i].shape[0] // 2
            mine = outs[i].at[pl.ds(c * hr, hr)]
            cp = pltpu.make_async_remote_copy(
                src_ref=mine, dst_ref=mine, send_sem=send_sems.at[i], recv_sem=recv_sems.at[i],
                device_id=(x, y, 1 - c), device_id_type=MESH)
            cp.start()
            copies.append((cp, outs[i].at[pl.ds((1 - c) * hr, hr)]))
        for i, (cp, theirs) in enumerate(copies):
            cp.wait_send()
            pltpu.make_async_remote_copy(
                src_ref=theirs, dst_ref=theirs, send_sem=send_sems.at[i], recv_sem=recv_sems.at[i],
                device_id=(x, y, c), device_id_type=MESH).wait_recv()

    return pl.pallas_call(
        body, name="share_halves",
        in_specs=[ANY] * n, out_specs=[ANY] * n,
        out_shape=[jax.ShapeDtypeStruct(b.shape, b.dtype) for b in bufs],
        input_output_aliases={i: i for i in range(n)},
        scratch_shapes=[pltpu.SemaphoreType.DMA((n,)), pltpu.SemaphoreType.DMA((n,))],
        compiler_params=pltpu.CompilerParams(has_side_effects=True),
    )(*bufs)


def allgather8(v, name):
    r, n = v.shape

    def body(v_ref, out_ref, send_sems, recv_sems):
        x, y, c, _ = _place()
        me = 4 * x + 2 * y + c
        out_ref[me] = v_ref[...]

        def copy(k, slot, to):
            return pltpu.make_async_remote_copy(
                src_ref=v_ref, dst_ref=out_ref.at[slot], send_sem=send_sems.at[k - 1],
                recv_sem=recv_sems.at[k - 1], device_id=to, device_id_type=MESH)

        peers = []
        for k in range(1, 8):
            px = 1 - x if (k >> 2) & 1 else x
            py = 1 - y if (k >> 1) & 1 else y
            pc = 1 - c if k & 1 else c
            peers.append((px, py, pc))
            copy(k, me, (px, py, pc)).start()
        for k, (px, py, pc) in enumerate(peers, start=1):
            copy(k, 4 * px + 2 * py + pc, (x, y, c)).wait_recv()
        for k, peer in enumerate(peers, start=1):
            copy(k, me, peer).wait_send()

    return pl.pallas_call(
        body, name=name, in_specs=[VMEM], out_specs=VMEM,
        out_shape=jax.ShapeDtypeStruct((8, r, n), v.dtype),
        scratch_shapes=[pltpu.SemaphoreType.DMA((7,)), pltpu.SemaphoreType.DMA((7,))],
        compiler_params=_cp(has_side_effects=True),
    )(v)


HBM = pl.BlockSpec(memory_space=pltpu.HBM)
SEM = pl.BlockSpec(memory_space=pltpu.SEMAPHORE)
DATAFLOW = pltpu.SideEffectType.DATAFLOW_SIDE_EFFECTING


def _descriptors(plan, refs, send_sems, recv_sems, arrivals=True):
    x, y, c, _ = _place()
    sends, recvs = plan(refs)
    out = [pltpu.make_async_remote_copy(src_ref=src, dst_ref=dst, send_sem=send_sems.at[k],
                                        recv_sem=recv_sems.at[k], device_id=to, device_id_type=MESH)
           for k, (src, dst, to) in enumerate(sends)]
    if not arrivals:
        return out, []
    inn = [pltpu.make_async_remote_copy(src_ref=land, dst_ref=land, send_sem=send_sems.at[k],
                                        recv_sem=recv_sems.at[k], device_id=(x, y, c), device_id_type=MESH)
           for k, land in enumerate(recvs)]
    return out, inn


def copies_start(name, arrays, n_copies, plan, after):
    na = len(arrays)

    def body(*refs):
        out, _ = _descriptors(plan, refs[:na], refs[na + 1], refs[na + 2], arrivals=False)
        for cp in out:
            cp.start()
        refs[-1][...] = jnp.zeros((8, 128), F32)

    res = pl.pallas_call(
        body, name=name,
        out_shape=(pltpu.SemaphoreType.DMA((n_copies,)), pltpu.SemaphoreType.DMA((n_copies,)),
                   *[pltpu.HBM(a.shape, a.dtype) for a in arrays], jax.ShapeDtypeStruct((8, 128), F32)),
        in_specs=[HBM] * na + [ANY], out_specs=(SEM, SEM, *[HBM] * na, VMEM),
        input_output_aliases={i: i + 2 for i in range(na)},
        compiler_params=pltpu.CompilerParams(has_side_effects=DATAFLOW),
    )(*[pltpu.with_memory_space_constraint(a, pltpu.HBM) for a in arrays], after)
    return res[0], res[1], list(res[2:2 + na]), res[-1]


def copies_wait(name, started, plan, after):
    send_sems, recv_sems, arrays, _ = started
    na = len(arrays)

    def body(*refs):
        out, inn = _descriptors(plan, refs[:na], refs[na], refs[na + 1])
        for cp in out:
            cp.wait_send()
        for cp in inn:
            cp.wait_recv()
        refs[-1][...] = jnp.zeros((8, 128), F32)

    res = pl.pallas_call(
        body, name=name,
        out_shape=(*[pltpu.HBM(a.shape, a.dtype) for a in arrays], jax.ShapeDtypeStruct((8, 128), F32)),
        in_specs=[HBM] * na + [SEM, SEM, ANY], out_specs=(*[HBM] * na, VMEM),
        input_output_aliases={i: i for i in range(na)},
        compiler_params=pltpu.CompilerParams(has_side_effects=DATAFLOW),
    )(*arrays, send_sems, recv_sems, after)
    return list(res[:na]), res[-1]


def _rows_half(r, c):
    return pl.ds(c * (r // 2), r // 2), pl.ds((1 - c) * (r // 2), r // 2)


def plan_gather_ici(refs):
    x, y, c, chips = _place()
    p = 2 * x + y
    sends, recvs = [], []
    for buf in refs:
        mine, _ = _rows_half(buf.shape[1], c)
        for chip in chips:
            sends.append((buf.at[p, mine], buf.at[p, mine], (*chip, c)))
            recvs.append(buf.at[2 * chip[0] + chip[1], mine])
    return sends, recvs


def plan_gather_d2d(refs):
    x, y, c, chips = _place()
    sends, recvs = [], []
    for buf in refs:
        mine, theirs = _rows_half(buf.shape[1], c)
        for chip in chips:
            slot = 2 * chip[0] + chip[1]
            sends.append((buf.at[slot, mine], buf.at[slot, mine], (x, y, 1 - c)))
            recvs.append(buf.at[slot, theirs])
    return sends, recvs


def plan_exchange(refs):
    x, y, c, _ = _place()
    n = len(refs) // 2
    sends, recvs = [], []
    for grad, land in zip(refs[:n], refs[n:]):
        _, theirs = _rows_half(grad.shape[1], c)
        sends.append((grad.at[:, theirs], land, (x, y, 1 - c)))
        recvs.append(land)
    return sends, recvs


def plan_scatter(refs):
    x, y, c, chips = _place()
    n = len(refs) // 2
    sends, recvs = [], []
    for part, land in zip(refs[:n], refs[n:]):
        for j, chip in enumerate(chips):
            sends.append((part.at[2 * chip[0] + chip[1]], land.at[j], (*chip, c)))
            recvs.append(land.at[j])
    return sends, recvs


def plan_share(refs):
    x, y, c, _ = _place()
    sends, recvs = [], []
    for buf in refs:
        mine, theirs = _rows_half(buf.shape[0], c)
        sends.append((buf.at[mine], buf.at[mine], (x, y, 1 - c)))
        recvs.append(buf.at[theirs])
    return sends, recvs


def put_in_slot(w, chip, dtype, name):
    r, c = w.shape
    tr = 256 if r % 256 == 0 else r

    def body(chip_ref, w_ref, o_ref):
        o_ref[...] = w_ref[...].astype(dtype)

    grid_spec = pltpu.PrefetchScalarGridSpec(
        num_scalar_prefetch=1, grid=(r // tr,),
        in_specs=[pl.BlockSpec((tr, c), lambda i, chip_ref: (i, 0))],
        out_specs=pl.BlockSpec((None, tr, c), lambda i, chip_ref: (chip_ref[0], i, 0)))
    return pl.pallas_call(body, name=name, grid_spec=grid_spec,
                          out_shape=jax.ShapeDtypeStruct((4, r, c), dtype), compiler_params=_cp())(chip, w)


def ada_fwd(s_in, ada_w, ada_b, tn):
    nl, d, ws = ada_w.shape

    def body(s_ref, w_ref, b_ref, so_ref, mod_ref):
        s = _silu(s_ref[...])
        so_ref[...] = s
        mod_ref[...] = _dot(s.astype(BF16), w_ref[...].astype(BF16)) + b_ref[...]

    return pl.pallas_call(
        body, name="ada_fwd", grid=(nl, ws // tn),
        in_specs=[pl.BlockSpec((16, d), lambda l, j: (0, 0)),
                  pl.BlockSpec((None, d, tn), lambda l, j: (l, 0, j)),
                  pl.BlockSpec((None, 1, tn), lambda l, j: (l, 0, j))],
        out_specs=[pl.BlockSpec((16, d), lambda l, j: (0, 0)),
                   pl.BlockSpec((None, 16, tn), lambda l, j: (l, 0, j))],
        out_shape=[jax.ShapeDtypeStruct((16, d), F32), jax.ShapeDtypeStruct((nl, 16, ws), F32)],
        compiler_params=_cp(),
    )(s_in, ada_w, ada_b)


def _adamw_math(w, g, m, v):
    m = ADAM_B1 * m + (1.0 - ADAM_B1) * g
    v = ADAM_B2 * v + (1.0 - ADAM_B2) * (g * g)
    m_hat = m / (1.0 - ADAM_B1 ** ADAM_STEP)
    v_hat = v / (1.0 - ADAM_B2 ** ADAM_STEP)
    delta = -ADAM_LR * (m_hat / (jnp.sqrt(v_hat) + ADAM_EPS) + ADAM_WD * w)
    return delta, m, v


def ada_bwd_adamw(s, dm, w, m, v):
    nl, d, ws = w.shape
    tr = 256 if d % 256 == 0 else 128

    def body(s_ref, dm_ref, w_ref, m_ref, v_ref, g_ref, dl_ref, mo_ref, vo_ref, dc_ref):
        dmv = dm_ref[...].astype(BF16)
        wv = w_ref[...]
        g = _dot(s_ref[...].astype(BF16), dmv, TN)
        g_ref[...] = g
        dl_ref[...], mo_ref[...], vo_ref[...] = _adamw_math(wv, g, m_ref[...], v_ref[...])
        dc_ref[...] = _dot(dmv[8:16, :], wv.astype(BF16), NT)

    wblk = pl.BlockSpec((None, tr, ws), lambda l, i: (l, i, 0))
    wshape = jax.ShapeDtypeStruct((nl, d, ws), F32)
    return pl.pallas_call(
        body, name="ada_bwd_adamw", grid=(nl, d // tr),
        in_specs=[pl.BlockSpec((16, tr), lambda l, i: (0, i)),
                  pl.BlockSpec((None, 16, ws), lambda l, i: (l, 0, 0)), wblk, wblk, wblk],
        out_specs=[wblk, wblk, wblk, wblk, pl.BlockSpec((None, 8, tr), lambda l, i: (l, 0, i))],
        out_shape=[wshape, wshape, wshape, wshape, jax.ShapeDtypeStruct((nl, 8, d), F32)],
        compiler_params=_cp(),
    )(s, dm, w, m, v)


def adamw(w, g, m, v, name):
    r, c = w.shape
    tr = 256 if r % 256 == 0 else r

    def body(w_ref, g_ref, m_ref, v_ref, dl_ref, mo_ref, vo_ref):
        dl_ref[...], mo_ref[...], vo_ref[...] = _adamw_math(w_ref[...], g_ref[...], m_ref[...], v_ref[...])

    blk = pl.BlockSpec((tr, c), lambda i: (i, 0))
    shape = jax.ShapeDtypeStruct((r, c), F32)
    return pl.pallas_call(body, name=name, grid=(r // tr,), in_specs=[blk] * 4, out_specs=[blk] * 3,
                          out_shape=[shape] * 3, compiler_params=_cp())(w, g, m, v)


SMALL_ROWS = 24
ROW_MOD = 10


def small_reduce(gathered):
    _, rows, d = gathered.shape

    def body(g_ref, o_ref):
        tot = g_ref[0]
        for b in range(1, 8):
            tot = tot + g_ref[b]
        o_ref[0:rows, :] = tot
        for layer in range(2):
            lat = ROW_MOD + 6 * layer
            o_ref[24 + 3 * layer:27 + 3 * layer, :] = tot[lat:lat + 3, :] + tot[lat + 3:lat + 6, :]
        o_ref[30:32, :] = jnp.zeros((2, d), F32)

    return pl.pallas_call(body, name="small_reduce", in_specs=[VMEM], out_specs=VMEM,
                          out_shape=jax.ShapeDtypeStruct((32, d), F32), compiler_params=_cp())(gathered)


def lb_logits_grad(lbl, dlb):
    _, _, n = lbl.shape

    def body(l_ref, d_ref, o_ref):
        for dr in range(2):
            _, (p0, p1, p2) = _lower_bound(l_ref, dr)
            dv = d_ref[dr:dr + 1, :]
            o_ref[dr, 0:1, :] = p0 * p2 * dv
            o_ref[dr, 1:2, :] = p1 * p2 * dv
            o_ref[dr, 2:3, :] = -p2 * (p0 + p1) * dv

    return pl.pallas_call(body, name="lb_logits_grad", in_specs=[VMEM, VMEM], out_specs=VMEM,
                          out_shape=jax.ShapeDtypeStruct((2, 3, n), F32), compiler_params=_cp())(lbl, dlb)


def c_ctx_grad(parts, c_ctx):
    d = c_ctx.shape[1]

    def body(p_ref, c_ref, o_ref):
        tot = p_ref[0, 0:1, :]
        for chip in range(1, 4):
            tot = tot + p_ref[2 * chip, 0:1, :]
        o_ref[...] = tot * _dsilu(c_ref[...])

    return pl.pallas_call(body, name="c_ctx_grad", in_specs=[VMEM, VMEM], out_specs=VMEM,
                          out_shape=jax.ShapeDtypeStruct((1, d), F32), compiler_params=_cp())(parts, c_ctx)


def _reduce_scatter(grads, core, chip_core):
    got = exchange_halves(grads)
    sums = [pair_sum(g, r, core) for g, r in zip(grads, got)]
    recv = scatter_to_owners([sb for _, sb in sums])
    reduced = [owner_sum(s, r, chip_core) for (s, _), r in zip(sums, recv)]
    return share_halves(reduced)


def kernel(x, c, ctx, c_ctx, ada_w, ada_b, pre_g, post_g, ev_w_in, ev_pool_w, ev_pool_scale, ev_conv_w, ev_conv_b, ev_w_out, od_w_in, od_onorm_g, od_w_out, lb_logits, loss_target, m_c_ctx, m_ada_w, m_ada_b, m_pre_g, m_post_g, m_ev_w_in, m_ev_pool_w, m_ev_pool_scale, m_ev_conv_w, m_ev_conv_b, m_ev_w_out, m_od_w_in, m_od_onorm_g, m_od_w_out, m_lb_logits, v_c_ctx, v_ada_w, v_ada_b, v_pre_g, v_post_g, v_ev_w_in, v_ev_pool_w, v_ev_pool_scale, v_ev_conv_w, v_ev_conv_b, v_ev_w_out, v_od_w_in, v_od_onorm_g, v_od_w_out, v_lb_logits):
    _, seq, d = x.shape
    cx = ctx.shape[1]
    t = cx + seq
    half_d = d // 2
    g = half_d // N_POOL
    tn = d // 4
    xi, yi, ci = lax.axis_index("x"), lax.axis_index("y"), lax.axis_index("c")
    chip = 2 * xi + yi
    me = 2 * chip + ci
    core_arr = jnp.reshape(ci, (1,)).astype(jnp.int32)
    chip_arr = jnp.reshape(chip, (1,)).astype(jnp.int32)
    chip_core_arr = jnp.stack([chip, ci]).astype(jnp.int32)

    pad = lambda a, rows: jnp.concatenate([a, jnp.zeros((rows - a.shape[0], g), F32)], axis=0)
    small = jnp.concatenate([
        ev_pool_w.reshape(g, g), pad(ev_conv_w.reshape(3, g), 8), pad(od_onorm_g.reshape(2, g), 8),
        pad(lb_logits.reshape(12, g), 16)], axis=0)
    ev_in_g, ev_out_g, small_g, ev_done = allgather_shards([
        put_in_slot(ev_w_in[0], chip_arr, BF16, "cast_ev_w_in"),
        put_in_slot(ev_w_out[0], chip_arr, BF16, "cast_ev_w_out"),
        put_in_slot(small, chip_arr, F32, "place_small")])
    ev_out3 = ev_out_g.reshape(1, d, d)
    pool_w_full = small_g[:, :g].reshape(4, N_POOL, g // 4, g).transpose(1, 0, 2, 3).reshape(N_POOL, g, g)
    conv_w_full = small_g[:, g:g + 3].transpose(1, 0, 2).reshape(3, half_d)
    onorm_full = small_g[:, g + 8:g + 10].reshape(1, d)
    lbl_full = small_g[:, g + 16:g + 28].reshape(4, 2, 3, 2 * g).transpose(1, 2, 0, 3).reshape(2, 3, d)

    c_rows = jnp.concatenate([c + ev_done[0:1, 0:1], jnp.zeros((7, d), F32)], axis=0)
    c_all = allgather8(c_rows, "allgather_c")[:, 0, :]
    s_in = jnp.concatenate([c_all, c_ctx.reshape(1, d), jnp.zeros((7, d), F32)], axis=0)
    ws_ada = ada_w.shape[2]
    ada_b_mine = lax.dynamic_slice(ada_b, (0, chip * ws_ada), (2, ws_ada)).reshape(2, 1, ws_ada)
    s_act, mod_mine = ada_fwd(s_in, ada_w, ada_b_mine, tn)
    mod_all = allgather8(mod_mine.reshape(32, ws_ada), "allgather_mod")
    od_ici = copies_start("gather_od_ici_start", [
        put_in_slot(od_w_in[0], chip_arr, BF16, "cast_od_w_in"),
        put_in_slot(od_w_out[0], chip_arr, BF16, "cast_od_w_out")], 6, plan_gather_ici, mod_all)
    mod_full = mod_all[0::2].reshape(4, 2, 16, ws_ada).transpose(1, 2, 0, 3).reshape(2, 16, 3 * d)
    mod_lat = lax.dynamic_slice(mod_full, (0, me, 0), (2, 1, 3 * d))
    mods = jnp.concatenate([mod_full[:, 8:9], mod_lat], axis=1)
    shift, scale, gate = mods[:, :, :d], mods[:, :, d:2 * d], mods[:, :, 2 * d:]

    xs = jnp.concatenate([ctx[0], x[0]], axis=0)

    h0 = normmod_fwd(xs, pre_g[0:1] + od_ici[3][0:1, 0:1], shift[0], scale[0], cx)
    z0 = mm_nn(h0, ev_in_g, half_d, tn, "mm_ev_in")
    u_a = mix_a_fwd(z0, pool_w_full, ev_pool_scale, cx)
    u_b = mix_b_fwd(z0, conv_w_full, ev_conv_b, cx)
    u = jnp.concatenate([u_a, u_b], axis=1)
    y0 = mm_nn(u, ev_out3, d, tn, "mm_ev_out")[0]
    xs1 = post_fwd(xs, y0, post_g[0:1], gate[0], cx)
    od_d2d = copies_start("gather_od_d2d_start",
                          copies_wait("gather_od_ici_wait", od_ici, plan_gather_ici, xs1)[0],
                          6, plan_gather_d2d, xs1)
    (od_in_g, od_out_g), _ = copies_wait("gather_od_d2d_wait", od_d2d, plan_gather_d2d, od_d2d[3])
    od_out3 = od_out_g.reshape(1, d, d)

    h1 = normmod_fwd(xs1, pre_g[1:2], shift[1], scale[1], cx)
    z1 = mm_nn(h1, od_in_g, d, tn, "mm_od_in")
    o1, r1 = hgrn_fwd(z1, lbl_full, onorm_full, cx)
    y1 = mm_nn(r1, od_out3, d, tn, "mm_od_out")[0]
    sq, dx2 = post_loss(xs1, y1, post_g[1:2], gate[1], loss_target[0], cx)
    loss = lax.psum(sq[0, 0] * (0.5 / d), ("x", "y", "c"))

    dy1, dgate1, dpost1 = post_bwd(dx2, y1, post_g[1:2], gate[1], cx, True)
    dr1 = mm_nt(dy1[None], None, od_out3, tn, "mm_od_out_dx")
    g_od_out = mm_tn(r1, dy1[None], None, d, tn, "mm_od_out_dw")
    dz1, donorm, dlb = hgrn_bwd(z1, lbl_full, onorm_full, o1, dr1, cx)
    dh1 = mm_nt(dz1, None, od_in_g, tn, "mm_od_in_dx")
    g_od_in = mm_tn(h1, dz1, None, od_in_g.shape[2], tn, "mm_od_in_dw")
    dxs1, dpre1, dshift1, dscale1 = normmod_bwd(xs1, dh1, pre_g[1:2], scale[1], dx2, cx, True)

    od_grads = [g_od_in, g_od_out.reshape(4, d // 4, d)]
    half_zone = lambda a, lead, dt: lax.empty((lead, a.shape[1] // 2, a.shape[2]), dt)
    od_ex = copies_start("reduce_od_exchange_start", od_grads + [half_zone(a, 4, a.dtype) for a in od_grads],
                         2, plan_exchange, dxs1)

    dy0, dgate0, dpost0 = post_bwd(dxs1, y0, post_g[0:1] + od_ex[3][0:1, 0:1], gate[0], cx, False)
    du = mm_nt(dy0[None], None, ev_out3, tn, "mm_ev_out_dx")
    g_ev_out = mm_tn(u, dy0[None], None, d, tn, "mm_ev_out_dw")
    od_got, _ = copies_wait("reduce_od_exchange_wait", od_ex, plan_exchange, g_ev_out)
    od_sums = [pair_sum(od_got[i], od_got[2 + i], chip_core_arr) for i in range(2)]
    od_sc = copies_start("reduce_od_scatter_start",
                         [sb for _, sb in od_sums] + [half_zone(a, 3, BF16) for a in od_grads],
                         6, plan_scatter, du)
    dz0a, g_pool_w, dpool_scale = mix_a_bwd(z0, du, pool_w_full, ev_pool_scale + od_sc[3][0:1, 0:1], cx)
    dz0b, dconv_w, dconv_b = mix_b_bwd(z0, du, conv_w_full, ev_conv_b + od_sc[3][0:1, 0:1], cx)
    g_ev_in = mm_tn(h0, dz0a, dz0b, ev_in_g.shape[2], tn, "mm_ev_in_dw")
    ev_grads = [g_ev_in, g_ev_out.reshape(4, d // 4, d), g_pool_w.reshape(4, g, g)]
    ev_ex = copies_start("reduce_ev_exchange_start", ev_grads + [half_zone(a, 4, a.dtype) for a in ev_grads],
                         3, plan_exchange, g_ev_in)
    dh0 = mm_nt(dz0a, dz0b, ev_in_g, tn, "mm_ev_in_dx")
    dxs0, dpre0, dshift0, dscale0 = normmod_bwd(xs, dh0, pre_g[0:1] + ev_ex[3][0:1, 0:1], scale[0], dxs1,
                                                cx, False, True)
    grad_x = dxs0[None]
    ev_got, _ = copies_wait("reduce_ev_exchange_wait", ev_ex, plan_exchange, dxs0)
    ev_sums = [pair_sum(ev_got[i], ev_got[3 + i], chip_core_arr) for i in range(3)]
    od_recv, _ = copies_wait("reduce_od_scatter_wait", od_sc, plan_scatter, dxs0)

    zrow = jnp.zeros((1, d), F32)
    small_rows = jnp.concatenate([
        dpre0, dpre1, dpost0, dpost1,
        jnp.concatenate([dpool_scale, dconv_b], axis=1),
        jnp.concatenate([dconv_w.reshape(1, 3 * half_d), jnp.zeros((1, half_d), F32)], axis=1).reshape(2, d),
        donorm, dlb,
        dshift0[1:2], dscale0[1:2], dgate0[1:2], dshift0[0:1], dscale0[0:1], dgate0[0:1],
        dshift1[1:2], dscale1[1:2], dgate1[1:2], dshift1[0:1], dscale1[0:1], zrow,
        zrow, zrow], axis=0)
    small_all = allgather8(small_rows, "allgather_small")
    ev_sc = copies_start("reduce_ev_scatter_start",
                         [sb for _, sb in ev_sums] + [half_zone(a, 3, BF16) for a in ev_grads],
                         9, plan_scatter, small_all)
    od_sh = copies_start("reduce_od_share_start",
                         [owner_sum(od_sums[i][0], od_recv[2 + i], chip_core_arr) for i in range(2)],
                         2, plan_share, ev_sc[3])
    tot = small_reduce(small_all + ev_sc[3][0:1, 0:1])

    dm_rows = []
    for layer in range(2):
        lat = ROW_MOD + 6 * layer
        dm_lat = small_all[:, lat:lat + 3].reshape(8, 3 * d)
        dm_ctx = tot[lat + 3:lat + 6].reshape(1, 3 * d)
        dm_rows.append(jnp.concatenate([dm_lat, dm_ctx, jnp.zeros((7, 3 * d), F32)], axis=0))
    dm_full = jnp.stack(dm_rows)
    dm_mine = lax.dynamic_slice(dm_full, (0, 0, chip * ws_ada), (2, 16, ws_ada))

    grad_ada_w, delta_ada_w, new_m_ada_w, new_v_ada_w, dctx_part = ada_bwd_adamw(
        s_act, dm_mine, ada_w, m_ada_w, v_ada_w)
    (grad_od_w_in, grad_od_w_out), _ = copies_wait("reduce_od_share_wait", od_sh, plan_share, ev_sc[3])
    od_w_in_upd = adamw(od_w_in[0], grad_od_w_in, m_od_w_in[0], v_od_w_in[0], "adamw_od_w_in")
    od_w_out_upd = adamw(od_w_out[0], grad_od_w_out, m_od_w_out[0], v_od_w_out[0], "adamw_od_w_out")
    ev_recv, ev_landed = copies_wait("reduce_ev_scatter_wait", ev_sc, plan_scatter, od_w_in_upd[0])
    grad_ev_w_in, grad_ev_w_out, grad_pool_w = share_halves(
        [owner_sum(ev_sums[i][0], ev_recv[3 + i], chip_core_arr) for i in range(3)])
    dctx_all = allgather8(dctx_part[0] + dctx_part[1] + ev_landed[0:1, 0:1], "allgather_dctx")
    grad_c_ctx = c_ctx_grad(dctx_all, c_ctx.reshape(1, d)).reshape(d)

    grad_ada_b = tot[24:30].reshape(2, 3 * d)
    grad_pre_g = tot[0:2]
    grad_post_g = tot[2:4]
    grad_ev_pool_scale = tot[4:5, :half_d]
    grad_ev_conv_b = tot[4:5, half_d:]
    conv_w_tot = tot[5:7].reshape(1, 2 * d)[:, :3 * half_d].reshape(3, N_POOL, g)
    grad_ev_conv_w = lax.dynamic_slice(conv_w_tot, (0, chip, 0), (3, 1, g)).reshape(1, 3, g)
    grad_od_onorm_g = lax.dynamic_slice(tot[7:8], (0, chip * 2 * g), (1, 2 * g))
    dlb_mine = lax.dynamic_slice(tot[8:10], (0, chip * 2 * g), (2, 2 * g))
    grad_lb_logits = lb_logits_grad(lb_logits, dlb_mine)
    grad_ev_w_in = grad_ev_w_in[None]
    grad_od_w_in = grad_od_w_in[None]
    grad_ev_w_out = grad_ev_w_out[None]
    grad_od_w_out = grad_od_w_out[None]
    grad_ev_pool_w = grad_pool_w.reshape(1, N_POOL, g // 4, g)

    def step(w, gr, m, v, name):
        shape = w.shape
        cols = shape[-1]
        two_d = lambda a: a.reshape(-1, cols)
        dl, mo, vo = adamw(two_d(w), two_d(gr), two_d(m), two_d(v), "adamw_" + name)
        return dl.reshape(shape), mo.reshape(shape), vo.reshape(shape)

    upd = {
        "c_ctx": step(c_ctx, grad_c_ctx, m_c_ctx, v_c_ctx, "c_ctx"),
        "ada_w": (delta_ada_w, new_m_ada_w, new_v_ada_w),
        "ada_b": step(ada_b, grad_ada_b, m_ada_b, v_ada_b, "ada_b"),
        "pre_g": step(pre_g, grad_pre_g, m_pre_g, v_pre_g, "pre_g"),
        "post_g": step(post_g, grad_post_g, m_post_g, v_post_g, "post_g"),
        "ev_w_in": step(ev_w_in, grad_ev_w_in, m_ev_w_in, v_ev_w_in, "ev_w_in"),
        "ev_pool_w": step(ev_pool_w, grad_ev_pool_w, m_ev_pool_w, v_ev_pool_w, "ev_pool_w"),
        "ev_pool_scale": step(ev_pool_scale, grad_ev_pool_scale, m_ev_pool_scale, v_ev_pool_scale, "ev_pool_scale"),
        "ev_conv_w": step(ev_conv_w, grad_ev_conv_w, m_ev_conv_w, v_ev_conv_w, "ev_conv_w"),
        "ev_conv_b": step(ev_conv_b, grad_ev_conv_b, m_ev_conv_b, v_ev_conv_b, "ev_conv_b"),
        "ev_w_out": step(ev_w_out, grad_ev_w_out, m_ev_w_out, v_ev_w_out, "ev_w_out"),
        "od_w_in": tuple(a[None] for a in od_w_in_upd),
        "od_onorm_g": step(od_onorm_g, grad_od_onorm_g, m_od_onorm_g, v_od_onorm_g, "od_onorm_g"),
        "od_w_out": tuple(a[None] for a in od_w_out_upd),
        "lb_logits": step(lb_logits, grad_lb_logits, m_lb_logits, v_lb_logits, "lb_logits"),
    }
    names = ["c_ctx", "ada_w", "ada_b", "pre_g", "post_g", "ev_w_in", "ev_pool_w", "ev_pool_scale",
             "ev_conv_w", "ev_conv_b", "ev_w_out", "od_w_in", "od_onorm_g", "od_w_out", "lb_logits"]
    grads = [grad_c_ctx, grad_ada_w, grad_ada_b, grad_pre_g, grad_post_g, grad_ev_w_in, grad_ev_pool_w,
             grad_ev_pool_scale, grad_ev_conv_w, grad_ev_conv_b, grad_ev_w_out, grad_od_w_in,
             grad_od_onorm_g, grad_od_w_out, grad_lb_logits]
    return (loss, grad_x, *grads, *[upd[k][0] for k in names], *[upd[k][1] for k in names],
            *[upd[k][2] for k in names])
```

```python
import functools

import jax
import jax.numpy as jnp
from jax import lax
from jax.experimental import pallas as pl
from jax.experimental.pallas import tpu as pltpu

EPS = 1e-6
GRID_W_LOG2 = 6
CHUNK = 64
HEAD = 128
N_POOL = 4
ADAM_LR, ADAM_B1, ADAM_B2, ADAM_EPS, ADAM_WD, ADAM_STEP = 0.001, 0.9, 0.999, 1e-08, 0.01, 10
VMEM_LIMIT = 56 * 1024 * 1024
MESH = pl.DeviceIdType.MESH
F32, BF16 = jnp.float32, jnp.bfloat16
ANY = pl.BlockSpec(memory_space=pl.ANY)
VMEM = pl.BlockSpec(memory_space=pltpu.VMEM)


def _cp(**kw):
    return pltpu.CompilerParams(vmem_limit_bytes=VMEM_LIMIT, **kw)


def _silu(x):
    return x * jax.nn.sigmoid(x)


def _dsilu(x):
    s = jax.nn.sigmoid(x)
    return s * (1.0 + x * (1.0 - s))


def _dot(a, b, dims=((1,), (0,)), precision=None):
    return lax.dot_general(a, b, (dims, ((), ())), preferred_element_type=F32, precision=precision)


NN = ((1,), (0,))
NT = ((1,), (1,))
TN = ((0,), (0,))


def _row_block(cx):
    return 256 if cx % 256 == 0 else 128


def normmod_fwd(xs, g, shift, scale, cx):
    t, d = xs.shape
    tm = _row_block(cx)
    nctx = cx // tm

    def body(x_ref, g_ref, sh_ref, sc_ref, h_ref):
        is_ctx = pl.program_id(0) < nctx
        x = x_ref[...]
        rstd = lax.rsqrt(jnp.mean(x * x, axis=-1, keepdims=True) + EPS)
        sc = jnp.where(is_ctx, sc_ref[0:1, :], sc_ref[1:2, :])
        sh = jnp.where(is_ctx, sh_ref[0:1, :], sh_ref[1:2, :])
        h_ref[...] = ((x * rstd) * g_ref[...] * (1.0 + sc) + sh).astype(BF16)

    row = pl.BlockSpec((tm, d), lambda i: (i, 0))
    vec = lambda r: pl.BlockSpec((r, d), lambda i: (0, 0))
    return pl.pallas_call(
        body, name="normmod_fwd", grid=(t // tm,),
        in_specs=[row, vec(1), vec(2), vec(2)], out_specs=row,
        out_shape=jax.ShapeDtypeStruct((t, d), BF16), compiler_params=_cp(),
    )(xs, g, shift, scale)


def normmod_bwd(xs, dh, g, scale, dres, cx, res_is_latent_only, dx_latent_only=False):
    t, d = xs.shape
    tm = _row_block(cx)
    nctx = cx // tm

    def body(x_ref, dh_ref, g_ref, sc_ref, dres_ref, dx_ref, dg_ref, dsh_ref, dsc_ref):
        i = pl.program_id(0)
        is_ctx = i < nctx

        @pl.when(i == 0)
        def _():
            dg_ref[...] = jnp.zeros_like(dg_ref)
            dsh_ref[...] = jnp.zeros_like(dsh_ref)
            dsc_ref[...] = jnp.zeros_like(dsc_ref)

        x = x_ref[...]
        dh = dh_ref[...]
        gv = g_ref[...]
        rstd = lax.rsqrt(jnp.mean(x * x, axis=-1, keepdims=True) + EPS)
        xhat = x * rstd
        sc = jnp.where(is_ctx, sc_ref[0:1, :], sc_ref[1:2, :])
        dsh = jnp.sum(dh, axis=0, keepdims=True)
        dhx = dh * xhat
        dsc = jnp.sum(dhx * gv, axis=0, keepdims=True)
        dg_ref[...] += jnp.sum(dhx * (1.0 + sc), axis=0, keepdims=True)
        zero = jnp.zeros_like(dsh)
        dsh_ref[0:1, :] += jnp.where(is_ctx, dsh, zero)
        dsh_ref[1:2, :] += jnp.where(is_ctx, zero, dsh)
        dsc_ref[0:1, :] += jnp.where(is_ctx, dsc, zero)
        dsc_ref[1:2, :] += jnp.where(is_ctx, zero, dsc)
        dxhat = dh * (gv * (1.0 + sc))
        dx = rstd * (dxhat - xhat * jnp.mean(dxhat * xhat, axis=-1, keepdims=True))
        res = dres_ref[...]
        if res_is_latent_only:
            res = jnp.where(is_ctx, jnp.zeros_like(res), res)
        dx_ref[...] = dx + res

    row = pl.BlockSpec((tm, d), lambda i: (i, 0))
    if res_is_latent_only:
        res_spec = pl.BlockSpec((tm, d), lambda i: (jnp.maximum(i - nctx, 0), 0))
    else:
        res_spec = row
    vec = lambda r: pl.BlockSpec((r, d), lambda i: (0, 0))
    dx_spec = pl.BlockSpec((tm, d), lambda i: (jnp.maximum(i - nctx, 0), 0)) if dx_latent_only else row
    return pl.pallas_call(
        body, name="normmod_bwd", grid=(t // tm,),
        in_specs=[row, row, vec(1), vec(2), res_spec],
        out_specs=[dx_spec, vec(1), vec(2), vec(2)],
        out_shape=[jax.ShapeDtypeStruct((t - cx if dx_latent_only else t, d), F32), jax.ShapeDtypeStruct((1, d), F32),
                   jax.ShapeDtypeStruct((2, d), F32), jax.ShapeDtypeStruct((2, d), F32)],
        compiler_params=_cp(),
    )(xs, dh, g, scale, dres)


def post_fwd(xs, y, pg, gate, cx):
    t, d = xs.shape
    tm = _row_block(cx)
    nctx = cx // tm

    def body(x_ref, y_ref, pg_ref, gate_ref, o_ref):
        is_ctx = pl.program_id(0) < nctx
        y = y_ref[...]
        rstd = lax.rsqrt(jnp.mean(y * y, axis=-1, keepdims=True) + EPS)
        gt = jnp.where(is_ctx, gate_ref[0:1, :], gate_ref[1:2, :])
        o_ref[...] = x_ref[...] + gt * ((y * rstd) * pg_ref[...])

    row = pl.BlockSpec((tm, d), lambda i: (i, 0))
    vec = lambda r: pl.BlockSpec((r, d), lambda i: (0, 0))
    return pl.pallas_call(
        body, name="post_fwd", grid=(t // tm,),
        in_specs=[row, row, vec(1), vec(2)], out_specs=row,
        out_shape=jax.ShapeDtypeStruct((t, d), F32), compiler_params=_cp(),
    )(xs, y, pg, gate)


def post_loss(xs, y, pg, gate, target, cx):
    t, d = xs.shape
    n = y.shape[0]
    tm = _row_block(cx)
    nctx = cx // tm

    def body(x_ref, y_ref, pg_ref, gate_ref, tgt_ref, sq_ref, dx_ref):
        @pl.when(pl.program_id(0) == 0)
        def _():
            sq_ref[...] = jnp.zeros_like(sq_ref)

        y = y_ref[...]
        rstd = lax.rsqrt(jnp.mean(y * y, axis=-1, keepdims=True) + EPS)
        x2 = x_ref[...] + gate_ref[1:2, :] * ((y * rstd) * pg_ref[...])
        err = x2 - tgt_ref[...]
        sq_ref[...] += jnp.sum(err * err)
        dx_ref[...] = err * (1.0 / d)

    row = pl.BlockSpec((tm, d), lambda i: (i, 0))
    xrow = pl.BlockSpec((tm, d), lambda i: (i + nctx, 0))
    vec = lambda r: pl.BlockSpec((r, d), lambda i: (0, 0))
    return pl.pallas_call(
        body, name="post_loss", grid=(n // tm,),
        in_specs=[xrow, row, vec(1), vec(2), row],
        out_specs=[pl.BlockSpec((8, 128), lambda i: (0, 0)), row],
        out_shape=[jax.ShapeDtypeStruct((8, 128), F32), jax.ShapeDtypeStruct((n, d), F32)],
        compiler_params=_cp(),
    )(xs, y, pg, gate, target)


def post_bwd(dxo, y, pg, gate, cx, latent_only):
    m, d = y.shape
    tm = _row_block(cx)
    nctx = 0 if latent_only else cx // tm

    def body(dx_ref, y_ref, pg_ref, gate_ref, dy_ref, dgate_ref, dpg_ref):
        i = pl.program_id(0)
        is_ctx = i < nctx

        @pl.when(i == 0)
        def _():
            dgate_ref[...] = jnp.zeros_like(dgate_ref)
            dpg_ref[...] = jnp.zeros_like(dpg_ref)

        y = y_ref[...]
        dx = dx_ref[...]
        pgv = pg_ref[...]
        rstd = lax.rsqrt(jnp.mean(y * y, axis=-1, keepdims=True) + EPS)
        yhat = y * rstd
        gt = jnp.where(is_ctx, gate_ref[0:1, :], gate_ref[1:2, :])
        dxy = dx * yhat
        dgt = jnp.sum(dxy * pgv, axis=0, keepdims=True)
        zero = jnp.zeros_like(dgt)
        dgate_ref[0:1, :] += jnp.where(is_ctx, dgt, zero)
        dgate_ref[1:2, :] += jnp.where(is_ctx, zero, dgt)
        dpg_ref[...] += jnp.sum(dxy * gt, axis=0, keepdims=True)
        dyhat = dx * (gt * pgv)
        dy = rstd * (dyhat - yhat * jnp.mean(dyhat * yhat, axis=-1, keepdims=True))
        dy_ref[...] = dy.astype(BF16)

    row = pl.BlockSpec((tm, d), lambda i: (i, 0))
    vec = lambda r: pl.BlockSpec((r, d), lambda i: (0, 0))
    return pl.pallas_call(
        body, name="post_bwd", grid=(m // tm,),
        in_specs=[row, row, vec(1), vec(2)], out_specs=[row, vec(2), vec(1)],
        out_shape=[jax.ShapeDtypeStruct((m, d), BF16), jax.ShapeDtypeStruct((2, d), F32),
                   jax.ShapeDtypeStruct((1, d), F32)],
        compiler_params=_cp(),
    )(dxo, y, pg, gate)


def _split_rows(m):
    for cand in (1024, 768, 512, 384, 256, 128):
        if m % cand == 0 and m // cand >= 2:
            return cand
    return m


def mm_nn(a, w3, sec, tn, name):
    m, k = a.shape
    q, _, ws = w3.shape
    n = q * ws
    tpq, tps = ws // tn, sec // tn
    tm = next(c for c in (768, 512, 256, 128) if m % c == 0)

    def body(a_ref, w_ref, o_ref):
        w = w_ref[...]

        def step(i, carry):
            rows = pl.ds(pl.multiple_of(i * tm, tm), tm)
            o_ref[rows, :] = _dot(a_ref[rows, :], w)
            return carry

        lax.fori_loop(0, m // tm, step, 0)

    return pl.pallas_call(
        body, name=name, grid=(n // tn,),
        in_specs=[pl.BlockSpec((m, k), lambda j: (0, 0)),
                  pl.BlockSpec((None, k, tn), lambda j: (j // tpq, 0, j % tpq))],
        out_specs=pl.BlockSpec((None, m, tn), lambda j: (j // tps, 0, j % tps)),
        out_shape=jax.ShapeDtypeStruct((n // sec, m, sec), F32), compiler_params=_cp(),
    )(a, w3)


def _two_stacks(a3, b3, tn):
    sec = a3.shape[2]
    tps = sec // tn
    n1 = a3.shape[0] * tps
    first = lambda j: (jnp.minimum(j, n1 - 1) // tps, jnp.minimum(j, n1 - 1) % tps)
    second = lambda j: (jnp.maximum(j - n1, 0) // tps, jnp.maximum(j - n1, 0) % tps)
    return n1, first, second


def mm_nt(a3, b3, w3, tn, name):
    if b3 is None:
        b3 = a3
    _, m, sec = a3.shape
    q, k, ws = w3.shape
    n = q * ws
    tpq = ws // tn
    mb = _split_rows(m)
    n1, first, second = _two_stacks(a3, b3, tn)

    def body(a_ref, b_ref, w_ref, o_ref):
        j = pl.program_id(1)

        @pl.when(j == 0)
        def _():
            o_ref[...] = jnp.zeros_like(o_ref)

        @pl.when(j < n1)
        def _():
            o_ref[...] += _dot(a_ref[...], w_ref[...], NT)

        @pl.when(j >= n1)
        def _():
            o_ref[...] += _dot(b_ref[...], w_ref[...], NT)

    return pl.pallas_call(
        body, name=name, grid=(m // mb, n // tn),
        in_specs=[pl.BlockSpec((None, mb, tn), lambda i, j: (first(j)[0], i, first(j)[1])),
                  pl.BlockSpec((None, mb, tn), lambda i, j: (second(j)[0], i, second(j)[1])),
                  pl.BlockSpec((None, k, tn), lambda i, j: (j // tpq, 0, j % tpq))],
        out_specs=pl.BlockSpec((mb, k), lambda i, j: (i, 0)),
        out_shape=jax.ShapeDtypeStruct((m, k), F32), compiler_params=_cp(),
    )(a3, b3, w3)


def mm_tn(a, b3, c3, ws, tn, name):
    m, k = a.shape
    sec = b3.shape[2]
    n = (b3.shape[0] + (0 if c3 is None else c3.shape[0])) * sec
    if c3 is None:
        c3 = b3
    tpq = ws // tn
    kb = 256 if k % 256 == 0 else 128
    n1, first, second = _two_stacks(b3, c3, tn)

    def body(a_ref, b_ref, c_ref, o_ref):
        def product(rhs_ref):
            rhs = rhs_ref[...]
            for i in range(k // kb):
                o_ref[i * kb:(i + 1) * kb, :] = _dot(a_ref[:, i * kb:(i + 1) * kb], rhs, TN).astype(BF16)

        @pl.when(pl.program_id(0) < n1)
        def _():
            product(b_ref)

        @pl.when(pl.program_id(0) >= n1)
        def _():
            product(c_ref)

    return pl.pallas_call(
        body, name=name, grid=(n // tn,),
        in_specs=[pl.BlockSpec((m, k), lambda j: (0, 0)),
                  pl.BlockSpec((None, m, tn), lambda j: (first(j)[0], 0, first(j)[1])),
                  pl.BlockSpec((None, m, tn), lambda j: (second(j)[0], 0, second(j)[1]))],
        out_specs=pl.BlockSpec((None, k, tn), lambda j: (j // tpq, 0, j % tpq)),
        out_shape=jax.ShapeDtypeStruct((n // ws, k, ws), BF16), compiler_params=_cp(),
    )(a, b3, c3)


POOL_REACH = 8 << GRID_W_LOG2


def _token_parts(tok, cx):
    lat = tok - cx
    return tok < cx, lat >> GRID_W_LOG2, lat & ((1 << GRID_W_LOG2) - 1)


def _pool_mask(gi, row0, col0, tm, ncols, cx, transposed):
    half = jnp.left_shift(1, gi)
    r = lax.broadcasted_iota(jnp.int32, (tm, 1), 0) + row0
    c = lax.broadcasted_iota(jnp.int32, (1, ncols), 1) + col0
    out_tok, src_tok = (c, r) if transposed else (r, c)
    o_ctx, o_row, o_col = _token_parts(out_tok, cx)
    s_ctx, s_row, s_col = _token_parts(src_tok, cx)

    def inside(o, s):
        return (s >= o - half) & (s <= o + half - 1)

    ctx_hit = o_ctx & s_ctx & inside(out_tok, src_tok)
    lat_hit = (~o_ctx) & (~s_ctx) & inside(o_row, s_row) & inside(o_col, s_col)
    return jnp.where(ctx_hit | lat_hit, 1.0, 0.0).astype(BF16)


def _pool_inv_count(gi, row0, tm, cx, seq):
    half = jnp.left_shift(1, gi)
    r = lax.broadcasted_iota(jnp.int32, (tm, 1), 0) + row0
    is_ctx, row, col = _token_parts(r, cx)

    def count(pos, size):
        return jnp.minimum(pos + half - 1, size - 1) - jnp.maximum(pos - half, 0) + 1

    cnt = jnp.where(is_ctx, count(r, cx), count(row, seq >> GRID_W_LOG2) * count(col, 1 << GRID_W_LOG2))
    return 1.0 / cnt.astype(F32)


def _window_sum(gi, i, tm, t, cx, src_ref, transposed):
    side = POOL_REACH // tm
    nblk = t // tm
    band = min(2 * side + 1, nblk)
    col0 = pl.multiple_of(jnp.clip(i - side, 0, nblk - band) * tm, tm)
    mask = _pool_mask(gi, i * tm, col0, tm, band * tm, cx, transposed)
    return _dot(mask, src_ref[pl.ds(col0, band * tm), :])


def mix_a_fwd(z0, pool_w, pool_scale, cx):
    _, t, half_d = z0.shape
    g = half_d // N_POOL
    seq = t - cx
    tm = _row_block(cx)

    def body(v_ref, ag_ref, w_ref, sc_ref, u_ref, vb_ref):
        gi = pl.program_id(0)
        vb_ref[...] = v_ref[...].astype(BF16)
        w = w_ref[...].astype(BF16)
        sc = sc_ref[...]

        def step(i, carry):
            row0 = pl.multiple_of(i * tm, tm)
            rows = pl.ds(row0, tm)
            inv = _pool_inv_count(gi, row0, tm, cx, seq)
            pooled = _window_sum(gi, i, tm, t, cx, vb_ref, False) * inv - v_ref[rows, :]
            mixed = _dot(pooled.astype(BF16), w) * sc
            u_ref[rows, :] = (mixed * _silu(ag_ref[rows, :])).astype(BF16)
            return carry

        lax.fori_loop(0, t // tm, step, 0)

    sec = lambda s: pl.BlockSpec((None, t, g), lambda j: (s, 0, j))
    return pl.pallas_call(
        body, name="mix_a_fwd", grid=(N_POOL,),
        in_specs=[sec(0), sec(1), pl.BlockSpec((None, g, g), lambda j: (j, 0, 0)),
                  pl.BlockSpec((1, g), lambda j: (0, j))],
        out_specs=pl.BlockSpec((t, g), lambda j: (0, j)),
        out_shape=jax.ShapeDtypeStruct((t, half_d), BF16),
        scratch_shapes=[pltpu.VMEM((t, g), BF16)], compiler_params=_cp(),
    )(z0, z0, pool_w, pool_scale)


def mix_a_bwd(z0, du, pool_w, pool_scale, cx):
    _, t, half_d = z0.shape
    g = half_d // N_POOL
    seq = t - cx
    tm = _row_block(cx)
    gq = g // 4

    def body(v_ref, ag_ref, du_ref, w_ref, sc_ref, dz_ref, dw_ref, dsc_ref,
             vb_ref, pooled_ref, dmx_ref, dpl_ref, wdp_ref):
        gi = pl.program_id(0)
        vb_ref[...] = v_ref[...].astype(BF16)
        w = w_ref[...].astype(BF16)
        sc = sc_ref[...]

        def first(i, dsc):
            row0 = pl.multiple_of(i * tm, tm)
            rows = pl.ds(row0, tm)
            inv = _pool_inv_count(gi, row0, tm, cx, seq)
            pooled = (_window_sum(gi, i, tm, t, cx, vb_ref, False) * inv - v_ref[rows, :]).astype(BF16)
            pooled_ref[rows, :] = pooled
            mixed = _dot(pooled, w)
            ag = ag_ref[rows, :]
            duv = du_ref[rows, :]
            dz_ref[1, rows, :] = (duv * (mixed * sc) * _dsilu(ag)).astype(BF16)
            dms = duv * _silu(ag)
            dmixed = (dms * sc).astype(BF16)
            dmx_ref[rows, :] = dmixed
            dpooled = _dot(dmixed, w, NT)
            dpl_ref[rows, :] = dpooled
            wdp_ref[rows, :] = (dpooled * inv).astype(BF16)
            return dsc + jnp.sum(dms * mixed, axis=0, keepdims=True)

        dsc_ref[...] = lax.fori_loop(0, t // tm, first, jnp.zeros((1, g), F32))
        dw = _dot(pooled_ref[...], dmx_ref[...], TN)
        for qi in range(4):
            dw_ref[qi] = dw[qi * gq:(qi + 1) * gq, :]

        def second(i, carry):
            row0 = pl.multiple_of(i * tm, tm)
            rows = pl.ds(row0, tm)
            dz_ref[0, rows, :] = (_window_sum(gi, i, tm, t, cx, wdp_ref, True) - dpl_ref[rows, :]).astype(BF16)
            return carry

        lax.fori_loop(0, t // tm, second, 0)

    sec = lambda s: pl.BlockSpec((None, t, g), lambda j: (s, 0, j))
    return pl.pallas_call(
        body, name="mix_a_bwd", grid=(N_POOL,),
        in_specs=[sec(0), sec(1), pl.BlockSpec((t, g), lambda j: (0, j)),
                  pl.BlockSpec((None, g, g), lambda j: (j, 0, 0)),
                  pl.BlockSpec((1, g), lambda j: (0, j))],
        out_specs=[pl.BlockSpec((2, t, g), lambda j: (0, 0, j)),
                   pl.BlockSpec((4, None, gq, g), lambda j: (0, j, 0, 0)),
                   pl.BlockSpec((1, g), lambda j: (0, j))],
        out_shape=[jax.ShapeDtypeStruct((2, t, half_d), BF16),
                   jax.ShapeDtypeStruct((4, N_POOL, gq, g), F32),
                   jax.ShapeDtypeStruct((1, half_d), F32)],
        scratch_shapes=[pltpu.VMEM((t, g), BF16), pltpu.VMEM((t, g), BF16), pltpu.VMEM((t, g), BF16),
                        pltpu.VMEM((t, g), F32), pltpu.VMEM((t, g), BF16)],
        compiler_params=_cp(),
    )(z0, z0, du, pool_w, pool_scale)


def _conv_masks(t, cx):
    r = lax.broadcasted_iota(jnp.int32, (t, 1), 0)
    has_prev = jnp.where((r == 0) | (r == cx), 0.0, 1.0)
    has_next = jnp.where((r == cx - 1) | (r == t - 1), 0.0, 1.0)
    return has_prev, has_next


def mix_b_fwd(z0, conv_w, conv_b, cx):
    _, t, half_d = z0.shape
    gb = 128

    def body(bx_ref, bb_ref, bc_ref, bg_ref, w_ref, b_ref, u_ref):
        has_prev, has_next = _conv_masks(t, cx)
        tt = bc_ref[...] * bx_ref[...]
        prev = pltpu.roll(tt, 1, 0) * has_prev
        nxt = pltpu.roll(tt, t - 1, 0) * has_next
        cv = prev * w_ref[0:1, :] + tt * w_ref[1:2, :] + nxt * w_ref[2:3, :] + b_ref[...]
        u_ref[...] = (bb_ref[...] * cv * _silu(bg_ref[...])).astype(BF16)

    sec = lambda s: pl.BlockSpec((None, t, gb), lambda j: (s, 0, j))
    return pl.pallas_call(
        body, name="mix_b_fwd", grid=(half_d // gb,),
        in_specs=[sec(2), sec(3), sec(4), sec(5), pl.BlockSpec((3, gb), lambda j: (0, j)),
                  pl.BlockSpec((1, gb), lambda j: (0, j))],
        out_specs=pl.BlockSpec((t, gb), lambda j: (0, j)),
        out_shape=jax.ShapeDtypeStruct((t, half_d), BF16), compiler_params=_cp(),
    )(z0, z0, z0, z0, conv_w, conv_b)


def mix_b_bwd(z0, du, conv_w, conv_b, cx):
    _, t, half_d = z0.shape
    gb = 128
    off = half_d // gb

    def body(bx_ref, bb_ref, bc_ref, bg_ref, du_ref, w_ref, b_ref, dz_ref, dw_ref, db_ref):
        has_prev, has_next = _conv_masks(t, cx)
        bx, bb, bc, bg = bx_ref[...], bb_ref[...], bc_ref[...], bg_ref[...]
        duv = du_ref[...]
        tt = bc * bx
        prev = pltpu.roll(tt, 1, 0) * has_prev
        nxt = pltpu.roll(tt, t - 1, 0) * has_next
        w0, w1, w2 = w_ref[0:1, :], w_ref[1:2, :], w_ref[2:3, :]
        cv = prev * w0 + tt * w1 + nxt * w2 + b_ref[...]
        sg = _silu(bg)
        dz_ref[1] = (duv * cv * sg).astype(BF16)
        dz_ref[3] = (duv * bb * cv * _dsilu(bg)).astype(BF16)
        dcv = duv * bb * sg
        dw_ref[0:1, :] = jnp.sum(dcv * prev, axis=0, keepdims=True)
        dw_ref[1:2, :] = jnp.sum(dcv * tt, axis=0, keepdims=True)
        dw_ref[2:3, :] = jnp.sum(dcv * nxt, axis=0, keepdims=True)
        db_ref[...] = jnp.sum(dcv, axis=0, keepdims=True)
        dt = (pltpu.roll(dcv * has_prev, t - 1, 0) * w0 + dcv * w1
              + pltpu.roll(dcv * has_next, 1, 0) * w2)
        dz_ref[0] = (dt * bc).astype(BF16)
        dz_ref[2] = (dt * bx).astype(BF16)

    sec = lambda s: pl.BlockSpec((None, t, gb), lambda j: (s, 0, j))
    return pl.pallas_call(
        body, name="mix_b_bwd", grid=(half_d // gb,),
        in_specs=[sec(2), sec(3), sec(4), sec(5), pl.BlockSpec((t, gb), lambda j: (0, j + off)),
                  pl.BlockSpec((3, gb), lambda j: (0, j)), pl.BlockSpec((1, gb), lambda j: (0, j))],
        out_specs=[pl.BlockSpec((4, t, gb), lambda j: (0, 0, j)),
                   pl.BlockSpec((3, gb), lambda j: (0, j)), pl.BlockSpec((1, gb), lambda j: (0, j))],
        out_shape=[jax.ShapeDtypeStruct((4, t, half_d), BF16),
                   jax.ShapeDtypeStruct((3, half_d), F32), jax.ShapeDtypeStruct((1, half_d), F32)],
        compiler_params=_cp(),
    )(z0, z0, z0, z0, du, conv_w, conv_b)


def _lower_bound(lbl_ref, d):
    l0, l1, l2 = lbl_ref[d, 0:1, :], lbl_ref[d, 1:2, :], lbl_ref[d, 2:3, :]
    mx = jnp.maximum(jnp.maximum(l0, l1), l2)
    e0, e1, e2 = jnp.exp(l0 - mx), jnp.exp(l1 - mx), jnp.exp(l2 - mx)
    inv = 1.0 / (e0 + e1 + e2)
    return (e0 + e1) * inv, (e0 * inv, e1 * inv, e2 * inv)


def _chunk_consts(d):
    r = lax.broadcasted_iota(jnp.int32, (CHUNK, CHUNK), 0)
    c = lax.broadcasted_iota(jnp.int32, (CHUNK, CHUNK), 1)
    keep = (c <= r) if d == 0 else (c >= r)
    return jnp.where(keep, 1.0, 0.0).astype(F32), keep


def _chunk_of_step(s, d, nc, ncc):
    if d == 0:
        return s
    return jnp.where(s < ncc, ncc - 1 - s, nc - 1 + ncc - s)


def _chunk_terms(lfc, kc, qc, cum):
    bc = _dot(cum, lfc, precision=lax.Precision.HIGHEST)
    bl = jnp.sum(lfc, axis=0, keepdims=True)
    e = jnp.exp(bc)
    einv = jnp.exp(-bc)
    erem = jnp.exp(bl - bc)
    return e, einv, erem, jnp.exp(bl), qc * e, kc * einv, kc * erem


def hgrn_fwd(z1, lbl, onorm, cx):
    _, t, d = z1.shape
    seq = t - cx
    nc, ncc = t // CHUNK, cx // CHUNK

    def body(zf_ref, zb_ref, v_ref, q_ref, g_ref, lbl_ref, on_ref, o_ref, r_ref,
             lf_ref, k_ref, oacc_ref, st_ref):
        for dr, z_ref in ((0, zf_ref), (1, zb_ref)):
            lbv, _ = _lower_bound(lbl_ref, dr)
            z = z_ref[...]
            lf_ref[...] = jnp.log(lbv + (1.0 - lbv) * jax.nn.sigmoid(z))
            k_ref[...] = (1.0 - lbv) * jax.nn.sigmoid(-z)
            st_ref[...] = jnp.zeros_like(st_ref)
            cum, keep = _chunk_consts(dr)

            def step(s, carry, dr=dr, cum=cum, keep=keep):
                n = _chunk_of_step(s, dr, nc, ncc)
                rows = pl.ds(pl.multiple_of(n * CHUNK, CHUNK), CHUNK)
                vc = v_ref[rows, :].astype(BF16)
                _, _, _, dec, qd, ki, kd = _chunk_terms(lf_ref[rows, :], k_ref[rows, :], q_ref[rows, :], cum)
                qdb = qd.astype(BF16)
                a = jnp.where(keep, _dot(qdb, ki.astype(BF16), NT), 0.0)
                st = st_ref[...]
                oc = _dot(qdb, st.astype(BF16), NT) + _dot(a.astype(BF16), vc)
                st_ref[...] = st * dec + _dot(vc, kd.astype(BF16), TN)
                if dr == 0:
                    oacc_ref[rows, :] = oc
                else:
                    oacc_ref[rows, :] += oc
                return carry

            lax.fori_loop(0, nc, step, 0, unroll=4)

        o = oacc_ref[cx:, :]
        o_ref[...] = o
        rstd = lax.rsqrt(jnp.mean(o * o, axis=-1, keepdims=True) + EPS)
        r_ref[...] = (o * rstd * on_ref[...] * _silu(g_ref[cx:, :])).astype(BF16)

    sec = lambda s: pl.BlockSpec((None, t, HEAD), lambda h: (s, 0, h))
    col = pl.BlockSpec((seq, HEAD), lambda h: (0, h))
    return pl.pallas_call(
        body, name="hgrn_fwd", grid=(d // HEAD,),
        in_specs=[sec(0), sec(1), sec(2), sec(3), sec(4),
                  pl.BlockSpec((2, 3, HEAD), lambda h: (0, 0, h)), pl.BlockSpec((1, HEAD), lambda h: (0, h))],
        out_specs=[col, col],
        out_shape=[jax.ShapeDtypeStruct((seq, d), F32), jax.ShapeDtypeStruct((seq, d), BF16)],
        scratch_shapes=[pltpu.VMEM((t, HEAD), F32), pltpu.VMEM((t, HEAD), F32), pltpu.VMEM((t, HEAD), F32),
                        pltpu.VMEM((HEAD, HEAD), F32)],
        compiler_params=_cp(),
    )(z1, z1, z1, z1, z1, lbl, onorm)


def hgrn_bwd(z1, lbl, onorm, o, dr_out, cx):
    _, t, d = z1.shape
    seq = t - cx
    nc, ncc = t // CHUNK, cx // CHUNK

    def body(zf_ref, zb_ref, v_ref, q_ref, g_ref, lbl_ref, on_ref, o_ref, dr_ref,
             dz_ref, don_ref, dlb_ref,
             lf_ref, k_ref, do_ref, dq_ref, dv_ref, dk_ref, dlf_ref, ssc_ref, dst_ref):
        o = o_ref[...]
        g = g_ref[cx:, :]
        drv = dr_ref[...]
        onv = on_ref[...]
        rstd = lax.rsqrt(jnp.mean(o * o, axis=-1, keepdims=True) + EPS)
        ohat = o * rstd
        sg = _silu(g)
        don_ref[...] = jnp.sum(drv * ohat * sg, axis=0, keepdims=True)
        dz_ref[4, :cx, :] = jnp.zeros((cx, HEAD), BF16)
        dz_ref[4, cx:, :] = (drv * ohat * onv * _dsilu(g)).astype(BF16)
        dohat = drv * onv * sg
        do_ref[:cx, :] = jnp.zeros((cx, HEAD), F32)
        do_ref[cx:, :] = rstd * (dohat - ohat * jnp.mean(dohat * ohat, axis=-1, keepdims=True))

        for dr, z_ref in ((0, zf_ref), (1, zb_ref)):
            lbv, _ = _lower_bound(lbl_ref, dr)
            z = z_ref[...]
            lf_ref[...] = jnp.log(lbv + (1.0 - lbv) * jax.nn.sigmoid(z))
            k_ref[...] = (1.0 - lbv) * jax.nn.sigmoid(-z)
            cum, keep = _chunk_consts(dr)
            cum_t, _ = _chunk_consts(1 - dr)

            st_init = jnp.zeros((HEAD, HEAD), F32)

            def state_step(s, st, dr=dr, cum=cum):
                n = _chunk_of_step(s, dr, nc, ncc)
                rows = pl.ds(pl.multiple_of(n * CHUNK, CHUNK), CHUNK)
                ssc_ref[n] = st
                _, _, _, dec, _, _, kd = _chunk_terms(lf_ref[rows, :], k_ref[rows, :], q_ref[rows, :], cum)
                return st * dec + _dot(v_ref[rows, :].astype(BF16), kd.astype(BF16), TN)

            lax.fori_loop(0, nc, state_step, st_init, unroll=4)
            dst_ref[...] = jnp.zeros_like(dst_ref)

            def grad_step(s2, carry, dr=dr, cum=cum, cum_t=cum_t, keep=keep):
                n = _chunk_of_step(nc - 1 - s2, dr, nc, ncc)
                rows = pl.ds(pl.multiple_of(n * CHUNK, CHUNK), CHUNK)
                vc = v_ref[rows, :].astype(BF16)
                e, einv, erem, dec, qd, ki, kd = _chunk_terms(
                    lf_ref[rows, :], k_ref[rows, :], q_ref[rows, :], cum)
                qdb, kib, kdb = qd.astype(BF16), ki.astype(BF16), kd.astype(BF16)
                doc = do_ref[rows, :].astype(BF16)
                st0 = ssc_ref[n]
                dst = dst_ref[...]
                dstb = dst.astype(BF16)
                a = jnp.where(keep, _dot(qdb, kib, NT), 0.0).astype(BF16)
                da = jnp.where(keep, _dot(doc, vc, NT), 0.0).astype(BF16)
                dqd = _dot(doc, st0.astype(BF16)) + _dot(da, kib)
                dki = _dot(da, qdb, TN)
                dv = _dot(a, doc, TN) + _dot(kdb, dstb, NT)
                dkd = _dot(vc, dstb)
                ddec = jnp.sum(dst * st0, axis=0, keepdims=True)
                dst_ref[...] = _dot(doc, qdb, TN) + dst * dec
                dbc = dqd * qd - dki * ki - dkd * kd
                dbl = jnp.sum(dkd * kd, axis=0, keepdims=True) + ddec * dec
                dlf_ref[rows, :] = _dot(cum_t, dbc, precision=lax.Precision.HIGHEST) + dbl
                dk_ref[rows, :] = dki * einv + dkd * erem
                if dr == 0:
                    dq_ref[rows, :] = dqd * e
                    dv_ref[rows, :] = dv
                else:
                    dq_ref[rows, :] += dqd * e
                    dv_ref[rows, :] += dv
                return carry

            lax.fori_loop(0, nc, grad_step, 0, unroll=2)

            sig = jax.nn.sigmoid(z)
            one_lb = 1.0 - lbv
            f = lbv + one_lb * sig
            dlf = dlf_ref[...]
            dk = dk_ref[...]
            dsig = (dlf / f - dk) * one_lb
            dz_ref[dr] = (dsig * sig * (1.0 - sig)).astype(BF16)
            dlb_ref[dr:dr + 1, :] = jnp.sum((dlf / f - dk) * (1.0 - sig), axis=0, keepdims=True)

        dz_ref[2] = dv_ref[...].astype(BF16)
        dz_ref[3] = dq_ref[...].astype(BF16)

    sec = lambda s: pl.BlockSpec((None, t, HEAD), lambda h: (s, 0, h))
    col = pl.BlockSpec((seq, HEAD), lambda h: (0, h))
    tvec = pltpu.VMEM((t, HEAD), F32)
    return pl.pallas_call(
        body, name="hgrn_bwd", grid=(d // HEAD,),
        in_specs=[sec(0), sec(1), sec(2), sec(3), sec(4),
                  pl.BlockSpec((2, 3, HEAD), lambda h: (0, 0, h)), pl.BlockSpec((1, HEAD), lambda h: (0, h)),
                  col, col],
        out_specs=[pl.BlockSpec((5, t, HEAD), lambda h: (0, 0, h)),
                   pl.BlockSpec((1, HEAD), lambda h: (0, h)), pl.BlockSpec((2, HEAD), lambda h: (0, h))],
        out_shape=[jax.ShapeDtypeStruct((5, t, d), BF16), jax.ShapeDtypeStruct((1, d), F32),
                   jax.ShapeDtypeStruct((2, d), F32)],
        scratch_shapes=[tvec, tvec, tvec, tvec, tvec, tvec, tvec,
                        pltpu.VMEM((nc, HEAD, HEAD), F32), pltpu.VMEM((HEAD, HEAD), F32)],
        compiler_params=_cp(),
    )(z1, z1, z1, z1, z1, lbl, onorm, o, dr_out)


def _gates(z, lbv):
    e = jnp.exp(-jnp.abs(z))
    r = 1.0 / (1.0 + e)
    er = e * r
    pos = z >= 0.0
    sig = jnp.where(pos, r, er)
    nsig = jnp.where(pos, er, r)
    return sig, nsig, lbv + (1.0 - lbv) * sig


def _split3(x):
    hi = x.astype(BF16)
    r1 = x - hi.astype(F32)
    mid = r1.astype(BF16)
    lo = (r1 - mid.astype(F32)).astype(BF16)
    return jnp.concatenate([hi, mid, lo], axis=1)


def _cumsum_chunk(cum, x):
    y = _dot(cum, _split3(x))
    return y[:, :HEAD] + y[:, HEAD:2 * HEAD] + y[:, 2 * HEAD:]


def _chunk_rows(n):
    return pl.ds(pl.multiple_of(n * CHUNK, CHUNK), CHUNK)


def _group(nc):
    return next(u for u in (4, 3, 2, 1) if nc % u == 0)


def _decay_pass(lf_ref, bc_ref, dec_ref, cum, nc):
    grp = _group(nc)

    def step(m, carry):
        ns = [m * grp + u for u in range(grp)]
        lfc = [lf_ref[_chunk_rows(n), :] for n in ns]
        bc = [_cumsum_chunk(cum, x) for x in lfc]
        for u, n in enumerate(ns):
            bc_ref[_chunk_rows(n), :] = bc[u]
            dec_ref[n] = jnp.broadcast_to(jnp.exp(jnp.sum(lfc[u], axis=0, keepdims=True)), (8, HEAD))
        return carry

    lax.fori_loop(0, nc // grp, step, 0)


def hgrn_fwd(z1, lbl, onorm, cx):
    _, t, d = z1.shape
    seq = t - cx
    nc, ncc = t // CHUNK, cx // CHUNK

    grp = _group(nc)

    def body(zf_ref, zb_ref, v_ref, q_ref, g_ref, lbl_ref, on_ref, o_ref, r_ref,
             lf_ref, k_ref, bc_ref, dec_ref, qd_ref, ki_ref, oacc_ref, ds_ref):
        for dr, z_ref in ((0, zf_ref), (1, zb_ref)):
            lbv, _ = _lower_bound(lbl_ref, dr)
            _, nsig, f = _gates(z_ref[...], lbv)
            lf_ref[...] = jnp.log(f)
            k_ref[...] = (1.0 - lbv) * nsig
            cum, keep = _chunk_consts(dr)
            _decay_pass(lf_ref, bc_ref, dec_ref, cum.astype(BF16), nc)
            bc = bc_ref[...]
            qd_ref[...] = (q_ref[...] * jnp.exp(bc)).astype(BF16)
            ki_ref[...] = (k_ref[...] * jnp.exp(-bc)).astype(BF16)

            def local_step(m, carry, dr=dr, keep=keep):
                ns = [m * grp + u for u in range(grp)]
                rows = [_chunk_rows(n) for n in ns]
                qd = [qd_ref[r, :] for r in rows]
                ki = [ki_ref[r, :] for r in rows]
                vc = [v_ref[r, :].astype(BF16) for r in rows]
                sc = [_dot(qd[u], ki[u], NT) for u in range(grp)]
                inc = [_dot(vc[u], ki[u], TN) for u in range(grp)]
                a = [jnp.where(keep, s, 0.0).astype(BF16) for s in sc]
                intra = [_dot(a[u], vc[u]) for u in range(grp)]
                for u in range(grp):
                    ds_ref[ns[u]] = inc[u] * dec_ref[ns[u]][0:1, :]
                    if dr == 0:
                        oacc_ref[rows[u], :] = intra[u]
                    else:
                        oacc_ref[rows[u], :] += intra[u]
                return carry

            lax.fori_loop(0, nc // grp, local_step, 0)

            def state_step(m, st, dr=dr):
                ns = [_chunk_of_step(m * grp + u, dr, nc, ncc) for u in range(grp)]
                rows = [_chunk_rows(n) for n in ns]
                sts = []
                for n in ns:
                    sts.append(st.astype(BF16))
                    st = st * dec_ref[n][0:1, :] + ds_ref[n]
                inter = [_dot(qd_ref[rows[u], :], sts[u], NT) for u in range(grp)]
                for u in range(grp):
                    oacc_ref[rows[u], :] += inter[u]
                return st

            lax.fori_loop(0, nc // grp, state_step, jnp.zeros((HEAD, HEAD), F32))

        o = oacc_ref[cx:, :]
        o_ref[...] = o
        rstd = lax.rsqrt(jnp.mean(o * o, axis=-1, keepdims=True) + EPS)
        r_ref[...] = (o * rstd * on_ref[...] * _silu(g_ref[cx:, :])).astype(BF16)

    sec = lambda s: pl.BlockSpec((None, t, HEAD), lambda h: (s, 0, h))
    col = pl.BlockSpec((seq, HEAD), lambda h: (0, h))
    tf32, tb16 = pltpu.VMEM((t, HEAD), F32), pltpu.VMEM((t, HEAD), BF16)
    return pl.pallas_call(
        body, name="hgrn_fwd", grid=(d // HEAD,),
        in_specs=[sec(0), sec(1), sec(2), sec(3), sec(4),
                  pl.BlockSpec((2, 3, HEAD), lambda h: (0, 0, h)), pl.BlockSpec((1, HEAD), lambda h: (0, h))],
        out_specs=[col, col],
        out_shape=[jax.ShapeDtypeStruct((seq, d), F32), jax.ShapeDtypeStruct((seq, d), BF16)],
        scratch_shapes=[tf32, tf32, tf32, pltpu.VMEM((nc, 8, HEAD), F32), tb16, tb16, tf32,
                        pltpu.VMEM((nc, HEAD, HEAD), F32)],
        compiler_params=_cp(),
    )(z1, z1, z1, z1, z1, lbl, onorm)


def hgrn_bwd(z1, lbl, onorm, o, dr_out, cx):
    _, t, d = z1.shape
    seq = t - cx
    nc, ncc = t // CHUNK, cx // CHUNK

    grp2 = grp = _group(nc)

    def body(zf_ref, zb_ref, v_ref, q_ref, g_ref, lbl_ref, on_ref, o_ref, dr_ref,
             dz_ref, don_ref, dlb_ref,
             lf_ref, k_ref, bc_ref, dec_ref, qd_ref, ki_ref, do_ref,
             dqd_ref, dki_ref, dq_ref, dv_ref, ds_ref, dsl_ref):
        o = o_ref[...]
        g = g_ref[cx:, :]
        drv = dr_ref[...]
        onv = on_ref[...]
        rstd = lax.rsqrt(jnp.mean(o * o, axis=-1, keepdims=True) + EPS)
        ohat = o * rstd
        sg = _silu(g)
        don_ref[...] = jnp.sum(drv * ohat * sg, axis=0, keepdims=True)
        dz_ref[4, :cx, :] = jnp.zeros((cx, HEAD), BF16)
        dz_ref[4, cx:, :] = (drv * ohat * onv * _dsilu(g)).astype(BF16)
        dohat = drv * onv * sg
        do_ref[:cx, :] = jnp.zeros((cx, HEAD), BF16)
        do_ref[cx:, :] = (rstd * (dohat - ohat * jnp.mean(dohat * ohat, axis=-1, keepdims=True))).astype(BF16)

        for dr, z_ref in ((0, zf_ref), (1, zb_ref)):
            lbv, _ = _lower_bound(lbl_ref, dr)
            _, nsig, f = _gates(z_ref[...], lbv)
            lf_ref[...] = jnp.log(f)
            k_ref[...] = (1.0 - lbv) * nsig
            cum, keep = _chunk_consts(dr)
            cum_t = _chunk_consts(1 - dr)[0].astype(BF16)
            _decay_pass(lf_ref, bc_ref, dec_ref, cum.astype(BF16), nc)
            bc = bc_ref[...]
            qd_ref[...] = (q_ref[...] * jnp.exp(bc)).astype(BF16)
            ki_ref[...] = (k_ref[...] * jnp.exp(-bc)).astype(BF16)

            def local_step(m, carry, dr=dr, keep=keep):
                ns = [m * grp + u for u in range(grp)]
                rows = [_chunk_rows(n) for n in ns]
                rng = range(grp)
                qd = [qd_ref[r, :] for r in rows]
                ki = [ki_ref[r, :] for r in rows]
                doc = [do_ref[r, :] for r in rows]
                vc = [v_ref[r, :].astype(BF16) for r in rows]
                sc = [_dot(qd[u], ki[u], NT) for u in rng]
                dsc = [_dot(doc[u], vc[u], NT) for u in rng]
                inc = [_dot(vc[u], ki[u], TN) for u in rng]
                dinc = [_dot(doc[u], qd[u], TN) for u in rng]
                a = [jnp.where(keep, s, 0.0).astype(BF16) for s in sc]
                da = [jnp.where(keep, s, 0.0).astype(BF16) for s in dsc]
                dqd = [_dot(da[u], ki[u]) for u in rng]
                dki = [_dot(da[u], qd[u], TN) for u in rng]
                dv = [_dot(a[u], doc[u], TN) for u in rng]
                for u in rng:
                    ds_ref[ns[u]] = inc[u] * dec_ref[ns[u]][0:1, :]
                    dsl_ref[ns[u]] = dinc[u]
                    dqd_ref[rows[u], :] = dqd[u]
                    dki_ref[rows[u], :] = dki[u]
                    if dr == 0:
                        dv_ref[rows[u], :] = dv[u]
                    else:
                        dv_ref[rows[u], :] += dv[u]
                return carry

            lax.fori_loop(0, nc // grp, local_step, 0)

            def state_step(s, st, dr=dr):
                n = _chunk_of_step(s, dr, nc, ncc)
                inc = ds_ref[n]
                ds_ref[n] = st
                return st * dec_ref[n][0:1, :] + inc

            lax.fori_loop(0, nc, state_step, jnp.zeros((HEAD, HEAD), F32), unroll=4)

            def dstate_step(s, dst, dr=dr):
                n = _chunk_of_step(nc - 1 - s, dr, nc, ncc)
                inc = dsl_ref[n]
                dsl_ref[n] = dst
                return inc + dst * dec_ref[n][0:1, :]

            lax.fori_loop(0, nc, dstate_step, jnp.zeros((HEAD, HEAD), F32), unroll=4)

            def grad_step(m, carry, dr=dr, cum_t=cum_t):
                ns = [m * grp2 + u for u in range(grp2)]
                rows = [_chunk_rows(n) for n in ns]
                rng = range(grp2)
                st0 = [ds_ref[n] for n in ns]
                dst = [dsl_ref[n] for n in ns]
                dstb = [x.astype(BF16) for x in dst]
                dec = [dec_ref[n][0:1, :] for n in ns]
                doc = [do_ref[r, :] for r in rows]
                vc = [v_ref[r, :].astype(BF16) for r in rows]
                e = [jnp.exp(bc_ref[r, :]) for r in rows]
                einv = [jnp.exp(-bc_ref[r, :]) for r in rows]
                qd = [q_ref[rows[u], :] * e[u] for u in rng]
                ki = [k_ref[rows[u], :] * einv[u] for u in rng]
                kd = [ki[u] * dec[u] for u in rng]
                dqd_st = [_dot(doc[u], st0[u].astype(BF16)) for u in rng]
                dkd = [_dot(vc[u], dstb[u]) for u in rng]
                dv_st = [_dot(kd[u].astype(BF16), dstb[u], NT) for u in rng]
                dqd = [dqd_ref[rows[u], :] + dqd_st[u] for u in rng]
                dki = [dki_ref[r, :] for r in rows]
                dbc = [dqd[u] * qd[u] - dki[u] * ki[u] - dkd[u] * kd[u] for u in rng]
                cs = [_cumsum_chunk(cum_t, x) for x in dbc]
                for u in rng:
                    ddec = jnp.sum(dst[u] * st0[u], axis=0, keepdims=True)
                    dbl = jnp.sum(dkd[u] * kd[u], axis=0, keepdims=True) + ddec * dec[u]
                    dv_ref[rows[u], :] += dv_st[u]
                    dqd_ref[rows[u], :] = cs[u] + dbl
                    dki_ref[rows[u], :] = dki[u] * einv[u] + dkd[u] * (einv[u] * dec[u])
                    if dr == 0:
                        dq_ref[rows[u], :] = dqd[u] * e[u]
                    else:
                        dq_ref[rows[u], :] += dqd[u] * e[u]
                return carry

            lax.fori_loop(0, nc // grp2, grad_step, 0)

            sig, nsig, f = _gates(z_ref[...], lbv)
            common = (dqd_ref[...] / f - dki_ref[...]) * nsig
            dz_ref[dr] = (common * ((1.0 - lbv) * sig)).astype(BF16)
            dlb_ref[dr:dr + 1, :] = jnp.sum(common, axis=0, keepdims=True)

        dz_ref[2] = dv_ref[...].astype(BF16)
        dz_ref[3] = dq_ref[...].astype(BF16)

    sec = lambda s: pl.BlockSpec((None, t, HEAD), lambda h: (s, 0, h))
    col = pl.BlockSpec((seq, HEAD), lambda h: (0, h))
    tf32, tb16 = pltpu.VMEM((t, HEAD), F32), pltpu.VMEM((t, HEAD), BF16)
    states = pltpu.VMEM((nc, HEAD, HEAD), F32)
    return pl.pallas_call(
        body, name="hgrn_bwd", grid=(d // HEAD,),
        in_specs=[sec(0), sec(1), sec(2), sec(3), sec(4),
                  pl.BlockSpec((2, 3, HEAD), lambda h: (0, 0, h)), pl.BlockSpec((1, HEAD), lambda h: (0, h)),
                  col, col],
        out_specs=[pl.BlockSpec((5, t, HEAD), lambda h: (0, 0, h)),
                   pl.BlockSpec((1, HEAD), lambda h: (0, h)), pl.BlockSpec((2, HEAD), lambda h: (0, h))],
        out_shape=[jax.ShapeDtypeStruct((5, t, d), BF16), jax.ShapeDtypeStruct((1, d), F32),
                   jax.ShapeDtypeStruct((2, d), F32)],
        scratch_shapes=[tf32, tf32, tf32, pltpu.VMEM((nc, 8, HEAD), F32), tb16, tb16, tb16,
                        tf32, tf32, tf32, tf32, states, states],
        compiler_params=_cp(),
    )(z1, z1, z1, z1, z1, lbl, onorm, o, dr_out)


def _place():
    x, y, c = lax.axis_index("x"), lax.axis_index("y"), lax.axis_index("c")
    chips = [(1 - x, y), (x, 1 - y), (1 - x, 1 - y)]
    return x, y, c, chips


def allgather_shards(bufs):
    n = len(bufs)

    def body(*refs):
        outs = refs[n:2 * n]
        done_ref, send_sems, recv_sems = refs[2 * n:]
        done_ref[...] = jnp.zeros((8, 128), F32)
        x, y, c, chips = _place()
        p = 2 * x + y
        half = [pl.ds(c * (s.shape[1] // 2), s.shape[1] // 2) for s in bufs]
        other = [pl.ds((1 - c) * (s.shape[1] // 2), s.shape[1] // 2) for s in bufs]

        def remote(i, k, src, dst, to):
            return pltpu.make_async_remote_copy(src_ref=src, dst_ref=dst, send_sem=send_sems.at[6 * i + k],
                                                recv_sem=recv_sems.at[6 * i + k], device_id=to, device_id_type=MESH)

        sends = []
        for i in range(n):
            for j, chip in enumerate(chips):
                mine = outs[i].at[p, half[i]]
                cp = remote(i, j, mine, mine, (*chip, c))
                cp.start()
                sends.append(cp)
        for i in range(n):
            for j, chip in enumerate(chips):
                landed = outs[i].at[2 * chip[0] + chip[1], half[i]]
                remote(i, j, landed, landed, (x, y, c)).wait_recv()
                cp = remote(i, 3 + j, landed, landed, (x, y, 1 - c))
                cp.start()
                sends.append(cp)
        for i in range(n):
            for j, chip in enumerate(chips):
                landed = outs[i].at[2 * chip[0] + chip[1], other[i]]
                remote(i, 3 + j, landed, landed, (x, y, c)).wait_recv()
        for cp in sends:
            cp.wait_send()

    return pl.pallas_call(
        body, name="allgather_shards",
        in_specs=[ANY] * n, out_specs=[ANY] * n + [VMEM],
        out_shape=[jax.ShapeDtypeStruct(s.shape, s.dtype) for s in bufs] + [jax.ShapeDtypeStruct((8, 128), F32)],
        input_output_aliases={i: i for i in range(n)},
        scratch_shapes=[pltpu.SemaphoreType.DMA((6 * n,)), pltpu.SemaphoreType.DMA((6 * n,))],
        compiler_params=pltpu.CompilerParams(has_side_effects=True),
    )(*bufs)


def exchange_halves(grads):
    n = len(grads)

    def body(*refs):
        ins, outs = refs[:n], refs[n:2 * n]
        send_sems, recv_sems = refs[2 * n:]
        x, y, c, _ = _place()
        copies = []
        for i in range(n):
            hr = grads[i].shape[1] // 2
            cp = pltpu.make_async_remote_copy(
                src_ref=ins[i].at[:, pl.ds((1 - c) * hr, hr)], dst_ref=outs[i],
                send_sem=send_sems.at[i], recv_sem=recv_sems.at[i],
                device_id=(x, y, 1 - c), device_id_type=MESH)
            cp.start()
            copies.append(cp)
        for cp in copies:
            cp.wait()

    return pl.pallas_call(
        body, name="exchange_halves",
        in_specs=[ANY] * n, out_specs=[ANY] * n,
        out_shape=[jax.ShapeDtypeStruct((4, g.shape[1] // 2, g.shape[2]), g.dtype) for g in grads],
        scratch_shapes=[pltpu.SemaphoreType.DMA((n,)), pltpu.SemaphoreType.DMA((n,))],
        compiler_params=pltpu.CompilerParams(has_side_effects=True),
    )(*grads)


def pair_sum(grad, got, chip_core):
    _, r, cc = grad.shape
    hr = r // 2
    tr = 256 if hr % 256 == 0 else hr
    nb = hr // tr

    def body(cc_ref, a_ref, b_ref, own_ref, sb_ref):
        s = a_ref[...].astype(F32) + b_ref[...].astype(F32)
        sb_ref[...] = s.astype(BF16)

        @pl.when(pl.program_id(1) == cc_ref[0])
        def _():
            own_ref[...] = s

    grid_spec = pltpu.PrefetchScalarGridSpec(
        num_scalar_prefetch=1, grid=(nb, 4),
        in_specs=[pl.BlockSpec((None, tr, cc), lambda i, qi, cc_ref: (qi, cc_ref[1] * nb + i, 0)),
                  pl.BlockSpec((None, tr, cc), lambda i, qi, cc_ref: (qi, i, 0))],
        out_specs=[pl.BlockSpec((tr, cc), lambda i, qi, cc_ref: (i, 0)),
                   pl.BlockSpec((None, tr, cc), lambda i, qi, cc_ref: (qi, i, 0))])
    return pl.pallas_call(
        body, name="pair_sum", grid_spec=grid_spec,
        out_shape=[jax.ShapeDtypeStruct((hr, cc), F32), jax.ShapeDtypeStruct((4, hr, cc), BF16)],
        compiler_params=_cp(),
    )(chip_core, grad, got)


def scatter_to_owners(parts):
    n = len(parts)

    def body(*refs):
        ins, outs = refs[:n], refs[n:2 * n]
        send_sems, recv_sems = refs[2 * n:]
        x, y, c, chips = _place()
        copies = []
        for i in range(n):
            for j, chip in enumerate(chips):
                cp = pltpu.make_async_remote_copy(
                    src_ref=ins[i].at[2 * chip[0] + chip[1]], dst_ref=outs[i].at[j],
                    send_sem=send_sems.at[3 * i + j], recv_sem=recv_sems.at[3 * i + j],
                    device_id=(*chip, c), device_id_type=MESH)
                cp.start()
                copies.append(cp)
        for cp in copies:
            cp.wait()

    return pl.pallas_call(
        body, name="scatter_to_owners",
        in_specs=[ANY] * n, out_specs=[ANY] * n,
        out_shape=[jax.ShapeDtypeStruct((3,) + p.shape[1:], p.dtype) for p in parts],
        scratch_shapes=[pltpu.SemaphoreType.DMA((3 * n,)), pltpu.SemaphoreType.DMA((3 * n,))],
        compiler_params=pltpu.CompilerParams(has_side_effects=True),
    )(*parts)


def owner_sum(own, got, chip_core):
    hr, cc = own.shape
    tr = 256 if hr % 256 == 0 else hr
    nb = hr // tr

    def body(cc_ref, a_ref, b_ref, o_ref):
        s = a_ref[...] + b_ref[0].astype(F32)
        s = s + b_ref[1].astype(F32)
        o_ref[...] = s + b_ref[2].astype(F32)

    grid_spec = pltpu.PrefetchScalarGridSpec(
        num_scalar_prefetch=1, grid=(nb,),
        in_specs=[pl.BlockSpec((tr, cc), lambda i, cc_ref: (i, 0)),
                  pl.BlockSpec((3, tr, cc), lambda i, cc_ref: (0, i, 0))],
        out_specs=pl.BlockSpec((tr, cc), lambda i, cc_ref: (cc_ref[1] * nb + i, 0)))
    return pl.pallas_call(
        body, name="owner_sum", grid_spec=grid_spec,
        out_shape=jax.ShapeDtypeStruct((2 * hr, cc), F32), compiler_params=_cp(),
    )(chip_core, own, got)


def share_halves(bufs):
    n = len(bufs)

    def body(*refs):
        outs = refs[n:2 * n]
        send_sems, recv_sems = refs[2 * n:]
        x, y, c, _ = _place()
        copies = []
        for i in range(n):
            hr = bufs[i].shape[0] // 2
            mine = outs[i].at[pl.ds(c * hr, hr)]
            cp = pltpu.make_async_remote_copy(
                src_ref=mine, dst_ref=mine, send_sem=send_sems.at[i], recv_sem=recv_sems.at[i],
                device_id=(x, y, 1 - c), device_id_type=MESH)
            cp.start()
            copies.append((cp, outs[i].at[pl.ds((1 - c) * hr, hr)]))
        for i, (cp, theirs) in enumerate(copies):
            cp.wait_send()
            pltpu.make_async_remote_copy(
                src_ref=theirs, dst_ref=theirs, send_sem=send_sems.at[i], recv_sem=recv_sems.at[i],
                device_id=(x, y, c), device_id_type=MESH).wait_recv()

    return pl.pallas_call(
        body, name="share_halves",
        in_specs=[ANY] * n, out_specs=[ANY] * n,
        out_shape=[jax.ShapeDtypeStruct(b.shape, b.dtype) for b in bufs],
        input_output_aliases={i: i for i in range(n)},
        scratch_shapes=[pltpu.SemaphoreType.DMA((n,)), pltpu.SemaphoreType.DMA((n,))],
        compiler_params=pltpu.CompilerParams(has_side_effects=True),
    )(*bufs)


def allgather8(v, name):
    r, n = v.shape

    def body(v_ref, out_ref, send_sems, recv_sems):
        x, y, c, _ = _place()
        me = 4 * x + 2 * y + c
        out_ref[me] = v_ref[...]

        def copy(k, slot, to):
            return pltpu.make_async_remote_copy(
                src_ref=v_ref, dst_ref=out_ref.at[slot], send_sem=send_sems.at[k - 1],
                recv_sem=recv_sems.at[k - 1], device_id=to, device_id_type=MESH)

        peers = []
        for k in range(1, 8):
            px = 1 - x if (k >> 2) & 1 else x
            py = 1 - y if (k >> 1) & 1 else y
            pc = 1 - c if k & 1 else c
            peers.append((px, py, pc))
            copy(k, me, (px, py, pc)).start()
        for k, (px, py, pc) in enumerate(peers, start=1):
            copy(k, 4 * px + 2 * py + pc, (x, y, c)).wait_recv()
        for k, peer in enumerate(peers, start=1):
            copy(k, me, peer).wait_send()

    return pl.pallas_call(
        body, name=name, in_specs=[VMEM], out_specs=VMEM,
        out_shape=jax.ShapeDtypeStruct((8, r, n), v.dtype),
        scratch_shapes=[pltpu.SemaphoreType.DMA((7,)), pltpu.SemaphoreType.DMA((7,))],
        compiler_params=_cp(has_side_effects=True),
    )(v)


HBM = pl.BlockSpec(memory_space=pltpu.HBM)
SEM = pl.BlockSpec(memory_space=pltpu.SEMAPHORE)
DATAFLOW = pltpu.SideEffectType.DATAFLOW_SIDE_EFFECTING


def _descriptors(plan, refs, send_sems, recv_sems, arrivals=True):
    x, y, c, _ = _place()
    sends, recvs = plan(refs)
    out = [pltpu.make_async_remote_copy(src_ref=src, dst_ref=dst, send_sem=send_sems.at[k],
                                        recv_sem=recv_sems.at[k], device_id=to, device_id_type=MESH)
           for k, (src, dst, to) in enumerate(sends)]
    if not arrivals:
        return out, []
    inn = [pltpu.make_async_remote_copy(src_ref=land, dst_ref=land, send_sem=send_sems.at[k],
                                        recv_sem=recv_sems.at[k], device_id=(x, y, c), device_id_type=MESH)
           for k, land in enumerate(recvs)]
    return out, inn


def copies_start(name, arrays, n_copies, plan, after):
    na = len(arrays)

    def body(*refs):
        out, _ = _descriptors(plan, refs[:na], refs[na + 1], refs[na + 2], arrivals=False)
        for cp in out:
            cp.start()
        refs[-1][...] = jnp.zeros((8, 128), F32)

    res = pl.pallas_call(
        body, name=name,
        out_shape=(pltpu.SemaphoreType.DMA((n_copies,)), pltpu.SemaphoreType.DMA((n_copies,)),
                   *[pltpu.HBM(a.shape, a.dtype) for a in arrays], jax.ShapeDtypeStruct((8, 128), F32)),
        in_specs=[HBM] * na + [ANY], out_specs=(SEM, SEM, *[HBM] * na, VMEM),
        input_output_aliases={i: i + 2 for i in range(na)},
        compiler_params=pltpu.CompilerParams(has_side_effects=DATAFLOW),
    )(*[pltpu.with_memory_space_constraint(a, pltpu.HBM) for a in arrays], after)
    return res[0], res[1], list(res[2:2 + na]), res[-1]


def copies_wait(name, started, plan, after):
    send_sems, recv_sems, arrays, _ = started
    na = len(arrays)
    after = list(after) if isinstance(after, (list, tuple)) else [after]

    def body(*refs):
        out, inn = _descriptors(plan, refs[:na], refs[na], refs[na + 1])
        for cp in out:
            cp.wait_send()
        for cp in inn:
            cp.wait_recv()
        refs[-1][...] = jnp.zeros((8, 128), F32)

    res = pl.pallas_call(
        body, name=name,
        out_shape=(*[pltpu.HBM(a.shape, a.dtype) for a in arrays], jax.ShapeDtypeStruct((8, 128), F32)),
        in_specs=[HBM] * na + [SEM, SEM] + [ANY] * len(after), out_specs=(*[HBM] * na, VMEM),
        input_output_aliases={i: i for i in range(na)},
        compiler_params=pltpu.CompilerParams(has_side_effects=DATAFLOW),
    )(*arrays, send_sems, recv_sems, *after)
    return list(res[:na]), res[-1]


def _rows_half(r, c):
    return pl.ds(c * (r // 2), r // 2), pl.ds((1 - c) * (r // 2), r // 2)


def plan_gather_ici(refs):
    x, y, c, chips = _place()
    p = 2 * x + y
    sends, recvs = [], []
    for buf in refs:
        mine, _ = _rows_half(buf.shape[1], c)
        for chip in chips:
            sends.append((buf.at[p, mine], buf.at[p, mine], (*chip, c)))
            recvs.append(buf.at[2 * chip[0] + chip[1], mine])
    return sends, recvs


def plan_gather_d2d(refs):
    x, y, c, chips = _place()
    sends, recvs = [], []
    for buf in refs:
        mine, theirs = _rows_half(buf.shape[1], c)
        for chip in chips:
            slot = 2 * chip[0] + chip[1]
            sends.append((buf.at[slot, mine], buf.at[slot, mine], (x, y, 1 - c)))
            recvs.append(buf.at[slot, theirs])
    return sends, recvs


def plan_exchange(refs):
    x, y, c, _ = _place()
    n = len(refs) // 2
    sends, recvs = [], []
    for grad, land in zip(refs[:n], refs[n:]):
        _, theirs = _rows_half(grad.shape[1], c)
        sends.append((grad.at[:, theirs], land, (x, y, 1 - c)))
        recvs.append(land)
    return sends, recvs


def plan_scatter(refs):
    x, y, c, chips = _place()
    n = len(refs) // 2
    sends, recvs = [], []
    for part, land in zip(refs[:n], refs[n:]):
        for j, chip in enumerate(chips):
            sends.append((part.at[2 * chip[0] + chip[1]], land.at[j], (*chip, c)))
            recvs.append(land.at[j])
    return sends, recvs


def plan_share(refs):
    x, y, c, _ = _place()
    sends, recvs = [], []
    for buf in refs:
        mine, theirs = _rows_half(buf.shape[0], c)
        sends.append((buf.at[mine], buf.at[mine], (x, y, 1 - c)))
        recvs.append(buf.at[theirs])
    return sends, recvs


def put_in_slot(w, chip, dtype, name):
    r, c = w.shape
    tr = 256 if r % 256 == 0 else r

    def body(chip_ref, w_ref, o_ref):
        o_ref[...] = w_ref[...].astype(dtype)

    grid_spec = pltpu.PrefetchScalarGridSpec(
        num_scalar_prefetch=1, grid=(r // tr,),
        in_specs=[pl.BlockSpec((tr, c), lambda i, chip_ref: (i, 0))],
        out_specs=pl.BlockSpec((None, tr, c), lambda i, chip_ref: (chip_ref[0], i, 0)))
    return pl.pallas_call(body, name=name, grid_spec=grid_spec,
                          out_shape=jax.ShapeDtypeStruct((4, r, c), dtype), compiler_params=_cp())(chip, w)


def ada_fwd(s_in, ada_w, ada_b, tn):
    nl, d, ws = ada_w.shape

    def body(s_ref, w_ref, b_ref, so_ref, mod_ref):
        s = _silu(s_ref[...])
        so_ref[...] = s
        mod_ref[...] = _dot(s.astype(BF16), w_ref[...].astype(BF16)) + b_ref[...]

    return pl.pallas_call(
        body, name="ada_fwd", grid=(nl, ws // tn),
        in_specs=[pl.BlockSpec((16, d), lambda l, j: (0, 0)),
                  pl.BlockSpec((None, d, tn), lambda l, j: (l, 0, j)),
                  pl.BlockSpec((None, 1, tn), lambda l, j: (l, 0, j))],
        out_specs=[pl.BlockSpec((16, d), lambda l, j: (0, 0)),
                   pl.BlockSpec((None, 16, tn), lambda l, j: (l, 0, j))],
        out_shape=[jax.ShapeDtypeStruct((16, d), F32), jax.ShapeDtypeStruct((nl, 16, ws), F32)],
        compiler_params=_cp(),
    )(s_in, ada_w, ada_b)


def _adamw_math(w, g, m, v):
    m = ADAM_B1 * m + (1.0 - ADAM_B1) * g
    v = ADAM_B2 * v + (1.0 - ADAM_B2) * (g * g)
    m_hat = m / (1.0 - ADAM_B1 ** ADAM_STEP)
    v_hat = v / (1.0 - ADAM_B2 ** ADAM_STEP)
    delta = -ADAM_LR * (m_hat / (jnp.sqrt(v_hat) + ADAM_EPS) + ADAM_WD * w)
    return delta, m, v


def ada_bwd_adamw(s, dm, w, m, v):
    nl, d, ws = w.shape
    tr = 256 if d % 256 == 0 else 128

    def body(s_ref, dm_ref, w_ref, m_ref, v_ref, g_ref, dl_ref, mo_ref, vo_ref, dc_ref):
        dmv = dm_ref[...].astype(BF16)
        wv = w_ref[...]
        g = _dot(s_ref[...].astype(BF16), dmv, TN)
        g_ref[...] = g
        dl_ref[...], mo_ref[...], vo_ref[...] = _adamw_math(wv, g, m_ref[...], v_ref[...])
        dc_ref[...] = _dot(dmv[8:16, :], wv.astype(BF16), NT)

    wblk = pl.BlockSpec((None, tr, ws), lambda l, i: (l, i, 0))
    wshape = jax.ShapeDtypeStruct((nl, d, ws), F32)
    return pl.pallas_call(
        body, name="ada_bwd_adamw", grid=(nl, d // tr),
        in_specs=[pl.BlockSpec((16, tr), lambda l, i: (0, i)),
                  pl.BlockSpec((None, 16, ws), lambda l, i: (l, 0, 0)), wblk, wblk, wblk],
        out_specs=[wblk, wblk, wblk, wblk, pl.BlockSpec((None, 8, tr), lambda l, i: (l, 0, i))],
        out_shape=[wshape, wshape, wshape, wshape, jax.ShapeDtypeStruct((nl, 8, d), F32)],
        compiler_params=_cp(),
    )(s, dm, w, m, v)


def adamw(w, g, m, v, name):
    r, c = w.shape
    tr = 256 if r % 256 == 0 else r

    def body(w_ref, g_ref, m_ref, v_ref, dl_ref, mo_ref, vo_ref):
        dl_ref[...], mo_ref[...], vo_ref[...] = _adamw_math(w_ref[...], g_ref[...], m_ref[...], v_ref[...])

    blk = pl.BlockSpec((tr, c), lambda i: (i, 0))
    shape = jax.ShapeDtypeStruct((r, c), F32)
    return pl.pallas_call(body, name=name, grid=(r // tr,), in_specs=[blk] * 4, out_specs=[blk] * 3,
                          out_shape=[shape] * 3, compiler_params=_cp())(w, g, m, v)


SMALL_ROWS = 24
ROW_MOD = 10


def small_reduce(gathered):
    _, rows, d = gathered.shape

    def body(g_ref, o_ref):
        tot = g_ref[0]
        for b in range(1, 8):
            tot = tot + g_ref[b]
        o_ref[0:rows, :] = tot
        for layer in range(2):
            lat = ROW_MOD + 6 * layer
            o_ref[24 + 3 * layer:27 + 3 * layer, :] = tot[lat:lat + 3, :] + tot[lat + 3:lat + 6, :]
        o_ref[30:32, :] = jnp.zeros((2, d), F32)

    return pl.pallas_call(body, name="small_reduce", in_specs=[VMEM], out_specs=VMEM,
                          out_shape=jax.ShapeDtypeStruct((32, d), F32), compiler_params=_cp())(gathered)


def lb_logits_grad(lbl, dlb):
    _, _, n = lbl.shape

    def body(l_ref, d_ref, o_ref):
        for dr in range(2):
            _, (p0, p1, p2) = _lower_bound(l_ref, dr)
            dv = d_ref[dr:dr + 1, :]
            o_ref[dr, 0:1, :] = p0 * p2 * dv
            o_ref[dr, 1:2, :] = p1 * p2 * dv
            o_ref[dr, 2:3, :] = -p2 * (p0 + p1) * dv

    return pl.pallas_call(body, name="lb_logits_grad", in_specs=[VMEM, VMEM], out_specs=VMEM,
                          out_shape=jax.ShapeDtypeStruct((2, 3, n), F32), compiler_params=_cp())(lbl, dlb)


def c_ctx_grad(parts, c_ctx):
    d = c_ctx.shape[1]

    def body(p_ref, c_ref, o_ref):
        tot = p_ref[0, 0:1, :]
        for chip in range(1, 4):
            tot = tot + p_ref[2 * chip, 0:1, :]
        o_ref[...] = tot * _dsilu(c_ref[...])

    return pl.pallas_call(body, name="c_ctx_grad", in_specs=[VMEM, VMEM], out_specs=VMEM,
                          out_shape=jax.ShapeDtypeStruct((1, d), F32), compiler_params=_cp())(parts, c_ctx)


def _reduce_scatter(grads, core, chip_core):
    got = exchange_halves(grads)
    sums = [pair_sum(g, r, core) for g, r in zip(grads, got)]
    recv = scatter_to_owners([sb for _, sb in sums])
    reduced = [owner_sum(s, r, chip_core) for (s, _), r in zip(sums, recv)]
    return share_halves(reduced)


def kernel(x, c, ctx, c_ctx, ada_w, ada_b, pre_g, post_g, ev_w_in, ev_pool_w, ev_pool_scale, ev_conv_w, ev_conv_b, ev_w_out, od_w_in, od_onorm_g, od_w_out, lb_logits, loss_target, m_c_ctx, m_ada_w, m_ada_b, m_pre_g, m_post_g, m_ev_w_in, m_ev_pool_w, m_ev_pool_scale, m_ev_conv_w, m_ev_conv_b, m_ev_w_out, m_od_w_in, m_od_onorm_g, m_od_w_out, m_lb_logits, v_c_ctx, v_ada_w, v_ada_b, v_pre_g, v_post_g, v_ev_w_in, v_ev_pool_w, v_ev_pool_scale, v_ev_conv_w, v_ev_conv_b, v_ev_w_out, v_od_w_in, v_od_onorm_g, v_od_w_out, v_lb_logits):
    _, seq, d = x.shape
    cx = ctx.shape[1]
    t = cx + seq
    half_d = d // 2
    g = half_d // N_POOL
    tn = d // 4
    xi, yi, ci = lax.axis_index("x"), lax.axis_index("y"), lax.axis_index("c")
    chip = 2 * xi + yi
    me = 2 * chip + ci
    core_arr = jnp.reshape(ci, (1,)).astype(jnp.int32)
    chip_arr = jnp.reshape(chip, (1,)).astype(jnp.int32)
    chip_core_arr = jnp.stack([chip, ci]).astype(jnp.int32)

    pad = lambda a, rows: jnp.concatenate([a, jnp.zeros((rows - a.shape[0], g), F32)], axis=0)
    small = jnp.concatenate([
        ev_pool_w.reshape(g, g), pad(ev_conv_w.reshape(3, g), 8), pad(od_onorm_g.reshape(2, g), 8),
        pad(lb_logits.reshape(12, g), 16)], axis=0)
    ev_in_g, ev_out_g, small_g, ev_done = allgather_shards([
        put_in_slot(ev_w_in[0], chip_arr, BF16, "cast_ev_w_in"),
        put_in_slot(ev_w_out[0], chip_arr, BF16, "cast_ev_w_out"),
        put_in_slot(small, chip_arr, F32, "place_small")])
    ev_out3 = ev_out_g.reshape(1, d, d)
    pool_w_full = small_g[:, :g].reshape(4, N_POOL, g // 4, g).transpose(1, 0, 2, 3).reshape(N_POOL, g, g)
    conv_w_full = small_g[:, g:g + 3].transpose(1, 0, 2).reshape(3, half_d)
    onorm_full = small_g[:, g + 8:g + 10].reshape(1, d)
    lbl_full = small_g[:, g + 16:g + 28].reshape(4, 2, 3, 2 * g).transpose(1, 2, 0, 3).reshape(2, 3, d)

    c_rows = jnp.concatenate([c + ev_done[0:1, 0:1], jnp.zeros((7, d), F32)], axis=0)
    c_all = allgather8(c_rows, "allgather_c")[:, 0, :]
    s_in = jnp.concatenate([c_all, c_ctx.reshape(1, d), jnp.zeros((7, d), F32)], axis=0)
    ws_ada = ada_w.shape[2]
    ada_b_mine = lax.dynamic_slice(ada_b, (0, chip * ws_ada), (2, ws_ada)).reshape(2, 1, ws_ada)
    s_act, mod_mine = ada_fwd(s_in, ada_w, ada_b_mine, tn)
    mod_all = allgather8(mod_mine.reshape(32, ws_ada), "allgather_mod")
    od_ici = copies_start("gather_od_ici_start", [
        put_in_slot(od_w_in[0], chip_arr, BF16, "cast_od_w_in"),
        put_in_slot(od_w_out[0], chip_arr, BF16, "cast_od_w_out")], 6, plan_gather_ici, mod_all)
    mod_full = mod_all[0::2].reshape(4, 2, 16, ws_ada).transpose(1, 2, 0, 3).reshape(2, 16, 3 * d)
    mod_lat = lax.dynamic_slice(mod_full, (0, me, 0), (2, 1, 3 * d))
    mods = jnp.concatenate([mod_full[:, 8:9], mod_lat], axis=1)
    shift, scale, gate = mods[:, :, :d], mods[:, :, d:2 * d], mods[:, :, 2 * d:]

    xs = jnp.concatenate([ctx[0], x[0]], axis=0)

    h0 = normmod_fwd(xs, pre_g[0:1] + od_ici[3][0:1, 0:1], shift[0], scale[0], cx)
    z0 = mm_nn(h0, ev_in_g, half_d, tn, "mm_ev_in")
    u_a = mix_a_fwd(z0, pool_w_full, ev_pool_scale, cx)
    u_b = mix_b_fwd(z0, conv_w_full, ev_conv_b, cx)
    u = jnp.concatenate([u_a, u_b], axis=1)
    y0 = mm_nn(u, ev_out3, d, tn, "mm_ev_out")[0]
    xs1 = post_fwd(xs, y0, post_g[0:1], gate[0], cx)
    od_d2d = copies_start("gather_od_d2d_start",
                          copies_wait("gather_od_ici_wait", od_ici, plan_gather_ici, xs1)[0],
                          6, plan_gather_d2d, xs1)
    (od_in_g, od_out_g), _ = copies_wait("gather_od_d2d_wait", od_d2d, plan_gather_d2d, od_d2d[3])
    od_out3 = od_out_g.reshape(1, d, d)

    h1 = normmod_fwd(xs1, pre_g[1:2], shift[1], scale[1], cx)
    z1 = mm_nn(h1, od_in_g, d, tn, "mm_od_in")
    o1, r1 = hgrn_fwd(z1, lbl_full, onorm_full, cx)
    y1 = mm_nn(r1, od_out3, d, tn, "mm_od_out")[0]
    sq, dx2 = post_loss(xs1, y1, post_g[1:2], gate[1], loss_target[0], cx)
    loss = lax.psum(sq[0, 0] * (0.5 / d), ("x", "y", "c"))

    dy1, dgate1, dpost1 = post_bwd(dx2, y1, post_g[1:2], gate[1], cx, True)
    dr1 = mm_nt(dy1[None], None, od_out3, tn, "mm_od_out_dx")
    g_od_out = mm_tn(r1, dy1[None], None, d, tn, "mm_od_out_dw")
    dz1, donorm, dlb = hgrn_bwd(z1, lbl_full, onorm_full, o1, dr1, cx)
    dh1 = mm_nt(dz1, None, od_in_g, tn, "mm_od_in_dx")
    g_od_in = mm_tn(h1, dz1, None, od_in_g.shape[2], tn, "mm_od_in_dw")
    dxs1, dpre1, dshift1, dscale1 = normmod_bwd(xs1, dh1, pre_g[1:2], scale[1], dx2, cx, True)

    od_grads = [g_od_in, g_od_out.reshape(4, d // 4, d)]
    half_zone = lambda a, lead, dt: lax.empty((lead, a.shape[1] // 2, a.shape[2]), dt)
    od_ex = copies_start("reduce_od_exchange_start", od_grads + [half_zone(a, 4, a.dtype) for a in od_grads],
                         2, plan_exchange, dxs1)

    dy0, dgate0, dpost0 = post_bwd(dxs1, y0, post_g[0:1] + od_ex[3][0:1, 0:1], gate[0], cx, False)
    du = mm_nt(dy0[None], None, ev_out3, tn, "mm_ev_out_dx")
    g_ev_out = mm_tn(u, dy0[None], None, d, tn, "mm_ev_out_dw")
    od_got, _ = copies_wait("reduce_od_exchange_wait", od_ex, plan_exchange, g_ev_out)
    od_sums = [pair_sum(od_got[i], od_got[2 + i], chip_core_arr) for i in range(2)]
    od_sc = copies_start("reduce_od_scatter_start",
                         [sb for _, sb in od_sums] + [half_zone(a, 3, BF16) for a in od_grads],
                         6, plan_scatter, du)
    dz0a, g_pool_w, dpool_scale = mix_a_bwd(z0, du, pool_w_full, ev_pool_scale + od_sc[3][0:1, 0:1], cx)
    dz0b, dconv_w, dconv_b = mix_b_bwd(z0, du, conv_w_full, ev_conv_b + od_sc[3][0:1, 0:1], cx)
    g_ev_in = mm_tn(h0, dz0a, dz0b, ev_in_g.shape[2], tn, "mm_ev_in_dw")
    ev_grads = [g_ev_in, g_ev_out.reshape(4, d // 4, d), g_pool_w.reshape(4, g, g)]
    ev_ex = copies_start("reduce_ev_exchange_start", ev_grads + [half_zone(a, 4, a.dtype) for a in ev_grads],
                         3, plan_exchange, dpool_scale)
    dh0 = mm_nt(dz0a, dz0b, ev_in_g, tn, "mm_ev_in_dx")
    dxs0, dpre0, dshift0, dscale0 = normmod_bwd(xs, dh0, pre_g[0:1] + ev_ex[3][0:1, 0:1], scale[0], dxs1,
                                                cx, False, True)
    grad_x = dxs0[None]
    ev_got, _ = copies_wait("reduce_ev_exchange_wait", ev_ex, plan_exchange, dxs0)
    ev_sums = [pair_sum(ev_got[i], ev_got[3 + i], chip_core_arr) for i in range(3)]
    od_recv, _ = copies_wait("reduce_od_scatter_wait", od_sc, plan_scatter, dxs0)

    zrow = jnp.zeros((1, d), F32)
    small_rows = jnp.concatenate([
        dpre0, dpre1, dpost0, dpost1,
        jnp.concatenate([dpool_scale, dconv_b], axis=1),
        jnp.concatenate([dconv_w.reshape(1, 3 * half_d), jnp.zeros((1, half_d), F32)], axis=1).reshape(2, d),
        donorm, dlb,
        dshift0[1:2], dscale0[1:2], dgate0[1:2], dshift0[0:1], dscale0[0:1], dgate0[0:1],
        dshift1[1:2], dscale1[1:2], dgate1[1:2], dshift1[0:1], dscale1[0:1], zrow,
        zrow, zrow], axis=0)
    small_all = allgather8(small_rows, "allgather_small")
    ev_sc = copies_start("reduce_ev_scatter_start",
                         [sb for _, sb in ev_sums] + [half_zone(a, 3, BF16) for a in ev_grads],
                         9, plan_scatter, small_all)
    od_sh = copies_start("reduce_od_share_start",
                         [owner_sum(od_sums[i][0], od_recv[2 + i], chip_core_arr) for i in range(2)],
                         2, plan_share, ev_sc[3])
    tot = small_reduce(small_all + ev_sc[3][0:1, 0:1])

    dm_rows = []
    for layer in range(2):
        lat = ROW_MOD + 6 * layer
        dm_lat = small_all[:, lat:lat + 3].reshape(8, 3 * d)
        dm_ctx = tot[lat + 3:lat + 6].reshape(1, 3 * d)
        dm_rows.append(jnp.concatenate([dm_lat, dm_ctx, jnp.zeros((7, 3 * d), F32)], axis=0))
    dm_full = jnp.stack(dm_rows)
    dm_mine = lax.dynamic_slice(dm_full, (0, 0, chip * ws_ada), (2, 16, ws_ada))

    def step(w, gr, m, v, name):
        shape = w.shape
        cols = shape[-1]
        two_d = lambda a: a.reshape(-1, cols)
        dl, mo, vo = adamw(two_d(w), two_d(gr), two_d(m), two_d(v), "adamw_" + name)
        return dl.reshape(shape), mo.reshape(shape), vo.reshape(shape)

    grad_ada_b = tot[24:30].reshape(2, 3 * d)
    grad_pre_g = tot[0:2]
    grad_post_g = tot[2:4]
    grad_ev_pool_scale = tot[4:5, :half_d]
    grad_ev_conv_b = tot[4:5, half_d:]
    conv_w_tot = tot[5:7].reshape(1, 2 * d)[:, :3 * half_d].reshape(3, N_POOL, g)
    grad_ev_conv_w = lax.dynamic_slice(conv_w_tot, (0, chip, 0), (3, 1, g)).reshape(1, 3, g)
    grad_od_onorm_g = lax.dynamic_slice(tot[7:8], (0, chip * 2 * g), (1, 2 * g))
    dlb_mine = lax.dynamic_slice(tot[8:10], (0, chip * 2 * g), (2, 2 * g))
    grad_lb_logits = lb_logits_grad(lb_logits, dlb_mine)
    upd = {
        "ada_b": step(ada_b, grad_ada_b, m_ada_b, v_ada_b, "ada_b"),
        "pre_g": step(pre_g, grad_pre_g, m_pre_g, v_pre_g, "pre_g"),
        "post_g": step(post_g, grad_post_g, m_post_g, v_post_g, "post_g"),
        "ev_pool_scale": step(ev_pool_scale, grad_ev_pool_scale, m_ev_pool_scale, v_ev_pool_scale, "ev_pool_scale"),
        "ev_conv_w": step(ev_conv_w, grad_ev_conv_w, m_ev_conv_w, v_ev_conv_w, "ev_conv_w"),
        "ev_conv_b": step(ev_conv_b, grad_ev_conv_b, m_ev_conv_b, v_ev_conv_b, "ev_conv_b"),
        "od_onorm_g": step(od_onorm_g, grad_od_onorm_g, m_od_onorm_g, v_od_onorm_g, "od_onorm_g"),
        "lb_logits": step(lb_logits, grad_lb_logits, m_lb_logits, v_lb_logits, "lb_logits"),
    }
    grad_ada_w, delta_ada_w, new_m_ada_w, new_v_ada_w, dctx_part = ada_bwd_adamw(
        s_act, dm_mine, ada_w, m_ada_w, v_ada_w)
    upd["ada_w"] = (delta_ada_w, new_m_ada_w, new_v_ada_w)
    (grad_od_w_in, grad_od_w_out), _ = copies_wait("reduce_od_share_wait", od_sh, plan_share, ev_sc[3])
    grad_od_w_in, grad_od_w_out = grad_od_w_in[None], grad_od_w_out[None]
    upd["od_w_in"] = step(od_w_in, grad_od_w_in, m_od_w_in, v_od_w_in, "od_w_in")
    upd["od_w_out"] = step(od_w_out, grad_od_w_out, m_od_w_out, v_od_w_out, "od_w_out")
    done_behind = [dctx_part] + [upd[k][0] for k in (
        "od_w_in", "od_w_out", "ada_b", "pre_g", "post_g", "ev_pool_scale", "ev_conv_w", "ev_conv_b",
        "od_onorm_g", "lb_logits")]
    ev_recv, ev_landed = copies_wait("reduce_ev_scatter_wait", ev_sc, plan_scatter, done_behind)
    grad_ev_w_in, grad_ev_w_out, grad_pool_w = share_halves(
        [owner_sum(ev_sums[i][0], ev_recv[3 + i], chip_core_arr) for i in range(3)])
    dctx_all = allgather8(dctx_part[0] + dctx_part[1] + ev_landed[0:1, 0:1], "allgather_dctx")
    grad_c_ctx = c_ctx_grad(dctx_all, c_ctx.reshape(1, d)).reshape(d)
    grad_ev_w_in, grad_ev_w_out = grad_ev_w_in[None], grad_ev_w_out[None]
    grad_ev_pool_w = grad_pool_w.reshape(1, N_POOL, g // 4, g)
    upd["c_ctx"] = step(c_ctx, grad_c_ctx, m_c_ctx, v_c_ctx, "c_ctx")
    upd["ev_w_in"] = step(ev_w_in, grad_ev_w_in, m_ev_w_in, v_ev_w_in, "ev_w_in")
    upd["ev_pool_w"] = step(ev_pool_w, grad_ev_pool_w, m_ev_pool_w, v_ev_pool_w, "ev_pool_w")
    upd["ev_w_out"] = step(ev_w_out, grad_ev_w_out, m_ev_w_out, v_ev_w_out, "ev_w_out")
    names = ["c_ctx", "ada_w", "ada_b", "pre_g", "post_g", "ev_w_in", "ev_pool_w", "ev_pool_scale",
             "ev_conv_w", "ev_conv_b", "ev_w_out", "od_w_in", "od_onorm_g", "od_w_out", "lb_logits"]
    grads = [grad_c_ctx, grad_ada_w, grad_ada_b, grad_pre_g, grad_post_g, grad_ev_w_in, grad_ev_pool_w,
             grad_ev_pool_scale, grad_ev_conv_w, grad_ev_conv_b, grad_ev_w_out, grad_od_w_in,
             grad_od_onorm_g, grad_od_w_out, grad_lb_logits]
    return (loss, grad_x, *grads, *[upd[k][0] for k in names], *[upd[k][1] for k in names],
            *[upd[k][2] for k in names])
```

```python
import functools

import jax
import jax.numpy as jnp
from jax import lax
from jax.experimental import pallas as pl
from jax.experimental.pallas import tpu as pltpu

EPS = 1e-6
GRID_W_LOG2 = 6
CHUNK = 64
HEAD = 128
N_POOL = 4
ADAM_LR, ADAM_B1, ADAM_B2, ADAM_EPS, ADAM_WD, ADAM_STEP = 0.001, 0.9, 0.999, 1e-08, 0.01, 10
VMEM_LIMIT = 56 * 1024 * 1024
MESH = pl.DeviceIdType.MESH
F32, BF16 = jnp.float32, jnp.bfloat16
ANY = pl.BlockSpec(memory_space=pl.ANY)
VMEM = pl.BlockSpec(memory_space=pltpu.VMEM)


def _cp(**kw):
    return pltpu.CompilerParams(vmem_limit_bytes=VMEM_LIMIT, **kw)


def _silu(x):
    return x * jax.nn.sigmoid(x)


def _dsilu(x):
    s = jax.nn.sigmoid(x)
    return s * (1.0 + x * (1.0 - s))


def _dot(a, b, dims=((1,), (0,)), precision=None):
    return lax.dot_general(a, b, (dims, ((), ())), preferred_element_type=F32, precision=precision)


NN = ((1,), (0,))
NT = ((1,), (1,))
TN = ((0,), (0,))


def _row_block(cx):
    return 256 if cx % 256 == 0 else 128


def normmod_fwd(xs, g, shift, scale, cx):
    t, d = xs.shape
    tm = _row_block(cx)
    nctx = cx // tm

    def body(x_ref, g_ref, sh_ref, sc_ref, h_ref):
        is_ctx = pl.program_id(0) < nctx
        x = x_ref[...]
        rstd = lax.rsqrt(jnp.mean(x * x, axis=-1, keepdims=True) + EPS)
        sc = jnp.where(is_ctx, sc_ref[0:1, :], sc_ref[1:2, :])
        sh = jnp.where(is_ctx, sh_ref[0:1, :], sh_ref[1:2, :])
        h_ref[...] = ((x * rstd) * g_ref[...] * (1.0 + sc) + sh).astype(BF16)

    row = pl.BlockSpec((tm, d), lambda i: (i, 0))
    vec = lambda r: pl.BlockSpec((r, d), lambda i: (0, 0))
    return pl.pallas_call(
        body, name="normmod_fwd", grid=(t // tm,),
        in_specs=[row, vec(1), vec(2), vec(2)], out_specs=row,
        out_shape=jax.ShapeDtypeStruct((t, d), BF16), compiler_params=_cp(),
    )(xs, g, shift, scale)


def normmod_bwd(xs, dh, g, scale, dres, cx, res_is_latent_only, dx_latent_only=False):
    t, d = xs.shape
    tm = _row_block(cx)
    nctx = cx // tm

    def body(x_ref, dh_ref, g_ref, sc_ref, dres_ref, dx_ref, dg_ref, dsh_ref, dsc_ref):
        i = pl.program_id(0)
        is_ctx = i < nctx

        @pl.when(i == 0)
        def _():
            dg_ref[...] = jnp.zeros_like(dg_ref)
            dsh_ref[...] = jnp.zeros_like(dsh_ref)
            dsc_ref[...] = jnp.zeros_like(dsc_ref)

        x = x_ref[...]
        dh = dh_ref[...]
        gv = g_ref[...]
        rstd = lax.rsqrt(jnp.mean(x * x, axis=-1, keepdims=True) + EPS)
        xhat = x * rstd
        sc = jnp.where(is_ctx, sc_ref[0:1, :], sc_ref[1:2, :])
        dsh = jnp.sum(dh, axis=0, keepdims=True)
        dhx = dh * xhat
        dsc = jnp.sum(dhx * gv, axis=0, keepdims=True)
        dg_ref[...] += jnp.sum(dhx * (1.0 + sc), axis=0, keepdims=True)
        zero = jnp.zeros_like(dsh)
        dsh_ref[0:1, :] += jnp.where(is_ctx, dsh, zero)
        dsh_ref[1:2, :] += jnp.where(is_ctx, zero, dsh)
        dsc_ref[0:1, :] += jnp.where(is_ctx, dsc, zero)
        dsc_ref[1:2, :] += jnp.where(is_ctx, zero, dsc)
        dxhat = dh * (gv * (1.0 + sc))
        dx = rstd * (dxhat - xhat * jnp.mean(dxhat * xhat, axis=-1, keepdims=True))
        res = dres_ref[...]
        if res_is_latent_only:
            res = jnp.where(is_ctx, jnp.zeros_like(res), res)
        dx_ref[...] = dx + res

    row = pl.BlockSpec((tm, d), lambda i: (i, 0))
    if res_is_latent_only:
        res_spec = pl.BlockSpec((tm, d), lambda i: (jnp.maximum(i - nctx, 0), 0))
    else:
        res_spec = row
    vec = lambda r: pl.BlockSpec((r, d), lambda i: (0, 0))
    dx_spec = pl.BlockSpec((tm, d), lambda i: (jnp.maximum(i - nctx, 0), 0)) if dx_latent_only else row
    return pl.pallas_call(
        body, name="normmod_bwd", grid=(t // tm,),
        in_specs=[row, row, vec(1), vec(2), res_spec],
        out_specs=[dx_spec, vec(1), vec(2), vec(2)],
        out_shape=[jax.ShapeDtypeStruct((t - cx if dx_latent_only else t, d), F32), jax.ShapeDtypeStruct((1, d), F32),
                   jax.ShapeDtypeStruct((2, d), F32), jax.ShapeDtypeStruct((2, d), F32)],
        compiler_params=_cp(),
    )(xs, dh, g, scale, dres)


def post_fwd(xs, y, pg, gate, cx):
    t, d = xs.shape
    tm = _row_block(cx)
    nctx = cx // tm

    def body(x_ref, y_ref, pg_ref, gate_ref, o_ref):
        is_ctx = pl.program_id(0) < nctx
        y = y_ref[...]
        rstd = lax.rsqrt(jnp.mean(y * y, axis=-1, keepdims=True) + EPS)
        gt = jnp.where(is_ctx, gate_ref[0:1, :], gate_ref[1:2, :])
        o_ref[...] = x_ref[...] + gt * ((y * rstd) * pg_ref[...])

    row = pl.BlockSpec((tm, d), lambda i: (i, 0))
    vec = lambda r: pl.BlockSpec((r, d), lambda i: (0, 0))
    return pl.pallas_call(
        body, name="post_fwd", grid=(t // tm,),
        in_specs=[row, row, vec(1), vec(2)], out_specs=row,
        out_shape=jax.ShapeDtypeStruct((t, d), F32), compiler_params=_cp(),
    )(xs, y, pg, gate)


def post_loss(xs, y, pg, gate, target, cx):
    t, d = xs.shape
    n = y.shape[0]
    tm = _row_block(cx)
    nctx = cx // tm

    def body(x_ref, y_ref, pg_ref, gate_ref, tgt_ref, sq_ref, dx_ref):
        @pl.when(pl.program_id(0) == 0)
        def _():
            sq_ref[...] = jnp.zeros_like(sq_ref)

        y = y_ref[...]
        rstd = lax.rsqrt(jnp.mean(y * y, axis=-1, keepdims=True) + EPS)
        x2 = x_ref[...] + gate_ref[1:2, :] * ((y * rstd) * pg_ref[...])
        err = x2 - tgt_ref[...]
        sq_ref[...] += jnp.sum(err * err)
        dx_ref[...] = err * (1.0 / d)

    row = pl.BlockSpec((tm, d), lambda i: (i, 0))
    xrow = pl.BlockSpec((tm, d), lambda i: (i + nctx, 0))
    vec = lambda r: pl.BlockSpec((r, d), lambda i: (0, 0))
    return pl.pallas_call(
        body, name="post_loss", grid=(n // tm,),
        in_specs=[xrow, row, vec(1), vec(2), row],
        out_specs=[pl.BlockSpec((8, 128), lambda i: (0, 0)), row],
        out_shape=[jax.ShapeDtypeStruct((8, 128), F32), jax.ShapeDtypeStruct((n, d), F32)],
        compiler_params=_cp(),
    )(xs, y, pg, gate, target)


def post_bwd(dxo, y, pg, gate, cx, latent_only):
    m, d = y.shape
    tm = _row_block(cx)
    nctx = 0 if latent_only else cx // tm

    def body(dx_ref, y_ref, pg_ref, gate_ref, dy_ref, dgate_ref, dpg_ref):
        i = pl.program_id(0)
        is_ctx = i < nctx

        @pl.when(i == 0)
        def _():
            dgate_ref[...] = jnp.zeros_like(dgate_ref)
            dpg_ref[...] = jnp.zeros_like(dpg_ref)

        y = y_ref[...]
        dx = dx_ref[...]
        pgv = pg_ref[...]
        rstd = lax.rsqrt(jnp.mean(y * y, axis=-1, keepdims=True) + EPS)
        yhat = y * rstd
        gt = jnp.where(is_ctx, gate_ref[0:1, :], gate_ref[1:2, :])
        dxy = dx * yhat
        dgt = jnp.sum(dxy * pgv, axis=0, keepdims=True)
        zero = jnp.zeros_like(dgt)
        dgate_ref[0:1, :] += jnp.where(is_ctx, dgt, zero)
        dgate_ref[1:2, :] += jnp.where(is_ctx, zero, dgt)
        dpg_ref[...] += jnp.sum(dxy * gt, axis=0, keepdims=True)
        dyhat = dx * (gt * pgv)
        dy = rstd * (dyhat - yhat * jnp.mean(dyhat * yhat, axis=-1, keepdims=True))
        dy_ref[...] = dy.astype(BF16)

    row = pl.BlockSpec((tm, d), lambda i: (i, 0))
    vec = lambda r: pl.BlockSpec((r, d), lambda i: (0, 0))
    return pl.pallas_call(
        body, name="post_bwd", grid=(m // tm,),
        in_specs=[row, row, vec(1), vec(2)], out_specs=[row, vec(2), vec(1)],
        out_shape=[jax.ShapeDtypeStruct((m, d), BF16), jax.ShapeDtypeStruct((2, d), F32),
                   jax.ShapeDtypeStruct((1, d), F32)],
        compiler_params=_cp(),
    )(dxo, y, pg, gate)


def _split_rows(m):
    for cand in (1024, 768, 512, 384, 256, 128):
        if m % cand == 0 and m // cand >= 2:
            return cand
    return m


def mm_nn(a, w3, sec, tn, name):
    m, k = a.shape
    q, _, ws = w3.shape
    n = q * ws
    tpq, tps = ws // tn, sec // tn
    tm = next(c for c in (768, 512, 256, 128) if m % c == 0)

    def body(a_ref, w_ref, o_ref):
        w = w_ref[...]

        def step(i, carry):
            rows = pl.ds(pl.multiple_of(i * tm, tm), tm)
            o_ref[rows, :] = _dot(a_ref[rows, :], w)
            return carry

        lax.fori_loop(0, m // tm, step, 0)

    return pl.pallas_call(
        body, name=name, grid=(n // tn,),
        in_specs=[pl.BlockSpec((m, k), lambda j: (0, 0)),
                  pl.BlockSpec((None, k, tn), lambda j: (j // tpq, 0, j % tpq))],
        out_specs=pl.BlockSpec((None, m, tn), lambda j: (j // tps, 0, j % tps)),
        out_shape=jax.ShapeDtypeStruct((n // sec, m, sec), F32), compiler_params=_cp(),
    )(a, w3)


def _two_stacks(a3, b3, tn):
    sec = a3.shape[2]
    tps = sec // tn
    n1 = a3.shape[0] * tps
    first = lambda j: (jnp.minimum(j, n1 - 1) // tps, jnp.minimum(j, n1 - 1) % tps)
    second = lambda j: (jnp.maximum(j - n1, 0) // tps, jnp.maximum(j - n1, 0) % tps)
    return n1, first, second


def mm_nt(a3, b3, w3, tn, name):
    if b3 is None:
        b3 = a3
    _, m, sec = a3.shape
    q, k, ws = w3.shape
    n = q * ws
    tpq = ws // tn
    mb = _split_rows(m)
    n1, first, second = _two_stacks(a3, b3, tn)

    def body(a_ref, b_ref, w_ref, o_ref):
        j = pl.program_id(1)

        @pl.when(j == 0)
        def _():
            o_ref[...] = jnp.zeros_like(o_ref)

        @pl.when(j < n1)
        def _():
            o_ref[...] += _dot(a_ref[...], w_ref[...], NT)

        @pl.when(j >= n1)
        def _():
            o_ref[...] += _dot(b_ref[...], w_ref[...], NT)

    return pl.pallas_call(
        body, name=name, grid=(m // mb, n // tn),
        in_specs=[pl.BlockSpec((None, mb, tn), lambda i, j: (first(j)[0], i, first(j)[1])),
                  pl.BlockSpec((None, mb, tn), lambda i, j: (second(j)[0], i, second(j)[1])),
                  pl.BlockSpec((None, k, tn), lambda i, j: (j // tpq, 0, j % tpq))],
        out_specs=pl.BlockSpec((mb, k), lambda i, j: (i, 0)),
        out_shape=jax.ShapeDtypeStruct((m, k), F32), compiler_params=_cp(),
    )(a3, b3, w3)


def mm_tn(a, b3, c3, ws, tn, name):
    m, k = a.shape
    sec = b3.shape[2]
    n = (b3.shape[0] + (0 if c3 is None else c3.shape[0])) * sec
    if c3 is None:
        c3 = b3
    tpq = ws // tn
    kb = 256 if k % 256 == 0 else 128
    n1, first, second = _two_stacks(b3, c3, tn)

    def body(a_ref, b_ref, c_ref, o_ref):
        def product(rhs_ref):
            rhs = rhs_ref[...]
            for i in range(k // kb):
                o_ref[i * kb:(i + 1) * kb, :] = _dot(a_ref[:, i * kb:(i + 1) * kb], rhs, TN).astype(BF16)

        @pl.when(pl.program_id(0) < n1)
        def _():
            product(b_ref)

        @pl.when(pl.program_id(0) >= n1)
        def _():
            product(c_ref)

    return pl.pallas_call(
        body, name=name, grid=(n // tn,),
        in_specs=[pl.BlockSpec((m, k), lambda j: (0, 0)),
                  pl.BlockSpec((None, m, tn), lambda j: (first(j)[0], 0, first(j)[1])),
                  pl.BlockSpec((None, m, tn), lambda j: (second(j)[0], 0, second(j)[1]))],
        out_specs=pl.BlockSpec((None, k, tn), lambda j: (j // tpq, 0, j % tpq)),
        out_shape=jax.ShapeDtypeStruct((n // ws, k, ws), BF16), compiler_params=_cp(),
    )(a, b3, c3)


POOL_REACH = 8 << GRID_W_LOG2


def _token_parts(tok, cx):
    lat = tok - cx
    return tok < cx, lat >> GRID_W_LOG2, lat & ((1 << GRID_W_LOG2) - 1)


def _pool_mask(gi, row0, col0, tm, ncols, cx, transposed):
    half = jnp.left_shift(1, gi)
    r = lax.broadcasted_iota(jnp.int32, (tm, 1), 0) + row0
    c = lax.broadcasted_iota(jnp.int32, (1, ncols), 1) + col0
    out_tok, src_tok = (c, r) if transposed else (r, c)
    o_ctx, o_row, o_col = _token_parts(out_tok, cx)
    s_ctx, s_row, s_col = _token_parts(src_tok, cx)

    def inside(o, s):
        return (s >= o - half) & (s <= o + half - 1)

    ctx_hit = o_ctx & s_ctx & inside(out_tok, src_tok)
    lat_hit = (~o_ctx) & (~s_ctx) & inside(o_row, s_row) & inside(o_col, s_col)
    return jnp.where(ctx_hit | lat_hit, 1.0, 0.0).astype(BF16)


def _pool_inv_count(gi, row0, tm, cx, seq):
    half = jnp.left_shift(1, gi)
    r = lax.broadcasted_iota(jnp.int32, (tm, 1), 0) + row0
    is_ctx, row, col = _token_parts(r, cx)

    def count(pos, size):
        return jnp.minimum(pos + half - 1, size - 1) - jnp.maximum(pos - half, 0) + 1

    cnt = jnp.where(is_ctx, count(r, cx), count(row, seq >> GRID_W_LOG2) * count(col, 1 << GRID_W_LOG2))
    return 1.0 / cnt.astype(F32)


def _window_sum(gi, i, tm, t, cx, src_ref, transposed):
    mask = _pool_mask(gi, i * tm, 0, tm, t, cx, transposed)
    return _dot(mask, src_ref[...])


def mix_a_fwd(z0, pool_w, pool_scale, cx):
    _, t, half_d = z0.shape
    g = half_d // N_POOL
    seq = t - cx
    tm = _row_block(cx)

    def body(v_ref, ag_ref, w_ref, sc_ref, u_ref, vb_ref):
        gi = pl.program_id(0)
        vb_ref[...] = v_ref[...].astype(BF16)
        w = w_ref[...].astype(BF16)
        sc = sc_ref[...]

        def step(i, carry):
            row0 = pl.multiple_of(i * tm, tm)
            rows = pl.ds(row0, tm)
            inv = _pool_inv_count(gi, row0, tm, cx, seq)
            pooled = _window_sum(gi, i, tm, t, cx, vb_ref, False) * inv - v_ref[rows, :]
            mixed = _dot(pooled.astype(BF16), w) * sc
            u_ref[rows, :] = (mixed * _silu(ag_ref[rows, :])).astype(BF16)
            return carry

        lax.fori_loop(0, t // tm, step, 0)

    sec = lambda s: pl.BlockSpec((None, t, g), lambda j: (s, 0, j))
    return pl.pallas_call(
        body, name="mix_a_fwd", grid=(N_POOL,),
        in_specs=[sec(0), sec(1), pl.BlockSpec((None, g, g), lambda j: (j, 0, 0)),
                  pl.BlockSpec((1, g), lambda j: (0, j))],
        out_specs=pl.BlockSpec((t, g), lambda j: (0, j)),
        out_shape=jax.ShapeDtypeStruct((t, half_d), BF16),
        scratch_shapes=[pltpu.VMEM((t, g), BF16)], compiler_params=_cp(),
    )(z0, z0, pool_w, pool_scale)


def mix_a_bwd(z0, du, pool_w, pool_scale, cx):
    _, t, half_d = z0.shape
    g = half_d // N_POOL
    seq = t - cx
    tm = _row_block(cx)
    gq = g // 4

    def body(v_ref, ag_ref, du_ref, w_ref, sc_ref, dz_ref, dw_ref, dsc_ref,
             vb_ref, pooled_ref, dmx_ref, dpl_ref, wdp_ref):
        gi = pl.program_id(0)
        vb_ref[...] = v_ref[...].astype(BF16)
        w = w_ref[...].astype(BF16)
        sc = sc_ref[...]

        def first(i, dsc):
            row0 = pl.multiple_of(i * tm, tm)
            rows = pl.ds(row0, tm)
            inv = _pool_inv_count(gi, row0, tm, cx, seq)
            pooled = (_window_sum(gi, i, tm, t, cx, vb_ref, False) * inv - v_ref[rows, :]).astype(BF16)
            pooled_ref[rows, :] = pooled
            mixed = _dot(pooled, w)
            ag = ag_ref[rows, :]
            duv = du_ref[rows, :]
            dz_ref[1, rows, :] = (duv * (mixed * sc) * _dsilu(ag)).astype(BF16)
            dms = duv * _silu(ag)
            dmixed = (dms * sc).astype(BF16)
            dmx_ref[rows, :] = dmixed
            dpooled = _dot(dmixed, w, NT)
            dpl_ref[rows, :] = dpooled
            wdp_ref[rows, :] = (dpooled * inv).astype(BF16)
            return dsc + jnp.sum(dms * mixed, axis=0, keepdims=True)

        dsc_ref[...] = lax.fori_loop(0, t // tm, first, jnp.zeros((1, g), F32))
        dw = _dot(pooled_ref[...], dmx_ref[...], TN)
        for qi in range(4):
            dw_ref[qi] = dw[qi * gq:(qi + 1) * gq, :]

        def second(i, carry):
            row0 = pl.multiple_of(i * tm, tm)
            rows = pl.ds(row0, tm)
            dz_ref[0, rows, :] = (_window_sum(gi, i, tm, t, cx, wdp_ref, True) - dpl_ref[rows, :]).astype(BF16)
            return carry

        lax.fori_loop(0, t // tm, second, 0)

    sec = lambda s: pl.BlockSpec((None, t, g), lambda j: (s, 0, j))
    return pl.pallas_call(
        body, name="mix_a_bwd", grid=(N_POOL,),
        in_specs=[sec(0), sec(1), pl.BlockSpec((t, g), lambda j: (0, j)),
                  pl.BlockSpec((None, g, g), lambda j: (j, 0, 0)),
                  pl.BlockSpec((1, g), lambda j: (0, j))],
        out_specs=[pl.BlockSpec((2, t, g), lambda j: (0, 0, j)),
                   pl.BlockSpec((4, None, gq, g), lambda j: (0, j, 0, 0)),
                   pl.BlockSpec((1, g), lambda j: (0, j))],
        out_shape=[jax.ShapeDtypeStruct((2, t, half_d), BF16),
                   jax.ShapeDtypeStruct((4, N_POOL, gq, g), F32),
                   jax.ShapeDtypeStruct((1, half_d), F32)],
        scratch_shapes=[pltpu.VMEM((t, g), BF16), pltpu.VMEM((t, g), BF16), pltpu.VMEM((t, g), BF16),
                        pltpu.VMEM((t, g), F32), pltpu.VMEM((t, g), BF16)],
        compiler_params=_cp(),
    )(z0, z0, du, pool_w, pool_scale)


def _conv_masks(t, cx):
    r = lax.broadcasted_iota(jnp.int32, (t, 1), 0)
    has_prev = jnp.where((r == 0) | (r == cx), 0.0, 1.0)
    has_next = jnp.where((r == cx - 1) | (r == t - 1), 0.0, 1.0)
    return has_prev, has_next


def mix_b_fwd(z0, conv_w, conv_b, cx):
    _, t, half_d = z0.shape
    gb = 128

    def body(bx_ref, bb_ref, bc_ref, bg_ref, w_ref, b_ref, u_ref):
        has_prev, has_next = _conv_masks(t, cx)
        tt = bc_ref[...] * bx_ref[...]
        prev = pltpu.roll(tt, 1, 0) * has_prev
        nxt = pltpu.roll(tt, t - 1, 0) * has_next
        cv = prev * w_ref[0:1, :] + tt * w_ref[1:2, :] + nxt * w_ref[2:3, :] + b_ref[...]
        u_ref[...] = (bb_ref[...] * cv * _silu(bg_ref[...])).astype(BF16)

    sec = lambda s: pl.BlockSpec((None, t, gb), lambda j: (s, 0, j))
    return pl.pallas_call(
        body, name="mix_b_fwd", grid=(half_d // gb,),
        in_specs=[sec(2), sec(3), sec(4), sec(5), pl.BlockSpec((3, gb), lambda j: (0, j)),
                  pl.BlockSpec((1, gb), lambda j: (0, j))],
        out_specs=pl.BlockSpec((t, gb), lambda j: (0, j)),
        out_shape=jax.ShapeDtypeStruct((t, half_d), BF16), compiler_params=_cp(),
    )(z0, z0, z0, z0, conv_w, conv_b)


def mix_b_bwd(z0, du, conv_w, conv_b, cx):
    _, t, half_d = z0.shape
    gb = 128
    off = half_d // gb

    def body(bx_ref, bb_ref, bc_ref, bg_ref, du_ref, w_ref, b_ref, dz_ref, dw_ref, db_ref):
        has_prev, has_next = _conv_masks(t, cx)
        bx, bb, bc, bg = bx_ref[...], bb_ref[...], bc_ref[...], bg_ref[...]
        duv = du_ref[...]
        tt = bc * bx
        prev = pltpu.roll(tt, 1, 0) * has_prev
        nxt = pltpu.roll(tt, t - 1, 0) * has_next
        w0, w1, w2 = w_ref[0:1, :], w_ref[1:2, :], w_ref[2:3, :]
        cv = prev * w0 + tt * w1 + nxt * w2 + b_ref[...]
        sg = _silu(bg)
        dz_ref[1] = (duv * cv * sg).astype(BF16)
        dz_ref[3] = (duv * bb * cv * _dsilu(bg)).astype(BF16)
        dcv = duv * bb * sg
        dw_ref[0:1, :] = jnp.sum(dcv * prev, axis=0, keepdims=True)
        dw_ref[1:2, :] = jnp.sum(dcv * tt, axis=0, keepdims=True)
        dw_ref[2:3, :] = jnp.sum(dcv * nxt, axis=0, keepdims=True)
        db_ref[...] = jnp.sum(dcv, axis=0, keepdims=True)
        dt = (pltpu.roll(dcv * has_prev, t - 1, 0) * w0 + dcv * w1
              + pltpu.roll(dcv * has_next, 1, 0) * w2)
        dz_ref[0] = (dt * bc).astype(BF16)
        dz_ref[2] = (dt * bx).astype(BF16)

    sec = lambda s: pl.BlockSpec((None, t, gb), lambda j: (s, 0, j))
    return pl.pallas_call(
        body, name="mix_b_bwd", grid=(half_d // gb,),
        in_specs=[sec(2), sec(3), sec(4), sec(5), pl.BlockSpec((t, gb), lambda j: (0, j + off)),
                  pl.BlockSpec((3, gb), lambda j: (0, j)), pl.BlockSpec((1, gb), lambda j: (0, j))],
        out_specs=[pl.BlockSpec((4, t, gb), lambda j: (0, 0, j)),
                   pl.BlockSpec((3, gb), lambda j: (0, j)), pl.BlockSpec((1, gb), lambda j: (0, j))],
        out_shape=[jax.ShapeDtypeStruct((4, t, half_d), BF16),
                   jax.ShapeDtypeStruct((3, half_d), F32), jax.ShapeDtypeStruct((1, half_d), F32)],
        compiler_params=_cp(),
    )(z0, z0, z0, z0, du, conv_w, conv_b)


def _lower_bound(lbl_ref, d):
    l0, l1, l2 = lbl_ref[d, 0:1, :], lbl_ref[d, 1:2, :], lbl_ref[d, 2:3, :]
    mx = jnp.maximum(jnp.maximum(l0, l1), l2)
    e0, e1, e2 = jnp.exp(l0 - mx), jnp.exp(l1 - mx), jnp.exp(l2 - mx)
    inv = 1.0 / (e0 + e1 + e2)
    return (e0 + e1) * inv, (e0 * inv, e1 * inv, e2 * inv)


def _chunk_consts(d):
    r = lax.broadcasted_iota(jnp.int32, (CHUNK, CHUNK), 0)
    c = lax.broadcasted_iota(jnp.int32, (CHUNK, CHUNK), 1)
    keep = (c <= r) if d == 0 else (c >= r)
    return jnp.where(keep, 1.0, 0.0).astype(F32), keep


def _chunk_of_step(s, d, nc, ncc):
    if d == 0:
        return s
    return jnp.where(s < ncc, ncc - 1 - s, nc - 1 + ncc - s)


def _chunk_terms(lfc, kc, qc, cum):
    bc = _dot(cum, lfc, precision=lax.Precision.HIGHEST)
    bl = jnp.sum(lfc, axis=0, keepdims=True)
    e = jnp.exp(bc)
    einv = jnp.exp(-bc)
    erem = jnp.exp(bl - bc)
    return e, einv, erem, jnp.exp(bl), qc * e, kc * einv, kc * erem


def hgrn_fwd(z1, lbl, onorm, cx):
    _, t, d = z1.shape
    seq = t - cx
    nc, ncc = t // CHUNK, cx // CHUNK

    def body(zf_ref, zb_ref, v_ref, q_ref, g_ref, lbl_ref, on_ref, o_ref, r_ref,
             lf_ref, k_ref, oacc_ref, st_ref):
        for dr, z_ref in ((0, zf_ref), (1, zb_ref)):
            lbv, _ = _lower_bound(lbl_ref, dr)
            z = z_ref[...]
            lf_ref[...] = jnp.log(lbv + (1.0 - lbv) * jax.nn.sigmoid(z))
            k_ref[...] = (1.0 - lbv) * jax.nn.sigmoid(-z)
            st_ref[...] = jnp.zeros_like(st_ref)
            cum, keep = _chunk_consts(dr)

            def step(s, carry, dr=dr, cum=cum, keep=keep):
                n = _chunk_of_step(s, dr, nc, ncc)
                rows = pl.ds(pl.multiple_of(n * CHUNK, CHUNK), CHUNK)
                vc = v_ref[rows, :].astype(BF16)
                _, _, _, dec, qd, ki, kd = _chunk_terms(lf_ref[rows, :], k_ref[rows, :], q_ref[rows, :], cum)
                qdb = qd.astype(BF16)
                a = jnp.where(keep, _dot(qdb, ki.astype(BF16), NT), 0.0)
                st = st_ref[...]
                oc = _dot(qdb, st.astype(BF16), NT) + _dot(a.astype(BF16), vc)
                st_ref[...] = st * dec + _dot(vc, kd.astype(BF16), TN)
                if dr == 0:
                    oacc_ref[rows, :] = oc
                else:
                    oacc_ref[rows, :] += oc
                return carry

            lax.fori_loop(0, nc, step, 0, unroll=4)

        o = oacc_ref[cx:, :]
        o_ref[...] = o
        rstd = lax.rsqrt(jnp.mean(o * o, axis=-1, keepdims=True) + EPS)
        r_ref[...] = (o * rstd * on_ref[...] * _silu(g_ref[cx:, :])).astype(BF16)

    sec = lambda s: pl.BlockSpec((None, t, HEAD), lambda h: (s, 0, h))
    col = pl.BlockSpec((seq, HEAD), lambda h: (0, h))
    return pl.pallas_call(
        body, name="hgrn_fwd", grid=(d // HEAD,),
        in_specs=[sec(0), sec(1), sec(2), sec(3), sec(4),
                  pl.BlockSpec((2, 3, HEAD), lambda h: (0, 0, h)), pl.BlockSpec((1, HEAD), lambda h: (0, h))],
        out_specs=[col, col],
        out_shape=[jax.ShapeDtypeStruct((seq, d), F32), jax.ShapeDtypeStruct((seq, d), BF16)],
        scratch_shapes=[pltpu.VMEM((t, HEAD), F32), pltpu.VMEM((t, HEAD), F32), pltpu.VMEM((t, HEAD), F32),
                        pltpu.VMEM((HEAD, HEAD), F32)],
        compiler_params=_cp(),
    )(z1, z1, z1, z1, z1, lbl, onorm)


def hgrn_bwd(z1, lbl, onorm, o, dr_out, cx):
    _, t, d = z1.shape
    seq = t - cx
    nc, ncc = t // CHUNK, cx // CHUNK

    def body(zf_ref, zb_ref, v_ref, q_ref, g_ref, lbl_ref, on_ref, o_ref, dr_ref,
             dz_ref, don_ref, dlb_ref,
             lf_ref, k_ref, do_ref, dq_ref, dv_ref, dk_ref, dlf_ref, ssc_ref, dst_ref):
        o = o_ref[...]
        g = g_ref[cx:, :]
        drv = dr_ref[...]
        onv = on_ref[...]
        rstd = lax.rsqrt(jnp.mean(o * o, axis=-1, keepdims=True) + EPS)
        ohat = o * rstd
        sg = _silu(g)
        don_ref[...] = jnp.sum(drv * ohat * sg, axis=0, keepdims=True)
        dz_ref[4, :cx, :] = jnp.zeros((cx, HEAD), BF16)
        dz_ref[4, cx:, :] = (drv * ohat * onv * _dsilu(g)).astype(BF16)
        dohat = drv * onv * sg
        do_ref[:cx, :] = jnp.zeros((cx, HEAD), F32)
        do_ref[cx:, :] = rstd * (dohat - ohat * jnp.mean(dohat * ohat, axis=-1, keepdims=True))

        for dr, z_ref in ((0, zf_ref), (1, zb_ref)):
            lbv, _ = _lower_bound(lbl_ref, dr)
            z = z_ref[...]
            lf_ref[...] = jnp.log(lbv + (1.0 - lbv) * jax.nn.sigmoid(z))
            k_ref[...] = (1.0 - lbv) * jax.nn.sigmoid(-z)
            cum, keep = _chunk_consts(dr)
            cum_t, _ = _chunk_consts(1 - dr)

            st_init = jnp.zeros((HEAD, HEAD), F32)

            def state_step(s, st, dr=dr, cum=cum):
                n = _chunk_of_step(s, dr, nc, ncc)
                rows = pl.ds(pl.multiple_of(n * CHUNK, CHUNK), CHUNK)
                ssc_ref[n] = st
                _, _, _, dec, _, _, kd = _chunk_terms(lf_ref[rows, :], k_ref[rows, :], q_ref[rows, :], cum)
                return st * dec + _dot(v_ref[rows, :].astype(BF16), kd.astype(BF16), TN)

            lax.fori_loop(0, nc, state_step, st_init, unroll=4)
            dst_ref[...] = jnp.zeros_like(dst_ref)

            def grad_step(s2, carry, dr=dr, cum=cum, cum_t=cum_t, keep=keep):
                n = _chunk_of_step(nc - 1 - s2, dr, nc, ncc)
                rows = pl.ds(pl.multiple_of(n * CHUNK, CHUNK), CHUNK)
                vc = v_ref[rows, :].astype(BF16)
                e, einv, erem, dec, qd, ki, kd = _chunk_terms(
                    lf_ref[rows, :], k_ref[rows, :], q_ref[rows, :], cum)
                qdb, kib, kdb = qd.astype(BF16), ki.astype(BF16), kd.astype(BF16)
                doc = do_ref[rows, :].astype(BF16)
                st0 = ssc_ref[n]
                dst = dst_ref[...]
                dstb = dst.astype(BF16)
                a = jnp.where(keep, _dot(qdb, kib, NT), 0.0).astype(BF16)
                da = jnp.where(keep, _dot(doc, vc, NT), 0.0).astype(BF16)
                dqd = _dot(doc, st0.astype(BF16)) + _dot(da, kib)
                dki = _dot(da, qdb, TN)
                dv = _dot(a, doc, TN) + _dot(kdb, dstb, NT)
                dkd = _dot(vc, dstb)
                ddec = jnp.sum(dst * st0, axis=0, keepdims=True)
                dst_ref[...] = _dot(doc, qdb, TN) + dst * dec
                dbc = dqd * qd - dki * ki - dkd * kd
                dbl = jnp.sum(dkd * kd, axis=0, keepdims=True) + ddec * dec
                dlf_ref[rows, :] = _dot(cum_t, dbc, precision=lax.Precision.HIGHEST) + dbl
                dk_ref[rows, :] = dki * einv + dkd * erem
                if dr == 0:
                    dq_ref[rows, :] = dqd * e
                    dv_ref[rows, :] = dv
                else:
                    dq_ref[rows, :] += dqd * e
                    dv_ref[rows, :] += dv
                return carry

            lax.fori_loop(0, nc, grad_step, 0, unroll=2)

            sig = jax.nn.sigmoid(z)
            one_lb = 1.0 - lbv
            f = lbv + one_lb * sig
            dlf = dlf_ref[...]
            dk = dk_ref[...]
            dsig = (dlf / f - dk) * one_lb
            dz_ref[dr] = (dsig * sig * (1.0 - sig)).astype(BF16)
            dlb_ref[dr:dr + 1, :] = jnp.sum((dlf / f - dk) * (1.0 - sig), axis=0, keepdims=True)

        dz_ref[2] = dv_ref[...].astype(BF16)
        dz_ref[3] = dq_ref[...].astype(BF16)

    sec = lambda s: pl.BlockSpec((None, t, HEAD), lambda h: (s, 0, h))
    col = pl.BlockSpec((seq, HEAD), lambda h: (0, h))
    tvec = pltpu.VMEM((t, HEAD), F32)
    return pl.pallas_call(
        body, name="hgrn_bwd", grid=(d // HEAD,),
        in_specs=[sec(0), sec(1), sec(2), sec(3), sec(4),
                  pl.BlockSpec((2, 3, HEAD), lambda h: (0, 0, h)), pl.BlockSpec((1, HEAD), lambda h: (0, h)),
                  col, col],
        out_specs=[pl.BlockSpec((5, t, HEAD), lambda h: (0, 0, h)),
                   pl.BlockSpec((1, HEAD), lambda h: (0, h)), pl.BlockSpec((2, HEAD), lambda h: (0, h))],
        out_shape=[jax.ShapeDtypeStruct((5, t, d), BF16), jax.ShapeDtypeStruct((1, d), F32),
                   jax.ShapeDtypeStruct((2, d), F32)],
        scratch_shapes=[tvec, tvec, tvec, tvec, tvec, tvec, tvec,
                        pltpu.VMEM((nc, HEAD, HEAD), F32), pltpu.VMEM((HEAD, HEAD), F32)],
        compiler_params=_cp(),
    )(z1, z1, z1, z1, z1, lbl, onorm, o, dr_out)


def _gates(z, lbv):
    e = jnp.exp(-jnp.abs(z))
    r = 1.0 / (1.0 + e)
    er = e * r
    pos = z >= 0.0
    sig = jnp.where(pos, r, er)
    nsig = jnp.where(pos, er, r)
    return sig, nsig, lbv + (1.0 - lbv) * sig


def _split3(x):
    hi = x.astype(BF16)
    r1 = x - hi.astype(F32)
    mid = r1.astype(BF16)
    lo = (r1 - mid.astype(F32)).astype(BF16)
    return jnp.concatenate([hi, mid, lo], axis=1)


def _cumsum_chunk(cum, x):
    y = _dot(cum, _split3(x))
    return y[:, :HEAD] + y[:, HEAD:2 * HEAD] + y[:, 2 * HEAD:]


def _chunk_rows(n):
    return pl.ds(pl.multiple_of(n * CHUNK, CHUNK), CHUNK)


def _group(nc, prefer=(4, 3, 2, 1)):
    return next(u for u in prefer if nc % u == 0)


WIDE_GROUP = (12, 6, 4, 3, 2, 1)


def _decay_pass(lf_ref, bc_ref, dec_ref, cum, nc):
    grp = _group(nc, WIDE_GROUP)

    def step(m, carry):
        ns = [m * grp + u for u in range(grp)]
        lfc = [lf_ref[_chunk_rows(n), :] for n in ns]
        bc = [_cumsum_chunk(cum, x) for x in lfc]
        for u, n in enumerate(ns):
            bc_ref[_chunk_rows(n), :] = bc[u]
            dec_ref[n] = jnp.broadcast_to(jnp.exp(jnp.sum(lfc[u], axis=0, keepdims=True)), (8, HEAD))
        return carry

    lax.fori_loop(0, nc // grp, step, 0)


def hgrn_fwd(z1, lbl, onorm, cx):
    _, t, d = z1.shape
    seq = t - cx
    nc, ncc = t // CHUNK, cx // CHUNK

    grp, sgrp = _group(nc, (6, 4, 3, 2, 1)), _group(nc, WIDE_GROUP)

    def body(zf_ref, zb_ref, v_ref, q_ref, g_ref, lbl_ref, on_ref, o_ref, r_ref,
             lf_ref, k_ref, bc_ref, dec_ref, qd_ref, ki_ref, oacc_ref, ds_ref):
        for dr, z_ref in ((0, zf_ref), (1, zb_ref)):
            lbv, _ = _lower_bound(lbl_ref, dr)
            _, nsig, f = _gates(z_ref[...], lbv)
            lf_ref[...] = jnp.log(f)
            k_ref[...] = (1.0 - lbv) * nsig
            cum, keep = _chunk_consts(dr)
            _decay_pass(lf_ref, bc_ref, dec_ref, cum.astype(BF16), nc)
            bc = bc_ref[...]
            qd_ref[...] = (q_ref[...] * jnp.exp(bc)).astype(BF16)
            ki_ref[...] = (k_ref[...] * jnp.exp(-bc)).astype(BF16)

            def local_step(m, carry, dr=dr, keep=keep):
                ns = [m * grp + u for u in range(grp)]
                rows = [_chunk_rows(n) for n in ns]
                qd = [qd_ref[r, :] for r in rows]
                ki = [ki_ref[r, :] for r in rows]
                vc = [v_ref[r, :].astype(BF16) for r in rows]
                sc = [_dot(qd[u], ki[u], NT) for u in range(grp)]
                inc = [_dot(vc[u], ki[u], TN) for u in range(grp)]
                a = [jnp.where(keep, s, 0.0).astype(BF16) for s in sc]
                intra = [_dot(a[u], vc[u]) for u in range(grp)]
                for u in range(grp):
                    ds_ref[ns[u]] = inc[u] * dec_ref[ns[u]][0:1, :]
                    if dr == 0:
                        oacc_ref[rows[u], :] = intra[u]
                    else:
                        oacc_ref[rows[u], :] += intra[u]
                return carry

            lax.fori_loop(0, nc // grp, local_step, 0)

            def state_step(m, st, dr=dr):
                ns = [_chunk_of_step(m * sgrp + u, dr, nc, ncc) for u in range(sgrp)]
                rows = [_chunk_rows(n) for n in ns]
                sts = []
                for n in ns:
                    sts.append(st.astype(BF16))
                    st = st * dec_ref[n][0:1, :] + ds_ref[n]
                inter = [_dot(qd_ref[rows[u], :], sts[u], NT) for u in range(sgrp)]
                for u in range(sgrp):
                    oacc_ref[rows[u], :] += inter[u]
                return st

            lax.fori_loop(0, nc // sgrp, state_step, jnp.zeros((HEAD, HEAD), F32))

        o = oacc_ref[cx:, :]
        o_ref[...] = o
        rstd = lax.rsqrt(jnp.mean(o * o, axis=-1, keepdims=True) + EPS)
        r_ref[...] = (o * rstd * on_ref[...] * _silu(g_ref[cx:, :])).astype(BF16)

    sec = lambda s: pl.BlockSpec((None, t, HEAD), lambda h: (s, 0, h))
    col = pl.BlockSpec((seq, HEAD), lambda h: (0, h))
    tf32, tb16 = pltpu.VMEM((t, HEAD), F32), pltpu.VMEM((t, HEAD), BF16)
    return pl.pallas_call(
        body, name="hgrn_fwd", grid=(d // HEAD,),
        in_specs=[sec(0), sec(1), sec(2), sec(3), sec(4),
                  pl.BlockSpec((2, 3, HEAD), lambda h: (0, 0, h)), pl.BlockSpec((1, HEAD), lambda h: (0, h))],
        out_specs=[col, col],
        out_shape=[jax.ShapeDtypeStruct((seq, d), F32), jax.ShapeDtypeStruct((seq, d), BF16)],
        scratch_shapes=[tf32, tf32, tf32, pltpu.VMEM((nc, 8, HEAD), F32), tb16, tb16, tf32,
                        pltpu.VMEM((nc, HEAD, HEAD), F32)],
        compiler_params=_cp(),
    )(z1, z1, z1, z1, z1, lbl, onorm)


def hgrn_bwd(z1, lbl, onorm, o, dr_out, cx):
    _, t, d = z1.shape
    seq = t - cx
    nc, ncc = t // CHUNK, cx // CHUNK

    grp2 = grp = _group(nc)

    def body(zf_ref, zb_ref, v_ref, q_ref, g_ref, lbl_ref, on_ref, o_ref, dr_ref,
             dz_ref, don_ref, dlb_ref,
             lf_ref, k_ref, bc_ref, dec_ref, qd_ref, ki_ref, do_ref,
             dqd_ref, dki_ref, dq_ref, dv_ref, ds_ref, dsl_ref):
        o = o_ref[...]
        g = g_ref[cx:, :]
        drv = dr_ref[...]
        onv = on_ref[...]
        rstd = lax.rsqrt(jnp.mean(o * o, axis=-1, keepdims=True) + EPS)
        ohat = o * rstd
        sg = _silu(g)
        don_ref[...] = jnp.sum(drv * ohat * sg, axis=0, keepdims=True)
        dz_ref[4, :cx, :] = jnp.zeros((cx, HEAD), BF16)
        dz_ref[4, cx:, :] = (drv * ohat * onv * _dsilu(g)).astype(BF16)
        dohat = drv * onv * sg
        do_ref[:cx, :] = jnp.zeros((cx, HEAD), BF16)
        do_ref[cx:, :] = (rstd * (dohat - ohat * jnp.mean(dohat * ohat, axis=-1, keepdims=True))).astype(BF16)

        for dr, z_ref in ((0, zf_ref), (1, zb_ref)):
            lbv, _ = _lower_bound(lbl_ref, dr)
            _, nsig, f = _gates(z_ref[...], lbv)
            lf_ref[...] = jnp.log(f)
            k_ref[...] = (1.0 - lbv) * nsig
            cum, keep = _chunk_consts(dr)
            cum_t = _chunk_consts(1 - dr)[0].astype(BF16)
            _decay_pass(lf_ref, bc_ref, dec_ref, cum.astype(BF16), nc)
            bc = bc_ref[...]
            qd_ref[...] = (q_ref[...] * jnp.exp(bc)).astype(BF16)
            ki_ref[...] = (k_ref[...] * jnp.exp(-bc)).astype(BF16)

            def local_step(m, carry, dr=dr, keep=keep):
                ns = [m * grp + u for u in range(grp)]
                rows = [_chunk_rows(n) for n in ns]
                rng = range(grp)
                qd = [qd_ref[r, :] for r in rows]
                ki = [ki_ref[r, :] for r in rows]
                doc = [do_ref[r, :] for r in rows]
                vc = [v_ref[r, :].astype(BF16) for r in rows]
                sc = [_dot(qd[u], ki[u], NT) for u in rng]
                dsc = [_dot(doc[u], vc[u], NT) for u in rng]
                inc = [_dot(vc[u], ki[u], TN) for u in rng]
                dinc = [_dot(doc[u], qd[u], TN) for u in rng]
                a = [jnp.where(keep, s, 0.0).astype(BF16) for s in sc]
                da = [jnp.where(keep, s, 0.0).astype(BF16) for s in dsc]
                dqd = [_dot(da[u], ki[u]) for u in rng]
                dki = [_dot(da[u], qd[u], TN) for u in rng]
                dv = [_dot(a[u], doc[u], TN) for u in rng]
                for u in rng:
                    ds_ref[ns[u]] = inc[u] * dec_ref[ns[u]][0:1, :]
                    dsl_ref[ns[u]] = dinc[u]
                    dqd_ref[rows[u], :] = dqd[u]
                    dki_ref[rows[u], :] = dki[u]
                    if dr == 0:
                        dv_ref[rows[u], :] = dv[u]
                    else:
                        dv_ref[rows[u], :] += dv[u]
                return carry

            lax.fori_loop(0, nc // grp, local_step, 0)

            def state_step(s, st, dr=dr):
                n = _chunk_of_step(s, dr, nc, ncc)
                inc = ds_ref[n]
                ds_ref[n] = st
                return st * dec_ref[n][0:1, :] + inc

            lax.fori_loop(0, nc, state_step, jnp.zeros((HEAD, HEAD), F32), unroll=4)

            def dstate_step(s, dst, dr=dr):
                n = _chunk_of_step(nc - 1 - s, dr, nc, ncc)
                inc = dsl_ref[n]
                dsl_ref[n] = dst
                return inc + dst * dec_ref[n][0:1, :]

            lax.fori_loop(0, nc, dstate_step, jnp.zeros((HEAD, HEAD), F32), unroll=4)

            def grad_step(m, carry, dr=dr, cum_t=cum_t):
                ns = [m * grp2 + u for u in range(grp2)]
                rows = [_chunk_rows(n) for n in ns]
                rng = range(grp2)
                st0 = [ds_ref[n] for n in ns]
                dst = [dsl_ref[n] for n in ns]
                dstb = [x.astype(BF16) for x in dst]
                dec = [dec_ref[n][0:1, :] for n in ns]
                doc = [do_ref[r, :] for r in rows]
                vc = [v_ref[r, :].astype(BF16) for r in rows]
                e = [jnp.exp(bc_ref[r, :]) for r in rows]
                einv = [jnp.exp(-bc_ref[r, :]) for r in rows]
                qd = [q_ref[rows[u], :] * e[u] for u in rng]
                ki = [k_ref[rows[u], :] * einv[u] for u in rng]
                kd = [ki[u] * dec[u] for u in rng]
                dqd_st = [_dot(doc[u], st0[u].astype(BF16)) for u in rng]
                dkd = [_dot(vc[u], dstb[u]) for u in rng]
                dv_st = [_dot(kd[u].astype(BF16), dstb[u], NT) for u in rng]
                dqd = [dqd_ref[rows[u], :] + dqd_st[u] for u in rng]
                dki = [dki_ref[r, :] for r in rows]
                dbc = [dqd[u] * qd[u] - dki[u] * ki[u] - dkd[u] * kd[u] for u in rng]
                cs = [_cumsum_chunk(cum_t, x) for x in dbc]
                for u in rng:
                    ddec = jnp.sum(dst[u] * st0[u], axis=0, keepdims=True)
                    dbl = jnp.sum(dkd[u] * kd[u], axis=0, keepdims=True) + ddec * dec[u]
                    dv_ref[rows[u], :] += dv_st[u]
                    dqd_ref[rows[u], :] = cs[u] + dbl
                    dki_ref[rows[u], :] = dki[u] * einv[u] + dkd[u] * (einv[u] * dec[u])
                    if dr == 0:
                        dq_ref[rows[u], :] = dqd[u] * e[u]
                    else:
                        dq_ref[rows[u], :] += dqd[u] * e[u]
                return carry

            lax.fori_loop(0, nc // grp2, grad_step, 0)

            sig, nsig, f = _gates(z_ref[...], lbv)
            common = (dqd_ref[...] / f - dki_ref[...]) * nsig
            dz_ref[dr] = (common * ((1.0 - lbv) * sig)).astype(BF16)
            dlb_ref[dr:dr + 1, :] = jnp.sum(common, axis=0, keepdims=True)

        dz_ref[2] = dv_ref[...].astype(BF16)
        dz_ref[3] = dq_ref[...].astype(BF16)

    sec = lambda s: pl.BlockSpec((None, t, HEAD), lambda h: (s, 0, h))
    col = pl.BlockSpec((seq, HEAD), lambda h: (0, h))
    tf32, tb16 = pltpu.VMEM((t, HEAD), F32), pltpu.VMEM((t, HEAD), BF16)
    states = pltpu.VMEM((nc, HEAD, HEAD), F32)
    return pl.pallas_call(
        body, name="hgrn_bwd", grid=(d // HEAD,),
        in_specs=[sec(0), sec(1), sec(2), sec(3), sec(4),
                  pl.BlockSpec((2, 3, HEAD), lambda h: (0, 0, h)), pl.BlockSpec((1, HEAD), lambda h: (0, h)),
                  col, col],
        out_specs=[pl.BlockSpec((5, t, HEAD), lambda h: (0, 0, h)),
                   pl.BlockSpec((1, HEAD), lambda h: (0, h)), pl.BlockSpec((2, HEAD), lambda h: (0, h))],
        out_shape=[jax.ShapeDtypeStruct((5, t, d), BF16), jax.ShapeDtypeStruct((1, d), F32),
                   jax.ShapeDtypeStruct((2, d), F32)],
        scratch_shapes=[tf32, tf32, tf32, pltpu.VMEM((nc, 8, HEAD), F32), tb16, tb16, tb16,
                        tf32, tf32, tf32, tf32, states, states],
        compiler_params=_cp(),
    )(z1, z1, z1, z1, z1, lbl, onorm, o, dr_out)


def _place():
    x, y, c = lax.axis_index("x"), lax.axis_index("y"), lax.axis_index("c")
    chips = [(1 - x, y), (x, 1 - y), (1 - x, 1 - y)]
    return x, y, c, chips


def allgather_shards(bufs):
    n = len(bufs)

    def body(*refs):
        outs = refs[n:2 * n]
        done_ref, send_sems, recv_sems = refs[2 * n:]
        done_ref[...] = jnp.zeros((8, 128), F32)
        x, y, c, chips = _place()
        p = 2 * x + y
        half = [pl.ds(c * (s.shape[1] // 2), s.shape[1] // 2) for s in bufs]
        other = [pl.ds((1 - c) * (s.shape[1] // 2), s.shape[1] // 2) for s in bufs]

        def remote(i, k, src, dst, to):
            return pltpu.make_async_remote_copy(src_ref=src, dst_ref=dst, send_sem=send_sems.at[6 * i + k],
                                                recv_sem=recv_sems.at[6 * i + k], device_id=to, device_id_type=MESH)

        sends = []
        for i in range(n):
            for j, chip in enumerate(chips):
                mine = outs[i].at[p, half[i]]
                cp = remote(i, j, mine, mine, (*chip, c))
                cp.start()
                sends.append(cp)
        for i in range(n):
            for j, chip in enumerate(chips):
                landed = outs[i].at[2 * chip[0] + chip[1], half[i]]
                remote(i, j, landed, landed, (x, y, c)).wait_recv()
                cp = remote(i, 3 + j, landed, landed, (x, y, 1 - c))
                cp.start()
                sends.append(cp)
        for i in range(n):
            for j, chip in enumerate(chips):
                landed = outs[i].at[2 * chip[0] + chip[1], other[i]]
                remote(i, 3 + j, landed, landed, (x, y, c)).wait_recv()
        for cp in sends:
            cp.wait_send()

    return pl.pallas_call(
        body, name="allgather_shards",
        in_specs=[ANY] * n, out_specs=[ANY] * n + [VMEM],
        out_shape=[jax.ShapeDtypeStruct(s.shape, s.dtype) for s in bufs] + [jax.ShapeDtypeStruct((8, 128), F32)],
        input_output_aliases={i: i for i in range(n)},
        scratch_shapes=[pltpu.SemaphoreType.DMA((6 * n,)), pltpu.SemaphoreType.DMA((6 * n,))],
        compiler_params=pltpu.CompilerParams(has_side_effects=True),
    )(*bufs)


def exchange_halves(grads):
    n = len(grads)

    def body(*refs):
        ins, outs = refs[:n], refs[n:2 * n]
        send_sems, recv_sems = refs[2 * n:]
        x, y, c, _ = _place()
        copies = []
        for i in range(n):
            hr = grads[i].shape[1] // 2
            cp = pltpu.make_async_remote_copy(
                src_ref=ins[i].at[:, pl.ds((1 - c) * hr, hr)], dst_ref=outs[i],
                send_sem=send_sems.at[i], recv_sem=recv_sems.at[i],
                device_id=(x, y, 1 - c), device_id_type=MESH)
            cp.start()
            copies.append(cp)
        for cp in copies:
            cp.wait()

    return pl.pallas_call(
        body, name="exchange_halves",
        in_specs=[ANY] * n, out_specs=[ANY] * n,
        out_shape=[jax.ShapeDtypeStruct((4, g.shape[1] // 2, g.shape[2]), g.dtype) for g in grads],
        scratch_shapes=[pltpu.SemaphoreType.DMA((n,)), pltpu.SemaphoreType.DMA((n,))],
        compiler_params=pltpu.CompilerParams(has_side_effects=True),
    )(*grads)


def pair_sum(grad, got, chip_core):
    _, r, cc = grad.shape
    hr = r // 2
    tr = 256 if hr % 256 == 0 else hr
    nb = hr // tr

    def body(cc_ref, a_ref, b_ref, own_ref, sb_ref):
        s = a_ref[...].astype(F32) + b_ref[...].astype(F32)
        sb_ref[...] = s.astype(BF16)

        @pl.when(pl.program_id(1) == cc_ref[0])
        def _():
            own_ref[...] = s

    grid_spec = pltpu.PrefetchScalarGridSpec(
        num_scalar_prefetch=1, grid=(nb, 4),
        in_specs=[pl.BlockSpec((None, tr, cc), lambda i, qi, cc_ref: (qi, cc_ref[1] * nb + i, 0)),
                  pl.BlockSpec((None, tr, cc), lambda i, qi, cc_ref: (qi, i, 0))],
        out_specs=[pl.BlockSpec((tr, cc), lambda i, qi, cc_ref: (i, 0)),
                   pl.BlockSpec((None, tr, cc), lambda i, qi, cc_ref: (qi, i, 0))])
    return pl.pallas_call(
        body, name="pair_sum", grid_spec=grid_spec,
        out_shape=[jax.ShapeDtypeStruct((hr, cc), F32), jax.ShapeDtypeStruct((4, hr, cc), BF16)],
        compiler_params=_cp(),
    )(chip_core, grad, got)


def scatter_to_owners(parts):
    n = len(parts)

    def body(*refs):
        ins, outs = refs[:n], refs[n:2 * n]
        send_sems, recv_sems = refs[2 * n:]
        x, y, c, chips = _place()
        copies = []
        for i in range(n):
            for j, chip in enumerate(chips):
                cp = pltpu.make_async_remote_copy(
                    src_ref=ins[i].at[2 * chip[0] + chip[1]], dst_ref=outs[i].at[j],
                    send_sem=send_sems.at[3 * i + j], recv_sem=recv_sems.at[3 * i + j],
                    device_id=(*chip, c), device_id_type=MESH)
                cp.start()
                copies.append(cp)
        for cp in copies:
            cp.wait()

    return pl.pallas_call(
        body, name="scatter_to_owners",
        in_specs=[ANY] * n, out_specs=[ANY] * n,
        out_shape=[jax.ShapeDtypeStruct((3,) + p.shape[1:], p.dtype) for p in parts],
        scratch_shapes=[pltpu.SemaphoreType.DMA((3 * n,)), pltpu.SemaphoreType.DMA((3 * n,))],
        compiler_params=pltpu.CompilerParams(has_side_effects=True),
    )(*parts)


def owner_sum(own, got, chip_core):
    hr, cc = own.shape
    tr = 256 if hr % 256 == 0 else hr
    nb = hr // tr

    def body(cc_ref, a_ref, b_ref, o_ref):
        s = a_ref[...] + b_ref[0].astype(F32)
        s = s + b_ref[1].astype(F32)
        o_ref[...] = s + b_ref[2].astype(F32)

    grid_spec = pltpu.PrefetchScalarGridSpec(
        num_scalar_prefetch=1, grid=(nb,),
        in_specs=[pl.BlockSpec((tr, cc), lambda i, cc_ref: (i, 0)),
                  pl.BlockSpec((3, tr, cc), lambda i, cc_ref: (0, i, 0))],
        out_specs=pl.BlockSpec((tr, cc), lambda i, cc_ref: (cc_ref[1] * nb + i, 0)))
    return pl.pallas_call(
        body, name="owner_sum", grid_spec=grid_spec,
        out_shape=jax.ShapeDtypeStruct((2 * hr, cc), F32), compiler_params=_cp(),
    )(chip_core, own, got)


def share_halves(bufs):
    n = len(bufs)

    def body(*refs):
        outs = refs[n:2 * n]
        send_sems, recv_sems = refs[2 * n:]
        x, y, c, _ = _place()
        copies = []
        for i in range(n):
            hr = bufs[i].shape[0] // 2
            mine = outs[i].at[pl.ds(c * hr, hr)]
            cp = pltpu.make_async_remote_copy(
                src_ref=mine, dst_ref=mine, send_sem=send_sems.at[i], recv_sem=recv_sems.at[i],
                device_id=(x, y, 1 - c), device_id_type=MESH)
            cp.start()
            copies.append((cp, outs[i].at[pl.ds((1 - c) * hr, hr)]))
        for i, (cp, theirs) in enumerate(copies):
            cp.wait_send()
            pltpu.make_async_remote_copy(
                src_ref=theirs, dst_ref=theirs, send_sem=send_sems.at[i], recv_sem=recv_sems.at[i],
                device_id=(x, y, c), device_id_type=MESH).wait_recv()

    return pl.pallas_call(
        body, name="share_halves",
        in_specs=[ANY] * n, out_specs=[ANY] * n,
        out_shape=[jax.ShapeDtypeStruct(b.shape, b.dtype) for b in bufs],
        input_output_aliases={i: i for i in range(n)},
        scratch_shapes=[pltpu.SemaphoreType.DMA((n,)), pltpu.SemaphoreType.DMA((n,))],
        compiler_params=pltpu.CompilerParams(has_side_effects=True),
    )(*bufs)


def allgather8(v, name):
    r, n = v.shape

    def body(v_ref, out_ref, send_sems, recv_sems):
        x, y, c, _ = _place()
        me = 4 * x + 2 * y + c
        out_ref[me] = v_ref[...]

        def copy(k, slot, to):
            return pltpu.make_async_remote_copy(
                src_ref=v_ref, dst_ref=out_ref.at[slot], send_sem=send_sems.at[k - 1],
                recv_sem=recv_sems.at[k - 1], device_id=to, device_id_type=MESH)

        peers = []
        for k in range(1, 8):
            px = 1 - x if (k >> 2) & 1 else x
            py = 1 - y if (k >> 1) & 1 else y
            pc = 1 - c if k & 1 else c
            peers.append((px, py, pc))
            copy(k, me, (px, py, pc)).start()
        for k, (px, py, pc) in enumerate(peers, start=1):
            copy(k, 4 * px + 2 * py + pc, (x, y, c)).wait_recv()
        for k, peer in enumerate(peers, start=1):
            copy(k, me, peer).wait_send()

    return pl.pallas_call(
        body, name=name, in_specs=[VMEM], out_specs=VMEM,
        out_shape=jax.ShapeDtypeStruct((8, r, n), v.dtype),
        scratch_shapes=[pltpu.SemaphoreType.DMA((7,)), pltpu.SemaphoreType.DMA((7,))],
        compiler_params=_cp(has_side_effects=True),
    )(v)


HBM = pl.BlockSpec(memory_space=pltpu.HBM)
SEM = pl.BlockSpec(memory_space=pltpu.SEMAPHORE)
DATAFLOW = pltpu.SideEffectType.DATAFLOW_SIDE_EFFECTING


def _descriptors(plan, refs, send_sems, recv_sems, arrivals=True):
    x, y, c, _ = _place()
    sends, recvs = plan(refs)
    out = [pltpu.make_async_remote_copy(src_ref=src, dst_ref=dst, send_sem=send_sems.at[k],
                                        recv_sem=recv_sems.at[k], device_id=to, device_id_type=MESH)
           for k, (src, dst, to) in enumerate(sends)]
    if not arrivals:
        return out, []
    inn = [pltpu.make_async_remote_copy(src_ref=land, dst_ref=land, send_sem=send_sems.at[k],
                                        recv_sem=recv_sems.at[k], device_id=(x, y, c), device_id_type=MESH)
           for k, land in enumerate(recvs)]
    return out, inn


def copies_start(name, arrays, n_copies, plan, after):
    na = len(arrays)

    def body(*refs):
        out, _ = _descriptors(plan, refs[:na], refs[na + 1], refs[na + 2], arrivals=False)
        for cp in out:
            cp.start()
        refs[-1][...] = jnp.zeros((8, 128), F32)

    res = pl.pallas_call(
        body, name=name,
        out_shape=(pltpu.SemaphoreType.DMA((n_copies,)), pltpu.SemaphoreType.DMA((n_copies,)),
                   *[pltpu.HBM(a.shape, a.dtype) for a in arrays], jax.ShapeDtypeStruct((8, 128), F32)),
        in_specs=[HBM] * na + [ANY], out_specs=(SEM, SEM, *[HBM] * na, VMEM),
        input_output_aliases={i: i + 2 for i in range(na)},
        compiler_params=pltpu.CompilerParams(has_side_effects=DATAFLOW),
    )(*[pltpu.with_memory_space_constraint(a, pltpu.HBM) for a in arrays], after)
    return res[0], res[1], list(res[2:2 + na]), res[-1]


def copies_wait(name, started, plan, after):
    send_sems, recv_sems, arrays, _ = started
    na = len(arrays)
    after = list(after) if isinstance(after, (list, tuple)) else [after]

    def body(*refs):
        out, inn = _descriptors(plan, refs[:na], refs[na], refs[na + 1])
        for cp in out:
            cp.wait_send()
        for cp in inn:
            cp.wait_recv()
        refs[-1][...] = jnp.zeros((8, 128), F32)

    res = pl.pallas_call(
        body, name=name,
        out_shape=(*[pltpu.HBM(a.shape, a.dtype) for a in arrays], jax.ShapeDtypeStruct((8, 128), F32)),
        in_specs=[HBM] * na + [SEM, SEM] + [ANY] * len(after), out_specs=(*[HBM] * na, VMEM),
        input_output_aliases={i: i for i in range(na)},
        compiler_params=pltpu.CompilerParams(has_side_effects=DATAFLOW),
    )(*arrays, send_sems, recv_sems, *after)
    return list(res[:na]), res[-1]


def _rows_half(r, c):
    return pl.ds(c * (r // 2), r // 2), pl.ds((1 - c) * (r // 2), r // 2)


def plan_gather_ici(refs):
    x, y, c, chips = _place()
    p = 2 * x + y
    sends, recvs = [], []
    for buf in refs:
        mine, _ = _rows_half(buf.shape[1], c)
        for chip in chips:
            sends.append((buf.at[p, mine], buf.at[p, mine], (*chip, c)))
            recvs.append(buf.at[2 * chip[0] + chip[1], mine])
    return sends, recvs


def plan_gather_d2d(refs):
    x, y, c, chips = _place()
    sends, recvs = [], []
    for buf in refs:
        mine, theirs = _rows_half(buf.shape[1], c)
        for chip in chips:
            slot = 2 * chip[0] + chip[1]
            sends.append((buf.at[slot, mine], buf.at[slot, mine], (x, y, 1 - c)))
            recvs.append(buf.at[slot, theirs])
    return sends, recvs


def plan_exchange(refs):
    x, y, c, _ = _place()
    n = len(refs) // 2
    sends, recvs = [], []
    for grad, land in zip(refs[:n], refs[n:]):
        _, theirs = _rows_half(grad.shape[1], c)
        sends.append((grad.at[:, theirs], land, (x, y, 1 - c)))
        recvs.append(land)
    return sends, recvs


def plan_scatter(refs):
    x, y, c, chips = _place()
    n = len(refs) // 2
    sends, recvs = [], []
    for part, land in zip(refs[:n], refs[n:]):
        for j, chip in enumerate(chips):
            sends.append((part.at[2 * chip[0] + chip[1]], land.at[j], (*chip, c)))
            recvs.append(land.at[j])
    return sends, recvs


def plan_share(refs):
    x, y, c, _ = _place()
    sends, recvs = [], []
    for buf in refs:
        mine, theirs = _rows_half(buf.shape[0], c)
        sends.append((buf.at[mine], buf.at[mine], (x, y, 1 - c)))
        recvs.append(buf.at[theirs])
    return sends, recvs


def put_in_slot(w, chip, dtype, name):
    r, c = w.shape
    tr = 256 if r % 256 == 0 else r

    def body(chip_ref, w_ref, o_ref):
        o_ref[...] = w_ref[...].astype(dtype)

    grid_spec = pltpu.PrefetchScalarGridSpec(
        num_scalar_prefetch=1, grid=(r // tr,),
        in_specs=[pl.BlockSpec((tr, c), lambda i, chip_ref: (i, 0))],
        out_specs=pl.BlockSpec((None, tr, c), lambda i, chip_ref: (chip_ref[0], i, 0)))
    return pl.pallas_call(body, name=name, grid_spec=grid_spec,
                          out_shape=jax.ShapeDtypeStruct((4, r, c), dtype), compiler_params=_cp())(chip, w)


def ada_fwd(s_in, ada_w, ada_b, tn):
    nl, d, ws = ada_w.shape

    def body(s_ref, w_ref, b_ref, so_ref, mod_ref):
        s = _silu(s_ref[...])
        so_ref[...] = s
        mod_ref[...] = _dot(s.astype(BF16), w_ref[...].astype(BF16)) + b_ref[...]

    return pl.pallas_call(
        body, name="ada_fwd", grid=(nl, ws // tn),
        in_specs=[pl.BlockSpec((16, d), lambda l, j: (0, 0)),
                  pl.BlockSpec((None, d, tn), lambda l, j: (l, 0, j)),
                  pl.BlockSpec((None, 1, tn), lambda l, j: (l, 0, j))],
        out_specs=[pl.BlockSpec((16, d), lambda l, j: (0, 0)),
                   pl.BlockSpec((None, 16, tn), lambda l, j: (l, 0, j))],
        out_shape=[jax.ShapeDtypeStruct((16, d), F32), jax.ShapeDtypeStruct((nl, 16, ws), F32)],
        compiler_params=_cp(),
    )(s_in, ada_w, ada_b)


def _adamw_math(w, g, m, v):
    m = ADAM_B1 * m + (1.0 - ADAM_B1) * g
    v = ADAM_B2 * v + (1.0 - ADAM_B2) * (g * g)
    m_hat = m / (1.0 - ADAM_B1 ** ADAM_STEP)
    v_hat = v / (1.0 - ADAM_B2 ** ADAM_STEP)
    delta = -ADAM_LR * (m_hat / (jnp.sqrt(v_hat) + ADAM_EPS) + ADAM_WD * w)
    return delta, m, v


def ada_bwd_adamw(s, dm, w, m, v):
    nl, d, ws = w.shape
    tr = 256 if d % 256 == 0 else 128

    def body(s_ref, dm_ref, w_ref, m_ref, v_ref, g_ref, dl_ref, mo_ref, vo_ref, dc_ref):
        dmv = dm_ref[...].astype(BF16)
        wv = w_ref[...]
        g = _dot(s_ref[...].astype(BF16), dmv, TN)
        g_ref[...] = g
        dl_ref[...], mo_ref[...], vo_ref[...] = _adamw_math(wv, g, m_ref[...], v_ref[...])
        dc_ref[...] = _dot(dmv[8:16, :], wv.astype(BF16), NT)

    wblk = pl.BlockSpec((None, tr, ws), lambda l, i: (l, i, 0))
    wshape = jax.ShapeDtypeStruct((nl, d, ws), F32)
    return pl.pallas_call(
        body, name="ada_bwd_adamw", grid=(nl, d // tr),
        in_specs=[pl.BlockSpec((16, tr), lambda l, i: (0, i)),
                  pl.BlockSpec((None, 16, ws), lambda l, i: (l, 0, 0)), wblk, wblk, wblk],
        out_specs=[wblk, wblk, wblk, wblk, pl.BlockSpec((None, 8, tr), lambda l, i: (l, 0, i))],
        out_shape=[wshape, wshape, wshape, wshape, jax.ShapeDtypeStruct((nl, 8, d), F32)],
        compiler_params=_cp(),
    )(s, dm, w, m, v)


def adamw(w, g, m, v, name):
    r, c = w.shape
    tr = 256 if r % 256 == 0 else r

    def body(w_ref, g_ref, m_ref, v_ref, dl_ref, mo_ref, vo_ref):
        dl_ref[...], mo_ref[...], vo_ref[...] = _adamw_math(w_ref[...], g_ref[...], m_ref[...], v_ref[...])

    blk = pl.BlockSpec((tr, c), lambda i: (i, 0))
    shape = jax.ShapeDtypeStruct((r, c), F32)
    return pl.pallas_call(body, name=name, grid=(r // tr,), in_specs=[blk] * 4, out_specs=[blk] * 3,
                          out_shape=[shape] * 3, compiler_params=_cp())(w, g, m, v)


SMALL_ROWS = 24
ROW_MOD = 10


def small_reduce(gathered):
    _, rows, d = gathered.shape

    def body(g_ref, o_ref):
        tot = g_ref[0]
        for b in range(1, 8):
            tot = tot + g_ref[b]
        o_ref[0:rows, :] = tot
        for layer in range(2):
            lat = ROW_MOD + 6 * layer
            o_ref[24 + 3 * layer:27 + 3 * layer, :] = tot[lat:lat + 3, :] + tot[lat + 3:lat + 6, :]
        o_ref[30:32, :] = jnp.zeros((2, d), F32)

    return pl.pallas_call(body, name="small_reduce", in_specs=[VMEM], out_specs=VMEM,
                          out_shape=jax.ShapeDtypeStruct((32, d), F32), compiler_params=_cp())(gathered)


def lb_logits_grad(lbl, dlb):
    _, _, n = lbl.shape

    def body(l_ref, d_ref, o_ref):
        for dr in range(2):
            _, (p0, p1, p2) = _lower_bound(l_ref, dr)
            dv = d_ref[dr:dr + 1, :]
            o_ref[dr, 0:1, :] = p0 * p2 * dv
            o_ref[dr, 1:2, :] = p1 * p2 * dv
            o_ref[dr, 2:3, :] = -p2 * (p0 + p1) * dv

    return pl.pallas_call(body, name="lb_logits_grad", in_specs=[VMEM, VMEM], out_specs=VMEM,
                          out_shape=jax.ShapeDtypeStruct((2, 3, n), F32), compiler_params=_cp())(lbl, dlb)


def c_ctx_grad(parts, c_ctx):
    d = c_ctx.shape[1]

    def body(p_ref, c_ref, o_ref):
        tot = p_ref[0, 0:1, :]
        for chip in range(1, 4):
            tot = tot + p_ref[2 * chip, 0:1, :]
        o_ref[...] = tot * _dsilu(c_ref[...])

    return pl.pallas_call(body, name="c_ctx_grad", in_specs=[VMEM, VMEM], out_specs=VMEM,
                          out_shape=jax.ShapeDtypeStruct((1, d), F32), compiler_params=_cp())(parts, c_ctx)


def _reduce_scatter(grads, core, chip_core):
    got = exchange_halves(grads)
    sums = [pair_sum(g, r, core) for g, r in zip(grads, got)]
    recv = scatter_to_owners([sb for _, sb in sums])
    reduced = [owner_sum(s, r, chip_core) for (s, _), r in zip(sums, recv)]
    return share_halves(reduced)


def kernel(x, c, ctx, c_ctx, ada_w, ada_b, pre_g, post_g, ev_w_in, ev_pool_w, ev_pool_scale, ev_conv_w, ev_conv_b, ev_w_out, od_w_in, od_onorm_g, od_w_out, lb_logits, loss_target, m_c_ctx, m_ada_w, m_ada_b, m_pre_g, m_post_g, m_ev_w_in, m_ev_pool_w, m_ev_pool_scale, m_ev_conv_w, m_ev_conv_b, m_ev_w_out, m_od_w_in, m_od_onorm_g, m_od_w_out, m_lb_logits, v_c_ctx, v_ada_w, v_ada_b, v_pre_g, v_post_g, v_ev_w_in, v_ev_pool_w, v_ev_pool_scale, v_ev_conv_w, v_ev_conv_b, v_ev_w_out, v_od_w_in, v_od_onorm_g, v_od_w_out, v_lb_logits):
    _, seq, d = x.shape
    cx = ctx.shape[1]
    t = cx + seq
    half_d = d // 2
    g = half_d // N_POOL
    tn = d // 4
    xi, yi, ci = lax.axis_index("x"), lax.axis_index("y"), lax.axis_index("c")
    chip = 2 * xi + yi
    me = 2 * chip + ci
    core_arr = jnp.reshape(ci, (1,)).astype(jnp.int32)
    chip_arr = jnp.reshape(chip, (1,)).astype(jnp.int32)
    chip_core_arr = jnp.stack([chip, ci]).astype(jnp.int32)

    pad = lambda a, rows: jnp.concatenate([a, jnp.zeros((rows - a.shape[0], g), F32)], axis=0)
    small = jnp.concatenate([
        ev_pool_w.reshape(g, g), pad(ev_conv_w.reshape(3, g), 8), pad(od_onorm_g.reshape(2, g), 8),
        pad(lb_logits.reshape(12, g), 16)], axis=0)
    ev_in_g, ev_out_g, small_g, ev_done = allgather_shards([
        put_in_slot(ev_w_in[0], chip_arr, BF16, "cast_ev_w_in"),
        put_in_slot(ev_w_out[0], chip_arr, BF16, "cast_ev_w_out"),
        put_in_slot(small, chip_arr, F32, "place_small")])
    ev_out3 = ev_out_g.reshape(1, d, d)
    pool_w_full = small_g[:, :g].reshape(4, N_POOL, g // 4, g).transpose(1, 0, 2, 3).reshape(N_POOL, g, g)
    conv_w_full = small_g[:, g:g + 3].transpose(1, 0, 2).reshape(3, half_d)
    onorm_full = small_g[:, g + 8:g + 10].reshape(1, d)
    lbl_full = small_g[:, g + 16:g + 28].reshape(4, 2, 3, 2 * g).transpose(1, 2, 0, 3).reshape(2, 3, d)

    c_rows = jnp.concatenate([c + ev_done[0:1, 0:1], jnp.zeros((7, d), F32)], axis=0)
    c_all = allgather8(c_rows, "allgather_c")[:, 0, :]
    s_in = jnp.concatenate([c_all, c_ctx.reshape(1, d), jnp.zeros((7, d), F32)], axis=0)
    ws_ada = ada_w.shape[2]
    ada_b_mine = lax.dynamic_slice(ada_b, (0, chip * ws_ada), (2, ws_ada)).reshape(2, 1, ws_ada)
    s_act, mod_mine = ada_fwd(s_in, ada_w, ada_b_mine, tn)
    mod_all = allgather8(mod_mine.reshape(32, ws_ada), "allgather_mod")
    od_ici = copies_start("gather_od_ici_start", [
        put_in_slot(od_w_in[0], chip_arr, BF16, "cast_od_w_in"),
        put_in_slot(od_w_out[0], chip_arr, BF16, "cast_od_w_out")], 6, plan_gather_ici, mod_all)
    mod_full = mod_all[0::2].reshape(4, 2, 16, ws_ada).transpose(1, 2, 0, 3).reshape(2, 16, 3 * d)
    mod_lat = lax.dynamic_slice(mod_full, (0, me, 0), (2, 1, 3 * d))
    mods = jnp.concatenate([mod_full[:, 8:9], mod_lat], axis=1)
    shift, scale, gate = mods[:, :, :d], mods[:, :, d:2 * d], mods[:, :, 2 * d:]

    xs = jnp.concatenate([ctx[0], x[0]], axis=0)

    h0 = normmod_fwd(xs, pre_g[0:1] + od_ici[3][0:1, 0:1], shift[0], scale[0], cx)
    z0 = mm_nn(h0, ev_in_g, half_d, tn, "mm_ev_in")
    u_a = mix_a_fwd(z0, pool_w_full, ev_pool_scale, cx)
    u_b = mix_b_fwd(z0, conv_w_full, ev_conv_b, cx)
    u = jnp.concatenate([u_a, u_b], axis=1)
    y0 = mm_nn(u, ev_out3, d, tn, "mm_ev_out")[0]
    xs1 = post_fwd(xs, y0, post_g[0:1], gate[0], cx)
    od_d2d = copies_start("gather_od_d2d_start",
                          copies_wait("gather_od_ici_wait", od_ici, plan_gather_ici, xs1)[0],
                          6, plan_gather_d2d, xs1)
    (od_in_g, od_out_g), _ = copies_wait("gather_od_d2d_wait", od_d2d, plan_gather_d2d, od_d2d[3])
    od_out3 = od_out_g.reshape(1, d, d)

    h1 = normmod_fwd(xs1, pre_g[1:2], shift[1], scale[1], cx)
    z1 = mm_nn(h1, od_in_g, d, tn, "mm_od_in")
    o1, r1 = hgrn_fwd(z1, lbl_full, onorm_full, cx)
    y1 = mm_nn(r1, od_out3, d, tn, "mm_od_out")[0]
    sq, dx2 = post_loss(xs1, y1, post_g[1:2], gate[1], loss_target[0], cx)
    loss = lax.psum(sq[0, 0] * (0.5 / d), ("x", "y", "c"))

    dy1, dgate1, dpost1 = post_bwd(dx2, y1, post_g[1:2], gate[1], cx, True)
    dr1 = mm_nt(dy1[None], None, od_out3, tn, "mm_od_out_dx")
    g_od_out = mm_tn(r1, dy1[None], None, d, tn, "mm_od_out_dw")
    dz1, donorm, dlb = hgrn_bwd(z1, lbl_full, onorm_full, o1, dr1, cx)
    dh1 = mm_nt(dz1, None, od_in_g, tn, "mm_od_in_dx")
    g_od_in = mm_tn(h1, dz1, None, od_in_g.shape[2], tn, "mm_od_in_dw")
    dxs1, dpre1, dshift1, dscale1 = normmod_bwd(xs1, dh1, pre_g[1:2], scale[1], dx2, cx, True)

    od_grads = [g_od_in, g_od_out.reshape(4, d // 4, d)]
    half_zone = lambda a, lead, dt: lax.empty((lead, a.shape[1] // 2, a.shape[2]), dt)
    od_ex = copies_start("reduce_od_exchange_start", od_grads + [half_zone(a, 4, a.dtype) for a in od_grads],
                         2, plan_exchange, dxs1)

    dy0, dgate0, dpost0 = post_bwd(dxs1, y0, post_g[0:1] + od_ex[3][0:1, 0:1], gate[0], cx, False)
    du = mm_nt(dy0[None], None, ev_out3, tn, "mm_ev_out_dx")
    g_ev_out = mm_tn(u, dy0[None], None, d, tn, "mm_ev_out_dw")
    od_got, _ = copies_wait("reduce_od_exchange_wait", od_ex, plan_exchange, g_ev_out)
    od_sums = [pair_sum(od_got[i], od_got[2 + i], chip_core_arr) for i in range(2)]
    od_sc = copies_start("reduce_od_scatter_start",
                         [sb for _, sb in od_sums] + [half_zone(a, 3, BF16) for a in od_grads],
                         6, plan_scatter, du)
    dz0a, g_pool_w, dpool_scale = mix_a_bwd(z0, du, pool_w_full, ev_pool_scale + od_sc[3][0:1, 0:1], cx)
    dz0b, dconv_w, dconv_b = mix_b_bwd(z0, du, conv_w_full, ev_conv_b + od_sc[3][0:1, 0:1], cx)
    g_ev_in = mm_tn(h0, dz0a, dz0b, ev_in_g.shape[2], tn, "mm_ev_in_dw")
    ev_grads = [g_ev_in, g_ev_out.reshape(4, d // 4, d), g_pool_w.reshape(4, g, g)]
    ev_ex = copies_start("reduce_ev_exchange_start", ev_grads + [half_zone(a, 4, a.dtype) for a in ev_grads],
                         3, plan_exchange, dpool_scale)
    dh0 = mm_nt(dz0a, dz0b, ev_in_g, tn, "mm_ev_in_dx")
    dxs0, dpre0, dshift0, dscale0 = normmod_bwd(xs, dh0, pre_g[0:1] + ev_ex[3][0:1, 0:1], scale[0], dxs1,
                                                cx, False, True)
    grad_x = dxs0[None]
    ev_got, _ = copies_wait("reduce_ev_exchange_wait", ev_ex, plan_exchange, dxs0)
    ev_sums = [pair_sum(ev_got[i], ev_got[3 + i], chip_core_arr) for i in range(3)]
    od_recv, _ = copies_wait("reduce_od_scatter_wait", od_sc, plan_scatter, dxs0)

    zrow = jnp.zeros((1, d), F32)
    small_rows = jnp.concatenate([
        dpre0, dpre1, dpost0, dpost1,
        jnp.concatenate([dpool_scale, dconv_b], axis=1),
        jnp.concatenate([dconv_w.reshape(1, 3 * half_d), jnp.zeros((1, half_d), F32)], axis=1).reshape(2, d),
        donorm, dlb,
        dshift0[1:2], dscale0[1:2], dgate0[1:2], dshift0[0:1], dscale0[0:1], dgate0[0:1],
        dshift1[1:2], dscale1[1:2], dgate1[1:2], dshift1[0:1], dscale1[0:1], zrow,
        zrow, zrow], axis=0)
    small_all = allgather8(small_rows, "allgather_small")
    ev_sc = copies_start("reduce_ev_scatter_start",
                         [sb for _, sb in ev_sums] + [half_zone(a, 3, BF16) for a in ev_grads],
                         9, plan_scatter, small_all)
    od_sh = copies_start("reduce_od_share_start",
                         [owner_sum(od_sums[i][0], od_recv[2 + i], chip_core_arr) for i in range(2)],
                         2, plan_share, ev_sc[3])
    tot = small_reduce(small_all + ev_sc[3][0:1, 0:1])

    dm_rows = []
    for layer in range(2):
        lat = ROW_MOD + 6 * layer
        dm_lat = small_all[:, lat:lat + 3].reshape(8, 3 * d)
        dm_ctx = tot[lat + 3:lat + 6].reshape(1, 3 * d)
        dm_rows.append(jnp.concatenate([dm_lat, dm_ctx, jnp.zeros((7, 3 * d), F32)], axis=0))
    dm_full = jnp.stack(dm_rows)
    dm_mine = lax.dynamic_slice(dm_full, (0, 0, chip * ws_ada), (2, 16, ws_ada))

    def step(w, gr, m, v, name):
        shape = w.shape
        cols = shape[-1]
        two_d = lambda a: a.reshape(-1, cols)
        dl, mo, vo = adamw(two_d(w), two_d(gr), two_d(m), two_d(v), "adamw_" + name)
        return dl.reshape(shape), mo.reshape(shape), vo.reshape(shape)

    grad_ada_b = tot[24:30].reshape(2, 3 * d)
    grad_pre_g = tot[0:2]
    grad_post_g = tot[2:4]
    grad_ev_pool_scale = tot[4:5, :half_d]
    grad_ev_conv_b = tot[4:5, half_d:]
    conv_w_tot = tot[5:7].reshape(1, 2 * d)[:, :3 * half_d].reshape(3, N_POOL, g)
    grad_ev_conv_w = lax.dynamic_slice(conv_w_tot, (0, chip, 0), (3, 1, g)).reshape(1, 3, g)
    grad_od_onorm_g = lax.dynamic_slice(tot[7:8], (0, chip * 2 * g), (1, 2 * g))
    dlb_mine = lax.dynamic_slice(tot[8:10], (0, chip * 2 * g), (2, 2 * g))
    grad_lb_logits = lb_logits_grad(lb_logits, dlb_mine)
    upd = {
        "ada_b": step(ada_b, grad_ada_b, m_ada_b, v_ada_b, "ada_b"),
        "pre_g": step(pre_g, grad_pre_g, m_pre_g, v_pre_g, "pre_g"),
        "post_g": step(post_g, grad_post_g, m_post_g, v_post_g, "post_g"),
        "ev_pool_scale": step(ev_pool_scale, grad_ev_pool_scale, m_ev_pool_scale, v_ev_pool_scale, "ev_pool_scale"),
        "ev_conv_w": step(ev_conv_w, grad_ev_conv_w, m_ev_conv_w, v_ev_conv_w, "ev_conv_w"),
        "ev_conv_b": step(ev_conv_b, grad_ev_conv_b, m_ev_conv_b, v_ev_conv_b, "ev_conv_b"),
        "od_onorm_g": step(od_onorm_g, grad_od_onorm_g, m_od_onorm_g, v_od_onorm_g, "od_onorm_g"),
        "lb_logits": step(lb_logits, grad_lb_logits, m_lb_logits, v_lb_logits, "lb_logits"),
    }
    grad_ada_w, delta_ada_w, new_m_ada_w, new_v_ada_w, dctx_part = ada_bwd_adamw(
        s_act, dm_mine, ada_w, m_ada_w, v_ada_w)
    upd["ada_w"] = (delta_ada_w, new_m_ada_w, new_v_ada_w)
    (grad_od_w_in, grad_od_w_out), _ = copies_wait("reduce_od_share_wait", od_sh, plan_share, ev_sc[3])
    grad_od_w_in, grad_od_w_out = grad_od_w_in[None], grad_od_w_out[None]
    upd["od_w_in"] = step(od_w_in, grad_od_w_in, m_od_w_in, v_od_w_in, "od_w_in")
    upd["od_w_out"] = step(od_w_out, grad_od_w_out, m_od_w_out, v_od_w_out, "od_w_out")
    done_behind = [dctx_part] + [upd[k][0] for k in (
        "od_w_in", "od_w_out", "ada_b", "pre_g", "post_g", "ev_pool_scale", "ev_conv_w", "ev_conv_b",
        "od_onorm_g", "lb_logits")]
    ev_recv, ev_landed = copies_wait("reduce_ev_scatter_wait", ev_sc, plan_scatter, done_behind)
    grad_ev_w_in, grad_ev_w_out, grad_pool_w = share_halves(
        [owner_sum(ev_sums[i][0], ev_recv[3 + i], chip_core_arr) for i in range(3)])
    dctx_all = allgather8(dctx_part[0] + dctx_part[1] + ev_landed[0:1, 0:1], "allgather_dctx")
    grad_c_ctx = c_ctx_grad(dctx_all, c_ctx.reshape(1, d)).reshape(d)
    grad_ev_w_in, grad_ev_w_out = grad_ev_w_in[None], grad_ev_w_out[None]
    grad_ev_pool_w = grad_pool_w.reshape(1, N_POOL, g // 4, g)
    upd["c_ctx"] = step(c_ctx, grad_c_ctx, m_c_ctx, v_c_ctx, "c_ctx")
    upd["ev_w_in"] = step(ev_w_in, grad_ev_w_in, m_ev_w_in, v_ev_w_in, "ev_w_in")
    upd["ev_pool_w"] = step(ev_pool_w, grad_ev_pool_w, m_ev_pool_w, v_ev_pool_w, "ev_pool_w")
    upd["ev_w_out"] = step(ev_w_out, grad_ev_w_out, m_ev_w_out, v_ev_w_out, "ev_w_out")
    names = ["c_ctx", "ada_w", "ada_b", "pre_g", "post_g", "ev_w_in", "ev_pool_w", "ev_pool_scale",
             "ev_conv_w", "ev_conv_b", "ev_w_out", "od_w_in", "od_onorm_g", "od_w_out", "lb_logits"]
    grads = [grad_c_ctx, grad_ada_w, grad_ada_b, grad_pre_g, grad_post_g, grad_ev_w_in, grad_ev_pool_w,
             grad_ev_pool_scale, grad_ev_conv_w, grad_ev_conv_b, grad_ev_w_out, grad_od_w_in,
             grad_od_onorm_g, grad_od_w_out, grad_lb_logits]
    return (loss, grad_x, *grads, *[upd[k][0] for k in names], *[upd[k][1] for k in names],
            *[upd[k][2] for k in names])
```

```python
import functools

import jax
import jax.numpy as jnp
from jax import lax
from jax.experimental import pallas as pl
from jax.experimental.pallas import tpu as pltpu

EPS = 1e-6
GRID_W_LOG2 = 6
CHUNK = 64
HEAD = 128
N_POOL = 4
ADAM_LR, ADAM_B1, ADAM_B2, ADAM_EPS, ADAM_WD, ADAM_STEP = 0.001, 0.9, 0.999, 1e-08, 0.01, 10
VMEM_LIMIT = 56 * 1024 * 1024
MESH = pl.DeviceIdType.MESH
F32, BF16 = jnp.float32, jnp.bfloat16
ANY = pl.BlockSpec(memory_space=pl.ANY)
VMEM = pl.BlockSpec(memory_space=pltpu.VMEM)


def _cp(**kw):
    return pltpu.CompilerParams(vmem_limit_bytes=VMEM_LIMIT, **kw)


def _silu(x):
    return x * jax.nn.sigmoid(x)


def _dsilu(x):
    s = jax.nn.sigmoid(x)
    return s * (1.0 + x * (1.0 - s))


def _dot(a, b, dims=((1,), (0,)), precision=None):
    return lax.dot_general(a, b, (dims, ((), ())), preferred_element_type=F32, precision=precision)


NN = ((1,), (0,))
NT = ((1,), (1,))
TN = ((0,), (0,))


def _row_block(cx):
    return 256 if cx % 256 == 0 else 128


def normmod_fwd(xs, g, shift, scale, cx):
    t, d = xs.shape
    tm = _row_block(cx)
    nctx = cx // tm

    def body(x_ref, g_ref, sh_ref, sc_ref, h_ref):
        is_ctx = pl.program_id(0) < nctx
        x = x_ref[...]
        rstd = lax.rsqrt(jnp.mean(x * x, axis=-1, keepdims=True) + EPS)
        sc = jnp.where(is_ctx, sc_ref[0:1, :], sc_ref[1:2, :])
        sh = jnp.where(is_ctx, sh_ref[0:1, :], sh_ref[1:2, :])
        h_ref[...] = ((x * rstd) * g_ref[...] * (1.0 + sc) + sh).astype(BF16)

    row = pl.BlockSpec((tm, d), lambda i: (i, 0))
    vec = lambda r: pl.BlockSpec((r, d), lambda i: (0, 0))
    return pl.pallas_call(
        body, name="normmod_fwd", grid=(t // tm,),
        in_specs=[row, vec(1), vec(2), vec(2)], out_specs=row,
        out_shape=jax.ShapeDtypeStruct((t, d), BF16), compiler_params=_cp(),
    )(xs, g, shift, scale)


def normmod_bwd(xs, dh, g, scale, dres, cx, res_is_latent_only, dx_latent_only=False):
    t, d = xs.shape
    tm = _row_block(cx)
    nctx = cx // tm

    def body(x_ref, dh_ref, g_ref, sc_ref, dres_ref, dx_ref, dg_ref, dsh_ref, dsc_ref):
        i = pl.program_id(0)
        is_ctx = i < nctx

        @pl.when(i == 0)
        def _():
            dg_ref[...] = jnp.zeros_like(dg_ref)
            dsh_ref[...] = jnp.zeros_like(dsh_ref)
            dsc_ref[...] = jnp.zeros_like(dsc_ref)

        x = x_ref[...]
        dh = dh_ref[...]
        gv = g_ref[...]
        rstd = lax.rsqrt(jnp.mean(x * x, axis=-1, keepdims=True) + EPS)
        xhat = x * rstd
        sc = jnp.where(is_ctx, sc_ref[0:1, :], sc_ref[1:2, :])
        dsh = jnp.sum(dh, axis=0, keepdims=True)
        dhx = dh * xhat
        dsc = jnp.sum(dhx * gv, axis=0, keepdims=True)
        dg_ref[...] += jnp.sum(dhx * (1.0 + sc), axis=0, keepdims=True)
        zero = jnp.zeros_like(dsh)
        dsh_ref[0:1, :] += jnp.where(is_ctx, dsh, zero)
        dsh_ref[1:2, :] += jnp.where(is_ctx, zero, dsh)
        dsc_ref[0:1, :] += jnp.where(is_ctx, dsc, zero)
        dsc_ref[1:2, :] += jnp.where(is_ctx, zero, dsc)
        dxhat = dh * (gv * (1.0 + sc))
        dx = rstd * (dxhat - xhat * jnp.mean(dxhat * xhat, axis=-1, keepdims=True))
        res = dres_ref[...]
        if res_is_latent_only:
            res = jnp.where(is_ctx, jnp.zeros_like(res), res)
        dx_ref[...] = dx + res

    row = pl.BlockSpec((tm, d), lambda i: (i, 0))
    if res_is_latent_only:
        res_spec = pl.BlockSpec((tm, d), lambda i: (jnp.maximum(i - nctx, 0), 0))
    else:
        res_spec = row
    vec = lambda r: pl.BlockSpec((r, d), lambda i: (0, 0))
    dx_spec = pl.BlockSpec((tm, d), lambda i: (jnp.maximum(i - nctx, 0), 0)) if dx_latent_only else row
    return pl.pallas_call(
        body, name="normmod_bwd", grid=(t // tm,),
        in_specs=[row, row, vec(1), vec(2), res_spec],
        out_specs=[dx_spec, vec(1), vec(2), vec(2)],
        out_shape=[jax.ShapeDtypeStruct((t - cx if dx_latent_only else t, d), F32), jax.ShapeDtypeStruct((1, d), F32),
                   jax.ShapeDtypeStruct((2, d), F32), jax.ShapeDtypeStruct((2, d), F32)],
        compiler_params=_cp(),
    )(xs, dh, g, scale, dres)


def post_fwd(xs, y, pg, gate, cx):
    t, d = xs.shape
    tm = _row_block(cx)
    nctx = cx // tm

    def body(x_ref, y_ref, pg_ref, gate_ref, o_ref):
        is_ctx = pl.program_id(0) < nctx
        y = y_ref[...]
        rstd = lax.rsqrt(jnp.mean(y * y, axis=-1, keepdims=True) + EPS)
        gt = jnp.where(is_ctx, gate_ref[0:1, :], gate_ref[1:2, :])
        o_ref[...] = x_ref[...] + gt * ((y * rstd) * pg_ref[...])

    row = pl.BlockSpec((tm, d), lambda i: (i, 0))
    vec = lambda r: pl.BlockSpec((r, d), lambda i: (0, 0))
    return pl.pallas_call(
        body, name="post_fwd", grid=(t // tm,),
        in_specs=[row, row, vec(1), vec(2)], out_specs=row,
        out_shape=jax.ShapeDtypeStruct((t, d), F32), compiler_params=_cp(),
    )(xs, y, pg, gate)


def post_loss(xs, y, pg, gate, target, cx):
    t, d = xs.shape
    n = y.shape[0]
    tm = _row_block(cx)
    nctx = cx // tm

    def body(x_ref, y_ref, pg_ref, gate_ref, tgt_ref, sq_ref, dx_ref):
        @pl.when(pl.program_id(0) == 0)
        def _():
            sq_ref[...] = jnp.zeros_like(sq_ref)

        y = y_ref[...]
        rstd = lax.rsqrt(jnp.mean(y * y, axis=-1, keepdims=True) + EPS)
        x2 = x_ref[...] + gate_ref[1:2, :] * ((y * rstd) * pg_ref[...])
        err = x2 - tgt_ref[...]
        sq_ref[...] += jnp.sum(err * err)
        dx_ref[...] = err * (1.0 / d)

    row = pl.BlockSpec((tm, d), lambda i: (i, 0))
    xrow = pl.BlockSpec((tm, d), lambda i: (i + nctx, 0))
    vec = lambda r: pl.BlockSpec((r, d), lambda i: (0, 0))
    return pl.pallas_call(
        body, name="post_loss", grid=(n // tm,),
        in_specs=[xrow, row, vec(1), vec(2), row],
        out_specs=[pl.BlockSpec((8, 128), lambda i: (0, 0)), row],
        out_shape=[jax.ShapeDtypeStruct((8, 128), F32), jax.ShapeDtypeStruct((n, d), F32)],
        compiler_params=_cp(),
    )(xs, y, pg, gate, target)


def post_bwd(dxo, y, pg, gate, cx, latent_only):
    m, d = y.shape
    tm = _row_block(cx)
    nctx = 0 if latent_only else cx // tm

    def body(dx_ref, y_ref, pg_ref, gate_ref, dy_ref, dgate_ref, dpg_ref):
        i = pl.program_id(0)
        is_ctx = i < nctx

        @pl.when(i == 0)
        def _():
            dgate_ref[...] = jnp.zeros_like(dgate_ref)
            dpg_ref[...] = jnp.zeros_like(dpg_ref)

        y = y_ref[...]
        dx = dx_ref[...]
        pgv = pg_ref[...]
        rstd = lax.rsqrt(jnp.mean(y * y, axis=-1, keepdims=True) + EPS)
        yhat = y * rstd
        gt = jnp.where(is_ctx, gate_ref[0:1, :], gate_ref[1:2, :])
        dxy = dx * yhat
        dgt = jnp.sum(dxy * pgv, axis=0, keepdims=True)
        zero = jnp.zeros_like(dgt)
        dgate_ref[0:1, :] += jnp.where(is_ctx, dgt, zero)
        dgate_ref[1:2, :] += jnp.where(is_ctx, zero, dgt)
        dpg_ref[...] += jnp.sum(dxy * gt, axis=0, keepdims=True)
        dyhat = dx * (gt * pgv)
        dy = rstd * (dyhat - yhat * jnp.mean(dyhat * yhat, axis=-1, keepdims=True))
        dy_ref[...] = dy.astype(BF16)

    row = pl.BlockSpec((tm, d), lambda i: (i, 0))
    vec = lambda r: pl.BlockSpec((r, d), lambda i: (0, 0))
    return pl.pallas_call(
        body, name="post_bwd", grid=(m // tm,),
        in_specs=[row, row, vec(1), vec(2)], out_specs=[row, vec(2), vec(1)],
        out_shape=[jax.ShapeDtypeStruct((m, d), BF16), jax.ShapeDtypeStruct((2, d), F32),
                   jax.ShapeDtypeStruct((1, d), F32)],
        compiler_params=_cp(),
    )(dxo, y, pg, gate)


def _split_rows(m):
    for cand in (1024, 768, 512, 384, 256, 128):
        if m % cand == 0 and m // cand >= 2:
            return cand
    return m


def mm_nn(a, w3, sec, tn, name):
    m, k = a.shape
    q, _, ws = w3.shape
    n = q * ws
    tpq, tps = ws // tn, sec // tn
    tm = next(c for c in (768, 512, 256, 128) if m % c == 0)

    def body(a_ref, w_ref, o_ref):
        w = w_ref[...]

        def step(i, carry):
            rows = pl.ds(pl.multiple_of(i * tm, tm), tm)
            o_ref[rows, :] = _dot(a_ref[rows, :], w)
            return carry

        lax.fori_loop(0, m // tm, step, 0)

    return pl.pallas_call(
        body, name=name, grid=(n // tn,),
        in_specs=[pl.BlockSpec((m, k), lambda j: (0, 0)),
                  pl.BlockSpec((None, k, tn), lambda j: (j // tpq, 0, j % tpq))],
        out_specs=pl.BlockSpec((None, m, tn), lambda j: (j // tps, 0, j % tps)),
        out_shape=jax.ShapeDtypeStruct((n // sec, m, sec), F32), compiler_params=_cp(),
    )(a, w3)


def _two_stacks(a3, b3, tn):
    sec = a3.shape[2]
    tps = sec // tn
    n1 = a3.shape[0] * tps
    first = lambda j: (jnp.minimum(j, n1 - 1) // tps, jnp.minimum(j, n1 - 1) % tps)
    second = lambda j: (jnp.maximum(j - n1, 0) // tps, jnp.maximum(j - n1, 0) % tps)
    return n1, first, second


def mm_nt(a3, b3, w3, tn, name):
    if b3 is None:
        b3 = a3
    _, m, sec = a3.shape
    q, k, ws = w3.shape
    n = q * ws
    tpq = ws // tn
    mb = _split_rows(m)
    n1, first, second = _two_stacks(a3, b3, tn)

    def body(a_ref, b_ref, w_ref, o_ref):
        j = pl.program_id(1)

        @pl.when(j == 0)
        def _():
            o_ref[...] = jnp.zeros_like(o_ref)

        @pl.when(j < n1)
        def _():
            o_ref[...] += _dot(a_ref[...], w_ref[...], NT)

        @pl.when(j >= n1)
        def _():
            o_ref[...] += _dot(b_ref[...], w_ref[...], NT)

    return pl.pallas_call(
        body, name=name, grid=(m // mb, n // tn),
        in_specs=[pl.BlockSpec((None, mb, tn), lambda i, j: (first(j)[0], i, first(j)[1])),
                  pl.BlockSpec((None, mb, tn), lambda i, j: (second(j)[0], i, second(j)[1])),
                  pl.BlockSpec((None, k, tn), lambda i, j: (j // tpq, 0, j % tpq))],
        out_specs=pl.BlockSpec((mb, k), lambda i, j: (i, 0)),
        out_shape=jax.ShapeDtypeStruct((m, k), F32), compiler_params=_cp(),
    )(a3, b3, w3)


def mm_tn(a, b3, c3, ws, tn, name):
    m, k = a.shape
    sec = b3.shape[2]
    n = (b3.shape[0] + (0 if c3 is None else c3.shape[0])) * sec
    if c3 is None:
        c3 = b3
    tpq = ws // tn
    kb = 256 if k % 256 == 0 else 128
    n1, first, second = _two_stacks(b3, c3, tn)

    def body(a_ref, b_ref, c_ref, o_ref):
        def product(rhs_ref):
            rhs = rhs_ref[...]
            for i in range(k // kb):
                o_ref[i * kb:(i + 1) * kb, :] = _dot(a_ref[:, i * kb:(i + 1) * kb], rhs, TN).astype(BF16)

        @pl.when(pl.program_id(0) < n1)
        def _():
            product(b_ref)

        @pl.when(pl.program_id(0) >= n1)
        def _():
            product(c_ref)

    return pl.pallas_call(
        body, name=name, grid=(n // tn,),
        in_specs=[pl.BlockSpec((m, k), lambda j: (0, 0)),
                  pl.BlockSpec((None, m, tn), lambda j: (first(j)[0], 0, first(j)[1])),
                  pl.BlockSpec((None, m, tn), lambda j: (second(j)[0], 0, second(j)[1]))],
        out_specs=pl.BlockSpec((None, k, tn), lambda j: (j // tpq, 0, j % tpq)),
        out_shape=jax.ShapeDtypeStruct((n // ws, k, ws), BF16), compiler_params=_cp(),
    )(a, b3, c3)


POOL_REACH = 8 << GRID_W_LOG2


def _token_parts(tok, cx):
    lat = tok - cx
    return tok < cx, lat >> GRID_W_LOG2, lat & ((1 << GRID_W_LOG2) - 1)


def _pool_mask(gi, row0, col0, tm, ncols, cx, transposed):
    half = jnp.left_shift(1, gi)
    r = lax.broadcasted_iota(jnp.int32, (tm, 1), 0) + row0
    c = lax.broadcasted_iota(jnp.int32, (1, ncols), 1) + col0
    out_tok, src_tok = (c, r) if transposed else (r, c)
    o_ctx, o_row, o_col = _token_parts(out_tok, cx)
    s_ctx, s_row, s_col = _token_parts(src_tok, cx)

    def inside(o, s):
        return (s >= o - half) & (s <= o + half - 1)

    ctx_hit = o_ctx & s_ctx & inside(out_tok, src_tok)
    lat_hit = (~o_ctx) & (~s_ctx) & inside(o_row, s_row) & inside(o_col, s_col)
    return jnp.where(ctx_hit | lat_hit, 1.0, 0.0).astype(BF16)


def _pool_inv_count(gi, row0, tm, cx, seq):
    half = jnp.left_shift(1, gi)
    r = lax.broadcasted_iota(jnp.int32, (tm, 1), 0) + row0
    is_ctx, row, col = _token_parts(r, cx)

    def count(pos, size):
        return jnp.minimum(pos + half - 1, size - 1) - jnp.maximum(pos - half, 0) + 1

    cnt = jnp.where(is_ctx, count(r, cx), count(row, seq >> GRID_W_LOG2) * count(col, 1 << GRID_W_LOG2))
    return 1.0 / cnt.astype(F32)


def _lat_band(tm):
    side = POOL_REACH // tm
    return side, 2 * side + 1


def _lat_mask(gi, tm, cx, transposed):
    side, band = _lat_band(tm)
    return _pool_mask(gi, cx + side * tm, cx, tm, band * tm, cx, transposed)


def _store_padded_lat(dst_ref, lat, tm):
    side, _ = _lat_band(tm)
    seq = lat.shape[0]
    zeros = jnp.zeros((side * tm, lat.shape[1]), dst_ref.dtype)
    dst_ref[0:side * tm, :] = zeros
    dst_ref[side * tm + seq:, :] = zeros
    dst_ref[side * tm:side * tm + seq, :] = lat.astype(dst_ref.dtype)


def mix_a_fwd(z0, pool_w, pool_scale, cx):
    _, t, half_d = z0.shape
    g = half_d // N_POOL
    seq = t - cx
    tm = _row_block(cx)
    side, band = _lat_band(tm)

    def body(v_ref, ag_ref, w_ref, sc_ref, u_ref, vlat_ref, mask_ref):
        gi = pl.program_id(0)
        w = w_ref[...].astype(BF16)
        sc = sc_ref[...]
        _store_padded_lat(vlat_ref, v_ref[cx:, :], tm)
        mask_ref[...] = _lat_mask(gi, tm, cx, False)

        def finish(row0, window_sum):
            rows = pl.ds(row0, tm)
            pooled = window_sum * _pool_inv_count(gi, row0, tm, cx, seq) - v_ref[rows, :]
            mixed = _dot(pooled.astype(BF16), w) * sc
            u_ref[rows, :] = (mixed * _silu(ag_ref[rows, :])).astype(BF16)

        vctx = v_ref[0:cx, :].astype(BF16)
        for i in range(cx // tm):
            finish(i * tm, _dot(_pool_mask(gi, i * tm, 0, tm, cx, cx, False), vctx))

        def step(j, carry):
            src = vlat_ref[pl.ds(pl.multiple_of(j * tm, tm), band * tm), :]
            finish(pl.multiple_of(cx + j * tm, tm), _dot(mask_ref[...], src))
            return carry

        lax.fori_loop(0, seq // tm, step, 0)

    sec = lambda s: pl.BlockSpec((None, t, g), lambda j: (s, 0, j))
    return pl.pallas_call(
        body, name="mix_a_fwd", grid=(N_POOL,),
        in_specs=[sec(0), sec(1), pl.BlockSpec((None, g, g), lambda j: (j, 0, 0)),
                  pl.BlockSpec((1, g), lambda j: (0, j))],
        out_specs=pl.BlockSpec((t, g), lambda j: (0, j)),
        out_shape=jax.ShapeDtypeStruct((t, half_d), BF16),
        scratch_shapes=[pltpu.VMEM((seq + 2 * side * tm, g), BF16), pltpu.VMEM((tm, band * tm), BF16)],
        compiler_params=_cp(),
    )(z0, z0, pool_w, pool_scale)


def mix_a_bwd(z0, du, pool_w, pool_scale, cx):
    _, t, half_d = z0.shape
    g = half_d // N_POOL
    seq = t - cx
    tm = _row_block(cx)
    gq = g // 4
    side, band = _lat_band(tm)

    def body(v_ref, ag_ref, du_ref, w_ref, sc_ref, dz_ref, dw_ref, dsc_ref,
             vlat_ref, mask_ref, pooled_ref, dmx_ref, dpl_ref, wlat_ref, wctx_ref):
        gi = pl.program_id(0)
        w = w_ref[...].astype(BF16)
        sc = sc_ref[...]
        _store_padded_lat(vlat_ref, v_ref[cx:, :], tm)
        _store_padded_lat(wlat_ref, jnp.zeros((seq, g), BF16), tm)
        mask_ref[...] = _lat_mask(gi, tm, cx, False)

        def first(row0, window_sum, weighted_ref, weighted_row0):
            rows = pl.ds(row0, tm)
            inv = _pool_inv_count(gi, row0, tm, cx, seq)
            pooled = (window_sum * inv - v_ref[rows, :]).astype(BF16)
            pooled_ref[rows, :] = pooled
            mixed = _dot(pooled, w)
            ag = ag_ref[rows, :]
            duv = du_ref[rows, :]
            dz_ref[1, rows, :] = (duv * (mixed * sc) * _dsilu(ag)).astype(BF16)
            dms = duv * _silu(ag)
            dmixed = (dms * sc).astype(BF16)
            dmx_ref[rows, :] = dmixed
            dpooled = _dot(dmixed, w, NT)
            dpl_ref[rows, :] = dpooled
            weighted_ref[pl.ds(weighted_row0, tm), :] = (dpooled * inv).astype(BF16)
            return jnp.sum(dms * mixed, axis=0, keepdims=True)

        dsc = jnp.zeros((1, g), F32)
        vctx = v_ref[0:cx, :].astype(BF16)
        for i in range(cx // tm):
            dsc += first(i * tm, _dot(_pool_mask(gi, i * tm, 0, tm, cx, cx, False), vctx), wctx_ref, i * tm)

        def first_lat(j, acc):
            src = vlat_ref[pl.ds(pl.multiple_of(j * tm, tm), band * tm), :]
            return acc + first(pl.multiple_of(cx + j * tm, tm), _dot(mask_ref[...], src),
                               wlat_ref, pl.multiple_of((side + j) * tm, tm))

        dsc_ref[...] = lax.fori_loop(0, seq // tm, first_lat, dsc)
        dw = _dot(pooled_ref[...], dmx_ref[...], TN)
        for qi in range(4):
            dw_ref[qi] = dw[qi * gq:(qi + 1) * gq, :]

        wctx = wctx_ref[...]
        for i in range(cx // tm):
            rows = pl.ds(i * tm, tm)
            dz_ref[0, rows, :] = (_dot(_pool_mask(gi, i * tm, 0, tm, cx, cx, True), wctx)
                                  - dpl_ref[rows, :]).astype(BF16)
        mask_ref[...] = _lat_mask(gi, tm, cx, True)

        def second_lat(j, carry):
            rows = pl.ds(pl.multiple_of(cx + j * tm, tm), tm)
            src = wlat_ref[pl.ds(pl.multiple_of(j * tm, tm), band * tm), :]
            dz_ref[0, rows, :] = (_dot(mask_ref[...], src) - dpl_ref[rows, :]).astype(BF16)
            return carry

        lax.fori_loop(0, seq // tm, second_lat, 0)

    sec = lambda s: pl.BlockSpec((None, t, g), lambda j: (s, 0, j))
    padded = pltpu.VMEM((seq + 2 * side * tm, g), BF16)
    return pl.pallas_call(
        body, name="mix_a_bwd", grid=(N_POOL,),
        in_specs=[sec(0), sec(1), pl.BlockSpec((t, g), lambda j: (0, j)),
                  pl.BlockSpec((None, g, g), lambda j: (j, 0, 0)),
                  pl.BlockSpec((1, g), lambda j: (0, j))],
        out_specs=[pl.BlockSpec((2, t, g), lambda j: (0, 0, j)),
                   pl.BlockSpec((4, None, gq, g), lambda j: (0, j, 0, 0)),
                   pl.BlockSpec((1, g), lambda j: (0, j))],
        out_shape=[jax.ShapeDtypeStruct((2, t, half_d), BF16),
                   jax.ShapeDtypeStruct((4, N_POOL, gq, g), F32),
                   jax.ShapeDtypeStruct((1, half_d), F32)],
        scratch_shapes=[padded, pltpu.VMEM((tm, band * tm), BF16), pltpu.VMEM((t, g), BF16),
                        pltpu.VMEM((t, g), BF16), pltpu.VMEM((t, g), F32), padded, pltpu.VMEM((cx, g), BF16)],
        compiler_params=_cp(),
    )(z0, z0, du, pool_w, pool_scale)


def _conv_masks(t, cx):
    r = lax.broadcasted_iota(jnp.int32, (t, 1), 0)
    has_prev = jnp.where((r == 0) | (r == cx), 0.0, 1.0)
    has_next = jnp.where((r == cx - 1) | (r == t - 1), 0.0, 1.0)
    return has_prev, has_next


def mix_b_fwd(z0, conv_w, conv_b, cx):
    _, t, half_d = z0.shape
    gb = 128

    def body(bx_ref, bb_ref, bc_ref, bg_ref, w_ref, b_ref, u_ref):
        has_prev, has_next = _conv_masks(t, cx)
        tt = bc_ref[...] * bx_ref[...]
        prev = pltpu.roll(tt, 1, 0) * has_prev
        nxt = pltpu.roll(tt, t - 1, 0) * has_next
        cv = prev * w_ref[0:1, :] + tt * w_ref[1:2, :] + nxt * w_ref[2:3, :] + b_ref[...]
        u_ref[...] = (bb_ref[...] * cv * _silu(bg_ref[...])).astype(BF16)

    sec = lambda s: pl.BlockSpec((None, t, gb), lambda j: (s, 0, j))
    return pl.pallas_call(
        body, name="mix_b_fwd", grid=(half_d // gb,),
        in_specs=[sec(2), sec(3), sec(4), sec(5), pl.BlockSpec((3, gb), lambda j: (0, j)),
                  pl.BlockSpec((1, gb), lambda j: (0, j))],
        out_specs=pl.BlockSpec((t, gb), lambda j: (0, j)),
        out_shape=jax.ShapeDtypeStruct((t, half_d), BF16), compiler_params=_cp(),
    )(z0, z0, z0, z0, conv_w, conv_b)


def mix_b_bwd(z0, du, conv_w, conv_b, cx):
    _, t, half_d = z0.shape
    gb = 128
    off = half_d // gb

    def body(bx_ref, bb_ref, bc_ref, bg_ref, du_ref, w_ref, b_ref, dz_ref, dw_ref, db_ref):
        has_prev, has_next = _conv_masks(t, cx)
        bx, bb, bc, bg = bx_ref[...], bb_ref[...], bc_ref[...], bg_ref[...]
        duv = du_ref[...]
        tt = bc * bx
        prev = pltpu.roll(tt, 1, 0) * has_prev
        nxt = pltpu.roll(tt, t - 1, 0) * has_next
        w0, w1, w2 = w_ref[0:1, :], w_ref[1:2, :], w_ref[2:3, :]
        cv = prev * w0 + tt * w1 + nxt * w2 + b_ref[...]
        sg = _silu(bg)
        dz_ref[1] = (duv * cv * sg).astype(BF16)
        dz_ref[3] = (duv * bb * cv * _dsilu(bg)).astype(BF16)
        dcv = duv * bb * sg
        dw_ref[0:1, :] = jnp.sum(dcv * prev, axis=0, keepdims=True)
        dw_ref[1:2, :] = jnp.sum(dcv * tt, axis=0, keepdims=True)
        dw_ref[2:3, :] = jnp.sum(dcv * nxt, axis=0, keepdims=True)
        db_ref[...] = jnp.sum(dcv, axis=0, keepdims=True)
        dt = (pltpu.roll(dcv * has_prev, t - 1, 0) * w0 + dcv * w1
              + pltpu.roll(dcv * has_next, 1, 0) * w2)
        dz_ref[0] = (dt * bc).astype(BF16)
        dz_ref[2] = (dt * bx).astype(BF16)

    sec = lambda s: pl.BlockSpec((None, t, gb), lambda j: (s, 0, j))
    return pl.pallas_call(
        body, name="mix_b_bwd", grid=(half_d // gb,),
        in_specs=[sec(2), sec(3), sec(4), sec(5), pl.BlockSpec((t, gb), lambda j: (0, j + off)),
                  pl.BlockSpec((3, gb), lambda j: (0, j)), pl.BlockSpec((1, gb), lambda j: (0, j))],
        out_specs=[pl.BlockSpec((4, t, gb), lambda j: (0, 0, j)),
                   pl.BlockSpec((3, gb), lambda j: (0, j)), pl.BlockSpec((1, gb), lambda j: (0, j))],
        out_shape=[jax.ShapeDtypeStruct((4, t, half_d), BF16),
                   jax.ShapeDtypeStruct((3, half_d), F32), jax.ShapeDtypeStruct((1, half_d), F32)],
        compiler_params=_cp(),
    )(z0, z0, z0, z0, du, conv_w, conv_b)


def _lower_bound(lbl_ref, d):
    l0, l1, l2 = lbl_ref[d, 0:1, :], lbl_ref[d, 1:2, :], lbl_ref[d, 2:3, :]
    mx = jnp.maximum(jnp.maximum(l0, l1), l2)
    e0, e1, e2 = jnp.exp(l0 - mx), jnp.exp(l1 - mx), jnp.exp(l2 - mx)
    inv = 1.0 / (e0 + e1 + e2)
    return (e0 + e1) * inv, (e0 * inv, e1 * inv, e2 * inv)


def _chunk_consts(d):
    r = lax.broadcasted_iota(jnp.int32, (CHUNK, CHUNK), 0)
    c = lax.broadcasted_iota(jnp.int32, (CHUNK, CHUNK), 1)
    keep = (c <= r) if d == 0 else (c >= r)
    return jnp.where(keep, 1.0, 0.0).astype(F32), keep


def _chunk_of_step(s, d, nc, ncc):
    if d == 0:
        return s
    return jnp.where(s < ncc, ncc - 1 - s, nc - 1 + ncc - s)


def _chunk_terms(lfc, kc, qc, cum):
    bc = _dot(cum, lfc, precision=lax.Precision.HIGHEST)
    bl = jnp.sum(lfc, axis=0, keepdims=True)
    e = jnp.exp(bc)
    einv = jnp.exp(-bc)
    erem = jnp.exp(bl - bc)
    return e, einv, erem, jnp.exp(bl), qc * e, kc * einv, kc * erem


def hgrn_fwd(z1, lbl, onorm, cx):
    _, t, d = z1.shape
    seq = t - cx
    nc, ncc = t // CHUNK, cx // CHUNK

    def body(zf_ref, zb_ref, v_ref, q_ref, g_ref, lbl_ref, on_ref, o_ref, r_ref,
             lf_ref, k_ref, oacc_ref, st_ref):
        for dr, z_ref in ((0, zf_ref), (1, zb_ref)):
            lbv, _ = _lower_bound(lbl_ref, dr)
            z = z_ref[...]
            lf_ref[...] = jnp.log(lbv + (1.0 - lbv) * jax.nn.sigmoid(z))
            k_ref[...] = (1.0 - lbv) * jax.nn.sigmoid(-z)
            st_ref[...] = jnp.zeros_like(st_ref)
            cum, keep = _chunk_consts(dr)

            def step(s, carry, dr=dr, cum=cum, keep=keep):
                n = _chunk_of_step(s, dr, nc, ncc)
                rows = pl.ds(pl.multiple_of(n * CHUNK, CHUNK), CHUNK)
                vc = v_ref[rows, :].astype(BF16)
                _, _, _, dec, qd, ki, kd = _chunk_terms(lf_ref[rows, :], k_ref[rows, :], q_ref[rows, :], cum)
                qdb = qd.astype(BF16)
                a = jnp.where(keep, _dot(qdb, ki.astype(BF16), NT), 0.0)
                st = st_ref[...]
                oc = _dot(qdb, st.astype(BF16), NT) + _dot(a.astype(BF16), vc)
                st_ref[...] = st * dec + _dot(vc, kd.astype(BF16), TN)
                if dr == 0:
                    oacc_ref[rows, :] = oc
                else:
                    oacc_ref[rows, :] += oc
                return carry

            lax.fori_loop(0, nc, step, 0, unroll=4)

        o = oacc_ref[cx:, :]
        o_ref[...] = o
        rstd = lax.rsqrt(jnp.mean(o * o, axis=-1, keepdims=True) + EPS)
        r_ref[...] = (o * rstd * on_ref[...] * _silu(g_ref[cx:, :])).astype(BF16)

    sec = lambda s: pl.BlockSpec((None, t, HEAD), lambda h: (s, 0, h))
    col = pl.BlockSpec((seq, HEAD), lambda h: (0, h))
    return pl.pallas_call(
        body, name="hgrn_fwd", grid=(d // HEAD,),
        in_specs=[sec(0), sec(1), sec(2), sec(3), sec(4),
                  pl.BlockSpec((2, 3, HEAD), lambda h: (0, 0, h)), pl.BlockSpec((1, HEAD), lambda h: (0, h))],
        out_specs=[col, col],
        out_shape=[jax.ShapeDtypeStruct((seq, d), F32), jax.ShapeDtypeStruct((seq, d), BF16)],
        scratch_shapes=[pltpu.VMEM((t, HEAD), F32), pltpu.VMEM((t, HEAD), F32), pltpu.VMEM((t, HEAD), F32),
                        pltpu.VMEM((HEAD, HEAD), F32)],
        compiler_params=_cp(),
    )(z1, z1, z1, z1, z1, lbl, onorm)


def hgrn_bwd(z1, lbl, onorm, o, dr_out, cx):
    _, t, d = z1.shape
    seq = t - cx
    nc, ncc = t // CHUNK, cx // CHUNK

    def body(zf_ref, zb_ref, v_ref, q_ref, g_ref, lbl_ref, on_ref, o_ref, dr_ref,
             dz_ref, don_ref, dlb_ref,
             lf_ref, k_ref, do_ref, dq_ref, dv_ref, dk_ref, dlf_ref, ssc_ref, dst_ref):
        o = o_ref[...]
        g = g_ref[cx:, :]
        drv = dr_ref[...]
        onv = on_ref[...]
        rstd = lax.rsqrt(jnp.mean(o * o, axis=-1, keepdims=True) + EPS)
        ohat = o * rstd
        sg = _silu(g)
        don_ref[...] = jnp.sum(drv * ohat * sg, axis=0, keepdims=True)
        dz_ref[4, :cx, :] = jnp.zeros((cx, HEAD), BF16)
        dz_ref[4, cx:, :] = (drv * ohat * onv * _dsilu(g)).astype(BF16)
        dohat = drv * onv * sg
        do_ref[:cx, :] = jnp.zeros((cx, HEAD), F32)
        do_ref[cx:, :] = rstd * (dohat - ohat * jnp.mean(dohat * ohat, axis=-1, keepdims=True))

        for dr, z_ref in ((0, zf_ref), (1, zb_ref)):
            lbv, _ = _lower_bound(lbl_ref, dr)
            z = z_ref[...]
            lf_ref[...] = jnp.log(lbv + (1.0 - lbv) * jax.nn.sigmoid(z))
            k_ref[...] = (1.0 - lbv) * jax.nn.sigmoid(-z)
            cum, keep = _chunk_consts(dr)
            cum_t, _ = _chunk_consts(1 - dr)

            st_init = jnp.zeros((HEAD, HEAD), F32)

            def state_step(s, st, dr=dr, cum=cum):
                n = _chunk_of_step(s, dr, nc, ncc)
                rows = pl.ds(pl.multiple_of(n * CHUNK, CHUNK), CHUNK)
                ssc_ref[n] = st
                _, _, _, dec, _, _, kd = _chunk_terms(lf_ref[rows, :], k_ref[rows, :], q_ref[rows, :], cum)
                return st * dec + _dot(v_ref[rows, :].astype(BF16), kd.astype(BF16), TN)

            lax.fori_loop(0, nc, state_step, st_init, unroll=4)
            dst_ref[...] = jnp.zeros_like(dst_ref)

            def grad_step(s2, carry, dr=dr, cum=cum, cum_t=cum_t, keep=keep):
                n = _chunk_of_step(nc - 1 - s2, dr, nc, ncc)
                rows = pl.ds(pl.multiple_of(n * CHUNK, CHUNK), CHUNK)
                vc = v_ref[rows, :].astype(BF16)
                e, einv, erem, dec, qd, ki, kd = _chunk_terms(
                    lf_ref[rows, :], k_ref[rows, :], q_ref[rows, :], cum)
                qdb, kib, kdb = qd.astype(BF16), ki.astype(BF16), kd.astype(BF16)
                doc = do_ref[rows, :].astype(BF16)
                st0 = ssc_ref[n]
                dst = dst_ref[...]
                dstb = dst.astype(BF16)
                a = jnp.where(keep, _dot(qdb, kib, NT), 0.0).astype(BF16)
                da = jnp.where(keep, _dot(doc, vc, NT), 0.0).astype(BF16)
                dqd = _dot(doc, st0.astype(BF16)) + _dot(da, kib)
                dki = _dot(da, qdb, TN)
                dv = _dot(a, doc, TN) + _dot(kdb, dstb, NT)
                dkd = _dot(vc, dstb)
                ddec = jnp.sum(dst * st0, axis=0, keepdims=True)
                dst_ref[...] = _dot(doc, qdb, TN) + dst * dec
                dbc = dqd * qd - dki * ki - dkd * kd
                dbl = jnp.sum(dkd * kd, axis=0, keepdims=True) + ddec * dec
                dlf_ref[rows, :] = _dot(cum_t, dbc, precision=lax.Precision.HIGHEST) + dbl
                dk_ref[rows, :] = dki * einv + dkd * erem
                if dr == 0:
                    dq_ref[rows, :] = dqd * e
                    dv_ref[rows, :] = dv
                else:
                    dq_ref[rows, :] += dqd * e
                    dv_ref[rows, :] += dv
                return carry

            lax.fori_loop(0, nc, grad_step, 0, unroll=2)

            sig = jax.nn.sigmoid(z)
            one_lb = 1.0 - lbv
            f = lbv + one_lb * sig
            dlf = dlf_ref[...]
            dk = dk_ref[...]
            dsig = (dlf / f - dk) * one_lb
            dz_ref[dr] = (dsig * sig * (1.0 - sig)).astype(BF16)
            dlb_ref[dr:dr + 1, :] = jnp.sum((dlf / f - dk) * (1.0 - sig), axis=0, keepdims=True)

        dz_ref[2] = dv_ref[...].astype(BF16)
        dz_ref[3] = dq_ref[...].astype(BF16)

    sec = lambda s: pl.BlockSpec((None, t, HEAD), lambda h: (s, 0, h))
    col = pl.BlockSpec((seq, HEAD), lambda h: (0, h))
    tvec = pltpu.VMEM((t, HEAD), F32)
    return pl.pallas_call(
        body, name="hgrn_bwd", grid=(d // HEAD,),
        in_specs=[sec(0), sec(1), sec(2), sec(3), sec(4),
                  pl.BlockSpec((2, 3, HEAD), lambda h: (0, 0, h)), pl.BlockSpec((1, HEAD), lambda h: (0, h)),
                  col, col],
        out_specs=[pl.BlockSpec((5, t, HEAD), lambda h: (0, 0, h)),
                   pl.BlockSpec((1, HEAD), lambda h: (0, h)), pl.BlockSpec((2, HEAD), lambda h: (0, h))],
        out_shape=[jax.ShapeDtypeStruct((5, t, d), BF16), jax.ShapeDtypeStruct((1, d), F32),
                   jax.ShapeDtypeStruct((2, d), F32)],
        scratch_shapes=[tvec, tvec, tvec, tvec, tvec, tvec, tvec,
                        pltpu.VMEM((nc, HEAD, HEAD), F32), pltpu.VMEM((HEAD, HEAD), F32)],
        compiler_params=_cp(),
    )(z1, z1, z1, z1, z1, lbl, onorm, o, dr_out)


def _gates(z, lbv):
    e = jnp.exp(-jnp.abs(z))
    r = 1.0 / (1.0 + e)
    er = e * r
    pos = z >= 0.0
    sig = jnp.where(pos, r, er)
    nsig = jnp.where(pos, er, r)
    return sig, nsig, lbv + (1.0 - lbv) * sig


def _split3(x):
    hi = x.astype(BF16)
    r1 = x - hi.astype(F32)
    mid = r1.astype(BF16)
    lo = (r1 - mid.astype(F32)).astype(BF16)
    return jnp.concatenate([hi, mid, lo], axis=1)


def _cumsum_chunk(cum, x):
    y = _dot(cum, _split3(x))
    return y[:, :HEAD] + y[:, HEAD:2 * HEAD] + y[:, 2 * HEAD:]


def _chunk_rows(n):
    return pl.ds(pl.multiple_of(n * CHUNK, CHUNK), CHUNK)


def _group(nc, prefer=(4, 3, 2, 1)):
    return next(u for u in prefer if nc % u == 0)


WIDE_GROUP = (12, 6, 4, 3, 2, 1)


def _decay_pass(lf_ref, bc_ref, dec_ref, cum, nc):
    grp = _group(nc, WIDE_GROUP)

    def step(m, carry):
        ns = [m * grp + u for u in range(grp)]
        lfc = [lf_ref[_chunk_rows(n), :] for n in ns]
        bc = [_cumsum_chunk(cum, x) for x in lfc]
        for u, n in enumerate(ns):
            bc_ref[_chunk_rows(n), :] = bc[u]
            dec_ref[n] = jnp.broadcast_to(jnp.exp(jnp.sum(lfc[u], axis=0, keepdims=True)), (8, HEAD))
        return carry

    lax.fori_loop(0, nc // grp, step, 0)


def hgrn_fwd(z1, lbl, onorm, cx):
    _, t, d = z1.shape
    seq = t - cx
    nc, ncc = t // CHUNK, cx // CHUNK

    grp, sgrp = _group(nc, (6, 4, 3, 2, 1)), _group(nc, WIDE_GROUP)

    def body(zf_ref, zb_ref, v_ref, q_ref, g_ref, lbl_ref, on_ref, o_ref, r_ref,
             lf_ref, k_ref, bc_ref, dec_ref, qd_ref, ki_ref, oacc_ref, ds_ref):
        for dr, z_ref in ((0, zf_ref), (1, zb_ref)):
            lbv, _ = _lower_bound(lbl_ref, dr)
            _, nsig, f = _gates(z_ref[...], lbv)
            lf_ref[...] = jnp.log(f)
            k_ref[...] = (1.0 - lbv) * nsig
            cum, keep = _chunk_consts(dr)
            _decay_pass(lf_ref, bc_ref, dec_ref, cum.astype(BF16), nc)
            bc = bc_ref[...]
            qd_ref[...] = (q_ref[...] * jnp.exp(bc)).astype(BF16)
            ki_ref[...] = (k_ref[...] * jnp.exp(-bc)).astype(BF16)

            def local_step(m, carry, dr=dr, keep=keep):
                ns = [m * grp + u for u in range(grp)]
                rows = [_chunk_rows(n) for n in ns]
                qd = [qd_ref[r, :] for r in rows]
                ki = [ki_ref[r, :] for r in rows]
                vc = [v_ref[r, :].astype(BF16) for r in rows]
                sc = [_dot(qd[u], ki[u], NT) for u in range(grp)]
                inc = [_dot(vc[u], ki[u], TN) for u in range(grp)]
                a = [jnp.where(keep, s, 0.0).astype(BF16) for s in sc]
                intra = [_dot(a[u], vc[u]) for u in range(grp)]
                for u in range(grp):
                    ds_ref[ns[u]] = inc[u] * dec_ref[ns[u]][0:1, :]
                    if dr == 0:
                        oacc_ref[rows[u], :] = intra[u]
                    else:
                        oacc_ref[rows[u], :] += intra[u]
                return carry

            lax.fori_loop(0, nc // grp, local_step, 0)

            def state_step(m, st, dr=dr):
                ns = [_chunk_of_step(m * sgrp + u, dr, nc, ncc) for u in range(sgrp)]
                rows = [_chunk_rows(n) for n in ns]
                sts = []
                for n in ns:
                    sts.append(st.astype(BF16))
                    st = st * dec_ref[n][0:1, :] + ds_ref[n]
                inter = [_dot(qd_ref[rows[u], :], sts[u], NT) for u in range(sgrp)]
                for u in range(sgrp):
                    oacc_ref[rows[u], :] += inter[u]
                return st

            lax.fori_loop(0, nc // sgrp, state_step, jnp.zeros((HEAD, HEAD), F32))

        o = oacc_ref[cx:, :]
        o_ref[...] = o
        rstd = lax.rsqrt(jnp.mean(o * o, axis=-1, keepdims=True) + EPS)
        r_ref[...] = (o * rstd * on_ref[...] * _silu(g_ref[cx:, :])).astype(BF16)

    sec = lambda s: pl.BlockSpec((None, t, HEAD), lambda h: (s, 0, h))
    col = pl.BlockSpec((seq, HEAD), lambda h: (0, h))
    tf32, tb16 = pltpu.VMEM((t, HEAD), F32), pltpu.VMEM((t, HEAD), BF16)
    return pl.pallas_call(
        body, name="hgrn_fwd", grid=(d // HEAD,),
        in_specs=[sec(0), sec(1), sec(2), sec(3), sec(4),
                  pl.BlockSpec((2, 3, HEAD), lambda h: (0, 0, h)), pl.BlockSpec((1, HEAD), lambda h: (0, h))],
        out_specs=[col, col],
        out_shape=[jax.ShapeDtypeStruct((seq, d), F32), jax.ShapeDtypeStruct((seq, d), BF16)],
        scratch_shapes=[tf32, tf32, tf32, pltpu.VMEM((nc, 8, HEAD), F32), tb16, tb16, tf32,
                        pltpu.VMEM((nc, HEAD, HEAD), F32)],
        compiler_params=_cp(),
    )(z1, z1, z1, z1, z1, lbl, onorm)


def hgrn_bwd(z1, lbl, onorm, o, dr_out, cx):
    _, t, d = z1.shape
    seq = t - cx
    nc, ncc = t // CHUNK, cx // CHUNK

    grp2 = grp = _group(nc)

    def body(zf_ref, zb_ref, v_ref, q_ref, g_ref, lbl_ref, on_ref, o_ref, dr_ref,
             dz_ref, don_ref, dlb_ref,
             lf_ref, k_ref, bc_ref, dec_ref, qd_ref, ki_ref, do_ref,
             dqd_ref, dki_ref, dq_ref, dv_ref, ds_ref, dsl_ref):
        o = o_ref[...]
        g = g_ref[cx:, :]
        drv = dr_ref[...]
        onv = on_ref[...]
        rstd = lax.rsqrt(jnp.mean(o * o, axis=-1, keepdims=True) + EPS)
        ohat = o * rstd
        sg = _silu(g)
        don_ref[...] = jnp.sum(drv * ohat * sg, axis=0, keepdims=True)
        dz_ref[4, :cx, :] = jnp.zeros((cx, HEAD), BF16)
        dz_ref[4, cx:, :] = (drv * ohat * onv * _dsilu(g)).astype(BF16)
        dohat = drv * onv * sg
        do_ref[:cx, :] = jnp.zeros((cx, HEAD), BF16)
        do_ref[cx:, :] = (rstd * (dohat - ohat * jnp.mean(dohat * ohat, axis=-1, keepdims=True))).astype(BF16)

        for dr, z_ref in ((0, zf_ref), (1, zb_ref)):
            lbv, _ = _lower_bound(lbl_ref, dr)
            _, nsig, f = _gates(z_ref[...], lbv)
            lf_ref[...] = jnp.log(f)
            k_ref[...] = (1.0 - lbv) * nsig
            cum, keep = _chunk_consts(dr)
            cum_t = _chunk_consts(1 - dr)[0].astype(BF16)
            _decay_pass(lf_ref, bc_ref, dec_ref, cum.astype(BF16), nc)
            bc = bc_ref[...]
            qd_ref[...] = (q_ref[...] * jnp.exp(bc)).astype(BF16)
            ki_ref[...] = (k_ref[...] * jnp.exp(-bc)).astype(BF16)

            def local_step(m, carry, dr=dr, keep=keep):
                ns = [m * grp + u for u in range(grp)]
                rows = [_chunk_rows(n) for n in ns]
                rng = range(grp)
                qd = [qd_ref[r, :] for r in rows]
                ki = [ki_ref[r, :] for r in rows]
                doc = [do_ref[r, :] for r in rows]
                vc = [v_ref[r, :].astype(BF16) for r in rows]
                sc = [_dot(qd[u], ki[u], NT) for u in rng]
                dsc = [_dot(doc[u], vc[u], NT) for u in rng]
                inc = [_dot(vc[u], ki[u], TN) for u in rng]
                dinc = [_dot(doc[u], qd[u], TN) for u in rng]
                a = [jnp.where(keep, s, 0.0).astype(BF16) for s in sc]
                da = [jnp.where(keep, s, 0.0).astype(BF16) for s in dsc]
                dqd = [_dot(da[u], ki[u]) for u in rng]
                dki = [_dot(da[u], qd[u], TN) for u in rng]
                dv = [_dot(a[u], doc[u], TN) for u in rng]
                for u in rng:
                    ds_ref[ns[u]] = inc[u] * dec_ref[ns[u]][0:1, :]
                    dsl_ref[ns[u]] = dinc[u]
                    dqd_ref[rows[u], :] = dqd[u]
                    dki_ref[rows[u], :] = dki[u]
                    if dr == 0:
                        dv_ref[rows[u], :] = dv[u]
                    else:
                        dv_ref[rows[u], :] += dv[u]
                return carry

            lax.fori_loop(0, nc // grp, local_step, 0)

            def state_step(s, st, dr=dr):
                n = _chunk_of_step(s, dr, nc, ncc)
                inc = ds_ref[n]
                ds_ref[n] = st
                return st * dec_ref[n][0:1, :] + inc

            lax.fori_loop(0, nc, state_step, jnp.zeros((HEAD, HEAD), F32), unroll=4)

            def dstate_step(s, dst, dr=dr):
                n = _chunk_of_step(nc - 1 - s, dr, nc, ncc)
                inc = dsl_ref[n]
                dsl_ref[n] = dst
                return inc + dst * dec_ref[n][0:1, :]

            lax.fori_loop(0, nc, dstate_step, jnp.zeros((HEAD, HEAD), F32), unroll=4)

            def grad_step(m, carry, dr=dr, cum_t=cum_t):
                ns = [m * grp2 + u for u in range(grp2)]
                rows = [_chunk_rows(n) for n in ns]
                rng = range(grp2)
                st0 = [ds_ref[n] for n in ns]
                dst = [dsl_ref[n] for n in ns]
                dstb = [x.astype(BF16) for x in dst]
                dec = [dec_ref[n][0:1, :] for n in ns]
                doc = [do_ref[r, :] for r in rows]
                vc = [v_ref[r, :].astype(BF16) for r in rows]
                e = [jnp.exp(bc_ref[r, :]) for r in rows]
                einv = [jnp.exp(-bc_ref[r, :]) for r in rows]
                qd = [q_ref[rows[u], :] * e[u] for u in rng]
                ki = [k_ref[rows[u], :] * einv[u] for u in rng]
                kd = [ki[u] * dec[u] for u in rng]
                dqd_st = [_dot(doc[u], st0[u].astype(BF16)) for u in rng]
                dkd = [_dot(vc[u], dstb[u]) for u in rng]
                dv_st = [_dot(kd[u].astype(BF16), dstb[u], NT) for u in rng]
                dqd = [dqd_ref[rows[u], :] + dqd_st[u] for u in rng]
                dki = [dki_ref[r, :] for r in rows]
                dbc = [dqd[u] * qd[u] - dki[u] * ki[u] - dkd[u] * kd[u] for u in rng]
                cs = [_cumsum_chunk(cum_t, x) for x in dbc]
                for u in rng:
                    ddec = jnp.sum(dst[u] * st0[u], axis=0, keepdims=True)
                    dbl = jnp.sum(dkd[u] * kd[u], axis=0, keepdims=True) + ddec * dec[u]
                    dv_ref[rows[u], :] += dv_st[u]
                    dqd_ref[rows[u], :] = cs[u] + dbl
                    dki_ref[rows[u], :] = dki[u] * einv[u] + dkd[u] * (einv[u] * dec[u])
                    if dr == 0:
                        dq_ref[rows[u], :] = dqd[u] * e[u]
                    else:
                        dq_ref[rows[u], :] += dqd[u] * e[u]
                return carry

            lax.fori_loop(0, nc // grp2, grad_step, 0)

            sig, nsig, f = _gates(z_ref[...], lbv)
            common = (dqd_ref[...] / f - dki_ref[...]) * nsig
            dz_ref[dr] = (common * ((1.0 - lbv) * sig)).astype(BF16)
            dlb_ref[dr:dr + 1, :] = jnp.sum(common, axis=0, keepdims=True)

        dz_ref[2] = dv_ref[...].astype(BF16)
        dz_ref[3] = dq_ref[...].astype(BF16)

    sec = lambda s: pl.BlockSpec((None, t, HEAD), lambda h: (s, 0, h))
    col = pl.BlockSpec((seq, HEAD), lambda h: (0, h))
    tf32, tb16 = pltpu.VMEM((t, HEAD), F32), pltpu.VMEM((t, HEAD), BF16)
    states = pltpu.VMEM((nc, HEAD, HEAD), F32)
    return pl.pallas_call(
        body, name="hgrn_bwd", grid=(d // HEAD,),
        in_specs=[sec(0), sec(1), sec(2), sec(3), sec(4),
                  pl.BlockSpec((2, 3, HEAD), lambda h: (0, 0, h)), pl.BlockSpec((1, HEAD), lambda h: (0, h)),
                  col, col],
        out_specs=[pl.BlockSpec((5, t, HEAD), lambda h: (0, 0, h)),
                   pl.BlockSpec((1, HEAD), lambda h: (0, h)), pl.BlockSpec((2, HEAD), lambda h: (0, h))],
        out_shape=[jax.ShapeDtypeStruct((5, t, d), BF16), jax.ShapeDtypeStruct((1, d), F32),
                   jax.ShapeDtypeStruct((2, d), F32)],
        scratch_shapes=[tf32, tf32, tf32, pltpu.VMEM((nc, 8, HEAD), F32), tb16, tb16, tb16,
                        tf32, tf32, tf32, tf32, states, states],
        compiler_params=_cp(),
    )(z1, z1, z1, z1, z1, lbl, onorm, o, dr_out)


def _place():
    x, y, c = lax.axis_index("x"), lax.axis_index("y"), lax.axis_index("c")
    chips = [(1 - x, y), (x, 1 - y), (1 - x, 1 - y)]
    return x, y, c, chips


def allgather_shards(bufs):
    n = len(bufs)

    def body(*refs):
        outs = refs[n:2 * n]
        done_ref, send_sems, recv_sems = refs[2 * n:]
        done_ref[...] = jnp.zeros((8, 128), F32)
        x, y, c, chips = _place()
        p = 2 * x + y
        half = [pl.ds(c * (s.shape[1] // 2), s.shape[1] // 2) for s in bufs]
        other = [pl.ds((1 - c) * (s.shape[1] // 2), s.shape[1] // 2) for s in bufs]

        def remote(i, k, src, dst, to):
            return pltpu.make_async_remote_copy(src_ref=src, dst_ref=dst, send_sem=send_sems.at[6 * i + k],
                                                recv_sem=recv_sems.at[6 * i + k], device_id=to, device_id_type=MESH)

        sends = []
        for i in range(n):
            for j, chip in enumerate(chips):
                mine = outs[i].at[p, half[i]]
                cp = remote(i, j, mine, mine, (*chip, c))
                cp.start()
                sends.append(cp)
        for i in range(n):
            for j, chip in enumerate(chips):
                landed = outs[i].at[2 * chip[0] + chip[1], half[i]]
                remote(i, j, landed, landed, (x, y, c)).wait_recv()
                cp = remote(i, 3 + j, landed, landed, (x, y, 1 - c))
                cp.start()
                sends.append(cp)
        for i in range(n):
            for j, chip in enumerate(chips):
                landed = outs[i].at[2 * chip[0] + chip[1], other[i]]
                remote(i, 3 + j, landed, landed, (x, y, c)).wait_recv()
        for cp in sends:
            cp.wait_send()

    return pl.pallas_call(
        body, name="allgather_shards",
        in_specs=[ANY] * n, out_specs=[ANY] * n + [VMEM],
        out_shape=[jax.ShapeDtypeStruct(s.shape, s.dtype) for s in bufs] + [jax.ShapeDtypeStruct((8, 128), F32)],
        input_output_aliases={i: i for i in range(n)},
        scratch_shapes=[pltpu.SemaphoreType.DMA((6 * n,)), pltpu.SemaphoreType.DMA((6 * n,))],
        compiler_params=pltpu.CompilerParams(has_side_effects=True),
    )(*bufs)


def exchange_halves(grads):
    n = len(grads)

    def body(*refs):
        ins, outs = refs[:n], refs[n:2 * n]
        send_sems, recv_sems = refs[2 * n:]
        x, y, c, _ = _place()
        copies = []
        for i in range(n):
            hr = grads[i].shape[1] // 2
            cp = pltpu.make_async_remote_copy(
                src_ref=ins[i].at[:, pl.ds((1 - c) * hr, hr)], dst_ref=outs[i],
                send_sem=send_sems.at[i], recv_sem=recv_sems.at[i],
                device_id=(x, y, 1 - c), device_id_type=MESH)
            cp.start()
            copies.append(cp)
        for cp in copies:
            cp.wait()

    return pl.pallas_call(
        body, name="exchange_halves",
        in_specs=[ANY] * n, out_specs=[ANY] * n,
        out_shape=[jax.ShapeDtypeStruct((4, g.shape[1] // 2, g.shape[2]), g.dtype) for g in grads],
        scratch_shapes=[pltpu.SemaphoreType.DMA((n,)), pltpu.SemaphoreType.DMA((n,))],
        compiler_params=pltpu.CompilerParams(has_side_effects=True),
    )(*grads)


def pair_sum(grad, got, chip_core):
    _, r, cc = grad.shape
    hr = r // 2
    tr = 256 if hr % 256 == 0 else hr
    nb = hr // tr

    def body(cc_ref, a_ref, b_ref, own_ref, sb_ref):
        s = a_ref[...].astype(F32) + b_ref[...].astype(F32)
        sb_ref[...] = s.astype(BF16)

        @pl.when(pl.program_id(1) == cc_ref[0])
        def _():
            own_ref[...] = s

    grid_spec = pltpu.PrefetchScalarGridSpec(
        num_scalar_prefetch=1, grid=(nb, 4),
        in_specs=[pl.BlockSpec((None, tr, cc), lambda i, qi, cc_ref: (qi, cc_ref[1] * nb + i, 0)),
                  pl.BlockSpec((None, tr, cc), lambda i, qi, cc_ref: (qi, i, 0))],
        out_specs=[pl.BlockSpec((tr, cc), lambda i, qi, cc_ref: (i, 0)),
                   pl.BlockSpec((None, tr, cc), lambda i, qi, cc_ref: (qi, i, 0))])
    return pl.pallas_call(
        body, name="pair_sum", grid_spec=grid_spec,
        out_shape=[jax.ShapeDtypeStruct((hr, cc), F32), jax.ShapeDtypeStruct((4, hr, cc), BF16)],
        compiler_params=_cp(),
    )(chip_core, grad, got)


def scatter_to_owners(parts):
    n = len(parts)

    def body(*refs):
        ins, outs = refs[:n], refs[n:2 * n]
        send_sems, recv_sems = refs[2 * n:]
        x, y, c, chips = _place()
        copies = []
        for i in range(n):
            for j, chip in enumerate(chips):
                cp = pltpu.make_async_remote_copy(
                    src_ref=ins[i].at[2 * chip[0] + chip[1]], dst_ref=outs[i].at[j],
                    send_sem=send_sems.at[3 * i + j], recv_sem=recv_sems.at[3 * i + j],
                    device_id=(*chip, c), device_id_type=MESH)
                cp.start()
                copies.append(cp)
        for cp in copies:
            cp.wait()

    return pl.pallas_call(
        body, name="scatter_to_owners",
        in_specs=[ANY] * n, out_specs=[ANY] * n,
        out_shape=[jax.ShapeDtypeStruct((3,) + p.shape[1:], p.dtype) for p in parts],
        scratch_shapes=[pltpu.SemaphoreType.DMA((3 * n,)), pltpu.SemaphoreType.DMA((3 * n,))],
        compiler_params=pltpu.CompilerParams(has_side_effects=True),
    )(*parts)


def owner_sum(own, got, chip_core):
    hr, cc = own.shape
    tr = 256 if hr % 256 == 0 else hr
    nb = hr // tr

    def body(cc_ref, a_ref, b_ref, o_ref):
        s = a_ref[...] + b_ref[0].astype(F32)
        s = s + b_ref[1].astype(F32)
        o_ref[...] = s + b_ref[2].astype(F32)

    grid_spec = pltpu.PrefetchScalarGridSpec(
        num_scalar_prefetch=1, grid=(nb,),
        in_specs=[pl.BlockSpec((tr, cc), lambda i, cc_ref: (i, 0)),
                  pl.BlockSpec((3, tr, cc), lambda i, cc_ref: (0, i, 0))],
        out_specs=pl.BlockSpec((tr, cc), lambda i, cc_ref: (cc_ref[1] * nb + i, 0)))
    return pl.pallas_call(
        body, name="owner_sum", grid_spec=grid_spec,
        out_shape=jax.ShapeDtypeStruct((2 * hr, cc), F32), compiler_params=_cp(),
    )(chip_core, own, got)


def share_halves(bufs):
    n = len(bufs)

    def body(*refs):
        outs = refs[n:2 * n]
        send_sems, recv_sems = refs[2 * n:]
        x, y, c, _ = _place()
        copies = []
        for i in range(n):
            hr = bufs[i].shape[0] // 2
            mine = outs[i].at[pl.ds(c * hr, hr)]
            cp = pltpu.make_async_remote_copy(
                src_ref=mine, dst_ref=mine, send_sem=send_sems.at[i], recv_sem=recv_sems.at[i],
                device_id=(x, y, 1 - c), device_id_type=MESH)
            cp.start()
            copies.append((cp, outs[i].at[pl.ds((1 - c) * hr, hr)]))
        for i, (cp, theirs) in enumerate(copies):
            cp.wait_send()
            pltpu.make_async_remote_copy(
                src_ref=theirs, dst_ref=theirs, send_sem=send_sems.at[i], recv_sem=recv_sems.at[i],
                device_id=(x, y, c), device_id_type=MESH).wait_recv()

    return pl.pallas_call(
        body, name="share_halves",
        in_specs=[ANY] * n, out_specs=[ANY] * n,
        out_shape=[jax.ShapeDtypeStruct(b.shape, b.dtype) for b in bufs],
        input_output_aliases={i: i for i in range(n)},
        scratch_shapes=[pltpu.SemaphoreType.DMA((n,)), pltpu.SemaphoreType.DMA((n,))],
        compiler_params=pltpu.CompilerParams(has_side_effects=True),
    )(*bufs)


def allgather8(v, name):
    r, n = v.shape

    def body(v_ref, out_ref, send_sems, recv_sems):
        x, y, c, _ = _place()
        me = 4 * x + 2 * y + c
        out_ref[me] = v_ref[...]

        def copy(k, slot, to):
            return pltpu.make_async_remote_copy(
                src_ref=v_ref, dst_ref=out_ref.at[slot], send_sem=send_sems.at[k - 1],
                recv_sem=recv_sems.at[k - 1], device_id=to, device_id_type=MESH)

        peers = []
        for k in range(1, 8):
            px = 1 - x if (k >> 2) & 1 else x
            py = 1 - y if (k >> 1) & 1 else y
            pc = 1 - c if k & 1 else c
            peers.append((px, py, pc))
            copy(k, me, (px, py, pc)).start()
        for k, (px, py, pc) in enumerate(peers, start=1):
            copy(k, 4 * px + 2 * py + pc, (x, y, c)).wait_recv()
        for k, peer in enumerate(peers, start=1):
            copy(k, me, peer).wait_send()

    return pl.pallas_call(
        body, name=name, in_specs=[VMEM], out_specs=VMEM,
        out_shape=jax.ShapeDtypeStruct((8, r, n), v.dtype),
        scratch_shapes=[pltpu.SemaphoreType.DMA((7,)), pltpu.SemaphoreType.DMA((7,))],
        compiler_params=_cp(has_side_effects=True),
    )(v)


HBM = pl.BlockSpec(memory_space=pltpu.HBM)
SEM = pl.BlockSpec(memory_space=pltpu.SEMAPHORE)
DATAFLOW = pltpu.SideEffectType.DATAFLOW_SIDE_EFFECTING


def _descriptors(plan, refs, send_sems, recv_sems, arrivals=True):
    x, y, c, _ = _place()
    sends, recvs = plan(refs)
    out = [pltpu.make_async_remote_copy(src_ref=src, dst_ref=dst, send_sem=send_sems.at[k],
                                        recv_sem=recv_sems.at[k], device_id=to, device_id_type=MESH)
           for k, (src, dst, to) in enumerate(sends)]
    if not arrivals:
        return out, []
    inn = [pltpu.make_async_remote_copy(src_ref=land, dst_ref=land, send_sem=send_sems.at[k],
                                        recv_sem=recv_sems.at[k], device_id=(x, y, c), device_id_type=MESH)
           for k, land in enumerate(recvs)]
    return out, inn


def copies_start(name, arrays, n_copies, plan, after):
    na = len(arrays)

    def body(*refs):
        out, _ = _descriptors(plan, refs[:na], refs[na + 1], refs[na + 2], arrivals=False)
        for cp in out:
            cp.start()
        refs[-1][...] = jnp.zeros((8, 128), F32)

    res = pl.pallas_call(
        body, name=name,
        out_shape=(pltpu.SemaphoreType.DMA((n_copies,)), pltpu.SemaphoreType.DMA((n_copies,)),
                   *[pltpu.HBM(a.shape, a.dtype) for a in arrays], jax.ShapeDtypeStruct((8, 128), F32)),
        in_specs=[HBM] * na + [ANY], out_specs=(SEM, SEM, *[HBM] * na, VMEM),
        input_output_aliases={i: i + 2 for i in range(na)},
        compiler_params=pltpu.CompilerParams(has_side_effects=DATAFLOW),
    )(*[pltpu.with_memory_space_constraint(a, pltpu.HBM) for a in arrays], after)
    return res[0], res[1], list(res[2:2 + na]), res[-1]


def copies_wait(name, started, plan, after):
    send_sems, recv_sems, arrays, _ = started
    na = len(arrays)
    after = list(after) if isinstance(after, (list, tuple)) else [after]

    def body(*refs):
        out, inn = _descriptors(plan, refs[:na], refs[na], refs[na + 1])
        for cp in out:
            cp.wait_send()
        for cp in inn:
            cp.wait_recv()
        refs[-1][...] = jnp.zeros((8, 128), F32)

    res = pl.pallas_call(
        body, name=name,
        out_shape=(*[pltpu.HBM(a.shape, a.dtype) for a in arrays], jax.ShapeDtypeStruct((8, 128), F32)),
        in_specs=[HBM] * na + [SEM, SEM] + [ANY] * len(after), out_specs=(*[HBM] * na, VMEM),
        input_output_aliases={i: i for i in range(na)},
        compiler_params=pltpu.CompilerParams(has_side_effects=DATAFLOW),
    )(*arrays, send_sems, recv_sems, *after)
    return list(res[:na]), res[-1]


def _rows_half(r, c):
    return pl.ds(c * (r // 2), r // 2), pl.ds((1 - c) * (r // 2), r // 2)


def plan_gather_ici(refs):
    x, y, c, chips = _place()
    p = 2 * x + y
    sends, recvs = [], []
    for buf in refs:
        mine, _ = _rows_half(buf.shape[1], c)
        for chip in chips:
            sends.append((buf.at[p, mine], buf.at[p, mine], (*chip, c)))
            recvs.append(buf.at[2 * chip[0] + chip[1], mine])
    return sends, recvs


def plan_gather_d2d(refs):
    x, y, c, chips = _place()
    sends, recvs = [], []
    for buf in refs:
        mine, theirs = _rows_half(buf.shape[1], c)
        for chip in chips:
            slot = 2 * chip[0] + chip[1]
            sends.append((buf.at[slot, mine], buf.at[slot, mine], (x, y, 1 - c)))
            recvs.append(buf.at[slot, theirs])
    return sends, recvs


def plan_exchange(refs):
    x, y, c, _ = _place()
    n = len(refs) // 2
    sends, recvs = [], []
    for grad, land in zip(refs[:n], refs[n:]):
        _, theirs = _rows_half(grad.shape[1], c)
        sends.append((grad.at[:, theirs], land, (x, y, 1 - c)))
        recvs.append(land)
    return sends, recvs


def plan_scatter(refs):
    x, y, c, chips = _place()
    n = len(refs) // 2
    sends, recvs = [], []
    for part, land in zip(refs[:n], refs[n:]):
        for j, chip in enumerate(chips):
            sends.append((part.at[2 * chip[0] + chip[1]], land.at[j], (*chip, c)))
            recvs.append(land.at[j])
    return sends, recvs


def plan_share(refs):
    x, y, c, _ = _place()
    sends, recvs = [], []
    for buf in refs:
        mine, theirs = _rows_half(buf.shape[0], c)
        sends.append((buf.at[mine], buf.at[mine], (x, y, 1 - c)))
        recvs.append(buf.at[theirs])
    return sends, recvs


def put_in_slot(w, chip, dtype, name):
    r, c = w.shape
    tr = 256 if r % 256 == 0 else r

    def body(chip_ref, w_ref, o_ref):
        o_ref[...] = w_ref[...].astype(dtype)

    grid_spec = pltpu.PrefetchScalarGridSpec(
        num_scalar_prefetch=1, grid=(r // tr,),
        in_specs=[pl.BlockSpec((tr, c), lambda i, chip_ref: (i, 0))],
        out_specs=pl.BlockSpec((None, tr, c), lambda i, chip_ref: (chip_ref[0], i, 0)))
    return pl.pallas_call(body, name=name, grid_spec=grid_spec,
                          out_shape=jax.ShapeDtypeStruct((4, r, c), dtype), compiler_params=_cp())(chip, w)


def ada_fwd(s_in, ada_w, ada_b, tn):
    nl, d, ws = ada_w.shape

    def body(s_ref, w_ref, b_ref, so_ref, mod_ref):
        s = _silu(s_ref[...])
        so_ref[...] = s
        mod_ref[...] = _dot(s.astype(BF16), w_ref[...].astype(BF16)) + b_ref[...]

    return pl.pallas_call(
        body, name="ada_fwd", grid=(nl, ws // tn),
        in_specs=[pl.BlockSpec((16, d), lambda l, j: (0, 0)),
                  pl.BlockSpec((None, d, tn), lambda l, j: (l, 0, j)),
                  pl.BlockSpec((None, 1, tn), lambda l, j: (l, 0, j))],
        out_specs=[pl.BlockSpec((16, d), lambda l, j: (0, 0)),
                   pl.BlockSpec((None, 16, tn), lambda l, j: (l, 0, j))],
        out_shape=[jax.ShapeDtypeStruct((16, d), F32), jax.ShapeDtypeStruct((nl, 16, ws), F32)],
        compiler_params=_cp(),
    )(s_in, ada_w, ada_b)


def _adamw_math(w, g, m, v):
    m = ADAM_B1 * m + (1.0 - ADAM_B1) * g
    v = ADAM_B2 * v + (1.0 - ADAM_B2) * (g * g)
    m_hat = m / (1.0 - ADAM_B1 ** ADAM_STEP)
    v_hat = v / (1.0 - ADAM_B2 ** ADAM_STEP)
    delta = -ADAM_LR * (m_hat / (jnp.sqrt(v_hat) + ADAM_EPS) + ADAM_WD * w)
    return delta, m, v


def ada_bwd_adamw(s, dm, w, m, v):
    nl, d, ws = w.shape
    tr = 256 if d % 256 == 0 else 128

    def body(s_ref, dm_ref, w_ref, m_ref, v_ref, g_ref, dl_ref, mo_ref, vo_ref, dc_ref):
        dmv = dm_ref[...].astype(BF16)
        wv = w_ref[...]
        g = _dot(s_ref[...].astype(BF16), dmv, TN)
        g_ref[...] = g
        dl_ref[...], mo_ref[...], vo_ref[...] = _adamw_math(wv, g, m_ref[...], v_ref[...])
        dc_ref[...] = _dot(dmv[8:16, :], wv.astype(BF16), NT)

    wblk = pl.BlockSpec((None, tr, ws), lambda l, i: (l, i, 0))
    wshape = jax.ShapeDtypeStruct((nl, d, ws), F32)
    return pl.pallas_call(
        body, name="ada_bwd_adamw", grid=(nl, d // tr),
        in_specs=[pl.BlockSpec((16, tr), lambda l, i: (0, i)),
                  pl.BlockSpec((None, 16, ws), lambda l, i: (l, 0, 0)), wblk, wblk, wblk],
        out_specs=[wblk, wblk, wblk, wblk, pl.BlockSpec((None, 8, tr), lambda l, i: (l, 0, i))],
        out_shape=[wshape, wshape, wshape, wshape, jax.ShapeDtypeStruct((nl, 8, d), F32)],
        compiler_params=_cp(),
    )(s, dm, w, m, v)


def adamw(w, g, m, v, name):
    r, c = w.shape
    tr = 256 if r % 256 == 0 else r

    def body(w_ref, g_ref, m_ref, v_ref, dl_ref, mo_ref, vo_ref):
        dl_ref[...], mo_ref[...], vo_ref[...] = _adamw_math(w_ref[...], g_ref[...], m_ref[...], v_ref[...])

    blk = pl.BlockSpec((tr, c), lambda i: (i, 0))
    shape = jax.ShapeDtypeStruct((r, c), F32)
    return pl.pallas_call(body, name=name, grid=(r // tr,), in_specs=[blk] * 4, out_specs=[blk] * 3,
                          out_shape=[shape] * 3, compiler_params=_cp())(w, g, m, v)


SMALL_ROWS = 24
ROW_MOD = 10


def small_reduce(gathered):
    _, rows, d = gathered.shape

    def body(g_ref, o_ref):
        tot = g_ref[0]
        for b in range(1, 8):
            tot = tot + g_ref[b]
        o_ref[0:rows, :] = tot
        for layer in range(2):
            lat = ROW_MOD + 6 * layer
            o_ref[24 + 3 * layer:27 + 3 * layer, :] = tot[lat:lat + 3, :] + tot[lat + 3:lat + 6, :]
        o_ref[30:32, :] = jnp.zeros((2, d), F32)

    return pl.pallas_call(body, name="small_reduce", in_specs=[VMEM], out_specs=VMEM,
                          out_shape=jax.ShapeDtypeStruct((32, d), F32), compiler_params=_cp())(gathered)


def lb_logits_grad(lbl, dlb):
    _, _, n = lbl.shape

    def body(l_ref, d_ref, o_ref):
        for dr in range(2):
            _, (p0, p1, p2) = _lower_bound(l_ref, dr)
            dv = d_ref[dr:dr + 1, :]
            o_ref[dr, 0:1, :] = p0 * p2 * dv
            o_ref[dr, 1:2, :] = p1 * p2 * dv
            o_ref[dr, 2:3, :] = -p2 * (p0 + p1) * dv

    return pl.pallas_call(body, name="lb_logits_grad", in_specs=[VMEM, VMEM], out_specs=VMEM,
                          out_shape=jax.ShapeDtypeStruct((2, 3, n), F32), compiler_params=_cp())(lbl, dlb)


def c_ctx_grad(parts, c_ctx):
    d = c_ctx.shape[1]

    def body(p_ref, c_ref, o_ref):
        tot = p_ref[0, 0:1, :]
        for chip in range(1, 4):
            tot = tot + p_ref[2 * chip, 0:1, :]
        o_ref[...] = tot * _dsilu(c_ref[...])

    return pl.pallas_call(body, name="c_ctx_grad", in_specs=[VMEM, VMEM], out_specs=VMEM,
                          out_shape=jax.ShapeDtypeStruct((1, d), F32), compiler_params=_cp())(parts, c_ctx)


def _reduce_scatter(grads, core, chip_core):
    got = exchange_halves(grads)
    sums = [pair_sum(g, r, core) for g, r in zip(grads, got)]
    recv = scatter_to_owners([sb for _, sb in sums])
    reduced = [owner_sum(s, r, chip_core) for (s, _), r in zip(sums, recv)]
    return share_halves(reduced)


def kernel(x, c, ctx, c_ctx, ada_w, ada_b, pre_g, post_g, ev_w_in, ev_pool_w, ev_pool_scale, ev_conv_w, ev_conv_b, ev_w_out, od_w_in, od_onorm_g, od_w_out, lb_logits, loss_target, m_c_ctx, m_ada_w, m_ada_b, m_pre_g, m_post_g, m_ev_w_in, m_ev_pool_w, m_ev_pool_scale, m_ev_conv_w, m_ev_conv_b, m_ev_w_out, m_od_w_in, m_od_onorm_g, m_od_w_out, m_lb_logits, v_c_ctx, v_ada_w, v_ada_b, v_pre_g, v_post_g, v_ev_w_in, v_ev_pool_w, v_ev_pool_scale, v_ev_conv_w, v_ev_conv_b, v_ev_w_out, v_od_w_in, v_od_onorm_g, v_od_w_out, v_lb_logits):
    _, seq, d = x.shape
    cx = ctx.shape[1]
    t = cx + seq
    half_d = d // 2
    g = half_d // N_POOL
    tn = d // 4
    xi, yi, ci = lax.axis_index("x"), lax.axis_index("y"), lax.axis_index("c")
    chip = 2 * xi + yi
    me = 2 * chip + ci
    core_arr = jnp.reshape(ci, (1,)).astype(jnp.int32)
    chip_arr = jnp.reshape(chip, (1,)).astype(jnp.int32)
    chip_core_arr = jnp.stack([chip, ci]).astype(jnp.int32)

    pad = lambda a, rows: jnp.concatenate([a, jnp.zeros((rows - a.shape[0], g), F32)], axis=0)
    small = jnp.concatenate([
        ev_pool_w.reshape(g, g), pad(ev_conv_w.reshape(3, g), 8), pad(od_onorm_g.reshape(2, g), 8),
        pad(lb_logits.reshape(12, g), 16)], axis=0)
    ev_in_g, ev_out_g, small_g, ev_done = allgather_shards([
        put_in_slot(ev_w_in[0], chip_arr, BF16, "cast_ev_w_in"),
        put_in_slot(ev_w_out[0], chip_arr, BF16, "cast_ev_w_out"),
        put_in_slot(small, chip_arr, F32, "place_small")])
    ev_out3 = ev_out_g.reshape(1, d, d)
    pool_w_full = small_g[:, :g].reshape(4, N_POOL, g // 4, g).transpose(1, 0, 2, 3).reshape(N_POOL, g, g)
    conv_w_full = small_g[:, g:g + 3].transpose(1, 0, 2).reshape(3, half_d)
    onorm_full = small_g[:, g + 8:g + 10].reshape(1, d)
    lbl_full = small_g[:, g + 16:g + 28].reshape(4, 2, 3, 2 * g).transpose(1, 2, 0, 3).reshape(2, 3, d)

    c_rows = jnp.concatenate([c + ev_done[0:1, 0:1], jnp.zeros((7, d), F32)], axis=0)
    c_all = allgather8(c_rows, "allgather_c")[:, 0, :]
    s_in = jnp.concatenate([c_all, c_ctx.reshape(1, d), jnp.zeros((7, d), F32)], axis=0)
    ws_ada = ada_w.shape[2]
    ada_b_mine = lax.dynamic_slice(ada_b, (0, chip * ws_ada), (2, ws_ada)).reshape(2, 1, ws_ada)
    s_act, mod_mine = ada_fwd(s_in, ada_w, ada_b_mine, tn)
    mod_all = allgather8(mod_mine.reshape(32, ws_ada), "allgather_mod")
    od_ici = copies_start("gather_od_ici_start", [
        put_in_slot(od_w_in[0], chip_arr, BF16, "cast_od_w_in"),
        put_in_slot(od_w_out[0], chip_arr, BF16, "cast_od_w_out")], 6, plan_gather_ici, mod_all)
    mod_full = mod_all[0::2].reshape(4, 2, 16, ws_ada).transpose(1, 2, 0, 3).reshape(2, 16, 3 * d)
    mod_lat = lax.dynamic_slice(mod_full, (0, me, 0), (2, 1, 3 * d))
    mods = jnp.concatenate([mod_full[:, 8:9], mod_lat], axis=1)
    shift, scale, gate = mods[:, :, :d], mods[:, :, d:2 * d], mods[:, :, 2 * d:]

    xs = jnp.concatenate([ctx[0], x[0]], axis=0)

    h0 = normmod_fwd(xs, pre_g[0:1] + od_ici[3][0:1, 0:1], shift[0], scale[0], cx)
    z0 = mm_nn(h0, ev_in_g, half_d, tn, "mm_ev_in")
    u_a = mix_a_fwd(z0, pool_w_full, ev_pool_scale, cx)
    u_b = mix_b_fwd(z0, conv_w_full, ev_conv_b, cx)
    u = jnp.concatenate([u_a, u_b], axis=1)
    y0 = mm_nn(u, ev_out3, d, tn, "mm_ev_out")[0]
    xs1 = post_fwd(xs, y0, post_g[0:1], gate[0], cx)
    od_d2d = copies_start("gather_od_d2d_start",
                          copies_wait("gather_od_ici_wait", od_ici, plan_gather_ici, xs1)[0],
                          6, plan_gather_d2d, xs1)
    (od_in_g, od_out_g), _ = copies_wait("gather_od_d2d_wait", od_d2d, plan_gather_d2d, od_d2d[3])
    od_out3 = od_out_g.reshape(1, d, d)

    h1 = normmod_fwd(xs1, pre_g[1:2], shift[1], scale[1], cx)
    z1 = mm_nn(h1, od_in_g, d, tn, "mm_od_in")
    o1, r1 = hgrn_fwd(z1, lbl_full, onorm_full, cx)
    y1 = mm_nn(r1, od_out3, d, tn, "mm_od_out")[0]
    sq, dx2 = post_loss(xs1, y1, post_g[1:2], gate[1], loss_target[0], cx)
    loss = lax.psum(sq[0, 0] * (0.5 / d), ("x", "y", "c"))

    dy1, dgate1, dpost1 = post_bwd(dx2, y1, post_g[1:2], gate[1], cx, True)
    dr1 = mm_nt(dy1[None], None, od_out3, tn, "mm_od_out_dx")
    g_od_out = mm_tn(r1, dy1[None], None, d, tn, "mm_od_out_dw")
    dz1, donorm, dlb = hgrn_bwd(z1, lbl_full, onorm_full, o1, dr1, cx)
    dh1 = mm_nt(dz1, None, od_in_g, tn, "mm_od_in_dx")
    g_od_in = mm_tn(h1, dz1, None, od_in_g.shape[2], tn, "mm_od_in_dw")
    dxs1, dpre1, dshift1, dscale1 = normmod_bwd(xs1, dh1, pre_g[1:2], scale[1], dx2, cx, True)

    od_grads = [g_od_in, g_od_out.reshape(4, d // 4, d)]
    half_zone = lambda a, lead, dt: lax.empty((lead, a.shape[1] // 2, a.shape[2]), dt)
    od_ex = copies_start("reduce_od_exchange_start", od_grads + [half_zone(a, 4, a.dtype) for a in od_grads],
                         2, plan_exchange, dxs1)

    dy0, dgate0, dpost0 = post_bwd(dxs1, y0, post_g[0:1] + od_ex[3][0:1, 0:1], gate[0], cx, False)
    du = mm_nt(dy0[None], None, ev_out3, tn, "mm_ev_out_dx")
    g_ev_out = mm_tn(u, dy0[None], None, d, tn, "mm_ev_out_dw")
    od_got, _ = copies_wait("reduce_od_exchange_wait", od_ex, plan_exchange, g_ev_out)
    od_sums = [pair_sum(od_got[i], od_got[2 + i], chip_core_arr) for i in range(2)]
    od_sc = copies_start("reduce_od_scatter_start",
                         [sb for _, sb in od_sums] + [half_zone(a, 3, BF16) for a in od_grads],
                         6, plan_scatter, du)
    dz0a, g_pool_w, dpool_scale = mix_a_bwd(z0, du, pool_w_full, ev_pool_scale + od_sc[3][0:1, 0:1], cx)
    dz0b, dconv_w, dconv_b = mix_b_bwd(z0, du, conv_w_full, ev_conv_b + od_sc[3][0:1, 0:1], cx)
    g_ev_in = mm_tn(h0, dz0a, dz0b, ev_in_g.shape[2], tn, "mm_ev_in_dw")
    ev_grads = [g_ev_in, g_ev_out.reshape(4, d // 4, d), g_pool_w.reshape(4, g, g)]
    ev_ex = copies_start("reduce_ev_exchange_start", ev_grads + [half_zone(a, 4, a.dtype) for a in ev_grads],
                         3, plan_exchange, dpool_scale)
    dh0 = mm_nt(dz0a, dz0b, ev_in_g, tn, "mm_ev_in_dx")
    dxs0, dpre0, dshift0, dscale0 = normmod_bwd(xs, dh0, pre_g[0:1] + ev_ex[3][0:1, 0:1], scale[0], dxs1,
                                                cx, False, True)
    grad_x = dxs0[None]
    ev_got, _ = copies_wait("reduce_ev_exchange_wait", ev_ex, plan_exchange, dxs0)
    ev_sums = [pair_sum(ev_got[i], ev_got[3 + i], chip_core_arr) for i in range(3)]
    od_recv, _ = copies_wait("reduce_od_scatter_wait", od_sc, plan_scatter, dxs0)

    zrow = jnp.zeros((1, d), F32)
    small_rows = jnp.concatenate([
        dpre0, dpre1, dpost0, dpost1,
        jnp.concatenate([dpool_scale, dconv_b], axis=1),
        jnp.concatenate([dconv_w.reshape(1, 3 * half_d), jnp.zeros((1, half_d), F32)], axis=1).reshape(2, d),
        donorm, dlb,
        dshift0[1:2], dscale0[1:2], dgate0[1:2], dshift0[0:1], dscale0[0:1], dgate0[0:1],
        dshift1[1:2], dscale1[1:2], dgate1[1:2], dshift1[0:1], dscale1[0:1], zrow,
        zrow, zrow], axis=0)
    small_all = allgather8(small_rows, "allgather_small")
    ev_sc = copies_start("reduce_ev_scatter_start",
                         [sb for _, sb in ev_sums] + [half_zone(a, 3, BF16) for a in ev_grads],
                         9, plan_scatter, small_all)
    od_sh = copies_start("reduce_od_share_start",
                         [owner_sum(od_sums[i][0], od_recv[2 + i], chip_core_arr) for i in range(2)],
                         2, plan_share, ev_sc[3])
    tot = small_reduce(small_all + ev_sc[3][0:1, 0:1])

    dm_rows = []
    for layer in range(2):
        lat = ROW_MOD + 6 * layer
        dm_lat = small_all[:, lat:lat + 3].reshape(8, 3 * d)
        dm_ctx = tot[lat + 3:lat + 6].reshape(1, 3 * d)
        dm_rows.append(jnp.concatenate([dm_lat, dm_ctx, jnp.zeros((7, 3 * d), F32)], axis=0))
    dm_full = jnp.stack(dm_rows)
    dm_mine = lax.dynamic_slice(dm_full, (0, 0, chip * ws_ada), (2, 16, ws_ada))

    def step(w, gr, m, v, name):
        shape = w.shape
        cols = shape[-1]
        two_d = lambda a: a.reshape(-1, cols)
        dl, mo, vo = adamw(two_d(w), two_d(gr), two_d(m), two_d(v), "adamw_" + name)
        return dl.reshape(shape), mo.reshape(shape), vo.reshape(shape)

    grad_ada_b = tot[24:30].reshape(2, 3 * d)
    grad_pre_g = tot[0:2]
    grad_post_g = tot[2:4]
    grad_ev_pool_scale = tot[4:5, :half_d]
    grad_ev_conv_b = tot[4:5, half_d:]
    conv_w_tot = tot[5:7].reshape(1, 2 * d)[:, :3 * half_d].reshape(3, N_POOL, g)
    grad_ev_conv_w = lax.dynamic_slice(conv_w_tot, (0, chip, 0), (3, 1, g)).reshape(1, 3, g)
    grad_od_onorm_g = lax.dynamic_slice(tot[7:8], (0, chip * 2 * g), (1, 2 * g))
    dlb_mine = lax.dynamic_slice(tot[8:10], (0, chip * 2 * g), (2, 2 * g))
    grad_lb_logits = lb_logits_grad(lb_logits, dlb_mine)
    upd = {
        "ada_b": step(ada_b, grad_ada_b, m_ada_b, v_ada_b, "ada_b"),
        "pre_g": step(pre_g, grad_pre_g, m_pre_g, v_pre_g, "pre_g"),
        "post_g": step(post_g, grad_post_g, m_post_g, v_post_g, "post_g"),
        "ev_pool_scale": step(ev_pool_scale, grad_ev_pool_scale, m_ev_pool_scale, v_ev_pool_scale, "ev_pool_scale"),
        "ev_conv_w": step(ev_conv_w, grad_ev_conv_w, m_ev_conv_w, v_ev_conv_w, "ev_conv_w"),
        "ev_conv_b": step(ev_conv_b, grad_ev_conv_b, m_ev_conv_b, v_ev_conv_b, "ev_conv_b"),
        "od_onorm_g": step(od_onorm_g, grad_od_onorm_g, m_od_onorm_g, v_od_onorm_g, "od_onorm_g"),
        "lb_logits": step(lb_logits, grad_lb_logits, m_lb_logits, v_lb_logits, "lb_logits"),
    }
    grad_ada_w, delta_ada_w, new_m_ada_w, new_v_ada_w, dctx_part = ada_bwd_adamw(
        s_act, dm_mine, ada_w, m_ada_w, v_ada_w)
    upd["ada_w"] = (delta_ada_w, new_m_ada_w, new_v_ada_w)
    (grad_od_w_in, grad_od_w_out), _ = copies_wait("reduce_od_share_wait", od_sh, plan_share, ev_sc[3])
    grad_od_w_in, grad_od_w_out = grad_od_w_in[None], grad_od_w_out[None]
    upd["od_w_in"] = step(od_w_in, grad_od_w_in, m_od_w_in, v_od_w_in, "od_w_in")
    upd["od_w_out"] = step(od_w_out, grad_od_w_out, m_od_w_out, v_od_w_out, "od_w_out")
    done_behind = [dctx_part] + [upd[k][0] for k in (
        "od_w_in", "od_w_out", "ada_b", "pre_g", "post_g", "ev_pool_scale", "ev_conv_w", "ev_conv_b",
        "od_onorm_g", "lb_logits")]
    ev_recv, ev_landed = copies_wait("reduce_ev_scatter_wait", ev_sc, plan_scatter, done_behind)
    grad_ev_w_in, grad_ev_w_out, grad_pool_w = share_halves(
        [owner_sum(ev_sums[i][0], ev_recv[3 + i], chip_core_arr) for i in range(3)])
    dctx_all = allgather8(dctx_part[0] + dctx_part[1] + ev_landed[0:1, 0:1], "allgather_dctx")
    grad_c_ctx = c_ctx_grad(dctx_all, c_ctx.reshape(1, d)).reshape(d)
    grad_ev_w_in, grad_ev_w_out = grad_ev_w_in[None], grad_ev_w_out[None]
    grad_ev_pool_w = grad_pool_w.reshape(1, N_POOL, g // 4, g)
    upd["c_ctx"] = step(c_ctx, grad_c_ctx, m_c_ctx, v_c_ctx, "c_ctx")
    upd["ev_w_in"] = step(ev_w_in, grad_ev_w_in, m_ev_w_in, v_ev_w_in, "ev_w_in")
    upd["ev_pool_w"] = step(ev_pool_w, grad_ev_pool_w, m_ev_pool_w, v_ev_pool_w, "ev_pool_w")
    upd["ev_w_out"] = step(ev_w_out, grad_ev_w_out, m_ev_w_out, v_ev_w_out, "ev_w_out")
    names = ["c_ctx", "ada_w", "ada_b", "pre_g", "post_g", "ev_w_in", "ev_pool_w", "ev_pool_scale",
             "ev_conv_w", "ev_conv_b", "ev_w_out", "od_w_in", "od_onorm_g", "od_w_out", "lb_logits"]
    grads = [grad_c_ctx, grad_ada_w, grad_ada_b, grad_pre_g, grad_post_g, grad_ev_w_in, grad_ev_pool_w,
             grad_ev_pool_scale, grad_ev_conv_w, grad_ev_conv_b, grad_ev_w_out, grad_od_w_in,
             grad_od_onorm_g, grad_od_w_out, grad_lb_logits]
    return (loss, grad_x, *grads, *[upd[k][0] for k in names], *[upd[k][1] for k in names],
            *[upd[k][2] for k in names])
```

```python
import functools

import jax
import jax.numpy as jnp
from jax import lax
from jax.experimental import pallas as pl
from jax.experimental.pallas import tpu as pltpu

EPS = 1e-6
GRID_W_LOG2 = 6
CHUNK = 64
HEAD = 128
N_POOL = 4
ADAM_LR, ADAM_B1, ADAM_B2, ADAM_EPS, ADAM_WD, ADAM_STEP = 0.001, 0.9, 0.999, 1e-08, 0.01, 10
VMEM_LIMIT = 56 * 1024 * 1024
MESH = pl.DeviceIdType.MESH
F32, BF16 = jnp.float32, jnp.bfloat16
ANY = pl.BlockSpec(memory_space=pl.ANY)
VMEM = pl.BlockSpec(memory_space=pltpu.VMEM)


def _cp(**kw):
    return pltpu.CompilerParams(vmem_limit_bytes=VMEM_LIMIT, **kw)


def _silu(x):
    return x * jax.nn.sigmoid(x)


def _dsilu(x):
    s = jax.nn.sigmoid(x)
    return s * (1.0 + x * (1.0 - s))


def _dot(a, b, dims=((1,), (0,)), precision=None):
    return lax.dot_general(a, b, (dims, ((), ())), preferred_element_type=F32, precision=precision)


NN = ((1,), (0,))
NT = ((1,), (1,))
TN = ((0,), (0,))


def _row_block(cx):
    return 256 if cx % 256 == 0 else 128


def normmod_fwd(xs, g, shift, scale, cx):
    t, d = xs.shape
    tm = _row_block(cx)
    nctx = cx // tm

    def body(x_ref, g_ref, sh_ref, sc_ref, h_ref):
        is_ctx = pl.program_id(0) < nctx
        x = x_ref[...]
        rstd = lax.rsqrt(jnp.mean(x * x, axis=-1, keepdims=True) + EPS)
        sc = jnp.where(is_ctx, sc_ref[0:1, :], sc_ref[1:2, :])
        sh = jnp.where(is_ctx, sh_ref[0:1, :], sh_ref[1:2, :])
        h_ref[...] = ((x * rstd) * g_ref[...] * (1.0 + sc) + sh).astype(BF16)

    row = pl.BlockSpec((tm, d), lambda i: (i, 0))
    vec = lambda r: pl.BlockSpec((r, d), lambda i: (0, 0))
    return pl.pallas_call(
        body, name="normmod_fwd", grid=(t // tm,),
        in_specs=[row, vec(1), vec(2), vec(2)], out_specs=row,
        out_shape=jax.ShapeDtypeStruct((t, d), BF16), compiler_params=_cp(),
    )(xs, g, shift, scale)


def normmod_bwd(xs, dh, g, scale, dres, cx, res_is_latent_only, dx_latent_only=False):
    t, d = xs.shape
    tm = _row_block(cx)
    nctx = cx // tm

    def body(x_ref, dh_ref, g_ref, sc_ref, dres_ref, dx_ref, dg_ref, dsh_ref, dsc_ref):
        i = pl.program_id(0)
        is_ctx = i < nctx

        @pl.when(i == 0)
        def _():
            dg_ref[...] = jnp.zeros_like(dg_ref)
            dsh_ref[...] = jnp.zeros_like(dsh_ref)
            dsc_ref[...] = jnp.zeros_like(dsc_ref)

        x = x_ref[...]
        dh = dh_ref[...]
        gv = g_ref[...]
        rstd = lax.rsqrt(jnp.mean(x * x, axis=-1, keepdims=True) + EPS)
        xhat = x * rstd
        sc = jnp.where(is_ctx, sc_ref[0:1, :], sc_ref[1:2, :])
        dsh = jnp.sum(dh, axis=0, keepdims=True)
        dhx = dh * xhat
        dsc = jnp.sum(dhx * gv, axis=0, keepdims=True)
        dg_ref[...] += jnp.sum(dhx * (1.0 + sc), axis=0, keepdims=True)
        zero = jnp.zeros_like(dsh)
        dsh_ref[0:1, :] += jnp.where(is_ctx, dsh, zero)
        dsh_ref[1:2, :] += jnp.where(is_ctx, zero, dsh)
        dsc_ref[0:1, :] += jnp.where(is_ctx, dsc, zero)
        dsc_ref[1:2, :] += jnp.where(is_ctx, zero, dsc)
        dxhat = dh * (gv * (1.0 + sc))
        dx = rstd * (dxhat - xhat * jnp.mean(dxhat * xhat, axis=-1, keepdims=True))
        res = dres_ref[...]
        if res_is_latent_only:
            res = jnp.where(is_ctx, jnp.zeros_like(res), res)
        dx_ref[...] = dx + res

    row = pl.BlockSpec((tm, d), lambda i: (i, 0))
    if res_is_latent_only:
        res_spec = pl.BlockSpec((tm, d), lambda i: (jnp.maximum(i - nctx, 0), 0))
    else:
        res_spec = row
    vec = lambda r: pl.BlockSpec((r, d), lambda i: (0, 0))
    dx_spec = pl.BlockSpec((tm, d), lambda i: (jnp.maximum(i - nctx, 0), 0)) if dx_latent_only else row
    return pl.pallas_call(
        body, name="normmod_bwd", grid=(t // tm,),
        in_specs=[row, row, vec(1), vec(2), res_spec],
        out_specs=[dx_spec, vec(1), vec(2), vec(2)],
        out_shape=[jax.ShapeDtypeStruct((t - cx if dx_latent_only else t, d), F32), jax.ShapeDtypeStruct((1, d), F32),
                   jax.ShapeDtypeStruct((2, d), F32), jax.ShapeDtypeStruct((2, d), F32)],
        compiler_params=_cp(),
    )(xs, dh, g, scale, dres)


def post_fwd(xs, y, pg, gate, cx):
    t, d = xs.shape
    tm = _row_block(cx)
    nctx = cx // tm

    def body(x_ref, y_ref, pg_ref, gate_ref, o_ref):
        is_ctx = pl.program_id(0) < nctx
        y = y_ref[...]
        rstd = lax.rsqrt(jnp.mean(y * y, axis=-1, keepdims=True) + EPS)
        gt = jnp.where(is_ctx, gate_ref[0:1, :], gate_ref[1:2, :])
        o_ref[...] = x_ref[...] + gt * ((y * rstd) * pg_ref[...])

    row = pl.BlockSpec((tm, d), lambda i: (i, 0))
    vec = lambda r: pl.BlockSpec((r, d), lambda i: (0, 0))
    return pl.pallas_call(
        body, name="post_fwd", grid=(t // tm,),
        in_specs=[row, row, vec(1), vec(2)], out_specs=row,
        out_shape=jax.ShapeDtypeStruct((t, d), F32), compiler_params=_cp(),
    )(xs, y, pg, gate)


def post_loss(xs, y, pg, gate, target, cx):
    t, d = xs.shape
    n = y.shape[0]
    tm = _row_block(cx)
    nctx = cx // tm

    def body(x_ref, y_ref, pg_ref, gate_ref, tgt_ref, sq_ref, dx_ref):
        @pl.when(pl.program_id(0) == 0)
        def _():
            sq_ref[...] = jnp.zeros_like(sq_ref)

        y = y_ref[...]
        rstd = lax.rsqrt(jnp.mean(y * y, axis=-1, keepdims=True) + EPS)
        x2 = x_ref[...] + gate_ref[1:2, :] * ((y * rstd) * pg_ref[...])
        err = x2 - tgt_ref[...]
        sq_ref[...] += jnp.sum(err * err)
        dx_ref[...] = err * (1.0 / d)

    row = pl.BlockSpec((tm, d), lambda i: (i, 0))
    xrow = pl.BlockSpec((tm, d), lambda i: (i + nctx, 0))
    vec = lambda r: pl.BlockSpec((r, d), lambda i: (0, 0))
    return pl.pallas_call(
        body, name="post_loss", grid=(n // tm,),
        in_specs=[xrow, row, vec(1), vec(2), row],
        out_specs=[pl.BlockSpec((8, 128), lambda i: (0, 0)), row],
        out_shape=[jax.ShapeDtypeStruct((8, 128), F32), jax.ShapeDtypeStruct((n, d), F32)],
        compiler_params=_cp(),
    )(xs, y, pg, gate, target)


def post_bwd(dxo, y, pg, gate, cx, latent_only):
    m, d = y.shape
    tm = _row_block(cx)
    nctx = 0 if latent_only else cx // tm

    def body(dx_ref, y_ref, pg_ref, gate_ref, dy_ref, dgate_ref, dpg_ref):
        i = pl.program_id(0)
        is_ctx = i < nctx

        @pl.when(i == 0)
        def _():
            dgate_ref[...] = jnp.zeros_like(dgate_ref)
            dpg_ref[...] = jnp.zeros_like(dpg_ref)

        y = y_ref[...]
        dx = dx_ref[...]
        pgv = pg_ref[...]
        rstd = lax.rsqrt(jnp.mean(y * y, axis=-1, keepdims=True) + EPS)
        yhat = y * rstd
        gt = jnp.where(is_ctx, gate_ref[0:1, :], gate_ref[1:2, :])
        dxy = dx * yhat
        dgt = jnp.sum(dxy * pgv, axis=0, keepdims=True)
        zero = jnp.zeros_like(dgt)
        dgate_ref[0:1, :] += jnp.where(is_ctx, dgt, zero)
        dgate_ref[1:2, :] += jnp.where(is_ctx, zero, dgt)
        dpg_ref[...] += jnp.sum(dxy * gt, axis=0, keepdims=True)
        dyhat = dx * (gt * pgv)
        dy = rstd * (dyhat - yhat * jnp.mean(dyhat * yhat, axis=-1, keepdims=True))
        dy_ref[...] = dy.astype(BF16)

    row = pl.BlockSpec((tm, d), lambda i: (i, 0))
    vec = lambda r: pl.BlockSpec((r, d), lambda i: (0, 0))
    return pl.pallas_call(
        body, name="post_bwd", grid=(m // tm,),
        in_specs=[row, row, vec(1), vec(2)], out_specs=[row, vec(2), vec(1)],
        out_shape=[jax.ShapeDtypeStruct((m, d), BF16), jax.ShapeDtypeStruct((2, d), F32),
                   jax.ShapeDtypeStruct((1, d), F32)],
        compiler_params=_cp(),
    )(dxo, y, pg, gate)


def _split_rows(m):
    for cand in (1024, 768, 512, 384, 256, 128):
        if m % cand == 0 and m // cand >= 2:
            return cand
    return m


def mm_nn(a, w3, sec, tn, name):
    m, k = a.shape
    q, _, ws = w3.shape
    n = q * ws
    tpq, tps = ws // tn, sec // tn
    tm = next(c for c in (768, 512, 256, 128) if m % c == 0)

    def body(a_ref, w_ref, o_ref):
        w = w_ref[...]

        def step(i, carry):
            rows = pl.ds(pl.multiple_of(i * tm, tm), tm)
            o_ref[rows, :] = _dot(a_ref[rows, :], w)
            return carry

        lax.fori_loop(0, m // tm, step, 0)

    return pl.pallas_call(
        body, name=name, grid=(n // tn,),
        in_specs=[pl.BlockSpec((m, k), lambda j: (0, 0)),
                  pl.BlockSpec((None, k, tn), lambda j: (j // tpq, 0, j % tpq))],
        out_specs=pl.BlockSpec((None, m, tn), lambda j: (j // tps, 0, j % tps)),
        out_shape=jax.ShapeDtypeStruct((n // sec, m, sec), F32), compiler_params=_cp(),
    )(a, w3)


def _two_stacks(a3, b3, tn):
    sec = a3.shape[2]
    tps = sec // tn
    n1 = a3.shape[0] * tps
    first = lambda j: (jnp.minimum(j, n1 - 1) // tps, jnp.minimum(j, n1 - 1) % tps)
    second = lambda j: (jnp.maximum(j - n1, 0) // tps, jnp.maximum(j - n1, 0) % tps)
    return n1, first, second


def mm_nt(a3, b3, w3, tn, name):
    if b3 is None:
        b3 = a3
    _, m, sec = a3.shape
    q, k, ws = w3.shape
    n = q * ws
    tpq = ws // tn
    mb = _split_rows(m)
    n1, first, second = _two_stacks(a3, b3, tn)

    def body(a_ref, b_ref, w_ref, o_ref):
        j = pl.program_id(1)

        @pl.when(j == 0)
        def _():
            o_ref[...] = jnp.zeros_like(o_ref)

        @pl.when(j < n1)
        def _():
            o_ref[...] += _dot(a_ref[...], w_ref[...], NT)

        @pl.when(j >= n1)
        def _():
            o_ref[...] += _dot(b_ref[...], w_ref[...], NT)

    return pl.pallas_call(
        body, name=name, grid=(m // mb, n // tn),
        in_specs=[pl.BlockSpec((None, mb, tn), lambda i, j: (first(j)[0], i, first(j)[1])),
                  pl.BlockSpec((None, mb, tn), lambda i, j: (second(j)[0], i, second(j)[1])),
                  pl.BlockSpec((None, k, tn), lambda i, j: (j // tpq, 0, j % tpq))],
        out_specs=pl.BlockSpec((mb, k), lambda i, j: (i, 0)),
        out_shape=jax.ShapeDtypeStruct((m, k), F32), compiler_params=_cp(),
    )(a3, b3, w3)


def mm_tn(a, b3, c3, ws, tn, name):
    m, k = a.shape
    sec = b3.shape[2]
    n = (b3.shape[0] + (0 if c3 is None else c3.shape[0])) * sec
    if c3 is None:
        c3 = b3
    tpq = ws // tn
    kb = 256 if k % 256 == 0 else 128
    n1, first, second = _two_stacks(b3, c3, tn)

    def body(a_ref, b_ref, c_ref, o_ref):
        def product(rhs_ref):
            rhs = rhs_ref[...]
            for i in range(k // kb):
                o_ref[i * kb:(i + 1) * kb, :] = _dot(a_ref[:, i * kb:(i + 1) * kb], rhs, TN).astype(BF16)

        @pl.when(pl.program_id(0) < n1)
        def _():
            product(b_ref)

        @pl.when(pl.program_id(0) >= n1)
        def _():
            product(c_ref)

    return pl.pallas_call(
        body, name=name, grid=(n // tn,),
        in_specs=[pl.BlockSpec((m, k), lambda j: (0, 0)),
                  pl.BlockSpec((None, m, tn), lambda j: (first(j)[0], 0, first(j)[1])),
                  pl.BlockSpec((None, m, tn), lambda j: (second(j)[0], 0, second(j)[1]))],
        out_specs=pl.BlockSpec((None, k, tn), lambda j: (j // tpq, 0, j % tpq)),
        out_shape=jax.ShapeDtypeStruct((n // ws, k, ws), BF16), compiler_params=_cp(),
    )(a, b3, c3)


POOL_REACH = 8 << GRID_W_LOG2


def _token_parts(tok, cx):
    lat = tok - cx
    return tok < cx, lat >> GRID_W_LOG2, lat & ((1 << GRID_W_LOG2) - 1)


def _pool_mask(gi, row0, col0, tm, ncols, cx, transposed):
    half = jnp.left_shift(1, gi)
    r = lax.broadcasted_iota(jnp.int32, (tm, 1), 0) + row0
    c = lax.broadcasted_iota(jnp.int32, (1, ncols), 1) + col0
    out_tok, src_tok = (c, r) if transposed else (r, c)
    o_ctx, o_row, o_col = _token_parts(out_tok, cx)
    s_ctx, s_row, s_col = _token_parts(src_tok, cx)

    def inside(o, s):
        return (s >= o - half) & (s <= o + half - 1)

    ctx_hit = o_ctx & s_ctx & inside(out_tok, src_tok)
    lat_hit = (~o_ctx) & (~s_ctx) & inside(o_row, s_row) & inside(o_col, s_col)
    return jnp.where(ctx_hit | lat_hit, 1.0, 0.0).astype(BF16)


def _pool_inv_count(gi, row0, tm, cx, seq):
    half = jnp.left_shift(1, gi)
    r = lax.broadcasted_iota(jnp.int32, (tm, 1), 0) + row0
    is_ctx, row, col = _token_parts(r, cx)

    def count(pos, size):
        return jnp.minimum(pos + half - 1, size - 1) - jnp.maximum(pos - half, 0) + 1

    cnt = jnp.where(is_ctx, count(r, cx), count(row, seq >> GRID_W_LOG2) * count(col, 1 << GRID_W_LOG2))
    return 1.0 / cnt.astype(F32)


def _lat_band(tm):
    side = POOL_REACH // tm
    return side, 2 * side + 1


def _lat_mask(gi, tm, cx, transposed):
    side, band = _lat_band(tm)
    return _pool_mask(gi, cx + side * tm, cx, tm, band * tm, cx, transposed)


def _store_padded_lat(dst_ref, lat, tm):
    side, _ = _lat_band(tm)
    seq = lat.shape[0]
    zeros = jnp.zeros((side * tm, lat.shape[1]), dst_ref.dtype)
    dst_ref[0:side * tm, :] = zeros
    dst_ref[side * tm + seq:, :] = zeros
    dst_ref[side * tm:side * tm + seq, :] = lat.astype(dst_ref.dtype)


def mix_a_fwd(z0, pool_w, pool_scale, cx):
    _, t, half_d = z0.shape
    g = half_d // N_POOL
    seq = t - cx
    tm = _row_block(cx)
    side, band = _lat_band(tm)

    def body(v_ref, ag_ref, w_ref, sc_ref, u_ref, vlat_ref, mask_ref):
        gi = pl.program_id(0)
        w = w_ref[...].astype(BF16)
        sc = sc_ref[...]
        _store_padded_lat(vlat_ref, v_ref[cx:, :], tm)
        mask_ref[...] = _lat_mask(gi, tm, cx, False)

        def finish(row0, window_sum):
            rows = pl.ds(row0, tm)
            pooled = window_sum * _pool_inv_count(gi, row0, tm, cx, seq) - v_ref[rows, :]
            mixed = _dot(pooled.astype(BF16), w) * sc
            u_ref[rows, :] = (mixed * _silu(ag_ref[rows, :])).astype(BF16)

        vctx = v_ref[0:cx, :].astype(BF16)
        for i in range(cx // tm):
            finish(i * tm, _dot(_pool_mask(gi, i * tm, 0, tm, cx, cx, False), vctx))

        def step(j, carry):
            src = vlat_ref[pl.ds(pl.multiple_of(j * tm, tm), band * tm), :]
            finish(pl.multiple_of(cx + j * tm, tm), _dot(mask_ref[...], src))
            return carry

        lax.fori_loop(0, seq // tm, step, 0)

    sec = lambda s: pl.BlockSpec((None, t, g), lambda j: (s, 0, j))
    return pl.pallas_call(
        body, name="mix_a_fwd", grid=(N_POOL,),
        in_specs=[sec(0), sec(1), pl.BlockSpec((None, g, g), lambda j: (j, 0, 0)),
                  pl.BlockSpec((1, g), lambda j: (0, j))],
        out_specs=pl.BlockSpec((t, g), lambda j: (0, j)),
        out_shape=jax.ShapeDtypeStruct((t, half_d), BF16),
        scratch_shapes=[pltpu.VMEM((seq + 2 * side * tm, g), BF16), pltpu.VMEM((tm, band * tm), BF16)],
        compiler_params=_cp(),
    )(z0, z0, pool_w, pool_scale)


def mix_a_bwd(z0, du, pool_w, pool_scale, cx):
    _, t, half_d = z0.shape
    g = half_d // N_POOL
    seq = t - cx
    tm = _row_block(cx)
    gq = g // 4
    side, band = _lat_band(tm)

    def body(v_ref, ag_ref, du_ref, w_ref, sc_ref, dz_ref, dw_ref, dsc_ref,
             vlat_ref, mask_ref, pooled_ref, dmx_ref, dpl_ref, wlat_ref, wctx_ref):
        gi = pl.program_id(0)
        w = w_ref[...].astype(BF16)
        sc = sc_ref[...]
        _store_padded_lat(vlat_ref, v_ref[cx:, :], tm)
        _store_padded_lat(wlat_ref, jnp.zeros((seq, g), BF16), tm)
        mask_ref[...] = _lat_mask(gi, tm, cx, False)

        def first(row0, window_sum, weighted_ref, weighted_row0):
            rows = pl.ds(row0, tm)
            inv = _pool_inv_count(gi, row0, tm, cx, seq)
            pooled = (window_sum * inv - v_ref[rows, :]).astype(BF16)
            pooled_ref[rows, :] = pooled
            mixed = _dot(pooled, w)
            ag = ag_ref[rows, :]
            duv = du_ref[rows, :]
            dz_ref[1, rows, :] = (duv * (mixed * sc) * _dsilu(ag)).astype(BF16)
            dms = duv * _silu(ag)
            dmixed = (dms * sc).astype(BF16)
            dmx_ref[rows, :] = dmixed
            dpooled = _dot(dmixed, w, NT)
            dpl_ref[rows, :] = dpooled
            weighted_ref[pl.ds(weighted_row0, tm), :] = (dpooled * inv).astype(BF16)
            return jnp.sum(dms * mixed, axis=0, keepdims=True)

        dsc = jnp.zeros((1, g), F32)
        vctx = v_ref[0:cx, :].astype(BF16)
        for i in range(cx // tm):
            dsc += first(i * tm, _dot(_pool_mask(gi, i * tm, 0, tm, cx, cx, False), vctx), wctx_ref, i * tm)

        def first_lat(j, acc):
            src = vlat_ref[pl.ds(pl.multiple_of(j * tm, tm), band * tm), :]
            return acc + first(pl.multiple_of(cx + j * tm, tm), _dot(mask_ref[...], src),
                               wlat_ref, pl.multiple_of((side + j) * tm, tm))

        dsc_ref[...] = lax.fori_loop(0, seq // tm, first_lat, dsc)
        dw = _dot(pooled_ref[...], dmx_ref[...], TN)
        for qi in range(4):
            dw_ref[qi] = dw[qi * gq:(qi + 1) * gq, :]

        wctx = wctx_ref[...]
        for i in range(cx // tm):
            rows = pl.ds(i * tm, tm)
            dz_ref[0, rows, :] = (_dot(_pool_mask(gi, i * tm, 0, tm, cx, cx, True), wctx)
                                  - dpl_ref[rows, :]).astype(BF16)
        mask_ref[...] = _lat_mask(gi, tm, cx, True)

        def second_lat(j, carry):
            rows = pl.ds(pl.multiple_of(cx + j * tm, tm), tm)
            src = wlat_ref[pl.ds(pl.multiple_of(j * tm, tm), band * tm), :]
            dz_ref[0, rows, :] = (_dot(mask_ref[...], src) - dpl_ref[rows, :]).astype(BF16)
            return carry

        lax.fori_loop(0, seq // tm, second_lat, 0)

    sec = lambda s: pl.BlockSpec((None, t, g), lambda j: (s, 0, j))
    padded = pltpu.VMEM((seq + 2 * side * tm, g), BF16)
    return pl.pallas_call(
        body, name="mix_a_bwd", grid=(N_POOL,),
        in_specs=[sec(0), sec(1), pl.BlockSpec((t, g), lambda j: (0, j)),
                  pl.BlockSpec((None, g, g), lambda j: (j, 0, 0)),
                  pl.BlockSpec((1, g), lambda j: (0, j))],
        out_specs=[pl.BlockSpec((2, t, g), lambda j: (0, 0, j)),
                   pl.BlockSpec((4, None, gq, g), lambda j: (0, j, 0, 0)),
                   pl.BlockSpec((1, g), lambda j: (0, j))],
        out_shape=[jax.ShapeDtypeStruct((2, t, half_d), BF16),
                   jax.ShapeDtypeStruct((4, N_POOL, gq, g), F32),
                   jax.ShapeDtypeStruct((1, half_d), F32)],
        scratch_shapes=[padded, pltpu.VMEM((tm, band * tm), BF16), pltpu.VMEM((t, g), BF16),
                        pltpu.VMEM((t, g), BF16), pltpu.VMEM((t, g), F32), padded, pltpu.VMEM((cx, g), BF16)],
        compiler_params=_cp(),
    )(z0, z0, du, pool_w, pool_scale)


def _conv_masks(t, cx):
    r = lax.broadcasted_iota(jnp.int32, (t, 1), 0)
    has_prev = jnp.where((r == 0) | (r == cx), 0.0, 1.0)
    has_next = jnp.where((r == cx - 1) | (r == t - 1), 0.0, 1.0)
    return has_prev, has_next


def mix_b_fwd(z0, conv_w, conv_b, cx):
    _, t, half_d = z0.shape
    gb = 128

    def body(bx_ref, bb_ref, bc_ref, bg_ref, w_ref, b_ref, u_ref):
        has_prev, has_next = _conv_masks(t, cx)
        tt = bc_ref[...] * bx_ref[...]
        prev = pltpu.roll(tt, 1, 0) * has_prev
        nxt = pltpu.roll(tt, t - 1, 0) * has_next
        cv = prev * w_ref[0:1, :] + tt * w_ref[1:2, :] + nxt * w_ref[2:3, :] + b_ref[...]
        u_ref[...] = (bb_ref[...] * cv * _silu(bg_ref[...])).astype(BF16)

    sec = lambda s: pl.BlockSpec((None, t, gb), lambda j: (s, 0, j))
    return pl.pallas_call(
        body, name="mix_b_fwd", grid=(half_d // gb,),
        in_specs=[sec(2), sec(3), sec(4), sec(5), pl.BlockSpec((3, gb), lambda j: (0, j)),
                  pl.BlockSpec((1, gb), lambda j: (0, j))],
        out_specs=pl.BlockSpec((t, gb), lambda j: (0, j)),
        out_shape=jax.ShapeDtypeStruct((t, half_d), BF16), compiler_params=_cp(),
    )(z0, z0, z0, z0, conv_w, conv_b)


def mix_b_bwd(z0, du, conv_w, conv_b, cx):
    _, t, half_d = z0.shape
    gb = 128
    off = half_d // gb

    def body(bx_ref, bb_ref, bc_ref, bg_ref, du_ref, w_ref, b_ref, dz_ref, dw_ref, db_ref):
        has_prev, has_next = _conv_masks(t, cx)
        bx, bb, bc, bg = bx_ref[...], bb_ref[...], bc_ref[...], bg_ref[...]
        duv = du_ref[...]
        tt = bc * bx
        prev = pltpu.roll(tt, 1, 0) * has_prev
        nxt = pltpu.roll(tt, t - 1, 0) * has_next
        w0, w1, w2 = w_ref[0:1, :], w_ref[1:2, :], w_ref[2:3, :]
        cv = prev * w0 + tt * w1 + nxt * w2 + b_ref[...]
        sg = _silu(bg)
        dz_ref[1] = (duv * cv * sg).astype(BF16)
        dz_ref[3] = (duv * bb * cv * _dsilu(bg)).astype(BF16)
        dcv = duv * bb * sg
        dw_ref[0:1, :] = jnp.sum(dcv * prev, axis=0, keepdims=True)
        dw_ref[1:2, :] = jnp.sum(dcv * tt, axis=0, keepdims=True)
        dw_ref[2:3, :] = jnp.sum(dcv * nxt, axis=0, keepdims=True)
        db_ref[...] = jnp.sum(dcv, axis=0, keepdims=True)
        dt = (pltpu.roll(dcv * has_prev, t - 1, 0) * w0 + dcv * w1
              + pltpu.roll(dcv * has_next, 1, 0) * w2)
        dz_ref[0] = (dt * bc).astype(BF16)
        dz_ref[2] = (dt * bx).astype(BF16)

    sec = lambda s: pl.BlockSpec((None, t, gb), lambda j: (s, 0, j))
    return pl.pallas_call(
        body, name="mix_b_bwd", grid=(half_d // gb,),
        in_specs=[sec(2), sec(3), sec(4), sec(5), pl.BlockSpec((t, gb), lambda j: (0, j + off)),
                  pl.BlockSpec((3, gb), lambda j: (0, j)), pl.BlockSpec((1, gb), lambda j: (0, j))],
        out_specs=[pl.BlockSpec((4, t, gb), lambda j: (0, 0, j)),
                   pl.BlockSpec((3, gb), lambda j: (0, j)), pl.BlockSpec((1, gb), lambda j: (0, j))],
        out_shape=[jax.ShapeDtypeStruct((4, t, half_d), BF16),
                   jax.ShapeDtypeStruct((3, half_d), F32), jax.ShapeDtypeStruct((1, half_d), F32)],
        compiler_params=_cp(),
    )(z0, z0, z0, z0, du, conv_w, conv_b)


def _lower_bound(lbl_ref, d):
    l0, l1, l2 = lbl_ref[d, 0:1, :], lbl_ref[d, 1:2, :], lbl_ref[d, 2:3, :]
    mx = jnp.maximum(jnp.maximum(l0, l1), l2)
    e0, e1, e2 = jnp.exp(l0 - mx), jnp.exp(l1 - mx), jnp.exp(l2 - mx)
    inv = 1.0 / (e0 + e1 + e2)
    return (e0 + e1) * inv, (e0 * inv, e1 * inv, e2 * inv)


def _chunk_consts(d):
    r = lax.broadcasted_iota(jnp.int32, (CHUNK, CHUNK), 0)
    c = lax.broadcasted_iota(jnp.int32, (CHUNK, CHUNK), 1)
    keep = (c <= r) if d == 0 else (c >= r)
    return jnp.where(keep, 1.0, 0.0).astype(F32), keep


def _chunk_of_step(s, d, nc, ncc):
    if d == 0:
        return s
    return jnp.where(s < ncc, ncc - 1 - s, nc - 1 + ncc - s)


def _chunk_terms(lfc, kc, qc, cum):
    bc = _dot(cum, lfc, precision=lax.Precision.HIGHEST)
    bl = jnp.sum(lfc, axis=0, keepdims=True)
    e = jnp.exp(bc)
    einv = jnp.exp(-bc)
    erem = jnp.exp(bl - bc)
    return e, einv, erem, jnp.exp(bl), qc * e, kc * einv, kc * erem


def hgrn_fwd(z1, lbl, onorm, cx):
    _, t, d = z1.shape
    seq = t - cx
    nc, ncc = t // CHUNK, cx // CHUNK

    def body(zf_ref, zb_ref, v_ref, q_ref, g_ref, lbl_ref, on_ref, o_ref, r_ref,
             lf_ref, k_ref, oacc_ref, st_ref):
        for dr, z_ref in ((0, zf_ref), (1, zb_ref)):
            lbv, _ = _lower_bound(lbl_ref, dr)
            z = z_ref[...]
            lf_ref[...] = jnp.log(lbv + (1.0 - lbv) * jax.nn.sigmoid(z))
            k_ref[...] = (1.0 - lbv) * jax.nn.sigmoid(-z)
            st_ref[...] = jnp.zeros_like(st_ref)
            cum, keep = _chunk_consts(dr)

            def step(s, carry, dr=dr, cum=cum, keep=keep):
                n = _chunk_of_step(s, dr, nc, ncc)
                rows = pl.ds(pl.multiple_of(n * CHUNK, CHUNK), CHUNK)
                vc = v_ref[rows, :].astype(BF16)
                _, _, _, dec, qd, ki, kd = _chunk_terms(lf_ref[rows, :], k_ref[rows, :], q_ref[rows, :], cum)
                qdb = qd.astype(BF16)
                a = jnp.where(keep, _dot(qdb, ki.astype(BF16), NT), 0.0)
                st = st_ref[...]
                oc = _dot(qdb, st.astype(BF16), NT) + _dot(a.astype(BF16), vc)
                st_ref[...] = st * dec + _dot(vc, kd.astype(BF16), TN)
                if dr == 0:
                    oacc_ref[rows, :] = oc
                else:
                    oacc_ref[rows, :] += oc
                return carry

            lax.fori_loop(0, nc, step, 0, unroll=4)

        o = oacc_ref[cx:, :]
        o_ref[...] = o
        rstd = lax.rsqrt(jnp.mean(o * o, axis=-1, keepdims=True) + EPS)
        r_ref[...] = (o * rstd * on_ref[...] * _silu(g_ref[cx:, :])).astype(BF16)

    sec = lambda s: pl.BlockSpec((None, t, HEAD), lambda h: (s, 0, h))
    col = pl.BlockSpec((seq, HEAD), lambda h: (0, h))
    return pl.pallas_call(
        body, name="hgrn_fwd", grid=(d // HEAD,),
        in_specs=[sec(0), sec(1), sec(2), sec(3), sec(4),
                  pl.BlockSpec((2, 3, HEAD), lambda h: (0, 0, h)), pl.BlockSpec((1, HEAD), lambda h: (0, h))],
        out_specs=[col, col],
        out_shape=[jax.ShapeDtypeStruct((seq, d), F32), jax.ShapeDtypeStruct((seq, d), BF16)],
        scratch_shapes=[pltpu.VMEM((t, HEAD), F32), pltpu.VMEM((t, HEAD), F32), pltpu.VMEM((t, HEAD), F32),
                        pltpu.VMEM((HEAD, HEAD), F32)],
        compiler_params=_cp(),
    )(z1, z1, z1, z1, z1, lbl, onorm)


def hgrn_bwd(z1, lbl, onorm, o, dr_out, cx):
    _, t, d = z1.shape
    seq = t - cx
    nc, ncc = t // CHUNK, cx // CHUNK

    def body(zf_ref, zb_ref, v_ref, q_ref, g_ref, lbl_ref, on_ref, o_ref, dr_ref,
             dz_ref, don_ref, dlb_ref,
             lf_ref, k_ref, do_ref, dq_ref, dv_ref, dk_ref, dlf_ref, ssc_ref, dst_ref):
        o = o_ref[...]
        g = g_ref[cx:, :]
        drv = dr_ref[...]
        onv = on_ref[...]
        rstd = lax.rsqrt(jnp.mean(o * o, axis=-1, keepdims=True) + EPS)
        ohat = o * rstd
        sg = _silu(g)
        don_ref[...] = jnp.sum(drv * ohat * sg, axis=0, keepdims=True)
        dz_ref[4, :cx, :] = jnp.zeros((cx, HEAD), BF16)
        dz_ref[4, cx:, :] = (drv * ohat * onv * _dsilu(g)).astype(BF16)
        dohat = drv * onv * sg
        do_ref[:cx, :] = jnp.zeros((cx, HEAD), F32)
        do_ref[cx:, :] = rstd * (dohat - ohat * jnp.mean(dohat * ohat, axis=-1, keepdims=True))

        for dr, z_ref in ((0, zf_ref), (1, zb_ref)):
            lbv, _ = _lower_bound(lbl_ref, dr)
            z = z_ref[...]
            lf_ref[...] = jnp.log(lbv + (1.0 - lbv) * jax.nn.sigmoid(z))
            k_ref[...] = (1.0 - lbv) * jax.nn.sigmoid(-z)
            cum, keep = _chunk_consts(dr)
            cum_t, _ = _chunk_consts(1 - dr)

            st_init = jnp.zeros((HEAD, HEAD), F32)

            def state_step(s, st, dr=dr, cum=cum):
                n = _chunk_of_step(s, dr, nc, ncc)
                rows = pl.ds(pl.multiple_of(n * CHUNK, CHUNK), CHUNK)
                ssc_ref[n] = st
                _, _, _, dec, _, _, kd = _chunk_terms(lf_ref[rows, :], k_ref[rows, :], q_ref[rows, :], cum)
                return st * dec + _dot(v_ref[rows, :].astype(BF16), kd.astype(BF16), TN)

            lax.fori_loop(0, nc, state_step, st_init, unroll=4)
            dst_ref[...] = jnp.zeros_like(dst_ref)

            def grad_step(s2, carry, dr=dr, cum=cum, cum_t=cum_t, keep=keep):
                n = _chunk_of_step(nc - 1 - s2, dr, nc, ncc)
                rows = pl.ds(pl.multiple_of(n * CHUNK, CHUNK), CHUNK)
                vc = v_ref[rows, :].astype(BF16)
                e, einv, erem, dec, qd, ki, kd = _chunk_terms(
                    lf_ref[rows, :], k_ref[rows, :], q_ref[rows, :], cum)
                qdb, kib, kdb = qd.astype(BF16), ki.astype(BF16), kd.astype(BF16)
                doc = do_ref[rows, :].astype(BF16)
                st0 = ssc_ref[n]
                dst = dst_ref[...]
                dstb = dst.astype(BF16)
                a = jnp.where(keep, _dot(qdb, kib, NT), 0.0).astype(BF16)
                da = jnp.where(keep, _dot(doc, vc, NT), 0.0).astype(BF16)
                dqd = _dot(doc, st0.astype(BF16)) + _dot(da, kib)
                dki = _dot(da, qdb, TN)
                dv = _dot(a, doc, TN) + _dot(kdb, dstb, NT)
                dkd = _dot(vc, dstb)
                ddec = jnp.sum(dst * st0, axis=0, keepdims=True)
                dst_ref[...] = _dot(doc, qdb, TN) + dst * dec
                dbc = dqd * qd - dki * ki - dkd * kd
                dbl = jnp.sum(dkd * kd, axis=0, keepdims=True) + ddec * dec
                dlf_ref[rows, :] = _dot(cum_t, dbc, precision=lax.Precision.HIGHEST) + dbl
                dk_ref[rows, :] = dki * einv + dkd * erem
                if dr == 0:
                    dq_ref[rows, :] = dqd * e
                    dv_ref[rows, :] = dv
                else:
                    dq_ref[rows, :] += dqd * e
                    dv_ref[rows, :] += dv
                return carry

            lax.fori_loop(0, nc, grad_step, 0, unroll=2)

            sig = jax.nn.sigmoid(z)
            one_lb = 1.0 - lbv
            f = lbv + one_lb * sig
            dlf = dlf_ref[...]
            dk = dk_ref[...]
            dsig = (dlf / f - dk) * one_lb
            dz_ref[dr] = (dsig * sig * (1.0 - sig)).astype(BF16)
            dlb_ref[dr:dr + 1, :] = jnp.sum((dlf / f - dk) * (1.0 - sig), axis=0, keepdims=True)

        dz_ref[2] = dv_ref[...].astype(BF16)
        dz_ref[3] = dq_ref[...].astype(BF16)

    sec = lambda s: pl.BlockSpec((None, t, HEAD), lambda h: (s, 0, h))
    col = pl.BlockSpec((seq, HEAD), lambda h: (0, h))
    tvec = pltpu.VMEM((t, HEAD), F32)
    return pl.pallas_call(
        body, name="hgrn_bwd", grid=(d // HEAD,),
        in_specs=[sec(0), sec(1), sec(2), sec(3), sec(4),
                  pl.BlockSpec((2, 3, HEAD), lambda h: (0, 0, h)), pl.BlockSpec((1, HEAD), lambda h: (0, h)),
                  col, col],
        out_specs=[pl.BlockSpec((5, t, HEAD), lambda h: (0, 0, h)),
                   pl.BlockSpec((1, HEAD), lambda h: (0, h)), pl.BlockSpec((2, HEAD), lambda h: (0, h))],
        out_shape=[jax.ShapeDtypeStruct((5, t, d), BF16), jax.ShapeDtypeStruct((1, d), F32),
                   jax.ShapeDtypeStruct((2, d), F32)],
        scratch_shapes=[tvec, tvec, tvec, tvec, tvec, tvec, tvec,
                        pltpu.VMEM((nc, HEAD, HEAD), F32), pltpu.VMEM((HEAD, HEAD), F32)],
        compiler_params=_cp(),
    )(z1, z1, z1, z1, z1, lbl, onorm, o, dr_out)


def _gates(z, lbv):
    e = jnp.exp(-jnp.abs(z))
    r = 1.0 / (1.0 + e)
    er = e * r
    pos = z >= 0.0
    sig = jnp.where(pos, r, er)
    nsig = jnp.where(pos, er, r)
    return sig, nsig, lbv + (1.0 - lbv) * sig


def _split3(x):
    hi = x.astype(BF16)
    r1 = x - hi.astype(F32)
    mid = r1.astype(BF16)
    lo = (r1 - mid.astype(F32)).astype(BF16)
    return jnp.concatenate([hi, mid, lo], axis=1)


def _cumsum_chunk(cum, x):
    y = _dot(cum, _split3(x))
    return y[:, :HEAD] + y[:, HEAD:2 * HEAD] + y[:, 2 * HEAD:]


def _chunk_rows(n):
    return pl.ds(pl.multiple_of(n * CHUNK, CHUNK), CHUNK)


def _group(nc, prefer=(4, 3, 2, 1)):
    return next(u for u in prefer if nc % u == 0)


WIDE_GROUP = (12, 6, 4, 3, 2, 1)


def _decay_pass(lf_ref, bc_ref, dec_ref, cum, nc):
    grp = _group(nc, WIDE_GROUP)

    def step(m, carry):
        ns = [m * grp + u for u in range(grp)]
        lfc = [lf_ref[_chunk_rows(n), :] for n in ns]
        bc = [_cumsum_chunk(cum, x) for x in lfc]
        for u, n in enumerate(ns):
            bc_ref[_chunk_rows(n), :] = bc[u]
            dec_ref[n] = jnp.broadcast_to(jnp.exp(jnp.sum(lfc[u], axis=0, keepdims=True)), (8, HEAD))
        return carry

    lax.fori_loop(0, nc // grp, step, 0)


def hgrn_fwd(z1, lbl, onorm, cx):
    _, t, d = z1.shape
    seq = t - cx
    nc, ncc = t // CHUNK, cx // CHUNK

    grp, sgrp = _group(nc, (6, 4, 3, 2, 1)), _group(nc, WIDE_GROUP)

    def body(zf_ref, zb_ref, v_ref, q_ref, g_ref, lbl_ref, on_ref, o_ref, r_ref,
             lf_ref, k_ref, bc_ref, dec_ref, qd_ref, ki_ref, oacc_ref, ds_ref):
        for dr, z_ref in ((0, zf_ref), (1, zb_ref)):
            lbv, _ = _lower_bound(lbl_ref, dr)
            _, nsig, f = _gates(z_ref[...], lbv)
            lf_ref[...] = jnp.log(f)
            k_ref[...] = (1.0 - lbv) * nsig
            cum, keep = _chunk_consts(dr)
            _decay_pass(lf_ref, bc_ref, dec_ref, cum.astype(BF16), nc)
            bc = bc_ref[...]
            qd_ref[...] = (q_ref[...] * jnp.exp(bc)).astype(BF16)
            ki_ref[...] = (k_ref[...] * jnp.exp(-bc)).astype(BF16)

            def local_step(m, carry, dr=dr, keep=keep):
                ns = [m * grp + u for u in range(grp)]
                rows = [_chunk_rows(n) for n in ns]
                qd = [qd_ref[r, :] for r in rows]
                ki = [ki_ref[r, :] for r in rows]
                vc = [v_ref[r, :].astype(BF16) for r in rows]
                sc = [_dot(qd[u], ki[u], NT) for u in range(grp)]
                inc = [_dot(vc[u], ki[u], TN) for u in range(grp)]
                a = [jnp.where(keep, s, 0.0).astype(BF16) for s in sc]
                intra = [_dot(a[u], vc[u]) for u in range(grp)]
                for u in range(grp):
                    ds_ref[ns[u]] = inc[u] * dec_ref[ns[u]][0:1, :]
                    if dr == 0:
                        oacc_ref[rows[u], :] = intra[u]
                    else:
                        oacc_ref[rows[u], :] += intra[u]
                return carry

            lax.fori_loop(0, nc // grp, local_step, 0)

            def state_step(m, st, dr=dr):
                ns = [_chunk_of_step(m * sgrp + u, dr, nc, ncc) for u in range(sgrp)]
                rows = [_chunk_rows(n) for n in ns]
                sts = []
                for n in ns:
                    sts.append(st.astype(BF16))
                    st = st * dec_ref[n][0:1, :] + ds_ref[n]
                inter = [_dot(qd_ref[rows[u], :], sts[u], NT) for u in range(sgrp)]
                for u in range(sgrp):
                    oacc_ref[rows[u], :] += inter[u]
                return st

            lax.fori_loop(0, nc // sgrp, state_step, jnp.zeros((HEAD, HEAD), F32))

        o = oacc_ref[cx:, :]
        o_ref[...] = o
        rstd = lax.rsqrt(jnp.mean(o * o, axis=-1, keepdims=True) + EPS)
        r_ref[...] = (o * rstd * on_ref[...] * _silu(g_ref[cx:, :])).astype(BF16)

    sec = lambda s: pl.BlockSpec((None, t, HEAD), lambda h: (s, 0, h))
    col = pl.BlockSpec((seq, HEAD), lambda h: (0, h))
    tf32, tb16 = pltpu.VMEM((t, HEAD), F32), pltpu.VMEM((t, HEAD), BF16)
    return pl.pallas_call(
        body, name="hgrn_fwd", grid=(d // HEAD,),
        in_specs=[sec(0), sec(1), sec(2), sec(3), sec(4),
                  pl.BlockSpec((2, 3, HEAD), lambda h: (0, 0, h)), pl.BlockSpec((1, HEAD), lambda h: (0, h))],
        out_specs=[col, col],
        out_shape=[jax.ShapeDtypeStruct((seq, d), F32), jax.ShapeDtypeStruct((seq, d), BF16)],
        scratch_shapes=[tf32, tf32, tf32, pltpu.VMEM((nc, 8, HEAD), F32), tb16, tb16, tf32,
                        pltpu.VMEM((nc, HEAD, HEAD), F32)],
        compiler_params=_cp(),
    )(z1, z1, z1, z1, z1, lbl, onorm)


def hgrn_bwd(z1, lbl, onorm, o, dr_out, cx):
    _, t, d = z1.shape
    seq = t - cx
    nc, ncc = t // CHUNK, cx // CHUNK

    grp2 = grp = _group(nc)

    def body(zf_ref, zb_ref, v_ref, q_ref, g_ref, lbl_ref, on_ref, o_ref, dr_ref,
             dz_ref, don_ref, dlb_ref,
             lf_ref, k_ref, bc_ref, dec_ref, qd_ref, ki_ref, do_ref,
             dqd_ref, dki_ref, dq_ref, dv_ref, ds_ref, dsl_ref):
        o = o_ref[...]
        g = g_ref[cx:, :]
        drv = dr_ref[...]
        onv = on_ref[...]
        rstd = lax.rsqrt(jnp.mean(o * o, axis=-1, keepdims=True) + EPS)
        ohat = o * rstd
        sg = _silu(g)
        don_ref[...] = jnp.sum(drv * ohat * sg, axis=0, keepdims=True)
        dz_ref[4, :cx, :] = jnp.zeros((cx, HEAD), BF16)
        dz_ref[4, cx:, :] = (drv * ohat * onv * _dsilu(g)).astype(BF16)
        dohat = drv * onv * sg
        do_ref[:cx, :] = jnp.zeros((cx, HEAD), BF16)
        do_ref[cx:, :] = (rstd * (dohat - ohat * jnp.mean(dohat * ohat, axis=-1, keepdims=True))).astype(BF16)

        for dr, z_ref in ((0, zf_ref), (1, zb_ref)):
            lbv, _ = _lower_bound(lbl_ref, dr)
            _, nsig, f = _gates(z_ref[...], lbv)
            lf_ref[...] = jnp.log(f)
            k_ref[...] = (1.0 - lbv) * nsig
            cum, keep = _chunk_consts(dr)
            cum_t = _chunk_consts(1 - dr)[0].astype(BF16)
            _decay_pass(lf_ref, bc_ref, dec_ref, cum.astype(BF16), nc)
            bc = bc_ref[...]
            qd_ref[...] = (q_ref[...] * jnp.exp(bc)).astype(BF16)
            ki_ref[...] = (k_ref[...] * jnp.exp(-bc)).astype(BF16)

            def local_step(m, carry, dr=dr, keep=keep):
                ns = [m * grp + u for u in range(grp)]
                rows = [_chunk_rows(n) for n in ns]
                rng = range(grp)
                qd = [qd_ref[r, :] for r in rows]
                ki = [ki_ref[r, :] for r in rows]
                doc = [do_ref[r, :] for r in rows]
                vc = [v_ref[r, :].astype(BF16) for r in rows]
                sc = [_dot(qd[u], ki[u], NT) for u in rng]
                dsc = [_dot(doc[u], vc[u], NT) for u in rng]
                inc = [_dot(vc[u], ki[u], TN) for u in rng]
                dinc = [_dot(doc[u], qd[u], TN) for u in rng]
                a = [jnp.where(keep, s, 0.0).astype(BF16) for s in sc]
                da = [jnp.where(keep, s, 0.0).astype(BF16) for s in dsc]
                dqd = [_dot(da[u], ki[u]) for u in rng]
                dki = [_dot(da[u], qd[u], TN) for u in rng]
                dv = [_dot(a[u], doc[u], TN) for u in rng]
                for u in rng:
                    ds_ref[ns[u]] = inc[u] * dec_ref[ns[u]][0:1, :]
                    dsl_ref[ns[u]] = dinc[u]
                    dqd_ref[rows[u], :] = dqd[u]
                    dki_ref[rows[u], :] = dki[u]
                    if dr == 0:
                        dv_ref[rows[u], :] = dv[u]
                    else:
                        dv_ref[rows[u], :] += dv[u]
                return carry

            lax.fori_loop(0, nc // grp, local_step, 0)

            def state_step(s, st, dr=dr):
                n = _chunk_of_step(s, dr, nc, ncc)
                inc = ds_ref[n]
                ds_ref[n] = st
                return st * dec_ref[n][0:1, :] + inc

            lax.fori_loop(0, nc, state_step, jnp.zeros((HEAD, HEAD), F32), unroll=4)

            def dstate_step(s, dst, dr=dr):
                n = _chunk_of_step(nc - 1 - s, dr, nc, ncc)
                inc = dsl_ref[n]
                dsl_ref[n] = dst
                return inc + dst * dec_ref[n][0:1, :]

            lax.fori_loop(0, nc, dstate_step, jnp.zeros((HEAD, HEAD), F32), unroll=4)

            def grad_step(m, carry, dr=dr, cum_t=cum_t):
                ns = [m * grp2 + u for u in range(grp2)]
                rows = [_chunk_rows(n) for n in ns]
                rng = range(grp2)
                st0 = [ds_ref[n] for n in ns]
                dst = [dsl_ref[n] for n in ns]
                dstb = [x.astype(BF16) for x in dst]
                dec = [dec_ref[n][0:1, :] for n in ns]
                doc = [do_ref[r, :] for r in rows]
                vc = [v_ref[r, :].astype(BF16) for r in rows]
                e = [jnp.exp(bc_ref[r, :]) for r in rows]
                einv = [jnp.exp(-bc_ref[r, :]) for r in rows]
                qd = [q_ref[rows[u], :] * e[u] for u in rng]
                ki = [k_ref[rows[u], :] * einv[u] for u in rng]
                kd = [ki[u] * dec[u] for u in rng]
                dqd_st = [_dot(doc[u], st0[u].astype(BF16)) for u in rng]
                dkd = [_dot(vc[u], dstb[u]) for u in rng]
                dv_st = [_dot(kd[u].astype(BF16), dstb[u], NT) for u in rng]
                dqd = [dqd_ref[rows[u], :] + dqd_st[u] for u in rng]
                dki = [dki_ref[r, :] for r in rows]
                dbc = [dqd[u] * qd[u] - dki[u] * ki[u] - dkd[u] * kd[u] for u in rng]
                cs = [_cumsum_chunk(cum_t, x) for x in dbc]
                for u in rng:
                    ddec = jnp.sum(dst[u] * st0[u], axis=0, keepdims=True)
                    dbl = jnp.sum(dkd[u] * kd[u], axis=0, keepdims=True) + ddec * dec[u]
                    dv_ref[rows[u], :] += dv_st[u]
                    dqd_ref[rows[u], :] = cs[u] + dbl
                    dki_ref[rows[u], :] = dki[u] * einv[u] + dkd[u] * (einv[u] * dec[u])
                    if dr == 0:
                        dq_ref[rows[u], :] = dqd[u] * e[u]
                    else:
                        dq_ref[rows[u], :] += dqd[u] * e[u]
                return carry

            lax.fori_loop(0, nc // grp2, grad_step, 0)

            sig, nsig, f = _gates(z_ref[...], lbv)
            common = (dqd_ref[...] / f - dki_ref[...]) * nsig
            dz_ref[dr] = (common * ((1.0 - lbv) * sig)).astype(BF16)
            dlb_ref[dr:dr + 1, :] = jnp.sum(common, axis=0, keepdims=True)

        dz_ref[2] = dv_ref[...].astype(BF16)
        dz_ref[3] = dq_ref[...].astype(BF16)

    sec = lambda s: pl.BlockSpec((None, t, HEAD), lambda h: (s, 0, h))
    col = pl.BlockSpec((seq, HEAD), lambda h: (0, h))
    tf32, tb16 = pltpu.VMEM((t, HEAD), F32), pltpu.VMEM((t, HEAD), BF16)
    states = pltpu.VMEM((nc, HEAD, HEAD), F32)
    return pl.pallas_call(
        body, name="hgrn_bwd", grid=(d // HEAD,),
        in_specs=[sec(0), sec(1), sec(2), sec(3), sec(4),
                  pl.BlockSpec((2, 3, HEAD), lambda h: (0, 0, h)), pl.BlockSpec((1, HEAD), lambda h: (0, h)),
                  col, col],
        out_specs=[pl.BlockSpec((5, t, HEAD), lambda h: (0, 0, h)),
                   pl.BlockSpec((1, HEAD), lambda h: (0, h)), pl.BlockSpec((2, HEAD), lambda h: (0, h))],
        out_shape=[jax.ShapeDtypeStruct((5, t, d), BF16), jax.ShapeDtypeStruct((1, d), F32),
                   jax.ShapeDtypeStruct((2, d), F32)],
        scratch_shapes=[tf32, tf32, tf32, pltpu.VMEM((nc, 8, HEAD), F32), tb16, tb16, tb16,
                        tf32, tf32, tf32, tf32, states, states],
        compiler_params=_cp(),
    )(z1, z1, z1, z1, z1, lbl, onorm, o, dr_out)


def _place():
    x, y, c = lax.axis_index("x"), lax.axis_index("y"), lax.axis_index("c")
    chips = [(1 - x, y), (x, 1 - y), (1 - x, 1 - y)]
    return x, y, c, chips


def _relay_chips():
    x, y, c, _ = _place()
    first = c == 0
    near = (jnp.where(first, 1 - x, x), jnp.where(first, y, 1 - y))
    far = (jnp.where(first, x, 1 - x), jnp.where(first, 1 - y, y))
    return near, far, (1 - x, 1 - y)


def allgather_shards(bufs):
    n = len(bufs)

    def body(*refs):
        outs = refs[n:2 * n]
        done_ref, send_sems, recv_sems = refs[2 * n:]
        done_ref[...] = jnp.zeros((8, 128), F32)
        x, y, c, _ = _place()
        me = (x, y, c)
        p = 2 * x + y
        near, far, diag = _relay_chips()
        half = [pl.ds(c * (s.shape[1] // 2), s.shape[1] // 2) for s in bufs]
        other = [pl.ds((1 - c) * (s.shape[1] // 2), s.shape[1] // 2) for s in bufs]
        slot = lambda chip: 2 * chip[0] + chip[1]

        def remote(i, k, ref, to):
            return pltpu.make_async_remote_copy(src_ref=ref, dst_ref=ref, send_sem=send_sems.at[6 * i + k],
                                                recv_sem=recv_sems.at[6 * i + k], device_id=to, device_id_type=MESH)

        sends = []

        def send(i, k, ref, to):
            cp = remote(i, k, ref, to)
            cp.start()
            sends.append(cp)

        for i in range(n):
            mine = outs[i].at[p, half[i]]
            send(i, 0, mine, (*near, c))
            send(i, 1, mine, (*far, c))
        for i in range(n):
            landed = outs[i].at[slot(near), half[i]]
            remote(i, 0, landed, me).wait_recv()
            send(i, 2, landed, (*far, c))
            send(i, 3, landed, (x, y, 1 - c))
        for i in range(n):
            landed = outs[i].at[slot(far), half[i]]
            remote(i, 1, landed, me).wait_recv()
            send(i, 4, landed, (x, y, 1 - c))
        for i in range(n):
            landed = outs[i].at[slot(diag), half[i]]
            remote(i, 2, landed, me).wait_recv()
            send(i, 5, landed, (x, y, 1 - c))
        for i in range(n):
            for k, chip in ((3, far), (4, near), (5, diag)):
                remote(i, k, outs[i].at[slot(chip), other[i]], me).wait_recv()
        for cp in sends:
            cp.wait_send()

    return pl.pallas_call(
        body, name="allgather_shards",
        in_specs=[ANY] * n, out_specs=[ANY] * n + [VMEM],
        out_shape=[jax.ShapeDtypeStruct(s.shape, s.dtype) for s in bufs] + [jax.ShapeDtypeStruct((8, 128), F32)],
        input_output_aliases={i: i for i in range(n)},
        scratch_shapes=[pltpu.SemaphoreType.DMA((6 * n,)), pltpu.SemaphoreType.DMA((6 * n,))],
        compiler_params=pltpu.CompilerParams(has_side_effects=True),
    )(*bufs)


def exchange_halves(grads):
    n = len(grads)

    def body(*refs):
        ins, outs = refs[:n], refs[n:2 * n]
        send_sems, recv_sems = refs[2 * n:]
        x, y, c, _ = _place()
        copies = []
        for i in range(n):
            hr = grads[i].shape[1] // 2
            cp = pltpu.make_async_remote_copy(
                src_ref=ins[i].at[:, pl.ds((1 - c) * hr, hr)], dst_ref=outs[i],
                send_sem=send_sems.at[i], recv_sem=recv_sems.at[i],
                device_id=(x, y, 1 - c), device_id_type=MESH)
            cp.start()
            copies.append(cp)
        for cp in copies:
            cp.wait()

    return pl.pallas_call(
        body, name="exchange_halves",
        in_specs=[ANY] * n, out_specs=[ANY] * n,
        out_shape=[jax.ShapeDtypeStruct((4, g.shape[1] // 2, g.shape[2]), g.dtype) for g in grads],
        scratch_shapes=[pltpu.SemaphoreType.DMA((n,)), pltpu.SemaphoreType.DMA((n,))],
        compiler_params=pltpu.CompilerParams(has_side_effects=True),
    )(*grads)


def pair_sum(grad, got, chip_core):
    _, r, cc = grad.shape
    hr = r // 2
    tr = 256 if hr % 256 == 0 else hr
    nb = hr // tr

    def body(cc_ref, a_ref, b_ref, own_ref, sb_ref):
        s = a_ref[...].astype(F32) + b_ref[...].astype(F32)
        sb_ref[...] = s.astype(BF16)

        @pl.when(pl.program_id(1) == cc_ref[0])
        def _():
            own_ref[...] = s

    grid_spec = pltpu.PrefetchScalarGridSpec(
        num_scalar_prefetch=1, grid=(nb, 4),
        in_specs=[pl.BlockSpec((None, tr, cc), lambda i, qi, cc_ref: (qi, cc_ref[1] * nb + i, 0)),
                  pl.BlockSpec((None, tr, cc), lambda i, qi, cc_ref: (qi, i, 0))],
        out_specs=[pl.BlockSpec((tr, cc), lambda i, qi, cc_ref: (i, 0)),
                   pl.BlockSpec((None, tr, cc), lambda i, qi, cc_ref: (qi, i, 0))])
    return pl.pallas_call(
        body, name="pair_sum", grid_spec=grid_spec,
        out_shape=[jax.ShapeDtypeStruct((hr, cc), F32), jax.ShapeDtypeStruct((4, hr, cc), BF16)],
        compiler_params=_cp(),
    )(chip_core, grad, got)


def scatter_to_owners(parts):
    n = len(parts)

    def body(*refs):
        ins, outs = refs[:n], refs[n:2 * n]
        send_sems, recv_sems = refs[2 * n:]
        x, y, c, chips = _place()
        copies = []
        for i in range(n):
            for j, chip in enumerate(chips):
                cp = pltpu.make_async_remote_copy(
                    src_ref=ins[i].at[2 * chip[0] + chip[1]], dst_ref=outs[i].at[j],
                    send_sem=send_sems.at[3 * i + j], recv_sem=recv_sems.at[3 * i + j],
                    device_id=(*chip, c), device_id_type=MESH)
                cp.start()
                copies.append(cp)
        for cp in copies:
            cp.wait()

    return pl.pallas_call(
        body, name="scatter_to_owners",
        in_specs=[ANY] * n, out_specs=[ANY] * n,
        out_shape=[jax.ShapeDtypeStruct((3,) + p.shape[1:], p.dtype) for p in parts],
        scratch_shapes=[pltpu.SemaphoreType.DMA((3 * n,)), pltpu.SemaphoreType.DMA((3 * n,))],
        compiler_params=pltpu.CompilerParams(has_side_effects=True),
    )(*parts)


def owner_sum(own, got, chip_core):
    hr, cc = own.shape
    tr = 256 if hr % 256 == 0 else hr
    nb = hr // tr

    def body(cc_ref, a_ref, b_ref, o_ref):
        s = a_ref[...] + b_ref[0].astype(F32)
        s = s + b_ref[1].astype(F32)
        o_ref[...] = s + b_ref[2].astype(F32)

    grid_spec = pltpu.PrefetchScalarGridSpec(
        num_scalar_prefetch=1, grid=(nb,),
        in_specs=[pl.BlockSpec((tr, cc), lambda i, cc_ref: (i, 0)),
                  pl.BlockSpec((3, tr, cc), lambda i, cc_ref: (0, i, 0))],
        out_specs=pl.BlockSpec((tr, cc), lambda i, cc_ref: (cc_ref[1] * nb + i, 0)))
    return pl.pallas_call(
        body, name="owner_sum", grid_spec=grid_spec,
        out_shape=jax.ShapeDtypeStruct((2 * hr, cc), F32), compiler_params=_cp(),
    )(chip_core, own, got)


def share_halves(bufs):
    n = len(bufs)

    def body(*refs):
        outs = refs[n:2 * n]
        send_sems, recv_sems = refs[2 * n:]
        x, y, c, _ = _place()
        copies = []
        for i in range(n):
            hr = bufs[i].shape[0] // 2
            mine = outs[i].at[pl.ds(c * hr, hr)]
            cp = pltpu.make_async_remote_copy(
                src_ref=mine, dst_ref=mine, send_sem=send_sems.at[i], recv_sem=recv_sems.at[i],
                device_id=(x, y, 1 - c), device_id_type=MESH)
            cp.start()
            copies.append((cp, outs[i].at[pl.ds((1 - c) * hr, hr)]))
        for i, (cp, theirs) in enumerate(copies):
            cp.wait_send()
            pltpu.make_async_remote_copy(
                src_ref=theirs, dst_ref=theirs, send_sem=send_sems.at[i], recv_sem=recv_sems.at[i],
                device_id=(x, y, c), device_id_type=MESH).wait_recv()

    return pl.pallas_call(
        body, name="share_halves",
        in_specs=[ANY] * n, out_specs=[ANY] * n,
        out_shape=[jax.ShapeDtypeStruct(b.shape, b.dtype) for b in bufs],
        input_output_aliases={i: i for i in range(n)},
        scratch_shapes=[pltpu.SemaphoreType.DMA((n,)), pltpu.SemaphoreType.DMA((n,))],
        compiler_params=pltpu.CompilerParams(has_side_effects=True),
    )(*bufs)


def allgather8(v, name):
    r, n = v.shape

    def body(v_ref, out_ref, send_sems, recv_sems):
        x, y, c, _ = _place()
        me = 4 * x + 2 * y + c
        out_ref[me] = v_ref[...]

        def copy(k, slot, to):
            return pltpu.make_async_remote_copy(
                src_ref=v_ref, dst_ref=out_ref.at[slot], send_sem=send_sems.at[k - 1],
                recv_sem=recv_sems.at[k - 1], device_id=to, device_id_type=MESH)

        peers = []
        for k in range(1, 8):
            px = 1 - x if (k >> 2) & 1 else x
            py = 1 - y if (k >> 1) & 1 else y
            pc = 1 - c if k & 1 else c
            peers.append((px, py, pc))
            copy(k, me, (px, py, pc)).start()
        for k, (px, py, pc) in enumerate(peers, start=1):
            copy(k, 4 * px + 2 * py + pc, (x, y, c)).wait_recv()
        for k, peer in enumerate(peers, start=1):
            copy(k, me, peer).wait_send()

    return pl.pallas_call(
        body, name=name, in_specs=[VMEM], out_specs=VMEM,
        out_shape=jax.ShapeDtypeStruct((8, r, n), v.dtype),
        scratch_shapes=[pltpu.SemaphoreType.DMA((7,)), pltpu.SemaphoreType.DMA((7,))],
        compiler_params=_cp(has_side_effects=True),
    )(v)


HBM = pl.BlockSpec(memory_space=pltpu.HBM)
SEM = pl.BlockSpec(memory_space=pltpu.SEMAPHORE)
DATAFLOW = pltpu.SideEffectType.DATAFLOW_SIDE_EFFECTING


def _descriptors(plan, refs, send_sems, recv_sems, arrivals=True):
    x, y, c, _ = _place()
    sends, recvs = plan(refs)
    out = [pltpu.make_async_remote_copy(src_ref=src, dst_ref=dst, send_sem=send_sems.at[k],
                                        recv_sem=recv_sems.at[k], device_id=to, device_id_type=MESH)
           for k, (src, dst, to) in enumerate(sends)]
    if not arrivals:
        return out, []
    inn = [pltpu.make_async_remote_copy(src_ref=land, dst_ref=land, send_sem=send_sems.at[k],
                                        recv_sem=recv_sems.at[k], device_id=(x, y, c), device_id_type=MESH)
           for k, land in enumerate(recvs)]
    return out, inn


def copies_start(name, arrays, n_copies, plan, after):
    na = len(arrays)

    def body(*refs):
        out, _ = _descriptors(plan, refs[:na], refs[na + 1], refs[na + 2], arrivals=False)
        for cp in out:
            cp.start()
        refs[-1][...] = jnp.zeros((8, 128), F32)

    res = pl.pallas_call(
        body, name=name,
        out_shape=(pltpu.SemaphoreType.DMA((n_copies,)), pltpu.SemaphoreType.DMA((n_copies,)),
                   *[pltpu.HBM(a.shape, a.dtype) for a in arrays], jax.ShapeDtypeStruct((8, 128), F32)),
        in_specs=[HBM] * na + [ANY], out_specs=(SEM, SEM, *[HBM] * na, VMEM),
        input_output_aliases={i: i + 2 for i in range(na)},
        compiler_params=pltpu.CompilerParams(has_side_effects=DATAFLOW),
    )(*[pltpu.with_memory_space_constraint(a, pltpu.HBM) for a in arrays], after)
    return res[0], res[1], list(res[2:2 + na]), res[-1]


def copies_wait(name, started, plan, after):
    send_sems, recv_sems, arrays, _ = started
    na = len(arrays)
    after = list(after) if isinstance(after, (list, tuple)) else [after]

    def body(*refs):
        out, inn = _descriptors(plan, refs[:na], refs[na], refs[na + 1])
        for cp in out:
            cp.wait_send()
        for cp in inn:
            cp.wait_recv()
        refs[-1][...] = jnp.zeros((8, 128), F32)

    res = pl.pallas_call(
        body, name=name,
        out_shape=(*[pltpu.HBM(a.shape, a.dtype) for a in arrays], jax.ShapeDtypeStruct((8, 128), F32)),
        in_specs=[HBM] * na + [SEM, SEM] + [ANY] * len(after), out_specs=(*[HBM] * na, VMEM),
        input_output_aliases={i: i for i in range(na)},
        compiler_params=pltpu.CompilerParams(has_side_effects=DATAFLOW),
    )(*arrays, send_sems, recv_sems, *after)
    return list(res[:na]), res[-1]


def _rows_half(r, c):
    return pl.ds(c * (r // 2), r // 2), pl.ds((1 - c) * (r // 2), r // 2)


def plan_gather_neighbours(refs):
    x, y, c, _ = _place()
    p = 2 * x + y
    near, far, _ = _relay_chips()
    sends, recvs = [], []
    for buf in refs:
        mine, _ = _rows_half(buf.shape[1], c)
        for chip in (near, far):
            sends.append((buf.at[p, mine], buf.at[p, mine], (*chip, c)))
            recvs.append(buf.at[2 * chip[0] + chip[1], mine])
    return sends, recvs


def plan_gather_relay(refs):
    _, _, c, _ = _place()
    near, far, diag = _relay_chips()
    sends, recvs = [], []
    for buf in refs:
        mine, _ = _rows_half(buf.shape[1], c)
        landed = buf.at[2 * near[0] + near[1], mine]
        sends.append((landed, landed, (*far, c)))
        recvs.append(buf.at[2 * diag[0] + diag[1], mine])
    return sends, recvs


def plan_gather_d2d(refs):
    x, y, c, _ = _place()
    near, far, diag = _relay_chips()
    sends, recvs = [], []
    for buf in refs:
        mine, theirs = _rows_half(buf.shape[1], c)
        for sent, got in ((near, far), (far, near), (diag, diag)):
            landed = buf.at[2 * sent[0] + sent[1], mine]
            sends.append((landed, landed, (x, y, 1 - c)))
            recvs.append(buf.at[2 * got[0] + got[1], theirs])
    return sends, recvs


def plan_exchange(refs):
    x, y, c, _ = _place()
    n = len(refs) // 2
    sends, recvs = [], []
    for grad, land in zip(refs[:n], refs[n:]):
        _, theirs = _rows_half(grad.shape[1], c)
        sends.append((grad.at[:, theirs], land, (x, y, 1 - c)))
        recvs.append(land)
    return sends, recvs


def plan_scatter(refs):
    x, y, c, chips = _place()
    n = len(refs) // 2
    sends, recvs = [], []
    for part, land in zip(refs[:n], refs[n:]):
        for j, chip in enumerate(chips):
            sends.append((part.at[2 * chip[0] + chip[1]], land.at[j], (*chip, c)))
            recvs.append(land.at[j])
    return sends, recvs


def plan_share(refs):
    x, y, c, _ = _place()
    sends, recvs = [], []
    for buf in refs:
        mine, theirs = _rows_half(buf.shape[0], c)
        sends.append((buf.at[mine], buf.at[mine], (x, y, 1 - c)))
        recvs.append(buf.at[theirs])
    return sends, recvs


def put_in_slot(w, chip, dtype, name):
    r, c = w.shape
    tr = 256 if r % 256 == 0 else r

    def body(chip_ref, w_ref, o_ref):
        o_ref[...] = w_ref[...].astype(dtype)

    grid_spec = pltpu.PrefetchScalarGridSpec(
        num_scalar_prefetch=1, grid=(r // tr,),
        in_specs=[pl.BlockSpec((tr, c), lambda i, chip_ref: (i, 0))],
        out_specs=pl.BlockSpec((None, tr, c), lambda i, chip_ref: (chip_ref[0], i, 0)))
    return pl.pallas_call(body, name=name, grid_spec=grid_spec,
                          out_shape=jax.ShapeDtypeStruct((4, r, c), dtype), compiler_params=_cp())(chip, w)


def ada_fwd(s_in, ada_w, ada_b, tn):
    nl, d, ws = ada_w.shape

    def body(s_ref, w_ref, b_ref, so_ref, mod_ref):
        s = _silu(s_ref[...])
        so_ref[...] = s
        mod_ref[...] = _dot(s.astype(BF16), w_ref[...].astype(BF16)) + b_ref[...]

    return pl.pallas_call(
        body, name="ada_fwd", grid=(nl, ws // tn),
        in_specs=[pl.BlockSpec((16, d), lambda l, j: (0, 0)),
                  pl.BlockSpec((None, d, tn), lambda l, j: (l, 0, j)),
                  pl.BlockSpec((None, 1, tn), lambda l, j: (l, 0, j))],
        out_specs=[pl.BlockSpec((16, d), lambda l, j: (0, 0)),
                   pl.BlockSpec((None, 16, tn), lambda l, j: (l, 0, j))],
        out_shape=[jax.ShapeDtypeStruct((16, d), F32), jax.ShapeDtypeStruct((nl, 16, ws), F32)],
        compiler_params=_cp(),
    )(s_in, ada_w, ada_b)


def _adamw_math(w, g, m, v):
    m = ADAM_B1 * m + (1.0 - ADAM_B1) * g
    v = ADAM_B2 * v + (1.0 - ADAM_B2) * (g * g)
    m_hat = m / (1.0 - ADAM_B1 ** ADAM_STEP)
    v_hat = v / (1.0 - ADAM_B2 ** ADAM_STEP)
    delta = -ADAM_LR * (m_hat / (jnp.sqrt(v_hat) + ADAM_EPS) + ADAM_WD * w)
    return delta, m, v


def ada_bwd_adamw(s, dm, w, m, v):
    nl, d, ws = w.shape
    tr = 256 if d % 256 == 0 else 128

    def body(s_ref, dm_ref, w_ref, m_ref, v_ref, g_ref, dl_ref, mo_ref, vo_ref, dc_ref):
        dmv = dm_ref[...].astype(BF16)
        wv = w_ref[...]
        g = _dot(s_ref[...].astype(BF16), dmv, TN)
        g_ref[...] = g
        dl_ref[...], mo_ref[...], vo_ref[...] = _adamw_math(wv, g, m_ref[...], v_ref[...])
        dc_ref[...] = _dot(dmv[8:16, :], wv.astype(BF16), NT)

    wblk = pl.BlockSpec((None, tr, ws), lambda l, i: (l, i, 0))
    wshape = jax.ShapeDtypeStruct((nl, d, ws), F32)
    return pl.pallas_call(
        body, name="ada_bwd_adamw", grid=(nl, d // tr),
        in_specs=[pl.BlockSpec((16, tr), lambda l, i: (0, i)),
                  pl.BlockSpec((None, 16, ws), lambda l, i: (l, 0, 0)), wblk, wblk, wblk],
        out_specs=[wblk, wblk, wblk, wblk, pl.BlockSpec((None, 8, tr), lambda l, i: (l, 0, i))],
        out_shape=[wshape, wshape, wshape, wshape, jax.ShapeDtypeStruct((nl, 8, d), F32)],
        compiler_params=_cp(),
    )(s, dm, w, m, v)


def adamw(w, g, m, v, name):
    r, c = w.shape
    tr = 256 if r % 256 == 0 else r

    def body(w_ref, g_ref, m_ref, v_ref, dl_ref, mo_ref, vo_ref):
        dl_ref[...], mo_ref[...], vo_ref[...] = _adamw_math(w_ref[...], g_ref[...], m_ref[...], v_ref[...])

    blk = pl.BlockSpec((tr, c), lambda i: (i, 0))
    shape = jax.ShapeDtypeStruct((r, c), F32)
    return pl.pallas_call(body, name=name, grid=(r // tr,), in_specs=[blk] * 4, out_specs=[blk] * 3,
                          out_shape=[shape] * 3, compiler_params=_cp())(w, g, m, v)


SMALL_ROWS = 24
ROW_MOD = 10


def small_reduce(gathered):
    _, rows, d = gathered.shape

    def body(g_ref, o_ref):
        tot = g_ref[0]
        for b in range(1, 8):
            tot = tot + g_ref[b]
        o_ref[0:rows, :] = tot
        for layer in range(2):
            lat = ROW_MOD + 6 * layer
            o_ref[24 + 3 * layer:27 + 3 * layer, :] = tot[lat:lat + 3, :] + tot[lat + 3:lat + 6, :]
        o_ref[30:32, :] = jnp.zeros((2, d), F32)

    return pl.pallas_call(body, name="small_reduce", in_specs=[VMEM], out_specs=VMEM,
                          out_shape=jax.ShapeDtypeStruct((32, d), F32), compiler_params=_cp())(gathered)


def lb_logits_grad(lbl, dlb):
    _, _, n = lbl.shape

    def body(l_ref, d_ref, o_ref):
        for dr in range(2):
            _, (p0, p1, p2) = _lower_bound(l_ref, dr)
            dv = d_ref[dr:dr + 1, :]
            o_ref[dr, 0:1, :] = p0 * p2 * dv
            o_ref[dr, 1:2, :] = p1 * p2 * dv
            o_ref[dr, 2:3, :] = -p2 * (p0 + p1) * dv

    return pl.pallas_call(body, name="lb_logits_grad", in_specs=[VMEM, VMEM], out_specs=VMEM,
                          out_shape=jax.ShapeDtypeStruct((2, 3, n), F32), compiler_params=_cp())(lbl, dlb)


def c_ctx_grad(parts, c_ctx):
    d = c_ctx.shape[1]

    def body(p_ref, c_ref, o_ref):
        tot = p_ref[0, 0:1, :]
        for chip in range(1, 4):
            tot = tot + p_ref[2 * chip, 0:1, :]
        o_ref[...] = tot * _dsilu(c_ref[...])

    return pl.pallas_call(body, name="c_ctx_grad", in_specs=[VMEM, VMEM], out_specs=VMEM,
                          out_shape=jax.ShapeDtypeStruct((1, d), F32), compiler_params=_cp())(parts, c_ctx)


def _reduce_scatter(grads, core, chip_core):
    got = exchange_halves(grads)
    sums = [pair_sum(g, r, core) for g, r in zip(grads, got)]
    recv = scatter_to_owners([sb for _, sb in sums])
    reduced = [owner_sum(s, r, chip_core) for (s, _), r in zip(sums, recv)]
    return share_halves(reduced)


def kernel(x, c, ctx, c_ctx, ada_w, ada_b, pre_g, post_g, ev_w_in, ev_pool_w, ev_pool_scale, ev_conv_w, ev_conv_b, ev_w_out, od_w_in, od_onorm_g, od_w_out, lb_logits, loss_target, m_c_ctx, m_ada_w, m_ada_b, m_pre_g, m_post_g, m_ev_w_in, m_ev_pool_w, m_ev_pool_scale, m_ev_conv_w, m_ev_conv_b, m_ev_w_out, m_od_w_in, m_od_onorm_g, m_od_w_out, m_lb_logits, v_c_ctx, v_ada_w, v_ada_b, v_pre_g, v_post_g, v_ev_w_in, v_ev_pool_w, v_ev_pool_scale, v_ev_conv_w, v_ev_conv_b, v_ev_w_out, v_od_w_in, v_od_onorm_g, v_od_w_out, v_lb_logits):
    _, seq, d = x.shape
    cx = ctx.shape[1]
    t = cx + seq
    half_d = d // 2
    g = half_d // N_POOL
    tn = d // 4
    xi, yi, ci = lax.axis_index("x"), lax.axis_index("y"), lax.axis_index("c")
    chip = 2 * xi + yi
    me = 2 * chip + ci
    core_arr = jnp.reshape(ci, (1,)).astype(jnp.int32)
    chip_arr = jnp.reshape(chip, (1,)).astype(jnp.int32)
    chip_core_arr = jnp.stack([chip, ci]).astype(jnp.int32)

    pad = lambda a, rows: jnp.concatenate([a, jnp.zeros((rows - a.shape[0], g), F32)], axis=0)
    small = jnp.concatenate([
        ev_pool_w.reshape(g, g), pad(ev_conv_w.reshape(3, g), 8), pad(od_onorm_g.reshape(2, g), 8),
        pad(lb_logits.reshape(12, g), 16)], axis=0)
    ev_in_g, ev_out_g, small_g, ev_done = allgather_shards([
        put_in_slot(ev_w_in[0], chip_arr, BF16, "cast_ev_w_in"),
        put_in_slot(ev_w_out[0], chip_arr, BF16, "cast_ev_w_out"),
        put_in_slot(small, chip_arr, F32, "place_small")])
    ev_out3 = ev_out_g.reshape(1, d, d)
    pool_w_full = small_g[:, :g].reshape(4, N_POOL, g // 4, g).transpose(1, 0, 2, 3).reshape(N_POOL, g, g)
    conv_w_full = small_g[:, g:g + 3].transpose(1, 0, 2).reshape(3, half_d)
    onorm_full = small_g[:, g + 8:g + 10].reshape(1, d)
    lbl_full = small_g[:, g + 16:g + 28].reshape(4, 2, 3, 2 * g).transpose(1, 2, 0, 3).reshape(2, 3, d)

    c_rows = jnp.concatenate([c + ev_done[0:1, 0:1], jnp.zeros((7, d), F32)], axis=0)
    c_all = allgather8(c_rows, "allgather_c")[:, 0, :]
    s_in = jnp.concatenate([c_all, c_ctx.reshape(1, d), jnp.zeros((7, d), F32)], axis=0)
    ws_ada = ada_w.shape[2]
    ada_b_mine = lax.dynamic_slice(ada_b, (0, chip * ws_ada), (2, ws_ada)).reshape(2, 1, ws_ada)
    s_act, mod_mine = ada_fwd(s_in, ada_w, ada_b_mine, tn)
    mod_all = allgather8(mod_mine.reshape(32, ws_ada), "allgather_mod")
    od_ici = copies_start("gather_od_ici_start", [
        put_in_slot(od_w_in[0], chip_arr, BF16, "cast_od_w_in"),
        put_in_slot(od_w_out[0], chip_arr, BF16, "cast_od_w_out")], 4, plan_gather_neighbours, mod_all)
    mod_full = mod_all[0::2].reshape(4, 2, 16, ws_ada).transpose(1, 2, 0, 3).reshape(2, 16, 3 * d)
    mod_lat = lax.dynamic_slice(mod_full, (0, me, 0), (2, 1, 3 * d))
    mods = jnp.concatenate([mod_full[:, 8:9], mod_lat], axis=1)
    shift, scale, gate = mods[:, :, :d], mods[:, :, d:2 * d], mods[:, :, 2 * d:]

    xs = jnp.concatenate([ctx[0], x[0]], axis=0)

    h0 = normmod_fwd(xs, pre_g[0:1] + od_ici[3][0:1, 0:1], shift[0], scale[0], cx)
    z0 = mm_nn(h0, ev_in_g, half_d, tn, "mm_ev_in")
    u_a = mix_a_fwd(z0, pool_w_full, ev_pool_scale, cx)
    u_b = mix_b_fwd(z0, conv_w_full, ev_conv_b, cx)
    u = jnp.concatenate([u_a, u_b], axis=1)
    od_relay = copies_start("gather_od_relay_start",
                            copies_wait("gather_od_ici_wait", od_ici, plan_gather_neighbours, u)[0],
                            2, plan_gather_relay, u)
    y0 = mm_nn(u, ev_out3, d, tn, "mm_ev_out")[0]
    xs1 = post_fwd(xs, y0, post_g[0:1] + od_relay[3][0:1, 0:1], gate[0], cx)
    od_d2d = copies_start("gather_od_d2d_start",
                          copies_wait("gather_od_relay_wait", od_relay, plan_gather_relay, xs1)[0],
                          6, plan_gather_d2d, xs1)
    (od_in_g, od_out_g), _ = copies_wait("gather_od_d2d_wait", od_d2d, plan_gather_d2d, od_d2d[3])
    od_out3 = od_out_g.reshape(1, d, d)

    h1 = normmod_fwd(xs1, pre_g[1:2], shift[1], scale[1], cx)
    z1 = mm_nn(h1, od_in_g, d, tn, "mm_od_in")
    o1, r1 = hgrn_fwd(z1, lbl_full, onorm_full, cx)
    y1 = mm_nn(r1, od_out3, d, tn, "mm_od_out")[0]
    sq, dx2 = post_loss(xs1, y1, post_g[1:2], gate[1], loss_target[0], cx)
    loss = lax.psum(sq[0, 0] * (0.5 / d), ("x", "y", "c"))

    dy1, dgate1, dpost1 = post_bwd(dx2, y1, post_g[1:2], gate[1], cx, True)
    dr1 = mm_nt(dy1[None], None, od_out3, tn, "mm_od_out_dx")
    g_od_out = mm_tn(r1, dy1[None], None, d, tn, "mm_od_out_dw")
    dz1, donorm, dlb = hgrn_bwd(z1, lbl_full, onorm_full, o1, dr1, cx)
    dh1 = mm_nt(dz1, None, od_in_g, tn, "mm_od_in_dx")
    g_od_in = mm_tn(h1, dz1, None, od_in_g.shape[2], tn, "mm_od_in_dw")
    dxs1, dpre1, dshift1, dscale1 = normmod_bwd(xs1, dh1, pre_g[1:2], scale[1], dx2, cx, True)

    od_grads = [g_od_in, g_od_out.reshape(4, d // 4, d)]
    half_zone = lambda a, lead, dt: lax.empty((lead, a.shape[1] // 2, a.shape[2]), dt)
    od_ex = copies_start("reduce_od_exchange_start", od_grads + [half_zone(a, 4, a.dtype) for a in od_grads],
                         2, plan_exchange, dxs1)

    dy0, dgate0, dpost0 = post_bwd(dxs1, y0, post_g[0:1] + od_ex[3][0:1, 0:1], gate[0], cx, False)
    du = mm_nt(dy0[None], None, ev_out3, tn, "mm_ev_out_dx")
    g_ev_out = mm_tn(u, dy0[None], None, d, tn, "mm_ev_out_dw")
    od_got, _ = copies_wait("reduce_od_exchange_wait", od_ex, plan_exchange, g_ev_out)
    od_sums = [pair_sum(od_got[i], od_got[2 + i], chip_core_arr) for i in range(2)]
    od_sc = copies_start("reduce_od_scatter_start",
                         [sb for _, sb in od_sums] + [half_zone(a, 3, BF16) for a in od_grads],
                         6, plan_scatter, du)
    dz0a, g_pool_w, dpool_scale = mix_a_bwd(z0, du, pool_w_full, ev_pool_scale + od_sc[3][0:1, 0:1], cx)
    dz0b, dconv_w, dconv_b = mix_b_bwd(z0, du, conv_w_full, ev_conv_b + od_sc[3][0:1, 0:1], cx)
    g_ev_in = mm_tn(h0, dz0a, dz0b, ev_in_g.shape[2], tn, "mm_ev_in_dw")
    ev_grads = [g_ev_in, g_ev_out.reshape(4, d // 4, d), g_pool_w.reshape(4, g, g)]
    ev_ex = copies_start("reduce_ev_exchange_start", ev_grads + [half_zone(a, 4, a.dtype) for a in ev_grads],
                         3, plan_exchange, dpool_scale)
    dh0 = mm_nt(dz0a, dz0b, ev_in_g, tn, "mm_ev_in_dx")
    dxs0, dpre0, dshift0, dscale0 = normmod_bwd(xs, dh0, pre_g[0:1] + ev_ex[3][0:1, 0:1], scale[0], dxs1,
                                                cx, False, True)
    grad_x = dxs0[None]
    ev_got, _ = copies_wait("reduce_ev_exchange_wait", ev_ex, plan_exchange, dxs0)
    ev_sums = [pair_sum(ev_got[i], ev_got[3 + i], chip_core_arr) for i in range(3)]
    od_recv, _ = copies_wait("reduce_od_scatter_wait", od_sc, plan_scatter, dxs0)

    zrow = jnp.zeros((1, d), F32)
    small_rows = jnp.concatenate([
        dpre0, dpre1, dpost0, dpost1,
        jnp.concatenate([dpool_scale, dconv_b], axis=1),
        jnp.concatenate([dconv_w.reshape(1, 3 * half_d), jnp.zeros((1, half_d), F32)], axis=1).reshape(2, d),
        donorm, dlb,
        dshift0[1:2], dscale0[1:2], dgate0[1:2], dshift0[0:1], dscale0[0:1], dgate0[0:1],
        dshift1[1:2], dscale1[1:2], dgate1[1:2], dshift1[0:1], dscale1[0:1], zrow,
        zrow, zrow], axis=0)
    small_all = allgather8(small_rows, "allgather_small")
    ev_sc = copies_start("reduce_ev_scatter_start",
                         [sb for _, sb in ev_sums] + [half_zone(a, 3, BF16) for a in ev_grads],
                         9, plan_scatter, small_all)
    od_sh = copies_start("reduce_od_share_start",
                         [owner_sum(od_sums[i][0], od_recv[2 + i], chip_core_arr) for i in range(2)],
                         2, plan_share, ev_sc[3])
    tot = small_reduce(small_all + ev_sc[3][0:1, 0:1])

    dm_rows = []
    for layer in range(2):
        lat = ROW_MOD + 6 * layer
        dm_lat = small_all[:, lat:lat + 3].reshape(8, 3 * d)
        dm_ctx = tot[lat + 3:lat + 6].reshape(1, 3 * d)
        dm_rows.append(jnp.concatenate([dm_lat, dm_ctx, jnp.zeros((7, 3 * d), F32)], axis=0))
    dm_full = jnp.stack(dm_rows)
    dm_mine = lax.dynamic_slice(dm_full, (0, 0, chip * ws_ada), (2, 16, ws_ada))

    def step(w, gr, m, v, name):
        shape = w.shape
        cols = shape[-1]
        two_d = lambda a: a.reshape(-1, cols)
        dl, mo, vo = adamw(two_d(w), two_d(gr), two_d(m), two_d(v), "adamw_" + name)
        return dl.reshape(shape), mo.reshape(shape), vo.reshape(shape)

    grad_ada_b = tot[24:30].reshape(2, 3 * d)
    grad_pre_g = tot[0:2]
    grad_post_g = tot[2:4]
    grad_ev_pool_scale = tot[4:5, :half_d]
    grad_ev_conv_b = tot[4:5, half_d:]
    conv_w_tot = tot[5:7].reshape(1, 2 * d)[:, :3 * half_d].reshape(3, N_POOL, g)
    grad_ev_conv_w = lax.dynamic_slice(conv_w_tot, (0, chip, 0), (3, 1, g)).reshape(1, 3, g)
    grad_od_onorm_g = lax.dynamic_slice(tot[7:8], (0, chip * 2 * g), (1, 2 * g))
    dlb_mine = lax.dynamic_slice(tot[8:10], (0, chip * 2 * g), (2, 2 * g))
    grad_lb_logits = lb_logits_grad(lb_logits, dlb_mine)
    upd = {
        "ada_b": step(ada_b, grad_ada_b, m_ada_b, v_ada_b, "ada_b"),
        "pre_g": step(pre_g, grad_pre_g, m_pre_g, v_pre_g, "pre_g"),
        "post_g": step(post_g, grad_post_g, m_post_g, v_post_g, "post_g"),
        "ev_pool_scale": step(ev_pool_scale, grad_ev_pool_scale, m_ev_pool_scale, v_ev_pool_scale, "ev_pool_scale"),
        "ev_conv_w": step(ev_conv_w, grad_ev_conv_w, m_ev_conv_w, v_ev_conv_w, "ev_conv_w"),
        "ev_conv_b": step(ev_conv_b, grad_ev_conv_b, m_ev_conv_b, v_ev_conv_b, "ev_conv_b"),
        "od_onorm_g": step(od_onorm_g, grad_od_onorm_g, m_od_onorm_g, v_od_onorm_g, "od_onorm_g"),
        "lb_logits": step(lb_logits, grad_lb_logits, m_lb_logits, v_lb_logits, "lb_logits"),
    }
    grad_ada_w, delta_ada_w, new_m_ada_w, new_v_ada_w, dctx_part = ada_bwd_adamw(
        s_act, dm_mine, ada_w, m_ada_w, v_ada_w)
    upd["ada_w"] = (delta_ada_w, new_m_ada_w, new_v_ada_w)
    (grad_od_w_in, grad_od_w_out), _ = copies_wait("reduce_od_share_wait", od_sh, plan_share, ev_sc[3])
    grad_od_w_in, grad_od_w_out = grad_od_w_in[None], grad_od_w_out[None]
    upd["od_w_in"] = step(od_w_in, grad_od_w_in, m_od_w_in, v_od_w_in, "od_w_in")
    upd["od_w_out"] = step(od_w_out, grad_od_w_out, m_od_w_out, v_od_w_out, "od_w_out")
    done_behind = [dctx_part] + [upd[k][0] for k in (
        "od_w_in", "od_w_out", "ada_b", "pre_g", "post_g", "ev_pool_scale", "ev_conv_w", "ev_conv_b",
        "od_onorm_g", "lb_logits")]
    ev_recv, ev_landed = copies_wait("reduce_ev_scatter_wait", ev_sc, plan_scatter, done_behind)
    grad_ev_w_in, grad_ev_w_out, grad_pool_w = share_halves(
        [owner_sum(ev_sums[i][0], ev_recv[3 + i], chip_core_arr) for i in range(3)])
    dctx_all = allgather8(dctx_part[0] + dctx_part[1] + ev_landed[0:1, 0:1], "allgather_dctx")
    grad_c_ctx = c_ctx_grad(dctx_all, c_ctx.reshape(1, d)).reshape(d)
    grad_ev_w_in, grad_ev_w_out = grad_ev_w_in[None], grad_ev_w_out[None]
    grad_ev_pool_w = grad_pool_w.reshape(1, N_POOL, g // 4, g)
    upd["c_ctx"] = step(c_ctx, grad_c_ctx, m_c_ctx, v_c_ctx, "c_ctx")
    upd["ev_w_in"] = step(ev_w_in, grad_ev_w_in, m_ev_w_in, v_ev_w_in, "ev_w_in")
    upd["ev_pool_w"] = step(ev_pool_w, grad_ev_pool_w, m_ev_pool_w, v_ev_pool_w, "ev_pool_w")
    upd["ev_w_out"] = step(ev_w_out, grad_ev_w_out, m_ev_w_out, v_ev_w_out, "ev_w_out")
    names = ["c_ctx", "ada_w", "ada_b", "pre_g", "post_g", "ev_w_in", "ev_pool_w", "ev_pool_scale",
             "ev_conv_w", "ev_conv_b", "ev_w_out", "od_w_in", "od_onorm_g", "od_w_out", "lb_logits"]
    grads = [grad_c_ctx, grad_ada_w, grad_ada_b, grad_pre_g, grad_post_g, grad_ev_w_in, grad_ev_pool_w,
             grad_ev_pool_scale, grad_ev_conv_w, grad_ev_conv_b, grad_ev_w_out, grad_od_w_in,
             grad_od_onorm_g, grad_od_w_out, grad_lb_logits]
    return (loss, grad_x, *grads, *[upd[k][0] for k in names], *[upd[k][1] for k in names],
            *[upd[k][2] for k in names])
```

```python
import functools

import jax
import jax.numpy as jnp
from jax import lax
from jax.experimental import pallas as pl
from jax.experimental.pallas import tpu as pltpu

EPS = 1e-6
GRID_W_LOG2 = 6
CHUNK = 64
HEAD = 128
N_POOL = 4
ADAM_LR, ADAM_B1, ADAM_B2, ADAM_EPS, ADAM_WD, ADAM_STEP = 0.001, 0.9, 0.999, 1e-08, 0.01, 10
VMEM_LIMIT = 56 * 1024 * 1024
MESH = pl.DeviceIdType.MESH
F32, BF16 = jnp.float32, jnp.bfloat16
ANY = pl.BlockSpec(memory_space=pl.ANY)
VMEM = pl.BlockSpec(memory_space=pltpu.VMEM)


def _cp(**kw):
    return pltpu.CompilerParams(vmem_limit_bytes=VMEM_LIMIT, **kw)


def _silu(x):
    return x * jax.nn.sigmoid(x)


def _dsilu(x):
    s = jax.nn.sigmoid(x)
    return s * (1.0 + x * (1.0 - s))


def _dot(a, b, dims=((1,), (0,)), precision=None):
    return lax.dot_general(a, b, (dims, ((), ())), preferred_element_type=F32, precision=precision)


NN = ((1,), (0,))
NT = ((1,), (1,))
TN = ((0,), (0,))


def _row_block(cx):
    return 256 if cx % 256 == 0 else 128


def normmod_fwd(xs, g, shift, scale, cx):
    t, d = xs.shape
    tm = _row_block(cx)
    nctx = cx // tm

    def body(x_ref, g_ref, sh_ref, sc_ref, h_ref):
        is_ctx = pl.program_id(0) < nctx
        x = x_ref[...]
        rstd = lax.rsqrt(jnp.mean(x * x, axis=-1, keepdims=True) + EPS)
        sc = jnp.where(is_ctx, sc_ref[0:1, :], sc_ref[1:2, :])
        sh = jnp.where(is_ctx, sh_ref[0:1, :], sh_ref[1:2, :])
        h_ref[...] = ((x * rstd) * g_ref[...] * (1.0 + sc) + sh).astype(BF16)

    row = pl.BlockSpec((tm, d), lambda i: (i, 0))
    vec = lambda r: pl.BlockSpec((r, d), lambda i: (0, 0))
    return pl.pallas_call(
        body, name="normmod_fwd", grid=(t // tm,),
        in_specs=[row, vec(1), vec(2), vec(2)], out_specs=row,
        out_shape=jax.ShapeDtypeStruct((t, d), BF16), compiler_params=_cp(),
    )(xs, g, shift, scale)


def normmod_bwd(xs, dh, g, scale, dres, cx, res_is_latent_only, dx_latent_only=False):
    t, d = xs.shape
    tm = _row_block(cx)
    nctx = cx // tm

    def body(x_ref, dh_ref, g_ref, sc_ref, dres_ref, dx_ref, dg_ref, dsh_ref, dsc_ref):
        i = pl.program_id(0)
        is_ctx = i < nctx

        @pl.when(i == 0)
        def _():
            dg_ref[...] = jnp.zeros_like(dg_ref)
            dsh_ref[...] = jnp.zeros_like(dsh_ref)
            dsc_ref[...] = jnp.zeros_like(dsc_ref)

        x = x_ref[...]
        dh = dh_ref[...]
        gv = g_ref[...]
        rstd = lax.rsqrt(jnp.mean(x * x, axis=-1, keepdims=True) + EPS)
        xhat = x * rstd
        sc = jnp.where(is_ctx, sc_ref[0:1, :], sc_ref[1:2, :])
        dsh = jnp.sum(dh, axis=0, keepdims=True)
        dhx = dh * xhat
        dsc = jnp.sum(dhx * gv, axis=0, keepdims=True)
        dg_ref[...] += jnp.sum(dhx * (1.0 + sc), axis=0, keepdims=True)
        zero = jnp.zeros_like(dsh)
        dsh_ref[0:1, :] += jnp.where(is_ctx, dsh, zero)
        dsh_ref[1:2, :] += jnp.where(is_ctx, zero, dsh)
        dsc_ref[0:1, :] += jnp.where(is_ctx, dsc, zero)
        dsc_ref[1:2, :] += jnp.where(is_ctx, zero, dsc)
        dxhat = dh * (gv * (1.0 + sc))
        dx = rstd * (dxhat - xhat * jnp.mean(dxhat * xhat, axis=-1, keepdims=True))
        res = dres_ref[...]
        if res_is_latent_only:
            res = jnp.where(is_ctx, jnp.zeros_like(res), res)
        dx_ref[...] = dx + res

    row = pl.BlockSpec((tm, d), lambda i: (i, 0))
    if res_is_latent_only:
        res_spec = pl.BlockSpec((tm, d), lambda i: (jnp.maximum(i - nctx, 0), 0))
    else:
        res_spec = row
    vec = lambda r: pl.BlockSpec((r, d), lambda i: (0, 0))
    dx_spec = pl.BlockSpec((tm, d), lambda i: (jnp.maximum(i - nctx, 0), 0)) if dx_latent_only else row
    return pl.pallas_call(
        body, name="normmod_bwd", grid=(t // tm,),
        in_specs=[row, row, vec(1), vec(2), res_spec],
        out_specs=[dx_spec, vec(1), vec(2), vec(2)],
        out_shape=[jax.ShapeDtypeStruct((t - cx if dx_latent_only else t, d), F32), jax.ShapeDtypeStruct((1, d), F32),
                   jax.ShapeDtypeStruct((2, d), F32), jax.ShapeDtypeStruct((2, d), F32)],
        compiler_params=_cp(),
    )(xs, dh, g, scale, dres)


def post_fwd(xs, y, pg, gate, cx):
    t, d = xs.shape
    tm = _row_block(cx)
    nctx = cx // tm

    def body(x_ref, y_ref, pg_ref, gate_ref, o_ref):
        is_ctx = pl.program_id(0) < nctx
        y = y_ref[...]
        rstd = lax.rsqrt(jnp.mean(y * y, axis=-1, keepdims=True) + EPS)
        gt = jnp.where(is_ctx, gate_ref[0:1, :], gate_ref[1:2, :])
        o_ref[...] = x_ref[...] + gt * ((y * rstd) * pg_ref[...])

    row = pl.BlockSpec((tm, d), lambda i: (i, 0))
    vec = lambda r: pl.BlockSpec((r, d), lambda i: (0, 0))
    return pl.pallas_call(
        body, name="post_fwd", grid=(t // tm,),
        in_specs=[row, row, vec(1), vec(2)], out_specs=row,
        out_shape=jax.ShapeDtypeStruct((t, d), F32), compiler_params=_cp(),
    )(xs, y, pg, gate)


def post_loss(xs, y, pg, gate, target, cx):
    t, d = xs.shape
    n = y.shape[0]
    tm = _row_block(cx)
    nctx = cx // tm

    def body(x_ref, y_ref, pg_ref, gate_ref, tgt_ref, sq_ref, dx_ref):
        @pl.when(pl.program_id(0) == 0)
        def _():
            sq_ref[...] = jnp.zeros_like(sq_ref)

        y = y_ref[...]
        rstd = lax.rsqrt(jnp.mean(y * y, axis=-1, keepdims=True) + EPS)
        x2 = x_ref[...] + gate_ref[1:2, :] * ((y * rstd) * pg_ref[...])
        err = x2 - tgt_ref[...]
        sq_ref[...] += jnp.sum(err * err)
        dx_ref[...] = err * (1.0 / d)

    row = pl.BlockSpec((tm, d), lambda i: (i, 0))
    xrow = pl.BlockSpec((tm, d), lambda i: (i + nctx, 0))
    vec = lambda r: pl.BlockSpec((r, d), lambda i: (0, 0))
    return pl.pallas_call(
        body, name="post_loss", grid=(n // tm,),
        in_specs=[xrow, row, vec(1), vec(2), row],
        out_specs=[pl.BlockSpec((8, 128), lambda i: (0, 0)), row],
        out_shape=[jax.ShapeDtypeStruct((8, 128), F32), jax.ShapeDtypeStruct((n, d), F32)],
        compiler_params=_cp(),
    )(xs, y, pg, gate, target)


def post_bwd(dxo, y, pg, gate, cx, latent_only):
    m, d = y.shape
    tm = _row_block(cx)
    nctx = 0 if latent_only else cx // tm

    def body(dx_ref, y_ref, pg_ref, gate_ref, dy_ref, dgate_ref, dpg_ref):
        i = pl.program_id(0)
        is_ctx = i < nctx

        @pl.when(i == 0)
        def _():
            dgate_ref[...] = jnp.zeros_like(dgate_ref)
            dpg_ref[...] = jnp.zeros_like(dpg_ref)

        y = y_ref[...]
        dx = dx_ref[...]
        pgv = pg_ref[...]
        rstd = lax.rsqrt(jnp.mean(y * y, axis=-1, keepdims=True) + EPS)
        yhat = y * rstd
        gt = jnp.where(is_ctx, gate_ref[0:1, :], gate_ref[1:2, :])
        dxy = dx * yhat
        dgt = jnp.sum(dxy * pgv, axis=0, keepdims=True)
        zero = jnp.zeros_like(dgt)
        dgate_ref[0:1, :] += jnp.where(is_ctx, dgt, zero)
        dgate_ref[1:2, :] += jnp.where(is_ctx, zero, dgt)
        dpg_ref[...] += jnp.sum(dxy * gt, axis=0, keepdims=True)
        dyhat = dx * (gt * pgv)
        dy = rstd * (dyhat - yhat * jnp.mean(dyhat * yhat, axis=-1, keepdims=True))
        dy_ref[...] = dy.astype(BF16)

    row = pl.BlockSpec((tm, d), lambda i: (i, 0))
    vec = lambda r: pl.BlockSpec((r, d), lambda i: (0, 0))
    return pl.pallas_call(
        body, name="post_bwd", grid=(m // tm,),
        in_specs=[row, row, vec(1), vec(2)], out_specs=[row, vec(2), vec(1)],
        out_shape=[jax.ShapeDtypeStruct((m, d), BF16), jax.ShapeDtypeStruct((2, d), F32),
                   jax.ShapeDtypeStruct((1, d), F32)],
        compiler_params=_cp(),
    )(dxo, y, pg, gate)


def _split_rows(m):
    for cand in (1024, 768, 512, 384, 256, 128):
        if m % cand == 0 and m // cand >= 2:
            return cand
    return m


def mm_nn(a, w3, sec, tn, name):
    m, k = a.shape
    q, _, ws = w3.shape
    n = q * ws
    tpq, tps = ws // tn, sec // tn
    tm = next(c for c in (768, 512, 256, 128) if m % c == 0)

    def body(a_ref, w_ref, o_ref):
        w = w_ref[...]

        def step(i, carry):
            rows = pl.ds(pl.multiple_of(i * tm, tm), tm)
            o_ref[rows, :] = _dot(a_ref[rows, :], w)
            return carry

        lax.fori_loop(0, m // tm, step, 0)

    return pl.pallas_call(
        body, name=name, grid=(n // tn,),
        in_specs=[pl.BlockSpec((m, k), lambda j: (0, 0)),
                  pl.BlockSpec((None, k, tn), lambda j: (j // tpq, 0, j % tpq))],
        out_specs=pl.BlockSpec((None, m, tn), lambda j: (j // tps, 0, j % tps)),
        out_shape=jax.ShapeDtypeStruct((n // sec, m, sec), F32), compiler_params=_cp(),
    )(a, w3)


def _two_stacks(a3, b3, tn):
    sec = a3.shape[2]
    tps = sec // tn
    n1 = a3.shape[0] * tps
    first = lambda j: (jnp.minimum(j, n1 - 1) // tps, jnp.minimum(j, n1 - 1) % tps)
    second = lambda j: (jnp.maximum(j - n1, 0) // tps, jnp.maximum(j - n1, 0) % tps)
    return n1, first, second


def mm_nt(a3, b3, w3, tn, name):
    if b3 is None:
        b3 = a3
    _, m, sec = a3.shape
    q, k, ws = w3.shape
    n = q * ws
    tpq = ws // tn
    mb = _split_rows(m)
    n1, first, second = _two_stacks(a3, b3, tn)

    def body(a_ref, b_ref, w_ref, o_ref):
        j = pl.program_id(1)

        @pl.when(j == 0)
        def _():
            o_ref[...] = jnp.zeros_like(o_ref)

        @pl.when(j < n1)
        def _():
            o_ref[...] += _dot(a_ref[...], w_ref[...], NT)

        @pl.when(j >= n1)
        def _():
            o_ref[...] += _dot(b_ref[...], w_ref[...], NT)

    return pl.pallas_call(
        body, name=name, grid=(m // mb, n // tn),
        in_specs=[pl.BlockSpec((None, mb, tn), lambda i, j: (first(j)[0], i, first(j)[1])),
                  pl.BlockSpec((None, mb, tn), lambda i, j: (second(j)[0], i, second(j)[1])),
                  pl.BlockSpec((None, k, tn), lambda i, j: (j // tpq, 0, j % tpq))],
        out_specs=pl.BlockSpec((mb, k), lambda i, j: (i, 0)),
        out_shape=jax.ShapeDtypeStruct((m, k), F32), compiler_params=_cp(),
    )(a3, b3, w3)


def mm_tn(a, b3, c3, ws, tn, name):
    m, k = a.shape
    sec = b3.shape[2]
    n = (b3.shape[0] + (0 if c3 is None else c3.shape[0])) * sec
    if c3 is None:
        c3 = b3
    tpq = ws // tn
    kb = 256 if k % 256 == 0 else 128
    n1, first, second = _two_stacks(b3, c3, tn)

    def body(a_ref, b_ref, c_ref, o_ref):
        def product(rhs_ref):
            rhs = rhs_ref[...]
            for i in range(k // kb):
                o_ref[i * kb:(i + 1) * kb, :] = _dot(a_ref[:, i * kb:(i + 1) * kb], rhs, TN).astype(BF16)

        @pl.when(pl.program_id(0) < n1)
        def _():
            product(b_ref)

        @pl.when(pl.program_id(0) >= n1)
        def _():
            product(c_ref)

    return pl.pallas_call(
        body, name=name, grid=(n // tn,),
        in_specs=[pl.BlockSpec((m, k), lambda j: (0, 0)),
                  pl.BlockSpec((None, m, tn), lambda j: (first(j)[0], 0, first(j)[1])),
                  pl.BlockSpec((None, m, tn), lambda j: (second(j)[0], 0, second(j)[1]))],
        out_specs=pl.BlockSpec((None, k, tn), lambda j: (j // tpq, 0, j % tpq)),
        out_shape=jax.ShapeDtypeStruct((n // ws, k, ws), BF16), compiler_params=_cp(),
    )(a, b3, c3)


POOL_REACH = 8 << GRID_W_LOG2


def _token_parts(tok, cx):
    lat = tok - cx
    return tok < cx, lat >> GRID_W_LOG2, lat & ((1 << GRID_W_LOG2) - 1)


def _pool_mask(gi, row0, col0, tm, ncols, cx, transposed):
    half = jnp.left_shift(1, gi)
    r = lax.broadcasted_iota(jnp.int32, (tm, 1), 0) + row0
    c = lax.broadcasted_iota(jnp.int32, (1, ncols), 1) + col0
    out_tok, src_tok = (c, r) if transposed else (r, c)
    o_ctx, o_row, o_col = _token_parts(out_tok, cx)
    s_ctx, s_row, s_col = _token_parts(src_tok, cx)

    def inside(o, s):
        return (s >= o - half) & (s <= o + half - 1)

    ctx_hit = o_ctx & s_ctx & inside(out_tok, src_tok)
    lat_hit = (~o_ctx) & (~s_ctx) & inside(o_row, s_row) & inside(o_col, s_col)
    return jnp.where(ctx_hit | lat_hit, 1.0, 0.0).astype(BF16)


def _pool_inv_count(gi, row0, tm, cx, seq):
    half = jnp.left_shift(1, gi)
    r = lax.broadcasted_iota(jnp.int32, (tm, 1), 0) + row0
    is_ctx, row, col = _token_parts(r, cx)

    def count(pos, size):
        return jnp.minimum(pos + half - 1, size - 1) - jnp.maximum(pos - half, 0) + 1

    cnt = jnp.where(is_ctx, count(r, cx), count(row, seq >> GRID_W_LOG2) * count(col, 1 << GRID_W_LOG2))
    return 1.0 / cnt.astype(F32)


def _lat_band(tm):
    side = POOL_REACH // tm
    return side, 2 * side + 1


def _lat_mask(gi, tm, cx, transposed):
    side, band = _lat_band(tm)
    return _pool_mask(gi, cx + side * tm, cx, tm, band * tm, cx, transposed)


def _store_padded_lat(dst_ref, lat, tm):
    side, _ = _lat_band(tm)
    seq = lat.shape[0]
    zeros = jnp.zeros((side * tm, lat.shape[1]), dst_ref.dtype)
    dst_ref[0:side * tm, :] = zeros
    dst_ref[side * tm + seq:, :] = zeros
    dst_ref[side * tm:side * tm + seq, :] = lat.astype(dst_ref.dtype)


def mix_a_fwd(z0, pool_w, pool_scale, cx):
    _, t, half_d = z0.shape
    g = half_d // N_POOL
    seq = t - cx
    tm = _row_block(cx)
    side, band = _lat_band(tm)

    def body(v_ref, ag_ref, w_ref, sc_ref, u_ref, vlat_ref, mask_ref):
        gi = pl.program_id(0)
        w = w_ref[...].astype(BF16)
        sc = sc_ref[...]
        _store_padded_lat(vlat_ref, v_ref[cx:, :], tm)
        mask_ref[...] = _lat_mask(gi, tm, cx, False)

        def finish(row0, window_sum):
            rows = pl.ds(row0, tm)
            pooled = window_sum * _pool_inv_count(gi, row0, tm, cx, seq) - v_ref[rows, :]
            mixed = _dot(pooled.astype(BF16), w) * sc
            u_ref[rows, :] = (mixed * _silu(ag_ref[rows, :])).astype(BF16)

        vctx = v_ref[0:cx, :].astype(BF16)
        for i in range(cx // tm):
            finish(i * tm, _dot(_pool_mask(gi, i * tm, 0, tm, cx, cx, False), vctx))

        def step(j, carry):
            src = vlat_ref[pl.ds(pl.multiple_of(j * tm, tm), band * tm), :]
            finish(pl.multiple_of(cx + j * tm, tm), _dot(mask_ref[...], src))
            return carry

        lax.fori_loop(0, seq // tm, step, 0)

    sec = lambda s: pl.BlockSpec((None, t, g), lambda j: (s, 0, j))
    return pl.pallas_call(
        body, name="mix_a_fwd", grid=(N_POOL,),
        in_specs=[sec(0), sec(1), pl.BlockSpec((None, g, g), lambda j: (j, 0, 0)),
                  pl.BlockSpec((1, g), lambda j: (0, j))],
        out_specs=pl.BlockSpec((t, g), lambda j: (0, j)),
        out_shape=jax.ShapeDtypeStruct((t, half_d), BF16),
        scratch_shapes=[pltpu.VMEM((seq + 2 * side * tm, g), BF16), pltpu.VMEM((tm, band * tm), BF16)],
        compiler_params=_cp(),
    )(z0, z0, pool_w, pool_scale)


def mix_a_bwd(z0, du, pool_w, pool_scale, cx):
    _, t, half_d = z0.shape
    g = half_d // N_POOL
    seq = t - cx
    tm = _row_block(cx)
    gq = g // 4
    side, band = _lat_band(tm)

    def body(v_ref, ag_ref, du_ref, w_ref, sc_ref, dz_ref, dw_ref, dsc_ref,
             vlat_ref, mask_ref, pooled_ref, dmx_ref, dpl_ref, wlat_ref, wctx_ref):
        gi = pl.program_id(0)
        w = w_ref[...].astype(BF16)
        sc = sc_ref[...]
        _store_padded_lat(vlat_ref, v_ref[cx:, :], tm)
        _store_padded_lat(wlat_ref, jnp.zeros((seq, g), BF16), tm)
        mask_ref[...] = _lat_mask(gi, tm, cx, False)

        def first(row0, window_sum, weighted_ref, weighted_row0):
            rows = pl.ds(row0, tm)
            inv = _pool_inv_count(gi, row0, tm, cx, seq)
            pooled = (window_sum * inv - v_ref[rows, :]).astype(BF16)
            pooled_ref[rows, :] = pooled
            mixed = _dot(pooled, w)
            ag = ag_ref[rows, :]
            duv = du_ref[rows, :]
            dz_ref[1, rows, :] = (duv * (mixed * sc) * _dsilu(ag)).astype(BF16)
            dms = duv * _silu(ag)
            dmixed = (dms * sc).astype(BF16)
            dmx_ref[rows, :] = dmixed
            dpooled = _dot(dmixed, w, NT)
            dpl_ref[rows, :] = dpooled
            weighted_ref[pl.ds(weighted_row0, tm), :] = (dpooled * inv).astype(BF16)
            return jnp.sum(dms * mixed, axis=0, keepdims=True)

        dsc = jnp.zeros((1, g), F32)
        vctx = v_ref[0:cx, :].astype(BF16)
        for i in range(cx // tm):
            dsc += first(i * tm, _dot(_pool_mask(gi, i * tm, 0, tm, cx, cx, False), vctx), wctx_ref, i * tm)

        def first_lat(j, acc):
            src = vlat_ref[pl.ds(pl.multiple_of(j * tm, tm), band * tm), :]
            return acc + first(pl.multiple_of(cx + j * tm, tm), _dot(mask_ref[...], src),
                               wlat_ref, pl.multiple_of((side + j) * tm, tm))

        dsc_ref[...] = lax.fori_loop(0, seq // tm, first_lat, dsc)
        dw = _dot(pooled_ref[...], dmx_ref[...], TN)
        for qi in range(4):
            dw_ref[qi] = dw[qi * gq:(qi + 1) * gq, :]

        wctx = wctx_ref[...]
        for i in range(cx // tm):
            rows = pl.ds(i * tm, tm)
            dz_ref[0, rows, :] = (_dot(_pool_mask(gi, i * tm, 0, tm, cx, cx, True), wctx)
                                  - dpl_ref[rows, :]).astype(BF16)
        mask_ref[...] = _lat_mask(gi, tm, cx, True)

        def second_lat(j, carry):
            rows = pl.ds(pl.multiple_of(cx + j * tm, tm), tm)
            src = wlat_ref[pl.ds(pl.multiple_of(j * tm, tm), band * tm), :]
            dz_ref[0, rows, :] = (_dot(mask_ref[...], src) - dpl_ref[rows, :]).astype(BF16)
            return carry

        lax.fori_loop(0, seq // tm, second_lat, 0)

    sec = lambda s: pl.BlockSpec((None, t, g), lambda j: (s, 0, j))
    padded = pltpu.VMEM((seq + 2 * side * tm, g), BF16)
    return pl.pallas_call(
        body, name="mix_a_bwd", grid=(N_POOL,),
        in_specs=[sec(0), sec(1), pl.BlockSpec((t, g), lambda j: (0, j)),
                  pl.BlockSpec((None, g, g), lambda j: (j, 0, 0)),
                  pl.BlockSpec((1, g), lambda j: (0, j))],
        out_specs=[pl.BlockSpec((2, t, g), lambda j: (0, 0, j)),
                   pl.BlockSpec((4, None, gq, g), lambda j: (0, j, 0, 0)),
                   pl.BlockSpec((1, g), lambda j: (0, j))],
        out_shape=[jax.ShapeDtypeStruct((2, t, half_d), BF16),
                   jax.ShapeDtypeStruct((4, N_POOL, gq, g), F32),
                   jax.ShapeDtypeStruct((1, half_d), F32)],
        scratch_shapes=[padded, pltpu.VMEM((tm, band * tm), BF16), pltpu.VMEM((t, g), BF16),
                        pltpu.VMEM((t, g), BF16), pltpu.VMEM((t, g), F32), padded, pltpu.VMEM((cx, g), BF16)],
        compiler_params=_cp(),
    )(z0, z0, du, pool_w, pool_scale)


def _conv_masks(t, cx):
    r = lax.broadcasted_iota(jnp.int32, (t, 1), 0)
    has_prev = jnp.where((r == 0) | (r == cx), 0.0, 1.0)
    has_next = jnp.where((r == cx - 1) | (r == t - 1), 0.0, 1.0)
    return has_prev, has_next


def mix_b_fwd(z0, conv_w, conv_b, cx):
    _, t, half_d = z0.shape
    gb = 128

    def body(bx_ref, bb_ref, bc_ref, bg_ref, w_ref, b_ref, u_ref):
        has_prev, has_next = _conv_masks(t, cx)
        tt = bc_ref[...] * bx_ref[...]
        prev = pltpu.roll(tt, 1, 0) * has_prev
        nxt = pltpu.roll(tt, t - 1, 0) * has_next
        cv = prev * w_ref[0:1, :] + tt * w_ref[1:2, :] + nxt * w_ref[2:3, :] + b_ref[...]
        u_ref[...] = (bb_ref[...] * cv * _silu(bg_ref[...])).astype(BF16)

    sec = lambda s: pl.BlockSpec((None, t, gb), lambda j: (s, 0, j))
    return pl.pallas_call(
        body, name="mix_b_fwd", grid=(half_d // gb,),
        in_specs=[sec(2), sec(3), sec(4), sec(5), pl.BlockSpec((3, gb), lambda j: (0, j)),
                  pl.BlockSpec((1, gb), lambda j: (0, j))],
        out_specs=pl.BlockSpec((t, gb), lambda j: (0, j)),
        out_shape=jax.ShapeDtypeStruct((t, half_d), BF16), compiler_params=_cp(),
    )(z0, z0, z0, z0, conv_w, conv_b)


def mix_b_bwd(z0, du, conv_w, conv_b, cx):
    _, t, half_d = z0.shape
    gb = 128
    off = half_d // gb

    def body(bx_ref, bb_ref, bc_ref, bg_ref, du_ref, w_ref, b_ref, dz_ref, dw_ref, db_ref):
        has_prev, has_next = _conv_masks(t, cx)
        bx, bb, bc, bg = bx_ref[...], bb_ref[...], bc_ref[...], bg_ref[...]
        duv = du_ref[...]
        tt = bc * bx
        prev = pltpu.roll(tt, 1, 0) * has_prev
        nxt = pltpu.roll(tt, t - 1, 0) * has_next
        w0, w1, w2 = w_ref[0:1, :], w_ref[1:2, :], w_ref[2:3, :]
        cv = prev * w0 + tt * w1 + nxt * w2 + b_ref[...]
        sg = _silu(bg)
        dz_ref[1] = (duv * cv * sg).astype(BF16)
        dz_ref[3] = (duv * bb * cv * _dsilu(bg)).astype(BF16)
        dcv = duv * bb * sg
        dw_ref[0:1, :] = jnp.sum(dcv * prev, axis=0, keepdims=True)
        dw_ref[1:2, :] = jnp.sum(dcv * tt, axis=0, keepdims=True)
        dw_ref[2:3, :] = jnp.sum(dcv * nxt, axis=0, keepdims=True)
        db_ref[...] = jnp.sum(dcv, axis=0, keepdims=True)
        dt = (pltpu.roll(dcv * has_prev, t - 1, 0) * w0 + dcv * w1
              + pltpu.roll(dcv * has_next, 1, 0) * w2)
        dz_ref[0] = (dt * bc).astype(BF16)
        dz_ref[2] = (dt * bx).astype(BF16)

    sec = lambda s: pl.BlockSpec((None, t, gb), lambda j: (s, 0, j))
    return pl.pallas_call(
        body, name="mix_b_bwd", grid=(half_d // gb,),
        in_specs=[sec(2), sec(3), sec(4), sec(5), pl.BlockSpec((t, gb), lambda j: (0, j + off)),
                  pl.BlockSpec((3, gb), lambda j: (0, j)), pl.BlockSpec((1, gb), lambda j: (0, j))],
        out_specs=[pl.BlockSpec((4, t, gb), lambda j: (0, 0, j)),
                   pl.BlockSpec((3, gb), lambda j: (0, j)), pl.BlockSpec((1, gb), lambda j: (0, j))],
        out_shape=[jax.ShapeDtypeStruct((4, t, half_d), BF16),
                   jax.ShapeDtypeStruct((3, half_d), F32), jax.ShapeDtypeStruct((1, half_d), F32)],
        compiler_params=_cp(),
    )(z0, z0, z0, z0, du, conv_w, conv_b)


def _lower_bound(lbl_ref, d):
    l0, l1, l2 = lbl_ref[d, 0:1, :], lbl_ref[d, 1:2, :], lbl_ref[d, 2:3, :]
    mx = jnp.maximum(jnp.maximum(l0, l1), l2)
    e0, e1, e2 = jnp.exp(l0 - mx), jnp.exp(l1 - mx), jnp.exp(l2 - mx)
    inv = 1.0 / (e0 + e1 + e2)
    return (e0 + e1) * inv, (e0 * inv, e1 * inv, e2 * inv)


def _chunk_consts(d):
    r = lax.broadcasted_iota(jnp.int32, (CHUNK, CHUNK), 0)
    c = lax.broadcasted_iota(jnp.int32, (CHUNK, CHUNK), 1)
    keep = (c <= r) if d == 0 else (c >= r)
    return jnp.where(keep, 1.0, 0.0).astype(F32), keep


def _chunk_of_step(s, d, nc, ncc):
    if d == 0:
        return s
    return jnp.where(s < ncc, ncc - 1 - s, nc - 1 + ncc - s)


def _chunk_terms(lfc, kc, qc, cum):
    bc = _dot(cum, lfc, precision=lax.Precision.HIGHEST)
    bl = jnp.sum(lfc, axis=0, keepdims=True)
    e = jnp.exp(bc)
    einv = jnp.exp(-bc)
    erem = jnp.exp(bl - bc)
    return e, einv, erem, jnp.exp(bl), qc * e, kc * einv, kc * erem


def hgrn_fwd(z1, lbl, onorm, cx):
    _, t, d = z1.shape
    seq = t - cx
    nc, ncc = t // CHUNK, cx // CHUNK

    def body(zf_ref, zb_ref, v_ref, q_ref, g_ref, lbl_ref, on_ref, o_ref, r_ref,
             lf_ref, k_ref, oacc_ref, st_ref):
        for dr, z_ref in ((0, zf_ref), (1, zb_ref)):
            lbv, _ = _lower_bound(lbl_ref, dr)
            z = z_ref[...]
            lf_ref[...] = jnp.log(lbv + (1.0 - lbv) * jax.nn.sigmoid(z))
            k_ref[...] = (1.0 - lbv) * jax.nn.sigmoid(-z)
            st_ref[...] = jnp.zeros_like(st_ref)
            cum, keep = _chunk_consts(dr)

            def step(s, carry, dr=dr, cum=cum, keep=keep):
                n = _chunk_of_step(s, dr, nc, ncc)
                rows = pl.ds(pl.multiple_of(n * CHUNK, CHUNK), CHUNK)
                vc = v_ref[rows, :].astype(BF16)
                _, _, _, dec, qd, ki, kd = _chunk_terms(lf_ref[rows, :], k_ref[rows, :], q_ref[rows, :], cum)
                qdb = qd.astype(BF16)
                a = jnp.where(keep, _dot(qdb, ki.astype(BF16), NT), 0.0)
                st = st_ref[...]
                oc = _dot(qdb, st.astype(BF16), NT) + _dot(a.astype(BF16), vc)
                st_ref[...] = st * dec + _dot(vc, kd.astype(BF16), TN)
                if dr == 0:
                    oacc_ref[rows, :] = oc
                else:
                    oacc_ref[rows, :] += oc
                return carry

            lax.fori_loop(0, nc, step, 0, unroll=4)

        o = oacc_ref[cx:, :]
        o_ref[...] = o
        rstd = lax.rsqrt(jnp.mean(o * o, axis=-1, keepdims=True) + EPS)
        r_ref[...] = (o * rstd * on_ref[...] * _silu(g_ref[cx:, :])).astype(BF16)

    sec = lambda s: pl.BlockSpec((None, t, HEAD), lambda h: (s, 0, h))
    col = pl.BlockSpec((seq, HEAD), lambda h: (0, h))
    return pl.pallas_call(
        body, name="hgrn_fwd", grid=(d // HEAD,),
        in_specs=[sec(0), sec(1), sec(2), sec(3), sec(4),
                  pl.BlockSpec((2, 3, HEAD), lambda h: (0, 0, h)), pl.BlockSpec((1, HEAD), lambda h: (0, h))],
        out_specs=[col, col],
        out_shape=[jax.ShapeDtypeStruct((seq, d), F32), jax.ShapeDtypeStruct((seq, d), BF16)],
        scratch_shapes=[pltpu.VMEM((t, HEAD), F32), pltpu.VMEM((t, HEAD), F32), pltpu.VMEM((t, HEAD), F32),
                        pltpu.VMEM((HEAD, HEAD), F32)],
        compiler_params=_cp(),
    )(z1, z1, z1, z1, z1, lbl, onorm)


def hgrn_bwd(z1, lbl, onorm, o, dr_out, cx):
    _, t, d = z1.shape
    seq = t - cx
    nc, ncc = t // CHUNK, cx // CHUNK

    def body(zf_ref, zb_ref, v_ref, q_ref, g_ref, lbl_ref, on_ref, o_ref, dr_ref,
             dz_ref, don_ref, dlb_ref,
             lf_ref, k_ref, do_ref, dq_ref, dv_ref, dk_ref, dlf_ref, ssc_ref, dst_ref):
        o = o_ref[...]
        g = g_ref[cx:, :]
        drv = dr_ref[...]
        onv = on_ref[...]
        rstd = lax.rsqrt(jnp.mean(o * o, axis=-1, keepdims=True) + EPS)
        ohat = o * rstd
        sg = _silu(g)
        don_ref[...] = jnp.sum(drv * ohat * sg, axis=0, keepdims=True)
        dz_ref[4, :cx, :] = jnp.zeros((cx, HEAD), BF16)
        dz_ref[4, cx:, :] = (drv * ohat * onv * _dsilu(g)).astype(BF16)
        dohat = drv * onv * sg
        do_ref[:cx, :] = jnp.zeros((cx, HEAD), F32)
        do_ref[cx:, :] = rstd * (dohat - ohat * jnp.mean(dohat * ohat, axis=-1, keepdims=True))

        for dr, z_ref in ((0, zf_ref), (1, zb_ref)):
            lbv, _ = _lower_bound(lbl_ref, dr)
            z = z_ref[...]
            lf_ref[...] = jnp.log(lbv + (1.0 - lbv) * jax.nn.sigmoid(z))
            k_ref[...] = (1.0 - lbv) * jax.nn.sigmoid(-z)
            cum, keep = _chunk_consts(dr)
            cum_t, _ = _chunk_consts(1 - dr)

            st_init = jnp.zeros((HEAD, HEAD), F32)

            def state_step(s, st, dr=dr, cum=cum):
                n = _chunk_of_step(s, dr, nc, ncc)
                rows = pl.ds(pl.multiple_of(n * CHUNK, CHUNK), CHUNK)
                ssc_ref[n] = st
                _, _, _, dec, _, _, kd = _chunk_terms(lf_ref[rows, :], k_ref[rows, :], q_ref[rows, :], cum)
                return st * dec + _dot(v_ref[rows, :].astype(BF16), kd.astype(BF16), TN)

            lax.fori_loop(0, nc, state_step, st_init, unroll=4)
            dst_ref[...] = jnp.zeros_like(dst_ref)

            def grad_step(s2, carry, dr=dr, cum=cum, cum_t=cum_t, keep=keep):
                n = _chunk_of_step(nc - 1 - s2, dr, nc, ncc)
                rows = pl.ds(pl.multiple_of(n * CHUNK, CHUNK), CHUNK)
                vc = v_ref[rows, :].astype(BF16)
                e, einv, erem, dec, qd, ki, kd = _chunk_terms(
                    lf_ref[rows, :], k_ref[rows, :], q_ref[rows, :], cum)
                qdb, kib, kdb = qd.astype(BF16), ki.astype(BF16), kd.astype(BF16)
                doc = do_ref[rows, :].astype(BF16)
                st0 = ssc_ref[n]
                dst = dst_ref[...]
                dstb = dst.astype(BF16)
                a = jnp.where(keep, _dot(qdb, kib, NT), 0.0).astype(BF16)
                da = jnp.where(keep, _dot(doc, vc, NT), 0.0).astype(BF16)
                dqd = _dot(doc, st0.astype(BF16)) + _dot(da, kib)
                dki = _dot(da, qdb, TN)
                dv = _dot(a, doc, TN) + _dot(kdb, dstb, NT)
                dkd = _dot(vc, dstb)
                ddec = jnp.sum(dst * st0, axis=0, keepdims=True)
                dst_ref[...] = _dot(doc, qdb, TN) + dst * dec
                dbc = dqd * qd - dki * ki - dkd * kd
                dbl = jnp.sum(dkd * kd, axis=0, keepdims=True) + ddec * dec
                dlf_ref[rows, :] = _dot(cum_t, dbc, precision=lax.Precision.HIGHEST) + dbl
                dk_ref[rows, :] = dki * einv + dkd * erem
                if dr == 0:
                    dq_ref[rows, :] = dqd * e
                    dv_ref[rows, :] = dv
                else:
                    dq_ref[rows, :] += dqd * e
                    dv_ref[rows, :] += dv
                return carry

            lax.fori_loop(0, nc, grad_step, 0, unroll=2)

            sig = jax.nn.sigmoid(z)
            one_lb = 1.0 - lbv
            f = lbv + one_lb * sig
            dlf = dlf_ref[...]
            dk = dk_ref[...]
            dsig = (dlf / f - dk) * one_lb
            dz_ref[dr] = (dsig * sig * (1.0 - sig)).astype(BF16)
            dlb_ref[dr:dr + 1, :] = jnp.sum((dlf / f - dk) * (1.0 - sig), axis=0, keepdims=True)

        dz_ref[2] = dv_ref[...].astype(BF16)
        dz_ref[3] = dq_ref[...].astype(BF16)

    sec = lambda s: pl.BlockSpec((None, t, HEAD), lambda h: (s, 0, h))
    col = pl.BlockSpec((seq, HEAD), lambda h: (0, h))
    tvec = pltpu.VMEM((t, HEAD), F32)
    return pl.pallas_call(
        body, name="hgrn_bwd", grid=(d // HEAD,),
        in_specs=[sec(0), sec(1), sec(2), sec(3), sec(4),
                  pl.BlockSpec((2, 3, HEAD), lambda h: (0, 0, h)), pl.BlockSpec((1, HEAD), lambda h: (0, h)),
                  col, col],
        out_specs=[pl.BlockSpec((5, t, HEAD), lambda h: (0, 0, h)),
                   pl.BlockSpec((1, HEAD), lambda h: (0, h)), pl.BlockSpec((2, HEAD), lambda h: (0, h))],
        out_shape=[jax.ShapeDtypeStruct((5, t, d), BF16), jax.ShapeDtypeStruct((1, d), F32),
                   jax.ShapeDtypeStruct((2, d), F32)],
        scratch_shapes=[tvec, tvec, tvec, tvec, tvec, tvec, tvec,
                        pltpu.VMEM((nc, HEAD, HEAD), F32), pltpu.VMEM((HEAD, HEAD), F32)],
        compiler_params=_cp(),
    )(z1, z1, z1, z1, z1, lbl, onorm, o, dr_out)


def _gates(z, lbv):
    e = jnp.exp(-jnp.abs(z))
    r = 1.0 / (1.0 + e)
    er = e * r
    pos = z >= 0.0
    sig = jnp.where(pos, r, er)
    nsig = jnp.where(pos, er, r)
    return sig, nsig, lbv + (1.0 - lbv) * sig


def _split3(x):
    hi = x.astype(BF16)
    r1 = x - hi.astype(F32)
    mid = r1.astype(BF16)
    lo = (r1 - mid.astype(F32)).astype(BF16)
    return jnp.concatenate([hi, mid, lo], axis=1)


def _cumsum_chunk(cum, x):
    y = _dot(cum, _split3(x))
    return y[:, :HEAD] + y[:, HEAD:2 * HEAD] + y[:, 2 * HEAD:]


def _chunk_rows(n):
    return pl.ds(pl.multiple_of(n * CHUNK, CHUNK), CHUNK)


def _group(nc, prefer=(4, 3, 2, 1)):
    return next(u for u in prefer if nc % u == 0)


WIDE_GROUP = (12, 6, 4, 3, 2, 1)


def _decay_pass(lf_ref, bc_ref, dec_ref, cum, nc):
    grp = _group(nc, WIDE_GROUP)

    def step(m, carry):
        ns = [m * grp + u for u in range(grp)]
        lfc = [lf_ref[_chunk_rows(n), :] for n in ns]
        bc = [_cumsum_chunk(cum, x) for x in lfc]
        for u, n in enumerate(ns):
            bc_ref[_chunk_rows(n), :] = bc[u]
            dec_ref[n] = jnp.broadcast_to(jnp.exp(jnp.sum(lfc[u], axis=0, keepdims=True)), (8, HEAD))
        return carry

    lax.fori_loop(0, nc // grp, step, 0)


def hgrn_fwd(z1, lbl, onorm, cx):
    _, t, d = z1.shape
    seq = t - cx
    nc, ncc = t // CHUNK, cx // CHUNK

    grp, sgrp = _group(nc, (6, 4, 3, 2, 1)), _group(nc, WIDE_GROUP)

    def body(zf_ref, zb_ref, v_ref, q_ref, g_ref, lbl_ref, on_ref, o_ref, r_ref,
             lf_ref, k_ref, bc_ref, dec_ref, qd_ref, ki_ref, oacc_ref, ds_ref):
        for dr, z_ref in ((0, zf_ref), (1, zb_ref)):
            lbv, _ = _lower_bound(lbl_ref, dr)
            _, nsig, f = _gates(z_ref[...], lbv)
            lf_ref[...] = jnp.log(f)
            k_ref[...] = (1.0 - lbv) * nsig
            cum, keep = _chunk_consts(dr)
            _decay_pass(lf_ref, bc_ref, dec_ref, cum.astype(BF16), nc)
            bc = bc_ref[...]
            qd_ref[...] = (q_ref[...] * jnp.exp(bc)).astype(BF16)
            ki_ref[...] = (k_ref[...] * jnp.exp(-bc)).astype(BF16)

            def local_step(m, carry, dr=dr, keep=keep):
                ns = [m * grp + u for u in range(grp)]
                rows = [_chunk_rows(n) for n in ns]
                qd = [qd_ref[r, :] for r in rows]
                ki = [ki_ref[r, :] for r in rows]
                vc = [v_ref[r, :].astype(BF16) for r in rows]
                sc = [_dot(qd[u], ki[u], NT) for u in range(grp)]
                inc = [_dot(vc[u], ki[u], TN) for u in range(grp)]
                a = [jnp.where(keep, s, 0.0).astype(BF16) for s in sc]
                intra = [_dot(a[u], vc[u]) for u in range(grp)]
                for u in range(grp):
                    ds_ref[ns[u]] = inc[u] * dec_ref[ns[u]][0:1, :]
                    if dr == 0:
                        oacc_ref[rows[u], :] = intra[u]
                    else:
                        oacc_ref[rows[u], :] += intra[u]
                return carry

            lax.fori_loop(0, nc // grp, local_step, 0)

            def state_step(m, st, dr=dr):
                ns = [_chunk_of_step(m * sgrp + u, dr, nc, ncc) for u in range(sgrp)]
                rows = [_chunk_rows(n) for n in ns]
                sts = []
                for n in ns:
                    sts.append(st.astype(BF16))
                    st = st * dec_ref[n][0:1, :] + ds_ref[n]
                inter = [_dot(qd_ref[rows[u], :], sts[u], NT) for u in range(sgrp)]
                for u in range(sgrp):
                    oacc_ref[rows[u], :] += inter[u]
                return st

            lax.fori_loop(0, nc // sgrp, state_step, jnp.zeros((HEAD, HEAD), F32))

        o = oacc_ref[cx:, :]
        o_ref[...] = o
        rstd = lax.rsqrt(jnp.mean(o * o, axis=-1, keepdims=True) + EPS)
        r_ref[...] = (o * rstd * on_ref[...] * _silu(g_ref[cx:, :])).astype(BF16)

    sec = lambda s: pl.BlockSpec((None, t, HEAD), lambda h: (s, 0, h))
    col = pl.BlockSpec((seq, HEAD), lambda h: (0, h))
    tf32, tb16 = pltpu.VMEM((t, HEAD), F32), pltpu.VMEM((t, HEAD), BF16)
    return pl.pallas_call(
        body, name="hgrn_fwd", grid=(d // HEAD,),
        in_specs=[sec(0), sec(1), sec(2), sec(3), sec(4),
                  pl.BlockSpec((2, 3, HEAD), lambda h: (0, 0, h)), pl.BlockSpec((1, HEAD), lambda h: (0, h))],
        out_specs=[col, col],
        out_shape=[jax.ShapeDtypeStruct((seq, d), F32), jax.ShapeDtypeStruct((seq, d), BF16)],
        scratch_shapes=[tf32, tf32, tf32, pltpu.VMEM((nc, 8, HEAD), F32), tb16, tb16, tf32,
                        pltpu.VMEM((nc, HEAD, HEAD), F32)],
        compiler_params=_cp(),
    )(z1, z1, z1, z1, z1, lbl, onorm)


def hgrn_bwd(z1, lbl, onorm, o, dr_out, cx):
    _, t, d = z1.shape
    seq = t - cx
    nc, ncc = t // CHUNK, cx // CHUNK

    grp2, grp = _group(nc, (6, 4, 3, 2, 1)), _group(nc, (9, 6, 4, 3, 2, 1))

    def body(zf_ref, zb_ref, v_ref, q_ref, g_ref, lbl_ref, on_ref, o_ref, dr_ref,
             dz_ref, don_ref, dlb_ref,
             lf_ref, k_ref, bc_ref, dec_ref, qd_ref, ki_ref, do_ref,
             dqd_ref, dki_ref, dq_ref, dv_ref, ds_ref, dsl_ref):
        o = o_ref[...]
        g = g_ref[cx:, :]
        drv = dr_ref[...]
        onv = on_ref[...]
        rstd = lax.rsqrt(jnp.mean(o * o, axis=-1, keepdims=True) + EPS)
        ohat = o * rstd
        sg = _silu(g)
        don_ref[...] = jnp.sum(drv * ohat * sg, axis=0, keepdims=True)
        dz_ref[4, :cx, :] = jnp.zeros((cx, HEAD), BF16)
        dz_ref[4, cx:, :] = (drv * ohat * onv * _dsilu(g)).astype(BF16)
        dohat = drv * onv * sg
        do_ref[:cx, :] = jnp.zeros((cx, HEAD), BF16)
        do_ref[cx:, :] = (rstd * (dohat - ohat * jnp.mean(dohat * ohat, axis=-1, keepdims=True))).astype(BF16)

        for dr, z_ref in ((0, zf_ref), (1, zb_ref)):
            lbv, _ = _lower_bound(lbl_ref, dr)
            _, nsig, f = _gates(z_ref[...], lbv)
            lf_ref[...] = jnp.log(f)
            k_ref[...] = (1.0 - lbv) * nsig
            cum, keep = _chunk_consts(dr)
            cum_t = _chunk_consts(1 - dr)[0].astype(BF16)
            _decay_pass(lf_ref, bc_ref, dec_ref, cum.astype(BF16), nc)
            bc = bc_ref[...]
            qd_ref[...] = (q_ref[...] * jnp.exp(bc)).astype(BF16)
            ki_ref[...] = (k_ref[...] * jnp.exp(-bc)).astype(BF16)

            def local_step(m, carry, dr=dr, keep=keep):
                ns = [m * grp + u for u in range(grp)]
                rows = [_chunk_rows(n) for n in ns]
                rng = range(grp)
                qd = [qd_ref[r, :] for r in rows]
                ki = [ki_ref[r, :] for r in rows]
                doc = [do_ref[r, :] for r in rows]
                vc = [v_ref[r, :].astype(BF16) for r in rows]
                sc = [_dot(qd[u], ki[u], NT) for u in rng]
                dsc = [_dot(doc[u], vc[u], NT) for u in rng]
                inc = [_dot(vc[u], ki[u], TN) for u in rng]
                dinc = [_dot(doc[u], qd[u], TN) for u in rng]
                a = [jnp.where(keep, s, 0.0).astype(BF16) for s in sc]
                da = [jnp.where(keep, s, 0.0).astype(BF16) for s in dsc]
                dqd = [_dot(da[u], ki[u]) for u in rng]
                dki = [_dot(da[u], qd[u], TN) for u in rng]
                dv = [_dot(a[u], doc[u], TN) for u in rng]
                for u in rng:
                    ds_ref[ns[u]] = inc[u] * dec_ref[ns[u]][0:1, :]
                    dsl_ref[ns[u]] = dinc[u]
                    dqd_ref[rows[u], :] = dqd[u]
                    dki_ref[rows[u], :] = dki[u]
                    if dr == 0:
                        dv_ref[rows[u], :] = dv[u]
                    else:
                        dv_ref[rows[u], :] += dv[u]
                return carry

            lax.fori_loop(0, nc // grp, local_step, 0)

            def state_step(s, st, dr=dr):
                n = _chunk_of_step(s, dr, nc, ncc)
                inc = ds_ref[n]
                ds_ref[n] = st
                return st * dec_ref[n][0:1, :] + inc

            lax.fori_loop(0, nc, state_step, jnp.zeros((HEAD, HEAD), F32), unroll=4)

            def dstate_step(s, dst, dr=dr):
                n = _chunk_of_step(nc - 1 - s, dr, nc, ncc)
                inc = dsl_ref[n]
                dsl_ref[n] = dst
                return inc + dst * dec_ref[n][0:1, :]

            lax.fori_loop(0, nc, dstate_step, jnp.zeros((HEAD, HEAD), F32), unroll=4)

            def grad_step(m, carry, dr=dr, cum_t=cum_t):
                ns = [m * grp2 + u for u in range(grp2)]
                rows = [_chunk_rows(n) for n in ns]
                rng = range(grp2)
                st0 = [ds_ref[n] for n in ns]
                dst = [dsl_ref[n] for n in ns]
                dstb = [x.astype(BF16) for x in dst]
                dec = [dec_ref[n][0:1, :] for n in ns]
                doc = [do_ref[r, :] for r in rows]
                vc = [v_ref[r, :].astype(BF16) for r in rows]
                e = [jnp.exp(bc_ref[r, :]) for r in rows]
                einv = [jnp.exp(-bc_ref[r, :]) for r in rows]
                qd = [q_ref[rows[u], :] * e[u] for u in rng]
                ki = [k_ref[rows[u], :] * einv[u] for u in rng]
                kd = [ki[u] * dec[u] for u in rng]
                dqd_st = [_dot(doc[u], st0[u].astype(BF16)) for u in rng]
                dkd = [_dot(vc[u], dstb[u]) for u in rng]
                dv_st = [_dot(kd[u].astype(BF16), dstb[u], NT) for u in rng]
                dqd = [dqd_ref[rows[u], :] + dqd_st[u] for u in rng]
                dki = [dki_ref[r, :] for r in rows]
                dbc = [dqd[u] * qd[u] - dki[u] * ki[u] - dkd[u] * kd[u] for u in rng]
                cs = [_cumsum_chunk(cum_t, x) for x in dbc]
                for u in rng:
                    ddec = jnp.sum(dst[u] * st0[u], axis=0, keepdims=True)
                    dbl = jnp.sum(dkd[u] * kd[u], axis=0, keepdims=True) + ddec * dec[u]
                    dv_ref[rows[u], :] += dv_st[u]
                    dqd_ref[rows[u], :] = cs[u] + dbl
                    dki_ref[rows[u], :] = dki[u] * einv[u] + dkd[u] * (einv[u] * dec[u])
                    if dr == 0:
                        dq_ref[rows[u], :] = dqd[u] * e[u]
                    else:
                        dq_ref[rows[u], :] += dqd[u] * e[u]
                return carry

            lax.fori_loop(0, nc // grp2, grad_step, 0)

            sig, nsig, f = _gates(z_ref[...], lbv)
            common = (dqd_ref[...] / f - dki_ref[...]) * nsig
            dz_ref[dr] = (common * ((1.0 - lbv) * sig)).astype(BF16)
            dlb_ref[dr:dr + 1, :] = jnp.sum(common, axis=0, keepdims=True)

        dz_ref[2] = dv_ref[...].astype(BF16)
        dz_ref[3] = dq_ref[...].astype(BF16)

    sec = lambda s: pl.BlockSpec((None, t, HEAD), lambda h: (s, 0, h))
    col = pl.BlockSpec((seq, HEAD), lambda h: (0, h))
    tf32, tb16 = pltpu.VMEM((t, HEAD), F32), pltpu.VMEM((t, HEAD), BF16)
    states = pltpu.VMEM((nc, HEAD, HEAD), F32)
    return pl.pallas_call(
        body, name="hgrn_bwd", grid=(d // HEAD,),
        in_specs=[sec(0), sec(1), sec(2), sec(3), sec(4),
                  pl.BlockSpec((2, 3, HEAD), lambda h: (0, 0, h)), pl.BlockSpec((1, HEAD), lambda h: (0, h)),
                  col, col],
        out_specs=[pl.BlockSpec((5, t, HEAD), lambda h: (0, 0, h)),
                   pl.BlockSpec((1, HEAD), lambda h: (0, h)), pl.BlockSpec((2, HEAD), lambda h: (0, h))],
        out_shape=[jax.ShapeDtypeStruct((5, t, d), BF16), jax.ShapeDtypeStruct((1, d), F32),
                   jax.ShapeDtypeStruct((2, d), F32)],
        scratch_shapes=[tf32, tf32, tf32, pltpu.VMEM((nc, 8, HEAD), F32), tb16, tb16, tb16,
                        tf32, tf32, tf32, tf32, states, states],
        compiler_params=_cp(),
    )(z1, z1, z1, z1, z1, lbl, onorm, o, dr_out)


def _place():
    x, y, c = lax.axis_index("x"), lax.axis_index("y"), lax.axis_index("c")
    chips = [(1 - x, y), (x, 1 - y), (1 - x, 1 - y)]
    return x, y, c, chips


def _relay_chips():
    x, y, c, _ = _place()
    first = c == 0
    near = (jnp.where(first, 1 - x, x), jnp.where(first, y, 1 - y))
    far = (jnp.where(first, x, 1 - x), jnp.where(first, 1 - y, y))
    return near, far, (1 - x, 1 - y)


def allgather_shards(bufs):
    n = len(bufs)

    def body(*refs):
        outs = refs[n:2 * n]
        done_ref, send_sems, recv_sems = refs[2 * n:]
        done_ref[...] = jnp.zeros((8, 128), F32)
        x, y, c, _ = _place()
        me = (x, y, c)
        p = 2 * x + y
        near, far, diag = _relay_chips()
        half = [pl.ds(c * (s.shape[1] // 2), s.shape[1] // 2) for s in bufs]
        other = [pl.ds((1 - c) * (s.shape[1] // 2), s.shape[1] // 2) for s in bufs]
        slot = lambda chip: 2 * chip[0] + chip[1]

        def remote(i, k, ref, to):
            return pltpu.make_async_remote_copy(src_ref=ref, dst_ref=ref, send_sem=send_sems.at[6 * i + k],
                                                recv_sem=recv_sems.at[6 * i + k], device_id=to, device_id_type=MESH)

        sends = []

        def send(i, k, ref, to):
            cp = remote(i, k, ref, to)
            cp.start()
            sends.append(cp)

        for i in range(n):
            mine = outs[i].at[p, half[i]]
            send(i, 0, mine, (*near, c))
            send(i, 1, mine, (*far, c))
        for i in range(n):
            landed = outs[i].at[slot(near), half[i]]
            remote(i, 0, landed, me).wait_recv()
            send(i, 2, landed, (*far, c))
            send(i, 3, landed, (x, y, 1 - c))
        for i in range(n):
            landed = outs[i].at[slot(far), half[i]]
            remote(i, 1, landed, me).wait_recv()
            send(i, 4, landed, (x, y, 1 - c))
        for i in range(n):
            landed = outs[i].at[slot(diag), half[i]]
            remote(i, 2, landed, me).wait_recv()
            send(i, 5, landed, (x, y, 1 - c))
        for i in range(n):
            for k, chip in ((3, far), (4, near), (5, diag)):
                remote(i, k, outs[i].at[slot(chip), other[i]], me).wait_recv()
        for cp in sends:
            cp.wait_send()

    return pl.pallas_call(
        body, name="allgather_shards",
        in_specs=[ANY] * n, out_specs=[ANY] * n + [VMEM],
        out_shape=[jax.ShapeDtypeStruct(s.shape, s.dtype) for s in bufs] + [jax.ShapeDtypeStruct((8, 128), F32)],
        input_output_aliases={i: i for i in range(n)},
        scratch_shapes=[pltpu.SemaphoreType.DMA((6 * n,)), pltpu.SemaphoreType.DMA((6 * n,))],
        compiler_params=pltpu.CompilerParams(has_side_effects=True),
    )(*bufs)


def exchange_halves(grads):
    n = len(grads)

    def body(*refs):
        ins, outs = refs[:n], refs[n:2 * n]
        send_sems, recv_sems = refs[2 * n:]
        x, y, c, _ = _place()
        copies = []
        for i in range(n):
            hr = grads[i].shape[1] // 2
            cp = pltpu.make_async_remote_copy(
                src_ref=ins[i].at[:, pl.ds((1 - c) * hr, hr)], dst_ref=outs[i],
                send_sem=send_sems.at[i], recv_sem=recv_sems.at[i],
                device_id=(x, y, 1 - c), device_id_type=MESH)
            cp.start()
            copies.append(cp)
        for cp in copies:
            cp.wait()

    return pl.pallas_call(
        body, name="exchange_halves",
        in_specs=[ANY] * n, out_specs=[ANY] * n,
        out_shape=[jax.ShapeDtypeStruct((4, g.shape[1] // 2, g.shape[2]), g.dtype) for g in grads],
        scratch_shapes=[pltpu.SemaphoreType.DMA((n,)), pltpu.SemaphoreType.DMA((n,))],
        compiler_params=pltpu.CompilerParams(has_side_effects=True),
    )(*grads)


def pair_sum(grad, got, chip_core):
    _, r, cc = grad.shape
    hr = r // 2
    tr = 256 if hr % 256 == 0 else hr
    nb = hr // tr

    def body(cc_ref, a_ref, b_ref, own_ref, sb_ref):
        s = a_ref[...].astype(F32) + b_ref[...].astype(F32)
        sb_ref[...] = s.astype(BF16)

        @pl.when(pl.program_id(1) == cc_ref[0])
        def _():
            own_ref[...] = s

    grid_spec = pltpu.PrefetchScalarGridSpec(
        num_scalar_prefetch=1, grid=(nb, 4),
        in_specs=[pl.BlockSpec((None, tr, cc), lambda i, qi, cc_ref: (qi, cc_ref[1] * nb + i, 0)),
                  pl.BlockSpec((None, tr, cc), lambda i, qi, cc_ref: (qi, i, 0))],
        out_specs=[pl.BlockSpec((tr, cc), lambda i, qi, cc_ref: (i, 0)),
                   pl.BlockSpec((None, tr, cc), lambda i, qi, cc_ref: (qi, i, 0))])
    return pl.pallas_call(
        body, name="pair_sum", grid_spec=grid_spec,
        out_shape=[jax.ShapeDtypeStruct((hr, cc), F32), jax.ShapeDtypeStruct((4, hr, cc), BF16)],
        compiler_params=_cp(),
    )(chip_core, grad, got)


def scatter_to_owners(parts):
    n = len(parts)

    def body(*refs):
        ins, outs = refs[:n], refs[n:2 * n]
        send_sems, recv_sems = refs[2 * n:]
        x, y, c, chips = _place()
        copies = []
        for i in range(n):
            for j, chip in enumerate(chips):
                cp = pltpu.make_async_remote_copy(
                    src_ref=ins[i].at[2 * chip[0] + chip[1]], dst_ref=outs[i].at[j],
                    send_sem=send_sems.at[3 * i + j], recv_sem=recv_sems.at[3 * i + j],
                    device_id=(*chip, c), device_id_type=MESH)
                cp.start()
                copies.append(cp)
        for cp in copies:
            cp.wait()

    return pl.pallas_call(
        body, name="scatter_to_owners",
        in_specs=[ANY] * n, out_specs=[ANY] * n,
        out_shape=[jax.ShapeDtypeStruct((3,) + p.shape[1:], p.dtype) for p in parts],
        scratch_shapes=[pltpu.SemaphoreType.DMA((3 * n,)), pltpu.SemaphoreType.DMA((3 * n,))],
        compiler_params=pltpu.CompilerParams(has_side_effects=True),
    )(*parts)


def owner_sum(own, got, chip_core):
    hr, cc = own.shape
    tr = 256 if hr % 256 == 0 else hr
    nb = hr // tr

    def body(cc_ref, a_ref, b_ref, o_ref):
        s = a_ref[...] + b_ref[0].astype(F32)
        s = s + b_ref[1].astype(F32)
        o_ref[...] = s + b_ref[2].astype(F32)

    grid_spec = pltpu.PrefetchScalarGridSpec(
        num_scalar_prefetch=1, grid=(nb,),
        in_specs=[pl.BlockSpec((tr, cc), lambda i, cc_ref: (i, 0)),
                  pl.BlockSpec((3, tr, cc), lambda i, cc_ref: (0, i, 0))],
        out_specs=pl.BlockSpec((tr, cc), lambda i, cc_ref: (cc_ref[1] * nb + i, 0)))
    return pl.pallas_call(
        body, name="owner_sum", grid_spec=grid_spec,
        out_shape=jax.ShapeDtypeStruct((2 * hr, cc), F32), compiler_params=_cp(),
    )(chip_core, own, got)


def share_halves(bufs):
    n = len(bufs)

    def body(*refs):
        outs = refs[n:2 * n]
        send_sems, recv_sems = refs[2 * n:]
        x, y, c, _ = _place()
        copies = []
        for i in range(n):
            hr = bufs[i].shape[0] // 2
            mine = outs[i].at[pl.ds(c * hr, hr)]
            cp = pltpu.make_async_remote_copy(
                src_ref=mine, dst_ref=mine, send_sem=send_sems.at[i], recv_sem=recv_sems.at[i],
                device_id=(x, y, 1 - c), device_id_type=MESH)
            cp.start()
            copies.append((cp, outs[i].at[pl.ds((1 - c) * hr, hr)]))
        for i, (cp, theirs) in enumerate(copies):
            cp.wait_send()
            pltpu.make_async_remote_copy(
                src_ref=theirs, dst_ref=theirs, send_sem=send_sems.at[i], recv_sem=recv_sems.at[i],
                device_id=(x, y, c), device_id_type=MESH).wait_recv()

    return pl.pallas_call(
        body, name="share_halves",
        in_specs=[ANY] * n, out_specs=[ANY] * n,
        out_shape=[jax.ShapeDtypeStruct(b.shape, b.dtype) for b in bufs],
        input_output_aliases={i: i for i in range(n)},
        scratch_shapes=[pltpu.SemaphoreType.DMA((n,)), pltpu.SemaphoreType.DMA((n,))],
        compiler_params=pltpu.CompilerParams(has_side_effects=True),
    )(*bufs)


def allgather8(v, name):
    r, n = v.shape

    def body(v_ref, out_ref, send_sems, recv_sems):
        x, y, c, _ = _place()
        me = 4 * x + 2 * y + c
        out_ref[me] = v_ref[...]

        def copy(k, slot, to):
            return pltpu.make_async_remote_copy(
                src_ref=v_ref, dst_ref=out_ref.at[slot], send_sem=send_sems.at[k - 1],
                recv_sem=recv_sems.at[k - 1], device_id=to, device_id_type=MESH)

        peers = []
        for k in range(1, 8):
            px = 1 - x if (k >> 2) & 1 else x
            py = 1 - y if (k >> 1) & 1 else y
            pc = 1 - c if k & 1 else c
            peers.append((px, py, pc))
            copy(k, me, (px, py, pc)).start()
        for k, (px, py, pc) in enumerate(peers, start=1):
            copy(k, 4 * px + 2 * py + pc, (x, y, c)).wait_recv()
        for k, peer in enumerate(peers, start=1):
            copy(k, me, peer).wait_send()

    return pl.pallas_call(
        body, name=name, in_specs=[VMEM], out_specs=VMEM,
        out_shape=jax.ShapeDtypeStruct((8, r, n), v.dtype),
        scratch_shapes=[pltpu.SemaphoreType.DMA((7,)), pltpu.SemaphoreType.DMA((7,))],
        compiler_params=_cp(has_side_effects=True),
    )(v)


HBM = pl.BlockSpec(memory_space=pltpu.HBM)
SEM = pl.BlockSpec(memory_space=pltpu.SEMAPHORE)
DATAFLOW = pltpu.SideEffectType.DATAFLOW_SIDE_EFFECTING


def _descriptors(plan, refs, send_sems, recv_sems, arrivals=True):
    x, y, c, _ = _place()
    sends, recvs = plan(refs)
    out = [pltpu.make_async_remote_copy(src_ref=src, dst_ref=dst, send_sem=send_sems.at[k],
                                        recv_sem=recv_sems.at[k], device_id=to, device_id_type=MESH)
           for k, (src, dst, to) in enumerate(sends)]
    if not arrivals:
        return out, []
    inn = [pltpu.make_async_remote_copy(src_ref=land, dst_ref=land, send_sem=send_sems.at[k],
                                        recv_sem=recv_sems.at[k], device_id=(x, y, c), device_id_type=MESH)
           for k, land in enumerate(recvs)]
    return out, inn


def copies_start(name, arrays, n_copies, plan, after):
    na = len(arrays)

    def body(*refs):
        out, _ = _descriptors(plan, refs[:na], refs[na + 1], refs[na + 2], arrivals=False)
        for cp in out:
            cp.start()
        refs[-1][...] = jnp.zeros((8, 128), F32)

    res = pl.pallas_call(
        body, name=name,
        out_shape=(pltpu.SemaphoreType.DMA((n_copies,)), pltpu.SemaphoreType.DMA((n_copies,)),
                   *[pltpu.HBM(a.shape, a.dtype) for a in arrays], jax.ShapeDtypeStruct((8, 128), F32)),
        in_specs=[HBM] * na + [ANY], out_specs=(SEM, SEM, *[HBM] * na, VMEM),
        input_output_aliases={i: i + 2 for i in range(na)},
        compiler_params=pltpu.CompilerParams(has_side_effects=DATAFLOW),
    )(*[pltpu.with_memory_space_constraint(a, pltpu.HBM) for a in arrays], after)
    return res[0], res[1], list(res[2:2 + na]), res[-1]


def copies_wait(name, started, plan, after):
    send_sems, recv_sems, arrays, _ = started
    na = len(arrays)
    after = list(after) if isinstance(after, (list, tuple)) else [after]

    def body(*refs):
        out, inn = _descriptors(plan, refs[:na], refs[na], refs[na + 1])
        for cp in out:
            cp.wait_send()
        for cp in inn:
            cp.wait_recv()
        refs[-1][...] = jnp.zeros((8, 128), F32)

    res = pl.pallas_call(
        body, name=name,
        out_shape=(*[pltpu.HBM(a.shape, a.dtype) for a in arrays], jax.ShapeDtypeStruct((8, 128), F32)),
        in_specs=[HBM] * na + [SEM, SEM] + [ANY] * len(after), out_specs=(*[HBM] * na, VMEM),
        input_output_aliases={i: i for i in range(na)},
        compiler_params=pltpu.CompilerParams(has_side_effects=DATAFLOW),
    )(*arrays, send_sems, recv_sems, *after)
    return list(res[:na]), res[-1]


def _rows_half(r, c):
    return pl.ds(c * (r // 2), r // 2), pl.ds((1 - c) * (r // 2), r // 2)


def plan_gather_neighbours(refs):
    x, y, c, _ = _place()
    p = 2 * x + y
    near, far, _ = _relay_chips()
    sends, recvs = [], []
    for buf in refs:
        mine, _ = _rows_half(buf.shape[1], c)
        for chip in (near, far):
            sends.append((buf.at[p, mine], buf.at[p, mine], (*chip, c)))
            recvs.append(buf.at[2 * chip[0] + chip[1], mine])
    return sends, recvs


def plan_gather_relay(refs):
    _, _, c, _ = _place()
    near, far, diag = _relay_chips()
    sends, recvs = [], []
    for buf in refs:
        mine, _ = _rows_half(buf.shape[1], c)
        landed = buf.at[2 * near[0] + near[1], mine]
        sends.append((landed, landed, (*far, c)))
        recvs.append(buf.at[2 * diag[0] + diag[1], mine])
    return sends, recvs


def plan_gather_d2d(refs):
    x, y, c, _ = _place()
    near, far, diag = _relay_chips()
    sends, recvs = [], []
    for buf in refs:
        mine, theirs = _rows_half(buf.shape[1], c)
        for sent, got in ((near, far), (far, near), (diag, diag)):
            landed = buf.at[2 * sent[0] + sent[1], mine]
            sends.append((landed, landed, (x, y, 1 - c)))
            recvs.append(buf.at[2 * got[0] + got[1], theirs])
    return sends, recvs


def plan_exchange(refs):
    x, y, c, _ = _place()
    n = len(refs) // 2
    sends, recvs = [], []
    for grad, land in zip(refs[:n], refs[n:]):
        _, theirs = _rows_half(grad.shape[1], c)
        sends.append((grad.at[:, theirs], land, (x, y, 1 - c)))
        recvs.append(land)
    return sends, recvs


def plan_scatter(refs):
    x, y, c, chips = _place()
    n = len(refs) // 2
    sends, recvs = [], []
    for part, land in zip(refs[:n], refs[n:]):
        for j, chip in enumerate(chips):
            sends.append((part.at[2 * chip[0] + chip[1]], land.at[j], (*chip, c)))
            recvs.append(land.at[j])
    return sends, recvs


def plan_share(refs):
    x, y, c, _ = _place()
    sends, recvs = [], []
    for buf in refs:
        mine, theirs = _rows_half(buf.shape[0], c)
        sends.append((buf.at[mine], buf.at[mine], (x, y, 1 - c)))
        recvs.append(buf.at[theirs])
    return sends, recvs


def put_in_slot(w, chip, dtype, name):
    r, c = w.shape
    tr = 256 if r % 256 == 0 else r

    def body(chip_ref, w_ref, o_ref):
        o_ref[...] = w_ref[...].astype(dtype)

    grid_spec = pltpu.PrefetchScalarGridSpec(
        num_scalar_prefetch=1, grid=(r // tr,),
        in_specs=[pl.BlockSpec((tr, c), lambda i, chip_ref: (i, 0))],
        out_specs=pl.BlockSpec((None, tr, c), lambda i, chip_ref: (chip_ref[0], i, 0)))
    return pl.pallas_call(body, name=name, grid_spec=grid_spec,
                          out_shape=jax.ShapeDtypeStruct((4, r, c), dtype), compiler_params=_cp())(chip, w)


def ada_fwd(s_in, ada_w, ada_b, tn):
    nl, d, ws = ada_w.shape

    def body(s_ref, w_ref, b_ref, so_ref, mod_ref):
        s = _silu(s_ref[...])
        so_ref[...] = s
        mod_ref[...] = _dot(s.astype(BF16), w_ref[...].astype(BF16)) + b_ref[...]

    return pl.pallas_call(
        body, name="ada_fwd", grid=(nl, ws // tn),
        in_specs=[pl.BlockSpec((16, d), lambda l, j: (0, 0)),
                  pl.BlockSpec((None, d, tn), lambda l, j: (l, 0, j)),
                  pl.BlockSpec((None, 1, tn), lambda l, j: (l, 0, j))],
        out_specs=[pl.BlockSpec((16, d), lambda l, j: (0, 0)),
                   pl.BlockSpec((None, 16, tn), lambda l, j: (l, 0, j))],
        out_shape=[jax.ShapeDtypeStruct((16, d), F32), jax.ShapeDtypeStruct((nl, 16, ws), F32)],
        compiler_params=_cp(),
    )(s_in, ada_w, ada_b)


def _adamw_math(w, g, m, v):
    m = ADAM_B1 * m + (1.0 - ADAM_B1) * g
    v = ADAM_B2 * v + (1.0 - ADAM_B2) * (g * g)
    m_hat = m / (1.0 - ADAM_B1 ** ADAM_STEP)
    v_hat = v / (1.0 - ADAM_B2 ** ADAM_STEP)
    delta = -ADAM_LR * (m_hat / (jnp.sqrt(v_hat) + ADAM_EPS) + ADAM_WD * w)
    return delta, m, v


def ada_bwd_adamw(s, dm, w, m, v):
    nl, d, ws = w.shape
    tr = 256 if d % 256 == 0 else 128

    def body(s_ref, dm_ref, w_ref, m_ref, v_ref, g_ref, dl_ref, mo_ref, vo_ref, dc_ref):
        dmv = dm_ref[...].astype(BF16)
        wv = w_ref[...]
        g = _dot(s_ref[...].astype(BF16), dmv, TN)
        g_ref[...] = g
        dl_ref[...], mo_ref[...], vo_ref[...] = _adamw_math(wv, g, m_ref[...], v_ref[...])
        dc_ref[...] = _dot(dmv[8:16, :], wv.astype(BF16), NT)

    wblk = pl.BlockSpec((None, tr, ws), lambda l, i: (l, i, 0))
    wshape = jax.ShapeDtypeStruct((nl, d, ws), F32)
    return pl.pallas_call(
        body, name="ada_bwd_adamw", grid=(nl, d // tr),
        in_specs=[pl.BlockSpec((16, tr), lambda l, i: (0, i)),
                  pl.BlockSpec((None, 16, ws), lambda l, i: (l, 0, 0)), wblk, wblk, wblk],
        out_specs=[wblk, wblk, wblk, wblk, pl.BlockSpec((None, 8, tr), lambda l, i: (l, 0, i))],
        out_shape=[wshape, wshape, wshape, wshape, jax.ShapeDtypeStruct((nl, 8, d), F32)],
        compiler_params=_cp(),
    )(s, dm, w, m, v)


def adamw(w, g, m, v, name):
    r, c = w.shape
    tr = 256 if r % 256 == 0 else r

    def body(w_ref, g_ref, m_ref, v_ref, dl_ref, mo_ref, vo_ref):
        dl_ref[...], mo_ref[...], vo_ref[...] = _adamw_math(w_ref[...], g_ref[...], m_ref[...], v_ref[...])

    blk = pl.BlockSpec((tr, c), lambda i: (i, 0))
    shape = jax.ShapeDtypeStruct((r, c), F32)
    return pl.pallas_call(body, name=name, grid=(r // tr,), in_specs=[blk] * 4, out_specs=[blk] * 3,
                          out_shape=[shape] * 3, compiler_params=_cp())(w, g, m, v)


SMALL_ROWS = 24
ROW_MOD = 10


def small_reduce(gathered):
    _, rows, d = gathered.shape

    def body(g_ref, o_ref):
        tot = g_ref[0]
        for b in range(1, 8):
            tot = tot + g_ref[b]
        o_ref[0:rows, :] = tot
        for layer in range(2):
            lat = ROW_MOD + 6 * layer
            o_ref[24 + 3 * layer:27 + 3 * layer, :] = tot[lat:lat + 3, :] + tot[lat + 3:lat + 6, :]
        o_ref[30:32, :] = jnp.zeros((2, d), F32)

    return pl.pallas_call(body, name="small_reduce", in_specs=[VMEM], out_specs=VMEM,
                          out_shape=jax.ShapeDtypeStruct((32, d), F32), compiler_params=_cp())(gathered)


def lb_logits_grad(lbl, dlb):
    _, _, n = lbl.shape

    def body(l_ref, d_ref, o_ref):
        for dr in range(2):
            _, (p0, p1, p2) = _lower_bound(l_ref, dr)
            dv = d_ref[dr:dr + 1, :]
            o_ref[dr, 0:1, :] = p0 * p2 * dv
            o_ref[dr, 1:2, :] = p1 * p2 * dv
            o_ref[dr, 2:3, :] = -p2 * (p0 + p1) * dv

    return pl.pallas_call(body, name="lb_logits_grad", in_specs=[VMEM, VMEM], out_specs=VMEM,
                          out_shape=jax.ShapeDtypeStruct((2, 3, n), F32), compiler_params=_cp())(lbl, dlb)


def c_ctx_grad(parts, c_ctx):
    d = c_ctx.shape[1]

    def body(p_ref, c_ref, o_ref):
        tot = p_ref[0, 0:1, :]
        for chip in range(1, 4):
            tot = tot + p_ref[2 * chip, 0:1, :]
        o_ref[...] = tot * _dsilu(c_ref[...])

    return pl.pallas_call(body, name="c_ctx_grad", in_specs=[VMEM, VMEM], out_specs=VMEM,
                          out_shape=jax.ShapeDtypeStruct((1, d), F32), compiler_params=_cp())(parts, c_ctx)


def _reduce_scatter(grads, core, chip_core):
    got = exchange_halves(grads)
    sums = [pair_sum(g, r, core) for g, r in zip(grads, got)]
    recv = scatter_to_owners([sb for _, sb in sums])
    reduced = [owner_sum(s, r, chip_core) for (s, _), r in zip(sums, recv)]
    return share_halves(reduced)


def kernel(x, c, ctx, c_ctx, ada_w, ada_b, pre_g, post_g, ev_w_in, ev_pool_w, ev_pool_scale, ev_conv_w, ev_conv_b, ev_w_out, od_w_in, od_onorm_g, od_w_out, lb_logits, loss_target, m_c_ctx, m_ada_w, m_ada_b, m_pre_g, m_post_g, m_ev_w_in, m_ev_pool_w, m_ev_pool_scale, m_ev_conv_w, m_ev_conv_b, m_ev_w_out, m_od_w_in, m_od_onorm_g, m_od_w_out, m_lb_logits, v_c_ctx, v_ada_w, v_ada_b, v_pre_g, v_post_g, v_ev_w_in, v_ev_pool_w, v_ev_pool_scale, v_ev_conv_w, v_ev_conv_b, v_ev_w_out, v_od_w_in, v_od_onorm_g, v_od_w_out, v_lb_logits):
    _, seq, d = x.shape
    cx = ctx.shape[1]
    t = cx + seq
    half_d = d // 2
    g = half_d // N_POOL
    tn = d // 4
    xi, yi, ci = lax.axis_index("x"), lax.axis_index("y"), lax.axis_index("c")
    chip = 2 * xi + yi
    me = 2 * chip + ci
    core_arr = jnp.reshape(ci, (1,)).astype(jnp.int32)
    chip_arr = jnp.reshape(chip, (1,)).astype(jnp.int32)
    chip_core_arr = jnp.stack([chip, ci]).astype(jnp.int32)

    pad = lambda a, rows: jnp.concatenate([a, jnp.zeros((rows - a.shape[0], g), F32)], axis=0)
    small = jnp.concatenate([
        ev_pool_w.reshape(g, g), pad(ev_conv_w.reshape(3, g), 8), pad(od_onorm_g.reshape(2, g), 8),
        pad(lb_logits.reshape(12, g), 16)], axis=0)
    ev_in_g, ev_out_g, small_g, ev_done = allgather_shards([
        put_in_slot(ev_w_in[0], chip_arr, BF16, "cast_ev_w_in"),
        put_in_slot(ev_w_out[0], chip_arr, BF16, "cast_ev_w_out"),
        put_in_slot(small, chip_arr, F32, "place_small")])
    ev_out3 = ev_out_g.reshape(1, d, d)
    pool_w_full = small_g[:, :g].reshape(4, N_POOL, g // 4, g).transpose(1, 0, 2, 3).reshape(N_POOL, g, g)
    conv_w_full = small_g[:, g:g + 3].transpose(1, 0, 2).reshape(3, half_d)
    onorm_full = small_g[:, g + 8:g + 10].reshape(1, d)
    lbl_full = small_g[:, g + 16:g + 28].reshape(4, 2, 3, 2 * g).transpose(1, 2, 0, 3).reshape(2, 3, d)

    c_rows = jnp.concatenate([c + ev_done[0:1, 0:1], jnp.zeros((7, d), F32)], axis=0)
    c_all = allgather8(c_rows, "allgather_c")[:, 0, :]
    s_in = jnp.concatenate([c_all, c_ctx.reshape(1, d), jnp.zeros((7, d), F32)], axis=0)
    ws_ada = ada_w.shape[2]
    ada_b_mine = lax.dynamic_slice(ada_b, (0, chip * ws_ada), (2, ws_ada)).reshape(2, 1, ws_ada)
    s_act, mod_mine = ada_fwd(s_in, ada_w, ada_b_mine, tn)
    mod_all = allgather8(mod_mine.reshape(32, ws_ada), "allgather_mod")
    od_ici = copies_start("gather_od_ici_start", [
        put_in_slot(od_w_in[0], chip_arr, BF16, "cast_od_w_in"),
        put_in_slot(od_w_out[0], chip_arr, BF16, "cast_od_w_out")], 4, plan_gather_neighbours, mod_all)
    mod_full = mod_all[0::2].reshape(4, 2, 16, ws_ada).transpose(1, 2, 0, 3).reshape(2, 16, 3 * d)
    mod_lat = lax.dynamic_slice(mod_full, (0, me, 0), (2, 1, 3 * d))
    mods = jnp.concatenate([mod_full[:, 8:9], mod_lat], axis=1)
    shift, scale, gate = mods[:, :, :d], mods[:, :, d:2 * d], mods[:, :, 2 * d:]

    xs = jnp.concatenate([ctx[0], x[0]], axis=0)

    h0 = normmod_fwd(xs, pre_g[0:1] + od_ici[3][0:1, 0:1], shift[0], scale[0], cx)
    z0 = mm_nn(h0, ev_in_g, half_d, tn, "mm_ev_in")
    u_a = mix_a_fwd(z0, pool_w_full, ev_pool_scale, cx)
    u_b = mix_b_fwd(z0, conv_w_full, ev_conv_b, cx)
    u = jnp.concatenate([u_a, u_b], axis=1)
    od_relay = copies_start("gather_od_relay_start",
                            copies_wait("gather_od_ici_wait", od_ici, plan_gather_neighbours, u)[0],
                            2, plan_gather_relay, u)
    y0 = mm_nn(u, ev_out3, d, tn, "mm_ev_out")[0]
    xs1 = post_fwd(xs, y0, post_g[0:1] + od_relay[3][0:1, 0:1], gate[0], cx)
    od_d2d = copies_start("gather_od_d2d_start",
                          copies_wait("gather_od_relay_wait", od_relay, plan_gather_relay, xs1)[0],
                          6, plan_gather_d2d, xs1)
    (od_in_g, od_out_g), _ = copies_wait("gather_od_d2d_wait", od_d2d, plan_gather_d2d, od_d2d[3])
    od_out3 = od_out_g.reshape(1, d, d)

    h1 = normmod_fwd(xs1, pre_g[1:2], shift[1], scale[1], cx)
    z1 = mm_nn(h1, od_in_g, d, tn, "mm_od_in")
    o1, r1 = hgrn_fwd(z1, lbl_full, onorm_full, cx)
    y1 = mm_nn(r1, od_out3, d, tn, "mm_od_out")[0]
    sq, dx2 = post_loss(xs1, y1, post_g[1:2], gate[1], loss_target[0], cx)

    dy1, dgate1, dpost1 = post_bwd(dx2, y1, post_g[1:2], gate[1], cx, True)
    dr1 = mm_nt(dy1[None], None, od_out3, tn, "mm_od_out_dx")
    g_od_out = mm_tn(r1, dy1[None], None, d, tn, "mm_od_out_dw")
    dz1, donorm, dlb = hgrn_bwd(z1, lbl_full, onorm_full, o1, dr1, cx)
    dh1 = mm_nt(dz1, None, od_in_g, tn, "mm_od_in_dx")
    g_od_in = mm_tn(h1, dz1, None, od_in_g.shape[2], tn, "mm_od_in_dw")
    dxs1, dpre1, dshift1, dscale1 = normmod_bwd(xs1, dh1, pre_g[1:2], scale[1], dx2, cx, True)

    od_grads = [g_od_in, g_od_out.reshape(4, d // 4, d)]
    half_zone = lambda a, lead, dt: lax.empty((lead, a.shape[1] // 2, a.shape[2]), dt)
    od_ex = copies_start("reduce_od_exchange_start", od_grads + [half_zone(a, 4, a.dtype) for a in od_grads],
                         2, plan_exchange, dxs1)

    dy0, dgate0, dpost0 = post_bwd(dxs1, y0, post_g[0:1] + od_ex[3][0:1, 0:1], gate[0], cx, False)
    du = mm_nt(dy0[None], None, ev_out3, tn, "mm_ev_out_dx")
    g_ev_out = mm_tn(u, dy0[None], None, d, tn, "mm_ev_out_dw")
    od_got, _ = copies_wait("reduce_od_exchange_wait", od_ex, plan_exchange, g_ev_out)
    od_sums = [pair_sum(od_got[i], od_got[2 + i], chip_core_arr) for i in range(2)]
    od_sc = copies_start("reduce_od_scatter_start",
                         [sb for _, sb in od_sums] + [half_zone(a, 3, BF16) for a in od_grads],
                         6, plan_scatter, du)
    dz0a, g_pool_w, dpool_scale = mix_a_bwd(z0, du, pool_w_full, ev_pool_scale + od_sc[3][0:1, 0:1], cx)
    dz0b, dconv_w, dconv_b = mix_b_bwd(z0, du, conv_w_full, ev_conv_b + od_sc[3][0:1, 0:1], cx)
    g_ev_in = mm_tn(h0, dz0a, dz0b, ev_in_g.shape[2], tn, "mm_ev_in_dw")
    ev_grads = [g_ev_in, g_ev_out.reshape(4, d // 4, d), g_pool_w.reshape(4, g, g)]
    ev_ex = copies_start("reduce_ev_exchange_start", ev_grads + [half_zone(a, 4, a.dtype) for a in ev_grads],
                         3, plan_exchange, dpool_scale)
    dh0 = mm_nt(dz0a, dz0b, ev_in_g, tn, "mm_ev_in_dx")
    dxs0, dpre0, dshift0, dscale0 = normmod_bwd(xs, dh0, pre_g[0:1] + ev_ex[3][0:1, 0:1], scale[0], dxs1,
                                                cx, False, True)
    grad_x = dxs0[None]
    ev_got, _ = copies_wait("reduce_ev_exchange_wait", ev_ex, plan_exchange, dxs0)
    ev_sums = [pair_sum(ev_got[i], ev_got[3 + i], chip_core_arr) for i in range(3)]
    od_recv, _ = copies_wait("reduce_od_scatter_wait", od_sc, plan_scatter, dxs0)

    zrow = jnp.zeros((1, d), F32)
    small_rows = jnp.concatenate([
        dpre0, dpre1, dpost0, dpost1,
        jnp.concatenate([dpool_scale, dconv_b], axis=1),
        jnp.concatenate([dconv_w.reshape(1, 3 * half_d), jnp.zeros((1, half_d), F32)], axis=1).reshape(2, d),
        donorm, dlb,
        dshift0[1:2], dscale0[1:2], dgate0[1:2], dshift0[0:1], dscale0[0:1], dgate0[0:1],
        dshift1[1:2], dscale1[1:2], dgate1[1:2], dshift1[0:1], dscale1[0:1], zrow,
        jnp.concatenate([sq[0:1], jnp.zeros((1, d - 128), F32)], axis=1),
        zrow], axis=0)
    small_all = allgather8(small_rows, "allgather_small")
    ev_sc = copies_start("reduce_ev_scatter_start",
                         [sb for _, sb in ev_sums] + [half_zone(a, 3, BF16) for a in ev_grads],
                         9, plan_scatter, small_all)
    od_sh = copies_start("reduce_od_share_start",
                         [owner_sum(od_sums[i][0], od_recv[2 + i], chip_core_arr) for i in range(2)],
                         2, plan_share, dxs0)
    tot = small_reduce(small_all + ev_sc[3][0:1, 0:1])
    loss = tot[22, 0] * (0.5 / d)

    dm_rows = []
    for layer in range(2):
        lat = ROW_MOD + 6 * layer
        dm_lat = small_all[:, lat:lat + 3].reshape(8, 3 * d)
        dm_ctx = tot[lat + 3:lat + 6].reshape(1, 3 * d)
        dm_rows.append(jnp.concatenate([dm_lat, dm_ctx, jnp.zeros((7, 3 * d), F32)], axis=0))
    dm_full = jnp.stack(dm_rows)
    dm_mine = lax.dynamic_slice(dm_full, (0, 0, chip * ws_ada), (2, 16, ws_ada))

    def step(w, gr, m, v, name):
        shape = w.shape
        cols = shape[-1]
        two_d = lambda a: a.reshape(-1, cols)
        dl, mo, vo = adamw(two_d(w), two_d(gr), two_d(m), two_d(v), "adamw_" + name)
        return dl.reshape(shape), mo.reshape(shape), vo.reshape(shape)

    grad_ada_b = tot[24:30].reshape(2, 3 * d)
    grad_pre_g = tot[0:2]
    grad_post_g = tot[2:4]
    grad_ev_pool_scale = tot[4:5, :half_d]
    grad_ev_conv_b = tot[4:5, half_d:]
    conv_w_tot = tot[5:7].reshape(1, 2 * d)[:, :3 * half_d].reshape(3, N_POOL, g)
    grad_ev_conv_w = lax.dynamic_slice(conv_w_tot, (0, chip, 0), (3, 1, g)).reshape(1, 3, g)
    grad_od_onorm_g = lax.dynamic_slice(tot[7:8], (0, chip * 2 * g), (1, 2 * g))
    dlb_mine = lax.dynamic_slice(tot[8:10], (0, chip * 2 * g), (2, 2 * g))
    grad_lb_logits = lb_logits_grad(lb_logits, dlb_mine)
    upd = {
        "ada_b": step(ada_b, grad_ada_b, m_ada_b, v_ada_b, "ada_b"),
        "pre_g": step(pre_g, grad_pre_g, m_pre_g, v_pre_g, "pre_g"),
        "post_g": step(post_g, grad_post_g, m_post_g, v_post_g, "post_g"),
        "ev_pool_scale": step(ev_pool_scale, grad_ev_pool_scale, m_ev_pool_scale, v_ev_pool_scale, "ev_pool_scale"),
        "ev_conv_w": step(ev_conv_w, grad_ev_conv_w, m_ev_conv_w, v_ev_conv_w, "ev_conv_w"),
        "ev_conv_b": step(ev_conv_b, grad_ev_conv_b, m_ev_conv_b, v_ev_conv_b, "ev_conv_b"),
        "od_onorm_g": step(od_onorm_g, grad_od_onorm_g, m_od_onorm_g, v_od_onorm_g, "od_onorm_g"),
        "lb_logits": step(lb_logits, grad_lb_logits, m_lb_logits, v_lb_logits, "lb_logits"),
    }
    grad_ada_w, delta_ada_w, new_m_ada_w, new_v_ada_w, dctx_part = ada_bwd_adamw(
        s_act, dm_mine, ada_w, m_ada_w, v_ada_w)
    upd["ada_w"] = (delta_ada_w, new_m_ada_w, new_v_ada_w)
    (grad_od_w_in, grad_od_w_out), _ = copies_wait("reduce_od_share_wait", od_sh, plan_share, ev_sc[3])
    grad_od_w_in, grad_od_w_out = grad_od_w_in[None], grad_od_w_out[None]
    upd["od_w_in"] = step(od_w_in, grad_od_w_in, m_od_w_in, v_od_w_in, "od_w_in")
    upd["od_w_out"] = step(od_w_out, grad_od_w_out, m_od_w_out, v_od_w_out, "od_w_out")
    done_behind = [dctx_part] + [upd[k][0] for k in (
        "od_w_in", "od_w_out", "ada_b", "pre_g", "post_g", "ev_pool_scale", "ev_conv_w", "ev_conv_b",
        "od_onorm_g", "lb_logits")]
    ev_recv, ev_landed = copies_wait("reduce_ev_scatter_wait", ev_sc, plan_scatter, done_behind)
    grad_ev_w_in, grad_ev_w_out, grad_pool_w = share_halves(
        [owner_sum(ev_sums[i][0], ev_recv[3 + i], chip_core_arr) for i in range(3)])
    dctx_all = allgather8(dctx_part[0] + dctx_part[1] + ev_landed[0:1, 0:1], "allgather_dctx")
    grad_c_ctx = c_ctx_grad(dctx_all, c_ctx.reshape(1, d)).reshape(d)
    grad_ev_w_in, grad_ev_w_out = grad_ev_w_in[None], grad_ev_w_out[None]
    grad_ev_pool_w = grad_pool_w.reshape(1, N_POOL, g // 4, g)
    upd["c_ctx"] = step(c_ctx, grad_c_ctx, m_c_ctx, v_c_ctx, "c_ctx")
    upd["ev_w_in"] = step(ev_w_in, grad_ev_w_in, m_ev_w_in, v_ev_w_in, "ev_w_in")
    upd["ev_pool_w"] = step(ev_pool_w, grad_ev_pool_w, m_ev_pool_w, v_ev_pool_w, "ev_pool_w")
    upd["ev_w_out"] = step(ev_w_out, grad_ev_w_out, m_ev_w_out, v_ev_w_out, "ev_w_out")
    names = ["c_ctx", "ada_w", "ada_b", "pre_g", "post_g", "ev_w_in", "ev_pool_w", "ev_pool_scale",
             "ev_conv_w", "ev_conv_b", "ev_w_out", "od_w_in", "od_onorm_g", "od_w_out", "lb_logits"]
    grads = [grad_c_ctx, grad_ada_w, grad_ada_b, grad_pre_g, grad_post_g, grad_ev_w_in, grad_ev_pool_w,
             grad_ev_pool_scale, grad_ev_conv_w, grad_ev_conv_b, grad_ev_w_out, grad_od_w_in,
             grad_od_onorm_g, grad_od_w_out, grad_lb_logits]
    return (loss, grad_x, *grads, *[upd[k][0] for k in names], *[upd[k][1] for k in names],
            *[upd[k][2] for k in names])
```

```python
import functools

import jax
import jax.numpy as jnp
from jax import lax
from jax.experimental import pallas as pl
from jax.experimental.pallas import tpu as pltpu

EPS = 1e-6
GRID_W_LOG2 = 6
CHUNK = 64
HEAD = 128
N_POOL = 4
ADAM_LR, ADAM_B1, ADAM_B2, ADAM_EPS, ADAM_WD, ADAM_STEP = 0.001, 0.9, 0.999, 1e-08, 0.01, 10
VMEM_LIMIT = 56 * 1024 * 1024
MESH = pl.DeviceIdType.MESH
F32, BF16 = jnp.float32, jnp.bfloat16
ANY = pl.BlockSpec(memory_space=pl.ANY)
VMEM = pl.BlockSpec(memory_space=pltpu.VMEM)


def _cp(**kw):
    return pltpu.CompilerParams(vmem_limit_bytes=VMEM_LIMIT, **kw)


def _silu(x):
    return x * jax.nn.sigmoid(x)


def _dsilu(x):
    s = jax.nn.sigmoid(x)
    return s * (1.0 + x * (1.0 - s))


def _dot(a, b, dims=((1,), (0,)), precision=None):
    return lax.dot_general(a, b, (dims, ((), ())), preferred_element_type=F32, precision=precision)


NN = ((1,), (0,))
NT = ((1,), (1,))
TN = ((0,), (0,))


def _row_block(cx):
    return 256 if cx % 256 == 0 else 128


def normmod_fwd(xs, g, shift, scale, cx):
    t, d = xs.shape
    tm = _row_block(cx)
    nctx = cx // tm

    def body(x_ref, g_ref, sh_ref, sc_ref, h_ref):
        is_ctx = pl.program_id(0) < nctx
        x = x_ref[...]
        rstd = lax.rsqrt(jnp.mean(x * x, axis=-1, keepdims=True) + EPS)
        sc = jnp.where(is_ctx, sc_ref[0:1, :], sc_ref[1:2, :])
        sh = jnp.where(is_ctx, sh_ref[0:1, :], sh_ref[1:2, :])
        h_ref[...] = ((x * rstd) * g_ref[...] * (1.0 + sc) + sh).astype(BF16)

    row = pl.BlockSpec((tm, d), lambda i: (i, 0))
    vec = lambda r: pl.BlockSpec((r, d), lambda i: (0, 0))
    return pl.pallas_call(
        body, name="normmod_fwd", grid=(t // tm,),
        in_specs=[row, vec(1), vec(2), vec(2)], out_specs=row,
        out_shape=jax.ShapeDtypeStruct((t, d), BF16), compiler_params=_cp(),
    )(xs, g, shift, scale)


def normmod_bwd(xs, dh, g, scale, dres, cx, res_is_latent_only, dx_latent_only=False):
    t, d = xs.shape
    tm = _row_block(cx)
    nctx = cx // tm

    def body(x_ref, dh_ref, g_ref, sc_ref, dres_ref, dx_ref, dg_ref, dsh_ref, dsc_ref):
        i = pl.program_id(0)
        is_ctx = i < nctx

        @pl.when(i == 0)
        def _():
            dg_ref[...] = jnp.zeros_like(dg_ref)
            dsh_ref[...] = jnp.zeros_like(dsh_ref)
            dsc_ref[...] = jnp.zeros_like(dsc_ref)

        x = x_ref[...]
        dh = dh_ref[...]
        gv = g_ref[...]
        rstd = lax.rsqrt(jnp.mean(x * x, axis=-1, keepdims=True) + EPS)
        xhat = x * rstd
        sc = jnp.where(is_ctx, sc_ref[0:1, :], sc_ref[1:2, :])
        dsh = jnp.sum(dh, axis=0, keepdims=True)
        dhx = dh * xhat
        dsc = jnp.sum(dhx * gv, axis=0, keepdims=True)
        dg_ref[...] += jnp.sum(dhx * (1.0 + sc), axis=0, keepdims=True)
        zero = jnp.zeros_like(dsh)
        dsh_ref[0:1, :] += jnp.where(is_ctx, dsh, zero)
        dsh_ref[1:2, :] += jnp.where(is_ctx, zero, dsh)
        dsc_ref[0:1, :] += jnp.where(is_ctx, dsc, zero)
        dsc_ref[1:2, :] += jnp.where(is_ctx, zero, dsc)
        dxhat = dh * (gv * (1.0 + sc))
        dx = rstd * (dxhat - xhat * jnp.mean(dxhat * xhat, axis=-1, keepdims=True))
        res = dres_ref[...]
        if res_is_latent_only:
            res = jnp.where(is_ctx, jnp.zeros_like(res), res)
        dx_ref[...] = dx + res

    row = pl.BlockSpec((tm, d), lambda i: (i, 0))
    if res_is_latent_only:
        res_spec = pl.BlockSpec((tm, d), lambda i: (jnp.maximum(i - nctx, 0), 0))
    else:
        res_spec = row
    vec = lambda r: pl.BlockSpec((r, d), lambda i: (0, 0))
    dx_spec = pl.BlockSpec((tm, d), lambda i: (jnp.maximum(i - nctx, 0), 0)) if dx_latent_only else row
    return pl.pallas_call(
        body, name="normmod_bwd", grid=(t // tm,),
        in_specs=[row, row, vec(1), vec(2), res_spec],
        out_specs=[dx_spec, vec(1), vec(2), vec(2)],
        out_shape=[jax.ShapeDtypeStruct((t - cx if dx_latent_only else t, d), F32), jax.ShapeDtypeStruct((1, d), F32),
                   jax.ShapeDtypeStruct((2, d), F32), jax.ShapeDtypeStruct((2, d), F32)],
        compiler_params=_cp(),
    )(xs, dh, g, scale, dres)


def post_fwd(xs, y, pg, gate, cx):
    t, d = xs.shape
    tm = _row_block(cx)
    nctx = cx // tm

    def body(x_ref, y_ref, pg_ref, gate_ref, o_ref):
        is_ctx = pl.program_id(0) < nctx
        y = y_ref[...]
        rstd = lax.rsqrt(jnp.mean(y * y, axis=-1, keepdims=True) + EPS)
        gt = jnp.where(is_ctx, gate_ref[0:1, :], gate_ref[1:2, :])
        o_ref[...] = x_ref[...] + gt * ((y * rstd) * pg_ref[...])

    row = pl.BlockSpec((tm, d), lambda i: (i, 0))
    vec = lambda r: pl.BlockSpec((r, d), lambda i: (0, 0))
    return pl.pallas_call(
        body, name="post_fwd", grid=(t // tm,),
        in_specs=[row, row, vec(1), vec(2)], out_specs=row,
        out_shape=jax.ShapeDtypeStruct((t, d), F32), compiler_params=_cp(),
    )(xs, y, pg, gate)


def post_loss(xs, y, pg, gate, target, cx):
    t, d = xs.shape
    n = y.shape[0]
    tm = _row_block(cx)
    nctx = cx // tm

    def body(x_ref, y_ref, pg_ref, gate_ref, tgt_ref, sq_ref, dx_ref):
        @pl.when(pl.program_id(0) == 0)
        def _():
            sq_ref[...] = jnp.zeros_like(sq_ref)

        y = y_ref[...]
        rstd = lax.rsqrt(jnp.mean(y * y, axis=-1, keepdims=True) + EPS)
        x2 = x_ref[...] + gate_ref[1:2, :] * ((y * rstd) * pg_ref[...])
        err = x2 - tgt_ref[...]
        sq_ref[...] += jnp.sum(err * err)
        dx_ref[...] = err * (1.0 / d)

    row = pl.BlockSpec((tm, d), lambda i: (i, 0))
    xrow = pl.BlockSpec((tm, d), lambda i: (i + nctx, 0))
    vec = lambda r: pl.BlockSpec((r, d), lambda i: (0, 0))
    return pl.pallas_call(
        body, name="post_loss", grid=(n // tm,),
        in_specs=[xrow, row, vec(1), vec(2), row],
        out_specs=[pl.BlockSpec((8, 128), lambda i: (0, 0)), row],
        out_shape=[jax.ShapeDtypeStruct((8, 128), F32), jax.ShapeDtypeStruct((n, d), F32)],
        compiler_params=_cp(),
    )(xs, y, pg, gate, target)


def post_bwd(dxo, y, pg, gate, cx, latent_only):
    m, d = y.shape
    tm = _row_block(cx)
    nctx = 0 if latent_only else cx // tm

    def body(dx_ref, y_ref, pg_ref, gate_ref, dy_ref, dgate_ref, dpg_ref):
        i = pl.program_id(0)
        is_ctx = i < nctx

        @pl.when(i == 0)
        def _():
            dgate_ref[...] = jnp.zeros_like(dgate_ref)
            dpg_ref[...] = jnp.zeros_like(dpg_ref)

        y = y_ref[...]
        dx = dx_ref[...]
        pgv = pg_ref[...]
        rstd = lax.rsqrt(jnp.mean(y * y, axis=-1, keepdims=True) + EPS)
        yhat = y * rstd
        gt = jnp.where(is_ctx, gate_ref[0:1, :], gate_ref[1:2, :])
        dxy = dx * yhat
        dgt = jnp.sum(dxy * pgv, axis=0, keepdims=True)
        zero = jnp.zeros_like(dgt)
        dgate_ref[0:1, :] += jnp.where(is_ctx, dgt, zero)
        dgate_ref[1:2, :] += jnp.where(is_ctx, zero, dgt)
        dpg_ref[...] += jnp.sum(dxy * gt, axis=0, keepdims=True)
        dyhat = dx * (gt * pgv)
        dy = rstd * (dyhat - yhat * jnp.mean(dyhat * yhat, axis=-1, keepdims=True))
        dy_ref[...] = dy.astype(BF16)

    row = pl.BlockSpec((tm, d), lambda i: (i, 0))
    vec = lambda r: pl.BlockSpec((r, d), lambda i: (0, 0))
    return pl.pallas_call(
        body, name="post_bwd", grid=(m // tm,),
        in_specs=[row, row, vec(1), vec(2)], out_specs=[row, vec(2), vec(1)],
        out_shape=[jax.ShapeDtypeStruct((m, d), BF16), jax.ShapeDtypeStruct((2, d), F32),
                   jax.ShapeDtypeStruct((1, d), F32)],
        compiler_params=_cp(),
    )(dxo, y, pg, gate)


def _split_rows(m):
    for cand in (1024, 768, 512, 384, 256, 128):
        if m % cand == 0 and m // cand >= 2:
            return cand
    return m


def mm_nn(a, w3, sec, tn, name):
    m, k = a.shape
    q, _, ws = w3.shape
    n = q * ws
    tpq, tps = ws // tn, sec // tn
    tm = next(c for c in (768, 512, 256, 128) if m % c == 0)

    def body(a_ref, w_ref, o_ref):
        w = w_ref[...]

        def step(i, carry):
            rows = pl.ds(pl.multiple_of(i * tm, tm), tm)
            o_ref[rows, :] = _dot(a_ref[rows, :], w)
            return carry

        lax.fori_loop(0, m // tm, step, 0)

    return pl.pallas_call(
        body, name=name, grid=(n // tn,),
        in_specs=[pl.BlockSpec((m, k), lambda j: (0, 0)),
                  pl.BlockSpec((None, k, tn), lambda j: (j // tpq, 0, j % tpq))],
        out_specs=pl.BlockSpec((None, m, tn), lambda j: (j // tps, 0, j % tps)),
        out_shape=jax.ShapeDtypeStruct((n // sec, m, sec), F32), compiler_params=_cp(),
    )(a, w3)


def _two_stacks(a3, b3, tn):
    sec = a3.shape[2]
    tps = sec // tn
    n1 = a3.shape[0] * tps
    first = lambda j: (jnp.minimum(j, n1 - 1) // tps, jnp.minimum(j, n1 - 1) % tps)
    second = lambda j: (jnp.maximum(j - n1, 0) // tps, jnp.maximum(j - n1, 0) % tps)
    return n1, first, second


def mm_nt(a3, b3, w3, tn, name):
    if b3 is None:
        b3 = a3
    _, m, sec = a3.shape
    q, k, ws = w3.shape
    n = q * ws
    tpq = ws // tn
    mb = _split_rows(m)
    n1, first, second = _two_stacks(a3, b3, tn)

    def body(a_ref, b_ref, w_ref, o_ref):
        j = pl.program_id(1)

        @pl.when(j == 0)
        def _():
            o_ref[...] = jnp.zeros_like(o_ref)

        @pl.when(j < n1)
        def _():
            o_ref[...] += _dot(a_ref[...], w_ref[...], NT)

        @pl.when(j >= n1)
        def _():
            o_ref[...] += _dot(b_ref[...], w_ref[...], NT)

    return pl.pallas_call(
        body, name=name, grid=(m // mb, n // tn),
        in_specs=[pl.BlockSpec((None, mb, tn), lambda i, j: (first(j)[0], i, first(j)[1])),
                  pl.BlockSpec((None, mb, tn), lambda i, j: (second(j)[0], i, second(j)[1])),
                  pl.BlockSpec((None, k, tn), lambda i, j: (j // tpq, 0, j % tpq))],
        out_specs=pl.BlockSpec((mb, k), lambda i, j: (i, 0)),
        out_shape=jax.ShapeDtypeStruct((m, k), F32), compiler_params=_cp(),
    )(a3, b3, w3)


def mm_tn(a, b3, c3, ws, tn, name):
    m, k = a.shape
    sec = b3.shape[2]
    n = (b3.shape[0] + (0 if c3 is None else c3.shape[0])) * sec
    if c3 is None:
        c3 = b3
    tpq = ws // tn
    kb = 256 if k % 256 == 0 else 128
    n1, first, second = _two_stacks(b3, c3, tn)

    def body(a_ref, b_ref, c_ref, o_ref):
        def product(rhs_ref):
            rhs = rhs_ref[...]
            for i in range(k // kb):
                o_ref[i * kb:(i + 1) * kb, :] = _dot(a_ref[:, i * kb:(i + 1) * kb], rhs, TN).astype(BF16)

        @pl.when(pl.program_id(0) < n1)
        def _():
            product(b_ref)

        @pl.when(pl.program_id(0) >= n1)
        def _():
            product(c_ref)

    return pl.pallas_call(
        body, name=name, grid=(n // tn,),
        in_specs=[pl.BlockSpec((m, k), lambda j: (0, 0)),
                  pl.BlockSpec((None, m, tn), lambda j: (first(j)[0], 0, first(j)[1])),
                  pl.BlockSpec((None, m, tn), lambda j: (second(j)[0], 0, second(j)[1]))],
        out_specs=pl.BlockSpec((None, k, tn), lambda j: (j // tpq, 0, j % tpq)),
        out_shape=jax.ShapeDtypeStruct((n // ws, k, ws), BF16), compiler_params=_cp(),
    )(a, b3, c3)


POOL_REACH = 8 << GRID_W_LOG2


def _token_parts(tok, cx):
    lat = tok - cx
    return tok < cx, lat >> GRID_W_LOG2, lat & ((1 << GRID_W_LOG2) - 1)


def _pool_mask(gi, row0, col0, tm, ncols, cx, transposed):
    half = jnp.left_shift(1, gi)
    r = lax.broadcasted_iota(jnp.int32, (tm, 1), 0) + row0
    c = lax.broadcasted_iota(jnp.int32, (1, ncols), 1) + col0
    out_tok, src_tok = (c, r) if transposed else (r, c)
    o_ctx, o_row, o_col = _token_parts(out_tok, cx)
    s_ctx, s_row, s_col = _token_parts(src_tok, cx)

    def inside(o, s):
        return (s >= o - half) & (s <= o + half - 1)

    ctx_hit = o_ctx & s_ctx & inside(out_tok, src_tok)
    lat_hit = (~o_ctx) & (~s_ctx) & inside(o_row, s_row) & inside(o_col, s_col)
    return jnp.where(ctx_hit | lat_hit, 1.0, 0.0).astype(BF16)


def _pool_inv_count(gi, row0, tm, cx, seq):
    half = jnp.left_shift(1, gi)
    r = lax.broadcasted_iota(jnp.int32, (tm, 1), 0) + row0
    is_ctx, row, col = _token_parts(r, cx)

    def count(pos, size):
        return jnp.minimum(pos + half - 1, size - 1) - jnp.maximum(pos - half, 0) + 1

    cnt = jnp.where(is_ctx, count(r, cx), count(row, seq >> GRID_W_LOG2) * count(col, 1 << GRID_W_LOG2))
    return 1.0 / cnt.astype(F32)


def _lat_band(tm):
    side = POOL_REACH // tm
    return side, 2 * side + 1


def _lat_mask(gi, tm, cx, transposed):
    side, band = _lat_band(tm)
    return _pool_mask(gi, cx + side * tm, cx, tm, band * tm, cx, transposed)


def _store_padded_lat(dst_ref, lat, tm):
    side, _ = _lat_band(tm)
    seq = lat.shape[0]
    zeros = jnp.zeros((side * tm, lat.shape[1]), dst_ref.dtype)
    dst_ref[0:side * tm, :] = zeros
    dst_ref[side * tm + seq:, :] = zeros
    dst_ref[side * tm:side * tm + seq, :] = lat.astype(dst_ref.dtype)


def mix_a_fwd(z0, pool_w, pool_scale, cx):
    _, t, half_d = z0.shape
    g = half_d // N_POOL
    seq = t - cx
    tm = _row_block(cx)
    side, band = _lat_band(tm)

    def body(v_ref, ag_ref, w_ref, sc_ref, u_ref, vlat_ref, mask_ref):
        gi = pl.program_id(0)
        w = w_ref[...].astype(BF16)
        sc = sc_ref[...]
        _store_padded_lat(vlat_ref, v_ref[cx:, :], tm)
        mask_ref[...] = _lat_mask(gi, tm, cx, False)

        def finish(row0, window_sum):
            rows = pl.ds(row0, tm)
            pooled = window_sum * _pool_inv_count(gi, row0, tm, cx, seq) - v_ref[rows, :]
            mixed = _dot(pooled.astype(BF16), w) * sc
            u_ref[rows, :] = (mixed * _silu(ag_ref[rows, :])).astype(BF16)

        vctx = v_ref[0:cx, :].astype(BF16)
        for i in range(cx // tm):
            finish(i * tm, _dot(_pool_mask(gi, i * tm, 0, tm, cx, cx, False), vctx))

        def step(j, carry):
            src = vlat_ref[pl.ds(pl.multiple_of(j * tm, tm), band * tm), :]
            finish(pl.multiple_of(cx + j * tm, tm), _dot(mask_ref[...], src))
            return carry

        lax.fori_loop(0, seq // tm, step, 0)

    sec = lambda s: pl.BlockSpec((None, t, g), lambda j: (s, 0, j))
    return pl.pallas_call(
        body, name="mix_a_fwd", grid=(N_POOL,),
        in_specs=[sec(0), sec(1), pl.BlockSpec((None, g, g), lambda j: (j, 0, 0)),
                  pl.BlockSpec((1, g), lambda j: (0, j))],
        out_specs=pl.BlockSpec((t, g), lambda j: (0, j)),
        out_shape=jax.ShapeDtypeStruct((t, half_d), BF16),
        scratch_shapes=[pltpu.VMEM((seq + 2 * side * tm, g), BF16), pltpu.VMEM((tm, band * tm), BF16)],
        compiler_params=_cp(),
    )(z0, z0, pool_w, pool_scale)


def mix_a_bwd(z0, du, pool_w, pool_scale, cx):
    _, t, half_d = z0.shape
    g = half_d // N_POOL
    seq = t - cx
    tm = _row_block(cx)
    gq = g // 4
    side, band = _lat_band(tm)

    def body(v_ref, ag_ref, du_ref, w_ref, sc_ref, dz_ref, dw_ref, dsc_ref,
             vlat_ref, mask_ref, pooled_ref, dmx_ref, dpl_ref, wlat_ref, wctx_ref):
        gi = pl.program_id(0)
        w = w_ref[...].astype(BF16)
        sc = sc_ref[...]
        _store_padded_lat(vlat_ref, v_ref[cx:, :], tm)
        _store_padded_lat(wlat_ref, jnp.zeros((seq, g), BF16), tm)
        mask_ref[...] = _lat_mask(gi, tm, cx, False)

        def first(row0, window_sum, weighted_ref, weighted_row0):
            rows = pl.ds(row0, tm)
            inv = _pool_inv_count(gi, row0, tm, cx, seq)
            pooled = (window_sum * inv - v_ref[rows, :]).astype(BF16)
            pooled_ref[rows, :] = pooled
            mixed = _dot(pooled, w)
            ag = ag_ref[rows, :]
            duv = du_ref[rows, :]
            dz_ref[1, rows, :] = (duv * (mixed * sc) * _dsilu(ag)).astype(BF16)
            dms = duv * _silu(ag)
            dmixed = (dms * sc).astype(BF16)
            dmx_ref[rows, :] = dmixed
            dpooled = _dot(dmixed, w, NT)
            dpl_ref[rows, :] = dpooled
            weighted_ref[pl.ds(weighted_row0, tm), :] = (dpooled * inv).astype(BF16)
            return jnp.sum(dms * mixed, axis=0, keepdims=True)

        dsc = jnp.zeros((1, g), F32)
        vctx = v_ref[0:cx, :].astype(BF16)
        for i in range(cx // tm):
            dsc += first(i * tm, _dot(_pool_mask(gi, i * tm, 0, tm, cx, cx, False), vctx), wctx_ref, i * tm)

        def first_lat(j, acc):
            src = vlat_ref[pl.ds(pl.multiple_of(j * tm, tm), band * tm), :]
            return acc + first(pl.multiple_of(cx + j * tm, tm), _dot(mask_ref[...], src),
                               wlat_ref, pl.multiple_of((side + j) * tm, tm))

        dsc_ref[...] = lax.fori_loop(0, seq // tm, first_lat, dsc)
        dw = _dot(pooled_ref[...], dmx_ref[...], TN)
        for qi in range(4):
            dw_ref[qi] = dw[qi * gq:(qi + 1) * gq, :]

        wctx = wctx_ref[...]
        for i in range(cx // tm):
            rows = pl.ds(i * tm, tm)
            dz_ref[0, rows, :] = (_dot(_pool_mask(gi, i * tm, 0, tm, cx, cx, True), wctx)
                                  - dpl_ref[rows, :]).astype(BF16)
        mask_ref[...] = _lat_mask(gi, tm, cx, True)

        def second_lat(j, carry):
            rows = pl.ds(pl.multiple_of(cx + j * tm, tm), tm)
            src = wlat_ref[pl.ds(pl.multiple_of(j * tm, tm), band * tm), :]
            dz_ref[0, rows, :] = (_dot(mask_ref[...], src) - dpl_ref[rows, :]).astype(BF16)
            return carry

        lax.fori_loop(0, seq // tm, second_lat, 0)

    sec = lambda s: pl.BlockSpec((None, t, g), lambda j: (s, 0, j))
    padded = pltpu.VMEM((seq + 2 * side * tm, g), BF16)
    return pl.pallas_call(
        body, name="mix_a_bwd", grid=(N_POOL,),
        in_specs=[sec(0), sec(1), pl.BlockSpec((t, g), lambda j: (0, j)),
                  pl.BlockSpec((None, g, g), lambda j: (j, 0, 0)),
                  pl.BlockSpec((1, g), lambda j: (0, j))],
        out_specs=[pl.BlockSpec((2, t, g), lambda j: (0, 0, j)),
                   pl.BlockSpec((4, None, gq, g), lambda j: (0, j, 0, 0)),
                   pl.BlockSpec((1, g), lambda j: (0, j))],
        out_shape=[jax.ShapeDtypeStruct((2, t, half_d), BF16),
                   jax.ShapeDtypeStruct((4, N_POOL, gq, g), F32),
                   jax.ShapeDtypeStruct((1, half_d), F32)],
        scratch_shapes=[padded, pltpu.VMEM((tm, band * tm), BF16), pltpu.VMEM((t, g), BF16),
                        pltpu.VMEM((t, g), BF16), pltpu.VMEM((t, g), F32), padded, pltpu.VMEM((cx, g), BF16)],
        compiler_params=_cp(),
    )(z0, z0, du, pool_w, pool_scale)


def _conv_masks(t, cx):
    r = lax.broadcasted_iota(jnp.int32, (t, 1), 0)
    has_prev = jnp.where((r == 0) | (r == cx), 0.0, 1.0)
    has_next = jnp.where((r == cx - 1) | (r == t - 1), 0.0, 1.0)
    return has_prev, has_next


def mix_b_fwd(z0, conv_w, conv_b, cx):
    _, t, half_d = z0.shape
    gb = 128

    def body(bx_ref, bb_ref, bc_ref, bg_ref, w_ref, b_ref, u_ref):
        has_prev, has_next = _conv_masks(t, cx)
        tt = bc_ref[...] * bx_ref[...]
        prev = pltpu.roll(tt, 1, 0) * has_prev
        nxt = pltpu.roll(tt, t - 1, 0) * has_next
        cv = prev * w_ref[0:1, :] + tt * w_ref[1:2, :] + nxt * w_ref[2:3, :] + b_ref[...]
        u_ref[...] = (bb_ref[...] * cv * _silu(bg_ref[...])).astype(BF16)

    sec = lambda s: pl.BlockSpec((None, t, gb), lambda j: (s, 0, j))
    return pl.pallas_call(
        body, name="mix_b_fwd", grid=(half_d // gb,),
        in_specs=[sec(2), sec(3), sec(4), sec(5), pl.BlockSpec((3, gb), lambda j: (0, j)),
                  pl.BlockSpec((1, gb), lambda j: (0, j))],
        out_specs=pl.BlockSpec((t, gb), lambda j: (0, j)),
        out_shape=jax.ShapeDtypeStruct((t, half_d), BF16), compiler_params=_cp(),
    )(z0, z0, z0, z0, conv_w, conv_b)


def mix_b_bwd(z0, du, conv_w, conv_b, cx):
    _, t, half_d = z0.shape
    gb = 128
    off = half_d // gb

    def body(bx_ref, bb_ref, bc_ref, bg_ref, du_ref, w_ref, b_ref, dz_ref, dw_ref, db_ref):
        has_prev, has_next = _conv_masks(t, cx)
        bx, bb, bc, bg = bx_ref[...], bb_ref[...], bc_ref[...], bg_ref[...]
        duv = du_ref[...]
        tt = bc * bx
        prev = pltpu.roll(tt, 1, 0) * has_prev
        nxt = pltpu.roll(tt, t - 1, 0) * has_next
        w0, w1, w2 = w_ref[0:1, :], w_ref[1:2, :], w_ref[2:3, :]
        cv = prev * w0 + tt * w1 + nxt * w2 + b_ref[...]
        sg = _silu(bg)
        dz_ref[1] = (duv * cv * sg).astype(BF16)
        dz_ref[3] = (duv * bb * cv * _dsilu(bg)).astype(BF16)
        dcv = duv * bb * sg
        dw_ref[0:1, :] = jnp.sum(dcv * prev, axis=0, keepdims=True)
        dw_ref[1:2, :] = jnp.sum(dcv * tt, axis=0, keepdims=True)
        dw_ref[2:3, :] = jnp.sum(dcv * nxt, axis=0, keepdims=True)
        db_ref[...] = jnp.sum(dcv, axis=0, keepdims=True)
        dt = (pltpu.roll(dcv * has_prev, t - 1, 0) * w0 + dcv * w1
              + pltpu.roll(dcv * has_next, 1, 0) * w2)
        dz_ref[0] = (dt * bc).astype(BF16)
        dz_ref[2] = (dt * bx).astype(BF16)

    sec = lambda s: pl.BlockSpec((None, t, gb), lambda j: (s, 0, j))
    return pl.pallas_call(
        body, name="mix_b_bwd", grid=(half_d // gb,),
        in_specs=[sec(2), sec(3), sec(4), sec(5), pl.BlockSpec((t, gb), lambda j: (0, j + off)),
                  pl.BlockSpec((3, gb), lambda j: (0, j)), pl.BlockSpec((1, gb), lambda j: (0, j))],
        out_specs=[pl.BlockSpec((4, t, gb), lambda j: (0, 0, j)),
                   pl.BlockSpec((3, gb), lambda j: (0, j)), pl.BlockSpec((1, gb), lambda j: (0, j))],
        out_shape=[jax.ShapeDtypeStruct((4, t, half_d), BF16),
                   jax.ShapeDtypeStruct((3, half_d), F32), jax.ShapeDtypeStruct((1, half_d), F32)],
        compiler_params=_cp(),
    )(z0, z0, z0, z0, du, conv_w, conv_b)


def _lower_bound(lbl_ref, d):
    l0, l1, l2 = lbl_ref[d, 0:1, :], lbl_ref[d, 1:2, :], lbl_ref[d, 2:3, :]
    mx = jnp.maximum(jnp.maximum(l0, l1), l2)
    e0, e1, e2 = jnp.exp(l0 - mx), jnp.exp(l1 - mx), jnp.exp(l2 - mx)
    inv = 1.0 / (e0 + e1 + e2)
    return (e0 + e1) * inv, (e0 * inv, e1 * inv, e2 * inv)


def _chunk_consts(d):
    r = lax.broadcasted_iota(jnp.int32, (CHUNK, CHUNK), 0)
    c = lax.broadcasted_iota(jnp.int32, (CHUNK, CHUNK), 1)
    keep = (c <= r) if d == 0 else (c >= r)
    return jnp.where(keep, 1.0, 0.0).astype(F32), keep


def _chunk_of_step(s, d, nc, ncc):
    if d == 0:
        return s
    return jnp.where(s < ncc, ncc - 1 - s, nc - 1 + ncc - s)


def _chunk_terms(lfc, kc, qc, cum):
    bc = _dot(cum, lfc, precision=lax.Precision.HIGHEST)
    bl = jnp.sum(lfc, axis=0, keepdims=True)
    e = jnp.exp(bc)
    einv = jnp.exp(-bc)
    erem = jnp.exp(bl - bc)
    return e, einv, erem, jnp.exp(bl), qc * e, kc * einv, kc * erem


def hgrn_fwd(z1, lbl, onorm, cx):
    _, t, d = z1.shape
    seq = t - cx
    nc, ncc = t // CHUNK, cx // CHUNK

    def body(zf_ref, zb_ref, v_ref, q_ref, g_ref, lbl_ref, on_ref, o_ref, r_ref,
             lf_ref, k_ref, oacc_ref, st_ref):
        for dr, z_ref in ((0, zf_ref), (1, zb_ref)):
            lbv, _ = _lower_bound(lbl_ref, dr)
            z = z_ref[...]
            lf_ref[...] = jnp.log(lbv + (1.0 - lbv) * jax.nn.sigmoid(z))
            k_ref[...] = (1.0 - lbv) * jax.nn.sigmoid(-z)
            st_ref[...] = jnp.zeros_like(st_ref)
            cum, keep = _chunk_consts(dr)

            def step(s, carry, dr=dr, cum=cum, keep=keep):
                n = _chunk_of_step(s, dr, nc, ncc)
                rows = pl.ds(pl.multiple_of(n * CHUNK, CHUNK), CHUNK)
                vc = v_ref[rows, :].astype(BF16)
                _, _, _, dec, qd, ki, kd = _chunk_terms(lf_ref[rows, :], k_ref[rows, :], q_ref[rows, :], cum)
                qdb = qd.astype(BF16)
                a = jnp.where(keep, _dot(qdb, ki.astype(BF16), NT), 0.0)
                st = st_ref[...]
                oc = _dot(qdb, st.astype(BF16), NT) + _dot(a.astype(BF16), vc)
                st_ref[...] = st * dec + _dot(vc, kd.astype(BF16), TN)
                if dr == 0:
                    oacc_ref[rows, :] = oc
                else:
                    oacc_ref[rows, :] += oc
                return carry

            lax.fori_loop(0, nc, step, 0, unroll=4)

        o = oacc_ref[cx:, :]
        o_ref[...] = o
        rstd = lax.rsqrt(jnp.mean(o * o, axis=-1, keepdims=True) + EPS)
        r_ref[...] = (o * rstd * on_ref[...] * _silu(g_ref[cx:, :])).astype(BF16)

    sec = lambda s: pl.BlockSpec((None, t, HEAD), lambda h: (s, 0, h))
    col = pl.BlockSpec((seq, HEAD), lambda h: (0, h))
    return pl.pallas_call(
        body, name="hgrn_fwd", grid=(d // HEAD,),
        in_specs=[sec(0), sec(1), sec(2), sec(3), sec(4),
                  pl.BlockSpec((2, 3, HEAD), lambda h: (0, 0, h)), pl.BlockSpec((1, HEAD), lambda h: (0, h))],
        out_specs=[col, col],
        out_shape=[jax.ShapeDtypeStruct((seq, d), F32), jax.ShapeDtypeStruct((seq, d), BF16)],
        scratch_shapes=[pltpu.VMEM((t, HEAD), F32), pltpu.VMEM((t, HEAD), F32), pltpu.VMEM((t, HEAD), F32),
                        pltpu.VMEM((HEAD, HEAD), F32)],
        compiler_params=_cp(),
    )(z1, z1, z1, z1, z1, lbl, onorm)


def hgrn_bwd(z1, lbl, onorm, o, dr_out, cx):
    _, t, d = z1.shape
    seq = t - cx
    nc, ncc = t // CHUNK, cx // CHUNK

    def body(zf_ref, zb_ref, v_ref, q_ref, g_ref, lbl_ref, on_ref, o_ref, dr_ref,
             dz_ref, don_ref, dlb_ref,
             lf_ref, k_ref, do_ref, dq_ref, dv_ref, dk_ref, dlf_ref, ssc_ref, dst_ref):
        o = o_ref[...]
        g = g_ref[cx:, :]
        drv = dr_ref[...]
        onv = on_ref[...]
        rstd = lax.rsqrt(jnp.mean(o * o, axis=-1, keepdims=True) + EPS)
        ohat = o * rstd
        sg = _silu(g)
        don_ref[...] = jnp.sum(drv * ohat * sg, axis=0, keepdims=True)
        dz_ref[4, :cx, :] = jnp.zeros((cx, HEAD), BF16)
        dz_ref[4, cx:, :] = (drv * ohat * onv * _dsilu(g)).astype(BF16)
        dohat = drv * onv * sg
        do_ref[:cx, :] = jnp.zeros((cx, HEAD), F32)
        do_ref[cx:, :] = rstd * (dohat - ohat * jnp.mean(dohat * ohat, axis=-1, keepdims=True))

        for dr, z_ref in ((0, zf_ref), (1, zb_ref)):
            lbv, _ = _lower_bound(lbl_ref, dr)
            z = z_ref[...]
            lf_ref[...] = jnp.log(lbv + (1.0 - lbv) * jax.nn.sigmoid(z))
            k_ref[...] = (1.0 - lbv) * jax.nn.sigmoid(-z)
            cum, keep = _chunk_consts(dr)
            cum_t, _ = _chunk_consts(1 - dr)

            st_init = jnp.zeros((HEAD, HEAD), F32)

            def state_step(s, st, dr=dr, cum=cum):
                n = _chunk_of_step(s, dr, nc, ncc)
                rows = pl.ds(pl.multiple_of(n * CHUNK, CHUNK), CHUNK)
                ssc_ref[n] = st
                _, _, _, dec, _, _, kd = _chunk_terms(lf_ref[rows, :], k_ref[rows, :], q_ref[rows, :], cum)
                return st * dec + _dot(v_ref[rows, :].astype(BF16), kd.astype(BF16), TN)

            lax.fori_loop(0, nc, state_step, st_init, unroll=4)
            dst_ref[...] = jnp.zeros_like(dst_ref)

            def grad_step(s2, carry, dr=dr, cum=cum, cum_t=cum_t, keep=keep):
                n = _chunk_of_step(nc - 1 - s2, dr, nc, ncc)
                rows = pl.ds(pl.multiple_of(n * CHUNK, CHUNK), CHUNK)
                vc = v_ref[rows, :].astype(BF16)
                e, einv, erem, dec, qd, ki, kd = _chunk_terms(
                    lf_ref[rows, :], k_ref[rows, :], q_ref[rows, :], cum)
                qdb, kib, kdb = qd.astype(BF16), ki.astype(BF16), kd.astype(BF16)
                doc = do_ref[rows, :].astype(BF16)
                st0 = ssc_ref[n]
                dst = dst_ref[...]
                dstb = dst.astype(BF16)
                a = jnp.where(keep, _dot(qdb, kib, NT), 0.0).astype(BF16)
                da = jnp.where(keep, _dot(doc, vc, NT), 0.0).astype(BF16)
                dqd = _dot(doc, st0.astype(BF16)) + _dot(da, kib)
                dki = _dot(da, qdb, TN)
                dv = _dot(a, doc, TN) + _dot(kdb, dstb, NT)
                dkd = _dot(vc, dstb)
                ddec = jnp.sum(dst * st0, axis=0, keepdims=True)
                dst_ref[...] = _dot(doc, qdb, TN) + dst * dec
                dbc = dqd * qd - dki * ki - dkd * kd
                dbl = jnp.sum(dkd * kd, axis=0, keepdims=True) + ddec * dec
                dlf_ref[rows, :] = _dot(cum_t, dbc, precision=lax.Precision.HIGHEST) + dbl
                dk_ref[rows, :] = dki * einv + dkd * erem
                if dr == 0:
                    dq_ref[rows, :] = dqd * e
                    dv_ref[rows, :] = dv
                else:
                    dq_ref[rows, :] += dqd * e
                    dv_ref[rows, :] += dv
                return carry

            lax.fori_loop(0, nc, grad_step, 0, unroll=2)

            sig = jax.nn.sigmoid(z)
            one_lb = 1.0 - lbv
            f = lbv + one_lb * sig
            dlf = dlf_ref[...]
            dk = dk_ref[...]
            dsig = (dlf / f - dk) * one_lb
            dz_ref[dr] = (dsig * sig * (1.0 - sig)).astype(BF16)
            dlb_ref[dr:dr + 1, :] = jnp.sum((dlf / f - dk) * (1.0 - sig), axis=0, keepdims=True)

        dz_ref[2] = dv_ref[...].astype(BF16)
        dz_ref[3] = dq_ref[...].astype(BF16)

    sec = lambda s: pl.BlockSpec((None, t, HEAD), lambda h: (s, 0, h))
    col = pl.BlockSpec((seq, HEAD), lambda h: (0, h))
    tvec = pltpu.VMEM((t, HEAD), F32)
    return pl.pallas_call(
        body, name="hgrn_bwd", grid=(d // HEAD,),
        in_specs=[sec(0), sec(1), sec(2), sec(3), sec(4),
                  pl.BlockSpec((2, 3, HEAD), lambda h: (0, 0, h)), pl.BlockSpec((1, HEAD), lambda h: (0, h)),
                  col, col],
        out_specs=[pl.BlockSpec((5, t, HEAD), lambda h: (0, 0, h)),
                   pl.BlockSpec((1, HEAD), lambda h: (0, h)), pl.BlockSpec((2, HEAD), lambda h: (0, h))],
        out_shape=[jax.ShapeDtypeStruct((5, t, d), BF16), jax.ShapeDtypeStruct((1, d), F32),
                   jax.ShapeDtypeStruct((2, d), F32)],
        scratch_shapes=[tvec, tvec, tvec, tvec, tvec, tvec, tvec,
                        pltpu.VMEM((nc, HEAD, HEAD), F32), pltpu.VMEM((HEAD, HEAD), F32)],
        compiler_params=_cp(),
    )(z1, z1, z1, z1, z1, lbl, onorm, o, dr_out)


def _gates(z, lbv):
    e = jnp.exp(-jnp.abs(z))
    r = 1.0 / (1.0 + e)
    er = e * r
    pos = z >= 0.0
    sig = jnp.where(pos, r, er)
    nsig = jnp.where(pos, er, r)
    return sig, nsig, lbv + (1.0 - lbv) * sig


def _split3(x):
    hi = x.astype(BF16)
    r1 = x - hi.astype(F32)
    mid = r1.astype(BF16)
    lo = (r1 - mid.astype(F32)).astype(BF16)
    return jnp.concatenate([hi, mid, lo], axis=1)


def _cumsum_chunk(cum, x):
    y = _dot(cum, _split3(x))
    return y[:, :HEAD] + y[:, HEAD:2 * HEAD] + y[:, 2 * HEAD:]


def _chunk_rows(n):
    return pl.ds(pl.multiple_of(n * CHUNK, CHUNK), CHUNK)


def _group(nc, prefer=(4, 3, 2, 1)):
    return next(u for u in prefer if nc % u == 0)


WIDE_GROUP = (12, 6, 4, 3, 2, 1)


def _decay_pass(lf_ref, bc_ref, dec_ref, cum, nc):
    grp = _group(nc, WIDE_GROUP)

    def step(m, carry):
        ns = [m * grp + u for u in range(grp)]
        lfc = [lf_ref[_chunk_rows(n), :] for n in ns]
        bc = [_cumsum_chunk(cum, x) for x in lfc]
        for u, n in enumerate(ns):
            bc_ref[_chunk_rows(n), :] = bc[u]
            dec_ref[n] = jnp.broadcast_to(jnp.exp(jnp.sum(lfc[u], axis=0, keepdims=True)), (8, HEAD))
        return carry

    lax.fori_loop(0, nc // grp, step, 0)


def hgrn_fwd(z1, lbl, onorm, cx):
    _, t, d = z1.shape
    seq = t - cx
    nc, ncc = t // CHUNK, cx // CHUNK

    grp, sgrp = _group(nc, (9, 6, 4, 3, 2, 1)), _group(nc, WIDE_GROUP)

    def body(zf_ref, zb_ref, v_ref, q_ref, g_ref, lbl_ref, on_ref, o_ref, r_ref,
             lf_ref, k_ref, bc_ref, dec_ref, qd_ref, ki_ref, oacc_ref, ds_ref):
        for dr, z_ref in ((0, zf_ref), (1, zb_ref)):
            lbv, _ = _lower_bound(lbl_ref, dr)
            _, nsig, f = _gates(z_ref[...], lbv)
            lf_ref[...] = jnp.log(f)
            k_ref[...] = (1.0 - lbv) * nsig
            cum, keep = _chunk_consts(dr)
            _decay_pass(lf_ref, bc_ref, dec_ref, cum.astype(BF16), nc)
            bc = bc_ref[...]
            qd_ref[...] = (q_ref[...] * jnp.exp(bc)).astype(BF16)
            ki_ref[...] = (k_ref[...] * jnp.exp(-bc)).astype(BF16)

            def local_step(m, carry, dr=dr, keep=keep):
                ns = [m * grp + u for u in range(grp)]
                rows = [_chunk_rows(n) for n in ns]
                qd = [qd_ref[r, :] for r in rows]
                ki = [ki_ref[r, :] for r in rows]
                vc = [v_ref[r, :].astype(BF16) for r in rows]
                sc = [_dot(qd[u], ki[u], NT) for u in range(grp)]
                inc = [_dot(vc[u], ki[u], TN) for u in range(grp)]
                a = [jnp.where(keep, s, 0.0).astype(BF16) for s in sc]
                intra = [_dot(a[u], vc[u]) for u in range(grp)]
                for u in range(grp):
                    ds_ref[ns[u]] = inc[u] * dec_ref[ns[u]][0:1, :]
                    if dr == 0:
                        oacc_ref[rows[u], :] = intra[u]
                    else:
                        oacc_ref[rows[u], :] += intra[u]
                return carry

            lax.fori_loop(0, nc // grp, local_step, 0)

            def state_step(m, st, dr=dr):
                ns = [_chunk_of_step(m * sgrp + u, dr, nc, ncc) for u in range(sgrp)]
                rows = [_chunk_rows(n) for n in ns]
                sts = []
                for n in ns:
                    sts.append(st.astype(BF16))
                    st = st * dec_ref[n][0:1, :] + ds_ref[n]
                inter = [_dot(qd_ref[rows[u], :], sts[u], NT) for u in range(sgrp)]
                for u in range(sgrp):
                    oacc_ref[rows[u], :] += inter[u]
                return st

            lax.fori_loop(0, nc // sgrp, state_step, jnp.zeros((HEAD, HEAD), F32))

        o = oacc_ref[cx:, :]
        o_ref[...] = o
        rstd = lax.rsqrt(jnp.mean(o * o, axis=-1, keepdims=True) + EPS)
        r_ref[...] = (o * rstd * on_ref[...] * _silu(g_ref[cx:, :])).astype(BF16)

    sec = lambda s: pl.BlockSpec((None, t, HEAD), lambda h: (s, 0, h))
    col = pl.BlockSpec((seq, HEAD), lambda h: (0, h))
    tf32, tb16 = pltpu.VMEM((t, HEAD), F32), pltpu.VMEM((t, HEAD), BF16)
    return pl.pallas_call(
        body, name="hgrn_fwd", grid=(d // HEAD,),
        in_specs=[sec(0), sec(1), sec(2), sec(3), sec(4),
                  pl.BlockSpec((2, 3, HEAD), lambda h: (0, 0, h)), pl.BlockSpec((1, HEAD), lambda h: (0, h))],
        out_specs=[col, col],
        out_shape=[jax.ShapeDtypeStruct((seq, d), F32), jax.ShapeDtypeStruct((seq, d), BF16)],
        scratch_shapes=[tf32, tf32, tf32, pltpu.VMEM((nc, 8, HEAD), F32), tb16, tb16, tf32,
                        pltpu.VMEM((nc, HEAD, HEAD), F32)],
        compiler_params=_cp(),
    )(z1, z1, z1, z1, z1, lbl, onorm)


def hgrn_bwd(z1, lbl, onorm, o, dr_out, cx):
    _, t, d = z1.shape
    seq = t - cx
    nc, ncc = t // CHUNK, cx // CHUNK

    grp2, grp = _group(nc, (6, 4, 3, 2, 1)), _group(nc, (9, 6, 4, 3, 2, 1))

    def body(zf_ref, zb_ref, v_ref, q_ref, g_ref, lbl_ref, on_ref, o_ref, dr_ref,
             dz_ref, don_ref, dlb_ref,
             lf_ref, k_ref, bc_ref, dec_ref, qd_ref, ki_ref, do_ref,
             dqd_ref, dki_ref, dq_ref, dv_ref, ds_ref, dsl_ref):
        o = o_ref[...]
        g = g_ref[cx:, :]
        drv = dr_ref[...]
        onv = on_ref[...]
        rstd = lax.rsqrt(jnp.mean(o * o, axis=-1, keepdims=True) + EPS)
        ohat = o * rstd
        sg = _silu(g)
        don_ref[...] = jnp.sum(drv * ohat * sg, axis=0, keepdims=True)
        dz_ref[4, :cx, :] = jnp.zeros((cx, HEAD), BF16)
        dz_ref[4, cx:, :] = (drv * ohat * onv * _dsilu(g)).astype(BF16)
        dohat = drv * onv * sg
        do_ref[:cx, :] = jnp.zeros((cx, HEAD), BF16)
        do_ref[cx:, :] = (rstd * (dohat - ohat * jnp.mean(dohat * ohat, axis=-1, keepdims=True))).astype(BF16)

        for dr, z_ref in ((0, zf_ref), (1, zb_ref)):
            lbv, _ = _lower_bound(lbl_ref, dr)
            _, nsig, f = _gates(z_ref[...], lbv)
            lf_ref[...] = jnp.log(f)
            k_ref[...] = (1.0 - lbv) * nsig
            cum, keep = _chunk_consts(dr)
            cum_t = _chunk_consts(1 - dr)[0].astype(BF16)
            _decay_pass(lf_ref, bc_ref, dec_ref, cum.astype(BF16), nc)
            bc = bc_ref[...]
            qd_ref[...] = (q_ref[...] * jnp.exp(bc)).astype(BF16)
            ki_ref[...] = (k_ref[...] * jnp.exp(-bc)).astype(BF16)

            def local_step(m, carry, dr=dr, keep=keep):
                ns = [m * grp + u for u in range(grp)]
                rows = [_chunk_rows(n) for n in ns]
                rng = range(grp)
                qd = [qd_ref[r, :] for r in rows]
                ki = [ki_ref[r, :] for r in rows]
                doc = [do_ref[r, :] for r in rows]
                vc = [v_ref[r, :].astype(BF16) for r in rows]
                sc = [_dot(qd[u], ki[u], NT) for u in rng]
                dsc = [_dot(doc[u], vc[u], NT) for u in rng]
                inc = [_dot(vc[u], ki[u], TN) for u in rng]
                dinc = [_dot(doc[u], qd[u], TN) for u in rng]
                a = [jnp.where(keep, s, 0.0).astype(BF16) for s in sc]
                da = [jnp.where(keep, s, 0.0).astype(BF16) for s in dsc]
                dqd = [_dot(da[u], ki[u]) for u in rng]
                dki = [_dot(da[u], qd[u], TN) for u in rng]
                dv = [_dot(a[u], doc[u], TN) for u in rng]
                for u in rng:
                    ds_ref[ns[u]] = inc[u] * dec_ref[ns[u]][0:1, :]
                    dsl_ref[ns[u]] = dinc[u]
                    dqd_ref[rows[u], :] = dqd[u]
                    dki_ref[rows[u], :] = dki[u]
                    if dr == 0:
                        dv_ref[rows[u], :] = dv[u]
                    else:
                        dv_ref[rows[u], :] += dv[u]
                return carry

            lax.fori_loop(0, nc // grp, local_step, 0)

            def state_step(s, st, dr=dr):
                n = _chunk_of_step(s, dr, nc, ncc)
                inc = ds_ref[n]
                ds_ref[n] = st
                return st * dec_ref[n][0:1, :] + inc

            lax.fori_loop(0, nc, state_step, jnp.zeros((HEAD, HEAD), F32), unroll=4)

            def dstate_step(s, dst, dr=dr):
                n = _chunk_of_step(nc - 1 - s, dr, nc, ncc)
                inc = dsl_ref[n]
                dsl_ref[n] = dst
                return inc + dst * dec_ref[n][0:1, :]

            lax.fori_loop(0, nc, dstate_step, jnp.zeros((HEAD, HEAD), F32), unroll=4)

            def grad_step(m, carry, dr=dr, cum_t=cum_t):
                ns = [m * grp2 + u for u in range(grp2)]
                rows = [_chunk_rows(n) for n in ns]
                rng = range(grp2)
                st0 = [ds_ref[n] for n in ns]
                dst = [dsl_ref[n] for n in ns]
                dstb = [x.astype(BF16) for x in dst]
                dec = [dec_ref[n][0:1, :] for n in ns]
                doc = [do_ref[r, :] for r in rows]
                vc = [v_ref[r, :].astype(BF16) for r in rows]
                e = [jnp.exp(bc_ref[r, :]) for r in rows]
                einv = [jnp.exp(-bc_ref[r, :]) for r in rows]
                qd = [q_ref[rows[u], :] * e[u] for u in rng]
                ki = [k_ref[rows[u], :] * einv[u] for u in rng]
                kd = [ki[u] * dec[u] for u in rng]
                dqd_st = [_dot(doc[u], st0[u].astype(BF16)) for u in rng]
                dkd = [_dot(vc[u], dstb[u]) for u in rng]
                dv_st = [_dot(kd[u].astype(BF16), dstb[u], NT) for u in rng]
                dqd = [dqd_ref[rows[u], :] + dqd_st[u] for u in rng]
                dki = [dki_ref[r, :] for r in rows]
                dbc = [dqd[u] * qd[u] - dki[u] * ki[u] - dkd[u] * kd[u] for u in rng]
                cs = [_cumsum_chunk(cum_t, x) for x in dbc]
                for u in rng:
                    ddec = jnp.sum(dst[u] * st0[u], axis=0, keepdims=True)
                    dbl = jnp.sum(dkd[u] * kd[u], axis=0, keepdims=True) + ddec * dec[u]
                    dv_ref[rows[u], :] += dv_st[u]
                    dqd_ref[rows[u], :] = cs[u] + dbl
                    dki_ref[rows[u], :] = dki[u] * einv[u] + dkd[u] * (einv[u] * dec[u])
                    if dr == 0:
                        dq_ref[rows[u], :] = dqd[u] * e[u]
                    else:
                        dq_ref[rows[u], :] += dqd[u] * e[u]
                return carry

            lax.fori_loop(0, nc // grp2, grad_step, 0)

            sig, nsig, f = _gates(z_ref[...], lbv)
            common = (dqd_ref[...] / f - dki_ref[...]) * nsig
            dz_ref[dr] = (common * ((1.0 - lbv) * sig)).astype(BF16)
            dlb_ref[dr:dr + 1, :] = jnp.sum(common, axis=0, keepdims=True)

        dz_ref[2] = dv_ref[...].astype(BF16)
        dz_ref[3] = dq_ref[...].astype(BF16)

    sec = lambda s: pl.BlockSpec((None, t, HEAD), lambda h: (s, 0, h))
    col = pl.BlockSpec((seq, HEAD), lambda h: (0, h))
    tf32, tb16 = pltpu.VMEM((t, HEAD), F32), pltpu.VMEM((t, HEAD), BF16)
    states = pltpu.VMEM((nc, HEAD, HEAD), F32)
    return pl.pallas_call(
        body, name="hgrn_bwd", grid=(d // HEAD,),
        in_specs=[sec(0), sec(1), sec(2), sec(3), sec(4),
                  pl.BlockSpec((2, 3, HEAD), lambda h: (0, 0, h)), pl.BlockSpec((1, HEAD), lambda h: (0, h)),
                  col, col],
        out_specs=[pl.BlockSpec((5, t, HEAD), lambda h: (0, 0, h)),
                   pl.BlockSpec((1, HEAD), lambda h: (0, h)), pl.BlockSpec((2, HEAD), lambda h: (0, h))],
        out_shape=[jax.ShapeDtypeStruct((5, t, d), BF16), jax.ShapeDtypeStruct((1, d), F32),
                   jax.ShapeDtypeStruct((2, d), F32)],
        scratch_shapes=[tf32, tf32, tf32, pltpu.VMEM((nc, 8, HEAD), F32), tb16, tb16, tb16,
                        tf32, tf32, tf32, tf32, states, states],
        compiler_params=_cp(),
    )(z1, z1, z1, z1, z1, lbl, onorm, o, dr_out)


def _place():
    x, y, c = lax.axis_index("x"), lax.axis_index("y"), lax.axis_index("c")
    chips = [(1 - x, y), (x, 1 - y), (1 - x, 1 - y)]
    return x, y, c, chips


def _relay_chips():
    x, y, c, _ = _place()
    first = c == 0
    near = (jnp.where(first, 1 - x, x), jnp.where(first, y, 1 - y))
    far = (jnp.where(first, x, 1 - x), jnp.where(first, 1 - y, y))
    return near, far, (1 - x, 1 - y)


def allgather_shards(bufs):
    n = len(bufs)

    def body(*refs):
        outs = refs[n:2 * n]
        done_ref, send_sems, recv_sems = refs[2 * n:]
        done_ref[...] = jnp.zeros((8, 128), F32)
        x, y, c, _ = _place()
        me = (x, y, c)
        p = 2 * x + y
        near, far, diag = _relay_chips()
        half = [pl.ds(c * (s.shape[1] // 2), s.shape[1] // 2) for s in bufs]
        other = [pl.ds((1 - c) * (s.shape[1] // 2), s.shape[1] // 2) for s in bufs]
        slot = lambda chip: 2 * chip[0] + chip[1]

        def remote(i, k, ref, to):
            return pltpu.make_async_remote_copy(src_ref=ref, dst_ref=ref, send_sem=send_sems.at[6 * i + k],
                                                recv_sem=recv_sems.at[6 * i + k], device_id=to, device_id_type=MESH)

        sends = []

        def send(i, k, ref, to):
            cp = remote(i, k, ref, to)
            cp.start()
            sends.append(cp)

        for i in range(n):
            mine = outs[i].at[p, half[i]]
            send(i, 0, mine, (*near, c))
            send(i, 1, mine, (*far, c))
        for i in range(n):
            landed = outs[i].at[slot(near), half[i]]
            remote(i, 0, landed, me).wait_recv()
            send(i, 2, landed, (*far, c))
            send(i, 3, landed, (x, y, 1 - c))
        for i in range(n):
            landed = outs[i].at[slot(far), half[i]]
            remote(i, 1, landed, me).wait_recv()
            send(i, 4, landed, (x, y, 1 - c))
        for i in range(n):
            landed = outs[i].at[slot(diag), half[i]]
            remote(i, 2, landed, me).wait_recv()
            send(i, 5, landed, (x, y, 1 - c))
        for i in range(n):
            for k, chip in ((3, far), (4, near), (5, diag)):
                remote(i, k, outs[i].at[slot(chip), other[i]], me).wait_recv()
        for cp in sends:
            cp.wait_send()

    return pl.pallas_call(
        body, name="allgather_shards",
        in_specs=[ANY] * n, out_specs=[ANY] * n + [VMEM],
        out_shape=[jax.ShapeDtypeStruct(s.shape, s.dtype) for s in bufs] + [jax.ShapeDtypeStruct((8, 128), F32)],
        input_output_aliases={i: i for i in range(n)},
        scratch_shapes=[pltpu.SemaphoreType.DMA((6 * n,)), pltpu.SemaphoreType.DMA((6 * n,))],
        compiler_params=pltpu.CompilerParams(has_side_effects=True),
    )(*bufs)


def exchange_halves(grads):
    n = len(grads)

    def body(*refs):
        ins, outs = refs[:n], refs[n:2 * n]
        send_sems, recv_sems = refs[2 * n:]
        x, y, c, _ = _place()
        copies = []
        for i in range(n):
            hr = grads[i].shape[1] // 2
            cp = pltpu.make_async_remote_copy(
                src_ref=ins[i].at[:, pl.ds((1 - c) * hr, hr)], dst_ref=outs[i],
                send_sem=send_sems.at[i], recv_sem=recv_sems.at[i],
                device_id=(x, y, 1 - c), device_id_type=MESH)
            cp.start()
            copies.append(cp)
        for cp in copies:
            cp.wait()

    return pl.pallas_call(
        body, name="exchange_halves",
        in_specs=[ANY] * n, out_specs=[ANY] * n,
        out_shape=[jax.ShapeDtypeStruct((4, g.shape[1] // 2, g.shape[2]), g.dtype) for g in grads],
        scratch_shapes=[pltpu.SemaphoreType.DMA((n,)), pltpu.SemaphoreType.DMA((n,))],
        compiler_params=pltpu.CompilerParams(has_side_effects=True),
    )(*grads)


def pair_sum(grad, got, chip_core):
    _, r, cc = grad.shape
    hr = r // 2
    tr = 256 if hr % 256 == 0 else hr
    nb = hr // tr

    def body(cc_ref, a_ref, b_ref, own_ref, sb_ref):
        s = a_ref[...].astype(F32) + b_ref[...].astype(F32)
        sb_ref[...] = s.astype(BF16)

        @pl.when(pl.program_id(1) == cc_ref[0])
        def _():
            own_ref[...] = s

    grid_spec = pltpu.PrefetchScalarGridSpec(
        num_scalar_prefetch=1, grid=(nb, 4),
        in_specs=[pl.BlockSpec((None, tr, cc), lambda i, qi, cc_ref: (qi, cc_ref[1] * nb + i, 0)),
                  pl.BlockSpec((None, tr, cc), lambda i, qi, cc_ref: (qi, i, 0))],
        out_specs=[pl.BlockSpec((tr, cc), lambda i, qi, cc_ref: (i, 0)),
                   pl.BlockSpec((None, tr, cc), lambda i, qi, cc_ref: (qi, i, 0))])
    return pl.pallas_call(
        body, name="pair_sum", grid_spec=grid_spec,
        out_shape=[jax.ShapeDtypeStruct((hr, cc), F32), jax.ShapeDtypeStruct((4, hr, cc), BF16)],
        compiler_params=_cp(),
    )(chip_core, grad, got)


def scatter_to_owners(parts):
    n = len(parts)

    def body(*refs):
        ins, outs = refs[:n], refs[n:2 * n]
        send_sems, recv_sems = refs[2 * n:]
        x, y, c, chips = _place()
        copies = []
        for i in range(n):
            for j, chip in enumerate(chips):
                cp = pltpu.make_async_remote_copy(
                    src_ref=ins[i].at[2 * chip[0] + chip[1]], dst_ref=outs[i].at[j],
                    send_sem=send_sems.at[3 * i + j], recv_sem=recv_sems.at[3 * i + j],
                    device_id=(*chip, c), device_id_type=MESH)
                cp.start()
                copies.append(cp)
        for cp in copies:
            cp.wait()

    return pl.pallas_call(
        body, name="scatter_to_owners",
        in_specs=[ANY] * n, out_specs=[ANY] * n,
        out_shape=[jax.ShapeDtypeStruct((3,) + p.shape[1:], p.dtype) for p in parts],
        scratch_shapes=[pltpu.SemaphoreType.DMA((3 * n,)), pltpu.SemaphoreType.DMA((3 * n,))],
        compiler_params=pltpu.CompilerParams(has_side_effects=True),
    )(*parts)


def owner_sum(own, got, chip_core):
    hr, cc = own.shape
    tr = 256 if hr % 256 == 0 else hr
    nb = hr // tr

    def body(cc_ref, a_ref, b_ref, o_ref):
        s = a_ref[...] + b_ref[0].astype(F32)
        s = s + b_ref[1].astype(F32)
        o_ref[...] = s + b_ref[2].astype(F32)

    grid_spec = pltpu.PrefetchScalarGridSpec(
        num_scalar_prefetch=1, grid=(nb,),
        in_specs=[pl.BlockSpec((tr, cc), lambda i, cc_ref: (i, 0)),
                  pl.BlockSpec((3, tr, cc), lambda i, cc_ref: (0, i, 0))],
        out_specs=pl.BlockSpec((tr, cc), lambda i, cc_ref: (cc_ref[1] * nb + i, 0)))
    return pl.pallas_call(
        body, name="owner_sum", grid_spec=grid_spec,
        out_shape=jax.ShapeDtypeStruct((2 * hr, cc), F32), compiler_params=_cp(),
    )(chip_core, own, got)


def share_halves(bufs):
    n = len(bufs)

    def body(*refs):
        outs = refs[n:2 * n]
        send_sems, recv_sems = refs[2 * n:]
        x, y, c, _ = _place()
        copies = []
        for i in range(n):
            hr = bufs[i].shape[0] // 2
            mine = outs[i].at[pl.ds(c * hr, hr)]
            cp = pltpu.make_async_remote_copy(
                src_ref=mine, dst_ref=mine, send_sem=send_sems.at[i], recv_sem=recv_sems.at[i],
                device_id=(x, y, 1 - c), device_id_type=MESH)
            cp.start()
            copies.append((cp, outs[i].at[pl.ds((1 - c) * hr, hr)]))
        for i, (cp, theirs) in enumerate(copies):
            cp.wait_send()
            pltpu.make_async_remote_copy(
                src_ref=theirs, dst_ref=theirs, send_sem=send_sems.at[i], recv_sem=recv_sems.at[i],
                device_id=(x, y, c), device_id_type=MESH).wait_recv()

    return pl.pallas_call(
        body, name="share_halves",
        in_specs=[ANY] * n, out_specs=[ANY] * n,
        out_shape=[jax.ShapeDtypeStruct(b.shape, b.dtype) for b in bufs],
        input_output_aliases={i: i for i in range(n)},
        scratch_shapes=[pltpu.SemaphoreType.DMA((n,)), pltpu.SemaphoreType.DMA((n,))],
        compiler_params=pltpu.CompilerParams(has_side_effects=True),
    )(*bufs)


def allgather8(v, name):
    r, n = v.shape

    def body(v_ref, out_ref, send_sems, recv_sems):
        x, y, c, _ = _place()
        me = 4 * x + 2 * y + c
        out_ref[me] = v_ref[...]

        def copy(k, slot, to):
            return pltpu.make_async_remote_copy(
                src_ref=v_ref, dst_ref=out_ref.at[slot], send_sem=send_sems.at[k - 1],
                recv_sem=recv_sems.at[k - 1], device_id=to, device_id_type=MESH)

        peers = []
        for k in range(1, 8):
            px = 1 - x if (k >> 2) & 1 else x
            py = 1 - y if (k >> 1) & 1 else y
            pc = 1 - c if k & 1 else c
            peers.append((px, py, pc))
            copy(k, me, (px, py, pc)).start()
        for k, (px, py, pc) in enumerate(peers, start=1):
            copy(k, 4 * px + 2 * py + pc, (x, y, c)).wait_recv()
        for k, peer in enumerate(peers, start=1):
            copy(k, me, peer).wait_send()

    return pl.pallas_call(
        body, name=name, in_specs=[VMEM], out_specs=VMEM,
        out_shape=jax.ShapeDtypeStruct((8, r, n), v.dtype),
        scratch_shapes=[pltpu.SemaphoreType.DMA((7,)), pltpu.SemaphoreType.DMA((7,))],
        compiler_params=_cp(has_side_effects=True),
    )(v)


HBM = pl.BlockSpec(memory_space=pltpu.HBM)
SEM = pl.BlockSpec(memory_space=pltpu.SEMAPHORE)
DATAFLOW = pltpu.SideEffectType.DATAFLOW_SIDE_EFFECTING


def _descriptors(plan, refs, send_sems, recv_sems, arrivals=True):
    x, y, c, _ = _place()
    sends, recvs = plan(refs)
    out = [pltpu.make_async_remote_copy(src_ref=src, dst_ref=dst, send_sem=send_sems.at[k],
                                        recv_sem=recv_sems.at[k], device_id=to, device_id_type=MESH)
           for k, (src, dst, to) in enumerate(sends)]
    if not arrivals:
        return out, []
    inn = [pltpu.make_async_remote_copy(src_ref=land, dst_ref=land, send_sem=send_sems.at[k],
                                        recv_sem=recv_sems.at[k], device_id=(x, y, c), device_id_type=MESH)
           for k, land in enumerate(recvs)]
    return out, inn


def copies_start(name, arrays, n_copies, plan, after):
    na = len(arrays)

    def body(*refs):
        out, _ = _descriptors(plan, refs[:na], refs[na + 1], refs[na + 2], arrivals=False)
        for cp in out:
            cp.start()
        refs[-1][...] = jnp.zeros((8, 128), F32)

    res = pl.pallas_call(
        body, name=name,
        out_shape=(pltpu.SemaphoreType.DMA((n_copies,)), pltpu.SemaphoreType.DMA((n_copies,)),
                   *[pltpu.HBM(a.shape, a.dtype) for a in arrays], jax.ShapeDtypeStruct((8, 128), F32)),
        in_specs=[HBM] * na + [ANY], out_specs=(SEM, SEM, *[HBM] * na, VMEM),
        input_output_aliases={i: i + 2 for i in range(na)},
        compiler_params=pltpu.CompilerParams(has_side_effects=DATAFLOW),
    )(*[pltpu.with_memory_space_constraint(a, pltpu.HBM) for a in arrays], after)
    return res[0], res[1], list(res[2:2 + na]), res[-1]


def copies_wait(name, started, plan, after):
    send_sems, recv_sems, arrays, _ = started
    na = len(arrays)
    after = list(after) if isinstance(after, (list, tuple)) else [after]

    def body(*refs):
        out, inn = _descriptors(plan, refs[:na], refs[na], refs[na + 1])
        for cp in out:
            cp.wait_send()
        for cp in inn:
            cp.wait_recv()
        refs[-1][...] = jnp.zeros((8, 128), F32)

    res = pl.pallas_call(
        body, name=name,
        out_shape=(*[pltpu.HBM(a.shape, a.dtype) for a in arrays], jax.ShapeDtypeStruct((8, 128), F32)),
        in_specs=[HBM] * na + [SEM, SEM] + [ANY] * len(after), out_specs=(*[HBM] * na, VMEM),
        input_output_aliases={i: i for i in range(na)},
        compiler_params=pltpu.CompilerParams(has_side_effects=DATAFLOW),
    )(*arrays, send_sems, recv_sems, *after)
    return list(res[:na]), res[-1]


def _rows_half(r, c):
    return pl.ds(c * (r // 2), r // 2), pl.ds((1 - c) * (r // 2), r // 2)


def plan_gather_neighbours(refs):
    x, y, c, _ = _place()
    p = 2 * x + y
    near, far, _ = _relay_chips()
    sends, recvs = [], []
    for buf in refs:
        mine, _ = _rows_half(buf.shape[1], c)
        for chip in (near, far):
            sends.append((buf.at[p, mine], buf.at[p, mine], (*chip, c)))
            recvs.append(buf.at[2 * chip[0] + chip[1], mine])
    return sends, recvs


def plan_gather_relay(refs):
    x, y, c, _ = _place()
    near, far, diag = _relay_chips()
    slot = lambda chip: 2 * chip[0] + chip[1]
    sends, recvs = [], []
    for buf in refs:
        mine, theirs = _rows_half(buf.shape[1], c)
        landed = buf.at[slot(near), mine]
        sends.append((landed, landed, (*far, c)))
        recvs.append(buf.at[slot(diag), mine])
        for sent, got in ((near, far), (far, near)):
            sends.append((buf.at[slot(sent), mine], buf.at[slot(sent), mine], (x, y, 1 - c)))
            recvs.append(buf.at[slot(got), theirs])
    return sends, recvs


def plan_gather_d2d(refs):
    x, y, c, _ = _place()
    _, _, diag = _relay_chips()
    sends, recvs = [], []
    for buf in refs:
        mine, theirs = _rows_half(buf.shape[1], c)
        landed = buf.at[2 * diag[0] + diag[1], mine]
        sends.append((landed, landed, (x, y, 1 - c)))
        recvs.append(buf.at[2 * diag[0] + diag[1], theirs])
    return sends, recvs


def plan_exchange(refs):
    x, y, c, _ = _place()
    n = len(refs) // 2
    sends, recvs = [], []
    for grad, land in zip(refs[:n], refs[n:]):
        _, theirs = _rows_half(grad.shape[1], c)
        sends.append((grad.at[:, theirs], land, (x, y, 1 - c)))
        recvs.append(land)
    return sends, recvs


def plan_scatter(refs):
    x, y, c, chips = _place()
    n = len(refs) // 2
    sends, recvs = [], []
    for part, land in zip(refs[:n], refs[n:]):
        for j, chip in enumerate(chips):
            sends.append((part.at[2 * chip[0] + chip[1]], land.at[j], (*chip, c)))
            recvs.append(land.at[j])
    return sends, recvs


def plan_share(refs):
    x, y, c, _ = _place()
    sends, recvs = [], []
    for buf in refs:
        mine, theirs = _rows_half(buf.shape[0], c)
        sends.append((buf.at[mine], buf.at[mine], (x, y, 1 - c)))
        recvs.append(buf.at[theirs])
    return sends, recvs


def put_in_slot(w, chip, dtype, name):
    r, c = w.shape
    tr = 256 if r % 256 == 0 else r

    def body(chip_ref, w_ref, o_ref):
        o_ref[...] = w_ref[...].astype(dtype)

    grid_spec = pltpu.PrefetchScalarGridSpec(
        num_scalar_prefetch=1, grid=(r // tr,),
        in_specs=[pl.BlockSpec((tr, c), lambda i, chip_ref: (i, 0))],
        out_specs=pl.BlockSpec((None, tr, c), lambda i, chip_ref: (chip_ref[0], i, 0)))
    return pl.pallas_call(body, name=name, grid_spec=grid_spec,
                          out_shape=jax.ShapeDtypeStruct((4, r, c), dtype), compiler_params=_cp())(chip, w)


def ada_fwd(s_in, ada_w, ada_b, tn):
    nl, d, ws = ada_w.shape

    def body(s_ref, w_ref, b_ref, so_ref, mod_ref):
        s = _silu(s_ref[...])
        so_ref[...] = s
        mod_ref[...] = _dot(s.astype(BF16), w_ref[...].astype(BF16)) + b_ref[...]

    return pl.pallas_call(
        body, name="ada_fwd", grid=(nl, ws // tn),
        in_specs=[pl.BlockSpec((16, d), lambda l, j: (0, 0)),
                  pl.BlockSpec((None, d, tn), lambda l, j: (l, 0, j)),
                  pl.BlockSpec((None, 1, tn), lambda l, j: (l, 0, j))],
        out_specs=[pl.BlockSpec((16, d), lambda l, j: (0, 0)),
                   pl.BlockSpec((None, 16, tn), lambda l, j: (l, 0, j))],
        out_shape=[jax.ShapeDtypeStruct((16, d), F32), jax.ShapeDtypeStruct((nl, 16, ws), F32)],
        compiler_params=_cp(),
    )(s_in, ada_w, ada_b)


def _adamw_math(w, g, m, v):
    m = ADAM_B1 * m + (1.0 - ADAM_B1) * g
    v = ADAM_B2 * v + (1.0 - ADAM_B2) * (g * g)
    m_hat = m / (1.0 - ADAM_B1 ** ADAM_STEP)
    v_hat = v / (1.0 - ADAM_B2 ** ADAM_STEP)
    delta = -ADAM_LR * (m_hat / (jnp.sqrt(v_hat) + ADAM_EPS) + ADAM_WD * w)
    return delta, m, v


def ada_bwd_adamw(s, dm, w, m, v):
    nl, d, ws = w.shape
    tr = 256 if d % 256 == 0 else 128

    def body(s_ref, dm_ref, w_ref, m_ref, v_ref, g_ref, dl_ref, mo_ref, vo_ref, dc_ref):
        dmv = dm_ref[...].astype(BF16)
        wv = w_ref[...]
        g = _dot(s_ref[...].astype(BF16), dmv, TN)
        g_ref[...] = g
        dl_ref[...], mo_ref[...], vo_ref[...] = _adamw_math(wv, g, m_ref[...], v_ref[...])
        dc_ref[...] = _dot(dmv[8:16, :], wv.astype(BF16), NT)

    wblk = pl.BlockSpec((None, tr, ws), lambda l, i: (l, i, 0))
    wshape = jax.ShapeDtypeStruct((nl, d, ws), F32)
    return pl.pallas_call(
        body, name="ada_bwd_adamw", grid=(nl, d // tr),
        in_specs=[pl.BlockSpec((16, tr), lambda l, i: (0, i)),
                  pl.BlockSpec((None, 16, ws), lambda l, i: (l, 0, 0)), wblk, wblk, wblk],
        out_specs=[wblk, wblk, wblk, wblk, pl.BlockSpec((None, 8, tr), lambda l, i: (l, 0, i))],
        out_shape=[wshape, wshape, wshape, wshape, jax.ShapeDtypeStruct((nl, 8, d), F32)],
        compiler_params=_cp(),
    )(s, dm, w, m, v)


def adamw(w, g, m, v, name, with_grad=False):
    r, c = w.shape
    tr = 256 if r % 256 == 0 else r

    def body(w_ref, g_ref, m_ref, v_ref, dl_ref, mo_ref, vo_ref, *g_out):
        gv = g_ref[...]
        dl_ref[...], mo_ref[...], vo_ref[...] = _adamw_math(w_ref[...], gv, m_ref[...], v_ref[...])
        if with_grad:
            g_out[0][...] = gv

    blk = pl.BlockSpec((tr, c), lambda i: (i, 0))
    shape = jax.ShapeDtypeStruct((r, c), F32)
    n_out = 4 if with_grad else 3
    return pl.pallas_call(body, name=name, grid=(r // tr,), in_specs=[blk] * 4, out_specs=[blk] * n_out,
                          out_shape=[shape] * n_out, compiler_params=_cp())(w, g, m, v)


SMALL_ROWS = 24
ROW_MOD = 10


def small_reduce(gathered):
    _, rows, d = gathered.shape

    def body(g_ref, o_ref):
        tot = g_ref[0]
        for b in range(1, 8):
            tot = tot + g_ref[b]
        o_ref[0:rows, :] = tot
        for layer in range(2):
            lat = ROW_MOD + 6 * layer
            o_ref[24 + 3 * layer:27 + 3 * layer, :] = tot[lat:lat + 3, :] + tot[lat + 3:lat + 6, :]
        o_ref[30:32, :] = jnp.zeros((2, d), F32)

    return pl.pallas_call(body, name="small_reduce", in_specs=[VMEM], out_specs=VMEM,
                          out_shape=jax.ShapeDtypeStruct((32, d), F32), compiler_params=_cp())(gathered)


def lb_logits_grad(lbl, dlb):
    _, _, n = lbl.shape

    def body(l_ref, d_ref, o_ref):
        for dr in range(2):
            _, (p0, p1, p2) = _lower_bound(l_ref, dr)
            dv = d_ref[dr:dr + 1, :]
            o_ref[dr, 0:1, :] = p0 * p2 * dv
            o_ref[dr, 1:2, :] = p1 * p2 * dv
            o_ref[dr, 2:3, :] = -p2 * (p0 + p1) * dv

    return pl.pallas_call(body, name="lb_logits_grad", in_specs=[VMEM, VMEM], out_specs=VMEM,
                          out_shape=jax.ShapeDtypeStruct((2, 3, n), F32), compiler_params=_cp())(lbl, dlb)


def c_ctx_grad(parts, c_ctx):
    d = c_ctx.shape[1]

    def body(p_ref, c_ref, o_ref):
        tot = p_ref[0, 0:1, :]
        for chip in range(1, 4):
            tot = tot + p_ref[2 * chip, 0:1, :]
        o_ref[...] = tot * _dsilu(c_ref[...])

    return pl.pallas_call(body, name="c_ctx_grad", in_specs=[VMEM, VMEM], out_specs=VMEM,
                          out_shape=jax.ShapeDtypeStruct((1, d), F32), compiler_params=_cp())(parts, c_ctx)


def _reduce_scatter(grads, core, chip_core):
    got = exchange_halves(grads)
    sums = [pair_sum(g, r, core) for g, r in zip(grads, got)]
    recv = scatter_to_owners([sb for _, sb in sums])
    reduced = [owner_sum(s, r, chip_core) for (s, _), r in zip(sums, recv)]
    return share_halves(reduced)


def kernel(x, c, ctx, c_ctx, ada_w, ada_b, pre_g, post_g, ev_w_in, ev_pool_w, ev_pool_scale, ev_conv_w, ev_conv_b, ev_w_out, od_w_in, od_onorm_g, od_w_out, lb_logits, loss_target, m_c_ctx, m_ada_w, m_ada_b, m_pre_g, m_post_g, m_ev_w_in, m_ev_pool_w, m_ev_pool_scale, m_ev_conv_w, m_ev_conv_b, m_ev_w_out, m_od_w_in, m_od_onorm_g, m_od_w_out, m_lb_logits, v_c_ctx, v_ada_w, v_ada_b, v_pre_g, v_post_g, v_ev_w_in, v_ev_pool_w, v_ev_pool_scale, v_ev_conv_w, v_ev_conv_b, v_ev_w_out, v_od_w_in, v_od_onorm_g, v_od_w_out, v_lb_logits):
    _, seq, d = x.shape
    cx = ctx.shape[1]
    t = cx + seq
    half_d = d // 2
    g = half_d // N_POOL
    tn = d // 4
    xi, yi, ci = lax.axis_index("x"), lax.axis_index("y"), lax.axis_index("c")
    chip = 2 * xi + yi
    me = 2 * chip + ci
    core_arr = jnp.reshape(ci, (1,)).astype(jnp.int32)
    chip_arr = jnp.reshape(chip, (1,)).astype(jnp.int32)
    chip_core_arr = jnp.stack([chip, ci]).astype(jnp.int32)

    pad = lambda a, rows: jnp.concatenate([a, jnp.zeros((rows - a.shape[0], g), F32)], axis=0)
    small = jnp.concatenate([
        ev_pool_w.reshape(g, g), pad(ev_conv_w.reshape(3, g), 8), pad(od_onorm_g.reshape(2, g), 8),
        pad(lb_logits.reshape(12, g), 16)], axis=0)
    ev_in_g, ev_out_g, small_g, ev_done = allgather_shards([
        put_in_slot(ev_w_in[0], chip_arr, BF16, "cast_ev_w_in"),
        put_in_slot(ev_w_out[0], chip_arr, BF16, "cast_ev_w_out"),
        put_in_slot(small, chip_arr, F32, "place_small")])
    ev_out3 = ev_out_g.reshape(1, d, d)
    pool_w_full = small_g[:, :g].reshape(4, N_POOL, g // 4, g).transpose(1, 0, 2, 3).reshape(N_POOL, g, g)
    conv_w_full = small_g[:, g:g + 3].transpose(1, 0, 2).reshape(3, half_d)
    onorm_full = small_g[:, g + 8:g + 10].reshape(1, d)
    lbl_full = small_g[:, g + 16:g + 28].reshape(4, 2, 3, 2 * g).transpose(1, 2, 0, 3).reshape(2, 3, d)

    c_rows = jnp.concatenate([c + ev_done[0:1, 0:1], jnp.zeros((7, d), F32)], axis=0)
    c_all = allgather8(c_rows, "allgather_c")[:, 0, :]
    s_in = jnp.concatenate([c_all, c_ctx.reshape(1, d), jnp.zeros((7, d), F32)], axis=0)
    ws_ada = ada_w.shape[2]
    ada_b_mine = lax.dynamic_slice(ada_b, (0, chip * ws_ada), (2, ws_ada)).reshape(2, 1, ws_ada)
    s_act, mod_mine = ada_fwd(s_in, ada_w, ada_b_mine, tn)
    mod_all = allgather8(mod_mine.reshape(32, ws_ada), "allgather_mod")
    od_ici = copies_start("gather_od_ici_start", [
        put_in_slot(od_w_in[0], chip_arr, BF16, "cast_od_w_in"),
        put_in_slot(od_w_out[0], chip_arr, BF16, "cast_od_w_out")], 4, plan_gather_neighbours, mod_all)
    mod_full = mod_all[0::2].reshape(4, 2, 16, ws_ada).transpose(1, 2, 0, 3).reshape(2, 16, 3 * d)
    mod_lat = lax.dynamic_slice(mod_full, (0, me, 0), (2, 1, 3 * d))
    mods = jnp.concatenate([mod_full[:, 8:9], mod_lat], axis=1)
    shift, scale, gate = mods[:, :, :d], mods[:, :, d:2 * d], mods[:, :, 2 * d:]

    xs = jnp.concatenate([ctx[0], x[0]], axis=0)

    h0 = normmod_fwd(xs, pre_g[0:1] + od_ici[3][0:1, 0:1], shift[0], scale[0], cx)
    z0 = mm_nn(h0, ev_in_g, half_d, tn, "mm_ev_in")
    u_a = mix_a_fwd(z0, pool_w_full, ev_pool_scale, cx)
    u_b = mix_b_fwd(z0, conv_w_full, ev_conv_b, cx)
    u = jnp.concatenate([u_a, u_b], axis=1)
    od_relay = copies_start("gather_od_relay_start",
                            copies_wait("gather_od_ici_wait", od_ici, plan_gather_neighbours, u)[0],
                            6, plan_gather_relay, u)
    y0 = mm_nn(u, ev_out3, d, tn, "mm_ev_out")[0]
    xs1 = post_fwd(xs, y0, post_g[0:1] + od_relay[3][0:1, 0:1], gate[0], cx)
    od_d2d = copies_start("gather_od_d2d_start",
                          copies_wait("gather_od_relay_wait", od_relay, plan_gather_relay, xs1)[0],
                          2, plan_gather_d2d, xs1)
    (od_in_g, od_out_g), _ = copies_wait("gather_od_d2d_wait", od_d2d, plan_gather_d2d, od_d2d[3])
    od_out3 = od_out_g.reshape(1, d, d)

    h1 = normmod_fwd(xs1, pre_g[1:2], shift[1], scale[1], cx)
    z1 = mm_nn(h1, od_in_g, d, tn, "mm_od_in")
    o1, r1 = hgrn_fwd(z1, lbl_full, onorm_full, cx)
    y1 = mm_nn(r1, od_out3, d, tn, "mm_od_out")[0]
    sq, dx2 = post_loss(xs1, y1, post_g[1:2], gate[1], loss_target[0], cx)

    dy1, dgate1, dpost1 = post_bwd(dx2, y1, post_g[1:2], gate[1], cx, True)
    dr1 = mm_nt(dy1[None], None, od_out3, tn, "mm_od_out_dx")
    g_od_out = mm_tn(r1, dy1[None], None, d, tn, "mm_od_out_dw")
    dz1, donorm, dlb = hgrn_bwd(z1, lbl_full, onorm_full, o1, dr1, cx)
    dh1 = mm_nt(dz1, None, od_in_g, tn, "mm_od_in_dx")
    g_od_in = mm_tn(h1, dz1, None, od_in_g.shape[2], tn, "mm_od_in_dw")
    dxs1, dpre1, dshift1, dscale1 = normmod_bwd(xs1, dh1, pre_g[1:2], scale[1], dx2, cx, True)

    od_grads = [g_od_in, g_od_out.reshape(4, d // 4, d)]
    half_zone = lambda a, lead, dt: lax.empty((lead, a.shape[1] // 2, a.shape[2]), dt)
    od_ex = copies_start("reduce_od_exchange_start", od_grads + [half_zone(a, 4, a.dtype) for a in od_grads],
                         2, plan_exchange, dxs1)

    dy0, dgate0, dpost0 = post_bwd(dxs1, y0, post_g[0:1] + od_ex[3][0:1, 0:1], gate[0], cx, False)
    du = mm_nt(dy0[None], None, ev_out3, tn, "mm_ev_out_dx")
    g_ev_out = mm_tn(u, dy0[None], None, d, tn, "mm_ev_out_dw")
    od_got, _ = copies_wait("reduce_od_exchange_wait", od_ex, plan_exchange, g_ev_out)
    od_sums = [pair_sum(od_got[i], od_got[2 + i], chip_core_arr) for i in range(2)]
    od_sc = copies_start("reduce_od_scatter_start",
                         [sb for _, sb in od_sums] + [half_zone(a, 3, BF16) for a in od_grads],
                         6, plan_scatter, du)
    dz0a, g_pool_w, dpool_scale = mix_a_bwd(z0, du, pool_w_full, ev_pool_scale + od_sc[3][0:1, 0:1], cx)
    dz0b, dconv_w, dconv_b = mix_b_bwd(z0, du, conv_w_full, ev_conv_b + od_sc[3][0:1, 0:1], cx)
    g_ev_in = mm_tn(h0, dz0a, dz0b, ev_in_g.shape[2], tn, "mm_ev_in_dw")
    ev_grads = [g_ev_in, g_ev_out.reshape(4, d // 4, d), g_pool_w.reshape(4, g, g)]
    ev_ex = copies_start("reduce_ev_exchange_start", ev_grads + [half_zone(a, 4, a.dtype) for a in ev_grads],
                         3, plan_exchange, dpool_scale)
    dh0 = mm_nt(dz0a, dz0b, ev_in_g, tn, "mm_ev_in_dx")
    dxs0, dpre0, dshift0, dscale0 = normmod_bwd(xs, dh0, pre_g[0:1] + ev_ex[3][0:1, 0:1], scale[0], dxs1,
                                                cx, False, True)
    grad_x = dxs0[None]
    ev_got, _ = copies_wait("reduce_ev_exchange_wait", ev_ex, plan_exchange, dxs0)
    ev_sums = [pair_sum(ev_got[i], ev_got[3 + i], chip_core_arr) for i in range(3)]
    od_recv, _ = copies_wait("reduce_od_scatter_wait", od_sc, plan_scatter, dxs0)

    zrow = jnp.zeros((1, d), F32)
    small_rows = jnp.concatenate([
        dpre0, dpre1, dpost0, dpost1,
        jnp.concatenate([dpool_scale, dconv_b], axis=1),
        jnp.concatenate([dconv_w.reshape(1, 3 * half_d), jnp.zeros((1, half_d), F32)], axis=1).reshape(2, d),
        donorm, dlb,
        dshift0[1:2], dscale0[1:2], dgate0[1:2], dshift0[0:1], dscale0[0:1], dgate0[0:1],
        dshift1[1:2], dscale1[1:2], dgate1[1:2], dshift1[0:1], dscale1[0:1], zrow,
        jnp.concatenate([sq[0:1], jnp.zeros((1, d - 128), F32)], axis=1),
        zrow], axis=0)
    small_all = allgather8(small_rows, "allgather_small")
    ev_sc = copies_start("reduce_ev_scatter_start",
                         [sb for _, sb in ev_sums] + [half_zone(a, 3, BF16) for a in ev_grads],
                         9, plan_scatter, small_all)
    od_sh = copies_start("reduce_od_share_start",
                         [owner_sum(od_sums[i][0], od_recv[2 + i], chip_core_arr) for i in range(2)],
                         2, plan_share, dxs0)
    tot = small_reduce(small_all + ev_sc[3][0:1, 0:1])
    loss = tot[22, 0] * (0.5 / d)

    dm_rows = []
    for layer in range(2):
        lat = ROW_MOD + 6 * layer
        dm_lat = small_all[:, lat:lat + 3].reshape(8, 3 * d)
        dm_ctx = tot[lat + 3:lat + 6].reshape(1, 3 * d)
        dm_rows.append(jnp.concatenate([dm_lat, dm_ctx, jnp.zeros((7, 3 * d), F32)], axis=0))
    dm_full = jnp.stack(dm_rows)
    dm_mine = lax.dynamic_slice(dm_full, (0, 0, chip * ws_ada), (2, 16, ws_ada))

    def step(w, gr, m, v, name, with_grad=False):
        shape = w.shape
        cols = shape[-1]
        two_d = lambda a: a.reshape(-1, cols)
        res = adamw(two_d(w), two_d(gr), two_d(m), two_d(v), "adamw_" + name, with_grad)
        return tuple(a.reshape(shape) for a in res)

    grad_ada_b = tot[24:30].reshape(2, 3 * d)
    grad_pre_g = tot[0:2]
    grad_post_g = tot[2:4]
    grad_ev_pool_scale = tot[4:5, :half_d]
    grad_ev_conv_b = tot[4:5, half_d:]
    conv_w_tot = tot[5:7].reshape(1, 2 * d)[:, :3 * half_d].reshape(3, N_POOL, g)
    grad_ev_conv_w = lax.dynamic_slice(conv_w_tot, (0, chip, 0), (3, 1, g)).reshape(1, 3, g)
    grad_od_onorm_g = lax.dynamic_slice(tot[7:8], (0, chip * 2 * g), (1, 2 * g))
    dlb_mine = lax.dynamic_slice(tot[8:10], (0, chip * 2 * g), (2, 2 * g))
    grad_lb_logits = lb_logits_grad(lb_logits, dlb_mine)
    upd = {
        "ada_b": step(ada_b, grad_ada_b, m_ada_b, v_ada_b, "ada_b"),
        "pre_g": step(pre_g, grad_pre_g, m_pre_g, v_pre_g, "pre_g"),
        "post_g": step(post_g, grad_post_g, m_post_g, v_post_g, "post_g"),
        "ev_pool_scale": step(ev_pool_scale, grad_ev_pool_scale, m_ev_pool_scale, v_ev_pool_scale, "ev_pool_scale"),
        "ev_conv_w": step(ev_conv_w, grad_ev_conv_w, m_ev_conv_w, v_ev_conv_w, "ev_conv_w"),
        "ev_conv_b": step(ev_conv_b, grad_ev_conv_b, m_ev_conv_b, v_ev_conv_b, "ev_conv_b"),
        "od_onorm_g": step(od_onorm_g, grad_od_onorm_g, m_od_onorm_g, v_od_onorm_g, "od_onorm_g"),
        "lb_logits": step(lb_logits, grad_lb_logits, m_lb_logits, v_lb_logits, "lb_logits"),
    }
    grad_ada_w, delta_ada_w, new_m_ada_w, new_v_ada_w, dctx_part = ada_bwd_adamw(
        s_act, dm_mine, ada_w, m_ada_w, v_ada_w)
    upd["ada_w"] = (delta_ada_w, new_m_ada_w, new_v_ada_w)
    (grad_od_w_in, grad_od_w_out), _ = copies_wait("reduce_od_share_wait", od_sh, plan_share, ev_sc[3])
    upd["od_w_in"] = step(od_w_in, grad_od_w_in[None], m_od_w_in, v_od_w_in, "od_w_in", True)
    upd["od_w_out"] = step(od_w_out, grad_od_w_out[None], m_od_w_out, v_od_w_out, "od_w_out", True)
    grad_od_w_in, grad_od_w_out = upd["od_w_in"][3], upd["od_w_out"][3]
    done_behind = [dctx_part] + [upd[k][0] for k in (
        "od_w_in", "od_w_out", "ada_b", "pre_g", "post_g", "ev_pool_scale", "ev_conv_w", "ev_conv_b",
        "od_onorm_g", "lb_logits")]
    ev_recv, ev_landed = copies_wait("reduce_ev_scatter_wait", ev_sc, plan_scatter, done_behind)
    grad_ev_w_in, grad_ev_w_out, grad_pool_w = share_halves(
        [owner_sum(ev_sums[i][0], ev_recv[3 + i], chip_core_arr) for i in range(3)])
    dctx_all = allgather8(dctx_part[0] + dctx_part[1] + ev_landed[0:1, 0:1], "allgather_dctx")
    grad_c_ctx = c_ctx_grad(dctx_all, c_ctx.reshape(1, d)).reshape(d)
    upd["c_ctx"] = step(c_ctx, grad_c_ctx, m_c_ctx, v_c_ctx, "c_ctx")
    upd["ev_w_in"] = step(ev_w_in, grad_ev_w_in[None], m_ev_w_in, v_ev_w_in, "ev_w_in", True)
    upd["ev_pool_w"] = step(ev_pool_w, grad_pool_w.reshape(1, N_POOL, g // 4, g), m_ev_pool_w, v_ev_pool_w,
                            "ev_pool_w", True)
    upd["ev_w_out"] = step(ev_w_out, grad_ev_w_out[None], m_ev_w_out, v_ev_w_out, "ev_w_out", True)
    grad_ev_w_in, grad_ev_pool_w, grad_ev_w_out = upd["ev_w_in"][3], upd["ev_pool_w"][3], upd["ev_w_out"][3]
    names = ["c_ctx", "ada_w", "ada_b", "pre_g", "post_g", "ev_w_in", "ev_pool_w", "ev_pool_scale",
             "ev_conv_w", "ev_conv_b", "ev_w_out", "od_w_in", "od_onorm_g", "od_w_out", "lb_logits"]
    grads = [grad_c_ctx, grad_ada_w, grad_ada_b, grad_pre_g, grad_post_g, grad_ev_w_in, grad_ev_pool_w,
             grad_ev_pool_scale, grad_ev_conv_w, grad_ev_conv_b, grad_ev_w_out, grad_od_w_in,
             grad_od_onorm_g, grad_od_w_out, grad_lb_logits]
    return (loss, grad_x, *grads, *[upd[k][0] for k in names], *[upd[k][1] for k in names],
            *[upd[k][2] for k in names])
```

```python
import jax
import jax.numpy as jnp
from jax import lax
from jax.experimental import pallas as pl
from jax.experimental.pallas import tpu as pltpu

EPS = 1e-6
GRID_W_LOG2 = 6
CHUNK = 64
HEAD = 128
N_POOL = 4
ADAM_LR, ADAM_B1, ADAM_B2, ADAM_EPS, ADAM_WD, ADAM_STEP = 0.001, 0.9, 0.999, 1e-08, 0.01, 10
VMEM_LIMIT = 56 * 1024 * 1024
MESH = pl.DeviceIdType.MESH
F32, BF16 = jnp.float32, jnp.bfloat16
ANY = pl.BlockSpec(memory_space=pl.ANY)
VMEM = pl.BlockSpec(memory_space=pltpu.VMEM)


def _cp(**kw):
    return pltpu.CompilerParams(vmem_limit_bytes=VMEM_LIMIT, **kw)


def _silu(x):
    return x * jax.nn.sigmoid(x)


def _dsilu(x):
    s = jax.nn.sigmoid(x)
    return s * (1.0 + x * (1.0 - s))


def _dot(a, b, dims=((1,), (0,)), precision=None):
    return lax.dot_general(a, b, (dims, ((), ())), preferred_element_type=F32, precision=precision)


NN = ((1,), (0,))
NT = ((1,), (1,))
TN = ((0,), (0,))


def _row_block(cx):
    return 256 if cx % 256 == 0 else 128


def normmod_fwd(xs, g, shift, scale, cx):
    t, d = xs.shape
    tm = _row_block(cx)
    nctx = cx // tm

    def body(x_ref, g_ref, sh_ref, sc_ref, h_ref):
        is_ctx = pl.program_id(0) < nctx
        x = x_ref[...]
        rstd = lax.rsqrt(jnp.mean(x * x, axis=-1, keepdims=True) + EPS)
        sc = jnp.where(is_ctx, sc_ref[0:1, :], sc_ref[1:2, :])
        sh = jnp.where(is_ctx, sh_ref[0:1, :], sh_ref[1:2, :])
        h_ref[...] = ((x * rstd) * g_ref[...] * (1.0 + sc) + sh).astype(BF16)

    row = pl.BlockSpec((tm, d), lambda i: (i, 0))
    vec = lambda r: pl.BlockSpec((r, d), lambda i: (0, 0))
    return pl.pallas_call(
        body, name="normmod_fwd", grid=(t // tm,),
        in_specs=[row, vec(1), vec(2), vec(2)], out_specs=row,
        out_shape=jax.ShapeDtypeStruct((t, d), BF16), compiler_params=_cp(),
    )(xs, g, shift, scale)


def normmod_fwd_joining(ctx, x, g, shift, scale):
    cx, d = ctx.shape
    t = cx + x.shape[0]
    tm = _row_block(cx)
    nctx = cx // tm

    def body(c_ref, x_ref, g_ref, sh_ref, sc_ref, h_ref, xs_ref):
        is_ctx = pl.program_id(0) < nctx
        x = jnp.where(is_ctx, c_ref[...], x_ref[...])
        xs_ref[...] = x
        rstd = lax.rsqrt(jnp.mean(x * x, axis=-1, keepdims=True) + EPS)
        sc = jnp.where(is_ctx, sc_ref[0:1, :], sc_ref[1:2, :])
        sh = jnp.where(is_ctx, sh_ref[0:1, :], sh_ref[1:2, :])
        h_ref[...] = ((x * rstd) * g_ref[...] * (1.0 + sc) + sh).astype(BF16)

    row = pl.BlockSpec((tm, d), lambda i: (i, 0))
    vec = lambda r: pl.BlockSpec((r, d), lambda i: (0, 0))
    return pl.pallas_call(
        body, name="normmod_fwd_joining", grid=(t // tm,),
        in_specs=[pl.BlockSpec((tm, d), lambda i: (jnp.minimum(i, nctx - 1), 0)),
                  pl.BlockSpec((tm, d), lambda i: (jnp.maximum(i - nctx, 0), 0)), vec(1), vec(2), vec(2)],
        out_specs=[row, row],
        out_shape=[jax.ShapeDtypeStruct((t, d), BF16), jax.ShapeDtypeStruct((t, d), F32)],
        compiler_params=_cp(),
    )(ctx, x, g, shift, scale)


def normmod_bwd(xs, dh, g, scale, dres, cx, res_is_latent_only, dx_latent_only=False):
    t, d = xs.shape
    tm = _row_block(cx)
    nctx = cx // tm

    def body(x_ref, dh_ref, g_ref, sc_ref, dres_ref, dx_ref, dg_ref, dsh_ref, dsc_ref):
        i = pl.program_id(0)
        is_ctx = i < nctx

        @pl.when(i == 0)
        def _():
            dg_ref[...] = jnp.zeros_like(dg_ref)
            dsh_ref[...] = jnp.zeros_like(dsh_ref)
            dsc_ref[...] = jnp.zeros_like(dsc_ref)

        x = x_ref[...]
        dh = dh_ref[...]
        gv = g_ref[...]
        rstd = lax.rsqrt(jnp.mean(x * x, axis=-1, keepdims=True) + EPS)
        xhat = x * rstd
        sc = jnp.where(is_ctx, sc_ref[0:1, :], sc_ref[1:2, :])
        dsh = jnp.sum(dh, axis=0, keepdims=True)
        dhx = dh * xhat
        dsc = jnp.sum(dhx * gv, axis=0, keepdims=True)
        dg_ref[...] += jnp.sum(dhx * (1.0 + sc), axis=0, keepdims=True)
        zero = jnp.zeros_like(dsh)
        dsh_ref[0:1, :] += jnp.where(is_ctx, dsh, zero)
        dsh_ref[1:2, :] += jnp.where(is_ctx, zero, dsh)
        dsc_ref[0:1, :] += jnp.where(is_ctx, dsc, zero)
        dsc_ref[1:2, :] += jnp.where(is_ctx, zero, dsc)
        dxhat = dh * (gv * (1.0 + sc))
        dx = rstd * (dxhat - xhat * jnp.mean(dxhat * xhat, axis=-1, keepdims=True))
        res = dres_ref[...]
        if res_is_latent_only:
            res = jnp.where(is_ctx, jnp.zeros_like(res), res)
        dx_ref[...] = dx + res

    row = pl.BlockSpec((tm, d), lambda i: (i, 0))
    if res_is_latent_only:
        res_spec = pl.BlockSpec((tm, d), lambda i: (jnp.maximum(i - nctx, 0), 0))
    else:
        res_spec = row
    vec = lambda r: pl.BlockSpec((r, d), lambda i: (0, 0))
    dx_spec = pl.BlockSpec((tm, d), lambda i: (jnp.maximum(i - nctx, 0), 0)) if dx_latent_only else row
    return pl.pallas_call(
        body, name="normmod_bwd", grid=(t // tm,),
        in_specs=[row, row, vec(1), vec(2), res_spec],
        out_specs=[dx_spec, vec(1), vec(2), vec(2)],
        out_shape=[jax.ShapeDtypeStruct((t - cx if dx_latent_only else t, d), F32), jax.ShapeDtypeStruct((1, d), F32),
                   jax.ShapeDtypeStruct((2, d), F32), jax.ShapeDtypeStruct((2, d), F32)],
        compiler_params=_cp(),
    )(xs, dh, g, scale, dres)


def post_fwd(xs, y, pg, gate, cx):
    t, d = xs.shape
    tm = _row_block(cx)
    nctx = cx // tm

    def body(x_ref, y_ref, pg_ref, gate_ref, o_ref):
        is_ctx = pl.program_id(0) < nctx
        y = y_ref[...]
        rstd = lax.rsqrt(jnp.mean(y * y, axis=-1, keepdims=True) + EPS)
        gt = jnp.where(is_ctx, gate_ref[0:1, :], gate_ref[1:2, :])
        o_ref[...] = x_ref[...] + gt * ((y * rstd) * pg_ref[...])

    row = pl.BlockSpec((tm, d), lambda i: (i, 0))
    vec = lambda r: pl.BlockSpec((r, d), lambda i: (0, 0))
    return pl.pallas_call(
        body, name="post_fwd", grid=(t // tm,),
        in_specs=[row, row, vec(1), vec(2)], out_specs=row,
        out_shape=jax.ShapeDtypeStruct((t, d), F32), compiler_params=_cp(),
    )(xs, y, pg, gate)


def post_loss(xs, y, pg, gate, target, cx):
    t, d = xs.shape
    n = y.shape[0]
    tm = _row_block(cx)
    nctx = cx // tm

    def body(x_ref, y_ref, pg_ref, gate_ref, tgt_ref, sq_ref, dx_ref):
        @pl.when(pl.program_id(0) == 0)
        def _():
            sq_ref[...] = jnp.zeros_like(sq_ref)

        y = y_ref[...]
        rstd = lax.rsqrt(jnp.mean(y * y, axis=-1, keepdims=True) + EPS)
        x2 = x_ref[...] + gate_ref[1:2, :] * ((y * rstd) * pg_ref[...])
        err = x2 - tgt_ref[...]
        sq_ref[...] += jnp.sum(err * err)
        dx_ref[...] = err * (1.0 / d)

    row = pl.BlockSpec((tm, d), lambda i: (i, 0))
    xrow = pl.BlockSpec((tm, d), lambda i: (i + nctx, 0))
    vec = lambda r: pl.BlockSpec((r, d), lambda i: (0, 0))
    return pl.pallas_call(
        body, name="post_loss", grid=(n // tm,),
        in_specs=[xrow, row, vec(1), vec(2), row],
        out_specs=[pl.BlockSpec((8, 128), lambda i: (0, 0)), row],
        out_shape=[jax.ShapeDtypeStruct((8, 128), F32), jax.ShapeDtypeStruct((n, d), F32)],
        compiler_params=_cp(),
    )(xs, y, pg, gate, target)


def post_bwd(dxo, y, pg, gate, cx, latent_only):
    m, d = y.shape
    tm = _row_block(cx)
    nctx = 0 if latent_only else cx // tm

    def body(dx_ref, y_ref, pg_ref, gate_ref, dy_ref, dgate_ref, dpg_ref):
        i = pl.program_id(0)
        is_ctx = i < nctx

        @pl.when(i == 0)
        def _():
            dgate_ref[...] = jnp.zeros_like(dgate_ref)
            dpg_ref[...] = jnp.zeros_like(dpg_ref)

        y = y_ref[...]
        dx = dx_ref[...]
        pgv = pg_ref[...]
        rstd = lax.rsqrt(jnp.mean(y * y, axis=-1, keepdims=True) + EPS)
        yhat = y * rstd
        gt = jnp.where(is_ctx, gate_ref[0:1, :], gate_ref[1:2, :])
        dxy = dx * yhat
        dgt = jnp.sum(dxy * pgv, axis=0, keepdims=True)
        zero = jnp.zeros_like(dgt)
        dgate_ref[0:1, :] += jnp.where(is_ctx, dgt, zero)
        dgate_ref[1:2, :] += jnp.where(is_ctx, zero, dgt)
        dpg_ref[...] += jnp.sum(dxy * gt, axis=0, keepdims=True)
        dyhat = dx * (gt * pgv)
        dy = rstd * (dyhat - yhat * jnp.mean(dyhat * yhat, axis=-1, keepdims=True))
        dy_ref[...] = dy.astype(BF16)

    row = pl.BlockSpec((tm, d), lambda i: (i, 0))
    vec = lambda r: pl.BlockSpec((r, d), lambda i: (0, 0))
    return pl.pallas_call(
        body, name="post_bwd", grid=(m // tm,),
        in_specs=[row, row, vec(1), vec(2)], out_specs=[row, vec(2), vec(1)],
        out_shape=[jax.ShapeDtypeStruct((m, d), BF16), jax.ShapeDtypeStruct((2, d), F32),
                   jax.ShapeDtypeStruct((1, d), F32)],
        compiler_params=_cp(),
    )(dxo, y, pg, gate)


def _split_rows(m):
    for cand in (1024, 768, 512, 384, 256, 128):
        if m % cand == 0 and m // cand >= 2:
            return cand
    return m


def mm_nn(a, w3, sec, tn, name):
    m, k = a.shape
    q, _, ws = w3.shape
    n = q * ws
    tpq, tps = ws // tn, sec // tn
    tm = next(c for c in (768, 512, 256, 128) if m % c == 0)

    def body(a_ref, w_ref, o_ref):
        w = w_ref[...]

        def step(i, carry):
            rows = pl.ds(pl.multiple_of(i * tm, tm), tm)
            o_ref[rows, :] = _dot(a_ref[rows, :], w)
            return carry

        lax.fori_loop(0, m // tm, step, 0)

    return pl.pallas_call(
        body, name=name, grid=(n // tn,),
        in_specs=[pl.BlockSpec((m, k), lambda j: (0, 0)),
                  pl.BlockSpec((None, k, tn), lambda j: (j // tpq, 0, j % tpq))],
        out_specs=pl.BlockSpec((None, m, tn), lambda j: (j // tps, 0, j % tps)),
        out_shape=jax.ShapeDtypeStruct((n // sec, m, sec), F32), compiler_params=_cp(),
    )(a, w3)


def _two_stacks(a3, b3, tn):
    sec = a3.shape[2]
    tps = sec // tn
    n1 = a3.shape[0] * tps
    first = lambda j: (jnp.minimum(j, n1 - 1) // tps, jnp.minimum(j, n1 - 1) % tps)
    second = lambda j: (jnp.maximum(j - n1, 0) // tps, jnp.maximum(j - n1, 0) % tps)
    return n1, first, second


def mm_nt(a3, b3, w3, tn, name):
    if b3 is None:
        b3 = a3
    _, m, sec = a3.shape
    q, k, ws = w3.shape
    n = q * ws
    tpq = ws // tn
    mb = _split_rows(m)
    n1, first, second = _two_stacks(a3, b3, tn)

    def body(a_ref, b_ref, w_ref, o_ref):
        j = pl.program_id(1)

        @pl.when(j == 0)
        def _():
            o_ref[...] = jnp.zeros_like(o_ref)

        @pl.when(j < n1)
        def _():
            o_ref[...] += _dot(a_ref[...], w_ref[...], NT)

        @pl.when(j >= n1)
        def _():
            o_ref[...] += _dot(b_ref[...], w_ref[...], NT)

    return pl.pallas_call(
        body, name=name, grid=(m // mb, n // tn),
        in_specs=[pl.BlockSpec((None, mb, tn), lambda i, j: (first(j)[0], i, first(j)[1])),
                  pl.BlockSpec((None, mb, tn), lambda i, j: (second(j)[0], i, second(j)[1])),
                  pl.BlockSpec((None, k, tn), lambda i, j: (j // tpq, 0, j % tpq))],
        out_specs=pl.BlockSpec((mb, k), lambda i, j: (i, 0)),
        out_shape=jax.ShapeDtypeStruct((m, k), F32), compiler_params=_cp(),
    )(a3, b3, w3)


def mm_tn(a, b3, c3, ws, tn, name):
    m, k = a.shape
    sec = b3.shape[2]
    n = (b3.shape[0] + (0 if c3 is None else c3.shape[0])) * sec
    if c3 is None:
        c3 = b3
    tpq = ws // tn
    kb = 256 if k % 256 == 0 else 128
    n1, first, second = _two_stacks(b3, c3, tn)

    def body(a_ref, b_ref, c_ref, o_ref):
        def product(rhs_ref):
            rhs = rhs_ref[...]
            for i in range(k // kb):
                o_ref[i * kb:(i + 1) * kb, :] = _dot(a_ref[:, i * kb:(i + 1) * kb], rhs, TN).astype(BF16)

        @pl.when(pl.program_id(0) < n1)
        def _():
            product(b_ref)

        @pl.when(pl.program_id(0) >= n1)
        def _():
            product(c_ref)

    return pl.pallas_call(
        body, name=name, grid=(n // tn,),
        in_specs=[pl.BlockSpec((m, k), lambda j: (0, 0)),
                  pl.BlockSpec((None, m, tn), lambda j: (first(j)[0], 0, first(j)[1])),
                  pl.BlockSpec((None, m, tn), lambda j: (second(j)[0], 0, second(j)[1]))],
        out_specs=pl.BlockSpec((None, k, tn), lambda j: (j // tpq, 0, j % tpq)),
        out_shape=jax.ShapeDtypeStruct((n // ws, k, ws), BF16), compiler_params=_cp(),
    )(a, b3, c3)


POOL_REACH = 8 << GRID_W_LOG2


def _token_parts(tok, cx):
    lat = tok - cx
    return tok < cx, lat >> GRID_W_LOG2, lat & ((1 << GRID_W_LOG2) - 1)


def _pool_mask(gi, row0, col0, tm, ncols, cx, transposed):
    half = jnp.left_shift(1, gi)
    r = lax.broadcasted_iota(jnp.int32, (tm, 1), 0) + row0
    c = lax.broadcasted_iota(jnp.int32, (1, ncols), 1) + col0
    out_tok, src_tok = (c, r) if transposed else (r, c)
    o_ctx, o_row, o_col = _token_parts(out_tok, cx)
    s_ctx, s_row, s_col = _token_parts(src_tok, cx)

    def inside(o, s):
        return (s >= o - half) & (s <= o + half - 1)

    ctx_hit = o_ctx & s_ctx & inside(out_tok, src_tok)
    lat_hit = (~o_ctx) & (~s_ctx) & inside(o_row, s_row) & inside(o_col, s_col)
    return jnp.where(ctx_hit | lat_hit, 1.0, 0.0).astype(BF16)


def _pool_inv_count(gi, row0, tm, cx, seq):
    half = jnp.left_shift(1, gi)
    r = lax.broadcasted_iota(jnp.int32, (tm, 1), 0) + row0
    is_ctx, row, col = _token_parts(r, cx)

    def count(pos, size):
        return jnp.minimum(pos + half - 1, size - 1) - jnp.maximum(pos - half, 0) + 1

    cnt = jnp.where(is_ctx, count(r, cx), count(row, seq >> GRID_W_LOG2) * count(col, 1 << GRID_W_LOG2))
    return 1.0 / cnt.astype(F32)


def _lat_band(tm):
    side = POOL_REACH // tm
    return side, 2 * side + 1


def _lat_mask(gi, tm, cx, transposed):
    side, band = _lat_band(tm)
    return _pool_mask(gi, cx + side * tm, cx, tm, band * tm, cx, transposed)


def _store_padded_lat(dst_ref, lat, tm):
    side, _ = _lat_band(tm)
    seq = lat.shape[0]
    zeros = jnp.zeros((side * tm, lat.shape[1]), dst_ref.dtype)
    dst_ref[0:side * tm, :] = zeros
    dst_ref[side * tm + seq:, :] = zeros
    dst_ref[side * tm:side * tm + seq, :] = lat.astype(dst_ref.dtype)


def mix_a_fwd(z0, pool_w, pool_scale, cx):
    _, t, half_d = z0.shape
    g = half_d // N_POOL
    seq = t - cx
    tm = _row_block(cx)
    side, band = _lat_band(tm)

    def body(v_ref, ag_ref, w_ref, sc_ref, u_ref, vlat_ref, mask_ref):
        gi = pl.program_id(0)
        w = w_ref[...].astype(BF16)
        sc = sc_ref[...]
        _store_padded_lat(vlat_ref, v_ref[cx:, :], tm)
        mask_ref[...] = _lat_mask(gi, tm, cx, False)

        def finish(row0, window_sum):
            rows = pl.ds(row0, tm)
            pooled = window_sum * _pool_inv_count(gi, row0, tm, cx, seq) - v_ref[rows, :]
            mixed = _dot(pooled.astype(BF16), w) * sc
            u_ref[rows, :] = (mixed * _silu(ag_ref[rows, :])).astype(BF16)

        vctx = v_ref[0:cx, :].astype(BF16)
        for i in range(cx // tm):
            finish(i * tm, _dot(_pool_mask(gi, i * tm, 0, tm, cx, cx, False), vctx))

        def step(j, carry):
            src = vlat_ref[pl.ds(pl.multiple_of(j * tm, tm), band * tm), :]
            finish(pl.multiple_of(cx + j * tm, tm), _dot(mask_ref[...], src))
            return carry

        lax.fori_loop(0, seq // tm, step, 0)

    sec = lambda s: pl.BlockSpec((None, t, g), lambda j: (s, 0, j))
    return pl.pallas_call(
        body, name="mix_a_fwd", grid=(N_POOL,),
        in_specs=[sec(0), sec(1), pl.BlockSpec((None, g, g), lambda j: (j, 0, 0)),
                  pl.BlockSpec((1, g), lambda j: (0, j))],
        out_specs=pl.BlockSpec((t, g), lambda j: (0, j)),
        out_shape=jax.ShapeDtypeStruct((t, 2 * half_d), BF16),
        scratch_shapes=[pltpu.VMEM((seq + 2 * side * tm, g), BF16), pltpu.VMEM((tm, band * tm), BF16)],
        compiler_params=_cp(),
    )(z0, z0, pool_w, pool_scale)


def mix_a_bwd(z0, du, pool_w, pool_scale, cx):
    _, t, half_d = z0.shape
    g = half_d // N_POOL
    seq = t - cx
    tm = _row_block(cx)
    gq = g // 4
    side, band = _lat_band(tm)

    def body(v_ref, ag_ref, du_ref, w_ref, sc_ref, dz_ref, dw_ref, dsc_ref,
             vlat_ref, mask_ref, pooled_ref, dmx_ref, dpl_ref, wlat_ref, wctx_ref):
        gi = pl.program_id(0)
        w = w_ref[...].astype(BF16)
        sc = sc_ref[...]
        _store_padded_lat(vlat_ref, v_ref[cx:, :], tm)
        _store_padded_lat(wlat_ref, jnp.zeros((seq, g), BF16), tm)
        mask_ref[...] = _lat_mask(gi, tm, cx, False)

        def first(row0, window_sum, weighted_ref, weighted_row0):
            rows = pl.ds(row0, tm)
            inv = _pool_inv_count(gi, row0, tm, cx, seq)
            pooled = (window_sum * inv - v_ref[rows, :]).astype(BF16)
            pooled_ref[rows, :] = pooled
            mixed = _dot(pooled, w)
            ag = ag_ref[rows, :]
            duv = du_ref[rows, :]
            dz_ref[1, rows, :] = (duv * (mixed * sc) * _dsilu(ag)).astype(BF16)
            dms = duv * _silu(ag)
            dmixed = (dms * sc).astype(BF16)
            dmx_ref[rows, :] = dmixed
            dpooled = _dot(dmixed, w, NT)
            dpl_ref[rows, :] = dpooled
            weighted_ref[pl.ds(weighted_row0, tm), :] = (dpooled * inv).astype(BF16)
            return jnp.sum(dms * mixed, axis=0, keepdims=True)

        dsc = jnp.zeros((1, g), F32)
        vctx = v_ref[0:cx, :].astype(BF16)
        for i in range(cx // tm):
            dsc += first(i * tm, _dot(_pool_mask(gi, i * tm, 0, tm, cx, cx, False), vctx), wctx_ref, i * tm)

        def first_lat(j, acc):
            src = vlat_ref[pl.ds(pl.multiple_of(j * tm, tm), band * tm), :]
            return acc + first(pl.multiple_of(cx + j * tm, tm), _dot(mask_ref[...], src),
                               wlat_ref, pl.multiple_of((side + j) * tm, tm))

        dsc_ref[...] = lax.fori_loop(0, seq // tm, first_lat, dsc)
        dw = _dot(pooled_ref[...], dmx_ref[...], TN)
        for qi in range(4):
            dw_ref[qi] = dw[qi * gq:(qi + 1) * gq, :]

        wctx = wctx_ref[...]
        for i in range(cx // tm):
            rows = pl.ds(i * tm, tm)
            dz_ref[0, rows, :] = (_dot(_pool_mask(gi, i * tm, 0, tm, cx, cx, True), wctx)
                                  - dpl_ref[rows, :]).astype(BF16)
        mask_ref[...] = _lat_mask(gi, tm, cx, True)

        def second_lat(j, carry):
            rows = pl.ds(pl.multiple_of(cx + j * tm, tm), tm)
            src = wlat_ref[pl.ds(pl.multiple_of(j * tm, tm), band * tm), :]
            dz_ref[0, rows, :] = (_dot(mask_ref[...], src) - dpl_ref[rows, :]).astype(BF16)
            return carry

        lax.fori_loop(0, seq // tm, second_lat, 0)

    sec = lambda s: pl.BlockSpec((None, t, g), lambda j: (s, 0, j))
    padded = pltpu.VMEM((seq + 2 * side * tm, g), BF16)
    return pl.pallas_call(
        body, name="mix_a_bwd", grid=(N_POOL,),
        in_specs=[sec(0), sec(1), pl.BlockSpec((t, g), lambda j: (0, j)),
                  pl.BlockSpec((None, g, g), lambda j: (j, 0, 0)),
                  pl.BlockSpec((1, g), lambda j: (0, j))],
        out_specs=[pl.BlockSpec((2, t, g), lambda j: (0, 0, j)),
                   pl.BlockSpec((4, None, gq, g), lambda j: (0, j, 0, 0)),
                   pl.BlockSpec((1, g), lambda j: (0, j))],
        out_shape=[jax.ShapeDtypeStruct((2, t, half_d), BF16),
                   jax.ShapeDtypeStruct((4, N_POOL, gq, g), F32),
                   jax.ShapeDtypeStruct((1, half_d), F32)],
        scratch_shapes=[padded, pltpu.VMEM((tm, band * tm), BF16), pltpu.VMEM((t, g), BF16),
                        pltpu.VMEM((t, g), BF16), pltpu.VMEM((t, g), F32), padded, pltpu.VMEM((cx, g), BF16)],
        compiler_params=_cp(),
    )(z0, z0, du, pool_w, pool_scale)


def _conv_masks(t, cx):
    r = lax.broadcasted_iota(jnp.int32, (t, 1), 0)
    has_prev = jnp.where((r == 0) | (r == cx), 0.0, 1.0)
    has_next = jnp.where((r == cx - 1) | (r == t - 1), 0.0, 1.0)
    return has_prev, has_next


def mix_b_fwd(z0, conv_w, conv_b, u, cx):
    _, t, half_d = z0.shape
    gb = 128
    off = half_d // gb

    def body(bx_ref, bb_ref, bc_ref, bg_ref, w_ref, b_ref, _, u_ref):
        has_prev, has_next = _conv_masks(t, cx)
        tt = bc_ref[...] * bx_ref[...]
        prev = pltpu.roll(tt, 1, 0) * has_prev
        nxt = pltpu.roll(tt, t - 1, 0) * has_next
        cv = prev * w_ref[0:1, :] + tt * w_ref[1:2, :] + nxt * w_ref[2:3, :] + b_ref[...]
        u_ref[...] = (bb_ref[...] * cv * _silu(bg_ref[...])).astype(BF16)

    sec = lambda s: pl.BlockSpec((None, t, gb), lambda j: (s, 0, j))
    return pl.pallas_call(
        body, name="mix_b_fwd", grid=(half_d // gb,),
        in_specs=[sec(2), sec(3), sec(4), sec(5), pl.BlockSpec((3, gb), lambda j: (0, j)),
                  pl.BlockSpec((1, gb), lambda j: (0, j)), ANY],
        out_specs=pl.BlockSpec((t, gb), lambda j: (0, j + off)),
        out_shape=jax.ShapeDtypeStruct((t, 2 * half_d), BF16), input_output_aliases={6: 0},
        compiler_params=_cp(),
    )(z0, z0, z0, z0, conv_w, conv_b, u)


def mix_b_bwd(z0, du, conv_w, conv_b, cx):
    _, t, half_d = z0.shape
    gb = 128
    off = half_d // gb

    def body(bx_ref, bb_ref, bc_ref, bg_ref, du_ref, w_ref, b_ref, dz_ref, dw_ref, db_ref):
        has_prev, has_next = _conv_masks(t, cx)
        bx, bb, bc, bg = bx_ref[...], bb_ref[...], bc_ref[...], bg_ref[...]
        duv = du_ref[...]
        tt = bc * bx
        prev = pltpu.roll(tt, 1, 0) * has_prev
        nxt = pltpu.roll(tt, t - 1, 0) * has_next
        w0, w1, w2 = w_ref[0:1, :], w_ref[1:2, :], w_ref[2:3, :]
        cv = prev * w0 + tt * w1 + nxt * w2 + b_ref[...]
        sg = _silu(bg)
        dz_ref[1] = (duv * cv * sg).astype(BF16)
        dz_ref[3] = (duv * bb * cv * _dsilu(bg)).astype(BF16)
        dcv = duv * bb * sg
        dw_ref[0:1, :] = jnp.sum(dcv * prev, axis=0, keepdims=True)
        dw_ref[1:2, :] = jnp.sum(dcv * tt, axis=0, keepdims=True)
        dw_ref[2:3, :] = jnp.sum(dcv * nxt, axis=0, keepdims=True)
        db_ref[...] = jnp.sum(dcv, axis=0, keepdims=True)
        dt = (pltpu.roll(dcv * has_prev, t - 1, 0) * w0 + dcv * w1
              + pltpu.roll(dcv * has_next, 1, 0) * w2)
        dz_ref[0] = (dt * bc).astype(BF16)
        dz_ref[2] = (dt * bx).astype(BF16)

    sec = lambda s: pl.BlockSpec((None, t, gb), lambda j: (s, 0, j))
    return pl.pallas_call(
        body, name="mix_b_bwd", grid=(half_d // gb,),
        in_specs=[sec(2), sec(3), sec(4), sec(5), pl.BlockSpec((t, gb), lambda j: (0, j + off)),
                  pl.BlockSpec((3, gb), lambda j: (0, j)), pl.BlockSpec((1, gb), lambda j: (0, j))],
        out_specs=[pl.BlockSpec((4, t, gb), lambda j: (0, 0, j)),
                   pl.BlockSpec((3, gb), lambda j: (0, j)), pl.BlockSpec((1, gb), lambda j: (0, j))],
        out_shape=[jax.ShapeDtypeStruct((4, t, half_d), BF16),
                   jax.ShapeDtypeStruct((3, half_d), F32), jax.ShapeDtypeStruct((1, half_d), F32)],
        compiler_params=_cp(),
    )(z0, z0, z0, z0, du, conv_w, conv_b)


def _lower_bound(lbl_ref, d):
    l0, l1, l2 = lbl_ref[d, 0:1, :], lbl_ref[d, 1:2, :], lbl_ref[d, 2:3, :]
    mx = jnp.maximum(jnp.maximum(l0, l1), l2)
    e0, e1, e2 = jnp.exp(l0 - mx), jnp.exp(l1 - mx), jnp.exp(l2 - mx)
    inv = 1.0 / (e0 + e1 + e2)
    return (e0 + e1) * inv, (e0 * inv, e1 * inv, e2 * inv)


def _chunk_consts(d):
    r = lax.broadcasted_iota(jnp.int32, (CHUNK, CHUNK), 0)
    c = lax.broadcasted_iota(jnp.int32, (CHUNK, CHUNK), 1)
    keep = (c <= r) if d == 0 else (c >= r)
    return jnp.where(keep, 1.0, 0.0).astype(F32), keep


def _chunk_of_step(s, d, nc, ncc):
    if d == 0:
        return s
    return jnp.where(s < ncc, ncc - 1 - s, nc - 1 + ncc - s)


def _gates(z, lbv):
    e = jnp.exp(-jnp.abs(z))
    r = 1.0 / (1.0 + e)
    er = e * r
    pos = z >= 0.0
    sig = jnp.where(pos, r, er)
    nsig = jnp.where(pos, er, r)
    return sig, nsig, lbv + (1.0 - lbv) * sig


def _split3(x):
    hi = x.astype(BF16)
    r1 = x - hi.astype(F32)
    mid = r1.astype(BF16)
    lo = (r1 - mid.astype(F32)).astype(BF16)
    return jnp.concatenate([hi, mid, lo], axis=1)


def _cumsum_chunk(cum, x):
    y = _dot(cum, _split3(x))
    return y[:, :HEAD] + y[:, HEAD:2 * HEAD] + y[:, 2 * HEAD:]


def _chunk_rows(n):
    return pl.ds(pl.multiple_of(n * CHUNK, CHUNK), CHUNK)


def _group(nc, prefer=(4, 3, 2, 1)):
    return next(u for u in prefer if nc % u == 0)


WIDE_GROUP = (12, 6, 4, 3, 2, 1)


def _decay_pass(lf_ref, bc_ref, dec_ref, cum, nc):
    grp = _group(nc, WIDE_GROUP)

    def step(m, carry):
        ns = [m * grp + u for u in range(grp)]
        lfc = [lf_ref[_chunk_rows(n), :] for n in ns]
        bc = [_cumsum_chunk(cum, x) for x in lfc]
        for u, n in enumerate(ns):
            bc_ref[_chunk_rows(n), :] = bc[u]
            dec_ref[n] = jnp.broadcast_to(jnp.exp(jnp.sum(lfc[u], axis=0, keepdims=True)), (8, HEAD))
        return carry

    lax.fori_loop(0, nc // grp, step, 0)


def hgrn_fwd(z1, lbl, onorm, cx):
    _, t, d = z1.shape
    seq = t - cx
    nc, ncc = t // CHUNK, cx // CHUNK

    grp, sgrp = _group(nc, (9, 6, 4, 3, 2, 1)), _group(nc, WIDE_GROUP)

    def body(zf_ref, zb_ref, v_ref, q_ref, g_ref, lbl_ref, on_ref, o_ref, r_ref, bcs_ref, ks_ref, decs_ref,
             lf_ref, k_ref, bc_ref, dec_ref, qd_ref, ki_ref, oacc_ref, ds_ref):
        for dr, z_ref in ((0, zf_ref), (1, zb_ref)):
            lbv, _ = _lower_bound(lbl_ref, dr)
            _, nsig, f = _gates(z_ref[...], lbv)
            lf_ref[...] = jnp.log(f)
            k_ref[...] = (1.0 - lbv) * nsig
            cum, keep = _chunk_consts(dr)
            _decay_pass(lf_ref, bc_ref, dec_ref, cum.astype(BF16), nc)
            bc = bc_ref[...]
            bcs_ref[dr] = bc
            ks_ref[dr] = k_ref[...]
            decs_ref[dr] = dec_ref[...]
            qd_ref[...] = (q_ref[...] * jnp.exp(bc)).astype(BF16)
            ki_ref[...] = (k_ref[...] * jnp.exp(-bc)).astype(BF16)

            def local_step(m, carry, dr=dr, keep=keep):
                ns = [m * grp + u for u in range(grp)]
                rows = [_chunk_rows(n) for n in ns]
                qd = [qd_ref[r, :] for r in rows]
                ki = [ki_ref[r, :] for r in rows]
                vc = [v_ref[r, :].astype(BF16) for r in rows]
                sc = [_dot(qd[u], ki[u], NT) for u in range(grp)]
                inc = [_dot(vc[u], ki[u], TN) for u in range(grp)]
                a = [jnp.where(keep, s, 0.0).astype(BF16) for s in sc]
                intra = [_dot(a[u], vc[u]) for u in range(grp)]
                for u in range(grp):
                    ds_ref[ns[u]] = inc[u] * dec_ref[ns[u]][0:1, :]
                    if dr == 0:
                        oacc_ref[rows[u], :] = intra[u]
                    else:
                        oacc_ref[rows[u], :] += intra[u]
                return carry

            lax.fori_loop(0, nc // grp, local_step, 0)

            def state_step(m, st, dr=dr):
                ns = [_chunk_of_step(m * sgrp + u, dr, nc, ncc) for u in range(sgrp)]
                rows = [_chunk_rows(n) for n in ns]
                sts = []
                for n in ns:
                    sts.append(st.astype(BF16))
                    st = st * dec_ref[n][0:1, :] + ds_ref[n]
                inter = [_dot(qd_ref[rows[u], :], sts[u], NT) for u in range(sgrp)]
                for u in range(sgrp):
                    oacc_ref[rows[u], :] += inter[u]
                return st

            lax.fori_loop(0, nc // sgrp, state_step, jnp.zeros((HEAD, HEAD), F32))

        o = oacc_ref[cx:, :]
        o_ref[...] = o
        rstd = lax.rsqrt(jnp.mean(o * o, axis=-1, keepdims=True) + EPS)
        r_ref[...] = (o * rstd * on_ref[...] * _silu(g_ref[cx:, :])).astype(BF16)

    sec = lambda s: pl.BlockSpec((None, t, HEAD), lambda h: (s, 0, h))
    col = pl.BlockSpec((seq, HEAD), lambda h: (0, h))
    tf32, tb16 = pltpu.VMEM((t, HEAD), F32), pltpu.VMEM((t, HEAD), BF16)
    return pl.pallas_call(
        body, name="hgrn_fwd", grid=(d // HEAD,),
        in_specs=[sec(0), sec(1), sec(2), sec(3), sec(4),
                  pl.BlockSpec((2, 3, HEAD), lambda h: (0, 0, h)), pl.BlockSpec((1, HEAD), lambda h: (0, h))],
        out_specs=[col, col, pl.BlockSpec((2, t, HEAD), lambda h: (0, 0, h)),
                   pl.BlockSpec((2, t, HEAD), lambda h: (0, 0, h)),
                   pl.BlockSpec((2, nc, 8, HEAD), lambda h: (0, 0, 0, h))],
        out_shape=[jax.ShapeDtypeStruct((seq, d), F32), jax.ShapeDtypeStruct((seq, d), BF16),
                   jax.ShapeDtypeStruct((2, t, d), F32), jax.ShapeDtypeStruct((2, t, d), F32),
                   jax.ShapeDtypeStruct((2, nc, 8, d), F32)],
        scratch_shapes=[tf32, tf32, tf32, pltpu.VMEM((nc, 8, HEAD), F32), tb16, tb16, tf32,
                        pltpu.VMEM((nc, HEAD, HEAD), F32)],
        compiler_params=_cp(),
    )(z1, z1, z1, z1, z1, lbl, onorm)


def hgrn_bwd(z1, lbl, onorm, o, dr_out, bcs, ks, decs, cx):
    _, t, d = z1.shape
    seq = t - cx
    nc, ncc = t // CHUNK, cx // CHUNK

    grp2, grp = _group(nc, (6, 4, 3, 2, 1)), _group(nc, (9, 6, 4, 3, 2, 1))

    def body(zf_ref, zb_ref, v_ref, q_ref, g_ref, lbl_ref, on_ref, o_ref, dr_ref, bcs_ref, ks_ref, decs_ref,
             dz_ref, don_ref, dlb_ref,
             qd_ref, ki_ref, do_ref, dqd_ref, dki_ref, dq_ref, dv_ref, ds_ref, dsl_ref):
        o = o_ref[...]
        g = g_ref[cx:, :]
        drv = dr_ref[...]
        onv = on_ref[...]
        rstd = lax.rsqrt(jnp.mean(o * o, axis=-1, keepdims=True) + EPS)
        ohat = o * rstd
        sg = _silu(g)
        don_ref[...] = jnp.sum(drv * ohat * sg, axis=0, keepdims=True)
        dz_ref[4, :cx, :] = jnp.zeros((cx, HEAD), BF16)
        dz_ref[4, cx:, :] = (drv * ohat * onv * _dsilu(g)).astype(BF16)
        dohat = drv * onv * sg
        do_ref[:cx, :] = jnp.zeros((cx, HEAD), BF16)
        do_ref[cx:, :] = (rstd * (dohat - ohat * jnp.mean(dohat * ohat, axis=-1, keepdims=True))).astype(BF16)

        for dr, z_ref in ((0, zf_ref), (1, zb_ref)):
            lbv, _ = _lower_bound(lbl_ref, dr)
            k_ref, bc_ref, dec_ref = ks_ref.at[dr], bcs_ref.at[dr], decs_ref.at[dr]
            _, keep = _chunk_consts(dr)
            cum_t = _chunk_consts(1 - dr)[0].astype(BF16)
            bc = bc_ref[...]
            qd_ref[...] = (q_ref[...] * jnp.exp(bc)).astype(BF16)
            ki_ref[...] = (k_ref[...] * jnp.exp(-bc)).astype(BF16)

            def local_step(m, carry, dr=dr, keep=keep):
                ns = [m * grp + u for u in range(grp)]
                rows = [_chunk_rows(n) for n in ns]
                rng = range(grp)
                qd = [qd_ref[r, :] for r in rows]
                ki = [ki_ref[r, :] for r in rows]
                doc = [do_ref[r, :] for r in rows]
                vc = [v_ref[r, :].astype(BF16) for r in rows]
                sc = [_dot(qd[u], ki[u], NT) for u in rng]
                dsc = [_dot(doc[u], vc[u], NT) for u in rng]
                inc = [_dot(vc[u], ki[u], TN) for u in rng]
                dinc = [_dot(doc[u], qd[u], TN) for u in rng]
                a = [jnp.where(keep, s, 0.0).astype(BF16) for s in sc]
                da = [jnp.where(keep, s, 0.0).astype(BF16) for s in dsc]
                dqd = [_dot(da[u], ki[u]) for u in rng]
                dki = [_dot(da[u], qd[u], TN) for u in rng]
                dv = [_dot(a[u], doc[u], TN) for u in rng]
                for u in rng:
                    ds_ref[ns[u]] = inc[u] * dec_ref[ns[u]][0:1, :]
                    dsl_ref[ns[u]] = dinc[u]
                    dqd_ref[rows[u], :] = dqd[u]
                    dki_ref[rows[u], :] = dki[u]
                    if dr == 0:
                        dv_ref[rows[u], :] = dv[u]
                    else:
                        dv_ref[rows[u], :] += dv[u]
                return carry

            lax.fori_loop(0, nc // grp, local_step, 0)

            def state_step(s, st, dr=dr):
                n = _chunk_of_step(s, dr, nc, ncc)
                inc = ds_ref[n]
                ds_ref[n] = st
                return st * dec_ref[n][0:1, :] + inc

            lax.fori_loop(0, nc, state_step, jnp.zeros((HEAD, HEAD), F32), unroll=4)

            def dstate_step(s, dst, dr=dr):
                n = _chunk_of_step(nc - 1 - s, dr, nc, ncc)
                inc = dsl_ref[n]
                dsl_ref[n] = dst
                return inc + dst * dec_ref[n][0:1, :]

            lax.fori_loop(0, nc, dstate_step, jnp.zeros((HEAD, HEAD), F32), unroll=4)

            def grad_step(m, carry, dr=dr, cum_t=cum_t):
                ns = [m * grp2 + u for u in range(grp2)]
                rows = [_chunk_rows(n) for n in ns]
                rng = range(grp2)
                st0 = [ds_ref[n] for n in ns]
                dst = [dsl_ref[n] for n in ns]
                dstb = [x.astype(BF16) for x in dst]
                dec = [dec_ref[n][0:1, :] for n in ns]
                doc = [do_ref[r, :] for r in rows]
                vc = [v_ref[r, :].astype(BF16) for r in rows]
                e = [jnp.exp(bc_ref[r, :]) for r in rows]
                einv = [jnp.exp(-bc_ref[r, :]) for r in rows]
                qd = [q_ref[rows[u], :] * e[u] for u in rng]
                ki = [k_ref[rows[u], :] * einv[u] for u in rng]
                kd = [ki[u] * dec[u] for u in rng]
                dqd_st = [_dot(doc[u], st0[u].astype(BF16)) for u in rng]
                dkd = [_dot(vc[u], dstb[u]) for u in rng]
                dv_st = [_dot(kd[u].astype(BF16), dstb[u], NT) for u in rng]
                dqd = [dqd_ref[rows[u], :] + dqd_st[u] for u in rng]
                dki = [dki_ref[r, :] for r in rows]
                dbc = [dqd[u] * qd[u] - dki[u] * ki[u] - dkd[u] * kd[u] for u in rng]
                cs = [_cumsum_chunk(cum_t, x) for x in dbc]
                for u in rng:
                    ddec = jnp.sum(dst[u] * st0[u], axis=0, keepdims=True)
                    dbl = jnp.sum(dkd[u] * kd[u], axis=0, keepdims=True) + ddec * dec[u]
                    dv_ref[rows[u], :] += dv_st[u]
                    dqd_ref[rows[u], :] = cs[u] + dbl
                    dki_ref[rows[u], :] = dki[u] * einv[u] + dkd[u] * (einv[u] * dec[u])
                    if dr == 0:
                        dq_ref[rows[u], :] = dqd[u] * e[u]
                    else:
                        dq_ref[rows[u], :] += dqd[u] * e[u]
                return carry

            lax.fori_loop(0, nc // grp2, grad_step, 0)

            sig, nsig, f = _gates(z_ref[...], lbv)
            common = (dqd_ref[...] / f - dki_ref[...]) * nsig
            dz_ref[dr] = (common * ((1.0 - lbv) * sig)).astype(BF16)
            dlb_ref[dr:dr + 1, :] = jnp.sum(common, axis=0, keepdims=True)

        dz_ref[2] = dv_ref[...].astype(BF16)
        dz_ref[3] = dq_ref[...].astype(BF16)

    sec = lambda s: pl.BlockSpec((None, t, HEAD), lambda h: (s, 0, h))
    col = pl.BlockSpec((seq, HEAD), lambda h: (0, h))
    tf32, tb16 = pltpu.VMEM((t, HEAD), F32), pltpu.VMEM((t, HEAD), BF16)
    states = pltpu.VMEM((nc, HEAD, HEAD), F32)
    return pl.pallas_call(
        body, name="hgrn_bwd", grid=(d // HEAD,),
        in_specs=[sec(0), sec(1), sec(2), sec(3), sec(4),
                  pl.BlockSpec((2, 3, HEAD), lambda h: (0, 0, h)), pl.BlockSpec((1, HEAD), lambda h: (0, h)),
                  col, col, pl.BlockSpec((2, t, HEAD), lambda h: (0, 0, h)),
                  pl.BlockSpec((2, t, HEAD), lambda h: (0, 0, h)),
                  pl.BlockSpec((2, nc, 8, HEAD), lambda h: (0, 0, 0, h))],
        out_specs=[pl.BlockSpec((5, t, HEAD), lambda h: (0, 0, h)),
                   pl.BlockSpec((1, HEAD), lambda h: (0, h)), pl.BlockSpec((2, HEAD), lambda h: (0, h))],
        out_shape=[jax.ShapeDtypeStruct((5, t, d), BF16), jax.ShapeDtypeStruct((1, d), F32),
                   jax.ShapeDtypeStruct((2, d), F32)],
        scratch_shapes=[tb16, tb16, tb16, tf32, tf32, tf32, tf32, states, states],
        compiler_params=_cp(),
    )(z1, z1, z1, z1, z1, lbl, onorm, o, dr_out, bcs, ks, decs)


def _place():
    x, y, c = lax.axis_index("x"), lax.axis_index("y"), lax.axis_index("c")
    chips = [(1 - x, y), (x, 1 - y), (1 - x, 1 - y)]
    return x, y, c, chips


def _relay_chips():
    x, y, c, _ = _place()
    first = c == 0
    near = (jnp.where(first, 1 - x, x), jnp.where(first, y, 1 - y))
    far = (jnp.where(first, x, 1 - x), jnp.where(first, 1 - y, y))
    return near, far, (1 - x, 1 - y)


def allgather_shards(bufs):
    n = len(bufs)

    def body(*refs):
        outs = refs[n:2 * n]
        done_ref, send_sems, recv_sems = refs[2 * n:]
        done_ref[...] = jnp.zeros((8, 128), F32)
        x, y, c, _ = _place()
        me = (x, y, c)
        p = 2 * x + y
        near, far, diag = _relay_chips()
        half = [pl.ds(c * (s.shape[1] // 2), s.shape[1] // 2) for s in bufs]
        other = [pl.ds((1 - c) * (s.shape[1] // 2), s.shape[1] // 2) for s in bufs]
        slot = lambda chip: 2 * chip[0] + chip[1]

        def remote(i, k, ref, to):
            return pltpu.make_async_remote_copy(src_ref=ref, dst_ref=ref, send_sem=send_sems.at[6 * i + k],
                                                recv_sem=recv_sems.at[6 * i + k], device_id=to, device_id_type=MESH)

        sends = []

        def send(i, k, ref, to):
            cp = remote(i, k, ref, to)
            cp.start()
            sends.append(cp)

        for i in range(n):
            mine = outs[i].at[p, half[i]]
            send(i, 0, mine, (*near, c))
            send(i, 1, mine, (*far, c))
        for i in range(n):
            landed = outs[i].at[slot(near), half[i]]
            remote(i, 0, landed, me).wait_recv()
            send(i, 2, landed, (*far, c))
            send(i, 3, landed, (x, y, 1 - c))
        for i in range(n):
            landed = outs[i].at[slot(far), half[i]]
            remote(i, 1, landed, me).wait_recv()
            send(i, 4, landed, (x, y, 1 - c))
        for i in range(n):
            landed = outs[i].at[slot(diag), half[i]]
            remote(i, 2, landed, me).wait_recv()
            send(i, 5, landed, (x, y, 1 - c))
        for i in range(n):
            for k, chip in ((3, far), (4, near), (5, diag)):
                remote(i, k, outs[i].at[slot(chip), other[i]], me).wait_recv()
        for cp in sends:
            cp.wait_send()

    return pl.pallas_call(
        body, name="allgather_shards",
        in_specs=[ANY] * n, out_specs=[ANY] * n + [VMEM],
        out_shape=[jax.ShapeDtypeStruct(s.shape, s.dtype) for s in bufs] + [jax.ShapeDtypeStruct((8, 128), F32)],
        input_output_aliases={i: i for i in range(n)},
        scratch_shapes=[pltpu.SemaphoreType.DMA((6 * n,)), pltpu.SemaphoreType.DMA((6 * n,))],
        compiler_params=pltpu.CompilerParams(has_side_effects=True),
    )(*bufs)


def pair_sum(grad, got, chip_core):
    _, r, cc = grad.shape
    hr = r // 2
    tr = 256 if hr % 256 == 0 else hr
    nb = hr // tr

    def body(cc_ref, a_ref, b_ref, own_ref, sb_ref):
        s = a_ref[...].astype(F32) + b_ref[...].astype(F32)
        sb_ref[...] = s.astype(BF16)

        @pl.when(pl.program_id(1) == cc_ref[0])
        def _():
            own_ref[...] = s

    grid_spec = pltpu.PrefetchScalarGridSpec(
        num_scalar_prefetch=1, grid=(nb, 4),
        in_specs=[pl.BlockSpec((None, tr, cc), lambda i, qi, cc_ref: (qi, cc_ref[1] * nb + i, 0)),
                  pl.BlockSpec((None, tr, cc), lambda i, qi, cc_ref: (qi, i, 0))],
        out_specs=[pl.BlockSpec((tr, cc), lambda i, qi, cc_ref: (i, 0)),
                   pl.BlockSpec((None, tr, cc), lambda i, qi, cc_ref: (qi, i, 0))])
    return pl.pallas_call(
        body, name="pair_sum", grid_spec=grid_spec,
        out_shape=[jax.ShapeDtypeStruct((hr, cc), F32), jax.ShapeDtypeStruct((4, hr, cc), BF16)],
        compiler_params=_cp(),
    )(chip_core, grad, got)


def owner_sum(own, got, chip_core):
    hr, cc = own.shape
    tr = 256 if hr % 256 == 0 else hr
    nb = hr // tr

    def body(cc_ref, a_ref, b_ref, o_ref):
        s = a_ref[...] + b_ref[0].astype(F32)
        s = s + b_ref[1].astype(F32)
        o_ref[...] = s + b_ref[2].astype(F32)

    grid_spec = pltpu.PrefetchScalarGridSpec(
        num_scalar_prefetch=1, grid=(nb,),
        in_specs=[pl.BlockSpec((tr, cc), lambda i, cc_ref: (i, 0)),
                  pl.BlockSpec((3, tr, cc), lambda i, cc_ref: (0, i, 0))],
        out_specs=pl.BlockSpec((tr, cc), lambda i, cc_ref: (cc_ref[1] * nb + i, 0)))
    return pl.pallas_call(
        body, name="owner_sum", grid_spec=grid_spec,
        out_shape=jax.ShapeDtypeStruct((2 * hr, cc), F32), compiler_params=_cp(),
    )(chip_core, own, got)


def share_halves(bufs):
    n = len(bufs)

    def body(*refs):
        outs = refs[n:2 * n]
        send_sems, recv_sems = refs[2 * n:]
        x, y, c, _ = _place()
        copies = []
        for i in range(n):
            hr = bufs[i].shape[0] // 2
            mine = outs[i].at[pl.ds(c * hr, hr)]
            cp = pltpu.make_async_remote_copy(
                src_ref=mine, dst_ref=mine, send_sem=send_sems.at[i], recv_sem=recv_sems.at[i],
                device_id=(x, y, 1 - c), device_id_type=MESH)
            cp.start()
            copies.append((cp, outs[i].at[pl.ds((1 - c) * hr, hr)]))
        for i, (cp, theirs) in enumerate(copies):
            cp.wait_send()
            pltpu.make_async_remote_copy(
                src_ref=theirs, dst_ref=theirs, send_sem=send_sems.at[i], recv_sem=recv_sems.at[i],
                device_id=(x, y, c), device_id_type=MESH).wait_recv()

    return pl.pallas_call(
        body, name="share_halves",
        in_specs=[ANY] * n, out_specs=[ANY] * n,
        out_shape=[jax.ShapeDtypeStruct(b.shape, b.dtype) for b in bufs],
        input_output_aliases={i: i for i in range(n)},
        scratch_shapes=[pltpu.SemaphoreType.DMA((n,)), pltpu.SemaphoreType.DMA((n,))],
        compiler_params=pltpu.CompilerParams(has_side_effects=True),
    )(*bufs)


def allgather8(v, name):
    r, n = v.shape

    def body(v_ref, out_ref, send_sems, recv_sems):
        x, y, c, _ = _place()
        me = 4 * x + 2 * y + c
        out_ref[me] = v_ref[...]

        def copy(k, slot, to):
            return pltpu.make_async_remote_copy(
                src_ref=v_ref, dst_ref=out_ref.at[slot], send_sem=send_sems.at[k - 1],
                recv_sem=recv_sems.at[k - 1], device_id=to, device_id_type=MESH)

        peers = []
        for k in range(1, 8):
            px = 1 - x if (k >> 2) & 1 else x
            py = 1 - y if (k >> 1) & 1 else y
            pc = 1 - c if k & 1 else c
            peers.append((px, py, pc))
            copy(k, me, (px, py, pc)).start()
        for k, (px, py, pc) in enumerate(peers, start=1):
            copy(k, 4 * px + 2 * py + pc, (x, y, c)).wait_recv()
        for k, peer in enumerate(peers, start=1):
            copy(k, me, peer).wait_send()

    return pl.pallas_call(
        body, name=name, in_specs=[VMEM], out_specs=VMEM,
        out_shape=jax.ShapeDtypeStruct((8, r, n), v.dtype),
        scratch_shapes=[pltpu.SemaphoreType.DMA((7,)), pltpu.SemaphoreType.DMA((7,))],
        compiler_params=_cp(has_side_effects=True),
    )(v)


HBM = pl.BlockSpec(memory_space=pltpu.HBM)
SEM = pl.BlockSpec(memory_space=pltpu.SEMAPHORE)
DATAFLOW = pltpu.SideEffectType.DATAFLOW_SIDE_EFFECTING


def _descriptors(plan, refs, send_sems, recv_sems, arrivals=True):
    x, y, c, _ = _place()
    sends, recvs = plan(refs)
    out = [pltpu.make_async_remote_copy(src_ref=src, dst_ref=dst, send_sem=send_sems.at[k],
                                        recv_sem=recv_sems.at[k], device_id=to, device_id_type=MESH)
           for k, (src, dst, to) in enumerate(sends)]
    if not arrivals:
        return out, []
    inn = [pltpu.make_async_remote_copy(src_ref=land, dst_ref=land, send_sem=send_sems.at[k],
                                        recv_sem=recv_sems.at[k], device_id=(x, y, c), device_id_type=MESH)
           for k, land in enumerate(recvs)]
    return out, inn


def copies_start(name, arrays, n_copies, plan, after):
    na = len(arrays)

    def body(*refs):
        out, _ = _descriptors(plan, refs[:na], refs[na + 1], refs[na + 2], arrivals=False)
        for cp in out:
            cp.start()
        refs[-1][...] = jnp.zeros((8, 128), F32)

    res = pl.pallas_call(
        body, name=name,
        out_shape=(pltpu.SemaphoreType.DMA((n_copies,)), pltpu.SemaphoreType.DMA((n_copies,)),
                   *[pltpu.HBM(a.shape, a.dtype) for a in arrays], jax.ShapeDtypeStruct((8, 128), F32)),
        in_specs=[HBM] * na + [ANY], out_specs=(SEM, SEM, *[HBM] * na, VMEM),
        input_output_aliases={i: i + 2 for i in range(na)},
        compiler_params=pltpu.CompilerParams(has_side_effects=DATAFLOW),
    )(*[pltpu.with_memory_space_constraint(a, pltpu.HBM) for a in arrays], after)
    return res[0], res[1], list(res[2:2 + na]), res[-1]


def copies_wait(name, started, plan, after):
    send_sems, recv_sems, arrays, _ = started
    na = len(arrays)
    after = list(after) if isinstance(after, (list, tuple)) else [after]

    def body(*refs):
        out, inn = _descriptors(plan, refs[:na], refs[na], refs[na + 1])
        for cp in out:
            cp.wait_send()
        for cp in inn:
            cp.wait_recv()
        refs[-1][...] = jnp.zeros((8, 128), F32)

    res = pl.pallas_call(
        body, name=name,
        out_shape=(*[pltpu.HBM(a.shape, a.dtype) for a in arrays], jax.ShapeDtypeStruct((8, 128), F32)),
        in_specs=[HBM] * na + [SEM, SEM] + [ANY] * len(after), out_specs=(*[HBM] * na, VMEM),
        input_output_aliases={i: i for i in range(na)},
        compiler_params=pltpu.CompilerParams(has_side_effects=DATAFLOW),
    )(*arrays, send_sems, recv_sems, *after)
    return list(res[:na]), res[-1]


def _rows_half(r, c):
    return pl.ds(c * (r // 2), r // 2), pl.ds((1 - c) * (r // 2), r // 2)


def plan_gather_neighbours(refs):
    x, y, c, _ = _place()
    p = 2 * x + y
    near, far, _ = _relay_chips()
    sends, recvs = [], []
    for buf in refs:
        mine, _ = _rows_half(buf.shape[1], c)
        for chip in (near, far):
            sends.append((buf.at[p, mine], buf.at[p, mine], (*chip, c)))
            recvs.append(buf.at[2 * chip[0] + chip[1], mine])
    return sends, recvs


def plan_gather_relay(refs):
    x, y, c, _ = _place()
    near, far, diag = _relay_chips()
    slot = lambda chip: 2 * chip[0] + chip[1]
    sends, recvs = [], []
    for buf in refs:
        mine, theirs = _rows_half(buf.shape[1], c)
        landed = buf.at[slot(near), mine]
        sends.append((landed, landed, (*far, c)))
        recvs.append(buf.at[slot(diag), mine])
        for sent, got in ((near, far), (far, near)):
            sends.append((buf.at[slot(sent), mine], buf.at[slot(sent), mine], (x, y, 1 - c)))
            recvs.append(buf.at[slot(got), theirs])
    return sends, recvs


def plan_gather_d2d(refs):
    x, y, c, _ = _place()
    _, _, diag = _relay_chips()
    sends, recvs = [], []
    for buf in refs:
        mine, theirs = _rows_half(buf.shape[1], c)
        landed = buf.at[2 * diag[0] + diag[1], mine]
        sends.append((landed, landed, (x, y, 1 - c)))
        recvs.append(buf.at[2 * diag[0] + diag[1], theirs])
    return sends, recvs


def plan_exchange(refs):
    x, y, c, _ = _place()
    n = len(refs) // 2
    sends, recvs = [], []
    for grad, land in zip(refs[:n], refs[n:]):
        _, theirs = _rows_half(grad.shape[1], c)
        sends.append((grad.at[:, theirs], land, (x, y, 1 - c)))
        recvs.append(land)
    return sends, recvs


def plan_scatter(refs):
    x, y, c, chips = _place()
    n = len(refs) // 2
    sends, recvs = [], []
    for part, land in zip(refs[:n], refs[n:]):
        for j, chip in enumerate(chips):
            sends.append((part.at[2 * chip[0] + chip[1]], land.at[j], (*chip, c)))
            recvs.append(land.at[j])
    return sends, recvs


def plan_share(refs):
    x, y, c, _ = _place()
    sends, recvs = [], []
    for buf in refs:
        mine, theirs = _rows_half(buf.shape[0], c)
        sends.append((buf.at[mine], buf.at[mine], (x, y, 1 - c)))
        recvs.append(buf.at[theirs])
    return sends, recvs


def put_in_slot(w, chip, dtype, name):
    r, c = w.shape
    tr = 256 if r % 256 == 0 else r

    def body(chip_ref, w_ref, o_ref):
        o_ref[...] = w_ref[...].astype(dtype)

    grid_spec = pltpu.PrefetchScalarGridSpec(
        num_scalar_prefetch=1, grid=(r // tr,),
        in_specs=[pl.BlockSpec((tr, c), lambda i, chip_ref: (i, 0))],
        out_specs=pl.BlockSpec((None, tr, c), lambda i, chip_ref: (chip_ref[0], i, 0)))
    return pl.pallas_call(body, name=name, grid_spec=grid_spec,
                          out_shape=jax.ShapeDtypeStruct((4, r, c), dtype), compiler_params=_cp())(chip, w)


def ada_fwd(s_in, ada_w, ada_b, tn):
    nl, d, ws = ada_w.shape

    def body(s_ref, w_ref, b_ref, so_ref, mod_ref):
        s = _silu(s_ref[...])
        so_ref[...] = s
        mod_ref[...] = _dot(s.astype(BF16), w_ref[...].astype(BF16)) + b_ref[...]

    return pl.pallas_call(
        body, name="ada_fwd", grid=(nl, ws // tn),
        in_specs=[pl.BlockSpec((16, d), lambda l, j: (0, 0)),
                  pl.BlockSpec((None, d, tn), lambda l, j: (l, 0, j)),
                  pl.BlockSpec((None, 1, tn), lambda l, j: (l, 0, j))],
        out_specs=[pl.BlockSpec((16, d), lambda l, j: (0, 0)),
                   pl.BlockSpec((None, 16, tn), lambda l, j: (l, 0, j))],
        out_shape=[jax.ShapeDtypeStruct((16, d), F32), jax.ShapeDtypeStruct((nl, 16, ws), F32)],
        compiler_params=_cp(),
    )(s_in, ada_w, ada_b)


def _adamw_math(w, g, m, v):
    m = ADAM_B1 * m + (1.0 - ADAM_B1) * g
    v = ADAM_B2 * v + (1.0 - ADAM_B2) * (g * g)
    m_hat = m / (1.0 - ADAM_B1 ** ADAM_STEP)
    v_hat = v / (1.0 - ADAM_B2 ** ADAM_STEP)
    delta = -ADAM_LR * (m_hat / (jnp.sqrt(v_hat) + ADAM_EPS) + ADAM_WD * w)
    return delta, m, v


def ada_bwd_adamw(s, dm, w, m, v):
    nl, d, ws = w.shape
    tr = 256 if d % 256 == 0 else 128

    def body(s_ref, dm_ref, w_ref, m_ref, v_ref, g_ref, dl_ref, mo_ref, vo_ref, dc_ref):
        dmv = dm_ref[...].astype(BF16)
        wv = w_ref[...]
        g = _dot(s_ref[...].astype(BF16), dmv, TN)
        g_ref[...] = g
        dl_ref[...], mo_ref[...], vo_ref[...] = _adamw_math(wv, g, m_ref[...], v_ref[...])
        dc_ref[...] = _dot(dmv[8:16, :], wv.astype(BF16), NT)

    wblk = pl.BlockSpec((None, tr, ws), lambda l, i: (l, i, 0))
    wshape = jax.ShapeDtypeStruct((nl, d, ws), F32)
    return pl.pallas_call(
        body, name="ada_bwd_adamw", grid=(nl, d // tr),
        in_specs=[pl.BlockSpec((16, tr), lambda l, i: (0, i)),
                  pl.BlockSpec((None, 16, ws), lambda l, i: (l, 0, 0)), wblk, wblk, wblk],
        out_specs=[wblk, wblk, wblk, wblk, pl.BlockSpec((None, 8, tr), lambda l, i: (l, 0, i))],
        out_shape=[wshape, wshape, wshape, wshape, jax.ShapeDtypeStruct((nl, 8, d), F32)],
        compiler_params=_cp(),
    )(s, dm, w, m, v)


def adamw(w, g, m, v, name, with_grad=False):
    r, c = w.shape
    tr = 256 if r % 256 == 0 else r

    def body(w_ref, g_ref, m_ref, v_ref, dl_ref, mo_ref, vo_ref, *g_out):
        gv = g_ref[...]
        dl_ref[...], mo_ref[...], vo_ref[...] = _adamw_math(w_ref[...], gv, m_ref[...], v_ref[...])
        if with_grad:
            g_out[0][...] = gv

    blk = pl.BlockSpec((tr, c), lambda i: (i, 0))
    shape = jax.ShapeDtypeStruct((r, c), F32)
    n_out = 4 if with_grad else 3
    return pl.pallas_call(body, name=name, grid=(r // tr,), in_specs=[blk] * 4, out_specs=[blk] * n_out,
                          out_shape=[shape] * n_out, compiler_params=_cp())(w, g, m, v)


ROW_MOD = 10


def small_reduce(gathered):
    _, rows, d = gathered.shape

    def body(g_ref, o_ref):
        tot = g_ref[0]
        for b in range(1, 8):
            tot = tot + g_ref[b]
        o_ref[0:rows, :] = tot
        for layer in range(2):
            lat = ROW_MOD + 6 * layer
            o_ref[24 + 3 * layer:27 + 3 * layer, :] = tot[lat:lat + 3, :] + tot[lat + 3:lat + 6, :]
        o_ref[30:32, :] = jnp.zeros((2, d), F32)

    return pl.pallas_call(body, name="small_reduce", in_specs=[VMEM], out_specs=VMEM,
                          out_shape=jax.ShapeDtypeStruct((32, d), F32), compiler_params=_cp())(gathered)


def lb_logits_grad(lbl, dlb):
    _, _, n = lbl.shape

    def body(l_ref, d_ref, o_ref):
        for dr in range(2):
            _, (p0, p1, p2) = _lower_bound(l_ref, dr)
            dv = d_ref[dr:dr + 1, :]
            o_ref[dr, 0:1, :] = p0 * p2 * dv
            o_ref[dr, 1:2, :] = p1 * p2 * dv
            o_ref[dr, 2:3, :] = -p2 * (p0 + p1) * dv

    return pl.pallas_call(body, name="lb_logits_grad", in_specs=[VMEM, VMEM], out_specs=VMEM,
                          out_shape=jax.ShapeDtypeStruct((2, 3, n), F32), compiler_params=_cp())(lbl, dlb)


def c_ctx_grad(parts, c_ctx):
    d = c_ctx.shape[1]

    def body(p_ref, c_ref, o_ref):
        tot = p_ref[0, 0:1, :]
        for chip in range(1, 4):
            tot = tot + p_ref[2 * chip, 0:1, :]
        o_ref[...] = tot * _dsilu(c_ref[...])

    return pl.pallas_call(body, name="c_ctx_grad", in_specs=[VMEM, VMEM], out_specs=VMEM,
                          out_shape=jax.ShapeDtypeStruct((1, d), F32), compiler_params=_cp())(parts, c_ctx)


def kernel(x, c, ctx, c_ctx, ada_w, ada_b, pre_g, post_g, ev_w_in, ev_pool_w, ev_pool_scale, ev_conv_w, ev_conv_b, ev_w_out, od_w_in, od_onorm_g, od_w_out, lb_logits, loss_target, m_c_ctx, m_ada_w, m_ada_b, m_pre_g, m_post_g, m_ev_w_in, m_ev_pool_w, m_ev_pool_scale, m_ev_conv_w, m_ev_conv_b, m_ev_w_out, m_od_w_in, m_od_onorm_g, m_od_w_out, m_lb_logits, v_c_ctx, v_ada_w, v_ada_b, v_pre_g, v_post_g, v_ev_w_in, v_ev_pool_w, v_ev_pool_scale, v_ev_conv_w, v_ev_conv_b, v_ev_w_out, v_od_w_in, v_od_onorm_g, v_od_w_out, v_lb_logits):
    _, seq, d = x.shape
    cx = ctx.shape[1]
    t = cx + seq
    half_d = d // 2
    g = half_d // N_POOL
    tn = d // 4
    xi, yi, ci = lax.axis_index("x"), lax.axis_index("y"), lax.axis_index("c")
    chip = 2 * xi + yi
    me = 2 * chip + ci
    chip_arr = jnp.reshape(chip, (1,)).astype(jnp.int32)
    chip_core_arr = jnp.stack([chip, ci]).astype(jnp.int32)

    pad = lambda a, rows: jnp.concatenate([a, jnp.zeros((rows - a.shape[0], g), F32)], axis=0)
    small = jnp.concatenate([
        ev_pool_w.reshape(g, g), pad(ev_conv_w.reshape(3, g), 8), pad(od_onorm_g.reshape(2, g), 8),
        pad(lb_logits.reshape(12, g), 16)], axis=0)
    ev_in_g, ev_out_g, small_g, ev_done = allgather_shards([
        put_in_slot(ev_w_in[0], chip_arr, BF16, "cast_ev_w_in"),
        put_in_slot(ev_w_out[0], chip_arr, BF16, "cast_ev_w_out"),
        put_in_slot(small, chip_arr, F32, "place_small")])
    ev_out3 = ev_out_g.reshape(1, d, d)
    pool_w_full = small_g[:, :g].reshape(4, N_POOL, g // 4, g).transpose(1, 0, 2, 3).reshape(N_POOL, g, g)
    conv_w_full = small_g[:, g:g + 3].transpose(1, 0, 2).reshape(3, half_d)
    onorm_full = small_g[:, g + 8:g + 10].reshape(1, d)
    lbl_full = small_g[:, g + 16:g + 28].reshape(4, 2, 3, 2 * g).transpose(1, 2, 0, 3).reshape(2, 3, d)

    c_rows = jnp.concatenate([c + ev_done[0:1, 0:1], jnp.zeros((7, d), F32)], axis=0)
    c_all = allgather8(c_rows, "allgather_c")[:, 0, :]
    s_in = jnp.concatenate([c_all, c_ctx.reshape(1, d), jnp.zeros((7, d), F32)], axis=0)
    ws_ada = ada_w.shape[2]
    ada_b_mine = lax.dynamic_slice(ada_b, (0, chip * ws_ada), (2, ws_ada)).reshape(2, 1, ws_ada)
    s_act, mod_mine = ada_fwd(s_in, ada_w, ada_b_mine, tn)
    mod_all = allgather8(mod_mine.reshape(32, ws_ada), "allgather_mod")
    od_ici = copies_start("gather_od_ici_start", [
        put_in_slot(od_w_in[0], chip_arr, BF16, "cast_od_w_in"),
        put_in_slot(od_w_out[0], chip_arr, BF16, "cast_od_w_out")], 4, plan_gather_neighbours, mod_all)
    mod_full = mod_all[0::2].reshape(4, 2, 16, ws_ada).transpose(1, 2, 0, 3).reshape(2, 16, 3 * d)
    mod_lat = lax.dynamic_slice(mod_full, (0, me, 0), (2, 1, 3 * d))
    mods = jnp.concatenate([mod_full[:, 8:9], mod_lat], axis=1)
    shift, scale, gate = mods[:, :, :d], mods[:, :, d:2 * d], mods[:, :, 2 * d:]

    h0, xs = normmod_fwd_joining(ctx[0], x[0], pre_g[0:1] + od_ici[3][0:1, 0:1], shift[0], scale[0])
    z0 = mm_nn(h0, ev_in_g, half_d, tn, "mm_ev_in")
    u = mix_b_fwd(z0, conv_w_full, ev_conv_b, mix_a_fwd(z0, pool_w_full, ev_pool_scale, cx), cx)
    od_relay = copies_start("gather_od_relay_start",
                            copies_wait("gather_od_ici_wait", od_ici, plan_gather_neighbours, u)[0],
                            6, plan_gather_relay, u)
    y0 = mm_nn(u, ev_out3, d, tn, "mm_ev_out")[0]
    xs1 = post_fwd(xs, y0, post_g[0:1] + od_relay[3][0:1, 0:1], gate[0], cx)
    od_d2d = copies_start("gather_od_d2d_start",
                          copies_wait("gather_od_relay_wait", od_relay, plan_gather_relay, xs1)[0],
                          2, plan_gather_d2d, xs1)
    (od_in_g, od_out_g), _ = copies_wait("gather_od_d2d_wait", od_d2d, plan_gather_d2d, od_d2d[3])
    od_out3 = od_out_g.reshape(1, d, d)

    h1 = normmod_fwd(xs1, pre_g[1:2], shift[1], scale[1], cx)
    z1 = mm_nn(h1, od_in_g, d, tn, "mm_od_in")
    o1, r1, bcs1, ks1, decs1 = hgrn_fwd(z1, lbl_full, onorm_full, cx)
    y1 = mm_nn(r1, od_out3, d, tn, "mm_od_out")[0]
    sq, dx2 = post_loss(xs1, y1, post_g[1:2], gate[1], loss_target[0], cx)

    dy1, dgate1, dpost1 = post_bwd(dx2, y1, post_g[1:2], gate[1], cx, True)
    dr1 = mm_nt(dy1[None], None, od_out3, tn, "mm_od_out_dx")
    g_od_out = mm_tn(r1, dy1[None], None, d, tn, "mm_od_out_dw")
    dz1, donorm, dlb = hgrn_bwd(z1, lbl_full, onorm_full, o1, dr1, bcs1, ks1, decs1, cx)
    dh1 = mm_nt(dz1, None, od_in_g, tn, "mm_od_in_dx")
    g_od_in = mm_tn(h1, dz1, None, od_in_g.shape[2], tn, "mm_od_in_dw")
    dxs1, dpre1, dshift1, dscale1 = normmod_bwd(xs1, dh1, pre_g[1:2], scale[1], dx2, cx, True)

    od_grads = [g_od_in, g_od_out.reshape(4, d // 4, d)]
    half_zone = lambda a, lead, dt: lax.empty((lead, a.shape[1] // 2, a.shape[2]), dt)
    od_ex = copies_start("reduce_od_exchange_start", od_grads + [half_zone(a, 4, a.dtype) for a in od_grads],
                         2, plan_exchange, dxs1)

    dy0, dgate0, dpost0 = post_bwd(dxs1, y0, post_g[0:1] + od_ex[3][0:1, 0:1], gate[0], cx, False)
    du = mm_nt(dy0[None], None, ev_out3, tn, "mm_ev_out_dx")
    g_ev_out = mm_tn(u, dy0[None], None, d, tn, "mm_ev_out_dw")
    od_got, _ = copies_wait("reduce_od_exchange_wait", od_ex, plan_exchange, g_ev_out)
    od_sums = [pair_sum(od_got[i], od_got[2 + i], chip_core_arr) for i in range(2)]
    od_sc = copies_start("reduce_od_scatter_start",
                         [sb for _, sb in od_sums] + [half_zone(a, 3, BF16) for a in od_grads],
                         6, plan_scatter, du)
    dz0a, g_pool_w, dpool_scale = mix_a_bwd(z0, du, pool_w_full, ev_pool_scale + od_sc[3][0:1, 0:1], cx)
    dz0b, dconv_w, dconv_b = mix_b_bwd(z0, du, conv_w_full, ev_conv_b + od_sc[3][0:1, 0:1], cx)
    g_ev_in = mm_tn(h0, dz0a, dz0b, ev_in_g.shape[2], tn, "mm_ev_in_dw")
    ev_grads = [g_ev_in, g_ev_out.reshape(4, d // 4, d), g_pool_w.reshape(4, g, g)]
    ev_ex = copies_start("reduce_ev_exchange_start", ev_grads + [half_zone(a, 4, a.dtype) for a in ev_grads],
                         3, plan_exchange, dpool_scale)
    dh0 = mm_nt(dz0a, dz0b, ev_in_g, tn, "mm_ev_in_dx")
    dxs0, dpre0, dshift0, dscale0 = normmod_bwd(xs, dh0, pre_g[0:1] + ev_ex[3][0:1, 0:1], scale[0], dxs1,
                                                cx, False, True)
    grad_x = dxs0[None]
    ev_got, _ = copies_wait("reduce_ev_exchange_wait", ev_ex, plan_exchange, dxs0)
    ev_sums = [pair_sum(ev_got[i], ev_got[3 + i], chip_core_arr) for i in range(3)]
    od_recv, _ = copies_wait("reduce_od_scatter_wait", od_sc, plan_scatter, dxs0)

    zrow = jnp.zeros((1, d), F32)
    small_rows = jnp.concatenate([
        dpre0, dpre1, dpost0, dpost1,
        jnp.concatenate([dpool_scale, dconv_b], axis=1),
        jnp.concatenate([dconv_w.reshape(1, 3 * half_d), jnp.zeros((1, half_d), F32)], axis=1).reshape(2, d),
        donorm, dlb,
        dshift0[1:2], dscale0[1:2], dgate0[1:2], dshift0[0:1], dscale0[0:1], dgate0[0:1],
        dshift1[1:2], dscale1[1:2], dgate1[1:2], dshift1[0:1], dscale1[0:1], zrow,
        jnp.concatenate([sq[0:1], jnp.zeros((1, d - 128), F32)], axis=1),
        zrow], axis=0)
    small_all = allgather8(small_rows, "allgather_small")
    ev_sc = copies_start("reduce_ev_scatter_start",
                         [sb for _, sb in ev_sums] + [half_zone(a, 3, BF16) for a in ev_grads],
                         9, plan_scatter, small_all)
    od_sh = copies_start("reduce_od_share_start",
                         [owner_sum(od_sums[i][0], od_recv[2 + i], chip_core_arr) for i in range(2)],
                         2, plan_share, dxs0)
    tot = small_reduce(small_all + ev_sc[3][0:1, 0:1])
    loss = tot[22, 0] * (0.5 / d)

    dm_rows = []
    for layer in range(2):
        lat = ROW_MOD + 6 * layer
        dm_lat = small_all[:, lat:lat + 3].reshape(8, 3 * d)
        dm_ctx = tot[lat + 3:lat + 6].reshape(1, 3 * d)
        dm_rows.append(jnp.concatenate([dm_lat, dm_ctx, jnp.zeros((7, 3 * d), F32)], axis=0))
    dm_full = jnp.stack(dm_rows)
    dm_mine = lax.dynamic_slice(dm_full, (0, 0, chip * ws_ada), (2, 16, ws_ada))

    def step(w, gr, m, v, name, with_grad=False):
        shape = w.shape
        cols = shape[-1]
        two_d = lambda a: a.reshape(-1, cols)
        res = adamw(two_d(w), two_d(gr), two_d(m), two_d(v), "adamw_" + name, with_grad)
        return tuple(a.reshape(shape) for a in res)

    grad_ada_b = tot[24:30].reshape(2, 3 * d)
    grad_pre_g = tot[0:2]
    grad_post_g = tot[2:4]
    grad_ev_pool_scale = tot[4:5, :half_d]
    grad_ev_conv_b = tot[4:5, half_d:]
    conv_w_tot = tot[5:7].reshape(1, 2 * d)[:, :3 * half_d].reshape(3, N_POOL, g)
    grad_ev_conv_w = lax.dynamic_slice(conv_w_tot, (0, chip, 0), (3, 1, g)).reshape(1, 3, g)
    grad_od_onorm_g = lax.dynamic_slice(tot[7:8], (0, chip * 2 * g), (1, 2 * g))
    dlb_mine = lax.dynamic_slice(tot[8:10], (0, chip * 2 * g), (2, 2 * g))
    grad_lb_logits = lb_logits_grad(lb_logits, dlb_mine)
    upd = {
        "ada_b": step(ada_b, grad_ada_b, m_ada_b, v_ada_b, "ada_b"),
        "pre_g": step(pre_g, grad_pre_g, m_pre_g, v_pre_g, "pre_g"),
        "post_g": step(post_g, grad_post_g, m_post_g, v_post_g, "post_g"),
        "ev_pool_scale": step(ev_pool_scale, grad_ev_pool_scale, m_ev_pool_scale, v_ev_pool_scale, "ev_pool_scale"),
        "ev_conv_w": step(ev_conv_w, grad_ev_conv_w, m_ev_conv_w, v_ev_conv_w, "ev_conv_w"),
        "ev_conv_b": step(ev_conv_b, grad_ev_conv_b, m_ev_conv_b, v_ev_conv_b, "ev_conv_b"),
        "od_onorm_g": step(od_onorm_g, grad_od_onorm_g, m_od_onorm_g, v_od_onorm_g, "od_onorm_g"),
        "lb_logits": step(lb_logits, grad_lb_logits, m_lb_logits, v_lb_logits, "lb_logits"),
    }
    grad_ada_w, delta_ada_w, new_m_ada_w, new_v_ada_w, dctx_part = ada_bwd_adamw(
        s_act, dm_mine, ada_w, m_ada_w, v_ada_w)
    upd["ada_w"] = (delta_ada_w, new_m_ada_w, new_v_ada_w)
    (grad_od_w_in, grad_od_w_out), _ = copies_wait("reduce_od_share_wait", od_sh, plan_share, ev_sc[3])
    upd["od_w_in"] = step(od_w_in, grad_od_w_in[None], m_od_w_in, v_od_w_in, "od_w_in", True)
    upd["od_w_out"] = step(od_w_out, grad_od_w_out[None], m_od_w_out, v_od_w_out, "od_w_out", True)
    grad_od_w_in, grad_od_w_out = upd["od_w_in"][3], upd["od_w_out"][3]
    done_behind = [dctx_part] + [upd[k][0] for k in (
        "od_w_in", "od_w_out", "ada_b", "pre_g", "post_g", "ev_pool_scale", "ev_conv_w", "ev_conv_b",
        "od_onorm_g", "lb_logits")]
    ev_recv, ev_landed = copies_wait("reduce_ev_scatter_wait", ev_sc, plan_scatter, done_behind)
    grad_ev_w_in, grad_ev_w_out, grad_pool_w = share_halves(
        [owner_sum(ev_sums[i][0], ev_recv[3 + i], chip_core_arr) for i in range(3)])
    dctx_all = allgather8(dctx_part[0] + dctx_part[1] + ev_landed[0:1, 0:1], "allgather_dctx")
    grad_c_ctx = c_ctx_grad(dctx_all, c_ctx.reshape(1, d)).reshape(d)
    upd["c_ctx"] = step(c_ctx, grad_c_ctx, m_c_ctx, v_c_ctx, "c_ctx")
    upd["ev_w_in"] = step(ev_w_in, grad_ev_w_in[None], m_ev_w_in, v_ev_w_in, "ev_w_in", True)
    upd["ev_pool_w"] = step(ev_pool_w, grad_pool_w.reshape(1, N_POOL, g // 4, g), m_ev_pool_w, v_ev_pool_w,
                            "ev_pool_w", True)
    upd["ev_w_out"] = step(ev_w_out, grad_ev_w_out[None], m_ev_w_out, v_ev_w_out, "ev_w_out", True)
    grad_ev_w_in, grad_ev_pool_w, grad_ev_w_out = upd["ev_w_in"][3], upd["ev_pool_w"][3], upd["ev_w_out"][3]
    names = ["c_ctx", "ada_w", "ada_b", "pre_g", "post_g", "ev_w_in", "ev_pool_w", "ev_pool_scale",
             "ev_conv_w", "ev_conv_b", "ev_w_out", "od_w_in", "od_onorm_g", "od_w_out", "lb_logits"]
    grads = [grad_c_ctx, grad_ada_w, grad_ada_b, grad_pre_g, grad_post_g, grad_ev_w_in, grad_ev_pool_w,
             grad_ev_pool_scale, grad_ev_conv_w, grad_ev_conv_b, grad_ev_w_out, grad_od_w_in,
             grad_od_onorm_g, grad_od_w_out, grad_lb_logits]
    return (loss, grad_x, *grads, *[upd[k][0] for k in names], *[upd[k][1] for k in names],
            *[upd[k][2] for k in names])
```

```python
import jax
import jax.numpy as jnp
from jax import lax
from jax.experimental import pallas as pl
from jax.experimental.pallas import tpu as pltpu

EPS = 1e-6
GRID_W_LOG2 = 6
CHUNK = 64
HEAD = 128
N_POOL = 4
ADAM_LR, ADAM_B1, ADAM_B2, ADAM_EPS, ADAM_WD, ADAM_STEP = 0.001, 0.9, 0.999, 1e-08, 0.01, 10
VMEM_LIMIT = 56 * 1024 * 1024
MESH = pl.DeviceIdType.MESH
F32, BF16 = jnp.float32, jnp.bfloat16
ANY = pl.BlockSpec(memory_space=pl.ANY)
VMEM = pl.BlockSpec(memory_space=pltpu.VMEM)


def _cp(**kw):
    return pltpu.CompilerParams(vmem_limit_bytes=VMEM_LIMIT, **kw)


def _silu(x):
    return x * jax.nn.sigmoid(x)


def _dsilu(x):
    s = jax.nn.sigmoid(x)
    return s * (1.0 + x * (1.0 - s))


def _dot(a, b, dims=((1,), (0,)), precision=None):
    return lax.dot_general(a, b, (dims, ((), ())), preferred_element_type=F32, precision=precision)


NN = ((1,), (0,))
NT = ((1,), (1,))
TN = ((0,), (0,))


def _row_block(cx):
    return 256 if cx % 256 == 0 else 128


def normmod_fwd_joining(ctx, x, g, shift, scale):
    cx, d = ctx.shape
    t = cx + x.shape[0]
    tm = _row_block(cx)
    nctx = cx // tm

    def body(c_ref, x_ref, g_ref, sh_ref, sc_ref, h_ref, xs_ref):
        is_ctx = pl.program_id(0) < nctx
        x = jnp.where(is_ctx, c_ref[...], x_ref[...])
        xs_ref[...] = x
        rstd = lax.rsqrt(jnp.mean(x * x, axis=-1, keepdims=True) + EPS)
        sc = jnp.where(is_ctx, sc_ref[0:1, :], sc_ref[1:2, :])
        sh = jnp.where(is_ctx, sh_ref[0:1, :], sh_ref[1:2, :])
        h_ref[...] = ((x * rstd) * g_ref[...] * (1.0 + sc) + sh).astype(BF16)

    row = pl.BlockSpec((tm, d), lambda i: (i, 0))
    vec = lambda r: pl.BlockSpec((r, d), lambda i: (0, 0))
    return pl.pallas_call(
        body, name="normmod_fwd_joining", grid=(t // tm,),
        in_specs=[pl.BlockSpec((tm, d), lambda i: (jnp.minimum(i, nctx - 1), 0)),
                  pl.BlockSpec((tm, d), lambda i: (jnp.maximum(i - nctx, 0), 0)), vec(1), vec(2), vec(2)],
        out_specs=[row, row],
        out_shape=[jax.ShapeDtypeStruct((t, d), BF16), jax.ShapeDtypeStruct((t, d), F32)],
        compiler_params=_cp(),
    )(ctx, x, g, shift, scale)


def normmod_bwd(xs, dh, g, scale, dres, cx, res_is_latent_only, dx_latent_only=False):
    t, d = xs.shape
    tm = _row_block(cx)
    nctx = cx // tm

    def body(x_ref, dh_ref, g_ref, sc_ref, dres_ref, dx_ref, dg_ref, dsh_ref, dsc_ref):
        i = pl.program_id(0)
        is_ctx = i < nctx

        @pl.when(i == 0)
        def _():
            dg_ref[...] = jnp.zeros_like(dg_ref)
            dsh_ref[...] = jnp.zeros_like(dsh_ref)
            dsc_ref[...] = jnp.zeros_like(dsc_ref)

        x = x_ref[...]
        dh = dh_ref[...]
        gv = g_ref[...]
        rstd = lax.rsqrt(jnp.mean(x * x, axis=-1, keepdims=True) + EPS)
        xhat = x * rstd
        sc = jnp.where(is_ctx, sc_ref[0:1, :], sc_ref[1:2, :])
        dsh = jnp.sum(dh, axis=0, keepdims=True)
        dhx = dh * xhat
        dsc = jnp.sum(dhx * gv, axis=0, keepdims=True)
        dg_ref[...] += jnp.sum(dhx * (1.0 + sc), axis=0, keepdims=True)
        zero = jnp.zeros_like(dsh)
        dsh_ref[0:1, :] += jnp.where(is_ctx, dsh, zero)
        dsh_ref[1:2, :] += jnp.where(is_ctx, zero, dsh)
        dsc_ref[0:1, :] += jnp.where(is_ctx, dsc, zero)
        dsc_ref[1:2, :] += jnp.where(is_ctx, zero, dsc)
        dxhat = dh * (gv * (1.0 + sc))
        dx = rstd * (dxhat - xhat * jnp.mean(dxhat * xhat, axis=-1, keepdims=True))
        res = dres_ref[...]
        if res_is_latent_only:
            res = jnp.where(is_ctx, jnp.zeros_like(res), res)
        dx_ref[...] = dx + res

    row = pl.BlockSpec((tm, d), lambda i: (i, 0))
    if res_is_latent_only:
        res_spec = pl.BlockSpec((tm, d), lambda i: (jnp.maximum(i - nctx, 0), 0))
    else:
        res_spec = row
    vec = lambda r: pl.BlockSpec((r, d), lambda i: (0, 0))
    dx_spec = pl.BlockSpec((tm, d), lambda i: (jnp.maximum(i - nctx, 0), 0)) if dx_latent_only else row
    return pl.pallas_call(
        body, name="normmod_bwd", grid=(t // tm,),
        in_specs=[row, row, vec(1), vec(2), res_spec],
        out_specs=[dx_spec, vec(1), vec(2), vec(2)],
        out_shape=[jax.ShapeDtypeStruct((t - cx if dx_latent_only else t, d), F32), jax.ShapeDtypeStruct((1, d), F32),
                   jax.ShapeDtypeStruct((2, d), F32), jax.ShapeDtypeStruct((2, d), F32)],
        compiler_params=_cp(),
    )(xs, dh, g, scale, dres)


def post_fwd_norm(xs, y, pg, gate, g_next, shift_next, scale_next, cx):
    t, d = xs.shape
    tm = _row_block(cx)
    nctx = cx // tm

    def body(x_ref, y_ref, pg_ref, gate_ref, g_ref, sh_ref, sc_ref, o_ref, h_ref):
        is_ctx = pl.program_id(0) < nctx
        pick = lambda ref: jnp.where(is_ctx, ref[0:1, :], ref[1:2, :])
        y = y_ref[...]
        rstd = lax.rsqrt(jnp.mean(y * y, axis=-1, keepdims=True) + EPS)
        x = x_ref[...] + pick(gate_ref) * ((y * rstd) * pg_ref[...])
        o_ref[...] = x
        rstd = lax.rsqrt(jnp.mean(x * x, axis=-1, keepdims=True) + EPS)
        h_ref[...] = ((x * rstd) * g_ref[...] * (1.0 + pick(sc_ref)) + pick(sh_ref)).astype(BF16)

    row = pl.BlockSpec((tm, d), lambda i: (i, 0))
    vec = lambda r: pl.BlockSpec((r, d), lambda i: (0, 0))
    return pl.pallas_call(
        body, name="post_fwd_norm", grid=(t // tm,),
        in_specs=[row, row, vec(1), vec(2), vec(1), vec(2), vec(2)], out_specs=[row, row],
        out_shape=[jax.ShapeDtypeStruct((t, d), F32), jax.ShapeDtypeStruct((t, d), BF16)],
        compiler_params=_cp(),
    )(xs, y, pg, gate, g_next, shift_next, scale_next)


def post_loss(xs, y, pg, gate, target, cx):
    t, d = xs.shape
    n = y.shape[0]
    tm = _row_block(cx)
    nctx = cx // tm

    def body(x_ref, y_ref, pg_ref, gate_ref, tgt_ref, sq_ref, dx_ref, dy_ref, dgate_ref, dpg_ref):
        @pl.when(pl.program_id(0) == 0)
        def _():
            sq_ref[...] = jnp.zeros_like(sq_ref)
            dgate_ref[...] = jnp.zeros_like(dgate_ref)
            dpg_ref[...] = jnp.zeros_like(dpg_ref)

        y = y_ref[...]
        pgv = pg_ref[...]
        gt = gate_ref[1:2, :]
        rstd = lax.rsqrt(jnp.mean(y * y, axis=-1, keepdims=True) + EPS)
        yhat = y * rstd
        err = x_ref[...] + gt * (yhat * pgv) - tgt_ref[...]
        sq_ref[...] += jnp.sum(err * err)
        dx = err * (1.0 / d)
        dx_ref[...] = dx
        dxy = dx * yhat
        dgate_ref[1:2, :] += jnp.sum(dxy * pgv, axis=0, keepdims=True)
        dpg_ref[...] += jnp.sum(dxy * gt, axis=0, keepdims=True)
        dyhat = dx * (gt * pgv)
        dy_ref[...] = (rstd * (dyhat - yhat * jnp.mean(dyhat * yhat, axis=-1, keepdims=True))).astype(BF16)

    row = pl.BlockSpec((tm, d), lambda i: (i, 0))
    xrow = pl.BlockSpec((tm, d), lambda i: (i + nctx, 0))
    vec = lambda r: pl.BlockSpec((r, d), lambda i: (0, 0))
    return pl.pallas_call(
        body, name="post_loss", grid=(n // tm,),
        in_specs=[xrow, row, vec(1), vec(2), row],
        out_specs=[pl.BlockSpec((8, 128), lambda i: (0, 0)), row, row, vec(2), vec(1)],
        out_shape=[jax.ShapeDtypeStruct((8, 128), F32), jax.ShapeDtypeStruct((n, d), F32),
                   jax.ShapeDtypeStruct((n, d), BF16), jax.ShapeDtypeStruct((2, d), F32),
                   jax.ShapeDtypeStruct((1, d), F32)],
        compiler_params=_cp(),
    )(xs, y, pg, gate, target)


def post_bwd(dxo, y, pg, gate, cx):
    m, d = y.shape
    tm = _row_block(cx)
    nctx = cx // tm

    def body(dx_ref, y_ref, pg_ref, gate_ref, dy_ref, dgate_ref, dpg_ref):
        i = pl.program_id(0)
        is_ctx = i < nctx

        @pl.when(i == 0)
        def _():
            dgate_ref[...] = jnp.zeros_like(dgate_ref)
            dpg_ref[...] = jnp.zeros_like(dpg_ref)

        y = y_ref[...]
        dx = dx_ref[...]
        pgv = pg_ref[...]
        rstd = lax.rsqrt(jnp.mean(y * y, axis=-1, keepdims=True) + EPS)
        yhat = y * rstd
        gt = jnp.where(is_ctx, gate_ref[0:1, :], gate_ref[1:2, :])
        dxy = dx * yhat
        dgt = jnp.sum(dxy * pgv, axis=0, keepdims=True)
        zero = jnp.zeros_like(dgt)
        dgate_ref[0:1, :] += jnp.where(is_ctx, dgt, zero)
        dgate_ref[1:2, :] += jnp.where(is_ctx, zero, dgt)
        dpg_ref[...] += jnp.sum(dxy * gt, axis=0, keepdims=True)
        dyhat = dx * (gt * pgv)
        dy = rstd * (dyhat - yhat * jnp.mean(dyhat * yhat, axis=-1, keepdims=True))
        dy_ref[...] = dy.astype(BF16)

    row = pl.BlockSpec((tm, d), lambda i: (i, 0))
    vec = lambda r: pl.BlockSpec((r, d), lambda i: (0, 0))
    return pl.pallas_call(
        body, name="post_bwd", grid=(m // tm,),
        in_specs=[row, row, vec(1), vec(2)], out_specs=[row, vec(2), vec(1)],
        out_shape=[jax.ShapeDtypeStruct((m, d), BF16), jax.ShapeDtypeStruct((2, d), F32),
                   jax.ShapeDtypeStruct((1, d), F32)],
        compiler_params=_cp(),
    )(dxo, y, pg, gate)


def _split_rows(m):
    for cand in (1024, 768, 512, 384, 256, 128):
        if m % cand == 0 and m // cand >= 2:
            return cand
    return m


def mm_nn(a, w3, sec, tn, name):
    m, k = a.shape
    q, _, ws = w3.shape
    n = q * ws
    tpq, tps = ws // tn, sec // tn
    tm = next(c for c in (768, 512, 256, 128) if m % c == 0)

    def body(a_ref, w_ref, o_ref):
        w = w_ref[...]

        def step(i, carry):
            rows = pl.ds(pl.multiple_of(i * tm, tm), tm)
            o_ref[rows, :] = _dot(a_ref[rows, :], w)
            return carry

        lax.fori_loop(0, m // tm, step, 0)

    return pl.pallas_call(
        body, name=name, grid=(n // tn,),
        in_specs=[pl.BlockSpec((m, k), lambda j: (0, 0)),
                  pl.BlockSpec((None, k, tn), lambda j: (j // tpq, 0, j % tpq))],
        out_specs=pl.BlockSpec((None, m, tn), lambda j: (j // tps, 0, j % tps)),
        out_shape=jax.ShapeDtypeStruct((n // sec, m, sec), F32), compiler_params=_cp(),
    )(a, w3)


def _two_stacks(a3, b3, tn):
    sec = a3.shape[2]
    tps = sec // tn
    n1 = a3.shape[0] * tps
    first = lambda j: (jnp.minimum(j, n1 - 1) // tps, jnp.minimum(j, n1 - 1) % tps)
    second = lambda j: (jnp.maximum(j - n1, 0) // tps, jnp.maximum(j - n1, 0) % tps)
    return n1, first, second


def mm_nt(a3, b3, w3, tn, name):
    if b3 is None:
        b3 = a3
    _, m, sec = a3.shape
    q, k, ws = w3.shape
    n = q * ws
    tpq = ws // tn
    mb = _split_rows(m)
    n1, first, second = _two_stacks(a3, b3, tn)

    def body(a_ref, b_ref, w_ref, o_ref):
        j = pl.program_id(1)

        @pl.when(j == 0)
        def _():
            o_ref[...] = jnp.zeros_like(o_ref)

        @pl.when(j < n1)
        def _():
            o_ref[...] += _dot(a_ref[...], w_ref[...], NT)

        @pl.when(j >= n1)
        def _():
            o_ref[...] += _dot(b_ref[...], w_ref[...], NT)

    return pl.pallas_call(
        body, name=name, grid=(m // mb, n // tn),
        in_specs=[pl.BlockSpec((None, mb, tn), lambda i, j: (first(j)[0], i, first(j)[1])),
                  pl.BlockSpec((None, mb, tn), lambda i, j: (second(j)[0], i, second(j)[1])),
                  pl.BlockSpec((None, k, tn), lambda i, j: (j // tpq, 0, j % tpq))],
        out_specs=pl.BlockSpec((mb, k), lambda i, j: (i, 0)),
        out_shape=jax.ShapeDtypeStruct((m, k), F32), compiler_params=_cp(),
    )(a3, b3, w3)


def mm_tn(a, b3, c3, ws, tn, name):
    m, k = a.shape
    sec = b3.shape[2]
    n = (b3.shape[0] + (0 if c3 is None else c3.shape[0])) * sec
    if c3 is None:
        c3 = b3
    tpq = ws // tn
    kb = 256 if k % 256 == 0 else 128
    n1, first, second = _two_stacks(b3, c3, tn)

    def body(a_ref, b_ref, c_ref, o_ref):
        def product(rhs_ref):
            rhs = rhs_ref[...]
            for i in range(k // kb):
                o_ref[i * kb:(i + 1) * kb, :] = _dot(a_ref[:, i * kb:(i + 1) * kb], rhs, TN).astype(BF16)

        @pl.when(pl.program_id(0) < n1)
        def _():
            product(b_ref)

        @pl.when(pl.program_id(0) >= n1)
        def _():
            product(c_ref)

    return pl.pallas_call(
        body, name=name, grid=(n // tn,),
        in_specs=[pl.BlockSpec((m, k), lambda j: (0, 0)),
                  pl.BlockSpec((None, m, tn), lambda j: (first(j)[0], 0, first(j)[1])),
                  pl.BlockSpec((None, m, tn), lambda j: (second(j)[0], 0, second(j)[1]))],
        out_specs=pl.BlockSpec((None, k, tn), lambda j: (j // tpq, 0, j % tpq)),
        out_shape=jax.ShapeDtypeStruct((n // ws, k, ws), BF16), compiler_params=_cp(),
    )(a, b3, c3)


POOL_REACH = 8 << GRID_W_LOG2


def _token_parts(tok, cx):
    lat = tok - cx
    return tok < cx, lat >> GRID_W_LOG2, lat & ((1 << GRID_W_LOG2) - 1)


def _pool_mask(gi, row0, col0, tm, ncols, cx, transposed):
    half = jnp.left_shift(1, gi)
    r = lax.broadcasted_iota(jnp.int32, (tm, 1), 0) + row0
    c = lax.broadcasted_iota(jnp.int32, (1, ncols), 1) + col0
    out_tok, src_tok = (c, r) if transposed else (r, c)
    o_ctx, o_row, o_col = _token_parts(out_tok, cx)
    s_ctx, s_row, s_col = _token_parts(src_tok, cx)

    def inside(o, s):
        return (s >= o - half) & (s <= o + half - 1)

    ctx_hit = o_ctx & s_ctx & inside(out_tok, src_tok)
    lat_hit = (~o_ctx) & (~s_ctx) & inside(o_row, s_row) & inside(o_col, s_col)
    return jnp.where(ctx_hit | lat_hit, 1.0, 0.0).astype(BF16)


def _pool_inv_count(gi, row0, tm, cx, seq):
    half = jnp.left_shift(1, gi)
    r = lax.broadcasted_iota(jnp.int32, (tm, 1), 0) + row0
    is_ctx, row, col = _token_parts(r, cx)

    def count(pos, size):
        return jnp.minimum(pos + half - 1, size - 1) - jnp.maximum(pos - half, 0) + 1

    cnt = jnp.where(is_ctx, count(r, cx), count(row, seq >> GRID_W_LOG2) * count(col, 1 << GRID_W_LOG2))
    return 1.0 / cnt.astype(F32)


def _lat_band(tm):
    side = POOL_REACH // tm
    return side, 2 * side + 1


def _lat_mask(gi, tm, cx, transposed):
    side, band = _lat_band(tm)
    return _pool_mask(gi, cx + side * tm, cx, tm, band * tm, cx, transposed)


def _store_padded_lat(dst_ref, lat, tm):
    side, _ = _lat_band(tm)
    seq = lat.shape[0]
    zeros = jnp.zeros((side * tm, lat.shape[1]), dst_ref.dtype)
    dst_ref[0:side * tm, :] = zeros
    dst_ref[side * tm + seq:, :] = zeros
    dst_ref[side * tm:side * tm + seq, :] = lat.astype(dst_ref.dtype)


def mix_a_fwd(z0, pool_w, pool_scale, cx):
    _, t, half_d = z0.shape
    g = half_d // N_POOL
    seq = t - cx
    tm = _row_block(cx)
    side, band = _lat_band(tm)

    def body(v_ref, ag_ref, w_ref, sc_ref, u_ref, vlat_ref, mask_ref):
        gi = pl.program_id(0)
        w = w_ref[...].astype(BF16)
        sc = sc_ref[...]
        _store_padded_lat(vlat_ref, v_ref[cx:, :], tm)
        mask_ref[...] = _lat_mask(gi, tm, cx, False)

        def finish(row0, window_sum):
            rows = pl.ds(row0, tm)
            pooled = window_sum * _pool_inv_count(gi, row0, tm, cx, seq) - v_ref[rows, :]
            mixed = _dot(pooled.astype(BF16), w) * sc
            u_ref[rows, :] = (mixed * _silu(ag_ref[rows, :])).astype(BF16)

        vctx = v_ref[0:cx, :].astype(BF16)
        for i in range(cx // tm):
            finish(i * tm, _dot(_pool_mask(gi, i * tm, 0, tm, cx, cx, False), vctx))

        def step(j, carry):
            src = vlat_ref[pl.ds(pl.multiple_of(j * tm, tm), band * tm), :]
            finish(pl.multiple_of(cx + j * tm, tm), _dot(mask_ref[...], src))
            return carry

        lax.fori_loop(0, seq // tm, step, 0)

    sec = lambda s: pl.BlockSpec((None, t, g), lambda j: (s, 0, j))
    return pl.pallas_call(
        body, name="mix_a_fwd", grid=(N_POOL,),
        in_specs=[sec(0), sec(1), pl.BlockSpec((None, g, g), lambda j: (j, 0, 0)),
                  pl.BlockSpec((1, g), lambda j: (0, j))],
        out_specs=pl.BlockSpec((t, g), lambda j: (0, j)),
        out_shape=jax.ShapeDtypeStruct((t, 2 * half_d), BF16),
        scratch_shapes=[pltpu.VMEM((seq + 2 * side * tm, g), BF16), pltpu.VMEM((tm, band * tm), BF16)],
        compiler_params=_cp(),
    )(z0, z0, pool_w, pool_scale)


def mix_a_bwd(z0, du, pool_w, pool_scale, cx):
    _, t, half_d = z0.shape
    g = half_d // N_POOL
    seq = t - cx
    tm = _row_block(cx)
    gq = g // 4
    side, band = _lat_band(tm)

    def body(v_ref, ag_ref, du_ref, w_ref, sc_ref, dz_ref, dw_ref, dsc_ref,
             vlat_ref, mask_ref, pooled_ref, dmx_ref, dpl_ref, wlat_ref, wctx_ref):
        gi = pl.program_id(0)
        w = w_ref[...].astype(BF16)
        sc = sc_ref[...]
        _store_padded_lat(vlat_ref, v_ref[cx:, :], tm)
        _store_padded_lat(wlat_ref, jnp.zeros((seq, g), BF16), tm)
        mask_ref[...] = _lat_mask(gi, tm, cx, False)

        def first(row0, window_sum, weighted_ref, weighted_row0):
            rows = pl.ds(row0, tm)
            inv = _pool_inv_count(gi, row0, tm, cx, seq)
            pooled = (window_sum * inv - v_ref[rows, :]).astype(BF16)
            pooled_ref[rows, :] = pooled
            mixed = _dot(pooled, w)
            ag = ag_ref[rows, :]
            duv = du_ref[rows, :]
            dz_ref[1, rows, :] = (duv * (mixed * sc) * _dsilu(ag)).astype(BF16)
            dms = duv * _silu(ag)
            dmixed = (dms * sc).astype(BF16)
            dmx_ref[rows, :] = dmixed
            dpooled = _dot(dmixed, w, NT)
            dpl_ref[rows, :] = dpooled
            weighted_ref[pl.ds(weighted_row0, tm), :] = (dpooled * inv).astype(BF16)
            return jnp.sum(dms * mixed, axis=0, keepdims=True)

        dsc = jnp.zeros((1, g), F32)
        vctx = v_ref[0:cx, :].astype(BF16)
        for i in range(cx // tm):
            dsc += first(i * tm, _dot(_pool_mask(gi, i * tm, 0, tm, cx, cx, False), vctx), wctx_ref, i * tm)

        def first_lat(j, acc):
            src = vlat_ref[pl.ds(pl.multiple_of(j * tm, tm), band * tm), :]
            return acc + first(pl.multiple_of(cx + j * tm, tm), _dot(mask_ref[...], src),
                               wlat_ref, pl.multiple_of((side + j) * tm, tm))

        dsc_ref[...] = lax.fori_loop(0, seq // tm, first_lat, dsc)
        dw = _dot(pooled_ref[...], dmx_ref[...], TN)
        for qi in range(4):
            dw_ref[qi] = dw[qi * gq:(qi + 1) * gq, :]

        wctx = wctx_ref[...]
        for i in range(cx // tm):
            rows = pl.ds(i * tm, tm)
            dz_ref[0, rows, :] = (_dot(_pool_mask(gi, i * tm, 0, tm, cx, cx, True), wctx)
                                  - dpl_ref[rows, :]).astype(BF16)
        mask_ref[...] = _lat_mask(gi, tm, cx, True)

        def second_lat(j, carry):
            rows = pl.ds(pl.multiple_of(cx + j * tm, tm), tm)
            src = wlat_ref[pl.ds(pl.multiple_of(j * tm, tm), band * tm), :]
            dz_ref[0, rows, :] = (_dot(mask_ref[...], src) - dpl_ref[rows, :]).astype(BF16)
            return carry

        lax.fori_loop(0, seq // tm, second_lat, 0)

    sec = lambda s: pl.BlockSpec((None, t, g), lambda j: (s, 0, j))
    padded = pltpu.VMEM((seq + 2 * side * tm, g), BF16)
    return pl.pallas_call(
        body, name="mix_a_bwd", grid=(N_POOL,),
        in_specs=[sec(0), sec(1), pl.BlockSpec((t, g), lambda j: (0, j)),
                  pl.BlockSpec((None, g, g), lambda j: (j, 0, 0)),
                  pl.BlockSpec((1, g), lambda j: (0, j))],
        out_specs=[pl.BlockSpec((2, t, g), lambda j: (0, 0, j)),
                   pl.BlockSpec((4, None, gq, g), lambda j: (0, j, 0, 0)),
                   pl.BlockSpec((1, g), lambda j: (0, j))],
        out_shape=[jax.ShapeDtypeStruct((2, t, half_d), BF16),
                   jax.ShapeDtypeStruct((4, N_POOL, gq, g), F32),
                   jax.ShapeDtypeStruct((1, half_d), F32)],
        scratch_shapes=[padded, pltpu.VMEM((tm, band * tm), BF16), pltpu.VMEM((t, g), BF16),
                        pltpu.VMEM((t, g), BF16), pltpu.VMEM((t, g), F32), padded, pltpu.VMEM((cx, g), BF16)],
        compiler_params=_cp(),
    )(z0, z0, du, pool_w, pool_scale)


def _conv_masks(t, cx):
    r = lax.broadcasted_iota(jnp.int32, (t, 1), 0)
    has_prev = jnp.where((r == 0) | (r == cx), 0.0, 1.0)
    has_next = jnp.where((r == cx - 1) | (r == t - 1), 0.0, 1.0)
    return has_prev, has_next


def mix_b_fwd(z0, conv_w, conv_b, u, cx):
    _, t, half_d = z0.shape
    gb = 128
    off = half_d // gb

    def body(bx_ref, bb_ref, bc_ref, bg_ref, w_ref, b_ref, _, u_ref):
        has_prev, has_next = _conv_masks(t, cx)
        tt = bc_ref[...] * bx_ref[...]
        prev = pltpu.roll(tt, 1, 0) * has_prev
        nxt = pltpu.roll(tt, t - 1, 0) * has_next
        cv = prev * w_ref[0:1, :] + tt * w_ref[1:2, :] + nxt * w_ref[2:3, :] + b_ref[...]
        u_ref[...] = (bb_ref[...] * cv * _silu(bg_ref[...])).astype(BF16)

    sec = lambda s: pl.BlockSpec((None, t, gb), lambda j: (s, 0, j))
    return pl.pallas_call(
        body, name="mix_b_fwd", grid=(half_d // gb,),
        in_specs=[sec(2), sec(3), sec(4), sec(5), pl.BlockSpec((3, gb), lambda j: (0, j)),
                  pl.BlockSpec((1, gb), lambda j: (0, j)), ANY],
        out_specs=pl.BlockSpec((t, gb), lambda j: (0, j + off)),
        out_shape=jax.ShapeDtypeStruct((t, 2 * half_d), BF16), input_output_aliases={6: 0},
        compiler_params=_cp(),
    )(z0, z0, z0, z0, conv_w, conv_b, u)


def mix_b_bwd(z0, du, conv_w, conv_b, cx):
    _, t, half_d = z0.shape
    gb = 128
    off = half_d // gb

    def body(bx_ref, bb_ref, bc_ref, bg_ref, du_ref, w_ref, b_ref, dz_ref, dw_ref, db_ref):
        has_prev, has_next = _conv_masks(t, cx)
        bx, bb, bc, bg = bx_ref[...], bb_ref[...], bc_ref[...], bg_ref[...]
        duv = du_ref[...]
        tt = bc * bx
        prev = pltpu.roll(tt, 1, 0) * has_prev
        nxt = pltpu.roll(tt, t - 1, 0) * has_next
        w0, w1, w2 = w_ref[0:1, :], w_ref[1:2, :], w_ref[2:3, :]
        cv = prev * w0 + tt * w1 + nxt * w2 + b_ref[...]
        sg = _silu(bg)
        dz_ref[1] = (duv * cv * sg).astype(BF16)
        dz_ref[3] = (duv * bb * cv * _dsilu(bg)).astype(BF16)
        dcv = duv * bb * sg
        dw_ref[0:1, :] = jnp.sum(dcv * prev, axis=0, keepdims=True)
        dw_ref[1:2, :] = jnp.sum(dcv * tt, axis=0, keepdims=True)
        dw_ref[2:3, :] = jnp.sum(dcv * nxt, axis=0, keepdims=True)
        db_ref[...] = jnp.sum(dcv, axis=0, keepdims=True)
        dt = (pltpu.roll(dcv * has_prev, t - 1, 0) * w0 + dcv * w1
              + pltpu.roll(dcv * has_next, 1, 0) * w2)
        dz_ref[0] = (dt * bc).astype(BF16)
        dz_ref[2] = (dt * bx).astype(BF16)

    sec = lambda s: pl.BlockSpec((None, t, gb), lambda j: (s, 0, j))
    return pl.pallas_call(
        body, name="mix_b_bwd", grid=(half_d // gb,),
        in_specs=[sec(2), sec(3), sec(4), sec(5), pl.BlockSpec((t, gb), lambda j: (0, j + off)),
                  pl.BlockSpec((3, gb), lambda j: (0, j)), pl.BlockSpec((1, gb), lambda j: (0, j))],
        out_specs=[pl.BlockSpec((4, t, gb), lambda j: (0, 0, j)),
                   pl.BlockSpec((3, gb), lambda j: (0, j)), pl.BlockSpec((1, gb), lambda j: (0, j))],
        out_shape=[jax.ShapeDtypeStruct((4, t, half_d), BF16),
                   jax.ShapeDtypeStruct((3, half_d), F32), jax.ShapeDtypeStruct((1, half_d), F32)],
        compiler_params=_cp(),
    )(z0, z0, z0, z0, du, conv_w, conv_b)


def _lower_bound(lbl_ref, d):
    l0, l1, l2 = lbl_ref[d, 0:1, :], lbl_ref[d, 1:2, :], lbl_ref[d, 2:3, :]
    mx = jnp.maximum(jnp.maximum(l0, l1), l2)
    e0, e1, e2 = jnp.exp(l0 - mx), jnp.exp(l1 - mx), jnp.exp(l2 - mx)
    inv = 1.0 / (e0 + e1 + e2)
    return (e0 + e1) * inv, (e0 * inv, e1 * inv, e2 * inv)


def _chunk_consts(d):
    r = lax.broadcasted_iota(jnp.int32, (CHUNK, CHUNK), 0)
    c = lax.broadcasted_iota(jnp.int32, (CHUNK, CHUNK), 1)
    keep = (c <= r) if d == 0 else (c >= r)
    return jnp.where(keep, 1.0, 0.0).astype(F32), keep


def _chunk_of_step(s, d, nc, ncc):
    if d == 0:
        return s
    return jnp.where(s < ncc, ncc - 1 - s, nc - 1 + ncc - s)


def _gates(z, lbv):
    e = jnp.exp(-jnp.abs(z))
    r = 1.0 / (1.0 + e)
    er = e * r
    pos = z >= 0.0
    sig = jnp.where(pos, r, er)
    nsig = jnp.where(pos, er, r)
    return sig, nsig, lbv + (1.0 - lbv) * sig


def _split3(x):
    hi = x.astype(BF16)
    r1 = x - hi.astype(F32)
    mid = r1.astype(BF16)
    lo = (r1 - mid.astype(F32)).astype(BF16)
    return jnp.concatenate([hi, mid, lo], axis=1)


def _cumsum_chunk(cum, x):
    y = _dot(cum, _split3(x))
    return y[:, :HEAD] + y[:, HEAD:2 * HEAD] + y[:, 2 * HEAD:]


def _chunk_rows(n):
    return pl.ds(pl.multiple_of(n * CHUNK, CHUNK), CHUNK)


def _group(nc, prefer=(4, 3, 2, 1)):
    return next(u for u in prefer if nc % u == 0)


WIDE_GROUP = (12, 6, 4, 3, 2, 1)


def _decay_pass(lf_ref, bc_ref, dec_ref, cum, nc):
    grp = _group(nc, WIDE_GROUP)

    def step(m, carry):
        ns = [m * grp + u for u in range(grp)]
        lfc = [lf_ref[_chunk_rows(n), :] for n in ns]
        bc = [_cumsum_chunk(cum, x) for x in lfc]
        for u, n in enumerate(ns):
            bc_ref[_chunk_rows(n), :] = bc[u]
            dec_ref[n] = jnp.broadcast_to(jnp.exp(jnp.sum(lfc[u], axis=0, keepdims=True)), (8, HEAD))
        return carry

    lax.fori_loop(0, nc // grp, step, 0)


def hgrn_fwd(z1, lbl, onorm, cx):
    _, t, d = z1.shape
    seq = t - cx
    nc, ncc = t // CHUNK, cx // CHUNK

    grp, sgrp = _group(nc, (9, 6, 4, 3, 2, 1)), _group(nc, WIDE_GROUP)

    def body(zf_ref, zb_ref, v_ref, q_ref, g_ref, lbl_ref, on_ref, o_ref, r_ref, bcs_ref, ks_ref, decs_ref,
             lf_ref, k_ref, bc_ref, dec_ref, qd_ref, ki_ref, oacc_ref, ds_ref):
        for dr, z_ref in ((0, zf_ref), (1, zb_ref)):
            lbv, _ = _lower_bound(lbl_ref, dr)
            _, nsig, f = _gates(z_ref[...], lbv)
            lf_ref[...] = jnp.log(f)
            k_ref[...] = (1.0 - lbv) * nsig
            cum, keep = _chunk_consts(dr)
            _decay_pass(lf_ref, bc_ref, dec_ref, cum.astype(BF16), nc)
            bc = bc_ref[...]
            bcs_ref[dr] = bc
            ks_ref[dr] = k_ref[...]
            decs_ref[dr] = dec_ref[...]
            qd_ref[...] = (q_ref[...] * jnp.exp(bc)).astype(BF16)
            ki_ref[...] = (k_ref[...] * jnp.exp(-bc)).astype(BF16)

            def local_step(m, carry, dr=dr, keep=keep):
                ns = [m * grp + u for u in range(grp)]
                rows = [_chunk_rows(n) for n in ns]
                qd = [qd_ref[r, :] for r in rows]
                ki = [ki_ref[r, :] for r in rows]
                vc = [v_ref[r, :].astype(BF16) for r in rows]
                sc = [_dot(qd[u], ki[u], NT) for u in range(grp)]
                inc = [_dot(vc[u], ki[u], TN) for u in range(grp)]
                a = [jnp.where(keep, s, 0.0).astype(BF16) for s in sc]
                intra = [_dot(a[u], vc[u]) for u in range(grp)]
                for u in range(grp):
                    ds_ref[ns[u]] = inc[u] * dec_ref[ns[u]][0:1, :]
                    if dr == 0:
                        oacc_ref[rows[u], :] = intra[u]
                    else:
                        oacc_ref[rows[u], :] += intra[u]
                return carry

            lax.fori_loop(0, nc // grp, local_step, 0)

            def state_step(m, st, dr=dr):
                ns = [_chunk_of_step(m * sgrp + u, dr, nc, ncc) for u in range(sgrp)]
                rows = [_chunk_rows(n) for n in ns]
                sts = []
                for n in ns:
                    sts.append(st.astype(BF16))
                    st = st * dec_ref[n][0:1, :] + ds_ref[n]
                inter = [_dot(qd_ref[rows[u], :], sts[u], NT) for u in range(sgrp)]
                for u in range(sgrp):
                    oacc_ref[rows[u], :] += inter[u]
                return st

            lax.fori_loop(0, nc // sgrp, state_step, jnp.zeros((HEAD, HEAD), F32))

        o = oacc_ref[cx:, :]
        o_ref[...] = o
        rstd = lax.rsqrt(jnp.mean(o * o, axis=-1, keepdims=True) + EPS)
        r_ref[...] = (o * rstd * on_ref[...] * _silu(g_ref[cx:, :])).astype(BF16)

    sec = lambda s: pl.BlockSpec((None, t, HEAD), lambda h: (s, 0, h))
    col = pl.BlockSpec((seq, HEAD), lambda h: (0, h))
    tf32, tb16 = pltpu.VMEM((t, HEAD), F32), pltpu.VMEM((t, HEAD), BF16)
    return pl.pallas_call(
        body, name="hgrn_fwd", grid=(d // HEAD,),
        in_specs=[sec(0), sec(1), sec(2), sec(3), sec(4),
                  pl.BlockSpec((2, 3, HEAD), lambda h: (0, 0, h)), pl.BlockSpec((1, HEAD), lambda h: (0, h))],
        out_specs=[col, col, pl.BlockSpec((2, t, HEAD), lambda h: (0, 0, h)),
                   pl.BlockSpec((2, t, HEAD), lambda h: (0, 0, h)),
                   pl.BlockSpec((2, nc, 8, HEAD), lambda h: (0, 0, 0, h))],
        out_shape=[jax.ShapeDtypeStruct((seq, d), F32), jax.ShapeDtypeStruct((seq, d), BF16),
                   jax.ShapeDtypeStruct((2, t, d), F32), jax.ShapeDtypeStruct((2, t, d), F32),
                   jax.ShapeDtypeStruct((2, nc, 8, d), F32)],
        scratch_shapes=[tf32, tf32, tf32, pltpu.VMEM((nc, 8, HEAD), F32), tb16, tb16, tf32,
                        pltpu.VMEM((nc, HEAD, HEAD), F32)],
        compiler_params=_cp(),
    )(z1, z1, z1, z1, z1, lbl, onorm)


def hgrn_bwd(z1, lbl, onorm, o, dr_out, bcs, ks, decs, cx):
    _, t, d = z1.shape
    seq = t - cx
    nc, ncc = t // CHUNK, cx // CHUNK

    grp2, grp = _group(nc, (9, 6, 4, 3, 2, 1)), _group(nc, (12, 9, 6, 4, 3, 2, 1))

    def body(zf_ref, zb_ref, v_ref, q_ref, g_ref, lbl_ref, on_ref, o_ref, dr_ref, bcs_ref, ks_ref, decs_ref,
             dz_ref, don_ref, dlb_ref,
             qd_ref, ki_ref, do_ref, dqd_ref, dki_ref, dq_ref, dv_ref, ds_ref, dsl_ref):
        o = o_ref[...]
        g = g_ref[cx:, :]
        drv = dr_ref[...]
        onv = on_ref[...]
        rstd = lax.rsqrt(jnp.mean(o * o, axis=-1, keepdims=True) + EPS)
        ohat = o * rstd
        sg = _silu(g)
        don_ref[...] = jnp.sum(drv * ohat * sg, axis=0, keepdims=True)
        dz_ref[4, :cx, :] = jnp.zeros((cx, HEAD), BF16)
        dz_ref[4, cx:, :] = (drv * ohat * onv * _dsilu(g)).astype(BF16)
        dohat = drv * onv * sg
        do_ref[:cx, :] = jnp.zeros((cx, HEAD), BF16)
        do_ref[cx:, :] = (rstd * (dohat - ohat * jnp.mean(dohat * ohat, axis=-1, keepdims=True))).astype(BF16)

        for dr, z_ref in ((0, zf_ref), (1, zb_ref)):
            lbv, _ = _lower_bound(lbl_ref, dr)
            k_ref, bc_ref, dec_ref = ks_ref.at[dr], bcs_ref.at[dr], decs_ref.at[dr]
            _, keep = _chunk_consts(dr)
            cum_t = _chunk_consts(1 - dr)[0].astype(BF16)
            bc = bc_ref[...]
            qd_ref[...] = (q_ref[...] * jnp.exp(bc)).astype(BF16)
            ki_ref[...] = (k_ref[...] * jnp.exp(-bc)).astype(BF16)

            def local_step(m, carry, dr=dr, keep=keep):
                ns = [m * grp + u for u in range(grp)]
                rows = [_chunk_rows(n) for n in ns]
                rng = range(grp)
                qd = [qd_ref[r, :] for r in rows]
                ki = [ki_ref[r, :] for r in rows]
                doc = [do_ref[r, :] for r in rows]
                vc = [v_ref[r, :].astype(BF16) for r in rows]
                sc = [_dot(qd[u], ki[u], NT) for u in rng]
                dsc = [_dot(doc[u], vc[u], NT) for u in rng]
                inc = [_dot(vc[u], ki[u], TN) for u in rng]
                dinc = [_dot(doc[u], qd[u], TN) for u in rng]
                a = [jnp.where(keep, s, 0.0).astype(BF16) for s in sc]
                da = [jnp.where(keep, s, 0.0).astype(BF16) for s in dsc]
                dqd = [_dot(da[u], ki[u]) for u in rng]
                dki = [_dot(da[u], qd[u], TN) for u in rng]
                dv = [_dot(a[u], doc[u], TN) for u in rng]
                for u in rng:
                    ds_ref[ns[u]] = inc[u] * dec_ref[ns[u]][0:1, :]
                    dsl_ref[ns[u]] = dinc[u]
                    dqd_ref[rows[u], :] = dqd[u]
                    dki_ref[rows[u], :] = dki[u]
                    if dr == 0:
                        dv_ref[rows[u], :] = dv[u]
                    else:
                        dv_ref[rows[u], :] += dv[u]
                return carry

            lax.fori_loop(0, nc // grp, local_step, 0)

            def state_step(s, st, dr=dr):
                n = _chunk_of_step(s, dr, nc, ncc)
                inc = ds_ref[n]
                ds_ref[n] = st
                return st * dec_ref[n][0:1, :] + inc

            lax.fori_loop(0, nc, state_step, jnp.zeros((HEAD, HEAD), F32), unroll=4)

            def dstate_step(s, dst, dr=dr):
                n = _chunk_of_step(nc - 1 - s, dr, nc, ncc)
                inc = dsl_ref[n]
                dsl_ref[n] = dst
                return inc + dst * dec_ref[n][0:1, :]

            lax.fori_loop(0, nc, dstate_step, jnp.zeros((HEAD, HEAD), F32), unroll=4)

            def grad_step(m, carry, dr=dr, cum_t=cum_t):
                ns = [m * grp2 + u for u in range(grp2)]
                rows = [_chunk_rows(n) for n in ns]
                rng = range(grp2)
                st0 = [ds_ref[n] for n in ns]
                dst = [dsl_ref[n] for n in ns]
                dstb = [x.astype(BF16) for x in dst]
                dec = [dec_ref[n][0:1, :] for n in ns]
                doc = [do_ref[r, :] for r in rows]
                vc = [v_ref[r, :].astype(BF16) for r in rows]
                e = [jnp.exp(bc_ref[r, :]) for r in rows]
                einv = [jnp.exp(-bc_ref[r, :]) for r in rows]
                qd = [q_ref[rows[u], :] * e[u] for u in rng]
                ki = [k_ref[rows[u], :] * einv[u] for u in rng]
                kd = [ki[u] * dec[u] for u in rng]
                dqd_st = [_dot(doc[u], st0[u].astype(BF16)) for u in rng]
                dkd = [_dot(vc[u], dstb[u]) for u in rng]
                dv_st = [_dot(kd[u].astype(BF16), dstb[u], NT) for u in rng]
                dqd = [dqd_ref[rows[u], :] + dqd_st[u] for u in rng]
                dki = [dki_ref[r, :] for r in rows]
                dbc = [dqd[u] * qd[u] - dki[u] * ki[u] - dkd[u] * kd[u] for u in rng]
                cs = [_cumsum_chunk(cum_t, x) for x in dbc]
                for u in rng:
                    ddec = jnp.sum(dst[u] * st0[u], axis=0, keepdims=True)
                    dbl = jnp.sum(dkd[u] * kd[u], axis=0, keepdims=True) + ddec * dec[u]
                    dv_ref[rows[u], :] += dv_st[u]
                    dqd_ref[rows[u], :] = cs[u] + dbl
                    dki_ref[rows[u], :] = dki[u] * einv[u] + dkd[u] * (einv[u] * dec[u])
                    if dr == 0:
                        dq_ref[rows[u], :] = dqd[u] * e[u]
                    else:
                        dq_ref[rows[u], :] += dqd[u] * e[u]
                return carry

            lax.fori_loop(0, nc // grp2, grad_step, 0)

            sig, nsig, f = _gates(z_ref[...], lbv)
            common = (dqd_ref[...] / f - dki_ref[...]) * nsig
            dz_ref[dr] = (common * ((1.0 - lbv) * sig)).astype(BF16)
            dlb_ref[dr:dr + 1, :] = jnp.sum(common, axis=0, keepdims=True)

        dz_ref[2] = dv_ref[...].astype(BF16)
        dz_ref[3] = dq_ref[...].astype(BF16)

    sec = lambda s: pl.BlockSpec((None, t, HEAD), lambda h: (s, 0, h))
    col = pl.BlockSpec((seq, HEAD), lambda h: (0, h))
    tf32, tb16 = pltpu.VMEM((t, HEAD), F32), pltpu.VMEM((t, HEAD), BF16)
    states = pltpu.VMEM((nc, HEAD, HEAD), F32)
    return pl.pallas_call(
        body, name="hgrn_bwd", grid=(d // HEAD,),
        in_specs=[sec(0), sec(1), sec(2), sec(3), sec(4),
                  pl.BlockSpec((2, 3, HEAD), lambda h: (0, 0, h)), pl.BlockSpec((1, HEAD), lambda h: (0, h)),
                  col, col, pl.BlockSpec((2, t, HEAD), lambda h: (0, 0, h)),
                  pl.BlockSpec((2, t, HEAD), lambda h: (0, 0, h)),
                  pl.BlockSpec((2, nc, 8, HEAD), lambda h: (0, 0, 0, h))],
        out_specs=[pl.BlockSpec((5, t, HEAD), lambda h: (0, 0, h)),
                   pl.BlockSpec((1, HEAD), lambda h: (0, h)), pl.BlockSpec((2, HEAD), lambda h: (0, h))],
        out_shape=[jax.ShapeDtypeStruct((5, t, d), BF16), jax.ShapeDtypeStruct((1, d), F32),
                   jax.ShapeDtypeStruct((2, d), F32)],
        scratch_shapes=[tb16, tb16, tb16, tf32, tf32, tf32, tf32, states, states],
        compiler_params=_cp(),
    )(z1, z1, z1, z1, z1, lbl, onorm, o, dr_out, bcs, ks, decs)


def _place():
    x, y, c = lax.axis_index("x"), lax.axis_index("y"), lax.axis_index("c")
    chips = [(1 - x, y), (x, 1 - y), (1 - x, 1 - y)]
    return x, y, c, chips


def _relay_chips():
    x, y, c, _ = _place()
    first = c == 0
    near = (jnp.where(first, 1 - x, x), jnp.where(first, y, 1 - y))
    far = (jnp.where(first, x, 1 - x), jnp.where(first, 1 - y, y))
    return near, far, (1 - x, 1 - y)


def allgather_shards(bufs):
    n = len(bufs)

    def body(*refs):
        outs = refs[n:2 * n]
        done_ref, send_sems, recv_sems = refs[2 * n:]
        done_ref[...] = jnp.zeros((8, 128), F32)
        x, y, c, _ = _place()
        me = (x, y, c)
        p = 2 * x + y
        near, far, diag = _relay_chips()
        half = [pl.ds(c * (s.shape[1] // 2), s.shape[1] // 2) for s in bufs]
        other = [pl.ds((1 - c) * (s.shape[1] // 2), s.shape[1] // 2) for s in bufs]
        slot = lambda chip: 2 * chip[0] + chip[1]

        def remote(i, k, ref, to):
            return pltpu.make_async_remote_copy(src_ref=ref, dst_ref=ref, send_sem=send_sems.at[6 * i + k],
                                                recv_sem=recv_sems.at[6 * i + k], device_id=to, device_id_type=MESH)

        sends = []

        def send(i, k, ref, to):
            cp = remote(i, k, ref, to)
            cp.start()
            sends.append(cp)

        for i in range(n):
            mine = outs[i].at[p, half[i]]
            send(i, 0, mine, (*near, c))
            send(i, 1, mine, (*far, c))
        for i in range(n):
            landed = outs[i].at[slot(near), half[i]]
            remote(i, 0, landed, me).wait_recv()
            send(i, 2, landed, (*far, c))
            send(i, 3, landed, (x, y, 1 - c))
        for i in range(n):
            landed = outs[i].at[slot(far), half[i]]
            remote(i, 1, landed, me).wait_recv()
            send(i, 4, landed, (x, y, 1 - c))
        for i in range(n):
            landed = outs[i].at[slot(diag), half[i]]
            remote(i, 2, landed, me).wait_recv()
            send(i, 5, landed, (x, y, 1 - c))
        for i in range(n):
            for k, chip in ((3, far), (4, near), (5, diag)):
                remote(i, k, outs[i].at[slot(chip), other[i]], me).wait_recv()
        for cp in sends:
            cp.wait_send()

    return pl.pallas_call(
        body, name="allgather_shards",
        in_specs=[ANY] * n, out_specs=[ANY] * n + [VMEM],
        out_shape=[jax.ShapeDtypeStruct(s.shape, s.dtype) for s in bufs] + [jax.ShapeDtypeStruct((8, 128), F32)],
        input_output_aliases={i: i for i in range(n)},
        scratch_shapes=[pltpu.SemaphoreType.DMA((6 * n,)), pltpu.SemaphoreType.DMA((6 * n,))],
        compiler_params=pltpu.CompilerParams(has_side_effects=True),
    )(*bufs)


def pair_sum(grad, got, chip_core):
    _, r, cc = grad.shape
    hr = r // 2
    tr = 256 if hr % 256 == 0 else hr
    nb = hr // tr

    def body(cc_ref, a_ref, b_ref, own_ref, sb_ref):
        s = a_ref[...].astype(F32) + b_ref[...].astype(F32)
        sb_ref[...] = s.astype(BF16)

        @pl.when(pl.program_id(1) == cc_ref[0])
        def _():
            own_ref[...] = s

    grid_spec = pltpu.PrefetchScalarGridSpec(
        num_scalar_prefetch=1, grid=(nb, 4),
        in_specs=[pl.BlockSpec((None, tr, cc), lambda i, qi, cc_ref: (qi, cc_ref[1] * nb + i, 0)),
                  pl.BlockSpec((None, tr, cc), lambda i, qi, cc_ref: (qi, i, 0))],
        out_specs=[pl.BlockSpec((tr, cc), lambda i, qi, cc_ref: (i, 0)),
                   pl.BlockSpec((None, tr, cc), lambda i, qi, cc_ref: (qi, i, 0))])
    return pl.pallas_call(
        body, name="pair_sum", grid_spec=grid_spec,
        out_shape=[jax.ShapeDtypeStruct((hr, cc), F32), jax.ShapeDtypeStruct((4, hr, cc), BF16)],
        compiler_params=_cp(),
    )(chip_core, grad, got)


def owner_sum(own, got, chip_core):
    hr, cc = own.shape
    tr = 256 if hr % 256 == 0 else hr
    nb = hr // tr

    def body(cc_ref, a_ref, b_ref, o_ref):
        s = a_ref[...] + b_ref[0].astype(F32)
        s = s + b_ref[1].astype(F32)
        o_ref[...] = s + b_ref[2].astype(F32)

    grid_spec = pltpu.PrefetchScalarGridSpec(
        num_scalar_prefetch=1, grid=(nb,),
        in_specs=[pl.BlockSpec((tr, cc), lambda i, cc_ref: (i, 0)),
                  pl.BlockSpec((3, tr, cc), lambda i, cc_ref: (0, i, 0))],
        out_specs=pl.BlockSpec((tr, cc), lambda i, cc_ref: (cc_ref[1] * nb + i, 0)))
    return pl.pallas_call(
        body, name="owner_sum", grid_spec=grid_spec,
        out_shape=jax.ShapeDtypeStruct((2 * hr, cc), F32), compiler_params=_cp(),
    )(chip_core, own, got)


def share_halves(bufs):
    n = len(bufs)

    def body(*refs):
        outs = refs[n:2 * n]
        send_sems, recv_sems = refs[2 * n:]
        x, y, c, _ = _place()
        copies = []
        for i in range(n):
            hr = bufs[i].shape[0] // 2
            mine = outs[i].at[pl.ds(c * hr, hr)]
            cp = pltpu.make_async_remote_copy(
                src_ref=mine, dst_ref=mine, send_sem=send_sems.at[i], recv_sem=recv_sems.at[i],
                device_id=(x, y, 1 - c), device_id_type=MESH)
            cp.start()
            copies.append((cp, outs[i].at[pl.ds((1 - c) * hr, hr)]))
        for i, (cp, theirs) in enumerate(copies):
            cp.wait_send()
            pltpu.make_async_remote_copy(
                src_ref=theirs, dst_ref=theirs, send_sem=send_sems.at[i], recv_sem=recv_sems.at[i],
                device_id=(x, y, c), device_id_type=MESH).wait_recv()

    return pl.pallas_call(
        body, name="share_halves",
        in_specs=[ANY] * n, out_specs=[ANY] * n,
        out_shape=[jax.ShapeDtypeStruct(b.shape, b.dtype) for b in bufs],
        input_output_aliases={i: i for i in range(n)},
        scratch_shapes=[pltpu.SemaphoreType.DMA((n,)), pltpu.SemaphoreType.DMA((n,))],
        compiler_params=pltpu.CompilerParams(has_side_effects=True),
    )(*bufs)


def allgather8(v, name):
    r, n = v.shape

    def body(v_ref, out_ref, send_sems, recv_sems):
        x, y, c, _ = _place()
        me = 4 * x + 2 * y + c
        out_ref[me] = v_ref[...]

        def copy(k, slot, to):
            return pltpu.make_async_remote_copy(
                src_ref=v_ref, dst_ref=out_ref.at[slot], send_sem=send_sems.at[k - 1],
                recv_sem=recv_sems.at[k - 1], device_id=to, device_id_type=MESH)

        peers = []
        for k in range(1, 8):
            px = 1 - x if (k >> 2) & 1 else x
            py = 1 - y if (k >> 1) & 1 else y
            pc = 1 - c if k & 1 else c
            peers.append((px, py, pc))
            copy(k, me, (px, py, pc)).start()
        for k, (px, py, pc) in enumerate(peers, start=1):
            copy(k, 4 * px + 2 * py + pc, (x, y, c)).wait_recv()
        for k, peer in enumerate(peers, start=1):
            copy(k, me, peer).wait_send()

    return pl.pallas_call(
        body, name=name, in_specs=[VMEM], out_specs=VMEM,
        out_shape=jax.ShapeDtypeStruct((8, r, n), v.dtype),
        scratch_shapes=[pltpu.SemaphoreType.DMA((7,)), pltpu.SemaphoreType.DMA((7,))],
        compiler_params=_cp(has_side_effects=True),
    )(v)


HBM = pl.BlockSpec(memory_space=pltpu.HBM)
SEM = pl.BlockSpec(memory_space=pltpu.SEMAPHORE)
DATAFLOW = pltpu.SideEffectType.DATAFLOW_SIDE_EFFECTING


def _descriptors(plan, refs, send_sems, recv_sems, arrivals=True):
    x, y, c, _ = _place()
    sends, recvs = plan(refs)
    out = [pltpu.make_async_remote_copy(src_ref=src, dst_ref=dst, send_sem=send_sems.at[k],
                                        recv_sem=recv_sems.at[k], device_id=to, device_id_type=MESH)
           for k, (src, dst, to) in enumerate(sends)]
    if not arrivals:
        return out, []
    inn = [pltpu.make_async_remote_copy(src_ref=land, dst_ref=land, send_sem=send_sems.at[k],
                                        recv_sem=recv_sems.at[k], device_id=(x, y, c), device_id_type=MESH)
           for k, land in enumerate(recvs)]
    return out, inn


def copies_start(name, arrays, n_copies, plan, after):
    na = len(arrays)

    def body(*refs):
        out, _ = _descriptors(plan, refs[:na], refs[na + 1], refs[na + 2], arrivals=False)
        for cp in out:
            cp.start()
        refs[-1][...] = jnp.zeros((8, 128), F32)

    res = pl.pallas_call(
        body, name=name,
        out_shape=(pltpu.SemaphoreType.DMA((n_copies,)), pltpu.SemaphoreType.DMA((n_copies,)),
                   *[pltpu.HBM(a.shape, a.dtype) for a in arrays], jax.ShapeDtypeStruct((8, 128), F32)),
        in_specs=[HBM] * na + [ANY], out_specs=(SEM, SEM, *[HBM] * na, VMEM),
        input_output_aliases={i: i + 2 for i in range(na)},
        compiler_params=pltpu.CompilerParams(has_side_effects=DATAFLOW),
    )(*[pltpu.with_memory_space_constraint(a, pltpu.HBM) for a in arrays], after)
    return res[0], res[1], list(res[2:2 + na]), res[-1]


def copies_wait(name, started, plan, after):
    send_sems, recv_sems, arrays, _ = started
    na = len(arrays)
    after = list(after) if isinstance(after, (list, tuple)) else [after]

    def body(*refs):
        out, inn = _descriptors(plan, refs[:na], refs[na], refs[na + 1])
        for cp in out:
            cp.wait_send()
        for cp in inn:
            cp.wait_recv()
        refs[-1][...] = jnp.zeros((8, 128), F32)

    res = pl.pallas_call(
        body, name=name,
        out_shape=(*[pltpu.HBM(a.shape, a.dtype) for a in arrays], jax.ShapeDtypeStruct((8, 128), F32)),
        in_specs=[HBM] * na + [SEM, SEM] + [ANY] * len(after), out_specs=(*[HBM] * na, VMEM),
        input_output_aliases={i: i for i in range(na)},
        compiler_params=pltpu.CompilerParams(has_side_effects=DATAFLOW),
    )(*arrays, send_sems, recv_sems, *after)
    return list(res[:na]), res[-1]


def _rows_half(r, c):
    return pl.ds(c * (r // 2), r // 2), pl.ds((1 - c) * (r // 2), r // 2)


def plan_gather_neighbours(refs):
    x, y, c, _ = _place()
    p = 2 * x + y
    near, far, _ = _relay_chips()
    sends, recvs = [], []
    for buf in refs:
        mine, _ = _rows_half(buf.shape[1], c)
        for chip in (near, far):
            sends.append((buf.at[p, mine], buf.at[p, mine], (*chip, c)))
            recvs.append(buf.at[2 * chip[0] + chip[1], mine])
    return sends, recvs


def plan_gather_relay(refs):
    x, y, c, _ = _place()
    near, far, diag = _relay_chips()
    slot = lambda chip: 2 * chip[0] + chip[1]
    sends, recvs = [], []
    for buf in refs:
        mine, theirs = _rows_half(buf.shape[1], c)
        landed = buf.at[slot(near), mine]
        sends.append((landed, landed, (*far, c)))
        recvs.append(buf.at[slot(diag), mine])
        for sent, got in ((near, far), (far, near)):
            sends.append((buf.at[slot(sent), mine], buf.at[slot(sent), mine], (x, y, 1 - c)))
            recvs.append(buf.at[slot(got), theirs])
    return sends, recvs


def plan_gather_d2d(refs):
    x, y, c, _ = _place()
    _, _, diag = _relay_chips()
    sends, recvs = [], []
    for buf in refs:
        mine, theirs = _rows_half(buf.shape[1], c)
        landed = buf.at[2 * diag[0] + diag[1], mine]
        sends.append((landed, landed, (x, y, 1 - c)))
        recvs.append(buf.at[2 * diag[0] + diag[1], theirs])
    return sends, recvs


def plan_exchange(refs):
    x, y, c, _ = _place()
    n = len(refs) // 2
    sends, recvs = [], []
    for grad, land in zip(refs[:n], refs[n:]):
        _, theirs = _rows_half(grad.shape[1], c)
        sends.append((grad.at[:, theirs], land, (x, y, 1 - c)))
        recvs.append(land)
    return sends, recvs


def plan_scatter(refs):
    x, y, c, chips = _place()
    n = len(refs) // 2
    sends, recvs = [], []
    for part, land in zip(refs[:n], refs[n:]):
        for j, chip in enumerate(chips):
            sends.append((part.at[2 * chip[0] + chip[1]], land.at[j], (*chip, c)))
            recvs.append(land.at[j])
    return sends, recvs


def plan_share(refs):
    x, y, c, _ = _place()
    sends, recvs = [], []
    for buf in refs:
        mine, theirs = _rows_half(buf.shape[0], c)
        sends.append((buf.at[mine], buf.at[mine], (x, y, 1 - c)))
        recvs.append(buf.at[theirs])
    return sends, recvs


def put_in_slot(w, chip, dtype, name):
    r, c = w.shape
    tr = 256 if r % 256 == 0 else r

    def body(chip_ref, w_ref, o_ref):
        o_ref[...] = w_ref[...].astype(dtype)

    grid_spec = pltpu.PrefetchScalarGridSpec(
        num_scalar_prefetch=1, grid=(r // tr,),
        in_specs=[pl.BlockSpec((tr, c), lambda i, chip_ref: (i, 0))],
        out_specs=pl.BlockSpec((None, tr, c), lambda i, chip_ref: (chip_ref[0], i, 0)))
    return pl.pallas_call(body, name=name, grid_spec=grid_spec,
                          out_shape=jax.ShapeDtypeStruct((4, r, c), dtype), compiler_params=_cp())(chip, w)


def ada_fwd(s_in, ada_w, ada_b, tn):
    nl, d, ws = ada_w.shape

    def body(s_ref, w_ref, b_ref, so_ref, mod_ref):
        s = _silu(s_ref[...])
        so_ref[...] = s
        mod_ref[...] = _dot(s.astype(BF16), w_ref[...].astype(BF16)) + b_ref[...]

    return pl.pallas_call(
        body, name="ada_fwd", grid=(nl, ws // tn),
        in_specs=[pl.BlockSpec((16, d), lambda l, j: (0, 0)),
                  pl.BlockSpec((None, d, tn), lambda l, j: (l, 0, j)),
                  pl.BlockSpec((None, 1, tn), lambda l, j: (l, 0, j))],
        out_specs=[pl.BlockSpec((16, d), lambda l, j: (0, 0)),
                   pl.BlockSpec((None, 16, tn), lambda l, j: (l, 0, j))],
        out_shape=[jax.ShapeDtypeStruct((16, d), F32), jax.ShapeDtypeStruct((nl, 16, ws), F32)],
        compiler_params=_cp(),
    )(s_in, ada_w, ada_b)


def _adamw_math(w, g, m, v):
    m = ADAM_B1 * m + (1.0 - ADAM_B1) * g
    v = ADAM_B2 * v + (1.0 - ADAM_B2) * (g * g)
    m_hat = m / (1.0 - ADAM_B1 ** ADAM_STEP)
    v_hat = v / (1.0 - ADAM_B2 ** ADAM_STEP)
    delta = -ADAM_LR * (m_hat / (jnp.sqrt(v_hat) + ADAM_EPS) + ADAM_WD * w)
    return delta, m, v


def ada_bwd_adamw(s, dm, w, m, v):
    nl, d, ws = w.shape
    tr = 256 if d % 256 == 0 else 128

    def body(s_ref, dm_ref, w_ref, m_ref, v_ref, g_ref, dl_ref, mo_ref, vo_ref, dc_ref):
        dmv = dm_ref[...].astype(BF16)
        wv = w_ref[...]
        g = _dot(s_ref[...].astype(BF16), dmv, TN)
        g_ref[...] = g
        dl_ref[...], mo_ref[...], vo_ref[...] = _adamw_math(wv, g, m_ref[...], v_ref[...])
        dc_ref[...] = _dot(dmv[8:16, :], wv.astype(BF16), NT)

    wblk = pl.BlockSpec((None, tr, ws), lambda l, i: (l, i, 0))
    wshape = jax.ShapeDtypeStruct((nl, d, ws), F32)
    return pl.pallas_call(
        body, name="ada_bwd_adamw", grid=(nl, d // tr),
        in_specs=[pl.BlockSpec((16, tr), lambda l, i: (0, i)),
                  pl.BlockSpec((None, 16, ws), lambda l, i: (l, 0, 0)), wblk, wblk, wblk],
        out_specs=[wblk, wblk, wblk, wblk, pl.BlockSpec((None, 8, tr), lambda l, i: (l, 0, i))],
        out_shape=[wshape, wshape, wshape, wshape, jax.ShapeDtypeStruct((nl, 8, d), F32)],
        compiler_params=_cp(),
    )(s, dm, w, m, v)


def adamw(w, g, m, v, name, with_grad=False):
    r, c = w.shape
    tr = 256 if r % 256 == 0 else r

    def body(w_ref, g_ref, m_ref, v_ref, dl_ref, mo_ref, vo_ref, *g_out):
        gv = g_ref[...]
        dl_ref[...], mo_ref[...], vo_ref[...] = _adamw_math(w_ref[...], gv, m_ref[...], v_ref[...])
        if with_grad:
            g_out[0][...] = gv

    blk = pl.BlockSpec((tr, c), lambda i: (i, 0))
    shape = jax.ShapeDtypeStruct((r, c), F32)
    n_out = 4 if with_grad else 3
    return pl.pallas_call(body, name=name, grid=(r // tr,), in_specs=[blk] * 4, out_specs=[blk] * n_out,
                          out_shape=[shape] * n_out, compiler_params=_cp())(w, g, m, v)


ROW_MOD = 10


def small_reduce(gathered):
    _, rows, d = gathered.shape

    def body(g_ref, o_ref):
        tot = g_ref[0]
        for b in range(1, 8):
            tot = tot + g_ref[b]
        o_ref[0:rows, :] = tot
        for layer in range(2):
            lat = ROW_MOD + 6 * layer
            o_ref[24 + 3 * layer:27 + 3 * layer, :] = tot[lat:lat + 3, :] + tot[lat + 3:lat + 6, :]
        o_ref[30:32, :] = jnp.zeros((2, d), F32)

    return pl.pallas_call(body, name="small_reduce", in_specs=[VMEM], out_specs=VMEM,
                          out_shape=jax.ShapeDtypeStruct((32, d), F32), compiler_params=_cp())(gathered)


def lb_logits_grad(lbl, dlb):
    _, _, n = lbl.shape

    def body(l_ref, d_ref, o_ref):
        for dr in range(2):
            _, (p0, p1, p2) = _lower_bound(l_ref, dr)
            dv = d_ref[dr:dr + 1, :]
            o_ref[dr, 0:1, :] = p0 * p2 * dv
            o_ref[dr, 1:2, :] = p1 * p2 * dv
            o_ref[dr, 2:3, :] = -p2 * (p0 + p1) * dv

    return pl.pallas_call(body, name="lb_logits_grad", in_specs=[VMEM, VMEM], out_specs=VMEM,
                          out_shape=jax.ShapeDtypeStruct((2, 3, n), F32), compiler_params=_cp())(lbl, dlb)


def c_ctx_grad(parts, c_ctx):
    d = c_ctx.shape[1]

    def body(p_ref, c_ref, o_ref):
        tot = p_ref[0, 0:1, :]
        for chip in range(1, 4):
            tot = tot + p_ref[2 * chip, 0:1, :]
        o_ref[...] = tot * _dsilu(c_ref[...])

    return pl.pallas_call(body, name="c_ctx_grad", in_specs=[VMEM, VMEM], out_specs=VMEM,
                          out_shape=jax.ShapeDtypeStruct((1, d), F32), compiler_params=_cp())(parts, c_ctx)


def kernel(x, c, ctx, c_ctx, ada_w, ada_b, pre_g, post_g, ev_w_in, ev_pool_w, ev_pool_scale, ev_conv_w, ev_conv_b, ev_w_out, od_w_in, od_onorm_g, od_w_out, lb_logits, loss_target, m_c_ctx, m_ada_w, m_ada_b, m_pre_g, m_post_g, m_ev_w_in, m_ev_pool_w, m_ev_pool_scale, m_ev_conv_w, m_ev_conv_b, m_ev_w_out, m_od_w_in, m_od_onorm_g, m_od_w_out, m_lb_logits, v_c_ctx, v_ada_w, v_ada_b, v_pre_g, v_post_g, v_ev_w_in, v_ev_pool_w, v_ev_pool_scale, v_ev_conv_w, v_ev_conv_b, v_ev_w_out, v_od_w_in, v_od_onorm_g, v_od_w_out, v_lb_logits):
    _, seq, d = x.shape
    cx = ctx.shape[1]
    t = cx + seq
    half_d = d // 2
    g = half_d // N_POOL
    tn = d // 4
    xi, yi, ci = lax.axis_index("x"), lax.axis_index("y"), lax.axis_index("c")
    chip = 2 * xi + yi
    me = 2 * chip + ci
    chip_arr = jnp.reshape(chip, (1,)).astype(jnp.int32)
    chip_core_arr = jnp.stack([chip, ci]).astype(jnp.int32)

    pad = lambda a, rows: jnp.concatenate([a, jnp.zeros((rows - a.shape[0], g), F32)], axis=0)
    small = jnp.concatenate([
        ev_pool_w.reshape(g, g), pad(ev_conv_w.reshape(3, g), 8), pad(od_onorm_g.reshape(2, g), 8),
        pad(lb_logits.reshape(12, g), 16)], axis=0)
    ev_in_g, ev_out_g, small_g, ev_done = allgather_shards([
        put_in_slot(ev_w_in[0], chip_arr, BF16, "cast_ev_w_in"),
        put_in_slot(ev_w_out[0], chip_arr, BF16, "cast_ev_w_out"),
        put_in_slot(small, chip_arr, F32, "place_small")])
    ev_out3 = ev_out_g.reshape(1, d, d)
    pool_w_full = small_g[:, :g].reshape(4, N_POOL, g // 4, g).transpose(1, 0, 2, 3).reshape(N_POOL, g, g)
    conv_w_full = small_g[:, g:g + 3].transpose(1, 0, 2).reshape(3, half_d)
    onorm_full = small_g[:, g + 8:g + 10].reshape(1, d)
    lbl_full = small_g[:, g + 16:g + 28].reshape(4, 2, 3, 2 * g).transpose(1, 2, 0, 3).reshape(2, 3, d)

    c_rows = jnp.concatenate([c + ev_done[0:1, 0:1], jnp.zeros((7, d), F32)], axis=0)
    c_all = allgather8(c_rows, "allgather_c")[:, 0, :]
    s_in = jnp.concatenate([c_all, c_ctx.reshape(1, d), jnp.zeros((7, d), F32)], axis=0)
    ws_ada = ada_w.shape[2]
    ada_b_mine = lax.dynamic_slice(ada_b, (0, chip * ws_ada), (2, ws_ada)).reshape(2, 1, ws_ada)
    s_act, mod_mine = ada_fwd(s_in, ada_w, ada_b_mine, tn)
    mod_all = allgather8(mod_mine.reshape(32, ws_ada), "allgather_mod")
    od_ici = copies_start("gather_od_ici_start", [
        put_in_slot(od_w_in[0], chip_arr, BF16, "cast_od_w_in"),
        put_in_slot(od_w_out[0], chip_arr, BF16, "cast_od_w_out")], 4, plan_gather_neighbours, mod_all)
    mod_full = mod_all[0::2].reshape(4, 2, 16, ws_ada).transpose(1, 2, 0, 3).reshape(2, 16, 3 * d)
    mod_lat = lax.dynamic_slice(mod_full, (0, me, 0), (2, 1, 3 * d))
    mods = jnp.concatenate([mod_full[:, 8:9], mod_lat], axis=1)
    shift, scale, gate = mods[:, :, :d], mods[:, :, d:2 * d], mods[:, :, 2 * d:]

    h0, xs = normmod_fwd_joining(ctx[0], x[0], pre_g[0:1] + od_ici[3][0:1, 0:1], shift[0], scale[0])
    z0 = mm_nn(h0, ev_in_g, half_d, tn, "mm_ev_in")
    u = mix_b_fwd(z0, conv_w_full, ev_conv_b, mix_a_fwd(z0, pool_w_full, ev_pool_scale, cx), cx)
    od_relay = copies_start("gather_od_relay_start",
                            copies_wait("gather_od_ici_wait", od_ici, plan_gather_neighbours, u)[0],
                            6, plan_gather_relay, u)
    y0 = mm_nn(u, ev_out3, d, tn, "mm_ev_out")[0]
    xs1, h1 = post_fwd_norm(xs, y0, post_g[0:1] + od_relay[3][0:1, 0:1], gate[0],
                            pre_g[1:2], shift[1], scale[1], cx)
    od_d2d = copies_start("gather_od_d2d_start",
                          copies_wait("gather_od_relay_wait", od_relay, plan_gather_relay, xs1)[0],
                          2, plan_gather_d2d, xs1)
    (od_in_g, od_out_g), _ = copies_wait("gather_od_d2d_wait", od_d2d, plan_gather_d2d, od_d2d[3])
    od_out3 = od_out_g.reshape(1, d, d)

    z1 = mm_nn(h1, od_in_g, d, tn, "mm_od_in")
    o1, r1, bcs1, ks1, decs1 = hgrn_fwd(z1, lbl_full, onorm_full, cx)
    y1 = mm_nn(r1, od_out3, d, tn, "mm_od_out")[0]
    sq, dx2, dy1, dgate1, dpost1 = post_loss(xs1, y1, post_g[1:2], gate[1], loss_target[0], cx)

    dr1 = mm_nt(dy1[None], None, od_out3, tn, "mm_od_out_dx")
    g_od_out = mm_tn(r1, dy1[None], None, d, tn, "mm_od_out_dw")
    dz1, donorm, dlb = hgrn_bwd(z1, lbl_full, onorm_full, o1, dr1, bcs1, ks1, decs1, cx)
    dh1 = mm_nt(dz1, None, od_in_g, tn, "mm_od_in_dx")
    g_od_in = mm_tn(h1, dz1, None, od_in_g.shape[2], tn, "mm_od_in_dw")
    dxs1, dpre1, dshift1, dscale1 = normmod_bwd(xs1, dh1, pre_g[1:2], scale[1], dx2, cx, True)

    od_grads = [g_od_in, g_od_out.reshape(4, d // 4, d)]
    half_zone = lambda a, lead, dt: lax.empty((lead, a.shape[1] // 2, a.shape[2]), dt)
    od_ex = copies_start("reduce_od_exchange_start", od_grads + [half_zone(a, 4, a.dtype) for a in od_grads],
                         2, plan_exchange, dxs1)

    dy0, dgate0, dpost0 = post_bwd(dxs1, y0, post_g[0:1] + od_ex[3][0:1, 0:1], gate[0], cx)
    du = mm_nt(dy0[None], None, ev_out3, tn, "mm_ev_out_dx")
    g_ev_out = mm_tn(u, dy0[None], None, d, tn, "mm_ev_out_dw")
    od_got, _ = copies_wait("reduce_od_exchange_wait", od_ex, plan_exchange, g_ev_out)
    od_sums = [pair_sum(od_got[i], od_got[2 + i], chip_core_arr) for i in range(2)]
    od_sc = copies_start("reduce_od_scatter_start",
                         [sb for _, sb in od_sums] + [half_zone(a, 3, BF16) for a in od_grads],
                         6, plan_scatter, du)
    dz0a, g_pool_w, dpool_scale = mix_a_bwd(z0, du, pool_w_full, ev_pool_scale + od_sc[3][0:1, 0:1], cx)
    dz0b, dconv_w, dconv_b = mix_b_bwd(z0, du, conv_w_full, ev_conv_b + od_sc[3][0:1, 0:1], cx)
    g_ev_in = mm_tn(h0, dz0a, dz0b, ev_in_g.shape[2], tn, "mm_ev_in_dw")
    ev_grads = [g_ev_in, g_ev_out.reshape(4, d // 4, d), g_pool_w.reshape(4, g, g)]
    ev_ex = copies_start("reduce_ev_exchange_start", ev_grads + [half_zone(a, 4, a.dtype) for a in ev_grads],
                         3, plan_exchange, dpool_scale)
    dh0 = mm_nt(dz0a, dz0b, ev_in_g, tn, "mm_ev_in_dx")
    dxs0, dpre0, dshift0, dscale0 = normmod_bwd(xs, dh0, pre_g[0:1] + ev_ex[3][0:1, 0:1], scale[0], dxs1,
                                                cx, False, True)
    grad_x = dxs0[None]
    ev_got, _ = copies_wait("reduce_ev_exchange_wait", ev_ex, plan_exchange, dxs0)
    ev_sums = [pair_sum(ev_got[i], ev_got[3 + i], chip_core_arr) for i in range(3)]
    od_recv, _ = copies_wait("reduce_od_scatter_wait", od_sc, plan_scatter, dxs0)

    zrow = jnp.zeros((1, d), F32)
    small_rows = jnp.concatenate([
        dpre0, dpre1, dpost0, dpost1,
        jnp.concatenate([dpool_scale, dconv_b], axis=1),
        jnp.concatenate([dconv_w.reshape(1, 3 * half_d), jnp.zeros((1, half_d), F32)], axis=1).reshape(2, d),
        donorm, dlb,
        dshift0[1:2], dscale0[1:2], dgate0[1:2], dshift0[0:1], dscale0[0:1], dgate0[0:1],
        dshift1[1:2], dscale1[1:2], dgate1[1:2], dshift1[0:1], dscale1[0:1], zrow,
        jnp.concatenate([sq[0:1], jnp.zeros((1, d - 128), F32)], axis=1),
        zrow], axis=0)
    small_all = allgather8(small_rows, "allgather_small")
    ev_sc = copies_start("reduce_ev_scatter_start",
                         [sb for _, sb in ev_sums] + [half_zone(a, 3, BF16) for a in ev_grads],
                         9, plan_scatter, small_all)
    od_sh = copies_start("reduce_od_share_start",
                         [owner_sum(od_sums[i][0], od_recv[2 + i], chip_core_arr) for i in range(2)],
                         2, plan_share, dxs0)
    tot = small_reduce(small_all + ev_sc[3][0:1, 0:1])
    loss = tot[22, 0] * (0.5 / d)

    dm_rows = []
    for layer in range(2):
        lat = ROW_MOD + 6 * layer
        dm_lat = small_all[:, lat:lat + 3].reshape(8, 3 * d)
        dm_ctx = tot[lat + 3:lat + 6].reshape(1, 3 * d)
        dm_rows.append(jnp.concatenate([dm_lat, dm_ctx, jnp.zeros((7, 3 * d), F32)], axis=0))
    dm_full = jnp.stack(dm_rows)
    dm_mine = lax.dynamic_slice(dm_full, (0, 0, chip * ws_ada), (2, 16, ws_ada))

    def step(w, gr, m, v, name, with_grad=False):
        shape = w.shape
        cols = shape[-1]
        two_d = lambda a: a.reshape(-1, cols)
        res = adamw(two_d(w), two_d(gr), two_d(m), two_d(v), "adamw_" + name, with_grad)
        return tuple(a.reshape(shape) for a in res)

    grad_ada_b = tot[24:30].reshape(2, 3 * d)
    grad_pre_g = tot[0:2]
    grad_post_g = tot[2:4]
    grad_ev_pool_scale = tot[4:5, :half_d]
    grad_ev_conv_b = tot[4:5, half_d:]
    conv_w_tot = tot[5:7].reshape(1, 2 * d)[:, :3 * half_d].reshape(3, N_POOL, g)
    grad_ev_conv_w = lax.dynamic_slice(conv_w_tot, (0, chip, 0), (3, 1, g)).reshape(1, 3, g)
    grad_od_onorm_g = lax.dynamic_slice(tot[7:8], (0, chip * 2 * g), (1, 2 * g))
    dlb_mine = lax.dynamic_slice(tot[8:10], (0, chip * 2 * g), (2, 2 * g))
    grad_lb_logits = lb_logits_grad(lb_logits, dlb_mine)
    upd = {
        "ada_b": step(ada_b, grad_ada_b, m_ada_b, v_ada_b, "ada_b"),
        "pre_g": step(pre_g, grad_pre_g, m_pre_g, v_pre_g, "pre_g"),
        "post_g": step(post_g, grad_post_g, m_post_g, v_post_g, "post_g"),
        "ev_pool_scale": step(ev_pool_scale, grad_ev_pool_scale, m_ev_pool_scale, v_ev_pool_scale, "ev_pool_scale"),
        "ev_conv_w": step(ev_conv_w, grad_ev_conv_w, m_ev_conv_w, v_ev_conv_w, "ev_conv_w"),
        "ev_conv_b": step(ev_conv_b, grad_ev_conv_b, m_ev_conv_b, v_ev_conv_b, "ev_conv_b"),
        "od_onorm_g": step(od_onorm_g, grad_od_onorm_g, m_od_onorm_g, v_od_onorm_g, "od_onorm_g"),
        "lb_logits": step(lb_logits, grad_lb_logits, m_lb_logits, v_lb_logits, "lb_logits"),
    }
    grad_ada_w, delta_ada_w, new_m_ada_w, new_v_ada_w, dctx_part = ada_bwd_adamw(
        s_act, dm_mine, ada_w, m_ada_w, v_ada_w)
    upd["ada_w"] = (delta_ada_w, new_m_ada_w, new_v_ada_w)
    (grad_od_w_in, grad_od_w_out), _ = copies_wait("reduce_od_share_wait", od_sh, plan_share, ev_sc[3])
    upd["od_w_in"] = step(od_w_in, grad_od_w_in[None], m_od_w_in, v_od_w_in, "od_w_in", True)
    upd["od_w_out"] = step(od_w_out, grad_od_w_out[None], m_od_w_out, v_od_w_out, "od_w_out", True)
    grad_od_w_in, grad_od_w_out = upd["od_w_in"][3], upd["od_w_out"][3]
    done_behind = [dctx_part] + [upd[k][0] for k in (
        "od_w_in", "od_w_out", "ada_b", "pre_g", "post_g", "ev_pool_scale", "ev_conv_w", "ev_conv_b",
        "od_onorm_g", "lb_logits")]
    ev_recv, ev_landed = copies_wait("reduce_ev_scatter_wait", ev_sc, plan_scatter, done_behind)
    grad_ev_w_in, grad_ev_w_out, grad_pool_w = share_halves(
        [owner_sum(ev_sums[i][0], ev_recv[3 + i], chip_core_arr) for i in range(3)])
    dctx_all = allgather8(dctx_part[0] + dctx_part[1] + ev_landed[0:1, 0:1], "allgather_dctx")
    grad_c_ctx = c_ctx_grad(dctx_all, c_ctx.reshape(1, d)).reshape(d)
    upd["c_ctx"] = step(c_ctx, grad_c_ctx, m_c_ctx, v_c_ctx, "c_ctx")
    upd["ev_w_in"] = step(ev_w_in, grad_ev_w_in[None], m_ev_w_in, v_ev_w_in, "ev_w_in", True)
    upd["ev_pool_w"] = step(ev_pool_w, grad_pool_w.reshape(1, N_POOL, g // 4, g), m_ev_pool_w, v_ev_pool_w,
                            "ev_pool_w", True)
    upd["ev_w_out"] = step(ev_w_out, grad_ev_w_out[None], m_ev_w_out, v_ev_w_out, "ev_w_out", True)
    grad_ev_w_in, grad_ev_pool_w, grad_ev_w_out = upd["ev_w_in"][3], upd["ev_pool_w"][3], upd["ev_w_out"][3]
    names = ["c_ctx", "ada_w", "ada_b", "pre_g", "post_g", "ev_w_in", "ev_pool_w", "ev_pool_scale",
             "ev_conv_w", "ev_conv_b", "ev_w_out", "od_w_in", "od_onorm_g", "od_w_out", "lb_logits"]
    grads = [grad_c_ctx, grad_ada_w, grad_ada_b, grad_pre_g, grad_post_g, grad_ev_w_in, grad_ev_pool_w,
             grad_ev_pool_scale, grad_ev_conv_w, grad_ev_conv_b, grad_ev_w_out, grad_od_w_in,
             grad_od_onorm_g, grad_od_w_out, grad_lb_logits]
    return (loss, grad_x, *grads, *[upd[k][0] for k in names], *[upd[k][1] for k in names],
            *[upd[k][2] for k in names])
```

```python
import jax
import jax.numpy as jnp
from jax import lax
from jax.experimental import pallas as pl
from jax.experimental.pallas import tpu as pltpu

EPS = 1e-6
GRID_W_LOG2 = 6
CHUNK = 64
HEAD = 128
N_POOL = 4
ADAM_LR, ADAM_B1, ADAM_B2, ADAM_EPS, ADAM_WD, ADAM_STEP = 0.001, 0.9, 0.999, 1e-08, 0.01, 10
VMEM_LIMIT = 56 * 1024 * 1024
MESH = pl.DeviceIdType.MESH
F32, BF16 = jnp.float32, jnp.bfloat16
ANY = pl.BlockSpec(memory_space=pl.ANY)
VMEM = pl.BlockSpec(memory_space=pltpu.VMEM)


def _cp(**kw):
    return pltpu.CompilerParams(vmem_limit_bytes=VMEM_LIMIT, **kw)


def _silu(x):
    return x * jax.nn.sigmoid(x)


def _dsilu(x):
    s = jax.nn.sigmoid(x)
    return s * (1.0 + x * (1.0 - s))


def _dot(a, b, dims=((1,), (0,)), precision=None):
    return lax.dot_general(a, b, (dims, ((), ())), preferred_element_type=F32, precision=precision)


NN = ((1,), (0,))
NT = ((1,), (1,))
TN = ((0,), (0,))


def _row_block(cx):
    return 256 if cx % 256 == 0 else 128


def normmod_fwd_joining(ctx, x, g, shift, scale):
    cx, d = ctx.shape
    t = cx + x.shape[0]
    tm = _row_block(cx)
    nctx = cx // tm

    def body(c_ref, x_ref, g_ref, sh_ref, sc_ref, h_ref, xs_ref):
        is_ctx = pl.program_id(0) < nctx
        x = jnp.where(is_ctx, c_ref[...], x_ref[...])
        xs_ref[...] = x
        rstd = lax.rsqrt(jnp.mean(x * x, axis=-1, keepdims=True) + EPS)
        sc = jnp.where(is_ctx, sc_ref[0:1, :], sc_ref[1:2, :])
        sh = jnp.where(is_ctx, sh_ref[0:1, :], sh_ref[1:2, :])
        h_ref[...] = ((x * rstd) * g_ref[...] * (1.0 + sc) + sh).astype(BF16)

    row = pl.BlockSpec((tm, d), lambda i: (i, 0))
    vec = lambda r: pl.BlockSpec((r, d), lambda i: (0, 0))
    return pl.pallas_call(
        body, name="normmod_fwd_joining", grid=(t // tm,),
        in_specs=[pl.BlockSpec((tm, d), lambda i: (jnp.minimum(i, nctx - 1), 0)),
                  pl.BlockSpec((tm, d), lambda i: (jnp.maximum(i - nctx, 0), 0)), vec(1), vec(2), vec(2)],
        out_specs=[row, row],
        out_shape=[jax.ShapeDtypeStruct((t, d), BF16), jax.ShapeDtypeStruct((t, d), F32)],
        compiler_params=_cp(),
    )(ctx, x, g, shift, scale)


def normmod_bwd(xs, dh, g, scale, dres, cx, res_is_latent_only, dx_latent_only=False):
    t, d = xs.shape
    tm = _row_block(cx)
    nctx = cx // tm

    def body(x_ref, dh_ref, g_ref, sc_ref, dres_ref, dx_ref, dg_ref, dsh_ref, dsc_ref):
        i = pl.program_id(0)
        is_ctx = i < nctx

        @pl.when(i == 0)
        def _():
            dg_ref[...] = jnp.zeros_like(dg_ref)
            dsh_ref[...] = jnp.zeros_like(dsh_ref)
            dsc_ref[...] = jnp.zeros_like(dsc_ref)

        x = x_ref[...]
        dh = dh_ref[...]
        gv = g_ref[...]
        rstd = lax.rsqrt(jnp.mean(x * x, axis=-1, keepdims=True) + EPS)
        xhat = x * rstd
        sc = jnp.where(is_ctx, sc_ref[0:1, :], sc_ref[1:2, :])
        dsh = jnp.sum(dh, axis=0, keepdims=True)
        dhx = dh * xhat
        dsc = jnp.sum(dhx * gv, axis=0, keepdims=True)
        dg_ref[...] += jnp.sum(dhx * (1.0 + sc), axis=0, keepdims=True)
        zero = jnp.zeros_like(dsh)
        dsh_ref[0:1, :] += jnp.where(is_ctx, dsh, zero)
        dsh_ref[1:2, :] += jnp.where(is_ctx, zero, dsh)
        dsc_ref[0:1, :] += jnp.where(is_ctx, dsc, zero)
        dsc_ref[1:2, :] += jnp.where(is_ctx, zero, dsc)
        dxhat = dh * (gv * (1.0 + sc))
        dx = rstd * (dxhat - xhat * jnp.mean(dxhat * xhat, axis=-1, keepdims=True))
        res = dres_ref[...]
        if res_is_latent_only:
            res = jnp.where(is_ctx, jnp.zeros_like(res), res)
        dx_ref[...] = dx + res

    row = pl.BlockSpec((tm, d), lambda i: (i, 0))
    if res_is_latent_only:
        res_spec = pl.BlockSpec((tm, d), lambda i: (jnp.maximum(i - nctx, 0), 0))
    else:
        res_spec = row
    vec = lambda r: pl.BlockSpec((r, d), lambda i: (0, 0))
    dx_spec = pl.BlockSpec((tm, d), lambda i: (jnp.maximum(i - nctx, 0), 0)) if dx_latent_only else row
    return pl.pallas_call(
        body, name="normmod_bwd", grid=(t // tm,),
        in_specs=[row, row, vec(1), vec(2), res_spec],
        out_specs=[dx_spec, vec(1), vec(2), vec(2)],
        out_shape=[jax.ShapeDtypeStruct((t - cx if dx_latent_only else t, d), F32), jax.ShapeDtypeStruct((1, d), F32),
                   jax.ShapeDtypeStruct((2, d), F32), jax.ShapeDtypeStruct((2, d), F32)],
        compiler_params=_cp(),
    )(xs, dh, g, scale, dres)


def post_fwd_norm(xs, y, pg, gate, g_next, shift_next, scale_next, cx):
    t, d = xs.shape
    tm = _row_block(cx)
    nctx = cx // tm

    def body(x_ref, y_ref, pg_ref, gate_ref, g_ref, sh_ref, sc_ref, o_ref, h_ref):
        is_ctx = pl.program_id(0) < nctx
        pick = lambda ref: jnp.where(is_ctx, ref[0:1, :], ref[1:2, :])
        y = y_ref[...]
        rstd = lax.rsqrt(jnp.mean(y * y, axis=-1, keepdims=True) + EPS)
        x = x_ref[...] + pick(gate_ref) * ((y * rstd) * pg_ref[...])
        o_ref[...] = x
        rstd = lax.rsqrt(jnp.mean(x * x, axis=-1, keepdims=True) + EPS)
        h_ref[...] = ((x * rstd) * g_ref[...] * (1.0 + pick(sc_ref)) + pick(sh_ref)).astype(BF16)

    row = pl.BlockSpec((tm, d), lambda i: (i, 0))
    vec = lambda r: pl.BlockSpec((r, d), lambda i: (0, 0))
    return pl.pallas_call(
        body, name="post_fwd_norm", grid=(t // tm,),
        in_specs=[row, row, vec(1), vec(2), vec(1), vec(2), vec(2)], out_specs=[row, row],
        out_shape=[jax.ShapeDtypeStruct((t, d), F32), jax.ShapeDtypeStruct((t, d), BF16)],
        compiler_params=_cp(),
    )(xs, y, pg, gate, g_next, shift_next, scale_next)


def post_loss(xs, y, pg, gate, target, cx):
    t, d = xs.shape
    n = y.shape[0]
    tm = _row_block(cx)
    nctx = cx // tm

    def body(x_ref, y_ref, pg_ref, gate_ref, tgt_ref, sq_ref, dx_ref, dy_ref, dgate_ref, dpg_ref):
        @pl.when(pl.program_id(0) == 0)
        def _():
            sq_ref[...] = jnp.zeros_like(sq_ref)
            dgate_ref[...] = jnp.zeros_like(dgate_ref)
            dpg_ref[...] = jnp.zeros_like(dpg_ref)

        y = y_ref[...]
        pgv = pg_ref[...]
        gt = gate_ref[1:2, :]
        rstd = lax.rsqrt(jnp.mean(y * y, axis=-1, keepdims=True) + EPS)
        yhat = y * rstd
        err = x_ref[...] + gt * (yhat * pgv) - tgt_ref[...]
        sq_ref[...] += jnp.sum(err * err)
        dx = err * (1.0 / d)
        dx_ref[...] = dx
        dxy = dx * yhat
        dgate_ref[1:2, :] += jnp.sum(dxy * pgv, axis=0, keepdims=True)
        dpg_ref[...] += jnp.sum(dxy * gt, axis=0, keepdims=True)
        dyhat = dx * (gt * pgv)
        dy_ref[...] = (rstd * (dyhat - yhat * jnp.mean(dyhat * yhat, axis=-1, keepdims=True))).astype(BF16)

    row = pl.BlockSpec((tm, d), lambda i: (i, 0))
    xrow = pl.BlockSpec((tm, d), lambda i: (i + nctx, 0))
    vec = lambda r: pl.BlockSpec((r, d), lambda i: (0, 0))
    return pl.pallas_call(
        body, name="post_loss", grid=(n // tm,),
        in_specs=[xrow, row, vec(1), vec(2), row],
        out_specs=[pl.BlockSpec((8, 128), lambda i: (0, 0)), row, row, vec(2), vec(1)],
        out_shape=[jax.ShapeDtypeStruct((8, 128), F32), jax.ShapeDtypeStruct((n, d), F32),
                   jax.ShapeDtypeStruct((n, d), BF16), jax.ShapeDtypeStruct((2, d), F32),
                   jax.ShapeDtypeStruct((1, d), F32)],
        compiler_params=_cp(),
    )(xs, y, pg, gate, target)


def post_bwd(dxo, y, pg, gate, cx):
    m, d = y.shape
    tm = _row_block(cx)
    nctx = cx // tm

    def body(dx_ref, y_ref, pg_ref, gate_ref, dy_ref, dgate_ref, dpg_ref):
        i = pl.program_id(0)
        is_ctx = i < nctx

        @pl.when(i == 0)
        def _():
            dgate_ref[...] = jnp.zeros_like(dgate_ref)
            dpg_ref[...] = jnp.zeros_like(dpg_ref)

        y = y_ref[...]
        dx = dx_ref[...]
        pgv = pg_ref[...]
        rstd = lax.rsqrt(jnp.mean(y * y, axis=-1, keepdims=True) + EPS)
        yhat = y * rstd
        gt = jnp.where(is_ctx, gate_ref[0:1, :], gate_ref[1:2, :])
        dxy = dx * yhat
        dgt = jnp.sum(dxy * pgv, axis=0, keepdims=True)
        zero = jnp.zeros_like(dgt)
        dgate_ref[0:1, :] += jnp.where(is_ctx, dgt, zero)
        dgate_ref[1:2, :] += jnp.where(is_ctx, zero, dgt)
        dpg_ref[...] += jnp.sum(dxy * gt, axis=0, keepdims=True)
        dyhat = dx * (gt * pgv)
        dy = rstd * (dyhat - yhat * jnp.mean(dyhat * yhat, axis=-1, keepdims=True))
        dy_ref[...] = dy.astype(BF16)

    row = pl.BlockSpec((tm, d), lambda i: (i, 0))
    vec = lambda r: pl.BlockSpec((r, d), lambda i: (0, 0))
    return pl.pallas_call(
        body, name="post_bwd", grid=(m // tm,),
        in_specs=[row, row, vec(1), vec(2)], out_specs=[row, vec(2), vec(1)],
        out_shape=[jax.ShapeDtypeStruct((m, d), BF16), jax.ShapeDtypeStruct((2, d), F32),
                   jax.ShapeDtypeStruct((1, d), F32)],
        compiler_params=_cp(),
    )(dxo, y, pg, gate)


def _split_rows(m):
    for cand in (1152, 1024, 768, 512, 384, 256, 128):
        if m % cand == 0 and m // cand >= 2:
            return cand
    return m


def mm_nn(a, w3, sec, tn, name):
    m, k = a.shape
    q, _, ws = w3.shape
    n = q * ws
    tpq, tps = ws // tn, sec // tn
    tm = next(c for c in (768, 512, 256, 128) if m % c == 0)

    def body(a_ref, w_ref, o_ref):
        w = w_ref[...]

        def step(i, carry):
            rows = pl.ds(pl.multiple_of(i * tm, tm), tm)
            o_ref[rows, :] = _dot(a_ref[rows, :], w)
            return carry

        lax.fori_loop(0, m // tm, step, 0)

    return pl.pallas_call(
        body, name=name, grid=(n // tn,),
        in_specs=[pl.BlockSpec((m, k), lambda j: (0, 0)),
                  pl.BlockSpec((None, k, tn), lambda j: (j // tpq, 0, j % tpq))],
        out_specs=pl.BlockSpec((None, m, tn), lambda j: (j // tps, 0, j % tps)),
        out_shape=jax.ShapeDtypeStruct((n // sec, m, sec), F32), compiler_params=_cp(),
    )(a, w3)


def _two_stacks(a3, b3, tn):
    sec = a3.shape[2]
    tps = sec // tn
    n1 = a3.shape[0] * tps
    first = lambda j: (jnp.minimum(j, n1 - 1) // tps, jnp.minimum(j, n1 - 1) % tps)
    second = lambda j: (jnp.maximum(j - n1, 0) // tps, jnp.maximum(j - n1, 0) % tps)
    return n1, first, second


def mm_nt(a3, b3, w3, tn, name):
    if b3 is None:
        b3 = a3
    _, m, sec = a3.shape
    q, k, ws = w3.shape
    n = q * ws
    tpq = ws // tn
    mb = _split_rows(m)
    n1, first, second = _two_stacks(a3, b3, tn)

    def body(a_ref, b_ref, w_ref, o_ref):
        j = pl.program_id(1)

        @pl.when(j == 0)
        def _():
            o_ref[...] = jnp.zeros_like(o_ref)

        @pl.when(j < n1)
        def _():
            o_ref[...] += _dot(a_ref[...], w_ref[...], NT)

        @pl.when(j >= n1)
        def _():
            o_ref[...] += _dot(b_ref[...], w_ref[...], NT)

    return pl.pallas_call(
        body, name=name, grid=(m // mb, n // tn),
        in_specs=[pl.BlockSpec((None, mb, tn), lambda i, j: (first(j)[0], i, first(j)[1])),
                  pl.BlockSpec((None, mb, tn), lambda i, j: (second(j)[0], i, second(j)[1])),
                  pl.BlockSpec((None, k, tn), lambda i, j: (j // tpq, 0, j % tpq))],
        out_specs=pl.BlockSpec((mb, k), lambda i, j: (i, 0)),
        out_shape=jax.ShapeDtypeStruct((m, k), F32), compiler_params=_cp(),
    )(a3, b3, w3)


def mm_tn(a, b3, c3, ws, tn, name):
    m, k = a.shape
    sec = b3.shape[2]
    n = (b3.shape[0] + (0 if c3 is None else c3.shape[0])) * sec
    if c3 is None:
        c3 = b3
    tpq = ws // tn
    kb = 256 if k % 256 == 0 else 128
    n1, first, second = _two_stacks(b3, c3, tn)

    def body(a_ref, b_ref, c_ref, o_ref):
        def product(rhs_ref):
            rhs = rhs_ref[...]
            for i in range(k // kb):
                o_ref[i * kb:(i + 1) * kb, :] = _dot(a_ref[:, i * kb:(i + 1) * kb], rhs, TN).astype(BF16)

        @pl.when(pl.program_id(0) < n1)
        def _():
            product(b_ref)

        @pl.when(pl.program_id(0) >= n1)
        def _():
            product(c_ref)

    return pl.pallas_call(
        body, name=name, grid=(n // tn,),
        in_specs=[pl.BlockSpec((m, k), lambda j: (0, 0)),
                  pl.BlockSpec((None, m, tn), lambda j: (first(j)[0], 0, first(j)[1])),
                  pl.BlockSpec((None, m, tn), lambda j: (second(j)[0], 0, second(j)[1]))],
        out_specs=pl.BlockSpec((None, k, tn), lambda j: (j // tpq, 0, j % tpq)),
        out_shape=jax.ShapeDtypeStruct((n // ws, k, ws), BF16), compiler_params=_cp(),
    )(a, b3, c3)


POOL_REACH = 8 << GRID_W_LOG2


def _token_parts(tok, cx):
    lat = tok - cx
    return tok < cx, lat >> GRID_W_LOG2, lat & ((1 << GRID_W_LOG2) - 1)


def _pool_mask(gi, row0, col0, tm, ncols, cx, transposed):
    half = jnp.left_shift(1, gi)
    r = lax.broadcasted_iota(jnp.int32, (tm, 1), 0) + row0
    c = lax.broadcasted_iota(jnp.int32, (1, ncols), 1) + col0
    out_tok, src_tok = (c, r) if transposed else (r, c)
    o_ctx, o_row, o_col = _token_parts(out_tok, cx)
    s_ctx, s_row, s_col = _token_parts(src_tok, cx)

    def inside(o, s):
        return (s >= o - half) & (s <= o + half - 1)

    ctx_hit = o_ctx & s_ctx & inside(out_tok, src_tok)
    lat_hit = (~o_ctx) & (~s_ctx) & inside(o_row, s_row) & inside(o_col, s_col)
    return jnp.where(ctx_hit | lat_hit, 1.0, 0.0).astype(BF16)


def _pool_inv_count(gi, row0, tm, cx, seq):
    half = jnp.left_shift(1, gi)
    r = lax.broadcasted_iota(jnp.int32, (tm, 1), 0) + row0
    is_ctx, row, col = _token_parts(r, cx)

    def count(pos, size):
        return jnp.minimum(pos + half - 1, size - 1) - jnp.maximum(pos - half, 0) + 1

    cnt = jnp.where(is_ctx, count(r, cx), count(row, seq >> GRID_W_LOG2) * count(col, 1 << GRID_W_LOG2))
    return 1.0 / cnt.astype(F32)


def _lat_band(tm):
    side = POOL_REACH // tm
    return side, 2 * side + 1


def _lat_mask(gi, tm, cx, transposed):
    side, band = _lat_band(tm)
    return _pool_mask(gi, cx + side * tm, cx, tm, band * tm, cx, transposed)


def _store_padded_lat(dst_ref, lat, tm):
    side, _ = _lat_band(tm)
    seq = lat.shape[0]
    zeros = jnp.zeros((side * tm, lat.shape[1]), dst_ref.dtype)
    dst_ref[0:side * tm, :] = zeros
    dst_ref[side * tm + seq:, :] = zeros
    dst_ref[side * tm:side * tm + seq, :] = lat.astype(dst_ref.dtype)


def mix_a_fwd(z0, pool_w, pool_scale, cx):
    _, t, half_d = z0.shape
    g = half_d // N_POOL
    seq = t - cx
    tm = _row_block(cx)
    side, band = _lat_band(tm)

    def body(v_ref, ag_ref, w_ref, sc_ref, u_ref, vlat_ref, mask_ref):
        gi = pl.program_id(0)
        w = w_ref[...].astype(BF16)
        sc = sc_ref[...]
        _store_padded_lat(vlat_ref, v_ref[cx:, :], tm)
        mask_ref[...] = _lat_mask(gi, tm, cx, False)

        def finish(row0, window_sum):
            rows = pl.ds(row0, tm)
            pooled = window_sum * _pool_inv_count(gi, row0, tm, cx, seq) - v_ref[rows, :]
            mixed = _dot(pooled.astype(BF16), w) * sc
            u_ref[rows, :] = (mixed * _silu(ag_ref[rows, :])).astype(BF16)

        vctx = v_ref[0:cx, :].astype(BF16)
        for i in range(cx // tm):
            finish(i * tm, _dot(_pool_mask(gi, i * tm, 0, tm, cx, cx, False), vctx))

        def step(j, carry):
            src = vlat_ref[pl.ds(pl.multiple_of(j * tm, tm), band * tm), :]
            finish(pl.multiple_of(cx + j * tm, tm), _dot(mask_ref[...], src))
            return carry

        lax.fori_loop(0, seq // tm, step, 0)

    sec = lambda s: pl.BlockSpec((None, t, g), lambda j: (s, 0, j))
    return pl.pallas_call(
        body, name="mix_a_fwd", grid=(N_POOL,),
        in_specs=[sec(0), sec(1), pl.BlockSpec((None, g, g), lambda j: (j, 0, 0)),
                  pl.BlockSpec((1, g), lambda j: (0, j))],
        out_specs=pl.BlockSpec((t, g), lambda j: (0, j)),
        out_shape=jax.ShapeDtypeStruct((t, 2 * half_d), BF16),
        scratch_shapes=[pltpu.VMEM((seq + 2 * side * tm, g), BF16), pltpu.VMEM((tm, band * tm), BF16)],
        compiler_params=_cp(),
    )(z0, z0, pool_w, pool_scale)


def mix_a_bwd(z0, du, pool_w, pool_scale, cx):
    _, t, half_d = z0.shape
    g = half_d // N_POOL
    seq = t - cx
    tm = _row_block(cx)
    gq = g // 4
    side, band = _lat_band(tm)

    def body(v_ref, ag_ref, du_ref, w_ref, sc_ref, dz_ref, dw_ref, dsc_ref,
             vlat_ref, mask_ref, pooled_ref, dmx_ref, dpl_ref, wlat_ref, wctx_ref):
        gi = pl.program_id(0)
        w = w_ref[...].astype(BF16)
        sc = sc_ref[...]
        _store_padded_lat(vlat_ref, v_ref[cx:, :], tm)
        _store_padded_lat(wlat_ref, jnp.zeros((seq, g), BF16), tm)
        mask_ref[...] = _lat_mask(gi, tm, cx, False)

        def first(row0, window_sum, weighted_ref, weighted_row0):
            rows = pl.ds(row0, tm)
            inv = _pool_inv_count(gi, row0, tm, cx, seq)
            pooled = (window_sum * inv - v_ref[rows, :]).astype(BF16)
            pooled_ref[rows, :] = pooled
            mixed = _dot(pooled, w)
            ag = ag_ref[rows, :]
            duv = du_ref[rows, :]
            dz_ref[1, rows, :] = (duv * (mixed * sc) * _dsilu(ag)).astype(BF16)
            dms = duv * _silu(ag)
            dmixed = (dms * sc).astype(BF16)
            dmx_ref[rows, :] = dmixed
            dpooled = _dot(dmixed, w, NT)
            dpl_ref[rows, :] = dpooled
            weighted_ref[pl.ds(weighted_row0, tm), :] = (dpooled * inv).astype(BF16)
            return jnp.sum(dms * mixed, axis=0, keepdims=True)

        dsc = jnp.zeros((1, g), F32)
        vctx = v_ref[0:cx, :].astype(BF16)
        for i in range(cx // tm):
            dsc += first(i * tm, _dot(_pool_mask(gi, i * tm, 0, tm, cx, cx, False), vctx), wctx_ref, i * tm)

        def first_lat(j, acc):
            src = vlat_ref[pl.ds(pl.multiple_of(j * tm, tm), band * tm), :]
            return acc + first(pl.multiple_of(cx + j * tm, tm), _dot(mask_ref[...], src),
                               wlat_ref, pl.multiple_of((side + j) * tm, tm))

        dsc_ref[...] = lax.fori_loop(0, seq // tm, first_lat, dsc)
        dw = _dot(pooled_ref[...], dmx_ref[...], TN)
        for qi in range(4):
            dw_ref[qi] = dw[qi * gq:(qi + 1) * gq, :]

        wctx = wctx_ref[...]
        for i in range(cx // tm):
            rows = pl.ds(i * tm, tm)
            dz_ref[0, rows, :] = (_dot(_pool_mask(gi, i * tm, 0, tm, cx, cx, True), wctx)
                                  - dpl_ref[rows, :]).astype(BF16)
        mask_ref[...] = _lat_mask(gi, tm, cx, True)

        def second_lat(j, carry):
            rows = pl.ds(pl.multiple_of(cx + j * tm, tm), tm)
            src = wlat_ref[pl.ds(pl.multiple_of(j * tm, tm), band * tm), :]
            dz_ref[0, rows, :] = (_dot(mask_ref[...], src) - dpl_ref[rows, :]).astype(BF16)
            return carry

        lax.fori_loop(0, seq // tm, second_lat, 0)

    sec = lambda s: pl.BlockSpec((None, t, g), lambda j: (s, 0, j))
    padded = pltpu.VMEM((seq + 2 * side * tm, g), BF16)
    return pl.pallas_call(
        body, name="mix_a_bwd", grid=(N_POOL,),
        in_specs=[sec(0), sec(1), pl.BlockSpec((t, g), lambda j: (0, j)),
                  pl.BlockSpec((None, g, g), lambda j: (j, 0, 0)),
                  pl.BlockSpec((1, g), lambda j: (0, j))],
        out_specs=[pl.BlockSpec((2, t, g), lambda j: (0, 0, j)),
                   pl.BlockSpec((4, None, gq, g), lambda j: (0, j, 0, 0)),
                   pl.BlockSpec((1, g), lambda j: (0, j))],
        out_shape=[jax.ShapeDtypeStruct((2, t, half_d), BF16),
                   jax.ShapeDtypeStruct((4, N_POOL, gq, g), F32),
                   jax.ShapeDtypeStruct((1, half_d), F32)],
        scratch_shapes=[padded, pltpu.VMEM((tm, band * tm), BF16), pltpu.VMEM((t, g), BF16),
                        pltpu.VMEM((t, g), BF16), pltpu.VMEM((t, g), F32), padded, pltpu.VMEM((cx, g), BF16)],
        compiler_params=_cp(),
    )(z0, z0, du, pool_w, pool_scale)


def _conv_masks(t, cx):
    r = lax.broadcasted_iota(jnp.int32, (t, 1), 0)
    has_prev = jnp.where((r == 0) | (r == cx), 0.0, 1.0)
    has_next = jnp.where((r == cx - 1) | (r == t - 1), 0.0, 1.0)
    return has_prev, has_next


def mix_b_fwd(z0, conv_w, conv_b, u, cx):
    _, t, half_d = z0.shape
    gb = 128
    off = half_d // gb

    def body(bx_ref, bb_ref, bc_ref, bg_ref, w_ref, b_ref, _, u_ref):
        has_prev, has_next = _conv_masks(t, cx)
        tt = bc_ref[...] * bx_ref[...]
        prev = pltpu.roll(tt, 1, 0) * has_prev
        nxt = pltpu.roll(tt, t - 1, 0) * has_next
        cv = prev * w_ref[0:1, :] + tt * w_ref[1:2, :] + nxt * w_ref[2:3, :] + b_ref[...]
        u_ref[...] = (bb_ref[...] * cv * _silu(bg_ref[...])).astype(BF16)

    sec = lambda s: pl.BlockSpec((None, t, gb), lambda j: (s, 0, j))
    return pl.pallas_call(
        body, name="mix_b_fwd", grid=(half_d // gb,),
        in_specs=[sec(2), sec(3), sec(4), sec(5), pl.BlockSpec((3, gb), lambda j: (0, j)),
                  pl.BlockSpec((1, gb), lambda j: (0, j)), ANY],
        out_specs=pl.BlockSpec((t, gb), lambda j: (0, j + off)),
        out_shape=jax.ShapeDtypeStruct((t, 2 * half_d), BF16), input_output_aliases={6: 0},
        compiler_params=_cp(),
    )(z0, z0, z0, z0, conv_w, conv_b, u)


def mix_b_bwd(z0, du, conv_w, conv_b, cx):
    _, t, half_d = z0.shape
    gb = 128
    off = half_d // gb

    def body(bx_ref, bb_ref, bc_ref, bg_ref, du_ref, w_ref, b_ref, dz_ref, dw_ref, db_ref):
        has_prev, has_next = _conv_masks(t, cx)
        bx, bb, bc, bg = bx_ref[...], bb_ref[...], bc_ref[...], bg_ref[...]
        duv = du_ref[...]
        tt = bc * bx
        prev = pltpu.roll(tt, 1, 0) * has_prev
        nxt = pltpu.roll(tt, t - 1, 0) * has_next
        w0, w1, w2 = w_ref[0:1, :], w_ref[1:2, :], w_ref[2:3, :]
        cv = prev * w0 + tt * w1 + nxt * w2 + b_ref[...]
        sg = _silu(bg)
        dz_ref[1] = (duv * cv * sg).astype(BF16)
        dz_ref[3] = (duv * bb * cv * _dsilu(bg)).astype(BF16)
        dcv = duv * bb * sg
        dw_ref[0:1, :] = jnp.sum(dcv * prev, axis=0, keepdims=True)
        dw_ref[1:2, :] = jnp.sum(dcv * tt, axis=0, keepdims=True)
        dw_ref[2:3, :] = jnp.sum(dcv * nxt, axis=0, keepdims=True)
        db_ref[...] = jnp.sum(dcv, axis=0, keepdims=True)
        dt = (pltpu.roll(dcv * has_prev, t - 1, 0) * w0 + dcv * w1
              + pltpu.roll(dcv * has_next, 1, 0) * w2)
        dz_ref[0] = (dt * bc).astype(BF16)
        dz_ref[2] = (dt * bx).astype(BF16)

    sec = lambda s: pl.BlockSpec((None, t, gb), lambda j: (s, 0, j))
    return pl.pallas_call(
        body, name="mix_b_bwd", grid=(half_d // gb,),
        in_specs=[sec(2), sec(3), sec(4), sec(5), pl.BlockSpec((t, gb), lambda j: (0, j + off)),
                  pl.BlockSpec((3, gb), lambda j: (0, j)), pl.BlockSpec((1, gb), lambda j: (0, j))],
        out_specs=[pl.BlockSpec((4, t, gb), lambda j: (0, 0, j)),
                   pl.BlockSpec((3, gb), lambda j: (0, j)), pl.BlockSpec((1, gb), lambda j: (0, j))],
        out_shape=[jax.ShapeDtypeStruct((4, t, half_d), BF16),
                   jax.ShapeDtypeStruct((3, half_d), F32), jax.ShapeDtypeStruct((1, half_d), F32)],
        compiler_params=_cp(),
    )(z0, z0, z0, z0, du, conv_w, conv_b)


def _lower_bound(lbl_ref, d):
    l0, l1, l2 = lbl_ref[d, 0:1, :], lbl_ref[d, 1:2, :], lbl_ref[d, 2:3, :]
    mx = jnp.maximum(jnp.maximum(l0, l1), l2)
    e0, e1, e2 = jnp.exp(l0 - mx), jnp.exp(l1 - mx), jnp.exp(l2 - mx)
    inv = 1.0 / (e0 + e1 + e2)
    return (e0 + e1) * inv, (e0 * inv, e1 * inv, e2 * inv)


def _chunk_consts(d):
    r = lax.broadcasted_iota(jnp.int32, (CHUNK, CHUNK), 0)
    c = lax.broadcasted_iota(jnp.int32, (CHUNK, CHUNK), 1)
    keep = (c <= r) if d == 0 else (c >= r)
    return jnp.where(keep, 1.0, 0.0).astype(F32), keep


def _chunk_of_step(s, d, nc, ncc):
    if d == 0:
        return s
    return jnp.where(s < ncc, ncc - 1 - s, nc - 1 + ncc - s)


def _gates(z, lbv):
    e = jnp.exp(-jnp.abs(z))
    r = 1.0 / (1.0 + e)
    er = e * r
    pos = z >= 0.0
    sig = jnp.where(pos, r, er)
    nsig = jnp.where(pos, er, r)
    return sig, nsig, lbv + (1.0 - lbv) * sig


def _split3(x):
    hi = x.astype(BF16)
    r1 = x - hi.astype(F32)
    mid = r1.astype(BF16)
    lo = (r1 - mid.astype(F32)).astype(BF16)
    return jnp.concatenate([hi, mid, lo], axis=1)


def _cumsum_chunk(cum, x):
    y = _dot(cum, _split3(x))
    return y[:, :HEAD] + y[:, HEAD:2 * HEAD] + y[:, 2 * HEAD:]


def _chunk_rows(n):
    return pl.ds(pl.multiple_of(n * CHUNK, CHUNK), CHUNK)


def _group(nc, prefer=(4, 3, 2, 1)):
    return next(u for u in prefer if nc % u == 0)


WIDE_GROUP = (12, 6, 4, 3, 2, 1)


def _decay_pass(lf_ref, bc_ref, dec_ref, cum, nc):
    grp = _group(nc, WIDE_GROUP)

    def step(m, carry):
        ns = [m * grp + u for u in range(grp)]
        lfc = [lf_ref[_chunk_rows(n), :] for n in ns]
        bc = [_cumsum_chunk(cum, x) for x in lfc]
        for u, n in enumerate(ns):
            bc_ref[_chunk_rows(n), :] = bc[u]
            dec_ref[n] = jnp.broadcast_to(jnp.exp(jnp.sum(lfc[u], axis=0, keepdims=True)), (8, HEAD))
        return carry

    lax.fori_loop(0, nc // grp, step, 0)


def hgrn_fwd(z1, lbl, onorm, cx):
    _, t, d = z1.shape
    seq = t - cx
    nc, ncc = t // CHUNK, cx // CHUNK

    grp, sgrp = _group(nc, (9, 6, 4, 3, 2, 1)), _group(nc, WIDE_GROUP)

    def body(zf_ref, zb_ref, v_ref, q_ref, g_ref, lbl_ref, on_ref, o_ref, r_ref, bcs_ref, ks_ref, decs_ref,
             lf_ref, k_ref, bc_ref, dec_ref, qd_ref, ki_ref, oacc_ref, ds_ref):
        for dr, z_ref in ((0, zf_ref), (1, zb_ref)):
            lbv, _ = _lower_bound(lbl_ref, dr)
            _, nsig, f = _gates(z_ref[...], lbv)
            lf_ref[...] = jnp.log(f)
            k_ref[...] = (1.0 - lbv) * nsig
            cum, keep = _chunk_consts(dr)
            _decay_pass(lf_ref, bc_ref, dec_ref, cum.astype(BF16), nc)
            bc = bc_ref[...]
            bcs_ref[dr] = bc
            ks_ref[dr] = k_ref[...]
            decs_ref[dr] = dec_ref[...]
            qd_ref[...] = (q_ref[...] * jnp.exp(bc)).astype(BF16)
            ki_ref[...] = (k_ref[...] * jnp.exp(-bc)).astype(BF16)

            def local_step(m, carry, dr=dr, keep=keep):
                ns = [m * grp + u for u in range(grp)]
                rows = [_chunk_rows(n) for n in ns]
                qd = [qd_ref[r, :] for r in rows]
                ki = [ki_ref[r, :] for r in rows]
                vc = [v_ref[r, :].astype(BF16) for r in rows]
                sc = [_dot(qd[u], ki[u], NT) for u in range(grp)]
                inc = [_dot(vc[u], ki[u], TN) for u in range(grp)]
                a = [jnp.where(keep, s, 0.0).astype(BF16) for s in sc]
                intra = [_dot(a[u], vc[u]) for u in range(grp)]
                for u in range(grp):
                    ds_ref[ns[u]] = inc[u] * dec_ref[ns[u]][0:1, :]
                    if dr == 0:
                        oacc_ref[rows[u], :] = intra[u]
                    else:
                        oacc_ref[rows[u], :] += intra[u]
                return carry

            lax.fori_loop(0, nc // grp, local_step, 0)

            def state_step(m, st, dr=dr):
                ns = [_chunk_of_step(m * sgrp + u, dr, nc, ncc) for u in range(sgrp)]
                rows = [_chunk_rows(n) for n in ns]
                sts = []
                for n in ns:
                    sts.append(st.astype(BF16))
                    st = st * dec_ref[n][0:1, :] + ds_ref[n]
                inter = [_dot(qd_ref[rows[u], :], sts[u], NT) for u in range(sgrp)]
                for u in range(sgrp):
                    oacc_ref[rows[u], :] += inter[u]
                return st

            lax.fori_loop(0, nc // sgrp, state_step, jnp.zeros((HEAD, HEAD), F32))

        o = oacc_ref[cx:, :]
        o_ref[...] = o
        rstd = lax.rsqrt(jnp.mean(o * o, axis=-1, keepdims=True) + EPS)
        r_ref[...] = (o * rstd * on_ref[...] * _silu(g_ref[cx:, :])).astype(BF16)

    sec = lambda s: pl.BlockSpec((None, t, HEAD), lambda h: (s, 0, h))
    col = pl.BlockSpec((seq, HEAD), lambda h: (0, h))
    tf32, tb16 = pltpu.VMEM((t, HEAD), F32), pltpu.VMEM((t, HEAD), BF16)
    return pl.pallas_call(
        body, name="hgrn_fwd", grid=(d // HEAD,),
        in_specs=[sec(0), sec(1), sec(2), sec(3), sec(4),
                  pl.BlockSpec((2, 3, HEAD), lambda h: (0, 0, h)), pl.BlockSpec((1, HEAD), lambda h: (0, h))],
        out_specs=[col, col, pl.BlockSpec((2, t, HEAD), lambda h: (0, 0, h)),
                   pl.BlockSpec((2, t, HEAD), lambda h: (0, 0, h)),
                   pl.BlockSpec((2, nc, 8, HEAD), lambda h: (0, 0, 0, h))],
        out_shape=[jax.ShapeDtypeStruct((seq, d), F32), jax.ShapeDtypeStruct((seq, d), BF16),
                   jax.ShapeDtypeStruct((2, t, d), F32), jax.ShapeDtypeStruct((2, t, d), F32),
                   jax.ShapeDtypeStruct((2, nc, 8, d), F32)],
        scratch_shapes=[tf32, tf32, tf32, pltpu.VMEM((nc, 8, HEAD), F32), tb16, tb16, tf32,
                        pltpu.VMEM((nc, HEAD, HEAD), F32)],
        compiler_params=_cp(),
    )(z1, z1, z1, z1, z1, lbl, onorm)


def hgrn_bwd(z1, lbl, onorm, o, dr_out, bcs, ks, decs, cx):
    _, t, d = z1.shape
    seq = t - cx
    nc, ncc = t // CHUNK, cx // CHUNK

    grp2, grp = _group(nc, (9, 6, 4, 3, 2, 1)), _group(nc, (12, 9, 6, 4, 3, 2, 1))

    def body(zf_ref, zb_ref, v_ref, q_ref, g_ref, lbl_ref, on_ref, o_ref, dr_ref, bcs_ref, ks_ref, decs_ref,
             dz_ref, don_ref, dlb_ref,
             qd_ref, ki_ref, do_ref, dqd_ref, dki_ref, dq_ref, dv_ref, ds_ref, dsl_ref):
        o = o_ref[...]
        g = g_ref[cx:, :]
        drv = dr_ref[...]
        onv = on_ref[...]
        rstd = lax.rsqrt(jnp.mean(o * o, axis=-1, keepdims=True) + EPS)
        ohat = o * rstd
        sg = _silu(g)
        don_ref[...] = jnp.sum(drv * ohat * sg, axis=0, keepdims=True)
        dz_ref[4, :cx, :] = jnp.zeros((cx, HEAD), BF16)
        dz_ref[4, cx:, :] = (drv * ohat * onv * _dsilu(g)).astype(BF16)
        dohat = drv * onv * sg
        do_ref[:cx, :] = jnp.zeros((cx, HEAD), BF16)
        do_ref[cx:, :] = (rstd * (dohat - ohat * jnp.mean(dohat * ohat, axis=-1, keepdims=True))).astype(BF16)

        for dr, z_ref in ((0, zf_ref), (1, zb_ref)):
            lbv, _ = _lower_bound(lbl_ref, dr)
            k_ref, bc_ref, dec_ref = ks_ref.at[dr], bcs_ref.at[dr], decs_ref.at[dr]
            _, keep = _chunk_consts(dr)
            cum_t = _chunk_consts(1 - dr)[0].astype(BF16)
            bc = bc_ref[...]
            qd_ref[...] = (q_ref[...] * jnp.exp(bc)).astype(BF16)
            ki_ref[...] = (k_ref[...] * jnp.exp(-bc)).astype(BF16)

            def local_step(m, carry, dr=dr, keep=keep):
                ns = [m * grp + u for u in range(grp)]
                rows = [_chunk_rows(n) for n in ns]
                rng = range(grp)
                qd = [qd_ref[r, :] for r in rows]
                ki = [ki_ref[r, :] for r in rows]
                doc = [do_ref[r, :] for r in rows]
                vc = [v_ref[r, :].astype(BF16) for r in rows]
                sc = [_dot(qd[u], ki[u], NT) for u in rng]
                dsc = [_dot(doc[u], vc[u], NT) for u in rng]
                inc = [_dot(vc[u], ki[u], TN) for u in rng]
                dinc = [_dot(doc[u], qd[u], TN) for u in rng]
                a = [jnp.where(keep, s, 0.0).astype(BF16) for s in sc]
                da = [jnp.where(keep, s, 0.0).astype(BF16) for s in dsc]
                dqd = [_dot(da[u], ki[u]) for u in rng]
                dki = [_dot(da[u], qd[u], TN) for u in rng]
                dv = [_dot(a[u], doc[u], TN) for u in rng]
                for u in rng:
                    ds_ref[ns[u]] = inc[u] * dec_ref[ns[u]][0:1, :]
                    dsl_ref[ns[u]] = dinc[u]
                    dqd_ref[rows[u], :] = dqd[u]
                    dki_ref[rows[u], :] = dki[u]
                    if dr == 0:
                        dv_ref[rows[u], :] = dv[u]
                    else:
                        dv_ref[rows[u], :] += dv[u]
                return carry

            lax.fori_loop(0, nc // grp, local_step, 0)

            def state_step(s, st, dr=dr):
                n = _chunk_of_step(s, dr, nc, ncc)
                inc = ds_ref[n]
                ds_ref[n] = st
                return st * dec_ref[n][0:1, :] + inc

            lax.fori_loop(0, nc, state_step, jnp.zeros((HEAD, HEAD), F32), unroll=4)

            def dstate_step(s, dst, dr=dr):
                n = _chunk_of_step(nc - 1 - s, dr, nc, ncc)
                inc = dsl_ref[n]
                dsl_ref[n] = dst
                return inc + dst * dec_ref[n][0:1, :]

            lax.fori_loop(0, nc, dstate_step, jnp.zeros((HEAD, HEAD), F32), unroll=4)

            def grad_step(m, carry, dr=dr, cum_t=cum_t):
                ns = [m * grp2 + u for u in range(grp2)]
                rows = [_chunk_rows(n) for n in ns]
                rng = range(grp2)
                st0 = [ds_ref[n] for n in ns]
                dst = [dsl_ref[n] for n in ns]
                dstb = [x.astype(BF16) for x in dst]
                dec = [dec_ref[n][0:1, :] for n in ns]
                doc = [do_ref[r, :] for r in rows]
                vc = [v_ref[r, :].astype(BF16) for r in rows]
                e = [jnp.exp(bc_ref[r, :]) for r in rows]
                einv = [jnp.exp(-bc_ref[r, :]) for r in rows]
                qd = [q_ref[rows[u], :] * e[u] for u in rng]
                ki = [k_ref[rows[u], :] * einv[u] for u in rng]
                kd = [ki[u] * dec[u] for u in rng]
                dqd_st = [_dot(doc[u], st0[u].astype(BF16)) for u in rng]
                dkd = [_dot(vc[u], dstb[u]) for u in rng]
                dv_st = [_dot(kd[u].astype(BF16), dstb[u], NT) for u in rng]
                dqd = [dqd_ref[rows[u], :] + dqd_st[u] for u in rng]
                dki = [dki_ref[r, :] for r in rows]
                dbc = [dqd[u] * qd[u] - dki[u] * ki[u] - dkd[u] * kd[u] for u in rng]
                cs = [_cumsum_chunk(cum_t, x) for x in dbc]
                for u in rng:
                    ddec = jnp.sum(dst[u] * st0[u], axis=0, keepdims=True)
                    dbl = jnp.sum(dkd[u] * kd[u], axis=0, keepdims=True) + ddec * dec[u]
                    dv_ref[rows[u], :] += dv_st[u]
                    dqd_ref[rows[u], :] = cs[u] + dbl
                    dki_ref[rows[u], :] = dki[u] * einv[u] + dkd[u] * (einv[u] * dec[u])
                    if dr == 0:
                        dq_ref[rows[u], :] = dqd[u] * e[u]
                    else:
                        dq_ref[rows[u], :] += dqd[u] * e[u]
                return carry

            lax.fori_loop(0, nc // grp2, grad_step, 0)

            sig, nsig, f = _gates(z_ref[...], lbv)
            common = (dqd_ref[...] / f - dki_ref[...]) * nsig
            dz_ref[dr] = (common * ((1.0 - lbv) * sig)).astype(BF16)
            dlb_ref[dr:dr + 1, :] = jnp.sum(common, axis=0, keepdims=True)

        dz_ref[2] = dv_ref[...].astype(BF16)
        dz_ref[3] = dq_ref[...].astype(BF16)

    sec = lambda s: pl.BlockSpec((None, t, HEAD), lambda h: (s, 0, h))
    col = pl.BlockSpec((seq, HEAD), lambda h: (0, h))
    tf32, tb16 = pltpu.VMEM((t, HEAD), F32), pltpu.VMEM((t, HEAD), BF16)
    states = pltpu.VMEM((nc, HEAD, HEAD), F32)
    return pl.pallas_call(
        body, name="hgrn_bwd", grid=(d // HEAD,),
        in_specs=[sec(0), sec(1), sec(2), sec(3), sec(4),
                  pl.BlockSpec((2, 3, HEAD), lambda h: (0, 0, h)), pl.BlockSpec((1, HEAD), lambda h: (0, h)),
                  col, col, pl.BlockSpec((2, t, HEAD), lambda h: (0, 0, h)),
                  pl.BlockSpec((2, t, HEAD), lambda h: (0, 0, h)),
                  pl.BlockSpec((2, nc, 8, HEAD), lambda h: (0, 0, 0, h))],
        out_specs=[pl.BlockSpec((5, t, HEAD), lambda h: (0, 0, h)),
                   pl.BlockSpec((1, HEAD), lambda h: (0, h)), pl.BlockSpec((2, HEAD), lambda h: (0, h))],
        out_shape=[jax.ShapeDtypeStruct((5, t, d), BF16), jax.ShapeDtypeStruct((1, d), F32),
                   jax.ShapeDtypeStruct((2, d), F32)],
        scratch_shapes=[tb16, tb16, tb16, tf32, tf32, tf32, tf32, states, states],
        compiler_params=_cp(),
    )(z1, z1, z1, z1, z1, lbl, onorm, o, dr_out, bcs, ks, decs)


def _place():
    x, y, c = lax.axis_index("x"), lax.axis_index("y"), lax.axis_index("c")
    chips = [(1 - x, y), (x, 1 - y), (1 - x, 1 - y)]
    return x, y, c, chips


def _relay_chips():
    x, y, c, _ = _place()
    first = c == 0
    near = (jnp.where(first, 1 - x, x), jnp.where(first, y, 1 - y))
    far = (jnp.where(first, x, 1 - x), jnp.where(first, 1 - y, y))
    return near, far, (1 - x, 1 - y)


def allgather_shards(bufs, after):
    n = len(bufs)

    def body(*refs):
        outs = refs[n + 1:2 * n + 1]
        done_ref, send_sems, recv_sems = refs[2 * n + 1:]
        done_ref[...] = jnp.zeros((8, 128), F32)
        x, y, c, _ = _place()
        me = (x, y, c)
        p = 2 * x + y
        near, far, diag = _relay_chips()
        half = [pl.ds(c * (s.shape[1] // 2), s.shape[1] // 2) for s in bufs]
        other = [pl.ds((1 - c) * (s.shape[1] // 2), s.shape[1] // 2) for s in bufs]
        slot = lambda chip: 2 * chip[0] + chip[1]

        def remote(i, k, ref, to):
            return pltpu.make_async_remote_copy(src_ref=ref, dst_ref=ref, send_sem=send_sems.at[6 * i + k],
                                                recv_sem=recv_sems.at[6 * i + k], device_id=to, device_id_type=MESH)

        sends = []

        def send(i, k, ref, to):
            cp = remote(i, k, ref, to)
            cp.start()
            sends.append(cp)

        for i in range(n):
            mine = outs[i].at[p, half[i]]
            send(i, 0, mine, (*near, c))
            send(i, 1, mine, (*far, c))
        for i in range(n):
            landed = outs[i].at[slot(near), half[i]]
            remote(i, 0, landed, me).wait_recv()
            send(i, 2, landed, (*far, c))
            send(i, 3, landed, (x, y, 1 - c))
        for i in range(n):
            landed = outs[i].at[slot(far), half[i]]
            remote(i, 1, landed, me).wait_recv()
            send(i, 4, landed, (x, y, 1 - c))
        for i in range(n):
            landed = outs[i].at[slot(diag), half[i]]
            remote(i, 2, landed, me).wait_recv()
            send(i, 5, landed, (x, y, 1 - c))
        for i in range(n):
            for k, chip in ((3, far), (4, near), (5, diag)):
                remote(i, k, outs[i].at[slot(chip), other[i]], me).wait_recv()
        for cp in sends:
            cp.wait_send()

    return pl.pallas_call(
        body, name="allgather_shards",
        in_specs=[ANY] * (n + 1), out_specs=[ANY] * n + [VMEM],
        out_shape=[jax.ShapeDtypeStruct(s.shape, s.dtype) for s in bufs] + [jax.ShapeDtypeStruct((8, 128), F32)],
        input_output_aliases={i: i for i in range(n)},
        scratch_shapes=[pltpu.SemaphoreType.DMA((6 * n,)), pltpu.SemaphoreType.DMA((6 * n,))],
        compiler_params=pltpu.CompilerParams(has_side_effects=True),
    )(*bufs, after)


def pair_sum(grad, got, chip_core):
    _, r, cc = grad.shape
    hr = r // 2
    tr = 256 if hr % 256 == 0 else hr
    nb = hr // tr

    def body(cc_ref, a_ref, b_ref, own_ref, sb_ref):
        s = a_ref[...].astype(F32) + b_ref[...].astype(F32)
        sb_ref[...] = s.astype(BF16)

        @pl.when(pl.program_id(1) == cc_ref[0])
        def _():
            own_ref[...] = s

    grid_spec = pltpu.PrefetchScalarGridSpec(
        num_scalar_prefetch=1, grid=(nb, 4),
        in_specs=[pl.BlockSpec((None, tr, cc), lambda i, qi, cc_ref: (qi, cc_ref[1] * nb + i, 0)),
                  pl.BlockSpec((None, tr, cc), lambda i, qi, cc_ref: (qi, i, 0))],
        out_specs=[pl.BlockSpec((tr, cc), lambda i, qi, cc_ref: (i, 0)),
                   pl.BlockSpec((None, tr, cc), lambda i, qi, cc_ref: (qi, i, 0))])
    return pl.pallas_call(
        body, name="pair_sum", grid_spec=grid_spec,
        out_shape=[jax.ShapeDtypeStruct((hr, cc), F32), jax.ShapeDtypeStruct((4, hr, cc), BF16)],
        compiler_params=_cp(),
    )(chip_core, grad, got)


def owner_sum(own, got, chip_core):
    hr, cc = own.shape
    tr = 256 if hr % 256 == 0 else hr
    nb = hr // tr

    def body(cc_ref, a_ref, b_ref, o_ref):
        s = a_ref[...] + b_ref[0].astype(F32)
        s = s + b_ref[1].astype(F32)
        o_ref[...] = s + b_ref[2].astype(F32)

    grid_spec = pltpu.PrefetchScalarGridSpec(
        num_scalar_prefetch=1, grid=(nb,),
        in_specs=[pl.BlockSpec((tr, cc), lambda i, cc_ref: (i, 0)),
                  pl.BlockSpec((3, tr, cc), lambda i, cc_ref: (0, i, 0))],
        out_specs=pl.BlockSpec((tr, cc), lambda i, cc_ref: (cc_ref[1] * nb + i, 0)))
    return pl.pallas_call(
        body, name="owner_sum", grid_spec=grid_spec,
        out_shape=jax.ShapeDtypeStruct((2 * hr, cc), F32), compiler_params=_cp(),
    )(chip_core, own, got)


def share_halves(bufs):
    n = len(bufs)

    def body(*refs):
        outs = refs[n:2 * n]
        send_sems, recv_sems = refs[2 * n:]
        x, y, c, _ = _place()
        copies = []
        for i in range(n):
            hr = bufs[i].shape[0] // 2
            mine = outs[i].at[pl.ds(c * hr, hr)]
            cp = pltpu.make_async_remote_copy(
                src_ref=mine, dst_ref=mine, send_sem=send_sems.at[i], recv_sem=recv_sems.at[i],
                device_id=(x, y, 1 - c), device_id_type=MESH)
            cp.start()
            copies.append((cp, outs[i].at[pl.ds((1 - c) * hr, hr)]))
        for i, (cp, theirs) in enumerate(copies):
            cp.wait_send()
            pltpu.make_async_remote_copy(
                src_ref=theirs, dst_ref=theirs, send_sem=send_sems.at[i], recv_sem=recv_sems.at[i],
                device_id=(x, y, c), device_id_type=MESH).wait_recv()

    return pl.pallas_call(
        body, name="share_halves",
        in_specs=[ANY] * n, out_specs=[ANY] * n,
        out_shape=[jax.ShapeDtypeStruct(b.shape, b.dtype) for b in bufs],
        input_output_aliases={i: i for i in range(n)},
        scratch_shapes=[pltpu.SemaphoreType.DMA((n,)), pltpu.SemaphoreType.DMA((n,))],
        compiler_params=pltpu.CompilerParams(has_side_effects=True),
    )(*bufs)


def allgather8(v, name):
    r, n = v.shape

    def body(v_ref, out_ref, send_sems, recv_sems):
        x, y, c, _ = _place()
        me = 4 * x + 2 * y + c
        out_ref[me] = v_ref[...]

        def copy(k, slot, to):
            return pltpu.make_async_remote_copy(
                src_ref=v_ref, dst_ref=out_ref.at[slot], send_sem=send_sems.at[k - 1],
                recv_sem=recv_sems.at[k - 1], device_id=to, device_id_type=MESH)

        peers = []
        for k in range(1, 8):
            px = 1 - x if (k >> 2) & 1 else x
            py = 1 - y if (k >> 1) & 1 else y
            pc = 1 - c if k & 1 else c
            peers.append((px, py, pc))
            copy(k, me, (px, py, pc)).start()
        for k, (px, py, pc) in enumerate(peers, start=1):
            copy(k, 4 * px + 2 * py + pc, (x, y, c)).wait_recv()
        for k, peer in enumerate(peers, start=1):
            copy(k, me, peer).wait_send()

    return pl.pallas_call(
        body, name=name, in_specs=[VMEM], out_specs=VMEM,
        out_shape=jax.ShapeDtypeStruct((8, r, n), v.dtype),
        scratch_shapes=[pltpu.SemaphoreType.DMA((7,)), pltpu.SemaphoreType.DMA((7,))],
        compiler_params=_cp(has_side_effects=True),
    )(v)


HBM = pl.BlockSpec(memory_space=pltpu.HBM)
SEM = pl.BlockSpec(memory_space=pltpu.SEMAPHORE)
DATAFLOW = pltpu.SideEffectType.DATAFLOW_SIDE_EFFECTING


def _descriptors(plan, refs, send_sems, recv_sems, arrivals=True):
    x, y, c, _ = _place()
    sends, recvs = plan(refs)
    out = [pltpu.make_async_remote_copy(src_ref=src, dst_ref=dst, send_sem=send_sems.at[k],
                                        recv_sem=recv_sems.at[k], device_id=to, device_id_type=MESH)
           for k, (src, dst, to) in enumerate(sends)]
    if not arrivals:
        return out, []
    inn = [pltpu.make_async_remote_copy(src_ref=land, dst_ref=land, send_sem=send_sems.at[k],
                                        recv_sem=recv_sems.at[k], device_id=(x, y, c), device_id_type=MESH)
           for k, land in enumerate(recvs)]
    return out, inn


def copies_start(name, arrays, n_copies, plan, after):
    na = len(arrays)

    def body(*refs):
        out, _ = _descriptors(plan, refs[:na], refs[na + 1], refs[na + 2], arrivals=False)
        for cp in out:
            cp.start()
        refs[-1][...] = jnp.zeros((8, 128), F32)

    res = pl.pallas_call(
        body, name=name,
        out_shape=(pltpu.SemaphoreType.DMA((n_copies,)), pltpu.SemaphoreType.DMA((n_copies,)),
                   *[pltpu.HBM(a.shape, a.dtype) for a in arrays], jax.ShapeDtypeStruct((8, 128), F32)),
        in_specs=[HBM] * na + [ANY], out_specs=(SEM, SEM, *[HBM] * na, VMEM),
        input_output_aliases={i: i + 2 for i in range(na)},
        compiler_params=pltpu.CompilerParams(has_side_effects=DATAFLOW),
    )(*[pltpu.with_memory_space_constraint(a, pltpu.HBM) for a in arrays], after)
    return res[0], res[1], list(res[2:2 + na]), res[-1]


def copies_wait(name, started, plan, after):
    send_sems, recv_sems, arrays, _ = started
    na = len(arrays)
    after = list(after) if isinstance(after, (list, tuple)) else [after]

    def body(*refs):
        out, inn = _descriptors(plan, refs[:na], refs[na], refs[na + 1])
        for cp in out:
            cp.wait_send()
        for cp in inn:
            cp.wait_recv()
        refs[-1][...] = jnp.zeros((8, 128), F32)

    res = pl.pallas_call(
        body, name=name,
        out_shape=(*[pltpu.HBM(a.shape, a.dtype) for a in arrays], jax.ShapeDtypeStruct((8, 128), F32)),
        in_specs=[HBM] * na + [SEM, SEM] + [ANY] * len(after), out_specs=(*[HBM] * na, VMEM),
        input_output_aliases={i: i for i in range(na)},
        compiler_params=pltpu.CompilerParams(has_side_effects=DATAFLOW),
    )(*arrays, send_sems, recv_sems, *after)
    return list(res[:na]), res[-1]


def _rows_half(r, c):
    return pl.ds(c * (r // 2), r // 2), pl.ds((1 - c) * (r // 2), r // 2)


def plan_gather_neighbours(refs):
    x, y, c, _ = _place()
    p = 2 * x + y
    near, far, _ = _relay_chips()
    sends, recvs = [], []
    for buf in refs:
        mine, _ = _rows_half(buf.shape[1], c)
        for chip in (near, far):
            sends.append((buf.at[p, mine], buf.at[p, mine], (*chip, c)))
            recvs.append(buf.at[2 * chip[0] + chip[1], mine])
    return sends, recvs


def plan_gather_relay(refs):
    x, y, c, _ = _place()
    near, far, diag = _relay_chips()
    slot = lambda chip: 2 * chip[0] + chip[1]
    sends, recvs = [], []
    for buf in refs:
        mine, theirs = _rows_half(buf.shape[1], c)
        landed = buf.at[slot(near), mine]
        sends.append((landed, landed, (*far, c)))
        recvs.append(buf.at[slot(diag), mine])
        for sent, got in ((near, far), (far, near)):
            sends.append((buf.at[slot(sent), mine], buf.at[slot(sent), mine], (x, y, 1 - c)))
            recvs.append(buf.at[slot(got), theirs])
    return sends, recvs


def plan_gather_d2d(refs):
    x, y, c, _ = _place()
    _, _, diag = _relay_chips()
    sends, recvs = [], []
    for buf in refs:
        mine, theirs = _rows_half(buf.shape[1], c)
        landed = buf.at[2 * diag[0] + diag[1], mine]
        sends.append((landed, landed, (x, y, 1 - c)))
        recvs.append(buf.at[2 * diag[0] + diag[1], theirs])
    return sends, recvs


def plan_exchange(refs):
    x, y, c, _ = _place()
    n = len(refs) // 2
    sends, recvs = [], []
    for grad, land in zip(refs[:n], refs[n:]):
        _, theirs = _rows_half(grad.shape[1], c)
        sends.append((grad.at[:, theirs], land, (x, y, 1 - c)))
        recvs.append(land)
    return sends, recvs


def plan_scatter(refs):
    x, y, c, chips = _place()
    n = len(refs) // 2
    sends, recvs = [], []
    for part, land in zip(refs[:n], refs[n:]):
        for j, chip in enumerate(chips):
            sends.append((part.at[2 * chip[0] + chip[1]], land.at[j], (*chip, c)))
            recvs.append(land.at[j])
    return sends, recvs


def plan_share(refs):
    x, y, c, _ = _place()
    sends, recvs = [], []
    for buf in refs:
        mine, theirs = _rows_half(buf.shape[0], c)
        sends.append((buf.at[mine], buf.at[mine], (x, y, 1 - c)))
        recvs.append(buf.at[theirs])
    return sends, recvs


def put_in_slot(w, chip, dtype, name):
    r, c = w.shape
    tr = 256 if r % 256 == 0 else r

    def body(chip_ref, w_ref, o_ref):
        o_ref[...] = w_ref[...].astype(dtype)

    grid_spec = pltpu.PrefetchScalarGridSpec(
        num_scalar_prefetch=1, grid=(r // tr,),
        in_specs=[pl.BlockSpec((tr, c), lambda i, chip_ref: (i, 0))],
        out_specs=pl.BlockSpec((None, tr, c), lambda i, chip_ref: (chip_ref[0], i, 0)))
    return pl.pallas_call(body, name=name, grid_spec=grid_spec,
                          out_shape=jax.ShapeDtypeStruct((4, r, c), dtype), compiler_params=_cp())(chip, w)


def ada_fwd(s_in, ada_w, ada_b, tn):
    nl, d, ws = ada_w.shape

    def body(s_ref, w_ref, b_ref, so_ref, mod_ref):
        s = _silu(s_ref[...])
        so_ref[...] = s
        mod_ref[...] = _dot(s.astype(BF16), w_ref[...].astype(BF16)) + b_ref[...]

    return pl.pallas_call(
        body, name="ada_fwd", grid=(nl, ws // tn),
        in_specs=[pl.BlockSpec((16, d), lambda l, j: (0, 0)),
                  pl.BlockSpec((None, d, tn), lambda l, j: (l, 0, j)),
                  pl.BlockSpec((None, 1, tn), lambda l, j: (l, 0, j))],
        out_specs=[pl.BlockSpec((16, d), lambda l, j: (0, 0)),
                   pl.BlockSpec((None, 16, tn), lambda l, j: (l, 0, j))],
        out_shape=[jax.ShapeDtypeStruct((16, d), F32), jax.ShapeDtypeStruct((nl, 16, ws), F32)],
        compiler_params=_cp(),
    )(s_in, ada_w, ada_b)


def _adamw_math(w, g, m, v):
    m = ADAM_B1 * m + (1.0 - ADAM_B1) * g
    v = ADAM_B2 * v + (1.0 - ADAM_B2) * (g * g)
    m_hat = m / (1.0 - ADAM_B1 ** ADAM_STEP)
    v_hat = v / (1.0 - ADAM_B2 ** ADAM_STEP)
    delta = -ADAM_LR * (m_hat / (jnp.sqrt(v_hat) + ADAM_EPS) + ADAM_WD * w)
    return delta, m, v


def ada_bwd_adamw(s, dm, w, m, v):
    nl, d, ws = w.shape
    tr = 256 if d % 256 == 0 else 128

    def body(s_ref, dm_ref, w_ref, m_ref, v_ref, g_ref, dl_ref, mo_ref, vo_ref, dc_ref):
        dmv = dm_ref[...].astype(BF16)
        wv = w_ref[...]
        g = _dot(s_ref[...].astype(BF16), dmv, TN)
        g_ref[...] = g
        dl_ref[...], mo_ref[...], vo_ref[...] = _adamw_math(wv, g, m_ref[...], v_ref[...])
        dc_ref[...] = _dot(dmv[8:16, :], wv.astype(BF16), NT)

    wblk = pl.BlockSpec((None, tr, ws), lambda l, i: (l, i, 0))
    wshape = jax.ShapeDtypeStruct((nl, d, ws), F32)
    return pl.pallas_call(
        body, name="ada_bwd_adamw", grid=(nl, d // tr),
        in_specs=[pl.BlockSpec((16, tr), lambda l, i: (0, i)),
                  pl.BlockSpec((None, 16, ws), lambda l, i: (l, 0, 0)), wblk, wblk, wblk],
        out_specs=[wblk, wblk, wblk, wblk, pl.BlockSpec((None, 8, tr), lambda l, i: (l, 0, i))],
        out_shape=[wshape, wshape, wshape, wshape, jax.ShapeDtypeStruct((nl, 8, d), F32)],
        compiler_params=_cp(),
    )(s, dm, w, m, v)


def adamw(w, g, m, v, name, with_grad=False):
    r, c = w.shape
    tr = 256 if r % 256 == 0 else r

    def body(w_ref, g_ref, m_ref, v_ref, dl_ref, mo_ref, vo_ref, *g_out):
        gv = g_ref[...]
        dl_ref[...], mo_ref[...], vo_ref[...] = _adamw_math(w_ref[...], gv, m_ref[...], v_ref[...])
        if with_grad:
            g_out[0][...] = gv

    blk = pl.BlockSpec((tr, c), lambda i: (i, 0))
    shape = jax.ShapeDtypeStruct((r, c), F32)
    n_out = 4 if with_grad else 3
    return pl.pallas_call(body, name=name, grid=(r // tr,), in_specs=[blk] * 4, out_specs=[blk] * n_out,
                          out_shape=[shape] * n_out, compiler_params=_cp())(w, g, m, v)


ROW_MOD = 10


def small_reduce(gathered):
    _, rows, d = gathered.shape

    def body(g_ref, o_ref):
        tot = g_ref[0]
        for b in range(1, 8):
            tot = tot + g_ref[b]
        o_ref[0:rows, :] = tot
        for layer in range(2):
            lat = ROW_MOD + 6 * layer
            o_ref[24 + 3 * layer:27 + 3 * layer, :] = tot[lat:lat + 3, :] + tot[lat + 3:lat + 6, :]
        o_ref[30:32, :] = jnp.zeros((2, d), F32)

    return pl.pallas_call(body, name="small_reduce", in_specs=[VMEM], out_specs=VMEM,
                          out_shape=jax.ShapeDtypeStruct((32, d), F32), compiler_params=_cp())(gathered)


def lb_logits_grad(lbl, dlb):
    _, _, n = lbl.shape

    def body(l_ref, d_ref, o_ref):
        for dr in range(2):
            _, (p0, p1, p2) = _lower_bound(l_ref, dr)
            dv = d_ref[dr:dr + 1, :]
            o_ref[dr, 0:1, :] = p0 * p2 * dv
            o_ref[dr, 1:2, :] = p1 * p2 * dv
            o_ref[dr, 2:3, :] = -p2 * (p0 + p1) * dv

    return pl.pallas_call(body, name="lb_logits_grad", in_specs=[VMEM, VMEM], out_specs=VMEM,
                          out_shape=jax.ShapeDtypeStruct((2, 3, n), F32), compiler_params=_cp())(lbl, dlb)


def c_ctx_grad(parts, c_ctx):
    d = c_ctx.shape[1]

    def body(p_ref, c_ref, o_ref):
        tot = p_ref[0, 0:1, :]
        for chip in range(1, 4):
            tot = tot + p_ref[2 * chip, 0:1, :]
        o_ref[...] = tot * _dsilu(c_ref[...])

    return pl.pallas_call(body, name="c_ctx_grad", in_specs=[VMEM, VMEM], out_specs=VMEM,
                          out_shape=jax.ShapeDtypeStruct((1, d), F32), compiler_params=_cp())(parts, c_ctx)


def kernel(x, c, ctx, c_ctx, ada_w, ada_b, pre_g, post_g, ev_w_in, ev_pool_w, ev_pool_scale, ev_conv_w, ev_conv_b, ev_w_out, od_w_in, od_onorm_g, od_w_out, lb_logits, loss_target, m_c_ctx, m_ada_w, m_ada_b, m_pre_g, m_post_g, m_ev_w_in, m_ev_pool_w, m_ev_pool_scale, m_ev_conv_w, m_ev_conv_b, m_ev_w_out, m_od_w_in, m_od_onorm_g, m_od_w_out, m_lb_logits, v_c_ctx, v_ada_w, v_ada_b, v_pre_g, v_post_g, v_ev_w_in, v_ev_pool_w, v_ev_pool_scale, v_ev_conv_w, v_ev_conv_b, v_ev_w_out, v_od_w_in, v_od_onorm_g, v_od_w_out, v_lb_logits):
    _, seq, d = x.shape
    cx = ctx.shape[1]
    t = cx + seq
    half_d = d // 2
    g = half_d // N_POOL
    tn = d // 4
    xi, yi, ci = lax.axis_index("x"), lax.axis_index("y"), lax.axis_index("c")
    chip = 2 * xi + yi
    me = 2 * chip + ci
    chip_arr = jnp.reshape(chip, (1,)).astype(jnp.int32)
    chip_core_arr = jnp.stack([chip, ci]).astype(jnp.int32)

    pad = lambda a, rows: jnp.concatenate([a, jnp.zeros((rows - a.shape[0], g), F32)], axis=0)
    small = jnp.concatenate([
        ev_pool_w.reshape(g, g), pad(ev_conv_w.reshape(3, g), 8), pad(od_onorm_g.reshape(2, g), 8),
        pad(lb_logits.reshape(12, g), 16)], axis=0)
    c_rows = jnp.concatenate([c, jnp.zeros((7, d), F32)], axis=0)
    c_all = allgather8(c_rows, "allgather_c")[:, 0, :]
    s_in = jnp.concatenate([c_all, c_ctx.reshape(1, d), jnp.zeros((7, d), F32)], axis=0)
    ws_ada = ada_w.shape[2]
    ada_b_mine = lax.dynamic_slice(ada_b, (0, chip * ws_ada), (2, ws_ada)).reshape(2, 1, ws_ada)
    s_act, mod_mine = ada_fwd(s_in, ada_w, ada_b_mine, tn)
    mod_all = allgather8(mod_mine.reshape(32, ws_ada), "allgather_mod")

    ev_in_g, ev_out_g, small_g, ev_done = allgather_shards([
        put_in_slot(ev_w_in[0], chip_arr, BF16, "cast_ev_w_in"),
        put_in_slot(ev_w_out[0], chip_arr, BF16, "cast_ev_w_out"),
        put_in_slot(small, chip_arr, F32, "place_small")], mod_all)
    od_ici = copies_start("gather_od_ici_start", [
        put_in_slot(od_w_in[0], chip_arr, BF16, "cast_od_w_in"),
        put_in_slot(od_w_out[0], chip_arr, BF16, "cast_od_w_out")], 4, plan_gather_neighbours, ev_done)
    ev_out3 = ev_out_g.reshape(1, d, d)
    pool_w_full = small_g[:, :g].reshape(4, N_POOL, g // 4, g).transpose(1, 0, 2, 3).reshape(N_POOL, g, g)
    conv_w_full = small_g[:, g:g + 3].transpose(1, 0, 2).reshape(3, half_d)
    onorm_full = small_g[:, g + 8:g + 10].reshape(1, d)
    lbl_full = small_g[:, g + 16:g + 28].reshape(4, 2, 3, 2 * g).transpose(1, 2, 0, 3).reshape(2, 3, d)

    mod_full = mod_all[0::2].reshape(4, 2, 16, ws_ada).transpose(1, 2, 0, 3).reshape(2, 16, 3 * d)
    mod_lat = lax.dynamic_slice(mod_full, (0, me, 0), (2, 1, 3 * d))
    mods = jnp.concatenate([mod_full[:, 8:9], mod_lat], axis=1)
    shift, scale, gate = mods[:, :, :d], mods[:, :, d:2 * d], mods[:, :, 2 * d:]

    h0, xs = normmod_fwd_joining(ctx[0], x[0], pre_g[0:1] + od_ici[3][0:1, 0:1], shift[0], scale[0])
    z0 = mm_nn(h0, ev_in_g, half_d, tn, "mm_ev_in")
    u = mix_b_fwd(z0, conv_w_full, ev_conv_b, mix_a_fwd(z0, pool_w_full, ev_pool_scale, cx), cx)
    od_relay = copies_start("gather_od_relay_start",
                            copies_wait("gather_od_ici_wait", od_ici, plan_gather_neighbours, u)[0],
                            6, plan_gather_relay, u)
    y0 = mm_nn(u, ev_out3, d, tn, "mm_ev_out")[0]
    xs1, h1 = post_fwd_norm(xs, y0, post_g[0:1] + od_relay[3][0:1, 0:1], gate[0],
                            pre_g[1:2], shift[1], scale[1], cx)
    od_d2d = copies_start("gather_od_d2d_start",
                          copies_wait("gather_od_relay_wait", od_relay, plan_gather_relay, xs1)[0],
                          2, plan_gather_d2d, xs1)
    (od_in_g, od_out_g), _ = copies_wait("gather_od_d2d_wait", od_d2d, plan_gather_d2d, od_d2d[3])
    od_out3 = od_out_g.reshape(1, d, d)

    z1 = mm_nn(h1, od_in_g, d, tn, "mm_od_in")
    o1, r1, bcs1, ks1, decs1 = hgrn_fwd(z1, lbl_full, onorm_full, cx)
    y1 = mm_nn(r1, od_out3, d, tn, "mm_od_out")[0]
    sq, dx2, dy1, dgate1, dpost1 = post_loss(xs1, y1, post_g[1:2], gate[1], loss_target[0], cx)

    dr1 = mm_nt(dy1[None], None, od_out3, tn, "mm_od_out_dx")
    g_od_out = mm_tn(r1, dy1[None], None, d, tn, "mm_od_out_dw")
    dz1, donorm, dlb = hgrn_bwd(z1, lbl_full, onorm_full, o1, dr1, bcs1, ks1, decs1, cx)
    dh1 = mm_nt(dz1, None, od_in_g, tn, "mm_od_in_dx")
    g_od_in = mm_tn(h1, dz1, None, od_in_g.shape[2], tn, "mm_od_in_dw")
    dxs1, dpre1, dshift1, dscale1 = normmod_bwd(xs1, dh1, pre_g[1:2], scale[1], dx2, cx, True)

    od_grads = [g_od_in, g_od_out.reshape(4, d // 4, d)]
    half_zone = lambda a, lead, dt: lax.empty((lead, a.shape[1] // 2, a.shape[2]), dt)
    od_ex = copies_start("reduce_od_exchange_start", od_grads + [half_zone(a, 4, a.dtype) for a in od_grads],
                         2, plan_exchange, dxs1)

    dy0, dgate0, dpost0 = post_bwd(dxs1, y0, post_g[0:1] + od_ex[3][0:1, 0:1], gate[0], cx)
    du = mm_nt(dy0[None], None, ev_out3, tn, "mm_ev_out_dx")
    g_ev_out = mm_tn(u, dy0[None], None, d, tn, "mm_ev_out_dw")
    od_got, _ = copies_wait("reduce_od_exchange_wait", od_ex, plan_exchange, g_ev_out)
    od_sums = [pair_sum(od_got[i], od_got[2 + i], chip_core_arr) for i in range(2)]
    od_sc = copies_start("reduce_od_scatter_start",
                         [sb for _, sb in od_sums] + [half_zone(a, 3, BF16) for a in od_grads],
                         6, plan_scatter, du)
    dz0a, g_pool_w, dpool_scale = mix_a_bwd(z0, du, pool_w_full, ev_pool_scale + od_sc[3][0:1, 0:1], cx)
    dz0b, dconv_w, dconv_b = mix_b_bwd(z0, du, conv_w_full, ev_conv_b + od_sc[3][0:1, 0:1], cx)
    g_ev_in = mm_tn(h0, dz0a, dz0b, ev_in_g.shape[2], tn, "mm_ev_in_dw")
    ev_grads = [g_ev_in, g_ev_out.reshape(4, d // 4, d), g_pool_w.reshape(4, g, g)]
    ev_ex = copies_start("reduce_ev_exchange_start", ev_grads + [half_zone(a, 4, a.dtype) for a in ev_grads],
                         3, plan_exchange, dpool_scale)
    dh0 = mm_nt(dz0a, dz0b, ev_in_g, tn, "mm_ev_in_dx")
    dxs0, dpre0, dshift0, dscale0 = normmod_bwd(xs, dh0, pre_g[0:1] + ev_ex[3][0:1, 0:1], scale[0], dxs1,
                                                cx, False, True)
    grad_x = dxs0[None]
    ev_got, _ = copies_wait("reduce_ev_exchange_wait", ev_ex, plan_exchange, dxs0)
    ev_sums = [pair_sum(ev_got[i], ev_got[3 + i], chip_core_arr) for i in range(3)]
    od_recv, _ = copies_wait("reduce_od_scatter_wait", od_sc, plan_scatter, dxs0)

    zrow = jnp.zeros((1, d), F32)
    small_rows = jnp.concatenate([
        dpre0, dpre1, dpost0, dpost1,
        jnp.concatenate([dpool_scale, dconv_b], axis=1),
        jnp.concatenate([dconv_w.reshape(1, 3 * half_d), jnp.zeros((1, half_d), F32)], axis=1).reshape(2, d),
        donorm, dlb,
        dshift0[1:2], dscale0[1:2], dgate0[1:2], dshift0[0:1], dscale0[0:1], dgate0[0:1],
        dshift1[1:2], dscale1[1:2], dgate1[1:2], dshift1[0:1], dscale1[0:1], zrow,
        jnp.concatenate([sq[0:1], jnp.zeros((1, d - 128), F32)], axis=1),
        zrow], axis=0)
    small_all = allgather8(small_rows, "allgather_small")
    ev_sc = copies_start("reduce_ev_scatter_start",
                         [sb for _, sb in ev_sums] + [half_zone(a, 3, BF16) for a in ev_grads],
                         9, plan_scatter, small_all)
    od_sh = copies_start("reduce_od_share_start",
                         [owner_sum(od_sums[i][0], od_recv[2 + i], chip_core_arr) for i in range(2)],
                         2, plan_share, dxs0)
    tot = small_reduce(small_all + ev_sc[3][0:1, 0:1])
    loss = tot[22, 0] * (0.5 / d)

    dm_rows = []
    for layer in range(2):
        lat = ROW_MOD + 6 * layer
        dm_lat = small_all[:, lat:lat + 3].reshape(8, 3 * d)
        dm_ctx = tot[lat + 3:lat + 6].reshape(1, 3 * d)
        dm_rows.append(jnp.concatenate([dm_lat, dm_ctx, jnp.zeros((7, 3 * d), F32)], axis=0))
    dm_full = jnp.stack(dm_rows)
    dm_mine = lax.dynamic_slice(dm_full, (0, 0, chip * ws_ada), (2, 16, ws_ada))

    def step(w, gr, m, v, name, with_grad=False):
        shape = w.shape
        cols = shape[-1]
        two_d = lambda a: a.reshape(-1, cols)
        res = adamw(two_d(w), two_d(gr), two_d(m), two_d(v), "adamw_" + name, with_grad)
        return tuple(a.reshape(shape) for a in res)

    grad_ada_b = tot[24:30].reshape(2, 3 * d)
    grad_pre_g = tot[0:2]
    grad_post_g = tot[2:4]
    grad_ev_pool_scale = tot[4:5, :half_d]
    grad_ev_conv_b = tot[4:5, half_d:]
    conv_w_tot = tot[5:7].reshape(1, 2 * d)[:, :3 * half_d].reshape(3, N_POOL, g)
    grad_ev_conv_w = lax.dynamic_slice(conv_w_tot, (0, chip, 0), (3, 1, g)).reshape(1, 3, g)
    grad_od_onorm_g = lax.dynamic_slice(tot[7:8], (0, chip * 2 * g), (1, 2 * g))
    dlb_mine = lax.dynamic_slice(tot[8:10], (0, chip * 2 * g), (2, 2 * g))
    grad_lb_logits = lb_logits_grad(lb_logits, dlb_mine)
    upd = {
        "ada_b": step(ada_b, grad_ada_b, m_ada_b, v_ada_b, "ada_b"),
        "pre_g": step(pre_g, grad_pre_g, m_pre_g, v_pre_g, "pre_g"),
        "post_g": step(post_g, grad_post_g, m_post_g, v_post_g, "post_g"),
        "ev_pool_scale": step(ev_pool_scale, grad_ev_pool_scale, m_ev_pool_scale, v_ev_pool_scale, "ev_pool_scale"),
        "ev_conv_w": step(ev_conv_w, grad_ev_conv_w, m_ev_conv_w, v_ev_conv_w, "ev_conv_w"),
        "ev_conv_b": step(ev_conv_b, grad_ev_conv_b, m_ev_conv_b, v_ev_conv_b, "ev_conv_b"),
        "od_onorm_g": step(od_onorm_g, grad_od_onorm_g, m_od_onorm_g, v_od_onorm_g, "od_onorm_g"),
        "lb_logits": step(lb_logits, grad_lb_logits, m_lb_logits, v_lb_logits, "lb_logits"),
    }
    grad_ada_w, delta_ada_w, new_m_ada_w, new_v_ada_w, dctx_part = ada_bwd_adamw(
        s_act, dm_mine, ada_w, m_ada_w, v_ada_w)
    upd["ada_w"] = (delta_ada_w, new_m_ada_w, new_v_ada_w)
    (grad_od_w_in, grad_od_w_out), _ = copies_wait("reduce_od_share_wait", od_sh, plan_share, ev_sc[3])
    upd["od_w_in"] = step(od_w_in, grad_od_w_in[None], m_od_w_in, v_od_w_in, "od_w_in", True)
    upd["od_w_out"] = step(od_w_out, grad_od_w_out[None], m_od_w_out, v_od_w_out, "od_w_out", True)
    grad_od_w_in, grad_od_w_out = upd["od_w_in"][3], upd["od_w_out"][3]
    done_behind = [dctx_part] + [upd[k][0] for k in (
        "od_w_in", "od_w_out", "ada_b", "pre_g", "post_g", "ev_pool_scale", "ev_conv_w", "ev_conv_b",
        "od_onorm_g", "lb_logits")]
    ev_recv, ev_landed = copies_wait("reduce_ev_scatter_wait", ev_sc, plan_scatter, done_behind)
    grad_ev_w_in, grad_ev_w_out, grad_pool_w = share_halves(
        [owner_sum(ev_sums[i][0], ev_recv[3 + i], chip_core_arr) for i in range(3)])
    dctx_all = allgather8(dctx_part[0] + dctx_part[1] + ev_landed[0:1, 0:1], "allgather_dctx")
    grad_c_ctx = c_ctx_grad(dctx_all, c_ctx.reshape(1, d)).reshape(d)
    upd["c_ctx"] = step(c_ctx, grad_c_ctx, m_c_ctx, v_c_ctx, "c_ctx")
    upd["ev_w_in"] = step(ev_w_in, grad_ev_w_in[None], m_ev_w_in, v_ev_w_in, "ev_w_in", True)
    upd["ev_pool_w"] = step(ev_pool_w, grad_pool_w.reshape(1, N_POOL, g // 4, g), m_ev_pool_w, v_ev_pool_w,
                            "ev_pool_w", True)
    upd["ev_w_out"] = step(ev_w_out, grad_ev_w_out[None], m_ev_w_out, v_ev_w_out, "ev_w_out", True)
    grad_ev_w_in, grad_ev_pool_w, grad_ev_w_out = upd["ev_w_in"][3], upd["ev_pool_w"][3], upd["ev_w_out"][3]
    names = ["c_ctx", "ada_w", "ada_b", "pre_g", "post_g", "ev_w_in", "ev_pool_w", "ev_pool_scale",
             "ev_conv_w", "ev_conv_b", "ev_w_out", "od_w_in", "od_onorm_g", "od_w_out", "lb_logits"]
    grads = [grad_c_ctx, grad_ada_w, grad_ada_b, grad_pre_g, grad_post_g, grad_ev_w_in, grad_ev_pool_w,
             grad_ev_pool_scale, grad_ev_conv_w, grad_ev_conv_b, grad_ev_w_out, grad_od_w_in,
             grad_od_onorm_g, grad_od_w_out, grad_lb_logits]
    return (loss, grad_x, *grads, *[upd[k][0] for k in names], *[upd[k][1] for k in names],
            *[upd[k][2] for k in names])
```

```python
import jax
import jax.numpy as jnp
from jax import lax
from jax.experimental import pallas as pl
from jax.experimental.pallas import tpu as pltpu

EPS = 1e-6
GRID_W_LOG2 = 6
CHUNK = 64
HEAD = 128
N_POOL = 4
ADAM_LR, ADAM_B1, ADAM_B2, ADAM_EPS, ADAM_WD, ADAM_STEP = 0.001, 0.9, 0.999, 1e-08, 0.01, 10
VMEM_LIMIT = 56 * 1024 * 1024
MESH = pl.DeviceIdType.MESH
F32, BF16 = jnp.float32, jnp.bfloat16
ANY = pl.BlockSpec(memory_space=pl.ANY)
VMEM = pl.BlockSpec(memory_space=pltpu.VMEM)


def _cp(**kw):
    return pltpu.CompilerParams(vmem_limit_bytes=VMEM_LIMIT, **kw)


def _silu(x):
    return x * jax.nn.sigmoid(x)


def _dsilu(x):
    s = jax.nn.sigmoid(x)
    return s * (1.0 + x * (1.0 - s))


def _dot(a, b, dims=((1,), (0,)), precision=None):
    return lax.dot_general(a, b, (dims, ((), ())), preferred_element_type=F32, precision=precision)


NN = ((1,), (0,))
NT = ((1,), (1,))
TN = ((0,), (0,))


def _row_block(cx):
    return 256 if cx % 256 == 0 else 128


def normmod_fwd_joining(ctx, x, g, shift, scale):
    cx, d = ctx.shape
    t = cx + x.shape[0]
    tm = _row_block(cx)
    nctx = cx // tm

    def body(c_ref, x_ref, g_ref, sh_ref, sc_ref, h_ref, xs_ref):
        is_ctx = pl.program_id(0) < nctx
        x = jnp.where(is_ctx, c_ref[...], x_ref[...])
        xs_ref[...] = x
        rstd = lax.rsqrt(jnp.mean(x * x, axis=-1, keepdims=True) + EPS)
        sc = jnp.where(is_ctx, sc_ref[0:1, :], sc_ref[1:2, :])
        sh = jnp.where(is_ctx, sh_ref[0:1, :], sh_ref[1:2, :])
        h_ref[...] = ((x * rstd) * g_ref[...] * (1.0 + sc) + sh).astype(BF16)

    row = pl.BlockSpec((tm, d), lambda i: (i, 0))
    vec = lambda r: pl.BlockSpec((r, d), lambda i: (0, 0))
    return pl.pallas_call(
        body, name="normmod_fwd_joining", grid=(t // tm,),
        in_specs=[pl.BlockSpec((tm, d), lambda i: (jnp.minimum(i, nctx - 1), 0)),
                  pl.BlockSpec((tm, d), lambda i: (jnp.maximum(i - nctx, 0), 0)), vec(1), vec(2), vec(2)],
        out_specs=[row, row],
        out_shape=[jax.ShapeDtypeStruct((t, d), BF16), jax.ShapeDtypeStruct((t, d), F32)],
        compiler_params=_cp(),
    )(ctx, x, g, shift, scale)


def normmod_bwd(xs, dh, g, scale, dres, cx, res_is_latent_only, dx_latent_only=False):
    t, d = xs.shape
    tm = _row_block(cx)
    nctx = cx // tm

    def body(x_ref, dh_ref, g_ref, sc_ref, dres_ref, dx_ref, dg_ref, dsh_ref, dsc_ref):
        i = pl.program_id(0)
        is_ctx = i < nctx

        @pl.when(i == 0)
        def _():
            dg_ref[...] = jnp.zeros_like(dg_ref)
            dsh_ref[...] = jnp.zeros_like(dsh_ref)
            dsc_ref[...] = jnp.zeros_like(dsc_ref)

        x = x_ref[...]
        dh = dh_ref[...]
        gv = g_ref[...]
        rstd = lax.rsqrt(jnp.mean(x * x, axis=-1, keepdims=True) + EPS)
        xhat = x * rstd
        sc = jnp.where(is_ctx, sc_ref[0:1, :], sc_ref[1:2, :])
        dsh = jnp.sum(dh, axis=0, keepdims=True)
        dhx = dh * xhat
        dsc = jnp.sum(dhx * gv, axis=0, keepdims=True)
        dg_ref[...] += jnp.sum(dhx * (1.0 + sc), axis=0, keepdims=True)
        zero = jnp.zeros_like(dsh)
        dsh_ref[0:1, :] += jnp.where(is_ctx, dsh, zero)
        dsh_ref[1:2, :] += jnp.where(is_ctx, zero, dsh)
        dsc_ref[0:1, :] += jnp.where(is_ctx, dsc, zero)
        dsc_ref[1:2, :] += jnp.where(is_ctx, zero, dsc)
        dxhat = dh * (gv * (1.0 + sc))
        dx = rstd * (dxhat - xhat * jnp.mean(dxhat * xhat, axis=-1, keepdims=True))
        res = dres_ref[...]
        if res_is_latent_only:
            res = jnp.where(is_ctx, jnp.zeros_like(res), res)
        dx_ref[...] = dx + res

    row = pl.BlockSpec((tm, d), lambda i: (i, 0))
    if res_is_latent_only:
        res_spec = pl.BlockSpec((tm, d), lambda i: (jnp.maximum(i - nctx, 0), 0))
    else:
        res_spec = row
    vec = lambda r: pl.BlockSpec((r, d), lambda i: (0, 0))
    dx_spec = pl.BlockSpec((tm, d), lambda i: (jnp.maximum(i - nctx, 0), 0)) if dx_latent_only else row
    return pl.pallas_call(
        body, name="normmod_bwd", grid=(t // tm,),
        in_specs=[row, row, vec(1), vec(2), res_spec],
        out_specs=[dx_spec, vec(1), vec(2), vec(2)],
        out_shape=[jax.ShapeDtypeStruct((t - cx if dx_latent_only else t, d), F32), jax.ShapeDtypeStruct((1, d), F32),
                   jax.ShapeDtypeStruct((2, d), F32), jax.ShapeDtypeStruct((2, d), F32)],
        compiler_params=_cp(),
    )(xs, dh, g, scale, dres)


def post_fwd_norm(xs, y, pg, gate, g_next, shift_next, scale_next, cx):
    t, d = xs.shape
    tm = _row_block(cx)
    nctx = cx // tm

    def body(x_ref, y_ref, pg_ref, gate_ref, g_ref, sh_ref, sc_ref, o_ref, h_ref):
        is_ctx = pl.program_id(0) < nctx
        pick = lambda ref: jnp.where(is_ctx, ref[0:1, :], ref[1:2, :])
        y = y_ref[...]
        rstd = lax.rsqrt(jnp.mean(y * y, axis=-1, keepdims=True) + EPS)
        x = x_ref[...] + pick(gate_ref) * ((y * rstd) * pg_ref[...])
        o_ref[...] = x
        rstd = lax.rsqrt(jnp.mean(x * x, axis=-1, keepdims=True) + EPS)
        h_ref[...] = ((x * rstd) * g_ref[...] * (1.0 + pick(sc_ref)) + pick(sh_ref)).astype(BF16)

    row = pl.BlockSpec((tm, d), lambda i: (i, 0))
    vec = lambda r: pl.BlockSpec((r, d), lambda i: (0, 0))
    return pl.pallas_call(
        body, name="post_fwd_norm", grid=(t // tm,),
        in_specs=[row, row, vec(1), vec(2), vec(1), vec(2), vec(2)], out_specs=[row, row],
        out_shape=[jax.ShapeDtypeStruct((t, d), F32), jax.ShapeDtypeStruct((t, d), BF16)],
        compiler_params=_cp(),
    )(xs, y, pg, gate, g_next, shift_next, scale_next)


def post_loss(xs, y, pg, gate, target, cx):
    t, d = xs.shape
    n = y.shape[0]
    tm = _row_block(cx)
    nctx = cx // tm

    def body(x_ref, y_ref, pg_ref, gate_ref, tgt_ref, sq_ref, dx_ref, dy_ref, dgate_ref, dpg_ref):
        @pl.when(pl.program_id(0) == 0)
        def _():
            sq_ref[...] = jnp.zeros_like(sq_ref)
            dgate_ref[...] = jnp.zeros_like(dgate_ref)
            dpg_ref[...] = jnp.zeros_like(dpg_ref)

        y = y_ref[...]
        pgv = pg_ref[...]
        gt = gate_ref[1:2, :]
        rstd = lax.rsqrt(jnp.mean(y * y, axis=-1, keepdims=True) + EPS)
        yhat = y * rstd
        err = x_ref[...] + gt * (yhat * pgv) - tgt_ref[...]
        sq_ref[...] += jnp.sum(err * err)
        dx = err * (1.0 / d)
        dx_ref[...] = dx
        dxy = dx * yhat
        dgate_ref[1:2, :] += jnp.sum(dxy * pgv, axis=0, keepdims=True)
        dpg_ref[...] += jnp.sum(dxy * gt, axis=0, keepdims=True)
        dyhat = dx * (gt * pgv)
        dy_ref[...] = (rstd * (dyhat - yhat * jnp.mean(dyhat * yhat, axis=-1, keepdims=True))).astype(BF16)

    row = pl.BlockSpec((tm, d), lambda i: (i, 0))
    xrow = pl.BlockSpec((tm, d), lambda i: (i + nctx, 0))
    vec = lambda r: pl.BlockSpec((r, d), lambda i: (0, 0))
    return pl.pallas_call(
        body, name="post_loss", grid=(n // tm,),
        in_specs=[xrow, row, vec(1), vec(2), row],
        out_specs=[pl.BlockSpec((8, 128), lambda i: (0, 0)), row, row, vec(2), vec(1)],
        out_shape=[jax.ShapeDtypeStruct((8, 128), F32), jax.ShapeDtypeStruct((n, d), F32),
                   jax.ShapeDtypeStruct((n, d), BF16), jax.ShapeDtypeStruct((2, d), F32),
                   jax.ShapeDtypeStruct((1, d), F32)],
        compiler_params=_cp(),
    )(xs, y, pg, gate, target)


def post_bwd(dxo, y, pg, gate, cx):
    m, d = y.shape
    tm = _row_block(cx)
    nctx = cx // tm

    def body(dx_ref, y_ref, pg_ref, gate_ref, dy_ref, dgate_ref, dpg_ref):
        i = pl.program_id(0)
        is_ctx = i < nctx

        @pl.when(i == 0)
        def _():
            dgate_ref[...] = jnp.zeros_like(dgate_ref)
            dpg_ref[...] = jnp.zeros_like(dpg_ref)

        y = y_ref[...]
        dx = dx_ref[...]
        pgv = pg_ref[...]
        rstd = lax.rsqrt(jnp.mean(y * y, axis=-1, keepdims=True) + EPS)
        yhat = y * rstd
        gt = jnp.where(is_ctx, gate_ref[0:1, :], gate_ref[1:2, :])
        dxy = dx * yhat
        dgt = jnp.sum(dxy * pgv, axis=0, keepdims=True)
        zero = jnp.zeros_like(dgt)
        dgate_ref[0:1, :] += jnp.where(is_ctx, dgt, zero)
        dgate_ref[1:2, :] += jnp.where(is_ctx, zero, dgt)
        dpg_ref[...] += jnp.sum(dxy * gt, axis=0, keepdims=True)
        dyhat = dx * (gt * pgv)
        dy = rstd * (dyhat - yhat * jnp.mean(dyhat * yhat, axis=-1, keepdims=True))
        dy_ref[...] = dy.astype(BF16)

    row = pl.BlockSpec((tm, d), lambda i: (i, 0))
    vec = lambda r: pl.BlockSpec((r, d), lambda i: (0, 0))
    return pl.pallas_call(
        body, name="post_bwd", grid=(m // tm,),
        in_specs=[row, row, vec(1), vec(2)], out_specs=[row, vec(2), vec(1)],
        out_shape=[jax.ShapeDtypeStruct((m, d), BF16), jax.ShapeDtypeStruct((2, d), F32),
                   jax.ShapeDtypeStruct((1, d), F32)],
        compiler_params=_cp(),
    )(dxo, y, pg, gate)


def _split_rows(m):
    for cand in (1152, 1024, 768, 512, 384, 256, 128):
        if m % cand == 0 and m // cand >= 2:
            return cand
    return m


def mm_nn(a, w3, sec, tn, name):
    m, k = a.shape
    q, _, ws = w3.shape
    n = q * ws
    tpq, tps = ws // tn, sec // tn
    tm = next(c for c in (768, 512, 256, 128) if m % c == 0)

    def body(a_ref, w_ref, o_ref):
        w = w_ref[...]

        def step(i, carry):
            rows = pl.ds(pl.multiple_of(i * tm, tm), tm)
            o_ref[rows, :] = _dot(a_ref[rows, :], w)
            return carry

        lax.fori_loop(0, m // tm, step, 0)

    return pl.pallas_call(
        body, name=name, grid=(n // tn,),
        in_specs=[pl.BlockSpec((m, k), lambda j: (0, 0)),
                  pl.BlockSpec((None, k, tn), lambda j: (j // tpq, 0, j % tpq))],
        out_specs=pl.BlockSpec((None, m, tn), lambda j: (j // tps, 0, j % tps)),
        out_shape=jax.ShapeDtypeStruct((n // sec, m, sec), F32), compiler_params=_cp(),
    )(a, w3)


def _two_stacks(a3, b3, tn):
    sec = a3.shape[2]
    tps = sec // tn
    n1 = a3.shape[0] * tps
    first = lambda j: (jnp.minimum(j, n1 - 1) // tps, jnp.minimum(j, n1 - 1) % tps)
    second = lambda j: (jnp.maximum(j - n1, 0) // tps, jnp.maximum(j - n1, 0) % tps)
    return n1, first, second


def mm_nt(a3, b3, w3, tn, name):
    if b3 is None:
        b3 = a3
    _, m, sec = a3.shape
    q, k, ws = w3.shape
    n = q * ws
    tpq = ws // tn
    mb = _split_rows(m)
    n1, first, second = _two_stacks(a3, b3, tn)

    def body(a_ref, b_ref, w_ref, o_ref):
        j = pl.program_id(1)

        @pl.when(j == 0)
        def _():
            o_ref[...] = jnp.zeros_like(o_ref)

        @pl.when(j < n1)
        def _():
            o_ref[...] += _dot(a_ref[...], w_ref[...], NT)

        @pl.when(j >= n1)
        def _():
            o_ref[...] += _dot(b_ref[...], w_ref[...], NT)

    return pl.pallas_call(
        body, name=name, grid=(m // mb, n // tn),
        in_specs=[pl.BlockSpec((None, mb, tn), lambda i, j: (first(j)[0], i, first(j)[1])),
                  pl.BlockSpec((None, mb, tn), lambda i, j: (second(j)[0], i, second(j)[1])),
                  pl.BlockSpec((None, k, tn), lambda i, j: (j // tpq, 0, j % tpq))],
        out_specs=pl.BlockSpec((mb, k), lambda i, j: (i, 0)),
        out_shape=jax.ShapeDtypeStruct((m, k), F32), compiler_params=_cp(),
    )(a3, b3, w3)


def mm_tn(a, b3, c3, ws, tn, name):
    m, k = a.shape
    sec = b3.shape[2]
    n = (b3.shape[0] + (0 if c3 is None else c3.shape[0])) * sec
    if c3 is None:
        c3 = b3
    tpq = ws // tn
    kb = 256 if k % 256 == 0 else 128
    n1, first, second = _two_stacks(b3, c3, tn)

    def body(a_ref, b_ref, c_ref, o_ref):
        def product(rhs_ref):
            rhs = rhs_ref[...]
            for i in range(k // kb):
                o_ref[i * kb:(i + 1) * kb, :] = _dot(a_ref[:, i * kb:(i + 1) * kb], rhs, TN).astype(BF16)

        @pl.when(pl.program_id(0) < n1)
        def _():
            product(b_ref)

        @pl.when(pl.program_id(0) >= n1)
        def _():
            product(c_ref)

    return pl.pallas_call(
        body, name=name, grid=(n // tn,),
        in_specs=[pl.BlockSpec((m, k), lambda j: (0, 0)),
                  pl.BlockSpec((None, m, tn), lambda j: (first(j)[0], 0, first(j)[1])),
                  pl.BlockSpec((None, m, tn), lambda j: (second(j)[0], 0, second(j)[1]))],
        out_specs=pl.BlockSpec((None, k, tn), lambda j: (j // tpq, 0, j % tpq)),
        out_shape=jax.ShapeDtypeStruct((n // ws, k, ws), BF16), compiler_params=_cp(),
    )(a, b3, c3)


POOL_REACH = 8 << GRID_W_LOG2


def _token_parts(tok, cx):
    lat = tok - cx
    return tok < cx, lat >> GRID_W_LOG2, lat & ((1 << GRID_W_LOG2) - 1)


def _pool_mask(gi, row0, col0, tm, ncols, cx, transposed):
    half = jnp.left_shift(1, gi)
    r = lax.broadcasted_iota(jnp.int32, (tm, 1), 0) + row0
    c = lax.broadcasted_iota(jnp.int32, (1, ncols), 1) + col0
    out_tok, src_tok = (c, r) if transposed else (r, c)
    o_ctx, o_row, o_col = _token_parts(out_tok, cx)
    s_ctx, s_row, s_col = _token_parts(src_tok, cx)

    def inside(o, s):
        return (s >= o - half) & (s <= o + half - 1)

    ctx_hit = o_ctx & s_ctx & inside(out_tok, src_tok)
    lat_hit = (~o_ctx) & (~s_ctx) & inside(o_row, s_row) & inside(o_col, s_col)
    return jnp.where(ctx_hit | lat_hit, 1.0, 0.0).astype(BF16)


def _pool_inv_count(gi, row0, tm, cx, seq):
    half = jnp.left_shift(1, gi)
    r = lax.broadcasted_iota(jnp.int32, (tm, 1), 0) + row0
    is_ctx, row, col = _token_parts(r, cx)

    def count(pos, size):
        return jnp.minimum(pos + half - 1, size - 1) - jnp.maximum(pos - half, 0) + 1

    cnt = jnp.where(is_ctx, count(r, cx), count(row, seq >> GRID_W_LOG2) * count(col, 1 << GRID_W_LOG2))
    return 1.0 / cnt.astype(F32)


def _lat_band(tm):
    side = POOL_REACH // tm
    return side, 2 * side + 1


def _lat_mask(gi, tm, cx, transposed):
    side, band = _lat_band(tm)
    return _pool_mask(gi, cx + side * tm, cx, tm, band * tm, cx, transposed)


def _store_padded_lat(dst_ref, lat, tm):
    side, _ = _lat_band(tm)
    seq = lat.shape[0]
    zeros = jnp.zeros((side * tm, lat.shape[1]), dst_ref.dtype)
    dst_ref[0:side * tm, :] = zeros
    dst_ref[side * tm + seq:, :] = zeros
    dst_ref[side * tm:side * tm + seq, :] = lat.astype(dst_ref.dtype)


def mix_a_fwd(z0, pool_w, pool_scale, cx):
    _, t, half_d = z0.shape
    g = half_d // N_POOL
    seq = t - cx
    tm = _row_block(cx)
    side, band = _lat_band(tm)

    def body(v_ref, ag_ref, w_ref, sc_ref, u_ref, vlat_ref, mask_ref):
        gi = pl.program_id(0)
        w = w_ref[...].astype(BF16)
        sc = sc_ref[...]
        _store_padded_lat(vlat_ref, v_ref[cx:, :], tm)
        mask_ref[...] = _lat_mask(gi, tm, cx, False)

        def finish(row0, window_sum):
            rows = pl.ds(row0, tm)
            pooled = window_sum * _pool_inv_count(gi, row0, tm, cx, seq) - v_ref[rows, :]
            mixed = _dot(pooled.astype(BF16), w) * sc
            u_ref[rows, :] = (mixed * _silu(ag_ref[rows, :])).astype(BF16)

        vctx = v_ref[0:cx, :].astype(BF16)
        for i in range(cx // tm):
            finish(i * tm, _dot(_pool_mask(gi, i * tm, 0, tm, cx, cx, False), vctx))

        def step(j, carry):
            src = vlat_ref[pl.ds(pl.multiple_of(j * tm, tm), band * tm), :]
            finish(pl.multiple_of(cx + j * tm, tm), _dot(mask_ref[...], src))
            return carry

        lax.fori_loop(0, seq // tm, step, 0)

    sec = lambda s: pl.BlockSpec((None, t, g), lambda j: (s, 0, j))
    return pl.pallas_call(
        body, name="mix_a_fwd", grid=(N_POOL,),
        in_specs=[sec(0), sec(1), pl.BlockSpec((None, g, g), lambda j: (j, 0, 0)),
                  pl.BlockSpec((1, g), lambda j: (0, j))],
        out_specs=pl.BlockSpec((t, g), lambda j: (0, j)),
        out_shape=jax.ShapeDtypeStruct((t, 2 * half_d), BF16),
        scratch_shapes=[pltpu.VMEM((seq + 2 * side * tm, g), BF16), pltpu.VMEM((tm, band * tm), BF16)],
        compiler_params=_cp(),
    )(z0, z0, pool_w, pool_scale)


def mix_a_bwd(z0, du, pool_w, pool_scale, cx):
    _, t, half_d = z0.shape
    g = half_d // N_POOL
    seq = t - cx
    tm = _row_block(cx)
    gq = g // 4
    side, band = _lat_band(tm)

    def body(v_ref, ag_ref, du_ref, w_ref, sc_ref, dz_ref, dw_ref, dsc_ref,
             vlat_ref, mask_ref, pooled_ref, dmx_ref, dpl_ref, wlat_ref, wctx_ref):
        gi = pl.program_id(0)
        w = w_ref[...].astype(BF16)
        sc = sc_ref[...]
        _store_padded_lat(vlat_ref, v_ref[cx:, :], tm)
        _store_padded_lat(wlat_ref, jnp.zeros((seq, g), BF16), tm)
        mask_ref[...] = _lat_mask(gi, tm, cx, False)

        def first(row0, window_sum, weighted_ref, weighted_row0):
            rows = pl.ds(row0, tm)
            inv = _pool_inv_count(gi, row0, tm, cx, seq)
            pooled = (window_sum * inv - v_ref[rows, :]).astype(BF16)
            pooled_ref[rows, :] = pooled
            mixed = _dot(pooled, w)
            ag = ag_ref[rows, :]
            duv = du_ref[rows, :]
            dz_ref[1, rows, :] = (duv * (mixed * sc) * _dsilu(ag)).astype(BF16)
            dms = duv * _silu(ag)
            dmixed = (dms * sc).astype(BF16)
            dmx_ref[rows, :] = dmixed
            dpooled = _dot(dmixed, w, NT)
            dpl_ref[rows, :] = dpooled
            weighted_ref[pl.ds(weighted_row0, tm), :] = (dpooled * inv).astype(BF16)
            return jnp.sum(dms * mixed, axis=0, keepdims=True)

        dsc = jnp.zeros((1, g), F32)
        vctx = v_ref[0:cx, :].astype(BF16)
        for i in range(cx // tm):
            dsc += first(i * tm, _dot(_pool_mask(gi, i * tm, 0, tm, cx, cx, False), vctx), wctx_ref, i * tm)

        def first_lat(j, acc):
            src = vlat_ref[pl.ds(pl.multiple_of(j * tm, tm), band * tm), :]
            return acc + first(pl.multiple_of(cx + j * tm, tm), _dot(mask_ref[...], src),
                               wlat_ref, pl.multiple_of((side + j) * tm, tm))

        dsc_ref[...] = lax.fori_loop(0, seq // tm, first_lat, dsc)
        dw = _dot(pooled_ref[...], dmx_ref[...], TN)
        for qi in range(4):
            dw_ref[qi] = dw[qi * gq:(qi + 1) * gq, :]

        wctx = wctx_ref[...]
        for i in range(cx // tm):
            rows = pl.ds(i * tm, tm)
            dz_ref[0, rows, :] = (_dot(_pool_mask(gi, i * tm, 0, tm, cx, cx, True), wctx)
                                  - dpl_ref[rows, :]).astype(BF16)
        mask_ref[...] = _lat_mask(gi, tm, cx, True)

        def second_lat(j, carry):
            rows = pl.ds(pl.multiple_of(cx + j * tm, tm), tm)
            src = wlat_ref[pl.ds(pl.multiple_of(j * tm, tm), band * tm), :]
            dz_ref[0, rows, :] = (_dot(mask_ref[...], src) - dpl_ref[rows, :]).astype(BF16)
            return carry

        lax.fori_loop(0, seq // tm, second_lat, 0)

    sec = lambda s: pl.BlockSpec((None, t, g), lambda j: (s, 0, j))
    padded = pltpu.VMEM((seq + 2 * side * tm, g), BF16)
    return pl.pallas_call(
        body, name="mix_a_bwd", grid=(N_POOL,),
        in_specs=[sec(0), sec(1), pl.BlockSpec((t, g), lambda j: (0, j)),
                  pl.BlockSpec((None, g, g), lambda j: (j, 0, 0)),
                  pl.BlockSpec((1, g), lambda j: (0, j))],
        out_specs=[pl.BlockSpec((2, t, g), lambda j: (0, 0, j)),
                   pl.BlockSpec((4, None, gq, g), lambda j: (0, j, 0, 0)),
                   pl.BlockSpec((1, g), lambda j: (0, j))],
        out_shape=[jax.ShapeDtypeStruct((2, t, half_d), BF16),
                   jax.ShapeDtypeStruct((4, N_POOL, gq, g), F32),
                   jax.ShapeDtypeStruct((1, half_d), F32)],
        scratch_shapes=[padded, pltpu.VMEM((tm, band * tm), BF16), pltpu.VMEM((t, g), BF16),
                        pltpu.VMEM((t, g), BF16), pltpu.VMEM((t, g), F32), padded, pltpu.VMEM((cx, g), BF16)],
        compiler_params=_cp(),
    )(z0, z0, du, pool_w, pool_scale)


def _conv_masks(t, cx):
    r = lax.broadcasted_iota(jnp.int32, (t, 1), 0)
    has_prev = jnp.where((r == 0) | (r == cx), 0.0, 1.0)
    has_next = jnp.where((r == cx - 1) | (r == t - 1), 0.0, 1.0)
    return has_prev, has_next


def mix_b_fwd(z0, conv_w, conv_b, u, cx):
    _, t, half_d = z0.shape
    gb = 128
    off = half_d // gb

    def body(bx_ref, bb_ref, bc_ref, bg_ref, w_ref, b_ref, _, u_ref):
        has_prev, has_next = _conv_masks(t, cx)
        tt = bc_ref[...] * bx_ref[...]
        prev = pltpu.roll(tt, 1, 0) * has_prev
        nxt = pltpu.roll(tt, t - 1, 0) * has_next
        cv = prev * w_ref[0:1, :] + tt * w_ref[1:2, :] + nxt * w_ref[2:3, :] + b_ref[...]
        u_ref[...] = (bb_ref[...] * cv * _silu(bg_ref[...])).astype(BF16)

    sec = lambda s: pl.BlockSpec((None, t, gb), lambda j: (s, 0, j))
    return pl.pallas_call(
        body, name="mix_b_fwd", grid=(half_d // gb,),
        in_specs=[sec(2), sec(3), sec(4), sec(5), pl.BlockSpec((3, gb), lambda j: (0, j)),
                  pl.BlockSpec((1, gb), lambda j: (0, j)), ANY],
        out_specs=pl.BlockSpec((t, gb), lambda j: (0, j + off)),
        out_shape=jax.ShapeDtypeStruct((t, 2 * half_d), BF16), input_output_aliases={6: 0},
        compiler_params=_cp(),
    )(z0, z0, z0, z0, conv_w, conv_b, u)


def mix_b_bwd(z0, du, conv_w, conv_b, cx):
    _, t, half_d = z0.shape
    gb = 128
    off = half_d // gb

    def body(bx_ref, bb_ref, bc_ref, bg_ref, du_ref, w_ref, b_ref, dz_ref, dw_ref, db_ref):
        has_prev, has_next = _conv_masks(t, cx)
        bx, bb, bc, bg = bx_ref[...], bb_ref[...], bc_ref[...], bg_ref[...]
        duv = du_ref[...]
        tt = bc * bx
        prev = pltpu.roll(tt, 1, 0) * has_prev
        nxt = pltpu.roll(tt, t - 1, 0) * has_next
        w0, w1, w2 = w_ref[0:1, :], w_ref[1:2, :], w_ref[2:3, :]
        cv = prev * w0 + tt * w1 + nxt * w2 + b_ref[...]
        sg = _silu(bg)
        dz_ref[1] = (duv * cv * sg).astype(BF16)
        dz_ref[3] = (duv * bb * cv * _dsilu(bg)).astype(BF16)
        dcv = duv * bb * sg
        dw_ref[0:1, :] = jnp.sum(dcv * prev, axis=0, keepdims=True)
        dw_ref[1:2, :] = jnp.sum(dcv * tt, axis=0, keepdims=True)
        dw_ref[2:3, :] = jnp.sum(dcv * nxt, axis=0, keepdims=True)
        db_ref[...] = jnp.sum(dcv, axis=0, keepdims=True)
        dt = (pltpu.roll(dcv * has_prev, t - 1, 0) * w0 + dcv * w1
              + pltpu.roll(dcv * has_next, 1, 0) * w2)
        dz_ref[0] = (dt * bc).astype(BF16)
        dz_ref[2] = (dt * bx).astype(BF16)

    sec = lambda s: pl.BlockSpec((None, t, gb), lambda j: (s, 0, j))
    return pl.pallas_call(
        body, name="mix_b_bwd", grid=(half_d // gb,),
        in_specs=[sec(2), sec(3), sec(4), sec(5), pl.BlockSpec((t, gb), lambda j: (0, j + off)),
                  pl.BlockSpec((3, gb), lambda j: (0, j)), pl.BlockSpec((1, gb), lambda j: (0, j))],
        out_specs=[pl.BlockSpec((4, t, gb), lambda j: (0, 0, j)),
                   pl.BlockSpec((3, gb), lambda j: (0, j)), pl.BlockSpec((1, gb), lambda j: (0, j))],
        out_shape=[jax.ShapeDtypeStruct((4, t, half_d), BF16),
                   jax.ShapeDtypeStruct((3, half_d), F32), jax.ShapeDtypeStruct((1, half_d), F32)],
        compiler_params=_cp(),
    )(z0, z0, z0, z0, du, conv_w, conv_b)


def _lower_bound(lbl_ref, d):
    l0, l1, l2 = lbl_ref[d, 0:1, :], lbl_ref[d, 1:2, :], lbl_ref[d, 2:3, :]
    mx = jnp.maximum(jnp.maximum(l0, l1), l2)
    e0, e1, e2 = jnp.exp(l0 - mx), jnp.exp(l1 - mx), jnp.exp(l2 - mx)
    inv = 1.0 / (e0 + e1 + e2)
    return (e0 + e1) * inv, (e0 * inv, e1 * inv, e2 * inv)


def _chunk_consts(d):
    r = lax.broadcasted_iota(jnp.int32, (CHUNK, CHUNK), 0)
    c = lax.broadcasted_iota(jnp.int32, (CHUNK, CHUNK), 1)
    keep = (c <= r) if d == 0 else (c >= r)
    return jnp.where(keep, 1.0, 0.0).astype(F32), keep


def _chunk_of_step(s, d, nc, ncc):
    if d == 0:
        return s
    return jnp.where(s < ncc, ncc - 1 - s, nc - 1 + ncc - s)


def _gates(z, lbv):
    e = jnp.exp(-jnp.abs(z))
    r = 1.0 / (1.0 + e)
    er = e * r
    pos = z >= 0.0
    sig = jnp.where(pos, r, er)
    nsig = jnp.where(pos, er, r)
    return sig, nsig, lbv + (1.0 - lbv) * sig


def _split3(x):
    hi = x.astype(BF16)
    r1 = x - hi.astype(F32)
    mid = r1.astype(BF16)
    lo = (r1 - mid.astype(F32)).astype(BF16)
    return jnp.concatenate([hi, mid, lo], axis=1)


def _cumsum_chunk(cum, x):
    y = _dot(cum, _split3(x))
    return y[:, :HEAD] + y[:, HEAD:2 * HEAD] + y[:, 2 * HEAD:]


def _chunk_rows(n):
    return pl.ds(pl.multiple_of(n * CHUNK, CHUNK), CHUNK)


def _group(nc, prefer=(4, 3, 2, 1)):
    return next(u for u in prefer if nc % u == 0)


WIDE_GROUP = (18, 12, 6, 4, 3, 2, 1)


def _decay_pass(lf_ref, bc_ref, dec_ref, cum, nc):
    grp = _group(nc, WIDE_GROUP)

    def step(m, carry):
        ns = [m * grp + u for u in range(grp)]
        lfc = [lf_ref[_chunk_rows(n), :] for n in ns]
        bc = [_cumsum_chunk(cum, x) for x in lfc]
        for u, n in enumerate(ns):
            bc_ref[_chunk_rows(n), :] = bc[u]
            dec_ref[n] = jnp.broadcast_to(jnp.exp(jnp.sum(lfc[u], axis=0, keepdims=True)), (8, HEAD))
        return carry

    lax.fori_loop(0, nc // grp, step, 0)


def hgrn_fwd(z1, lbl, onorm, cx):
    _, t, d = z1.shape
    seq = t - cx
    nc, ncc = t // CHUNK, cx // CHUNK

    grp, sgrp = _group(nc, (9, 6, 4, 3, 2, 1)), _group(nc, WIDE_GROUP)

    def body(zf_ref, zb_ref, v_ref, q_ref, g_ref, lbl_ref, on_ref, o_ref, r_ref, bcs_ref, ks_ref, decs_ref,
             lf_ref, k_ref, bc_ref, dec_ref, qd_ref, ki_ref, oacc_ref, ds_ref):
        for dr, z_ref in ((0, zf_ref), (1, zb_ref)):
            lbv, _ = _lower_bound(lbl_ref, dr)
            _, nsig, f = _gates(z_ref[...], lbv)
            lf_ref[...] = jnp.log(f)
            k_ref[...] = (1.0 - lbv) * nsig
            cum, keep = _chunk_consts(dr)
            _decay_pass(lf_ref, bc_ref, dec_ref, cum.astype(BF16), nc)
            bc = bc_ref[...]
            bcs_ref[dr] = bc
            ks_ref[dr] = k_ref[...]
            decs_ref[dr] = dec_ref[...]
            qd_ref[...] = (q_ref[...] * jnp.exp(bc)).astype(BF16)
            ki_ref[...] = (k_ref[...] * jnp.exp(-bc)).astype(BF16)

            def local_step(m, carry, dr=dr, keep=keep):
                ns = [m * grp + u for u in range(grp)]
                rows = [_chunk_rows(n) for n in ns]
                qd = [qd_ref[r, :] for r in rows]
                ki = [ki_ref[r, :] for r in rows]
                vc = [v_ref[r, :].astype(BF16) for r in rows]
                sc = [_dot(qd[u], ki[u], NT) for u in range(grp)]
                inc = [_dot(vc[u], ki[u], TN) for u in range(grp)]
                a = [jnp.where(keep, s, 0.0).astype(BF16) for s in sc]
                intra = [_dot(a[u], vc[u]) for u in range(grp)]
                for u in range(grp):
                    ds_ref[ns[u]] = inc[u] * dec_ref[ns[u]][0:1, :]
                    if dr == 0:
                        oacc_ref[rows[u], :] = intra[u]
                    else:
                        oacc_ref[rows[u], :] += intra[u]
                return carry

            lax.fori_loop(0, nc // grp, local_step, 0)

            def state_step(m, st, dr=dr):
                ns = [_chunk_of_step(m * sgrp + u, dr, nc, ncc) for u in range(sgrp)]
                rows = [_chunk_rows(n) for n in ns]
                sts = []
                for n in ns:
                    sts.append(st.astype(BF16))
                    st = st * dec_ref[n][0:1, :] + ds_ref[n]
                inter = [_dot(qd_ref[rows[u], :], sts[u], NT) for u in range(sgrp)]
                for u in range(sgrp):
                    oacc_ref[rows[u], :] += inter[u]
                return st

            lax.fori_loop(0, nc // sgrp, state_step, jnp.zeros((HEAD, HEAD), F32))

        o = oacc_ref[cx:, :]
        o_ref[...] = o
        rstd = lax.rsqrt(jnp.mean(o * o, axis=-1, keepdims=True) + EPS)
        r_ref[...] = (o * rstd * on_ref[...] * _silu(g_ref[cx:, :])).astype(BF16)

    sec = lambda s: pl.BlockSpec((None, t, HEAD), lambda h: (s, 0, h))
    col = pl.BlockSpec((seq, HEAD), lambda h: (0, h))
    tf32, tb16 = pltpu.VMEM((t, HEAD), F32), pltpu.VMEM((t, HEAD), BF16)
    return pl.pallas_call(
        body, name="hgrn_fwd", grid=(d // HEAD,),
        in_specs=[sec(0), sec(1), sec(2), sec(3), sec(4),
                  pl.BlockSpec((2, 3, HEAD), lambda h: (0, 0, h)), pl.BlockSpec((1, HEAD), lambda h: (0, h))],
        out_specs=[col, col, pl.BlockSpec((2, t, HEAD), lambda h: (0, 0, h)),
                   pl.BlockSpec((2, t, HEAD), lambda h: (0, 0, h)),
                   pl.BlockSpec((2, nc, 8, HEAD), lambda h: (0, 0, 0, h))],
        out_shape=[jax.ShapeDtypeStruct((seq, d), F32), jax.ShapeDtypeStruct((seq, d), BF16),
                   jax.ShapeDtypeStruct((2, t, d), F32), jax.ShapeDtypeStruct((2, t, d), F32),
                   jax.ShapeDtypeStruct((2, nc, 8, d), F32)],
        scratch_shapes=[tf32, tf32, tf32, pltpu.VMEM((nc, 8, HEAD), F32), tb16, tb16, tf32,
                        pltpu.VMEM((nc, HEAD, HEAD), F32)],
        compiler_params=_cp(),
    )(z1, z1, z1, z1, z1, lbl, onorm)


def hgrn_bwd(z1, lbl, onorm, o, dr_out, bcs, ks, decs, cx):
    _, t, d = z1.shape
    seq = t - cx
    nc, ncc = t // CHUNK, cx // CHUNK

    grp2, grp = _group(nc, (9, 6, 4, 3, 2, 1)), _group(nc, (12, 9, 6, 4, 3, 2, 1))

    def body(zf_ref, zb_ref, v_ref, q_ref, g_ref, lbl_ref, on_ref, o_ref, dr_ref, bcs_ref, ks_ref, decs_ref,
             dz_ref, don_ref, dlb_ref,
             qd_ref, ki_ref, do_ref, dqd_ref, dki_ref, dq_ref, dv_ref, ds_ref, dsl_ref):
        o = o_ref[...]
        g = g_ref[cx:, :]
        drv = dr_ref[...]
        onv = on_ref[...]
        rstd = lax.rsqrt(jnp.mean(o * o, axis=-1, keepdims=True) + EPS)
        ohat = o * rstd
        sg = _silu(g)
        don_ref[...] = jnp.sum(drv * ohat * sg, axis=0, keepdims=True)
        dz_ref[4, :cx, :] = jnp.zeros((cx, HEAD), BF16)
        dz_ref[4, cx:, :] = (drv * ohat * onv * _dsilu(g)).astype(BF16)
        dohat = drv * onv * sg
        do_ref[:cx, :] = jnp.zeros((cx, HEAD), BF16)
        do_ref[cx:, :] = (rstd * (dohat - ohat * jnp.mean(dohat * ohat, axis=-1, keepdims=True))).astype(BF16)

        for dr, z_ref in ((0, zf_ref), (1, zb_ref)):
            lbv, _ = _lower_bound(lbl_ref, dr)
            k_ref, bc_ref, dec_ref = ks_ref.at[dr], bcs_ref.at[dr], decs_ref.at[dr]
            _, keep = _chunk_consts(dr)
            cum_t = _chunk_consts(1 - dr)[0].astype(BF16)
            bc = bc_ref[...]
            qd_ref[...] = (q_ref[...] * jnp.exp(bc)).astype(BF16)
            ki_ref[...] = (k_ref[...] * jnp.exp(-bc)).astype(BF16)

            def local_step(m, carry, dr=dr, keep=keep):
                ns = [m * grp + u for u in range(grp)]
                rows = [_chunk_rows(n) for n in ns]
                rng = range(grp)
                qd = [qd_ref[r, :] for r in rows]
                ki = [ki_ref[r, :] for r in rows]
                doc = [do_ref[r, :] for r in rows]
                vc = [v_ref[r, :].astype(BF16) for r in rows]
                sc = [_dot(qd[u], ki[u], NT) for u in rng]
                dsc = [_dot(doc[u], vc[u], NT) for u in rng]
                inc = [_dot(vc[u], ki[u], TN) for u in rng]
                dinc = [_dot(doc[u], qd[u], TN) for u in rng]
                a = [jnp.where(keep, s, 0.0).astype(BF16) for s in sc]
                da = [jnp.where(keep, s, 0.0).astype(BF16) for s in dsc]
                dqd = [_dot(da[u], ki[u]) for u in rng]
                dki = [_dot(da[u], qd[u], TN) for u in rng]
                dv = [_dot(a[u], doc[u], TN) for u in rng]
                for u in rng:
                    ds_ref[ns[u]] = inc[u] * dec_ref[ns[u]][0:1, :]
                    dsl_ref[ns[u]] = dinc[u]
                    dqd_ref[rows[u], :] = dqd[u]
                    dki_ref[rows[u], :] = dki[u]
                    if dr == 0:
                        dv_ref[rows[u], :] = dv[u]
                    else:
                        dv_ref[rows[u], :] += dv[u]
                return carry

            lax.fori_loop(0, nc // grp, local_step, 0)

            def state_step(s, st, dr=dr):
                n = _chunk_of_step(s, dr, nc, ncc)
                inc = ds_ref[n]
                ds_ref[n] = st
                return st * dec_ref[n][0:1, :] + inc

            lax.fori_loop(0, nc, state_step, jnp.zeros((HEAD, HEAD), F32), unroll=4)

            def dstate_step(s, dst, dr=dr):
                n = _chunk_of_step(nc - 1 - s, dr, nc, ncc)
                inc = dsl_ref[n]
                dsl_ref[n] = dst
                return inc + dst * dec_ref[n][0:1, :]

            lax.fori_loop(0, nc, dstate_step, jnp.zeros((HEAD, HEAD), F32), unroll=4)

            def grad_step(m, carry, dr=dr, cum_t=cum_t):
                ns = [m * grp2 + u for u in range(grp2)]
                rows = [_chunk_rows(n) for n in ns]
                rng = range(grp2)
                st0 = [ds_ref[n] for n in ns]
                dst = [dsl_ref[n] for n in ns]
                dstb = [x.astype(BF16) for x in dst]
                dec = [dec_ref[n][0:1, :] for n in ns]
                doc = [do_ref[r, :] for r in rows]
                vc = [v_ref[r, :].astype(BF16) for r in rows]
                e = [jnp.exp(bc_ref[r, :]) for r in rows]
                einv = [jnp.exp(-bc_ref[r, :]) for r in rows]
                qd = [q_ref[rows[u], :] * e[u] for u in rng]
                ki = [k_ref[rows[u], :] * einv[u] for u in rng]
                kd = [ki[u] * dec[u] for u in rng]
                dqd_st = [_dot(doc[u], st0[u].astype(BF16)) for u in rng]
                dkd = [_dot(vc[u], dstb[u]) for u in rng]
                dv_st = [_dot(kd[u].astype(BF16), dstb[u], NT) for u in rng]
                dqd = [dqd_ref[rows[u], :] + dqd_st[u] for u in rng]
                dki = [dki_ref[r, :] for r in rows]
                dbc = [dqd[u] * qd[u] - dki[u] * ki[u] - dkd[u] * kd[u] for u in rng]
                cs = [_cumsum_chunk(cum_t, x) for x in dbc]
                for u in rng:
                    ddec = jnp.sum(dst[u] * st0[u], axis=0, keepdims=True)
                    dbl = jnp.sum(dkd[u] * kd[u], axis=0, keepdims=True) + ddec * dec[u]
                    dv_ref[rows[u], :] += dv_st[u]
                    dqd_ref[rows[u], :] = cs[u] + dbl
                    dki_ref[rows[u], :] = dki[u] * einv[u] + dkd[u] * (einv[u] * dec[u])
                    if dr == 0:
                        dq_ref[rows[u], :] = dqd[u] * e[u]
                    else:
                        dq_ref[rows[u], :] += dqd[u] * e[u]
                return carry

            lax.fori_loop(0, nc // grp2, grad_step, 0)

            sig, nsig, f = _gates(z_ref[...], lbv)
            common = (dqd_ref[...] / f - dki_ref[...]) * nsig
            dz_ref[dr] = (common * ((1.0 - lbv) * sig)).astype(BF16)
            dlb_ref[dr:dr + 1, :] = jnp.sum(common, axis=0, keepdims=True)

        dz_ref[2] = dv_ref[...].astype(BF16)
        dz_ref[3] = dq_ref[...].astype(BF16)

    sec = lambda s: pl.BlockSpec((None, t, HEAD), lambda h: (s, 0, h))
    col = pl.BlockSpec((seq, HEAD), lambda h: (0, h))
    tf32, tb16 = pltpu.VMEM((t, HEAD), F32), pltpu.VMEM((t, HEAD), BF16)
    states = pltpu.VMEM((nc, HEAD, HEAD), F32)
    return pl.pallas_call(
        body, name="hgrn_bwd", grid=(d // HEAD,),
        in_specs=[sec(0), sec(1), sec(2), sec(3), sec(4),
                  pl.BlockSpec((2, 3, HEAD), lambda h: (0, 0, h)), pl.BlockSpec((1, HEAD), lambda h: (0, h)),
                  col, col, pl.BlockSpec((2, t, HEAD), lambda h: (0, 0, h)),
                  pl.BlockSpec((2, t, HEAD), lambda h: (0, 0, h)),
                  pl.BlockSpec((2, nc, 8, HEAD), lambda h: (0, 0, 0, h))],
        out_specs=[pl.BlockSpec((5, t, HEAD), lambda h: (0, 0, h)),
                   pl.BlockSpec((1, HEAD), lambda h: (0, h)), pl.BlockSpec((2, HEAD), lambda h: (0, h))],
        out_shape=[jax.ShapeDtypeStruct((5, t, d), BF16), jax.ShapeDtypeStruct((1, d), F32),
                   jax.ShapeDtypeStruct((2, d), F32)],
        scratch_shapes=[tb16, tb16, tb16, tf32, tf32, tf32, tf32, states, states],
        compiler_params=_cp(),
    )(z1, z1, z1, z1, z1, lbl, onorm, o, dr_out, bcs, ks, decs)


def _place():
    x, y, c = lax.axis_index("x"), lax.axis_index("y"), lax.axis_index("c")
    chips = [(1 - x, y), (x, 1 - y), (1 - x, 1 - y)]
    return x, y, c, chips


def _relay_chips():
    x, y, c, _ = _place()
    first = c == 0
    near = (jnp.where(first, 1 - x, x), jnp.where(first, y, 1 - y))
    far = (jnp.where(first, x, 1 - x), jnp.where(first, 1 - y, y))
    return near, far, (1 - x, 1 - y)


def allgather_shards(bufs, after):
    n = len(bufs)

    def body(*refs):
        outs = refs[n + 1:2 * n + 1]
        done_ref, send_sems, recv_sems = refs[2 * n + 1:]
        done_ref[...] = jnp.zeros((8, 128), F32)
        x, y, c, _ = _place()
        me = (x, y, c)
        p = 2 * x + y
        near, far, diag = _relay_chips()
        half = [pl.ds(c * (s.shape[1] // 2), s.shape[1] // 2) for s in bufs]
        other = [pl.ds((1 - c) * (s.shape[1] // 2), s.shape[1] // 2) for s in bufs]
        slot = lambda chip: 2 * chip[0] + chip[1]

        def remote(i, k, ref, to):
            return pltpu.make_async_remote_copy(src_ref=ref, dst_ref=ref, send_sem=send_sems.at[6 * i + k],
                                                recv_sem=recv_sems.at[6 * i + k], device_id=to, device_id_type=MESH)

        sends = []

        def send(i, k, ref, to):
            cp = remote(i, k, ref, to)
            cp.start()
            sends.append(cp)

        for i in range(n):
            mine = outs[i].at[p, half[i]]
            send(i, 0, mine, (*near, c))
            send(i, 1, mine, (*far, c))
        for i in range(n):
            landed = outs[i].at[slot(near), half[i]]
            remote(i, 0, landed, me).wait_recv()
            send(i, 2, landed, (*far, c))
            send(i, 3, landed, (x, y, 1 - c))
        for i in range(n):
            landed = outs[i].at[slot(far), half[i]]
            remote(i, 1, landed, me).wait_recv()
            send(i, 4, landed, (x, y, 1 - c))
        for i in range(n):
            landed = outs[i].at[slot(diag), half[i]]
            remote(i, 2, landed, me).wait_recv()
            send(i, 5, landed, (x, y, 1 - c))
        for i in range(n):
            for k, chip in ((3, far), (4, near), (5, diag)):
                remote(i, k, outs[i].at[slot(chip), other[i]], me).wait_recv()
        for cp in sends:
            cp.wait_send()

    return pl.pallas_call(
        body, name="allgather_shards",
        in_specs=[ANY] * (n + 1), out_specs=[ANY] * n + [VMEM],
        out_shape=[jax.ShapeDtypeStruct(s.shape, s.dtype) for s in bufs] + [jax.ShapeDtypeStruct((8, 128), F32)],
        input_output_aliases={i: i for i in range(n)},
        scratch_shapes=[pltpu.SemaphoreType.DMA((6 * n,)), pltpu.SemaphoreType.DMA((6 * n,))],
        compiler_params=pltpu.CompilerParams(has_side_effects=True),
    )(*bufs, after)


def pair_sum(grad, got, chip_core):
    _, r, cc = grad.shape
    hr = r // 2
    tr = 256 if hr % 256 == 0 else hr
    nb = hr // tr

    def body(cc_ref, a_ref, b_ref, own_ref, sb_ref):
        s = a_ref[...].astype(F32) + b_ref[...].astype(F32)
        sb_ref[...] = s.astype(BF16)

        @pl.when(pl.program_id(1) == cc_ref[0])
        def _():
            own_ref[...] = s

    grid_spec = pltpu.PrefetchScalarGridSpec(
        num_scalar_prefetch=1, grid=(nb, 4),
        in_specs=[pl.BlockSpec((None, tr, cc), lambda i, qi, cc_ref: (qi, cc_ref[1] * nb + i, 0)),
                  pl.BlockSpec((None, tr, cc), lambda i, qi, cc_ref: (qi, i, 0))],
        out_specs=[pl.BlockSpec((tr, cc), lambda i, qi, cc_ref: (i, 0)),
                   pl.BlockSpec((None, tr, cc), lambda i, qi, cc_ref: (qi, i, 0))])
    return pl.pallas_call(
        body, name="pair_sum", grid_spec=grid_spec,
        out_shape=[jax.ShapeDtypeStruct((hr, cc), F32), jax.ShapeDtypeStruct((4, hr, cc), BF16)],
        compiler_params=_cp(),
    )(chip_core, grad, got)


def owner_sum(own, got, chip_core):
    hr, cc = own.shape
    tr = 256 if hr % 256 == 0 else hr
    nb = hr // tr

    def body(cc_ref, a_ref, b_ref, o_ref):
        s = a_ref[...] + b_ref[0].astype(F32)
        s = s + b_ref[1].astype(F32)
        o_ref[...] = s + b_ref[2].astype(F32)

    grid_spec = pltpu.PrefetchScalarGridSpec(
        num_scalar_prefetch=1, grid=(nb,),
        in_specs=[pl.BlockSpec((tr, cc), lambda i, cc_ref: (i, 0)),
                  pl.BlockSpec((3, tr, cc), lambda i, cc_ref: (0, i, 0))],
        out_specs=pl.BlockSpec((tr, cc), lambda i, cc_ref: (cc_ref[1] * nb + i, 0)))
    return pl.pallas_call(
        body, name="owner_sum", grid_spec=grid_spec,
        out_shape=jax.ShapeDtypeStruct((2 * hr, cc), F32), compiler_params=_cp(),
    )(chip_core, own, got)


def share_halves(bufs):
    n = len(bufs)

    def body(*refs):
        outs = refs[n:2 * n]
        send_sems, recv_sems = refs[2 * n:]
        x, y, c, _ = _place()
        copies = []
        for i in range(n):
            hr = bufs[i].shape[0] // 2
            mine = outs[i].at[pl.ds(c * hr, hr)]
            cp = pltpu.make_async_remote_copy(
                src_ref=mine, dst_ref=mine, send_sem=send_sems.at[i], recv_sem=recv_sems.at[i],
                device_id=(x, y, 1 - c), device_id_type=MESH)
            cp.start()
            copies.append((cp, outs[i].at[pl.ds((1 - c) * hr, hr)]))
        for i, (cp, theirs) in enumerate(copies):
            cp.wait_send()
            pltpu.make_async_remote_copy(
                src_ref=theirs, dst_ref=theirs, send_sem=send_sems.at[i], recv_sem=recv_sems.at[i],
                device_id=(x, y, c), device_id_type=MESH).wait_recv()

    return pl.pallas_call(
        body, name="share_halves",
        in_specs=[ANY] * n, out_specs=[ANY] * n,
        out_shape=[jax.ShapeDtypeStruct(b.shape, b.dtype) for b in bufs],
        input_output_aliases={i: i for i in range(n)},
        scratch_shapes=[pltpu.SemaphoreType.DMA((n,)), pltpu.SemaphoreType.DMA((n,))],
        compiler_params=pltpu.CompilerParams(has_side_effects=True),
    )(*bufs)


def allgather8(v, name):
    r, n = v.shape

    def body(v_ref, out_ref, send_sems, recv_sems):
        x, y, c, _ = _place()
        me = 4 * x + 2 * y + c
        out_ref[me] = v_ref[...]

        def copy(k, slot, to):
            return pltpu.make_async_remote_copy(
                src_ref=v_ref, dst_ref=out_ref.at[slot], send_sem=send_sems.at[k - 1],
                recv_sem=recv_sems.at[k - 1], device_id=to, device_id_type=MESH)

        peers = []
        for k in range(1, 8):
            px = 1 - x if (k >> 2) & 1 else x
            py = 1 - y if (k >> 1) & 1 else y
            pc = 1 - c if k & 1 else c
            peers.append((px, py, pc))
            copy(k, me, (px, py, pc)).start()
        for k, (px, py, pc) in enumerate(peers, start=1):
            copy(k, 4 * px + 2 * py + pc, (x, y, c)).wait_recv()
        for k, peer in enumerate(peers, start=1):
            copy(k, me, peer).wait_send()

    return pl.pallas_call(
        body, name=name, in_specs=[VMEM], out_specs=VMEM,
        out_shape=jax.ShapeDtypeStruct((8, r, n), v.dtype),
        scratch_shapes=[pltpu.SemaphoreType.DMA((7,)), pltpu.SemaphoreType.DMA((7,))],
        compiler_params=_cp(has_side_effects=True),
    )(v)


HBM = pl.BlockSpec(memory_space=pltpu.HBM)
SEM = pl.BlockSpec(memory_space=pltpu.SEMAPHORE)
DATAFLOW = pltpu.SideEffectType.DATAFLOW_SIDE_EFFECTING


def _descriptors(plan, refs, send_sems, recv_sems, arrivals=True):
    x, y, c, _ = _place()
    sends, recvs = plan(refs)
    out = [pltpu.make_async_remote_copy(src_ref=src, dst_ref=dst, send_sem=send_sems.at[k],
                                        recv_sem=recv_sems.at[k], device_id=to, device_id_type=MESH)
           for k, (src, dst, to) in enumerate(sends)]
    if not arrivals:
        return out, []
    inn = [pltpu.make_async_remote_copy(src_ref=land, dst_ref=land, send_sem=send_sems.at[k],
                                        recv_sem=recv_sems.at[k], device_id=(x, y, c), device_id_type=MESH)
           for k, land in enumerate(recvs)]
    return out, inn


def copies_start(name, arrays, n_copies, plan, after):
    na = len(arrays)

    def body(*refs):
        out, _ = _descriptors(plan, refs[:na], refs[na + 1], refs[na + 2], arrivals=False)
        for cp in out:
            cp.start()
        refs[-1][...] = jnp.zeros((8, 128), F32)

    res = pl.pallas_call(
        body, name=name,
        out_shape=(pltpu.SemaphoreType.DMA((n_copies,)), pltpu.SemaphoreType.DMA((n_copies,)),
                   *[pltpu.HBM(a.shape, a.dtype) for a in arrays], jax.ShapeDtypeStruct((8, 128), F32)),
        in_specs=[HBM] * na + [ANY], out_specs=(SEM, SEM, *[HBM] * na, VMEM),
        input_output_aliases={i: i + 2 for i in range(na)},
        compiler_params=pltpu.CompilerParams(has_side_effects=DATAFLOW),
    )(*[pltpu.with_memory_space_constraint(a, pltpu.HBM) for a in arrays], after)
    return res[0], res[1], list(res[2:2 + na]), res[-1]


def copies_wait(name, started, plan, after):
    send_sems, recv_sems, arrays, _ = started
    na = len(arrays)
    after = list(after) if isinstance(after, (list, tuple)) else [after]

    def body(*refs):
        out, inn = _descriptors(plan, refs[:na], refs[na], refs[na + 1])
        for cp in out:
            cp.wait_send()
        for cp in inn:
            cp.wait_recv()
        refs[-1][...] = jnp.zeros((8, 128), F32)

    res = pl.pallas_call(
        body, name=name,
        out_shape=(*[pltpu.HBM(a.shape, a.dtype) for a in arrays], jax.ShapeDtypeStruct((8, 128), F32)),
        in_specs=[HBM] * na + [SEM, SEM] + [ANY] * len(after), out_specs=(*[HBM] * na, VMEM),
        input_output_aliases={i: i for i in range(na)},
        compiler_params=pltpu.CompilerParams(has_side_effects=DATAFLOW),
    )(*arrays, send_sems, recv_sems, *after)
    return list(res[:na]), res[-1]


def _rows_half(r, c):
    return pl.ds(c * (r // 2), r // 2), pl.ds((1 - c) * (r // 2), r // 2)


def plan_gather_neighbours(refs):
    x, y, c, _ = _place()
    p = 2 * x + y
    near, far, _ = _relay_chips()
    sends, recvs = [], []
    for buf in refs:
        mine, _ = _rows_half(buf.shape[1], c)
        for chip in (near, far):
            sends.append((buf.at[p, mine], buf.at[p, mine], (*chip, c)))
            recvs.append(buf.at[2 * chip[0] + chip[1], mine])
    return sends, recvs


def plan_gather_relay(refs):
    x, y, c, _ = _place()
    near, far, diag = _relay_chips()
    slot = lambda chip: 2 * chip[0] + chip[1]
    sends, recvs = [], []
    for buf in refs:
        mine, theirs = _rows_half(buf.shape[1], c)
        landed = buf.at[slot(near), mine]
        sends.append((landed, landed, (*far, c)))
        recvs.append(buf.at[slot(diag), mine])
        for sent, got in ((near, far), (far, near)):
            sends.append((buf.at[slot(sent), mine], buf.at[slot(sent), mine], (x, y, 1 - c)))
            recvs.append(buf.at[slot(got), theirs])
    return sends, recvs


def plan_gather_d2d(refs):
    x, y, c, _ = _place()
    _, _, diag = _relay_chips()
    sends, recvs = [], []
    for buf in refs:
        mine, theirs = _rows_half(buf.shape[1], c)
        landed = buf.at[2 * diag[0] + diag[1], mine]
        sends.append((landed, landed, (x, y, 1 - c)))
        recvs.append(buf.at[2 * diag[0] + diag[1], theirs])
    return sends, recvs


def plan_exchange(refs):
    x, y, c, _ = _place()
    n = len(refs) // 2
    sends, recvs = [], []
    for grad, land in zip(refs[:n], refs[n:]):
        _, theirs = _rows_half(grad.shape[1], c)
        sends.append((grad.at[:, theirs], land, (x, y, 1 - c)))
        recvs.append(land)
    return sends, recvs


def plan_scatter(refs):
    x, y, c, chips = _place()
    n = len(refs) // 2
    sends, recvs = [], []
    for part, land in zip(refs[:n], refs[n:]):
        for j, chip in enumerate(chips):
            sends.append((part.at[2 * chip[0] + chip[1]], land.at[j], (*chip, c)))
            recvs.append(land.at[j])
    return sends, recvs


def plan_share(refs):
    x, y, c, _ = _place()
    sends, recvs = [], []
    for buf in refs:
        mine, theirs = _rows_half(buf.shape[0], c)
        sends.append((buf.at[mine], buf.at[mine], (x, y, 1 - c)))
        recvs.append(buf.at[theirs])
    return sends, recvs


def put_in_slot(w, chip, dtype, name):
    r, c = w.shape
    tr = 256 if r % 256 == 0 else r

    def body(chip_ref, w_ref, o_ref):
        o_ref[...] = w_ref[...].astype(dtype)

    grid_spec = pltpu.PrefetchScalarGridSpec(
        num_scalar_prefetch=1, grid=(r // tr,),
        in_specs=[pl.BlockSpec((tr, c), lambda i, chip_ref: (i, 0))],
        out_specs=pl.BlockSpec((None, tr, c), lambda i, chip_ref: (chip_ref[0], i, 0)))
    return pl.pallas_call(body, name=name, grid_spec=grid_spec,
                          out_shape=jax.ShapeDtypeStruct((4, r, c), dtype), compiler_params=_cp())(chip, w)


def ada_fwd(s_in, ada_w, ada_b, tn):
    nl, d, ws = ada_w.shape

    def body(s_ref, w_ref, b_ref, so_ref, mod_ref):
        s = _silu(s_ref[...])
        so_ref[...] = s
        mod_ref[...] = _dot(s.astype(BF16), w_ref[...].astype(BF16)) + b_ref[...]

    return pl.pallas_call(
        body, name="ada_fwd", grid=(nl, ws // tn),
        in_specs=[pl.BlockSpec((16, d), lambda l, j: (0, 0)),
                  pl.BlockSpec((None, d, tn), lambda l, j: (l, 0, j)),
                  pl.BlockSpec((None, 1, tn), lambda l, j: (l, 0, j))],
        out_specs=[pl.BlockSpec((16, d), lambda l, j: (0, 0)),
                   pl.BlockSpec((None, 16, tn), lambda l, j: (l, 0, j))],
        out_shape=[jax.ShapeDtypeStruct((16, d), F32), jax.ShapeDtypeStruct((nl, 16, ws), F32)],
        compiler_params=_cp(),
    )(s_in, ada_w, ada_b)


def _adamw_math(w, g, m, v):
    m = ADAM_B1 * m + (1.0 - ADAM_B1) * g
    v = ADAM_B2 * v + (1.0 - ADAM_B2) * (g * g)
    m_hat = m / (1.0 - ADAM_B1 ** ADAM_STEP)
    v_hat = v / (1.0 - ADAM_B2 ** ADAM_STEP)
    delta = -ADAM_LR * (m_hat / (jnp.sqrt(v_hat) + ADAM_EPS) + ADAM_WD * w)
    return delta, m, v


def ada_bwd_adamw(s, dm, w, m, v):
    nl, d, ws = w.shape
    tr = 256 if d % 256 == 0 else 128

    def body(s_ref, dm_ref, w_ref, m_ref, v_ref, g_ref, dl_ref, mo_ref, vo_ref, dc_ref):
        dmv = dm_ref[...].astype(BF16)
        wv = w_ref[...]
        g = _dot(s_ref[...].astype(BF16), dmv, TN)
        g_ref[...] = g
        dl_ref[...], mo_ref[...], vo_ref[...] = _adamw_math(wv, g, m_ref[...], v_ref[...])
        dc_ref[...] = _dot(dmv[8:16, :], wv.astype(BF16), NT)

    wblk = pl.BlockSpec((None, tr, ws), lambda l, i: (l, i, 0))
    wshape = jax.ShapeDtypeStruct((nl, d, ws), F32)
    return pl.pallas_call(
        body, name="ada_bwd_adamw", grid=(nl, d // tr),
        in_specs=[pl.BlockSpec((16, tr), lambda l, i: (0, i)),
                  pl.BlockSpec((None, 16, ws), lambda l, i: (l, 0, 0)), wblk, wblk, wblk],
        out_specs=[wblk, wblk, wblk, wblk, pl.BlockSpec((None, 8, tr), lambda l, i: (l, 0, i))],
        out_shape=[wshape, wshape, wshape, wshape, jax.ShapeDtypeStruct((nl, 8, d), F32)],
        compiler_params=_cp(),
    )(s, dm, w, m, v)


def adamw(w, g, m, v, name, with_grad=False):
    r, c = w.shape
    tr = 256 if r % 256 == 0 else r

    def body(w_ref, g_ref, m_ref, v_ref, dl_ref, mo_ref, vo_ref, *g_out):
        gv = g_ref[...]
        dl_ref[...], mo_ref[...], vo_ref[...] = _adamw_math(w_ref[...], gv, m_ref[...], v_ref[...])
        if with_grad:
            g_out[0][...] = gv

    blk = pl.BlockSpec((tr, c), lambda i: (i, 0))
    shape = jax.ShapeDtypeStruct((r, c), F32)
    n_out = 4 if with_grad else 3
    return pl.pallas_call(body, name=name, grid=(r // tr,), in_specs=[blk] * 4, out_specs=[blk] * n_out,
                          out_shape=[shape] * n_out, compiler_params=_cp())(w, g, m, v)


ROW_MOD = 10


def small_reduce(gathered):
    _, rows, d = gathered.shape

    def body(g_ref, o_ref):
        tot = g_ref[0]
        for b in range(1, 8):
            tot = tot + g_ref[b]
        o_ref[0:rows, :] = tot
        for layer in range(2):
            lat = ROW_MOD + 6 * layer
            o_ref[24 + 3 * layer:27 + 3 * layer, :] = tot[lat:lat + 3, :] + tot[lat + 3:lat + 6, :]
        o_ref[30:32, :] = jnp.zeros((2, d), F32)

    return pl.pallas_call(body, name="small_reduce", in_specs=[VMEM], out_specs=VMEM,
                          out_shape=jax.ShapeDtypeStruct((32, d), F32), compiler_params=_cp())(gathered)


def lb_logits_grad(lbl, dlb):
    _, _, n = lbl.shape

    def body(l_ref, d_ref, o_ref):
        for dr in range(2):
            _, (p0, p1, p2) = _lower_bound(l_ref, dr)
            dv = d_ref[dr:dr + 1, :]
            o_ref[dr, 0:1, :] = p0 * p2 * dv
            o_ref[dr, 1:2, :] = p1 * p2 * dv
            o_ref[dr, 2:3, :] = -p2 * (p0 + p1) * dv

    return pl.pallas_call(body, name="lb_logits_grad", in_specs=[VMEM, VMEM], out_specs=VMEM,
                          out_shape=jax.ShapeDtypeStruct((2, 3, n), F32), compiler_params=_cp())(lbl, dlb)


def c_ctx_grad(parts, c_ctx):
    d = c_ctx.shape[1]

    def body(p_ref, c_ref, o_ref):
        tot = p_ref[0, 0:1, :]
        for chip in range(1, 4):
            tot = tot + p_ref[2 * chip, 0:1, :]
        o_ref[...] = tot * _dsilu(c_ref[...])

    return pl.pallas_call(body, name="c_ctx_grad", in_specs=[VMEM, VMEM], out_specs=VMEM,
                          out_shape=jax.ShapeDtypeStruct((1, d), F32), compiler_params=_cp())(parts, c_ctx)


def kernel(x, c, ctx, c_ctx, ada_w, ada_b, pre_g, post_g, ev_w_in, ev_pool_w, ev_pool_scale, ev_conv_w, ev_conv_b, ev_w_out, od_w_in, od_onorm_g, od_w_out, lb_logits, loss_target, m_c_ctx, m_ada_w, m_ada_b, m_pre_g, m_post_g, m_ev_w_in, m_ev_pool_w, m_ev_pool_scale, m_ev_conv_w, m_ev_conv_b, m_ev_w_out, m_od_w_in, m_od_onorm_g, m_od_w_out, m_lb_logits, v_c_ctx, v_ada_w, v_ada_b, v_pre_g, v_post_g, v_ev_w_in, v_ev_pool_w, v_ev_pool_scale, v_ev_conv_w, v_ev_conv_b, v_ev_w_out, v_od_w_in, v_od_onorm_g, v_od_w_out, v_lb_logits):
    _, seq, d = x.shape
    cx = ctx.shape[1]
    t = cx + seq
    half_d = d // 2
    g = half_d // N_POOL
    tn = d // 4
    xi, yi, ci = lax.axis_index("x"), lax.axis_index("y"), lax.axis_index("c")
    chip = 2 * xi + yi
    me = 2 * chip + ci
    chip_arr = jnp.reshape(chip, (1,)).astype(jnp.int32)
    chip_core_arr = jnp.stack([chip, ci]).astype(jnp.int32)

    c_rows = jnp.concatenate([c, jnp.zeros((7, d), F32)], axis=0)
    c_all = allgather8(c_rows, "allgather_c")[:, 0, :]
    s_in = jnp.concatenate([c_all, c_ctx.reshape(1, d), jnp.zeros((7, d), F32)], axis=0)
    ws_ada = ada_w.shape[2]
    ada_b_mine = lax.dynamic_slice(ada_b, (0, chip * ws_ada), (2, ws_ada)).reshape(2, 1, ws_ada)
    s_act, mod_mine = ada_fwd(s_in, ada_w, ada_b_mine, tn)
    mod_all = allgather8(mod_mine.reshape(32, ws_ada), "allgather_mod")

    pad = lambda a, rows: jnp.concatenate([a, jnp.zeros((rows - a.shape[0], g), F32)], axis=0)
    small = jnp.concatenate([
        ev_pool_w.reshape(g, g), pad(ev_conv_w.reshape(3, g), 8), pad(od_onorm_g.reshape(2, g), 8),
        pad(lb_logits.reshape(12, g), 16)], axis=0)
    ev_in_g, ev_out_g, small_g, ev_done = allgather_shards([
        put_in_slot(ev_w_in[0], chip_arr, BF16, "cast_ev_w_in"),
        put_in_slot(ev_w_out[0], chip_arr, BF16, "cast_ev_w_out"),
        put_in_slot(small, chip_arr, F32, "place_small")], mod_all)
    od_ici = copies_start("gather_od_ici_start", [
        put_in_slot(od_w_in[0], chip_arr, BF16, "cast_od_w_in"),
        put_in_slot(od_w_out[0], chip_arr, BF16, "cast_od_w_out")], 4, plan_gather_neighbours, ev_done)
    ev_out3 = ev_out_g.reshape(1, d, d)
    pool_w_full = small_g[:, :g].reshape(4, N_POOL, g // 4, g).transpose(1, 0, 2, 3).reshape(N_POOL, g, g)
    conv_w_full = small_g[:, g:g + 3].transpose(1, 0, 2).reshape(3, half_d)
    onorm_full = small_g[:, g + 8:g + 10].reshape(1, d)
    lbl_full = small_g[:, g + 16:g + 28].reshape(4, 2, 3, 2 * g).transpose(1, 2, 0, 3).reshape(2, 3, d)

    mod_full = mod_all[0::2].reshape(4, 2, 16, ws_ada).transpose(1, 2, 0, 3).reshape(2, 16, 3 * d)
    mod_lat = lax.dynamic_slice(mod_full, (0, me, 0), (2, 1, 3 * d))
    mods = jnp.concatenate([mod_full[:, 8:9], mod_lat], axis=1)
    shift, scale, gate = mods[:, :, :d], mods[:, :, d:2 * d], mods[:, :, 2 * d:]

    h0, xs = normmod_fwd_joining(ctx[0], x[0], pre_g[0:1] + od_ici[3][0:1, 0:1], shift[0], scale[0])
    z0 = mm_nn(h0, ev_in_g, half_d, tn, "mm_ev_in")
    u = mix_b_fwd(z0, conv_w_full, ev_conv_b, mix_a_fwd(z0, pool_w_full, ev_pool_scale, cx), cx)
    od_relay = copies_start("gather_od_relay_start",
                            copies_wait("gather_od_ici_wait", od_ici, plan_gather_neighbours, u)[0],
                            6, plan_gather_relay, u)
    y0 = mm_nn(u, ev_out3, d, tn, "mm_ev_out")[0]
    xs1, h1 = post_fwd_norm(xs, y0, post_g[0:1] + od_relay[3][0:1, 0:1], gate[0],
                            pre_g[1:2], shift[1], scale[1], cx)
    od_d2d = copies_start("gather_od_d2d_start",
                          copies_wait("gather_od_relay_wait", od_relay, plan_gather_relay, xs1)[0],
                          2, plan_gather_d2d, xs1)
    (od_in_g, od_out_g), _ = copies_wait("gather_od_d2d_wait", od_d2d, plan_gather_d2d, od_d2d[3])
    od_out3 = od_out_g.reshape(1, d, d)

    z1 = mm_nn(h1, od_in_g, d, tn, "mm_od_in")
    o1, r1, bcs1, ks1, decs1 = hgrn_fwd(z1, lbl_full, onorm_full, cx)
    y1 = mm_nn(r1, od_out3, d, tn, "mm_od_out")[0]
    sq, dx2, dy1, dgate1, dpost1 = post_loss(xs1, y1, post_g[1:2], gate[1], loss_target[0], cx)

    dr1 = mm_nt(dy1[None], None, od_out3, tn, "mm_od_out_dx")
    g_od_out = mm_tn(r1, dy1[None], None, d, tn, "mm_od_out_dw")
    dz1, donorm, dlb = hgrn_bwd(z1, lbl_full, onorm_full, o1, dr1, bcs1, ks1, decs1, cx)
    dh1 = mm_nt(dz1, None, od_in_g, tn, "mm_od_in_dx")
    g_od_in = mm_tn(h1, dz1, None, od_in_g.shape[2], tn, "mm_od_in_dw")
    dxs1, dpre1, dshift1, dscale1 = normmod_bwd(xs1, dh1, pre_g[1:2], scale[1], dx2, cx, True)

    od_grads = [g_od_in, g_od_out.reshape(4, d // 4, d)]
    half_zone = lambda a, lead, dt: lax.empty((lead, a.shape[1] // 2, a.shape[2]), dt)
    od_ex = copies_start("reduce_od_exchange_start", od_grads + [half_zone(a, 4, a.dtype) for a in od_grads],
                         2, plan_exchange, dxs1)

    dy0, dgate0, dpost0 = post_bwd(dxs1, y0, post_g[0:1] + od_ex[3][0:1, 0:1], gate[0], cx)
    du = mm_nt(dy0[None], None, ev_out3, tn, "mm_ev_out_dx")
    g_ev_out = mm_tn(u, dy0[None], None, d, tn, "mm_ev_out_dw")
    od_got, _ = copies_wait("reduce_od_exchange_wait", od_ex, plan_exchange, g_ev_out)
    od_sums = [pair_sum(od_got[i], od_got[2 + i], chip_core_arr) for i in range(2)]
    od_sc = copies_start("reduce_od_scatter_start",
                         [sb for _, sb in od_sums] + [half_zone(a, 3, BF16) for a in od_grads],
                         6, plan_scatter, du)
    dz0a, g_pool_w, dpool_scale = mix_a_bwd(z0, du, pool_w_full, ev_pool_scale + od_sc[3][0:1, 0:1], cx)
    dz0b, dconv_w, dconv_b = mix_b_bwd(z0, du, conv_w_full, ev_conv_b + od_sc[3][0:1, 0:1], cx)
    g_ev_in = mm_tn(h0, dz0a, dz0b, ev_in_g.shape[2], tn, "mm_ev_in_dw")
    ev_grads = [g_ev_in, g_ev_out.reshape(4, d // 4, d), g_pool_w.reshape(4, g, g)]
    ev_ex = copies_start("reduce_ev_exchange_start", ev_grads + [half_zone(a, 4, a.dtype) for a in ev_grads],
                         3, plan_exchange, dpool_scale)
    dh0 = mm_nt(dz0a, dz0b, ev_in_g, tn, "mm_ev_in_dx")
    dxs0, dpre0, dshift0, dscale0 = normmod_bwd(xs, dh0, pre_g[0:1] + ev_ex[3][0:1, 0:1], scale[0], dxs1,
                                                cx, False, True)
    grad_x = dxs0[None]
    ev_got, _ = copies_wait("reduce_ev_exchange_wait", ev_ex, plan_exchange, dxs0)
    ev_sums = [pair_sum(ev_got[i], ev_got[3 + i], chip_core_arr) for i in range(3)]
    od_recv, _ = copies_wait("reduce_od_scatter_wait", od_sc, plan_scatter, dxs0)

    zrow = jnp.zeros((1, d), F32)
    small_rows = jnp.concatenate([
        dpre0, dpre1, dpost0, dpost1,
        jnp.concatenate([dpool_scale, dconv_b], axis=1),
        jnp.concatenate([dconv_w.reshape(1, 3 * half_d), jnp.zeros((1, half_d), F32)], axis=1).reshape(2, d),
        donorm, dlb,
        dshift0[1:2], dscale0[1:2], dgate0[1:2], dshift0[0:1], dscale0[0:1], dgate0[0:1],
        dshift1[1:2], dscale1[1:2], dgate1[1:2], dshift1[0:1], dscale1[0:1], zrow,
        jnp.concatenate([sq[0:1], jnp.zeros((1, d - 128), F32)], axis=1),
        zrow], axis=0)
    small_all = allgather8(small_rows, "allgather_small")
    ev_sc = copies_start("reduce_ev_scatter_start",
                         [sb for _, sb in ev_sums] + [half_zone(a, 3, BF16) for a in ev_grads],
                         9, plan_scatter, small_all)
    od_sh = copies_start("reduce_od_share_start",
                         [owner_sum(od_sums[i][0], od_recv[2 + i], chip_core_arr) for i in range(2)],
                         2, plan_share, dxs0)
    tot = small_reduce(small_all + ev_sc[3][0:1, 0:1])
    loss = tot[22, 0] * (0.5 / d)

    dm_rows = []
    for layer in range(2):
        lat = ROW_MOD + 6 * layer
        dm_lat = small_all[:, lat:lat + 3].reshape(8, 3 * d)
        dm_ctx = tot[lat + 3:lat + 6].reshape(1, 3 * d)
        dm_rows.append(jnp.concatenate([dm_lat, dm_ctx, jnp.zeros((7, 3 * d), F32)], axis=0))
    dm_full = jnp.stack(dm_rows)
    dm_mine = lax.dynamic_slice(dm_full, (0, 0, chip * ws_ada), (2, 16, ws_ada))

    def step(w, gr, m, v, name, with_grad=False):
        shape = w.shape
        cols = shape[-1]
        two_d = lambda a: a.reshape(-1, cols)
        res = adamw(two_d(w), two_d(gr), two_d(m), two_d(v), "adamw_" + name, with_grad)
        return tuple(a.reshape(shape) for a in res)

    grad_ada_b = tot[24:30].reshape(2, 3 * d)
    grad_pre_g = tot[0:2]
    grad_post_g = tot[2:4]
    grad_ev_pool_scale = tot[4:5, :half_d]
    grad_ev_conv_b = tot[4:5, half_d:]
    conv_w_tot = tot[5:7].reshape(1, 2 * d)[:, :3 * half_d].reshape(3, N_POOL, g)
    grad_ev_conv_w = lax.dynamic_slice(conv_w_tot, (0, chip, 0), (3, 1, g)).reshape(1, 3, g)
    grad_od_onorm_g = lax.dynamic_slice(tot[7:8], (0, chip * 2 * g), (1, 2 * g))
    dlb_mine = lax.dynamic_slice(tot[8:10], (0, chip * 2 * g), (2, 2 * g))
    grad_lb_logits = lb_logits_grad(lb_logits, dlb_mine)
    upd = {
        "ada_b": step(ada_b, grad_ada_b, m_ada_b, v_ada_b, "ada_b"),
        "pre_g": step(pre_g, grad_pre_g, m_pre_g, v_pre_g, "pre_g"),
        "post_g": step(post_g, grad_post_g, m_post_g, v_post_g, "post_g"),
        "ev_pool_scale": step(ev_pool_scale, grad_ev_pool_scale, m_ev_pool_scale, v_ev_pool_scale, "ev_pool_scale"),
        "ev_conv_w": step(ev_conv_w, grad_ev_conv_w, m_ev_conv_w, v_ev_conv_w, "ev_conv_w"),
        "ev_conv_b": step(ev_conv_b, grad_ev_conv_b, m_ev_conv_b, v_ev_conv_b, "ev_conv_b"),
        "od_onorm_g": step(od_onorm_g, grad_od_onorm_g, m_od_onorm_g, v_od_onorm_g, "od_onorm_g"),
        "lb_logits": step(lb_logits, grad_lb_logits, m_lb_logits, v_lb_logits, "lb_logits"),
    }
    grad_ada_w, delta_ada_w, new_m_ada_w, new_v_ada_w, dctx_part = ada_bwd_adamw(
        s_act, dm_mine, ada_w, m_ada_w, v_ada_w)
    upd["ada_w"] = (delta_ada_w, new_m_ada_w, new_v_ada_w)
    (grad_od_w_in, grad_od_w_out), _ = copies_wait("reduce_od_share_wait", od_sh, plan_share, ev_sc[3])
    upd["od_w_in"] = step(od_w_in, grad_od_w_in[None], m_od_w_in, v_od_w_in, "od_w_in", True)
    upd["od_w_out"] = step(od_w_out, grad_od_w_out[None], m_od_w_out, v_od_w_out, "od_w_out", True)
    grad_od_w_in, grad_od_w_out = upd["od_w_in"][3], upd["od_w_out"][3]
    done_behind = [dctx_part] + [upd[k][0] for k in (
        "od_w_in", "od_w_out", "ada_b", "pre_g", "post_g", "ev_pool_scale", "ev_conv_w", "ev_conv_b",
        "od_onorm_g", "lb_logits")]
    ev_recv, ev_landed = copies_wait("reduce_ev_scatter_wait", ev_sc, plan_scatter, done_behind)
    grad_ev_w_in, grad_ev_w_out, grad_pool_w = share_halves(
        [owner_sum(ev_sums[i][0], ev_recv[3 + i], chip_core_arr) for i in range(3)])
    dctx_all = allgather8(dctx_part[0] + dctx_part[1] + ev_landed[0:1, 0:1], "allgather_dctx")
    grad_c_ctx = c_ctx_grad(dctx_all, c_ctx.reshape(1, d)).reshape(d)
    upd["c_ctx"] = step(c_ctx, grad_c_ctx, m_c_ctx, v_c_ctx, "c_ctx")
    upd["ev_w_in"] = step(ev_w_in, grad_ev_w_in[None], m_ev_w_in, v_ev_w_in, "ev_w_in", True)
    upd["ev_pool_w"] = step(ev_pool_w, grad_pool_w.reshape(1, N_POOL, g // 4, g), m_ev_pool_w, v_ev_pool_w,
                            "ev_pool_w", True)
    upd["ev_w_out"] = step(ev_w_out, grad_ev_w_out[None], m_ev_w_out, v_ev_w_out, "ev_w_out", True)
    grad_ev_w_in, grad_ev_pool_w, grad_ev_w_out = upd["ev_w_in"][3], upd["ev_pool_w"][3], upd["ev_w_out"][3]
    names = ["c_ctx", "ada_w", "ada_b", "pre_g", "post_g", "ev_w_in", "ev_pool_w", "ev_pool_scale",
             "ev_conv_w", "ev_conv_b", "ev_w_out", "od_w_in", "od_onorm_g", "od_w_out", "lb_logits"]
    grads = [grad_c_ctx, grad_ada_w, grad_ada_b, grad_pre_g, grad_post_g, grad_ev_w_in, grad_ev_pool_w,
             grad_ev_pool_scale, grad_ev_conv_w, grad_ev_conv_b, grad_ev_w_out, grad_od_w_in,
             grad_od_onorm_g, grad_od_w_out, grad_lb_logits]
    return (loss, grad_x, *grads, *[upd[k][0] for k in names], *[upd[k][1] for k in names],
            *[upd[k][2] for k in names])
```

```python
import jax
import jax.numpy as jnp
from jax import lax
from jax.experimental import pallas as pl
from jax.experimental.pallas import tpu as pltpu

EPS = 1e-6
GRID_W_LOG2 = 6
CHUNK = 64
HEAD = 128
N_POOL = 4
ADAM_LR, ADAM_B1, ADAM_B2, ADAM_EPS, ADAM_WD, ADAM_STEP = 0.001, 0.9, 0.999, 1e-08, 0.01, 10
VMEM_LIMIT = 56 * 1024 * 1024
MESH = pl.DeviceIdType.MESH
F32, BF16 = jnp.float32, jnp.bfloat16
ANY = pl.BlockSpec(memory_space=pl.ANY)
VMEM = pl.BlockSpec(memory_space=pltpu.VMEM)


def _cp(**kw):
    return pltpu.CompilerParams(vmem_limit_bytes=VMEM_LIMIT, **kw)


def _silu(x):
    return x * jax.nn.sigmoid(x)


def _dsilu(x):
    s = jax.nn.sigmoid(x)
    return s * (1.0 + x * (1.0 - s))


def _dot(a, b, dims=((1,), (0,)), precision=None):
    return lax.dot_general(a, b, (dims, ((), ())), preferred_element_type=F32, precision=precision)


NN = ((1,), (0,))
NT = ((1,), (1,))
TN = ((0,), (0,))


def _row_block(cx):
    return 256 if cx % 256 == 0 else 128


def normmod_fwd_joining(ctx, x, g, shift, scale):
    cx, d = ctx.shape
    t = cx + x.shape[0]
    tm = _row_block(cx)
    nctx = cx // tm

    def body(c_ref, x_ref, g_ref, sh_ref, sc_ref, h_ref, xs_ref):
        is_ctx = pl.program_id(0) < nctx
        x = jnp.where(is_ctx, c_ref[...], x_ref[...])
        xs_ref[...] = x
        rstd = lax.rsqrt(jnp.mean(x * x, axis=-1, keepdims=True) + EPS)
        sc = jnp.where(is_ctx, sc_ref[0:1, :], sc_ref[1:2, :])
        sh = jnp.where(is_ctx, sh_ref[0:1, :], sh_ref[1:2, :])
        h_ref[...] = ((x * rstd) * g_ref[...] * (1.0 + sc) + sh).astype(BF16)

    row = pl.BlockSpec((tm, d), lambda i: (i, 0))
    vec = lambda r: pl.BlockSpec((r, d), lambda i: (0, 0))
    return pl.pallas_call(
        body, name="normmod_fwd_joining", grid=(t // tm,),
        in_specs=[pl.BlockSpec((tm, d), lambda i: (jnp.minimum(i, nctx - 1), 0)),
                  pl.BlockSpec((tm, d), lambda i: (jnp.maximum(i - nctx, 0), 0)), vec(1), vec(2), vec(2)],
        out_specs=[row, row],
        out_shape=[jax.ShapeDtypeStruct((t, d), BF16), jax.ShapeDtypeStruct((t, d), F32)],
        compiler_params=_cp(),
    )(ctx, x, g, shift, scale)


def normmod_bwd(xs, dh, g, scale, dres, cx, res_is_latent_only, dx_latent_only=False, prev=None):
    t, d = xs.shape
    tm = _row_block(cx)
    nctx = cx // tm
    n_prev = 0 if prev is None else 3

    def body(x_ref, dh_ref, g_ref, sc_ref, dres_ref, *rest):
        dx_ref, dg_ref, dsh_ref, dsc_ref = rest[n_prev:n_prev + 4]
        i = pl.program_id(0)
        is_ctx = i < nctx

        @pl.when(i == 0)
        def _():
            for ref in rest[n_prev + 1:n_prev + 4] + rest[n_prev + 5:]:
                ref[...] = jnp.zeros_like(ref)

        x = x_ref[...]
        dh = dh_ref[...]
        gv = g_ref[...]
        rstd = lax.rsqrt(jnp.mean(x * x, axis=-1, keepdims=True) + EPS)
        xhat = x * rstd
        sc = jnp.where(is_ctx, sc_ref[0:1, :], sc_ref[1:2, :])
        dsh = jnp.sum(dh, axis=0, keepdims=True)
        dhx = dh * xhat
        dsc = jnp.sum(dhx * gv, axis=0, keepdims=True)
        dg_ref[...] += jnp.sum(dhx * (1.0 + sc), axis=0, keepdims=True)
        zero = jnp.zeros_like(dsh)
        dsh_ref[0:1, :] += jnp.where(is_ctx, dsh, zero)
        dsh_ref[1:2, :] += jnp.where(is_ctx, zero, dsh)
        dsc_ref[0:1, :] += jnp.where(is_ctx, dsc, zero)
        dsc_ref[1:2, :] += jnp.where(is_ctx, zero, dsc)
        dxhat = dh * (gv * (1.0 + sc))
        dx = rstd * (dxhat - xhat * jnp.mean(dxhat * xhat, axis=-1, keepdims=True))
        res = dres_ref[...]
        if res_is_latent_only:
            res = jnp.where(is_ctx, jnp.zeros_like(res), res)
        dxt = dx + res
        dx_ref[...] = dxt
        if prev is not None:
            y_ref, pg_ref, gate_ref = rest[:3]
            dy_ref, dgate_ref, dpg_ref = rest[7:]
            y = y_ref[...]
            pgv = pg_ref[...]
            rstd_y = lax.rsqrt(jnp.mean(y * y, axis=-1, keepdims=True) + EPS)
            yhat = y * rstd_y
            gt = jnp.where(is_ctx, gate_ref[0:1, :], gate_ref[1:2, :])
            dxy = dxt * yhat
            dgt = jnp.sum(dxy * pgv, axis=0, keepdims=True)
            dgate_ref[0:1, :] += jnp.where(is_ctx, dgt, zero)
            dgate_ref[1:2, :] += jnp.where(is_ctx, zero, dgt)
            dpg_ref[...] += jnp.sum(dxy * gt, axis=0, keepdims=True)
            dyhat = dxt * (gt * pgv)
            dy_ref[...] = (rstd_y * (dyhat - yhat * jnp.mean(dyhat * yhat, axis=-1, keepdims=True))).astype(BF16)

    row = pl.BlockSpec((tm, d), lambda i: (i, 0))
    if res_is_latent_only:
        res_spec = pl.BlockSpec((tm, d), lambda i: (jnp.maximum(i - nctx, 0), 0))
    else:
        res_spec = row
    vec = lambda r: pl.BlockSpec((r, d), lambda i: (0, 0))
    dx_spec = pl.BlockSpec((tm, d), lambda i: (jnp.maximum(i - nctx, 0), 0)) if dx_latent_only else row
    in_specs = [row, row, vec(1), vec(2), res_spec]
    out_specs = [dx_spec, vec(1), vec(2), vec(2)]
    out_shape = [jax.ShapeDtypeStruct((t - cx if dx_latent_only else t, d), F32), jax.ShapeDtypeStruct((1, d), F32),
                 jax.ShapeDtypeStruct((2, d), F32), jax.ShapeDtypeStruct((2, d), F32)]
    if prev is not None:
        in_specs += [row, vec(1), vec(2)]
        out_specs += [row, vec(2), vec(1)]
        out_shape += [jax.ShapeDtypeStruct((t, d), BF16), jax.ShapeDtypeStruct((2, d), F32),
                      jax.ShapeDtypeStruct((1, d), F32)]
    return pl.pallas_call(
        body, name="normmod_bwd", grid=(t // tm,), in_specs=in_specs, out_specs=out_specs, out_shape=out_shape,
        compiler_params=_cp(),
    )(xs, dh, g, scale, dres, *(() if prev is None else prev))


def post_fwd_norm(xs, y, pg, gate, g_next, shift_next, scale_next, cx):
    t, d = xs.shape
    tm = _row_block(cx)
    nctx = cx // tm

    def body(x_ref, y_ref, pg_ref, gate_ref, g_ref, sh_ref, sc_ref, o_ref, h_ref):
        is_ctx = pl.program_id(0) < nctx
        pick = lambda ref: jnp.where(is_ctx, ref[0:1, :], ref[1:2, :])
        y = y_ref[...]
        rstd = lax.rsqrt(jnp.mean(y * y, axis=-1, keepdims=True) + EPS)
        x = x_ref[...] + pick(gate_ref) * ((y * rstd) * pg_ref[...])
        o_ref[...] = x
        rstd = lax.rsqrt(jnp.mean(x * x, axis=-1, keepdims=True) + EPS)
        h_ref[...] = ((x * rstd) * g_ref[...] * (1.0 + pick(sc_ref)) + pick(sh_ref)).astype(BF16)

    row = pl.BlockSpec((tm, d), lambda i: (i, 0))
    vec = lambda r: pl.BlockSpec((r, d), lambda i: (0, 0))
    return pl.pallas_call(
        body, name="post_fwd_norm", grid=(t // tm,),
        in_specs=[row, row, vec(1), vec(2), vec(1), vec(2), vec(2)], out_specs=[row, row],
        out_shape=[jax.ShapeDtypeStruct((t, d), F32), jax.ShapeDtypeStruct((t, d), BF16)],
        compiler_params=_cp(),
    )(xs, y, pg, gate, g_next, shift_next, scale_next)


def post_loss(xs, y, pg, gate, target, cx):
    t, d = xs.shape
    n = y.shape[0]
    tm = _row_block(cx)
    nctx = cx // tm

    def body(x_ref, y_ref, pg_ref, gate_ref, tgt_ref, sq_ref, dx_ref, dy_ref, dgate_ref, dpg_ref):
        @pl.when(pl.program_id(0) == 0)
        def _():
            sq_ref[...] = jnp.zeros_like(sq_ref)
            dgate_ref[...] = jnp.zeros_like(dgate_ref)
            dpg_ref[...] = jnp.zeros_like(dpg_ref)

        y = y_ref[...]
        pgv = pg_ref[...]
        gt = gate_ref[1:2, :]
        rstd = lax.rsqrt(jnp.mean(y * y, axis=-1, keepdims=True) + EPS)
        yhat = y * rstd
        err = x_ref[...] + gt * (yhat * pgv) - tgt_ref[...]
        sq_ref[...] += jnp.sum(err * err)
        dx = err * (1.0 / d)
        dx_ref[...] = dx
        dxy = dx * yhat
        dgate_ref[1:2, :] += jnp.sum(dxy * pgv, axis=0, keepdims=True)
        dpg_ref[...] += jnp.sum(dxy * gt, axis=0, keepdims=True)
        dyhat = dx * (gt * pgv)
        dy_ref[...] = (rstd * (dyhat - yhat * jnp.mean(dyhat * yhat, axis=-1, keepdims=True))).astype(BF16)

    row = pl.BlockSpec((tm, d), lambda i: (i, 0))
    xrow = pl.BlockSpec((tm, d), lambda i: (i + nctx, 0))
    vec = lambda r: pl.BlockSpec((r, d), lambda i: (0, 0))
    return pl.pallas_call(
        body, name="post_loss", grid=(n // tm,),
        in_specs=[xrow, row, vec(1), vec(2), row],
        out_specs=[pl.BlockSpec((8, 128), lambda i: (0, 0)), row, row, vec(2), vec(1)],
        out_shape=[jax.ShapeDtypeStruct((8, 128), F32), jax.ShapeDtypeStruct((n, d), F32),
                   jax.ShapeDtypeStruct((n, d), BF16), jax.ShapeDtypeStruct((2, d), F32),
                   jax.ShapeDtypeStruct((1, d), F32)],
        compiler_params=_cp(),
    )(xs, y, pg, gate, target)


def _split_rows(m):
    for cand in (1152, 1024, 768, 512, 384, 256, 128):
        if m % cand == 0 and m // cand >= 2:
            return cand
    return m


def mm_nn(a, w3, sec, tn, name):
    m, k = a.shape
    q, _, ws = w3.shape
    n = q * ws
    tpq, tps = ws // tn, sec // tn
    tm = next(c for c in (768, 512, 256, 128) if m % c == 0)

    def body(a_ref, w_ref, o_ref):
        w = w_ref[...]

        def step(i, carry):
            rows = pl.ds(pl.multiple_of(i * tm, tm), tm)
            o_ref[rows, :] = _dot(a_ref[rows, :], w)
            return carry

        lax.fori_loop(0, m // tm, step, 0)

    return pl.pallas_call(
        body, name=name, grid=(n // tn,),
        in_specs=[pl.BlockSpec((m, k), lambda j: (0, 0)),
                  pl.BlockSpec((None, k, tn), lambda j: (j // tpq, 0, j % tpq))],
        out_specs=pl.BlockSpec((None, m, tn), lambda j: (j // tps, 0, j % tps)),
        out_shape=jax.ShapeDtypeStruct((n // sec, m, sec), F32), compiler_params=_cp(),
    )(a, w3)


def _two_stacks(a3, b3, tn):
    sec = a3.shape[2]
    tps = sec // tn
    n1 = a3.shape[0] * tps
    first = lambda j: (jnp.minimum(j, n1 - 1) // tps, jnp.minimum(j, n1 - 1) % tps)
    second = lambda j: (jnp.maximum(j - n1, 0) // tps, jnp.maximum(j - n1, 0) % tps)
    return n1, first, second


def mm_nt(a3, b3, w3, tn, name):
    if b3 is None:
        b3 = a3
    _, m, sec = a3.shape
    q, k, ws = w3.shape
    n = q * ws
    tpq = ws // tn
    mb = _split_rows(m)
    n1, first, second = _two_stacks(a3, b3, tn)

    def body(a_ref, b_ref, w_ref, o_ref):
        j = pl.program_id(1)

        @pl.when(j == 0)
        def _():
            o_ref[...] = jnp.zeros_like(o_ref)

        @pl.when(j < n1)
        def _():
            o_ref[...] += _dot(a_ref[...], w_ref[...], NT)

        @pl.when(j >= n1)
        def _():
            o_ref[...] += _dot(b_ref[...], w_ref[...], NT)

    return pl.pallas_call(
        body, name=name, grid=(m // mb, n // tn),
        in_specs=[pl.BlockSpec((None, mb, tn), lambda i, j: (first(j)[0], i, first(j)[1])),
                  pl.BlockSpec((None, mb, tn), lambda i, j: (second(j)[0], i, second(j)[1])),
                  pl.BlockSpec((None, k, tn), lambda i, j: (j // tpq, 0, j % tpq))],
        out_specs=pl.BlockSpec((mb, k), lambda i, j: (i, 0)),
        out_shape=jax.ShapeDtypeStruct((m, k), F32), compiler_params=_cp(),
    )(a3, b3, w3)


def mm_tn(a, b3, c3, ws, tn, name):
    m, k = a.shape
    sec = b3.shape[2]
    n = (b3.shape[0] + (0 if c3 is None else c3.shape[0])) * sec
    if c3 is None:
        c3 = b3
    tpq = ws // tn
    kb = 256 if k % 256 == 0 else 128
    n1, first, second = _two_stacks(b3, c3, tn)

    def body(a_ref, b_ref, c_ref, o_ref):
        def product(rhs_ref):
            rhs = rhs_ref[...]
            for i in range(k // kb):
                o_ref[i * kb:(i + 1) * kb, :] = _dot(a_ref[:, i * kb:(i + 1) * kb], rhs, TN).astype(BF16)

        @pl.when(pl.program_id(0) < n1)
        def _():
            product(b_ref)

        @pl.when(pl.program_id(0) >= n1)
        def _():
            product(c_ref)

    return pl.pallas_call(
        body, name=name, grid=(n // tn,),
        in_specs=[pl.BlockSpec((m, k), lambda j: (0, 0)),
                  pl.BlockSpec((None, m, tn), lambda j: (first(j)[0], 0, first(j)[1])),
                  pl.BlockSpec((None, m, tn), lambda j: (second(j)[0], 0, second(j)[1]))],
        out_specs=pl.BlockSpec((None, k, tn), lambda j: (j // tpq, 0, j % tpq)),
        out_shape=jax.ShapeDtypeStruct((n // ws, k, ws), BF16), compiler_params=_cp(),
    )(a, b3, c3)


POOL_REACH = 8 << GRID_W_LOG2


def _token_parts(tok, cx):
    lat = tok - cx
    return tok < cx, lat >> GRID_W_LOG2, lat & ((1 << GRID_W_LOG2) - 1)


def _pool_mask(gi, row0, col0, tm, ncols, cx, transposed):
    half = jnp.left_shift(1, gi)
    r = lax.broadcasted_iota(jnp.int32, (tm, 1), 0) + row0
    c = lax.broadcasted_iota(jnp.int32, (1, ncols), 1) + col0
    out_tok, src_tok = (c, r) if transposed else (r, c)
    o_ctx, o_row, o_col = _token_parts(out_tok, cx)
    s_ctx, s_row, s_col = _token_parts(src_tok, cx)

    def inside(o, s):
        return (s >= o - half) & (s <= o + half - 1)

    ctx_hit = o_ctx & s_ctx & inside(out_tok, src_tok)
    lat_hit = (~o_ctx) & (~s_ctx) & inside(o_row, s_row) & inside(o_col, s_col)
    return jnp.where(ctx_hit | lat_hit, 1.0, 0.0).astype(BF16)


def _pool_inv_count(gi, row0, tm, cx, seq):
    half = jnp.left_shift(1, gi)
    r = lax.broadcasted_iota(jnp.int32, (tm, 1), 0) + row0
    is_ctx, row, col = _token_parts(r, cx)

    def count(pos, size):
        return jnp.minimum(pos + half - 1, size - 1) - jnp.maximum(pos - half, 0) + 1

    cnt = jnp.where(is_ctx, count(r, cx), count(row, seq >> GRID_W_LOG2) * count(col, 1 << GRID_W_LOG2))
    return 1.0 / cnt.astype(F32)


def _lat_band(tm):
    side = POOL_REACH // tm
    return side, 2 * side + 1


def _lat_mask(gi, tm, cx, transposed):
    side, band = _lat_band(tm)
    return _pool_mask(gi, cx + side * tm, cx, tm, band * tm, cx, transposed)


def _store_padded_lat(dst_ref, lat, tm):
    side, _ = _lat_band(tm)
    seq = lat.shape[0]
    zeros = jnp.zeros((side * tm, lat.shape[1]), dst_ref.dtype)
    dst_ref[0:side * tm, :] = zeros
    dst_ref[side * tm + seq:, :] = zeros
    dst_ref[side * tm:side * tm + seq, :] = lat.astype(dst_ref.dtype)


def mix_a_fwd(z0, pool_w, pool_scale, cx):
    _, t, half_d = z0.shape
    g = half_d // N_POOL
    seq = t - cx
    tm = _row_block(cx)
    side, band = _lat_band(tm)

    def body(v_ref, ag_ref, w_ref, sc_ref, u_ref, vlat_ref, mask_ref):
        gi = pl.program_id(0)
        w = w_ref[...].astype(BF16)
        sc = sc_ref[...]
        _store_padded_lat(vlat_ref, v_ref[cx:, :], tm)
        mask_ref[...] = _lat_mask(gi, tm, cx, False)

        def finish(row0, window_sum):
            rows = pl.ds(row0, tm)
            pooled = window_sum * _pool_inv_count(gi, row0, tm, cx, seq) - v_ref[rows, :]
            mixed = _dot(pooled.astype(BF16), w) * sc
            u_ref[rows, :] = (mixed * _silu(ag_ref[rows, :])).astype(BF16)

        vctx = v_ref[0:cx, :].astype(BF16)
        for i in range(cx // tm):
            finish(i * tm, _dot(_pool_mask(gi, i * tm, 0, tm, cx, cx, False), vctx))

        def step(j, carry):
            src = vlat_ref[pl.ds(pl.multiple_of(j * tm, tm), band * tm), :]
            finish(pl.multiple_of(cx + j * tm, tm), _dot(mask_ref[...], src))
            return carry

        lax.fori_loop(0, seq // tm, step, 0)

    sec = lambda s: pl.BlockSpec((None, t, g), lambda j: (s, 0, j))
    return pl.pallas_call(
        body, name="mix_a_fwd", grid=(N_POOL,),
        in_specs=[sec(0), sec(1), pl.BlockSpec((None, g, g), lambda j: (j, 0, 0)),
                  pl.BlockSpec((1, g), lambda j: (0, j))],
        out_specs=pl.BlockSpec((t, g), lambda j: (0, j)),
        out_shape=jax.ShapeDtypeStruct((t, 2 * half_d), BF16),
        scratch_shapes=[pltpu.VMEM((seq + 2 * side * tm, g), BF16), pltpu.VMEM((tm, band * tm), BF16)],
        compiler_params=_cp(),
    )(z0, z0, pool_w, pool_scale)


def mix_a_bwd(z0, du, pool_w, pool_scale, cx):
    _, t, half_d = z0.shape
    g = half_d // N_POOL
    seq = t - cx
    tm = _row_block(cx)
    gq = g // 4
    side, band = _lat_band(tm)

    def body(v_ref, ag_ref, du_ref, w_ref, sc_ref, dz_ref, dw_ref, dsc_ref,
             vlat_ref, mask_ref, pooled_ref, dmx_ref, dpl_ref, wlat_ref, wctx_ref):
        gi = pl.program_id(0)
        w = w_ref[...].astype(BF16)
        sc = sc_ref[...]
        _store_padded_lat(vlat_ref, v_ref[cx:, :], tm)
        _store_padded_lat(wlat_ref, jnp.zeros((seq, g), BF16), tm)
        mask_ref[...] = _lat_mask(gi, tm, cx, False)

        def first(row0, window_sum, weighted_ref, weighted_row0):
            rows = pl.ds(row0, tm)
            inv = _pool_inv_count(gi, row0, tm, cx, seq)
            pooled = (window_sum * inv - v_ref[rows, :]).astype(BF16)
            pooled_ref[rows, :] = pooled
            mixed = _dot(pooled, w)
            ag = ag_ref[rows, :]
            duv = du_ref[rows, :]
            dz_ref[1, rows, :] = (duv * (mixed * sc) * _dsilu(ag)).astype(BF16)
            dms = duv * _silu(ag)
            dmixed = (dms * sc).astype(BF16)
            dmx_ref[rows, :] = dmixed
            dpooled = _dot(dmixed, w, NT)
            dpl_ref[rows, :] = dpooled
            weighted_ref[pl.ds(weighted_row0, tm), :] = (dpooled * inv).astype(BF16)
            return jnp.sum(dms * mixed, axis=0, keepdims=True)

        dsc = jnp.zeros((1, g), F32)
        vctx = v_ref[0:cx, :].astype(BF16)
        for i in range(cx // tm):
            dsc += first(i * tm, _dot(_pool_mask(gi, i * tm, 0, tm, cx, cx, False), vctx), wctx_ref, i * tm)

        def first_lat(j, acc):
            src = vlat_ref[pl.ds(pl.multiple_of(j * tm, tm), band * tm), :]
            return acc + first(pl.multiple_of(cx + j * tm, tm), _dot(mask_ref[...], src),
                               wlat_ref, pl.multiple_of((side + j) * tm, tm))

        dsc_ref[...] = lax.fori_loop(0, seq // tm, first_lat, dsc)
        dw = _dot(pooled_ref[...], dmx_ref[...], TN)
        for qi in range(4):
            dw_ref[qi] = dw[qi * gq:(qi + 1) * gq, :]

        wctx = wctx_ref[...]
        for i in range(cx // tm):
            rows = pl.ds(i * tm, tm)
            dz_ref[0, rows, :] = (_dot(_pool_mask(gi, i * tm, 0, tm, cx, cx, True), wctx)
                                  - dpl_ref[rows, :]).astype(BF16)
        mask_ref[...] = _lat_mask(gi, tm, cx, True)

        def second_lat(j, carry):
            rows = pl.ds(pl.multiple_of(cx + j * tm, tm), tm)
            src = wlat_ref[pl.ds(pl.multiple_of(j * tm, tm), band * tm), :]
            dz_ref[0, rows, :] = (_dot(mask_ref[...], src) - dpl_ref[rows, :]).astype(BF16)
            return carry

        lax.fori_loop(0, seq // tm, second_lat, 0)

    sec = lambda s: pl.BlockSpec((None, t, g), lambda j: (s, 0, j))
    padded = pltpu.VMEM((seq + 2 * side * tm, g), BF16)
    return pl.pallas_call(
        body, name="mix_a_bwd", grid=(N_POOL,),
        in_specs=[sec(0), sec(1), pl.BlockSpec((t, g), lambda j: (0, j)),
                  pl.BlockSpec((None, g, g), lambda j: (j, 0, 0)),
                  pl.BlockSpec((1, g), lambda j: (0, j))],
        out_specs=[pl.BlockSpec((2, t, g), lambda j: (0, 0, j)),
                   pl.BlockSpec((4, None, gq, g), lambda j: (0, j, 0, 0)),
                   pl.BlockSpec((1, g), lambda j: (0, j))],
        out_shape=[jax.ShapeDtypeStruct((2, t, half_d), BF16),
                   jax.ShapeDtypeStruct((4, N_POOL, gq, g), F32),
                   jax.ShapeDtypeStruct((1, half_d), F32)],
        scratch_shapes=[padded, pltpu.VMEM((tm, band * tm), BF16), pltpu.VMEM((t, g), BF16),
                        pltpu.VMEM((t, g), BF16), pltpu.VMEM((t, g), F32), padded, pltpu.VMEM((cx, g), BF16)],
        compiler_params=_cp(),
    )(z0, z0, du, pool_w, pool_scale)


def _conv_masks(t, cx):
    r = lax.broadcasted_iota(jnp.int32, (t, 1), 0)
    has_prev = jnp.where((r == 0) | (r == cx), 0.0, 1.0)
    has_next = jnp.where((r == cx - 1) | (r == t - 1), 0.0, 1.0)
    return has_prev, has_next


def mix_b_fwd(z0, conv_w, conv_b, u, cx):
    _, t, half_d = z0.shape
    gb = 128
    off = half_d // gb

    def body(bx_ref, bb_ref, bc_ref, bg_ref, w_ref, b_ref, _, u_ref):
        has_prev, has_next = _conv_masks(t, cx)
        tt = bc_ref[...] * bx_ref[...]
        prev = pltpu.roll(tt, 1, 0) * has_prev
        nxt = pltpu.roll(tt, t - 1, 0) * has_next
        cv = prev * w_ref[0:1, :] + tt * w_ref[1:2, :] + nxt * w_ref[2:3, :] + b_ref[...]
        u_ref[...] = (bb_ref[...] * cv * _silu(bg_ref[...])).astype(BF16)

    sec = lambda s: pl.BlockSpec((None, t, gb), lambda j: (s, 0, j))
    return pl.pallas_call(
        body, name="mix_b_fwd", grid=(half_d // gb,),
        in_specs=[sec(2), sec(3), sec(4), sec(5), pl.BlockSpec((3, gb), lambda j: (0, j)),
                  pl.BlockSpec((1, gb), lambda j: (0, j)), ANY],
        out_specs=pl.BlockSpec((t, gb), lambda j: (0, j + off)),
        out_shape=jax.ShapeDtypeStruct((t, 2 * half_d), BF16), input_output_aliases={6: 0},
        compiler_params=_cp(),
    )(z0, z0, z0, z0, conv_w, conv_b, u)


def mix_b_bwd(z0, du, conv_w, conv_b, cx):
    _, t, half_d = z0.shape
    gb = 128
    off = half_d // gb

    def body(bx_ref, bb_ref, bc_ref, bg_ref, du_ref, w_ref, b_ref, dz_ref, dw_ref, db_ref):
        has_prev, has_next = _conv_masks(t, cx)
        bx, bb, bc, bg = bx_ref[...], bb_ref[...], bc_ref[...], bg_ref[...]
        duv = du_ref[...]
        tt = bc * bx
        prev = pltpu.roll(tt, 1, 0) * has_prev
        nxt = pltpu.roll(tt, t - 1, 0) * has_next
        w0, w1, w2 = w_ref[0:1, :], w_ref[1:2, :], w_ref[2:3, :]
        cv = prev * w0 + tt * w1 + nxt * w2 + b_ref[...]
        sg = _silu(bg)
        dz_ref[1] = (duv * cv * sg).astype(BF16)
        dz_ref[3] = (duv * bb * cv * _dsilu(bg)).astype(BF16)
        dcv = duv * bb * sg
        dw_ref[0:1, :] = jnp.sum(dcv * prev, axis=0, keepdims=True)
        dw_ref[1:2, :] = jnp.sum(dcv * tt, axis=0, keepdims=True)
        dw_ref[2:3, :] = jnp.sum(dcv * nxt, axis=0, keepdims=True)
        db_ref[...] = jnp.sum(dcv, axis=0, keepdims=True)
        dt = (pltpu.roll(dcv * has_prev, t - 1, 0) * w0 + dcv * w1
              + pltpu.roll(dcv * has_next, 1, 0) * w2)
        dz_ref[0] = (dt * bc).astype(BF16)
        dz_ref[2] = (dt * bx).astype(BF16)

    sec = lambda s: pl.BlockSpec((None, t, gb), lambda j: (s, 0, j))
    return pl.pallas_call(
        body, name="mix_b_bwd", grid=(half_d // gb,),
        in_specs=[sec(2), sec(3), sec(4), sec(5), pl.BlockSpec((t, gb), lambda j: (0, j + off)),
                  pl.BlockSpec((3, gb), lambda j: (0, j)), pl.BlockSpec((1, gb), lambda j: (0, j))],
        out_specs=[pl.BlockSpec((4, t, gb), lambda j: (0, 0, j)),
                   pl.BlockSpec((3, gb), lambda j: (0, j)), pl.BlockSpec((1, gb), lambda j: (0, j))],
        out_shape=[jax.ShapeDtypeStruct((4, t, half_d), BF16),
                   jax.ShapeDtypeStruct((3, half_d), F32), jax.ShapeDtypeStruct((1, half_d), F32)],
        compiler_params=_cp(),
    )(z0, z0, z0, z0, du, conv_w, conv_b)


def _lower_bound(lbl_ref, d):
    l0, l1, l2 = lbl_ref[d, 0:1, :], lbl_ref[d, 1:2, :], lbl_ref[d, 2:3, :]
    mx = jnp.maximum(jnp.maximum(l0, l1), l2)
    e0, e1, e2 = jnp.exp(l0 - mx), jnp.exp(l1 - mx), jnp.exp(l2 - mx)
    inv = 1.0 / (e0 + e1 + e2)
    return (e0 + e1) * inv, (e0 * inv, e1 * inv, e2 * inv)


def _chunk_consts(d):
    r = lax.broadcasted_iota(jnp.int32, (CHUNK, CHUNK), 0)
    c = lax.broadcasted_iota(jnp.int32, (CHUNK, CHUNK), 1)
    keep = (c <= r) if d == 0 else (c >= r)
    return jnp.where(keep, 1.0, 0.0).astype(F32), keep


def _chunk_of_step(s, d, nc, ncc):
    if d == 0:
        return s
    return jnp.where(s < ncc, ncc - 1 - s, nc - 1 + ncc - s)


def _gates(z, lbv):
    e = jnp.exp(-jnp.abs(z))
    r = 1.0 / (1.0 + e)
    er = e * r
    pos = z >= 0.0
    sig = jnp.where(pos, r, er)
    nsig = jnp.where(pos, er, r)
    return sig, nsig, lbv + (1.0 - lbv) * sig


def _split3(x):
    hi = x.astype(BF16)
    r1 = x - hi.astype(F32)
    mid = r1.astype(BF16)
    lo = (r1 - mid.astype(F32)).astype(BF16)
    return jnp.concatenate([hi, mid, lo], axis=1)


def _cumsum_chunk(cum, x):
    y = _dot(cum, _split3(x))
    return y[:, :HEAD] + y[:, HEAD:2 * HEAD] + y[:, 2 * HEAD:]


def _chunk_rows(n):
    return pl.ds(pl.multiple_of(n * CHUNK, CHUNK), CHUNK)


def _group(nc, prefer=(4, 3, 2, 1)):
    return next(u for u in prefer if nc % u == 0)


WIDE_GROUP = (18, 12, 6, 4, 3, 2, 1)


def _decay_pass(lf_ref, bc_ref, dec_ref, cum, nc):
    grp = _group(nc, WIDE_GROUP)

    def step(m, carry):
        ns = [m * grp + u for u in range(grp)]
        lfc = [lf_ref[_chunk_rows(n), :] for n in ns]
        bc = [_cumsum_chunk(cum, x) for x in lfc]
        for u, n in enumerate(ns):
            bc_ref[_chunk_rows(n), :] = bc[u]
            dec_ref[n] = jnp.broadcast_to(jnp.exp(jnp.sum(lfc[u], axis=0, keepdims=True)), (8, HEAD))
        return carry

    lax.fori_loop(0, nc // grp, step, 0)


def hgrn_fwd(z1, lbl, onorm, cx):
    _, t, d = z1.shape
    seq = t - cx
    nc, ncc = t // CHUNK, cx // CHUNK

    grp, sgrp = _group(nc, (9, 6, 4, 3, 2, 1)), _group(nc, WIDE_GROUP)

    def body(zf_ref, zb_ref, v_ref, q_ref, g_ref, lbl_ref, on_ref, o_ref, r_ref, bcs_ref, ks_ref, decs_ref,
             lf_ref, k_ref, bc_ref, dec_ref, qd_ref, ki_ref, oacc_ref, ds_ref):
        for dr, z_ref in ((0, zf_ref), (1, zb_ref)):
            lbv, _ = _lower_bound(lbl_ref, dr)
            _, nsig, f = _gates(z_ref[...], lbv)
            lf_ref[...] = jnp.log(f)
            k_ref[...] = (1.0 - lbv) * nsig
            cum, keep = _chunk_consts(dr)
            _decay_pass(lf_ref, bc_ref, dec_ref, cum.astype(BF16), nc)
            bc = bc_ref[...]
            bcs_ref[dr] = bc
            ks_ref[dr] = k_ref[...]
            decs_ref[dr] = dec_ref[...]
            qd_ref[...] = (q_ref[...] * jnp.exp(bc)).astype(BF16)
            ki_ref[...] = (k_ref[...] * jnp.exp(-bc)).astype(BF16)

            def local_step(m, carry, dr=dr, keep=keep):
                ns = [m * grp + u for u in range(grp)]
                rows = [_chunk_rows(n) for n in ns]
                qd = [qd_ref[r, :] for r in rows]
                ki = [ki_ref[r, :] for r in rows]
                vc = [v_ref[r, :].astype(BF16) for r in rows]
                sc = [_dot(qd[u], ki[u], NT) for u in range(grp)]
                inc = [_dot(vc[u], ki[u], TN) for u in range(grp)]
                a = [jnp.where(keep, s, 0.0).astype(BF16) for s in sc]
                intra = [_dot(a[u], vc[u]) for u in range(grp)]
                for u in range(grp):
                    ds_ref[ns[u]] = inc[u] * dec_ref[ns[u]][0:1, :]
                    if dr == 0:
                        oacc_ref[rows[u], :] = intra[u]
                    else:
                        oacc_ref[rows[u], :] += intra[u]
                return carry

            lax.fori_loop(0, nc // grp, local_step, 0)

            def state_step(m, st, dr=dr):
                ns = [_chunk_of_step(m * sgrp + u, dr, nc, ncc) for u in range(sgrp)]
                rows = [_chunk_rows(n) for n in ns]
                sts = []
                for n in ns:
                    sts.append(st.astype(BF16))
                    st = st * dec_ref[n][0:1, :] + ds_ref[n]
                inter = [_dot(qd_ref[rows[u], :], sts[u], NT) for u in range(sgrp)]
                for u in range(sgrp):
                    oacc_ref[rows[u], :] += inter[u]
                return st

            lax.fori_loop(0, nc // sgrp, state_step, jnp.zeros((HEAD, HEAD), F32))

        o = oacc_ref[cx:, :]
        o_ref[...] = o
        rstd = lax.rsqrt(jnp.mean(o * o, axis=-1, keepdims=True) + EPS)
        r_ref[...] = (o * rstd * on_ref[...] * _silu(g_ref[cx:, :])).astype(BF16)

    sec = lambda s: pl.BlockSpec((None, t, HEAD), lambda h: (s, 0, h))
    col = pl.BlockSpec((seq, HEAD), lambda h: (0, h))
    tf32, tb16 = pltpu.VMEM((t, HEAD), F32), pltpu.VMEM((t, HEAD), BF16)
    return pl.pallas_call(
        body, name="hgrn_fwd", grid=(d // HEAD,),
        in_specs=[sec(0), sec(1), sec(2), sec(3), sec(4),
                  pl.BlockSpec((2, 3, HEAD), lambda h: (0, 0, h)), pl.BlockSpec((1, HEAD), lambda h: (0, h))],
        out_specs=[col, col, pl.BlockSpec((2, t, HEAD), lambda h: (0, 0, h)),
                   pl.BlockSpec((2, t, HEAD), lambda h: (0, 0, h)),
                   pl.BlockSpec((2, nc, 8, HEAD), lambda h: (0, 0, 0, h))],
        out_shape=[jax.ShapeDtypeStruct((seq, d), F32), jax.ShapeDtypeStruct((seq, d), BF16),
                   jax.ShapeDtypeStruct((2, t, d), F32), jax.ShapeDtypeStruct((2, t, d), F32),
                   jax.ShapeDtypeStruct((2, nc, 8, d), F32)],
        scratch_shapes=[tf32, tf32, tf32, pltpu.VMEM((nc, 8, HEAD), F32), tb16, tb16, tf32,
                        pltpu.VMEM((nc, HEAD, HEAD), F32)],
        compiler_params=_cp(),
    )(z1, z1, z1, z1, z1, lbl, onorm)


def hgrn_bwd(z1, lbl, onorm, o, dr_out, bcs, ks, decs, cx):
    _, t, d = z1.shape
    seq = t - cx
    nc, ncc = t // CHUNK, cx // CHUNK

    grp2, grp = _group(nc, (9, 6, 4, 3, 2, 1)), _group(nc, (12, 9, 6, 4, 3, 2, 1))

    def body(zf_ref, zb_ref, v_ref, q_ref, g_ref, lbl_ref, on_ref, o_ref, dr_ref, bcs_ref, ks_ref, decs_ref,
             dz_ref, don_ref, dlb_ref,
             qd_ref, ki_ref, do_ref, dqd_ref, dki_ref, dq_ref, dv_ref, ds_ref, dsl_ref):
        o = o_ref[...]
        g = g_ref[cx:, :]
        drv = dr_ref[...]
        onv = on_ref[...]
        rstd = lax.rsqrt(jnp.mean(o * o, axis=-1, keepdims=True) + EPS)
        ohat = o * rstd
        sg = _silu(g)
        don_ref[...] = jnp.sum(drv * ohat * sg, axis=0, keepdims=True)
        dz_ref[4, :cx, :] = jnp.zeros((cx, HEAD), BF16)
        dz_ref[4, cx:, :] = (drv * ohat * onv * _dsilu(g)).astype(BF16)
        dohat = drv * onv * sg
        do_ref[:cx, :] = jnp.zeros((cx, HEAD), BF16)
        do_ref[cx:, :] = (rstd * (dohat - ohat * jnp.mean(dohat * ohat, axis=-1, keepdims=True))).astype(BF16)

        for dr, z_ref in ((0, zf_ref), (1, zb_ref)):
            lbv, _ = _lower_bound(lbl_ref, dr)
            k_ref, bc_ref, dec_ref = ks_ref.at[dr], bcs_ref.at[dr], decs_ref.at[dr]
            _, keep = _chunk_consts(dr)
            cum_t = _chunk_consts(1 - dr)[0].astype(BF16)
            bc = bc_ref[...]
            qd_ref[...] = (q_ref[...] * jnp.exp(bc)).astype(BF16)
            ki_ref[...] = (k_ref[...] * jnp.exp(-bc)).astype(BF16)

            def local_step(m, carry, dr=dr, keep=keep):
                ns = [m * grp + u for u in range(grp)]
                rows = [_chunk_rows(n) for n in ns]
                rng = range(grp)
                qd = [qd_ref[r, :] for r in rows]
                ki = [ki_ref[r, :] for r in rows]
                doc = [do_ref[r, :] for r in rows]
                vc = [v_ref[r, :].astype(BF16) for r in rows]
                sc = [_dot(qd[u], ki[u], NT) for u in rng]
                dsc = [_dot(doc[u], vc[u], NT) for u in rng]
                inc = [_dot(vc[u], ki[u], TN) for u in rng]
                dinc = [_dot(doc[u], qd[u], TN) for u in rng]
                a = [jnp.where(keep, s, 0.0).astype(BF16) for s in sc]
                da = [jnp.where(keep, s, 0.0).astype(BF16) for s in dsc]
                dqd = [_dot(da[u], ki[u]) for u in rng]
                dki = [_dot(da[u], qd[u], TN) for u in rng]
                dv = [_dot(a[u], doc[u], TN) for u in rng]
                for u in rng:
                    ds_ref[ns[u]] = inc[u] * dec_ref[ns[u]][0:1, :]
                    dsl_ref[ns[u]] = dinc[u]
                    dqd_ref[rows[u], :] = dqd[u]
                    dki_ref[rows[u], :] = dki[u]
                    if dr == 0:
                        dv_ref[rows[u], :] = dv[u]
                    else:
                        dv_ref[rows[u], :] += dv[u]
                return carry

            lax.fori_loop(0, nc // grp, local_step, 0)

            def state_step(s, st, dr=dr):
                n = _chunk_of_step(s, dr, nc, ncc)
                inc = ds_ref[n]
                ds_ref[n] = st
                return st * dec_ref[n][0:1, :] + inc

            lax.fori_loop(0, nc, state_step, jnp.zeros((HEAD, HEAD), F32), unroll=4)

            def dstate_step(s, dst, dr=dr):
                n = _chunk_of_step(nc - 1 - s, dr, nc, ncc)
                inc = dsl_ref[n]
                dsl_ref[n] = dst
                return inc + dst * dec_ref[n][0:1, :]

            lax.fori_loop(0, nc, dstate_step, jnp.zeros((HEAD, HEAD), F32), unroll=4)

            def grad_step(m, carry, dr=dr, cum_t=cum_t):
                ns = [m * grp2 + u for u in range(grp2)]
                rows = [_chunk_rows(n) for n in ns]
                rng = range(grp2)
                st0 = [ds_ref[n] for n in ns]
                dst = [dsl_ref[n] for n in ns]
                dstb = [x.astype(BF16) for x in dst]
                dec = [dec_ref[n][0:1, :] for n in ns]
                doc = [do_ref[r, :] for r in rows]
                vc = [v_ref[r, :].astype(BF16) for r in rows]
                e = [jnp.exp(bc_ref[r, :]) for r in rows]
                einv = [jnp.exp(-bc_ref[r, :]) for r in rows]
                qd = [q_ref[rows[u], :] * e[u] for u in rng]
                ki = [k_ref[rows[u], :] * einv[u] for u in rng]
                kd = [ki[u] * dec[u] for u in rng]
                dqd_st = [_dot(doc[u], st0[u].astype(BF16)) for u in rng]
                dkd = [_dot(vc[u], dstb[u]) for u in rng]
                dv_st = [_dot(kd[u].astype(BF16), dstb[u], NT) for u in rng]
                dqd = [dqd_ref[rows[u], :] + dqd_st[u] for u in rng]
                dki = [dki_ref[r, :] for r in rows]
                dbc = [dqd[u] * qd[u] - dki[u] * ki[u] - dkd[u] * kd[u] for u in rng]
                cs = [_cumsum_chunk(cum_t, x) for x in dbc]
                for u in rng:
                    ddec = jnp.sum(dst[u] * st0[u], axis=0, keepdims=True)
                    dbl = jnp.sum(dkd[u] * kd[u], axis=0, keepdims=True) + ddec * dec[u]
                    dv_ref[rows[u], :] += dv_st[u]
                    dqd_ref[rows[u], :] = cs[u] + dbl
                    dki_ref[rows[u], :] = dki[u] * einv[u] + dkd[u] * (einv[u] * dec[u])
                    if dr == 0:
                        dq_ref[rows[u], :] = dqd[u] * e[u]
                    else:
                        dq_ref[rows[u], :] += dqd[u] * e[u]
                return carry

            lax.fori_loop(0, nc // grp2, grad_step, 0)

            sig, nsig, f = _gates(z_ref[...], lbv)
            common = (dqd_ref[...] / f - dki_ref[...]) * nsig
            dz_ref[dr] = (common * ((1.0 - lbv) * sig)).astype(BF16)
            dlb_ref[dr:dr + 1, :] = jnp.sum(common, axis=0, keepdims=True)

        dz_ref[2] = dv_ref[...].astype(BF16)
        dz_ref[3] = dq_ref[...].astype(BF16)

    sec = lambda s: pl.BlockSpec((None, t, HEAD), lambda h: (s, 0, h))
    col = pl.BlockSpec((seq, HEAD), lambda h: (0, h))
    tf32, tb16 = pltpu.VMEM((t, HEAD), F32), pltpu.VMEM((t, HEAD), BF16)
    states = pltpu.VMEM((nc, HEAD, HEAD), F32)
    return pl.pallas_call(
        body, name="hgrn_bwd", grid=(d // HEAD,),
        in_specs=[sec(0), sec(1), sec(2), sec(3), sec(4),
                  pl.BlockSpec((2, 3, HEAD), lambda h: (0, 0, h)), pl.BlockSpec((1, HEAD), lambda h: (0, h)),
                  col, col, pl.BlockSpec((2, t, HEAD), lambda h: (0, 0, h)),
                  pl.BlockSpec((2, t, HEAD), lambda h: (0, 0, h)),
                  pl.BlockSpec((2, nc, 8, HEAD), lambda h: (0, 0, 0, h))],
        out_specs=[pl.BlockSpec((5, t, HEAD), lambda h: (0, 0, h)),
                   pl.BlockSpec((1, HEAD), lambda h: (0, h)), pl.BlockSpec((2, HEAD), lambda h: (0, h))],
        out_shape=[jax.ShapeDtypeStruct((5, t, d), BF16), jax.ShapeDtypeStruct((1, d), F32),
                   jax.ShapeDtypeStruct((2, d), F32)],
        scratch_shapes=[tb16, tb16, tb16, tf32, tf32, tf32, tf32, states, states],
        compiler_params=_cp(),
    )(z1, z1, z1, z1, z1, lbl, onorm, o, dr_out, bcs, ks, decs)


def _place():
    x, y, c = lax.axis_index("x"), lax.axis_index("y"), lax.axis_index("c")
    chips = [(1 - x, y), (x, 1 - y), (1 - x, 1 - y)]
    return x, y, c, chips


def _relay_chips():
    x, y, c, _ = _place()
    first = c == 0
    near = (jnp.where(first, 1 - x, x), jnp.where(first, y, 1 - y))
    far = (jnp.where(first, x, 1 - x), jnp.where(first, 1 - y, y))
    return near, far, (1 - x, 1 - y)


def allgather_shards(bufs, after):
    n = len(bufs)

    def body(*refs):
        outs = refs[n + 1:2 * n + 1]
        done_ref, send_sems, recv_sems = refs[2 * n + 1:]
        done_ref[...] = jnp.zeros((8, 128), F32)
        x, y, c, _ = _place()
        me = (x, y, c)
        p = 2 * x + y
        near, far, diag = _relay_chips()
        half = [pl.ds(c * (s.shape[1] // 2), s.shape[1] // 2) for s in bufs]
        other = [pl.ds((1 - c) * (s.shape[1] // 2), s.shape[1] // 2) for s in bufs]
        slot = lambda chip: 2 * chip[0] + chip[1]

        def remote(i, k, ref, to):
            return pltpu.make_async_remote_copy(src_ref=ref, dst_ref=ref, send_sem=send_sems.at[6 * i + k],
                                                recv_sem=recv_sems.at[6 * i + k], device_id=to, device_id_type=MESH)

        sends = []

        def send(i, k, ref, to):
            cp = remote(i, k, ref, to)
            cp.start()
            sends.append(cp)

        for i in range(n):
            mine = outs[i].at[p, half[i]]
            send(i, 0, mine, (*near, c))
            send(i, 1, mine, (*far, c))
        for i in range(n):
            landed = outs[i].at[slot(near), half[i]]
            remote(i, 0, landed, me).wait_recv()
            send(i, 2, landed, (*far, c))
            send(i, 3, landed, (x, y, 1 - c))
        for i in range(n):
            landed = outs[i].at[slot(far), half[i]]
            remote(i, 1, landed, me).wait_recv()
            send(i, 4, landed, (x, y, 1 - c))
        for i in range(n):
            landed = outs[i].at[slot(diag), half[i]]
            remote(i, 2, landed, me).wait_recv()
            send(i, 5, landed, (x, y, 1 - c))
        for i in range(n):
            for k, chip in ((3, far), (4, near), (5, diag)):
                remote(i, k, outs[i].at[slot(chip), other[i]], me).wait_recv()
        for cp in sends:
            cp.wait_send()

    return pl.pallas_call(
        body, name="allgather_shards",
        in_specs=[ANY] * (n + 1), out_specs=[ANY] * n + [VMEM],
        out_shape=[jax.ShapeDtypeStruct(s.shape, s.dtype) for s in bufs] + [jax.ShapeDtypeStruct((8, 128), F32)],
        input_output_aliases={i: i for i in range(n)},
        scratch_shapes=[pltpu.SemaphoreType.DMA((6 * n,)), pltpu.SemaphoreType.DMA((6 * n,))],
        compiler_params=pltpu.CompilerParams(has_side_effects=True),
    )(*bufs, after)


def pair_sum(grad, got, chip_core):
    _, r, cc = grad.shape
    hr = r // 2
    tr = 256 if hr % 256 == 0 else hr
    nb = hr // tr

    def body(cc_ref, a_ref, b_ref, own_ref, sb_ref):
        s = a_ref[...].astype(F32) + b_ref[...].astype(F32)
        sb_ref[...] = s.astype(BF16)

        @pl.when(pl.program_id(1) == cc_ref[0])
        def _():
            own_ref[...] = s

    grid_spec = pltpu.PrefetchScalarGridSpec(
        num_scalar_prefetch=1, grid=(nb, 4),
        in_specs=[pl.BlockSpec((None, tr, cc), lambda i, qi, cc_ref: (qi, cc_ref[1] * nb + i, 0)),
                  pl.BlockSpec((None, tr, cc), lambda i, qi, cc_ref: (qi, i, 0))],
        out_specs=[pl.BlockSpec((tr, cc), lambda i, qi, cc_ref: (i, 0)),
                   pl.BlockSpec((None, tr, cc), lambda i, qi, cc_ref: (qi, i, 0))])
    return pl.pallas_call(
        body, name="pair_sum", grid_spec=grid_spec,
        out_shape=[jax.ShapeDtypeStruct((hr, cc), F32), jax.ShapeDtypeStruct((4, hr, cc), BF16)],
        compiler_params=_cp(),
    )(chip_core, grad, got)


def owner_sum(own, got, chip_core):
    hr, cc = own.shape
    tr = 256 if hr % 256 == 0 else hr
    nb = hr // tr

    def body(cc_ref, a_ref, b_ref, o_ref):
        s = a_ref[...] + b_ref[0].astype(F32)
        s = s + b_ref[1].astype(F32)
        o_ref[...] = s + b_ref[2].astype(F32)

    grid_spec = pltpu.PrefetchScalarGridSpec(
        num_scalar_prefetch=1, grid=(nb,),
        in_specs=[pl.BlockSpec((tr, cc), lambda i, cc_ref: (i, 0)),
                  pl.BlockSpec((3, tr, cc), lambda i, cc_ref: (0, i, 0))],
        out_specs=pl.BlockSpec((tr, cc), lambda i, cc_ref: (cc_ref[1] * nb + i, 0)))
    return pl.pallas_call(
        body, name="owner_sum", grid_spec=grid_spec,
        out_shape=jax.ShapeDtypeStruct((2 * hr, cc), F32), compiler_params=_cp(),
    )(chip_core, own, got)


def share_halves(bufs):
    n = len(bufs)

    def body(*refs):
        outs = refs[n:2 * n]
        send_sems, recv_sems = refs[2 * n:]
        x, y, c, _ = _place()
        copies = []
        for i in range(n):
            hr = bufs[i].shape[0] // 2
            mine = outs[i].at[pl.ds(c * hr, hr)]
            cp = pltpu.make_async_remote_copy(
                src_ref=mine, dst_ref=mine, send_sem=send_sems.at[i], recv_sem=recv_sems.at[i],
                device_id=(x, y, 1 - c), device_id_type=MESH)
            cp.start()
            copies.append((cp, outs[i].at[pl.ds((1 - c) * hr, hr)]))
        for i, (cp, theirs) in enumerate(copies):
            cp.wait_send()
            pltpu.make_async_remote_copy(
                src_ref=theirs, dst_ref=theirs, send_sem=send_sems.at[i], recv_sem=recv_sems.at[i],
                device_id=(x, y, c), device_id_type=MESH).wait_recv()

    return pl.pallas_call(
        body, name="share_halves",
        in_specs=[ANY] * n, out_specs=[ANY] * n,
        out_shape=[jax.ShapeDtypeStruct(b.shape, b.dtype) for b in bufs],
        input_output_aliases={i: i for i in range(n)},
        scratch_shapes=[pltpu.SemaphoreType.DMA((n,)), pltpu.SemaphoreType.DMA((n,))],
        compiler_params=pltpu.CompilerParams(has_side_effects=True),
    )(*bufs)


def allgather8(v, name, per_peer=False):
    r, n = v.shape[-2:]

    def body(v_ref, out_ref, send_sems, recv_sems):
        x, y, c, _ = _place()
        me = 4 * x + 2 * y + c
        out_ref[me] = v_ref[me] if per_peer else v_ref[...]

        def copy(k, peer_index, slot, to):
            return pltpu.make_async_remote_copy(
                src_ref=v_ref.at[peer_index] if per_peer else v_ref, dst_ref=out_ref.at[slot],
                send_sem=send_sems.at[k - 1], recv_sem=recv_sems.at[k - 1], device_id=to, device_id_type=MESH)

        peers = []
        for k in range(1, 8):
            px = 1 - x if (k >> 2) & 1 else x
            py = 1 - y if (k >> 1) & 1 else y
            pc = 1 - c if k & 1 else c
            peers.append((px, py, pc))
            copy(k, 4 * px + 2 * py + pc, me, (px, py, pc)).start()
        for k, (px, py, pc) in enumerate(peers, start=1):
            copy(k, me, 4 * px + 2 * py + pc, (x, y, c)).wait_recv()
        for k, (px, py, pc) in enumerate(peers, start=1):
            copy(k, 4 * px + 2 * py + pc, me, (px, py, pc)).wait_send()

    return pl.pallas_call(
        body, name=name, in_specs=[VMEM], out_specs=VMEM,
        out_shape=jax.ShapeDtypeStruct((8, r, n), v.dtype),
        scratch_shapes=[pltpu.SemaphoreType.DMA((7,)), pltpu.SemaphoreType.DMA((7,))],
        compiler_params=_cp(has_side_effects=True),
    )(v)


HBM = pl.BlockSpec(memory_space=pltpu.HBM)
SEM = pl.BlockSpec(memory_space=pltpu.SEMAPHORE)
DATAFLOW = pltpu.SideEffectType.DATAFLOW_SIDE_EFFECTING


def _descriptors(plan, refs, send_sems, recv_sems, arrivals=True):
    x, y, c, _ = _place()
    sends, recvs = plan(refs)
    out = [pltpu.make_async_remote_copy(src_ref=src, dst_ref=dst, send_sem=send_sems.at[k],
                                        recv_sem=recv_sems.at[k], device_id=to, device_id_type=MESH)
           for k, (src, dst, to) in enumerate(sends)]
    if not arrivals:
        return out, []
    inn = [pltpu.make_async_remote_copy(src_ref=land, dst_ref=land, send_sem=send_sems.at[k],
                                        recv_sem=recv_sems.at[k], device_id=(x, y, c), device_id_type=MESH)
           for k, land in enumerate(recvs)]
    return out, inn


def copies_start(name, arrays, n_copies, plan, after):
    na = len(arrays)

    def body(*refs):
        out, _ = _descriptors(plan, refs[:na], refs[na + 1], refs[na + 2], arrivals=False)
        for cp in out:
            cp.start()
        refs[-1][...] = jnp.zeros((8, 128), F32)

    res = pl.pallas_call(
        body, name=name,
        out_shape=(pltpu.SemaphoreType.DMA((n_copies,)), pltpu.SemaphoreType.DMA((n_copies,)),
                   *[pltpu.HBM(a.shape, a.dtype) for a in arrays], jax.ShapeDtypeStruct((8, 128), F32)),
        in_specs=[HBM] * na + [ANY], out_specs=(SEM, SEM, *[HBM] * na, VMEM),
        input_output_aliases={i: i + 2 for i in range(na)},
        compiler_params=pltpu.CompilerParams(has_side_effects=DATAFLOW),
    )(*[pltpu.with_memory_space_constraint(a, pltpu.HBM) for a in arrays], after)
    return res[0], res[1], list(res[2:2 + na]), res[-1]


def copies_wait(name, started, plan, after):
    send_sems, recv_sems, arrays, _ = started
    na = len(arrays)
    after = list(after) if isinstance(after, (list, tuple)) else [after]

    def body(*refs):
        out, inn = _descriptors(plan, refs[:na], refs[na], refs[na + 1])
        for cp in out:
            cp.wait_send()
        for cp in inn:
            cp.wait_recv()
        refs[-1][...] = jnp.zeros((8, 128), F32)

    res = pl.pallas_call(
        body, name=name,
        out_shape=(*[pltpu.HBM(a.shape, a.dtype) for a in arrays], jax.ShapeDtypeStruct((8, 128), F32)),
        in_specs=[HBM] * na + [SEM, SEM] + [ANY] * len(after), out_specs=(*[HBM] * na, VMEM),
        input_output_aliases={i: i for i in range(na)},
        compiler_params=pltpu.CompilerParams(has_side_effects=DATAFLOW),
    )(*arrays, send_sems, recv_sems, *after)
    return list(res[:na]), res[-1]


def _rows_half(r, c):
    return pl.ds(c * (r // 2), r // 2), pl.ds((1 - c) * (r // 2), r // 2)


def plan_gather_neighbours(refs):
    x, y, c, _ = _place()
    p = 2 * x + y
    near, far, _ = _relay_chips()
    sends, recvs = [], []
    for buf in refs:
        mine, _ = _rows_half(buf.shape[1], c)
        for chip in (near, far):
            sends.append((buf.at[p, mine], buf.at[p, mine], (*chip, c)))
            recvs.append(buf.at[2 * chip[0] + chip[1], mine])
    return sends, recvs


def plan_gather_relay(refs):
    x, y, c, _ = _place()
    near, far, diag = _relay_chips()
    slot = lambda chip: 2 * chip[0] + chip[1]
    sends, recvs = [], []
    for buf in refs:
        mine, theirs = _rows_half(buf.shape[1], c)
        landed = buf.at[slot(near), mine]
        sends.append((landed, landed, (*far, c)))
        recvs.append(buf.at[slot(diag), mine])
        for sent, got in ((near, far), (far, near)):
            sends.append((buf.at[slot(sent), mine], buf.at[slot(sent), mine], (x, y, 1 - c)))
            recvs.append(buf.at[slot(got), theirs])
    return sends, recvs


def plan_gather_d2d(refs):
    x, y, c, _ = _place()
    _, _, diag = _relay_chips()
    sends, recvs = [], []
    for buf in refs:
        mine, theirs = _rows_half(buf.shape[1], c)
        landed = buf.at[2 * diag[0] + diag[1], mine]
        sends.append((landed, landed, (x, y, 1 - c)))
        recvs.append(buf.at[2 * diag[0] + diag[1], theirs])
    return sends, recvs


def plan_exchange(refs):
    x, y, c, _ = _place()
    n = len(refs) // 2
    sends, recvs = [], []
    for grad, land in zip(refs[:n], refs[n:]):
        _, theirs = _rows_half(grad.shape[1], c)
        sends.append((grad.at[:, theirs], land, (x, y, 1 - c)))
        recvs.append(land)
    return sends, recvs


def plan_scatter(refs):
    x, y, c, chips = _place()
    n = len(refs) // 2
    sends, recvs = [], []
    for part, land in zip(refs[:n], refs[n:]):
        for j, chip in enumerate(chips):
            sends.append((part.at[2 * chip[0] + chip[1]], land.at[j], (*chip, c)))
            recvs.append(land.at[j])
    return sends, recvs


def plan_share(refs):
    x, y, c, _ = _place()
    sends, recvs = [], []
    for buf in refs:
        mine, theirs = _rows_half(buf.shape[0], c)
        sends.append((buf.at[mine], buf.at[mine], (x, y, 1 - c)))
        recvs.append(buf.at[theirs])
    return sends, recvs


def put_in_slot(w, chip, dtype, name):
    r, c = w.shape
    tr = 256 if r % 256 == 0 else r

    def body(chip_ref, w_ref, o_ref):
        o_ref[...] = w_ref[...].astype(dtype)

    grid_spec = pltpu.PrefetchScalarGridSpec(
        num_scalar_prefetch=1, grid=(r // tr,),
        in_specs=[pl.BlockSpec((tr, c), lambda i, chip_ref: (i, 0))],
        out_specs=pl.BlockSpec((None, tr, c), lambda i, chip_ref: (chip_ref[0], i, 0)))
    return pl.pallas_call(body, name=name, grid_spec=grid_spec,
                          out_shape=jax.ShapeDtypeStruct((4, r, c), dtype), compiler_params=_cp())(chip, w)


def ada_fwd(s_in, ada_w, ada_b, tn):
    nl, d, ws = ada_w.shape

    def body(s_ref, w_ref, b_ref, so_ref, mod_ref):
        s = _silu(s_ref[...])
        so_ref[...] = s
        mod_ref[...] = _dot(s.astype(BF16), w_ref[...].astype(BF16)) + b_ref[...]

    return pl.pallas_call(
        body, name="ada_fwd", grid=(nl, ws // tn),
        in_specs=[pl.BlockSpec((16, d), lambda l, j: (0, 0)),
                  pl.BlockSpec((None, d, tn), lambda l, j: (l, 0, j)),
                  pl.BlockSpec((None, 1, tn), lambda l, j: (l, 0, j))],
        out_specs=[pl.BlockSpec((16, d), lambda l, j: (0, 0)),
                   pl.BlockSpec((None, 16, tn), lambda l, j: (l, 0, j))],
        out_shape=[jax.ShapeDtypeStruct((16, d), F32), jax.ShapeDtypeStruct((nl, 16, ws), F32)],
        compiler_params=_cp(),
    )(s_in, ada_w, ada_b)


def _adamw_math(w, g, m, v):
    m = ADAM_B1 * m + (1.0 - ADAM_B1) * g
    v = ADAM_B2 * v + (1.0 - ADAM_B2) * (g * g)
    m_hat = m / (1.0 - ADAM_B1 ** ADAM_STEP)
    v_hat = v / (1.0 - ADAM_B2 ** ADAM_STEP)
    delta = -ADAM_LR * (m_hat / (jnp.sqrt(v_hat) + ADAM_EPS) + ADAM_WD * w)
    return delta, m, v


def ada_bwd_adamw(s, dm, w, m, v):
    nl, d, ws = w.shape
    tr = 256 if d % 256 == 0 else 128

    def body(s_ref, dm_ref, w_ref, m_ref, v_ref, g_ref, dl_ref, mo_ref, vo_ref, dc_ref):
        dmv = dm_ref[...].astype(BF16)
        wv = w_ref[...]
        g = _dot(s_ref[...].astype(BF16), dmv, TN)
        g_ref[...] = g
        dl_ref[...], mo_ref[...], vo_ref[...] = _adamw_math(wv, g, m_ref[...], v_ref[...])
        dc_ref[...] = _dot(dmv[8:16, :], wv.astype(BF16), NT)

    wblk = pl.BlockSpec((None, tr, ws), lambda l, i: (l, i, 0))
    wshape = jax.ShapeDtypeStruct((nl, d, ws), F32)
    return pl.pallas_call(
        body, name="ada_bwd_adamw", grid=(nl, d // tr),
        in_specs=[pl.BlockSpec((16, tr), lambda l, i: (0, i)),
                  pl.BlockSpec((None, 16, ws), lambda l, i: (l, 0, 0)), wblk, wblk, wblk],
        out_specs=[wblk, wblk, wblk, wblk, pl.BlockSpec((None, 8, tr), lambda l, i: (l, 0, i))],
        out_shape=[wshape, wshape, wshape, wshape, jax.ShapeDtypeStruct((nl, 8, d), F32)],
        compiler_params=_cp(),
    )(s, dm, w, m, v)


def adamw(w, g, m, v, name, with_grad=False):
    r, c = w.shape
    tr = 256 if r % 256 == 0 else r

    def body(w_ref, g_ref, m_ref, v_ref, dl_ref, mo_ref, vo_ref, *g_out):
        gv = g_ref[...]
        dl_ref[...], mo_ref[...], vo_ref[...] = _adamw_math(w_ref[...], gv, m_ref[...], v_ref[...])
        if with_grad:
            g_out[0][...] = gv

    blk = pl.BlockSpec((tr, c), lambda i: (i, 0))
    shape = jax.ShapeDtypeStruct((r, c), F32)
    n_out = 4 if with_grad else 3
    return pl.pallas_call(body, name=name, grid=(r // tr,), in_specs=[blk] * 4, out_specs=[blk] * n_out,
                          out_shape=[shape] * n_out, compiler_params=_cp())(w, g, m, v)


ROW_MOD = 10


def small_reduce(gathered):
    _, rows, d = gathered.shape

    def body(g_ref, o_ref):
        tot = g_ref[0]
        for b in range(1, 8):
            tot = tot + g_ref[b]
        o_ref[0:rows, :] = tot
        for layer in range(2):
            lat = ROW_MOD + 6 * layer
            o_ref[24 + 3 * layer:27 + 3 * layer, :] = tot[lat:lat + 3, :] + tot[lat + 3:lat + 6, :]
        o_ref[30:32, :] = jnp.zeros((2, d), F32)

    return pl.pallas_call(body, name="small_reduce", in_specs=[VMEM], out_specs=VMEM,
                          out_shape=jax.ShapeDtypeStruct((32, d), F32), compiler_params=_cp())(gathered)


def lb_logits_grad(lbl, dlb):
    _, _, n = lbl.shape

    def body(l_ref, d_ref, o_ref):
        for dr in range(2):
            _, (p0, p1, p2) = _lower_bound(l_ref, dr)
            dv = d_ref[dr:dr + 1, :]
            o_ref[dr, 0:1, :] = p0 * p2 * dv
            o_ref[dr, 1:2, :] = p1 * p2 * dv
            o_ref[dr, 2:3, :] = -p2 * (p0 + p1) * dv

    return pl.pallas_call(body, name="lb_logits_grad", in_specs=[VMEM, VMEM], out_specs=VMEM,
                          out_shape=jax.ShapeDtypeStruct((2, 3, n), F32), compiler_params=_cp())(lbl, dlb)


def c_ctx_grad(parts, c_ctx):
    d = c_ctx.shape[1]

    def body(p_ref, c_ref, o_ref):
        tot = p_ref[0, 0:1, :]
        for chip in range(1, 4):
            tot = tot + p_ref[2 * chip, 0:1, :]
        o_ref[...] = tot * _dsilu(c_ref[...])

    return pl.pallas_call(body, name="c_ctx_grad", in_specs=[VMEM, VMEM], out_specs=VMEM,
                          out_shape=jax.ShapeDtypeStruct((1, d), F32), compiler_params=_cp())(parts, c_ctx)


def kernel(x, c, ctx, c_ctx, ada_w, ada_b, pre_g, post_g, ev_w_in, ev_pool_w, ev_pool_scale, ev_conv_w, ev_conv_b, ev_w_out, od_w_in, od_onorm_g, od_w_out, lb_logits, loss_target, m_c_ctx, m_ada_w, m_ada_b, m_pre_g, m_post_g, m_ev_w_in, m_ev_pool_w, m_ev_pool_scale, m_ev_conv_w, m_ev_conv_b, m_ev_w_out, m_od_w_in, m_od_onorm_g, m_od_w_out, m_lb_logits, v_c_ctx, v_ada_w, v_ada_b, v_pre_g, v_post_g, v_ev_w_in, v_ev_pool_w, v_ev_pool_scale, v_ev_conv_w, v_ev_conv_b, v_ev_w_out, v_od_w_in, v_od_onorm_g, v_od_w_out, v_lb_logits):
    _, seq, d = x.shape
    cx = ctx.shape[1]
    t = cx + seq
    half_d = d // 2
    g = half_d // N_POOL
    tn = d // 4
    xi, yi, ci = lax.axis_index("x"), lax.axis_index("y"), lax.axis_index("c")
    chip = 2 * xi + yi
    chip_arr = jnp.reshape(chip, (1,)).astype(jnp.int32)
    chip_core_arr = jnp.stack([chip, ci]).astype(jnp.int32)

    c_rows = jnp.concatenate([c, jnp.zeros((7, d), F32)], axis=0)
    c_all = allgather8(c_rows, "allgather_c")[:, 0, :]
    s_in = jnp.concatenate([c_all, c_ctx.reshape(1, d), jnp.zeros((7, d), F32)], axis=0)
    ws_ada = ada_w.shape[2]
    ada_b_mine = lax.dynamic_slice(ada_b, (0, chip * ws_ada), (2, ws_ada)).reshape(2, 1, ws_ada)
    s_act, mod_mine = ada_fwd(s_in, ada_w, ada_b_mine, tn)
    mod_rows = jnp.concatenate([
        mod_mine[:, :8].transpose(1, 0, 2), jnp.broadcast_to(mod_mine[:, 8][None], (8, 2, ws_ada)),
        jnp.zeros((8, 4, ws_ada), F32)], axis=1)
    mod_all = allgather8(mod_rows, "exchange_mod", per_peer=True)

    pad = lambda a, rows: jnp.concatenate([a, jnp.zeros((rows - a.shape[0], g), F32)], axis=0)
    small = jnp.concatenate([
        ev_pool_w.reshape(g, g), pad(ev_conv_w.reshape(3, g), 8), pad(od_onorm_g.reshape(2, g), 8),
        pad(lb_logits.reshape(12, g), 16)], axis=0)
    ev_in_g, ev_out_g, small_g, ev_done = allgather_shards([
        put_in_slot(ev_w_in[0], chip_arr, BF16, "cast_ev_w_in"),
        put_in_slot(ev_w_out[0], chip_arr, BF16, "cast_ev_w_out"),
        put_in_slot(small, chip_arr, F32, "place_small")], mod_all)
    od_ici = copies_start("gather_od_ici_start", [
        put_in_slot(od_w_in[0], chip_arr, BF16, "cast_od_w_in"),
        put_in_slot(od_w_out[0], chip_arr, BF16, "cast_od_w_out")], 4, plan_gather_neighbours, ev_done)
    ev_out3 = ev_out_g.reshape(1, d, d)
    pool_w_full = small_g[:, :g].reshape(4, N_POOL, g // 4, g).transpose(1, 0, 2, 3).reshape(N_POOL, g, g)
    conv_w_full = small_g[:, g:g + 3].transpose(1, 0, 2).reshape(3, half_d)
    onorm_full = small_g[:, g + 8:g + 10].reshape(1, d)
    lbl_full = small_g[:, g + 16:g + 28].reshape(4, 2, 3, 2 * g).transpose(1, 2, 0, 3).reshape(2, 3, d)

    by_chip = mod_all[0::2]
    mods = jnp.stack([by_chip[:, 2:4], by_chip[:, 0:2]]).transpose(2, 0, 1, 3).reshape(2, 2, 3 * d)
    shift, scale, gate = mods[:, :, :d], mods[:, :, d:2 * d], mods[:, :, 2 * d:]

    h0, xs = normmod_fwd_joining(ctx[0], x[0], pre_g[0:1] + od_ici[3][0:1, 0:1], shift[0], scale[0])
    z0 = mm_nn(h0, ev_in_g, half_d, tn, "mm_ev_in")
    u = mix_b_fwd(z0, conv_w_full, ev_conv_b, mix_a_fwd(z0, pool_w_full, ev_pool_scale, cx), cx)
    od_relay = copies_start("gather_od_relay_start",
                            copies_wait("gather_od_ici_wait", od_ici, plan_gather_neighbours, u)[0],
                            6, plan_gather_relay, u)
    y0 = mm_nn(u, ev_out3, d, tn, "mm_ev_out")[0]
    xs1, h1 = post_fwd_norm(xs, y0, post_g[0:1] + od_relay[3][0:1, 0:1], gate[0],
                            pre_g[1:2], shift[1], scale[1], cx)
    od_d2d = copies_start("gather_od_d2d_start",
                          copies_wait("gather_od_relay_wait", od_relay, plan_gather_relay, xs1)[0],
                          2, plan_gather_d2d, xs1)
    (od_in_g, od_out_g), _ = copies_wait("gather_od_d2d_wait", od_d2d, plan_gather_d2d, od_d2d[3])
    od_out3 = od_out_g.reshape(1, d, d)

    z1 = mm_nn(h1, od_in_g, d, tn, "mm_od_in")
    o1, r1, bcs1, ks1, decs1 = hgrn_fwd(z1, lbl_full, onorm_full, cx)
    y1 = mm_nn(r1, od_out3, d, tn, "mm_od_out")[0]
    sq, dx2, dy1, dgate1, dpost1 = post_loss(xs1, y1, post_g[1:2], gate[1], loss_target[0], cx)

    dr1 = mm_nt(dy1[None], None, od_out3, tn, "mm_od_out_dx")
    g_od_out = mm_tn(r1, dy1[None], None, d, tn, "mm_od_out_dw")
    dz1, donorm, dlb = hgrn_bwd(z1, lbl_full, onorm_full, o1, dr1, bcs1, ks1, decs1, cx)
    dh1 = mm_nt(dz1, None, od_in_g, tn, "mm_od_in_dx")
    g_od_in = mm_tn(h1, dz1, None, od_in_g.shape[2], tn, "mm_od_in_dw")
    od_grads = [g_od_in, g_od_out.reshape(4, d // 4, d)]
    half_zone = lambda a, lead, dt: lax.empty((lead, a.shape[1] // 2, a.shape[2]), dt)
    od_ex = copies_start("reduce_od_exchange_start", od_grads + [half_zone(a, 4, a.dtype) for a in od_grads],
                         2, plan_exchange, dh1)

    dxs1, dpre1, dshift1, dscale1, dy0, dgate0, dpost0 = normmod_bwd(
        xs1, dh1, pre_g[1:2] + od_ex[3][0:1, 0:1], scale[1], dx2, cx, True,
        prev=(y0, post_g[0:1], gate[0]))
    du = mm_nt(dy0[None], None, ev_out3, tn, "mm_ev_out_dx")
    g_ev_out = mm_tn(u, dy0[None], None, d, tn, "mm_ev_out_dw")
    od_got, _ = copies_wait("reduce_od_exchange_wait", od_ex, plan_exchange, g_ev_out)
    od_sums = [pair_sum(od_got[i], od_got[2 + i], chip_core_arr) for i in range(2)]
    od_sc = copies_start("reduce_od_scatter_start",
                         [sb for _, sb in od_sums] + [half_zone(a, 3, BF16) for a in od_grads],
                         6, plan_scatter, du)
    dz0a, g_pool_w, dpool_scale = mix_a_bwd(z0, du, pool_w_full, ev_pool_scale + od_sc[3][0:1, 0:1], cx)
    dz0b, dconv_w, dconv_b = mix_b_bwd(z0, du, conv_w_full, ev_conv_b + od_sc[3][0:1, 0:1], cx)
    g_ev_in = mm_tn(h0, dz0a, dz0b, ev_in_g.shape[2], tn, "mm_ev_in_dw")
    ev_grads = [g_ev_in, g_ev_out.reshape(4, d // 4, d), g_pool_w.reshape(4, g, g)]
    ev_ex = copies_start("reduce_ev_exchange_start", ev_grads + [half_zone(a, 4, a.dtype) for a in ev_grads],
                         3, plan_exchange, dpool_scale)
    dh0 = mm_nt(dz0a, dz0b, ev_in_g, tn, "mm_ev_in_dx")
    dxs0, dpre0, dshift0, dscale0 = normmod_bwd(xs, dh0, pre_g[0:1] + ev_ex[3][0:1, 0:1], scale[0], dxs1,
                                                cx, False, True)
    grad_x = dxs0[None]
    ev_got, _ = copies_wait("reduce_ev_exchange_wait", ev_ex, plan_exchange, dxs0)
    ev_sums = [pair_sum(ev_got[i], ev_got[3 + i], chip_core_arr) for i in range(3)]
    od_recv, _ = copies_wait("reduce_od_scatter_wait", od_sc, plan_scatter, dxs0)

    zrow = jnp.zeros((1, d), F32)
    small_rows = jnp.concatenate([
        dpre0, dpre1, dpost0, dpost1,
        jnp.concatenate([dpool_scale, dconv_b], axis=1),
        jnp.concatenate([dconv_w.reshape(1, 3 * half_d), jnp.zeros((1, half_d), F32)], axis=1).reshape(2, d),
        donorm, dlb,
        dshift0[1:2], dscale0[1:2], dgate0[1:2], dshift0[0:1], dscale0[0:1], dgate0[0:1],
        dshift1[1:2], dscale1[1:2], dgate1[1:2], dshift1[0:1], dscale1[0:1], zrow,
        jnp.concatenate([sq[0:1], jnp.zeros((1, d - 128), F32)], axis=1),
        zrow], axis=0)
    small_all = allgather8(small_rows, "allgather_small")
    ev_sc = copies_start("reduce_ev_scatter_start",
                         [sb for _, sb in ev_sums] + [half_zone(a, 3, BF16) for a in ev_grads],
                         9, plan_scatter, small_all)
    od_sh = copies_start("reduce_od_share_start",
                         [owner_sum(od_sums[i][0], od_recv[2 + i], chip_core_arr) for i in range(2)],
                         2, plan_share, dxs0)
    tot = small_reduce(small_all + ev_sc[3][0:1, 0:1])
    loss = tot[22, 0] * (0.5 / d)

    dm_rows = []
    for layer in range(2):
        lat = ROW_MOD + 6 * layer
        dm_lat = small_all[:, lat:lat + 3].reshape(8, 3 * d)
        dm_ctx = tot[lat + 3:lat + 6].reshape(1, 3 * d)
        dm_rows.append(jnp.concatenate([dm_lat, dm_ctx, jnp.zeros((7, 3 * d), F32)], axis=0))
    dm_full = jnp.stack(dm_rows)
    dm_mine = lax.dynamic_slice(dm_full, (0, 0, chip * ws_ada), (2, 16, ws_ada))

    def step(w, gr, m, v, name, with_grad=False):
        shape = w.shape
        cols = shape[-1]
        two_d = lambda a: a.reshape(-1, cols)
        res = adamw(two_d(w), two_d(gr), two_d(m), two_d(v), "adamw_" + name, with_grad)
        return tuple(a.reshape(shape) for a in res)

    grad_ada_b = tot[24:30].reshape(2, 3 * d)
    grad_pre_g = tot[0:2]
    grad_post_g = tot[2:4]
    grad_ev_pool_scale = tot[4:5, :half_d]
    grad_ev_conv_b = tot[4:5, half_d:]
    conv_w_tot = tot[5:7].reshape(1, 2 * d)[:, :3 * half_d].reshape(3, N_POOL, g)
    grad_ev_conv_w = lax.dynamic_slice(conv_w_tot, (0, chip, 0), (3, 1, g)).reshape(1, 3, g)
    grad_od_onorm_g = lax.dynamic_slice(tot[7:8], (0, chip * 2 * g), (1, 2 * g))
    dlb_mine = lax.dynamic_slice(tot[8:10], (0, chip * 2 * g), (2, 2 * g))
    grad_lb_logits = lb_logits_grad(lb_logits, dlb_mine)
    upd = {
        "ada_b": step(ada_b, grad_ada_b, m_ada_b, v_ada_b, "ada_b"),
        "pre_g": step(pre_g, grad_pre_g, m_pre_g, v_pre_g, "pre_g"),
        "post_g": step(post_g, grad_post_g, m_post_g, v_post_g, "post_g"),
        "ev_pool_scale": step(ev_pool_scale, grad_ev_pool_scale, m_ev_pool_scale, v_ev_pool_scale, "ev_pool_scale"),
        "ev_conv_w": step(ev_conv_w, grad_ev_conv_w, m_ev_conv_w, v_ev_conv_w, "ev_conv_w"),
        "ev_conv_b": step(ev_conv_b, grad_ev_conv_b, m_ev_conv_b, v_ev_conv_b, "ev_conv_b"),
        "od_onorm_g": step(od_onorm_g, grad_od_onorm_g, m_od_onorm_g, v_od_onorm_g, "od_onorm_g"),
        "lb_logits": step(lb_logits, grad_lb_logits, m_lb_logits, v_lb_logits, "lb_logits"),
    }
    grad_ada_w, delta_ada_w, new_m_ada_w, new_v_ada_w, dctx_part = ada_bwd_adamw(
        s_act, dm_mine, ada_w, m_ada_w, v_ada_w)
    upd["ada_w"] = (delta_ada_w, new_m_ada_w, new_v_ada_w)
    (grad_od_w_in, grad_od_w_out), _ = copies_wait("reduce_od_share_wait", od_sh, plan_share, ev_sc[3])
    upd["od_w_in"] = step(od_w_in, grad_od_w_in[None], m_od_w_in, v_od_w_in, "od_w_in", True)
    upd["od_w_out"] = step(od_w_out, grad_od_w_out[None], m_od_w_out, v_od_w_out, "od_w_out", True)
    grad_od_w_in, grad_od_w_out = upd["od_w_in"][3], upd["od_w_out"][3]
    done_behind = [dctx_part] + [upd[k][0] for k in (
        "od_w_in", "od_w_out", "ada_b", "pre_g", "post_g", "ev_pool_scale", "ev_conv_w", "ev_conv_b",
        "od_onorm_g", "lb_logits")]
    ev_recv, ev_landed = copies_wait("reduce_ev_scatter_wait", ev_sc, plan_scatter, done_behind)
    grad_ev_w_in, grad_ev_w_out, grad_pool_w = share_halves(
        [owner_sum(ev_sums[i][0], ev_recv[3 + i], chip_core_arr) for i in range(3)])
    dctx_all = allgather8(dctx_part[0] + dctx_part[1] + ev_landed[0:1, 0:1], "allgather_dctx")
    grad_c_ctx = c_ctx_grad(dctx_all, c_ctx.reshape(1, d)).reshape(d)
    upd["c_ctx"] = step(c_ctx, grad_c_ctx, m_c_ctx, v_c_ctx, "c_ctx")
    upd["ev_w_in"] = step(ev_w_in, grad_ev_w_in[None], m_ev_w_in, v_ev_w_in, "ev_w_in", True)
    upd["ev_pool_w"] = step(ev_pool_w, grad_pool_w.reshape(1, N_POOL, g // 4, g), m_ev_pool_w, v_ev_pool_w,
                            "ev_pool_w", True)
    upd["ev_w_out"] = step(ev_w_out, grad_ev_w_out[None], m_ev_w_out, v_ev_w_out, "ev_w_out", True)
    grad_ev_w_in, grad_ev_pool_w, grad_ev_w_out = upd["ev_w_in"][3], upd["ev_pool_w"][3], upd["ev_w_out"][3]
    names = ["c_ctx", "ada_w", "ada_b", "pre_g", "post_g", "ev_w_in", "ev_pool_w", "ev_pool_scale",
             "ev_conv_w", "ev_conv_b", "ev_w_out", "od_w_in", "od_onorm_g", "od_w_out", "lb_logits"]
    grads = [grad_c_ctx, grad_ada_w, grad_ada_b, grad_pre_g, grad_post_g, grad_ev_w_in, grad_ev_pool_w,
             grad_ev_pool_scale, grad_ev_conv_w, grad_ev_conv_b, grad_ev_w_out, grad_od_w_in,
             grad_od_onorm_g, grad_od_w_out, grad_lb_logits]
    return (loss, grad_x, *grads, *[upd[k][0] for k in names], *[upd[k][1] for k in names],
            *[upd[k][2] for k in names])
```

```python
import jax
import jax.numpy as jnp
from jax import lax
from jax.experimental import pallas as pl
from jax.experimental.pallas import tpu as pltpu

EPS = 1e-6
GRID_W_LOG2 = 6
CHUNK = 64
HEAD = 128
N_POOL = 4
ADAM_LR, ADAM_B1, ADAM_B2, ADAM_EPS, ADAM_WD, ADAM_STEP = 0.001, 0.9, 0.999, 1e-08, 0.01, 10
VMEM_LIMIT = 56 * 1024 * 1024
MESH = pl.DeviceIdType.MESH
F32, BF16 = jnp.float32, jnp.bfloat16
ANY = pl.BlockSpec(memory_space=pl.ANY)
VMEM = pl.BlockSpec(memory_space=pltpu.VMEM)


def _cp(**kw):
    return pltpu.CompilerParams(vmem_limit_bytes=VMEM_LIMIT, **kw)


def _silu(x):
    return x * jax.nn.sigmoid(x)


def _dsilu(x):
    s = jax.nn.sigmoid(x)
    return s * (1.0 + x * (1.0 - s))


def _dot(a, b, dims=((1,), (0,)), precision=None):
    return lax.dot_general(a, b, (dims, ((), ())), preferred_element_type=F32, precision=precision)


NN = ((1,), (0,))
NT = ((1,), (1,))
TN = ((0,), (0,))


def _row_block(cx):
    return 256 if cx % 256 == 0 else 128


def normmod_fwd_joining(ctx, x, g, shift, scale):
    cx, d = ctx.shape
    t = cx + x.shape[0]
    tm = _row_block(cx)
    nctx = cx // tm

    def body(c_ref, x_ref, g_ref, sh_ref, sc_ref, h_ref, xs_ref):
        is_ctx = pl.program_id(0) < nctx
        x = jnp.where(is_ctx, c_ref[...], x_ref[...])
        xs_ref[...] = x
        rstd = lax.rsqrt(jnp.mean(x * x, axis=-1, keepdims=True) + EPS)
        sc = jnp.where(is_ctx, sc_ref[0:1, :], sc_ref[1:2, :])
        sh = jnp.where(is_ctx, sh_ref[0:1, :], sh_ref[1:2, :])
        h_ref[...] = ((x * rstd) * g_ref[...] * (1.0 + sc) + sh).astype(BF16)

    row = pl.BlockSpec((tm, d), lambda i: (i, 0))
    vec = lambda r: pl.BlockSpec((r, d), lambda i: (0, 0))
    return pl.pallas_call(
        body, name="normmod_fwd_joining", grid=(t // tm,),
        in_specs=[pl.BlockSpec((tm, d), lambda i: (jnp.minimum(i, nctx - 1), 0)),
                  pl.BlockSpec((tm, d), lambda i: (jnp.maximum(i - nctx, 0), 0)), vec(1), vec(2), vec(2)],
        out_specs=[row, row],
        out_shape=[jax.ShapeDtypeStruct((t, d), BF16), jax.ShapeDtypeStruct((t, d), F32)],
        compiler_params=_cp(),
    )(ctx, x, g, shift, scale)


def normmod_bwd(xs, dh, g, scale, dres, cx, res_is_latent_only, dx_latent_only=False, prev=None):
    t, d = xs.shape
    tm = _row_block(cx)
    nctx = cx // tm
    n_prev = 0 if prev is None else 3

    def body(x_ref, dh_ref, g_ref, sc_ref, dres_ref, *rest):
        dx_ref, dg_ref, dsh_ref, dsc_ref = rest[n_prev:n_prev + 4]
        i = pl.program_id(0)
        is_ctx = i < nctx

        @pl.when(i == 0)
        def _():
            for ref in rest[n_prev + 1:n_prev + 4] + rest[n_prev + 5:]:
                ref[...] = jnp.zeros_like(ref)

        x = x_ref[...]
        dh = dh_ref[...]
        gv = g_ref[...]
        rstd = lax.rsqrt(jnp.mean(x * x, axis=-1, keepdims=True) + EPS)
        xhat = x * rstd
        sc = jnp.where(is_ctx, sc_ref[0:1, :], sc_ref[1:2, :])
        dsh = jnp.sum(dh, axis=0, keepdims=True)
        dhx = dh * xhat
        dsc = jnp.sum(dhx * gv, axis=0, keepdims=True)
        dg_ref[...] += jnp.sum(dhx * (1.0 + sc), axis=0, keepdims=True)
        zero = jnp.zeros_like(dsh)
        dsh_ref[0:1, :] += jnp.where(is_ctx, dsh, zero)
        dsh_ref[1:2, :] += jnp.where(is_ctx, zero, dsh)
        dsc_ref[0:1, :] += jnp.where(is_ctx, dsc, zero)
        dsc_ref[1:2, :] += jnp.where(is_ctx, zero, dsc)
        dxhat = dh * (gv * (1.0 + sc))
        dx = rstd * (dxhat - xhat * jnp.mean(dxhat * xhat, axis=-1, keepdims=True))
        res = dres_ref[...]
        if res_is_latent_only:
            res = jnp.where(is_ctx, jnp.zeros_like(res), res)
        dxt = dx + res
        dx_ref[...] = dxt
        if prev is not None:
            y_ref, pg_ref, gate_ref = rest[:3]
            dy_ref, dgate_ref, dpg_ref = rest[7:]
            y = y_ref[...]
            pgv = pg_ref[...]
            rstd_y = lax.rsqrt(jnp.mean(y * y, axis=-1, keepdims=True) + EPS)
            yhat = y * rstd_y
            gt = jnp.where(is_ctx, gate_ref[0:1, :], gate_ref[1:2, :])
            dxy = dxt * yhat
            dgt = jnp.sum(dxy * pgv, axis=0, keepdims=True)
            dgate_ref[0:1, :] += jnp.where(is_ctx, dgt, zero)
            dgate_ref[1:2, :] += jnp.where(is_ctx, zero, dgt)
            dpg_ref[...] += jnp.sum(dxy * gt, axis=0, keepdims=True)
            dyhat = dxt * (gt * pgv)
            dy_ref[...] = (rstd_y * (dyhat - yhat * jnp.mean(dyhat * yhat, axis=-1, keepdims=True))).astype(BF16)

    row = pl.BlockSpec((tm, d), lambda i: (i, 0))
    if res_is_latent_only:
        res_spec = pl.BlockSpec((tm, d), lambda i: (jnp.maximum(i - nctx, 0), 0))
    else:
        res_spec = row
    vec = lambda r: pl.BlockSpec((r, d), lambda i: (0, 0))
    dx_spec = pl.BlockSpec((tm, d), lambda i: (jnp.maximum(i - nctx, 0), 0)) if dx_latent_only else row
    in_specs = [row, row, vec(1), vec(2), res_spec]
    out_specs = [dx_spec, vec(1), vec(2), vec(2)]
    out_shape = [jax.ShapeDtypeStruct((t - cx if dx_latent_only else t, d), F32), jax.ShapeDtypeStruct((1, d), F32),
                 jax.ShapeDtypeStruct((2, d), F32), jax.ShapeDtypeStruct((2, d), F32)]
    if prev is not None:
        in_specs += [row, vec(1), vec(2)]
        out_specs += [row, vec(2), vec(1)]
        out_shape += [jax.ShapeDtypeStruct((t, d), BF16), jax.ShapeDtypeStruct((2, d), F32),
                      jax.ShapeDtypeStruct((1, d), F32)]
    return pl.pallas_call(
        body, name="normmod_bwd", grid=(t // tm,), in_specs=in_specs, out_specs=out_specs, out_shape=out_shape,
        compiler_params=_cp(),
    )(xs, dh, g, scale, dres, *(() if prev is None else prev))


def post_fwd_norm(xs, y, pg, gate, g_next, shift_next, scale_next, cx):
    t, d = xs.shape
    tm = _row_block(cx)
    nctx = cx // tm

    def body(x_ref, y_ref, pg_ref, gate_ref, g_ref, sh_ref, sc_ref, o_ref, h_ref):
        is_ctx = pl.program_id(0) < nctx
        pick = lambda ref: jnp.where(is_ctx, ref[0:1, :], ref[1:2, :])
        y = y_ref[...]
        rstd = lax.rsqrt(jnp.mean(y * y, axis=-1, keepdims=True) + EPS)
        x = x_ref[...] + pick(gate_ref) * ((y * rstd) * pg_ref[...])
        o_ref[...] = x
        rstd = lax.rsqrt(jnp.mean(x * x, axis=-1, keepdims=True) + EPS)
        h_ref[...] = ((x * rstd) * g_ref[...] * (1.0 + pick(sc_ref)) + pick(sh_ref)).astype(BF16)

    row = pl.BlockSpec((tm, d), lambda i: (i, 0))
    vec = lambda r: pl.BlockSpec((r, d), lambda i: (0, 0))
    return pl.pallas_call(
        body, name="post_fwd_norm", grid=(t // tm,),
        in_specs=[row, row, vec(1), vec(2), vec(1), vec(2), vec(2)], out_specs=[row, row],
        out_shape=[jax.ShapeDtypeStruct((t, d), F32), jax.ShapeDtypeStruct((t, d), BF16)],
        compiler_params=_cp(),
    )(xs, y, pg, gate, g_next, shift_next, scale_next)


def post_loss(xs, y, pg, gate, target, cx):
    t, d = xs.shape
    n = y.shape[0]
    tm = _row_block(cx)
    nctx = cx // tm

    def body(x_ref, y_ref, pg_ref, gate_ref, tgt_ref, sq_ref, dx_ref, dy_ref, dgate_ref, dpg_ref):
        @pl.when(pl.program_id(0) == 0)
        def _():
            sq_ref[...] = jnp.zeros_like(sq_ref)
            dgate_ref[...] = jnp.zeros_like(dgate_ref)
            dpg_ref[...] = jnp.zeros_like(dpg_ref)

        y = y_ref[...]
        pgv = pg_ref[...]
        gt = gate_ref[1:2, :]
        rstd = lax.rsqrt(jnp.mean(y * y, axis=-1, keepdims=True) + EPS)
        yhat = y * rstd
        err = x_ref[...] + gt * (yhat * pgv) - tgt_ref[...]
        sq_ref[...] += jnp.sum(err * err)
        dx = err * (1.0 / d)
        dx_ref[...] = dx
        dxy = dx * yhat
        dgate_ref[1:2, :] += jnp.sum(dxy * pgv, axis=0, keepdims=True)
        dpg_ref[...] += jnp.sum(dxy * gt, axis=0, keepdims=True)
        dyhat = dx * (gt * pgv)
        dy_ref[...] = (rstd * (dyhat - yhat * jnp.mean(dyhat * yhat, axis=-1, keepdims=True))).astype(BF16)

    row = pl.BlockSpec((tm, d), lambda i: (i, 0))
    xrow = pl.BlockSpec((tm, d), lambda i: (i + nctx, 0))
    vec = lambda r: pl.BlockSpec((r, d), lambda i: (0, 0))
    return pl.pallas_call(
        body, name="post_loss", grid=(n // tm,),
        in_specs=[xrow, row, vec(1), vec(2), row],
        out_specs=[pl.BlockSpec((8, 128), lambda i: (0, 0)), row, row, vec(2), vec(1)],
        out_shape=[jax.ShapeDtypeStruct((8, 128), F32), jax.ShapeDtypeStruct((n, d), F32),
                   jax.ShapeDtypeStruct((n, d), BF16), jax.ShapeDtypeStruct((2, d), F32),
                   jax.ShapeDtypeStruct((1, d), F32)],
        compiler_params=_cp(),
    )(xs, y, pg, gate, target)


def _split_rows(m):
    for cand in (1152, 1024, 768, 512, 384, 256, 128):
        if m % cand == 0 and m // cand >= 2:
            return cand
    return m


def mm_nn(a, w3, sec, tn, name):
    m, k = a.shape
    q, _, ws = w3.shape
    n = q * ws
    tpq, tps = ws // tn, sec // tn
    tm = next(c for c in (768, 512, 256, 128) if m % c == 0)

    def body(a_ref, w_ref, o_ref):
        w = w_ref[...]

        def step(i, carry):
            rows = pl.ds(pl.multiple_of(i * tm, tm), tm)
            o_ref[rows, :] = _dot(a_ref[rows, :], w)
            return carry

        lax.fori_loop(0, m // tm, step, 0)

    return pl.pallas_call(
        body, name=name, grid=(n // tn,),
        in_specs=[pl.BlockSpec((m, k), lambda j: (0, 0)),
                  pl.BlockSpec((None, k, tn), lambda j: (j // tpq, 0, j % tpq))],
        out_specs=pl.BlockSpec((None, m, tn), lambda j: (j // tps, 0, j % tps)),
        out_shape=jax.ShapeDtypeStruct((n // sec, m, sec), F32), compiler_params=_cp(),
    )(a, w3)


def _two_stacks(a3, b3, tn):
    sec = a3.shape[2]
    tps = sec // tn
    n1 = a3.shape[0] * tps
    first = lambda j: (jnp.minimum(j, n1 - 1) // tps, jnp.minimum(j, n1 - 1) % tps)
    second = lambda j: (jnp.maximum(j - n1, 0) // tps, jnp.maximum(j - n1, 0) % tps)
    return n1, first, second


def mm_nt(a3, b3, w3, tn, name):
    if b3 is None:
        b3 = a3
    _, m, sec = a3.shape
    q, k, ws = w3.shape
    n = q * ws
    tpq = ws // tn
    mb = _split_rows(m)
    n1, first, second = _two_stacks(a3, b3, tn)

    def body(a_ref, b_ref, w_ref, o_ref):
        j = pl.program_id(1)

        @pl.when(j == 0)
        def _():
            o_ref[...] = jnp.zeros_like(o_ref)

        @pl.when(j < n1)
        def _():
            o_ref[...] += _dot(a_ref[...], w_ref[...], NT)

        @pl.when(j >= n1)
        def _():
            o_ref[...] += _dot(b_ref[...], w_ref[...], NT)

    return pl.pallas_call(
        body, name=name, grid=(m // mb, n // tn),
        in_specs=[pl.BlockSpec((None, mb, tn), lambda i, j: (first(j)[0], i, first(j)[1])),
                  pl.BlockSpec((None, mb, tn), lambda i, j: (second(j)[0], i, second(j)[1])),
                  pl.BlockSpec((None, k, tn), lambda i, j: (j // tpq, 0, j % tpq))],
        out_specs=pl.BlockSpec((mb, k), lambda i, j: (i, 0)),
        out_shape=jax.ShapeDtypeStruct((m, k), F32), compiler_params=_cp(),
    )(a3, b3, w3)


def mm_tn(a, b3, c3, ws, tn, name):
    m, k = a.shape
    sec = b3.shape[2]
    n = (b3.shape[0] + (0 if c3 is None else c3.shape[0])) * sec
    if c3 is None:
        c3 = b3
    tpq = ws // tn
    kb = 256 if k % 256 == 0 else 128
    n1, first, second = _two_stacks(b3, c3, tn)

    def body(a_ref, b_ref, c_ref, o_ref):
        def product(rhs_ref):
            rhs = rhs_ref[...]
            for i in range(k // kb):
                o_ref[i * kb:(i + 1) * kb, :] = _dot(a_ref[:, i * kb:(i + 1) * kb], rhs, TN).astype(BF16)

        @pl.when(pl.program_id(0) < n1)
        def _():
            product(b_ref)

        @pl.when(pl.program_id(0) >= n1)
        def _():
            product(c_ref)

    return pl.pallas_call(
        body, name=name, grid=(n // tn,),
        in_specs=[pl.BlockSpec((m, k), lambda j: (0, 0)),
                  pl.BlockSpec((None, m, tn), lambda j: (first(j)[0], 0, first(j)[1])),
                  pl.BlockSpec((None, m, tn), lambda j: (second(j)[0], 0, second(j)[1]))],
        out_specs=pl.BlockSpec((None, k, tn), lambda j: (j // tpq, 0, j % tpq)),
        out_shape=jax.ShapeDtypeStruct((n // ws, k, ws), BF16), compiler_params=_cp(),
    )(a, b3, c3)


POOL_REACH = 8 << GRID_W_LOG2


def _token_parts(tok, cx):
    lat = tok - cx
    return tok < cx, lat >> GRID_W_LOG2, lat & ((1 << GRID_W_LOG2) - 1)


def _pool_mask(gi, row0, col0, tm, ncols, cx, transposed):
    half = jnp.left_shift(1, gi)
    r = lax.broadcasted_iota(jnp.int32, (tm, 1), 0) + row0
    c = lax.broadcasted_iota(jnp.int32, (1, ncols), 1) + col0
    out_tok, src_tok = (c, r) if transposed else (r, c)
    o_ctx, o_row, o_col = _token_parts(out_tok, cx)
    s_ctx, s_row, s_col = _token_parts(src_tok, cx)

    def inside(o, s):
        return (s >= o - half) & (s <= o + half - 1)

    ctx_hit = o_ctx & s_ctx & inside(out_tok, src_tok)
    lat_hit = (~o_ctx) & (~s_ctx) & inside(o_row, s_row) & inside(o_col, s_col)
    return jnp.where(ctx_hit | lat_hit, 1.0, 0.0).astype(BF16)


def _pool_inv_count(gi, row0, tm, cx, seq):
    half = jnp.left_shift(1, gi)
    r = lax.broadcasted_iota(jnp.int32, (tm, 1), 0) + row0
    is_ctx, row, col = _token_parts(r, cx)

    def count(pos, size):
        return jnp.minimum(pos + half - 1, size - 1) - jnp.maximum(pos - half, 0) + 1

    cnt = jnp.where(is_ctx, count(r, cx), count(row, seq >> GRID_W_LOG2) * count(col, 1 << GRID_W_LOG2))
    return 1.0 / cnt.astype(F32)


def _lat_band(tm):
    side = POOL_REACH // tm
    return side, 2 * side + 1


def _lat_mask(gi, tm, cx, transposed):
    side, band = _lat_band(tm)
    return _pool_mask(gi, cx + side * tm, cx, tm, band * tm, cx, transposed)


def _store_padded_lat(dst_ref, lat, tm):
    side, _ = _lat_band(tm)
    seq = lat.shape[0]
    zeros = jnp.zeros((side * tm, lat.shape[1]), dst_ref.dtype)
    dst_ref[0:side * tm, :] = zeros
    dst_ref[side * tm + seq:, :] = zeros
    dst_ref[side * tm:side * tm + seq, :] = lat.astype(dst_ref.dtype)


def mix_a_fwd(z0, pool_w, pool_scale, cx):
    _, t, half_d = z0.shape
    g = half_d // N_POOL
    seq = t - cx
    tm = _row_block(cx)
    side, band = _lat_band(tm)

    def body(v_ref, ag_ref, w_ref, sc_ref, u_ref, vlat_ref, mask_ref):
        gi = pl.program_id(0)
        w = w_ref[...].astype(BF16)
        sc = sc_ref[...]
        _store_padded_lat(vlat_ref, v_ref[cx:, :], tm)
        mask_ref[...] = _lat_mask(gi, tm, cx, False)

        def finish(row0, window_sum):
            rows = pl.ds(row0, tm)
            pooled = window_sum * _pool_inv_count(gi, row0, tm, cx, seq) - v_ref[rows, :]
            mixed = _dot(pooled.astype(BF16), w) * sc
            u_ref[rows, :] = (mixed * _silu(ag_ref[rows, :])).astype(BF16)

        vctx = v_ref[0:cx, :].astype(BF16)
        for i in range(cx // tm):
            finish(i * tm, _dot(_pool_mask(gi, i * tm, 0, tm, cx, cx, False), vctx))

        def step(j, carry):
            src = vlat_ref[pl.ds(pl.multiple_of(j * tm, tm), band * tm), :]
            finish(pl.multiple_of(cx + j * tm, tm), _dot(mask_ref[...], src))
            return carry

        lax.fori_loop(0, seq // tm, step, 0)

    sec = lambda s: pl.BlockSpec((None, t, g), lambda j: (s, 0, j))
    return pl.pallas_call(
        body, name="mix_a_fwd", grid=(N_POOL,),
        in_specs=[sec(0), sec(1), pl.BlockSpec((None, g, g), lambda j: (j, 0, 0)),
                  pl.BlockSpec((1, g), lambda j: (0, j))],
        out_specs=pl.BlockSpec((t, g), lambda j: (0, j)),
        out_shape=jax.ShapeDtypeStruct((t, 2 * half_d), BF16),
        scratch_shapes=[pltpu.VMEM((seq + 2 * side * tm, g), BF16), pltpu.VMEM((tm, band * tm), BF16)],
        compiler_params=_cp(),
    )(z0, z0, pool_w, pool_scale)


def mix_a_bwd(z0, du, pool_w, pool_scale, cx):
    _, t, half_d = z0.shape
    g = half_d // N_POOL
    seq = t - cx
    tm = _row_block(cx)
    gq = g // 4
    side, band = _lat_band(tm)

    def body(v_ref, ag_ref, du_ref, w_ref, sc_ref, dz_ref, dw_ref, dsc_ref,
             vlat_ref, mask_ref, pooled_ref, dmx_ref, dpl_ref, wlat_ref, wctx_ref):
        gi = pl.program_id(0)
        w = w_ref[...].astype(BF16)
        sc = sc_ref[...]
        _store_padded_lat(vlat_ref, v_ref[cx:, :], tm)
        _store_padded_lat(wlat_ref, jnp.zeros((seq, g), BF16), tm)
        mask_ref[...] = _lat_mask(gi, tm, cx, False)

        def first(row0, window_sum, weighted_ref, weighted_row0):
            rows = pl.ds(row0, tm)
            inv = _pool_inv_count(gi, row0, tm, cx, seq)
            pooled = (window_sum * inv - v_ref[rows, :]).astype(BF16)
            pooled_ref[rows, :] = pooled
            mixed = _dot(pooled, w)
            ag = ag_ref[rows, :]
            duv = du_ref[rows, :]
            dz_ref[1, rows, :] = (duv * (mixed * sc) * _dsilu(ag)).astype(BF16)
            dms = duv * _silu(ag)
            dmixed = (dms * sc).astype(BF16)
            dmx_ref[rows, :] = dmixed
            dpooled = _dot(dmixed, w, NT)
            dpl_ref[rows, :] = dpooled
            weighted_ref[pl.ds(weighted_row0, tm), :] = (dpooled * inv).astype(BF16)
            return jnp.sum(dms * mixed, axis=0, keepdims=True)

        dsc = jnp.zeros((1, g), F32)
        vctx = v_ref[0:cx, :].astype(BF16)
        for i in range(cx // tm):
            dsc += first(i * tm, _dot(_pool_mask(gi, i * tm, 0, tm, cx, cx, False), vctx), wctx_ref, i * tm)

        def first_lat(j, acc):
            src = vlat_ref[pl.ds(pl.multiple_of(j * tm, tm), band * tm), :]
            return acc + first(pl.multiple_of(cx + j * tm, tm), _dot(mask_ref[...], src),
                               wlat_ref, pl.multiple_of((side + j) * tm, tm))

        dsc_ref[...] = lax.fori_loop(0, seq // tm, first_lat, dsc)
        dw = _dot(pooled_ref[...], dmx_ref[...], TN)
        for qi in range(4):
            dw_ref[qi] = dw[qi * gq:(qi + 1) * gq, :]

        wctx = wctx_ref[...]
        for i in range(cx // tm):
            rows = pl.ds(i * tm, tm)
            dz_ref[0, rows, :] = (_dot(_pool_mask(gi, i * tm, 0, tm, cx, cx, True), wctx)
                                  - dpl_ref[rows, :]).astype(BF16)
        mask_ref[...] = _lat_mask(gi, tm, cx, True)

        def second_lat(j, carry):
            rows = pl.ds(pl.multiple_of(cx + j * tm, tm), tm)
            src = wlat_ref[pl.ds(pl.multiple_of(j * tm, tm), band * tm), :]
            dz_ref[0, rows, :] = (_dot(mask_ref[...], src) - dpl_ref[rows, :]).astype(BF16)
            return carry

        lax.fori_loop(0, seq // tm, second_lat, 0)

    sec = lambda s: pl.BlockSpec((None, t, g), lambda j: (s, 0, j))
    padded = pltpu.VMEM((seq + 2 * side * tm, g), BF16)
    return pl.pallas_call(
        body, name="mix_a_bwd", grid=(N_POOL,),
        in_specs=[sec(0), sec(1), pl.BlockSpec((t, g), lambda j: (0, j)),
                  pl.BlockSpec((None, g, g), lambda j: (j, 0, 0)),
                  pl.BlockSpec((1, g), lambda j: (0, j))],
        out_specs=[pl.BlockSpec((2, t, g), lambda j: (0, 0, j)),
                   pl.BlockSpec((4, None, gq, g), lambda j: (0, j, 0, 0)),
                   pl.BlockSpec((1, g), lambda j: (0, j))],
        out_shape=[jax.ShapeDtypeStruct((2, t, half_d), BF16),
                   jax.ShapeDtypeStruct((4, N_POOL, gq, g), F32),
                   jax.ShapeDtypeStruct((1, half_d), F32)],
        scratch_shapes=[padded, pltpu.VMEM((tm, band * tm), BF16), pltpu.VMEM((t, g), BF16),
                        pltpu.VMEM((t, g), BF16), pltpu.VMEM((t, g), F32), padded, pltpu.VMEM((cx, g), BF16)],
        compiler_params=_cp(),
    )(z0, z0, du, pool_w, pool_scale)


def _conv_masks(t, cx):
    r = lax.broadcasted_iota(jnp.int32, (t, 1), 0)
    has_prev = jnp.where((r == 0) | (r == cx), 0.0, 1.0)
    has_next = jnp.where((r == cx - 1) | (r == t - 1), 0.0, 1.0)
    return has_prev, has_next


def mix_b_fwd(z0, conv_w, conv_b, u, cx):
    _, t, half_d = z0.shape
    gb = 128
    off = half_d // gb

    def body(bx_ref, bb_ref, bc_ref, bg_ref, w_ref, b_ref, _, u_ref):
        has_prev, has_next = _conv_masks(t, cx)
        tt = bc_ref[...] * bx_ref[...]
        prev = pltpu.roll(tt, 1, 0) * has_prev
        nxt = pltpu.roll(tt, t - 1, 0) * has_next
        cv = prev * w_ref[0:1, :] + tt * w_ref[1:2, :] + nxt * w_ref[2:3, :] + b_ref[...]
        u_ref[...] = (bb_ref[...] * cv * _silu(bg_ref[...])).astype(BF16)

    sec = lambda s: pl.BlockSpec((None, t, gb), lambda j: (s, 0, j))
    return pl.pallas_call(
        body, name="mix_b_fwd", grid=(half_d // gb,),
        in_specs=[sec(2), sec(3), sec(4), sec(5), pl.BlockSpec((3, gb), lambda j: (0, j)),
                  pl.BlockSpec((1, gb), lambda j: (0, j)), ANY],
        out_specs=pl.BlockSpec((t, gb), lambda j: (0, j + off)),
        out_shape=jax.ShapeDtypeStruct((t, 2 * half_d), BF16), input_output_aliases={6: 0},
        compiler_params=_cp(),
    )(z0, z0, z0, z0, conv_w, conv_b, u)


def mix_b_bwd(z0, du, conv_w, conv_b, cx):
    _, t, half_d = z0.shape
    gb = 128
    off = half_d // gb

    def body(bx_ref, bb_ref, bc_ref, bg_ref, du_ref, w_ref, b_ref, dz_ref, dw_ref, db_ref):
        has_prev, has_next = _conv_masks(t, cx)
        bx, bb, bc, bg = bx_ref[...], bb_ref[...], bc_ref[...], bg_ref[...]
        duv = du_ref[...]
        tt = bc * bx
        prev = pltpu.roll(tt, 1, 0) * has_prev
        nxt = pltpu.roll(tt, t - 1, 0) * has_next
        w0, w1, w2 = w_ref[0:1, :], w_ref[1:2, :], w_ref[2:3, :]
        cv = prev * w0 + tt * w1 + nxt * w2 + b_ref[...]
        sg = _silu(bg)
        dz_ref[1] = (duv * cv * sg).astype(BF16)
        dz_ref[3] = (duv * bb * cv * _dsilu(bg)).astype(BF16)
        dcv = duv * bb * sg
        dw_ref[0:1, :] = jnp.sum(dcv * prev, axis=0, keepdims=True)
        dw_ref[1:2, :] = jnp.sum(dcv * tt, axis=0, keepdims=True)
        dw_ref[2:3, :] = jnp.sum(dcv * nxt, axis=0, keepdims=True)
        db_ref[...] = jnp.sum(dcv, axis=0, keepdims=True)
        dt = (pltpu.roll(dcv * has_prev, t - 1, 0) * w0 + dcv * w1
              + pltpu.roll(dcv * has_next, 1, 0) * w2)
        dz_ref[0] = (dt * bc).astype(BF16)
        dz_ref[2] = (dt * bx).astype(BF16)

    sec = lambda s: pl.BlockSpec((None, t, gb), lambda j: (s, 0, j))
    return pl.pallas_call(
        body, name="mix_b_bwd", grid=(half_d // gb,),
        in_specs=[sec(2), sec(3), sec(4), sec(5), pl.BlockSpec((t, gb), lambda j: (0, j + off)),
                  pl.BlockSpec((3, gb), lambda j: (0, j)), pl.BlockSpec((1, gb), lambda j: (0, j))],
        out_specs=[pl.BlockSpec((4, t, gb), lambda j: (0, 0, j)),
                   pl.BlockSpec((3, gb), lambda j: (0, j)), pl.BlockSpec((1, gb), lambda j: (0, j))],
        out_shape=[jax.ShapeDtypeStruct((4, t, half_d), BF16),
                   jax.ShapeDtypeStruct((3, half_d), F32), jax.ShapeDtypeStruct((1, half_d), F32)],
        compiler_params=_cp(),
    )(z0, z0, z0, z0, du, conv_w, conv_b)


def _lower_bound(lbl_ref, d):
    l0, l1, l2 = lbl_ref[d, 0:1, :], lbl_ref[d, 1:2, :], lbl_ref[d, 2:3, :]
    mx = jnp.maximum(jnp.maximum(l0, l1), l2)
    e0, e1, e2 = jnp.exp(l0 - mx), jnp.exp(l1 - mx), jnp.exp(l2 - mx)
    inv = 1.0 / (e0 + e1 + e2)
    return (e0 + e1) * inv, (e0 * inv, e1 * inv, e2 * inv)


def _chunk_consts(d):
    r = lax.broadcasted_iota(jnp.int32, (CHUNK, CHUNK), 0)
    c = lax.broadcasted_iota(jnp.int32, (CHUNK, CHUNK), 1)
    keep = (c <= r) if d == 0 else (c >= r)
    return jnp.where(keep, 1.0, 0.0).astype(F32), keep


def _chunk_of_step(s, d, nc, ncc):
    if d == 0:
        return s
    return jnp.where(s < ncc, ncc - 1 - s, nc - 1 + ncc - s)


def _gates(z, lbv):
    e = jnp.exp(-jnp.abs(z))
    r = 1.0 / (1.0 + e)
    er = e * r
    pos = z >= 0.0
    sig = jnp.where(pos, r, er)
    nsig = jnp.where(pos, er, r)
    return sig, nsig, lbv + (1.0 - lbv) * sig


def _split3(x):
    hi = x.astype(BF16)
    r1 = x - hi.astype(F32)
    mid = r1.astype(BF16)
    lo = (r1 - mid.astype(F32)).astype(BF16)
    return jnp.concatenate([hi, mid, lo], axis=1)


def _cumsum_chunk(cum, x):
    y = _dot(cum, _split3(x))
    return y[:, :HEAD] + y[:, HEAD:2 * HEAD] + y[:, 2 * HEAD:]


def _chunk_rows(n):
    return pl.ds(pl.multiple_of(n * CHUNK, CHUNK), CHUNK)


def _group(nc, prefer=(4, 3, 2, 1)):
    return next(u for u in prefer if nc % u == 0)


WIDE_GROUP = (18, 12, 6, 4, 3, 2, 1)


def _decay_pass(lf_ref, bc_ref, dec_ref, cum, nc):
    grp = _group(nc, WIDE_GROUP)

    def step(m, carry):
        ns = [m * grp + u for u in range(grp)]
        lfc = [lf_ref[_chunk_rows(n), :] for n in ns]
        bc = [_cumsum_chunk(cum, x) for x in lfc]
        for u, n in enumerate(ns):
            bc_ref[_chunk_rows(n), :] = bc[u]
            dec_ref[n] = jnp.broadcast_to(jnp.exp(jnp.sum(lfc[u], axis=0, keepdims=True)), (8, HEAD))
        return carry

    lax.fori_loop(0, nc // grp, step, 0)


def hgrn_fwd(z1, lbl, onorm, cx):
    _, t, d = z1.shape
    seq = t - cx
    nc, ncc = t // CHUNK, cx // CHUNK

    grp, sgrp = _group(nc, (9, 6, 4, 3, 2, 1)), _group(nc, WIDE_GROUP)

    def body(zf_ref, zb_ref, v_ref, q_ref, g_ref, lbl_ref, on_ref, o_ref, r_ref, bcs_ref, ks_ref, decs_ref,
             lf_ref, k_ref, bc_ref, dec_ref, qd_ref, ki_ref, oacc_ref, ds_ref):
        for dr, z_ref in ((0, zf_ref), (1, zb_ref)):
            lbv, _ = _lower_bound(lbl_ref, dr)
            _, nsig, f = _gates(z_ref[...], lbv)
            lf_ref[...] = jnp.log(f)
            k_ref[...] = (1.0 - lbv) * nsig
            cum, keep = _chunk_consts(dr)
            _decay_pass(lf_ref, bc_ref, dec_ref, cum.astype(BF16), nc)
            bc = bc_ref[...]
            bcs_ref[dr] = bc
            ks_ref[dr] = k_ref[...]
            decs_ref[dr] = dec_ref[...]
            qd_ref[...] = (q_ref[...] * jnp.exp(bc)).astype(BF16)
            ki_ref[...] = (k_ref[...] * jnp.exp(-bc)).astype(BF16)

            def local_step(m, carry, dr=dr, keep=keep):
                ns = [m * grp + u for u in range(grp)]
                rows = [_chunk_rows(n) for n in ns]
                qd = [qd_ref[r, :] for r in rows]
                ki = [ki_ref[r, :] for r in rows]
                vc = [v_ref[r, :].astype(BF16) for r in rows]
                sc = [_dot(qd[u], ki[u], NT) for u in range(grp)]
                inc = [_dot(vc[u], ki[u], TN) for u in range(grp)]
                a = [jnp.where(keep, s, 0.0).astype(BF16) for s in sc]
                intra = [_dot(a[u], vc[u]) for u in range(grp)]
                for u in range(grp):
                    ds_ref[ns[u]] = inc[u] * dec_ref[ns[u]][0:1, :]
                    if dr == 0:
                        oacc_ref[rows[u], :] = intra[u]
                    else:
                        oacc_ref[rows[u], :] += intra[u]
                return carry

            lax.fori_loop(0, nc // grp, local_step, 0)

            def state_step(m, st, dr=dr):
                ns = [_chunk_of_step(m * sgrp + u, dr, nc, ncc) for u in range(sgrp)]
                rows = [_chunk_rows(n) for n in ns]
                sts = []
                for n in ns:
                    sts.append(st.astype(BF16))
                    st = st * dec_ref[n][0:1, :] + ds_ref[n]
                inter = [_dot(qd_ref[rows[u], :], sts[u], NT) for u in range(sgrp)]
                for u in range(sgrp):
                    oacc_ref[rows[u], :] += inter[u]
                return st

            lax.fori_loop(0, nc // sgrp, state_step, jnp.zeros((HEAD, HEAD), F32))

        o = oacc_ref[cx:, :]
        o_ref[...] = o
        rstd = lax.rsqrt(jnp.mean(o * o, axis=-1, keepdims=True) + EPS)
        r_ref[...] = (o * rstd * on_ref[...] * _silu(g_ref[cx:, :])).astype(BF16)

    sec = lambda s: pl.BlockSpec((None, t, HEAD), lambda h: (s, 0, h))
    col = pl.BlockSpec((seq, HEAD), lambda h: (0, h))
    tf32, tb16 = pltpu.VMEM((t, HEAD), F32), pltpu.VMEM((t, HEAD), BF16)
    return pl.pallas_call(
        body, name="hgrn_fwd", grid=(d // HEAD,),
        in_specs=[sec(0), sec(1), sec(2), sec(3), sec(4),
                  pl.BlockSpec((2, 3, HEAD), lambda h: (0, 0, h)), pl.BlockSpec((1, HEAD), lambda h: (0, h))],
        out_specs=[col, col, pl.BlockSpec((2, t, HEAD), lambda h: (0, 0, h)),
                   pl.BlockSpec((2, t, HEAD), lambda h: (0, 0, h)),
                   pl.BlockSpec((2, nc, 8, HEAD), lambda h: (0, 0, 0, h))],
        out_shape=[jax.ShapeDtypeStruct((seq, d), F32), jax.ShapeDtypeStruct((seq, d), BF16),
                   jax.ShapeDtypeStruct((2, t, d), F32), jax.ShapeDtypeStruct((2, t, d), F32),
                   jax.ShapeDtypeStruct((2, nc, 8, d), F32)],
        scratch_shapes=[tf32, tf32, tf32, pltpu.VMEM((nc, 8, HEAD), F32), tb16, tb16, tf32,
                        pltpu.VMEM((nc, HEAD, HEAD), F32)],
        compiler_params=_cp(),
    )(z1, z1, z1, z1, z1, lbl, onorm)


def hgrn_bwd(z1, lbl, onorm, o, dr_out, bcs, ks, decs, cx):
    _, t, d = z1.shape
    seq = t - cx
    nc, ncc = t // CHUNK, cx // CHUNK

    grp2, grp = _group(nc, (12, 9, 6, 4, 3, 2, 1)), _group(nc, (18, 12, 9, 6, 4, 3, 2, 1))

    def body(zf_ref, zb_ref, v_ref, q_ref, g_ref, lbl_ref, on_ref, o_ref, dr_ref, bcs_ref, ks_ref, decs_ref,
             dz_ref, don_ref, dlb_ref,
             qd_ref, ki_ref, do_ref, dqd_ref, dki_ref, dq_ref, dv_ref, ds_ref, dsl_ref):
        o = o_ref[...]
        g = g_ref[cx:, :]
        drv = dr_ref[...]
        onv = on_ref[...]
        rstd = lax.rsqrt(jnp.mean(o * o, axis=-1, keepdims=True) + EPS)
        ohat = o * rstd
        sg = _silu(g)
        don_ref[...] = jnp.sum(drv * ohat * sg, axis=0, keepdims=True)
        dz_ref[4, :cx, :] = jnp.zeros((cx, HEAD), BF16)
        dz_ref[4, cx:, :] = (drv * ohat * onv * _dsilu(g)).astype(BF16)
        dohat = drv * onv * sg
        do_ref[:cx, :] = jnp.zeros((cx, HEAD), BF16)
        do_ref[cx:, :] = (rstd * (dohat - ohat * jnp.mean(dohat * ohat, axis=-1, keepdims=True))).astype(BF16)

        for dr, z_ref in ((0, zf_ref), (1, zb_ref)):
            lbv, _ = _lower_bound(lbl_ref, dr)
            k_ref, bc_ref, dec_ref = ks_ref.at[dr], bcs_ref.at[dr], decs_ref.at[dr]
            _, keep = _chunk_consts(dr)
            cum_t = _chunk_consts(1 - dr)[0].astype(BF16)
            bc = bc_ref[...]
            qd_ref[...] = (q_ref[...] * jnp.exp(bc)).astype(BF16)
            ki_ref[...] = (k_ref[...] * jnp.exp(-bc)).astype(BF16)

            def local_step(m, carry, dr=dr, keep=keep):
                ns = [m * grp + u for u in range(grp)]
                rows = [_chunk_rows(n) for n in ns]
                rng = range(grp)
                qd = [qd_ref[r, :] for r in rows]
                ki = [ki_ref[r, :] for r in rows]
                doc = [do_ref[r, :] for r in rows]
                vc = [v_ref[r, :].astype(BF16) for r in rows]
                sc = [_dot(qd[u], ki[u], NT) for u in rng]
                dsc = [_dot(doc[u], vc[u], NT) for u in rng]
                inc = [_dot(vc[u], ki[u], TN) for u in rng]
                dinc = [_dot(doc[u], qd[u], TN) for u in rng]
                a = [jnp.where(keep, s, 0.0).astype(BF16) for s in sc]
                da = [jnp.where(keep, s, 0.0).astype(BF16) for s in dsc]
                dqd = [_dot(da[u], ki[u]) for u in rng]
                dki = [_dot(da[u], qd[u], TN) for u in rng]
                dv = [_dot(a[u], doc[u], TN) for u in rng]
                for u in rng:
                    ds_ref[ns[u]] = inc[u] * dec_ref[ns[u]][0:1, :]
                    dsl_ref[ns[u]] = dinc[u]
                    dqd_ref[rows[u], :] = dqd[u]
                    dki_ref[rows[u], :] = dki[u]
                    if dr == 0:
                        dv_ref[rows[u], :] = dv[u]
                    else:
                        dv_ref[rows[u], :] += dv[u]
                return carry

            lax.fori_loop(0, nc // grp, local_step, 0)

            def state_step(s, st, dr=dr):
                n = _chunk_of_step(s, dr, nc, ncc)
                inc = ds_ref[n]
                ds_ref[n] = st
                return st * dec_ref[n][0:1, :] + inc

            lax.fori_loop(0, nc, state_step, jnp.zeros((HEAD, HEAD), F32), unroll=4)

            def dstate_step(s, dst, dr=dr):
                n = _chunk_of_step(nc - 1 - s, dr, nc, ncc)
                inc = dsl_ref[n]
                dsl_ref[n] = dst
                return inc + dst * dec_ref[n][0:1, :]

            lax.fori_loop(0, nc, dstate_step, jnp.zeros((HEAD, HEAD), F32), unroll=4)

            def grad_step(m, carry, dr=dr, cum_t=cum_t):
                ns = [m * grp2 + u for u in range(grp2)]
                rows = [_chunk_rows(n) for n in ns]
                rng = range(grp2)
                st0 = [ds_ref[n] for n in ns]
                dst = [dsl_ref[n] for n in ns]
                dstb = [x.astype(BF16) for x in dst]
                dec = [dec_ref[n][0:1, :] for n in ns]
                doc = [do_ref[r, :] for r in rows]
                vc = [v_ref[r, :].astype(BF16) for r in rows]
                e = [jnp.exp(bc_ref[r, :]) for r in rows]
                einv = [jnp.exp(-bc_ref[r, :]) for r in rows]
                qd = [q_ref[rows[u], :] * e[u] for u in rng]
                ki = [k_ref[rows[u], :] * einv[u] for u in rng]
                kd = [ki[u] * dec[u] for u in rng]
                dqd_st = [_dot(doc[u], st0[u].astype(BF16)) for u in rng]
                dkd = [_dot(vc[u], dstb[u]) for u in rng]
                dv_st = [_dot(kd[u].astype(BF16), dstb[u], NT) for u in rng]
                dqd = [dqd_ref[rows[u], :] + dqd_st[u] for u in rng]
                dki = [dki_ref[r, :] for r in rows]
                dbc = [dqd[u] * qd[u] - dki[u] * ki[u] - dkd[u] * kd[u] for u in rng]
                cs = [_cumsum_chunk(cum_t, x) for x in dbc]
                for u in rng:
                    ddec = jnp.sum(dst[u] * st0[u], axis=0, keepdims=True)
                    dbl = jnp.sum(dkd[u] * kd[u], axis=0, keepdims=True) + ddec * dec[u]
                    dv_ref[rows[u], :] += dv_st[u]
                    dqd_ref[rows[u], :] = cs[u] + dbl
                    dki_ref[rows[u], :] = dki[u] * einv[u] + dkd[u] * (einv[u] * dec[u])
                    if dr == 0:
                        dq_ref[rows[u], :] = dqd[u] * e[u]
                    else:
                        dq_ref[rows[u], :] += dqd[u] * e[u]
                return carry

            lax.fori_loop(0, nc // grp2, grad_step, 0)

            sig, nsig, f = _gates(z_ref[...], lbv)
            common = (dqd_ref[...] / f - dki_ref[...]) * nsig
            dz_ref[dr] = (common * ((1.0 - lbv) * sig)).astype(BF16)
            dlb_ref[dr:dr + 1, :] = jnp.sum(common, axis=0, keepdims=True)

        dz_ref[2] = dv_ref[...].astype(BF16)
        dz_ref[3] = dq_ref[...].astype(BF16)

    sec = lambda s: pl.BlockSpec((None, t, HEAD), lambda h: (s, 0, h))
    col = pl.BlockSpec((seq, HEAD), lambda h: (0, h))
    tf32, tb16 = pltpu.VMEM((t, HEAD), F32), pltpu.VMEM((t, HEAD), BF16)
    states = pltpu.VMEM((nc, HEAD, HEAD), F32)
    return pl.pallas_call(
        body, name="hgrn_bwd", grid=(d // HEAD,),
        in_specs=[sec(0), sec(1), sec(2), sec(3), sec(4),
                  pl.BlockSpec((2, 3, HEAD), lambda h: (0, 0, h)), pl.BlockSpec((1, HEAD), lambda h: (0, h)),
                  col, col, pl.BlockSpec((2, t, HEAD), lambda h: (0, 0, h)),
                  pl.BlockSpec((2, t, HEAD), lambda h: (0, 0, h)),
                  pl.BlockSpec((2, nc, 8, HEAD), lambda h: (0, 0, 0, h))],
        out_specs=[pl.BlockSpec((5, t, HEAD), lambda h: (0, 0, h)),
                   pl.BlockSpec((1, HEAD), lambda h: (0, h)), pl.BlockSpec((2, HEAD), lambda h: (0, h))],
        out_shape=[jax.ShapeDtypeStruct((5, t, d), BF16), jax.ShapeDtypeStruct((1, d), F32),
                   jax.ShapeDtypeStruct((2, d), F32)],
        scratch_shapes=[tb16, tb16, tb16, tf32, tf32, tf32, tf32, states, states],
        compiler_params=_cp(),
    )(z1, z1, z1, z1, z1, lbl, onorm, o, dr_out, bcs, ks, decs)


def _place():
    x, y, c = lax.axis_index("x"), lax.axis_index("y"), lax.axis_index("c")
    chips = [(1 - x, y), (x, 1 - y), (1 - x, 1 - y)]
    return x, y, c, chips


def _relay_chips():
    x, y, c, _ = _place()
    first = c == 0
    near = (jnp.where(first, 1 - x, x), jnp.where(first, y, 1 - y))
    far = (jnp.where(first, x, 1 - x), jnp.where(first, 1 - y, y))
    return near, far, (1 - x, 1 - y)


def _cast_rows(src_ref, dst_ref, bufs):
    fbuf, bbuf, load_sems, store_sems = bufs
    tr = fbuf.shape[1]
    nblk = src_ref.shape[0] // tr

    def load(i, s):
        return pltpu.make_async_copy(src_ref.at[pl.ds(i * tr, tr)], fbuf.at[s], load_sems.at[s])

    def store(i, s):
        return pltpu.make_async_copy(bbuf.at[s], dst_ref.at[pl.ds(i * tr, tr)], store_sems.at[s])

    load(0, 0).start()

    def step(i, carry):
        s = i % 2
        load(i, s).wait()

        @pl.when(i + 1 < nblk)
        def _():
            load(i + 1, 1 - s).start()

        @pl.when(i >= 2)
        def _():
            store(i - 2, s).wait()

        bbuf[s] = fbuf[s].astype(BF16)
        store(i, s).start()
        return carry

    lax.fori_loop(0, nblk, step, 0)
    for i in range(max(nblk - 2, 0), nblk):
        store(i, i % 2).wait()


def allgather_shards(bufs, after, casts):
    n, m = len(bufs), len(casts)
    cast_rows = [256 if a.shape[0] % 256 == 0 else a.shape[0] for a in casts]

    def body(*refs):
        cast_src = refs[n + 1:n + 1 + m]
        outs = refs[n + 1 + m:2 * n + 1 + m]
        done_ref = refs[2 * n + 1 + m]
        cast_dst = refs[2 * n + 2 + m:2 * n + 2 + 2 * m]
        send_sems, recv_sems = refs[2 * n + 2 + 2 * m:2 * n + 4 + 2 * m]
        cast_bufs = refs[2 * n + 4 + 2 * m:]
        done_ref[...] = jnp.zeros((8, 128), F32)
        x, y, c, _ = _place()
        me = (x, y, c)
        p = 2 * x + y
        near, far, diag = _relay_chips()
        half = [pl.ds(c * (s.shape[1] // 2), s.shape[1] // 2) for s in bufs]
        other = [pl.ds((1 - c) * (s.shape[1] // 2), s.shape[1] // 2) for s in bufs]
        slot = lambda chip: 2 * chip[0] + chip[1]

        def remote(i, k, ref, to):
            return pltpu.make_async_remote_copy(src_ref=ref, dst_ref=ref, send_sem=send_sems.at[6 * i + k],
                                                recv_sem=recv_sems.at[6 * i + k], device_id=to, device_id_type=MESH)

        sends = []

        def send(i, k, ref, to):
            cp = remote(i, k, ref, to)
            cp.start()
            sends.append(cp)

        for i in range(n):
            mine = outs[i].at[p, half[i]]
            send(i, 0, mine, (*near, c))
            send(i, 1, mine, (*far, c))
        for j in range(m):
            _cast_rows(cast_src[j], cast_dst[j].at[p], cast_bufs[4 * j:4 * j + 4])
        for i in range(n):
            landed = outs[i].at[slot(near), half[i]]
            remote(i, 0, landed, me).wait_recv()
            send(i, 2, landed, (*far, c))
            send(i, 3, landed, (x, y, 1 - c))
        for i in range(n):
            landed = outs[i].at[slot(far), half[i]]
            remote(i, 1, landed, me).wait_recv()
            send(i, 4, landed, (x, y, 1 - c))
        for i in range(n):
            landed = outs[i].at[slot(diag), half[i]]
            remote(i, 2, landed, me).wait_recv()
            send(i, 5, landed, (x, y, 1 - c))
        for i in range(n):
            for k, chip in ((3, far), (4, near), (5, diag)):
                remote(i, k, outs[i].at[slot(chip), other[i]], me).wait_recv()
        for cp in sends:
            cp.wait_send()

    return pl.pallas_call(
        body, name="allgather_shards",
        in_specs=[ANY] * (n + 1 + m), out_specs=[ANY] * n + [VMEM] + [ANY] * m,
        out_shape=[jax.ShapeDtypeStruct(s.shape, s.dtype) for s in bufs] + [jax.ShapeDtypeStruct((8, 128), F32)]
        + [jax.ShapeDtypeStruct((4,) + a.shape, BF16) for a in casts],
        input_output_aliases={i: i for i in range(n)},
        scratch_shapes=[pltpu.SemaphoreType.DMA((6 * n,)), pltpu.SemaphoreType.DMA((6 * n,))] + [
            s for a, tr in zip(casts, cast_rows) for s in (
                pltpu.VMEM((2, tr, a.shape[1]), F32), pltpu.VMEM((2, tr, a.shape[1]), BF16),
                pltpu.SemaphoreType.DMA((2,)), pltpu.SemaphoreType.DMA((2,)))],
        compiler_params=_cp(has_side_effects=True),
    )(*bufs, after, *casts)


def pair_sum(grad, got, chip_core):
    _, r, cc = grad.shape
    hr = r // 2
    tr = 256 if hr % 256 == 0 else hr
    nb = hr // tr

    def body(cc_ref, a_ref, b_ref, own_ref, sb_ref):
        s = a_ref[...].astype(F32) + b_ref[...].astype(F32)
        sb_ref[...] = s.astype(BF16)

        @pl.when(pl.program_id(1) == cc_ref[0])
        def _():
            own_ref[...] = s

    grid_spec = pltpu.PrefetchScalarGridSpec(
        num_scalar_prefetch=1, grid=(nb, 4),
        in_specs=[pl.BlockSpec((None, tr, cc), lambda i, qi, cc_ref: (qi, cc_ref[1] * nb + i, 0)),
                  pl.BlockSpec((None, tr, cc), lambda i, qi, cc_ref: (qi, i, 0))],
        out_specs=[pl.BlockSpec((tr, cc), lambda i, qi, cc_ref: (i, 0)),
                   pl.BlockSpec((None, tr, cc), lambda i, qi, cc_ref: (qi, i, 0))])
    return pl.pallas_call(
        body, name="pair_sum", grid_spec=grid_spec,
        out_shape=[jax.ShapeDtypeStruct((hr, cc), F32), jax.ShapeDtypeStruct((4, hr, cc), BF16)],
        compiler_params=_cp(),
    )(chip_core, grad, got)


def owner_sum(own, got, chip_core):
    hr, cc = own.shape
    tr = 256 if hr % 256 == 0 else hr
    nb = hr // tr

    def body(cc_ref, a_ref, b_ref, o_ref):
        s = a_ref[...] + b_ref[0].astype(F32)
        s = s + b_ref[1].astype(F32)
        o_ref[...] = s + b_ref[2].astype(F32)

    grid_spec = pltpu.PrefetchScalarGridSpec(
        num_scalar_prefetch=1, grid=(nb,),
        in_specs=[pl.BlockSpec((tr, cc), lambda i, cc_ref: (i, 0)),
                  pl.BlockSpec((3, tr, cc), lambda i, cc_ref: (0, i, 0))],
        out_specs=pl.BlockSpec((tr, cc), lambda i, cc_ref: (cc_ref[1] * nb + i, 0)))
    return pl.pallas_call(
        body, name="owner_sum", grid_spec=grid_spec,
        out_shape=jax.ShapeDtypeStruct((2 * hr, cc), F32), compiler_params=_cp(),
    )(chip_core, own, got)


def share_halves(bufs):
    n = len(bufs)

    def body(*refs):
        outs = refs[n:2 * n]
        send_sems, recv_sems = refs[2 * n:]
        x, y, c, _ = _place()
        copies = []
        for i in range(n):
            hr = bufs[i].shape[0] // 2
            mine = outs[i].at[pl.ds(c * hr, hr)]
            cp = pltpu.make_async_remote_copy(
                src_ref=mine, dst_ref=mine, send_sem=send_sems.at[i], recv_sem=recv_sems.at[i],
                device_id=(x, y, 1 - c), device_id_type=MESH)
            cp.start()
            copies.append((cp, outs[i].at[pl.ds((1 - c) * hr, hr)]))
        for i, (cp, theirs) in enumerate(copies):
            cp.wait_send()
            pltpu.make_async_remote_copy(
                src_ref=theirs, dst_ref=theirs, send_sem=send_sems.at[i], recv_sem=recv_sems.at[i],
                device_id=(x, y, c), device_id_type=MESH).wait_recv()

    return pl.pallas_call(
        body, name="share_halves",
        in_specs=[ANY] * n, out_specs=[ANY] * n,
        out_shape=[jax.ShapeDtypeStruct(b.shape, b.dtype) for b in bufs],
        input_output_aliases={i: i for i in range(n)},
        scratch_shapes=[pltpu.SemaphoreType.DMA((n,)), pltpu.SemaphoreType.DMA((n,))],
        compiler_params=pltpu.CompilerParams(has_side_effects=True),
    )(*bufs)


def allgather8(v, name, per_peer=False):
    r, n = v.shape[-2:]

    def body(v_ref, out_ref, send_sems, recv_sems):
        x, y, c, _ = _place()
        me = 4 * x + 2 * y + c
        out_ref[me] = v_ref[me] if per_peer else v_ref[...]

        def copy(k, peer_index, slot, to):
            return pltpu.make_async_remote_copy(
                src_ref=v_ref.at[peer_index] if per_peer else v_ref, dst_ref=out_ref.at[slot],
                send_sem=send_sems.at[k - 1], recv_sem=recv_sems.at[k - 1], device_id=to, device_id_type=MESH)

        peers = []
        for k in range(1, 8):
            px = 1 - x if (k >> 2) & 1 else x
            py = 1 - y if (k >> 1) & 1 else y
            pc = 1 - c if k & 1 else c
            peers.append((px, py, pc))
            copy(k, 4 * px + 2 * py + pc, me, (px, py, pc)).start()
        for k, (px, py, pc) in enumerate(peers, start=1):
            copy(k, me, 4 * px + 2 * py + pc, (x, y, c)).wait_recv()
        for k, (px, py, pc) in enumerate(peers, start=1):
            copy(k, 4 * px + 2 * py + pc, me, (px, py, pc)).wait_send()

    return pl.pallas_call(
        body, name=name, in_specs=[VMEM], out_specs=VMEM,
        out_shape=jax.ShapeDtypeStruct((8, r, n), v.dtype),
        scratch_shapes=[pltpu.SemaphoreType.DMA((7,)), pltpu.SemaphoreType.DMA((7,))],
        compiler_params=_cp(has_side_effects=True),
    )(v)


HBM = pl.BlockSpec(memory_space=pltpu.HBM)
SEM = pl.BlockSpec(memory_space=pltpu.SEMAPHORE)
DATAFLOW = pltpu.SideEffectType.DATAFLOW_SIDE_EFFECTING


def _descriptors(plan, refs, send_sems, recv_sems, arrivals=True):
    x, y, c, _ = _place()
    sends, recvs = plan(refs)
    out = [pltpu.make_async_remote_copy(src_ref=src, dst_ref=dst, send_sem=send_sems.at[k],
                                        recv_sem=recv_sems.at[k], device_id=to, device_id_type=MESH)
           for k, (src, dst, to) in enumerate(sends)]
    if not arrivals:
        return out, []
    inn = [pltpu.make_async_remote_copy(src_ref=land, dst_ref=land, send_sem=send_sems.at[k],
                                        recv_sem=recv_sems.at[k], device_id=(x, y, c), device_id_type=MESH)
           for k, land in enumerate(recvs)]
    return out, inn


def copies_start(name, arrays, n_copies, plan, after):
    na = len(arrays)

    def body(*refs):
        out, _ = _descriptors(plan, refs[:na], refs[na + 1], refs[na + 2], arrivals=False)
        for cp in out:
            cp.start()
        refs[-1][...] = jnp.zeros((8, 128), F32)

    res = pl.pallas_call(
        body, name=name,
        out_shape=(pltpu.SemaphoreType.DMA((n_copies,)), pltpu.SemaphoreType.DMA((n_copies,)),
                   *[pltpu.HBM(a.shape, a.dtype) for a in arrays], jax.ShapeDtypeStruct((8, 128), F32)),
        in_specs=[HBM] * na + [ANY], out_specs=(SEM, SEM, *[HBM] * na, VMEM),
        input_output_aliases={i: i + 2 for i in range(na)},
        compiler_params=pltpu.CompilerParams(has_side_effects=DATAFLOW),
    )(*[pltpu.with_memory_space_constraint(a, pltpu.HBM) for a in arrays], after)
    return res[0], res[1], list(res[2:2 + na]), res[-1]


def copies_wait(name, started, plan, after):
    send_sems, recv_sems, arrays, _ = started
    na = len(arrays)
    after = list(after) if isinstance(after, (list, tuple)) else [after]

    def body(*refs):
        out, inn = _descriptors(plan, refs[:na], refs[na], refs[na + 1])
        for cp in out:
            cp.wait_send()
        for cp in inn:
            cp.wait_recv()
        refs[-1][...] = jnp.zeros((8, 128), F32)

    res = pl.pallas_call(
        body, name=name,
        out_shape=(*[pltpu.HBM(a.shape, a.dtype) for a in arrays], jax.ShapeDtypeStruct((8, 128), F32)),
        in_specs=[HBM] * na + [SEM, SEM] + [ANY] * len(after), out_specs=(*[HBM] * na, VMEM),
        input_output_aliases={i: i for i in range(na)},
        compiler_params=pltpu.CompilerParams(has_side_effects=DATAFLOW),
    )(*arrays, send_sems, recv_sems, *after)
    return list(res[:na]), res[-1]


def _rows_half(r, c):
    return pl.ds(c * (r // 2), r // 2), pl.ds((1 - c) * (r // 2), r // 2)


def plan_gather_neighbours(refs):
    x, y, c, _ = _place()
    p = 2 * x + y
    near, far, _ = _relay_chips()
    sends, recvs = [], []
    for buf in refs:
        mine, _ = _rows_half(buf.shape[1], c)
        for chip in (near, far):
            sends.append((buf.at[p, mine], buf.at[p, mine], (*chip, c)))
            recvs.append(buf.at[2 * chip[0] + chip[1], mine])
    return sends, recvs


def plan_gather_relay(refs):
    x, y, c, _ = _place()
    near, far, diag = _relay_chips()
    slot = lambda chip: 2 * chip[0] + chip[1]
    sends, recvs = [], []
    for buf in refs:
        mine, theirs = _rows_half(buf.shape[1], c)
        landed = buf.at[slot(near), mine]
        sends.append((landed, landed, (*far, c)))
        recvs.append(buf.at[slot(diag), mine])
        for sent, got in ((near, far), (far, near)):
            sends.append((buf.at[slot(sent), mine], buf.at[slot(sent), mine], (x, y, 1 - c)))
            recvs.append(buf.at[slot(got), theirs])
    return sends, recvs


def plan_gather_d2d(refs):
    x, y, c, _ = _place()
    _, _, diag = _relay_chips()
    sends, recvs = [], []
    for buf in refs:
        mine, theirs = _rows_half(buf.shape[1], c)
        landed = buf.at[2 * diag[0] + diag[1], mine]
        sends.append((landed, landed, (x, y, 1 - c)))
        recvs.append(buf.at[2 * diag[0] + diag[1], theirs])
    return sends, recvs


def plan_exchange(refs):
    x, y, c, _ = _place()
    n = len(refs) // 2
    sends, recvs = [], []
    for grad, land in zip(refs[:n], refs[n:]):
        _, theirs = _rows_half(grad.shape[1], c)
        sends.append((grad.at[:, theirs], land, (x, y, 1 - c)))
        recvs.append(land)
    return sends, recvs


def plan_scatter(refs):
    x, y, c, chips = _place()
    n = len(refs) // 2
    sends, recvs = [], []
    for part, land in zip(refs[:n], refs[n:]):
        for j, chip in enumerate(chips):
            sends.append((part.at[2 * chip[0] + chip[1]], land.at[j], (*chip, c)))
            recvs.append(land.at[j])
    return sends, recvs


def plan_share(refs):
    x, y, c, _ = _place()
    sends, recvs = [], []
    for buf in refs:
        mine, theirs = _rows_half(buf.shape[0], c)
        sends.append((buf.at[mine], buf.at[mine], (x, y, 1 - c)))
        recvs.append(buf.at[theirs])
    return sends, recvs


def put_in_slot(w, chip, dtype, name):
    r, c = w.shape
    tr = 256 if r % 256 == 0 else r

    def body(chip_ref, w_ref, o_ref):
        o_ref[...] = w_ref[...].astype(dtype)

    grid_spec = pltpu.PrefetchScalarGridSpec(
        num_scalar_prefetch=1, grid=(r // tr,),
        in_specs=[pl.BlockSpec((tr, c), lambda i, chip_ref: (i, 0))],
        out_specs=pl.BlockSpec((None, tr, c), lambda i, chip_ref: (chip_ref[0], i, 0)))
    return pl.pallas_call(body, name=name, grid_spec=grid_spec,
                          out_shape=jax.ShapeDtypeStruct((4, r, c), dtype), compiler_params=_cp())(chip, w)


def ada_fwd(s_in, ada_w, ada_b, tn):
    nl, d, ws = ada_w.shape

    def body(s_ref, w_ref, b_ref, so_ref, mod_ref):
        s = _silu(s_ref[...])
        so_ref[...] = s
        mod_ref[...] = _dot(s.astype(BF16), w_ref[...].astype(BF16)) + b_ref[...]

    return pl.pallas_call(
        body, name="ada_fwd", grid=(nl, ws // tn),
        in_specs=[pl.BlockSpec((16, d), lambda l, j: (0, 0)),
                  pl.BlockSpec((None, d, tn), lambda l, j: (l, 0, j)),
                  pl.BlockSpec((None, 1, tn), lambda l, j: (l, 0, j))],
        out_specs=[pl.BlockSpec((16, d), lambda l, j: (0, 0)),
                   pl.BlockSpec((None, 16, tn), lambda l, j: (l, 0, j))],
        out_shape=[jax.ShapeDtypeStruct((16, d), F32), jax.ShapeDtypeStruct((nl, 16, ws), F32)],
        compiler_params=_cp(),
    )(s_in, ada_w, ada_b)


def _adamw_math(w, g, m, v):
    m = ADAM_B1 * m + (1.0 - ADAM_B1) * g
    v = ADAM_B2 * v + (1.0 - ADAM_B2) * (g * g)
    m_hat = m / (1.0 - ADAM_B1 ** ADAM_STEP)
    v_hat = v / (1.0 - ADAM_B2 ** ADAM_STEP)
    delta = -ADAM_LR * (m_hat / (jnp.sqrt(v_hat) + ADAM_EPS) + ADAM_WD * w)
    return delta, m, v


def ada_bwd_adamw(s, dm, w, m, v):
    nl, d, ws = w.shape
    tr = 256 if d % 256 == 0 else 128

    def body(s_ref, dm_ref, w_ref, m_ref, v_ref, g_ref, dl_ref, mo_ref, vo_ref, dc_ref):
        dmv = dm_ref[...].astype(BF16)
        wv = w_ref[...]
        g = _dot(s_ref[...].astype(BF16), dmv, TN)
        g_ref[...] = g
        dl_ref[...], mo_ref[...], vo_ref[...] = _adamw_math(wv, g, m_ref[...], v_ref[...])
        dc_ref[...] = _dot(dmv[8:16, :], wv.astype(BF16), NT)

    wblk = pl.BlockSpec((None, tr, ws), lambda l, i: (l, i, 0))
    wshape = jax.ShapeDtypeStruct((nl, d, ws), F32)
    return pl.pallas_call(
        body, name="ada_bwd_adamw", grid=(nl, d // tr),
        in_specs=[pl.BlockSpec((16, tr), lambda l, i: (0, i)),
                  pl.BlockSpec((None, 16, ws), lambda l, i: (l, 0, 0)), wblk, wblk, wblk],
        out_specs=[wblk, wblk, wblk, wblk, pl.BlockSpec((None, 8, tr), lambda l, i: (l, 0, i))],
        out_shape=[wshape, wshape, wshape, wshape, jax.ShapeDtypeStruct((nl, 8, d), F32)],
        compiler_params=_cp(),
    )(s, dm, w, m, v)


def adamw(w, g, m, v, name, with_grad=False):
    r, c = w.shape
    tr = 256 if r % 256 == 0 else r

    def body(w_ref, g_ref, m_ref, v_ref, dl_ref, mo_ref, vo_ref, *g_out):
        gv = g_ref[...]
        dl_ref[...], mo_ref[...], vo_ref[...] = _adamw_math(w_ref[...], gv, m_ref[...], v_ref[...])
        if with_grad:
            g_out[0][...] = gv

    blk = pl.BlockSpec((tr, c), lambda i: (i, 0))
    shape = jax.ShapeDtypeStruct((r, c), F32)
    n_out = 4 if with_grad else 3
    return pl.pallas_call(body, name=name, grid=(r // tr,), in_specs=[blk] * 4, out_specs=[blk] * n_out,
                          out_shape=[shape] * n_out, compiler_params=_cp())(w, g, m, v)


ROW_MOD = 10


def small_reduce(gathered):
    _, rows, d = gathered.shape

    def body(g_ref, o_ref):
        tot = g_ref[0]
        for b in range(1, 8):
            tot = tot + g_ref[b]
        o_ref[0:rows, :] = tot
        for layer in range(2):
            lat = ROW_MOD + 6 * layer
            o_ref[24 + 3 * layer:27 + 3 * layer, :] = tot[lat:lat + 3, :] + tot[lat + 3:lat + 6, :]
        o_ref[30:32, :] = jnp.zeros((2, d), F32)

    return pl.pallas_call(body, name="small_reduce", in_specs=[VMEM], out_specs=VMEM,
                          out_shape=jax.ShapeDtypeStruct((32, d), F32), compiler_params=_cp())(gathered)


def lb_logits_grad(lbl, dlb):
    _, _, n = lbl.shape

    def body(l_ref, d_ref, o_ref):
        for dr in range(2):
            _, (p0, p1, p2) = _lower_bound(l_ref, dr)
            dv = d_ref[dr:dr + 1, :]
            o_ref[dr, 0:1, :] = p0 * p2 * dv
            o_ref[dr, 1:2, :] = p1 * p2 * dv
            o_ref[dr, 2:3, :] = -p2 * (p0 + p1) * dv

    return pl.pallas_call(body, name="lb_logits_grad", in_specs=[VMEM, VMEM], out_specs=VMEM,
                          out_shape=jax.ShapeDtypeStruct((2, 3, n), F32), compiler_params=_cp())(lbl, dlb)


def c_ctx_grad(parts, c_ctx):
    d = c_ctx.shape[1]

    def body(p_ref, c_ref, o_ref):
        tot = p_ref[0, 0:1, :]
        for chip in range(1, 4):
            tot = tot + p_ref[2 * chip, 0:1, :]
        o_ref[...] = tot * _dsilu(c_ref[...])

    return pl.pallas_call(body, name="c_ctx_grad", in_specs=[VMEM, VMEM], out_specs=VMEM,
                          out_shape=jax.ShapeDtypeStruct((1, d), F32), compiler_params=_cp())(parts, c_ctx)


def kernel(x, c, ctx, c_ctx, ada_w, ada_b, pre_g, post_g, ev_w_in, ev_pool_w, ev_pool_scale, ev_conv_w, ev_conv_b, ev_w_out, od_w_in, od_onorm_g, od_w_out, lb_logits, loss_target, m_c_ctx, m_ada_w, m_ada_b, m_pre_g, m_post_g, m_ev_w_in, m_ev_pool_w, m_ev_pool_scale, m_ev_conv_w, m_ev_conv_b, m_ev_w_out, m_od_w_in, m_od_onorm_g, m_od_w_out, m_lb_logits, v_c_ctx, v_ada_w, v_ada_b, v_pre_g, v_post_g, v_ev_w_in, v_ev_pool_w, v_ev_pool_scale, v_ev_conv_w, v_ev_conv_b, v_ev_w_out, v_od_w_in, v_od_onorm_g, v_od_w_out, v_lb_logits):
    _, seq, d = x.shape
    cx = ctx.shape[1]
    t = cx + seq
    half_d = d // 2
    g = half_d // N_POOL
    tn = d // 4
    xi, yi, ci = lax.axis_index("x"), lax.axis_index("y"), lax.axis_index("c")
    chip = 2 * xi + yi
    chip_arr = jnp.reshape(chip, (1,)).astype(jnp.int32)
    chip_core_arr = jnp.stack([chip, ci]).astype(jnp.int32)

    c_rows = jnp.concatenate([c, jnp.zeros((7, d), F32)], axis=0)
    c_all = allgather8(c_rows, "allgather_c")[:, 0, :]
    s_in = jnp.concatenate([c_all, c_ctx.reshape(1, d), jnp.zeros((7, d), F32)], axis=0)
    ws_ada = ada_w.shape[2]
    ada_b_mine = lax.dynamic_slice(ada_b, (0, chip * ws_ada), (2, ws_ada)).reshape(2, 1, ws_ada)
    s_act, mod_mine = ada_fwd(s_in, ada_w, ada_b_mine, tn)
    mod_rows = jnp.concatenate([
        mod_mine[:, :8].transpose(1, 0, 2), jnp.broadcast_to(mod_mine[:, 8][None], (8, 2, ws_ada)),
        jnp.zeros((8, 4, ws_ada), F32)], axis=1)
    mod_all = allgather8(mod_rows, "exchange_mod", per_peer=True)

    pad = lambda a, rows: jnp.concatenate([a, jnp.zeros((rows - a.shape[0], g), F32)], axis=0)
    small = jnp.concatenate([
        ev_pool_w.reshape(g, g), pad(ev_conv_w.reshape(3, g), 8), pad(od_onorm_g.reshape(2, g), 8),
        pad(lb_logits.reshape(12, g), 16)], axis=0)
    ev_in_g, ev_out_g, small_g, ev_done, od_in_mine, od_out_mine = allgather_shards([
        put_in_slot(ev_w_in[0], chip_arr, BF16, "cast_ev_w_in"),
        put_in_slot(ev_w_out[0], chip_arr, BF16, "cast_ev_w_out"),
        put_in_slot(small, chip_arr, F32, "place_small")], mod_all, [od_w_in[0], od_w_out[0]])
    od_ici = copies_start("gather_od_ici_start", [od_in_mine, od_out_mine], 4, plan_gather_neighbours, ev_done)
    ev_out3 = ev_out_g.reshape(1, d, d)
    pool_w_full = small_g[:, :g].reshape(4, N_POOL, g // 4, g).transpose(1, 0, 2, 3).reshape(N_POOL, g, g)
    conv_w_full = small_g[:, g:g + 3].transpose(1, 0, 2).reshape(3, half_d)
    onorm_full = small_g[:, g + 8:g + 10].reshape(1, d)
    lbl_full = small_g[:, g + 16:g + 28].reshape(4, 2, 3, 2 * g).transpose(1, 2, 0, 3).reshape(2, 3, d)

    by_chip = mod_all[0::2]
    mods = jnp.stack([by_chip[:, 2:4], by_chip[:, 0:2]]).transpose(2, 0, 1, 3).reshape(2, 2, 3 * d)
    shift, scale, gate = mods[:, :, :d], mods[:, :, d:2 * d], mods[:, :, 2 * d:]

    h0, xs = normmod_fwd_joining(ctx[0], x[0], pre_g[0:1] + od_ici[3][0:1, 0:1], shift[0], scale[0])
    z0 = mm_nn(h0, ev_in_g, half_d, tn, "mm_ev_in")
    u = mix_b_fwd(z0, conv_w_full, ev_conv_b, mix_a_fwd(z0, pool_w_full, ev_pool_scale, cx), cx)
    od_relay = copies_start("gather_od_relay_start",
                            copies_wait("gather_od_ici_wait", od_ici, plan_gather_neighbours, u)[0],
                            6, plan_gather_relay, u)
    y0 = mm_nn(u, ev_out3, d, tn, "mm_ev_out")[0]
    xs1, h1 = post_fwd_norm(xs, y0, post_g[0:1] + od_relay[3][0:1, 0:1], gate[0],
                            pre_g[1:2], shift[1], scale[1], cx)
    od_d2d = copies_start("gather_od_d2d_start",
                          copies_wait("gather_od_relay_wait", od_relay, plan_gather_relay, xs1)[0],
                          2, plan_gather_d2d, xs1)
    (od_in_g, od_out_g), _ = copies_wait("gather_od_d2d_wait", od_d2d, plan_gather_d2d, od_d2d[3])
    od_out3 = od_out_g.reshape(1, d, d)

    z1 = mm_nn(h1, od_in_g, d, tn, "mm_od_in")
    o1, r1, bcs1, ks1, decs1 = hgrn_fwd(z1, lbl_full, onorm_full, cx)
    y1 = mm_nn(r1, od_out3, d, tn, "mm_od_out")[0]
    sq, dx2, dy1, dgate1, dpost1 = post_loss(xs1, y1, post_g[1:2], gate[1], loss_target[0], cx)

    dr1 = mm_nt(dy1[None], None, od_out3, tn, "mm_od_out_dx")
    g_od_out = mm_tn(r1, dy1[None], None, d, tn, "mm_od_out_dw")
    dz1, donorm, dlb = hgrn_bwd(z1, lbl_full, onorm_full, o1, dr1, bcs1, ks1, decs1, cx)
    dh1 = mm_nt(dz1, None, od_in_g, tn, "mm_od_in_dx")
    g_od_in = mm_tn(h1, dz1, None, od_in_g.shape[2], tn, "mm_od_in_dw")
    od_grads = [g_od_in, g_od_out.reshape(4, d // 4, d)]
    half_zone = lambda a, lead, dt: lax.empty((lead, a.shape[1] // 2, a.shape[2]), dt)
    od_ex = copies_start("reduce_od_exchange_start", od_grads + [half_zone(a, 4, a.dtype) for a in od_grads],
                         2, plan_exchange, dh1)

    dxs1, dpre1, dshift1, dscale1, dy0, dgate0, dpost0 = normmod_bwd(
        xs1, dh1, pre_g[1:2] + od_ex[3][0:1, 0:1], scale[1], dx2, cx, True,
        prev=(y0, post_g[0:1], gate[0]))
    du = mm_nt(dy0[None], None, ev_out3, tn, "mm_ev_out_dx")
    g_ev_out = mm_tn(u, dy0[None], None, d, tn, "mm_ev_out_dw")
    od_got, _ = copies_wait("reduce_od_exchange_wait", od_ex, plan_exchange, g_ev_out)
    od_sums = [pair_sum(od_got[i], od_got[2 + i], chip_core_arr) for i in range(2)]
    od_sc = copies_start("reduce_od_scatter_start",
                         [sb for _, sb in od_sums] + [half_zone(a, 3, BF16) for a in od_grads],
                         6, plan_scatter, du)
    dz0a, g_pool_w, dpool_scale = mix_a_bwd(z0, du, pool_w_full, ev_pool_scale + od_sc[3][0:1, 0:1], cx)
    dz0b, dconv_w, dconv_b = mix_b_bwd(z0, du, conv_w_full, ev_conv_b + od_sc[3][0:1, 0:1], cx)
    g_ev_in = mm_tn(h0, dz0a, dz0b, ev_in_g.shape[2], tn, "mm_ev_in_dw")
    ev_grads = [g_ev_in, g_ev_out.reshape(4, d // 4, d), g_pool_w.reshape(4, g, g)]
    ev_ex = copies_start("reduce_ev_exchange_start", ev_grads + [half_zone(a, 4, a.dtype) for a in ev_grads],
                         3, plan_exchange, dpool_scale)
    dh0 = mm_nt(dz0a, dz0b, ev_in_g, tn, "mm_ev_in_dx")
    dxs0, dpre0, dshift0, dscale0 = normmod_bwd(xs, dh0, pre_g[0:1] + ev_ex[3][0:1, 0:1], scale[0], dxs1,
                                                cx, False, True)
    grad_x = dxs0[None]
    ev_got, _ = copies_wait("reduce_ev_exchange_wait", ev_ex, plan_exchange, dxs0)
    ev_sums = [pair_sum(ev_got[i], ev_got[3 + i], chip_core_arr) for i in range(3)]
    od_recv, _ = copies_wait("reduce_od_scatter_wait", od_sc, plan_scatter, dxs0)

    zrow = jnp.zeros((1, d), F32)
    small_rows = jnp.concatenate([
        dpre0, dpre1, dpost0, dpost1,
        jnp.concatenate([dpool_scale, dconv_b], axis=1),
        jnp.concatenate([dconv_w.reshape(1, 3 * half_d), jnp.zeros((1, half_d), F32)], axis=1).reshape(2, d),
        donorm, dlb,
        dshift0[1:2], dscale0[1:2], dgate0[1:2], dshift0[0:1], dscale0[0:1], dgate0[0:1],
        dshift1[1:2], dscale1[1:2], dgate1[1:2], dshift1[0:1], dscale1[0:1], zrow,
        jnp.concatenate([sq[0:1], jnp.zeros((1, d - 128), F32)], axis=1),
        zrow], axis=0)
    small_all = allgather8(small_rows, "allgather_small")
    ev_sc = copies_start("reduce_ev_scatter_start",
                         [sb for _, sb in ev_sums] + [half_zone(a, 3, BF16) for a in ev_grads],
                         9, plan_scatter, small_all)
    od_sh = copies_start("reduce_od_share_start",
                         [owner_sum(od_sums[i][0], od_recv[2 + i], chip_core_arr) for i in range(2)],
                         2, plan_share, dxs0)
    tot = small_reduce(small_all + ev_sc[3][0:1, 0:1])
    loss = tot[22, 0] * (0.5 / d)

    dm_rows = []
    for layer in range(2):
        lat = ROW_MOD + 6 * layer
        dm_lat = small_all[:, lat:lat + 3].reshape(8, 3 * d)
        dm_ctx = tot[lat + 3:lat + 6].reshape(1, 3 * d)
        dm_rows.append(jnp.concatenate([dm_lat, dm_ctx, jnp.zeros((7, 3 * d), F32)], axis=0))
    dm_full = jnp.stack(dm_rows)
    dm_mine = lax.dynamic_slice(dm_full, (0, 0, chip * ws_ada), (2, 16, ws_ada))

    def step(w, gr, m, v, name, with_grad=False):
        shape = w.shape
        cols = shape[-1]
        two_d = lambda a: a.reshape(-1, cols)
        res = adamw(two_d(w), two_d(gr), two_d(m), two_d(v), "adamw_" + name, with_grad)
        return tuple(a.reshape(shape) for a in res)

    grad_ada_b = tot[24:30].reshape(2, 3 * d)
    grad_pre_g = tot[0:2]
    grad_post_g = tot[2:4]
    grad_ev_pool_scale = tot[4:5, :half_d]
    grad_ev_conv_b = tot[4:5, half_d:]
    conv_w_tot = tot[5:7].reshape(1, 2 * d)[:, :3 * half_d].reshape(3, N_POOL, g)
    grad_ev_conv_w = lax.dynamic_slice(conv_w_tot, (0, chip, 0), (3, 1, g)).reshape(1, 3, g)
    grad_od_onorm_g = lax.dynamic_slice(tot[7:8], (0, chip * 2 * g), (1, 2 * g))
    dlb_mine = lax.dynamic_slice(tot[8:10], (0, chip * 2 * g), (2, 2 * g))
    grad_lb_logits = lb_logits_grad(lb_logits, dlb_mine)
    upd = {
        "ada_b": step(ada_b, grad_ada_b, m_ada_b, v_ada_b, "ada_b"),
        "pre_g": step(pre_g, grad_pre_g, m_pre_g, v_pre_g, "pre_g"),
        "post_g": step(post_g, grad_post_g, m_post_g, v_post_g, "post_g"),
        "ev_pool_scale": step(ev_pool_scale, grad_ev_pool_scale, m_ev_pool_scale, v_ev_pool_scale, "ev_pool_scale"),
        "ev_conv_w": step(ev_conv_w, grad_ev_conv_w, m_ev_conv_w, v_ev_conv_w, "ev_conv_w"),
        "ev_conv_b": step(ev_conv_b, grad_ev_conv_b, m_ev_conv_b, v_ev_conv_b, "ev_conv_b"),
        "od_onorm_g": step(od_onorm_g, grad_od_onorm_g, m_od_onorm_g, v_od_onorm_g, "od_onorm_g"),
        "lb_logits": step(lb_logits, grad_lb_logits, m_lb_logits, v_lb_logits, "lb_logits"),
    }
    grad_ada_w, delta_ada_w, new_m_ada_w, new_v_ada_w, dctx_part = ada_bwd_adamw(
        s_act, dm_mine, ada_w, m_ada_w, v_ada_w)
    upd["ada_w"] = (delta_ada_w, new_m_ada_w, new_v_ada_w)
    (grad_od_w_in, grad_od_w_out), _ = copies_wait("reduce_od_share_wait", od_sh, plan_share, ev_sc[3])
    upd["od_w_in"] = step(od_w_in, grad_od_w_in[None], m_od_w_in, v_od_w_in, "od_w_in", True)
    upd["od_w_out"] = step(od_w_out, grad_od_w_out[None], m_od_w_out, v_od_w_out, "od_w_out", True)
    grad_od_w_in, grad_od_w_out = upd["od_w_in"][3], upd["od_w_out"][3]
    done_behind = [dctx_part] + [upd[k][0] for k in (
        "od_w_in", "od_w_out", "ada_b", "pre_g", "post_g", "ev_pool_scale", "ev_conv_w", "ev_conv_b",
        "od_onorm_g", "lb_logits")]
    ev_recv, ev_landed = copies_wait("reduce_ev_scatter_wait", ev_sc, plan_scatter, done_behind)
    grad_ev_w_in, grad_ev_w_out, grad_pool_w = share_halves(
        [owner_sum(ev_sums[i][0], ev_recv[3 + i], chip_core_arr) for i in range(3)])
    dctx_all = allgather8(dctx_part[0] + dctx_part[1] + ev_landed[0:1, 0:1], "allgather_dctx")
    grad_c_ctx = c_ctx_grad(dctx_all, c_ctx.reshape(1, d)).reshape(d)
    upd["c_ctx"] = step(c_ctx, grad_c_ctx, m_c_ctx, v_c_ctx, "c_ctx")
    upd["ev_w_in"] = step(ev_w_in, grad_ev_w_in[None], m_ev_w_in, v_ev_w_in, "ev_w_in", True)
    upd["ev_pool_w"] = step(ev_pool_w, grad_pool_w.reshape(1, N_POOL, g // 4, g), m_ev_pool_w, v_ev_pool_w,
                            "ev_pool_w", True)
    upd["ev_w_out"] = step(ev_w_out, grad_ev_w_out[None], m_ev_w_out, v_ev_w_out, "ev_w_out", True)
    grad_ev_w_in, grad_ev_pool_w, grad_ev_w_out = upd["ev_w_in"][3], upd["ev_pool_w"][3], upd["ev_w_out"][3]
    names = ["c_ctx", "ada_w", "ada_b", "pre_g", "post_g", "ev_w_in", "ev_pool_w", "ev_pool_scale",
             "ev_conv_w", "ev_conv_b", "ev_w_out", "od_w_in", "od_onorm_g", "od_w_out", "lb_logits"]
    grads = [grad_c_ctx, grad_ada_w, grad_ada_b, grad_pre_g, grad_post_g, grad_ev_w_in, grad_ev_pool_w,
             grad_ev_pool_scale, grad_ev_conv_w, grad_ev_conv_b, grad_ev_w_out, grad_od_w_in,
             grad_od_onorm_g, grad_od_w_out, grad_lb_logits]
    return (loss, grad_x, *grads, *[upd[k][0] for k in names], *[upd[k][1] for k in names],
            *[upd[k][2] for k in names])
```

```python
import jax
import jax.numpy as jnp
from jax import lax
from jax.experimental import pallas as pl
from jax.experimental.pallas import tpu as pltpu

EPS = 1e-6
GRID_W_LOG2 = 6
CHUNK = 64
HEAD = 128
N_POOL = 4
ADAM_LR, ADAM_B1, ADAM_B2, ADAM_EPS, ADAM_WD, ADAM_STEP = 0.001, 0.9, 0.999, 1e-08, 0.01, 10
VMEM_LIMIT = 56 * 1024 * 1024
MESH = pl.DeviceIdType.MESH
F32, BF16 = jnp.float32, jnp.bfloat16
ANY = pl.BlockSpec(memory_space=pl.ANY)
VMEM = pl.BlockSpec(memory_space=pltpu.VMEM)


def _cp(**kw):
    return pltpu.CompilerParams(vmem_limit_bytes=VMEM_LIMIT, **kw)


def _silu(x):
    return x * jax.nn.sigmoid(x)


def _dsilu(x):
    s = jax.nn.sigmoid(x)
    return s * (1.0 + x * (1.0 - s))


def _dot(a, b, dims=((1,), (0,)), precision=None):
    return lax.dot_general(a, b, (dims, ((), ())), preferred_element_type=F32, precision=precision)


NN = ((1,), (0,))
NT = ((1,), (1,))
TN = ((0,), (0,))


def _row_block(cx):
    return 256 if cx % 256 == 0 else 128


def normmod_bwd(xs, dh, g, scale, dres, cx, res_is_latent_only, dx_latent_only=False, prev=None):
    t, d = xs.shape
    tm = _row_block(cx)
    nctx = cx // tm
    n_prev = 0 if prev is None else 3

    def body(x_ref, dh_ref, g_ref, sc_ref, dres_ref, *rest):
        dx_ref, dg_ref, dsh_ref, dsc_ref = rest[n_prev:n_prev + 4]
        i = pl.program_id(0)
        is_ctx = i < nctx

        @pl.when(i == 0)
        def _():
            for ref in rest[n_prev + 1:n_prev + 4] + rest[n_prev + 5:]:
                ref[...] = jnp.zeros_like(ref)

        x = x_ref[...]
        dh = dh_ref[...]
        gv = g_ref[...]
        rstd = lax.rsqrt(jnp.mean(x * x, axis=-1, keepdims=True) + EPS)
        xhat = x * rstd
        sc = jnp.where(is_ctx, sc_ref[0:1, :], sc_ref[1:2, :])
        dsh = jnp.sum(dh, axis=0, keepdims=True)
        dhx = dh * xhat
        dsc = jnp.sum(dhx * gv, axis=0, keepdims=True)
        dg_ref[...] += jnp.sum(dhx * (1.0 + sc), axis=0, keepdims=True)
        zero = jnp.zeros_like(dsh)
        dsh_ref[0:1, :] += jnp.where(is_ctx, dsh, zero)
        dsh_ref[1:2, :] += jnp.where(is_ctx, zero, dsh)
        dsc_ref[0:1, :] += jnp.where(is_ctx, dsc, zero)
        dsc_ref[1:2, :] += jnp.where(is_ctx, zero, dsc)
        dxhat = dh * (gv * (1.0 + sc))
        dx = rstd * (dxhat - xhat * jnp.mean(dxhat * xhat, axis=-1, keepdims=True))
        res = dres_ref[...]
        if res_is_latent_only:
            res = jnp.where(is_ctx, jnp.zeros_like(res), res)
        dxt = dx + res
        dx_ref[...] = dxt
        if prev is not None:
            y_ref, pg_ref, gate_ref = rest[:3]
            dy_ref, dgate_ref, dpg_ref = rest[7:]
            y = y_ref[...]
            pgv = pg_ref[...]
            rstd_y = lax.rsqrt(jnp.mean(y * y, axis=-1, keepdims=True) + EPS)
            yhat = y * rstd_y
            gt = jnp.where(is_ctx, gate_ref[0:1, :], gate_ref[1:2, :])
            dxy = dxt * yhat
            dgt = jnp.sum(dxy * pgv, axis=0, keepdims=True)
            dgate_ref[0:1, :] += jnp.where(is_ctx, dgt, zero)
            dgate_ref[1:2, :] += jnp.where(is_ctx, zero, dgt)
            dpg_ref[...] += jnp.sum(dxy * gt, axis=0, keepdims=True)
            dyhat = dxt * (gt * pgv)
            dy_ref[...] = (rstd_y * (dyhat - yhat * jnp.mean(dyhat * yhat, axis=-1, keepdims=True))).astype(BF16)

    row = pl.BlockSpec((tm, d), lambda i: (i, 0))
    if res_is_latent_only:
        res_spec = pl.BlockSpec((tm, d), lambda i: (jnp.maximum(i - nctx, 0), 0))
    else:
        res_spec = row
    vec = lambda r: pl.BlockSpec((r, d), lambda i: (0, 0))
    dx_spec = pl.BlockSpec((tm, d), lambda i: (jnp.maximum(i - nctx, 0), 0)) if dx_latent_only else row
    in_specs = [row, row, vec(1), vec(2), res_spec]
    out_specs = [dx_spec, vec(1), vec(2), vec(2)]
    out_shape = [jax.ShapeDtypeStruct((t - cx if dx_latent_only else t, d), F32), jax.ShapeDtypeStruct((1, d), F32),
                 jax.ShapeDtypeStruct((2, d), F32), jax.ShapeDtypeStruct((2, d), F32)]
    if prev is not None:
        in_specs += [row, vec(1), vec(2)]
        out_specs += [row, vec(2), vec(1)]
        out_shape += [jax.ShapeDtypeStruct((t, d), BF16), jax.ShapeDtypeStruct((2, d), F32),
                      jax.ShapeDtypeStruct((1, d), F32)]
    return pl.pallas_call(
        body, name="normmod_bwd", grid=(t // tm,), in_specs=in_specs, out_specs=out_specs, out_shape=out_shape,
        compiler_params=_cp(),
    )(xs, dh, g, scale, dres, *(() if prev is None else prev))


def post_fwd_norm(xs, y, pg, gate, g_next, shift_next, scale_next, cx):
    t, d = xs.shape
    tm = _row_block(cx)
    nctx = cx // tm

    def body(x_ref, y_ref, pg_ref, gate_ref, g_ref, sh_ref, sc_ref, o_ref, h_ref):
        is_ctx = pl.program_id(0) < nctx
        pick = lambda ref: jnp.where(is_ctx, ref[0:1, :], ref[1:2, :])
        y = y_ref[...]
        rstd = lax.rsqrt(jnp.mean(y * y, axis=-1, keepdims=True) + EPS)
        x = x_ref[...] + pick(gate_ref) * ((y * rstd) * pg_ref[...])
        o_ref[...] = x
        rstd = lax.rsqrt(jnp.mean(x * x, axis=-1, keepdims=True) + EPS)
        h_ref[...] = ((x * rstd) * g_ref[...] * (1.0 + pick(sc_ref)) + pick(sh_ref)).astype(BF16)

    row = pl.BlockSpec((tm, d), lambda i: (i, 0))
    vec = lambda r: pl.BlockSpec((r, d), lambda i: (0, 0))
    return pl.pallas_call(
        body, name="post_fwd_norm", grid=(t // tm,),
        in_specs=[row, row, vec(1), vec(2), vec(1), vec(2), vec(2)], out_specs=[row, row],
        out_shape=[jax.ShapeDtypeStruct((t, d), F32), jax.ShapeDtypeStruct((t, d), BF16)],
        compiler_params=_cp(),
    )(xs, y, pg, gate, g_next, shift_next, scale_next)


def post_loss(xs, y, pg, gate, target, cx):
    t, d = xs.shape
    n = y.shape[0]
    tm = _row_block(cx)
    nctx = cx // tm

    def body(x_ref, y_ref, pg_ref, gate_ref, tgt_ref, sq_ref, dx_ref, dy_ref, dgate_ref, dpg_ref):
        @pl.when(pl.program_id(0) == 0)
        def _():
            sq_ref[...] = jnp.zeros_like(sq_ref)
            dgate_ref[...] = jnp.zeros_like(dgate_ref)
            dpg_ref[...] = jnp.zeros_like(dpg_ref)

        y = y_ref[...]
        pgv = pg_ref[...]
        gt = gate_ref[1:2, :]
        rstd = lax.rsqrt(jnp.mean(y * y, axis=-1, keepdims=True) + EPS)
        yhat = y * rstd
        err = x_ref[...] + gt * (yhat * pgv) - tgt_ref[...]
        sq_ref[...] += jnp.sum(err * err)
        dx = err * (1.0 / d)
        dx_ref[...] = dx
        dxy = dx * yhat
        dgate_ref[1:2, :] += jnp.sum(dxy * pgv, axis=0, keepdims=True)
        dpg_ref[...] += jnp.sum(dxy * gt, axis=0, keepdims=True)
        dyhat = dx * (gt * pgv)
        dy_ref[...] = (rstd * (dyhat - yhat * jnp.mean(dyhat * yhat, axis=-1, keepdims=True))).astype(BF16)

    row = pl.BlockSpec((tm, d), lambda i: (i, 0))
    xrow = pl.BlockSpec((tm, d), lambda i: (i + nctx, 0))
    vec = lambda r: pl.BlockSpec((r, d), lambda i: (0, 0))
    return pl.pallas_call(
        body, name="post_loss", grid=(n // tm,),
        in_specs=[xrow, row, vec(1), vec(2), row],
        out_specs=[pl.BlockSpec((8, 128), lambda i: (0, 0)), row, row, vec(2), vec(1)],
        out_shape=[jax.ShapeDtypeStruct((8, 128), F32), jax.ShapeDtypeStruct((n, d), F32),
                   jax.ShapeDtypeStruct((n, d), BF16), jax.ShapeDtypeStruct((2, d), F32),
                   jax.ShapeDtypeStruct((1, d), F32)],
        compiler_params=_cp(),
    )(xs, y, pg, gate, target)


def _split_rows(m):
    for cand in (1152, 1024, 768, 512, 384, 256, 128):
        if m % cand == 0 and m // cand >= 2:
            return cand
    return m


def mm_nn(a, w3, sec, tn, name):
    m, k = a.shape
    q, _, ws = w3.shape
    n = q * ws
    tpq, tps = ws // tn, sec // tn
    tm = next(c for c in (768, 512, 256, 128) if m % c == 0)

    def body(a_ref, w_ref, o_ref):
        w = w_ref[...]

        def step(i, carry):
            rows = pl.ds(pl.multiple_of(i * tm, tm), tm)
            o_ref[rows, :] = _dot(a_ref[rows, :], w)
            return carry

        lax.fori_loop(0, m // tm, step, 0)

    return pl.pallas_call(
        body, name=name, grid=(n // tn,),
        in_specs=[pl.BlockSpec((m, k), lambda j: (0, 0)),
                  pl.BlockSpec((None, k, tn), lambda j: (j // tpq, 0, j % tpq))],
        out_specs=pl.BlockSpec((None, m, tn), lambda j: (j // tps, 0, j % tps)),
        out_shape=jax.ShapeDtypeStruct((n // sec, m, sec), F32), compiler_params=_cp(),
    )(a, w3)


def _two_stacks(a3, b3, tn):
    sec = a3.shape[2]
    tps = sec // tn
    n1 = a3.shape[0] * tps
    first = lambda j: (jnp.minimum(j, n1 - 1) // tps, jnp.minimum(j, n1 - 1) % tps)
    second = lambda j: (jnp.maximum(j - n1, 0) // tps, jnp.maximum(j - n1, 0) % tps)
    return n1, first, second


def mm_nt(a3, b3, w3, tn, name):
    if b3 is None:
        b3 = a3
    _, m, sec = a3.shape
    q, k, ws = w3.shape
    n = q * ws
    tpq = ws // tn
    mb = _split_rows(m)
    n1, first, second = _two_stacks(a3, b3, tn)

    def body(a_ref, b_ref, w_ref, o_ref):
        j = pl.program_id(1)

        @pl.when(j == 0)
        def _():
            o_ref[...] = jnp.zeros_like(o_ref)

        @pl.when(j < n1)
        def _():
            o_ref[...] += _dot(a_ref[...], w_ref[...], NT)

        @pl.when(j >= n1)
        def _():
            o_ref[...] += _dot(b_ref[...], w_ref[...], NT)

    return pl.pallas_call(
        body, name=name, grid=(m // mb, n // tn),
        in_specs=[pl.BlockSpec((None, mb, tn), lambda i, j: (first(j)[0], i, first(j)[1])),
                  pl.BlockSpec((None, mb, tn), lambda i, j: (second(j)[0], i, second(j)[1])),
                  pl.BlockSpec((None, k, tn), lambda i, j: (j // tpq, 0, j % tpq))],
        out_specs=pl.BlockSpec((mb, k), lambda i, j: (i, 0)),
        out_shape=jax.ShapeDtypeStruct((m, k), F32), compiler_params=_cp(),
    )(a3, b3, w3)


def mm_tn(a, b3, c3, ws, tn, name):
    m, k = a.shape
    sec = b3.shape[2]
    n = (b3.shape[0] + (0 if c3 is None else c3.shape[0])) * sec
    if c3 is None:
        c3 = b3
    tpq = ws // tn
    kb = 256 if k % 256 == 0 else 128
    n1, first, second = _two_stacks(b3, c3, tn)

    def body(a_ref, b_ref, c_ref, o_ref):
        def product(rhs_ref):
            rhs = rhs_ref[...]
            for i in range(k // kb):
                o_ref[i * kb:(i + 1) * kb, :] = _dot(a_ref[:, i * kb:(i + 1) * kb], rhs, TN).astype(BF16)

        @pl.when(pl.program_id(0) < n1)
        def _():
            product(b_ref)

        @pl.when(pl.program_id(0) >= n1)
        def _():
            product(c_ref)

    return pl.pallas_call(
        body, name=name, grid=(n // tn,),
        in_specs=[pl.BlockSpec((m, k), lambda j: (0, 0)),
                  pl.BlockSpec((None, m, tn), lambda j: (first(j)[0], 0, first(j)[1])),
                  pl.BlockSpec((None, m, tn), lambda j: (second(j)[0], 0, second(j)[1]))],
        out_specs=pl.BlockSpec((None, k, tn), lambda j: (j // tpq, 0, j % tpq)),
        out_shape=jax.ShapeDtypeStruct((n // ws, k, ws), BF16), compiler_params=_cp(),
    )(a, b3, c3)


POOL_REACH = 8 << GRID_W_LOG2


def _token_parts(tok, cx):
    lat = tok - cx
    return tok < cx, lat >> GRID_W_LOG2, lat & ((1 << GRID_W_LOG2) - 1)


def _pool_mask(gi, row0, col0, tm, ncols, cx, transposed):
    half = jnp.left_shift(1, gi)
    r = lax.broadcasted_iota(jnp.int32, (tm, 1), 0) + row0
    c = lax.broadcasted_iota(jnp.int32, (1, ncols), 1) + col0
    out_tok, src_tok = (c, r) if transposed else (r, c)
    o_ctx, o_row, o_col = _token_parts(out_tok, cx)
    s_ctx, s_row, s_col = _token_parts(src_tok, cx)

    def inside(o, s):
        return (s >= o - half) & (s <= o + half - 1)

    ctx_hit = o_ctx & s_ctx & inside(out_tok, src_tok)
    lat_hit = (~o_ctx) & (~s_ctx) & inside(o_row, s_row) & inside(o_col, s_col)
    return jnp.where(ctx_hit | lat_hit, 1.0, 0.0).astype(BF16)


def _pool_inv_count(gi, row0, tm, cx, seq):
    half = jnp.left_shift(1, gi)
    r = lax.broadcasted_iota(jnp.int32, (tm, 1), 0) + row0
    is_ctx, row, col = _token_parts(r, cx)

    def count(pos, size):
        return jnp.minimum(pos + half - 1, size - 1) - jnp.maximum(pos - half, 0) + 1

    cnt = jnp.where(is_ctx, count(r, cx), count(row, seq >> GRID_W_LOG2) * count(col, 1 << GRID_W_LOG2))
    return 1.0 / cnt.astype(F32)


def _lat_band(tm):
    side = POOL_REACH // tm
    return side, 2 * side + 1


def _lat_mask(gi, tm, cx, transposed):
    side, band = _lat_band(tm)
    return _pool_mask(gi, cx + side * tm, cx, tm, band * tm, cx, transposed)


def _store_padded_lat(dst_ref, lat, tm):
    side, _ = _lat_band(tm)
    seq = lat.shape[0]
    zeros = jnp.zeros((side * tm, lat.shape[1]), dst_ref.dtype)
    dst_ref[0:side * tm, :] = zeros
    dst_ref[side * tm + seq:, :] = zeros
    dst_ref[side * tm:side * tm + seq, :] = lat.astype(dst_ref.dtype)


def mix_a_fwd(z0, pool_w, pool_scale, cx):
    _, t, half_d = z0.shape
    g = half_d // N_POOL
    seq = t - cx
    tm = _row_block(cx)
    side, band = _lat_band(tm)

    def body(v_ref, ag_ref, w_ref, sc_ref, u_ref, vlat_ref, mask_ref):
        gi = pl.program_id(0)
        w = w_ref[...].astype(BF16)
        sc = sc_ref[...]
        _store_padded_lat(vlat_ref, v_ref[cx:, :], tm)
        mask_ref[...] = _lat_mask(gi, tm, cx, False)

        def finish(row0, window_sum):
            rows = pl.ds(row0, tm)
            pooled = window_sum * _pool_inv_count(gi, row0, tm, cx, seq) - v_ref[rows, :]
            mixed = _dot(pooled.astype(BF16), w) * sc
            u_ref[rows, :] = (mixed * _silu(ag_ref[rows, :])).astype(BF16)

        vctx = v_ref[0:cx, :].astype(BF16)
        for i in range(cx // tm):
            finish(i * tm, _dot(_pool_mask(gi, i * tm, 0, tm, cx, cx, False), vctx))

        def step(j, carry):
            src = vlat_ref[pl.ds(pl.multiple_of(j * tm, tm), band * tm), :]
            finish(pl.multiple_of(cx + j * tm, tm), _dot(mask_ref[...], src))
            return carry

        lax.fori_loop(0, seq // tm, step, 0)

    sec = lambda s: pl.BlockSpec((None, t, g), lambda j: (s, 0, j))
    return pl.pallas_call(
        body, name="mix_a_fwd", grid=(N_POOL,),
        in_specs=[sec(0), sec(1), pl.BlockSpec((None, g, g), lambda j: (j, 0, 0)),
                  pl.BlockSpec((1, g), lambda j: (0, j))],
        out_specs=pl.BlockSpec((t, g), lambda j: (0, j)),
        out_shape=jax.ShapeDtypeStruct((t, 2 * half_d), BF16),
        scratch_shapes=[pltpu.VMEM((seq + 2 * side * tm, g), BF16), pltpu.VMEM((tm, band * tm), BF16)],
        compiler_params=_cp(),
    )(z0, z0, pool_w, pool_scale)


def mix_a_bwd(z0, du, pool_w, pool_scale, cx):
    _, t, half_d = z0.shape
    g = half_d // N_POOL
    seq = t - cx
    tm = _row_block(cx)
    gq = g // 4
    side, band = _lat_band(tm)

    def body(v_ref, ag_ref, du_ref, w_ref, sc_ref, dz_ref, dw_ref, dsc_ref,
             vlat_ref, mask_ref, pooled_ref, dmx_ref, dpl_ref, wlat_ref, wctx_ref):
        gi = pl.program_id(0)
        w = w_ref[...].astype(BF16)
        sc = sc_ref[...]
        _store_padded_lat(vlat_ref, v_ref[cx:, :], tm)
        _store_padded_lat(wlat_ref, jnp.zeros((seq, g), BF16), tm)
        mask_ref[...] = _lat_mask(gi, tm, cx, False)

        def first(row0, window_sum, weighted_ref, weighted_row0):
            rows = pl.ds(row0, tm)
            inv = _pool_inv_count(gi, row0, tm, cx, seq)
            pooled = (window_sum * inv - v_ref[rows, :]).astype(BF16)
            pooled_ref[rows, :] = pooled
            mixed = _dot(pooled, w)
            ag = ag_ref[rows, :]
            duv = du_ref[rows, :]
            dz_ref[1, rows, :] = (duv * (mixed * sc) * _dsilu(ag)).astype(BF16)
            dms = duv * _silu(ag)
            dmixed = (dms * sc).astype(BF16)
            dmx_ref[rows, :] = dmixed
            dpooled = _dot(dmixed, w, NT)
            dpl_ref[rows, :] = dpooled
            weighted_ref[pl.ds(weighted_row0, tm), :] = (dpooled * inv).astype(BF16)
            return jnp.sum(dms * mixed, axis=0, keepdims=True)

        dsc = jnp.zeros((1, g), F32)
        vctx = v_ref[0:cx, :].astype(BF16)
        for i in range(cx // tm):
            dsc += first(i * tm, _dot(_pool_mask(gi, i * tm, 0, tm, cx, cx, False), vctx), wctx_ref, i * tm)

        def first_lat(j, acc):
            src = vlat_ref[pl.ds(pl.multiple_of(j * tm, tm), band * tm), :]
            return acc + first(pl.multiple_of(cx + j * tm, tm), _dot(mask_ref[...], src),
                               wlat_ref, pl.multiple_of((side + j) * tm, tm))

        dsc_ref[...] = lax.fori_loop(0, seq // tm, first_lat, dsc)
        dw = _dot(pooled_ref[...], dmx_ref[...], TN)
        for qi in range(4):
            dw_ref[qi] = dw[qi * gq:(qi + 1) * gq, :]

        wctx = wctx_ref[...]
        for i in range(cx // tm):
            rows = pl.ds(i * tm, tm)
            dz_ref[0, rows, :] = (_dot(_pool_mask(gi, i * tm, 0, tm, cx, cx, True), wctx)
                                  - dpl_ref[rows, :]).astype(BF16)
        mask_ref[...] = _lat_mask(gi, tm, cx, True)

        def second_lat(j, carry):
            rows = pl.ds(pl.multiple_of(cx + j * tm, tm), tm)
            src = wlat_ref[pl.ds(pl.multiple_of(j * tm, tm), band * tm), :]
            dz_ref[0, rows, :] = (_dot(mask_ref[...], src) - dpl_ref[rows, :]).astype(BF16)
            return carry

        lax.fori_loop(0, seq // tm, second_lat, 0)

    sec = lambda s: pl.BlockSpec((None, t, g), lambda j: (s, 0, j))
    padded = pltpu.VMEM((seq + 2 * side * tm, g), BF16)
    return pl.pallas_call(
        body, name="mix_a_bwd", grid=(N_POOL,),
        in_specs=[sec(0), sec(1), pl.BlockSpec((t, g), lambda j: (0, j)),
                  pl.BlockSpec((None, g, g), lambda j: (j, 0, 0)),
                  pl.BlockSpec((1, g), lambda j: (0, j))],
        out_specs=[pl.BlockSpec((2, t, g), lambda j: (0, 0, j)),
                   pl.BlockSpec((4, None, gq, g), lambda j: (0, j, 0, 0)),
                   pl.BlockSpec((1, g), lambda j: (0, j))],
        out_shape=[jax.ShapeDtypeStruct((2, t, half_d), BF16),
                   jax.ShapeDtypeStruct((4, N_POOL, gq, g), F32),
                   jax.ShapeDtypeStruct((1, half_d), F32)],
        scratch_shapes=[padded, pltpu.VMEM((tm, band * tm), BF16), pltpu.VMEM((t, g), BF16),
                        pltpu.VMEM((t, g), BF16), pltpu.VMEM((t, g), F32), padded, pltpu.VMEM((cx, g), BF16)],
        compiler_params=_cp(),
    )(z0, z0, du, pool_w, pool_scale)


def _conv_masks(t, cx):
    r = lax.broadcasted_iota(jnp.int32, (t, 1), 0)
    has_prev = jnp.where((r == 0) | (r == cx), 0.0, 1.0)
    has_next = jnp.where((r == cx - 1) | (r == t - 1), 0.0, 1.0)
    return has_prev, has_next


def mix_b_fwd(z0, conv_w, conv_b, u, cx):
    _, t, half_d = z0.shape
    gb = 128
    off = half_d // gb

    def body(bx_ref, bb_ref, bc_ref, bg_ref, w_ref, b_ref, _, u_ref):
        has_prev, has_next = _conv_masks(t, cx)
        tt = bc_ref[...] * bx_ref[...]
        prev = pltpu.roll(tt, 1, 0) * has_prev
        nxt = pltpu.roll(tt, t - 1, 0) * has_next
        cv = prev * w_ref[0:1, :] + tt * w_ref[1:2, :] + nxt * w_ref[2:3, :] + b_ref[...]
        u_ref[...] = (bb_ref[...] * cv * _silu(bg_ref[...])).astype(BF16)

    sec = lambda s: pl.BlockSpec((None, t, gb), lambda j: (s, 0, j))
    return pl.pallas_call(
        body, name="mix_b_fwd", grid=(half_d // gb,),
        in_specs=[sec(2), sec(3), sec(4), sec(5), pl.BlockSpec((3, gb), lambda j: (0, j)),
                  pl.BlockSpec((1, gb), lambda j: (0, j)), ANY],
        out_specs=pl.BlockSpec((t, gb), lambda j: (0, j + off)),
        out_shape=jax.ShapeDtypeStruct((t, 2 * half_d), BF16), input_output_aliases={6: 0},
        compiler_params=_cp(),
    )(z0, z0, z0, z0, conv_w, conv_b, u)


def mix_b_bwd(z0, du, conv_w, conv_b, cx):
    _, t, half_d = z0.shape
    gb = 128
    off = half_d // gb

    def body(bx_ref, bb_ref, bc_ref, bg_ref, du_ref, w_ref, b_ref, dz_ref, dw_ref, db_ref):
        has_prev, has_next = _conv_masks(t, cx)
        bx, bb, bc, bg = bx_ref[...], bb_ref[...], bc_ref[...], bg_ref[...]
        duv = du_ref[...]
        tt = bc * bx
        prev = pltpu.roll(tt, 1, 0) * has_prev
        nxt = pltpu.roll(tt, t - 1, 0) * has_next
        w0, w1, w2 = w_ref[0:1, :], w_ref[1:2, :], w_ref[2:3, :]
        cv = prev * w0 + tt * w1 + nxt * w2 + b_ref[...]
        sg = _silu(bg)
        dz_ref[1] = (duv * cv * sg).astype(BF16)
        dz_ref[3] = (duv * bb * cv * _dsilu(bg)).astype(BF16)
        dcv = duv * bb * sg
        dw_ref[0:1, :] = jnp.sum(dcv * prev, axis=0, keepdims=True)
        dw_ref[1:2, :] = jnp.sum(dcv * tt, axis=0, keepdims=True)
        dw_ref[2:3, :] = jnp.sum(dcv * nxt, axis=0, keepdims=True)
        db_ref[...] = jnp.sum(dcv, axis=0, keepdims=True)
        dt = (pltpu.roll(dcv * has_prev, t - 1, 0) * w0 + dcv * w1
              + pltpu.roll(dcv * has_next, 1, 0) * w2)
        dz_ref[0] = (dt * bc).astype(BF16)
        dz_ref[2] = (dt * bx).astype(BF16)

    sec = lambda s: pl.BlockSpec((None, t, gb), lambda j: (s, 0, j))
    return pl.pallas_call(
        body, name="mix_b_bwd", grid=(half_d // gb,),
        in_specs=[sec(2), sec(3), sec(4), sec(5), pl.BlockSpec((t, gb), lambda j: (0, j + off)),
                  pl.BlockSpec((3, gb), lambda j: (0, j)), pl.BlockSpec((1, gb), lambda j: (0, j))],
        out_specs=[pl.BlockSpec((4, t, gb), lambda j: (0, 0, j)),
                   pl.BlockSpec((3, gb), lambda j: (0, j)), pl.BlockSpec((1, gb), lambda j: (0, j))],
        out_shape=[jax.ShapeDtypeStruct((4, t, half_d), BF16),
                   jax.ShapeDtypeStruct((3, half_d), F32), jax.ShapeDtypeStruct((1, half_d), F32)],
        compiler_params=_cp(),
    )(z0, z0, z0, z0, du, conv_w, conv_b)


def _lower_bound(lbl_ref, d):
    l0, l1, l2 = lbl_ref[d, 0:1, :], lbl_ref[d, 1:2, :], lbl_ref[d, 2:3, :]
    mx = jnp.maximum(jnp.maximum(l0, l1), l2)
    e0, e1, e2 = jnp.exp(l0 - mx), jnp.exp(l1 - mx), jnp.exp(l2 - mx)
    inv = 1.0 / (e0 + e1 + e2)
    return (e0 + e1) * inv, (e0 * inv, e1 * inv, e2 * inv)


def _chunk_consts(d):
    r = lax.broadcasted_iota(jnp.int32, (CHUNK, CHUNK), 0)
    c = lax.broadcasted_iota(jnp.int32, (CHUNK, CHUNK), 1)
    keep = (c <= r) if d == 0 else (c >= r)
    return jnp.where(keep, 1.0, 0.0).astype(F32), keep


def _chunk_of_step(s, d, nc, ncc):
    if d == 0:
        return s
    return jnp.where(s < ncc, ncc - 1 - s, nc - 1 + ncc - s)


def _gates(z, lbv):
    e = jnp.exp(-jnp.abs(z))
    r = 1.0 / (1.0 + e)
    er = e * r
    pos = z >= 0.0
    sig = jnp.where(pos, r, er)
    nsig = jnp.where(pos, er, r)
    return sig, nsig, lbv + (1.0 - lbv) * sig


def _split3(x):
    hi = x.astype(BF16)
    r1 = x - hi.astype(F32)
    mid = r1.astype(BF16)
    lo = (r1 - mid.astype(F32)).astype(BF16)
    return jnp.concatenate([hi, mid, lo], axis=1)


def _cumsum_chunk(cum, x):
    y = _dot(cum, _split3(x))
    return y[:, :HEAD] + y[:, HEAD:2 * HEAD] + y[:, 2 * HEAD:]


def _chunk_rows(n):
    return pl.ds(pl.multiple_of(n * CHUNK, CHUNK), CHUNK)


def _group(nc, prefer=(4, 3, 2, 1)):
    return next(u for u in prefer if nc % u == 0)


WIDE_GROUP = (18, 12, 6, 4, 3, 2, 1)


def _decay_pass(lf_ref, bc_ref, dec_ref, cum, nc):
    grp = _group(nc, WIDE_GROUP)

    def step(m, carry):
        ns = [m * grp + u for u in range(grp)]
        lfc = [lf_ref[_chunk_rows(n), :] for n in ns]
        bc = [_cumsum_chunk(cum, x) for x in lfc]
        for u, n in enumerate(ns):
            bc_ref[_chunk_rows(n), :] = bc[u]
            dec_ref[n] = jnp.broadcast_to(jnp.exp(jnp.sum(lfc[u], axis=0, keepdims=True)), (8, HEAD))
        return carry

    lax.fori_loop(0, nc // grp, step, 0)


def hgrn_fwd(z1, lbl, onorm, cx):
    _, t, d = z1.shape
    seq = t - cx
    nc, ncc = t // CHUNK, cx // CHUNK

    grp, sgrp = _group(nc, (9, 6, 4, 3, 2, 1)), _group(nc, WIDE_GROUP)

    def body(zf_ref, zb_ref, v_ref, q_ref, g_ref, lbl_ref, on_ref, o_ref, r_ref, bcs_ref, ks_ref, decs_ref,
             lf_ref, k_ref, bc_ref, dec_ref, qd_ref, ki_ref, oacc_ref, ds_ref):
        for dr, z_ref in ((0, zf_ref), (1, zb_ref)):
            lbv, _ = _lower_bound(lbl_ref, dr)
            _, nsig, f = _gates(z_ref[...], lbv)
            lf_ref[...] = jnp.log(f)
            k_ref[...] = (1.0 - lbv) * nsig
            cum, keep = _chunk_consts(dr)
            _decay_pass(lf_ref, bc_ref, dec_ref, cum.astype(BF16), nc)
            bc = bc_ref[...]
            bcs_ref[dr] = bc
            ks_ref[dr] = k_ref[...]
            decs_ref[dr] = dec_ref[...]
            qd_ref[...] = (q_ref[...] * jnp.exp(bc)).astype(BF16)
            ki_ref[...] = (k_ref[...] * jnp.exp(-bc)).astype(BF16)

            def local_step(m, carry, dr=dr, keep=keep):
                ns = [m * grp + u for u in range(grp)]
                rows = [_chunk_rows(n) for n in ns]
                qd = [qd_ref[r, :] for r in rows]
                ki = [ki_ref[r, :] for r in rows]
                vc = [v_ref[r, :].astype(BF16) for r in rows]
                sc = [_dot(qd[u], ki[u], NT) for u in range(grp)]
                inc = [_dot(vc[u], ki[u], TN) for u in range(grp)]
                a = [jnp.where(keep, s, 0.0).astype(BF16) for s in sc]
                intra = [_dot(a[u], vc[u]) for u in range(grp)]
                for u in range(grp):
                    ds_ref[ns[u]] = inc[u] * dec_ref[ns[u]][0:1, :]
                    if dr == 0:
                        oacc_ref[rows[u], :] = intra[u]
                    else:
                        oacc_ref[rows[u], :] += intra[u]
                return carry

            lax.fori_loop(0, nc // grp, local_step, 0)

            def state_step(m, st, dr=dr):
                ns = [_chunk_of_step(m * sgrp + u, dr, nc, ncc) for u in range(sgrp)]
                rows = [_chunk_rows(n) for n in ns]
                sts = []
                for n in ns:
                    sts.append(st.astype(BF16))
                    st = st * dec_ref[n][0:1, :] + ds_ref[n]
                inter = [_dot(qd_ref[rows[u], :], sts[u], NT) for u in range(sgrp)]
                for u in range(sgrp):
                    oacc_ref[rows[u], :] += inter[u]
                return st

            lax.fori_loop(0, nc // sgrp, state_step, jnp.zeros((HEAD, HEAD), F32))

        o = oacc_ref[cx:, :]
        o_ref[...] = o
        rstd = lax.rsqrt(jnp.mean(o * o, axis=-1, keepdims=True) + EPS)
        r_ref[...] = (o * rstd * on_ref[...] * _silu(g_ref[cx:, :])).astype(BF16)

    sec = lambda s: pl.BlockSpec((None, t, HEAD), lambda h: (s, 0, h))
    col = pl.BlockSpec((seq, HEAD), lambda h: (0, h))
    tf32, tb16 = pltpu.VMEM((t, HEAD), F32), pltpu.VMEM((t, HEAD), BF16)
    return pl.pallas_call(
        body, name="hgrn_fwd", grid=(d // HEAD,),
        in_specs=[sec(0), sec(1), sec(2), sec(3), sec(4),
                  pl.BlockSpec((2, 3, HEAD), lambda h: (0, 0, h)), pl.BlockSpec((1, HEAD), lambda h: (0, h))],
        out_specs=[col, col, pl.BlockSpec((2, t, HEAD), lambda h: (0, 0, h)),
                   pl.BlockSpec((2, t, HEAD), lambda h: (0, 0, h)),
                   pl.BlockSpec((2, nc, 8, HEAD), lambda h: (0, 0, 0, h))],
        out_shape=[jax.ShapeDtypeStruct((seq, d), F32), jax.ShapeDtypeStruct((seq, d), BF16),
                   jax.ShapeDtypeStruct((2, t, d), F32), jax.ShapeDtypeStruct((2, t, d), F32),
                   jax.ShapeDtypeStruct((2, nc, 8, d), F32)],
        scratch_shapes=[tf32, tf32, tf32, pltpu.VMEM((nc, 8, HEAD), F32), tb16, tb16, tf32,
                        pltpu.VMEM((nc, HEAD, HEAD), F32)],
        compiler_params=_cp(),
    )(z1, z1, z1, z1, z1, lbl, onorm)


def hgrn_bwd(z1, lbl, onorm, o, dr_out, bcs, ks, decs, cx):
    _, t, d = z1.shape
    seq = t - cx
    nc, ncc = t // CHUNK, cx // CHUNK

    grp2, grp = _group(nc, (12, 9, 6, 4, 3, 2, 1)), _group(nc, (18, 12, 9, 6, 4, 3, 2, 1))

    def body(zf_ref, zb_ref, v_ref, q_ref, g_ref, lbl_ref, on_ref, o_ref, dr_ref, bcs_ref, ks_ref, decs_ref,
             dz_ref, don_ref, dlb_ref,
             qd_ref, ki_ref, do_ref, dqd_ref, dki_ref, dq_ref, dv_ref, ds_ref, dsl_ref):
        o = o_ref[...]
        g = g_ref[cx:, :]
        drv = dr_ref[...]
        onv = on_ref[...]
        rstd = lax.rsqrt(jnp.mean(o * o, axis=-1, keepdims=True) + EPS)
        ohat = o * rstd
        sg = _silu(g)
        don_ref[...] = jnp.sum(drv * ohat * sg, axis=0, keepdims=True)
        dz_ref[4, :cx, :] = jnp.zeros((cx, HEAD), BF16)
        dz_ref[4, cx:, :] = (drv * ohat * onv * _dsilu(g)).astype(BF16)
        dohat = drv * onv * sg
        do_ref[:cx, :] = jnp.zeros((cx, HEAD), BF16)
        do_ref[cx:, :] = (rstd * (dohat - ohat * jnp.mean(dohat * ohat, axis=-1, keepdims=True))).astype(BF16)

        for dr, z_ref in ((0, zf_ref), (1, zb_ref)):
            lbv, _ = _lower_bound(lbl_ref, dr)
            k_ref, bc_ref, dec_ref = ks_ref.at[dr], bcs_ref.at[dr], decs_ref.at[dr]
            _, keep = _chunk_consts(dr)
            cum_t = _chunk_consts(1 - dr)[0].astype(BF16)
            bc = bc_ref[...]
            qd_ref[...] = (q_ref[...] * jnp.exp(bc)).astype(BF16)
            ki_ref[...] = (k_ref[...] * jnp.exp(-bc)).astype(BF16)

            def local_step(m, carry, dr=dr, keep=keep):
                ns = [m * grp + u for u in range(grp)]
                rows = [_chunk_rows(n) for n in ns]
                rng = range(grp)
                qd = [qd_ref[r, :] for r in rows]
                ki = [ki_ref[r, :] for r in rows]
                doc = [do_ref[r, :] for r in rows]
                vc = [v_ref[r, :].astype(BF16) for r in rows]
                sc = [_dot(qd[u], ki[u], NT) for u in rng]
                dsc = [_dot(doc[u], vc[u], NT) for u in rng]
                inc = [_dot(vc[u], ki[u], TN) for u in rng]
                dinc = [_dot(doc[u], qd[u], TN) for u in rng]
                a = [jnp.where(keep, s, 0.0).astype(BF16) for s in sc]
                da = [jnp.where(keep, s, 0.0).astype(BF16) for s in dsc]
                dqd = [_dot(da[u], ki[u]) for u in rng]
                dki = [_dot(da[u], qd[u], TN) for u in rng]
                dv = [_dot(a[u], doc[u], TN) for u in rng]
                for u in rng:
                    ds_ref[ns[u]] = inc[u] * dec_ref[ns[u]][0:1, :]
                    dsl_ref[ns[u]] = dinc[u]
                    dqd_ref[rows[u], :] = dqd[u]
                    dki_ref[rows[u], :] = dki[u]
                    if dr == 0:
                        dv_ref[rows[u], :] = dv[u]
                    else:
                        dv_ref[rows[u], :] += dv[u]
                return carry

            lax.fori_loop(0, nc // grp, local_step, 0)

            def state_step(s, st, dr=dr):
                n = _chunk_of_step(s, dr, nc, ncc)
                inc = ds_ref[n]
                ds_ref[n] = st
                return st * dec_ref[n][0:1, :] + inc

            lax.fori_loop(0, nc, state_step, jnp.zeros((HEAD, HEAD), F32), unroll=4)

            def dstate_step(s, dst, dr=dr):
                n = _chunk_of_step(nc - 1 - s, dr, nc, ncc)
                inc = dsl_ref[n]
                dsl_ref[n] = dst
                return inc + dst * dec_ref[n][0:1, :]

            lax.fori_loop(0, nc, dstate_step, jnp.zeros((HEAD, HEAD), F32), unroll=4)

            def grad_step(m, carry, dr=dr, cum_t=cum_t):
                ns = [m * grp2 + u for u in range(grp2)]
                rows = [_chunk_rows(n) for n in ns]
                rng = range(grp2)
                st0 = [ds_ref[n] for n in ns]
                dst = [dsl_ref[n] for n in ns]
                dstb = [x.astype(BF16) for x in dst]
                dec = [dec_ref[n][0:1, :] for n in ns]
                doc = [do_ref[r, :] for r in rows]
                vc = [v_ref[r, :].astype(BF16) for r in rows]
                e = [jnp.exp(bc_ref[r, :]) for r in rows]
                einv = [jnp.exp(-bc_ref[r, :]) for r in rows]
                qd = [q_ref[rows[u], :] * e[u] for u in rng]
                ki = [k_ref[rows[u], :] * einv[u] for u in rng]
                kd = [ki[u] * dec[u] for u in rng]
                dqd_st = [_dot(doc[u], st0[u].astype(BF16)) for u in rng]
                dkd = [_dot(vc[u], dstb[u]) for u in rng]
                dv_st = [_dot(kd[u].astype(BF16), dstb[u], NT) for u in rng]
                dqd = [dqd_ref[rows[u], :] + dqd_st[u] for u in rng]
                dki = [dki_ref[r, :] for r in rows]
                dbc = [dqd[u] * qd[u] - dki[u] * ki[u] - dkd[u] * kd[u] for u in rng]
                cs = [_cumsum_chunk(cum_t, x) for x in dbc]
                for u in rng:
                    ddec = jnp.sum(dst[u] * st0[u], axis=0, keepdims=True)
                    dbl = jnp.sum(dkd[u] * kd[u], axis=0, keepdims=True) + ddec * dec[u]
                    dv_ref[rows[u], :] += dv_st[u]
                    dqd_ref[rows[u], :] = cs[u] + dbl
                    dki_ref[rows[u], :] = dki[u] * einv[u] + dkd[u] * (einv[u] * dec[u])
                    if dr == 0:
                        dq_ref[rows[u], :] = dqd[u] * e[u]
                    else:
                        dq_ref[rows[u], :] += dqd[u] * e[u]
                return carry

            lax.fori_loop(0, nc // grp2, grad_step, 0)

            sig, nsig, f = _gates(z_ref[...], lbv)
            common = (dqd_ref[...] / f - dki_ref[...]) * nsig
            dz_ref[dr] = (common * ((1.0 - lbv) * sig)).astype(BF16)
            dlb_ref[dr:dr + 1, :] = jnp.sum(common, axis=0, keepdims=True)

        dz_ref[2] = dv_ref[...].astype(BF16)
        dz_ref[3] = dq_ref[...].astype(BF16)

    sec = lambda s: pl.BlockSpec((None, t, HEAD), lambda h: (s, 0, h))
    col = pl.BlockSpec((seq, HEAD), lambda h: (0, h))
    tf32, tb16 = pltpu.VMEM((t, HEAD), F32), pltpu.VMEM((t, HEAD), BF16)
    states = pltpu.VMEM((nc, HEAD, HEAD), F32)
    return pl.pallas_call(
        body, name="hgrn_bwd", grid=(d // HEAD,),
        in_specs=[sec(0), sec(1), sec(2), sec(3), sec(4),
                  pl.BlockSpec((2, 3, HEAD), lambda h: (0, 0, h)), pl.BlockSpec((1, HEAD), lambda h: (0, h)),
                  col, col, pl.BlockSpec((2, t, HEAD), lambda h: (0, 0, h)),
                  pl.BlockSpec((2, t, HEAD), lambda h: (0, 0, h)),
                  pl.BlockSpec((2, nc, 8, HEAD), lambda h: (0, 0, 0, h))],
        out_specs=[pl.BlockSpec((5, t, HEAD), lambda h: (0, 0, h)),
                   pl.BlockSpec((1, HEAD), lambda h: (0, h)), pl.BlockSpec((2, HEAD), lambda h: (0, h))],
        out_shape=[jax.ShapeDtypeStruct((5, t, d), BF16), jax.ShapeDtypeStruct((1, d), F32),
                   jax.ShapeDtypeStruct((2, d), F32)],
        scratch_shapes=[tb16, tb16, tb16, tf32, tf32, tf32, tf32, states, states],
        compiler_params=_cp(),
    )(z1, z1, z1, z1, z1, lbl, onorm, o, dr_out, bcs, ks, decs)


def _place():
    x, y, c = lax.axis_index("x"), lax.axis_index("y"), lax.axis_index("c")
    chips = [(1 - x, y), (x, 1 - y), (1 - x, 1 - y)]
    return x, y, c, chips


def _relay_chips():
    x, y, c, _ = _place()
    first = c == 0
    near = (jnp.where(first, 1 - x, x), jnp.where(first, y, 1 - y))
    far = (jnp.where(first, x, 1 - x), jnp.where(first, 1 - y, y))
    return near, far, (1 - x, 1 - y)


def _cast_rows(src_ref, dst_ref, bufs):
    fbuf, bbuf, load_sems, store_sems = bufs
    tr = fbuf.shape[1]
    nblk = src_ref.shape[0] // tr

    def load(i, s):
        return pltpu.make_async_copy(src_ref.at[pl.ds(i * tr, tr)], fbuf.at[s], load_sems.at[s])

    def store(i, s):
        return pltpu.make_async_copy(bbuf.at[s], dst_ref.at[pl.ds(i * tr, tr)], store_sems.at[s])

    load(0, 0).start()

    def step(i, carry):
        s = i % 2
        load(i, s).wait()

        @pl.when(i + 1 < nblk)
        def _():
            load(i + 1, 1 - s).start()

        @pl.when(i >= 2)
        def _():
            store(i - 2, s).wait()

        bbuf[s] = fbuf[s].astype(BF16)
        store(i, s).start()
        return carry

    lax.fori_loop(0, nblk, step, 0)
    for i in range(max(nblk - 2, 0), nblk):
        store(i, i % 2).wait()


def _norm_rows(src_ref, xs_ref, h_ref, row0, g, shift, scale, bufs):
    fbuf, bbuf, load_sems, xs_sems, h_sems = bufs
    tm = fbuf.shape[1]
    nblk = src_ref.shape[0] // tm

    def load(i, s):
        return pltpu.make_async_copy(src_ref.at[pl.ds(i * tm, tm)], fbuf.at[s], load_sems.at[s])

    def put_x(i, s):
        return pltpu.make_async_copy(fbuf.at[s], xs_ref.at[pl.ds(row0 + i * tm, tm)], xs_sems.at[s])

    def put_h(i, s):
        return pltpu.make_async_copy(bbuf.at[s], h_ref.at[pl.ds(row0 + i * tm, tm)], h_sems.at[s])

    load(0, 0).start()

    def step(i, carry):
        s = i % 2
        load(i, s).wait()

        @pl.when(i >= 1)
        def _():
            put_x(i - 1, 1 - s).wait()

        @pl.when(i + 1 < nblk)
        def _():
            load(i + 1, 1 - s).start()

        @pl.when(i >= 2)
        def _():
            put_h(i - 2, s).wait()

        x = fbuf[s]
        rstd = lax.rsqrt(jnp.mean(x * x, axis=-1, keepdims=True) + EPS)
        bbuf[s] = ((x * rstd) * g * (1.0 + scale) + shift).astype(BF16)
        put_x(i, s).start()
        put_h(i, s).start()
        return carry

    lax.fori_loop(0, nblk, step, 0)
    put_x(nblk - 1, (nblk - 1) % 2).wait()
    for i in range(max(nblk - 2, 0), nblk):
        put_h(i, i % 2).wait()


def allgather_shards(bufs, after, casts, norm):
    n, m = len(bufs), len(casts)
    cast_rows = [256 if a.shape[0] % 256 == 0 else a.shape[0] for a in casts]
    ctx, x_lat = norm[:2]
    cx, d = ctx.shape
    t = cx + x_lat.shape[0]
    tm = _row_block(cx)

    def body(*refs):
        refs = list(refs)
        take = lambda k: [refs.pop(0) for _ in range(k)]
        take(n + 1)
        cast_src = take(m)
        ctx_ref, x_ref, g_ref, sh_ref, sc_ref = take(5)
        outs = take(n)
        done_ref, = take(1)
        cast_dst = take(m)
        h_ref, xs_ref = take(2)
        send_sems, recv_sems = take(2)
        cast_bufs = take(4 * m)
        norm_bufs = take(5)
        done_ref[...] = jnp.zeros((8, 128), F32)
        x, y, c, _ = _place()
        me = (x, y, c)
        p = 2 * x + y
        near, far, diag = _relay_chips()
        half = [pl.ds(c * (s.shape[1] // 2), s.shape[1] // 2) for s in bufs]
        other = [pl.ds((1 - c) * (s.shape[1] // 2), s.shape[1] // 2) for s in bufs]
        slot = lambda chip: 2 * chip[0] + chip[1]

        def remote(i, k, ref, to):
            return pltpu.make_async_remote_copy(src_ref=ref, dst_ref=ref, send_sem=send_sems.at[6 * i + k],
                                                recv_sem=recv_sems.at[6 * i + k], device_id=to, device_id_type=MESH)

        sends = []

        def send(i, k, ref, to):
            cp = remote(i, k, ref, to)
            cp.start()
            sends.append(cp)

        for i in range(n):
            mine = outs[i].at[p, half[i]]
            send(i, 0, mine, (*near, c))
            send(i, 1, mine, (*far, c))
        for j in range(m):
            _cast_rows(cast_src[j], cast_dst[j].at[p], cast_bufs[4 * j:4 * j + 4])
        gv = g_ref[...]
        _norm_rows(ctx_ref, xs_ref, h_ref, 0, gv, sh_ref[0:1, :], sc_ref[0:1, :], norm_bufs)
        _norm_rows(x_ref, xs_ref, h_ref, cx, gv, sh_ref[1:2, :], sc_ref[1:2, :], norm_bufs)
        for i in range(n):
            landed = outs[i].at[slot(near), half[i]]
            remote(i, 0, landed, me).wait_recv()
            send(i, 2, landed, (*far, c))
            send(i, 3, landed, (x, y, 1 - c))
        for i in range(n):
            landed = outs[i].at[slot(far), half[i]]
            remote(i, 1, landed, me).wait_recv()
            send(i, 4, landed, (x, y, 1 - c))
        for i in range(n):
            landed = outs[i].at[slot(diag), half[i]]
            remote(i, 2, landed, me).wait_recv()
            send(i, 5, landed, (x, y, 1 - c))
        for i in range(n):
            for k, chip in ((3, far), (4, near), (5, diag)):
                remote(i, k, outs[i].at[slot(chip), other[i]], me).wait_recv()
        for cp in sends:
            cp.wait_send()

    return pl.pallas_call(
        body, name="allgather_shards",
        in_specs=[ANY] * (n + 1 + m + 2) + [VMEM] * 3, out_specs=[ANY] * n + [VMEM] + [ANY] * (m + 2),
        out_shape=[jax.ShapeDtypeStruct(s.shape, s.dtype) for s in bufs] + [jax.ShapeDtypeStruct((8, 128), F32)]
        + [jax.ShapeDtypeStruct((4,) + a.shape, BF16) for a in casts]
        + [jax.ShapeDtypeStruct((t, d), BF16), jax.ShapeDtypeStruct((t, d), F32)],
        input_output_aliases={i: i for i in range(n)},
        scratch_shapes=[pltpu.SemaphoreType.DMA((6 * n,)), pltpu.SemaphoreType.DMA((6 * n,))] + [
            s for a, tr in zip(casts, cast_rows) for s in (
                pltpu.VMEM((2, tr, a.shape[1]), F32), pltpu.VMEM((2, tr, a.shape[1]), BF16),
                pltpu.SemaphoreType.DMA((2,)), pltpu.SemaphoreType.DMA((2,)))] + [
            pltpu.VMEM((2, tm, d), F32), pltpu.VMEM((2, tm, d), BF16), pltpu.SemaphoreType.DMA((2,)),
            pltpu.SemaphoreType.DMA((2,)), pltpu.SemaphoreType.DMA((2,))],
        compiler_params=_cp(has_side_effects=True),
    )(*bufs, after, *casts, *norm)


def pair_sum(grad, got, chip_core):
    _, r, cc = grad.shape
    hr = r // 2
    tr = 256 if hr % 256 == 0 else hr
    nb = hr // tr

    def body(cc_ref, a_ref, b_ref, own_ref, sb_ref):
        s = a_ref[...].astype(F32) + b_ref[...].astype(F32)
        sb_ref[...] = s.astype(BF16)

        @pl.when(pl.program_id(1) == cc_ref[0])
        def _():
            own_ref[...] = s

    grid_spec = pltpu.PrefetchScalarGridSpec(
        num_scalar_prefetch=1, grid=(nb, 4),
        in_specs=[pl.BlockSpec((None, tr, cc), lambda i, qi, cc_ref: (qi, cc_ref[1] * nb + i, 0)),
                  pl.BlockSpec((None, tr, cc), lambda i, qi, cc_ref: (qi, i, 0))],
        out_specs=[pl.BlockSpec((tr, cc), lambda i, qi, cc_ref: (i, 0)),
                   pl.BlockSpec((None, tr, cc), lambda i, qi, cc_ref: (qi, i, 0))])
    return pl.pallas_call(
        body, name="pair_sum", grid_spec=grid_spec,
        out_shape=[jax.ShapeDtypeStruct((hr, cc), F32), jax.ShapeDtypeStruct((4, hr, cc), BF16)],
        compiler_params=_cp(),
    )(chip_core, grad, got)


def owner_sum(own, got, chip_core):
    hr, cc = own.shape
    tr = 256 if hr % 256 == 0 else hr
    nb = hr // tr

    def body(cc_ref, a_ref, b_ref, o_ref):
        s = a_ref[...] + b_ref[0].astype(F32)
        s = s + b_ref[1].astype(F32)
        o_ref[...] = s + b_ref[2].astype(F32)

    grid_spec = pltpu.PrefetchScalarGridSpec(
        num_scalar_prefetch=1, grid=(nb,),
        in_specs=[pl.BlockSpec((tr, cc), lambda i, cc_ref: (i, 0)),
                  pl.BlockSpec((3, tr, cc), lambda i, cc_ref: (0, i, 0))],
        out_specs=pl.BlockSpec((tr, cc), lambda i, cc_ref: (cc_ref[1] * nb + i, 0)))
    return pl.pallas_call(
        body, name="owner_sum", grid_spec=grid_spec,
        out_shape=jax.ShapeDtypeStruct((2 * hr, cc), F32), compiler_params=_cp(),
    )(chip_core, own, got)


def share_halves(bufs):
    n = len(bufs)

    def body(*refs):
        outs = refs[n:2 * n]
        send_sems, recv_sems = refs[2 * n:]
        x, y, c, _ = _place()
        copies = []
        for i in range(n):
            hr = bufs[i].shape[0] // 2
            mine = outs[i].at[pl.ds(c * hr, hr)]
            cp = pltpu.make_async_remote_copy(
                src_ref=mine, dst_ref=mine, send_sem=send_sems.at[i], recv_sem=recv_sems.at[i],
                device_id=(x, y, 1 - c), device_id_type=MESH)
            cp.start()
            copies.append((cp, outs[i].at[pl.ds((1 - c) * hr, hr)]))
        for i, (cp, theirs) in enumerate(copies):
            cp.wait_send()
            pltpu.make_async_remote_copy(
                src_ref=theirs, dst_ref=theirs, send_sem=send_sems.at[i], recv_sem=recv_sems.at[i],
                device_id=(x, y, c), device_id_type=MESH).wait_recv()

    return pl.pallas_call(
        body, name="share_halves",
        in_specs=[ANY] * n, out_specs=[ANY] * n,
        out_shape=[jax.ShapeDtypeStruct(b.shape, b.dtype) for b in bufs],
        input_output_aliases={i: i for i in range(n)},
        scratch_shapes=[pltpu.SemaphoreType.DMA((n,)), pltpu.SemaphoreType.DMA((n,))],
        compiler_params=pltpu.CompilerParams(has_side_effects=True),
    )(*bufs)


def allgather8(v, name, per_peer=False):
    r, n = v.shape[-2:]

    def body(v_ref, out_ref, send_sems, recv_sems):
        x, y, c, _ = _place()
        me = 4 * x + 2 * y + c
        out_ref[me] = v_ref[me] if per_peer else v_ref[...]

        def copy(k, peer_index, slot, to):
            return pltpu.make_async_remote_copy(
                src_ref=v_ref.at[peer_index] if per_peer else v_ref, dst_ref=out_ref.at[slot],
                send_sem=send_sems.at[k - 1], recv_sem=recv_sems.at[k - 1], device_id=to, device_id_type=MESH)

        peers = []
        for k in range(1, 8):
            px = 1 - x if (k >> 2) & 1 else x
            py = 1 - y if (k >> 1) & 1 else y
            pc = 1 - c if k & 1 else c
            peers.append((px, py, pc))
            copy(k, 4 * px + 2 * py + pc, me, (px, py, pc)).start()
        for k, (px, py, pc) in enumerate(peers, start=1):
            copy(k, me, 4 * px + 2 * py + pc, (x, y, c)).wait_recv()
        for k, (px, py, pc) in enumerate(peers, start=1):
            copy(k, 4 * px + 2 * py + pc, me, (px, py, pc)).wait_send()

    return pl.pallas_call(
        body, name=name, in_specs=[VMEM], out_specs=VMEM,
        out_shape=jax.ShapeDtypeStruct((8, r, n), v.dtype),
        scratch_shapes=[pltpu.SemaphoreType.DMA((7,)), pltpu.SemaphoreType.DMA((7,))],
        compiler_params=_cp(has_side_effects=True),
    )(v)


HBM = pl.BlockSpec(memory_space=pltpu.HBM)
SEM = pl.BlockSpec(memory_space=pltpu.SEMAPHORE)
DATAFLOW = pltpu.SideEffectType.DATAFLOW_SIDE_EFFECTING


def _descriptors(plan, refs, send_sems, recv_sems, arrivals=True):
    x, y, c, _ = _place()
    sends, recvs = plan(refs)
    out = [pltpu.make_async_remote_copy(src_ref=src, dst_ref=dst, send_sem=send_sems.at[k],
                                        recv_sem=recv_sems.at[k], device_id=to, device_id_type=MESH)
           for k, (src, dst, to) in enumerate(sends)]
    if not arrivals:
        return out, []
    inn = [pltpu.make_async_remote_copy(src_ref=land, dst_ref=land, send_sem=send_sems.at[k],
                                        recv_sem=recv_sems.at[k], device_id=(x, y, c), device_id_type=MESH)
           for k, land in enumerate(recvs)]
    return out, inn


def copies_start(name, arrays, n_copies, plan, after, carried=()):
    na, nall = len(arrays), len(arrays) + len(carried)
    everything = list(arrays) + list(carried)

    def body(*refs):
        out, _ = _descriptors(plan, refs[:na], refs[nall + 1], refs[nall + 2], arrivals=False)
        for cp in out:
            cp.start()
        refs[-1][...] = jnp.zeros((8, 128), F32)

    res = pl.pallas_call(
        body, name=name,
        out_shape=(pltpu.SemaphoreType.DMA((n_copies,)), pltpu.SemaphoreType.DMA((n_copies,)),
                   *[pltpu.HBM(a.shape, a.dtype) for a in everything], jax.ShapeDtypeStruct((8, 128), F32)),
        in_specs=[HBM] * nall + [ANY], out_specs=(SEM, SEM, *[HBM] * nall, VMEM),
        input_output_aliases={i: i + 2 for i in range(nall)},
        compiler_params=pltpu.CompilerParams(has_side_effects=DATAFLOW),
    )(*[pltpu.with_memory_space_constraint(a, pltpu.HBM) for a in everything], after)
    return res[0], res[1], list(res[2:2 + na]), res[-1], list(res[2 + na:2 + nall])


def copies_wait(name, started, plan, after):
    send_sems, recv_sems, arrays = started[:3]
    na = len(arrays)
    after = list(after) if isinstance(after, (list, tuple)) else [after]

    def body(*refs):
        out, inn = _descriptors(plan, refs[:na], refs[na], refs[na + 1])
        for cp in out:
            cp.wait_send()
        for cp in inn:
            cp.wait_recv()
        refs[-1][...] = jnp.zeros((8, 128), F32)

    res = pl.pallas_call(
        body, name=name,
        out_shape=(*[pltpu.HBM(a.shape, a.dtype) for a in arrays], jax.ShapeDtypeStruct((8, 128), F32)),
        in_specs=[HBM] * na + [SEM, SEM] + [ANY] * len(after), out_specs=(*[HBM] * na, VMEM),
        input_output_aliases={i: i for i in range(na)},
        compiler_params=pltpu.CompilerParams(has_side_effects=DATAFLOW),
    )(*arrays, send_sems, recv_sems, *after)
    return list(res[:na]), res[-1]


def _rows_half(r, c):
    return pl.ds(c * (r // 2), r // 2), pl.ds((1 - c) * (r // 2), r // 2)


def plan_gather_neighbours(refs):
    x, y, c, _ = _place()
    p = 2 * x + y
    near, far, _ = _relay_chips()
    sends, recvs = [], []
    for buf in refs:
        mine, _ = _rows_half(buf.shape[1], c)
        for chip in (near, far):
            sends.append((buf.at[p, mine], buf.at[p, mine], (*chip, c)))
            recvs.append(buf.at[2 * chip[0] + chip[1], mine])
    return sends, recvs


def plan_gather_relay(refs):
    x, y, c, _ = _place()
    near, far, diag = _relay_chips()
    slot = lambda chip: 2 * chip[0] + chip[1]
    sends, recvs = [], []
    for buf in refs:
        mine, theirs = _rows_half(buf.shape[1], c)
        landed = buf.at[slot(near), mine]
        sends.append((landed, landed, (*far, c)))
        recvs.append(buf.at[slot(diag), mine])
        for sent, got in ((near, far), (far, near)):
            sends.append((buf.at[slot(sent), mine], buf.at[slot(sent), mine], (x, y, 1 - c)))
            recvs.append(buf.at[slot(got), theirs])
    return sends, recvs


def plan_gather_d2d(refs):
    x, y, c, _ = _place()
    _, _, diag = _relay_chips()
    sends, recvs = [], []
    for buf in refs:
        mine, theirs = _rows_half(buf.shape[1], c)
        landed = buf.at[2 * diag[0] + diag[1], mine]
        sends.append((landed, landed, (x, y, 1 - c)))
        recvs.append(buf.at[2 * diag[0] + diag[1], theirs])
    return sends, recvs


def plan_exchange(refs):
    x, y, c, _ = _place()
    n = len(refs) // 2
    sends, recvs = [], []
    for grad, land in zip(refs[:n], refs[n:]):
        _, theirs = _rows_half(grad.shape[1], c)
        sends.append((grad.at[:, theirs], land, (x, y, 1 - c)))
        recvs.append(land)
    return sends, recvs


def plan_scatter(refs):
    x, y, c, chips = _place()
    n = len(refs) // 2
    sends, recvs = [], []
    for part, land in zip(refs[:n], refs[n:]):
        for j, chip in enumerate(chips):
            sends.append((part.at[2 * chip[0] + chip[1]], land.at[j], (*chip, c)))
            recvs.append(land.at[j])
    return sends, recvs


def plan_share(refs):
    x, y, c, _ = _place()
    sends, recvs = [], []
    for buf in refs:
        mine, theirs = _rows_half(buf.shape[0], c)
        sends.append((buf.at[mine], buf.at[mine], (x, y, 1 - c)))
        recvs.append(buf.at[theirs])
    return sends, recvs


def put_in_slot(w, chip, dtype, name):
    r, c = w.shape
    tr = 256 if r % 256 == 0 else r

    def body(chip_ref, w_ref, o_ref):
        o_ref[...] = w_ref[...].astype(dtype)

    grid_spec = pltpu.PrefetchScalarGridSpec(
        num_scalar_prefetch=1, grid=(r // tr,),
        in_specs=[pl.BlockSpec((tr, c), lambda i, chip_ref: (i, 0))],
        out_specs=pl.BlockSpec((None, tr, c), lambda i, chip_ref: (chip_ref[0], i, 0)))
    return pl.pallas_call(body, name=name, grid_spec=grid_spec,
                          out_shape=jax.ShapeDtypeStruct((4, r, c), dtype), compiler_params=_cp())(chip, w)


def ada_fwd(s_in, ada_w, ada_b, tn):
    nl, d, ws = ada_w.shape

    def body(s_ref, w_ref, b_ref, so_ref, mod_ref):
        s = _silu(s_ref[...])
        so_ref[...] = s
        mod_ref[...] = _dot(s.astype(BF16), w_ref[...].astype(BF16)) + b_ref[...]

    return pl.pallas_call(
        body, name="ada_fwd", grid=(nl, ws // tn),
        in_specs=[pl.BlockSpec((16, d), lambda l, j: (0, 0)),
                  pl.BlockSpec((None, d, tn), lambda l, j: (l, 0, j)),
                  pl.BlockSpec((None, 1, tn), lambda l, j: (l, 0, j))],
        out_specs=[pl.BlockSpec((16, d), lambda l, j: (0, 0)),
                   pl.BlockSpec((None, 16, tn), lambda l, j: (l, 0, j))],
        out_shape=[jax.ShapeDtypeStruct((16, d), F32), jax.ShapeDtypeStruct((nl, 16, ws), F32)],
        compiler_params=_cp(),
    )(s_in, ada_w, ada_b)


def _adamw_math(w, g, m, v):
    m = ADAM_B1 * m + (1.0 - ADAM_B1) * g
    v = ADAM_B2 * v + (1.0 - ADAM_B2) * (g * g)
    m_hat = m / (1.0 - ADAM_B1 ** ADAM_STEP)
    v_hat = v / (1.0 - ADAM_B2 ** ADAM_STEP)
    delta = -ADAM_LR * (m_hat / (jnp.sqrt(v_hat) + ADAM_EPS) + ADAM_WD * w)
    return delta, m, v


def ada_bwd_adamw(s, dm, w, m, v):
    nl, d, ws = w.shape
    tr = 256 if d % 256 == 0 else 128

    def body(s_ref, dm_ref, w_ref, m_ref, v_ref, g_ref, dl_ref, mo_ref, vo_ref, dc_ref):
        dmv = dm_ref[...].astype(BF16)
        wv = w_ref[...]
        g = _dot(s_ref[...].astype(BF16), dmv, TN)
        g_ref[...] = g
        dl_ref[...], mo_ref[...], vo_ref[...] = _adamw_math(wv, g, m_ref[...], v_ref[...])
        dc_ref[...] = _dot(dmv[8:16, :], wv.astype(BF16), NT)

    wblk = pl.BlockSpec((None, tr, ws), lambda l, i: (l, i, 0))
    wshape = jax.ShapeDtypeStruct((nl, d, ws), F32)
    return pl.pallas_call(
        body, name="ada_bwd_adamw", grid=(nl, d // tr),
        in_specs=[pl.BlockSpec((16, tr), lambda l, i: (0, i)),
                  pl.BlockSpec((None, 16, ws), lambda l, i: (l, 0, 0)), wblk, wblk, wblk],
        out_specs=[wblk, wblk, wblk, wblk, pl.BlockSpec((None, 8, tr), lambda l, i: (l, 0, i))],
        out_shape=[wshape, wshape, wshape, wshape, jax.ShapeDtypeStruct((nl, 8, d), F32)],
        compiler_params=_cp(),
    )(s, dm, w, m, v)


def adamw(w, g, m, v, name, with_grad=False):
    r, c = w.shape
    tr = 256 if r % 256 == 0 else r

    def body(w_ref, g_ref, m_ref, v_ref, dl_ref, mo_ref, vo_ref, *g_out):
        gv = g_ref[...]
        dl_ref[...], mo_ref[...], vo_ref[...] = _adamw_math(w_ref[...], gv, m_ref[...], v_ref[...])
        if with_grad:
            g_out[0][...] = gv

    blk = pl.BlockSpec((tr, c), lambda i: (i, 0))
    shape = jax.ShapeDtypeStruct((r, c), F32)
    n_out = 4 if with_grad else 3
    return pl.pallas_call(body, name=name, grid=(r // tr,), in_specs=[blk] * 4, out_specs=[blk] * n_out,
                          out_shape=[shape] * n_out, compiler_params=_cp())(w, g, m, v)


ROW_MOD = 10


def small_reduce(gathered):
    _, rows, d = gathered.shape

    def body(g_ref, o_ref):
        tot = g_ref[0]
        for b in range(1, 8):
            tot = tot + g_ref[b]
        o_ref[0:rows, :] = tot
        for layer in range(2):
            lat = ROW_MOD + 6 * layer
            o_ref[24 + 3 * layer:27 + 3 * layer, :] = tot[lat:lat + 3, :] + tot[lat + 3:lat + 6, :]
        o_ref[30:32, :] = jnp.zeros((2, d), F32)

    return pl.pallas_call(body, name="small_reduce", in_specs=[VMEM], out_specs=VMEM,
                          out_shape=jax.ShapeDtypeStruct((32, d), F32), compiler_params=_cp())(gathered)


def lb_logits_grad(lbl, dlb):
    _, _, n = lbl.shape

    def body(l_ref, d_ref, o_ref):
        for dr in range(2):
            _, (p0, p1, p2) = _lower_bound(l_ref, dr)
            dv = d_ref[dr:dr + 1, :]
            o_ref[dr, 0:1, :] = p0 * p2 * dv
            o_ref[dr, 1:2, :] = p1 * p2 * dv
            o_ref[dr, 2:3, :] = -p2 * (p0 + p1) * dv

    return pl.pallas_call(body, name="lb_logits_grad", in_specs=[VMEM, VMEM], out_specs=VMEM,
                          out_shape=jax.ShapeDtypeStruct((2, 3, n), F32), compiler_params=_cp())(lbl, dlb)


def c_ctx_grad(parts, c_ctx):
    d = c_ctx.shape[1]

    def body(p_ref, c_ref, o_ref):
        tot = p_ref[0, 0:1, :]
        for chip in range(1, 4):
            tot = tot + p_ref[2 * chip, 0:1, :]
        o_ref[...] = tot * _dsilu(c_ref[...])

    return pl.pallas_call(body, name="c_ctx_grad", in_specs=[VMEM, VMEM], out_specs=VMEM,
                          out_shape=jax.ShapeDtypeStruct((1, d), F32), compiler_params=_cp())(parts, c_ctx)


def kernel(x, c, ctx, c_ctx, ada_w, ada_b, pre_g, post_g, ev_w_in, ev_pool_w, ev_pool_scale, ev_conv_w, ev_conv_b, ev_w_out, od_w_in, od_onorm_g, od_w_out, lb_logits, loss_target, m_c_ctx, m_ada_w, m_ada_b, m_pre_g, m_post_g, m_ev_w_in, m_ev_pool_w, m_ev_pool_scale, m_ev_conv_w, m_ev_conv_b, m_ev_w_out, m_od_w_in, m_od_onorm_g, m_od_w_out, m_lb_logits, v_c_ctx, v_ada_w, v_ada_b, v_pre_g, v_post_g, v_ev_w_in, v_ev_pool_w, v_ev_pool_scale, v_ev_conv_w, v_ev_conv_b, v_ev_w_out, v_od_w_in, v_od_onorm_g, v_od_w_out, v_lb_logits):
    _, seq, d = x.shape
    cx = ctx.shape[1]
    t = cx + seq
    half_d = d // 2
    g = half_d // N_POOL
    tn = d // 4
    xi, yi, ci = lax.axis_index("x"), lax.axis_index("y"), lax.axis_index("c")
    chip = 2 * xi + yi
    chip_arr = jnp.reshape(chip, (1,)).astype(jnp.int32)
    chip_core_arr = jnp.stack([chip, ci]).astype(jnp.int32)

    c_rows = jnp.concatenate([c, jnp.zeros((7, d), F32)], axis=0)
    c_all = allgather8(c_rows, "allgather_c")[:, 0, :]
    s_in = jnp.concatenate([c_all, c_ctx.reshape(1, d), jnp.zeros((7, d), F32)], axis=0)
    ws_ada = ada_w.shape[2]
    ada_b_mine = lax.dynamic_slice(ada_b, (0, chip * ws_ada), (2, ws_ada)).reshape(2, 1, ws_ada)
    s_act, mod_mine = ada_fwd(s_in, ada_w, ada_b_mine, tn)
    mod_rows = jnp.concatenate([
        mod_mine[:, :8].transpose(1, 0, 2), jnp.broadcast_to(mod_mine[:, 8][None], (8, 2, ws_ada)),
        jnp.zeros((8, 4, ws_ada), F32)], axis=1)
    mod_all = allgather8(mod_rows, "exchange_mod", per_peer=True)

    pad = lambda a, rows: jnp.concatenate([a, jnp.zeros((rows - a.shape[0], g), F32)], axis=0)
    small = jnp.concatenate([
        ev_pool_w.reshape(g, g), pad(ev_conv_w.reshape(3, g), 8), pad(od_onorm_g.reshape(2, g), 8),
        pad(lb_logits.reshape(12, g), 16)], axis=0)
    by_chip = mod_all[0::2]
    mods = jnp.stack([by_chip[:, 2:4], by_chip[:, 0:2]]).transpose(2, 0, 1, 3).reshape(2, 2, 3 * d)
    shift, scale, gate = mods[:, :, :d], mods[:, :, d:2 * d], mods[:, :, 2 * d:]
    ev_in_g, ev_out_g, small_g, ev_done, od_in_mine, od_out_mine, h0, xs = allgather_shards([
        put_in_slot(ev_w_in[0], chip_arr, BF16, "cast_ev_w_in"),
        put_in_slot(ev_w_out[0], chip_arr, BF16, "cast_ev_w_out"),
        put_in_slot(small, chip_arr, F32, "place_small")], mod_all, [od_w_in[0], od_w_out[0]],
        (ctx[0], x[0], pre_g[0:1], shift[0], scale[0]))
    od_ici = copies_start("gather_od_ici_start", [od_in_mine, od_out_mine], 4, plan_gather_neighbours, ev_done,
                          carried=[h0])
    h0 = od_ici[4][0]
    ev_out3 = ev_out_g.reshape(1, d, d)
    pool_w_full = small_g[:, :g].reshape(4, N_POOL, g // 4, g).transpose(1, 0, 2, 3).reshape(N_POOL, g, g)
    conv_w_full = small_g[:, g:g + 3].transpose(1, 0, 2).reshape(3, half_d)
    onorm_full = small_g[:, g + 8:g + 10].reshape(1, d)
    lbl_full = small_g[:, g + 16:g + 28].reshape(4, 2, 3, 2 * g).transpose(1, 2, 0, 3).reshape(2, 3, d)

    z0 = mm_nn(h0, ev_in_g, half_d, tn, "mm_ev_in")
    u = mix_b_fwd(z0, conv_w_full, ev_conv_b, mix_a_fwd(z0, pool_w_full, ev_pool_scale, cx), cx)
    od_relay = copies_start("gather_od_relay_start",
                            copies_wait("gather_od_ici_wait", od_ici, plan_gather_neighbours, u)[0],
                            6, plan_gather_relay, u)
    y0 = mm_nn(u, ev_out3, d, tn, "mm_ev_out")[0]
    xs1, h1 = post_fwd_norm(xs, y0, post_g[0:1] + od_relay[3][0:1, 0:1], gate[0],
                            pre_g[1:2], shift[1], scale[1], cx)
    od_d2d = copies_start("gather_od_d2d_start",
                          copies_wait("gather_od_relay_wait", od_relay, plan_gather_relay, xs1)[0],
                          2, plan_gather_d2d, xs1)
    (od_in_g, od_out_g), _ = copies_wait("gather_od_d2d_wait", od_d2d, plan_gather_d2d, od_d2d[3])
    od_out3 = od_out_g.reshape(1, d, d)

    z1 = mm_nn(h1, od_in_g, d, tn, "mm_od_in")
    o1, r1, bcs1, ks1, decs1 = hgrn_fwd(z1, lbl_full, onorm_full, cx)
    y1 = mm_nn(r1, od_out3, d, tn, "mm_od_out")[0]
    sq, dx2, dy1, dgate1, dpost1 = post_loss(xs1, y1, post_g[1:2], gate[1], loss_target[0], cx)

    dr1 = mm_nt(dy1[None], None, od_out3, tn, "mm_od_out_dx")
    g_od_out = mm_tn(r1, dy1[None], None, d, tn, "mm_od_out_dw")
    dz1, donorm, dlb = hgrn_bwd(z1, lbl_full, onorm_full, o1, dr1, bcs1, ks1, decs1, cx)
    dh1 = mm_nt(dz1, None, od_in_g, tn, "mm_od_in_dx")
    g_od_in = mm_tn(h1, dz1, None, od_in_g.shape[2], tn, "mm_od_in_dw")
    od_grads = [g_od_in, g_od_out.reshape(4, d // 4, d)]
    half_zone = lambda a, lead, dt: lax.empty((lead, a.shape[1] // 2, a.shape[2]), dt)
    od_ex = copies_start("reduce_od_exchange_start", od_grads + [half_zone(a, 4, a.dtype) for a in od_grads],
                         2, plan_exchange, dh1)

    dxs1, dpre1, dshift1, dscale1, dy0, dgate0, dpost0 = normmod_bwd(
        xs1, dh1, pre_g[1:2] + od_ex[3][0:1, 0:1], scale[1], dx2, cx, True,
        prev=(y0, post_g[0:1], gate[0]))
    du = mm_nt(dy0[None], None, ev_out3, tn, "mm_ev_out_dx")
    g_ev_out = mm_tn(u, dy0[None], None, d, tn, "mm_ev_out_dw")
    od_got, _ = copies_wait("reduce_od_exchange_wait", od_ex, plan_exchange, g_ev_out)
    od_sums = [pair_sum(od_got[i], od_got[2 + i], chip_core_arr) for i in range(2)]
    od_sc = copies_start("reduce_od_scatter_start",
                         [sb for _, sb in od_sums] + [half_zone(a, 3, BF16) for a in od_grads],
                         6, plan_scatter, du)
    dz0a, g_pool_w, dpool_scale = mix_a_bwd(z0, du, pool_w_full, ev_pool_scale + od_sc[3][0:1, 0:1], cx)
    dz0b, dconv_w, dconv_b = mix_b_bwd(z0, du, conv_w_full, ev_conv_b + od_sc[3][0:1, 0:1], cx)
    g_ev_in = mm_tn(h0, dz0a, dz0b, ev_in_g.shape[2], tn, "mm_ev_in_dw")
    ev_grads = [g_ev_in, g_ev_out.reshape(4, d // 4, d), g_pool_w.reshape(4, g, g)]
    ev_ex = copies_start("reduce_ev_exchange_start", ev_grads + [half_zone(a, 4, a.dtype) for a in ev_grads],
                         3, plan_exchange, dpool_scale)
    dh0 = mm_nt(dz0a, dz0b, ev_in_g, tn, "mm_ev_in_dx")
    dxs0, dpre0, dshift0, dscale0 = normmod_bwd(xs, dh0, pre_g[0:1] + ev_ex[3][0:1, 0:1], scale[0], dxs1,
                                                cx, False, True)
    grad_x = dxs0[None]
    ev_got, _ = copies_wait("reduce_ev_exchange_wait", ev_ex, plan_exchange, dxs0)
    ev_sums = [pair_sum(ev_got[i], ev_got[3 + i], chip_core_arr) for i in range(3)]
    od_recv, _ = copies_wait("reduce_od_scatter_wait", od_sc, plan_scatter, dxs0)

    zrow = jnp.zeros((1, d), F32)
    small_rows = jnp.concatenate([
        dpre0, dpre1, dpost0, dpost1,
        jnp.concatenate([dpool_scale, dconv_b], axis=1),
        jnp.concatenate([dconv_w.reshape(1, 3 * half_d), jnp.zeros((1, half_d), F32)], axis=1).reshape(2, d),
        donorm, dlb,
        dshift0[1:2], dscale0[1:2], dgate0[1:2], dshift0[0:1], dscale0[0:1], dgate0[0:1],
        dshift1[1:2], dscale1[1:2], dgate1[1:2], dshift1[0:1], dscale1[0:1], zrow,
        jnp.concatenate([sq[0:1], jnp.zeros((1, d - 128), F32)], axis=1),
        zrow], axis=0)
    small_all = allgather8(small_rows, "allgather_small")
    ev_sc = copies_start("reduce_ev_scatter_start",
                         [sb for _, sb in ev_sums] + [half_zone(a, 3, BF16) for a in ev_grads],
                         9, plan_scatter, small_all)
    od_sh = copies_start("reduce_od_share_start",
                         [owner_sum(od_sums[i][0], od_recv[2 + i], chip_core_arr) for i in range(2)],
                         2, plan_share, dxs0)
    tot = small_reduce(small_all + ev_sc[3][0:1, 0:1])
    loss = tot[22, 0] * (0.5 / d)

    dm_rows = []
    for layer in range(2):
        lat = ROW_MOD + 6 * layer
        dm_lat = small_all[:, lat:lat + 3].reshape(8, 3 * d)
        dm_ctx = tot[lat + 3:lat + 6].reshape(1, 3 * d)
        dm_rows.append(jnp.concatenate([dm_lat, dm_ctx, jnp.zeros((7, 3 * d), F32)], axis=0))
    dm_full = jnp.stack(dm_rows)
    dm_mine = lax.dynamic_slice(dm_full, (0, 0, chip * ws_ada), (2, 16, ws_ada))

    def step(w, gr, m, v, name, with_grad=False):
        shape = w.shape
        cols = shape[-1]
        two_d = lambda a: a.reshape(-1, cols)
        res = adamw(two_d(w), two_d(gr), two_d(m), two_d(v), "adamw_" + name, with_grad)
        return tuple(a.reshape(shape) for a in res)

    grad_ada_b = tot[24:30].reshape(2, 3 * d)
    grad_pre_g = tot[0:2]
    grad_post_g = tot[2:4]
    grad_ev_pool_scale = tot[4:5, :half_d]
    grad_ev_conv_b = tot[4:5, half_d:]
    conv_w_tot = tot[5:7].reshape(1, 2 * d)[:, :3 * half_d].reshape(3, N_POOL, g)
    grad_ev_conv_w = lax.dynamic_slice(conv_w_tot, (0, chip, 0), (3, 1, g)).reshape(1, 3, g)
    grad_od_onorm_g = lax.dynamic_slice(tot[7:8], (0, chip * 2 * g), (1, 2 * g))
    dlb_mine = lax.dynamic_slice(tot[8:10], (0, chip * 2 * g), (2, 2 * g))
    grad_lb_logits = lb_logits_grad(lb_logits, dlb_mine)
    upd = {
        "ada_b": step(ada_b, grad_ada_b, m_ada_b, v_ada_b, "ada_b"),
        "pre_g": step(pre_g, grad_pre_g, m_pre_g, v_pre_g, "pre_g"),
        "post_g": step(post_g, grad_post_g, m_post_g, v_post_g, "post_g"),
        "ev_pool_scale": step(ev_pool_scale, grad_ev_pool_scale, m_ev_pool_scale, v_ev_pool_scale, "ev_pool_scale"),
        "ev_conv_w": step(ev_conv_w, grad_ev_conv_w, m_ev_conv_w, v_ev_conv_w, "ev_conv_w"),
        "ev_conv_b": step(ev_conv_b, grad_ev_conv_b, m_ev_conv_b, v_ev_conv_b, "ev_conv_b"),
        "od_onorm_g": step(od_onorm_g, grad_od_onorm_g, m_od_onorm_g, v_od_onorm_g, "od_onorm_g"),
        "lb_logits": step(lb_logits, grad_lb_logits, m_lb_logits, v_lb_logits, "lb_logits"),
    }
    grad_ada_w, delta_ada_w, new_m_ada_w, new_v_ada_w, dctx_part = ada_bwd_adamw(
        s_act, dm_mine, ada_w, m_ada_w, v_ada_w)
    upd["ada_w"] = (delta_ada_w, new_m_ada_w, new_v_ada_w)
    (grad_od_w_in, grad_od_w_out), _ = copies_wait("reduce_od_share_wait", od_sh, plan_share, ev_sc[3])
    upd["od_w_in"] = step(od_w_in, grad_od_w_in[None], m_od_w_in, v_od_w_in, "od_w_in", True)
    upd["od_w_out"] = step(od_w_out, grad_od_w_out[None], m_od_w_out, v_od_w_out, "od_w_out", True)
    grad_od_w_in, grad_od_w_out = upd["od_w_in"][3], upd["od_w_out"][3]
    done_behind = [dctx_part] + [upd[k][0] for k in (
        "od_w_in", "od_w_out", "ada_b", "pre_g", "post_g", "ev_pool_scale", "ev_conv_w", "ev_conv_b",
        "od_onorm_g", "lb_logits")]
    ev_recv, ev_landed = copies_wait("reduce_ev_scatter_wait", ev_sc, plan_scatter, done_behind)
    grad_ev_w_in, grad_ev_w_out, grad_pool_w = share_halves(
        [owner_sum(ev_sums[i][0], ev_recv[3 + i], chip_core_arr) for i in range(3)])
    dctx_all = allgather8(dctx_part[0] + dctx_part[1] + ev_landed[0:1, 0:1], "allgather_dctx")
    grad_c_ctx = c_ctx_grad(dctx_all, c_ctx.reshape(1, d)).reshape(d)
    upd["c_ctx"] = step(c_ctx, grad_c_ctx, m_c_ctx, v_c_ctx, "c_ctx")
    upd["ev_w_in"] = step(ev_w_in, grad_ev_w_in[None], m_ev_w_in, v_ev_w_in, "ev_w_in", True)
    upd["ev_pool_w"] = step(ev_pool_w, grad_pool_w.reshape(1, N_POOL, g // 4, g), m_ev_pool_w, v_ev_pool_w,
                            "ev_pool_w", True)
    upd["ev_w_out"] = step(ev_w_out, grad_ev_w_out[None], m_ev_w_out, v_ev_w_out, "ev_w_out", True)
    grad_ev_w_in, grad_ev_pool_w, grad_ev_w_out = upd["ev_w_in"][3], upd["ev_pool_w"][3], upd["ev_w_out"][3]
    names = ["c_ctx", "ada_w", "ada_b", "pre_g", "post_g", "ev_w_in", "ev_pool_w", "ev_pool_scale",
             "ev_conv_w", "ev_conv_b", "ev_w_out", "od_w_in", "od_onorm_g", "od_w_out", "lb_logits"]
    grads = [grad_c_ctx, grad_ada_w, grad_ada_b, grad_pre_g, grad_post_g, grad_ev_w_in, grad_ev_pool_w,
             grad_ev_pool_scale, grad_ev_conv_w, grad_ev_conv_b, grad_ev_w_out, grad_od_w_in,
             grad_od_onorm_g, grad_od_w_out, grad_lb_logits]
    return (loss, grad_x, *grads, *[upd[k][0] for k in names], *[upd[k][1] for k in names],
            *[upd[k][2] for k in names])
```

```python
import jax
import jax.numpy as jnp
from jax import lax
from jax.experimental import pallas as pl
from jax.experimental.pallas import tpu as pltpu

EPS = 1e-6
GRID_W_LOG2 = 6
CHUNK = 64
HEAD = 128
N_POOL = 4
ADAM_LR, ADAM_B1, ADAM_B2, ADAM_EPS, ADAM_WD, ADAM_STEP = 0.001, 0.9, 0.999, 1e-08, 0.01, 10
VMEM_LIMIT = 56 * 1024 * 1024
MESH = pl.DeviceIdType.MESH
F32, BF16 = jnp.float32, jnp.bfloat16
ANY = pl.BlockSpec(memory_space=pl.ANY)
VMEM = pl.BlockSpec(memory_space=pltpu.VMEM)


def _cp(**kw):
    return pltpu.CompilerParams(vmem_limit_bytes=VMEM_LIMIT, **kw)


def _silu(x):
    return x * jax.nn.sigmoid(x)


def _dsilu(x):
    s = jax.nn.sigmoid(x)
    return s * (1.0 + x * (1.0 - s))


def _dot(a, b, dims=((1,), (0,)), precision=None):
    return lax.dot_general(a, b, (dims, ((), ())), preferred_element_type=F32, precision=precision)


NN = ((1,), (0,))
NT = ((1,), (1,))
TN = ((0,), (0,))


def _row_block(cx):
    return 256 if cx % 256 == 0 else 128


def normmod_bwd(xs, dh, g, scale, dres, cx, res_is_latent_only, dx_latent_only=False, prev=None):
    t, d = xs.shape
    tm = _row_block(cx)
    nctx = cx // tm
    n_prev = 0 if prev is None else 3

    def body(x_ref, dh_ref, g_ref, sc_ref, dres_ref, *rest):
        dx_ref, dg_ref, dsh_ref, dsc_ref = rest[n_prev:n_prev + 4]
        i = pl.program_id(0)
        is_ctx = i < nctx

        @pl.when(i == 0)
        def _():
            for ref in rest[n_prev + 1:n_prev + 4] + rest[n_prev + 5:]:
                ref[...] = jnp.zeros_like(ref)

        x = x_ref[...]
        dh = dh_ref[...]
        gv = g_ref[...]
        rstd = lax.rsqrt(jnp.mean(x * x, axis=-1, keepdims=True) + EPS)
        xhat = x * rstd
        sc = jnp.where(is_ctx, sc_ref[0:1, :], sc_ref[1:2, :])
        dsh = jnp.sum(dh, axis=0, keepdims=True)
        dhx = dh * xhat
        dsc = jnp.sum(dhx * gv, axis=0, keepdims=True)
        dg_ref[...] += jnp.sum(dhx * (1.0 + sc), axis=0, keepdims=True)
        zero = jnp.zeros_like(dsh)
        dsh_ref[0:1, :] += jnp.where(is_ctx, dsh, zero)
        dsh_ref[1:2, :] += jnp.where(is_ctx, zero, dsh)
        dsc_ref[0:1, :] += jnp.where(is_ctx, dsc, zero)
        dsc_ref[1:2, :] += jnp.where(is_ctx, zero, dsc)
        dxhat = dh * (gv * (1.0 + sc))
        dx = rstd * (dxhat - xhat * jnp.mean(dxhat * xhat, axis=-1, keepdims=True))
        res = dres_ref[...]
        if res_is_latent_only:
            res = jnp.where(is_ctx, jnp.zeros_like(res), res)
        dxt = dx + res
        dx_ref[...] = dxt
        if prev is not None:
            y_ref, pg_ref, gate_ref = rest[:3]
            dy_ref, dgate_ref, dpg_ref = rest[7:]
            y = y_ref[...]
            pgv = pg_ref[...]
            rstd_y = lax.rsqrt(jnp.mean(y * y, axis=-1, keepdims=True) + EPS)
            yhat = y * rstd_y
            gt = jnp.where(is_ctx, gate_ref[0:1, :], gate_ref[1:2, :])
            dxy = dxt * yhat
            dgt = jnp.sum(dxy * pgv, axis=0, keepdims=True)
            dgate_ref[0:1, :] += jnp.where(is_ctx, dgt, zero)
            dgate_ref[1:2, :] += jnp.where(is_ctx, zero, dgt)
            dpg_ref[...] += jnp.sum(dxy * gt, axis=0, keepdims=True)
            dyhat = dxt * (gt * pgv)
            dy_ref[...] = (rstd_y * (dyhat - yhat * jnp.mean(dyhat * yhat, axis=-1, keepdims=True))).astype(BF16)

    row = pl.BlockSpec((tm, d), lambda i: (i, 0))
    if res_is_latent_only:
        res_spec = pl.BlockSpec((tm, d), lambda i: (jnp.maximum(i - nctx, 0), 0))
    else:
        res_spec = row
    vec = lambda r: pl.BlockSpec((r, d), lambda i: (0, 0))
    dx_spec = pl.BlockSpec((tm, d), lambda i: (jnp.maximum(i - nctx, 0), 0)) if dx_latent_only else row
    in_specs = [row, row, vec(1), vec(2), res_spec]
    out_specs = [dx_spec, vec(1), vec(2), vec(2)]
    out_shape = [jax.ShapeDtypeStruct((t - cx if dx_latent_only else t, d), F32), jax.ShapeDtypeStruct((1, d), F32),
                 jax.ShapeDtypeStruct((2, d), F32), jax.ShapeDtypeStruct((2, d), F32)]
    if prev is not None:
        in_specs += [row, vec(1), vec(2)]
        out_specs += [row, vec(2), vec(1)]
        out_shape += [jax.ShapeDtypeStruct((t, d), BF16), jax.ShapeDtypeStruct((2, d), F32),
                      jax.ShapeDtypeStruct((1, d), F32)]
    return pl.pallas_call(
        body, name="normmod_bwd", grid=(t // tm,), in_specs=in_specs, out_specs=out_specs, out_shape=out_shape,
        compiler_params=_cp(),
    )(xs, dh, g, scale, dres, *(() if prev is None else prev))


def post_fwd_norm(xs, y, pg, gate, g_next, shift_next, scale_next, cx):
    t, d = xs.shape
    tm = _row_block(cx)
    nctx = cx // tm

    def body(x_ref, y_ref, pg_ref, gate_ref, g_ref, sh_ref, sc_ref, o_ref, h_ref):
        is_ctx = pl.program_id(0) < nctx
        pick = lambda ref: jnp.where(is_ctx, ref[0:1, :], ref[1:2, :])
        y = y_ref[...]
        rstd = lax.rsqrt(jnp.mean(y * y, axis=-1, keepdims=True) + EPS)
        x = x_ref[...] + pick(gate_ref) * ((y * rstd) * pg_ref[...])
        o_ref[...] = x
        rstd = lax.rsqrt(jnp.mean(x * x, axis=-1, keepdims=True) + EPS)
        h_ref[...] = ((x * rstd) * g_ref[...] * (1.0 + pick(sc_ref)) + pick(sh_ref)).astype(BF16)

    row = pl.BlockSpec((tm, d), lambda i: (i, 0))
    vec = lambda r: pl.BlockSpec((r, d), lambda i: (0, 0))
    return pl.pallas_call(
        body, name="post_fwd_norm", grid=(t // tm,),
        in_specs=[row, row, vec(1), vec(2), vec(1), vec(2), vec(2)], out_specs=[row, row],
        out_shape=[jax.ShapeDtypeStruct((t, d), F32), jax.ShapeDtypeStruct((t, d), BF16)],
        compiler_params=_cp(),
    )(xs, y, pg, gate, g_next, shift_next, scale_next)


def post_loss(xs, y, pg, gate, target, cx):
    t, d = xs.shape
    n = y.shape[0]
    tm = _row_block(cx)
    nctx = cx // tm

    def body(x_ref, y_ref, pg_ref, gate_ref, tgt_ref, sq_ref, dx_ref, dy_ref, dgate_ref, dpg_ref):
        @pl.when(pl.program_id(0) == 0)
        def _():
            sq_ref[...] = jnp.zeros_like(sq_ref)
            dgate_ref[...] = jnp.zeros_like(dgate_ref)
            dpg_ref[...] = jnp.zeros_like(dpg_ref)

        y = y_ref[...]
        pgv = pg_ref[...]
        gt = gate_ref[1:2, :]
        rstd = lax.rsqrt(jnp.mean(y * y, axis=-1, keepdims=True) + EPS)
        yhat = y * rstd
        err = x_ref[...] + gt * (yhat * pgv) - tgt_ref[...]
        sq_ref[...] += jnp.sum(err * err)
        dx = err * (1.0 / d)
        dx_ref[...] = dx
        dxy = dx * yhat
        dgate_ref[1:2, :] += jnp.sum(dxy * pgv, axis=0, keepdims=True)
        dpg_ref[...] += jnp.sum(dxy * gt, axis=0, keepdims=True)
        dyhat = dx * (gt * pgv)
        dy_ref[...] = (rstd * (dyhat - yhat * jnp.mean(dyhat * yhat, axis=-1, keepdims=True))).astype(BF16)

    row = pl.BlockSpec((tm, d), lambda i: (i, 0))
    xrow = pl.BlockSpec((tm, d), lambda i: (i + nctx, 0))
    vec = lambda r: pl.BlockSpec((r, d), lambda i: (0, 0))
    return pl.pallas_call(
        body, name="post_loss", grid=(n // tm,),
        in_specs=[xrow, row, vec(1), vec(2), row],
        out_specs=[pl.BlockSpec((8, 128), lambda i: (0, 0)), row, row, vec(2), vec(1)],
        out_shape=[jax.ShapeDtypeStruct((8, 128), F32), jax.ShapeDtypeStruct((n, d), F32),
                   jax.ShapeDtypeStruct((n, d), BF16), jax.ShapeDtypeStruct((2, d), F32),
                   jax.ShapeDtypeStruct((1, d), F32)],
        compiler_params=_cp(),
    )(xs, y, pg, gate, target)


def _split_rows(m):
    for cand in (1152, 1024, 768, 512, 384, 256, 128):
        if m % cand == 0 and m // cand >= 2:
            return cand
    return m


def mm_nn(a, w3, sec, tn, name):
    m, k = a.shape
    q, _, ws = w3.shape
    n = q * ws
    tpq, tps = ws // tn, sec // tn
    tm = next(c for c in (768, 512, 256, 128) if m % c == 0)

    def body(a_ref, w_ref, o_ref):
        w = w_ref[...]

        def step(i, carry):
            rows = pl.ds(pl.multiple_of(i * tm, tm), tm)
            o_ref[rows, :] = _dot(a_ref[rows, :], w)
            return carry

        lax.fori_loop(0, m // tm, step, 0)

    return pl.pallas_call(
        body, name=name, grid=(n // tn,),
        in_specs=[pl.BlockSpec((m, k), lambda j: (0, 0)),
                  pl.BlockSpec((None, k, tn), lambda j: (j // tpq, 0, j % tpq))],
        out_specs=pl.BlockSpec((None, m, tn), lambda j: (j // tps, 0, j % tps)),
        out_shape=jax.ShapeDtypeStruct((n // sec, m, sec), F32), compiler_params=_cp(),
    )(a, w3)


def _two_stacks(a3, b3, tn):
    sec = a3.shape[2]
    tps = sec // tn
    n1 = a3.shape[0] * tps
    first = lambda j: (jnp.minimum(j, n1 - 1) // tps, jnp.minimum(j, n1 - 1) % tps)
    second = lambda j: (jnp.maximum(j - n1, 0) // tps, jnp.maximum(j - n1, 0) % tps)
    return n1, first, second


def mm_nt(a3, b3, w3, tn, name):
    if b3 is None:
        b3 = a3
    _, m, sec = a3.shape
    q, k, ws = w3.shape
    n = q * ws
    tpq = ws // tn
    mb = _split_rows(m)
    n1, first, second = _two_stacks(a3, b3, tn)

    def body(a_ref, b_ref, w_ref, o_ref):
        j = pl.program_id(1)

        @pl.when(j == 0)
        def _():
            o_ref[...] = jnp.zeros_like(o_ref)

        @pl.when(j < n1)
        def _():
            o_ref[...] += _dot(a_ref[...], w_ref[...], NT)

        @pl.when(j >= n1)
        def _():
            o_ref[...] += _dot(b_ref[...], w_ref[...], NT)

    return pl.pallas_call(
        body, name=name, grid=(m // mb, n // tn),
        in_specs=[pl.BlockSpec((None, mb, tn), lambda i, j: (first(j)[0], i, first(j)[1])),
                  pl.BlockSpec((None, mb, tn), lambda i, j: (second(j)[0], i, second(j)[1])),
                  pl.BlockSpec((None, k, tn), lambda i, j: (j // tpq, 0, j % tpq))],
        out_specs=pl.BlockSpec((mb, k), lambda i, j: (i, 0)),
        out_shape=jax.ShapeDtypeStruct((m, k), F32), compiler_params=_cp(),
    )(a3, b3, w3)


def mm_tn(a, b3, c3, ws, tn, name):
    m, k = a.shape
    sec = b3.shape[2]
    n = (b3.shape[0] + (0 if c3 is None else c3.shape[0])) * sec
    if c3 is None:
        c3 = b3
    tpq = ws // tn
    kb = 256 if k % 256 == 0 else 128
    n1, first, second = _two_stacks(b3, c3, tn)

    def body(a_ref, b_ref, c_ref, o_ref):
        def product(rhs_ref):
            rhs = rhs_ref[...]
            for i in range(k // kb):
                o_ref[i * kb:(i + 1) * kb, :] = _dot(a_ref[:, i * kb:(i + 1) * kb], rhs, TN).astype(BF16)

        @pl.when(pl.program_id(0) < n1)
        def _():
            product(b_ref)

        @pl.when(pl.program_id(0) >= n1)
        def _():
            product(c_ref)

    return pl.pallas_call(
        body, name=name, grid=(n // tn,),
        in_specs=[pl.BlockSpec((m, k), lambda j: (0, 0)),
                  pl.BlockSpec((None, m, tn), lambda j: (first(j)[0], 0, first(j)[1])),
                  pl.BlockSpec((None, m, tn), lambda j: (second(j)[0], 0, second(j)[1]))],
        out_specs=pl.BlockSpec((None, k, tn), lambda j: (j // tpq, 0, j % tpq)),
        out_shape=jax.ShapeDtypeStruct((n // ws, k, ws), BF16), compiler_params=_cp(),
    )(a, b3, c3)


POOL_REACH = 8 << GRID_W_LOG2


def _token_parts(tok, cx):
    lat = tok - cx
    return tok < cx, lat >> GRID_W_LOG2, lat & ((1 << GRID_W_LOG2) - 1)


def _pool_mask(gi, row0, col0, tm, ncols, cx, transposed):
    half = jnp.left_shift(1, gi)
    r = lax.broadcasted_iota(jnp.int32, (tm, 1), 0) + row0
    c = lax.broadcasted_iota(jnp.int32, (1, ncols), 1) + col0
    out_tok, src_tok = (c, r) if transposed else (r, c)
    o_ctx, o_row, o_col = _token_parts(out_tok, cx)
    s_ctx, s_row, s_col = _token_parts(src_tok, cx)

    def inside(o, s):
        return (s >= o - half) & (s <= o + half - 1)

    ctx_hit = o_ctx & s_ctx & inside(out_tok, src_tok)
    lat_hit = (~o_ctx) & (~s_ctx) & inside(o_row, s_row) & inside(o_col, s_col)
    return jnp.where(ctx_hit | lat_hit, 1.0, 0.0).astype(BF16)


def _pool_inv_count(gi, row0, tm, cx, seq):
    half = jnp.left_shift(1, gi)
    r = lax.broadcasted_iota(jnp.int32, (tm, 1), 0) + row0
    is_ctx, row, col = _token_parts(r, cx)

    def count(pos, size):
        return jnp.minimum(pos + half - 1, size - 1) - jnp.maximum(pos - half, 0) + 1

    cnt = jnp.where(is_ctx, count(r, cx), count(row, seq >> GRID_W_LOG2) * count(col, 1 << GRID_W_LOG2))
    return 1.0 / cnt.astype(F32)


def _lat_band(tm):
    side = POOL_REACH // tm
    return side, 2 * side + 1


def _lat_mask(gi, tm, cx, transposed):
    side, band = _lat_band(tm)
    return _pool_mask(gi, cx + side * tm, cx, tm, band * tm, cx, transposed)


def _store_padded_lat(dst_ref, lat, tm):
    side, _ = _lat_band(tm)
    seq = lat.shape[0]
    zeros = jnp.zeros((side * tm, lat.shape[1]), dst_ref.dtype)
    dst_ref[0:side * tm, :] = zeros
    dst_ref[side * tm + seq:, :] = zeros
    dst_ref[side * tm:side * tm + seq, :] = lat.astype(dst_ref.dtype)


def mix_a_fwd(z0, pool_w, pool_scale, cx):
    _, t, half_d = z0.shape
    g = half_d // N_POOL
    seq = t - cx
    tm = _row_block(cx)
    side, band = _lat_band(tm)

    def body(v_ref, ag_ref, w_ref, sc_ref, u_ref, vlat_ref, mask_ref):
        gi = pl.program_id(0)
        w = w_ref[...].astype(BF16)
        sc = sc_ref[...]
        _store_padded_lat(vlat_ref, v_ref[cx:, :], tm)
        mask_ref[...] = _lat_mask(gi, tm, cx, False)

        def finish(row0, window_sum):
            rows = pl.ds(row0, tm)
            pooled = window_sum * _pool_inv_count(gi, row0, tm, cx, seq) - v_ref[rows, :]
            mixed = _dot(pooled.astype(BF16), w) * sc
            u_ref[rows, :] = (mixed * _silu(ag_ref[rows, :])).astype(BF16)

        vctx = v_ref[0:cx, :].astype(BF16)
        for i in range(cx // tm):
            finish(i * tm, _dot(_pool_mask(gi, i * tm, 0, tm, cx, cx, False), vctx))

        def step(j, carry):
            src = vlat_ref[pl.ds(pl.multiple_of(j * tm, tm), band * tm), :]
            finish(pl.multiple_of(cx + j * tm, tm), _dot(mask_ref[...], src))
            return carry

        lax.fori_loop(0, seq // tm, step, 0)

    sec = lambda s: pl.BlockSpec((None, t, g), lambda j: (s, 0, j))
    return pl.pallas_call(
        body, name="mix_a_fwd", grid=(N_POOL,),
        in_specs=[sec(0), sec(1), pl.BlockSpec((None, g, g), lambda j: (j, 0, 0)),
                  pl.BlockSpec((1, g), lambda j: (0, j))],
        out_specs=pl.BlockSpec((t, g), lambda j: (0, j)),
        out_shape=jax.ShapeDtypeStruct((t, 2 * half_d), BF16),
        scratch_shapes=[pltpu.VMEM((seq + 2 * side * tm, g), BF16), pltpu.VMEM((tm, band * tm), BF16)],
        compiler_params=_cp(),
    )(z0, z0, pool_w, pool_scale)


def mix_a_bwd(z0, du, pool_w, pool_scale, cx):
    _, t, half_d = z0.shape
    g = half_d // N_POOL
    seq = t - cx
    tm = _row_block(cx)
    gq = g // 4
    side, band = _lat_band(tm)

    def body(v_ref, ag_ref, du_ref, w_ref, sc_ref, dz_ref, dw_ref, dsc_ref,
             vlat_ref, mask_ref, pooled_ref, dmx_ref, dpl_ref, wlat_ref, wctx_ref):
        gi = pl.program_id(0)
        w = w_ref[...].astype(BF16)
        sc = sc_ref[...]
        _store_padded_lat(vlat_ref, v_ref[cx:, :], tm)
        _store_padded_lat(wlat_ref, jnp.zeros((seq, g), BF16), tm)
        mask_ref[...] = _lat_mask(gi, tm, cx, False)

        def first(row0, window_sum, weighted_ref, weighted_row0):
            rows = pl.ds(row0, tm)
            inv = _pool_inv_count(gi, row0, tm, cx, seq)
            pooled = (window_sum * inv - v_ref[rows, :]).astype(BF16)
            pooled_ref[rows, :] = pooled
            mixed = _dot(pooled, w)
            ag = ag_ref[rows, :]
            duv = du_ref[rows, :]
            dz_ref[1, rows, :] = (duv * (mixed * sc) * _dsilu(ag)).astype(BF16)
            dms = duv * _silu(ag)
            dmixed = (dms * sc).astype(BF16)
            dmx_ref[rows, :] = dmixed
            dpooled = _dot(dmixed, w, NT)
            dpl_ref[rows, :] = dpooled
            weighted_ref[pl.ds(weighted_row0, tm), :] = (dpooled * inv).astype(BF16)
            return jnp.sum(dms * mixed, axis=0, keepdims=True)

        dsc = jnp.zeros((1, g), F32)
        vctx = v_ref[0:cx, :].astype(BF16)
        for i in range(cx // tm):
            dsc += first(i * tm, _dot(_pool_mask(gi, i * tm, 0, tm, cx, cx, False), vctx), wctx_ref, i * tm)

        def first_lat(j, acc):
            src = vlat_ref[pl.ds(pl.multiple_of(j * tm, tm), band * tm), :]
            return acc + first(pl.multiple_of(cx + j * tm, tm), _dot(mask_ref[...], src),
                               wlat_ref, pl.multiple_of((side + j) * tm, tm))

        dsc_ref[...] = lax.fori_loop(0, seq // tm, first_lat, dsc)
        dw = _dot(pooled_ref[...], dmx_ref[...], TN)
        for qi in range(4):
            dw_ref[qi] = dw[qi * gq:(qi + 1) * gq, :]

        wctx = wctx_ref[...]
        for i in range(cx // tm):
            rows = pl.ds(i * tm, tm)
            dz_ref[0, rows, :] = (_dot(_pool_mask(gi, i * tm, 0, tm, cx, cx, True), wctx)
                                  - dpl_ref[rows, :]).astype(BF16)
        mask_ref[...] = _lat_mask(gi, tm, cx, True)

        def second_lat(j, carry):
            rows = pl.ds(pl.multiple_of(cx + j * tm, tm), tm)
            src = wlat_ref[pl.ds(pl.multiple_of(j * tm, tm), band * tm), :]
            dz_ref[0, rows, :] = (_dot(mask_ref[...], src) - dpl_ref[rows, :]).astype(BF16)
            return carry

        lax.fori_loop(0, seq // tm, second_lat, 0)

    sec = lambda s: pl.BlockSpec((None, t, g), lambda j: (s, 0, j))
    padded = pltpu.VMEM((seq + 2 * side * tm, g), BF16)
    return pl.pallas_call(
        body, name="mix_a_bwd", grid=(N_POOL,),
        in_specs=[sec(0), sec(1), pl.BlockSpec((t, g), lambda j: (0, j)),
                  pl.BlockSpec((None, g, g), lambda j: (j, 0, 0)),
                  pl.BlockSpec((1, g), lambda j: (0, j))],
        out_specs=[pl.BlockSpec((2, t, g), lambda j: (0, 0, j)),
                   pl.BlockSpec((4, None, gq, g), lambda j: (0, j, 0, 0)),
                   pl.BlockSpec((1, g), lambda j: (0, j))],
        out_shape=[jax.ShapeDtypeStruct((2, t, half_d), BF16),
                   jax.ShapeDtypeStruct((4, N_POOL, gq, g), F32),
                   jax.ShapeDtypeStruct((1, half_d), F32)],
        scratch_shapes=[padded, pltpu.VMEM((tm, band * tm), BF16), pltpu.VMEM((t, g), BF16),
                        pltpu.VMEM((t, g), BF16), pltpu.VMEM((t, g), F32), padded, pltpu.VMEM((cx, g), BF16)],
        compiler_params=_cp(),
    )(z0, z0, du, pool_w, pool_scale)


def _conv_masks(t, cx):
    r = lax.broadcasted_iota(jnp.int32, (t, 1), 0)
    has_prev = jnp.where((r == 0) | (r == cx), 0.0, 1.0)
    has_next = jnp.where((r == cx - 1) | (r == t - 1), 0.0, 1.0)
    return has_prev, has_next


def mix_b_fwd(z0, conv_w, conv_b, u, cx):
    _, t, half_d = z0.shape
    gb = 128
    off = half_d // gb

    def body(bx_ref, bb_ref, bc_ref, bg_ref, w_ref, b_ref, _, u_ref):
        has_prev, has_next = _conv_masks(t, cx)
        tt = bc_ref[...] * bx_ref[...]
        prev = pltpu.roll(tt, 1, 0) * has_prev
        nxt = pltpu.roll(tt, t - 1, 0) * has_next
        cv = prev * w_ref[0:1, :] + tt * w_ref[1:2, :] + nxt * w_ref[2:3, :] + b_ref[...]
        u_ref[...] = (bb_ref[...] * cv * _silu(bg_ref[...])).astype(BF16)

    sec = lambda s: pl.BlockSpec((None, t, gb), lambda j: (s, 0, j))
    return pl.pallas_call(
        body, name="mix_b_fwd", grid=(half_d // gb,),
        in_specs=[sec(2), sec(3), sec(4), sec(5), pl.BlockSpec((3, gb), lambda j: (0, j)),
                  pl.BlockSpec((1, gb), lambda j: (0, j)), ANY],
        out_specs=pl.BlockSpec((t, gb), lambda j: (0, j + off)),
        out_shape=jax.ShapeDtypeStruct((t, 2 * half_d), BF16), input_output_aliases={6: 0},
        compiler_params=_cp(),
    )(z0, z0, z0, z0, conv_w, conv_b, u)


def mix_b_bwd(z0, du, conv_w, conv_b, cx):
    _, t, half_d = z0.shape
    gb = 128
    off = half_d // gb

    def body(bx_ref, bb_ref, bc_ref, bg_ref, du_ref, w_ref, b_ref, dz_ref, dw_ref, db_ref):
        has_prev, has_next = _conv_masks(t, cx)
        bx, bb, bc, bg = bx_ref[...], bb_ref[...], bc_ref[...], bg_ref[...]
        duv = du_ref[...]
        tt = bc * bx
        prev = pltpu.roll(tt, 1, 0) * has_prev
        nxt = pltpu.roll(tt, t - 1, 0) * has_next
        w0, w1, w2 = w_ref[0:1, :], w_ref[1:2, :], w_ref[2:3, :]
        cv = prev * w0 + tt * w1 + nxt * w2 + b_ref[...]
        sg = _silu(bg)
        dz_ref[1] = (duv * cv * sg).astype(BF16)
        dz_ref[3] = (duv * bb * cv * _dsilu(bg)).astype(BF16)
        dcv = duv * bb * sg
        dw_ref[0:1, :] = jnp.sum(dcv * prev, axis=0, keepdims=True)
        dw_ref[1:2, :] = jnp.sum(dcv * tt, axis=0, keepdims=True)
        dw_ref[2:3, :] = jnp.sum(dcv * nxt, axis=0, keepdims=True)
        db_ref[...] = jnp.sum(dcv, axis=0, keepdims=True)
        dt = (pltpu.roll(dcv * has_prev, t - 1, 0) * w0 + dcv * w1
              + pltpu.roll(dcv * has_next, 1, 0) * w2)
        dz_ref[0] = (dt * bc).astype(BF16)
        dz_ref[2] = (dt * bx).astype(BF16)

    sec = lambda s: pl.BlockSpec((None, t, gb), lambda j: (s, 0, j))
    return pl.pallas_call(
        body, name="mix_b_bwd", grid=(half_d // gb,),
        in_specs=[sec(2), sec(3), sec(4), sec(5), pl.BlockSpec((t, gb), lambda j: (0, j + off)),
                  pl.BlockSpec((3, gb), lambda j: (0, j)), pl.BlockSpec((1, gb), lambda j: (0, j))],
        out_specs=[pl.BlockSpec((4, t, gb), lambda j: (0, 0, j)),
                   pl.BlockSpec((3, gb), lambda j: (0, j)), pl.BlockSpec((1, gb), lambda j: (0, j))],
        out_shape=[jax.ShapeDtypeStruct((4, t, half_d), BF16),
                   jax.ShapeDtypeStruct((3, half_d), F32), jax.ShapeDtypeStruct((1, half_d), F32)],
        compiler_params=_cp(),
    )(z0, z0, z0, z0, du, conv_w, conv_b)


def _lower_bound(lbl_ref, d):
    l0, l1, l2 = lbl_ref[d, 0:1, :], lbl_ref[d, 1:2, :], lbl_ref[d, 2:3, :]
    mx = jnp.maximum(jnp.maximum(l0, l1), l2)
    e0, e1, e2 = jnp.exp(l0 - mx), jnp.exp(l1 - mx), jnp.exp(l2 - mx)
    inv = 1.0 / (e0 + e1 + e2)
    return (e0 + e1) * inv, (e0 * inv, e1 * inv, e2 * inv)


def _chunk_consts(d):
    r = lax.broadcasted_iota(jnp.int32, (CHUNK, CHUNK), 0)
    c = lax.broadcasted_iota(jnp.int32, (CHUNK, CHUNK), 1)
    keep = (c <= r) if d == 0 else (c >= r)
    return jnp.where(keep, 1.0, 0.0).astype(F32), keep


def _chunk_of_step(s, d, nc, ncc):
    if d == 0:
        return s
    return jnp.where(s < ncc, ncc - 1 - s, nc - 1 + ncc - s)


def _gates(z, lbv):
    e = jnp.exp(-jnp.abs(z))
    r = 1.0 / (1.0 + e)
    er = e * r
    pos = z >= 0.0
    sig = jnp.where(pos, r, er)
    nsig = jnp.where(pos, er, r)
    return sig, nsig, lbv + (1.0 - lbv) * sig


def _split3(x):
    hi = x.astype(BF16)
    r1 = x - hi.astype(F32)
    mid = r1.astype(BF16)
    lo = (r1 - mid.astype(F32)).astype(BF16)
    return jnp.concatenate([hi, mid, lo], axis=1)


def _cumsum_chunk(cum, x):
    y = _dot(cum, _split3(x))
    return y[:, :HEAD] + y[:, HEAD:2 * HEAD] + y[:, 2 * HEAD:]


def _chunk_rows(n):
    return pl.ds(pl.multiple_of(n * CHUNK, CHUNK), CHUNK)


def _group(nc, prefer=(4, 3, 2, 1)):
    return next(u for u in prefer if nc % u == 0)


WIDE_GROUP = (18, 12, 6, 4, 3, 2, 1)


def _decay_pass(lf_ref, bc_ref, dec_ref, cum, nc):
    grp = _group(nc, WIDE_GROUP)

    def step(m, carry):
        ns = [m * grp + u for u in range(grp)]
        lfc = [lf_ref[_chunk_rows(n), :] for n in ns]
        bc = [_cumsum_chunk(cum, x) for x in lfc]
        for u, n in enumerate(ns):
            bc_ref[_chunk_rows(n), :] = bc[u]
            dec_ref[n] = jnp.broadcast_to(jnp.exp(jnp.sum(lfc[u], axis=0, keepdims=True)), (8, HEAD))
        return carry

    lax.fori_loop(0, nc // grp, step, 0)


def hgrn_fwd(z1, lbl, onorm, cx):
    _, t, d = z1.shape
    seq = t - cx
    nc, ncc = t // CHUNK, cx // CHUNK

    grp, sgrp = _group(nc, WIDE_GROUP), _group(nc, WIDE_GROUP)

    def body(zf_ref, zb_ref, v_ref, q_ref, g_ref, lbl_ref, on_ref, o_ref, r_ref, bcs_ref, ks_ref, decs_ref,
             lf_ref, k_ref, bc_ref, dec_ref, qd_ref, ki_ref, oacc_ref, ds_ref):
        for dr, z_ref in ((0, zf_ref), (1, zb_ref)):
            lbv, _ = _lower_bound(lbl_ref, dr)
            _, nsig, f = _gates(z_ref[...], lbv)
            lf_ref[...] = jnp.log(f)
            k_ref[...] = (1.0 - lbv) * nsig
            cum, keep = _chunk_consts(dr)
            _decay_pass(lf_ref, bc_ref, dec_ref, cum.astype(BF16), nc)
            bc = bc_ref[...]
            bcs_ref[dr] = bc
            ks_ref[dr] = k_ref[...]
            decs_ref[dr] = dec_ref[...]
            qd_ref[...] = (q_ref[...] * jnp.exp(bc)).astype(BF16)
            ki_ref[...] = (k_ref[...] * jnp.exp(-bc)).astype(BF16)

            def local_step(m, carry, dr=dr, keep=keep):
                ns = [m * grp + u for u in range(grp)]
                rows = [_chunk_rows(n) for n in ns]
                qd = [qd_ref[r, :] for r in rows]
                ki = [ki_ref[r, :] for r in rows]
                vc = [v_ref[r, :].astype(BF16) for r in rows]
                sc = [_dot(qd[u], ki[u], NT) for u in range(grp)]
                inc = [_dot(vc[u], ki[u], TN) for u in range(grp)]
                a = [jnp.where(keep, s, 0.0).astype(BF16) for s in sc]
                intra = [_dot(a[u], vc[u]) for u in range(grp)]
                for u in range(grp):
                    ds_ref[ns[u]] = inc[u] * dec_ref[ns[u]][0:1, :]
                    if dr == 0:
                        oacc_ref[rows[u], :] = intra[u]
                    else:
                        oacc_ref[rows[u], :] += intra[u]
                return carry

            lax.fori_loop(0, nc // grp, local_step, 0)

            def state_step(m, st, dr=dr):
                ns = [_chunk_of_step(m * sgrp + u, dr, nc, ncc) for u in range(sgrp)]
                rows = [_chunk_rows(n) for n in ns]
                sts = []
                for n in ns:
                    sts.append(st.astype(BF16))
                    st = st * dec_ref[n][0:1, :] + ds_ref[n]
                inter = [_dot(qd_ref[rows[u], :], sts[u], NT) for u in range(sgrp)]
                for u in range(sgrp):
                    oacc_ref[rows[u], :] += inter[u]
                return st

            lax.fori_loop(0, nc // sgrp, state_step, jnp.zeros((HEAD, HEAD), F32))

        o = oacc_ref[cx:, :]
        o_ref[...] = o
        rstd = lax.rsqrt(jnp.mean(o * o, axis=-1, keepdims=True) + EPS)
        r_ref[...] = (o * rstd * on_ref[...] * _silu(g_ref[cx:, :])).astype(BF16)

    sec = lambda s: pl.BlockSpec((None, t, HEAD), lambda h: (s, 0, h))
    col = pl.BlockSpec((seq, HEAD), lambda h: (0, h))
    tf32, tb16 = pltpu.VMEM((t, HEAD), F32), pltpu.VMEM((t, HEAD), BF16)
    return pl.pallas_call(
        body, name="hgrn_fwd", grid=(d // HEAD,),
        in_specs=[sec(0), sec(1), sec(2), sec(3), sec(4),
                  pl.BlockSpec((2, 3, HEAD), lambda h: (0, 0, h)), pl.BlockSpec((1, HEAD), lambda h: (0, h))],
        out_specs=[col, col, pl.BlockSpec((2, t, HEAD), lambda h: (0, 0, h)),
                   pl.BlockSpec((2, t, HEAD), lambda h: (0, 0, h)),
                   pl.BlockSpec((2, nc, 8, HEAD), lambda h: (0, 0, 0, h))],
        out_shape=[jax.ShapeDtypeStruct((seq, d), F32), jax.ShapeDtypeStruct((seq, d), BF16),
                   jax.ShapeDtypeStruct((2, t, d), F32), jax.ShapeDtypeStruct((2, t, d), F32),
                   jax.ShapeDtypeStruct((2, nc, 8, d), F32)],
        scratch_shapes=[tf32, tf32, tf32, pltpu.VMEM((nc, 8, HEAD), F32), tb16, tb16, tf32,
                        pltpu.VMEM((nc, HEAD, HEAD), F32)],
        compiler_params=_cp(),
    )(z1, z1, z1, z1, z1, lbl, onorm)


def hgrn_bwd(z1, lbl, onorm, o, dr_out, bcs, ks, decs, cx):
    _, t, d = z1.shape
    seq = t - cx
    nc, ncc = t // CHUNK, cx // CHUNK

    grp2, grp = _group(nc, (12, 9, 6, 4, 3, 2, 1)), _group(nc, (18, 12, 9, 6, 4, 3, 2, 1))

    def body(zf_ref, zb_ref, v_ref, q_ref, g_ref, lbl_ref, on_ref, o_ref, dr_ref, bcs_ref, ks_ref, decs_ref,
             dz_ref, don_ref, dlb_ref,
             qd_ref, ki_ref, do_ref, dqd_ref, dki_ref, dq_ref, dv_ref, ds_ref, dsl_ref):
        o = o_ref[...]
        g = g_ref[cx:, :]
        drv = dr_ref[...]
        onv = on_ref[...]
        rstd = lax.rsqrt(jnp.mean(o * o, axis=-1, keepdims=True) + EPS)
        ohat = o * rstd
        sg = _silu(g)
        don_ref[...] = jnp.sum(drv * ohat * sg, axis=0, keepdims=True)
        dz_ref[4, :cx, :] = jnp.zeros((cx, HEAD), BF16)
        dz_ref[4, cx:, :] = (drv * ohat * onv * _dsilu(g)).astype(BF16)
        dohat = drv * onv * sg
        do_ref[:cx, :] = jnp.zeros((cx, HEAD), BF16)
        do_ref[cx:, :] = (rstd * (dohat - ohat * jnp.mean(dohat * ohat, axis=-1, keepdims=True))).astype(BF16)

        for dr, z_ref in ((0, zf_ref), (1, zb_ref)):
            lbv, _ = _lower_bound(lbl_ref, dr)
            k_ref, bc_ref, dec_ref = ks_ref.at[dr], bcs_ref.at[dr], decs_ref.at[dr]
            _, keep = _chunk_consts(dr)
            cum_t = _chunk_consts(1 - dr)[0].astype(BF16)
            bc = bc_ref[...]
            qd_ref[...] = (q_ref[...] * jnp.exp(bc)).astype(BF16)
            ki_ref[...] = (k_ref[...] * jnp.exp(-bc)).astype(BF16)

            def local_step(m, carry, dr=dr, keep=keep):
                ns = [m * grp + u for u in range(grp)]
                rows = [_chunk_rows(n) for n in ns]
                rng = range(grp)
                qd = [qd_ref[r, :] for r in rows]
                ki = [ki_ref[r, :] for r in rows]
                doc = [do_ref[r, :] for r in rows]
                vc = [v_ref[r, :].astype(BF16) for r in rows]
                sc = [_dot(qd[u], ki[u], NT) for u in rng]
                dsc = [_dot(doc[u], vc[u], NT) for u in rng]
                inc = [_dot(vc[u], ki[u], TN) for u in rng]
                dinc = [_dot(doc[u], qd[u], TN) for u in rng]
                a = [jnp.where(keep, s, 0.0).astype(BF16) for s in sc]
                da = [jnp.where(keep, s, 0.0).astype(BF16) for s in dsc]
                dqd = [_dot(da[u], ki[u]) for u in rng]
                dki = [_dot(da[u], qd[u], TN) for u in rng]
                dv = [_dot(a[u], doc[u], TN) for u in rng]
                for u in rng:
                    ds_ref[ns[u]] = inc[u] * dec_ref[ns[u]][0:1, :]
                    dsl_ref[ns[u]] = dinc[u]
                    dqd_ref[rows[u], :] = dqd[u]
                    dki_ref[rows[u], :] = dki[u]
                    if dr == 0:
                        dv_ref[rows[u], :] = dv[u]
                    else:
                        dv_ref[rows[u], :] += dv[u]
                return carry

            lax.fori_loop(0, nc // grp, local_step, 0)

            def state_step(s, st, dr=dr):
                n = _chunk_of_step(s, dr, nc, ncc)
                inc = ds_ref[n]
                ds_ref[n] = st
                return st * dec_ref[n][0:1, :] + inc

            lax.fori_loop(0, nc, state_step, jnp.zeros((HEAD, HEAD), F32), unroll=4)

            def dstate_step(s, dst, dr=dr):
                n = _chunk_of_step(nc - 1 - s, dr, nc, ncc)
                inc = dsl_ref[n]
                dsl_ref[n] = dst
                return inc + dst * dec_ref[n][0:1, :]

            lax.fori_loop(0, nc, dstate_step, jnp.zeros((HEAD, HEAD), F32), unroll=4)

            def grad_step(m, carry, dr=dr, cum_t=cum_t):
                ns = [m * grp2 + u for u in range(grp2)]
                rows = [_chunk_rows(n) for n in ns]
                rng = range(grp2)
                st0 = [ds_ref[n] for n in ns]
                dst = [dsl_ref[n] for n in ns]
                dstb = [x.astype(BF16) for x in dst]
                dec = [dec_ref[n][0:1, :] for n in ns]
                doc = [do_ref[r, :] for r in rows]
                vc = [v_ref[r, :].astype(BF16) for r in rows]
                e = [jnp.exp(bc_ref[r, :]) for r in rows]
                einv = [jnp.exp(-bc_ref[r, :]) for r in rows]
                qd = [q_ref[rows[u], :] * e[u] for u in rng]
                ki = [k_ref[rows[u], :] * einv[u] for u in rng]
                kd = [ki[u] * dec[u] for u in rng]
                dqd_st = [_dot(doc[u], st0[u].astype(BF16)) for u in rng]
                dkd = [_dot(vc[u], dstb[u]) for u in rng]
                dv_st = [_dot(kd[u].astype(BF16), dstb[u], NT) for u in rng]
                dqd = [dqd_ref[rows[u], :] + dqd_st[u] for u in rng]
                dki = [dki_ref[r, :] for r in rows]
                dbc = [dqd[u] * qd[u] - dki[u] * ki[u] - dkd[u] * kd[u] for u in rng]
                cs = [_cumsum_chunk(cum_t, x) for x in dbc]
                for u in rng:
                    ddec = jnp.sum(dst[u] * st0[u], axis=0, keepdims=True)
                    dbl = jnp.sum(dkd[u] * kd[u], axis=0, keepdims=True) + ddec * dec[u]
                    dv_ref[rows[u], :] += dv_st[u]
                    dqd_ref[rows[u], :] = cs[u] + dbl
                    dki_ref[rows[u], :] = dki[u] * einv[u] + dkd[u] * (einv[u] * dec[u])
                    if dr == 0:
                        dq_ref[rows[u], :] = dqd[u] * e[u]
                    else:
                        dq_ref[rows[u], :] += dqd[u] * e[u]
                return carry

            lax.fori_loop(0, nc // grp2, grad_step, 0)

            sig, nsig, f = _gates(z_ref[...], lbv)
            common = (dqd_ref[...] / f - dki_ref[...]) * nsig
            dz_ref[dr] = (common * ((1.0 - lbv) * sig)).astype(BF16)
            dlb_ref[dr:dr + 1, :] = jnp.sum(common, axis=0, keepdims=True)

        dz_ref[2] = dv_ref[...].astype(BF16)
        dz_ref[3] = dq_ref[...].astype(BF16)

    sec = lambda s: pl.BlockSpec((None, t, HEAD), lambda h: (s, 0, h))
    col = pl.BlockSpec((seq, HEAD), lambda h: (0, h))
    tf32, tb16 = pltpu.VMEM((t, HEAD), F32), pltpu.VMEM((t, HEAD), BF16)
    states = pltpu.VMEM((nc, HEAD, HEAD), F32)
    return pl.pallas_call(
        body, name="hgrn_bwd", grid=(d // HEAD,),
        in_specs=[sec(0), sec(1), sec(2), sec(3), sec(4),
                  pl.BlockSpec((2, 3, HEAD), lambda h: (0, 0, h)), pl.BlockSpec((1, HEAD), lambda h: (0, h)),
                  col, col, pl.BlockSpec((2, t, HEAD), lambda h: (0, 0, h)),
                  pl.BlockSpec((2, t, HEAD), lambda h: (0, 0, h)),
                  pl.BlockSpec((2, nc, 8, HEAD), lambda h: (0, 0, 0, h))],
        out_specs=[pl.BlockSpec((5, t, HEAD), lambda h: (0, 0, h)),
                   pl.BlockSpec((1, HEAD), lambda h: (0, h)), pl.BlockSpec((2, HEAD), lambda h: (0, h))],
        out_shape=[jax.ShapeDtypeStruct((5, t, d), BF16), jax.ShapeDtypeStruct((1, d), F32),
                   jax.ShapeDtypeStruct((2, d), F32)],
        scratch_shapes=[tb16, tb16, tb16, tf32, tf32, tf32, tf32, states, states],
        compiler_params=_cp(),
    )(z1, z1, z1, z1, z1, lbl, onorm, o, dr_out, bcs, ks, decs)


def _place():
    x, y, c = lax.axis_index("x"), lax.axis_index("y"), lax.axis_index("c")
    chips = [(1 - x, y), (x, 1 - y), (1 - x, 1 - y)]
    return x, y, c, chips


def _relay_chips():
    x, y, c, _ = _place()
    first = c == 0
    near = (jnp.where(first, 1 - x, x), jnp.where(first, y, 1 - y))
    far = (jnp.where(first, x, 1 - x), jnp.where(first, 1 - y, y))
    return near, far, (1 - x, 1 - y)


def _cast_rows(src_ref, dst_ref, bufs):
    fbuf, bbuf, load_sems, store_sems = bufs
    tr = fbuf.shape[1]
    nblk = src_ref.shape[0] // tr

    def load(i, s):
        return pltpu.make_async_copy(src_ref.at[pl.ds(i * tr, tr)], fbuf.at[s], load_sems.at[s])

    def store(i, s):
        return pltpu.make_async_copy(bbuf.at[s], dst_ref.at[pl.ds(i * tr, tr)], store_sems.at[s])

    load(0, 0).start()

    def step(i, carry):
        s = i % 2
        load(i, s).wait()

        @pl.when(i + 1 < nblk)
        def _():
            load(i + 1, 1 - s).start()

        @pl.when(i >= 2)
        def _():
            store(i - 2, s).wait()

        bbuf[s] = fbuf[s].astype(BF16)
        store(i, s).start()
        return carry

    lax.fori_loop(0, nblk, step, 0)
    for i in range(max(nblk - 2, 0), nblk):
        store(i, i % 2).wait()


def _norm_rows(src_ref, xs_ref, h_ref, row0, g, shift, scale, bufs):
    fbuf, bbuf, load_sems, xs_sems, h_sems = bufs
    tm = fbuf.shape[1]
    nblk = src_ref.shape[0] // tm

    def load(i, s):
        return pltpu.make_async_copy(src_ref.at[pl.ds(i * tm, tm)], fbuf.at[s], load_sems.at[s])

    def put_x(i, s):
        return pltpu.make_async_copy(fbuf.at[s], xs_ref.at[pl.ds(row0 + i * tm, tm)], xs_sems.at[s])

    def put_h(i, s):
        return pltpu.make_async_copy(bbuf.at[s], h_ref.at[pl.ds(row0 + i * tm, tm)], h_sems.at[s])

    load(0, 0).start()

    def step(i, carry):
        s = i % 2
        load(i, s).wait()

        @pl.when(i >= 1)
        def _():
            put_x(i - 1, 1 - s).wait()

        @pl.when(i + 1 < nblk)
        def _():
            load(i + 1, 1 - s).start()

        @pl.when(i >= 2)
        def _():
            put_h(i - 2, s).wait()

        x = fbuf[s]
        rstd = lax.rsqrt(jnp.mean(x * x, axis=-1, keepdims=True) + EPS)
        bbuf[s] = ((x * rstd) * g * (1.0 + scale) + shift).astype(BF16)
        put_x(i, s).start()
        put_h(i, s).start()
        return carry

    lax.fori_loop(0, nblk, step, 0)
    put_x(nblk - 1, (nblk - 1) % 2).wait()
    for i in range(max(nblk - 2, 0), nblk):
        put_h(i, i % 2).wait()


def allgather_shards(bufs, after, casts, norm):
    n, m = len(bufs), len(casts)
    cast_rows = [256 if a.shape[0] % 256 == 0 else a.shape[0] for a in casts]
    ctx, x_lat = norm[:2]
    cx, d = ctx.shape
    t = cx + x_lat.shape[0]
    tm = _row_block(cx)

    def body(*refs):
        refs = list(refs)
        take = lambda k: [refs.pop(0) for _ in range(k)]
        take(n + 1)
        cast_src = take(m)
        ctx_ref, x_ref, g_ref, sh_ref, sc_ref = take(5)
        outs = take(n)
        done_ref, = take(1)
        cast_dst = take(m)
        h_ref, xs_ref = take(2)
        send_sems, recv_sems = take(2)
        cast_bufs = take(4 * m)
        norm_bufs = take(5)
        done_ref[...] = jnp.zeros((8, 128), F32)
        x, y, c, _ = _place()
        me = (x, y, c)
        p = 2 * x + y
        near, far, diag = _relay_chips()
        half = [pl.ds(c * (s.shape[1] // 2), s.shape[1] // 2) for s in bufs]
        other = [pl.ds((1 - c) * (s.shape[1] // 2), s.shape[1] // 2) for s in bufs]
        slot = lambda chip: 2 * chip[0] + chip[1]

        def remote(i, k, ref, to):
            return pltpu.make_async_remote_copy(src_ref=ref, dst_ref=ref, send_sem=send_sems.at[6 * i + k],
                                                recv_sem=recv_sems.at[6 * i + k], device_id=to, device_id_type=MESH)

        sends = []

        def send(i, k, ref, to):
            cp = remote(i, k, ref, to)
            cp.start()
            sends.append(cp)

        for i in range(n):
            mine = outs[i].at[p, half[i]]
            send(i, 0, mine, (*near, c))
            send(i, 1, mine, (*far, c))
        for j in range(m):
            _cast_rows(cast_src[j], cast_dst[j].at[p], cast_bufs[4 * j:4 * j + 4])
        gv = g_ref[...]
        _norm_rows(ctx_ref, xs_ref, h_ref, 0, gv, sh_ref[0:1, :], sc_ref[0:1, :], norm_bufs)
        _norm_rows(x_ref, xs_ref, h_ref, cx, gv, sh_ref[1:2, :], sc_ref[1:2, :], norm_bufs)
        for i in range(n):
            landed = outs[i].at[slot(near), half[i]]
            remote(i, 0, landed, me).wait_recv()
            send(i, 2, landed, (*far, c))
            send(i, 3, landed, (x, y, 1 - c))
        for i in range(n):
            landed = outs[i].at[slot(far), half[i]]
            remote(i, 1, landed, me).wait_recv()
            send(i, 4, landed, (x, y, 1 - c))
        for i in range(n):
            landed = outs[i].at[slot(diag), half[i]]
            remote(i, 2, landed, me).wait_recv()
            send(i, 5, landed, (x, y, 1 - c))
        for i in range(n):
            for k, chip in ((3, far), (4, near), (5, diag)):
                remote(i, k, outs[i].at[slot(chip), other[i]], me).wait_recv()
        for cp in sends:
            cp.wait_send()

    return pl.pallas_call(
        body, name="allgather_shards",
        in_specs=[ANY] * (n + 1 + m + 2) + [VMEM] * 3, out_specs=[ANY] * n + [VMEM] + [ANY] * (m + 2),
        out_shape=[jax.ShapeDtypeStruct(s.shape, s.dtype) for s in bufs] + [jax.ShapeDtypeStruct((8, 128), F32)]
        + [jax.ShapeDtypeStruct((4,) + a.shape, BF16) for a in casts]
        + [jax.ShapeDtypeStruct((t, d), BF16), jax.ShapeDtypeStruct((t, d), F32)],
        input_output_aliases={i: i for i in range(n)},
        scratch_shapes=[pltpu.SemaphoreType.DMA((6 * n,)), pltpu.SemaphoreType.DMA((6 * n,))] + [
            s for a, tr in zip(casts, cast_rows) for s in (
                pltpu.VMEM((2, tr, a.shape[1]), F32), pltpu.VMEM((2, tr, a.shape[1]), BF16),
                pltpu.SemaphoreType.DMA((2,)), pltpu.SemaphoreType.DMA((2,)))] + [
            pltpu.VMEM((2, tm, d), F32), pltpu.VMEM((2, tm, d), BF16), pltpu.SemaphoreType.DMA((2,)),
            pltpu.SemaphoreType.DMA((2,)), pltpu.SemaphoreType.DMA((2,))],
        compiler_params=_cp(has_side_effects=True),
    )(*bufs, after, *casts, *norm)


def pair_sum(grad, got, chip_core):
    _, r, cc = grad.shape
    hr = r // 2
    tr = 256 if hr % 256 == 0 else hr
    nb = hr // tr

    def body(cc_ref, a_ref, b_ref, own_ref, sb_ref):
        s = a_ref[...].astype(F32) + b_ref[...].astype(F32)
        sb_ref[...] = s.astype(BF16)

        @pl.when(pl.program_id(1) == cc_ref[0])
        def _():
            own_ref[...] = s

    grid_spec = pltpu.PrefetchScalarGridSpec(
        num_scalar_prefetch=1, grid=(nb, 4),
        in_specs=[pl.BlockSpec((None, tr, cc), lambda i, qi, cc_ref: (qi, cc_ref[1] * nb + i, 0)),
                  pl.BlockSpec((None, tr, cc), lambda i, qi, cc_ref: (qi, i, 0))],
        out_specs=[pl.BlockSpec((tr, cc), lambda i, qi, cc_ref: (i, 0)),
                   pl.BlockSpec((None, tr, cc), lambda i, qi, cc_ref: (qi, i, 0))])
    return pl.pallas_call(
        body, name="pair_sum", grid_spec=grid_spec,
        out_shape=[jax.ShapeDtypeStruct((hr, cc), F32), jax.ShapeDtypeStruct((4, hr, cc), BF16)],
        compiler_params=_cp(),
    )(chip_core, grad, got)


def owner_sum(own, got, chip_core):
    hr, cc = own.shape
    tr = 256 if hr % 256 == 0 else hr
    nb = hr // tr

    def body(cc_ref, a_ref, b_ref, o_ref):
        s = a_ref[...] + b_ref[0].astype(F32)
        s = s + b_ref[1].astype(F32)
        o_ref[...] = s + b_ref[2].astype(F32)

    grid_spec = pltpu.PrefetchScalarGridSpec(
        num_scalar_prefetch=1, grid=(nb,),
        in_specs=[pl.BlockSpec((tr, cc), lambda i, cc_ref: (i, 0)),
                  pl.BlockSpec((3, tr, cc), lambda i, cc_ref: (0, i, 0))],
        out_specs=pl.BlockSpec((tr, cc), lambda i, cc_ref: (cc_ref[1] * nb + i, 0)))
    return pl.pallas_call(
        body, name="owner_sum", grid_spec=grid_spec,
        out_shape=jax.ShapeDtypeStruct((2 * hr, cc), F32), compiler_params=_cp(),
    )(chip_core, own, got)


def share_halves(bufs):
    n = len(bufs)

    def body(*refs):
        outs = refs[n:2 * n]
        send_sems, recv_sems = refs[2 * n:]
        x, y, c, _ = _place()
        copies = []
        for i in range(n):
            hr = bufs[i].shape[0] // 2
            mine = outs[i].at[pl.ds(c * hr, hr)]
            cp = pltpu.make_async_remote_copy(
                src_ref=mine, dst_ref=mine, send_sem=send_sems.at[i], recv_sem=recv_sems.at[i],
                device_id=(x, y, 1 - c), device_id_type=MESH)
            cp.start()
            copies.append((cp, outs[i].at[pl.ds((1 - c) * hr, hr)]))
        for i, (cp, theirs) in enumerate(copies):
            cp.wait_send()
            pltpu.make_async_remote_copy(
                src_ref=theirs, dst_ref=theirs, send_sem=send_sems.at[i], recv_sem=recv_sems.at[i],
                device_id=(x, y, c), device_id_type=MESH).wait_recv()

    return pl.pallas_call(
        body, name="share_halves",
        in_specs=[ANY] * n, out_specs=[ANY] * n,
        out_shape=[jax.ShapeDtypeStruct(b.shape, b.dtype) for b in bufs],
        input_output_aliases={i: i for i in range(n)},
        scratch_shapes=[pltpu.SemaphoreType.DMA((n,)), pltpu.SemaphoreType.DMA((n,))],
        compiler_params=pltpu.CompilerParams(has_side_effects=True),
    )(*bufs)


def allgather8(v, name, per_peer=False):
    r, n = v.shape[-2:]

    def body(v_ref, out_ref, send_sems, recv_sems):
        x, y, c, _ = _place()
        me = 4 * x + 2 * y + c
        out_ref[me] = v_ref[me] if per_peer else v_ref[...]

        def copy(k, peer_index, slot, to):
            return pltpu.make_async_remote_copy(
                src_ref=v_ref.at[peer_index] if per_peer else v_ref, dst_ref=out_ref.at[slot],
                send_sem=send_sems.at[k - 1], recv_sem=recv_sems.at[k - 1], device_id=to, device_id_type=MESH)

        peers = []
        for k in range(1, 8):
            px = 1 - x if (k >> 2) & 1 else x
            py = 1 - y if (k >> 1) & 1 else y
            pc = 1 - c if k & 1 else c
            peers.append((px, py, pc))
            copy(k, 4 * px + 2 * py + pc, me, (px, py, pc)).start()
        for k, (px, py, pc) in enumerate(peers, start=1):
            copy(k, me, 4 * px + 2 * py + pc, (x, y, c)).wait_recv()
        for k, (px, py, pc) in enumerate(peers, start=1):
            copy(k, 4 * px + 2 * py + pc, me, (px, py, pc)).wait_send()

    return pl.pallas_call(
        body, name=name, in_specs=[VMEM], out_specs=VMEM,
        out_shape=jax.ShapeDtypeStruct((8, r, n), v.dtype),
        scratch_shapes=[pltpu.SemaphoreType.DMA((7,)), pltpu.SemaphoreType.DMA((7,))],
        compiler_params=_cp(has_side_effects=True),
    )(v)


HBM = pl.BlockSpec(memory_space=pltpu.HBM)
SEM = pl.BlockSpec(memory_space=pltpu.SEMAPHORE)
DATAFLOW = pltpu.SideEffectType.DATAFLOW_SIDE_EFFECTING


def _descriptors(plan, refs, send_sems, recv_sems, arrivals=True):
    x, y, c, _ = _place()
    sends, recvs = plan(refs)
    out = [pltpu.make_async_remote_copy(src_ref=src, dst_ref=dst, send_sem=send_sems.at[k],
                                        recv_sem=recv_sems.at[k], device_id=to, device_id_type=MESH)
           for k, (src, dst, to) in enumerate(sends)]
    if not arrivals:
        return out, []
    inn = [pltpu.make_async_remote_copy(src_ref=land, dst_ref=land, send_sem=send_sems.at[k],
                                        recv_sem=recv_sems.at[k], device_id=(x, y, c), device_id_type=MESH)
           for k, land in enumerate(recvs)]
    return out, inn


def copies_start(name, arrays, n_copies, plan, after, carried=()):
    na, nall = len(arrays), len(arrays) + len(carried)
    everything = list(arrays) + list(carried)

    def body(*refs):
        out, _ = _descriptors(plan, refs[:na], refs[nall + 1], refs[nall + 2], arrivals=False)
        for cp in out:
            cp.start()
        refs[-1][...] = jnp.zeros((8, 128), F32)

    res = pl.pallas_call(
        body, name=name,
        out_shape=(pltpu.SemaphoreType.DMA((n_copies,)), pltpu.SemaphoreType.DMA((n_copies,)),
                   *[pltpu.HBM(a.shape, a.dtype) for a in everything], jax.ShapeDtypeStruct((8, 128), F32)),
        in_specs=[HBM] * nall + [ANY], out_specs=(SEM, SEM, *[HBM] * nall, VMEM),
        input_output_aliases={i: i + 2 for i in range(nall)},
        compiler_params=pltpu.CompilerParams(has_side_effects=DATAFLOW),
    )(*[pltpu.with_memory_space_constraint(a, pltpu.HBM) for a in everything], after)
    return res[0], res[1], list(res[2:2 + na]), res[-1], list(res[2 + na:2 + nall])


def copies_wait(name, started, plan, after):
    send_sems, recv_sems, arrays = started[:3]
    na = len(arrays)
    after = list(after) if isinstance(after, (list, tuple)) else [after]

    def body(*refs):
        out, inn = _descriptors(plan, refs[:na], refs[na], refs[na + 1])
        for cp in out:
            cp.wait_send()
        for cp in inn:
            cp.wait_recv()
        refs[-1][...] = jnp.zeros((8, 128), F32)

    res = pl.pallas_call(
        body, name=name,
        out_shape=(*[pltpu.HBM(a.shape, a.dtype) for a in arrays], jax.ShapeDtypeStruct((8, 128), F32)),
        in_specs=[HBM] * na + [SEM, SEM] + [ANY] * len(after), out_specs=(*[HBM] * na, VMEM),
        input_output_aliases={i: i for i in range(na)},
        compiler_params=pltpu.CompilerParams(has_side_effects=DATAFLOW),
    )(*arrays, send_sems, recv_sems, *after)
    return list(res[:na]), res[-1]


def _rows_half(r, c):
    return pl.ds(c * (r // 2), r // 2), pl.ds((1 - c) * (r // 2), r // 2)


def plan_gather_neighbours(refs):
    x, y, c, _ = _place()
    p = 2 * x + y
    near, far, _ = _relay_chips()
    sends, recvs = [], []
    for buf in refs:
        mine, _ = _rows_half(buf.shape[1], c)
        for chip in (near, far):
            sends.append((buf.at[p, mine], buf.at[p, mine], (*chip, c)))
            recvs.append(buf.at[2 * chip[0] + chip[1], mine])
    return sends, recvs


def plan_gather_relay(refs):
    x, y, c, _ = _place()
    near, far, diag = _relay_chips()
    slot = lambda chip: 2 * chip[0] + chip[1]
    sends, recvs = [], []
    for buf in refs:
        mine, theirs = _rows_half(buf.shape[1], c)
        landed = buf.at[slot(near), mine]
        sends.append((landed, landed, (*far, c)))
        recvs.append(buf.at[slot(diag), mine])
        for sent, got in ((near, far), (far, near)):
            sends.append((buf.at[slot(sent), mine], buf.at[slot(sent), mine], (x, y, 1 - c)))
            recvs.append(buf.at[slot(got), theirs])
    return sends, recvs


def plan_gather_d2d(refs):
    x, y, c, _ = _place()
    _, _, diag = _relay_chips()
    sends, recvs = [], []
    for buf in refs:
        mine, theirs = _rows_half(buf.shape[1], c)
        landed = buf.at[2 * diag[0] + diag[1], mine]
        sends.append((landed, landed, (x, y, 1 - c)))
        recvs.append(buf.at[2 * diag[0] + diag[1], theirs])
    return sends, recvs


def plan_exchange(refs):
    x, y, c, _ = _place()
    n = len(refs) // 2
    sends, recvs = [], []
    for grad, land in zip(refs[:n], refs[n:]):
        _, theirs = _rows_half(grad.shape[1], c)
        sends.append((grad.at[:, theirs], land, (x, y, 1 - c)))
        recvs.append(land)
    return sends, recvs


def plan_scatter(refs):
    x, y, c, chips = _place()
    n = len(refs) // 2
    sends, recvs = [], []
    for part, land in zip(refs[:n], refs[n:]):
        for j, chip in enumerate(chips):
            sends.append((part.at[2 * chip[0] + chip[1]], land.at[j], (*chip, c)))
            recvs.append(land.at[j])
    return sends, recvs


def plan_share(refs):
    x, y, c, _ = _place()
    sends, recvs = [], []
    for buf in refs:
        mine, theirs = _rows_half(buf.shape[0], c)
        sends.append((buf.at[mine], buf.at[mine], (x, y, 1 - c)))
        recvs.append(buf.at[theirs])
    return sends, recvs


def put_in_slot(w, chip, dtype, name):
    r, c = w.shape
    tr = 256 if r % 256 == 0 else r

    def body(chip_ref, w_ref, o_ref):
        o_ref[...] = w_ref[...].astype(dtype)

    grid_spec = pltpu.PrefetchScalarGridSpec(
        num_scalar_prefetch=1, grid=(r // tr,),
        in_specs=[pl.BlockSpec((tr, c), lambda i, chip_ref: (i, 0))],
        out_specs=pl.BlockSpec((None, tr, c), lambda i, chip_ref: (chip_ref[0], i, 0)))
    return pl.pallas_call(body, name=name, grid_spec=grid_spec,
                          out_shape=jax.ShapeDtypeStruct((4, r, c), dtype), compiler_params=_cp())(chip, w)


def ada_fwd(s_in, ada_w, ada_b, tn):
    nl, d, ws = ada_w.shape

    def body(s_ref, w_ref, b_ref, so_ref, mod_ref):
        s = _silu(s_ref[...])
        so_ref[...] = s
        mod_ref[...] = _dot(s.astype(BF16), w_ref[...].astype(BF16)) + b_ref[...]

    return pl.pallas_call(
        body, name="ada_fwd", grid=(nl, ws // tn),
        in_specs=[pl.BlockSpec((16, d), lambda l, j: (0, 0)),
                  pl.BlockSpec((None, d, tn), lambda l, j: (l, 0, j)),
                  pl.BlockSpec((None, 1, tn), lambda l, j: (l, 0, j))],
        out_specs=[pl.BlockSpec((16, d), lambda l, j: (0, 0)),
                   pl.BlockSpec((None, 16, tn), lambda l, j: (l, 0, j))],
        out_shape=[jax.ShapeDtypeStruct((16, d), F32), jax.ShapeDtypeStruct((nl, 16, ws), F32)],
        compiler_params=_cp(),
    )(s_in, ada_w, ada_b)


def _adamw_math(w, g, m, v):
    m = ADAM_B1 * m + (1.0 - ADAM_B1) * g
    v = ADAM_B2 * v + (1.0 - ADAM_B2) * (g * g)
    m_hat = m / (1.0 - ADAM_B1 ** ADAM_STEP)
    v_hat = v / (1.0 - ADAM_B2 ** ADAM_STEP)
    delta = -ADAM_LR * (m_hat / (jnp.sqrt(v_hat) + ADAM_EPS) + ADAM_WD * w)
    return delta, m, v


def ada_bwd_adamw(s, dm, w, m, v):
    nl, d, ws = w.shape
    tr = 256 if d % 256 == 0 else 128

    def body(s_ref, dm_ref, w_ref, m_ref, v_ref, g_ref, dl_ref, mo_ref, vo_ref, dc_ref):
        dmv = dm_ref[...].astype(BF16)
        wv = w_ref[...]
        g = _dot(s_ref[...].astype(BF16), dmv, TN)
        g_ref[...] = g
        dl_ref[...], mo_ref[...], vo_ref[...] = _adamw_math(wv, g, m_ref[...], v_ref[...])
        dc_ref[...] = _dot(dmv[8:16, :], wv.astype(BF16), NT)

    wblk = pl.BlockSpec((None, tr, ws), lambda l, i: (l, i, 0))
    wshape = jax.ShapeDtypeStruct((nl, d, ws), F32)
    return pl.pallas_call(
        body, name="ada_bwd_adamw", grid=(nl, d // tr),
        in_specs=[pl.BlockSpec((16, tr), lambda l, i: (0, i)),
                  pl.BlockSpec((None, 16, ws), lambda l, i: (l, 0, 0)), wblk, wblk, wblk],
        out_specs=[wblk, wblk, wblk, wblk, pl.BlockSpec((None, 8, tr), lambda l, i: (l, 0, i))],
        out_shape=[wshape, wshape, wshape, wshape, jax.ShapeDtypeStruct((nl, 8, d), F32)],
        compiler_params=_cp(),
    )(s, dm, w, m, v)


def adamw(w, g, m, v, name, with_grad=False):
    r, c = w.shape
    tr = 256 if r % 256 == 0 else r

    def body(w_ref, g_ref, m_ref, v_ref, dl_ref, mo_ref, vo_ref, *g_out):
        gv = g_ref[...]
        dl_ref[...], mo_ref[...], vo_ref[...] = _adamw_math(w_ref[...], gv, m_ref[...], v_ref[...])
        if with_grad:
            g_out[0][...] = gv

    blk = pl.BlockSpec((tr, c), lambda i: (i, 0))
    shape = jax.ShapeDtypeStruct((r, c), F32)
    n_out = 4 if with_grad else 3
    return pl.pallas_call(body, name=name, grid=(r // tr,), in_specs=[blk] * 4, out_specs=[blk] * n_out,
                          out_shape=[shape] * n_out, compiler_params=_cp())(w, g, m, v)


ROW_MOD = 10


def small_reduce(gathered):
    _, rows, d = gathered.shape

    def body(g_ref, o_ref):
        tot = g_ref[0]
        for b in range(1, 8):
            tot = tot + g_ref[b]
        o_ref[0:rows, :] = tot
        for layer in range(2):
            lat = ROW_MOD + 6 * layer
            o_ref[24 + 3 * layer:27 + 3 * layer, :] = tot[lat:lat + 3, :] + tot[lat + 3:lat + 6, :]
        o_ref[30:32, :] = jnp.zeros((2, d), F32)

    return pl.pallas_call(body, name="small_reduce", in_specs=[VMEM], out_specs=VMEM,
                          out_shape=jax.ShapeDtypeStruct((32, d), F32), compiler_params=_cp())(gathered)


def lb_logits_grad(lbl, dlb):
    _, _, n = lbl.shape

    def body(l_ref, d_ref, o_ref):
        for dr in range(2):
            _, (p0, p1, p2) = _lower_bound(l_ref, dr)
            dv = d_ref[dr:dr + 1, :]
            o_ref[dr, 0:1, :] = p0 * p2 * dv
            o_ref[dr, 1:2, :] = p1 * p2 * dv
            o_ref[dr, 2:3, :] = -p2 * (p0 + p1) * dv

    return pl.pallas_call(body, name="lb_logits_grad", in_specs=[VMEM, VMEM], out_specs=VMEM,
                          out_shape=jax.ShapeDtypeStruct((2, 3, n), F32), compiler_params=_cp())(lbl, dlb)


def c_ctx_grad(parts, c_ctx):
    d = c_ctx.shape[1]

    def body(p_ref, c_ref, o_ref):
        tot = p_ref[0, 0:1, :]
        for chip in range(1, 4):
            tot = tot + p_ref[2 * chip, 0:1, :]
        o_ref[...] = tot * _dsilu(c_ref[...])

    return pl.pallas_call(body, name="c_ctx_grad", in_specs=[VMEM, VMEM], out_specs=VMEM,
                          out_shape=jax.ShapeDtypeStruct((1, d), F32), compiler_params=_cp())(parts, c_ctx)


def kernel(x, c, ctx, c_ctx, ada_w, ada_b, pre_g, post_g, ev_w_in, ev_pool_w, ev_pool_scale, ev_conv_w, ev_conv_b, ev_w_out, od_w_in, od_onorm_g, od_w_out, lb_logits, loss_target, m_c_ctx, m_ada_w, m_ada_b, m_pre_g, m_post_g, m_ev_w_in, m_ev_pool_w, m_ev_pool_scale, m_ev_conv_w, m_ev_conv_b, m_ev_w_out, m_od_w_in, m_od_onorm_g, m_od_w_out, m_lb_logits, v_c_ctx, v_ada_w, v_ada_b, v_pre_g, v_post_g, v_ev_w_in, v_ev_pool_w, v_ev_pool_scale, v_ev_conv_w, v_ev_conv_b, v_ev_w_out, v_od_w_in, v_od_onorm_g, v_od_w_out, v_lb_logits):
    _, seq, d = x.shape
    cx = ctx.shape[1]
    t = cx + seq
    half_d = d // 2
    g = half_d // N_POOL
    tn = d // 4
    xi, yi, ci = lax.axis_index("x"), lax.axis_index("y"), lax.axis_index("c")
    chip = 2 * xi + yi
    chip_arr = jnp.reshape(chip, (1,)).astype(jnp.int32)
    chip_core_arr = jnp.stack([chip, ci]).astype(jnp.int32)

    c_rows = jnp.concatenate([c, jnp.zeros((7, d), F32)], axis=0)
    c_all = allgather8(c_rows, "allgather_c")[:, 0, :]
    s_in = jnp.concatenate([c_all, c_ctx.reshape(1, d), jnp.zeros((7, d), F32)], axis=0)
    ws_ada = ada_w.shape[2]
    ada_b_mine = lax.dynamic_slice(ada_b, (0, chip * ws_ada), (2, ws_ada)).reshape(2, 1, ws_ada)
    s_act, mod_mine = ada_fwd(s_in, ada_w, ada_b_mine, tn)
    mod_rows = jnp.concatenate([
        mod_mine[:, :8].transpose(1, 0, 2), jnp.broadcast_to(mod_mine[:, 8][None], (8, 2, ws_ada)),
        jnp.zeros((8, 4, ws_ada), F32)], axis=1)
    mod_all = allgather8(mod_rows, "exchange_mod", per_peer=True)

    pad = lambda a, rows: jnp.concatenate([a, jnp.zeros((rows - a.shape[0], g), F32)], axis=0)
    small = jnp.concatenate([
        ev_pool_w.reshape(g, g), pad(ev_conv_w.reshape(3, g), 8), pad(od_onorm_g.reshape(2, g), 8),
        pad(lb_logits.reshape(12, g), 16)], axis=0)
    by_chip = mod_all[0::2]
    mods = jnp.stack([by_chip[:, 2:4], by_chip[:, 0:2]]).transpose(2, 0, 1, 3).reshape(2, 2, 3 * d)
    shift, scale, gate = mods[:, :, :d], mods[:, :, d:2 * d], mods[:, :, 2 * d:]
    ev_in_g, ev_out_g, small_g, ev_done, od_in_mine, od_out_mine, h0, xs = allgather_shards([
        put_in_slot(ev_w_in[0], chip_arr, BF16, "cast_ev_w_in"),
        put_in_slot(ev_w_out[0], chip_arr, BF16, "cast_ev_w_out"),
        put_in_slot(small, chip_arr, F32, "place_small")], mod_all, [od_w_in[0], od_w_out[0]],
        (ctx[0], x[0], pre_g[0:1], shift[0], scale[0]))
    od_ici = copies_start("gather_od_ici_start", [od_in_mine, od_out_mine], 4, plan_gather_neighbours, ev_done,
                          carried=[h0])
    h0 = od_ici[4][0]
    ev_out3 = ev_out_g.reshape(1, d, d)
    pool_w_full = small_g[:, :g].reshape(4, N_POOL, g // 4, g).transpose(1, 0, 2, 3).reshape(N_POOL, g, g)
    conv_w_full = small_g[:, g:g + 3].transpose(1, 0, 2).reshape(3, half_d)
    onorm_full = small_g[:, g + 8:g + 10].reshape(1, d)
    lbl_full = small_g[:, g + 16:g + 28].reshape(4, 2, 3, 2 * g).transpose(1, 2, 0, 3).reshape(2, 3, d)

    z0 = mm_nn(h0, ev_in_g, half_d, tn, "mm_ev_in")
    u = mix_b_fwd(z0, conv_w_full, ev_conv_b, mix_a_fwd(z0, pool_w_full, ev_pool_scale, cx), cx)
    od_relay = copies_start("gather_od_relay_start",
                            copies_wait("gather_od_ici_wait", od_ici, plan_gather_neighbours, u)[0],
                            6, plan_gather_relay, u)
    y0 = mm_nn(u, ev_out3, d, tn, "mm_ev_out")[0]
    xs1, h1 = post_fwd_norm(xs, y0, post_g[0:1] + od_relay[3][0:1, 0:1], gate[0],
                            pre_g[1:2], shift[1], scale[1], cx)
    od_d2d = copies_start("gather_od_d2d_start",
                          copies_wait("gather_od_relay_wait", od_relay, plan_gather_relay, xs1)[0],
                          2, plan_gather_d2d, xs1)
    (od_in_g, od_out_g), _ = copies_wait("gather_od_d2d_wait", od_d2d, plan_gather_d2d, od_d2d[3])
    od_out3 = od_out_g.reshape(1, d, d)

    z1 = mm_nn(h1, od_in_g, d, tn, "mm_od_in")
    o1, r1, bcs1, ks1, decs1 = hgrn_fwd(z1, lbl_full, onorm_full, cx)
    y1 = mm_nn(r1, od_out3, d, tn, "mm_od_out")[0]
    sq, dx2, dy1, dgate1, dpost1 = post_loss(xs1, y1, post_g[1:2], gate[1], loss_target[0], cx)

    dr1 = mm_nt(dy1[None], None, od_out3, tn, "mm_od_out_dx")
    g_od_out = mm_tn(r1, dy1[None], None, d, tn, "mm_od_out_dw")
    dz1, donorm, dlb = hgrn_bwd(z1, lbl_full, onorm_full, o1, dr1, bcs1, ks1, decs1, cx)
    dh1 = mm_nt(dz1, None, od_in_g, tn, "mm_od_in_dx")
    g_od_in = mm_tn(h1, dz1, None, od_in_g.shape[2], tn, "mm_od_in_dw")
    od_grads = [g_od_in, g_od_out.reshape(4, d // 4, d)]
    half_zone = lambda a, lead, dt: lax.empty((lead, a.shape[1] // 2, a.shape[2]), dt)
    od_ex = copies_start("reduce_od_exchange_start", od_grads + [half_zone(a, 4, a.dtype) for a in od_grads],
                         2, plan_exchange, dh1)

    dxs1, dpre1, dshift1, dscale1, dy0, dgate0, dpost0 = normmod_bwd(
        xs1, dh1, pre_g[1:2] + od_ex[3][0:1, 0:1], scale[1], dx2, cx, True,
        prev=(y0, post_g[0:1], gate[0]))
    du = mm_nt(dy0[None], None, ev_out3, tn, "mm_ev_out_dx")
    g_ev_out = mm_tn(u, dy0[None], None, d, tn, "mm_ev_out_dw")
    od_got, _ = copies_wait("reduce_od_exchange_wait", od_ex, plan_exchange, g_ev_out)
    od_sums = [pair_sum(od_got[i], od_got[2 + i], chip_core_arr) for i in range(2)]
    od_sc = copies_start("reduce_od_scatter_start",
                         [sb for _, sb in od_sums] + [half_zone(a, 3, BF16) for a in od_grads],
                         6, plan_scatter, du)
    dz0a, g_pool_w, dpool_scale = mix_a_bwd(z0, du, pool_w_full, ev_pool_scale + od_sc[3][0:1, 0:1], cx)
    dz0b, dconv_w, dconv_b = mix_b_bwd(z0, du, conv_w_full, ev_conv_b + od_sc[3][0:1, 0:1], cx)
    g_ev_in = mm_tn(h0, dz0a, dz0b, ev_in_g.shape[2], tn, "mm_ev_in_dw")
    ev_grads = [g_ev_in, g_ev_out.reshape(4, d // 4, d), g_pool_w.reshape(4, g, g)]
    ev_ex = copies_start("reduce_ev_exchange_start", ev_grads + [half_zone(a, 4, a.dtype) for a in ev_grads],
                         3, plan_exchange, dpool_scale)
    dh0 = mm_nt(dz0a, dz0b, ev_in_g, tn, "mm_ev_in_dx")
    dxs0, dpre0, dshift0, dscale0 = normmod_bwd(xs, dh0, pre_g[0:1] + ev_ex[3][0:1, 0:1], scale[0], dxs1,
                                                cx, False, True)
    grad_x = dxs0[None]
    ev_got, _ = copies_wait("reduce_ev_exchange_wait", ev_ex, plan_exchange, dxs0)
    ev_sums = [pair_sum(ev_got[i], ev_got[3 + i], chip_core_arr) for i in range(3)]
    od_recv, _ = copies_wait("reduce_od_scatter_wait", od_sc, plan_scatter, dxs0)

    zrow = jnp.zeros((1, d), F32)
    small_rows = jnp.concatenate([
        dpre0, dpre1, dpost0, dpost1,
        jnp.concatenate([dpool_scale, dconv_b], axis=1),
        jnp.concatenate([dconv_w.reshape(1, 3 * half_d), jnp.zeros((1, half_d), F32)], axis=1).reshape(2, d),
        donorm, dlb,
        dshift0[1:2], dscale0[1:2], dgate0[1:2], dshift0[0:1], dscale0[0:1], dgate0[0:1],
        dshift1[1:2], dscale1[1:2], dgate1[1:2], dshift1[0:1], dscale1[0:1], zrow,
        jnp.concatenate([sq[0:1], jnp.zeros((1, d - 128), F32)], axis=1),
        zrow], axis=0)
    small_all = allgather8(small_rows, "allgather_small")
    ev_sc = copies_start("reduce_ev_scatter_start",
                         [sb for _, sb in ev_sums] + [half_zone(a, 3, BF16) for a in ev_grads],
                         9, plan_scatter, small_all)
    od_sh = copies_start("reduce_od_share_start",
                         [owner_sum(od_sums[i][0], od_recv[2 + i], chip_core_arr) for i in range(2)],
                         2, plan_share, dxs0)
    tot = small_reduce(small_all + ev_sc[3][0:1, 0:1])
    loss = tot[22, 0] * (0.5 / d)

    dm_rows = []
    for layer in range(2):
        lat = ROW_MOD + 6 * layer
        dm_lat = small_all[:, lat:lat + 3].reshape(8, 3 * d)
        dm_ctx = tot[lat + 3:lat + 6].reshape(1, 3 * d)
        dm_rows.append(jnp.concatenate([dm_lat, dm_ctx, jnp.zeros((7, 3 * d), F32)], axis=0))
    dm_full = jnp.stack(dm_rows)
    dm_mine = lax.dynamic_slice(dm_full, (0, 0, chip * ws_ada), (2, 16, ws_ada))

    def step(w, gr, m, v, name, with_grad=False):
        shape = w.shape
        cols = shape[-1]
        two_d = lambda a: a.reshape(-1, cols)
        res = adamw(two_d(w), two_d(gr), two_d(m), two_d(v), "adamw_" + name, with_grad)
        return tuple(a.reshape(shape) for a in res)

    grad_ada_b = tot[24:30].reshape(2, 3 * d)
    grad_pre_g = tot[0:2]
    grad_post_g = tot[2:4]
    grad_ev_pool_scale = tot[4:5, :half_d]
    grad_ev_conv_b = tot[4:5, half_d:]
    conv_w_tot = tot[5:7].reshape(1, 2 * d)[:, :3 * half_d].reshape(3, N_POOL, g)
    grad_ev_conv_w = lax.dynamic_slice(conv_w_tot, (0, chip, 0), (3, 1, g)).reshape(1, 3, g)
    grad_od_onorm_g = lax.dynamic_slice(tot[7:8], (0, chip * 2 * g), (1, 2 * g))
    dlb_mine = lax.dynamic_slice(tot[8:10], (0, chip * 2 * g), (2, 2 * g))
    grad_lb_logits = lb_logits_grad(lb_logits, dlb_mine)
    upd = {
        "ada_b": step(ada_b, grad_ada_b, m_ada_b, v_ada_b, "ada_b"),
        "pre_g": step(pre_g, grad_pre_g, m_pre_g, v_pre_g, "pre_g"),
        "post_g": step(post_g, grad_post_g, m_post_g, v_post_g, "post_g"),
        "ev_pool_scale": step(ev_pool_scale, grad_ev_pool_scale, m_ev_pool_scale, v_ev_pool_scale, "ev_pool_scale"),
        "ev_conv_w": step(ev_conv_w, grad_ev_conv_w, m_ev_conv_w, v_ev_conv_w, "ev_conv_w"),
        "ev_conv_b": step(ev_conv_b, grad_ev_conv_b, m_ev_conv_b, v_ev_conv_b, "ev_conv_b"),
        "od_onorm_g": step(od_onorm_g, grad_od_onorm_g, m_od_onorm_g, v_od_onorm_g, "od_onorm_g"),
        "lb_logits": step(lb_logits, grad_lb_logits, m_lb_logits, v_lb_logits, "lb_logits"),
    }
    grad_ada_w, delta_ada_w, new_m_ada_w, new_v_ada_w, dctx_part = ada_bwd_adamw(
        s_act, dm_mine, ada_w, m_ada_w, v_ada_w)
    upd["ada_w"] = (delta_ada_w, new_m_ada_w, new_v_ada_w)
    (grad_od_w_in, grad_od_w_out), _ = copies_wait("reduce_od_share_wait", od_sh, plan_share, ev_sc[3])
    upd["od_w_in"] = step(od_w_in, grad_od_w_in[None], m_od_w_in, v_od_w_in, "od_w_in", True)
    upd["od_w_out"] = step(od_w_out, grad_od_w_out[None], m_od_w_out, v_od_w_out, "od_w_out", True)
    grad_od_w_in, grad_od_w_out = upd["od_w_in"][3], upd["od_w_out"][3]
    done_behind = [dctx_part] + [upd[k][0] for k in (
        "od_w_in", "od_w_out", "ada_b", "pre_g", "post_g", "ev_pool_scale", "ev_conv_w", "ev_conv_b",
        "od_onorm_g", "lb_logits")]
    ev_recv, ev_landed = copies_wait("reduce_ev_scatter_wait", ev_sc, plan_scatter, done_behind)
    grad_ev_w_in, grad_ev_w_out, grad_pool_w = share_halves(
        [owner_sum(ev_sums[i][0], ev_recv[3 + i], chip_core_arr) for i in range(3)])
    dctx_all = allgather8(dctx_part[0] + dctx_part[1] + ev_landed[0:1, 0:1], "allgather_dctx")
    grad_c_ctx = c_ctx_grad(dctx_all, c_ctx.reshape(1, d)).reshape(d)
    upd["c_ctx"] = step(c_ctx, grad_c_ctx, m_c_ctx, v_c_ctx, "c_ctx")
    upd["ev_w_in"] = step(ev_w_in, grad_ev_w_in[None], m_ev_w_in, v_ev_w_in, "ev_w_in", True)
    upd["ev_pool_w"] = step(ev_pool_w, grad_pool_w.reshape(1, N_POOL, g // 4, g), m_ev_pool_w, v_ev_pool_w,
                            "ev_pool_w", True)
    upd["ev_w_out"] = step(ev_w_out, grad_ev_w_out[None], m_ev_w_out, v_ev_w_out, "ev_w_out", True)
    grad_ev_w_in, grad_ev_pool_w, grad_ev_w_out = upd["ev_w_in"][3], upd["ev_pool_w"][3], upd["ev_w_out"][3]
    names = ["c_ctx", "ada_w", "ada_b", "pre_g", "post_g", "ev_w_in", "ev_pool_w", "ev_pool_scale",
             "ev_conv_w", "ev_conv_b", "ev_w_out", "od_w_in", "od_onorm_g", "od_w_out", "lb_logits"]
    grads = [grad_c_ctx, grad_ada_w, grad_ada_b, grad_pre_g, grad_post_g, grad_ev_w_in, grad_ev_pool_w,
             grad_ev_pool_scale, grad_ev_conv_w, grad_ev_conv_b, grad_ev_w_out, grad_od_w_in,
             grad_od_onorm_g, grad_od_w_out, grad_lb_logits]
    return (loss, grad_x, *grads, *[upd[k][0] for k in names], *[upd[k][1] for k in names],
            *[upd[k][2] for k in names])
```

```python
import jax
import jax.numpy as jnp
from jax import lax
from jax.experimental import pallas as pl
from jax.experimental.pallas import tpu as pltpu

EPS = 1e-6
GRID_W_LOG2 = 6
CHUNK = 64
HEAD = 128
N_POOL = 4
ADAM_LR, ADAM_B1, ADAM_B2, ADAM_EPS, ADAM_WD, ADAM_STEP = 0.001, 0.9, 0.999, 1e-08, 0.01, 10
VMEM_LIMIT = 56 * 1024 * 1024
MESH = pl.DeviceIdType.MESH
F32, BF16 = jnp.float32, jnp.bfloat16
ANY = pl.BlockSpec(memory_space=pl.ANY)
VMEM = pl.BlockSpec(memory_space=pltpu.VMEM)


def _cp(**kw):
    return pltpu.CompilerParams(vmem_limit_bytes=VMEM_LIMIT, **kw)


def _silu(x):
    return x * jax.nn.sigmoid(x)


def _dsilu(x):
    s = jax.nn.sigmoid(x)
    return s * (1.0 + x * (1.0 - s))


def _dot(a, b, dims=((1,), (0,)), precision=None):
    return lax.dot_general(a, b, (dims, ((), ())), preferred_element_type=F32, precision=precision)


NN = ((1,), (0,))
NT = ((1,), (1,))
TN = ((0,), (0,))


def _row_block(cx):
    return 256 if cx % 256 == 0 else 128


def normmod_bwd(xs, dh, g, scale, dres, cx, res_is_latent_only, dx_latent_only=False, prev=None):
    t, d = xs.shape
    tm = _row_block(cx)
    nctx = cx // tm
    n_prev = 0 if prev is None else 3

    def body(x_ref, dh_ref, g_ref, sc_ref, dres_ref, *rest):
        dx_ref, dg_ref, dsh_ref, dsc_ref = rest[n_prev:n_prev + 4]
        i = pl.program_id(0)
        is_ctx = i < nctx

        @pl.when(i == 0)
        def _():
            for ref in rest[n_prev + 1:n_prev + 4] + rest[n_prev + 5:]:
                ref[...] = jnp.zeros_like(ref)

        x = x_ref[...]
        dh = dh_ref[...]
        gv = g_ref[...]
        rstd = lax.rsqrt(jnp.mean(x * x, axis=-1, keepdims=True) + EPS)
        xhat = x * rstd
        sc = jnp.where(is_ctx, sc_ref[0:1, :], sc_ref[1:2, :])
        dsh = jnp.sum(dh, axis=0, keepdims=True)
        dhx = dh * xhat
        dsc = jnp.sum(dhx * gv, axis=0, keepdims=True)
        dg_ref[...] += jnp.sum(dhx * (1.0 + sc), axis=0, keepdims=True)
        zero = jnp.zeros_like(dsh)
        dsh_ref[0:1, :] += jnp.where(is_ctx, dsh, zero)
        dsh_ref[1:2, :] += jnp.where(is_ctx, zero, dsh)
        dsc_ref[0:1, :] += jnp.where(is_ctx, dsc, zero)
        dsc_ref[1:2, :] += jnp.where(is_ctx, zero, dsc)
        dxhat = dh * (gv * (1.0 + sc))
        dx = rstd * (dxhat - xhat * jnp.mean(dxhat * xhat, axis=-1, keepdims=True))
        res = dres_ref[...]
        if res_is_latent_only:
            res = jnp.where(is_ctx, jnp.zeros_like(res), res)
        dxt = dx + res
        dx_ref[...] = dxt
        if prev is not None:
            y_ref, pg_ref, gate_ref = rest[:3]
            dy_ref, dgate_ref, dpg_ref = rest[7:]
            y = y_ref[...]
            pgv = pg_ref[...]
            rstd_y = lax.rsqrt(jnp.mean(y * y, axis=-1, keepdims=True) + EPS)
            yhat = y * rstd_y
            gt = jnp.where(is_ctx, gate_ref[0:1, :], gate_ref[1:2, :])
            dxy = dxt * yhat
            dgt = jnp.sum(dxy * pgv, axis=0, keepdims=True)
            dgate_ref[0:1, :] += jnp.where(is_ctx, dgt, zero)
            dgate_ref[1:2, :] += jnp.where(is_ctx, zero, dgt)
            dpg_ref[...] += jnp.sum(dxy * gt, axis=0, keepdims=True)
            dyhat = dxt * (gt * pgv)
            dy_ref[...] = (rstd_y * (dyhat - yhat * jnp.mean(dyhat * yhat, axis=-1, keepdims=True))).astype(BF16)

    row = pl.BlockSpec((tm, d), lambda i: (i, 0))
    if res_is_latent_only:
        res_spec = pl.BlockSpec((tm, d), lambda i: (jnp.maximum(i - nctx, 0), 0))
    else:
        res_spec = row
    vec = lambda r: pl.BlockSpec((r, d), lambda i: (0, 0))
    dx_spec = pl.BlockSpec((tm, d), lambda i: (jnp.maximum(i - nctx, 0), 0)) if dx_latent_only else row
    in_specs = [row, row, vec(1), vec(2), res_spec]
    out_specs = [dx_spec, vec(1), vec(2), vec(2)]
    out_shape = [jax.ShapeDtypeStruct((t - cx if dx_latent_only else t, d), F32), jax.ShapeDtypeStruct((1, d), F32),
                 jax.ShapeDtypeStruct((2, d), F32), jax.ShapeDtypeStruct((2, d), F32)]
    if prev is not None:
        in_specs += [row, vec(1), vec(2)]
        out_specs += [row, vec(2), vec(1)]
        out_shape += [jax.ShapeDtypeStruct((t, d), BF16), jax.ShapeDtypeStruct((2, d), F32),
                      jax.ShapeDtypeStruct((1, d), F32)]
    return pl.pallas_call(
        body, name="normmod_bwd", grid=(t // tm,), in_specs=in_specs, out_specs=out_specs, out_shape=out_shape,
        compiler_params=_cp(),
    )(xs, dh, g, scale, dres, *(() if prev is None else prev))


def post_fwd_norm(xs, y, pg, gate, g_next, shift_next, scale_next, cx):
    t, d = xs.shape
    tm = _row_block(cx)
    nctx = cx // tm

    def body(x_ref, y_ref, pg_ref, gate_ref, g_ref, sh_ref, sc_ref, o_ref, h_ref):
        is_ctx = pl.program_id(0) < nctx
        pick = lambda ref: jnp.where(is_ctx, ref[0:1, :], ref[1:2, :])
        y = y_ref[...]
        rstd = lax.rsqrt(jnp.mean(y * y, axis=-1, keepdims=True) + EPS)
        x = x_ref[...] + pick(gate_ref) * ((y * rstd) * pg_ref[...])
        o_ref[...] = x
        rstd = lax.rsqrt(jnp.mean(x * x, axis=-1, keepdims=True) + EPS)
        h_ref[...] = ((x * rstd) * g_ref[...] * (1.0 + pick(sc_ref)) + pick(sh_ref)).astype(BF16)

    row = pl.BlockSpec((tm, d), lambda i: (i, 0))
    vec = lambda r: pl.BlockSpec((r, d), lambda i: (0, 0))
    return pl.pallas_call(
        body, name="post_fwd_norm", grid=(t // tm,),
        in_specs=[row, row, vec(1), vec(2), vec(1), vec(2), vec(2)], out_specs=[row, row],
        out_shape=[jax.ShapeDtypeStruct((t, d), F32), jax.ShapeDtypeStruct((t, d), BF16)],
        compiler_params=_cp(),
    )(xs, y, pg, gate, g_next, shift_next, scale_next)


def post_loss(xs, y, pg, gate, target, cx):
    t, d = xs.shape
    n = y.shape[0]
    tm = _row_block(cx)
    nctx = cx // tm

    def body(x_ref, y_ref, pg_ref, gate_ref, tgt_ref, sq_ref, dx_ref, dy_ref, dgate_ref, dpg_ref):
        @pl.when(pl.program_id(0) == 0)
        def _():
            sq_ref[...] = jnp.zeros_like(sq_ref)
            dgate_ref[...] = jnp.zeros_like(dgate_ref)
            dpg_ref[...] = jnp.zeros_like(dpg_ref)

        y = y_ref[...]
        pgv = pg_ref[...]
        gt = gate_ref[1:2, :]
        rstd = lax.rsqrt(jnp.mean(y * y, axis=-1, keepdims=True) + EPS)
        yhat = y * rstd
        err = x_ref[...] + gt * (yhat * pgv) - tgt_ref[...]
        sq_ref[...] += jnp.sum(err * err)
        dx = err * (1.0 / d)
        dx_ref[...] = dx
        dxy = dx * yhat
        dgate_ref[1:2, :] += jnp.sum(dxy * pgv, axis=0, keepdims=True)
        dpg_ref[...] += jnp.sum(dxy * gt, axis=0, keepdims=True)
        dyhat = dx * (gt * pgv)
        dy_ref[...] = (rstd * (dyhat - yhat * jnp.mean(dyhat * yhat, axis=-1, keepdims=True))).astype(BF16)

    row = pl.BlockSpec((tm, d), lambda i: (i, 0))
    xrow = pl.BlockSpec((tm, d), lambda i: (i + nctx, 0))
    vec = lambda r: pl.BlockSpec((r, d), lambda i: (0, 0))
    return pl.pallas_call(
        body, name="post_loss", grid=(n // tm,),
        in_specs=[xrow, row, vec(1), vec(2), row],
        out_specs=[pl.BlockSpec((8, 128), lambda i: (0, 0)), row, row, vec(2), vec(1)],
        out_shape=[jax.ShapeDtypeStruct((8, 128), F32), jax.ShapeDtypeStruct((n, d), F32),
                   jax.ShapeDtypeStruct((n, d), BF16), jax.ShapeDtypeStruct((2, d), F32),
                   jax.ShapeDtypeStruct((1, d), F32)],
        compiler_params=_cp(),
    )(xs, y, pg, gate, target)


def _split_rows(m):
    for cand in (1152, 1024, 768, 512, 384, 256, 128):
        if m % cand == 0 and m // cand >= 2:
            return cand
    return m


def mm_nn(a, w3, sec, tn, name):
    m, k = a.shape
    q, _, ws = w3.shape
    n = q * ws
    tpq, tps = ws // tn, sec // tn
    tm = next(c for c in (768, 512, 256, 128) if m % c == 0)

    def body(a_ref, w_ref, o_ref):
        w = w_ref[...]

        def step(i, carry):
            rows = pl.ds(pl.multiple_of(i * tm, tm), tm)
            o_ref[rows, :] = _dot(a_ref[rows, :], w)
            return carry

        lax.fori_loop(0, m // tm, step, 0)

    return pl.pallas_call(
        body, name=name, grid=(n // tn,),
        in_specs=[pl.BlockSpec((m, k), lambda j: (0, 0)),
                  pl.BlockSpec((None, k, tn), lambda j: (j // tpq, 0, j % tpq))],
        out_specs=pl.BlockSpec((None, m, tn), lambda j: (j // tps, 0, j % tps)),
        out_shape=jax.ShapeDtypeStruct((n // sec, m, sec), F32), compiler_params=_cp(),
    )(a, w3)


def _two_stacks(a3, b3, tn):
    sec = a3.shape[2]
    tps = sec // tn
    n1 = a3.shape[0] * tps
    first = lambda j: (jnp.minimum(j, n1 - 1) // tps, jnp.minimum(j, n1 - 1) % tps)
    second = lambda j: (jnp.maximum(j - n1, 0) // tps, jnp.maximum(j - n1, 0) % tps)
    return n1, first, second


def mm_nt(a3, b3, w3, tn, name):
    if b3 is None:
        b3 = a3
    _, m, sec = a3.shape
    q, k, ws = w3.shape
    n = q * ws
    tpq = ws // tn
    mb = _split_rows(m)
    n1, first, second = _two_stacks(a3, b3, tn)

    def body(a_ref, b_ref, w_ref, o_ref):
        j = pl.program_id(1)

        @pl.when(j == 0)
        def _():
            o_ref[...] = jnp.zeros_like(o_ref)

        @pl.when(j < n1)
        def _():
            o_ref[...] += _dot(a_ref[...], w_ref[...], NT)

        @pl.when(j >= n1)
        def _():
            o_ref[...] += _dot(b_ref[...], w_ref[...], NT)

    return pl.pallas_call(
        body, name=name, grid=(m // mb, n // tn),
        in_specs=[pl.BlockSpec((None, mb, tn), lambda i, j: (first(j)[0], i, first(j)[1])),
                  pl.BlockSpec((None, mb, tn), lambda i, j: (second(j)[0], i, second(j)[1])),
                  pl.BlockSpec((None, k, tn), lambda i, j: (j // tpq, 0, j % tpq))],
        out_specs=pl.BlockSpec((mb, k), lambda i, j: (i, 0)),
        out_shape=jax.ShapeDtypeStruct((m, k), F32), compiler_params=_cp(),
    )(a3, b3, w3)


def mm_tn(a, b3, c3, ws, tn, name):
    m, k = a.shape
    sec = b3.shape[2]
    n = (b3.shape[0] + (0 if c3 is None else c3.shape[0])) * sec
    if c3 is None:
        c3 = b3
    tpq = ws // tn
    kb = 256 if k % 256 == 0 else 128
    n1, first, second = _two_stacks(b3, c3, tn)

    def body(a_ref, b_ref, c_ref, o_ref):
        def product(rhs_ref):
            rhs = rhs_ref[...]
            for i in range(k // kb):
                o_ref[i * kb:(i + 1) * kb, :] = _dot(a_ref[:, i * kb:(i + 1) * kb], rhs, TN).astype(BF16)

        @pl.when(pl.program_id(0) < n1)
        def _():
            product(b_ref)

        @pl.when(pl.program_id(0) >= n1)
        def _():
            product(c_ref)

    return pl.pallas_call(
        body, name=name, grid=(n // tn,),
        in_specs=[pl.BlockSpec((m, k), lambda j: (0, 0)),
                  pl.BlockSpec((None, m, tn), lambda j: (first(j)[0], 0, first(j)[1])),
                  pl.BlockSpec((None, m, tn), lambda j: (second(j)[0], 0, second(j)[1]))],
        out_specs=pl.BlockSpec((None, k, tn), lambda j: (j // tpq, 0, j % tpq)),
        out_shape=jax.ShapeDtypeStruct((n // ws, k, ws), BF16), compiler_params=_cp(),
    )(a, b3, c3)


POOL_REACH = 8 << GRID_W_LOG2


def _token_parts(tok, cx):
    lat = tok - cx
    return tok < cx, lat >> GRID_W_LOG2, lat & ((1 << GRID_W_LOG2) - 1)


def _pool_mask(gi, row0, col0, tm, ncols, cx, transposed):
    half = jnp.left_shift(1, gi)
    r = lax.broadcasted_iota(jnp.int32, (tm, 1), 0) + row0
    c = lax.broadcasted_iota(jnp.int32, (1, ncols), 1) + col0
    out_tok, src_tok = (c, r) if transposed else (r, c)
    o_ctx, o_row, o_col = _token_parts(out_tok, cx)
    s_ctx, s_row, s_col = _token_parts(src_tok, cx)

    def inside(o, s):
        return (s >= o - half) & (s <= o + half - 1)

    ctx_hit = o_ctx & s_ctx & inside(out_tok, src_tok)
    lat_hit = (~o_ctx) & (~s_ctx) & inside(o_row, s_row) & inside(o_col, s_col)
    return jnp.where(ctx_hit | lat_hit, 1.0, 0.0).astype(BF16)


def _pool_inv_count(gi, row0, tm, cx, seq):
    half = jnp.left_shift(1, gi)
    r = lax.broadcasted_iota(jnp.int32, (tm, 1), 0) + row0
    is_ctx, row, col = _token_parts(r, cx)

    def count(pos, size):
        return jnp.minimum(pos + half - 1, size - 1) - jnp.maximum(pos - half, 0) + 1

    cnt = jnp.where(is_ctx, count(r, cx), count(row, seq >> GRID_W_LOG2) * count(col, 1 << GRID_W_LOG2))
    return 1.0 / cnt.astype(F32)


def _lat_band(tm):
    side = POOL_REACH // tm
    return side, 2 * side + 1


def _lat_mask(gi, tm, cx, transposed):
    side, band = _lat_band(tm)
    return _pool_mask(gi, cx + side * tm, cx, tm, band * tm, cx, transposed)


def _store_padded_lat(dst_ref, lat, tm):
    side, _ = _lat_band(tm)
    seq = lat.shape[0]
    zeros = jnp.zeros((side * tm, lat.shape[1]), dst_ref.dtype)
    dst_ref[0:side * tm, :] = zeros
    dst_ref[side * tm + seq:, :] = zeros
    dst_ref[side * tm:side * tm + seq, :] = lat.astype(dst_ref.dtype)


def mix_a_fwd(z0, pool_w, pool_scale, cx):
    _, t, half_d = z0.shape
    g = half_d // N_POOL
    seq = t - cx
    tm = _row_block(cx)
    side, band = _lat_band(tm)

    def body(v_ref, ag_ref, w_ref, sc_ref, u_ref, vlat_ref, mask_ref):
        gi = pl.program_id(0)
        w = w_ref[...].astype(BF16)
        sc = sc_ref[...]
        _store_padded_lat(vlat_ref, v_ref[cx:, :], tm)
        mask_ref[...] = _lat_mask(gi, tm, cx, False)

        def finish(row0, window_sum):
            rows = pl.ds(row0, tm)
            pooled = window_sum * _pool_inv_count(gi, row0, tm, cx, seq) - v_ref[rows, :]
            mixed = _dot(pooled.astype(BF16), w) * sc
            u_ref[rows, :] = (mixed * _silu(ag_ref[rows, :])).astype(BF16)

        vctx = v_ref[0:cx, :].astype(BF16)
        for i in range(cx // tm):
            finish(i * tm, _dot(_pool_mask(gi, i * tm, 0, tm, cx, cx, False), vctx))

        def step(j, carry):
            src = vlat_ref[pl.ds(pl.multiple_of(j * tm, tm), band * tm), :]
            finish(pl.multiple_of(cx + j * tm, tm), _dot(mask_ref[...], src))
            return carry

        lax.fori_loop(0, seq // tm, step, 0)

    sec = lambda s: pl.BlockSpec((None, t, g), lambda j: (s, 0, j))
    return pl.pallas_call(
        body, name="mix_a_fwd", grid=(N_POOL,),
        in_specs=[sec(0), sec(1), pl.BlockSpec((None, g, g), lambda j: (j, 0, 0)),
                  pl.BlockSpec((1, g), lambda j: (0, j))],
        out_specs=pl.BlockSpec((t, g), lambda j: (0, j)),
        out_shape=jax.ShapeDtypeStruct((t, 2 * half_d), BF16),
        scratch_shapes=[pltpu.VMEM((seq + 2 * side * tm, g), BF16), pltpu.VMEM((tm, band * tm), BF16)],
        compiler_params=_cp(),
    )(z0, z0, pool_w, pool_scale)


def mix_a_bwd(z0, du, pool_w, pool_scale, cx):
    _, t, half_d = z0.shape
    g = half_d // N_POOL
    seq = t - cx
    tm = _row_block(cx)
    gq = g // 4
    side, band = _lat_band(tm)

    def body(v_ref, ag_ref, du_ref, w_ref, sc_ref, dz_ref, dw_ref, dsc_ref,
             vlat_ref, mask_ref, pooled_ref, dmx_ref, dpl_ref, wlat_ref, wctx_ref):
        gi = pl.program_id(0)
        w = w_ref[...].astype(BF16)
        sc = sc_ref[...]
        _store_padded_lat(vlat_ref, v_ref[cx:, :], tm)
        _store_padded_lat(wlat_ref, jnp.zeros((seq, g), BF16), tm)
        mask_ref[...] = _lat_mask(gi, tm, cx, False)

        def first(row0, window_sum, weighted_ref, weighted_row0):
            rows = pl.ds(row0, tm)
            inv = _pool_inv_count(gi, row0, tm, cx, seq)
            pooled = (window_sum * inv - v_ref[rows, :]).astype(BF16)
            pooled_ref[rows, :] = pooled
            mixed = _dot(pooled, w)
            ag = ag_ref[rows, :]
            duv = du_ref[rows, :]
            dz_ref[1, rows, :] = (duv * (mixed * sc) * _dsilu(ag)).astype(BF16)
            dms = duv * _silu(ag)
            dmixed = (dms * sc).astype(BF16)
            dmx_ref[rows, :] = dmixed
            dpooled = _dot(dmixed, w, NT)
            dpl_ref[rows, :] = dpooled
            weighted_ref[pl.ds(weighted_row0, tm), :] = (dpooled * inv).astype(BF16)
            return jnp.sum(dms * mixed, axis=0, keepdims=True)

        dsc = jnp.zeros((1, g), F32)
        vctx = v_ref[0:cx, :].astype(BF16)
        for i in range(cx // tm):
            dsc += first(i * tm, _dot(_pool_mask(gi, i * tm, 0, tm, cx, cx, False), vctx), wctx_ref, i * tm)

        def first_lat(j, acc):
            src = vlat_ref[pl.ds(pl.multiple_of(j * tm, tm), band * tm), :]
            return acc + first(pl.multiple_of(cx + j * tm, tm), _dot(mask_ref[...], src),
                               wlat_ref, pl.multiple_of((side + j) * tm, tm))

        dsc_ref[...] = lax.fori_loop(0, seq // tm, first_lat, dsc)
        dw = _dot(pooled_ref[...], dmx_ref[...], TN)
        for qi in range(4):
            dw_ref[qi] = dw[qi * gq:(qi + 1) * gq, :]

        wctx = wctx_ref[...]
        for i in range(cx // tm):
            rows = pl.ds(i * tm, tm)
            dz_ref[0, rows, :] = (_dot(_pool_mask(gi, i * tm, 0, tm, cx, cx, True), wctx)
                                  - dpl_ref[rows, :]).astype(BF16)
        mask_ref[...] = _lat_mask(gi, tm, cx, True)

        def second_lat(j, carry):
            rows = pl.ds(pl.multiple_of(cx + j * tm, tm), tm)
            src = wlat_ref[pl.ds(pl.multiple_of(j * tm, tm), band * tm), :]
            dz_ref[0, rows, :] = (_dot(mask_ref[...], src) - dpl_ref[rows, :]).astype(BF16)
            return carry

        lax.fori_loop(0, seq // tm, second_lat, 0)

    sec = lambda s: pl.BlockSpec((None, t, g), lambda j: (s, 0, j))
    padded = pltpu.VMEM((seq + 2 * side * tm, g), BF16)
    return pl.pallas_call(
        body, name="mix_a_bwd", grid=(N_POOL,),
        in_specs=[sec(0), sec(1), pl.BlockSpec((t, g), lambda j: (0, j)),
                  pl.BlockSpec((None, g, g), lambda j: (j, 0, 0)),
                  pl.BlockSpec((1, g), lambda j: (0, j))],
        out_specs=[pl.BlockSpec((2, t, g), lambda j: (0, 0, j)),
                   pl.BlockSpec((4, None, gq, g), lambda j: (0, j, 0, 0)),
                   pl.BlockSpec((1, g), lambda j: (0, j))],
        out_shape=[jax.ShapeDtypeStruct((2, t, half_d), BF16),
                   jax.ShapeDtypeStruct((4, N_POOL, gq, g), F32),
                   jax.ShapeDtypeStruct((1, half_d), F32)],
        scratch_shapes=[padded, pltpu.VMEM((tm, band * tm), BF16), pltpu.VMEM((t, g), BF16),
                        pltpu.VMEM((t, g), BF16), pltpu.VMEM((t, g), F32), padded, pltpu.VMEM((cx, g), BF16)],
        compiler_params=_cp(),
    )(z0, z0, du, pool_w, pool_scale)


def _conv_masks(t, cx):
    r = lax.broadcasted_iota(jnp.int32, (t, 1), 0)
    has_prev = jnp.where((r == 0) | (r == cx), 0.0, 1.0)
    has_next = jnp.where((r == cx - 1) | (r == t - 1), 0.0, 1.0)
    return has_prev, has_next


def mix_b_fwd(z0, conv_w, conv_b, u, cx):
    _, t, half_d = z0.shape
    gb = 128
    off = half_d // gb

    def body(bx_ref, bb_ref, bc_ref, bg_ref, w_ref, b_ref, _, u_ref):
        has_prev, has_next = _conv_masks(t, cx)
        tt = bc_ref[...] * bx_ref[...]
        prev = pltpu.roll(tt, 1, 0) * has_prev
        nxt = pltpu.roll(tt, t - 1, 0) * has_next
        cv = prev * w_ref[0:1, :] + tt * w_ref[1:2, :] + nxt * w_ref[2:3, :] + b_ref[...]
        u_ref[...] = (bb_ref[...] * cv * _silu(bg_ref[...])).astype(BF16)

    sec = lambda s: pl.BlockSpec((None, t, gb), lambda j: (s, 0, j))
    return pl.pallas_call(
        body, name="mix_b_fwd", grid=(half_d // gb,),
        in_specs=[sec(2), sec(3), sec(4), sec(5), pl.BlockSpec((3, gb), lambda j: (0, j)),
                  pl.BlockSpec((1, gb), lambda j: (0, j)), ANY],
        out_specs=pl.BlockSpec((t, gb), lambda j: (0, j + off)),
        out_shape=jax.ShapeDtypeStruct((t, 2 * half_d), BF16), input_output_aliases={6: 0},
        compiler_params=_cp(),
    )(z0, z0, z0, z0, conv_w, conv_b, u)


def mix_b_bwd(z0, du, conv_w, conv_b, cx):
    _, t, half_d = z0.shape
    gb = 128
    off = half_d // gb

    def body(bx_ref, bb_ref, bc_ref, bg_ref, du_ref, w_ref, b_ref, dz_ref, dw_ref, db_ref):
        has_prev, has_next = _conv_masks(t, cx)
        bx, bb, bc, bg = bx_ref[...], bb_ref[...], bc_ref[...], bg_ref[...]
        duv = du_ref[...]
        tt = bc * bx
        prev = pltpu.roll(tt, 1, 0) * has_prev
        nxt = pltpu.roll(tt, t - 1, 0) * has_next
        w0, w1, w2 = w_ref[0:1, :], w_ref[1:2, :], w_ref[2:3, :]
        cv = prev * w0 + tt * w1 + nxt * w2 + b_ref[...]
        sg = _silu(bg)
        dz_ref[1] = (duv * cv * sg).astype(BF16)
        dz_ref[3] = (duv * bb * cv * _dsilu(bg)).astype(BF16)
        dcv = duv * bb * sg
        dw_ref[0:1, :] = jnp.sum(dcv * prev, axis=0, keepdims=True)
        dw_ref[1:2, :] = jnp.sum(dcv * tt, axis=0, keepdims=True)
        dw_ref[2:3, :] = jnp.sum(dcv * nxt, axis=0, keepdims=True)
        db_ref[...] = jnp.sum(dcv, axis=0, keepdims=True)
        dt = (pltpu.roll(dcv * has_prev, t - 1, 0) * w0 + dcv * w1
              + pltpu.roll(dcv * has_next, 1, 0) * w2)
        dz_ref[0] = (dt * bc).astype(BF16)
        dz_ref[2] = (dt * bx).astype(BF16)

    sec = lambda s: pl.BlockSpec((None, t, gb), lambda j: (s, 0, j))
    return pl.pallas_call(
        body, name="mix_b_bwd", grid=(half_d // gb,),
        in_specs=[sec(2), sec(3), sec(4), sec(5), pl.BlockSpec((t, gb), lambda j: (0, j + off)),
                  pl.BlockSpec((3, gb), lambda j: (0, j)), pl.BlockSpec((1, gb), lambda j: (0, j))],
        out_specs=[pl.BlockSpec((4, t, gb), lambda j: (0, 0, j)),
                   pl.BlockSpec((3, gb), lambda j: (0, j)), pl.BlockSpec((1, gb), lambda j: (0, j))],
        out_shape=[jax.ShapeDtypeStruct((4, t, half_d), BF16),
                   jax.ShapeDtypeStruct((3, half_d), F32), jax.ShapeDtypeStruct((1, half_d), F32)],
        compiler_params=_cp(),
    )(z0, z0, z0, z0, du, conv_w, conv_b)


def _lower_bound(lbl_ref, d):
    l0, l1, l2 = lbl_ref[d, 0:1, :], lbl_ref[d, 1:2, :], lbl_ref[d, 2:3, :]
    mx = jnp.maximum(jnp.maximum(l0, l1), l2)
    e0, e1, e2 = jnp.exp(l0 - mx), jnp.exp(l1 - mx), jnp.exp(l2 - mx)
    inv = 1.0 / (e0 + e1 + e2)
    return (e0 + e1) * inv, (e0 * inv, e1 * inv, e2 * inv)


def _chunk_consts(d):
    r = lax.broadcasted_iota(jnp.int32, (CHUNK, CHUNK), 0)
    c = lax.broadcasted_iota(jnp.int32, (CHUNK, CHUNK), 1)
    keep = (c <= r) if d == 0 else (c >= r)
    return jnp.where(keep, 1.0, 0.0).astype(F32), keep


def _chunk_of_step(s, d, nc, ncc):
    if d == 0:
        return s
    return jnp.where(s < ncc, ncc - 1 - s, nc - 1 + ncc - s)


def _gates(z, lbv):
    e = jnp.exp(-jnp.abs(z))
    r = 1.0 / (1.0 + e)
    er = e * r
    pos = z >= 0.0
    sig = jnp.where(pos, r, er)
    nsig = jnp.where(pos, er, r)
    return sig, nsig, lbv + (1.0 - lbv) * sig


def _split3(x):
    hi = x.astype(BF16)
    r1 = x - hi.astype(F32)
    mid = r1.astype(BF16)
    lo = (r1 - mid.astype(F32)).astype(BF16)
    return jnp.concatenate([hi, mid, lo], axis=1)


def _cumsum_chunk(cum, x):
    y = _dot(cum, _split3(x))
    return y[:, :HEAD] + y[:, HEAD:2 * HEAD] + y[:, 2 * HEAD:]


def _chunk_rows(n):
    return pl.ds(pl.multiple_of(n * CHUNK, CHUNK), CHUNK)


def _group(nc, prefer=(4, 3, 2, 1)):
    return next(u for u in prefer if nc % u == 0)


WIDE_GROUP = (18, 12, 6, 4, 3, 2, 1)


def _decay_pass(lf_ref, bc_ref, dec_ref, cum, nc):
    grp = _group(nc, WIDE_GROUP)

    def step(m, carry):
        ns = [m * grp + u for u in range(grp)]
        lfc = [lf_ref[_chunk_rows(n), :] for n in ns]
        bc = [_cumsum_chunk(cum, x) for x in lfc]
        for u, n in enumerate(ns):
            bc_ref[_chunk_rows(n), :] = bc[u]
            dec_ref[n] = jnp.broadcast_to(jnp.exp(jnp.sum(lfc[u], axis=0, keepdims=True)), (8, HEAD))
        return carry

    lax.fori_loop(0, nc // grp, step, 0)


def hgrn_fwd(z1, lbl, onorm, cx):
    _, t, d = z1.shape
    seq = t - cx
    nc, ncc = t // CHUNK, cx // CHUNK

    grp, sgrp = _group(nc, WIDE_GROUP), _group(nc, WIDE_GROUP)

    def body(zf_ref, zb_ref, v_ref, q_ref, g_ref, lbl_ref, on_ref, o_ref, r_ref, bcs_ref, ks_ref, decs_ref,
             lf_ref, k_ref, bc_ref, dec_ref, qd_ref, ki_ref, oacc_ref, ds_ref):
        for dr, z_ref in ((0, zf_ref), (1, zb_ref)):
            lbv, _ = _lower_bound(lbl_ref, dr)
            _, nsig, f = _gates(z_ref[...], lbv)
            lf_ref[...] = jnp.log(f)
            k_ref[...] = (1.0 - lbv) * nsig
            cum, keep = _chunk_consts(dr)
            _decay_pass(lf_ref, bc_ref, dec_ref, cum.astype(BF16), nc)
            bc = bc_ref[...]
            bcs_ref[dr] = bc
            ks_ref[dr] = k_ref[...]
            decs_ref[dr] = dec_ref[...]
            qd_ref[...] = (q_ref[...] * jnp.exp(bc)).astype(BF16)
            ki_ref[...] = (k_ref[...] * jnp.exp(-bc)).astype(BF16)

            def local_step(m, carry, dr=dr, keep=keep):
                ns = [m * grp + u for u in range(grp)]
                rows = [_chunk_rows(n) for n in ns]
                qd = [qd_ref[r, :] for r in rows]
                ki = [ki_ref[r, :] for r in rows]
                vc = [v_ref[r, :].astype(BF16) for r in rows]
                sc = [_dot(qd[u], ki[u], NT) for u in range(grp)]
                inc = [_dot(vc[u], ki[u], TN) for u in range(grp)]
                a = [jnp.where(keep, s, 0.0).astype(BF16) for s in sc]
                intra = [_dot(a[u], vc[u]) for u in range(grp)]
                for u in range(grp):
                    ds_ref[ns[u]] = inc[u] * dec_ref[ns[u]][0:1, :]
                    if dr == 0:
                        oacc_ref[rows[u], :] = intra[u]
                    else:
                        oacc_ref[rows[u], :] += intra[u]
                return carry

            lax.fori_loop(0, nc // grp, local_step, 0)

            def state_step(m, st, dr=dr):
                ns = [_chunk_of_step(m * sgrp + u, dr, nc, ncc) for u in range(sgrp)]
                rows = [_chunk_rows(n) for n in ns]
                sts = []
                for n in ns:
                    sts.append(st.astype(BF16))
                    st = st * dec_ref[n][0:1, :] + ds_ref[n]
                inter = [_dot(qd_ref[rows[u], :], sts[u], NT) for u in range(sgrp)]
                for u in range(sgrp):
                    oacc_ref[rows[u], :] += inter[u]
                return st

            lax.fori_loop(0, nc // sgrp, state_step, jnp.zeros((HEAD, HEAD), F32))

        o = oacc_ref[cx:, :]
        o_ref[...] = o
        rstd = lax.rsqrt(jnp.mean(o * o, axis=-1, keepdims=True) + EPS)
        r_ref[...] = (o * rstd * on_ref[...] * _silu(g_ref[cx:, :])).astype(BF16)

    sec = lambda s: pl.BlockSpec((None, t, HEAD), lambda h: (s, 0, h))
    col = pl.BlockSpec((seq, HEAD), lambda h: (0, h))
    tf32, tb16 = pltpu.VMEM((t, HEAD), F32), pltpu.VMEM((t, HEAD), BF16)
    return pl.pallas_call(
        body, name="hgrn_fwd", grid=(d // HEAD,),
        in_specs=[sec(0), sec(1), sec(2), sec(3), sec(4),
                  pl.BlockSpec((2, 3, HEAD), lambda h: (0, 0, h)), pl.BlockSpec((1, HEAD), lambda h: (0, h))],
        out_specs=[col, col, pl.BlockSpec((2, t, HEAD), lambda h: (0, 0, h)),
                   pl.BlockSpec((2, t, HEAD), lambda h: (0, 0, h)),
                   pl.BlockSpec((2, nc, 8, HEAD), lambda h: (0, 0, 0, h))],
        out_shape=[jax.ShapeDtypeStruct((seq, d), F32), jax.ShapeDtypeStruct((seq, d), BF16),
                   jax.ShapeDtypeStruct((2, t, d), F32), jax.ShapeDtypeStruct((2, t, d), F32),
                   jax.ShapeDtypeStruct((2, nc, 8, d), F32)],
        scratch_shapes=[tf32, tf32, tf32, pltpu.VMEM((nc, 8, HEAD), F32), tb16, tb16, tf32,
                        pltpu.VMEM((nc, HEAD, HEAD), F32)],
        compiler_params=_cp(),
    )(z1, z1, z1, z1, z1, lbl, onorm)


def hgrn_bwd(z1, lbl, onorm, o, dr_out, bcs, ks, decs, cx):
    _, t, d = z1.shape
    seq = t - cx
    nc, ncc = t // CHUNK, cx // CHUNK

    grp2, grp = _group(nc, (12, 9, 6, 4, 3, 2, 1)), _group(nc, (18, 12, 9, 6, 4, 3, 2, 1))

    def body(zf_ref, zb_ref, v_ref, q_ref, g_ref, lbl_ref, on_ref, o_ref, dr_ref, bcs_ref, ks_ref, decs_ref,
             dz_ref, don_ref, dlb_ref,
             qd_ref, ki_ref, do_ref, dqd_ref, dki_ref, dq_ref, dv_ref, ds_ref, dsl_ref):
        o = o_ref[...]
        g = g_ref[cx:, :]
        drv = dr_ref[...]
        onv = on_ref[...]
        rstd = lax.rsqrt(jnp.mean(o * o, axis=-1, keepdims=True) + EPS)
        ohat = o * rstd
        sg = _silu(g)
        don_ref[...] = jnp.sum(drv * ohat * sg, axis=0, keepdims=True)
        dz_ref[4, :cx, :] = jnp.zeros((cx, HEAD), BF16)
        dz_ref[4, cx:, :] = (drv * ohat * onv * _dsilu(g)).astype(BF16)
        dohat = drv * onv * sg
        do_ref[:cx, :] = jnp.zeros((cx, HEAD), BF16)
        do_ref[cx:, :] = (rstd * (dohat - ohat * jnp.mean(dohat * ohat, axis=-1, keepdims=True))).astype(BF16)

        for dr, z_ref in ((0, zf_ref), (1, zb_ref)):
            lbv, _ = _lower_bound(lbl_ref, dr)
            k_ref, bc_ref, dec_ref = ks_ref.at[dr], bcs_ref.at[dr], decs_ref.at[dr]
            _, keep = _chunk_consts(dr)
            cum_t = _chunk_consts(1 - dr)[0].astype(BF16)
            bc = bc_ref[...]
            qd_ref[...] = (q_ref[...] * jnp.exp(bc)).astype(BF16)
            ki_ref[...] = (k_ref[...] * jnp.exp(-bc)).astype(BF16)

            def local_step(m, carry, dr=dr, keep=keep):
                ns = [m * grp + u for u in range(grp)]
                rows = [_chunk_rows(n) for n in ns]
                rng = range(grp)
                qd = [qd_ref[r, :] for r in rows]
                ki = [ki_ref[r, :] for r in rows]
                doc = [do_ref[r, :] for r in rows]
                vc = [v_ref[r, :].astype(BF16) for r in rows]
                sc = [_dot(qd[u], ki[u], NT) for u in rng]
                dsc = [_dot(doc[u], vc[u], NT) for u in rng]
                inc = [_dot(vc[u], ki[u], TN) for u in rng]
                dinc = [_dot(doc[u], qd[u], TN) for u in rng]
                a = [jnp.where(keep, s, 0.0).astype(BF16) for s in sc]
                da = [jnp.where(keep, s, 0.0).astype(BF16) for s in dsc]
                dqd = [_dot(da[u], ki[u]) for u in rng]
                dki = [_dot(da[u], qd[u], TN) for u in rng]
                dv = [_dot(a[u], doc[u], TN) for u in rng]
                for u in rng:
                    ds_ref[ns[u]] = inc[u] * dec_ref[ns[u]][0:1, :]
                    dsl_ref[ns[u]] = dinc[u]
                    dqd_ref[rows[u], :] = dqd[u]
                    dki_ref[rows[u], :] = dki[u]
                    if dr == 0:
                        dv_ref[rows[u], :] = dv[u]
                    else:
                        dv_ref[rows[u], :] += dv[u]
                return carry

            lax.fori_loop(0, nc // grp, local_step, 0)

            def state_step(s, st, dr=dr):
                n = _chunk_of_step(s, dr, nc, ncc)
                inc = ds_ref[n]
                ds_ref[n] = st
                return st * dec_ref[n][0:1, :] + inc

            lax.fori_loop(0, nc, state_step, jnp.zeros((HEAD, HEAD), F32), unroll=4)

            def dstate_step(s, dst, dr=dr):
                n = _chunk_of_step(nc - 1 - s, dr, nc, ncc)
                inc = dsl_ref[n]
                dsl_ref[n] = dst
                return inc + dst * dec_ref[n][0:1, :]

            lax.fori_loop(0, nc, dstate_step, jnp.zeros((HEAD, HEAD), F32), unroll=4)

            def grad_step(m, carry, dr=dr, cum_t=cum_t):
                ns = [m * grp2 + u for u in range(grp2)]
                rows = [_chunk_rows(n) for n in ns]
                rng = range(grp2)
                st0 = [ds_ref[n] for n in ns]
                dst = [dsl_ref[n] for n in ns]
                dstb = [x.astype(BF16) for x in dst]
                dec = [dec_ref[n][0:1, :] for n in ns]
                doc = [do_ref[r, :] for r in rows]
                vc = [v_ref[r, :].astype(BF16) for r in rows]
                e = [jnp.exp(bc_ref[r, :]) for r in rows]
                einv = [jnp.exp(-bc_ref[r, :]) for r in rows]
                qd = [q_ref[rows[u], :] * e[u] for u in rng]
                ki = [k_ref[rows[u], :] * einv[u] for u in rng]
                kd = [ki[u] * dec[u] for u in rng]
                dqd_st = [_dot(doc[u], st0[u].astype(BF16)) for u in rng]
                dkd = [_dot(vc[u], dstb[u]) for u in rng]
                dv_st = [_dot(kd[u].astype(BF16), dstb[u], NT) for u in rng]
                dqd = [dqd_ref[rows[u], :] + dqd_st[u] for u in rng]
                dki = [dki_ref[r, :] for r in rows]
                dbc = [dqd[u] * qd[u] - dki[u] * ki[u] - dkd[u] * kd[u] for u in rng]
                cs = [_cumsum_chunk(cum_t, x) for x in dbc]
                for u in rng:
                    ddec = jnp.sum(dst[u] * st0[u], axis=0, keepdims=True)
                    dbl = jnp.sum(dkd[u] * kd[u], axis=0, keepdims=True) + ddec * dec[u]
                    dv_ref[rows[u], :] += dv_st[u]
                    dqd_ref[rows[u], :] = cs[u] + dbl
                    dki_ref[rows[u], :] = dki[u] * einv[u] + dkd[u] * (einv[u] * dec[u])
                    if dr == 0:
                        dq_ref[rows[u], :] = dqd[u] * e[u]
                    else:
                        dq_ref[rows[u], :] += dqd[u] * e[u]
                return carry

            lax.fori_loop(0, nc // grp2, grad_step, 0)

            sig, nsig, f = _gates(z_ref[...], lbv)
            common = (dqd_ref[...] / f - dki_ref[...]) * nsig
            dz_ref[dr] = (common * ((1.0 - lbv) * sig)).astype(BF16)
            dlb_ref[dr:dr + 1, :] = jnp.sum(common, axis=0, keepdims=True)

        dz_ref[2] = dv_ref[...].astype(BF16)
        dz_ref[3] = dq_ref[...].astype(BF16)

    sec = lambda s: pl.BlockSpec((None, t, HEAD), lambda h: (s, 0, h))
    col = pl.BlockSpec((seq, HEAD), lambda h: (0, h))
    tf32, tb16 = pltpu.VMEM((t, HEAD), F32), pltpu.VMEM((t, HEAD), BF16)
    states = pltpu.VMEM((nc, HEAD, HEAD), F32)
    return pl.pallas_call(
        body, name="hgrn_bwd", grid=(d // HEAD,),
        in_specs=[sec(0), sec(1), sec(2), sec(3), sec(4),
                  pl.BlockSpec((2, 3, HEAD), lambda h: (0, 0, h)), pl.BlockSpec((1, HEAD), lambda h: (0, h)),
                  col, col, pl.BlockSpec((2, t, HEAD), lambda h: (0, 0, h)),
                  pl.BlockSpec((2, t, HEAD), lambda h: (0, 0, h)),
                  pl.BlockSpec((2, nc, 8, HEAD), lambda h: (0, 0, 0, h))],
        out_specs=[pl.BlockSpec((5, t, HEAD), lambda h: (0, 0, h)),
                   pl.BlockSpec((1, HEAD), lambda h: (0, h)), pl.BlockSpec((2, HEAD), lambda h: (0, h))],
        out_shape=[jax.ShapeDtypeStruct((5, t, d), BF16), jax.ShapeDtypeStruct((1, d), F32),
                   jax.ShapeDtypeStruct((2, d), F32)],
        scratch_shapes=[tb16, tb16, tb16, tf32, tf32, tf32, tf32, states, states],
        compiler_params=_cp(),
    )(z1, z1, z1, z1, z1, lbl, onorm, o, dr_out, bcs, ks, decs)


def _place():
    x, y, c = lax.axis_index("x"), lax.axis_index("y"), lax.axis_index("c")
    chips = [(1 - x, y), (x, 1 - y), (1 - x, 1 - y)]
    return x, y, c, chips


def _relay_chips():
    x, y, c, _ = _place()
    first = c == 0
    near = (jnp.where(first, 1 - x, x), jnp.where(first, y, 1 - y))
    far = (jnp.where(first, x, 1 - x), jnp.where(first, 1 - y, y))
    return near, far, (1 - x, 1 - y)


def _cast_rows(src_ref, dst_ref, bufs):
    fbuf, bbuf, load_sems, store_sems = bufs
    tr = fbuf.shape[1]
    nblk = src_ref.shape[0] // tr

    def load(i, s):
        return pltpu.make_async_copy(src_ref.at[pl.ds(i * tr, tr)], fbuf.at[s], load_sems.at[s])

    def store(i, s):
        return pltpu.make_async_copy(bbuf.at[s], dst_ref.at[pl.ds(i * tr, tr)], store_sems.at[s])

    load(0, 0).start()

    def step(i, carry):
        s = i % 2
        load(i, s).wait()

        @pl.when(i + 1 < nblk)
        def _():
            load(i + 1, 1 - s).start()

        @pl.when(i >= 2)
        def _():
            store(i - 2, s).wait()

        bbuf[s] = fbuf[s].astype(BF16)
        store(i, s).start()
        return carry

    lax.fori_loop(0, nblk, step, 0)
    for i in range(max(nblk - 2, 0), nblk):
        store(i, i % 2).wait()


def _norm_rows(src_ref, xs_ref, h_ref, row0, g, shift, scale, bufs):
    fbuf, bbuf, load_sems, xs_sems, h_sems = bufs
    tm = fbuf.shape[1]
    nblk = src_ref.shape[0] // tm

    def load(i, s):
        return pltpu.make_async_copy(src_ref.at[pl.ds(i * tm, tm)], fbuf.at[s], load_sems.at[s])

    def put_x(i, s):
        return pltpu.make_async_copy(fbuf.at[s], xs_ref.at[pl.ds(row0 + i * tm, tm)], xs_sems.at[s])

    def put_h(i, s):
        return pltpu.make_async_copy(bbuf.at[s], h_ref.at[pl.ds(row0 + i * tm, tm)], h_sems.at[s])

    load(0, 0).start()

    def step(i, carry):
        s = i % 2
        load(i, s).wait()

        @pl.when(i >= 1)
        def _():
            put_x(i - 1, 1 - s).wait()

        @pl.when(i + 1 < nblk)
        def _():
            load(i + 1, 1 - s).start()

        @pl.when(i >= 2)
        def _():
            put_h(i - 2, s).wait()

        x = fbuf[s]
        rstd = lax.rsqrt(jnp.mean(x * x, axis=-1, keepdims=True) + EPS)
        bbuf[s] = ((x * rstd) * g * (1.0 + scale) + shift).astype(BF16)
        put_x(i, s).start()
        put_h(i, s).start()
        return carry

    lax.fori_loop(0, nblk, step, 0)
    put_x(nblk - 1, (nblk - 1) % 2).wait()
    for i in range(max(nblk - 2, 0), nblk):
        put_h(i, i % 2).wait()


def allgather_shards(bufs, after, casts, norm):
    n, m = len(bufs), len(casts)
    cast_rows = [256 if a.shape[0] % 256 == 0 else a.shape[0] for a in casts]
    ctx, x_lat = norm[:2]
    cx, d = ctx.shape
    t = cx + x_lat.shape[0]
    tm = _row_block(cx)

    def body(*refs):
        refs = list(refs)
        take = lambda k: [refs.pop(0) for _ in range(k)]
        take(n + 1)
        cast_src = take(m)
        ctx_ref, x_ref, g_ref, sh_ref, sc_ref = take(5)
        outs = take(n)
        done_ref, = take(1)
        cast_dst = take(m)
        h_ref, xs_ref = take(2)
        send_sems, recv_sems = take(2)
        cast_bufs = take(4 * m)
        norm_bufs = take(5)
        done_ref[...] = jnp.zeros((8, 128), F32)
        x, y, c, _ = _place()
        me = (x, y, c)
        p = 2 * x + y
        near, far, diag = _relay_chips()
        half = [pl.ds(c * (s.shape[1] // 2), s.shape[1] // 2) for s in bufs]
        other = [pl.ds((1 - c) * (s.shape[1] // 2), s.shape[1] // 2) for s in bufs]
        slot = lambda chip: 2 * chip[0] + chip[1]

        def remote(i, k, ref, to):
            return pltpu.make_async_remote_copy(src_ref=ref, dst_ref=ref, send_sem=send_sems.at[6 * i + k],
                                                recv_sem=recv_sems.at[6 * i + k], device_id=to, device_id_type=MESH)

        sends = []

        def send(i, k, ref, to):
            cp = remote(i, k, ref, to)
            cp.start()
            sends.append(cp)

        for i in range(n):
            mine = outs[i].at[p, half[i]]
            send(i, 0, mine, (*near, c))
            send(i, 1, mine, (*far, c))
        for j in range(m):
            _cast_rows(cast_src[j], cast_dst[j].at[p], cast_bufs[4 * j:4 * j + 4])
        gv = g_ref[...]
        _norm_rows(ctx_ref, xs_ref, h_ref, 0, gv, sh_ref[0:1, :], sc_ref[0:1, :], norm_bufs)
        _norm_rows(x_ref, xs_ref, h_ref, cx, gv, sh_ref[1:2, :], sc_ref[1:2, :], norm_bufs)
        for i in range(n):
            landed = outs[i].at[slot(near), half[i]]
            remote(i, 0, landed, me).wait_recv()
            send(i, 2, landed, (*far, c))
            send(i, 3, landed, (x, y, 1 - c))
        for i in range(n):
            landed = outs[i].at[slot(far), half[i]]
            remote(i, 1, landed, me).wait_recv()
            send(i, 4, landed, (x, y, 1 - c))
        for i in range(n):
            landed = outs[i].at[slot(diag), half[i]]
            remote(i, 2, landed, me).wait_recv()
            send(i, 5, landed, (x, y, 1 - c))
        for i in range(n):
            for k, chip in ((3, far), (4, near), (5, diag)):
                remote(i, k, outs[i].at[slot(chip), other[i]], me).wait_recv()
        for cp in sends:
            cp.wait_send()

    return pl.pallas_call(
        body, name="allgather_shards",
        in_specs=[ANY] * (n + 1 + m + 2) + [VMEM] * 3, out_specs=[ANY] * n + [VMEM] + [ANY] * (m + 2),
        out_shape=[jax.ShapeDtypeStruct(s.shape, s.dtype) for s in bufs] + [jax.ShapeDtypeStruct((8, 128), F32)]
        + [jax.ShapeDtypeStruct((4,) + a.shape, BF16) for a in casts]
        + [jax.ShapeDtypeStruct((t, d), BF16), jax.ShapeDtypeStruct((t, d), F32)],
        input_output_aliases={i: i for i in range(n)},
        scratch_shapes=[pltpu.SemaphoreType.DMA((6 * n,)), pltpu.SemaphoreType.DMA((6 * n,))] + [
            s for a, tr in zip(casts, cast_rows) for s in (
                pltpu.VMEM((2, tr, a.shape[1]), F32), pltpu.VMEM((2, tr, a.shape[1]), BF16),
                pltpu.SemaphoreType.DMA((2,)), pltpu.SemaphoreType.DMA((2,)))] + [
            pltpu.VMEM((2, tm, d), F32), pltpu.VMEM((2, tm, d), BF16), pltpu.SemaphoreType.DMA((2,)),
            pltpu.SemaphoreType.DMA((2,)), pltpu.SemaphoreType.DMA((2,))],
        compiler_params=_cp(has_side_effects=True),
    )(*bufs, after, *casts, *norm)


def pair_sum(grad, got, chip_core):
    _, r, cc = grad.shape
    hr = r // 2
    tr = 256 if hr % 256 == 0 else hr
    nb = hr // tr

    def body(cc_ref, a_ref, b_ref, own_ref, sb_ref):
        s = a_ref[...].astype(F32) + b_ref[...].astype(F32)
        sb_ref[...] = s.astype(BF16)

        @pl.when(pl.program_id(1) == cc_ref[0])
        def _():
            own_ref[...] = s

    grid_spec = pltpu.PrefetchScalarGridSpec(
        num_scalar_prefetch=1, grid=(nb, 4),
        in_specs=[pl.BlockSpec((None, tr, cc), lambda i, qi, cc_ref: (qi, cc_ref[1] * nb + i, 0)),
                  pl.BlockSpec((None, tr, cc), lambda i, qi, cc_ref: (qi, i, 0))],
        out_specs=[pl.BlockSpec((tr, cc), lambda i, qi, cc_ref: (i, 0)),
                   pl.BlockSpec((None, tr, cc), lambda i, qi, cc_ref: (qi, i, 0))])
    return pl.pallas_call(
        body, name="pair_sum", grid_spec=grid_spec,
        out_shape=[jax.ShapeDtypeStruct((hr, cc), F32), jax.ShapeDtypeStruct((4, hr, cc), BF16)],
        compiler_params=_cp(),
    )(chip_core, grad, got)


def owner_sum(own, got, chip_core):
    hr, cc = own.shape
    tr = 256 if hr % 256 == 0 else hr
    nb = hr // tr

    def body(cc_ref, a_ref, b_ref, o_ref):
        s = a_ref[...] + b_ref[0].astype(F32)
        s = s + b_ref[1].astype(F32)
        o_ref[...] = s + b_ref[2].astype(F32)

    grid_spec = pltpu.PrefetchScalarGridSpec(
        num_scalar_prefetch=1, grid=(nb,),
        in_specs=[pl.BlockSpec((tr, cc), lambda i, cc_ref: (i, 0)),
                  pl.BlockSpec((3, tr, cc), lambda i, cc_ref: (0, i, 0))],
        out_specs=pl.BlockSpec((tr, cc), lambda i, cc_ref: (cc_ref[1] * nb + i, 0)))
    return pl.pallas_call(
        body, name="owner_sum", grid_spec=grid_spec,
        out_shape=jax.ShapeDtypeStruct((2 * hr, cc), F32), compiler_params=_cp(),
    )(chip_core, own, got)


def share_halves(bufs):
    n = len(bufs)

    def body(*refs):
        outs = refs[n:2 * n]
        send_sems, recv_sems = refs[2 * n:]
        x, y, c, _ = _place()
        copies = []
        for i in range(n):
            hr = bufs[i].shape[0] // 2
            mine = outs[i].at[pl.ds(c * hr, hr)]
            cp = pltpu.make_async_remote_copy(
                src_ref=mine, dst_ref=mine, send_sem=send_sems.at[i], recv_sem=recv_sems.at[i],
                device_id=(x, y, 1 - c), device_id_type=MESH)
            cp.start()
            copies.append((cp, outs[i].at[pl.ds((1 - c) * hr, hr)]))
        for i, (cp, theirs) in enumerate(copies):
            cp.wait_send()
            pltpu.make_async_remote_copy(
                src_ref=theirs, dst_ref=theirs, send_sem=send_sems.at[i], recv_sem=recv_sems.at[i],
                device_id=(x, y, c), device_id_type=MESH).wait_recv()

    return pl.pallas_call(
        body, name="share_halves",
        in_specs=[ANY] * n, out_specs=[ANY] * n,
        out_shape=[jax.ShapeDtypeStruct(b.shape, b.dtype) for b in bufs],
        input_output_aliases={i: i for i in range(n)},
        scratch_shapes=[pltpu.SemaphoreType.DMA((n,)), pltpu.SemaphoreType.DMA((n,))],
        compiler_params=pltpu.CompilerParams(has_side_effects=True),
    )(*bufs)


def allgather8(v, name, per_peer=False):
    r, n = v.shape[-2:]

    def body(v_ref, out_ref, send_sems, recv_sems):
        x, y, c, _ = _place()
        me = 4 * x + 2 * y + c
        out_ref[me] = v_ref[me] if per_peer else v_ref[...]

        def copy(k, peer_index, slot, to):
            return pltpu.make_async_remote_copy(
                src_ref=v_ref.at[peer_index] if per_peer else v_ref, dst_ref=out_ref.at[slot],
                send_sem=send_sems.at[k - 1], recv_sem=recv_sems.at[k - 1], device_id=to, device_id_type=MESH)

        peers = []
        for k in range(1, 8):
            px = 1 - x if (k >> 2) & 1 else x
            py = 1 - y if (k >> 1) & 1 else y
            pc = 1 - c if k & 1 else c
            peers.append((px, py, pc))
            copy(k, 4 * px + 2 * py + pc, me, (px, py, pc)).start()
        for k, (px, py, pc) in enumerate(peers, start=1):
            copy(k, me, 4 * px + 2 * py + pc, (x, y, c)).wait_recv()
        for k, (px, py, pc) in enumerate(peers, start=1):
            copy(k, 4 * px + 2 * py + pc, me, (px, py, pc)).wait_send()

    return pl.pallas_call(
        body, name=name, in_specs=[VMEM], out_specs=VMEM,
        out_shape=jax.ShapeDtypeStruct((8, r, n), v.dtype),
        scratch_shapes=[pltpu.SemaphoreType.DMA((7,)), pltpu.SemaphoreType.DMA((7,))],
        compiler_params=_cp(has_side_effects=True),
    )(v)


HBM = pl.BlockSpec(memory_space=pltpu.HBM)
SEM = pl.BlockSpec(memory_space=pltpu.SEMAPHORE)
DATAFLOW = pltpu.SideEffectType.DATAFLOW_SIDE_EFFECTING


def _descriptors(plan, refs, send_sems, recv_sems, arrivals=True):
    x, y, c, _ = _place()
    sends, recvs = plan(refs)
    out = [pltpu.make_async_remote_copy(src_ref=src, dst_ref=dst, send_sem=send_sems.at[k],
                                        recv_sem=recv_sems.at[k], device_id=to, device_id_type=MESH)
           for k, (src, dst, to) in enumerate(sends)]
    if not arrivals:
        return out, []
    inn = [pltpu.make_async_remote_copy(src_ref=land, dst_ref=land, send_sem=send_sems.at[k],
                                        recv_sem=recv_sems.at[k], device_id=(x, y, c), device_id_type=MESH)
           for k, land in enumerate(recvs)]
    return out, inn


def copies_start(name, arrays, n_copies, plan, after, carried=()):
    na, nall = len(arrays), len(arrays) + len(carried)
    everything = list(arrays) + list(carried)

    def body(*refs):
        out, _ = _descriptors(plan, refs[:na], refs[nall + 1], refs[nall + 2], arrivals=False)
        for cp in out:
            cp.start()
        refs[-1][...] = jnp.zeros((8, 128), F32)

    res = pl.pallas_call(
        body, name=name,
        out_shape=(pltpu.SemaphoreType.DMA((n_copies,)), pltpu.SemaphoreType.DMA((n_copies,)),
                   *[pltpu.HBM(a.shape, a.dtype) for a in everything], jax.ShapeDtypeStruct((8, 128), F32)),
        in_specs=[HBM] * nall + [ANY], out_specs=(SEM, SEM, *[HBM] * nall, VMEM),
        input_output_aliases={i: i + 2 for i in range(nall)},
        compiler_params=pltpu.CompilerParams(has_side_effects=DATAFLOW),
    )(*[pltpu.with_memory_space_constraint(a, pltpu.HBM) for a in everything], after)
    return res[0], res[1], list(res[2:2 + na]), res[-1], list(res[2 + na:2 + nall])


def copies_wait(name, started, plan, after):
    send_sems, recv_sems, arrays = started[:3]
    na = len(arrays)
    after = list(after) if isinstance(after, (list, tuple)) else [after]

    def body(*refs):
        out, inn = _descriptors(plan, refs[:na], refs[na], refs[na + 1])
        for cp in out:
            cp.wait_send()
        for cp in inn:
            cp.wait_recv()
        refs[-1][...] = jnp.zeros((8, 128), F32)

    res = pl.pallas_call(
        body, name=name,
        out_shape=(*[pltpu.HBM(a.shape, a.dtype) for a in arrays], jax.ShapeDtypeStruct((8, 128), F32)),
        in_specs=[HBM] * na + [SEM, SEM] + [ANY] * len(after), out_specs=(*[HBM] * na, VMEM),
        input_output_aliases={i: i for i in range(na)},
        compiler_params=pltpu.CompilerParams(has_side_effects=DATAFLOW),
    )(*arrays, send_sems, recv_sems, *after)
    return list(res[:na]), res[-1]


def _rows_half(r, c):
    return pl.ds(c * (r // 2), r // 2), pl.ds((1 - c) * (r // 2), r // 2)


def plan_gather_neighbours(refs):
    x, y, c, _ = _place()
    p = 2 * x + y
    near, far, _ = _relay_chips()
    sends, recvs = [], []
    for buf in refs:
        mine, _ = _rows_half(buf.shape[1], c)
        for chip in (near, far):
            sends.append((buf.at[p, mine], buf.at[p, mine], (*chip, c)))
            recvs.append(buf.at[2 * chip[0] + chip[1], mine])
    return sends, recvs


def plan_gather_relay(refs):
    x, y, c, _ = _place()
    near, far, diag = _relay_chips()
    slot = lambda chip: 2 * chip[0] + chip[1]
    sends, recvs = [], []
    for buf in refs:
        mine, theirs = _rows_half(buf.shape[1], c)
        landed = buf.at[slot(near), mine]
        sends.append((landed, landed, (*far, c)))
        recvs.append(buf.at[slot(diag), mine])
        for sent, got in ((near, far), (far, near)):
            sends.append((buf.at[slot(sent), mine], buf.at[slot(sent), mine], (x, y, 1 - c)))
            recvs.append(buf.at[slot(got), theirs])
    return sends, recvs


def plan_gather_d2d(refs):
    x, y, c, _ = _place()
    _, _, diag = _relay_chips()
    sends, recvs = [], []
    for buf in refs:
        mine, theirs = _rows_half(buf.shape[1], c)
        landed = buf.at[2 * diag[0] + diag[1], mine]
        sends.append((landed, landed, (x, y, 1 - c)))
        recvs.append(buf.at[2 * diag[0] + diag[1], theirs])
    return sends, recvs


def plan_exchange(refs):
    x, y, c, _ = _place()
    n = len(refs) // 2
    sends, recvs = [], []
    for grad, land in zip(refs[:n], refs[n:]):
        _, theirs = _rows_half(grad.shape[1], c)
        sends.append((grad.at[:, theirs], land, (x, y, 1 - c)))
        recvs.append(land)
    return sends, recvs


def plan_scatter(refs):
    x, y, c, chips = _place()
    n = len(refs) // 2
    sends, recvs = [], []
    for part, land in zip(refs[:n], refs[n:]):
        for j, chip in enumerate(chips):
            sends.append((part.at[2 * chip[0] + chip[1]], land.at[j], (*chip, c)))
            recvs.append(land.at[j])
    return sends, recvs


def plan_share(refs):
    x, y, c, _ = _place()
    sends, recvs = [], []
    for buf in refs:
        mine, theirs = _rows_half(buf.shape[0], c)
        sends.append((buf.at[mine], buf.at[mine], (x, y, 1 - c)))
        recvs.append(buf.at[theirs])
    return sends, recvs


def put_in_slot(w, chip, dtype, name):
    r, c = w.shape
    tr = 256 if r % 256 == 0 else r

    def body(chip_ref, w_ref, o_ref):
        o_ref[...] = w_ref[...].astype(dtype)

    grid_spec = pltpu.PrefetchScalarGridSpec(
        num_scalar_prefetch=1, grid=(r // tr,),
        in_specs=[pl.BlockSpec((tr, c), lambda i, chip_ref: (i, 0))],
        out_specs=pl.BlockSpec((None, tr, c), lambda i, chip_ref: (chip_ref[0], i, 0)))
    return pl.pallas_call(body, name=name, grid_spec=grid_spec,
                          out_shape=jax.ShapeDtypeStruct((4, r, c), dtype), compiler_params=_cp())(chip, w)


def ada_fwd(s_in, ada_w, ada_b, tn):
    nl, d, ws = ada_w.shape

    def body(s_ref, w_ref, b_ref, so_ref, mod_ref):
        s = _silu(s_ref[...])
        so_ref[...] = s
        mod_ref[...] = _dot(s.astype(BF16), w_ref[...].astype(BF16)) + b_ref[...]

    return pl.pallas_call(
        body, name="ada_fwd", grid=(nl, ws // tn),
        in_specs=[pl.BlockSpec((16, d), lambda l, j: (0, 0)),
                  pl.BlockSpec((None, d, tn), lambda l, j: (l, 0, j)),
                  pl.BlockSpec((None, 1, tn), lambda l, j: (l, 0, j))],
        out_specs=[pl.BlockSpec((16, d), lambda l, j: (0, 0)),
                   pl.BlockSpec((None, 16, tn), lambda l, j: (l, 0, j))],
        out_shape=[jax.ShapeDtypeStruct((16, d), F32), jax.ShapeDtypeStruct((nl, 16, ws), F32)],
        compiler_params=_cp(),
    )(s_in, ada_w, ada_b)


def _adamw_math(w, g, m, v):
    m = ADAM_B1 * m + (1.0 - ADAM_B1) * g
    v = ADAM_B2 * v + (1.0 - ADAM_B2) * (g * g)
    m_hat = m / (1.0 - ADAM_B1 ** ADAM_STEP)
    v_hat = v / (1.0 - ADAM_B2 ** ADAM_STEP)
    delta = -ADAM_LR * (m_hat / (jnp.sqrt(v_hat) + ADAM_EPS) + ADAM_WD * w)
    return delta, m, v


def ada_bwd_adamw(s, dm, w, m, v):
    nl, d, ws = w.shape
    tr = 256 if d % 256 == 0 else 128

    def body(s_ref, dm_ref, w_ref, m_ref, v_ref, g_ref, dl_ref, mo_ref, vo_ref, dc_ref):
        dmv = dm_ref[...].astype(BF16)
        wv = w_ref[...]
        g = _dot(s_ref[...].astype(BF16), dmv, TN)
        g_ref[...] = g
        dl_ref[...], mo_ref[...], vo_ref[...] = _adamw_math(wv, g, m_ref[...], v_ref[...])
        dc_ref[...] = _dot(dmv[8:16, :], wv.astype(BF16), NT)

    wblk = pl.BlockSpec((None, tr, ws), lambda l, i: (l, i, 0))
    wshape = jax.ShapeDtypeStruct((nl, d, ws), F32)
    return pl.pallas_call(
        body, name="ada_bwd_adamw", grid=(nl, d // tr),
        in_specs=[pl.BlockSpec((16, tr), lambda l, i: (0, i)),
                  pl.BlockSpec((None, 16, ws), lambda l, i: (l, 0, 0)), wblk, wblk, wblk],
        out_specs=[wblk, wblk, wblk, wblk, pl.BlockSpec((None, 8, tr), lambda l, i: (l, 0, i))],
        out_shape=[wshape, wshape, wshape, wshape, jax.ShapeDtypeStruct((nl, 8, d), F32)],
        compiler_params=_cp(),
    )(s, dm, w, m, v)


def adamw(w, g, m, v, name, with_grad=False):
    r, c = w.shape
    tr = 256 if r % 256 == 0 else r

    def body(w_ref, g_ref, m_ref, v_ref, dl_ref, mo_ref, vo_ref, *g_out):
        gv = g_ref[...]
        dl_ref[...], mo_ref[...], vo_ref[...] = _adamw_math(w_ref[...], gv, m_ref[...], v_ref[...])
        if with_grad:
            g_out[0][...] = gv

    blk = pl.BlockSpec((tr, c), lambda i: (i, 0))
    shape = jax.ShapeDtypeStruct((r, c), F32)
    n_out = 4 if with_grad else 3
    return pl.pallas_call(body, name=name, grid=(r // tr,), in_specs=[blk] * 4, out_specs=[blk] * n_out,
                          out_shape=[shape] * n_out, compiler_params=_cp())(w, g, m, v)


ROW_MOD = 10


def small_reduce(gathered):
    _, rows, d = gathered.shape

    def body(g_ref, o_ref):
        tot = g_ref[0]
        for b in range(1, 8):
            tot = tot + g_ref[b]
        o_ref[0:rows, :] = tot
        for layer in range(2):
            lat = ROW_MOD + 6 * layer
            o_ref[24 + 3 * layer:27 + 3 * layer, :] = tot[lat:lat + 3, :] + tot[lat + 3:lat + 6, :]
        o_ref[30:32, :] = jnp.zeros((2, d), F32)

    return pl.pallas_call(body, name="small_reduce", in_specs=[VMEM], out_specs=VMEM,
                          out_shape=jax.ShapeDtypeStruct((32, d), F32), compiler_params=_cp())(gathered)


def lb_logits_grad(lbl, dlb):
    _, _, n = lbl.shape

    def body(l_ref, d_ref, o_ref):
        for dr in range(2):
            _, (p0, p1, p2) = _lower_bound(l_ref, dr)
            dv = d_ref[dr:dr + 1, :]
            o_ref[dr, 0:1, :] = p0 * p2 * dv
            o_ref[dr, 1:2, :] = p1 * p2 * dv
            o_ref[dr, 2:3, :] = -p2 * (p0 + p1) * dv

    return pl.pallas_call(body, name="lb_logits_grad", in_specs=[VMEM, VMEM], out_specs=VMEM,
                          out_shape=jax.ShapeDtypeStruct((2, 3, n), F32), compiler_params=_cp())(lbl, dlb)


def c_ctx_grad(parts, c_ctx):
    d = c_ctx.shape[1]

    def body(p_ref, c_ref, o_ref):
        tot = p_ref[0, 0:1, :]
        for chip in range(1, 4):
            tot = tot + p_ref[2 * chip, 0:1, :]
        o_ref[...] = tot * _dsilu(c_ref[...])

    return pl.pallas_call(body, name="c_ctx_grad", in_specs=[VMEM, VMEM], out_specs=VMEM,
                          out_shape=jax.ShapeDtypeStruct((1, d), F32), compiler_params=_cp())(parts, c_ctx)


def kernel(x, c, ctx, c_ctx, ada_w, ada_b, pre_g, post_g, ev_w_in, ev_pool_w, ev_pool_scale, ev_conv_w, ev_conv_b, ev_w_out, od_w_in, od_onorm_g, od_w_out, lb_logits, loss_target, m_c_ctx, m_ada_w, m_ada_b, m_pre_g, m_post_g, m_ev_w_in, m_ev_pool_w, m_ev_pool_scale, m_ev_conv_w, m_ev_conv_b, m_ev_w_out, m_od_w_in, m_od_onorm_g, m_od_w_out, m_lb_logits, v_c_ctx, v_ada_w, v_ada_b, v_pre_g, v_post_g, v_ev_w_in, v_ev_pool_w, v_ev_pool_scale, v_ev_conv_w, v_ev_conv_b, v_ev_w_out, v_od_w_in, v_od_onorm_g, v_od_w_out, v_lb_logits):
    _, seq, d = x.shape
    cx = ctx.shape[1]
    t = cx + seq
    half_d = d // 2
    g = half_d // N_POOL
    tn = d // 4
    xi, yi, ci = lax.axis_index("x"), lax.axis_index("y"), lax.axis_index("c")
    chip = 2 * xi + yi
    chip_arr = jnp.reshape(chip, (1,)).astype(jnp.int32)
    chip_core_arr = jnp.stack([chip, ci]).astype(jnp.int32)

    c_rows = jnp.concatenate([c, jnp.zeros((7, d), F32)], axis=0)
    c_all = allgather8(c_rows, "allgather_c")[:, 0, :]
    s_in = jnp.concatenate([c_all, c_ctx.reshape(1, d), jnp.zeros((7, d), F32)], axis=0)
    ws_ada = ada_w.shape[2]
    ada_b_mine = lax.dynamic_slice(ada_b, (0, chip * ws_ada), (2, ws_ada)).reshape(2, 1, ws_ada)
    s_act, mod_mine = ada_fwd(s_in, ada_w, ada_b_mine, tn)
    mod_rows = jnp.concatenate([
        mod_mine[:, :8].transpose(1, 0, 2), jnp.broadcast_to(mod_mine[:, 8][None], (8, 2, ws_ada)),
        jnp.zeros((8, 4, ws_ada), F32)], axis=1)
    mod_all = allgather8(mod_rows, "exchange_mod", per_peer=True)

    pad = lambda a, rows: jnp.concatenate([a, jnp.zeros((rows - a.shape[0], g), F32)], axis=0)
    small = jnp.concatenate([
        ev_pool_w.reshape(g, g), pad(ev_conv_w.reshape(3, g), 8), pad(od_onorm_g.reshape(2, g), 8),
        pad(lb_logits.reshape(12, g), 16)], axis=0)
    by_chip = mod_all[0::2]
    mods = jnp.stack([by_chip[:, 2:4], by_chip[:, 0:2]]).transpose(2, 0, 1, 3).reshape(2, 2, 3 * d)
    shift, scale, gate = mods[:, :, :d], mods[:, :, d:2 * d], mods[:, :, 2 * d:]
    ev_in_g, ev_out_g, small_g, ev_done, od_in_mine, od_out_mine, h0, xs = allgather_shards([
        put_in_slot(ev_w_in[0], chip_arr, BF16, "cast_ev_w_in"),
        put_in_slot(ev_w_out[0], chip_arr, BF16, "cast_ev_w_out"),
        put_in_slot(small, chip_arr, F32, "place_small")], mod_all, [od_w_in[0], od_w_out[0]],
        (ctx[0], x[0], pre_g[0:1], shift[0], scale[0]))
    od_ici = copies_start("gather_od_ici_start", [od_in_mine, od_out_mine], 4, plan_gather_neighbours, ev_done,
                          carried=[h0])
    h0 = od_ici[4][0]
    ev_out3 = ev_out_g.reshape(1, d, d)
    pool_w_full = small_g[:, :g].reshape(4, N_POOL, g // 4, g).transpose(1, 0, 2, 3).reshape(N_POOL, g, g)
    conv_w_full = small_g[:, g:g + 3].transpose(1, 0, 2).reshape(3, half_d)
    onorm_full = small_g[:, g + 8:g + 10].reshape(1, d)
    lbl_full = small_g[:, g + 16:g + 28].reshape(4, 2, 3, 2 * g).transpose(1, 2, 0, 3).reshape(2, 3, d)

    z0 = mm_nn(h0, ev_in_g, half_d, tn, "mm_ev_in")
    u = mix_b_fwd(z0, conv_w_full, ev_conv_b, mix_a_fwd(z0, pool_w_full, ev_pool_scale, cx), cx)
    od_relay = copies_start("gather_od_relay_start",
                            copies_wait("gather_od_ici_wait", od_ici, plan_gather_neighbours, u)[0],
                            6, plan_gather_relay, u)
    y0 = mm_nn(u, ev_out3, d, tn, "mm_ev_out")[0]
    xs1, h1 = post_fwd_norm(xs, y0, post_g[0:1] + od_relay[3][0:1, 0:1], gate[0],
                            pre_g[1:2], shift[1], scale[1], cx)
    od_d2d = copies_start("gather_od_d2d_start",
                          copies_wait("gather_od_relay_wait", od_relay, plan_gather_relay, xs1)[0],
                          2, plan_gather_d2d, xs1)
    (od_in_g, od_out_g), _ = copies_wait("gather_od_d2d_wait", od_d2d, plan_gather_d2d, od_d2d[3])
    od_out3 = od_out_g.reshape(1, d, d)

    z1 = mm_nn(h1, od_in_g, d, tn, "mm_od_in")
    o1, r1, bcs1, ks1, decs1 = hgrn_fwd(z1, lbl_full, onorm_full, cx)
    y1 = mm_nn(r1, od_out3, d, tn, "mm_od_out")[0]
    sq, dx2, dy1, dgate1, dpost1 = post_loss(xs1, y1, post_g[1:2], gate[1], loss_target[0], cx)

    dr1 = mm_nt(dy1[None], None, od_out3, tn, "mm_od_out_dx")
    g_od_out = mm_tn(r1, dy1[None], None, d, tn, "mm_od_out_dw")
    dz1, donorm, dlb = hgrn_bwd(z1, lbl_full, onorm_full, o1, dr1, bcs1, ks1, decs1, cx)
    dh1 = mm_nt(dz1, None, od_in_g, tn, "mm_od_in_dx")
    g_od_in = mm_tn(h1, dz1, None, od_in_g.shape[2], tn, "mm_od_in_dw")
    od_grads = [g_od_in, g_od_out.reshape(4, d // 4, d)]
    half_zone = lambda a, lead, dt: lax.empty((lead, a.shape[1] // 2, a.shape[2]), dt)
    od_ex = copies_start("reduce_od_exchange_start", od_grads + [half_zone(a, 4, a.dtype) for a in od_grads],
                         2, plan_exchange, dh1)

    dxs1, dpre1, dshift1, dscale1, dy0, dgate0, dpost0 = normmod_bwd(
        xs1, dh1, pre_g[1:2] + od_ex[3][0:1, 0:1], scale[1], dx2, cx, True,
        prev=(y0, post_g[0:1], gate[0]))
    du = mm_nt(dy0[None], None, ev_out3, tn, "mm_ev_out_dx")
    g_ev_out = mm_tn(u, dy0[None], None, d, tn, "mm_ev_out_dw")
    od_got, _ = copies_wait("reduce_od_exchange_wait", od_ex, plan_exchange, g_ev_out)
    od_sums = [pair_sum(od_got[i], od_got[2 + i], chip_core_arr) for i in range(2)]
    od_sc = copies_start("reduce_od_scatter_start",
                         [sb for _, sb in od_sums] + [half_zone(a, 3, BF16) for a in od_grads],
                         6, plan_scatter, du)
    dz0a, g_pool_w, dpool_scale = mix_a_bwd(z0, du, pool_w_full, ev_pool_scale + od_sc[3][0:1, 0:1], cx)
    dz0b, dconv_w, dconv_b = mix_b_bwd(z0, du, conv_w_full, ev_conv_b + od_sc[3][0:1, 0:1], cx)
    g_ev_in = mm_tn(h0, dz0a, dz0b, ev_in_g.shape[2], tn, "mm_ev_in_dw")
    ev_grads = [g_ev_in, g_ev_out.reshape(4, d // 4, d), g_pool_w.reshape(4, g, g)]
    ev_ex = copies_start("reduce_ev_exchange_start", ev_grads + [half_zone(a, 4, a.dtype) for a in ev_grads],
                         3, plan_exchange, dpool_scale)
    dh0 = mm_nt(dz0a, dz0b, ev_in_g, tn, "mm_ev_in_dx")
    dxs0, dpre0, dshift0, dscale0 = normmod_bwd(xs, dh0, pre_g[0:1] + ev_ex[3][0:1, 0:1], scale[0], dxs1,
                                                cx, False, True)
    grad_x = dxs0[None]
    ev_got, _ = copies_wait("reduce_ev_exchange_wait", ev_ex, plan_exchange, dxs0)
    ev_sums = [pair_sum(ev_got[i], ev_got[3 + i], chip_core_arr) for i in range(3)]

    zrow = jnp.zeros((1, d), F32)
    small_rows = jnp.concatenate([
        dpre0, dpre1, dpost0, dpost1,
        jnp.concatenate([dpool_scale, dconv_b], axis=1),
        jnp.concatenate([dconv_w.reshape(1, 3 * half_d), jnp.zeros((1, half_d), F32)], axis=1).reshape(2, d),
        donorm, dlb,
        dshift0[1:2], dscale0[1:2], dgate0[1:2], dshift0[0:1], dscale0[0:1], dgate0[0:1],
        dshift1[1:2], dscale1[1:2], dgate1[1:2], dshift1[0:1], dscale1[0:1], zrow,
        jnp.concatenate([sq[0:1], jnp.zeros((1, d - 128), F32)], axis=1),
        zrow], axis=0)
    small_all = allgather8(small_rows, "allgather_small")
    ev_sc = copies_start("reduce_ev_scatter_start",
                         [sb for _, sb in ev_sums] + [half_zone(a, 3, BF16) for a in ev_grads],
                         9, plan_scatter, small_all)
    od_recv, _ = copies_wait("reduce_od_scatter_wait", od_sc, plan_scatter, [dxs0, ev_sc[3]])
    od_sh = copies_start("reduce_od_share_start",
                         [owner_sum(od_sums[i][0], od_recv[2 + i], chip_core_arr) for i in range(2)],
                         2, plan_share, dxs0)
    tot = small_reduce(small_all + ev_sc[3][0:1, 0:1])
    loss = tot[22, 0] * (0.5 / d)

    dm_rows = []
    for layer in range(2):
        lat = ROW_MOD + 6 * layer
        dm_lat = small_all[:, lat:lat + 3].reshape(8, 3 * d)
        dm_ctx = tot[lat + 3:lat + 6].reshape(1, 3 * d)
        dm_rows.append(jnp.concatenate([dm_lat, dm_ctx, jnp.zeros((7, 3 * d), F32)], axis=0))
    dm_full = jnp.stack(dm_rows)
    dm_mine = lax.dynamic_slice(dm_full, (0, 0, chip * ws_ada), (2, 16, ws_ada))

    def step(w, gr, m, v, name, with_grad=False):
        shape = w.shape
        cols = shape[-1]
        two_d = lambda a: a.reshape(-1, cols)
        res = adamw(two_d(w), two_d(gr), two_d(m), two_d(v), "adamw_" + name, with_grad)
        return tuple(a.reshape(shape) for a in res)

    grad_ada_b = tot[24:30].reshape(2, 3 * d)
    grad_pre_g = tot[0:2]
    grad_post_g = tot[2:4]
    grad_ev_pool_scale = tot[4:5, :half_d]
    grad_ev_conv_b = tot[4:5, half_d:]
    conv_w_tot = tot[5:7].reshape(1, 2 * d)[:, :3 * half_d].reshape(3, N_POOL, g)
    grad_ev_conv_w = lax.dynamic_slice(conv_w_tot, (0, chip, 0), (3, 1, g)).reshape(1, 3, g)
    grad_od_onorm_g = lax.dynamic_slice(tot[7:8], (0, chip * 2 * g), (1, 2 * g))
    dlb_mine = lax.dynamic_slice(tot[8:10], (0, chip * 2 * g), (2, 2 * g))
    grad_lb_logits = lb_logits_grad(lb_logits, dlb_mine)
    upd = {
        "ada_b": step(ada_b, grad_ada_b, m_ada_b, v_ada_b, "ada_b"),
        "pre_g": step(pre_g, grad_pre_g, m_pre_g, v_pre_g, "pre_g"),
        "post_g": step(post_g, grad_post_g, m_post_g, v_post_g, "post_g"),
        "ev_pool_scale": step(ev_pool_scale, grad_ev_pool_scale, m_ev_pool_scale, v_ev_pool_scale, "ev_pool_scale"),
        "ev_conv_w": step(ev_conv_w, grad_ev_conv_w, m_ev_conv_w, v_ev_conv_w, "ev_conv_w"),
        "ev_conv_b": step(ev_conv_b, grad_ev_conv_b, m_ev_conv_b, v_ev_conv_b, "ev_conv_b"),
        "od_onorm_g": step(od_onorm_g, grad_od_onorm_g, m_od_onorm_g, v_od_onorm_g, "od_onorm_g"),
        "lb_logits": step(lb_logits, grad_lb_logits, m_lb_logits, v_lb_logits, "lb_logits"),
    }
    grad_ada_w, delta_ada_w, new_m_ada_w, new_v_ada_w, dctx_part = ada_bwd_adamw(
        s_act, dm_mine, ada_w, m_ada_w, v_ada_w)
    upd["ada_w"] = (delta_ada_w, new_m_ada_w, new_v_ada_w)
    (grad_od_w_in, grad_od_w_out), _ = copies_wait("reduce_od_share_wait", od_sh, plan_share, dctx_part)
    upd["od_w_in"] = step(od_w_in, grad_od_w_in[None], m_od_w_in, v_od_w_in, "od_w_in", True)
    upd["od_w_out"] = step(od_w_out, grad_od_w_out[None], m_od_w_out, v_od_w_out, "od_w_out", True)
    grad_od_w_in, grad_od_w_out = upd["od_w_in"][3], upd["od_w_out"][3]
    done_behind = [dctx_part] + [upd[k][0] for k in (
        "od_w_in", "od_w_out", "ada_b", "pre_g", "post_g", "ev_pool_scale", "ev_conv_w", "ev_conv_b",
        "od_onorm_g", "lb_logits")]
    ev_recv, ev_landed = copies_wait("reduce_ev_scatter_wait", ev_sc, plan_scatter, done_behind)
    grad_ev_w_in, grad_ev_w_out, grad_pool_w = share_halves(
        [owner_sum(ev_sums[i][0], ev_recv[3 + i], chip_core_arr) for i in range(3)])
    dctx_all = allgather8(dctx_part[0] + dctx_part[1] + ev_landed[0:1, 0:1], "allgather_dctx")
    grad_c_ctx = c_ctx_grad(dctx_all, c_ctx.reshape(1, d)).reshape(d)
    upd["c_ctx"] = step(c_ctx, grad_c_ctx, m_c_ctx, v_c_ctx, "c_ctx")
    upd["ev_w_in"] = step(ev_w_in, grad_ev_w_in[None], m_ev_w_in, v_ev_w_in, "ev_w_in", True)
    upd["ev_pool_w"] = step(ev_pool_w, grad_pool_w.reshape(1, N_POOL, g // 4, g), m_ev_pool_w, v_ev_pool_w,
                            "ev_pool_w", True)
    upd["ev_w_out"] = step(ev_w_out, grad_ev_w_out[None], m_ev_w_out, v_ev_w_out, "ev_w_out", True)
    grad_ev_w_in, grad_ev_pool_w, grad_ev_w_out = upd["ev_w_in"][3], upd["ev_pool_w"][3], upd["ev_w_out"][3]
    names = ["c_ctx", "ada_w", "ada_b", "pre_g", "post_g", "ev_w_in", "ev_pool_w", "ev_pool_scale",
             "ev_conv_w", "ev_conv_b", "ev_w_out", "od_w_in", "od_onorm_g", "od_w_out", "lb_logits"]
    grads = [grad_c_ctx, grad_ada_w, grad_ada_b, grad_pre_g, grad_post_g, grad_ev_w_in, grad_ev_pool_w,
             grad_ev_pool_scale, grad_ev_conv_w, grad_ev_conv_b, grad_ev_w_out, grad_od_w_in,
             grad_od_onorm_g, grad_od_w_out, grad_lb_logits]
    return (loss, grad_x, *grads, *[upd[k][0] for k in names], *[upd[k][1] for k in names],
            *[upd[k][2] for k in names])
```

```python
import jax
import jax.numpy as jnp
from jax import lax
from jax.experimental import pallas as pl
from jax.experimental.pallas import tpu as pltpu

EPS = 1e-6
GRID_W_LOG2 = 6
CHUNK = 64
HEAD = 128
N_POOL = 4
ADAM_LR, ADAM_B1, ADAM_B2, ADAM_EPS, ADAM_WD, ADAM_STEP = 0.001, 0.9, 0.999, 1e-08, 0.01, 10
VMEM_LIMIT = 56 * 1024 * 1024
MESH = pl.DeviceIdType.MESH
F32, BF16 = jnp.float32, jnp.bfloat16
ANY = pl.BlockSpec(memory_space=pl.ANY)
VMEM = pl.BlockSpec(memory_space=pltpu.VMEM)


def _cp(**kw):
    return pltpu.CompilerParams(vmem_limit_bytes=VMEM_LIMIT, **kw)


def _silu(x):
    return x * jax.nn.sigmoid(x)


def _dsilu(x):
    s = jax.nn.sigmoid(x)
    return s * (1.0 + x * (1.0 - s))


def _dot(a, b, dims=((1,), (0,)), precision=None):
    return lax.dot_general(a, b, (dims, ((), ())), preferred_element_type=F32, precision=precision)


NN = ((1,), (0,))
NT = ((1,), (1,))
TN = ((0,), (0,))


def _row_block(cx):
    return 256 if cx % 256 == 0 else 128


def normmod_bwd(xs, dh, g, scale, dres, cx, res_is_latent_only, dx_latent_only=False, prev=None):
    t, d = xs.shape
    tm = _row_block(cx)
    nctx = cx // tm
    n_prev = 0 if prev is None else 3

    def body(x_ref, dh_ref, g_ref, sc_ref, dres_ref, *rest):
        dx_ref, dg_ref, dsh_ref, dsc_ref = rest[n_prev:n_prev + 4]
        i = pl.program_id(0)
        is_ctx = i < nctx

        @pl.when(i == 0)
        def _():
            for ref in rest[n_prev + 1:n_prev + 4] + rest[n_prev + 5:]:
                ref[...] = jnp.zeros_like(ref)

        x = x_ref[...]
        dh = dh_ref[...]
        gv = g_ref[...]
        rstd = lax.rsqrt(jnp.mean(x * x, axis=-1, keepdims=True) + EPS)
        xhat = x * rstd
        sc = jnp.where(is_ctx, sc_ref[0:1, :], sc_ref[1:2, :])
        dsh = jnp.sum(dh, axis=0, keepdims=True)
        dhx = dh * xhat
        dsc = jnp.sum(dhx * gv, axis=0, keepdims=True)
        dg_ref[...] += jnp.sum(dhx * (1.0 + sc), axis=0, keepdims=True)
        zero = jnp.zeros_like(dsh)
        dsh_ref[0:1, :] += jnp.where(is_ctx, dsh, zero)
        dsh_ref[1:2, :] += jnp.where(is_ctx, zero, dsh)
        dsc_ref[0:1, :] += jnp.where(is_ctx, dsc, zero)
        dsc_ref[1:2, :] += jnp.where(is_ctx, zero, dsc)
        dxhat = dh * (gv * (1.0 + sc))
        dx = rstd * (dxhat - xhat * jnp.mean(dxhat * xhat, axis=-1, keepdims=True))
        res = dres_ref[...]
        if res_is_latent_only:
            res = jnp.where(is_ctx, jnp.zeros_like(res), res)
        dxt = dx + res
        dx_ref[...] = dxt
        if prev is not None:
            y_ref, pg_ref, gate_ref = rest[:3]
            dy_ref, dgate_ref, dpg_ref = rest[7:]
            y = y_ref[...]
            pgv = pg_ref[...]
            rstd_y = lax.rsqrt(jnp.mean(y * y, axis=-1, keepdims=True) + EPS)
            yhat = y * rstd_y
            gt = jnp.where(is_ctx, gate_ref[0:1, :], gate_ref[1:2, :])
            dxy = dxt * yhat
            dgt = jnp.sum(dxy * pgv, axis=0, keepdims=True)
            dgate_ref[0:1, :] += jnp.where(is_ctx, dgt, zero)
            dgate_ref[1:2, :] += jnp.where(is_ctx, zero, dgt)
            dpg_ref[...] += jnp.sum(dxy * gt, axis=0, keepdims=True)
            dyhat = dxt * (gt * pgv)
            dy_ref[...] = (rstd_y * (dyhat - yhat * jnp.mean(dyhat * yhat, axis=-1, keepdims=True))).astype(BF16)

    row = pl.BlockSpec((tm, d), lambda i: (i, 0))
    if res_is_latent_only:
        res_spec = pl.BlockSpec((tm, d), lambda i: (jnp.maximum(i - nctx, 0), 0))
    else:
        res_spec = row
    vec = lambda r: pl.BlockSpec((r, d), lambda i: (0, 0))
    dx_spec = pl.BlockSpec((tm, d), lambda i: (jnp.maximum(i - nctx, 0), 0)) if dx_latent_only else row
    in_specs = [row, row, vec(1), vec(2), res_spec]
    out_specs = [dx_spec, vec(1), vec(2), vec(2)]
    out_shape = [jax.ShapeDtypeStruct((t - cx if dx_latent_only else t, d), F32), jax.ShapeDtypeStruct((1, d), F32),
                 jax.ShapeDtypeStruct((2, d), F32), jax.ShapeDtypeStruct((2, d), F32)]
    if prev is not None:
        in_specs += [row, vec(1), vec(2)]
        out_specs += [row, vec(2), vec(1)]
        out_shape += [jax.ShapeDtypeStruct((t, d), BF16), jax.ShapeDtypeStruct((2, d), F32),
                      jax.ShapeDtypeStruct((1, d), F32)]
    return pl.pallas_call(
        body, name="normmod_bwd", grid=(t // tm,), in_specs=in_specs, out_specs=out_specs, out_shape=out_shape,
        compiler_params=_cp(),
    )(xs, dh, g, scale, dres, *(() if prev is None else prev))


def post_fwd_norm(xs, y, pg, gate, g_next, shift_next, scale_next, cx):
    t, d = xs.shape
    tm = _row_block(cx)
    nctx = cx // tm

    def body(x_ref, y_ref, pg_ref, gate_ref, g_ref, sh_ref, sc_ref, o_ref, h_ref):
        is_ctx = pl.program_id(0) < nctx
        pick = lambda ref: jnp.where(is_ctx, ref[0:1, :], ref[1:2, :])
        y = y_ref[...]
        rstd = lax.rsqrt(jnp.mean(y * y, axis=-1, keepdims=True) + EPS)
        x = x_ref[...] + pick(gate_ref) * ((y * rstd) * pg_ref[...])
        o_ref[...] = x
        rstd = lax.rsqrt(jnp.mean(x * x, axis=-1, keepdims=True) + EPS)
        h_ref[...] = ((x * rstd) * g_ref[...] * (1.0 + pick(sc_ref)) + pick(sh_ref)).astype(BF16)

    row = pl.BlockSpec((tm, d), lambda i: (i, 0))
    vec = lambda r: pl.BlockSpec((r, d), lambda i: (0, 0))
    return pl.pallas_call(
        body, name="post_fwd_norm", grid=(t // tm,),
        in_specs=[row, row, vec(1), vec(2), vec(1), vec(2), vec(2)], out_specs=[row, row],
        out_shape=[jax.ShapeDtypeStruct((t, d), F32), jax.ShapeDtypeStruct((t, d), BF16)],
        compiler_params=_cp(),
    )(xs, y, pg, gate, g_next, shift_next, scale_next)


def post_loss(xs, y, pg, gate, target, cx):
    t, d = xs.shape
    n = y.shape[0]
    tm = _row_block(cx)
    nctx = cx // tm

    def body(x_ref, y_ref, pg_ref, gate_ref, tgt_ref, sq_ref, dx_ref, dy_ref, dgate_ref, dpg_ref):
        @pl.when(pl.program_id(0) == 0)
        def _():
            sq_ref[...] = jnp.zeros_like(sq_ref)
            dgate_ref[...] = jnp.zeros_like(dgate_ref)
            dpg_ref[...] = jnp.zeros_like(dpg_ref)

        y = y_ref[...]
        pgv = pg_ref[...]
        gt = gate_ref[1:2, :]
        rstd = lax.rsqrt(jnp.mean(y * y, axis=-1, keepdims=True) + EPS)
        yhat = y * rstd
        err = x_ref[...] + gt * (yhat * pgv) - tgt_ref[...]
        sq_ref[...] += jnp.sum(err * err)
        dx = err * (1.0 / d)
        dx_ref[...] = dx
        dxy = dx * yhat
        dgate_ref[1:2, :] += jnp.sum(dxy * pgv, axis=0, keepdims=True)
        dpg_ref[...] += jnp.sum(dxy * gt, axis=0, keepdims=True)
        dyhat = dx * (gt * pgv)
        dy_ref[...] = (rstd * (dyhat - yhat * jnp.mean(dyhat * yhat, axis=-1, keepdims=True))).astype(BF16)

    row = pl.BlockSpec((tm, d), lambda i: (i, 0))
    xrow = pl.BlockSpec((tm, d), lambda i: (i + nctx, 0))
    vec = lambda r: pl.BlockSpec((r, d), lambda i: (0, 0))
    return pl.pallas_call(
        body, name="post_loss", grid=(n // tm,),
        in_specs=[xrow, row, vec(1), vec(2), row],
        out_specs=[pl.BlockSpec((8, 128), lambda i: (0, 0)), row, row, vec(2), vec(1)],
        out_shape=[jax.ShapeDtypeStruct((8, 128), F32), jax.ShapeDtypeStruct((n, d), F32),
                   jax.ShapeDtypeStruct((n, d), BF16), jax.ShapeDtypeStruct((2, d), F32),
                   jax.ShapeDtypeStruct((1, d), F32)],
        compiler_params=_cp(),
    )(xs, y, pg, gate, target)


def _split_rows(m):
    for cand in (1152, 1024, 768, 512, 384, 256, 128):
        if m % cand == 0 and m // cand >= 2:
            return cand
    return m


def mm_nn(a, w3, sec, tn, name):
    m, k = a.shape
    q, _, ws = w3.shape
    n = q * ws
    tpq, tps = ws // tn, sec // tn
    tm = next(c for c in (768, 512, 256, 128) if m % c == 0)

    def body(a_ref, w_ref, o_ref):
        w = w_ref[...]

        def step(i, carry):
            rows = pl.ds(pl.multiple_of(i * tm, tm), tm)
            o_ref[rows, :] = _dot(a_ref[rows, :], w)
            return carry

        lax.fori_loop(0, m // tm, step, 0)

    return pl.pallas_call(
        body, name=name, grid=(n // tn,),
        in_specs=[pl.BlockSpec((m, k), lambda j: (0, 0)),
                  pl.BlockSpec((None, k, tn), lambda j: (j // tpq, 0, j % tpq))],
        out_specs=pl.BlockSpec((None, m, tn), lambda j: (j // tps, 0, j % tps)),
        out_shape=jax.ShapeDtypeStruct((n // sec, m, sec), F32), compiler_params=_cp(),
    )(a, w3)


def _two_stacks(a3, b3, tn):
    sec = a3.shape[2]
    tps = sec // tn
    n1 = a3.shape[0] * tps
    first = lambda j: (jnp.minimum(j, n1 - 1) // tps, jnp.minimum(j, n1 - 1) % tps)
    second = lambda j: (jnp.maximum(j - n1, 0) // tps, jnp.maximum(j - n1, 0) % tps)
    return n1, first, second


def mm_nt(a3, b3, w3, tn, name):
    if b3 is None:
        b3 = a3
    _, m, sec = a3.shape
    q, k, ws = w3.shape
    n = q * ws
    tpq = ws // tn
    mb = _split_rows(m)
    n1, first, second = _two_stacks(a3, b3, tn)

    def body(a_ref, b_ref, w_ref, o_ref):
        j = pl.program_id(1)

        @pl.when(j == 0)
        def _():
            o_ref[...] = jnp.zeros_like(o_ref)

        @pl.when(j < n1)
        def _():
            o_ref[...] += _dot(a_ref[...], w_ref[...], NT)

        @pl.when(j >= n1)
        def _():
            o_ref[...] += _dot(b_ref[...], w_ref[...], NT)

    return pl.pallas_call(
        body, name=name, grid=(m // mb, n // tn),
        in_specs=[pl.BlockSpec((None, mb, tn), lambda i, j: (first(j)[0], i, first(j)[1])),
                  pl.BlockSpec((None, mb, tn), lambda i, j: (second(j)[0], i, second(j)[1])),
                  pl.BlockSpec((None, k, tn), lambda i, j: (j // tpq, 0, j % tpq))],
        out_specs=pl.BlockSpec((mb, k), lambda i, j: (i, 0)),
        out_shape=jax.ShapeDtypeStruct((m, k), F32), compiler_params=_cp(),
    )(a3, b3, w3)


def mm_tn(a, b3, c3, ws, tn, name):
    m, k = a.shape
    sec = b3.shape[2]
    n = (b3.shape[0] + (0 if c3 is None else c3.shape[0])) * sec
    if c3 is None:
        c3 = b3
    tpq = ws // tn
    kb = 256 if k % 256 == 0 else 128
    n1, first, second = _two_stacks(b3, c3, tn)

    def body(a_ref, b_ref, c_ref, o_ref):
        def product(rhs_ref):
            rhs = rhs_ref[...]
            for i in range(k // kb):
                o_ref[i * kb:(i + 1) * kb, :] = _dot(a_ref[:, i * kb:(i + 1) * kb], rhs, TN).astype(BF16)

        @pl.when(pl.program_id(0) < n1)
        def _():
            product(b_ref)

        @pl.when(pl.program_id(0) >= n1)
        def _():
            product(c_ref)

    return pl.pallas_call(
        body, name=name, grid=(n // tn,),
        in_specs=[pl.BlockSpec((m, k), lambda j: (0, 0)),
                  pl.BlockSpec((None, m, tn), lambda j: (first(j)[0], 0, first(j)[1])),
                  pl.BlockSpec((None, m, tn), lambda j: (second(j)[0], 0, second(j)[1]))],
        out_specs=pl.BlockSpec((None, k, tn), lambda j: (j // tpq, 0, j % tpq)),
        out_shape=jax.ShapeDtypeStruct((n // ws, k, ws), BF16), compiler_params=_cp(),
    )(a, b3, c3)


POOL_REACH = 8 << GRID_W_LOG2


def _token_parts(tok, cx):
    lat = tok - cx
    return tok < cx, lat >> GRID_W_LOG2, lat & ((1 << GRID_W_LOG2) - 1)


def _pool_mask(gi, row0, col0, tm, ncols, cx, transposed):
    half = jnp.left_shift(1, gi)
    r = lax.broadcasted_iota(jnp.int32, (tm, 1), 0) + row0
    c = lax.broadcasted_iota(jnp.int32, (1, ncols), 1) + col0
    out_tok, src_tok = (c, r) if transposed else (r, c)
    o_ctx, o_row, o_col = _token_parts(out_tok, cx)
    s_ctx, s_row, s_col = _token_parts(src_tok, cx)

    def inside(o, s):
        return (s >= o - half) & (s <= o + half - 1)

    ctx_hit = o_ctx & s_ctx & inside(out_tok, src_tok)
    lat_hit = (~o_ctx) & (~s_ctx) & inside(o_row, s_row) & inside(o_col, s_col)
    return jnp.where(ctx_hit | lat_hit, 1.0, 0.0).astype(BF16)


def _pool_inv_count(gi, row0, tm, cx, seq):
    half = jnp.left_shift(1, gi)
    r = lax.broadcasted_iota(jnp.int32, (tm, 1), 0) + row0
    is_ctx, row, col = _token_parts(r, cx)

    def count(pos, size):
        return jnp.minimum(pos + half - 1, size - 1) - jnp.maximum(pos - half, 0) + 1

    cnt = jnp.where(is_ctx, count(r, cx), count(row, seq >> GRID_W_LOG2) * count(col, 1 << GRID_W_LOG2))
    return 1.0 / cnt.astype(F32)


def _lat_band(tm):
    side = POOL_REACH // tm
    return side, 2 * side + 1


def _lat_mask(gi, tm, cx, transposed):
    side, band = _lat_band(tm)
    return _pool_mask(gi, cx + side * tm, cx, tm, band * tm, cx, transposed)


def _store_padded_lat(dst_ref, lat, tm):
    side, _ = _lat_band(tm)
    seq = lat.shape[0]
    zeros = jnp.zeros((side * tm, lat.shape[1]), dst_ref.dtype)
    dst_ref[0:side * tm, :] = zeros
    dst_ref[side * tm + seq:, :] = zeros
    dst_ref[side * tm:side * tm + seq, :] = lat.astype(dst_ref.dtype)


def mix_a_fwd(z0, pool_w, pool_scale, cx):
    _, t, half_d = z0.shape
    g = half_d // N_POOL
    seq = t - cx
    tm = _row_block(cx)
    side, band = _lat_band(tm)

    def body(v_ref, ag_ref, w_ref, sc_ref, u_ref, vlat_ref, mask_ref):
        gi = pl.program_id(0)
        w = w_ref[...].astype(BF16)
        sc = sc_ref[...]
        _store_padded_lat(vlat_ref, v_ref[cx:, :], tm)
        mask_ref[...] = _lat_mask(gi, tm, cx, False)

        def finish(row0, window_sum):
            rows = pl.ds(row0, tm)
            pooled = window_sum * _pool_inv_count(gi, row0, tm, cx, seq) - v_ref[rows, :]
            mixed = _dot(pooled.astype(BF16), w) * sc
            u_ref[rows, :] = (mixed * _silu(ag_ref[rows, :])).astype(BF16)

        vctx = v_ref[0:cx, :].astype(BF16)
        for i in range(cx // tm):
            finish(i * tm, _dot(_pool_mask(gi, i * tm, 0, tm, cx, cx, False), vctx))

        def step(j, carry):
            src = vlat_ref[pl.ds(pl.multiple_of(j * tm, tm), band * tm), :]
            finish(pl.multiple_of(cx + j * tm, tm), _dot(mask_ref[...], src))
            return carry

        lax.fori_loop(0, seq // tm, step, 0)

    sec = lambda s: pl.BlockSpec((None, t, g), lambda j: (s, 0, j))
    return pl.pallas_call(
        body, name="mix_a_fwd", grid=(N_POOL,),
        in_specs=[sec(0), sec(1), pl.BlockSpec((None, g, g), lambda j: (j, 0, 0)),
                  pl.BlockSpec((1, g), lambda j: (0, j))],
        out_specs=pl.BlockSpec((t, g), lambda j: (0, j)),
        out_shape=jax.ShapeDtypeStruct((t, 2 * half_d), BF16),
        scratch_shapes=[pltpu.VMEM((seq + 2 * side * tm, g), BF16), pltpu.VMEM((tm, band * tm), BF16)],
        compiler_params=_cp(),
    )(z0, z0, pool_w, pool_scale)


def mix_a_bwd(z0, du, pool_w, pool_scale, cx):
    _, t, half_d = z0.shape
    g = half_d // N_POOL
    seq = t - cx
    tm = _row_block(cx)
    gq = g // 4
    side, band = _lat_band(tm)

    def body(v_ref, ag_ref, du_ref, w_ref, sc_ref, dz_ref, dw_ref, dsc_ref,
             vlat_ref, mask_ref, pooled_ref, dmx_ref, dpl_ref, wlat_ref, wctx_ref):
        gi = pl.program_id(0)
        w = w_ref[...].astype(BF16)
        sc = sc_ref[...]
        _store_padded_lat(vlat_ref, v_ref[cx:, :], tm)
        _store_padded_lat(wlat_ref, jnp.zeros((seq, g), BF16), tm)
        mask_ref[...] = _lat_mask(gi, tm, cx, False)

        def first(row0, window_sum, weighted_ref, weighted_row0):
            rows = pl.ds(row0, tm)
            inv = _pool_inv_count(gi, row0, tm, cx, seq)
            pooled = (window_sum * inv - v_ref[rows, :]).astype(BF16)
            pooled_ref[rows, :] = pooled
            mixed = _dot(pooled, w)
            ag = ag_ref[rows, :]
            duv = du_ref[rows, :]
            dz_ref[1, rows, :] = (duv * (mixed * sc) * _dsilu(ag)).astype(BF16)
            dms = duv * _silu(ag)
            dmixed = (dms * sc).astype(BF16)
            dmx_ref[rows, :] = dmixed
            dpooled = _dot(dmixed, w, NT)
            dpl_ref[rows, :] = dpooled
            weighted_ref[pl.ds(weighted_row0, tm), :] = (dpooled * inv).astype(BF16)
            return jnp.sum(dms * mixed, axis=0, keepdims=True)

        dsc = jnp.zeros((1, g), F32)
        vctx = v_ref[0:cx, :].astype(BF16)
        for i in range(cx // tm):
            dsc += first(i * tm, _dot(_pool_mask(gi, i * tm, 0, tm, cx, cx, False), vctx), wctx_ref, i * tm)

        def first_lat(j, acc):
            src = vlat_ref[pl.ds(pl.multiple_of(j * tm, tm), band * tm), :]
            return acc + first(pl.multiple_of(cx + j * tm, tm), _dot(mask_ref[...], src),
                               wlat_ref, pl.multiple_of((side + j) * tm, tm))

        dsc_ref[...] = lax.fori_loop(0, seq // tm, first_lat, dsc)
        dw = _dot(pooled_ref[...], dmx_ref[...], TN)
        for qi in range(4):
            dw_ref[qi] = dw[qi * gq:(qi + 1) * gq, :]

        wctx = wctx_ref[...]
        for i in range(cx // tm):
            rows = pl.ds(i * tm, tm)
            dz_ref[0, rows, :] = (_dot(_pool_mask(gi, i * tm, 0, tm, cx, cx, True), wctx)
                                  - dpl_ref[rows, :]).astype(BF16)
        mask_ref[...] = _lat_mask(gi, tm, cx, True)

        def second_lat(j, carry):
            rows = pl.ds(pl.multiple_of(cx + j * tm, tm), tm)
            src = wlat_ref[pl.ds(pl.multiple_of(j * tm, tm), band * tm), :]
            dz_ref[0, rows, :] = (_dot(mask_ref[...], src) - dpl_ref[rows, :]).astype(BF16)
            return carry

        lax.fori_loop(0, seq // tm, second_lat, 0)

    sec = lambda s: pl.BlockSpec((None, t, g), lambda j: (s, 0, j))
    padded = pltpu.VMEM((seq + 2 * side * tm, g), BF16)
    return pl.pallas_call(
        body, name="mix_a_bwd", grid=(N_POOL,),
        in_specs=[sec(0), sec(1), pl.BlockSpec((t, g), lambda j: (0, j)),
                  pl.BlockSpec((None, g, g), lambda j: (j, 0, 0)),
                  pl.BlockSpec((1, g), lambda j: (0, j))],
        out_specs=[pl.BlockSpec((2, t, g), lambda j: (0, 0, j)),
                   pl.BlockSpec((4, None, gq, g), lambda j: (0, j, 0, 0)),
                   pl.BlockSpec((1, g), lambda j: (0, j))],
        out_shape=[jax.ShapeDtypeStruct((2, t, half_d), BF16),
                   jax.ShapeDtypeStruct((4, N_POOL, gq, g), F32),
                   jax.ShapeDtypeStruct((1, half_d), F32)],
        scratch_shapes=[padded, pltpu.VMEM((tm, band * tm), BF16), pltpu.VMEM((t, g), BF16),
                        pltpu.VMEM((t, g), BF16), pltpu.VMEM((t, g), F32), padded, pltpu.VMEM((cx, g), BF16)],
        compiler_params=_cp(),
    )(z0, z0, du, pool_w, pool_scale)


def _conv_masks(t, cx):
    r = lax.broadcasted_iota(jnp.int32, (t, 1), 0)
    has_prev = jnp.where((r == 0) | (r == cx), 0.0, 1.0)
    has_next = jnp.where((r == cx - 1) | (r == t - 1), 0.0, 1.0)
    return has_prev, has_next


def mix_b_fwd(z0, conv_w, conv_b, u, cx):
    _, t, half_d = z0.shape
    gb = 128
    off = half_d // gb

    def body(bx_ref, bb_ref, bc_ref, bg_ref, w_ref, b_ref, _, u_ref):
        has_prev, has_next = _conv_masks(t, cx)
        tt = bc_ref[...] * bx_ref[...]
        prev = pltpu.roll(tt, 1, 0) * has_prev
        nxt = pltpu.roll(tt, t - 1, 0) * has_next
        cv = prev * w_ref[0:1, :] + tt * w_ref[1:2, :] + nxt * w_ref[2:3, :] + b_ref[...]
        u_ref[...] = (bb_ref[...] * cv * _silu(bg_ref[...])).astype(BF16)

    sec = lambda s: pl.BlockSpec((None, t, gb), lambda j: (s, 0, j))
    return pl.pallas_call(
        body, name="mix_b_fwd", grid=(half_d // gb,),
        in_specs=[sec(2), sec(3), sec(4), sec(5), pl.BlockSpec((3, gb), lambda j: (0, j)),
                  pl.BlockSpec((1, gb), lambda j: (0, j)), ANY],
        out_specs=pl.BlockSpec((t, gb), lambda j: (0, j + off)),
        out_shape=jax.ShapeDtypeStruct((t, 2 * half_d), BF16), input_output_aliases={6: 0},
        compiler_params=_cp(),
    )(z0, z0, z0, z0, conv_w, conv_b, u)


def mix_b_bwd(z0, du, conv_w, conv_b, cx):
    _, t, half_d = z0.shape
    gb = 128
    off = half_d // gb

    def body(bx_ref, bb_ref, bc_ref, bg_ref, du_ref, w_ref, b_ref, dz_ref, dw_ref, db_ref):
        has_prev, has_next = _conv_masks(t, cx)
        bx, bb, bc, bg = bx_ref[...], bb_ref[...], bc_ref[...], bg_ref[...]
        duv = du_ref[...]
        tt = bc * bx
        prev = pltpu.roll(tt, 1, 0) * has_prev
        nxt = pltpu.roll(tt, t - 1, 0) * has_next
        w0, w1, w2 = w_ref[0:1, :], w_ref[1:2, :], w_ref[2:3, :]
        cv = prev * w0 + tt * w1 + nxt * w2 + b_ref[...]
        sg = _silu(bg)
        dz_ref[1] = (duv * cv * sg).astype(BF16)
        dz_ref[3] = (duv * bb * cv * _dsilu(bg)).astype(BF16)
        dcv = duv * bb * sg
        dw_ref[0:1, :] = jnp.sum(dcv * prev, axis=0, keepdims=True)
        dw_ref[1:2, :] = jnp.sum(dcv * tt, axis=0, keepdims=True)
        dw_ref[2:3, :] = jnp.sum(dcv * nxt, axis=0, keepdims=True)
        db_ref[...] = jnp.sum(dcv, axis=0, keepdims=True)
        dt = (pltpu.roll(dcv * has_prev, t - 1, 0) * w0 + dcv * w1
              + pltpu.roll(dcv * has_next, 1, 0) * w2)
        dz_ref[0] = (dt * bc).astype(BF16)
        dz_ref[2] = (dt * bx).astype(BF16)

    sec = lambda s: pl.BlockSpec((None, t, gb), lambda j: (s, 0, j))
    return pl.pallas_call(
        body, name="mix_b_bwd", grid=(half_d // gb,),
        in_specs=[sec(2), sec(3), sec(4), sec(5), pl.BlockSpec((t, gb), lambda j: (0, j + off)),
                  pl.BlockSpec((3, gb), lambda j: (0, j)), pl.BlockSpec((1, gb), lambda j: (0, j))],
        out_specs=[pl.BlockSpec((4, t, gb), lambda j: (0, 0, j)),
                   pl.BlockSpec((3, gb), lambda j: (0, j)), pl.BlockSpec((1, gb), lambda j: (0, j))],
        out_shape=[jax.ShapeDtypeStruct((4, t, half_d), BF16),
                   jax.ShapeDtypeStruct((3, half_d), F32), jax.ShapeDtypeStruct((1, half_d), F32)],
        compiler_params=_cp(),
    )(z0, z0, z0, z0, du, conv_w, conv_b)


def _lower_bound(lbl_ref, d):
    l0, l1, l2 = lbl_ref[d, 0:1, :], lbl_ref[d, 1:2, :], lbl_ref[d, 2:3, :]
    mx = jnp.maximum(jnp.maximum(l0, l1), l2)
    e0, e1, e2 = jnp.exp(l0 - mx), jnp.exp(l1 - mx), jnp.exp(l2 - mx)
    inv = 1.0 / (e0 + e1 + e2)
    return (e0 + e1) * inv, (e0 * inv, e1 * inv, e2 * inv)


def _chunk_consts(d):
    r = lax.broadcasted_iota(jnp.int32, (CHUNK, CHUNK), 0)
    c = lax.broadcasted_iota(jnp.int32, (CHUNK, CHUNK), 1)
    keep = (c <= r) if d == 0 else (c >= r)
    return jnp.where(keep, 1.0, 0.0).astype(F32), keep


def _chunk_of_step(s, d, nc, ncc):
    if d == 0:
        return s
    return jnp.where(s < ncc, ncc - 1 - s, nc - 1 + ncc - s)


def _gates(z, lbv):
    e = jnp.exp(-jnp.abs(z))
    r = 1.0 / (1.0 + e)
    er = e * r
    pos = z >= 0.0
    sig = jnp.where(pos, r, er)
    nsig = jnp.where(pos, er, r)
    return sig, nsig, lbv + (1.0 - lbv) * sig


def _split3(x):
    hi = x.astype(BF16)
    r1 = x - hi.astype(F32)
    mid = r1.astype(BF16)
    lo = (r1 - mid.astype(F32)).astype(BF16)
    return jnp.concatenate([hi, mid, lo], axis=1)


def _cumsum_chunk(cum, x):
    y = _dot(cum, _split3(x))
    return y[:, :HEAD] + y[:, HEAD:2 * HEAD] + y[:, 2 * HEAD:]


def _chunk_rows(n):
    return pl.ds(pl.multiple_of(n * CHUNK, CHUNK), CHUNK)


def _group(nc, prefer=(4, 3, 2, 1)):
    return next(u for u in prefer if nc % u == 0)


WIDE_GROUP = (18, 12, 6, 4, 3, 2, 1)


def _decay_pass(lf_ref, bc_ref, dec_ref, cum, nc):
    grp = _group(nc, WIDE_GROUP)

    def step(m, carry):
        ns = [m * grp + u for u in range(grp)]
        lfc = [lf_ref[_chunk_rows(n), :] for n in ns]
        bc = [_cumsum_chunk(cum, x) for x in lfc]
        for u, n in enumerate(ns):
            bc_ref[_chunk_rows(n), :] = bc[u]
            dec_ref[n] = jnp.broadcast_to(jnp.exp(jnp.sum(lfc[u], axis=0, keepdims=True)), (8, HEAD))
        return carry

    lax.fori_loop(0, nc // grp, step, 0)


def hgrn_fwd(z1, lbl, onorm, cx):
    _, t, d = z1.shape
    seq = t - cx
    nc, ncc = t // CHUNK, cx // CHUNK

    grp, sgrp = _group(nc, WIDE_GROUP), _group(nc, WIDE_GROUP)

    def body(zf_ref, zb_ref, v_ref, q_ref, g_ref, lbl_ref, on_ref, o_ref, r_ref, bcs_ref, ks_ref, decs_ref,
             lf_ref, k_ref, bc_ref, dec_ref, qd_ref, ki_ref, oacc_ref, ds_ref):
        for dr, z_ref in ((0, zf_ref), (1, zb_ref)):
            lbv, _ = _lower_bound(lbl_ref, dr)
            _, nsig, f = _gates(z_ref[...], lbv)
            lf_ref[...] = jnp.log(f)
            k_ref[...] = (1.0 - lbv) * nsig
            cum, keep = _chunk_consts(dr)
            _decay_pass(lf_ref, bc_ref, dec_ref, cum.astype(BF16), nc)
            bc = bc_ref[...]
            bcs_ref[dr] = bc
            ks_ref[dr] = k_ref[...]
            decs_ref[dr] = dec_ref[...]
            qd_ref[...] = (q_ref[...] * jnp.exp(bc)).astype(BF16)
            ki_ref[...] = (k_ref[...] * jnp.exp(-bc)).astype(BF16)

            def local_step(m, carry, dr=dr, keep=keep):
                ns = [m * grp + u for u in range(grp)]
                rows = [_chunk_rows(n) for n in ns]
                qd = [qd_ref[r, :] for r in rows]
                ki = [ki_ref[r, :] for r in rows]
                vc = [v_ref[r, :].astype(BF16) for r in rows]
                sc = [_dot(qd[u], ki[u], NT) for u in range(grp)]
                inc = [_dot(vc[u], ki[u], TN) for u in range(grp)]
                a = [jnp.where(keep, s, 0.0).astype(BF16) for s in sc]
                intra = [_dot(a[u], vc[u]) for u in range(grp)]
                for u in range(grp):
                    ds_ref[ns[u]] = inc[u] * dec_ref[ns[u]][0:1, :]
                    if dr == 0:
                        oacc_ref[rows[u], :] = intra[u]
                    else:
                        oacc_ref[rows[u], :] += intra[u]
                return carry

            lax.fori_loop(0, nc // grp, local_step, 0)

            def state_step(m, st, dr=dr):
                ns = [_chunk_of_step(m * sgrp + u, dr, nc, ncc) for u in range(sgrp)]
                rows = [_chunk_rows(n) for n in ns]
                sts = []
                for n in ns:
                    sts.append(st.astype(BF16))
                    st = st * dec_ref[n][0:1, :] + ds_ref[n]
                inter = [_dot(qd_ref[rows[u], :], sts[u], NT) for u in range(sgrp)]
                for u in range(sgrp):
                    oacc_ref[rows[u], :] += inter[u]
                return st

            lax.fori_loop(0, nc // sgrp, state_step, jnp.zeros((HEAD, HEAD), F32))

        o = oacc_ref[cx:, :]
        o_ref[...] = o
        rstd = lax.rsqrt(jnp.mean(o * o, axis=-1, keepdims=True) + EPS)
        r_ref[...] = (o * rstd * on_ref[...] * _silu(g_ref[cx:, :])).astype(BF16)

    sec = lambda s: pl.BlockSpec((None, t, HEAD), lambda h: (s, 0, h))
    col = pl.BlockSpec((seq, HEAD), lambda h: (0, h))
    tf32, tb16 = pltpu.VMEM((t, HEAD), F32), pltpu.VMEM((t, HEAD), BF16)
    return pl.pallas_call(
        body, name="hgrn_fwd", grid=(d // HEAD,),
        in_specs=[sec(0), sec(1), sec(2), sec(3), sec(4),
                  pl.BlockSpec((2, 3, HEAD), lambda h: (0, 0, h)), pl.BlockSpec((1, HEAD), lambda h: (0, h))],
        out_specs=[col, col, pl.BlockSpec((2, t, HEAD), lambda h: (0, 0, h)),
                   pl.BlockSpec((2, t, HEAD), lambda h: (0, 0, h)),
                   pl.BlockSpec((2, nc, 8, HEAD), lambda h: (0, 0, 0, h))],
        out_shape=[jax.ShapeDtypeStruct((seq, d), F32), jax.ShapeDtypeStruct((seq, d), BF16),
                   jax.ShapeDtypeStruct((2, t, d), F32), jax.ShapeDtypeStruct((2, t, d), F32),
                   jax.ShapeDtypeStruct((2, nc, 8, d), F32)],
        scratch_shapes=[tf32, tf32, tf32, pltpu.VMEM((nc, 8, HEAD), F32), tb16, tb16, tf32,
                        pltpu.VMEM((nc, HEAD, HEAD), F32)],
        compiler_params=_cp(),
    )(z1, z1, z1, z1, z1, lbl, onorm)


def hgrn_bwd(z1, lbl, onorm, o, dr_out, bcs, ks, decs, cx):
    _, t, d = z1.shape
    seq = t - cx
    nc, ncc = t // CHUNK, cx // CHUNK

    grp2, grp = _group(nc, (12, 9, 6, 4, 3, 2, 1)), _group(nc, (18, 12, 9, 6, 4, 3, 2, 1))

    def body(zf_ref, zb_ref, v_ref, q_ref, g_ref, lbl_ref, on_ref, o_ref, dr_ref, bcs_ref, ks_ref, decs_ref,
             dz_ref, don_ref, dlb_ref,
             qd_ref, ki_ref, do_ref, dqd_ref, dki_ref, dq_ref, dv_ref, ds_ref, dsl_ref):
        o = o_ref[...]
        g = g_ref[cx:, :]
        drv = dr_ref[...]
        onv = on_ref[...]
        rstd = lax.rsqrt(jnp.mean(o * o, axis=-1, keepdims=True) + EPS)
        ohat = o * rstd
        sg = _silu(g)
        don_ref[...] = jnp.sum(drv * ohat * sg, axis=0, keepdims=True)
        dz_ref[4, :cx, :] = jnp.zeros((cx, HEAD), BF16)
        dz_ref[4, cx:, :] = (drv * ohat * onv * _dsilu(g)).astype(BF16)
        dohat = drv * onv * sg
        do_ref[:cx, :] = jnp.zeros((cx, HEAD), BF16)
        do_ref[cx:, :] = (rstd * (dohat - ohat * jnp.mean(dohat * ohat, axis=-1, keepdims=True))).astype(BF16)

        for dr, z_ref in ((0, zf_ref), (1, zb_ref)):
            lbv, _ = _lower_bound(lbl_ref, dr)
            k_ref, bc_ref, dec_ref = ks_ref.at[dr], bcs_ref.at[dr], decs_ref.at[dr]
            _, keep = _chunk_consts(dr)
            cum_t = _chunk_consts(1 - dr)[0].astype(BF16)
            bc = bc_ref[...]
            qd_ref[...] = (q_ref[...] * jnp.exp(bc)).astype(BF16)
            ki_ref[...] = (k_ref[...] * jnp.exp(-bc)).astype(BF16)

            def local_step(m, carry, dr=dr, keep=keep):
                ns = [m * grp + u for u in range(grp)]
                rows = [_chunk_rows(n) for n in ns]
                rng = range(grp)
                qd = [qd_ref[r, :] for r in rows]
                ki = [ki_ref[r, :] for r in rows]
                doc = [do_ref[r, :] for r in rows]
                vc = [v_ref[r, :].astype(BF16) for r in rows]
                sc = [_dot(qd[u], ki[u], NT) for u in rng]
                dsc = [_dot(doc[u], vc[u], NT) for u in rng]
                inc = [_dot(vc[u], ki[u], TN) for u in rng]
                dinc = [_dot(doc[u], qd[u], TN) for u in rng]
                a = [jnp.where(keep, s, 0.0).astype(BF16) for s in sc]
                da = [jnp.where(keep, s, 0.0).astype(BF16) for s in dsc]
                dqd = [_dot(da[u], ki[u]) for u in rng]
                dki = [_dot(da[u], qd[u], TN) for u in rng]
                dv = [_dot(a[u], doc[u], TN) for u in rng]
                for u in rng:
                    ds_ref[ns[u]] = inc[u] * dec_ref[ns[u]][0:1, :]
                    dsl_ref[ns[u]] = dinc[u]
                    dqd_ref[rows[u], :] = dqd[u]
                    dki_ref[rows[u], :] = dki[u]
                    if dr == 0:
                        dv_ref[rows[u], :] = dv[u]
                    else:
                        dv_ref[rows[u], :] += dv[u]
                return carry

            lax.fori_loop(0, nc // grp, local_step, 0)

            def state_step(s, st, dr=dr):
                n = _chunk_of_step(s, dr, nc, ncc)
                inc = ds_ref[n]
                ds_ref[n] = st
                return st * dec_ref[n][0:1, :] + inc

            lax.fori_loop(0, nc, state_step, jnp.zeros((HEAD, HEAD), F32), unroll=4)

            def dstate_step(s, dst, dr=dr):
                n = _chunk_of_step(nc - 1 - s, dr, nc, ncc)
                inc = dsl_ref[n]
                dsl_ref[n] = dst
                return inc + dst * dec_ref[n][0:1, :]

            lax.fori_loop(0, nc, dstate_step, jnp.zeros((HEAD, HEAD), F32), unroll=4)

            def grad_step(m, carry, dr=dr, cum_t=cum_t):
                ns = [m * grp2 + u for u in range(grp2)]
                rows = [_chunk_rows(n) for n in ns]
                rng = range(grp2)
                st0 = [ds_ref[n] for n in ns]
                dst = [dsl_ref[n] for n in ns]
                dstb = [x.astype(BF16) for x in dst]
                dec = [dec_ref[n][0:1, :] for n in ns]
                doc = [do_ref[r, :] for r in rows]
                vc = [v_ref[r, :].astype(BF16) for r in rows]
                e = [jnp.exp(bc_ref[r, :]) for r in rows]
                einv = [jnp.exp(-bc_ref[r, :]) for r in rows]
                qd = [q_ref[rows[u], :] * e[u] for u in rng]
                ki = [k_ref[rows[u], :] * einv[u] for u in rng]
                kd = [ki[u] * dec[u] for u in rng]
                dqd_st = [_dot(doc[u], st0[u].astype(BF16)) for u in rng]
                dkd = [_dot(vc[u], dstb[u]) for u in rng]
                dv_st = [_dot(kd[u].astype(BF16), dstb[u], NT) for u in rng]
                dqd = [dqd_ref[rows[u], :] + dqd_st[u] for u in rng]
                dki = [dki_ref[r, :] for r in rows]
                dbc = [dqd[u] * qd[u] - dki[u] * ki[u] - dkd[u] * kd[u] for u in rng]
                cs = [_cumsum_chunk(cum_t, x) for x in dbc]
                for u in rng:
                    ddec = jnp.sum(dst[u] * st0[u], axis=0, keepdims=True)
                    dbl = jnp.sum(dkd[u] * kd[u], axis=0, keepdims=True) + ddec * dec[u]
                    dv_ref[rows[u], :] += dv_st[u]
                    dqd_ref[rows[u], :] = cs[u] + dbl
                    dki_ref[rows[u], :] = dki[u] * einv[u] + dkd[u] * (einv[u] * dec[u])
                    if dr == 0:
                        dq_ref[rows[u], :] = dqd[u] * e[u]
                    else:
                        dq_ref[rows[u], :] += dqd[u] * e[u]
                return carry

            lax.fori_loop(0, nc // grp2, grad_step, 0)

            sig, nsig, f = _gates(z_ref[...], lbv)
            common = (dqd_ref[...] / f - dki_ref[...]) * nsig
            dz_ref[dr] = (common * ((1.0 - lbv) * sig)).astype(BF16)
            dlb_ref[dr:dr + 1, :] = jnp.sum(common, axis=0, keepdims=True)

        dz_ref[2] = dv_ref[...].astype(BF16)
        dz_ref[3] = dq_ref[...].astype(BF16)

    sec = lambda s: pl.BlockSpec((None, t, HEAD), lambda h: (s, 0, h))
    col = pl.BlockSpec((seq, HEAD), lambda h: (0, h))
    tf32, tb16 = pltpu.VMEM((t, HEAD), F32), pltpu.VMEM((t, HEAD), BF16)
    states = pltpu.VMEM((nc, HEAD, HEAD), F32)
    return pl.pallas_call(
        body, name="hgrn_bwd", grid=(d // HEAD,),
        in_specs=[sec(0), sec(1), sec(2), sec(3), sec(4),
                  pl.BlockSpec((2, 3, HEAD), lambda h: (0, 0, h)), pl.BlockSpec((1, HEAD), lambda h: (0, h)),
                  col, col, pl.BlockSpec((2, t, HEAD), lambda h: (0, 0, h)),
                  pl.BlockSpec((2, t, HEAD), lambda h: (0, 0, h)),
                  pl.BlockSpec((2, nc, 8, HEAD), lambda h: (0, 0, 0, h))],
        out_specs=[pl.BlockSpec((5, t, HEAD), lambda h: (0, 0, h)),
                   pl.BlockSpec((1, HEAD), lambda h: (0, h)), pl.BlockSpec((2, HEAD), lambda h: (0, h))],
        out_shape=[jax.ShapeDtypeStruct((5, t, d), BF16), jax.ShapeDtypeStruct((1, d), F32),
                   jax.ShapeDtypeStruct((2, d), F32)],
        scratch_shapes=[tb16, tb16, tb16, tf32, tf32, tf32, tf32, states, states],
        compiler_params=_cp(),
    )(z1, z1, z1, z1, z1, lbl, onorm, o, dr_out, bcs, ks, decs)


def _place():
    x, y, c = lax.axis_index("x"), lax.axis_index("y"), lax.axis_index("c")
    chips = [(1 - x, y), (x, 1 - y), (1 - x, 1 - y)]
    return x, y, c, chips


def _relay_chips():
    x, y, c, _ = _place()
    first = c == 0
    near = (jnp.where(first, 1 - x, x), jnp.where(first, y, 1 - y))
    far = (jnp.where(first, x, 1 - x), jnp.where(first, 1 - y, y))
    return near, far, (1 - x, 1 - y)


def _cast_rows(src_ref, dst_ref, bufs):
    fbuf, bbuf, load_sems, store_sems = bufs
    tr = fbuf.shape[1]
    nblk = src_ref.shape[0] // tr

    def load(i, s):
        return pltpu.make_async_copy(src_ref.at[pl.ds(i * tr, tr)], fbuf.at[s], load_sems.at[s])

    def store(i, s):
        return pltpu.make_async_copy(bbuf.at[s], dst_ref.at[pl.ds(i * tr, tr)], store_sems.at[s])

    load(0, 0).start()

    def step(i, carry):
        s = i % 2
        load(i, s).wait()

        @pl.when(i + 1 < nblk)
        def _():
            load(i + 1, 1 - s).start()

        @pl.when(i >= 2)
        def _():
            store(i - 2, s).wait()

        bbuf[s] = fbuf[s].astype(BF16)
        store(i, s).start()
        return carry

    lax.fori_loop(0, nblk, step, 0)
    for i in range(max(nblk - 2, 0), nblk):
        store(i, i % 2).wait()


def _norm_rows(src_ref, xs_ref, h_ref, row0, g, shift, scale, bufs):
    fbuf, bbuf, load_sems, xs_sems, h_sems = bufs
    tm = fbuf.shape[1]
    nblk = src_ref.shape[0] // tm

    def load(i, s):
        return pltpu.make_async_copy(src_ref.at[pl.ds(i * tm, tm)], fbuf.at[s], load_sems.at[s])

    def put_x(i, s):
        return pltpu.make_async_copy(fbuf.at[s], xs_ref.at[pl.ds(row0 + i * tm, tm)], xs_sems.at[s])

    def put_h(i, s):
        return pltpu.make_async_copy(bbuf.at[s], h_ref.at[pl.ds(row0 + i * tm, tm)], h_sems.at[s])

    load(0, 0).start()

    def step(i, carry):
        s = i % 2
        load(i, s).wait()

        @pl.when(i >= 1)
        def _():
            put_x(i - 1, 1 - s).wait()

        @pl.when(i + 1 < nblk)
        def _():
            load(i + 1, 1 - s).start()

        @pl.when(i >= 2)
        def _():
            put_h(i - 2, s).wait()

        x = fbuf[s]
        rstd = lax.rsqrt(jnp.mean(x * x, axis=-1, keepdims=True) + EPS)
        bbuf[s] = ((x * rstd) * g * (1.0 + scale) + shift).astype(BF16)
        put_x(i, s).start()
        put_h(i, s).start()
        return carry

    lax.fori_loop(0, nblk, step, 0)
    put_x(nblk - 1, (nblk - 1) % 2).wait()
    for i in range(max(nblk - 2, 0), nblk):
        put_h(i, i % 2).wait()


def allgather_shards(bufs, after, casts, norm):
    n, m = len(bufs), len(casts)
    cast_rows = [256 if a.shape[0] % 256 == 0 else a.shape[0] for a in casts]
    ctx, x_lat = norm[:2]
    cx, d = ctx.shape
    t = cx + x_lat.shape[0]
    tm = _row_block(cx)

    def body(*refs):
        refs = list(refs)
        take = lambda k: [refs.pop(0) for _ in range(k)]
        take(n + 1)
        cast_src = take(m)
        ctx_ref, x_ref, g_ref, sh_ref, sc_ref = take(5)
        outs = take(n)
        done_ref, = take(1)
        cast_dst = take(m)
        h_ref, xs_ref = take(2)
        send_sems, recv_sems = take(2)
        cast_bufs = take(4 * m)
        norm_bufs = take(5)
        done_ref[...] = jnp.zeros((8, 128), F32)
        x, y, c, _ = _place()
        me = (x, y, c)
        p = 2 * x + y
        near, far, diag = _relay_chips()
        half = [pl.ds(c * (s.shape[1] // 2), s.shape[1] // 2) for s in bufs]
        other = [pl.ds((1 - c) * (s.shape[1] // 2), s.shape[1] // 2) for s in bufs]
        slot = lambda chip: 2 * chip[0] + chip[1]

        def remote(i, k, ref, to):
            return pltpu.make_async_remote_copy(src_ref=ref, dst_ref=ref, send_sem=send_sems.at[6 * i + k],
                                                recv_sem=recv_sems.at[6 * i + k], device_id=to, device_id_type=MESH)

        sends = []

        def send(i, k, ref, to):
            cp = remote(i, k, ref, to)
            cp.start()
            sends.append(cp)

        for i in range(n):
            mine = outs[i].at[p, half[i]]
            send(i, 0, mine, (*near, c))
            send(i, 1, mine, (*far, c))
        for j in range(m):
            _cast_rows(cast_src[j], cast_dst[j].at[p], cast_bufs[4 * j:4 * j + 4])
        gv = g_ref[...]
        _norm_rows(ctx_ref, xs_ref, h_ref, 0, gv, sh_ref[0:1, :], sc_ref[0:1, :], norm_bufs)
        _norm_rows(x_ref, xs_ref, h_ref, cx, gv, sh_ref[1:2, :], sc_ref[1:2, :], norm_bufs)
        for i in range(n):
            landed = outs[i].at[slot(near), half[i]]
            remote(i, 0, landed, me).wait_recv()
            send(i, 2, landed, (*far, c))
            send(i, 3, landed, (x, y, 1 - c))
        for i in range(n):
            landed = outs[i].at[slot(far), half[i]]
            remote(i, 1, landed, me).wait_recv()
            send(i, 4, landed, (x, y, 1 - c))
        for i in range(n):
            landed = outs[i].at[slot(diag), half[i]]
            remote(i, 2, landed, me).wait_recv()
            send(i, 5, landed, (x, y, 1 - c))
        for i in range(n):
            for k, chip in ((3, far), (4, near), (5, diag)):
                remote(i, k, outs[i].at[slot(chip), other[i]], me).wait_recv()
        for cp in sends:
            cp.wait_send()

    return pl.pallas_call(
        body, name="allgather_shards",
        in_specs=[ANY] * (n + 1 + m + 2) + [VMEM] * 3, out_specs=[ANY] * n + [VMEM] + [ANY] * (m + 2),
        out_shape=[jax.ShapeDtypeStruct(s.shape, s.dtype) for s in bufs] + [jax.ShapeDtypeStruct((8, 128), F32)]
        + [jax.ShapeDtypeStruct((4,) + a.shape, BF16) for a in casts]
        + [jax.ShapeDtypeStruct((t, d), BF16), jax.ShapeDtypeStruct((t, d), F32)],
        input_output_aliases={i: i for i in range(n)},
        scratch_shapes=[pltpu.SemaphoreType.DMA((6 * n,)), pltpu.SemaphoreType.DMA((6 * n,))] + [
            s for a, tr in zip(casts, cast_rows) for s in (
                pltpu.VMEM((2, tr, a.shape[1]), F32), pltpu.VMEM((2, tr, a.shape[1]), BF16),
                pltpu.SemaphoreType.DMA((2,)), pltpu.SemaphoreType.DMA((2,)))] + [
            pltpu.VMEM((2, tm, d), F32), pltpu.VMEM((2, tm, d), BF16), pltpu.SemaphoreType.DMA((2,)),
            pltpu.SemaphoreType.DMA((2,)), pltpu.SemaphoreType.DMA((2,))],
        compiler_params=_cp(has_side_effects=True),
    )(*bufs, after, *casts, *norm)


def pair_sum(grad, got, chip_core):
    _, r, cc = grad.shape
    hr = r // 2
    tr = 256 if hr % 256 == 0 else hr
    nb = hr // tr

    def body(cc_ref, a_ref, b_ref, own_ref, sb_ref):
        s = a_ref[...].astype(F32) + b_ref[...].astype(F32)
        sb_ref[...] = s.astype(BF16)

        @pl.when(pl.program_id(1) == cc_ref[0])
        def _():
            own_ref[...] = s

    grid_spec = pltpu.PrefetchScalarGridSpec(
        num_scalar_prefetch=1, grid=(nb, 4),
        in_specs=[pl.BlockSpec((None, tr, cc), lambda i, qi, cc_ref: (qi, cc_ref[1] * nb + i, 0)),
                  pl.BlockSpec((None, tr, cc), lambda i, qi, cc_ref: (qi, i, 0))],
        out_specs=[pl.BlockSpec((tr, cc), lambda i, qi, cc_ref: (i, 0)),
                   pl.BlockSpec((None, tr, cc), lambda i, qi, cc_ref: (qi, i, 0))])
    return pl.pallas_call(
        body, name="pair_sum", grid_spec=grid_spec,
        out_shape=[jax.ShapeDtypeStruct((hr, cc), F32), jax.ShapeDtypeStruct((4, hr, cc), BF16)],
        compiler_params=_cp(),
    )(chip_core, grad, got)


def owner_sum(own, got, chip_core):
    hr, cc = own.shape
    tr = 256 if hr % 256 == 0 else hr
    nb = hr // tr

    def body(cc_ref, a_ref, b_ref, o_ref):
        s = a_ref[...] + b_ref[0].astype(F32)
        s = s + b_ref[1].astype(F32)
        o_ref[...] = s + b_ref[2].astype(F32)

    grid_spec = pltpu.PrefetchScalarGridSpec(
        num_scalar_prefetch=1, grid=(nb,),
        in_specs=[pl.BlockSpec((tr, cc), lambda i, cc_ref: (i, 0)),
                  pl.BlockSpec((3, tr, cc), lambda i, cc_ref: (0, i, 0))],
        out_specs=pl.BlockSpec((tr, cc), lambda i, cc_ref: (cc_ref[1] * nb + i, 0)))
    return pl.pallas_call(
        body, name="owner_sum", grid_spec=grid_spec,
        out_shape=jax.ShapeDtypeStruct((2 * hr, cc), F32), compiler_params=_cp(),
    )(chip_core, own, got)


def share_halves(bufs):
    n = len(bufs)

    def body(*refs):
        outs = refs[n:2 * n]
        send_sems, recv_sems = refs[2 * n:]
        x, y, c, _ = _place()
        copies = []
        for i in range(n):
            hr = bufs[i].shape[0] // 2
            mine = outs[i].at[pl.ds(c * hr, hr)]
            cp = pltpu.make_async_remote_copy(
                src_ref=mine, dst_ref=mine, send_sem=send_sems.at[i], recv_sem=recv_sems.at[i],
                device_id=(x, y, 1 - c), device_id_type=MESH)
            cp.start()
            copies.append((cp, outs[i].at[pl.ds((1 - c) * hr, hr)]))
        for i, (cp, theirs) in enumerate(copies):
            cp.wait_send()
            pltpu.make_async_remote_copy(
                src_ref=theirs, dst_ref=theirs, send_sem=send_sems.at[i], recv_sem=recv_sems.at[i],
                device_id=(x, y, c), device_id_type=MESH).wait_recv()

    return pl.pallas_call(
        body, name="share_halves",
        in_specs=[ANY] * n, out_specs=[ANY] * n,
        out_shape=[jax.ShapeDtypeStruct(b.shape, b.dtype) for b in bufs],
        input_output_aliases={i: i for i in range(n)},
        scratch_shapes=[pltpu.SemaphoreType.DMA((n,)), pltpu.SemaphoreType.DMA((n,))],
        compiler_params=pltpu.CompilerParams(has_side_effects=True),
    )(*bufs)


def allgather8(v, name, per_peer=False):
    r, n = v.shape[-2:]

    def body(v_ref, out_ref, send_sems, recv_sems):
        x, y, c, _ = _place()
        me = 4 * x + 2 * y + c
        out_ref[me] = v_ref[me] if per_peer else v_ref[...]

        def copy(k, peer_index, slot, to):
            return pltpu.make_async_remote_copy(
                src_ref=v_ref.at[peer_index] if per_peer else v_ref, dst_ref=out_ref.at[slot],
                send_sem=send_sems.at[k - 1], recv_sem=recv_sems.at[k - 1], device_id=to, device_id_type=MESH)

        peers = []
        for k in range(1, 8):
            px = 1 - x if (k >> 2) & 1 else x
            py = 1 - y if (k >> 1) & 1 else y
            pc = 1 - c if k & 1 else c
            peers.append((px, py, pc))
            copy(k, 4 * px + 2 * py + pc, me, (px, py, pc)).start()
        for k, (px, py, pc) in enumerate(peers, start=1):
            copy(k, me, 4 * px + 2 * py + pc, (x, y, c)).wait_recv()
        for k, (px, py, pc) in enumerate(peers, start=1):
            copy(k, 4 * px + 2 * py + pc, me, (px, py, pc)).wait_send()

    return pl.pallas_call(
        body, name=name, in_specs=[VMEM], out_specs=VMEM,
        out_shape=jax.ShapeDtypeStruct((8, r, n), v.dtype),
        scratch_shapes=[pltpu.SemaphoreType.DMA((7,)), pltpu.SemaphoreType.DMA((7,))],
        compiler_params=_cp(has_side_effects=True),
    )(v)


HBM = pl.BlockSpec(memory_space=pltpu.HBM)
SEM = pl.BlockSpec(memory_space=pltpu.SEMAPHORE)
DATAFLOW = pltpu.SideEffectType.DATAFLOW_SIDE_EFFECTING


def _descriptors(plan, refs, send_sems, recv_sems, arrivals=True):
    x, y, c, _ = _place()
    sends, recvs = plan(refs)
    out = [pltpu.make_async_remote_copy(src_ref=src, dst_ref=dst, send_sem=send_sems.at[k],
                                        recv_sem=recv_sems.at[k], device_id=to, device_id_type=MESH)
           for k, (src, dst, to) in enumerate(sends)]
    if not arrivals:
        return out, []
    inn = [pltpu.make_async_remote_copy(src_ref=land, dst_ref=land, send_sem=send_sems.at[k],
                                        recv_sem=recv_sems.at[k], device_id=(x, y, c), device_id_type=MESH)
           for k, land in enumerate(recvs)]
    return out, inn


def copies_start(name, arrays, n_copies, plan, after, carried=()):
    na, nall = len(arrays), len(arrays) + len(carried)
    everything = list(arrays) + list(carried)

    def body(*refs):
        out, _ = _descriptors(plan, refs[:na], refs[nall + 1], refs[nall + 2], arrivals=False)
        for cp in out:
            cp.start()
        refs[-1][...] = jnp.zeros((8, 128), F32)

    res = pl.pallas_call(
        body, name=name,
        out_shape=(pltpu.SemaphoreType.DMA((n_copies,)), pltpu.SemaphoreType.DMA((n_copies,)),
                   *[pltpu.HBM(a.shape, a.dtype) for a in everything], jax.ShapeDtypeStruct((8, 128), F32)),
        in_specs=[HBM] * nall + [ANY], out_specs=(SEM, SEM, *[HBM] * nall, VMEM),
        input_output_aliases={i: i + 2 for i in range(nall)},
        compiler_params=pltpu.CompilerParams(has_side_effects=DATAFLOW),
    )(*[pltpu.with_memory_space_constraint(a, pltpu.HBM) for a in everything], after)
    return res[0], res[1], list(res[2:2 + na]), res[-1], list(res[2 + na:2 + nall])


def copies_wait(name, started, plan, after):
    send_sems, recv_sems, arrays = started[:3]
    na = len(arrays)
    after = list(after) if isinstance(after, (list, tuple)) else [after]

    def body(*refs):
        out, inn = _descriptors(plan, refs[:na], refs[na], refs[na + 1])
        for cp in out:
            cp.wait_send()
        for cp in inn:
            cp.wait_recv()
        refs[-1][...] = jnp.zeros((8, 128), F32)

    res = pl.pallas_call(
        body, name=name,
        out_shape=(*[pltpu.HBM(a.shape, a.dtype) for a in arrays], jax.ShapeDtypeStruct((8, 128), F32)),
        in_specs=[HBM] * na + [SEM, SEM] + [ANY] * len(after), out_specs=(*[HBM] * na, VMEM),
        input_output_aliases={i: i for i in range(na)},
        compiler_params=pltpu.CompilerParams(has_side_effects=DATAFLOW),
    )(*arrays, send_sems, recv_sems, *after)
    return list(res[:na]), res[-1]


def _rows_half(r, c):
    return pl.ds(c * (r // 2), r // 2), pl.ds((1 - c) * (r // 2), r // 2)


def plan_gather_neighbours(refs):
    x, y, c, _ = _place()
    p = 2 * x + y
    near, far, _ = _relay_chips()
    sends, recvs = [], []
    for buf in refs:
        mine, _ = _rows_half(buf.shape[1], c)
        for chip in (near, far):
            sends.append((buf.at[p, mine], buf.at[p, mine], (*chip, c)))
            recvs.append(buf.at[2 * chip[0] + chip[1], mine])
    return sends, recvs


def plan_gather_relay(refs):
    x, y, c, _ = _place()
    near, far, diag = _relay_chips()
    slot = lambda chip: 2 * chip[0] + chip[1]
    sends, recvs = [], []
    for buf in refs:
        mine, theirs = _rows_half(buf.shape[1], c)
        landed = buf.at[slot(near), mine]
        sends.append((landed, landed, (*far, c)))
        recvs.append(buf.at[slot(diag), mine])
        for sent, got in ((near, far), (far, near)):
            sends.append((buf.at[slot(sent), mine], buf.at[slot(sent), mine], (x, y, 1 - c)))
            recvs.append(buf.at[slot(got), theirs])
    return sends, recvs


def plan_gather_d2d(refs):
    x, y, c, _ = _place()
    _, _, diag = _relay_chips()
    sends, recvs = [], []
    for buf in refs:
        mine, theirs = _rows_half(buf.shape[1], c)
        landed = buf.at[2 * diag[0] + diag[1], mine]
        sends.append((landed, landed, (x, y, 1 - c)))
        recvs.append(buf.at[2 * diag[0] + diag[1], theirs])
    return sends, recvs


def plan_exchange(refs):
    x, y, c, _ = _place()
    n = len(refs) // 2
    sends, recvs = [], []
    for grad, land in zip(refs[:n], refs[n:]):
        _, theirs = _rows_half(grad.shape[1], c)
        sends.append((grad.at[:, theirs], land, (x, y, 1 - c)))
        recvs.append(land)
    return sends, recvs


def plan_scatter(refs):
    x, y, c, chips = _place()
    n = len(refs) // 2
    sends, recvs = [], []
    for part, land in zip(refs[:n], refs[n:]):
        for j, chip in enumerate(chips):
            sends.append((part.at[2 * chip[0] + chip[1]], land.at[j], (*chip, c)))
            recvs.append(land.at[j])
    return sends, recvs


def plan_rows_to_all(refs):
    x, y, c, _ = _place()
    me = 4 * x + 2 * y + c
    buf, = refs
    sends, recvs = [], []
    for k in range(1, 8):
        px = 1 - x if (k >> 2) & 1 else x
        py = 1 - y if (k >> 1) & 1 else y
        pc = 1 - c if k & 1 else c
        sends.append((buf.at[me], buf.at[me], (px, py, pc)))
        recvs.append(buf.at[4 * px + 2 * py + pc])
    return sends, recvs


def plan_share(refs):
    x, y, c, _ = _place()
    sends, recvs = [], []
    for buf in refs:
        mine, theirs = _rows_half(buf.shape[0], c)
        sends.append((buf.at[mine], buf.at[mine], (x, y, 1 - c)))
        recvs.append(buf.at[theirs])
    return sends, recvs


def put_in_slot(w, chip, dtype, name):
    r, c = w.shape
    tr = 256 if r % 256 == 0 else r

    def body(chip_ref, w_ref, o_ref):
        o_ref[...] = w_ref[...].astype(dtype)

    grid_spec = pltpu.PrefetchScalarGridSpec(
        num_scalar_prefetch=1, grid=(r // tr,),
        in_specs=[pl.BlockSpec((tr, c), lambda i, chip_ref: (i, 0))],
        out_specs=pl.BlockSpec((None, tr, c), lambda i, chip_ref: (chip_ref[0], i, 0)))
    return pl.pallas_call(body, name=name, grid_spec=grid_spec,
                          out_shape=jax.ShapeDtypeStruct((4, r, c), dtype), compiler_params=_cp())(chip, w)


def ada_fwd(s_in, ada_w, ada_b, tn):
    nl, d, ws = ada_w.shape

    def body(s_ref, w_ref, b_ref, so_ref, mod_ref):
        s = _silu(s_ref[...])
        so_ref[...] = s
        mod_ref[...] = _dot(s.astype(BF16), w_ref[...].astype(BF16)) + b_ref[...]

    return pl.pallas_call(
        body, name="ada_fwd", grid=(nl, ws // tn),
        in_specs=[pl.BlockSpec((16, d), lambda l, j: (0, 0)),
                  pl.BlockSpec((None, d, tn), lambda l, j: (l, 0, j)),
                  pl.BlockSpec((None, 1, tn), lambda l, j: (l, 0, j))],
        out_specs=[pl.BlockSpec((16, d), lambda l, j: (0, 0)),
                   pl.BlockSpec((None, 16, tn), lambda l, j: (l, 0, j))],
        out_shape=[jax.ShapeDtypeStruct((16, d), F32), jax.ShapeDtypeStruct((nl, 16, ws), F32)],
        compiler_params=_cp(),
    )(s_in, ada_w, ada_b)


def _adamw_math(w, g, m, v):
    m = ADAM_B1 * m + (1.0 - ADAM_B1) * g
    v = ADAM_B2 * v + (1.0 - ADAM_B2) * (g * g)
    m_hat = m / (1.0 - ADAM_B1 ** ADAM_STEP)
    v_hat = v / (1.0 - ADAM_B2 ** ADAM_STEP)
    delta = -ADAM_LR * (m_hat / (jnp.sqrt(v_hat) + ADAM_EPS) + ADAM_WD * w)
    return delta, m, v


def ada_bwd_adamw(s, dm, w, m, v):
    nl, d, ws = w.shape
    tr = 256 if d % 256 == 0 else 128

    def body(s_ref, dm_ref, w_ref, m_ref, v_ref, g_ref, dl_ref, mo_ref, vo_ref, dc_ref):
        dmv = dm_ref[...].astype(BF16)
        wv = w_ref[...]
        g = _dot(s_ref[...].astype(BF16), dmv, TN)
        g_ref[...] = g
        dl_ref[...], mo_ref[...], vo_ref[...] = _adamw_math(wv, g, m_ref[...], v_ref[...])
        dc_ref[...] = _dot(dmv[8:16, :], wv.astype(BF16), NT)

    wblk = pl.BlockSpec((None, tr, ws), lambda l, i: (l, i, 0))
    wshape = jax.ShapeDtypeStruct((nl, d, ws), F32)
    return pl.pallas_call(
        body, name="ada_bwd_adamw", grid=(nl, d // tr),
        in_specs=[pl.BlockSpec((16, tr), lambda l, i: (0, i)),
                  pl.BlockSpec((None, 16, ws), lambda l, i: (l, 0, 0)), wblk, wblk, wblk],
        out_specs=[wblk, wblk, wblk, wblk, pl.BlockSpec((None, 8, tr), lambda l, i: (l, 0, i))],
        out_shape=[wshape, wshape, wshape, wshape, jax.ShapeDtypeStruct((nl, 8, d), F32)],
        compiler_params=_cp(),
    )(s, dm, w, m, v)


def adamw(w, g, m, v, name, with_grad=False):
    r, c = w.shape
    tr = 256 if r % 256 == 0 else r

    def body(w_ref, g_ref, m_ref, v_ref, dl_ref, mo_ref, vo_ref, *g_out):
        gv = g_ref[...]
        dl_ref[...], mo_ref[...], vo_ref[...] = _adamw_math(w_ref[...], gv, m_ref[...], v_ref[...])
        if with_grad:
            g_out[0][...] = gv

    blk = pl.BlockSpec((tr, c), lambda i: (i, 0))
    shape = jax.ShapeDtypeStruct((r, c), F32)
    n_out = 4 if with_grad else 3
    return pl.pallas_call(body, name=name, grid=(r // tr,), in_specs=[blk] * 4, out_specs=[blk] * n_out,
                          out_shape=[shape] * n_out, compiler_params=_cp())(w, g, m, v)


ROW_MOD = 10


def small_reduce(gathered):
    _, rows, d = gathered.shape

    def body(g_ref, o_ref):
        tot = g_ref[0]
        for b in range(1, 8):
            tot = tot + g_ref[b]
        o_ref[0:rows, :] = tot
        for layer in range(2):
            lat = ROW_MOD + 6 * layer
            o_ref[24 + 3 * layer:27 + 3 * layer, :] = tot[lat:lat + 3, :] + tot[lat + 3:lat + 6, :]
        o_ref[30:32, :] = jnp.zeros((2, d), F32)

    return pl.pallas_call(body, name="small_reduce", in_specs=[VMEM], out_specs=VMEM,
                          out_shape=jax.ShapeDtypeStruct((32, d), F32), compiler_params=_cp())(gathered)


def lb_logits_grad(lbl, dlb):
    _, _, n = lbl.shape

    def body(l_ref, d_ref, o_ref):
        for dr in range(2):
            _, (p0, p1, p2) = _lower_bound(l_ref, dr)
            dv = d_ref[dr:dr + 1, :]
            o_ref[dr, 0:1, :] = p0 * p2 * dv
            o_ref[dr, 1:2, :] = p1 * p2 * dv
            o_ref[dr, 2:3, :] = -p2 * (p0 + p1) * dv

    return pl.pallas_call(body, name="lb_logits_grad", in_specs=[VMEM, VMEM], out_specs=VMEM,
                          out_shape=jax.ShapeDtypeStruct((2, 3, n), F32), compiler_params=_cp())(lbl, dlb)


def c_ctx_grad(parts, c_ctx):
    d = c_ctx.shape[1]

    def body(p_ref, c_ref, o_ref):
        tot = p_ref[0, 0:1, :]
        for chip in range(1, 4):
            tot = tot + p_ref[2 * chip, 0:1, :]
        o_ref[...] = tot * _dsilu(c_ref[...])

    return pl.pallas_call(body, name="c_ctx_grad", in_specs=[VMEM, VMEM], out_specs=VMEM,
                          out_shape=jax.ShapeDtypeStruct((1, d), F32), compiler_params=_cp())(parts, c_ctx)


def kernel(x, c, ctx, c_ctx, ada_w, ada_b, pre_g, post_g, ev_w_in, ev_pool_w, ev_pool_scale, ev_conv_w, ev_conv_b, ev_w_out, od_w_in, od_onorm_g, od_w_out, lb_logits, loss_target, m_c_ctx, m_ada_w, m_ada_b, m_pre_g, m_post_g, m_ev_w_in, m_ev_pool_w, m_ev_pool_scale, m_ev_conv_w, m_ev_conv_b, m_ev_w_out, m_od_w_in, m_od_onorm_g, m_od_w_out, m_lb_logits, v_c_ctx, v_ada_w, v_ada_b, v_pre_g, v_post_g, v_ev_w_in, v_ev_pool_w, v_ev_pool_scale, v_ev_conv_w, v_ev_conv_b, v_ev_w_out, v_od_w_in, v_od_onorm_g, v_od_w_out, v_lb_logits):
    _, seq, d = x.shape
    cx = ctx.shape[1]
    t = cx + seq
    half_d = d // 2
    g = half_d // N_POOL
    tn = d // 4
    xi, yi, ci = lax.axis_index("x"), lax.axis_index("y"), lax.axis_index("c")
    chip = 2 * xi + yi
    chip_arr = jnp.reshape(chip, (1,)).astype(jnp.int32)
    chip_core_arr = jnp.stack([chip, ci]).astype(jnp.int32)

    c_rows = jnp.concatenate([c, jnp.zeros((7, d), F32)], axis=0)
    c_all = allgather8(c_rows, "allgather_c")[:, 0, :]
    s_in = jnp.concatenate([c_all, c_ctx.reshape(1, d), jnp.zeros((7, d), F32)], axis=0)
    ws_ada = ada_w.shape[2]
    ada_b_mine = lax.dynamic_slice(ada_b, (0, chip * ws_ada), (2, ws_ada)).reshape(2, 1, ws_ada)
    s_act, mod_mine = ada_fwd(s_in, ada_w, ada_b_mine, tn)
    mod_rows = jnp.concatenate([
        mod_mine[:, :8].transpose(1, 0, 2), jnp.broadcast_to(mod_mine[:, 8][None], (8, 2, ws_ada)),
        jnp.zeros((8, 4, ws_ada), F32)], axis=1)
    mod_all = allgather8(mod_rows, "exchange_mod", per_peer=True)

    pad = lambda a, rows: jnp.concatenate([a, jnp.zeros((rows - a.shape[0], g), F32)], axis=0)
    small = jnp.concatenate([
        ev_pool_w.reshape(g, g), pad(ev_conv_w.reshape(3, g), 8), pad(od_onorm_g.reshape(2, g), 8),
        pad(lb_logits.reshape(12, g), 16)], axis=0)
    by_chip = mod_all[0::2]
    mods = jnp.stack([by_chip[:, 2:4], by_chip[:, 0:2]]).transpose(2, 0, 1, 3).reshape(2, 2, 3 * d)
    shift, scale, gate = mods[:, :, :d], mods[:, :, d:2 * d], mods[:, :, 2 * d:]
    ev_in_g, ev_out_g, small_g, ev_done, od_in_mine, od_out_mine, h0, xs = allgather_shards([
        put_in_slot(ev_w_in[0], chip_arr, BF16, "cast_ev_w_in"),
        put_in_slot(ev_w_out[0], chip_arr, BF16, "cast_ev_w_out"),
        put_in_slot(small, chip_arr, F32, "place_small")], mod_all, [od_w_in[0], od_w_out[0]],
        (ctx[0], x[0], pre_g[0:1], shift[0], scale[0]))
    od_ici = copies_start("gather_od_ici_start", [od_in_mine, od_out_mine], 4, plan_gather_neighbours, ev_done,
                          carried=[h0])
    h0 = od_ici[4][0]
    ev_out3 = ev_out_g.reshape(1, d, d)
    pool_w_full = small_g[:, :g].reshape(4, N_POOL, g // 4, g).transpose(1, 0, 2, 3).reshape(N_POOL, g, g)
    conv_w_full = small_g[:, g:g + 3].transpose(1, 0, 2).reshape(3, half_d)
    onorm_full = small_g[:, g + 8:g + 10].reshape(1, d)
    lbl_full = small_g[:, g + 16:g + 28].reshape(4, 2, 3, 2 * g).transpose(1, 2, 0, 3).reshape(2, 3, d)

    z0 = mm_nn(h0, ev_in_g, half_d, tn, "mm_ev_in")
    u = mix_b_fwd(z0, conv_w_full, ev_conv_b, mix_a_fwd(z0, pool_w_full, ev_pool_scale, cx), cx)
    od_relay = copies_start("gather_od_relay_start",
                            copies_wait("gather_od_ici_wait", od_ici, plan_gather_neighbours, u)[0],
                            6, plan_gather_relay, u)
    y0 = mm_nn(u, ev_out3, d, tn, "mm_ev_out")[0]
    xs1, h1 = post_fwd_norm(xs, y0, post_g[0:1] + od_relay[3][0:1, 0:1], gate[0],
                            pre_g[1:2], shift[1], scale[1], cx)
    od_d2d = copies_start("gather_od_d2d_start",
                          copies_wait("gather_od_relay_wait", od_relay, plan_gather_relay, xs1)[0],
                          2, plan_gather_d2d, xs1)
    (od_in_g, od_out_g), _ = copies_wait("gather_od_d2d_wait", od_d2d, plan_gather_d2d, od_d2d[3])
    od_out3 = od_out_g.reshape(1, d, d)

    z1 = mm_nn(h1, od_in_g, d, tn, "mm_od_in")
    o1, r1, bcs1, ks1, decs1 = hgrn_fwd(z1, lbl_full, onorm_full, cx)
    y1 = mm_nn(r1, od_out3, d, tn, "mm_od_out")[0]
    sq, dx2, dy1, dgate1, dpost1 = post_loss(xs1, y1, post_g[1:2], gate[1], loss_target[0], cx)

    dr1 = mm_nt(dy1[None], None, od_out3, tn, "mm_od_out_dx")
    g_od_out = mm_tn(r1, dy1[None], None, d, tn, "mm_od_out_dw")
    dz1, donorm, dlb = hgrn_bwd(z1, lbl_full, onorm_full, o1, dr1, bcs1, ks1, decs1, cx)
    dh1 = mm_nt(dz1, None, od_in_g, tn, "mm_od_in_dx")
    g_od_in = mm_tn(h1, dz1, None, od_in_g.shape[2], tn, "mm_od_in_dw")
    od_grads = [g_od_in, g_od_out.reshape(4, d // 4, d)]
    half_zone = lambda a, lead, dt: lax.empty((lead, a.shape[1] // 2, a.shape[2]), dt)
    od_ex = copies_start("reduce_od_exchange_start", od_grads + [half_zone(a, 4, a.dtype) for a in od_grads],
                         2, plan_exchange, dh1)

    dxs1, dpre1, dshift1, dscale1, dy0, dgate0, dpost0 = normmod_bwd(
        xs1, dh1, pre_g[1:2] + od_ex[3][0:1, 0:1], scale[1], dx2, cx, True,
        prev=(y0, post_g[0:1], gate[0]))
    du = mm_nt(dy0[None], None, ev_out3, tn, "mm_ev_out_dx")
    g_ev_out = mm_tn(u, dy0[None], None, d, tn, "mm_ev_out_dw")
    od_got, _ = copies_wait("reduce_od_exchange_wait", od_ex, plan_exchange, g_ev_out)
    od_sums = [pair_sum(od_got[i], od_got[2 + i], chip_core_arr) for i in range(2)]
    od_sc = copies_start("reduce_od_scatter_start",
                         [sb for _, sb in od_sums] + [half_zone(a, 3, BF16) for a in od_grads],
                         6, plan_scatter, du)
    dz0a, g_pool_w, dpool_scale = mix_a_bwd(z0, du, pool_w_full, ev_pool_scale + od_sc[3][0:1, 0:1], cx)
    dz0b, dconv_w, dconv_b = mix_b_bwd(z0, du, conv_w_full, ev_conv_b + od_sc[3][0:1, 0:1], cx)
    g_ev_in = mm_tn(h0, dz0a, dz0b, ev_in_g.shape[2], tn, "mm_ev_in_dw")
    ev_grads = [g_ev_in, g_ev_out.reshape(4, d // 4, d), g_pool_w.reshape(4, g, g)]
    ev_ex = copies_start("reduce_ev_exchange_start", ev_grads + [half_zone(a, 4, a.dtype) for a in ev_grads],
                         3, plan_exchange, dpool_scale)
    dh0 = mm_nt(dz0a, dz0b, ev_in_g, tn, "mm_ev_in_dx")
    dxs0, dpre0, dshift0, dscale0 = normmod_bwd(xs, dh0, pre_g[0:1] + ev_ex[3][0:1, 0:1], scale[0], dxs1,
                                                cx, False, True)
    grad_x = dxs0[None]
    ev_got, _ = copies_wait("reduce_ev_exchange_wait", ev_ex, plan_exchange, dxs0)
    ev_sums = [pair_sum(ev_got[i], ev_got[3 + i], chip_core_arr) for i in range(3)]

    zrow = jnp.zeros((1, d), F32)
    small_rows = jnp.concatenate([
        dpre0, dpre1, dpost0, dpost1,
        jnp.concatenate([dpool_scale, dconv_b], axis=1),
        jnp.concatenate([dconv_w.reshape(1, 3 * half_d), jnp.zeros((1, half_d), F32)], axis=1).reshape(2, d),
        donorm, dlb,
        dshift0[1:2], dscale0[1:2], dgate0[1:2], dshift0[0:1], dscale0[0:1], dgate0[0:1],
        dshift1[1:2], dscale1[1:2], dgate1[1:2], dshift1[0:1], dscale1[0:1], zrow,
        jnp.concatenate([sq[0:1], jnp.zeros((1, d - 128), F32)], axis=1),
        zrow], axis=0)
    small_all = allgather8(small_rows, "allgather_small")
    ev_sc = copies_start("reduce_ev_scatter_start",
                         [sb for _, sb in ev_sums] + [half_zone(a, 3, BF16) for a in ev_grads],
                         9, plan_scatter, small_all)
    od_recv, _ = copies_wait("reduce_od_scatter_wait", od_sc, plan_scatter, [dxs0, ev_sc[3]])
    od_sh = copies_start("reduce_od_share_start",
                         [owner_sum(od_sums[i][0], od_recv[2 + i], chip_core_arr) for i in range(2)],
                         2, plan_share, dxs0)
    tot = small_reduce(small_all + ev_sc[3][0:1, 0:1])
    loss = tot[22, 0] * (0.5 / d)

    dm_rows = []
    for layer in range(2):
        lat = ROW_MOD + 6 * layer
        dm_lat = small_all[:, lat:lat + 3].reshape(8, 3 * d)
        dm_ctx = tot[lat + 3:lat + 6].reshape(1, 3 * d)
        dm_rows.append(jnp.concatenate([dm_lat, dm_ctx, jnp.zeros((7, 3 * d), F32)], axis=0))
    dm_full = jnp.stack(dm_rows)
    dm_mine = lax.dynamic_slice(dm_full, (0, 0, chip * ws_ada), (2, 16, ws_ada))

    def step(w, gr, m, v, name, with_grad=False):
        shape = w.shape
        cols = shape[-1]
        two_d = lambda a: a.reshape(-1, cols)
        res = adamw(two_d(w), two_d(gr), two_d(m), two_d(v), "adamw_" + name, with_grad)
        return tuple(a.reshape(shape) for a in res)

    grad_ada_b = tot[24:30].reshape(2, 3 * d)
    grad_pre_g = tot[0:2]
    grad_post_g = tot[2:4]
    grad_ev_pool_scale = tot[4:5, :half_d]
    grad_ev_conv_b = tot[4:5, half_d:]
    conv_w_tot = tot[5:7].reshape(1, 2 * d)[:, :3 * half_d].reshape(3, N_POOL, g)
    grad_ev_conv_w = lax.dynamic_slice(conv_w_tot, (0, chip, 0), (3, 1, g)).reshape(1, 3, g)
    grad_od_onorm_g = lax.dynamic_slice(tot[7:8], (0, chip * 2 * g), (1, 2 * g))
    dlb_mine = lax.dynamic_slice(tot[8:10], (0, chip * 2 * g), (2, 2 * g))
    grad_lb_logits = lb_logits_grad(lb_logits, dlb_mine)
    upd = {
        "ada_b": step(ada_b, grad_ada_b, m_ada_b, v_ada_b, "ada_b"),
        "pre_g": step(pre_g, grad_pre_g, m_pre_g, v_pre_g, "pre_g"),
        "post_g": step(post_g, grad_post_g, m_post_g, v_post_g, "post_g"),
        "ev_pool_scale": step(ev_pool_scale, grad_ev_pool_scale, m_ev_pool_scale, v_ev_pool_scale, "ev_pool_scale"),
        "ev_conv_w": step(ev_conv_w, grad_ev_conv_w, m_ev_conv_w, v_ev_conv_w, "ev_conv_w"),
        "ev_conv_b": step(ev_conv_b, grad_ev_conv_b, m_ev_conv_b, v_ev_conv_b, "ev_conv_b"),
        "od_onorm_g": step(od_onorm_g, grad_od_onorm_g, m_od_onorm_g, v_od_onorm_g, "od_onorm_g"),
        "lb_logits": step(lb_logits, grad_lb_logits, m_lb_logits, v_lb_logits, "lb_logits"),
    }
    grad_ada_w, delta_ada_w, new_m_ada_w, new_v_ada_w, dctx_part = ada_bwd_adamw(
        s_act, dm_mine, ada_w, m_ada_w, v_ada_w)
    upd["ada_w"] = (delta_ada_w, new_m_ada_w, new_v_ada_w)
    dctx_rows = dctx_part[0] + dctx_part[1]
    dctx_sent = copies_start(
        "exchange_dctx_start",
        [lax.dynamic_update_slice(jnp.zeros((8, 8, d), F32), dctx_rows[None], (2 * chip + ci, 0, 0))],
        7, plan_rows_to_all, dctx_rows)
    (grad_od_w_in, grad_od_w_out), _ = copies_wait("reduce_od_share_wait", od_sh, plan_share, dctx_sent[3])
    upd["od_w_in"] = step(od_w_in, grad_od_w_in[None], m_od_w_in, v_od_w_in, "od_w_in", True)
    upd["od_w_out"] = step(od_w_out, grad_od_w_out[None], m_od_w_out, v_od_w_out, "od_w_out", True)
    grad_od_w_in, grad_od_w_out = upd["od_w_in"][3], upd["od_w_out"][3]
    done_behind = [dctx_part] + [upd[k][0] for k in (
        "od_w_in", "od_w_out", "ada_b", "pre_g", "post_g", "ev_pool_scale", "ev_conv_w", "ev_conv_b",
        "od_onorm_g", "lb_logits")]
    ev_recv, ev_landed = copies_wait("reduce_ev_scatter_wait", ev_sc, plan_scatter, done_behind)
    grad_ev_w_in, grad_ev_w_out, grad_pool_w = share_halves(
        [owner_sum(ev_sums[i][0], ev_recv[3 + i], chip_core_arr) for i in range(3)])
    upd["ev_w_in"] = step(ev_w_in, grad_ev_w_in[None], m_ev_w_in, v_ev_w_in, "ev_w_in", True)
    upd["ev_pool_w"] = step(ev_pool_w, grad_pool_w.reshape(1, N_POOL, g // 4, g), m_ev_pool_w, v_ev_pool_w,
                            "ev_pool_w", True)
    upd["ev_w_out"] = step(ev_w_out, grad_ev_w_out[None], m_ev_w_out, v_ev_w_out, "ev_w_out", True)
    grad_ev_w_in, grad_ev_pool_w, grad_ev_w_out = upd["ev_w_in"][3], upd["ev_pool_w"][3], upd["ev_w_out"][3]
    (dctx_all,), _ = copies_wait("exchange_dctx_wait", dctx_sent, plan_rows_to_all,
                                 [ev_landed, upd["ev_w_in"][0], upd["ev_w_out"][0]])
    grad_c_ctx = c_ctx_grad(dctx_all, c_ctx.reshape(1, d)).reshape(d)
    upd["c_ctx"] = step(c_ctx, grad_c_ctx, m_c_ctx, v_c_ctx, "c_ctx")
    names = ["c_ctx", "ada_w", "ada_b", "pre_g", "post_g", "ev_w_in", "ev_pool_w", "ev_pool_scale",
             "ev_conv_w", "ev_conv_b", "ev_w_out", "od_w_in", "od_onorm_g", "od_w_out", "lb_logits"]
    grads = [grad_c_ctx, grad_ada_w, grad_ada_b, grad_pre_g, grad_post_g, grad_ev_w_in, grad_ev_pool_w,
             grad_ev_pool_scale, grad_ev_conv_w, grad_ev_conv_b, grad_ev_w_out, grad_od_w_in,
             grad_od_onorm_g, grad_od_w_out, grad_lb_logits]
    return (loss, grad_x, *grads, *[upd[k][0] for k in names], *[upd[k][1] for k in names],
            *[upd[k][2] for k in names])
```
